```python
import jax, jax.numpy as jnp
from jax import lax
import numpy as np

D_MODEL = 1024
BATCH = 16
SEQ = 2048
DEPTH = 1

MEM_LEN = 256
HEAD_DIM = 64
CHUNK = 128
A_GROUPS = 4
A_WIDTH = D_MODEL // 2
A_GROUP_W = A_WIDTH // A_GROUPS
SWA_HEADS = 4
SWA_KV_HEADS = 2
SWA_WIDTH = SWA_HEADS * HEAD_DIM
SWA_KV_WIDTH = SWA_KV_HEADS * HEAD_DIM
WINDOW = 128
MEM_HEADS = 4
MEM_WIDTH = MEM_HEADS * HEAD_DIM
MIX_WIDTH = A_WIDTH + SWA_WIDTH + MEM_WIDTH
IN_WIDTH = 2 * A_WIDTH + SWA_WIDTH + 2 * SWA_KV_WIDTH + MEM_WIDTH + MIX_WIDTH
N_BUCKETS = 32
MAX_DISTANCE = 128
EPS = 1e-6
NEG = -1e30

kernel_name = "hymba_gmlp_swa_sink_memxattn_layer"


def rms_norm(x, g):
    xf = x.astype(jnp.float32)
    y = xf * lax.rsqrt(jnp.mean(xf * xf, axis=-1, keepdims=True) + EPS)
    return (y * g.astype(jnp.float32)).astype(x.dtype)


def t5_causal_buckets(dist):
    n = np.maximum(dist, 0)
    max_exact = N_BUCKETS // 2
    large = max_exact + (np.log(np.maximum(n, 1) / max_exact) / np.log(MAX_DISTANCE / max_exact)
                         * (N_BUCKETS - max_exact)).astype(np.int32)
    large = np.minimum(large, N_BUCKETS - 1)
    return np.where(n < max_exact, n, large).astype(np.int32)


def chunked_spatial_gating(u, v, v_g, v_b, w_s, b_s):
    b, s, _ = u.shape
    nc = s // CHUNK
    vg = v.reshape(b, s, A_GROUPS, A_GROUP_W).astype(jnp.float32)
    mu = jnp.mean(vg, axis=-1, keepdims=True)
    var = jnp.mean(jnp.square(vg - mu), axis=-1, keepdims=True)
    vg = (vg - mu) * lax.rsqrt(var + EPS)
    vg = vg * v_g.reshape(A_GROUPS, A_GROUP_W).astype(jnp.float32) + v_b.reshape(A_GROUPS, A_GROUP_W).astype(jnp.float32)
    vc = vg.astype(v.dtype).reshape(b, nc, CHUNK, A_GROUPS, A_GROUP_W)
    causal = jnp.tril(jnp.ones((CHUNK, CHUNK), dtype=w_s.dtype))
    w = w_s * causal[None]
    sv = jnp.einsum('gts,bnsgc->bntgc', w, vc) + b_s.T[None, None, :, :, None]
    return u * sv.reshape(b, s, A_WIDTH)


def sliding_window_attention(q, k, v, sinks, rel_bias):
    b, s, hq, dh = q.shape
    nb = s // CHUNK
    g = hq // SWA_KV_HEADS
    qb = q.reshape(b, nb, CHUNK, SWA_KV_HEADS, g, dh)

    def band(t):
        tb = t.reshape(b, nb, CHUNK, SWA_KV_HEADS, dh)
        prev = jnp.pad(tb, ((0, 0), (1, 0), (0, 0), (0, 0), (0, 0)))[:, :-1]
        return jnp.concatenate([prev, tb], axis=2)

    kb, vb = band(k), band(v)
    logits = jnp.einsum('bnqhgd,bnjhd->bnhgqj', qb, kb).astype(jnp.float32) * (dh ** -0.5)

    qi = np.arange(CHUNK)[:, None]
    kj = np.arange(2 * CHUNK)[None, :]
    dist = qi + CHUNK - kj
    blk = np.arange(nb)[:, None, None]
    valid = (dist >= 0) & (dist < WINDOW) & (blk * CHUNK + kj - CHUNK >= 0)
    buckets = t5_causal_buckets(dist)
    bias = rel_bias.astype(jnp.float32)[buckets]
    bias = jnp.transpose(bias, (2, 0, 1)).reshape(SWA_KV_HEADS, g, CHUNK, 2 * CHUNK)

    logits = jnp.where(valid[None, :, None, None], logits + bias[None, None], NEG)
    sink = sinks.astype(jnp.float32).reshape(1, 1, SWA_KV_HEADS, g, 1, 1)
    m = jnp.maximum(jnp.max(logits, axis=-1, keepdims=True), sink)
    p = jnp.exp(logits - m)
    probs = p / (jnp.sum(p, axis=-1, keepdims=True) + jnp.exp(sink - m))
    out = jnp.einsum('bnhgqj,bnjhd->bnqhgd', probs.astype(v.dtype), vb)
    return out.reshape(b, s, hq * dh)


def memory_cross_attention(q, mem_k, mem_v):
    b, s, h, dh = q.shape
    logits = jnp.einsum('bshd,bmhd->bhsm', q, mem_k).astype(jnp.float32) * (dh ** -0.5)
    probs = jax.nn.softmax(logits, axis=-1)
    out = jnp.einsum('bhsm,bmhd->bshd', probs.astype(mem_v.dtype), mem_v)
    return out.reshape(b, s, h * dh)


def _fwd_setup_inputs(seed: int = 0) -> dict:
    key = jax.random.key(seed)
    ks = jax.random.split(key, 16)
    f32 = jnp.float32
    x = jax.random.normal(ks[0], (BATCH, SEQ, D_MODEL), f32)
    mem = jax.random.normal(ks[1], (BATCH, MEM_LEN, D_MODEL), f32)
    pre_norm_g = 1.0 + 0.05 * jax.random.normal(ks[2], (DEPTH, D_MODEL), f32)
    post_norm_g = 1.0 + 0.05 * jax.random.normal(ks[3], (DEPTH, D_MODEL), f32)
    mem_norm_g = 1.0 + 0.05 * jax.random.normal(ks[4], (DEPTH, D_MODEL), f32)
    w_in = jax.random.normal(ks[5], (DEPTH, D_MODEL, IN_WIDTH), f32) * D_MODEL ** -0.5
    w_mem_kv = jax.random.normal(ks[6], (DEPTH, D_MODEL, 2 * MEM_WIDTH), f32) * D_MODEL ** -0.5
    v_norm_g = 1.0 + 0.05 * jax.random.normal(ks[7], (DEPTH, A_WIDTH), f32)
    v_norm_b = 0.02 * jax.random.normal(ks[8], (DEPTH, A_WIDTH), f32)
    w_spatial = jax.random.normal(ks[9], (DEPTH, A_GROUPS, CHUNK, CHUNK), f32) * CHUNK ** -0.5
    b_spatial = 1.0 + 0.1 * jax.random.normal(ks[10], (DEPTH, A_GROUPS, CHUNK), f32)
    attn_sinks = 0.5 * jax.random.normal(ks[11], (DEPTH, SWA_HEADS), f32)
    rel_bias = 0.5 * jax.random.normal(ks[12], (N_BUCKETS, SWA_HEADS), f32)
    w_out = jax.random.normal(ks[13], (DEPTH, MIX_WIDTH, D_MODEL), f32) * MIX_WIDTH ** -0.5
    return {"x": x, "mem": mem, "pre_norm_g": pre_norm_g, "post_norm_g": post_norm_g,
            "mem_norm_g": mem_norm_g, "w_in": w_in, "w_mem_kv": w_mem_kv,
            "v_norm_g": v_norm_g, "v_norm_b": v_norm_b, "w_spatial": w_spatial,
            "b_spatial": b_spatial, "attn_sinks": attn_sinks, "rel_bias": rel_bias,
            "w_out": w_out}


def _fwd_reference(x, mem, pre_norm_g, post_norm_g, mem_norm_g, w_in, w_mem_kv, v_norm_g, v_norm_b,
              w_spatial, b_spatial, attn_sinks, rel_bias, w_out):
    b, s, _ = x.shape
    m_len = mem.shape[1]
    split_at = np.cumsum([A_WIDTH, A_WIDTH, SWA_WIDTH, SWA_KV_WIDTH, SWA_KV_WIDTH, MEM_WIDTH]).tolist()
    for layer in range(DEPTH):
        h = rms_norm(x, pre_norm_g[layer])
        proj = h @ w_in[layer]
        a_u, a_v, sq, sk, sv, mq, z = jnp.split(proj, split_at, axis=-1)

        y_a = chunked_spatial_gating(jax.nn.gelu(a_u), jax.nn.gelu(a_v), v_norm_g[layer],
                                     v_norm_b[layer], w_spatial[layer], b_spatial[layer])

        y_b = sliding_window_attention(sq.reshape(b, s, SWA_HEADS, HEAD_DIM),
                                       sk.reshape(b, s, SWA_KV_HEADS, HEAD_DIM),
                                       sv.reshape(b, s, SWA_KV_HEADS, HEAD_DIM),
                                       attn_sinks[layer], rel_bias)

        mkv = rms_norm(mem, mem_norm_g[layer]) @ w_mem_kv[layer]
        mk, mv = jnp.split(mkv, 2, axis=-1)
        y_c = memory_cross_attention(mq.reshape(b, s, MEM_HEADS, HEAD_DIM),
                                     mk.reshape(b, m_len, MEM_HEADS, HEAD_DIM),
                                     mv.reshape(b, m_len, MEM_HEADS, HEAD_DIM))

        y = jnp.concatenate([y_a, y_b, y_c], axis=-1) * jax.nn.silu(z)
        x = x + rms_norm(y @ w_out[layer], post_norm_g[layer])
    return x


import jax as _jax
import jax.numpy as _jnp

TWIN_FORMAT = 'train_step'
FWD_PARAMS = ['x', 'mem', 'pre_norm_g', 'post_norm_g', 'mem_norm_g', 'w_in', 'w_mem_kv', 'v_norm_g', 'v_norm_b', 'w_spatial', 'b_spatial', 'attn_sinks', 'rel_bias', 'w_out']
TWIN_WEIGHTS = ['pre_norm_g', 'post_norm_g', 'mem_norm_g', 'w_in', 'w_mem_kv', 'v_norm_g', 'v_norm_b', 'w_spatial', 'b_spatial', 'attn_sinks', 'rel_bias', 'w_out']
TWIN_DIFF_INPUT = 'x'
TWIN_INPUTS = ['x', 'mem', 'pre_norm_g', 'post_norm_g', 'mem_norm_g', 'w_in', 'w_mem_kv', 'v_norm_g', 'v_norm_b', 'w_spatial', 'b_spatial', 'attn_sinks', 'rel_bias', 'w_out', 'loss_target', 'm_pre_norm_g', 'm_post_norm_g', 'm_mem_norm_g', 'm_w_in', 'm_w_mem_kv', 'm_v_norm_g', 'm_v_norm_b', 'm_w_spatial', 'm_b_spatial', 'm_attn_sinks', 'm_rel_bias', 'm_w_out', 'v_pre_norm_g', 'v_post_norm_g', 'v_mem_norm_g', 'v_w_in', 'v_w_mem_kv', 'v_v_norm_g', 'v_v_norm_b', 'v_w_spatial', 'v_b_spatial', 'v_attn_sinks', 'v_rel_bias', 'v_w_out']
TWIN_OUTPUTS = ['loss', 'grad_x', 'grad_pre_norm_g', 'grad_post_norm_g', 'grad_mem_norm_g', 'grad_w_in', 'grad_w_mem_kv', 'grad_v_norm_g', 'grad_v_norm_b', 'grad_w_spatial', 'grad_b_spatial', 'grad_attn_sinks', 'grad_rel_bias', 'grad_w_out', 'delta_pre_norm_g', 'delta_post_norm_g', 'delta_mem_norm_g', 'delta_w_in', 'delta_w_mem_kv', 'delta_v_norm_g', 'delta_v_norm_b', 'delta_w_spatial', 'delta_b_spatial', 'delta_attn_sinks', 'delta_rel_bias', 'delta_w_out', 'new_m_pre_norm_g', 'new_m_post_norm_g', 'new_m_mem_norm_g', 'new_m_w_in', 'new_m_w_mem_kv', 'new_m_v_norm_g', 'new_m_v_norm_b', 'new_m_w_spatial', 'new_m_b_spatial', 'new_m_attn_sinks', 'new_m_rel_bias', 'new_m_w_out', 'new_v_pre_norm_g', 'new_v_post_norm_g', 'new_v_mem_norm_g', 'new_v_w_in', 'new_v_w_mem_kv', 'new_v_v_norm_g', 'new_v_v_norm_b', 'new_v_w_spatial', 'new_v_b_spatial', 'new_v_attn_sinks', 'new_v_rel_bias', 'new_v_w_out']
TWIN_LEAF_KINDS = {'loss': 'loss', 'grad_x': 'grad_x', 'grad_pre_norm_g': 'grad_w', 'grad_post_norm_g': 'grad_w', 'grad_mem_norm_g': 'grad_w', 'grad_w_in': 'grad_w', 'grad_w_mem_kv': 'grad_w', 'grad_v_norm_g': 'grad_w', 'grad_v_norm_b': 'grad_w', 'grad_w_spatial': 'grad_w', 'grad_b_spatial': 'grad_w', 'grad_attn_sinks': 'grad_w', 'grad_rel_bias': 'grad_w', 'grad_w_out': 'grad_w', 'delta_pre_norm_g': 'delta_w', 'delta_post_norm_g': 'delta_w', 'delta_mem_norm_g': 'delta_w', 'delta_w_in': 'delta_w', 'delta_w_mem_kv': 'delta_w', 'delta_v_norm_g': 'delta_w', 'delta_v_norm_b': 'delta_w', 'delta_w_spatial': 'delta_w', 'delta_b_spatial': 'delta_w', 'delta_attn_sinks': 'delta_w', 'delta_rel_bias': 'delta_w', 'delta_w_out': 'delta_w', 'new_m_pre_norm_g': 'new_m', 'new_m_post_norm_g': 'new_m', 'new_m_mem_norm_g': 'new_m', 'new_m_w_in': 'new_m', 'new_m_w_mem_kv': 'new_m', 'new_m_v_norm_g': 'new_m', 'new_m_v_norm_b': 'new_m', 'new_m_w_spatial': 'new_m', 'new_m_b_spatial': 'new_m', 'new_m_attn_sinks': 'new_m', 'new_m_rel_bias': 'new_m', 'new_m_w_out': 'new_m', 'new_v_pre_norm_g': 'new_v', 'new_v_post_norm_g': 'new_v', 'new_v_mem_norm_g': 'new_v', 'new_v_w_in': 'new_v', 'new_v_w_mem_kv': 'new_v', 'new_v_v_norm_g': 'new_v', 'new_v_v_norm_b': 'new_v', 'new_v_w_spatial': 'new_v', 'new_v_b_spatial': 'new_v', 'new_v_attn_sinks': 'new_v', 'new_v_rel_bias': 'new_v', 'new_v_w_out': 'new_v'}


def _forward(args):
    return _fwd_reference(*[args[k] for k in FWD_PARAMS])


def _output_shape():
    out = _jax.eval_shape(lambda: _forward(_fwd_setup_inputs(0)))
    return out.shape, out.dtype

N_MICROBATCH = 1
ADAM_LR = 0.001
ADAM_B1 = 0.9
ADAM_B2 = 0.999
ADAM_EPS = 1e-08
ADAM_WD = 0.01
ADAM_STEP = 10
PER_EXAMPLE_BATCH_AXIS = {'x': 0, 'mem': 0, 'loss_target': 0}
SHARED_INPUTS = []
_WEIGHT_DTYPES = {'pre_norm_g': _jnp.float32, 'post_norm_g': _jnp.float32, 'mem_norm_g': _jnp.float32, 'w_in': _jnp.float32, 'w_mem_kv': _jnp.float32, 'v_norm_g': _jnp.float32, 'v_norm_b': _jnp.float32, 'w_spatial': _jnp.float32, 'b_spatial': _jnp.float32, 'attn_sinks': _jnp.float32, 'rel_bias': _jnp.float32, 'w_out': _jnp.float32}
MOMENT_SCALE = {'pre_norm_g': 4.233450e-01, 'post_norm_g': 3.192690e+01, 'mem_norm_g': 4.565777e-02, 'w_in': 2.704969e-01, 'w_mem_kv': 5.728352e-02, 'v_norm_g': 1.817637e-01, 'v_norm_b': 1.815925e-01, 'w_spatial': 1.964086e-01, 'b_spatial': 2.851223e-01, 'attn_sinks': 8.224469e-02, 'rel_bias': 1.122018e-01, 'w_out': 3.331538e-01}


def _to_microbatches(a, axis):
    t = _jnp.moveaxis(a, axis, 0)
    t = t.reshape((N_MICROBATCH, t.shape[0] // N_MICROBATCH) + t.shape[1:])
    return _jnp.moveaxis(t, 1, axis + 1)


def setup_inputs(seed: int = 0) -> dict:
    inp = _fwd_setup_inputs(seed)
    key = _jax.random.fold_in(_jax.random.key(seed), 7919)
    shape, _ = _output_shape()
    out = dict(inp)
    out["loss_target"] = _jax.random.normal(_jax.random.fold_in(key, 0), shape, _jnp.float32)
    for i, name in enumerate(TWIN_WEIGHTS):
        w = inp[name].astype(_jnp.float32)
        if MOMENT_SCALE is None:
            s = _jnp.sqrt(_jnp.mean(_jnp.square(w)) + 1e-30)
        else:
            s = MOMENT_SCALE[name]
        km, kv = _jax.random.split(_jax.random.fold_in(key, i + 1))
        out[name] = w
        out["m_" + name] = s * _jax.random.normal(km, w.shape, _jnp.float32)
        out["v_" + name] = (s * s) * _jax.random.uniform(kv, w.shape, _jnp.float32, 0.5, 1.5)
    if N_MICROBATCH > 1:
        for name, axis in PER_EXAMPLE_BATCH_AXIS.items():
            out[name] = _to_microbatches(out[name], axis)
    return {'x': out['x'], 'mem': out['mem'], 'pre_norm_g': out['pre_norm_g'], 'post_norm_g': out['post_norm_g'], 'mem_norm_g': out['mem_norm_g'], 'w_in': out['w_in'], 'w_mem_kv': out['w_mem_kv'], 'v_norm_g': out['v_norm_g'], 'v_norm_b': out['v_norm_b'], 'w_spatial': out['w_spatial'], 'b_spatial': out['b_spatial'], 'attn_sinks': out['attn_sinks'], 'rel_bias': out['rel_bias'], 'w_out': out['w_out'], 'loss_target': out['loss_target'], 'm_pre_norm_g': out['m_pre_norm_g'], 'm_post_norm_g': out['m_post_norm_g'], 'm_mem_norm_g': out['m_mem_norm_g'], 'm_w_in': out['m_w_in'], 'm_w_mem_kv': out['m_w_mem_kv'], 'm_v_norm_g': out['m_v_norm_g'], 'm_v_norm_b': out['m_v_norm_b'], 'm_w_spatial': out['m_w_spatial'], 'm_b_spatial': out['m_b_spatial'], 'm_attn_sinks': out['m_attn_sinks'], 'm_rel_bias': out['m_rel_bias'], 'm_w_out': out['m_w_out'], 'v_pre_norm_g': out['v_pre_norm_g'], 'v_post_norm_g': out['v_post_norm_g'], 'v_mem_norm_g': out['v_mem_norm_g'], 'v_w_in': out['v_w_in'], 'v_w_mem_kv': out['v_w_mem_kv'], 'v_v_norm_g': out['v_v_norm_g'], 'v_v_norm_b': out['v_v_norm_b'], 'v_w_spatial': out['v_w_spatial'], 'v_b_spatial': out['v_b_spatial'], 'v_attn_sinks': out['v_attn_sinks'], 'v_rel_bias': out['v_rel_bias'], 'v_w_out': out['v_w_out']}


def _loss(weights, diff, rest, loss_target):
    with _jax.named_scope("forward"):
        args = {**rest, TWIN_DIFF_INPUT: diff, **{k: w.astype(_WEIGHT_DTYPES[k]) for k, w in weights.items()}}
        y = _forward(args)
    with _jax.named_scope("loss_head"):
        err = _jnp.square(y.astype(_jnp.float32) - loss_target)
        return 0.5 * _jnp.sum(_jnp.mean(err, axis=-1)) if err.ndim else 0.5 * err


def _adamw(w, g, m, v):
    m = ADAM_B1 * m + (1.0 - ADAM_B1) * g
    v = ADAM_B2 * v + (1.0 - ADAM_B2) * _jnp.square(g)
    m_hat = m / (1.0 - ADAM_B1 ** ADAM_STEP)
    v_hat = v / (1.0 - ADAM_B2 ** ADAM_STEP)
    delta = -ADAM_LR * (m_hat / (_jnp.sqrt(v_hat) + ADAM_EPS) + ADAM_WD * w)
    return delta, m, v


def reference(x, mem, pre_norm_g, post_norm_g, mem_norm_g, w_in, w_mem_kv, v_norm_g, v_norm_b, w_spatial, b_spatial, attn_sinks, rel_bias, w_out, loss_target, m_pre_norm_g, m_post_norm_g, m_mem_norm_g, m_w_in, m_w_mem_kv, m_v_norm_g, m_v_norm_b, m_w_spatial, m_b_spatial, m_attn_sinks, m_rel_bias, m_w_out, v_pre_norm_g, v_post_norm_g, v_mem_norm_g, v_w_in, v_w_mem_kv, v_v_norm_g, v_v_norm_b, v_w_spatial, v_b_spatial, v_attn_sinks, v_rel_bias, v_w_out):
    given = dict(x=x, mem=mem, pre_norm_g=pre_norm_g, post_norm_g=post_norm_g, mem_norm_g=mem_norm_g, w_in=w_in, w_mem_kv=w_mem_kv, v_norm_g=v_norm_g, v_norm_b=v_norm_b, w_spatial=w_spatial, b_spatial=b_spatial, attn_sinks=attn_sinks, rel_bias=rel_bias, w_out=w_out, loss_target=loss_target, m_pre_norm_g=m_pre_norm_g, m_post_norm_g=m_post_norm_g, m_mem_norm_g=m_mem_norm_g, m_w_in=m_w_in, m_w_mem_kv=m_w_mem_kv, m_v_norm_g=m_v_norm_g, m_v_norm_b=m_v_norm_b, m_w_spatial=m_w_spatial, m_b_spatial=m_b_spatial, m_attn_sinks=m_attn_sinks, m_rel_bias=m_rel_bias, m_w_out=m_w_out, v_pre_norm_g=v_pre_norm_g, v_post_norm_g=v_post_norm_g, v_mem_norm_g=v_mem_norm_g, v_w_in=v_w_in, v_w_mem_kv=v_w_mem_kv, v_v_norm_g=v_v_norm_g, v_v_norm_b=v_v_norm_b, v_w_spatial=v_w_spatial, v_b_spatial=v_b_spatial, v_attn_sinks=v_attn_sinks, v_rel_bias=v_rel_bias, v_w_out=v_w_out)
    weights = {n: given[n] for n in TWIN_WEIGHTS}
    shared = {n: given[n] for n in SHARED_INPUTS}
    per_example = {n: given[n] for n in ['x', 'mem']}
    grad_fn = _jax.value_and_grad(_loss, argnums=(0, 1))

    def one_microbatch(ex, loss_target):
        ex = dict(ex)
        diff = ex.pop(TWIN_DIFF_INPUT)
        return grad_fn(weights, diff, {**shared, **ex}, loss_target)

    if N_MICROBATCH == 1:
        loss, (grad_w, grad_x) = one_microbatch(per_example, given["loss_target"])
    else:
        def body(carry, xs):
            loss_sum, grad_sum = carry
            l_k, (gw_k, gx_k) = one_microbatch(xs[0], xs[1])
            with _jax.named_scope("update"):
                return (loss_sum + l_k, _jax.tree.map(_jnp.add, grad_sum, gw_k)), gx_k

        init = (_jnp.zeros((), _jnp.float32), _jax.tree.map(_jnp.zeros_like, weights))
        (loss, grad_w), grad_x = _jax.lax.scan(body, init, (per_example, given["loss_target"]))
    with _jax.named_scope("update"):
        delta_w, new_m, new_v = {}, {}, {}
        for n in TWIN_WEIGHTS:
            delta_w[n], new_m[n], new_v[n] = _adamw(weights[n], grad_w[n], given["m_" + n], given["v_" + n])
    return (loss, grad_x, *[grad_w[n] for n in TWIN_WEIGHTS], *[delta_w[n] for n in TWIN_WEIGHTS],
            *[new_m[n] for n in TWIN_WEIGHTS], *[new_v[n] for n in TWIN_WEIGHTS])
```

```python
import functools

import numpy as np
import jax
import jax.numpy as jnp
from jax import lax
from jax.experimental import pallas as pl
from jax.experimental.pallas import tpu as pltpu

F32 = jnp.float32
BF16 = jnp.bfloat16
MESH = pl.DeviceIdType.MESH
ALL_AXES = ("x", "y", "c")

D_MODEL = 1024
CHUNK = 128
A_WIDTH = 512
A_GROUPS = 4
SWA_WIDTH = 256
KV_WIDTH = 128
MEM_WIDTH = 256
MEM_LEN = 256
MIX_WIDTH = 1024
IN_WIDTH = 2816
N_BUCKETS = 32
MAX_DISTANCE = 128
EPS = 1e-6
NEG = -1e30
QK_SCALE = 0.125
HALF_HEAD_PAIR = 64

ADAM_LR = 0.001
ADAM_B1 = 0.9
ADAM_B2 = 0.999
ADAM_EPS = 1e-08
ADAM_WD = 0.01
ADAM_STEP = 10

N_CHIPS = 4
N_DEV = 8
TILE_CHUNKS = 2
TILE = TILE_CHUNKS * CHUNK
PROJ_TILE = 256
VMEM_LIMIT = 56 * 1024 * 1024

SMALL_A_ROWS = 8
ROW_WS = 0
ROW_BS = 512
ROW_SINK = 520
ROW_REL = 528
SMALL_B_ROWS = 560


def _mm(a, b):
    return lax.dot_general(a, b, (((1,), (0,)), ((), ())), preferred_element_type=F32)


def _mm_nt(a, b):
    return lax.dot_general(a, b, (((1,), (1,)), ((), ())), preferred_element_type=F32)


def _mm_tn(a, b):
    return lax.dot_general(a, b, (((0,), (0,)), ((), ())), preferred_element_type=F32)


def _bucket_map():
    qi = np.arange(CHUNK)[:, None]
    kj = np.arange(2 * CHUNK)[None, :]
    n = np.maximum(qi + CHUNK - kj, 0)
    max_exact = N_BUCKETS // 2
    large = max_exact + (np.log(np.maximum(n, 1) / max_exact) / np.log(MAX_DISTANCE / max_exact)
                         * (N_BUCKETS - max_exact)).astype(np.int32)
    large = np.minimum(large, N_BUCKETS - 1)
    return np.where(n < max_exact, n, large).astype(np.int32)


_GELU_C = 0.7978845608028654
_GELU_A = 0.044715


def _gelu(x):
    t = jnp.tanh(_GELU_C * (x + _GELU_A * x * x * x))
    return 0.5 * x * (1.0 + t), t


def _gelu_grad(x, t):
    return 0.5 * (1.0 + t) + 0.5 * x * (1.0 - t * t) * (_GELU_C * (1.0 + 3.0 * _GELU_A * x * x))


def _sigmoid(x):
    return 1.0 / (1.0 + jnp.exp(-x))


def _lane_lo(shape):
    return lax.broadcasted_iota(jnp.int32, shape, 1) < HALF_HEAD_PAIR


def _swa_variants(t):
    lo = _lane_lo(t.shape)
    tr = pltpu.roll(t, HALF_HEAD_PAIR, 1)
    zero = jnp.zeros_like(t)
    return (jnp.where(lo, t, zero).astype(BF16), jnp.where(lo, zero, tr).astype(BF16),
            jnp.where(lo, tr, zero).astype(BF16), jnp.where(lo, zero, t).astype(BF16))


def _swa_unvariants(d0, d1, d2, d3):
    lo = _lane_lo(d0.shape)
    zero = jnp.zeros_like(d0)
    rolled = jnp.where(lo, zero, d1) + jnp.where(lo, d2, zero)
    return jnp.where(lo, d0, zero) + jnp.where(lo, zero, d3) + pltpu.roll(rolled, HALF_HEAD_PAIR, 1)


def _mem_variants(t):
    out = []
    for pair in range(2):
        tp = t[:, pair * 128:(pair + 1) * 128]
        lo = _lane_lo(tp.shape)
        zero = jnp.zeros_like(tp)
        out.append(jnp.where(lo, tp, zero).astype(BF16))
        out.append(jnp.where(lo, zero, tp).astype(BF16))
    return out


def _mem_unvariants(d0, d1, d2, d3):
    lo = _lane_lo(d0.shape)
    return jnp.concatenate([jnp.where(lo, d0, d1), jnp.where(lo, d2, d3)], axis=-1)


def _softmax(logits, sink):
    m = jnp.max(logits, axis=-1, keepdims=True)
    if sink is not None:
        m = jnp.maximum(m, sink)
    p = jnp.exp(logits - m)
    den = jnp.sum(p, axis=-1, keepdims=True)
    if sink is None:
        return p * (1.0 / den), None
    es = jnp.exp(sink - m)
    inv = 1.0 / (den + es)
    return p * inv, es * inv


def _band_mask(prev_valid):
    qi = lax.broadcasted_iota(jnp.int32, (CHUNK, 2 * CHUNK), 0)
    kj = lax.broadcasted_iota(jnp.int32, (CHUNK, 2 * CHUNK), 1)
    in_prev = (kj < CHUNK) & (kj > qi)
    in_cur = (kj >= CHUNK) & (kj - CHUNK <= qi)
    if prev_valid is True:
        return in_prev | in_cur
    return (in_prev & prev_valid) | in_cur


def _causal_weights(ws_ref):
    row = lax.broadcasted_iota(jnp.int32, (CHUNK, CHUNK), 0)
    col = lax.broadcasted_iota(jnp.int32, (CHUNK, CHUNK), 1)
    return [jnp.where(row >= col, ws_ref[g], 0.0).astype(BF16) for g in range(A_GROUPS)]


def _chunk_forward(au, av, q, kband, vband, mq, mk_v, mv_v, vg, vb, wm, bs, sinks, bias_ref, mask):
    gu, tu = _gelu(au)
    gv, tv = _gelu(av)
    ya, a_res = [], []
    for g in range(A_GROUPS):
        sl = slice(g * 128, (g + 1) * 128)
        xg = gv[:, sl]
        xc = xg - jnp.mean(xg, axis=-1, keepdims=True)
        rstd = lax.rsqrt(jnp.mean(xc * xc, axis=-1, keepdims=True) + EPS)
        xhat = xc * rstd
        vn = (xhat * vg[:, sl] + vb[:, sl]).astype(BF16)
        s = _mm(wm[g], vn) + bs[g]
        ya.append(gu[:, sl] * s)
        a_res.append((xhat, rstd, vn, s))

    qp = (q[:, :128].astype(BF16), q[:, 128:].astype(BF16))
    k_v = _swa_variants(kband)
    v_v = _swa_variants(vband)
    b_p, b_sink, b_out = [], [], []
    for h in range(4):
        logits = _mm_nt(qp[h // 2], k_v[h]) * QK_SCALE + bias_ref[h]
        p, ps = _softmax(jnp.where(mask, logits, NEG), sinks[h])
        b_p.append(p)
        b_sink.append(ps)
        b_out.append(_mm(p.astype(BF16), v_v[h]))
    yb = jnp.concatenate([b_out[0] + b_out[1], b_out[2] + b_out[3]], axis=-1)

    mqp = (mq[:, :128].astype(BF16), mq[:, 128:].astype(BF16))
    c_p, c_out = [], []
    for h in range(4):
        p, _ = _softmax(_mm_nt(mqp[h // 2], mk_v[h]) * QK_SCALE, None)
        c_p.append(p)
        c_out.append(_mm(p.astype(BF16), mv_v[h]))
    yc = jnp.concatenate([c_out[0] + c_out[1], c_out[2] + c_out[3]], axis=-1)

    ycat = jnp.concatenate(ya + [yb, yc], axis=-1)
    return ycat, dict(gu=gu, tu=tu, tv=tv, a_res=a_res, qp=qp, k_v=k_v, v_v=v_v, b_p=b_p, b_sink=b_sink,
                      mqp=mqp, c_p=c_p)


def _tile_specs(n_tiles_ex, width):
    return pl.BlockSpec((TILE, width), lambda b, i: (b * n_tiles_ex + jnp.minimum(i, n_tiles_ex - 1), 0))


def _prev_chunk_spec(n_tiles_ex, width):
    def index(b, i):
        chunk = TILE_CHUNKS * jnp.minimum(i, n_tiles_ex - 1)
        return (b * n_tiles_ex * TILE_CHUNKS + jnp.maximum(chunk - 1, 0), 0)
    return pl.BlockSpec((CHUNK, width), index)


def _full_spec(shape):
    zeros = (0,) * len(shape)
    return pl.BlockSpec(shape, lambda *_: zeros)


SMEM_SPEC = pl.BlockSpec(memory_space=pltpu.SMEM)
ANY_SPEC = pl.BlockSpec(memory_space=pl.ANY)
VMEM_SPEC = pl.BlockSpec(memory_space=pltpu.VMEM)


def _make_bias(rel_bias, buckets):
    def body(rel_ref, bk_ref, out_ref):
        bk = bk_ref[...]
        for h in range(4):
            acc = jnp.zeros((CHUNK, 2 * CHUNK), F32)
            for b in range(N_BUCKETS):
                acc = jnp.where(bk == b, rel_ref[b, h], acc)
            out_ref[h] = acc

    return pl.pallas_call(
        body, name="make_bias", out_shape=jax.ShapeDtypeStruct((4, CHUNK, 2 * CHUNK), F32),
        in_specs=[SMEM_SPEC, VMEM_SPEC], out_specs=VMEM_SPEC,
    )(rel_bias, buckets)


def _gather_weights(w_in_s, w_mkv_s, w_out_s):
    shapes = [w_in_s.shape, w_mkv_s.shape, w_out_s.shape]
    n_w = len(shapes)

    def body(win_ref, wmkv_ref, wout_ref, gin_ref, gmkv_ref, gout_ref, send_sems, recv_sems):
        x, y, c = lax.axis_index("x"), lax.axis_index("y"), lax.axis_index("c")
        me, sibling = (x, y, c), (x, y, 1 - c)
        chips = [(1 - x, y), (x, 1 - y), (1 - x, 1 - y)]
        ins = [win_ref, wmkv_ref, wout_ref]
        outs = [gin_ref, gmkv_ref, gout_ref]
        my_shard = 2 * x + y
        for w in range(n_w):
            outs[w][my_shard] = ins[w][...].astype(BF16)

        def copy(k, w, shard, half, to):
            rows = shapes[w][0] // 2
            ref = outs[w].at[shard, pl.ds(half * rows, rows), :]
            return pltpu.make_async_remote_copy(src_ref=ref, dst_ref=ref, send_sem=send_sems.at[k],
                                                recv_sem=recv_sems.at[k], device_id=to, device_id_type=MESH)

        pairs = [(w, j) for w in range(n_w) for j in range(3)]
        first = [copy(3 * w + j, w, my_shard, c, (*chips[j], c)) for w, j in pairs]
        for cp in first:
            cp.start()
        passed = []
        for w, j in pairs:
            shard = 2 * chips[j][0] + chips[j][1]
            copy(3 * w + j, w, shard, c, me).wait_recv()
            fwd = copy(9 + 3 * w + j, w, shard, c, sibling)
            fwd.start()
            passed.append(fwd)
        for w, j in pairs:
            shard = 2 * chips[j][0] + chips[j][1]
            copy(9 + 3 * w + j, w, shard, 1 - c, me).wait_recv()
        for cp in first + passed:
            cp.wait_send()

    return pl.pallas_call(
        body, name="gather_weights",
        out_shape=[jax.ShapeDtypeStruct((N_CHIPS,) + s, BF16) for s in shapes],
        in_specs=[VMEM_SPEC] * 3, out_specs=[VMEM_SPEC] * 3,
        scratch_shapes=[pltpu.SemaphoreType.DMA((18,)), pltpu.SemaphoreType.DMA((18,))],
        compiler_params=pltpu.CompilerParams(vmem_limit_bytes=VMEM_LIMIT),
    )(w_in_s, w_mkv_s, w_out_s)


def _memkv_forward(mem, g_mem, w_mkv):
    n_ex = mem.shape[0]

    def body(mem_ref, g_ref, w_ref, out_ref):
        m = mem_ref[0]
        r = lax.rsqrt(jnp.mean(m * m, axis=-1, keepdims=True) + EPS)
        out_ref[0] = _mm((m * r * g_ref[...]).astype(BF16), w_ref[...])

    return pl.pallas_call(
        body, name="memkv_forward", grid=(n_ex,),
        out_shape=jax.ShapeDtypeStruct((n_ex, MEM_LEN, 2 * MEM_WIDTH), F32),
        in_specs=[pl.BlockSpec((1, MEM_LEN, D_MODEL), lambda b: (b, 0, 0)), _full_spec((1, D_MODEL)),
                  _full_spec((D_MODEL, 2 * MEM_WIDTH))],
        out_specs=pl.BlockSpec((1, MEM_LEN, 2 * MEM_WIDTH), lambda b: (b, 0, 0)),
    )(mem, g_mem, w_mkv)


PROJ_WIDTHS = (A_WIDTH, A_WIDTH, SWA_WIDTH, KV_WIDTH, KV_WIDTH, MEM_WIDTH, MIX_WIDTH)
PROJ_OFFSETS = tuple(int(v) for v in np.cumsum((0,) + PROJ_WIDTHS))


def _forward_projection(x2, g_pre, w_in):
    n_tok = x2.shape[0]

    def body(x_ref, g_ref, w_ref, *out_refs):
        xv = x_ref[...]
        r = lax.rsqrt(jnp.mean(xv * xv, axis=-1, keepdims=True) + EPS)
        proj = _mm((xv * r * g_ref[...]).astype(BF16), w_ref[...])
        for k, ref in enumerate(out_refs):
            ref[...] = proj[:, PROJ_OFFSETS[k]:PROJ_OFFSETS[k + 1]]

    return pl.pallas_call(
        body, name="forward_projection", grid=(n_tok // PROJ_TILE,),
        out_shape=[jax.ShapeDtypeStruct((n_tok, w), F32) for w in PROJ_WIDTHS],
        in_specs=[pl.BlockSpec((PROJ_TILE, D_MODEL), lambda i: (i, 0)), _full_spec((1, D_MODEL)),
                  _full_spec((D_MODEL, IN_WIDTH))],
        out_specs=[pl.BlockSpec((PROJ_TILE, w), lambda i: (i, 0)) for w in PROJ_WIDTHS],
        compiler_params=pltpu.CompilerParams(vmem_limit_bytes=VMEM_LIMIT),
    )(x2, g_pre, w_in)


def _load_chunk(j, i, sk_ref, sv_ref, skp_ref, svp_ref):
    rows = slice(j * CHUNK, (j + 1) * CHUNK)
    if j == 0:
        k_prev, v_prev, prev_valid = skp_ref[...], svp_ref[...], i > 0
    else:
        prev = slice((j - 1) * CHUNK, j * CHUNK)
        k_prev, v_prev, prev_valid = sk_ref[prev, :], sv_ref[prev, :], True
    kband = jnp.concatenate([k_prev, sk_ref[rows, :]], axis=0)
    vband = jnp.concatenate([v_prev, sv_ref[rows, :]], axis=0)
    return rows, kband, vband, _band_mask(prev_valid)


def _forward_mix(parts, mkv, x2, tgt2, v_g, v_b, w_sp, b_sp, sinks, bias, w_out, g_post, n_ex, seq):
    n_tiles_ex = seq // TILE
    n_tok = n_ex * seq
    au, av, sq, sk, sv, mq, z = parts

    def body(au_ref, av_ref, sq_ref, sk_ref, sv_ref, skp_ref, svp_ref, mq_ref, z_ref, mkv_ref, x_ref, tgt_ref,
             vg_ref, vb_ref, ws_ref, bs_ref, sink_ref, bias_ref, wout_ref, gpost_ref,
             dout_ref, do_ref, loss_ref, dgpost_ref):
        b, i = pl.program_id(0), pl.program_id(1)

        @pl.when((b == 0) & (i == 0))
        def _():
            loss_ref[...] = jnp.zeros_like(loss_ref)
            dgpost_ref[...] = jnp.zeros_like(dgpost_ref)

        wm = _causal_weights(ws_ref)
        bs = [bs_ref[g] for g in range(A_GROUPS)]
        sinks_s = [sink_ref[0, h] for h in range(4)]
        vg, vb = vg_ref[...], vb_ref[...]
        mkv_v = mkv_ref[0]
        mk_v = _mem_variants(mkv_v[:, :MEM_WIDTH])
        mv_v = _mem_variants(mkv_v[:, MEM_WIDTH:])
        ycat = []
        for j in range(TILE_CHUNKS):
            rows, kband, vband, mask = _load_chunk(j, i, sk_ref, sv_ref, skp_ref, svp_ref)
            yc, _ = _chunk_forward(au_ref[rows, :], av_ref[rows, :], sq_ref[rows, :], kband, vband, mq_ref[rows, :],
                                   mk_v, mv_v, vg, vb, wm, bs, sinks_s, bias_ref, mask)
            ycat.append(yc)
        ycat = jnp.concatenate(ycat, axis=0)
        zv = z_ref[...]
        y = ycat * (zv * _sigmoid(zv))
        o = _mm(y.astype(BF16), wout_ref[...])
        r2 = lax.rsqrt(jnp.mean(o * o, axis=-1, keepdims=True) + EPS)
        nrm = o * r2
        gp = gpost_ref[...]
        diff = x_ref[...] + nrm * gp - tgt_ref[...]
        loss_ref[...] += jnp.sum(diff * diff) * (0.5 / D_MODEL)
        dout = diff * (1.0 / D_MODEL)
        dout_ref[...] = dout
        dgpost_ref[...] += jnp.sum(dout * nrm, axis=0, keepdims=True)
        dn = dout * gp
        do_ref[...] = r2 * (dn - nrm * jnp.mean(dn * nrm, axis=-1, keepdims=True))

    tile = functools.partial(_tile_specs, n_tiles_ex)
    prev = functools.partial(_prev_chunk_spec, n_tiles_ex)
    return pl.pallas_call(
        body, name="forward_mix", grid=(n_ex, n_tiles_ex),
        out_shape=[jax.ShapeDtypeStruct((n_tok, D_MODEL), F32), jax.ShapeDtypeStruct((n_tok, D_MODEL), F32),
                   jax.ShapeDtypeStruct((1, 128), F32), jax.ShapeDtypeStruct((1, D_MODEL), F32)],
        in_specs=[tile(A_WIDTH), tile(A_WIDTH), tile(SWA_WIDTH), tile(KV_WIDTH), tile(KV_WIDTH),
                  prev(KV_WIDTH), prev(KV_WIDTH), tile(MEM_WIDTH), tile(MIX_WIDTH),
                  pl.BlockSpec((1, MEM_LEN, 2 * MEM_WIDTH), lambda b, i: (b, 0, 0)),
                  tile(D_MODEL), tile(D_MODEL),
                  _full_spec((1, A_WIDTH)), _full_spec((1, A_WIDTH)), _full_spec((A_GROUPS, CHUNK, CHUNK)),
                  _full_spec((A_GROUPS, CHUNK, 1)), SMEM_SPEC, _full_spec((4, CHUNK, 2 * CHUNK)),
                  _full_spec((MIX_WIDTH, D_MODEL)), _full_spec((1, D_MODEL))],
        out_specs=[tile(D_MODEL), tile(D_MODEL), _full_spec((1, 128)), _full_spec((1, D_MODEL))],
        compiler_params=pltpu.CompilerParams(vmem_limit_bytes=VMEM_LIMIT),
    )(au, av, sq, sk, sv, sk, sv, mq, z, mkv, x2, tgt2, v_g, v_b, w_sp, b_sp, sinks, bias, w_out, g_post)


def _backward_mix(parts, mkv, do, v_g, v_b, w_sp, b_sp, sinks, bias, w_out, n_ex, seq):
    n_tiles_ex = seq // TILE
    n_tok = n_ex * seq
    au, av, sq, sk, sv, mq, z = parts

    def body(do_ref, au_ref, av_ref, sq_ref, sk_ref, sv_ref, skp_ref, svp_ref, mq_ref, z_ref, mkv_ref,
             vg_ref, vb_ref, ws_ref, bs_ref, sink_ref, bias_ref, wout_ref,
             dau_ref, dav_ref, dsq_ref, dsk_ref, dsv_ref, dmq_ref, dz_ref, dmkv_ref,
             dwout_ref, dvg_ref, dvb_ref, dws_ref, dbs_ref, dsink_ref, drel_ref,
             carry_k, carry_v):
        b, i = pl.program_id(0), pl.program_id(1)

        @pl.when((b == 0) & (i == 0))
        def _():
            for ref in (dwout_ref, dvg_ref, dvb_ref, dws_ref, dbs_ref, dsink_ref, drel_ref):
                ref[...] = jnp.zeros_like(ref)

        @pl.when(i == 0)
        def _():
            dmkv_ref[...] = jnp.zeros_like(dmkv_ref)
            carry_k[...] = jnp.zeros_like(carry_k)
            carry_v[...] = jnp.zeros_like(carry_v)

        @pl.when(i < n_tiles_ex)
        def _():
            wm = _causal_weights(ws_ref)
            bs = [bs_ref[g] for g in range(A_GROUPS)]
            sinks_s = [sink_ref[0, h] for h in range(4)]
            vg, vb = vg_ref[...], vb_ref[...]
            mkv_v = mkv_ref[0]
            mk_v = _mem_variants(mkv_v[:, :MEM_WIDTH])
            mv_v = _mem_variants(mkv_v[:, MEM_WIDTH:])
            do_v = do_ref[...]
            do_b = do_v.astype(BF16)
            dy = _mm_nt(do_b, wout_ref[...])
            zv = z_ref[...]
            sig = _sigmoid(zv)
            sz = zv * sig
            dycat_all = dy * sz
            lane4 = lax.broadcasted_iota(jnp.int32, (1, 128), 1)

            ycat, dk_parts, dv_parts = [], [], []
            dmk_acc = [jnp.zeros((MEM_LEN, 128), F32) for _ in range(4)]
            dmv_acc = [jnp.zeros((MEM_LEN, 128), F32) for _ in range(4)]
            dsink_vec = jnp.zeros((1, 128), F32)
            for j in range(TILE_CHUNKS):
                rows, kband, vband, mask = _load_chunk(j, i, sk_ref, sv_ref, skp_ref, svp_ref)
                au_v, av_v = au_ref[rows, :], av_ref[rows, :]
                yc, res = _chunk_forward(au_v, av_v, sq_ref[rows, :], kband, vband, mq_ref[rows, :],
                                         mk_v, mv_v, vg, vb, wm, bs, sinks_s, bias_ref, mask)
                ycat.append(yc)
                dyc = dycat_all[rows, :]

                dgu, dgv = [], []
                for g in range(A_GROUPS):
                    sl = slice(g * 128, (g + 1) * 128)
                    xhat, rstd, vn, s = res["a_res"][g]
                    dya = dyc[:, sl]
                    dgu.append(dya * s)
                    ds = dya * res["gu"][:, sl]
                    dbs_ref[:, sl] += ds
                    ds_b = ds.astype(BF16)
                    dws_ref[g] += _mm_nt(ds_b, vn)
                    dvn = _mm_tn(wm[g], ds_b)
                    dvg_ref[:, sl] += jnp.sum(dvn * xhat, axis=0, keepdims=True)
                    dvb_ref[:, sl] += jnp.sum(dvn, axis=0, keepdims=True)
                    dxh = dvn * vg[:, sl]
                    dgv.append(rstd * (dxh - jnp.mean(dxh, axis=-1, keepdims=True)
                                       - xhat * jnp.mean(dxh * xhat, axis=-1, keepdims=True)))
                dau_ref[rows, :] = jnp.concatenate(dgu, axis=-1) * _gelu_grad(au_v, res["tu"])
                dav_ref[rows, :] = jnp.concatenate(dgv, axis=-1) * _gelu_grad(av_v, res["tv"])

                dob = (dyc[:, A_WIDTH:A_WIDTH + 128].astype(BF16), dyc[:, A_WIDTH + 128:A_WIDTH + 256].astype(BF16))
                dq = [jnp.zeros((CHUNK, 128), F32), jnp.zeros((CHUNK, 128), F32)]
                dk_h, dv_h = [], []
                for h in range(4):
                    p, ps = res["b_p"][h], res["b_sink"][h]
                    dp = _mm_nt(dob[h // 2], res["v_v"][h])
                    delta = jnp.sum(p * dp, axis=-1, keepdims=True)
                    dl = p * (dp - delta)
                    dsink_vec = dsink_vec + jnp.where(lane4 == h, -jnp.sum(ps * delta), 0.0)
                    drel_ref[h] += dl
                    dl_b = dl.astype(BF16)
                    dq[h // 2] = dq[h // 2] + _mm(dl_b, res["k_v"][h])
                    dk_h.append(_mm_tn(dl_b, res["qp"][h // 2]))
                    dv_h.append(_mm_tn(p.astype(BF16), dob[h // 2]))
                dsq_ref[rows, :] = jnp.concatenate(dq, axis=-1) * QK_SCALE
                dk_parts.append(_swa_unvariants(*dk_h) * QK_SCALE)
                dv_parts.append(_swa_unvariants(*dv_h))

                dcb = (dyc[:, 768:896].astype(BF16), dyc[:, 896:1024].astype(BF16))
                dmq = [jnp.zeros((CHUNK, 128), F32), jnp.zeros((CHUNK, 128), F32)]
                for h in range(4):
                    p = res["c_p"][h]
                    dp = _mm_nt(dcb[h // 2], mv_v[h])
                    dl = p * (dp - jnp.sum(p * dp, axis=-1, keepdims=True))
                    dl_b = dl.astype(BF16)
                    dmq[h // 2] = dmq[h // 2] + _mm(dl_b, mk_v[h])
                    dmk_acc[h] = dmk_acc[h] + _mm_tn(dl_b, res["mqp"][h // 2])
                    dmv_acc[h] = dmv_acc[h] + _mm_tn(p.astype(BF16), dcb[h // 2])
                dmq_ref[rows, :] = jnp.concatenate(dmq, axis=-1) * QK_SCALE

            ycat = jnp.concatenate(ycat, axis=0)
            dwout_ref[...] += _mm_tn((ycat * sz).astype(BF16), do_b)
            dz_ref[...] = dy * ycat * (sig * (1.0 + zv * (1.0 - sig)))
            dsink_ref[...] += dsink_vec
            dmkv_ref[0] += jnp.concatenate([_mem_unvariants(*dmk_acc) * QK_SCALE, _mem_unvariants(*dmv_acc)], axis=-1)

            for parts_c, carry, out_ref in ((dk_parts, carry_k, dsk_ref), (dv_parts, carry_v, dsv_ref)):
                @pl.when(i > 0)
                def _():
                    out_ref[...] = carry[...] + jnp.concatenate(
                        [jnp.zeros((TILE - CHUNK, KV_WIDTH), F32), parts_c[0][:CHUNK]], axis=0)
                new = [parts_c[0][CHUNK:]]
                for j in range(1, TILE_CHUNKS):
                    new[-1] = new[-1] + parts_c[j][:CHUNK]
                    new.append(parts_c[j][CHUNK:])
                carry[...] = jnp.concatenate(new, axis=0)

        @pl.when(i == n_tiles_ex)
        def _():
            dsk_ref[...] = carry_k[...]
            dsv_ref[...] = carry_v[...]

    tile = functools.partial(_tile_specs, n_tiles_ex)
    prev = functools.partial(_prev_chunk_spec, n_tiles_ex)
    late = pl.BlockSpec((TILE, KV_WIDTH), lambda b, i: (b * n_tiles_ex + jnp.maximum(i - 1, 0), 0))
    tok = lambda w: jax.ShapeDtypeStruct((n_tok, w), F32)
    return pl.pallas_call(
        body, name="backward_mix", grid=(n_ex, n_tiles_ex + 1),
        out_shape=[tok(A_WIDTH), tok(A_WIDTH), tok(SWA_WIDTH), tok(KV_WIDTH), tok(KV_WIDTH), tok(MEM_WIDTH),
                   tok(MIX_WIDTH), jax.ShapeDtypeStruct((n_ex, MEM_LEN, 2 * MEM_WIDTH), F32),
                   jax.ShapeDtypeStruct((MIX_WIDTH, D_MODEL), F32), jax.ShapeDtypeStruct((1, A_WIDTH), F32),
                   jax.ShapeDtypeStruct((1, A_WIDTH), F32), jax.ShapeDtypeStruct((A_GROUPS, CHUNK, CHUNK), F32),
                   jax.ShapeDtypeStruct((CHUNK, A_WIDTH), F32), jax.ShapeDtypeStruct((1, 128), F32),
                   jax.ShapeDtypeStruct((4, CHUNK, 2 * CHUNK), F32)],
        in_specs=[tile(D_MODEL), tile(A_WIDTH), tile(A_WIDTH), tile(SWA_WIDTH), tile(KV_WIDTH), tile(KV_WIDTH),
                  prev(KV_WIDTH), prev(KV_WIDTH), tile(MEM_WIDTH), tile(MIX_WIDTH),
                  pl.BlockSpec((1, MEM_LEN, 2 * MEM_WIDTH), lambda b, i: (b, 0, 0)),
                  _full_spec((1, A_WIDTH)), _full_spec((1, A_WIDTH)), _full_spec((A_GROUPS, CHUNK, CHUNK)),
                  _full_spec((A_GROUPS, CHUNK, 1)), SMEM_SPEC, _full_spec((4, CHUNK, 2 * CHUNK)),
                  _full_spec((MIX_WIDTH, D_MODEL))],
        out_specs=[tile(A_WIDTH), tile(A_WIDTH), tile(SWA_WIDTH), late, late, tile(MEM_WIDTH), tile(MIX_WIDTH),
                   pl.BlockSpec((1, MEM_LEN, 2 * MEM_WIDTH), lambda b, i: (b, 0, 0)),
                   _full_spec((MIX_WIDTH, D_MODEL)), _full_spec((1, A_WIDTH)), _full_spec((1, A_WIDTH)),
                   _full_spec((A_GROUPS, CHUNK, CHUNK)), _full_spec((CHUNK, A_WIDTH)), _full_spec((1, 128)),
                   _full_spec((4, CHUNK, 2 * CHUNK))],
        scratch_shapes=[pltpu.VMEM((TILE, KV_WIDTH), F32), pltpu.VMEM((TILE, KV_WIDTH), F32)],
        compiler_params=pltpu.CompilerParams(vmem_limit_bytes=VMEM_LIMIT),
    )(do, au, av, sq, sk, sv, sk, sv, mq, z, mkv, v_g, v_b, w_sp, b_sp, sinks, bias, w_out)


def _backward_projection(x2, dout, dparts, g_pre, w_in):
    n_tok = x2.shape[0]
    n_steps = n_tok // PROJ_TILE

    def body(x_ref, dout_ref, dau, dav, dsq, dsk, dsv, dmq, dz, g_ref, w_hbm,
             dx_ref, dwin_hbm, dgpre_ref, w_vmem, acc, sem):
        step = pl.program_id(0)

        @pl.when(step == 0)
        def _():
            load = pltpu.make_async_copy(w_hbm, w_vmem, sem)
            load.start()
            acc[...] = jnp.zeros_like(acc)
            dgpre_ref[...] = jnp.zeros_like(dgpre_ref)
            load.wait()

        xv = x_ref[...]
        r = lax.rsqrt(jnp.mean(xv * xv, axis=-1, keepdims=True) + EPS)
        xn = xv * r
        g = g_ref[...]
        h_b = (xn * g).astype(BF16)
        dh = jnp.zeros((PROJ_TILE, D_MODEL), F32)
        for k, ref in enumerate((dau, dav, dsq, dsk, dsv, dmq, dz)):
            cols = slice(PROJ_OFFSETS[k], PROJ_OFFSETS[k + 1])
            dp = ref[...].astype(BF16)
            acc[:, cols] += _mm_tn(h_b, dp)
            dh = dh + _mm_nt(dp, w_vmem[:, cols])
        dgpre_ref[...] += jnp.sum(dh * xn, axis=0, keepdims=True)
        dhg = dh * g
        dx_ref[...] = r * (dhg - xn * jnp.mean(dhg * xn, axis=-1, keepdims=True)) + dout_ref[...]

        @pl.when(step == n_steps - 1)
        def _():
            store = pltpu.make_async_copy(acc, dwin_hbm, sem)
            store.start()
            store.wait()

    row = lambda w: pl.BlockSpec((PROJ_TILE, w), lambda i: (i, 0))
    return pl.pallas_call(
        body, name="backward_projection", grid=(n_steps,),
        out_shape=[jax.ShapeDtypeStruct((n_tok, D_MODEL), F32), jax.ShapeDtypeStruct((D_MODEL, IN_WIDTH), F32),
                   jax.ShapeDtypeStruct((1, D_MODEL), F32)],
        in_specs=[row(D_MODEL), row(D_MODEL)] + [row(w) for w in PROJ_WIDTHS] + [_full_spec((1, D_MODEL)), ANY_SPEC],
        out_specs=[row(D_MODEL), ANY_SPEC, _full_spec((1, D_MODEL))],
        scratch_shapes=[pltpu.VMEM((D_MODEL, IN_WIDTH), BF16), pltpu.VMEM((D_MODEL, IN_WIDTH), F32),
                        pltpu.SemaphoreType.DMA],
        input_output_aliases={1: 0},
        compiler_params=pltpu.CompilerParams(vmem_limit_bytes=VMEM_LIMIT),
    )(x2, dout, *dparts, g_pre, w_in)


def _memkv_backward(mem, dmkv, g_mem, w_mkv):
    n_ex = mem.shape[0]

    def body(mem_ref, d_ref, g_ref, w_ref, dw_ref, dg_ref):
        @pl.when(pl.program_id(0) == 0)
        def _():
            dw_ref[...] = jnp.zeros_like(dw_ref)
            dg_ref[...] = jnp.zeros_like(dg_ref)

        m = mem_ref[0]
        mn = m * lax.rsqrt(jnp.mean(m * m, axis=-1, keepdims=True) + EPS)
        d_b = d_ref[0].astype(BF16)
        dw_ref[...] += _mm_tn((mn * g_ref[...]).astype(BF16), d_b)
        dg_ref[...] += jnp.sum(_mm_nt(d_b, w_ref[...]) * mn, axis=0, keepdims=True)

    return pl.pallas_call(
        body, name="memkv_backward", grid=(n_ex,),
        out_shape=[jax.ShapeDtypeStruct((D_MODEL, 2 * MEM_WIDTH), F32), jax.ShapeDtypeStruct((1, D_MODEL), F32)],
        in_specs=[pl.BlockSpec((1, MEM_LEN, D_MODEL), lambda b: (b, 0, 0)),
                  pl.BlockSpec((1, MEM_LEN, 2 * MEM_WIDTH), lambda b: (b, 0, 0)),
                  _full_spec((1, D_MODEL)), _full_spec((D_MODEL, 2 * MEM_WIDTH))],
        out_specs=[_full_spec((D_MODEL, 2 * MEM_WIDTH)), _full_spec((1, D_MODEL))],
    )(mem, dmkv, g_mem, w_mkv)


def _pack_small_grads(dgpre, dgpost, dgmem, dvg, dvb, dws, dbs, dsink, drel, buckets):
    def body(dgpre_ref, dgpost_ref, dgmem_ref, dvg_ref, dvb_ref, dws_ref, dbs_ref, dsink_ref, drel_ref, bk_ref,
             a_ref, b_ref):
        a_ref[...] = jnp.zeros_like(a_ref)
        b_ref[...] = jnp.zeros_like(b_ref)
        a_ref[0:1, :] = dgpre_ref[...]
        a_ref[1:2, :] = dgpost_ref[...]
        a_ref[2:3, :] = dgmem_ref[...]
        a_ref[3:4, :] = jnp.concatenate([dvg_ref[...], dvb_ref[...]], axis=-1)
        row = lax.broadcasted_iota(jnp.int32, (CHUNK, CHUNK), 0)
        col = lax.broadcasted_iota(jnp.int32, (CHUNK, CHUNK), 1)
        for g in range(A_GROUPS):
            b_ref[ROW_WS + g * CHUNK:ROW_WS + (g + 1) * CHUNK, :] = jnp.where(row >= col, dws_ref[g], 0.0)
            by_token = jnp.transpose(dbs_ref[:, g * 128:(g + 1) * 128])
            b_ref[ROW_BS + g:ROW_BS + g + 1, :] = jnp.sum(by_token, axis=0, keepdims=True)
        b_ref[ROW_SINK:ROW_SINK + 1, :] = dsink_ref[...]
        bk = bk_ref[...]
        rel_row = lax.broadcasted_iota(jnp.int32, (N_BUCKETS, 128), 0)
        rel_col = lax.broadcasted_iota(jnp.int32, (N_BUCKETS, 128), 1)
        rel = jnp.zeros((N_BUCKETS, 128), F32)
        for h in range(4):
            acc = drel_ref[h]
            for b in range(N_BUCKETS):
                rel = jnp.where((rel_row == b) & (rel_col == h), jnp.sum(jnp.where(bk == b, acc, 0.0)), rel)
        b_ref[ROW_REL:ROW_REL + N_BUCKETS, :] = rel

    return pl.pallas_call(
        body, name="pack_small_grads",
        out_shape=[jax.ShapeDtypeStruct((SMALL_A_ROWS, D_MODEL), F32), jax.ShapeDtypeStruct((SMALL_B_ROWS, 128), F32)],
        in_specs=[VMEM_SPEC] * 10, out_specs=[VMEM_SPEC] * 2,
    )(dgpre, dgpost, dgmem, dvg, dvb, dws, dbs, dsink, drel, buckets)


def _exchange_siblings_and_small(big, small_a, small_b):
    def body(g0, g1, g2, sa, sb, r0, r1, r2, ra, rb, send_sems, recv_sems, local_sems):
        x, y, c = lax.axis_index("x"), lax.axis_index("y"), lax.axis_index("c")
        me = 4 * x + 2 * y + c
        own = [pltpu.make_async_copy(sa, ra.at[me], local_sems.at[0]),
               pltpu.make_async_copy(sb, rb.at[me], local_sems.at[1])]
        for cp in own:
            cp.start()
        copies = []
        for w, (g, r) in enumerate(((g0, r0), (g1, r1), (g2, r2))):
            copies.append(pltpu.make_async_remote_copy(
                src_ref=g.at[:, pl.ds(1 - c, 1)], dst_ref=r, send_sem=send_sems.at[w], recv_sem=recv_sems.at[w],
                device_id=(x, y, 1 - c), device_id_type=MESH))
        k = 3
        for rel in range(1, N_DEV):
            peer = (x ^ (rel >> 2), y ^ ((rel >> 1) & 1), c ^ (rel & 1))
            for src, dst in ((sa, ra), (sb, rb)):
                copies.append(pltpu.make_async_remote_copy(
                    src_ref=src, dst_ref=dst.at[me], send_sem=send_sems.at[k], recv_sem=recv_sems.at[k],
                    device_id=peer, device_id_type=MESH))
                k += 1
        for cp in copies:
            cp.start()
        for cp in copies:
            cp.wait_recv()
        for cp in copies:
            cp.wait_send()
        for cp in own:
            cp.wait()

    n_sems = 3 + 2 * (N_DEV - 1)
    out_shape = [jax.ShapeDtypeStruct((g.shape[0], 1) + g.shape[2:], F32) for g in big]
    out_shape += [jax.ShapeDtypeStruct((N_DEV,) + small_a.shape, F32),
                  jax.ShapeDtypeStruct((N_DEV,) + small_b.shape, F32)]
    return pl.pallas_call(
        body, name="exchange_siblings_and_small", out_shape=out_shape,
        in_specs=[ANY_SPEC] * 5, out_specs=[ANY_SPEC] * 5,
        scratch_shapes=[pltpu.SemaphoreType.DMA((n_sems,)), pltpu.SemaphoreType.DMA((n_sems,)),
                        pltpu.SemaphoreType.DMA((2,))],
    )(*big, small_a, small_b)


def _chip_sum(big, recv, c_arr):
    def body(c_ref, g0, g1, g2, r0, r1, r2, p0, p1, p2):
        for g, r, p in ((g0, r0, p0), (g1, r1, p1), (g2, r2, p2)):
            p[...] = g[...] + r[...]

    def own(g):
        return pl.BlockSpec((1, 1) + g.shape[2:], lambda j, c_ref: (j, c_ref[0], 0, 0))

    def got(g):
        return pl.BlockSpec((1, 1) + g.shape[2:], lambda j, c_ref: (j, 0, 0, 0))

    return pl.pallas_call(
        body, name="chip_sum",
        out_shape=[jax.ShapeDtypeStruct(r.shape, F32) for r in recv],
        grid_spec=pltpu.PrefetchScalarGridSpec(
            num_scalar_prefetch=1, grid=(N_CHIPS,),
            in_specs=[own(g) for g in big] + [got(g) for g in big],
            out_specs=[got(g) for g in big]),
        compiler_params=pltpu.CompilerParams(vmem_limit_bytes=VMEM_LIMIT),
    )(c_arr, *big, *recv)


def _exchange_chips(partials):
    def body(p0, p1, p2, r0, r1, r2, send_sems, recv_sems):
        x, y, c = lax.axis_index("x"), lax.axis_index("y"), lax.axis_index("c")
        chips = [(1 - x, y), (x, 1 - y), (1 - x, 1 - y)]
        copies = []
        for w, (p, r) in enumerate(((p0, r0), (p1, r1), (p2, r2))):
            for j, chip in enumerate(chips):
                copies.append(pltpu.make_async_remote_copy(
                    src_ref=p.at[2 * chip[0] + chip[1]], dst_ref=r.at[j],
                    send_sem=send_sems.at[3 * w + j], recv_sem=recv_sems.at[3 * w + j],
                    device_id=(*chip, c), device_id_type=MESH))
        for cp in copies:
            cp.start()
        for cp in copies:
            cp.wait_recv()
        for cp in copies:
            cp.wait_send()

    return pl.pallas_call(
        body, name="exchange_chips",
        out_shape=[jax.ShapeDtypeStruct((3,) + p.shape[1:], F32) for p in partials],
        in_specs=[ANY_SPEC] * 3, out_specs=[ANY_SPEC] * 3,
        scratch_shapes=[pltpu.SemaphoreType.DMA((9,)), pltpu.SemaphoreType.DMA((9,))],
    )(*partials)


def _shard_sum(partials, recv, shard_arr):
    def body(s_ref, p0, p1, p2, r0, r1, r2, o0, o1, o2):
        for p, r, o in ((p0, r0, o0), (p1, r1, o1), (p2, r2, o2)):
            o[...] = ((p[0, 0] + r[0, 0]) + r[1, 0]) + r[2, 0]

    def own(p):
        return pl.BlockSpec((1,) + p.shape[1:], lambda i, s_ref: (s_ref[0], 0, 0, 0))

    def got(p):
        return pl.BlockSpec((3,) + p.shape[1:], lambda i, s_ref: (0, 0, 0, 0))

    return pl.pallas_call(
        body, name="shard_sum",
        out_shape=[jax.ShapeDtypeStruct(p.shape[2:], F32) for p in partials],
        grid_spec=pltpu.PrefetchScalarGridSpec(
            num_scalar_prefetch=1, grid=(1,),
            in_specs=[own(p) for p in partials] + [got(p) for p in partials],
            out_specs=[pl.BlockSpec(p.shape[2:], lambda i, s_ref: (0, 0)) for p in partials]),
        compiler_params=pltpu.CompilerParams(vmem_limit_bytes=VMEM_LIMIT),
    )(shard_arr, *partials, *recv)


def _exchange_halves(halves):
    def body(h0, h1, h2, r0, r1, r2, send_sems, recv_sems):
        x, y, c = lax.axis_index("x"), lax.axis_index("y"), lax.axis_index("c")
        copies = [pltpu.make_async_remote_copy(src_ref=h, dst_ref=r, send_sem=send_sems.at[w],
                                               recv_sem=recv_sems.at[w], device_id=(x, y, 1 - c),
                                               device_id_type=MESH)
                  for w, (h, r) in enumerate(((h0, r0), (h1, r1), (h2, r2)))]
        for cp in copies:
            cp.start()
        for cp in copies:
            cp.wait_recv()
        for cp in copies:
            cp.wait_send()

    return pl.pallas_call(
        body, name="exchange_halves",
        out_shape=[jax.ShapeDtypeStruct(h.shape, F32) for h in halves],
        in_specs=[ANY_SPEC] * 3, out_specs=[ANY_SPEC] * 3,
        scratch_shapes=[pltpu.SemaphoreType.DMA((3,)), pltpu.SemaphoreType.DMA((3,))],
    )(*halves)


def _adamw(w, g, m, v):
    m2 = ADAM_B1 * m + (1.0 - ADAM_B1) * g
    v2 = ADAM_B2 * v + (1.0 - ADAM_B2) * (g * g)
    m_hat = m2 / (1.0 - ADAM_B1 ** ADAM_STEP)
    v_hat = v2 / (1.0 - ADAM_B2 ** ADAM_STEP)
    delta = -ADAM_LR * (m_hat / (jnp.sqrt(v_hat) + ADAM_EPS) + ADAM_WD * w)
    return delta, m2, v2


ADAM_ROWS = 128


def _adamw_sharded(mine, other, w, m, v, c_arr, name):
    rows, cols = w.shape
    half = rows // 2
    steps = half // ADAM_ROWS

    def body(c_ref, mine_ref, other_ref, w_ref, m_ref, v_ref, g_out, d_out, m_out, v_out):
        g = jnp.where(pl.program_id(0) == c_ref[0], mine_ref[...], other_ref[...])
        delta, m2, v2 = _adamw(w_ref[...], g, m_ref[...], v_ref[...])
        g_out[...] = g
        d_out[...] = delta
        m_out[...] = m2
        v_out[...] = v2

    part = pl.BlockSpec((ADAM_ROWS, cols), lambda h, k, c_ref: (k, 0))
    full = pl.BlockSpec((ADAM_ROWS, cols), lambda h, k, c_ref: (h * steps + k, 0))
    return pl.pallas_call(
        body, name=name, out_shape=[jax.ShapeDtypeStruct((rows, cols), F32)] * 4,
        grid_spec=pltpu.PrefetchScalarGridSpec(
            num_scalar_prefetch=1, grid=(2, steps), in_specs=[part, part, full, full, full], out_specs=[full] * 4),
    )(c_arr, mine, other, w, m, v)


def _adamw_small(ra, rb, weights, moments_m, moments_v):
    n = len(weights)

    def body(*refs):
        ra_ref, rb_ref = refs[0], refs[1]
        w_refs, m_refs, v_refs = refs[2:2 + n], refs[2 + n:2 + 2 * n], refs[2 + 2 * n:2 + 3 * n]
        outs = refs[2 + 3 * n:]
        g_outs, d_outs, m_outs, v_outs = outs[:n], outs[n:2 * n], outs[2 * n:3 * n], outs[3 * n:]
        ga, gb = ra_ref[0], rb_ref[0]
        for dev in range(1, N_DEV):
            ga = ga + ra_ref[dev]
            gb = gb + rb_ref[dev]
        grads = [ga[0:1, :], ga[1:2, :], ga[2:3, :], ga[3:4, :A_WIDTH], ga[3:4, A_WIDTH:],
                 gb[ROW_WS:ROW_WS + A_GROUPS * CHUNK, :].reshape(A_GROUPS, CHUNK, CHUNK),
                 gb[ROW_BS:ROW_BS + A_GROUPS, :], gb[ROW_SINK:ROW_SINK + 1, 0:4],
                 gb[ROW_REL:ROW_REL + N_BUCKETS, 0:4]]
        for k in range(n):
            delta, m2, v2 = _adamw(w_refs[k][...], grads[k], m_refs[k][...], v_refs[k][...])
            g_outs[k][...] = grads[k]
            d_outs[k][...] = delta
            m_outs[k][...] = m2
            v_outs[k][...] = v2

    out_shape = [jax.ShapeDtypeStruct(w.shape, F32) for w in weights] * 4
    return pl.pallas_call(
        body, name="adamw_small", out_shape=out_shape,
        in_specs=[VMEM_SPEC] * (2 + 3 * n), out_specs=[VMEM_SPEC] * (4 * n),
    )(ra, rb, *weights, *moments_m, *moments_v)


def kernel(x, mem, pre_norm_g, post_norm_g, mem_norm_g, w_in, w_mem_kv, v_norm_g, v_norm_b, w_spatial, b_spatial, attn_sinks, rel_bias, w_out, loss_target, m_pre_norm_g, m_post_norm_g, m_mem_norm_g, m_w_in, m_w_mem_kv, m_v_norm_g, m_v_norm_b, m_w_spatial, m_b_spatial, m_attn_sinks, m_rel_bias, m_w_out, v_pre_norm_g, v_post_norm_g, v_mem_norm_g, v_w_in, v_w_mem_kv, v_v_norm_g, v_v_norm_b, v_w_spatial, v_b_spatial, v_attn_sinks, v_rel_bias, v_w_out):
    n_ex, seq, _ = x.shape
    n_tok = n_ex * seq
    x2 = x.reshape(n_tok, D_MODEL)
    tgt2 = loss_target.reshape(n_tok, D_MODEL)
    buckets = jnp.asarray(_bucket_map())
    c_arr = lax.axis_index("c").astype(jnp.int32).reshape(1)
    shard_arr = (2 * lax.axis_index("x") + lax.axis_index("y")).astype(jnp.int32).reshape(1)
    w_sp = w_spatial[0]
    b_sp = b_spatial[0][:, :, None]

    g_in, g_mkv, g_out = _gather_weights(w_in[0], w_mem_kv[0], w_out[0])
    w_in_b = jnp.transpose(g_in, (1, 0, 2)).reshape(D_MODEL, IN_WIDTH)
    w_mkv_b = g_mkv.reshape(D_MODEL, 2 * MEM_WIDTH)
    w_out_b = g_out.reshape(MIX_WIDTH, D_MODEL)

    bias = _make_bias(rel_bias, buckets)
    mkv = _memkv_forward(mem, mem_norm_g, w_mkv_b)
    parts = _forward_projection(x2, pre_norm_g, w_in_b)
    dout, do, loss_vec, dgpost = _forward_mix(parts, mkv, x2, tgt2, v_norm_g, v_norm_b, w_sp, b_sp, attn_sinks, bias,
                                             w_out_b, post_norm_g, n_ex, seq)

    (dau, dav, dsq, dsk, dsv, dmq, dz, dmkv, dwout, dvg, dvb, dws, dbs, dsink, drel) = _backward_mix(
        parts, mkv, do, v_norm_g, v_norm_b, w_sp, b_sp, attn_sinks, bias, w_out_b, n_ex, seq)
    dx, dwin, dgpre = _backward_projection(x2, dout, (dau, dav, dsq, dsk, dsv, dmq, dz), pre_norm_g, w_in_b)
    dwmkv, dgmem = _memkv_backward(mem, dmkv, mem_norm_g, w_mkv_b)
    small_a, small_b = _pack_small_grads(dgpre, dgpost, dgmem, dvg, dvb, dws, dbs, dsink, drel, buckets)

    shard_shapes = [w_in.shape[1:], w_mem_kv.shape[1:], w_out.shape[1:]]
    big = [jnp.transpose(dwin.reshape(D_MODEL, N_CHIPS, IN_WIDTH // N_CHIPS), (1, 0, 2)), dwmkv, dwout]
    big = [g.reshape(N_CHIPS, 2, s[0] // 2, s[1]) for g, s in zip(big, shard_shapes)]
    *recv, ra, rb = _exchange_siblings_and_small(big, small_a, small_b)
    partials = _chip_sum(big, recv, c_arr)
    recv2 = _exchange_chips(partials)
    mine = _shard_sum(partials, recv2, shard_arr)
    other = _exchange_halves(mine)

    big_w = [(w_in[0], m_w_in[0], v_w_in[0]), (w_mem_kv[0], m_w_mem_kv[0], v_w_mem_kv[0]),
             (w_out[0], m_w_out[0], v_w_out[0])]
    big_names = ["adamw_w_in", "adamw_w_mem_kv", "adamw_w_out"]
    big_out = [_adamw_sharded(mine[k], other[k], *big_w[k], c_arr, big_names[k]) for k in range(3)]
    small_w = [pre_norm_g, post_norm_g, mem_norm_g, v_norm_g, v_norm_b, w_sp, b_spatial[0], attn_sinks, rel_bias]
    small_m = [m_pre_norm_g, m_post_norm_g, m_mem_norm_g, m_v_norm_g, m_v_norm_b, m_w_spatial[0], m_b_spatial[0],
               m_attn_sinks, m_rel_bias]
    small_v = [v_pre_norm_g, v_post_norm_g, v_mem_norm_g, v_v_norm_g, v_v_norm_b, v_w_spatial[0], v_b_spatial[0],
               v_attn_sinks, v_rel_bias]
    small_out = _adamw_small(ra, rb, small_w, small_m, small_v)
    n_small = len(small_w)

    loss = lax.psum(loss_vec[0, 0], ALL_AXES)
    outputs = [loss, dx.reshape(x.shape)]
    for kind in range(4):
        s = small_out[kind * n_small:(kind + 1) * n_small]
        outputs += [s[0], s[1], s[2], big_out[0][kind][None], big_out[1][kind][None], s[3], s[4], s[5][None],
                    s[6][None], s[7], s[8], big_out[2][kind][None]]
    return tuple(outputs)
```

```python
import functools

import numpy as np
import jax
import jax.numpy as jnp
from jax import lax
from jax.experimental import pallas as pl
from jax.experimental.pallas import tpu as pltpu

F32 = jnp.float32
BF16 = jnp.bfloat16
MESH = pl.DeviceIdType.MESH
ALL_AXES = ("x", "y", "c")

D_MODEL = 1024
CHUNK = 128
A_WIDTH = 512
A_GROUPS = 4
SWA_WIDTH = 256
KV_WIDTH = 128
MEM_WIDTH = 256
MEM_LEN = 256
MIX_WIDTH = 1024
IN_WIDTH = 2816
N_BUCKETS = 32
MAX_DISTANCE = 128
EPS = 1e-6
NEG = -1e30
QK_SCALE = 0.125
HALF_HEAD_PAIR = 64

ADAM_LR = 0.001
ADAM_B1 = 0.9
ADAM_B2 = 0.999
ADAM_EPS = 1e-08
ADAM_WD = 0.01
ADAM_STEP = 10

N_CHIPS = 4
N_DEV = 8
TILE_CHUNKS = 2
TILE = TILE_CHUNKS * CHUNK
PROJ_TILE = 256
VMEM_LIMIT = 56 * 1024 * 1024

SMALL_A_ROWS = 8
ROW_WS = 0
ROW_BS = 512
ROW_SINK = 520
ROW_REL = 528
SMALL_B_ROWS = 536


def _mm(a, b):
    return lax.dot_general(a, b, (((1,), (0,)), ((), ())), preferred_element_type=F32)


def _mm_nt(a, b):
    return lax.dot_general(a, b, (((1,), (1,)), ((), ())), preferred_element_type=F32)


def _mm_tn(a, b):
    return lax.dot_general(a, b, (((0,), (0,)), ((), ())), preferred_element_type=F32)


def _bucket_map():
    qi = np.arange(CHUNK)[:, None]
    kj = np.arange(2 * CHUNK)[None, :]
    n = np.maximum(qi + CHUNK - kj, 0)
    max_exact = N_BUCKETS // 2
    large = max_exact + (np.log(np.maximum(n, 1) / max_exact) / np.log(MAX_DISTANCE / max_exact)
                         * (N_BUCKETS - max_exact)).astype(np.int32)
    large = np.minimum(large, N_BUCKETS - 1)
    return np.where(n < max_exact, n, large).astype(np.int32)


_GELU_C = 0.7978845608028654
_GELU_A = 0.044715


def _gelu(x):
    t = jnp.tanh(_GELU_C * (x + _GELU_A * x * x * x))
    return 0.5 * x * (1.0 + t), t


def _gelu_grad(x, t):
    return 0.5 * (1.0 + t) + 0.5 * x * (1.0 - t * t) * (_GELU_C * (1.0 + 3.0 * _GELU_A * x * x))


def _sigmoid(x):
    return 1.0 / (1.0 + jnp.exp(-x))


def _lane_lo(shape):
    return lax.broadcasted_iota(jnp.int32, shape, 1) < HALF_HEAD_PAIR


def _swa_variants(t):
    lo = _lane_lo(t.shape)
    tr = pltpu.roll(t, HALF_HEAD_PAIR, 1)
    zero = jnp.zeros_like(t)
    return (jnp.where(lo, t, zero).astype(BF16), jnp.where(lo, zero, tr).astype(BF16),
            jnp.where(lo, tr, zero).astype(BF16), jnp.where(lo, zero, t).astype(BF16))


def _swa_unvariants(d0, d1, d2, d3):
    lo = _lane_lo(d0.shape)
    zero = jnp.zeros_like(d0)
    rolled = jnp.where(lo, zero, d1) + jnp.where(lo, d2, zero)
    return jnp.where(lo, d0, zero) + jnp.where(lo, zero, d3) + pltpu.roll(rolled, HALF_HEAD_PAIR, 1)


def _mem_variants(t):
    out = []
    for pair in range(2):
        tp = t[:, pair * 128:(pair + 1) * 128]
        lo = _lane_lo(tp.shape)
        zero = jnp.zeros_like(tp)
        out.append(jnp.where(lo, tp, zero).astype(BF16))
        out.append(jnp.where(lo, zero, tp).astype(BF16))
    return out


def _mem_unvariants(d0, d1, d2, d3):
    lo = _lane_lo(d0.shape)
    return jnp.concatenate([jnp.where(lo, d0, d1), jnp.where(lo, d2, d3)], axis=-1)


def _softmax(logits, sink):
    m = jnp.max(logits, axis=-1, keepdims=True)
    if sink is not None:
        m = jnp.maximum(m, sink)
    p = jnp.exp(logits - m)
    den = jnp.sum(p, axis=-1, keepdims=True)
    if sink is None:
        return p * (1.0 / den), None
    es = jnp.exp(sink - m)
    inv = 1.0 / (den + es)
    return p * inv, es * inv


def _band_mask(prev_valid):
    qi = lax.broadcasted_iota(jnp.int32, (CHUNK, 2 * CHUNK), 0)
    kj = lax.broadcasted_iota(jnp.int32, (CHUNK, 2 * CHUNK), 1)
    in_prev = (kj < CHUNK) & (kj > qi)
    in_cur = (kj >= CHUNK) & (kj - CHUNK <= qi)
    if prev_valid is True:
        return in_prev | in_cur
    return (in_prev & prev_valid) | in_cur


def _causal_weights(ws_ref):
    row = lax.broadcasted_iota(jnp.int32, (CHUNK, CHUNK), 0)
    col = lax.broadcasted_iota(jnp.int32, (CHUNK, CHUNK), 1)
    return [jnp.where(row >= col, ws_ref[g], 0.0).astype(BF16) for g in range(A_GROUPS)]


def _chunk_forward(au, av, q, kband, vband, mq, mk_v, mv_v, vg, vb, wm, bs, sinks, bias_ref, mask):
    gu, tu = _gelu(au)
    gv, tv = _gelu(av)
    ya, a_res = [], []
    for g in range(A_GROUPS):
        sl = slice(g * 128, (g + 1) * 128)
        xg = gv[:, sl]
        xc = xg - jnp.mean(xg, axis=-1, keepdims=True)
        rstd = lax.rsqrt(jnp.mean(xc * xc, axis=-1, keepdims=True) + EPS)
        xhat = xc * rstd
        vn = (xhat * vg[:, sl] + vb[:, sl]).astype(BF16)
        s = _mm(wm[g], vn) + bs[g]
        ya.append(gu[:, sl] * s)
        a_res.append((xhat, rstd, vn, s))

    qp = (q[:, :128].astype(BF16), q[:, 128:].astype(BF16))
    k_v = _swa_variants(kband)
    v_v = _swa_variants(vband)
    b_p, b_sink, b_out = [], [], []
    for h in range(4):
        logits = _mm_nt(qp[h // 2], k_v[h]) * QK_SCALE + bias_ref[h]
        p, ps = _softmax(jnp.where(mask, logits, NEG), sinks[h])
        b_p.append(p)
        b_sink.append(ps)
        b_out.append(_mm(p.astype(BF16), v_v[h]))
    yb = jnp.concatenate([b_out[0] + b_out[1], b_out[2] + b_out[3]], axis=-1)

    mqp = (mq[:, :128].astype(BF16), mq[:, 128:].astype(BF16))
    c_p, c_out = [], []
    for h in range(4):
        p, _ = _softmax(_mm_nt(mqp[h // 2], mk_v[h]) * QK_SCALE, None)
        c_p.append(p)
        c_out.append(_mm(p.astype(BF16), mv_v[h]))
    yc = jnp.concatenate([c_out[0] + c_out[1], c_out[2] + c_out[3]], axis=-1)

    ycat = jnp.concatenate(ya + [yb, yc], axis=-1)
    return ycat, dict(gu=gu, tu=tu, tv=tv, a_res=a_res, qp=qp, k_v=k_v, v_v=v_v, b_p=b_p, b_sink=b_sink,
                      mqp=mqp, c_p=c_p)


def _tile_specs(n_tiles_ex, width):
    return pl.BlockSpec((TILE, width), lambda b, i: (b * n_tiles_ex + jnp.minimum(i, n_tiles_ex - 1), 0))


def _prev_chunk_spec(n_tiles_ex, width):
    def index(b, i):
        chunk = TILE_CHUNKS * jnp.minimum(i, n_tiles_ex - 1)
        return (b * n_tiles_ex * TILE_CHUNKS + jnp.maximum(chunk - 1, 0), 0)
    return pl.BlockSpec((CHUNK, width), index)


def _full_spec(shape):
    zeros = (0,) * len(shape)
    return pl.BlockSpec(shape, lambda *_: zeros)


SMEM_SPEC = pl.BlockSpec(memory_space=pltpu.SMEM)
ANY_SPEC = pl.BlockSpec(memory_space=pl.ANY)
VMEM_SPEC = pl.BlockSpec(memory_space=pltpu.VMEM)


def _make_bias(rel_bias_t, buckets):
    def body(rel_ref, bk_ref, out_ref):
        bk = bk_ref[...]
        for h in range(4):
            acc = jnp.zeros((CHUNK, 2 * CHUNK), F32)
            for b in range(N_BUCKETS):
                acc = jnp.where(bk == b, rel_ref[h, b], acc)
            out_ref[h] = acc

    return pl.pallas_call(
        body, name="make_bias", out_shape=jax.ShapeDtypeStruct((4, CHUNK, 2 * CHUNK), F32),
        in_specs=[SMEM_SPEC, VMEM_SPEC], out_specs=VMEM_SPEC,
    )(rel_bias_t, buckets)


def _gather_weights(w_in_s, w_mkv_s, w_out_s):
    shapes = [w_in_s.shape, w_mkv_s.shape, w_out_s.shape]
    n_w = len(shapes)

    def body(win_ref, wmkv_ref, wout_ref, gin_ref, gmkv_ref, gout_ref, send_sems, recv_sems):
        x, y, c = lax.axis_index("x"), lax.axis_index("y"), lax.axis_index("c")
        me, sibling = (x, y, c), (x, y, 1 - c)
        chips = [(1 - x, y), (x, 1 - y), (1 - x, 1 - y)]
        ins = [win_ref, wmkv_ref, wout_ref]
        outs = [gin_ref, gmkv_ref, gout_ref]
        my_shard = 2 * x + y
        for w in range(n_w):
            outs[w][my_shard] = ins[w][...].astype(BF16)

        def copy(k, w, shard, half, to):
            rows = shapes[w][0] // 2
            ref = outs[w].at[shard, pl.ds(half * rows, rows), :]
            return pltpu.make_async_remote_copy(src_ref=ref, dst_ref=ref, send_sem=send_sems.at[k],
                                                recv_sem=recv_sems.at[k], device_id=to, device_id_type=MESH)

        pairs = [(w, j) for w in range(n_w) for j in range(3)]
        first = [copy(3 * w + j, w, my_shard, c, (*chips[j], c)) for w, j in pairs]
        for cp in first:
            cp.start()
        passed = []
        for w, j in pairs:
            shard = 2 * chips[j][0] + chips[j][1]
            copy(3 * w + j, w, shard, c, me).wait_recv()
            fwd = copy(9 + 3 * w + j, w, shard, c, sibling)
            fwd.start()
            passed.append(fwd)
        for w, j in pairs:
            shard = 2 * chips[j][0] + chips[j][1]
            copy(9 + 3 * w + j, w, shard, 1 - c, me).wait_recv()
        for cp in first + passed:
            cp.wait_send()

    return pl.pallas_call(
        body, name="gather_weights",
        out_shape=[jax.ShapeDtypeStruct((N_CHIPS,) + s, BF16) for s in shapes],
        in_specs=[VMEM_SPEC] * 3, out_specs=[VMEM_SPEC] * 3,
        scratch_shapes=[pltpu.SemaphoreType.DMA((18,)), pltpu.SemaphoreType.DMA((18,))],
        compiler_params=pltpu.CompilerParams(vmem_limit_bytes=VMEM_LIMIT),
    )(w_in_s, w_mkv_s, w_out_s)


def _memkv_forward(mem, g_mem, w_mkv):
    n_ex = mem.shape[0]

    def body(mem_ref, g_ref, w_ref, out_ref):
        m = mem_ref[0]
        r = lax.rsqrt(jnp.mean(m * m, axis=-1, keepdims=True) + EPS)
        out_ref[0] = _mm((m * r * g_ref[...]).astype(BF16), w_ref[...])

    return pl.pallas_call(
        body, name="memkv_forward", grid=(n_ex,),
        out_shape=jax.ShapeDtypeStruct((n_ex, MEM_LEN, 2 * MEM_WIDTH), F32),
        in_specs=[pl.BlockSpec((1, MEM_LEN, D_MODEL), lambda b: (b, 0, 0)), _full_spec((1, D_MODEL)),
                  _full_spec((D_MODEL, 2 * MEM_WIDTH))],
        out_specs=pl.BlockSpec((1, MEM_LEN, 2 * MEM_WIDTH), lambda b: (b, 0, 0)),
    )(mem, g_mem, w_mkv)


PROJ_WIDTHS = (A_WIDTH, A_WIDTH, SWA_WIDTH, KV_WIDTH, KV_WIDTH, MEM_WIDTH, MIX_WIDTH)
PROJ_OFFSETS = tuple(int(v) for v in np.cumsum((0,) + PROJ_WIDTHS))


def _forward_projection(x2, g_pre, w_in_t):
    n_tok = x2.shape[0]

    def body(x_ref, g_ref, w_ref, *out_refs):
        xv = x_ref[...]
        r = lax.rsqrt(jnp.mean(xv * xv, axis=-1, keepdims=True) + EPS)
        proj = _mm_nt((xv * r * g_ref[...]).astype(BF16), w_ref[...])
        for k, ref in enumerate(out_refs):
            ref[...] = proj[:, PROJ_OFFSETS[k]:PROJ_OFFSETS[k + 1]]

    return pl.pallas_call(
        body, name="forward_projection", grid=(n_tok // PROJ_TILE,),
        out_shape=[jax.ShapeDtypeStruct((n_tok, w), F32) for w in PROJ_WIDTHS],
        in_specs=[pl.BlockSpec((PROJ_TILE, D_MODEL), lambda i: (i, 0)), _full_spec((1, D_MODEL)),
                  _full_spec((IN_WIDTH, D_MODEL))],
        out_specs=[pl.BlockSpec((PROJ_TILE, w), lambda i: (i, 0)) for w in PROJ_WIDTHS],
        compiler_params=pltpu.CompilerParams(vmem_limit_bytes=VMEM_LIMIT),
    )(x2, g_pre, w_in_t)


def _load_chunk(j, i, sk_ref, sv_ref, skp_ref, svp_ref):
    rows = slice(j * CHUNK, (j + 1) * CHUNK)
    if j == 0:
        k_prev, v_prev, prev_valid = skp_ref[...], svp_ref[...], i > 0
    else:
        prev = slice((j - 1) * CHUNK, j * CHUNK)
        k_prev, v_prev, prev_valid = sk_ref[prev, :], sv_ref[prev, :], True
    kband = jnp.concatenate([k_prev, sk_ref[rows, :]], axis=0)
    vband = jnp.concatenate([v_prev, sv_ref[rows, :]], axis=0)
    return rows, kband, vband, _band_mask(prev_valid)


def _forward_mix(parts, mkv, x2, tgt2, v_g, v_b, w_sp, b_sp, sinks, bias, w_out, g_post, n_ex, seq):
    n_tiles_ex = seq // TILE
    n_tok = n_ex * seq
    au, av, sq, sk, sv, mq, z = parts

    def body(au_ref, av_ref, sq_ref, sk_ref, sv_ref, skp_ref, svp_ref, mq_ref, z_ref, mkv_ref, x_ref, tgt_ref,
             vg_ref, vb_ref, ws_ref, bs_ref, sink_ref, bias_ref, wout_ref, gpost_ref,
             dout_ref, do_ref, loss_ref, dgpost_ref):
        b, i = pl.program_id(0), pl.program_id(1)

        @pl.when((b == 0) & (i == 0))
        def _():
            loss_ref[...] = jnp.zeros_like(loss_ref)
            dgpost_ref[...] = jnp.zeros_like(dgpost_ref)

        wm = _causal_weights(ws_ref)
        bs = [bs_ref[g] for g in range(A_GROUPS)]
        sinks_s = [sink_ref[0, h] for h in range(4)]
        vg, vb = vg_ref[...], vb_ref[...]
        mkv_v = mkv_ref[0]
        mk_v = _mem_variants(mkv_v[:, :MEM_WIDTH])
        mv_v = _mem_variants(mkv_v[:, MEM_WIDTH:])
        ycat = []
        for j in range(TILE_CHUNKS):
            rows, kband, vband, mask = _load_chunk(j, i, sk_ref, sv_ref, skp_ref, svp_ref)
            yc, _ = _chunk_forward(au_ref[rows, :], av_ref[rows, :], sq_ref[rows, :], kband, vband, mq_ref[rows, :],
                                   mk_v, mv_v, vg, vb, wm, bs, sinks_s, bias_ref, mask)
            ycat.append(yc)
        ycat = jnp.concatenate(ycat, axis=0)
        zv = z_ref[...]
        y = ycat * (zv * _sigmoid(zv))
        o = _mm(y.astype(BF16), wout_ref[...])
        r2 = lax.rsqrt(jnp.mean(o * o, axis=-1, keepdims=True) + EPS)
        nrm = o * r2
        gp = gpost_ref[...]
        diff = x_ref[...] + nrm * gp - tgt_ref[...]
        loss_ref[...] += jnp.sum(diff * diff) * (0.5 / D_MODEL)
        dout = diff * (1.0 / D_MODEL)
        dout_ref[...] = dout
        dgpost_ref[...] += jnp.sum(dout * nrm, axis=0, keepdims=True)
        dn = dout * gp
        do_ref[...] = r2 * (dn - nrm * jnp.mean(dn * nrm, axis=-1, keepdims=True))

    tile = functools.partial(_tile_specs, n_tiles_ex)
    prev = functools.partial(_prev_chunk_spec, n_tiles_ex)
    return pl.pallas_call(
        body, name="forward_mix", grid=(n_ex, n_tiles_ex),
        out_shape=[jax.ShapeDtypeStruct((n_tok, D_MODEL), F32), jax.ShapeDtypeStruct((n_tok, D_MODEL), F32),
                   jax.ShapeDtypeStruct((1, 128), F32), jax.ShapeDtypeStruct((1, D_MODEL), F32)],
        in_specs=[tile(A_WIDTH), tile(A_WIDTH), tile(SWA_WIDTH), tile(KV_WIDTH), tile(KV_WIDTH),
                  prev(KV_WIDTH), prev(KV_WIDTH), tile(MEM_WIDTH), tile(MIX_WIDTH),
                  pl.BlockSpec((1, MEM_LEN, 2 * MEM_WIDTH), lambda b, i: (b, 0, 0)),
                  tile(D_MODEL), tile(D_MODEL),
                  _full_spec((1, A_WIDTH)), _full_spec((1, A_WIDTH)), _full_spec((A_GROUPS, CHUNK, CHUNK)),
                  _full_spec((A_GROUPS, CHUNK, 1)), SMEM_SPEC, _full_spec((4, CHUNK, 2 * CHUNK)),
                  _full_spec((MIX_WIDTH, D_MODEL)), _full_spec((1, D_MODEL))],
        out_specs=[tile(D_MODEL), tile(D_MODEL), _full_spec((1, 128)), _full_spec((1, D_MODEL))],
        compiler_params=pltpu.CompilerParams(vmem_limit_bytes=VMEM_LIMIT),
    )(au, av, sq, sk, sv, sk, sv, mq, z, mkv, x2, tgt2, v_g, v_b, w_sp, b_sp, sinks, bias, w_out, g_post)


def _backward_mix(parts, mkv, do, v_g, v_b, w_sp, b_sp, sinks, bias, w_out, n_ex, seq):
    n_tiles_ex = seq // TILE
    n_tok = n_ex * seq
    au, av, sq, sk, sv, mq, z = parts

    def body(do_ref, au_ref, av_ref, sq_ref, sk_ref, sv_ref, skp_ref, svp_ref, mq_ref, z_ref, mkv_ref,
             vg_ref, vb_ref, ws_ref, bs_ref, sink_ref, bias_ref, wout_ref,
             dau_ref, dav_ref, dsq_ref, dsk_ref, dsv_ref, dmq_ref, dz_ref, dmkv_ref,
             dwout_ref, dvg_ref, dvb_ref, dws_ref, dbs_ref, dsink_ref, drel_ref,
             carry_k, carry_v):
        b, i = pl.program_id(0), pl.program_id(1)

        @pl.when((b == 0) & (i == 0))
        def _():
            for ref in (dwout_ref, dvg_ref, dvb_ref, dws_ref, dbs_ref, dsink_ref, drel_ref):
                ref[...] = jnp.zeros_like(ref)

        @pl.when(i == 0)
        def _():
            dmkv_ref[...] = jnp.zeros_like(dmkv_ref)
            carry_k[...] = jnp.zeros_like(carry_k)
            carry_v[...] = jnp.zeros_like(carry_v)

        @pl.when(i < n_tiles_ex)
        def _():
            wm = _causal_weights(ws_ref)
            bs = [bs_ref[g] for g in range(A_GROUPS)]
            sinks_s = [sink_ref[0, h] for h in range(4)]
            vg, vb = vg_ref[...], vb_ref[...]
            mkv_v = mkv_ref[0]
            mk_v = _mem_variants(mkv_v[:, :MEM_WIDTH])
            mv_v = _mem_variants(mkv_v[:, MEM_WIDTH:])
            do_v = do_ref[...]
            do_b = do_v.astype(BF16)
            dy = _mm_nt(do_b, wout_ref[...])
            zv = z_ref[...]
            sig = _sigmoid(zv)
            sz = zv * sig
            dycat_all = dy * sz
            lane4 = lax.broadcasted_iota(jnp.int32, (1, 128), 1)

            ycat, dk_parts, dv_parts = [], [], []
            dmk_acc = [jnp.zeros((MEM_LEN, 128), F32) for _ in range(4)]
            dmv_acc = [jnp.zeros((MEM_LEN, 128), F32) for _ in range(4)]
            dsink_vec = jnp.zeros((1, 128), F32)
            for j in range(TILE_CHUNKS):
                rows, kband, vband, mask = _load_chunk(j, i, sk_ref, sv_ref, skp_ref, svp_ref)
                au_v, av_v = au_ref[rows, :], av_ref[rows, :]
                yc, res = _chunk_forward(au_v, av_v, sq_ref[rows, :], kband, vband, mq_ref[rows, :],
                                         mk_v, mv_v, vg, vb, wm, bs, sinks_s, bias_ref, mask)
                ycat.append(yc)
                dyc = dycat_all[rows, :]

                dgu, dgv = [], []
                for g in range(A_GROUPS):
                    sl = slice(g * 128, (g + 1) * 128)
                    xhat, rstd, vn, s = res["a_res"][g]
                    dya = dyc[:, sl]
                    dgu.append(dya * s)
                    ds = dya * res["gu"][:, sl]
                    dbs_ref[:, sl] += ds
                    ds_b = ds.astype(BF16)
                    dws_ref[g] += _mm_nt(ds_b, vn)
                    dvn = _mm_tn(wm[g], ds_b)
                    dvg_ref[:, sl] += jnp.sum(dvn * xhat, axis=0, keepdims=True)
                    dvb_ref[:, sl] += jnp.sum(dvn, axis=0, keepdims=True)
                    dxh = dvn * vg[:, sl]
                    dgv.append(rstd * (dxh - jnp.mean(dxh, axis=-1, keepdims=True)
                                       - xhat * jnp.mean(dxh * xhat, axis=-1, keepdims=True)))
                dau_ref[rows, :] = jnp.concatenate(dgu, axis=-1) * _gelu_grad(au_v, res["tu"])
                dav_ref[rows, :] = jnp.concatenate(dgv, axis=-1) * _gelu_grad(av_v, res["tv"])

                dob = (dyc[:, A_WIDTH:A_WIDTH + 128].astype(BF16), dyc[:, A_WIDTH + 128:A_WIDTH + 256].astype(BF16))
                dq = [jnp.zeros((CHUNK, 128), F32), jnp.zeros((CHUNK, 128), F32)]
                dk_h, dv_h = [], []
                for h in range(4):
                    p, ps = res["b_p"][h], res["b_sink"][h]
                    dp = _mm_nt(dob[h // 2], res["v_v"][h])
                    delta = jnp.sum(p * dp, axis=-1, keepdims=True)
                    dl = p * (dp - delta)
                    dsink_vec = dsink_vec + jnp.where(lane4 == h, -jnp.sum(ps * delta), 0.0)
                    drel_ref[h] += dl
                    dl_b = dl.astype(BF16)
                    dq[h // 2] = dq[h // 2] + _mm(dl_b, res["k_v"][h])
                    dk_h.append(_mm_tn(dl_b, res["qp"][h // 2]))
                    dv_h.append(_mm_tn(p.astype(BF16), dob[h // 2]))
                dsq_ref[rows, :] = jnp.concatenate(dq, axis=-1) * QK_SCALE
                dk_parts.append(_swa_unvariants(*dk_h) * QK_SCALE)
                dv_parts.append(_swa_unvariants(*dv_h))

                dcb = (dyc[:, 768:896].astype(BF16), dyc[:, 896:1024].astype(BF16))
                dmq = [jnp.zeros((CHUNK, 128), F32), jnp.zeros((CHUNK, 128), F32)]
                for h in range(4):
                    p = res["c_p"][h]
                    dp = _mm_nt(dcb[h // 2], mv_v[h])
                    dl = p * (dp - jnp.sum(p * dp, axis=-1, keepdims=True))
                    dl_b = dl.astype(BF16)
                    dmq[h // 2] = dmq[h // 2] + _mm(dl_b, mk_v[h])
                    dmk_acc[h] = dmk_acc[h] + _mm_tn(dl_b, res["mqp"][h // 2])
                    dmv_acc[h] = dmv_acc[h] + _mm_tn(p.astype(BF16), dcb[h // 2])
                dmq_ref[rows, :] = jnp.concatenate(dmq, axis=-1) * QK_SCALE

            ycat = jnp.concatenate(ycat, axis=0)
            dwout_ref[...] += _mm_tn((ycat * sz).astype(BF16), do_b)
            dz_ref[...] = dy * ycat * (sig * (1.0 + zv * (1.0 - sig)))
            dsink_ref[...] += dsink_vec
            dmkv_ref[0] += jnp.concatenate([_mem_unvariants(*dmk_acc) * QK_SCALE, _mem_unvariants(*dmv_acc)], axis=-1)

            for parts_c, carry, out_ref in ((dk_parts, carry_k, dsk_ref), (dv_parts, carry_v, dsv_ref)):
                @pl.when(i > 0)
                def _():
                    out_ref[...] = carry[...] + jnp.concatenate(
                        [jnp.zeros((TILE - CHUNK, KV_WIDTH), F32), parts_c[0][:CHUNK]], axis=0)
                new = [parts_c[0][CHUNK:]]
                for j in range(1, TILE_CHUNKS):
                    new[-1] = new[-1] + parts_c[j][:CHUNK]
                    new.append(parts_c[j][CHUNK:])
                carry[...] = jnp.concatenate(new, axis=0)

        @pl.when(i == n_tiles_ex)
        def _():
            dsk_ref[...] = carry_k[...]
            dsv_ref[...] = carry_v[...]

    tile = functools.partial(_tile_specs, n_tiles_ex)
    prev = functools.partial(_prev_chunk_spec, n_tiles_ex)
    late = pl.BlockSpec((TILE, KV_WIDTH), lambda b, i: (b * n_tiles_ex + jnp.maximum(i - 1, 0), 0))
    tok = lambda w: jax.ShapeDtypeStruct((n_tok, w), F32)
    return pl.pallas_call(
        body, name="backward_mix", grid=(n_ex, n_tiles_ex + 1),
        out_shape=[tok(A_WIDTH), tok(A_WIDTH), tok(SWA_WIDTH), tok(KV_WIDTH), tok(KV_WIDTH), tok(MEM_WIDTH),
                   tok(MIX_WIDTH), jax.ShapeDtypeStruct((n_ex, MEM_LEN, 2 * MEM_WIDTH), F32),
                   jax.ShapeDtypeStruct((MIX_WIDTH, D_MODEL), F32), jax.ShapeDtypeStruct((1, A_WIDTH), F32),
                   jax.ShapeDtypeStruct((1, A_WIDTH), F32), jax.ShapeDtypeStruct((A_GROUPS, CHUNK, CHUNK), F32),
                   jax.ShapeDtypeStruct((CHUNK, A_WIDTH), F32), jax.ShapeDtypeStruct((1, 128), F32),
                   jax.ShapeDtypeStruct((4, CHUNK, 2 * CHUNK), F32)],
        in_specs=[tile(D_MODEL), tile(A_WIDTH), tile(A_WIDTH), tile(SWA_WIDTH), tile(KV_WIDTH), tile(KV_WIDTH),
                  prev(KV_WIDTH), prev(KV_WIDTH), tile(MEM_WIDTH), tile(MIX_WIDTH),
                  pl.BlockSpec((1, MEM_LEN, 2 * MEM_WIDTH), lambda b, i: (b, 0, 0)),
                  _full_spec((1, A_WIDTH)), _full_spec((1, A_WIDTH)), _full_spec((A_GROUPS, CHUNK, CHUNK)),
                  _full_spec((A_GROUPS, CHUNK, 1)), SMEM_SPEC, _full_spec((4, CHUNK, 2 * CHUNK)),
                  _full_spec((MIX_WIDTH, D_MODEL))],
        out_specs=[tile(A_WIDTH), tile(A_WIDTH), tile(SWA_WIDTH), late, late, tile(MEM_WIDTH), tile(MIX_WIDTH),
                   pl.BlockSpec((1, MEM_LEN, 2 * MEM_WIDTH), lambda b, i: (b, 0, 0)),
                   _full_spec((MIX_WIDTH, D_MODEL)), _full_spec((1, A_WIDTH)), _full_spec((1, A_WIDTH)),
                   _full_spec((A_GROUPS, CHUNK, CHUNK)), _full_spec((CHUNK, A_WIDTH)), _full_spec((1, 128)),
                   _full_spec((4, CHUNK, 2 * CHUNK))],
        scratch_shapes=[pltpu.VMEM((TILE, KV_WIDTH), F32), pltpu.VMEM((TILE, KV_WIDTH), F32)],
        compiler_params=pltpu.CompilerParams(vmem_limit_bytes=VMEM_LIMIT),
    )(do, au, av, sq, sk, sv, sk, sv, mq, z, mkv, v_g, v_b, w_sp, b_sp, sinks, bias, w_out)


def _backward_projection(x2, dout, dparts, g_pre, w_in_t):
    n_tok = x2.shape[0]
    n_steps = n_tok // PROJ_TILE

    def body(x_ref, dout_ref, dau, dav, dsq, dsk, dsv, dmq, dz, g_ref, w_hbm,
             dx_ref, dwin_hbm, dgpre_ref, w_vmem, acc, sem):
        step = pl.program_id(0)

        @pl.when(step == 0)
        def _():
            load = pltpu.make_async_copy(w_hbm, w_vmem, sem)
            load.start()
            acc[...] = jnp.zeros_like(acc)
            dgpre_ref[...] = jnp.zeros_like(dgpre_ref)
            load.wait()

        xv = x_ref[...]
        r = lax.rsqrt(jnp.mean(xv * xv, axis=-1, keepdims=True) + EPS)
        xn = xv * r
        g = g_ref[...]
        h_b = (xn * g).astype(BF16)
        dh = jnp.zeros((PROJ_TILE, D_MODEL), F32)
        for k, ref in enumerate((dau, dav, dsq, dsk, dsv, dmq, dz)):
            rows = slice(PROJ_OFFSETS[k], PROJ_OFFSETS[k + 1])
            dp = ref[...].astype(BF16)
            acc[rows, :] += _mm_tn(dp, h_b)
            dh = dh + _mm(dp, w_vmem[rows, :])
        dgpre_ref[...] += jnp.sum(dh * xn, axis=0, keepdims=True)
        dhg = dh * g
        dx_ref[...] = r * (dhg - xn * jnp.mean(dhg * xn, axis=-1, keepdims=True)) + dout_ref[...]

        @pl.when(step == n_steps - 1)
        def _():
            store = pltpu.make_async_copy(acc, dwin_hbm, sem)
            store.start()
            store.wait()

    row = lambda w: pl.BlockSpec((PROJ_TILE, w), lambda i: (i, 0))
    return pl.pallas_call(
        body, name="backward_projection", grid=(n_steps,),
        out_shape=[jax.ShapeDtypeStruct((n_tok, D_MODEL), F32), jax.ShapeDtypeStruct((IN_WIDTH, D_MODEL), F32),
                   jax.ShapeDtypeStruct((1, D_MODEL), F32)],
        in_specs=[row(D_MODEL), row(D_MODEL)] + [row(w) for w in PROJ_WIDTHS] + [_full_spec((1, D_MODEL)), ANY_SPEC],
        out_specs=[row(D_MODEL), ANY_SPEC, _full_spec((1, D_MODEL))],
        scratch_shapes=[pltpu.VMEM((IN_WIDTH, D_MODEL), BF16), pltpu.VMEM((IN_WIDTH, D_MODEL), F32),
                        pltpu.SemaphoreType.DMA],
        input_output_aliases={1: 0},
        compiler_params=pltpu.CompilerParams(vmem_limit_bytes=VMEM_LIMIT),
    )(x2, dout, *dparts, g_pre, w_in_t)


def _memkv_backward(mem, dmkv, g_mem, w_mkv):
    n_ex = mem.shape[0]

    def body(mem_ref, d_ref, g_ref, w_ref, dw_ref, dg_ref):
        @pl.when(pl.program_id(0) == 0)
        def _():
            dw_ref[...] = jnp.zeros_like(dw_ref)
            dg_ref[...] = jnp.zeros_like(dg_ref)

        m = mem_ref[0]
        mn = m * lax.rsqrt(jnp.mean(m * m, axis=-1, keepdims=True) + EPS)
        d_b = d_ref[0].astype(BF16)
        dw_ref[...] += _mm_tn((mn * g_ref[...]).astype(BF16), d_b)
        dg_ref[...] += jnp.sum(_mm_nt(d_b, w_ref[...]) * mn, axis=0, keepdims=True)

    return pl.pallas_call(
        body, name="memkv_backward", grid=(n_ex,),
        out_shape=[jax.ShapeDtypeStruct((D_MODEL, 2 * MEM_WIDTH), F32), jax.ShapeDtypeStruct((1, D_MODEL), F32)],
        in_specs=[pl.BlockSpec((1, MEM_LEN, D_MODEL), lambda b: (b, 0, 0)),
                  pl.BlockSpec((1, MEM_LEN, 2 * MEM_WIDTH), lambda b: (b, 0, 0)),
                  _full_spec((1, D_MODEL)), _full_spec((D_MODEL, 2 * MEM_WIDTH))],
        out_specs=[_full_spec((D_MODEL, 2 * MEM_WIDTH)), _full_spec((1, D_MODEL))],
    )(mem, dmkv, g_mem, w_mkv)


def _pack_small_grads(dgpre, dgpost, dgmem, dvg, dvb, dws, dbs, dsink, drel, buckets):
    def body(dgpre_ref, dgpost_ref, dgmem_ref, dvg_ref, dvb_ref, dws_ref, dbs_ref, dsink_ref, drel_ref, bk_ref,
             a_ref, b_ref):
        a_ref[...] = jnp.zeros_like(a_ref)
        b_ref[...] = jnp.zeros_like(b_ref)
        a_ref[0:1, :] = dgpre_ref[...]
        a_ref[1:2, :] = dgpost_ref[...]
        a_ref[2:3, :] = dgmem_ref[...]
        a_ref[3:4, :] = jnp.concatenate([dvg_ref[...], dvb_ref[...]], axis=-1)
        row = lax.broadcasted_iota(jnp.int32, (CHUNK, CHUNK), 0)
        col = lax.broadcasted_iota(jnp.int32, (CHUNK, CHUNK), 1)
        for g in range(A_GROUPS):
            b_ref[ROW_WS + g * CHUNK:ROW_WS + (g + 1) * CHUNK, :] = jnp.where(row >= col, dws_ref[g], 0.0)
            by_token = jnp.transpose(dbs_ref[:, g * 128:(g + 1) * 128])
            b_ref[ROW_BS + g:ROW_BS + g + 1, :] = jnp.sum(by_token, axis=0, keepdims=True)
        b_ref[ROW_SINK:ROW_SINK + 1, :] = dsink_ref[...]
        bk = bk_ref[...]
        rel_row = lax.broadcasted_iota(jnp.int32, (8, 128), 0)
        rel_col = lax.broadcasted_iota(jnp.int32, (8, 128), 1)
        rel = jnp.zeros((8, 128), F32)
        for h in range(4):
            acc = drel_ref[h]
            for b in range(N_BUCKETS):
                rel = jnp.where((rel_row == h) & (rel_col == b), jnp.sum(jnp.where(bk == b, acc, 0.0)), rel)
        b_ref[ROW_REL:ROW_REL + 8, :] = rel

    return pl.pallas_call(
        body, name="pack_small_grads",
        out_shape=[jax.ShapeDtypeStruct((SMALL_A_ROWS, D_MODEL), F32), jax.ShapeDtypeStruct((SMALL_B_ROWS, 128), F32)],
        in_specs=[VMEM_SPEC] * 10, out_specs=[VMEM_SPEC] * 2,
    )(dgpre, dgpost, dgmem, dvg, dvb, dws, dbs, dsink, drel, buckets)


def _exchange_siblings(big, small_a, small_b):
    def body(g0, g1, g2, sa, sb, r0, r1, r2, ra, rb, send_sems, recv_sems):
        x, y, c = lax.axis_index("x"), lax.axis_index("y"), lax.axis_index("c")
        pairs = [(g.at[:, pl.ds(1 - c, 1)], r) for g, r in ((g0, r0), (g1, r1), (g2, r2))] + [(sa, ra), (sb, rb)]
        copies = [pltpu.make_async_remote_copy(src_ref=src, dst_ref=dst, send_sem=send_sems.at[k],
                                               recv_sem=recv_sems.at[k], device_id=(x, y, 1 - c),
                                               device_id_type=MESH)
                  for k, (src, dst) in enumerate(pairs)]
        for cp in copies:
            cp.start()
        for cp in copies:
            cp.wait_recv()
        for cp in copies:
            cp.wait_send()

    out_shape = [jax.ShapeDtypeStruct((g.shape[0], 1) + g.shape[2:], F32) for g in big]
    out_shape += [jax.ShapeDtypeStruct(small_a.shape, F32), jax.ShapeDtypeStruct(small_b.shape, F32)]
    return pl.pallas_call(
        body, name="exchange_siblings", out_shape=out_shape,
        in_specs=[ANY_SPEC] * 5, out_specs=[ANY_SPEC] * 5,
        scratch_shapes=[pltpu.SemaphoreType.DMA((5,)), pltpu.SemaphoreType.DMA((5,))],
    )(*big, small_a, small_b)


def _chip_sum(big, recv, small, small_recv, c_arr):
    def body(c_ref, g0, g1, g2, r0, r1, r2, sa, sb, ra, rb, p0, p1, p2, ca, cb):
        for g, r, p in ((g0, r0, p0), (g1, r1, p1), (g2, r2, p2)):
            p[...] = (g[...] + r[...]).astype(BF16)

        @pl.when(pl.program_id(0) == 0)
        def _():
            ca[...] = sa[...] + ra[...]
            cb[...] = sb[...] + rb[...]

    def own(g):
        return pl.BlockSpec((1, 1) + g.shape[2:], lambda j, c_ref: (j, c_ref[0], 0, 0))

    def got(g):
        return pl.BlockSpec((1, 1) + g.shape[2:], lambda j, c_ref: (j, 0, 0, 0))

    def whole(a):
        return pl.BlockSpec(a.shape, lambda j, c_ref: (0, 0))

    return pl.pallas_call(
        body, name="chip_sum",
        out_shape=[jax.ShapeDtypeStruct(r.shape, BF16) for r in recv]
        + [jax.ShapeDtypeStruct(a.shape, F32) for a in small],
        grid_spec=pltpu.PrefetchScalarGridSpec(
            num_scalar_prefetch=1, grid=(N_CHIPS,),
            in_specs=[own(g) for g in big] + [got(g) for g in big] + [whole(a) for a in small + small_recv],
            out_specs=[got(g) for g in big] + [whole(a) for a in small]),
        compiler_params=pltpu.CompilerParams(vmem_limit_bytes=VMEM_LIMIT),
    )(c_arr, *big, *recv, *small, *small_recv)


def _exchange_chips(partials, small):
    def body(p0, p1, p2, ca, cb, r0, r1, r2, ga, gb, send_sems, recv_sems, local_sems):
        x, y, c = lax.axis_index("x"), lax.axis_index("y"), lax.axis_index("c")
        chips = [(1 - x, y), (x, 1 - y), (1 - x, 1 - y)]
        my_chip = 2 * x + y
        own = [pltpu.make_async_copy(ca, ga.at[my_chip], local_sems.at[0]),
               pltpu.make_async_copy(cb, gb.at[my_chip], local_sems.at[1])]
        for cp in own:
            cp.start()
        copies = []
        for j, chip in enumerate(chips):
            pairs = [(p.at[2 * chip[0] + chip[1]], r.at[j]) for p, r in ((p0, r0), (p1, r1), (p2, r2))]
            pairs += [(ca, ga.at[my_chip]), (cb, gb.at[my_chip])]
            for w, (src, dst) in enumerate(pairs):
                copies.append(pltpu.make_async_remote_copy(
                    src_ref=src, dst_ref=dst, send_sem=send_sems.at[5 * j + w], recv_sem=recv_sems.at[5 * j + w],
                    device_id=(*chip, c), device_id_type=MESH))
        for cp in copies:
            cp.start()
        for cp in copies:
            cp.wait_recv()
        for cp in copies:
            cp.wait_send()
        for cp in own:
            cp.wait()

    return pl.pallas_call(
        body, name="exchange_chips",
        out_shape=[jax.ShapeDtypeStruct((3,) + p.shape[1:], BF16) for p in partials]
        + [jax.ShapeDtypeStruct((N_CHIPS,) + a.shape, F32) for a in small],
        in_specs=[ANY_SPEC] * 5, out_specs=[ANY_SPEC] * 5,
        scratch_shapes=[pltpu.SemaphoreType.DMA((15,)), pltpu.SemaphoreType.DMA((15,)),
                        pltpu.SemaphoreType.DMA((2,))],
    )(*partials, *small)


def _shard_sum(partials, recv, shard_arr):
    def body(s_ref, p0, p1, p2, r0, r1, r2, o0, o1, o2):
        for p, r, o in ((p0, r0, o0), (p1, r1, o1), (p2, r2, o2)):
            o[...] = ((p[0, 0].astype(F32) + r[0, 0].astype(F32)) + r[1, 0].astype(F32)) + r[2, 0].astype(F32)

    def own(p):
        return pl.BlockSpec((1,) + p.shape[1:], lambda i, s_ref: (s_ref[0], 0, 0, 0))

    def got(p):
        return pl.BlockSpec((3,) + p.shape[1:], lambda i, s_ref: (0, 0, 0, 0))

    return pl.pallas_call(
        body, name="shard_sum",
        out_shape=[jax.ShapeDtypeStruct(p.shape[2:], F32) for p in partials],
        grid_spec=pltpu.PrefetchScalarGridSpec(
            num_scalar_prefetch=1, grid=(1,),
            in_specs=[own(p) for p in partials] + [got(p) for p in partials],
            out_specs=[pl.BlockSpec(p.shape[2:], lambda i, s_ref: (0, 0)) for p in partials]),
        compiler_params=pltpu.CompilerParams(vmem_limit_bytes=VMEM_LIMIT),
    )(shard_arr, *partials, *recv)


def _exchange_halves(halves):
    def body(h0, h1, h2, r0, r1, r2, send_sems, recv_sems):
        x, y, c = lax.axis_index("x"), lax.axis_index("y"), lax.axis_index("c")
        copies = [pltpu.make_async_remote_copy(src_ref=h, dst_ref=r, send_sem=send_sems.at[w],
                                               recv_sem=recv_sems.at[w], device_id=(x, y, 1 - c),
                                               device_id_type=MESH)
                  for w, (h, r) in enumerate(((h0, r0), (h1, r1), (h2, r2)))]
        for cp in copies:
            cp.start()
        for cp in copies:
            cp.wait_recv()
        for cp in copies:
            cp.wait_send()

    return pl.pallas_call(
        body, name="exchange_halves",
        out_shape=[jax.ShapeDtypeStruct(h.shape, F32) for h in halves],
        in_specs=[ANY_SPEC] * 3, out_specs=[ANY_SPEC] * 3,
        scratch_shapes=[pltpu.SemaphoreType.DMA((3,)), pltpu.SemaphoreType.DMA((3,))],
    )(*halves)


def _adamw(w, g, m, v):
    m2 = ADAM_B1 * m + (1.0 - ADAM_B1) * g
    v2 = ADAM_B2 * v + (1.0 - ADAM_B2) * (g * g)
    m_hat = m2 / (1.0 - ADAM_B1 ** ADAM_STEP)
    v_hat = v2 / (1.0 - ADAM_B2 ** ADAM_STEP)
    delta = -ADAM_LR * (m_hat / (jnp.sqrt(v_hat) + ADAM_EPS) + ADAM_WD * w)
    return delta, m2, v2


ADAM_MAX_ROWS = 176


def _adamw_sharded(mine, other, w, m, v, c_arr, name):
    rows, cols = w.shape
    half = rows // 2
    steps = -(-half // ADAM_MAX_ROWS)
    block_rows = half // steps
    assert block_rows * steps == half and block_rows % 8 == 0

    def body(c_ref, mine_ref, other_ref, w_ref, m_ref, v_ref, g_out, d_out, m_out, v_out):
        g = jnp.where(pl.program_id(0) == c_ref[0], mine_ref[...], other_ref[...])
        delta, m2, v2 = _adamw(w_ref[...], g, m_ref[...], v_ref[...])
        g_out[...] = g
        d_out[...] = delta
        m_out[...] = m2
        v_out[...] = v2

    part = pl.BlockSpec((block_rows, cols), lambda h, k, c_ref: (k, 0))
    full = pl.BlockSpec((block_rows, cols), lambda h, k, c_ref: (h * steps + k, 0))
    return pl.pallas_call(
        body, name=name, out_shape=[jax.ShapeDtypeStruct((rows, cols), F32)] * 4,
        grid_spec=pltpu.PrefetchScalarGridSpec(
            num_scalar_prefetch=1, grid=(2, steps), in_specs=[part, part, full, full, full], out_specs=[full] * 4),
    )(c_arr, mine, other, w, m, v)


def _adamw_small(ra, rb, weights, moments_m, moments_v):
    n = len(weights)

    def body(*refs):
        ra_ref, rb_ref = refs[0], refs[1]
        w_refs, m_refs, v_refs = refs[2:2 + n], refs[2 + n:2 + 2 * n], refs[2 + 2 * n:2 + 3 * n]
        outs = refs[2 + 3 * n:]
        g_outs, d_outs, m_outs, v_outs = outs[:n], outs[n:2 * n], outs[2 * n:3 * n], outs[3 * n:]
        ga, gb = ra_ref[0], rb_ref[0]
        for chip in range(1, N_CHIPS):
            ga = ga + ra_ref[chip]
            gb = gb + rb_ref[chip]
        grads = [ga[0:1, :], ga[1:2, :], ga[2:3, :], ga[3:4, :A_WIDTH], ga[3:4, A_WIDTH:],
                 gb[ROW_WS:ROW_WS + A_GROUPS * CHUNK, :].reshape(A_GROUPS, CHUNK, CHUNK),
                 gb[ROW_BS:ROW_BS + A_GROUPS, :], gb[ROW_SINK:ROW_SINK + 1, 0:4],
                 gb[ROW_REL:ROW_REL + 4, 0:N_BUCKETS]]
        for k in range(n):
            delta, m2, v2 = _adamw(w_refs[k][...], grads[k], m_refs[k][...], v_refs[k][...])
            g_outs[k][...] = grads[k]
            d_outs[k][...] = delta
            m_outs[k][...] = m2
            v_outs[k][...] = v2

    out_shape = [jax.ShapeDtypeStruct(w.shape, F32) for w in weights] * 4
    return pl.pallas_call(
        body, name="adamw_small", out_shape=out_shape,
        in_specs=[VMEM_SPEC] * (2 + 3 * n), out_specs=[VMEM_SPEC] * (4 * n),
    )(ra, rb, *weights, *moments_m, *moments_v)


def kernel(x, mem, pre_norm_g, post_norm_g, mem_norm_g, w_in, w_mem_kv, v_norm_g, v_norm_b, w_spatial, b_spatial, attn_sinks, rel_bias, w_out, loss_target, m_pre_norm_g, m_post_norm_g, m_mem_norm_g, m_w_in, m_w_mem_kv, m_v_norm_g, m_v_norm_b, m_w_spatial, m_b_spatial, m_attn_sinks, m_rel_bias, m_w_out, v_pre_norm_g, v_post_norm_g, v_mem_norm_g, v_w_in, v_w_mem_kv, v_v_norm_g, v_v_norm_b, v_w_spatial, v_b_spatial, v_attn_sinks, v_rel_bias, v_w_out):
    n_ex, seq, _ = x.shape
    n_tok = n_ex * seq
    x2 = x.reshape(n_tok, D_MODEL)
    tgt2 = loss_target.reshape(n_tok, D_MODEL)
    buckets = jnp.asarray(_bucket_map())
    c_arr = lax.axis_index("c").astype(jnp.int32).reshape(1)
    shard_arr = (2 * lax.axis_index("x") + lax.axis_index("y")).astype(jnp.int32).reshape(1)
    w_sp = w_spatial[0]
    b_sp = b_spatial[0][:, :, None]
    w_in_t, m_w_in_t, v_w_in_t = (jnp.transpose(a[0]) for a in (w_in, m_w_in, v_w_in))
    rel_t, m_rel_t, v_rel_t = (jnp.transpose(a) for a in (rel_bias, m_rel_bias, v_rel_bias))

    g_in, g_mkv, g_out = _gather_weights(w_in_t, w_mem_kv[0], w_out[0])
    w_in_b = g_in.reshape(IN_WIDTH, D_MODEL)
    w_mkv_b = g_mkv.reshape(D_MODEL, 2 * MEM_WIDTH)
    w_out_b = g_out.reshape(MIX_WIDTH, D_MODEL)

    bias = _make_bias(rel_t, buckets)
    mkv = _memkv_forward(mem, mem_norm_g, w_mkv_b)
    parts = _forward_projection(x2, pre_norm_g, w_in_b)
    dout, do, loss_vec, dgpost = _forward_mix(parts, mkv, x2, tgt2, v_norm_g, v_norm_b, w_sp, b_sp, attn_sinks, bias,
                                             w_out_b, post_norm_g, n_ex, seq)

    (dau, dav, dsq, dsk, dsv, dmq, dz, dmkv, dwout, dvg, dvb, dws, dbs, dsink, drel) = _backward_mix(
        parts, mkv, do, v_norm_g, v_norm_b, w_sp, b_sp, attn_sinks, bias, w_out_b, n_ex, seq)
    dx, dwin, dgpre = _backward_projection(x2, dout, (dau, dav, dsq, dsk, dsv, dmq, dz), pre_norm_g, w_in_b)
    dwmkv, dgmem = _memkv_backward(mem, dmkv, mem_norm_g, w_mkv_b)
    small_a, small_b = _pack_small_grads(dgpre, dgpost, dgmem, dvg, dvb, dws, dbs, dsink, drel, buckets)

    shard_shapes = [w_in_t.shape, w_mem_kv.shape[1:], w_out.shape[1:]]
    big = [g.reshape(N_CHIPS, 2, s[0] // 2, s[1]) for g, s in zip((dwin, dwmkv, dwout), shard_shapes)]
    *recv, ra, rb = _exchange_siblings(big, small_a, small_b)
    *partials, ca, cb = _chip_sum(big, recv, [small_a, small_b], [ra, rb], c_arr)
    *recv2, ga, gb = _exchange_chips(partials, [ca, cb])
    mine = _shard_sum(partials, recv2, shard_arr)
    other = _exchange_halves(mine)

    big_w = [(w_in_t, m_w_in_t, v_w_in_t), (w_mem_kv[0], m_w_mem_kv[0], v_w_mem_kv[0]),
             (w_out[0], m_w_out[0], v_w_out[0])]
    big_names = ["adamw_w_in", "adamw_w_mem_kv", "adamw_w_out"]
    big_out = [_adamw_sharded(mine[k], other[k], *big_w[k], c_arr, big_names[k]) for k in range(3)]
    small_w = [pre_norm_g, post_norm_g, mem_norm_g, v_norm_g, v_norm_b, w_sp, b_spatial[0], attn_sinks, rel_t]
    small_m = [m_pre_norm_g, m_post_norm_g, m_mem_norm_g, m_v_norm_g, m_v_norm_b, m_w_spatial[0], m_b_spatial[0],
               m_attn_sinks, m_rel_t]
    small_v = [v_pre_norm_g, v_post_norm_g, v_mem_norm_g, v_v_norm_g, v_v_norm_b, v_w_spatial[0], v_b_spatial[0],
               v_attn_sinks, v_rel_t]
    small_out = _adamw_small(ga, gb, small_w, small_m, small_v)
    n_small = len(small_w)

    loss = lax.psum(loss_vec[0, 0], ALL_AXES)
    outputs = [loss, dx.reshape(x.shape)]
    for kind in range(4):
        s = small_out[kind * n_small:(kind + 1) * n_small]
        outputs += [s[0], s[1], s[2], jnp.transpose(big_out[0][kind])[None], big_out[1][kind][None], s[3], s[4],
                    s[5][None], s[6][None], s[7], jnp.transpose(s[8]), big_out[2][kind][None]]
    return tuple(outputs)
```

```python
import functools

import numpy as np
import jax
import jax.numpy as jnp
from jax import lax
from jax.experimental import pallas as pl
from jax.experimental.pallas import tpu as pltpu

F32 = jnp.float32
BF16 = jnp.bfloat16
MESH = pl.DeviceIdType.MESH
ALL_AXES = ("x", "y", "c")

D_MODEL = 1024
CHUNK = 128
A_WIDTH = 512
A_GROUPS = 4
SWA_WIDTH = 256
KV_WIDTH = 128
MEM_WIDTH = 256
MEM_LEN = 256
MIX_WIDTH = 1024
IN_WIDTH = 2816
N_BUCKETS = 32
MAX_DISTANCE = 128
EPS = 1e-6
NEG = -1e30
QK_SCALE = 0.125
HALF_HEAD_PAIR = 64

ADAM_LR = 0.001
ADAM_B1 = 0.9
ADAM_B2 = 0.999
ADAM_EPS = 1e-08
ADAM_WD = 0.01
ADAM_STEP = 10

N_CHIPS = 4
N_DEV = 8
TILE_CHUNKS = 2
TILE = TILE_CHUNKS * CHUNK
PROJ_TILE = 256
VMEM_LIMIT = 56 * 1024 * 1024

SMALL_A_ROWS = 8
ROW_WS = 0
ROW_BS = 512
ROW_SINK = 520
ROW_REL = 528
SMALL_B_ROWS = 536


def _mm(a, b):
    return lax.dot_general(a, b, (((1,), (0,)), ((), ())), preferred_element_type=F32)


def _mm_nt(a, b):
    return lax.dot_general(a, b, (((1,), (1,)), ((), ())), preferred_element_type=F32)


def _mm_tn(a, b):
    return lax.dot_general(a, b, (((0,), (0,)), ((), ())), preferred_element_type=F32)


def _bucket_map():
    qi = np.arange(CHUNK)[:, None]
    kj = np.arange(2 * CHUNK)[None, :]
    n = np.maximum(qi + CHUNK - kj, 0)
    max_exact = N_BUCKETS // 2
    large = max_exact + (np.log(np.maximum(n, 1) / max_exact) / np.log(MAX_DISTANCE / max_exact)
                         * (N_BUCKETS - max_exact)).astype(np.int32)
    large = np.minimum(large, N_BUCKETS - 1)
    return np.where(n < max_exact, n, large).astype(np.int32)


_GELU_C = 0.7978845608028654
_GELU_A = 0.044715


def _gelu(x):
    t = jnp.tanh(_GELU_C * (x + _GELU_A * x * x * x))
    return 0.5 * x * (1.0 + t), t


def _gelu_grad(x, t):
    return 0.5 * (1.0 + t) + 0.5 * x * (1.0 - t * t) * (_GELU_C * (1.0 + 3.0 * _GELU_A * x * x))


def _sigmoid(x):
    return 1.0 / (1.0 + jnp.exp(-x))


def _lane_lo(shape):
    return lax.broadcasted_iota(jnp.int32, shape, 1) < HALF_HEAD_PAIR


def _swa_variants(t):
    lo = _lane_lo(t.shape)
    tr = pltpu.roll(t, HALF_HEAD_PAIR, 1)
    zero = jnp.zeros_like(t)
    return (jnp.where(lo, t, zero).astype(BF16), jnp.where(lo, zero, tr).astype(BF16),
            jnp.where(lo, tr, zero).astype(BF16), jnp.where(lo, zero, t).astype(BF16))


def _swa_unvariants(d0, d1, d2, d3):
    lo = _lane_lo(d0.shape)
    zero = jnp.zeros_like(d0)
    rolled = jnp.where(lo, zero, d1) + jnp.where(lo, d2, zero)
    return jnp.where(lo, d0, zero) + jnp.where(lo, zero, d3) + pltpu.roll(rolled, HALF_HEAD_PAIR, 1)


def _mem_variants(t):
    out = []
    for pair in range(2):
        tp = t[:, pair * 128:(pair + 1) * 128]
        lo = _lane_lo(tp.shape)
        zero = jnp.zeros_like(tp)
        out.append(jnp.where(lo, tp, zero).astype(BF16))
        out.append(jnp.where(lo, zero, tp).astype(BF16))
    return out


def _mem_unvariants(d0, d1, d2, d3):
    lo = _lane_lo(d0.shape)
    return jnp.concatenate([jnp.where(lo, d0, d1), jnp.where(lo, d2, d3)], axis=-1)


def _softmax(logits, sinks):
    m = jnp.max(logits, axis=-1, keepdims=True)
    if sinks is not None:
        m = jnp.maximum(m, sinks)
    p = jnp.exp(logits - m)
    den = jnp.sum(p, axis=-1, keepdims=True)
    if sinks is None:
        return p * (1.0 / den), None
    es = jnp.exp(sinks - m)
    inv = 1.0 / (den + es)
    return p * inv, es * inv


def _band_valid(with_prev):
    qi = lax.broadcasted_iota(jnp.int32, (CHUNK, 2 * CHUNK), 0)
    kj = lax.broadcasted_iota(jnp.int32, (CHUNK, 2 * CHUNK), 1)
    in_cur = (kj >= CHUNK) & (kj - CHUNK <= qi)
    if not with_prev:
        return in_cur
    return in_cur | ((kj < CHUNK) & (kj > qi))


def _causal_weights(ws_ref):
    row = lax.broadcasted_iota(jnp.int32, (CHUNK, CHUNK), 0)
    col = lax.broadcasted_iota(jnp.int32, (CHUNK, CHUNK), 1)
    return [jnp.where(row >= col, ws_ref[g], 0.0).astype(BF16) for g in range(A_GROUPS)]


def _rows_to_lanes(a, n):
    return jnp.concatenate([a[c * CHUNK:(c + 1) * CHUNK] for c in range(n)], axis=1)


def _lanes_to_rows(a, n):
    w = a.shape[1] // n
    return jnp.concatenate([a[:, c * w:(c + 1) * w] for c in range(n)], axis=0)


def _stack_heads(pair01, pair23):
    return jnp.concatenate([pair01[:, :256], pair01[:, 256:], pair23[:, :256], pair23[:, 256:]], axis=0)


def _pair_heads(s, r):
    return (jnp.concatenate([s[0:r], s[r:2 * r]], axis=1), jnp.concatenate([s[2 * r:3 * r], s[3 * r:4 * r]], axis=1))


def _pair_operands(variants):
    return (jnp.concatenate(variants[0:2], axis=0), jnp.concatenate(variants[2:4], axis=0))


def _split_pair_grads(d_pairs):
    return d_pairs[0][:256], d_pairs[0][256:], d_pairs[1][:256], d_pairs[1][256:]


def _halves_bf16(a):
    return (a[:, :128].astype(BF16), a[:, 128:].astype(BF16))


def _group_a_forward(au, av, vg, vb, wm, bs_rows):
    gu, tu = _gelu(au)
    gv, tv = _gelu(av)
    ya, res = [], []
    for g in range(A_GROUPS):
        sl = slice(g * 128, (g + 1) * 128)
        xg = gv[:, sl]
        xc = xg - jnp.mean(xg, axis=-1, keepdims=True)
        rstd = lax.rsqrt(jnp.mean(xc * xc, axis=-1, keepdims=True) + EPS)
        xhat = xc * rstd
        vn = _rows_to_lanes((xhat * vg[:, sl] + vb[:, sl]).astype(BF16), TILE_CHUNKS)
        s = _lanes_to_rows(_mm(wm[g], vn), TILE_CHUNKS) + bs_rows[g]
        ya.append(gu[:, sl] * s)
        res.append((xhat, rstd, vn, s))
    return ya, dict(gu=gu, tu=tu, tv=tv, groups=res)


def _attention_probs(qp, k_pairs, bias, sink_col):
    logits = _stack_heads(_mm_nt(qp[0], k_pairs[0]), _mm_nt(qp[1], k_pairs[1])) * QK_SCALE
    if bias is not None:
        logits = logits + bias
    return _softmax(logits, sink_col)


def _attention_out(p, v_pairs, r):
    pp = _pair_heads(p.astype(BF16), r)
    return jnp.concatenate([_mm(pp[0], v_pairs[0]), _mm(pp[1], v_pairs[1])], axis=-1), pp


def _attention_backward(p, pp, do_pairs, qp, k_pairs, v_pairs, r):
    dp = _stack_heads(_mm_nt(do_pairs[0], v_pairs[0]), _mm_nt(do_pairs[1], v_pairs[1]))
    delta = jnp.sum(p * dp, axis=-1, keepdims=True)
    dl = p * (dp - delta)
    dlp = _pair_heads(dl.astype(BF16), r)
    dq = jnp.concatenate([_mm(dlp[0], k_pairs[0]), _mm(dlp[1], k_pairs[1])], axis=-1)
    dk = (_mm_tn(dlp[0], qp[0]), _mm_tn(dlp[1], qp[1]))
    dv = (_mm_tn(pp[0], do_pairs[0]), _mm_tn(pp[1], do_pairs[1]))
    return dl, delta, dq, dk, dv


def _tile_specs(n_tiles_ex, width):
    return pl.BlockSpec((TILE, width), lambda b, i: (b * n_tiles_ex + jnp.minimum(i, n_tiles_ex - 1), 0))


def _prev_chunk_spec(n_tiles_ex, width):
    def index(b, i):
        chunk = TILE_CHUNKS * jnp.minimum(i, n_tiles_ex - 1)
        return (b * n_tiles_ex * TILE_CHUNKS + jnp.maximum(chunk - 1, 0), 0)
    return pl.BlockSpec((CHUNK, width), index)


def _full_spec(shape):
    zeros = (0,) * len(shape)
    return pl.BlockSpec(shape, lambda *_: zeros)


SMEM_SPEC = pl.BlockSpec(memory_space=pltpu.SMEM)
ANY_SPEC = pl.BlockSpec(memory_space=pl.ANY)
VMEM_SPEC = pl.BlockSpec(memory_space=pltpu.VMEM)


def _make_bias(rel_bias_t, buckets):
    def body(rel_ref, bk_ref, out_ref):
        bk = bk_ref[...]
        for h in range(4):
            acc = jnp.zeros((CHUNK, 2 * CHUNK), F32)
            for b in range(N_BUCKETS):
                acc = jnp.where(bk == b, rel_ref[h, b], acc)
            for t, with_prev in enumerate((True, False)):
                out_ref[t, h * CHUNK:(h + 1) * CHUNK, :] = jnp.where(_band_valid(with_prev), acc, NEG)

    return pl.pallas_call(
        body, name="make_bias", out_shape=jax.ShapeDtypeStruct((2, 4 * CHUNK, 2 * CHUNK), F32),
        in_specs=[SMEM_SPEC, VMEM_SPEC], out_specs=VMEM_SPEC,
    )(rel_bias_t, buckets)


def _gather_weights(w_in_s, w_mkv_s, w_out_s):
    shapes = [w_in_s.shape, w_mkv_s.shape, w_out_s.shape]
    n_w = len(shapes)

    def body(win_ref, wmkv_ref, wout_ref, gin_ref, gmkv_ref, gout_ref, send_sems, recv_sems):
        x, y, c = lax.axis_index("x"), lax.axis_index("y"), lax.axis_index("c")
        me, sibling = (x, y, c), (x, y, 1 - c)
        chips = [(1 - x, y), (x, 1 - y), (1 - x, 1 - y)]
        ins = [win_ref, wmkv_ref, wout_ref]
        outs = [gin_ref, gmkv_ref, gout_ref]
        my_shard = 2 * x + y
        for w in range(n_w):
            outs[w][my_shard] = ins[w][...].astype(BF16)

        def copy(k, w, shard, half, to):
            rows = shapes[w][0] // 2
            ref = outs[w].at[shard, pl.ds(half * rows, rows), :]
            return pltpu.make_async_remote_copy(src_ref=ref, dst_ref=ref, send_sem=send_sems.at[k],
                                                recv_sem=recv_sems.at[k], device_id=to, device_id_type=MESH)

        pairs = [(w, j) for w in range(n_w) for j in range(3)]
        first = [copy(3 * w + j, w, my_shard, c, (*chips[j], c)) for w, j in pairs]
        for cp in first:
            cp.start()
        passed = []
        for w, j in pairs:
            shard = 2 * chips[j][0] + chips[j][1]
            copy(3 * w + j, w, shard, c, me).wait_recv()
            fwd = copy(9 + 3 * w + j, w, shard, c, sibling)
            fwd.start()
            passed.append(fwd)
        for w, j in pairs:
            shard = 2 * chips[j][0] + chips[j][1]
            copy(9 + 3 * w + j, w, shard, 1 - c, me).wait_recv()
        for cp in first + passed:
            cp.wait_send()

    return pl.pallas_call(
        body, name="gather_weights",
        out_shape=[jax.ShapeDtypeStruct((N_CHIPS,) + s, BF16) for s in shapes],
        in_specs=[VMEM_SPEC] * 3, out_specs=[VMEM_SPEC] * 3,
        scratch_shapes=[pltpu.SemaphoreType.DMA((18,)), pltpu.SemaphoreType.DMA((18,))],
        compiler_params=pltpu.CompilerParams(vmem_limit_bytes=VMEM_LIMIT),
    )(w_in_s, w_mkv_s, w_out_s)


def _memkv_forward(mem, g_mem, w_mkv):
    n_ex = mem.shape[0]

    def body(mem_ref, g_ref, w_ref, out_ref):
        m = mem_ref[0]
        r = lax.rsqrt(jnp.mean(m * m, axis=-1, keepdims=True) + EPS)
        out_ref[0] = _mm((m * r * g_ref[...]).astype(BF16), w_ref[...])

    return pl.pallas_call(
        body, name="memkv_forward", grid=(n_ex,),
        out_shape=jax.ShapeDtypeStruct((n_ex, MEM_LEN, 2 * MEM_WIDTH), F32),
        in_specs=[pl.BlockSpec((1, MEM_LEN, D_MODEL), lambda b: (b, 0, 0)), _full_spec((1, D_MODEL)),
                  _full_spec((D_MODEL, 2 * MEM_WIDTH))],
        out_specs=pl.BlockSpec((1, MEM_LEN, 2 * MEM_WIDTH), lambda b: (b, 0, 0)),
    )(mem, g_mem, w_mkv)


PROJ_WIDTHS = (A_WIDTH, A_WIDTH, SWA_WIDTH, KV_WIDTH, KV_WIDTH, MEM_WIDTH, MIX_WIDTH)
PROJ_OFFSETS = tuple(int(v) for v in np.cumsum((0,) + PROJ_WIDTHS))


def _forward_projection(x2, g_pre, w_in_t):
    n_tok = x2.shape[0]

    def body(x_ref, g_ref, w_ref, *out_refs):
        xv = x_ref[...]
        r = lax.rsqrt(jnp.mean(xv * xv, axis=-1, keepdims=True) + EPS)
        proj = _mm_nt((xv * r * g_ref[...]).astype(BF16), w_ref[...])
        for k, ref in enumerate(out_refs):
            ref[...] = proj[:, PROJ_OFFSETS[k]:PROJ_OFFSETS[k + 1]]

    return pl.pallas_call(
        body, name="forward_projection", grid=(n_tok // PROJ_TILE,),
        out_shape=[jax.ShapeDtypeStruct((n_tok, w), F32) for w in PROJ_WIDTHS],
        in_specs=[pl.BlockSpec((PROJ_TILE, D_MODEL), lambda i: (i, 0)), _full_spec((1, D_MODEL)),
                  _full_spec((IN_WIDTH, D_MODEL))],
        out_specs=[pl.BlockSpec((PROJ_TILE, w), lambda i: (i, 0)) for w in PROJ_WIDTHS],
        compiler_params=pltpu.CompilerParams(vmem_limit_bytes=VMEM_LIMIT),
    )(x2, g_pre, w_in_t)


def _load_chunk(j, i, sk_ref, sv_ref, skp_ref, svp_ref):
    rows = slice(j * CHUNK, (j + 1) * CHUNK)
    if j == 0:
        k_prev, v_prev, table = skp_ref[...], svp_ref[...], jnp.where(i > 0, 0, 1)
    else:
        prev = slice((j - 1) * CHUNK, j * CHUNK)
        k_prev, v_prev, table = sk_ref[prev, :], sv_ref[prev, :], 0
    k_pairs = _pair_operands(_swa_variants(jnp.concatenate([k_prev, sk_ref[rows, :]], axis=0)))
    v_pairs = _pair_operands(_swa_variants(jnp.concatenate([v_prev, sv_ref[rows, :]], axis=0)))
    return rows, k_pairs, v_pairs, table


def _tile_constants(ws_ref, bs_ref, sink_ref, mkv_ref):
    wm = _causal_weights(ws_ref)
    bs_rows = [jnp.concatenate([bs_ref[g]] * TILE_CHUNKS, axis=0) for g in range(A_GROUPS)]
    sink_col = jnp.max(jnp.concatenate([jnp.full((CHUNK, 128), sink_ref[0, h], F32) for h in range(4)], axis=0),
                       axis=-1, keepdims=True)
    mkv_v = mkv_ref[0]
    mk_pairs = _pair_operands(_mem_variants(mkv_v[:, :MEM_WIDTH]))
    mv_pairs = _pair_operands(_mem_variants(mkv_v[:, MEM_WIDTH:]))
    return wm, bs_rows, sink_col, mk_pairs, mv_pairs


def _forward_mix(parts, mkv, x2, tgt2, v_g, v_b, w_sp, b_sp, sinks, bias, w_out, g_post, n_ex, seq):
    n_tiles_ex = seq // TILE
    n_tok = n_ex * seq
    au, av, sq, sk, sv, mq, z = parts

    def body(au_ref, av_ref, sq_ref, sk_ref, sv_ref, skp_ref, svp_ref, mq_ref, z_ref, mkv_ref, x_ref, tgt_ref,
             vg_ref, vb_ref, ws_ref, bs_ref, sink_ref, bias_ref, wout_ref, gpost_ref,
             dout_ref, do_ref, loss_ref, dgpost_ref):
        b, i = pl.program_id(0), pl.program_id(1)

        @pl.when((b == 0) & (i == 0))
        def _():
            loss_ref[...] = jnp.zeros_like(loss_ref)
            dgpost_ref[...] = jnp.zeros_like(dgpost_ref)

        wm, bs_rows, sink_col, mk_pairs, mv_pairs = _tile_constants(ws_ref, bs_ref, sink_ref, mkv_ref)
        ya, _ = _group_a_forward(au_ref[...], av_ref[...], vg_ref[...], vb_ref[...], wm, bs_rows)
        yb = []
        for j in range(TILE_CHUNKS):
            rows, k_pairs, v_pairs, table = _load_chunk(j, i, sk_ref, sv_ref, skp_ref, svp_ref)
            p, _ = _attention_probs(_halves_bf16(sq_ref[rows, :]), k_pairs, bias_ref[table], sink_col)
            yb.append(_attention_out(p, v_pairs, CHUNK)[0])
        pm, _ = _attention_probs(_halves_bf16(mq_ref[...]), mk_pairs, None, None)
        yc = _attention_out(pm, mv_pairs, TILE)[0]
        ycat = jnp.concatenate(ya + [jnp.concatenate(yb, axis=0), yc], axis=-1)
        zv = z_ref[...]
        y = ycat * (zv * _sigmoid(zv))
        o = _mm(y.astype(BF16), wout_ref[...])
        r2 = lax.rsqrt(jnp.mean(o * o, axis=-1, keepdims=True) + EPS)
        nrm = o * r2
        gp = gpost_ref[...]
        diff = x_ref[...] + nrm * gp - tgt_ref[...]
        loss_ref[...] += jnp.sum(diff * diff) * (0.5 / D_MODEL)
        dout = diff * (1.0 / D_MODEL)
        dout_ref[...] = dout
        dgpost_ref[...] += jnp.sum(dout * nrm, axis=0, keepdims=True)
        dn = dout * gp
        do_ref[...] = r2 * (dn - nrm * jnp.mean(dn * nrm, axis=-1, keepdims=True))

    tile = functools.partial(_tile_specs, n_tiles_ex)
    prev = functools.partial(_prev_chunk_spec, n_tiles_ex)
    return pl.pallas_call(
        body, name="forward_mix", grid=(n_ex, n_tiles_ex),
        out_shape=[jax.ShapeDtypeStruct((n_tok, D_MODEL), F32), jax.ShapeDtypeStruct((n_tok, D_MODEL), F32),
                   jax.ShapeDtypeStruct((1, 128), F32), jax.ShapeDtypeStruct((1, D_MODEL), F32)],
        in_specs=[tile(A_WIDTH), tile(A_WIDTH), tile(SWA_WIDTH), tile(KV_WIDTH), tile(KV_WIDTH),
                  prev(KV_WIDTH), prev(KV_WIDTH), tile(MEM_WIDTH), tile(MIX_WIDTH),
                  pl.BlockSpec((1, MEM_LEN, 2 * MEM_WIDTH), lambda b, i: (b, 0, 0)),
                  tile(D_MODEL), tile(D_MODEL),
                  _full_spec((1, A_WIDTH)), _full_spec((1, A_WIDTH)), _full_spec((A_GROUPS, CHUNK, CHUNK)),
                  _full_spec((A_GROUPS, CHUNK, CHUNK)), SMEM_SPEC, _full_spec((2, 4 * CHUNK, 2 * CHUNK)),
                  _full_spec((MIX_WIDTH, D_MODEL)), _full_spec((1, D_MODEL))],
        out_specs=[tile(D_MODEL), tile(D_MODEL), _full_spec((1, 128)), _full_spec((1, D_MODEL))],
        compiler_params=pltpu.CompilerParams(vmem_limit_bytes=VMEM_LIMIT),
    )(au, av, sq, sk, sv, sk, sv, mq, z, mkv, x2, tgt2, v_g, v_b, w_sp, b_sp, sinks, bias, w_out, g_post)


def _backward_mix(parts, mkv, do, v_g, v_b, w_sp, b_sp, sinks, bias, w_out, n_ex, seq):
    n_tiles_ex = seq // TILE
    n_tok = n_ex * seq
    au, av, sq, sk, sv, mq, z = parts

    def body(do_ref, au_ref, av_ref, sq_ref, sk_ref, sv_ref, skp_ref, svp_ref, mq_ref, z_ref, mkv_ref,
             vg_ref, vb_ref, ws_ref, bs_ref, sink_ref, bias_ref, wout_ref,
             dau_ref, dav_ref, dsq_ref, dsk_ref, dsv_ref, dmq_ref, dz_ref, dmkv_ref,
             dwout_ref, dvg_ref, dvb_ref, dws_ref, dbs_ref, dsink_ref, drel_ref,
             carry_k, carry_v):
        b, i = pl.program_id(0), pl.program_id(1)

        @pl.when((b == 0) & (i == 0))
        def _():
            for ref in (dwout_ref, dvg_ref, dvb_ref, dws_ref, dbs_ref, dsink_ref, drel_ref):
                ref[...] = jnp.zeros_like(ref)

        @pl.when(i == 0)
        def _():
            dmkv_ref[...] = jnp.zeros_like(dmkv_ref)
            carry_k[...] = jnp.zeros_like(carry_k)
            carry_v[...] = jnp.zeros_like(carry_v)

        @pl.when(i < n_tiles_ex)
        def _():
            wm, bs_rows, sink_col, mk_pairs, mv_pairs = _tile_constants(ws_ref, bs_ref, sink_ref, mkv_ref)
            vg = vg_ref[...]
            do_b = do_ref[...].astype(BF16)
            dy = _mm_nt(do_b, wout_ref[...])
            zv = z_ref[...]
            sig = _sigmoid(zv)
            sz = zv * sig
            dyc = dy * sz

            au_v, av_v = au_ref[...], av_ref[...]
            ya, res = _group_a_forward(au_v, av_v, vg, vb_ref[...], wm, bs_rows)
            dgu, dgv = [], []
            for g in range(A_GROUPS):
                sl = slice(g * 128, (g + 1) * 128)
                xhat, rstd, vn, s = res["groups"][g]
                dya = dyc[:, sl]
                dgu.append(dya * s)
                ds = dya * res["gu"][:, sl]
                dbs_ref[:, sl] += sum(ds[c * CHUNK:(c + 1) * CHUNK] for c in range(TILE_CHUNKS))
                ds_b = _rows_to_lanes(ds.astype(BF16), TILE_CHUNKS)
                dws_ref[g] += _mm_nt(ds_b, vn)
                dvn = _lanes_to_rows(_mm_tn(wm[g], ds_b), TILE_CHUNKS)
                dvg_ref[:, sl] += jnp.sum(dvn * xhat, axis=0, keepdims=True)
                dvb_ref[:, sl] += jnp.sum(dvn, axis=0, keepdims=True)
                dxh = dvn * vg[:, sl]
                dgv.append(rstd * (dxh - jnp.mean(dxh, axis=-1, keepdims=True)
                                   - xhat * jnp.mean(dxh * xhat, axis=-1, keepdims=True)))
            dau_ref[...] = jnp.concatenate(dgu, axis=-1) * _gelu_grad(au_v, res["tu"])
            dav_ref[...] = jnp.concatenate(dgv, axis=-1) * _gelu_grad(av_v, res["tv"])

            lane4 = lax.broadcasted_iota(jnp.int32, (1, 128), 1)
            dsink_vec = jnp.zeros((1, 128), F32)
            yb, dk_parts, dv_parts = [], [], []
            for j in range(TILE_CHUNKS):
                rows, k_pairs, v_pairs, table = _load_chunk(j, i, sk_ref, sv_ref, skp_ref, svp_ref)
                qp = _halves_bf16(sq_ref[rows, :])
                p, ps = _attention_probs(qp, k_pairs, bias_ref[table], sink_col)
                out, pp = _attention_out(p, v_pairs, CHUNK)
                yb.append(out)
                do_pairs = _halves_bf16(dyc[rows, A_WIDTH:A_WIDTH + SWA_WIDTH])
                dl, delta, dq, dk, dv = _attention_backward(p, pp, do_pairs, qp, k_pairs, v_pairs, CHUNK)
                sink_terms = ps * delta
                for h in range(4):
                    dsink_vec = dsink_vec + jnp.where(lane4 == h, -jnp.sum(sink_terms[h * CHUNK:(h + 1) * CHUNK]), 0.0)
                drel_ref[...] += dl
                dsq_ref[rows, :] = dq * QK_SCALE
                dk_parts.append(_swa_unvariants(*_split_pair_grads(dk)) * QK_SCALE)
                dv_parts.append(_swa_unvariants(*_split_pair_grads(dv)))

            mqp = _halves_bf16(mq_ref[...])
            pm, _ = _attention_probs(mqp, mk_pairs, None, None)
            yc, ppm = _attention_out(pm, mv_pairs, TILE)
            dc_pairs = _halves_bf16(dyc[:, A_WIDTH + SWA_WIDTH:])
            _, _, dmq, dmk, dmv = _attention_backward(pm, ppm, dc_pairs, mqp, mk_pairs, mv_pairs, TILE)
            dmq_ref[...] = dmq * QK_SCALE
            dmkv_ref[0] += jnp.concatenate([_mem_unvariants(*_split_pair_grads(dmk)) * QK_SCALE,
                                            _mem_unvariants(*_split_pair_grads(dmv))], axis=-1)

            ycat = jnp.concatenate(ya + [jnp.concatenate(yb, axis=0), yc], axis=-1)
            dwout_ref[...] += _mm_tn((ycat * sz).astype(BF16), do_b)
            dz_ref[...] = dy * ycat * (sig * (1.0 + zv * (1.0 - sig)))
            dsink_ref[...] += dsink_vec

            for parts_c, carry, out_ref in ((dk_parts, carry_k, dsk_ref), (dv_parts, carry_v, dsv_ref)):
                @pl.when(i > 0)
                def _():
                    out_ref[...] = carry[...] + jnp.concatenate(
                        [jnp.zeros((TILE - CHUNK, KV_WIDTH), F32), parts_c[0][:CHUNK]], axis=0)
                new = [parts_c[0][CHUNK:]]
                for j in range(1, TILE_CHUNKS):
                    new[-1] = new[-1] + parts_c[j][:CHUNK]
                    new.append(parts_c[j][CHUNK:])
                carry[...] = jnp.concatenate(new, axis=0)

        @pl.when(i == n_tiles_ex)
        def _():
            dsk_ref[...] = carry_k[...]
            dsv_ref[...] = carry_v[...]

    tile = functools.partial(_tile_specs, n_tiles_ex)
    prev = functools.partial(_prev_chunk_spec, n_tiles_ex)
    late = pl.BlockSpec((TILE, KV_WIDTH), lambda b, i: (b * n_tiles_ex + jnp.maximum(i - 1, 0), 0))
    tok = lambda w: jax.ShapeDtypeStruct((n_tok, w), F32)
    return pl.pallas_call(
        body, name="backward_mix", grid=(n_ex, n_tiles_ex + 1),
        out_shape=[tok(A_WIDTH), tok(A_WIDTH), tok(SWA_WIDTH), tok(KV_WIDTH), tok(KV_WIDTH), tok(MEM_WIDTH),
                   tok(MIX_WIDTH), jax.ShapeDtypeStruct((n_ex, MEM_LEN, 2 * MEM_WIDTH), F32),
                   jax.ShapeDtypeStruct((MIX_WIDTH, D_MODEL), F32), jax.ShapeDtypeStruct((1, A_WIDTH), F32),
                   jax.ShapeDtypeStruct((1, A_WIDTH), F32), jax.ShapeDtypeStruct((A_GROUPS, CHUNK, CHUNK), F32),
                   jax.ShapeDtypeStruct((CHUNK, A_WIDTH), F32), jax.ShapeDtypeStruct((1, 128), F32),
                   jax.ShapeDtypeStruct((4 * CHUNK, 2 * CHUNK), F32)],
        in_specs=[tile(D_MODEL), tile(A_WIDTH), tile(A_WIDTH), tile(SWA_WIDTH), tile(KV_WIDTH), tile(KV_WIDTH),
                  prev(KV_WIDTH), prev(KV_WIDTH), tile(MEM_WIDTH), tile(MIX_WIDTH),
                  pl.BlockSpec((1, MEM_LEN, 2 * MEM_WIDTH), lambda b, i: (b, 0, 0)),
                  _full_spec((1, A_WIDTH)), _full_spec((1, A_WIDTH)), _full_spec((A_GROUPS, CHUNK, CHUNK)),
                  _full_spec((A_GROUPS, CHUNK, CHUNK)), SMEM_SPEC, _full_spec((2, 4 * CHUNK, 2 * CHUNK)),
                  _full_spec((MIX_WIDTH, D_MODEL))],
        out_specs=[tile(A_WIDTH), tile(A_WIDTH), tile(SWA_WIDTH), late, late, tile(MEM_WIDTH), tile(MIX_WIDTH),
                   pl.BlockSpec((1, MEM_LEN, 2 * MEM_WIDTH), lambda b, i: (b, 0, 0)),
                   _full_spec((MIX_WIDTH, D_MODEL)), _full_spec((1, A_WIDTH)), _full_spec((1, A_WIDTH)),
                   _full_spec((A_GROUPS, CHUNK, CHUNK)), _full_spec((CHUNK, A_WIDTH)), _full_spec((1, 128)),
                   _full_spec((4 * CHUNK, 2 * CHUNK))],
        scratch_shapes=[pltpu.VMEM((TILE, KV_WIDTH), F32), pltpu.VMEM((TILE, KV_WIDTH), F32)],
        compiler_params=pltpu.CompilerParams(vmem_limit_bytes=VMEM_LIMIT),
    )(do, au, av, sq, sk, sv, sk, sv, mq, z, mkv, v_g, v_b, w_sp, b_sp, sinks, bias, w_out)


def _backward_projection(x2, dout, dparts, g_pre, w_in_t):
    n_tok = x2.shape[0]
    n_steps = n_tok // PROJ_TILE

    def body(x_ref, dout_ref, dau, dav, dsq, dsk, dsv, dmq, dz, g_ref, w_hbm,
             dx_ref, dwin_hbm, dgpre_ref, w_vmem, acc, sem):
        step = pl.program_id(0)

        @pl.when(step == 0)
        def _():
            load = pltpu.make_async_copy(w_hbm, w_vmem, sem)
            load.start()
            acc[...] = jnp.zeros_like(acc)
            dgpre_ref[...] = jnp.zeros_like(dgpre_ref)
            load.wait()

        xv = x_ref[...]
        r = lax.rsqrt(jnp.mean(xv * xv, axis=-1, keepdims=True) + EPS)
        xn = xv * r
        g = g_ref[...]
        h_b = (xn * g).astype(BF16)
        dh = jnp.zeros((PROJ_TILE, D_MODEL), F32)
        for k, ref in enumerate((dau, dav, dsq, dsk, dsv, dmq, dz)):
            rows = slice(PROJ_OFFSETS[k], PROJ_OFFSETS[k + 1])
            dp = ref[...].astype(BF16)
            acc[rows, :] += _mm_tn(dp, h_b)
            dh = dh + _mm(dp, w_vmem[rows, :])
        dgpre_ref[...] += jnp.sum(dh * xn, axis=0, keepdims=True)
        dhg = dh * g
        dx_ref[...] = r * (dhg - xn * jnp.mean(dhg * xn, axis=-1, keepdims=True)) + dout_ref[...]

        @pl.when(step == n_steps - 1)
        def _():
            store = pltpu.make_async_copy(acc, dwin_hbm, sem)
            store.start()
            store.wait()

    row = lambda w: pl.BlockSpec((PROJ_TILE, w), lambda i: (i, 0))
    return pl.pallas_call(
        body, name="backward_projection", grid=(n_steps,),
        out_shape=[jax.ShapeDtypeStruct((n_tok, D_MODEL), F32), jax.ShapeDtypeStruct((IN_WIDTH, D_MODEL), F32),
                   jax.ShapeDtypeStruct((1, D_MODEL), F32)],
        in_specs=[row(D_MODEL), row(D_MODEL)] + [row(w) for w in PROJ_WIDTHS] + [_full_spec((1, D_MODEL)), ANY_SPEC],
        out_specs=[row(D_MODEL), ANY_SPEC, _full_spec((1, D_MODEL))],
        scratch_shapes=[pltpu.VMEM((IN_WIDTH, D_MODEL), BF16), pltpu.VMEM((IN_WIDTH, D_MODEL), F32),
                        pltpu.SemaphoreType.DMA],
        input_output_aliases={1: 0},
        compiler_params=pltpu.CompilerParams(vmem_limit_bytes=VMEM_LIMIT),
    )(x2, dout, *dparts, g_pre, w_in_t)


def _memkv_backward(mem, dmkv, g_mem, w_mkv):
    n_ex = mem.shape[0]

    def body(mem_ref, d_ref, g_ref, w_ref, dw_ref, dg_ref):
        @pl.when(pl.program_id(0) == 0)
        def _():
            dw_ref[...] = jnp.zeros_like(dw_ref)
            dg_ref[...] = jnp.zeros_like(dg_ref)

        m = mem_ref[0]
        mn = m * lax.rsqrt(jnp.mean(m * m, axis=-1, keepdims=True) + EPS)
        d_b = d_ref[0].astype(BF16)
        dw_ref[...] += _mm_tn((mn * g_ref[...]).astype(BF16), d_b)
        dg_ref[...] += jnp.sum(_mm_nt(d_b, w_ref[...]) * mn, axis=0, keepdims=True)

    return pl.pallas_call(
        body, name="memkv_backward", grid=(n_ex,),
        out_shape=[jax.ShapeDtypeStruct((D_MODEL, 2 * MEM_WIDTH), F32), jax.ShapeDtypeStruct((1, D_MODEL), F32)],
        in_specs=[pl.BlockSpec((1, MEM_LEN, D_MODEL), lambda b: (b, 0, 0)),
                  pl.BlockSpec((1, MEM_LEN, 2 * MEM_WIDTH), lambda b: (b, 0, 0)),
                  _full_spec((1, D_MODEL)), _full_spec((D_MODEL, 2 * MEM_WIDTH))],
        out_specs=[_full_spec((D_MODEL, 2 * MEM_WIDTH)), _full_spec((1, D_MODEL))],
    )(mem, dmkv, g_mem, w_mkv)


def _pack_small_grads(dgpre, dgpost, dgmem, dvg, dvb, dws, dbs, dsink, drel, buckets):
    def body(dgpre_ref, dgpost_ref, dgmem_ref, dvg_ref, dvb_ref, dws_ref, dbs_ref, dsink_ref, drel_ref, bk_ref,
             a_ref, b_ref):
        a_ref[...] = jnp.zeros_like(a_ref)
        b_ref[...] = jnp.zeros_like(b_ref)
        a_ref[0:1, :] = dgpre_ref[...]
        a_ref[1:2, :] = dgpost_ref[...]
        a_ref[2:3, :] = dgmem_ref[...]
        a_ref[3:4, :] = jnp.concatenate([dvg_ref[...], dvb_ref[...]], axis=-1)
        row = lax.broadcasted_iota(jnp.int32, (CHUNK, CHUNK), 0)
        col = lax.broadcasted_iota(jnp.int32, (CHUNK, CHUNK), 1)
        for g in range(A_GROUPS):
            b_ref[ROW_WS + g * CHUNK:ROW_WS + (g + 1) * CHUNK, :] = jnp.where(row >= col, dws_ref[g], 0.0)
            by_token = jnp.transpose(dbs_ref[:, g * 128:(g + 1) * 128])
            b_ref[ROW_BS + g:ROW_BS + g + 1, :] = jnp.sum(by_token, axis=0, keepdims=True)
        b_ref[ROW_SINK:ROW_SINK + 1, :] = dsink_ref[...]
        bk = bk_ref[...]
        rel_row = lax.broadcasted_iota(jnp.int32, (8, 128), 0)
        rel_col = lax.broadcasted_iota(jnp.int32, (8, 128), 1)
        rel = jnp.zeros((8, 128), F32)
        for h in range(4):
            acc = drel_ref[h * CHUNK:(h + 1) * CHUNK, :]
            for b in range(N_BUCKETS):
                rel = jnp.where((rel_row == h) & (rel_col == b), jnp.sum(jnp.where(bk == b, acc, 0.0)), rel)
        b_ref[ROW_REL:ROW_REL + 8, :] = rel

    return pl.pallas_call(
        body, name="pack_small_grads",
        out_shape=[jax.ShapeDtypeStruct((SMALL_A_ROWS, D_MODEL), F32), jax.ShapeDtypeStruct((SMALL_B_ROWS, 128), F32)],
        in_specs=[VMEM_SPEC] * 10, out_specs=[VMEM_SPEC] * 2,
    )(dgpre, dgpost, dgmem, dvg, dvb, dws, dbs, dsink, drel, buckets)


def _exchange_siblings(big, small_a, small_b):
    def body(g0, g1, g2, sa, sb, r0, r1, r2, ra, rb, send_sems, recv_sems):
        x, y, c = lax.axis_index("x"), lax.axis_index("y"), lax.axis_index("c")
        pairs = [(g.at[:, pl.ds(1 - c, 1)], r) for g, r in ((g0, r0), (g1, r1), (g2, r2))] + [(sa, ra), (sb, rb)]
        copies = [pltpu.make_async_remote_copy(src_ref=src, dst_ref=dst, send_sem=send_sems.at[k],
                                               recv_sem=recv_sems.at[k], device_id=(x, y, 1 - c),
                                               device_id_type=MESH)
                  for k, (src, dst) in enumerate(pairs)]
        for cp in copies:
            cp.start()
        for cp in copies:
            cp.wait_recv()
        for cp in copies:
            cp.wait_send()

    out_shape = [jax.ShapeDtypeStruct((g.shape[0], 1) + g.shape[2:], F32) for g in big]
    out_shape += [jax.ShapeDtypeStruct(small_a.shape, F32), jax.ShapeDtypeStruct(small_b.shape, F32)]
    return pl.pallas_call(
        body, name="exchange_siblings", out_shape=out_shape,
        in_specs=[ANY_SPEC] * 5, out_specs=[ANY_SPEC] * 5,
        scratch_shapes=[pltpu.SemaphoreType.DMA((5,)), pltpu.SemaphoreType.DMA((5,))],
    )(*big, small_a, small_b)


def _chip_sum(big, recv, small, small_recv, c_arr):
    def body(c_ref, g0, g1, g2, r0, r1, r2, sa, sb, ra, rb, p0, p1, p2, ca, cb):
        for g, r, p in ((g0, r0, p0), (g1, r1, p1), (g2, r2, p2)):
            p[...] = (g[...] + r[...]).astype(BF16)

        @pl.when(pl.program_id(0) == 0)
        def _():
            ca[...] = sa[...] + ra[...]
            cb[...] = sb[...] + rb[...]

    def own(g):
        return pl.BlockSpec((1, 1) + g.shape[2:], lambda j, c_ref: (j, c_ref[0], 0, 0))

    def got(g):
        return pl.BlockSpec((1, 1) + g.shape[2:], lambda j, c_ref: (j, 0, 0, 0))

    def whole(a):
        return pl.BlockSpec(a.shape, lambda j, c_ref: (0, 0))

    return pl.pallas_call(
        body, name="chip_sum",
        out_shape=[jax.ShapeDtypeStruct(r.shape, BF16) for r in recv]
        + [jax.ShapeDtypeStruct(a.shape, F32) for a in small],
        grid_spec=pltpu.PrefetchScalarGridSpec(
            num_scalar_prefetch=1, grid=(N_CHIPS,),
            in_specs=[own(g) for g in big] + [got(g) for g in big] + [whole(a) for a in small + small_recv],
            out_specs=[got(g) for g in big] + [whole(a) for a in small]),
        compiler_params=pltpu.CompilerParams(vmem_limit_bytes=VMEM_LIMIT),
    )(c_arr, *big, *recv, *small, *small_recv)


def _exchange_chips(partials, small):
    def body(p0, p1, p2, ca, cb, r0, r1, r2, ga, gb, send_sems, recv_sems, local_sems):
        x, y, c = lax.axis_index("x"), lax.axis_index("y"), lax.axis_index("c")
        chips = [(1 - x, y), (x, 1 - y), (1 - x, 1 - y)]
        my_chip = 2 * x + y
        own = [pltpu.make_async_copy(ca, ga.at[my_chip], local_sems.at[0]),
               pltpu.make_async_copy(cb, gb.at[my_chip], local_sems.at[1])]
        for cp in own:
            cp.start()
        copies = []
        for j, chip in enumerate(chips):
            pairs = [(p.at[2 * chip[0] + chip[1]], r.at[j]) for p, r in ((p0, r0), (p1, r1), (p2, r2))]
            pairs += [(ca, ga.at[my_chip]), (cb, gb.at[my_chip])]
            for w, (src, dst) in enumerate(pairs):
                copies.append(pltpu.make_async_remote_copy(
                    src_ref=src, dst_ref=dst, send_sem=send_sems.at[5 * j + w], recv_sem=recv_sems.at[5 * j + w],
                    device_id=(*chip, c), device_id_type=MESH))
        for cp in copies:
            cp.start()
        for cp in copies:
            cp.wait_recv()
        for cp in copies:
            cp.wait_send()
        for cp in own:
            cp.wait()

    return pl.pallas_call(
        body, name="exchange_chips",
        out_shape=[jax.ShapeDtypeStruct((3,) + p.shape[1:], BF16) for p in partials]
        + [jax.ShapeDtypeStruct((N_CHIPS,) + a.shape, F32) for a in small],
        in_specs=[ANY_SPEC] * 5, out_specs=[ANY_SPEC] * 5,
        scratch_shapes=[pltpu.SemaphoreType.DMA((15,)), pltpu.SemaphoreType.DMA((15,)),
                        pltpu.SemaphoreType.DMA((2,))],
    )(*partials, *small)


def _shard_sum(partials, recv, shard_arr):
    def body(s_ref, p0, p1, p2, r0, r1, r2, o0, o1, o2):
        for p, r, o in ((p0, r0, o0), (p1, r1, o1), (p2, r2, o2)):
            o[...] = ((p[0, 0].astype(F32) + r[0, 0].astype(F32)) + r[1, 0].astype(F32)) + r[2, 0].astype(F32)

    def own(p):
        return pl.BlockSpec((1,) + p.shape[1:], lambda i, s_ref: (s_ref[0], 0, 0, 0))

    def got(p):
        return pl.BlockSpec((3,) + p.shape[1:], lambda i, s_ref: (0, 0, 0, 0))

    return pl.pallas_call(
        body, name="shard_sum",
        out_shape=[jax.ShapeDtypeStruct(p.shape[2:], F32) for p in partials],
        grid_spec=pltpu.PrefetchScalarGridSpec(
            num_scalar_prefetch=1, grid=(1,),
            in_specs=[own(p) for p in partials] + [got(p) for p in partials],
            out_specs=[pl.BlockSpec(p.shape[2:], lambda i, s_ref: (0, 0)) for p in partials]),
        compiler_params=pltpu.CompilerParams(vmem_limit_bytes=VMEM_LIMIT),
    )(shard_arr, *partials, *recv)


def _exchange_halves(halves):
    def body(h0, h1, h2, r0, r1, r2, send_sems, recv_sems):
        x, y, c = lax.axis_index("x"), lax.axis_index("y"), lax.axis_index("c")
        copies = [pltpu.make_async_remote_copy(src_ref=h, dst_ref=r, send_sem=send_sems.at[w],
                                               recv_sem=recv_sems.at[w], device_id=(x, y, 1 - c),
                                               device_id_type=MESH)
                  for w, (h, r) in enumerate(((h0, r0), (h1, r1), (h2, r2)))]
        for cp in copies:
            cp.start()
        for cp in copies:
            cp.wait_recv()
        for cp in copies:
            cp.wait_send()

    return pl.pallas_call(
        body, name="exchange_halves",
        out_shape=[jax.ShapeDtypeStruct(h.shape, F32) for h in halves],
        in_specs=[ANY_SPEC] * 3, out_specs=[ANY_SPEC] * 3,
        scratch_shapes=[pltpu.SemaphoreType.DMA((3,)), pltpu.SemaphoreType.DMA((3,))],
    )(*halves)


def _adamw(w, g, m, v):
    m2 = ADAM_B1 * m + (1.0 - ADAM_B1) * g
    v2 = ADAM_B2 * v + (1.0 - ADAM_B2) * (g * g)
    m_hat = m2 / (1.0 - ADAM_B1 ** ADAM_STEP)
    v_hat = v2 / (1.0 - ADAM_B2 ** ADAM_STEP)
    delta = -ADAM_LR * (m_hat / (jnp.sqrt(v_hat) + ADAM_EPS) + ADAM_WD * w)
    return delta, m2, v2


ADAM_MAX_ROWS = 176


def _adamw_sharded(mine, other, w, m, v, c_arr, name):
    rows, cols = w.shape
    half = rows // 2
    steps = -(-half // ADAM_MAX_ROWS)
    block_rows = half // steps
    assert block_rows * steps == half and block_rows % 8 == 0

    def body(c_ref, mine_ref, other_ref, w_ref, m_ref, v_ref, g_out, d_out, m_out, v_out):
        g = jnp.where(pl.program_id(0) == c_ref[0], mine_ref[...], other_ref[...])
        delta, m2, v2 = _adamw(w_ref[...], g, m_ref[...], v_ref[...])
        g_out[...] = g
        d_out[...] = delta
        m_out[...] = m2
        v_out[...] = v2

    part = pl.BlockSpec((block_rows, cols), lambda h, k, c_ref: (k, 0))
    full = pl.BlockSpec((block_rows, cols), lambda h, k, c_ref: (h * steps + k, 0))
    return pl.pallas_call(
        body, name=name, out_shape=[jax.ShapeDtypeStruct((rows, cols), F32)] * 4,
        grid_spec=pltpu.PrefetchScalarGridSpec(
            num_scalar_prefetch=1, grid=(2, steps), in_specs=[part, part, full, full, full], out_specs=[full] * 4),
    )(c_arr, mine, other, w, m, v)


def _adamw_small(ra, rb, weights, moments_m, moments_v):
    n = len(weights)

    def body(*refs):
        ra_ref, rb_ref = refs[0], refs[1]
        w_refs, m_refs, v_refs = refs[2:2 + n], refs[2 + n:2 + 2 * n], refs[2 + 2 * n:2 + 3 * n]
        outs = refs[2 + 3 * n:]
        g_outs, d_outs, m_outs, v_outs = outs[:n], outs[n:2 * n], outs[2 * n:3 * n], outs[3 * n:]
        ga, gb = ra_ref[0], rb_ref[0]
        for chip in range(1, N_CHIPS):
            ga = ga + ra_ref[chip]
            gb = gb + rb_ref[chip]
        grads = [ga[0:1, :], ga[1:2, :], ga[2:3, :], ga[3:4, :A_WIDTH], ga[3:4, A_WIDTH:],
                 gb[ROW_WS:ROW_WS + A_GROUPS * CHUNK, :].reshape(A_GROUPS, CHUNK, CHUNK),
                 gb[ROW_BS:ROW_BS + A_GROUPS, :], gb[ROW_SINK:ROW_SINK + 1, 0:4],
                 gb[ROW_REL:ROW_REL + 4, 0:N_BUCKETS]]
        for k in range(n):
            delta, m2, v2 = _adamw(w_refs[k][...], grads[k], m_refs[k][...], v_refs[k][...])
            g_outs[k][...] = grads[k]
            d_outs[k][...] = delta
            m_outs[k][...] = m2
            v_outs[k][...] = v2

    out_shape = [jax.ShapeDtypeStruct(w.shape, F32) for w in weights] * 4
    return pl.pallas_call(
        body, name="adamw_small", out_shape=out_shape,
        in_specs=[VMEM_SPEC] * (2 + 3 * n), out_specs=[VMEM_SPEC] * (4 * n),
    )(ra, rb, *weights, *moments_m, *moments_v)


def kernel(x, mem, pre_norm_g, post_norm_g, mem_norm_g, w_in, w_mem_kv, v_norm_g, v_norm_b, w_spatial, b_spatial, attn_sinks, rel_bias, w_out, loss_target, m_pre_norm_g, m_post_norm_g, m_mem_norm_g, m_w_in, m_w_mem_kv, m_v_norm_g, m_v_norm_b, m_w_spatial, m_b_spatial, m_attn_sinks, m_rel_bias, m_w_out, v_pre_norm_g, v_post_norm_g, v_mem_norm_g, v_w_in, v_w_mem_kv, v_v_norm_g, v_v_norm_b, v_w_spatial, v_b_spatial, v_attn_sinks, v_rel_bias, v_w_out):
    n_ex, seq, _ = x.shape
    n_tok = n_ex * seq
    x2 = x.reshape(n_tok, D_MODEL)
    tgt2 = loss_target.reshape(n_tok, D_MODEL)
    buckets = jnp.asarray(_bucket_map())
    c_arr = lax.axis_index("c").astype(jnp.int32).reshape(1)
    shard_arr = (2 * lax.axis_index("x") + lax.axis_index("y")).astype(jnp.int32).reshape(1)
    w_sp = w_spatial[0]
    b_sp = jnp.broadcast_to(b_spatial[0][:, :, None], (A_GROUPS, CHUNK, CHUNK))
    w_in_t, m_w_in_t, v_w_in_t = (jnp.transpose(a[0]) for a in (w_in, m_w_in, v_w_in))
    rel_t, m_rel_t, v_rel_t = (jnp.transpose(a) for a in (rel_bias, m_rel_bias, v_rel_bias))

    g_in, g_mkv, g_out = _gather_weights(w_in_t, w_mem_kv[0], w_out[0])
    w_in_b = g_in.reshape(IN_WIDTH, D_MODEL)
    w_mkv_b = g_mkv.reshape(D_MODEL, 2 * MEM_WIDTH)
    w_out_b = g_out.reshape(MIX_WIDTH, D_MODEL)

    bias = _make_bias(rel_t, buckets)
    mkv = _memkv_forward(mem, mem_norm_g, w_mkv_b)
    parts = _forward_projection(x2, pre_norm_g, w_in_b)
    dout, do, loss_vec, dgpost = _forward_mix(parts, mkv, x2, tgt2, v_norm_g, v_norm_b, w_sp, b_sp, attn_sinks, bias,
                                             w_out_b, post_norm_g, n_ex, seq)

    (dau, dav, dsq, dsk, dsv, dmq, dz, dmkv, dwout, dvg, dvb, dws, dbs, dsink, drel) = _backward_mix(
        parts, mkv, do, v_norm_g, v_norm_b, w_sp, b_sp, attn_sinks, bias, w_out_b, n_ex, seq)
    dx, dwin, dgpre = _backward_projection(x2, dout, (dau, dav, dsq, dsk, dsv, dmq, dz), pre_norm_g, w_in_b)
    dwmkv, dgmem = _memkv_backward(mem, dmkv, mem_norm_g, w_mkv_b)
    small_a, small_b = _pack_small_grads(dgpre, dgpost, dgmem, dvg, dvb, dws, dbs, dsink, drel, buckets)

    shard_shapes = [w_in_t.shape, w_mem_kv.shape[1:], w_out.shape[1:]]
    big = [g.reshape(N_CHIPS, 2, s[0] // 2, s[1]) for g, s in zip((dwin, dwmkv, dwout), shard_shapes)]
    *recv, ra, rb = _exchange_siblings(big, small_a, small_b)
    *partials, ca, cb = _chip_sum(big, recv, [small_a, small_b], [ra, rb], c_arr)
    *recv2, ga, gb = _exchange_chips(partials, [ca, cb])
    mine = _shard_sum(partials, recv2, shard_arr)
    other = _exchange_halves(mine)

    big_w = [(w_in_t, m_w_in_t, v_w_in_t), (w_mem_kv[0], m_w_mem_kv[0], v_w_mem_kv[0]),
             (w_out[0], m_w_out[0], v_w_out[0])]
    big_names = ["adamw_w_in", "adamw_w_mem_kv", "adamw_w_out"]
    big_out = [_adamw_sharded(mine[k], other[k], *big_w[k], c_arr, big_names[k]) for k in range(3)]
    small_w = [pre_norm_g, post_norm_g, mem_norm_g, v_norm_g, v_norm_b, w_sp, b_spatial[0], attn_sinks, rel_t]
    small_m = [m_pre_norm_g, m_post_norm_g, m_mem_norm_g, m_v_norm_g, m_v_norm_b, m_w_spatial[0], m_b_spatial[0],
               m_attn_sinks, m_rel_t]
    small_v = [v_pre_norm_g, v_post_norm_g, v_mem_norm_g, v_v_norm_g, v_v_norm_b, v_w_spatial[0], v_b_spatial[0],
               v_attn_sinks, v_rel_t]
    small_out = _adamw_small(ga, gb, small_w, small_m, small_v)
    n_small = len(small_w)

    loss = lax.psum(loss_vec[0, 0], ALL_AXES)
    outputs = [loss, dx.reshape(x.shape)]
    for kind in range(4):
        s = small_out[kind * n_small:(kind + 1) * n_small]
        outputs += [s[0], s[1], s[2], jnp.transpose(big_out[0][kind])[None], big_out[1][kind][None], s[3], s[4],
                    s[5][None], s[6][None], s[7], jnp.transpose(s[8]), big_out[2][kind][None]]
    return tuple(outputs)
```

```python
import functools

import numpy as np
import jax
import jax.numpy as jnp
from jax import lax
from jax.experimental import pallas as pl
from jax.experimental.pallas import tpu as pltpu

F32 = jnp.float32
BF16 = jnp.bfloat16
MESH = pl.DeviceIdType.MESH
ALL_AXES = ("x", "y", "c")

D_MODEL = 1024
CHUNK = 128
A_WIDTH = 512
A_GROUPS = 4
SWA_WIDTH = 256
KV_WIDTH = 128
MEM_WIDTH = 256
MEM_LEN = 256
MIX_WIDTH = 1024
IN_WIDTH = 2816
N_BUCKETS = 32
MAX_DISTANCE = 128
EPS = 1e-6
NEG = -1e30
QK_SCALE = 0.125
HALF_HEAD_PAIR = 64

ADAM_LR = 0.001
ADAM_B1 = 0.9
ADAM_B2 = 0.999
ADAM_EPS = 1e-08
ADAM_WD = 0.01
ADAM_STEP = 10

N_CHIPS = 4
N_DEV = 8
TILE_CHUNKS = 2
TILE = TILE_CHUNKS * CHUNK
PROJ_TILE = 256
VMEM_LIMIT = 56 * 1024 * 1024

SMALL_A_ROWS = 8
ROW_WS = 0
ROW_BS = 512
ROW_SINK = 520
ROW_REL = 528
SMALL_B_ROWS = 536


def _mm(a, b):
    return lax.dot_general(a, b, (((1,), (0,)), ((), ())), preferred_element_type=F32)


def _mm_nt(a, b):
    return lax.dot_general(a, b, (((1,), (1,)), ((), ())), preferred_element_type=F32)


def _mm_tn(a, b):
    return lax.dot_general(a, b, (((0,), (0,)), ((), ())), preferred_element_type=F32)


def _bucket_map():
    qi = np.arange(CHUNK)[:, None]
    kj = np.arange(2 * CHUNK)[None, :]
    n = np.maximum(qi + CHUNK - kj, 0)
    max_exact = N_BUCKETS // 2
    large = max_exact + (np.log(np.maximum(n, 1) / max_exact) / np.log(MAX_DISTANCE / max_exact)
                         * (N_BUCKETS - max_exact)).astype(np.int32)
    large = np.minimum(large, N_BUCKETS - 1)
    return np.where(n < max_exact, n, large).astype(np.int32)


_GELU_C = 0.7978845608028654
_GELU_A = 0.044715


def _gelu(x):
    t = jnp.tanh(_GELU_C * (x + _GELU_A * x * x * x))
    return 0.5 * x * (1.0 + t), t


def _gelu_grad(x, t):
    return 0.5 * (1.0 + t) + 0.5 * x * (1.0 - t * t) * (_GELU_C * (1.0 + 3.0 * _GELU_A * x * x))


def _sigmoid(x):
    return 1.0 / (1.0 + jnp.exp(-x))


def _lane_lo(shape):
    return lax.broadcasted_iota(jnp.int32, shape, 1) < HALF_HEAD_PAIR


def _swa_variants(t):
    lo = _lane_lo(t.shape)
    tr = pltpu.roll(t, HALF_HEAD_PAIR, 1)
    zero = jnp.zeros_like(t)
    return (jnp.where(lo, t, zero).astype(BF16), jnp.where(lo, zero, tr).astype(BF16),
            jnp.where(lo, tr, zero).astype(BF16), jnp.where(lo, zero, t).astype(BF16))


def _swa_unvariants(d0, d1, d2, d3):
    lo = _lane_lo(d0.shape)
    zero = jnp.zeros_like(d0)
    rolled = jnp.where(lo, zero, d1) + jnp.where(lo, d2, zero)
    return jnp.where(lo, d0, zero) + jnp.where(lo, zero, d3) + pltpu.roll(rolled, HALF_HEAD_PAIR, 1)


def _mem_variants(t):
    out = []
    for pair in range(2):
        tp = t[:, pair * 128:(pair + 1) * 128]
        lo = _lane_lo(tp.shape)
        zero = jnp.zeros_like(tp)
        out.append(jnp.where(lo, tp, zero).astype(BF16))
        out.append(jnp.where(lo, zero, tp).astype(BF16))
    return out


def _mem_unvariants(d0, d1, d2, d3):
    lo = _lane_lo(d0.shape)
    return jnp.concatenate([jnp.where(lo, d0, d1), jnp.where(lo, d2, d3)], axis=-1)


def _softmax(logits, sinks):
    m = jnp.max(logits, axis=-1, keepdims=True)
    if sinks is not None:
        m = jnp.maximum(m, sinks)
    p = jnp.exp(logits - m)
    den = jnp.sum(p, axis=-1, keepdims=True)
    if sinks is None:
        return p * (1.0 / den), None
    es = jnp.exp(sinks - m)
    inv = 1.0 / (den + es)
    return p * inv, es * inv


def _band_valid(with_prev):
    qi = lax.broadcasted_iota(jnp.int32, (CHUNK, 2 * CHUNK), 0)
    kj = lax.broadcasted_iota(jnp.int32, (CHUNK, 2 * CHUNK), 1)
    in_cur = (kj >= CHUNK) & (kj - CHUNK <= qi)
    if not with_prev:
        return in_cur
    return in_cur | ((kj < CHUNK) & (kj > qi))


def _causal_weights(ws_ref):
    row = lax.broadcasted_iota(jnp.int32, (CHUNK, CHUNK), 0)
    col = lax.broadcasted_iota(jnp.int32, (CHUNK, CHUNK), 1)
    return [jnp.where(row >= col, ws_ref[g], 0.0).astype(BF16) for g in range(A_GROUPS)]


def _rows_to_lanes(a, n):
    return jnp.concatenate([a[c * CHUNK:(c + 1) * CHUNK] for c in range(n)], axis=1)


def _lanes_to_rows(a, n):
    w = a.shape[1] // n
    return jnp.concatenate([a[:, c * w:(c + 1) * w] for c in range(n)], axis=0)


def _stack_heads(pair01, pair23):
    return jnp.concatenate([pair01[:, :256], pair01[:, 256:], pair23[:, :256], pair23[:, 256:]], axis=0)


def _pair_heads(s, r):
    return (jnp.concatenate([s[0:r], s[r:2 * r]], axis=1), jnp.concatenate([s[2 * r:3 * r], s[3 * r:4 * r]], axis=1))


def _pair_operands(variants):
    return (jnp.concatenate(variants[0:2], axis=0), jnp.concatenate(variants[2:4], axis=0))


def _split_pair_grads(d_pairs):
    return d_pairs[0][:256], d_pairs[0][256:], d_pairs[1][:256], d_pairs[1][256:]


def _halves_bf16(a):
    return (a[:, :128].astype(BF16), a[:, 128:].astype(BF16))


def _group_a_forward(au, av, vg, vb, wm, bs_rows):
    gu, tu = _gelu(au)
    gv, tv = _gelu(av)
    ya, res = [], []
    for g in range(A_GROUPS):
        sl = slice(g * 128, (g + 1) * 128)
        xg = gv[:, sl]
        xc = xg - jnp.mean(xg, axis=-1, keepdims=True)
        rstd = lax.rsqrt(jnp.mean(xc * xc, axis=-1, keepdims=True) + EPS)
        xhat = xc * rstd
        vn = _rows_to_lanes((xhat * vg[:, sl] + vb[:, sl]).astype(BF16), TILE_CHUNKS)
        s = _lanes_to_rows(_mm(wm[g], vn), TILE_CHUNKS) + bs_rows[g]
        ya.append(gu[:, sl] * s)
        res.append((xhat, rstd, vn, s))
    return ya, dict(gu=gu, tu=tu, tv=tv, groups=res)


def _attention_probs(qp, k_pairs, bias, sink_col):
    logits = _stack_heads(_mm_nt(qp[0], k_pairs[0]), _mm_nt(qp[1], k_pairs[1])) * QK_SCALE
    if bias is not None:
        logits = logits + bias
    return _softmax(logits, sink_col)


def _attention_out(p, v_pairs, r):
    pp = _pair_heads(p.astype(BF16), r)
    return jnp.concatenate([_mm(pp[0], v_pairs[0]), _mm(pp[1], v_pairs[1])], axis=-1), pp


def _attention_backward(p, pp, do_pairs, qp, k_pairs, v_pairs, r):
    dp = _stack_heads(_mm_nt(do_pairs[0], v_pairs[0]), _mm_nt(do_pairs[1], v_pairs[1]))
    delta = jnp.sum(p * dp, axis=-1, keepdims=True)
    dl = p * (dp - delta)
    dlp = _pair_heads(dl.astype(BF16), r)
    dq = jnp.concatenate([_mm(dlp[0], k_pairs[0]), _mm(dlp[1], k_pairs[1])], axis=-1)
    dk = (_mm_tn(dlp[0], qp[0]), _mm_tn(dlp[1], qp[1]))
    dv = (_mm_tn(pp[0], do_pairs[0]), _mm_tn(pp[1], do_pairs[1]))
    return dl, delta, dq, dk, dv


def _tile_specs(n_tiles_ex, width):
    return pl.BlockSpec((TILE, width), lambda b, i: (b * n_tiles_ex + jnp.minimum(i, n_tiles_ex - 1), 0))


def _prev_chunk_spec(n_tiles_ex, width):
    def index(b, i):
        chunk = TILE_CHUNKS * jnp.minimum(i, n_tiles_ex - 1)
        return (b * n_tiles_ex * TILE_CHUNKS + jnp.maximum(chunk - 1, 0), 0)
    return pl.BlockSpec((CHUNK, width), index)


def _full_spec(shape):
    zeros = (0,) * len(shape)
    return pl.BlockSpec(shape, lambda *_: zeros)


SMEM_SPEC = pl.BlockSpec(memory_space=pltpu.SMEM)
ANY_SPEC = pl.BlockSpec(memory_space=pl.ANY)
VMEM_SPEC = pl.BlockSpec(memory_space=pltpu.VMEM)


def _make_bias(rel_bias_t, buckets):
    def body(rel_ref, bk_ref, out_ref):
        bk = bk_ref[...]
        for h in range(4):
            acc = jnp.zeros((CHUNK, 2 * CHUNK), F32)
            for b in range(N_BUCKETS):
                acc = jnp.where(bk == b, rel_ref[h, b], acc)
            for t, with_prev in enumerate((True, False)):
                out_ref[t, h * CHUNK:(h + 1) * CHUNK, :] = jnp.where(_band_valid(with_prev), acc, NEG)

    return pl.pallas_call(
        body, name="make_bias", out_shape=jax.ShapeDtypeStruct((2, 4 * CHUNK, 2 * CHUNK), F32),
        in_specs=[SMEM_SPEC, VMEM_SPEC], out_specs=VMEM_SPEC,
    )(rel_bias_t, buckets)


def _gather_weights(w_in_s, w_mkv_s, w_out_s):
    shapes = [w_in_s.shape, w_mkv_s.shape, w_out_s.shape]
    n_w = len(shapes)

    def body(win_ref, wmkv_ref, wout_ref, gin_ref, gmkv_ref, gout_ref, send_sems, recv_sems):
        x, y, c = lax.axis_index("x"), lax.axis_index("y"), lax.axis_index("c")
        me, sibling = (x, y, c), (x, y, 1 - c)
        chips = [(1 - x, y), (x, 1 - y), (1 - x, 1 - y)]
        ins = [win_ref, wmkv_ref, wout_ref]
        outs = [gin_ref, gmkv_ref, gout_ref]
        my_shard = 2 * x + y
        for w in range(n_w):
            outs[w][my_shard] = ins[w][...].astype(BF16)

        def copy(k, w, shard, half, to):
            rows = shapes[w][0] // 2
            ref = outs[w].at[shard, pl.ds(half * rows, rows), :]
            return pltpu.make_async_remote_copy(src_ref=ref, dst_ref=ref, send_sem=send_sems.at[k],
                                                recv_sem=recv_sems.at[k], device_id=to, device_id_type=MESH)

        pairs = [(w, j) for w in range(n_w) for j in range(3)]
        first = [copy(3 * w + j, w, my_shard, c, (*chips[j], c)) for w, j in pairs]
        for cp in first:
            cp.start()
        passed = []
        for w, j in pairs:
            shard = 2 * chips[j][0] + chips[j][1]
            copy(3 * w + j, w, shard, c, me).wait_recv()
            fwd = copy(9 + 3 * w + j, w, shard, c, sibling)
            fwd.start()
            passed.append(fwd)
        for w, j in pairs:
            shard = 2 * chips[j][0] + chips[j][1]
            copy(9 + 3 * w + j, w, shard, 1 - c, me).wait_recv()
        for cp in first + passed:
            cp.wait_send()

    return pl.pallas_call(
        body, name="gather_weights",
        out_shape=[jax.ShapeDtypeStruct((N_CHIPS,) + s, BF16) for s in shapes],
        in_specs=[VMEM_SPEC] * 3, out_specs=[VMEM_SPEC] * 3,
        scratch_shapes=[pltpu.SemaphoreType.DMA((18,)), pltpu.SemaphoreType.DMA((18,))],
        compiler_params=pltpu.CompilerParams(vmem_limit_bytes=VMEM_LIMIT),
    )(w_in_s, w_mkv_s, w_out_s)


def _memkv_forward(mem, g_mem, w_mkv):
    n_ex = mem.shape[0]

    def body(mem_ref, g_ref, w_ref, out_ref):
        m = mem_ref[0]
        r = lax.rsqrt(jnp.mean(m * m, axis=-1, keepdims=True) + EPS)
        out_ref[0] = _mm((m * r * g_ref[...]).astype(BF16), w_ref[...])

    return pl.pallas_call(
        body, name="memkv_forward", grid=(n_ex,),
        out_shape=jax.ShapeDtypeStruct((n_ex, MEM_LEN, 2 * MEM_WIDTH), F32),
        in_specs=[pl.BlockSpec((1, MEM_LEN, D_MODEL), lambda b: (b, 0, 0)), _full_spec((1, D_MODEL)),
                  _full_spec((D_MODEL, 2 * MEM_WIDTH))],
        out_specs=pl.BlockSpec((1, MEM_LEN, 2 * MEM_WIDTH), lambda b: (b, 0, 0)),
    )(mem, g_mem, w_mkv)


PROJ_WIDTHS = (A_WIDTH, A_WIDTH, SWA_WIDTH, KV_WIDTH, KV_WIDTH, MEM_WIDTH, MIX_WIDTH)
PROJ_OFFSETS = tuple(int(v) for v in np.cumsum((0,) + PROJ_WIDTHS))


def _forward_projection(x2, g_pre, w_in_t):
    n_tok = x2.shape[0]

    def body(x_ref, g_ref, w_ref, h_ref, *out_refs):
        xv = x_ref[...]
        r = lax.rsqrt(jnp.mean(xv * xv, axis=-1, keepdims=True) + EPS)
        h = (xv * r * g_ref[...]).astype(BF16)
        h_ref[...] = h
        proj = _mm_nt(h, w_ref[...])
        for k, ref in enumerate(out_refs):
            ref[...] = proj[:, PROJ_OFFSETS[k]:PROJ_OFFSETS[k + 1]]

    widths = (D_MODEL,) + PROJ_WIDTHS
    h, *parts = pl.pallas_call(
        body, name="forward_projection", grid=(n_tok // PROJ_TILE,),
        out_shape=[jax.ShapeDtypeStruct((n_tok, D_MODEL), BF16)]
        + [jax.ShapeDtypeStruct((n_tok, w), F32) for w in PROJ_WIDTHS],
        in_specs=[pl.BlockSpec((PROJ_TILE, D_MODEL), lambda i: (i, 0)), _full_spec((1, D_MODEL)),
                  _full_spec((IN_WIDTH, D_MODEL))],
        out_specs=[pl.BlockSpec((PROJ_TILE, w), lambda i: (i, 0)) for w in widths],
        compiler_params=pltpu.CompilerParams(vmem_limit_bytes=VMEM_LIMIT),
    )(x2, g_pre, w_in_t)
    return h, parts


def _load_chunk(j, i, sk_ref, sv_ref, skp_ref, svp_ref):
    rows = slice(j * CHUNK, (j + 1) * CHUNK)
    if j == 0:
        k_prev, v_prev, table = skp_ref[...], svp_ref[...], jnp.where(i > 0, 0, 1)
    else:
        prev = slice((j - 1) * CHUNK, j * CHUNK)
        k_prev, v_prev, table = sk_ref[prev, :], sv_ref[prev, :], 0
    k_pairs = _pair_operands(_swa_variants(jnp.concatenate([k_prev, sk_ref[rows, :]], axis=0)))
    v_pairs = _pair_operands(_swa_variants(jnp.concatenate([v_prev, sv_ref[rows, :]], axis=0)))
    return rows, k_pairs, v_pairs, table


def _tile_constants(ws_ref, bs_ref, sink_ref, mkv_ref):
    wm = _causal_weights(ws_ref)
    bs_rows = [jnp.concatenate([bs_ref[g]] * TILE_CHUNKS, axis=0) for g in range(A_GROUPS)]
    sink_col = jnp.max(jnp.concatenate([jnp.full((CHUNK, 128), sink_ref[0, h], F32) for h in range(4)], axis=0),
                       axis=-1, keepdims=True)
    mkv_v = mkv_ref[0]
    mk_pairs = _pair_operands(_mem_variants(mkv_v[:, :MEM_WIDTH]))
    mv_pairs = _pair_operands(_mem_variants(mkv_v[:, MEM_WIDTH:]))
    return wm, bs_rows, sink_col, mk_pairs, mv_pairs


def _forward_mix(parts, mkv, x2, tgt2, v_g, v_b, w_sp, b_sp, sinks, bias, w_out, g_post, n_ex, seq):
    n_tiles_ex = seq // TILE
    n_tok = n_ex * seq
    au, av, sq, sk, sv, mq, z = parts

    def body(au_ref, av_ref, sq_ref, sk_ref, sv_ref, skp_ref, svp_ref, mq_ref, z_ref, mkv_ref, x_ref, tgt_ref,
             vg_ref, vb_ref, ws_ref, bs_ref, sink_ref, bias_ref, wout_ref, gpost_ref,
             dout_ref, do_ref, loss_ref, dgpost_ref):
        b, i = pl.program_id(0), pl.program_id(1)

        @pl.when((b == 0) & (i == 0))
        def _():
            loss_ref[...] = jnp.zeros_like(loss_ref)
            dgpost_ref[...] = jnp.zeros_like(dgpost_ref)

        wm, bs_rows, sink_col, mk_pairs, mv_pairs = _tile_constants(ws_ref, bs_ref, sink_ref, mkv_ref)
        ya, _ = _group_a_forward(au_ref[...], av_ref[...], vg_ref[...], vb_ref[...], wm, bs_rows)
        yb = []
        for j in range(TILE_CHUNKS):
            rows, k_pairs, v_pairs, table = _load_chunk(j, i, sk_ref, sv_ref, skp_ref, svp_ref)
            p, _ = _attention_probs(_halves_bf16(sq_ref[rows, :]), k_pairs, bias_ref[table], sink_col)
            yb.append(_attention_out(p, v_pairs, CHUNK)[0])
        pm, _ = _attention_probs(_halves_bf16(mq_ref[...]), mk_pairs, None, None)
        yc = _attention_out(pm, mv_pairs, TILE)[0]
        ycat = jnp.concatenate(ya + [jnp.concatenate(yb, axis=0), yc], axis=-1)
        zv = z_ref[...]
        y = ycat * (zv * _sigmoid(zv))
        o = _mm(y.astype(BF16), wout_ref[...])
        r2 = lax.rsqrt(jnp.mean(o * o, axis=-1, keepdims=True) + EPS)
        nrm = o * r2
        gp = gpost_ref[...]
        diff = x_ref[...] + nrm * gp - tgt_ref[...]
        loss_ref[...] += jnp.sum(diff * diff) * (0.5 / D_MODEL)
        dout = diff * (1.0 / D_MODEL)
        dout_ref[...] = dout
        dgpost_ref[...] += jnp.sum(dout * nrm, axis=0, keepdims=True)
        dn = dout * gp
        do_ref[...] = r2 * (dn - nrm * jnp.mean(dn * nrm, axis=-1, keepdims=True))

    tile = functools.partial(_tile_specs, n_tiles_ex)
    prev = functools.partial(_prev_chunk_spec, n_tiles_ex)
    return pl.pallas_call(
        body, name="forward_mix", grid=(n_ex, n_tiles_ex),
        out_shape=[jax.ShapeDtypeStruct((n_tok, D_MODEL), F32), jax.ShapeDtypeStruct((n_tok, D_MODEL), F32),
                   jax.ShapeDtypeStruct((1, 128), F32), jax.ShapeDtypeStruct((1, D_MODEL), F32)],
        in_specs=[tile(A_WIDTH), tile(A_WIDTH), tile(SWA_WIDTH), tile(KV_WIDTH), tile(KV_WIDTH),
                  prev(KV_WIDTH), prev(KV_WIDTH), tile(MEM_WIDTH), tile(MIX_WIDTH),
                  pl.BlockSpec((1, MEM_LEN, 2 * MEM_WIDTH), lambda b, i: (b, 0, 0)),
                  tile(D_MODEL), tile(D_MODEL),
                  _full_spec((1, A_WIDTH)), _full_spec((1, A_WIDTH)), _full_spec((A_GROUPS, CHUNK, CHUNK)),
                  _full_spec((A_GROUPS, CHUNK, CHUNK)), SMEM_SPEC, _full_spec((2, 4 * CHUNK, 2 * CHUNK)),
                  _full_spec((MIX_WIDTH, D_MODEL)), _full_spec((1, D_MODEL))],
        out_specs=[tile(D_MODEL), tile(D_MODEL), _full_spec((1, 128)), _full_spec((1, D_MODEL))],
        compiler_params=pltpu.CompilerParams(vmem_limit_bytes=VMEM_LIMIT),
    )(au, av, sq, sk, sv, sk, sv, mq, z, mkv, x2, tgt2, v_g, v_b, w_sp, b_sp, sinks, bias, w_out, g_post)


def _backward_mix(parts, mkv, do, v_g, v_b, w_sp, b_sp, sinks, bias, w_out, n_ex, seq):
    n_tiles_ex = seq // TILE
    n_tok = n_ex * seq
    au, av, sq, sk, sv, mq, z = parts
    col = dict(zip(("au", "av", "sq", "sk", "sv", "mq", "z"),
                   (slice(PROJ_OFFSETS[k], PROJ_OFFSETS[k + 1]) for k in range(len(PROJ_WIDTHS)))))
    before_kv, after_kv = slice(0, col["sk"].start), slice(col["sv"].stop, IN_WIDTH)

    def body(do_ref, au_ref, av_ref, sq_ref, sk_ref, sv_ref, skp_ref, svp_ref, mq_ref, z_ref, mkv_ref,
             vg_ref, vb_ref, ws_ref, bs_ref, sink_ref, bias_ref, wout_ref,
             dproj_ref, dmkv_ref, dwout_ref, dvg_ref, dvb_ref, dws_ref, dbs_ref, dsink_ref, drel_ref,
             carry_dp, carry_k, carry_v):
        b, i = pl.program_id(0), pl.program_id(1)

        @pl.when((b == 0) & (i == 0))
        def _():
            for ref in (dwout_ref, dvg_ref, dvb_ref, dws_ref, dbs_ref, dsink_ref, drel_ref):
                ref[...] = jnp.zeros_like(ref)

        @pl.when(i == 0)
        def _():
            dmkv_ref[...] = jnp.zeros_like(dmkv_ref)
            carry_k[...] = jnp.zeros_like(carry_k)
            carry_v[...] = jnp.zeros_like(carry_v)

        @pl.when(i > 0)
        def _():
            dproj_ref[:, before_kv] = carry_dp[:, before_kv]
            dproj_ref[:, after_kv] = carry_dp[:, after_kv]

        @pl.when(i < n_tiles_ex)
        def _():
            wm, bs_rows, sink_col, mk_pairs, mv_pairs = _tile_constants(ws_ref, bs_ref, sink_ref, mkv_ref)
            vg = vg_ref[...]
            do_b = do_ref[...].astype(BF16)
            dy = _mm_nt(do_b, wout_ref[...])
            zv = z_ref[...]
            sig = _sigmoid(zv)
            sz = zv * sig
            dyc = dy * sz

            au_v, av_v = au_ref[...], av_ref[...]
            ya, res = _group_a_forward(au_v, av_v, vg, vb_ref[...], wm, bs_rows)
            dgu, dgv = [], []
            for g in range(A_GROUPS):
                sl = slice(g * 128, (g + 1) * 128)
                xhat, rstd, vn, s = res["groups"][g]
                dya = dyc[:, sl]
                dgu.append(dya * s)
                ds = dya * res["gu"][:, sl]
                dbs_ref[:, sl] += sum(ds[c * CHUNK:(c + 1) * CHUNK] for c in range(TILE_CHUNKS))
                ds_b = _rows_to_lanes(ds.astype(BF16), TILE_CHUNKS)
                dws_ref[g] += _mm_nt(ds_b, vn)
                dvn = _lanes_to_rows(_mm_tn(wm[g], ds_b), TILE_CHUNKS)
                dvg_ref[:, sl] += jnp.sum(dvn * xhat, axis=0, keepdims=True)
                dvb_ref[:, sl] += jnp.sum(dvn, axis=0, keepdims=True)
                dxh = dvn * vg[:, sl]
                dgv.append(rstd * (dxh - jnp.mean(dxh, axis=-1, keepdims=True)
                                   - xhat * jnp.mean(dxh * xhat, axis=-1, keepdims=True)))
            carry_dp[:, col["au"]] = (jnp.concatenate(dgu, axis=-1) * _gelu_grad(au_v, res["tu"])).astype(BF16)
            carry_dp[:, col["av"]] = (jnp.concatenate(dgv, axis=-1) * _gelu_grad(av_v, res["tv"])).astype(BF16)

            lane4 = lax.broadcasted_iota(jnp.int32, (1, 128), 1)
            dsink_vec = jnp.zeros((1, 128), F32)
            yb, dk_parts, dv_parts = [], [], []
            for j in range(TILE_CHUNKS):
                rows, k_pairs, v_pairs, table = _load_chunk(j, i, sk_ref, sv_ref, skp_ref, svp_ref)
                qp = _halves_bf16(sq_ref[rows, :])
                p, ps = _attention_probs(qp, k_pairs, bias_ref[table], sink_col)
                out, pp = _attention_out(p, v_pairs, CHUNK)
                yb.append(out)
                do_pairs = _halves_bf16(dyc[rows, A_WIDTH:A_WIDTH + SWA_WIDTH])
                dl, delta, dq, dk, dv = _attention_backward(p, pp, do_pairs, qp, k_pairs, v_pairs, CHUNK)
                sink_terms = ps * delta
                for h in range(4):
                    dsink_vec = dsink_vec + jnp.where(lane4 == h, -jnp.sum(sink_terms[h * CHUNK:(h + 1) * CHUNK]), 0.0)
                drel_ref[...] += dl
                carry_dp[rows, col["sq"]] = (dq * QK_SCALE).astype(BF16)
                dk_parts.append(_swa_unvariants(*_split_pair_grads(dk)) * QK_SCALE)
                dv_parts.append(_swa_unvariants(*_split_pair_grads(dv)))

            mqp = _halves_bf16(mq_ref[...])
            pm, _ = _attention_probs(mqp, mk_pairs, None, None)
            yc, ppm = _attention_out(pm, mv_pairs, TILE)
            dc_pairs = _halves_bf16(dyc[:, A_WIDTH + SWA_WIDTH:])
            _, _, dmq, dmk, dmv = _attention_backward(pm, ppm, dc_pairs, mqp, mk_pairs, mv_pairs, TILE)
            carry_dp[:, col["mq"]] = (dmq * QK_SCALE).astype(BF16)
            dmkv_ref[0] += jnp.concatenate([_mem_unvariants(*_split_pair_grads(dmk)) * QK_SCALE,
                                            _mem_unvariants(*_split_pair_grads(dmv))], axis=-1)

            ycat = jnp.concatenate(ya + [jnp.concatenate(yb, axis=0), yc], axis=-1)
            dwout_ref[...] += _mm_tn((ycat * sz).astype(BF16), do_b)
            carry_dp[:, col["z"]] = (dy * ycat * (sig * (1.0 + zv * (1.0 - sig)))).astype(BF16)
            dsink_ref[...] += dsink_vec

            for parts_c, carry, cols in ((dk_parts, carry_k, col["sk"]), (dv_parts, carry_v, col["sv"])):
                @pl.when(i > 0)
                def _():
                    dproj_ref[:, cols] = (carry[...] + jnp.concatenate(
                        [jnp.zeros((TILE - CHUNK, KV_WIDTH), F32), parts_c[0][:CHUNK]], axis=0)).astype(BF16)
                new = [parts_c[0][CHUNK:]]
                for j in range(1, TILE_CHUNKS):
                    new[-1] = new[-1] + parts_c[j][:CHUNK]
                    new.append(parts_c[j][CHUNK:])
                carry[...] = jnp.concatenate(new, axis=0)

        @pl.when(i == n_tiles_ex)
        def _():
            dproj_ref[:, col["sk"]] = carry_k[...].astype(BF16)
            dproj_ref[:, col["sv"]] = carry_v[...].astype(BF16)

    tile = functools.partial(_tile_specs, n_tiles_ex)
    prev = functools.partial(_prev_chunk_spec, n_tiles_ex)
    late = pl.BlockSpec((TILE, IN_WIDTH), lambda b, i: (b * n_tiles_ex + jnp.maximum(i - 1, 0), 0))
    return pl.pallas_call(
        body, name="backward_mix", grid=(n_ex, n_tiles_ex + 1),
        out_shape=[jax.ShapeDtypeStruct((n_tok, IN_WIDTH), BF16),
                   jax.ShapeDtypeStruct((n_ex, MEM_LEN, 2 * MEM_WIDTH), F32),
                   jax.ShapeDtypeStruct((MIX_WIDTH, D_MODEL), F32), jax.ShapeDtypeStruct((1, A_WIDTH), F32),
                   jax.ShapeDtypeStruct((1, A_WIDTH), F32), jax.ShapeDtypeStruct((A_GROUPS, CHUNK, CHUNK), F32),
                   jax.ShapeDtypeStruct((CHUNK, A_WIDTH), F32), jax.ShapeDtypeStruct((1, 128), F32),
                   jax.ShapeDtypeStruct((4 * CHUNK, 2 * CHUNK), F32)],
        in_specs=[tile(D_MODEL), tile(A_WIDTH), tile(A_WIDTH), tile(SWA_WIDTH), tile(KV_WIDTH), tile(KV_WIDTH),
                  prev(KV_WIDTH), prev(KV_WIDTH), tile(MEM_WIDTH), tile(MIX_WIDTH),
                  pl.BlockSpec((1, MEM_LEN, 2 * MEM_WIDTH), lambda b, i: (b, 0, 0)),
                  _full_spec((1, A_WIDTH)), _full_spec((1, A_WIDTH)), _full_spec((A_GROUPS, CHUNK, CHUNK)),
                  _full_spec((A_GROUPS, CHUNK, CHUNK)), SMEM_SPEC, _full_spec((2, 4 * CHUNK, 2 * CHUNK)),
                  _full_spec((MIX_WIDTH, D_MODEL))],
        out_specs=[late, pl.BlockSpec((1, MEM_LEN, 2 * MEM_WIDTH), lambda b, i: (b, 0, 0)),
                   _full_spec((MIX_WIDTH, D_MODEL)), _full_spec((1, A_WIDTH)), _full_spec((1, A_WIDTH)),
                   _full_spec((A_GROUPS, CHUNK, CHUNK)), _full_spec((CHUNK, A_WIDTH)), _full_spec((1, 128)),
                   _full_spec((4 * CHUNK, 2 * CHUNK))],
        scratch_shapes=[pltpu.VMEM((TILE, IN_WIDTH), BF16), pltpu.VMEM((TILE, KV_WIDTH), F32),
                        pltpu.VMEM((TILE, KV_WIDTH), F32)],
        compiler_params=pltpu.CompilerParams(vmem_limit_bytes=VMEM_LIMIT),
    )(do, au, av, sq, sk, sv, sk, sv, mq, z, mkv, v_g, v_b, w_sp, b_sp, sinks, bias, w_out)


BWD_PROJ_TILE = 512


def _backward_projection(x2, dout, dproj, g_pre, w_in_t):
    n_tok = x2.shape[0]
    n_steps = n_tok // BWD_PROJ_TILE

    def body(x_ref, dout_ref, dp_ref, g_ref, w_hbm, dx_ref, dgpre_ref, w_vmem, sem):
        @pl.when(pl.program_id(0) == 0)
        def _():
            load = pltpu.make_async_copy(w_hbm, w_vmem, sem)
            load.start()
            dgpre_ref[...] = jnp.zeros_like(dgpre_ref)
            load.wait()

        xv = x_ref[...]
        r = lax.rsqrt(jnp.mean(xv * xv, axis=-1, keepdims=True) + EPS)
        xn = xv * r
        dh = _mm(dp_ref[...], w_vmem[...])
        dgpre_ref[...] += jnp.sum(dh * xn, axis=0, keepdims=True)
        dhg = dh * g_ref[...]
        dx_ref[...] = r * (dhg - xn * jnp.mean(dhg * xn, axis=-1, keepdims=True)) + dout_ref[...]

    row = lambda w: pl.BlockSpec((BWD_PROJ_TILE, w), lambda i: (i, 0))
    return pl.pallas_call(
        body, name="backward_projection", grid=(n_steps,),
        out_shape=[jax.ShapeDtypeStruct((n_tok, D_MODEL), F32), jax.ShapeDtypeStruct((1, D_MODEL), F32)],
        in_specs=[row(D_MODEL), row(D_MODEL), row(IN_WIDTH), _full_spec((1, D_MODEL)), ANY_SPEC],
        out_specs=[row(D_MODEL), _full_spec((1, D_MODEL))],
        scratch_shapes=[pltpu.VMEM((IN_WIDTH, D_MODEL), BF16), pltpu.SemaphoreType.DMA],
        input_output_aliases={1: 0},
        compiler_params=pltpu.CompilerParams(vmem_limit_bytes=VMEM_LIMIT),
    )(x2, dout, dproj, g_pre, w_in_t)


SHARD_ROWS = IN_WIDTH // N_CHIPS
SHARD_WINDOW = 768
DWIN_TILE = 256


def _shard_window_start(shard):
    return (shard * SHARD_ROWS // 128) * 128


def _dwin_partial(dproj, h):
    n_tok = h.shape[0]
    n_steps = n_tok // DWIN_TILE

    def body(dp_ref, h_ref, out_ref, acc):
        s, t = pl.program_id(0), pl.program_id(1)

        @pl.when(t == 0)
        def _():
            acc[...] = jnp.zeros_like(acc)

        acc[...] += _mm_tn(dp_ref[...], h_ref[...])

        @pl.when(t == n_steps - 1)
        def _():
            for parity in range(2):
                @pl.when(s % 2 == parity)
                def _():
                    out_ref[0] = acc[64 * parity:64 * parity + SHARD_ROWS, :]

    return pl.pallas_call(
        body, name="dwin_partial", grid=(N_CHIPS, n_steps),
        out_shape=jax.ShapeDtypeStruct((N_CHIPS, SHARD_ROWS, D_MODEL), F32),
        in_specs=[pl.BlockSpec((pl.Element(DWIN_TILE), pl.Element(SHARD_WINDOW)),
                               lambda s, t: (t * DWIN_TILE, _shard_window_start(s))),
                  pl.BlockSpec((DWIN_TILE, D_MODEL), lambda s, t: (t, 0))],
        out_specs=pl.BlockSpec((1, SHARD_ROWS, D_MODEL), lambda s, t: (s, 0, 0)),
        scratch_shapes=[pltpu.VMEM((SHARD_WINDOW, D_MODEL), F32)],
        compiler_params=pltpu.CompilerParams(vmem_limit_bytes=VMEM_LIMIT),
    )(dproj, h)


def _memkv_backward(mem, dmkv, g_mem, w_mkv):
    n_ex = mem.shape[0]

    def body(mem_ref, d_ref, g_ref, w_ref, dw_ref, dg_ref):
        @pl.when(pl.program_id(0) == 0)
        def _():
            dw_ref[...] = jnp.zeros_like(dw_ref)
            dg_ref[...] = jnp.zeros_like(dg_ref)

        m = mem_ref[0]
        mn = m * lax.rsqrt(jnp.mean(m * m, axis=-1, keepdims=True) + EPS)
        d_b = d_ref[0].astype(BF16)
        dw_ref[...] += _mm_tn((mn * g_ref[...]).astype(BF16), d_b)
        dg_ref[...] += jnp.sum(_mm_nt(d_b, w_ref[...]) * mn, axis=0, keepdims=True)

    return pl.pallas_call(
        body, name="memkv_backward", grid=(n_ex,),
        out_shape=[jax.ShapeDtypeStruct((D_MODEL, 2 * MEM_WIDTH), F32), jax.ShapeDtypeStruct((1, D_MODEL), F32)],
        in_specs=[pl.BlockSpec((1, MEM_LEN, D_MODEL), lambda b: (b, 0, 0)),
                  pl.BlockSpec((1, MEM_LEN, 2 * MEM_WIDTH), lambda b: (b, 0, 0)),
                  _full_spec((1, D_MODEL)), _full_spec((D_MODEL, 2 * MEM_WIDTH))],
        out_specs=[_full_spec((D_MODEL, 2 * MEM_WIDTH)), _full_spec((1, D_MODEL))],
    )(mem, dmkv, g_mem, w_mkv)


def _pack_small_grads(dgpre, dgpost, dgmem, dvg, dvb, dws, dbs, dsink, drel, buckets):
    def body(dgpre_ref, dgpost_ref, dgmem_ref, dvg_ref, dvb_ref, dws_ref, dbs_ref, dsink_ref, drel_ref, bk_ref,
             a_ref, b_ref):
        a_ref[...] = jnp.zeros_like(a_ref)
        b_ref[...] = jnp.zeros_like(b_ref)
        a_ref[0:1, :] = dgpre_ref[...]
        a_ref[1:2, :] = dgpost_ref[...]
        a_ref[2:3, :] = dgmem_ref[...]
        a_ref[3:4, :] = jnp.concatenate([dvg_ref[...], dvb_ref[...]], axis=-1)
        row = lax.broadcasted_iota(jnp.int32, (CHUNK, CHUNK), 0)
        col = lax.broadcasted_iota(jnp.int32, (CHUNK, CHUNK), 1)
        for g in range(A_GROUPS):
            b_ref[ROW_WS + g * CHUNK:ROW_WS + (g + 1) * CHUNK, :] = jnp.where(row >= col, dws_ref[g], 0.0)
            by_token = jnp.transpose(dbs_ref[:, g * 128:(g + 1) * 128])
            b_ref[ROW_BS + g:ROW_BS + g + 1, :] = jnp.sum(by_token, axis=0, keepdims=True)
        b_ref[ROW_SINK:ROW_SINK + 1, :] = dsink_ref[...]
        bk = bk_ref[...]
        rel_row = lax.broadcasted_iota(jnp.int32, (8, 128), 0)
        rel_col = lax.broadcasted_iota(jnp.int32, (8, 128), 1)
        rel = jnp.zeros((8, 128), F32)
        for h in range(4):
            acc = drel_ref[h * CHUNK:(h + 1) * CHUNK, :]
            for b in range(N_BUCKETS):
                rel = jnp.where((rel_row == h) & (rel_col == b), jnp.sum(jnp.where(bk == b, acc, 0.0)), rel)
        b_ref[ROW_REL:ROW_REL + 8, :] = rel

    return pl.pallas_call(
        body, name="pack_small_grads",
        out_shape=[jax.ShapeDtypeStruct((SMALL_A_ROWS, D_MODEL), F32), jax.ShapeDtypeStruct((SMALL_B_ROWS, 128), F32)],
        in_specs=[VMEM_SPEC] * 10, out_specs=[VMEM_SPEC] * 2,
    )(dgpre, dgpost, dgmem, dvg, dvb, dws, dbs, dsink, drel, buckets)


def _exchange_siblings(big, small_a, small_b):
    def body(g0, g1, g2, sa, sb, r0, r1, r2, ra, rb, send_sems, recv_sems):
        x, y, c = lax.axis_index("x"), lax.axis_index("y"), lax.axis_index("c")
        pairs = [(g.at[:, pl.ds(1 - c, 1)], r) for g, r in ((g0, r0), (g1, r1), (g2, r2))] + [(sa, ra), (sb, rb)]
        copies = [pltpu.make_async_remote_copy(src_ref=src, dst_ref=dst, send_sem=send_sems.at[k],
                                               recv_sem=recv_sems.at[k], device_id=(x, y, 1 - c),
                                               device_id_type=MESH)
                  for k, (src, dst) in enumerate(pairs)]
        for cp in copies:
            cp.start()
        for cp in copies:
            cp.wait_recv()
        for cp in copies:
            cp.wait_send()

    out_shape = [jax.ShapeDtypeStruct((g.shape[0], 1) + g.shape[2:], F32) for g in big]
    out_shape += [jax.ShapeDtypeStruct(small_a.shape, F32), jax.ShapeDtypeStruct(small_b.shape, F32)]
    return pl.pallas_call(
        body, name="exchange_siblings", out_shape=out_shape,
        in_specs=[ANY_SPEC] * 5, out_specs=[ANY_SPEC] * 5,
        scratch_shapes=[pltpu.SemaphoreType.DMA((5,)), pltpu.SemaphoreType.DMA((5,))],
    )(*big, small_a, small_b)


def _chip_sum(big, recv, small, small_recv, c_arr):
    def body(c_ref, g0, g1, g2, r0, r1, r2, sa, sb, ra, rb, p0, p1, p2, ca, cb):
        for g, r, p in ((g0, r0, p0), (g1, r1, p1), (g2, r2, p2)):
            p[...] = (g[...] + r[...]).astype(BF16)

        @pl.when(pl.program_id(0) == 0)
        def _():
            ca[...] = sa[...] + ra[...]
            cb[...] = sb[...] + rb[...]

    def own(g):
        return pl.BlockSpec((1, 1) + g.shape[2:], lambda j, c_ref: (j, c_ref[0], 0, 0))

    def got(g):
        return pl.BlockSpec((1, 1) + g.shape[2:], lambda j, c_ref: (j, 0, 0, 0))

    def whole(a):
        return pl.BlockSpec(a.shape, lambda j, c_ref: (0, 0))

    return pl.pallas_call(
        body, name="chip_sum",
        out_shape=[jax.ShapeDtypeStruct(r.shape, BF16) for r in recv]
        + [jax.ShapeDtypeStruct(a.shape, F32) for a in small],
        grid_spec=pltpu.PrefetchScalarGridSpec(
            num_scalar_prefetch=1, grid=(N_CHIPS,),
            in_specs=[own(g) for g in big] + [got(g) for g in big] + [whole(a) for a in small + small_recv],
            out_specs=[got(g) for g in big] + [whole(a) for a in small]),
        compiler_params=pltpu.CompilerParams(vmem_limit_bytes=VMEM_LIMIT),
    )(c_arr, *big, *recv, *small, *small_recv)


def _exchange_chips(partials, small):
    def body(p0, p1, p2, ca, cb, r0, r1, r2, ga, gb, send_sems, recv_sems, local_sems):
        x, y, c = lax.axis_index("x"), lax.axis_index("y"), lax.axis_index("c")
        chips = [(1 - x, y), (x, 1 - y), (1 - x, 1 - y)]
        my_chip = 2 * x + y
        own = [pltpu.make_async_copy(ca, ga.at[my_chip], local_sems.at[0]),
               pltpu.make_async_copy(cb, gb.at[my_chip], local_sems.at[1])]
        for cp in own:
            cp.start()
        copies = []
        for j, chip in enumerate(chips):
            pairs = [(p.at[2 * chip[0] + chip[1]], r.at[j]) for p, r in ((p0, r0), (p1, r1), (p2, r2))]
            pairs += [(ca, ga.at[my_chip]), (cb, gb.at[my_chip])]
            for w, (src, dst) in enumerate(pairs):
                copies.append(pltpu.make_async_remote_copy(
                    src_ref=src, dst_ref=dst, send_sem=send_sems.at[5 * j + w], recv_sem=recv_sems.at[5 * j + w],
                    device_id=(*chip, c), device_id_type=MESH))
        for cp in copies:
            cp.start()
        for cp in copies:
            cp.wait_recv()
        for cp in copies:
            cp.wait_send()
        for cp in own:
            cp.wait()

    return pl.pallas_call(
        body, name="exchange_chips",
        out_shape=[jax.ShapeDtypeStruct((3,) + p.shape[1:], BF16) for p in partials]
        + [jax.ShapeDtypeStruct((N_CHIPS,) + a.shape, F32) for a in small],
        in_specs=[ANY_SPEC] * 5, out_specs=[ANY_SPEC] * 5,
        scratch_shapes=[pltpu.SemaphoreType.DMA((15,)), pltpu.SemaphoreType.DMA((15,)),
                        pltpu.SemaphoreType.DMA((2,))],
    )(*partials, *small)


def _shard_sum(partials, recv, shard_arr):
    def body(s_ref, p0, p1, p2, r0, r1, r2, o0, o1, o2):
        for p, r, o in ((p0, r0, o0), (p1, r1, o1), (p2, r2, o2)):
            o[...] = ((p[0, 0].astype(F32) + r[0, 0].astype(F32)) + r[1, 0].astype(F32)) + r[2, 0].astype(F32)

    def own(p):
        return pl.BlockSpec((1,) + p.shape[1:], lambda i, s_ref: (s_ref[0], 0, 0, 0))

    def got(p):
        return pl.BlockSpec((3,) + p.shape[1:], lambda i, s_ref: (0, 0, 0, 0))

    return pl.pallas_call(
        body, name="shard_sum",
        out_shape=[jax.ShapeDtypeStruct(p.shape[2:], F32) for p in partials],
        grid_spec=pltpu.PrefetchScalarGridSpec(
            num_scalar_prefetch=1, grid=(1,),
            in_specs=[own(p) for p in partials] + [got(p) for p in partials],
            out_specs=[pl.BlockSpec(p.shape[2:], lambda i, s_ref: (0, 0)) for p in partials]),
        compiler_params=pltpu.CompilerParams(vmem_limit_bytes=VMEM_LIMIT),
    )(shard_arr, *partials, *recv)


def _exchange_halves(halves):
    def body(h0, h1, h2, r0, r1, r2, send_sems, recv_sems):
        x, y, c = lax.axis_index("x"), lax.axis_index("y"), lax.axis_index("c")
        copies = [pltpu.make_async_remote_copy(src_ref=h, dst_ref=r, send_sem=send_sems.at[w],
                                               recv_sem=recv_sems.at[w], device_id=(x, y, 1 - c),
                                               device_id_type=MESH)
                  for w, (h, r) in enumerate(((h0, r0), (h1, r1), (h2, r2)))]
        for cp in copies:
            cp.start()
        for cp in copies:
            cp.wait_recv()
        for cp in copies:
            cp.wait_send()

    return pl.pallas_call(
        body, name="exchange_halves",
        out_shape=[jax.ShapeDtypeStruct(h.shape, F32) for h in halves],
        in_specs=[ANY_SPEC] * 3, out_specs=[ANY_SPEC] * 3,
        scratch_shapes=[pltpu.SemaphoreType.DMA((3,)), pltpu.SemaphoreType.DMA((3,))],
    )(*halves)


def _adamw(w, g, m, v):
    m2 = ADAM_B1 * m + (1.0 - ADAM_B1) * g
    v2 = ADAM_B2 * v + (1.0 - ADAM_B2) * (g * g)
    m_hat = m2 / (1.0 - ADAM_B1 ** ADAM_STEP)
    v_hat = v2 / (1.0 - ADAM_B2 ** ADAM_STEP)
    delta = -ADAM_LR * (m_hat / (jnp.sqrt(v_hat) + ADAM_EPS) + ADAM_WD * w)
    return delta, m2, v2


ADAM_MAX_ROWS = 176


def _adamw_sharded(mine, other, w, m, v, c_arr, name):
    rows, cols = w.shape
    half = rows // 2
    steps = -(-half // ADAM_MAX_ROWS)
    block_rows = half // steps
    assert block_rows * steps == half and block_rows % 8 == 0

    def body(c_ref, mine_ref, other_ref, w_ref, m_ref, v_ref, g_out, d_out, m_out, v_out):
        g = jnp.where(pl.program_id(0) == c_ref[0], mine_ref[...], other_ref[...])
        delta, m2, v2 = _adamw(w_ref[...], g, m_ref[...], v_ref[...])
        g_out[...] = g
        d_out[...] = delta
        m_out[...] = m2
        v_out[...] = v2

    part = pl.BlockSpec((block_rows, cols), lambda h, k, c_ref: (k, 0))
    full = pl.BlockSpec((block_rows, cols), lambda h, k, c_ref: (h * steps + k, 0))
    return pl.pallas_call(
        body, name=name, out_shape=[jax.ShapeDtypeStruct((rows, cols), F32)] * 4,
        grid_spec=pltpu.PrefetchScalarGridSpec(
            num_scalar_prefetch=1, grid=(2, steps), in_specs=[part, part, full, full, full], out_specs=[full] * 4),
    )(c_arr, mine, other, w, m, v)


def _adamw_small(ra, rb, weights, moments_m, moments_v):
    n = len(weights)

    def body(*refs):
        ra_ref, rb_ref = refs[0], refs[1]
        w_refs, m_refs, v_refs = refs[2:2 + n], refs[2 + n:2 + 2 * n], refs[2 + 2 * n:2 + 3 * n]
        outs = refs[2 + 3 * n:]
        g_outs, d_outs, m_outs, v_outs = outs[:n], outs[n:2 * n], outs[2 * n:3 * n], outs[3 * n:]
        ga, gb = ra_ref[0], rb_ref[0]
        for chip in range(1, N_CHIPS):
            ga = ga + ra_ref[chip]
            gb = gb + rb_ref[chip]
        grads = [ga[0:1, :], ga[1:2, :], ga[2:3, :], ga[3:4, :A_WIDTH], ga[3:4, A_WIDTH:],
                 gb[ROW_WS:ROW_WS + A_GROUPS * CHUNK, :].reshape(A_GROUPS, CHUNK, CHUNK),
                 gb[ROW_BS:ROW_BS + A_GROUPS, :], gb[ROW_SINK:ROW_SINK + 1, 0:4],
                 gb[ROW_REL:ROW_REL + 4, 0:N_BUCKETS]]
        for k in range(n):
            delta, m2, v2 = _adamw(w_refs[k][...], grads[k], m_refs[k][...], v_refs[k][...])
            g_outs[k][...] = grads[k]
            d_outs[k][...] = delta
            m_outs[k][...] = m2
            v_outs[k][...] = v2

    out_shape = [jax.ShapeDtypeStruct(w.shape, F32) for w in weights] * 4
    return pl.pallas_call(
        body, name="adamw_small", out_shape=out_shape,
        in_specs=[VMEM_SPEC] * (2 + 3 * n), out_specs=[VMEM_SPEC] * (4 * n),
    )(ra, rb, *weights, *moments_m, *moments_v)


def kernel(x, mem, pre_norm_g, post_norm_g, mem_norm_g, w_in, w_mem_kv, v_norm_g, v_norm_b, w_spatial, b_spatial, attn_sinks, rel_bias, w_out, loss_target, m_pre_norm_g, m_post_norm_g, m_mem_norm_g, m_w_in, m_w_mem_kv, m_v_norm_g, m_v_norm_b, m_w_spatial, m_b_spatial, m_attn_sinks, m_rel_bias, m_w_out, v_pre_norm_g, v_post_norm_g, v_mem_norm_g, v_w_in, v_w_mem_kv, v_v_norm_g, v_v_norm_b, v_w_spatial, v_b_spatial, v_attn_sinks, v_rel_bias, v_w_out):
    n_ex, seq, _ = x.shape
    n_tok = n_ex * seq
    x2 = x.reshape(n_tok, D_MODEL)
    tgt2 = loss_target.reshape(n_tok, D_MODEL)
    buckets = jnp.asarray(_bucket_map())
    c_arr = lax.axis_index("c").astype(jnp.int32).reshape(1)
    shard_arr = (2 * lax.axis_index("x") + lax.axis_index("y")).astype(jnp.int32).reshape(1)
    w_sp = w_spatial[0]
    b_sp = jnp.broadcast_to(b_spatial[0][:, :, None], (A_GROUPS, CHUNK, CHUNK))
    w_in_t, m_w_in_t, v_w_in_t = (jnp.transpose(a[0]) for a in (w_in, m_w_in, v_w_in))
    rel_t, m_rel_t, v_rel_t = (jnp.transpose(a) for a in (rel_bias, m_rel_bias, v_rel_bias))

    g_in, g_mkv, g_out = _gather_weights(w_in_t, w_mem_kv[0], w_out[0])
    w_in_b = g_in.reshape(IN_WIDTH, D_MODEL)
    w_mkv_b = g_mkv.reshape(D_MODEL, 2 * MEM_WIDTH)
    w_out_b = g_out.reshape(MIX_WIDTH, D_MODEL)

    bias = _make_bias(rel_t, buckets)
    mkv = _memkv_forward(mem, mem_norm_g, w_mkv_b)
    h_b, parts = _forward_projection(x2, pre_norm_g, w_in_b)
    dout, do, loss_vec, dgpost = _forward_mix(parts, mkv, x2, tgt2, v_norm_g, v_norm_b, w_sp, b_sp, attn_sinks, bias,
                                             w_out_b, post_norm_g, n_ex, seq)

    dproj, dmkv, dwout, dvg, dvb, dws, dbs, dsink, drel = _backward_mix(
        parts, mkv, do, v_norm_g, v_norm_b, w_sp, b_sp, attn_sinks, bias, w_out_b, n_ex, seq)
    dx, dgpre = _backward_projection(x2, dout, dproj, pre_norm_g, w_in_b)
    dwin = _dwin_partial(dproj, h_b)
    dwmkv, dgmem = _memkv_backward(mem, dmkv, mem_norm_g, w_mkv_b)
    small_a, small_b = _pack_small_grads(dgpre, dgpost, dgmem, dvg, dvb, dws, dbs, dsink, drel, buckets)

    shard_shapes = [w_in_t.shape, w_mem_kv.shape[1:], w_out.shape[1:]]
    big = [g.reshape(N_CHIPS, 2, s[0] // 2, s[1]) for g, s in zip((dwin, dwmkv, dwout), shard_shapes)]
    *recv, ra, rb = _exchange_siblings(big, small_a, small_b)
    *partials, ca, cb = _chip_sum(big, recv, [small_a, small_b], [ra, rb], c_arr)
    *recv2, ga, gb = _exchange_chips(partials, [ca, cb])
    mine = _shard_sum(partials, recv2, shard_arr)
    other = _exchange_halves(mine)

    big_w = [(w_in_t, m_w_in_t, v_w_in_t), (w_mem_kv[0], m_w_mem_kv[0], v_w_mem_kv[0]),
             (w_out[0], m_w_out[0], v_w_out[0])]
    big_names = ["adamw_w_in", "adamw_w_mem_kv", "adamw_w_out"]
    big_out = [_adamw_sharded(mine[k], other[k], *big_w[k], c_arr, big_names[k]) for k in range(3)]
    small_w = [pre_norm_g, post_norm_g, mem_norm_g, v_norm_g, v_norm_b, w_sp, b_spatial[0], attn_sinks, rel_t]
    small_m = [m_pre_norm_g, m_post_norm_g, m_mem_norm_g, m_v_norm_g, m_v_norm_b, m_w_spatial[0], m_b_spatial[0],
               m_attn_sinks, m_rel_t]
    small_v = [v_pre_norm_g, v_post_norm_g, v_mem_norm_g, v_v_norm_g, v_v_norm_b, v_w_spatial[0], v_b_spatial[0],
               v_attn_sinks, v_rel_t]
    small_out = _adamw_small(ga, gb, small_w, small_m, small_v)
    n_small = len(small_w)

    loss = lax.psum(loss_vec[0, 0], ALL_AXES)
    outputs = [loss, dx.reshape(x.shape)]
    for kind in range(4):
        s = small_out[kind * n_small:(kind + 1) * n_small]
        outputs += [s[0], s[1], s[2], jnp.transpose(big_out[0][kind])[None], big_out[1][kind][None], s[3], s[4],
                    s[5][None], s[6][None], s[7], jnp.transpose(s[8]), big_out[2][kind][None]]
    return tuple(outputs)
```

```python
import functools

import numpy as np
import jax
import jax.numpy as jnp
from jax import lax
from jax.experimental import pallas as pl
from jax.experimental.pallas import tpu as pltpu

F32 = jnp.float32
BF16 = jnp.bfloat16
MESH = pl.DeviceIdType.MESH
ALL_AXES = ("x", "y", "c")

D_MODEL = 1024
CHUNK = 128
A_WIDTH = 512
A_GROUPS = 4
SWA_WIDTH = 256
KV_WIDTH = 128
MEM_WIDTH = 256
MEM_LEN = 256
MIX_WIDTH = 1024
IN_WIDTH = 2816
N_BUCKETS = 32
MAX_DISTANCE = 128
EPS = 1e-6
NEG = -1e30
QK_SCALE = 0.125
HALF_HEAD_PAIR = 64

ADAM_LR = 0.001
ADAM_B1 = 0.9
ADAM_B2 = 0.999
ADAM_EPS = 1e-08
ADAM_WD = 0.01
ADAM_STEP = 10

N_CHIPS = 4
N_DEV = 8
TILE_CHUNKS = 2
TILE = TILE_CHUNKS * CHUNK
PROJ_TILE = 256
VMEM_LIMIT = 56 * 1024 * 1024

SMALL_A_ROWS = 8
ROW_WS = 0
ROW_BS = 512
ROW_SINK = 520
ROW_REL = 528
SMALL_B_ROWS = 536


def _mm(a, b):
    return lax.dot_general(a, b, (((1,), (0,)), ((), ())), preferred_element_type=F32)


def _mm_nt(a, b):
    return lax.dot_general(a, b, (((1,), (1,)), ((), ())), preferred_element_type=F32)


def _mm_tn(a, b):
    return lax.dot_general(a, b, (((0,), (0,)), ((), ())), preferred_element_type=F32)


def _bucket_map():
    qi = np.arange(CHUNK)[:, None]
    kj = np.arange(2 * CHUNK)[None, :]
    n = np.maximum(qi + CHUNK - kj, 0)
    max_exact = N_BUCKETS // 2
    large = max_exact + (np.log(np.maximum(n, 1) / max_exact) / np.log(MAX_DISTANCE / max_exact)
                         * (N_BUCKETS - max_exact)).astype(np.int32)
    large = np.minimum(large, N_BUCKETS - 1)
    return np.where(n < max_exact, n, large).astype(np.int32)


_GELU_C = 0.7978845608028654
_GELU_A = 0.044715


def _gelu(x):
    t = jnp.tanh(_GELU_C * (x + _GELU_A * x * x * x))
    return 0.5 * x * (1.0 + t), t


def _gelu_grad(x, t):
    return 0.5 * (1.0 + t) + 0.5 * x * (1.0 - t * t) * (_GELU_C * (1.0 + 3.0 * _GELU_A * x * x))


def _sigmoid(x):
    return 1.0 / (1.0 + jnp.exp(-x))


def _lane_lo(shape):
    return lax.broadcasted_iota(jnp.int32, shape, 1) < HALF_HEAD_PAIR


def _swa_variants(t):
    lo = _lane_lo(t.shape)
    tr = pltpu.roll(t, HALF_HEAD_PAIR, 1)
    zero = jnp.zeros_like(t)
    return (jnp.where(lo, t, zero).astype(BF16), jnp.where(lo, zero, tr).astype(BF16),
            jnp.where(lo, tr, zero).astype(BF16), jnp.where(lo, zero, t).astype(BF16))


def _swa_unvariants(d0, d1, d2, d3):
    lo = _lane_lo(d0.shape)
    zero = jnp.zeros_like(d0)
    rolled = jnp.where(lo, zero, d1) + jnp.where(lo, d2, zero)
    return jnp.where(lo, d0, zero) + jnp.where(lo, zero, d3) + pltpu.roll(rolled, HALF_HEAD_PAIR, 1)


def _mem_variants(t):
    out = []
    for pair in range(2):
        tp = t[:, pair * 128:(pair + 1) * 128]
        lo = _lane_lo(tp.shape)
        zero = jnp.zeros_like(tp)
        out.append(jnp.where(lo, tp, zero).astype(BF16))
        out.append(jnp.where(lo, zero, tp).astype(BF16))
    return out


def _mem_unvariants(d0, d1, d2, d3):
    lo = _lane_lo(d0.shape)
    return jnp.concatenate([jnp.where(lo, d0, d1), jnp.where(lo, d2, d3)], axis=-1)


def _softmax(logits, sinks):
    m = jnp.max(logits, axis=-1, keepdims=True)
    if sinks is not None:
        m = jnp.maximum(m, sinks)
    p = jnp.exp(logits - m)
    den = jnp.sum(p, axis=-1, keepdims=True)
    if sinks is None:
        return p * (1.0 / den), None
    es = jnp.exp(sinks - m)
    inv = 1.0 / (den + es)
    return p * inv, es * inv


def _band_valid(with_prev):
    qi = lax.broadcasted_iota(jnp.int32, (CHUNK, 2 * CHUNK), 0)
    kj = lax.broadcasted_iota(jnp.int32, (CHUNK, 2 * CHUNK), 1)
    in_cur = (kj >= CHUNK) & (kj - CHUNK <= qi)
    if not with_prev:
        return in_cur
    return in_cur | ((kj < CHUNK) & (kj > qi))


def _causal_weights(ws_ref):
    row = lax.broadcasted_iota(jnp.int32, (CHUNK, CHUNK), 0)
    col = lax.broadcasted_iota(jnp.int32, (CHUNK, CHUNK), 1)
    return [jnp.where(row >= col, ws_ref[g], 0.0).astype(BF16) for g in range(A_GROUPS)]


def _rows_to_lanes(a, n):
    return jnp.concatenate([a[c * CHUNK:(c + 1) * CHUNK] for c in range(n)], axis=1)


def _lanes_to_rows(a, n):
    w = a.shape[1] // n
    return jnp.concatenate([a[:, c * w:(c + 1) * w] for c in range(n)], axis=0)


def _stack_heads(pair01, pair23):
    return jnp.concatenate([pair01[:, :256], pair01[:, 256:], pair23[:, :256], pair23[:, 256:]], axis=0)


def _pair_heads(s, r):
    return (jnp.concatenate([s[0:r], s[r:2 * r]], axis=1), jnp.concatenate([s[2 * r:3 * r], s[3 * r:4 * r]], axis=1))


def _pair_operands(variants):
    return (jnp.concatenate(variants[0:2], axis=0), jnp.concatenate(variants[2:4], axis=0))


def _split_pair_grads(d_pairs):
    return d_pairs[0][:256], d_pairs[0][256:], d_pairs[1][:256], d_pairs[1][256:]


def _halves_bf16(a):
    return (a[:, :128].astype(BF16), a[:, 128:].astype(BF16))


def _group_a_forward(au, av, vg, vb, wm, bs_rows):
    gu, tu = _gelu(au)
    gv, tv = _gelu(av)
    ya, res = [], []
    for g in range(A_GROUPS):
        sl = slice(g * 128, (g + 1) * 128)
        xg = gv[:, sl]
        xc = xg - jnp.mean(xg, axis=-1, keepdims=True)
        rstd = lax.rsqrt(jnp.mean(xc * xc, axis=-1, keepdims=True) + EPS)
        xhat = xc * rstd
        vn = _rows_to_lanes((xhat * vg[:, sl] + vb[:, sl]).astype(BF16), TILE_CHUNKS)
        s = _lanes_to_rows(_mm(wm[g], vn), TILE_CHUNKS) + bs_rows[g]
        ya.append(gu[:, sl] * s)
        res.append((xhat, rstd, vn, s))
    return ya, dict(gu=gu, tu=tu, tv=tv, groups=res)


def _attention_probs(qp, k_pairs, bias, sink_col):
    logits = _stack_heads(_mm_nt(qp[0], k_pairs[0]), _mm_nt(qp[1], k_pairs[1])) * QK_SCALE
    if bias is not None:
        logits = logits + bias
    return _softmax(logits, sink_col)


def _attention_out(p, v_pairs, r):
    pp = _pair_heads(p.astype(BF16), r)
    return jnp.concatenate([_mm(pp[0], v_pairs[0]), _mm(pp[1], v_pairs[1])], axis=-1), pp


def _attention_backward(p, pp, do_pairs, qp, k_pairs, v_pairs, r):
    dp = _stack_heads(_mm_nt(do_pairs[0], v_pairs[0]), _mm_nt(do_pairs[1], v_pairs[1]))
    delta = jnp.sum(p * dp, axis=-1, keepdims=True)
    dl = p * (dp - delta)
    dlp = _pair_heads(dl.astype(BF16), r)
    dq = jnp.concatenate([_mm(dlp[0], k_pairs[0]), _mm(dlp[1], k_pairs[1])], axis=-1)
    dk = (_mm_tn(dlp[0], qp[0]), _mm_tn(dlp[1], qp[1]))
    dv = (_mm_tn(pp[0], do_pairs[0]), _mm_tn(pp[1], do_pairs[1]))
    return dl, delta, dq, dk, dv


def _tile_specs(n_tiles_ex, width):
    return pl.BlockSpec((TILE, width), lambda b, i: (b * n_tiles_ex + jnp.minimum(i, n_tiles_ex - 1), 0))


def _prev_chunk_spec(n_tiles_ex, width):
    def index(b, i):
        chunk = TILE_CHUNKS * jnp.minimum(i, n_tiles_ex - 1)
        return (b * n_tiles_ex * TILE_CHUNKS + jnp.maximum(chunk - 1, 0), 0)
    return pl.BlockSpec((CHUNK, width), index)


def _full_spec(shape):
    zeros = (0,) * len(shape)
    return pl.BlockSpec(shape, lambda *_: zeros)


SMEM_SPEC = pl.BlockSpec(memory_space=pltpu.SMEM)
ANY_SPEC = pl.BlockSpec(memory_space=pl.ANY)
VMEM_SPEC = pl.BlockSpec(memory_space=pltpu.VMEM)


def _make_bias(rel_bias_t, buckets):
    def body(rel_ref, bk_ref, out_ref):
        bk = bk_ref[...]
        for h in range(4):
            acc = jnp.zeros((CHUNK, 2 * CHUNK), F32)
            for b in range(N_BUCKETS):
                acc = jnp.where(bk == b, rel_ref[h, b], acc)
            for t, with_prev in enumerate((True, False)):
                out_ref[t, h * CHUNK:(h + 1) * CHUNK, :] = jnp.where(_band_valid(with_prev), acc, NEG)

    return pl.pallas_call(
        body, name="make_bias", out_shape=jax.ShapeDtypeStruct((2, 4 * CHUNK, 2 * CHUNK), F32),
        in_specs=[SMEM_SPEC, VMEM_SPEC], out_specs=VMEM_SPEC,
    )(rel_bias_t, buckets)


def _gather_weights(w_in_s, w_mkv_s, w_out_s):
    shapes = [w_in_s.shape, w_mkv_s.shape, w_out_s.shape]
    n_w = len(shapes)

    def body(win_ref, wmkv_ref, wout_ref, gin_ref, gmkv_ref, gout_ref, send_sems, recv_sems):
        x, y, c = lax.axis_index("x"), lax.axis_index("y"), lax.axis_index("c")
        me, sibling = (x, y, c), (x, y, 1 - c)
        chips = [(1 - x, y), (x, 1 - y), (1 - x, 1 - y)]
        ins = [win_ref, wmkv_ref, wout_ref]
        outs = [gin_ref, gmkv_ref, gout_ref]
        my_shard = 2 * x + y
        for w in range(n_w):
            outs[w][my_shard] = ins[w][...].astype(BF16)

        def copy(k, w, shard, half, to):
            rows = shapes[w][0] // 2
            ref = outs[w].at[shard, pl.ds(half * rows, rows), :]
            return pltpu.make_async_remote_copy(src_ref=ref, dst_ref=ref, send_sem=send_sems.at[k],
                                                recv_sem=recv_sems.at[k], device_id=to, device_id_type=MESH)

        pairs = [(w, j) for w in range(n_w) for j in range(3)]
        first = [copy(3 * w + j, w, my_shard, c, (*chips[j], c)) for w, j in pairs]
        for cp in first:
            cp.start()
        passed = []
        for w, j in pairs:
            shard = 2 * chips[j][0] + chips[j][1]
            copy(3 * w + j, w, shard, c, me).wait_recv()
            fwd = copy(9 + 3 * w + j, w, shard, c, sibling)
            fwd.start()
            passed.append(fwd)
        for w, j in pairs:
            shard = 2 * chips[j][0] + chips[j][1]
            copy(9 + 3 * w + j, w, shard, 1 - c, me).wait_recv()
        for cp in first + passed:
            cp.wait_send()

    return pl.pallas_call(
        body, name="gather_weights",
        out_shape=[jax.ShapeDtypeStruct((N_CHIPS,) + s, BF16) for s in shapes],
        in_specs=[VMEM_SPEC] * 3, out_specs=[VMEM_SPEC] * 3,
        scratch_shapes=[pltpu.SemaphoreType.DMA((18,)), pltpu.SemaphoreType.DMA((18,))],
        compiler_params=pltpu.CompilerParams(vmem_limit_bytes=VMEM_LIMIT),
    )(w_in_s, w_mkv_s, w_out_s)


def _memkv_forward(mem, g_mem, w_mkv):
    n_ex = mem.shape[0]

    def body(mem_ref, g_ref, w_ref, out_ref):
        m = mem_ref[0]
        r = lax.rsqrt(jnp.mean(m * m, axis=-1, keepdims=True) + EPS)
        out_ref[0] = _mm((m * r * g_ref[...]).astype(BF16), w_ref[...])

    return pl.pallas_call(
        body, name="memkv_forward", grid=(n_ex,),
        out_shape=jax.ShapeDtypeStruct((n_ex, MEM_LEN, 2 * MEM_WIDTH), F32),
        in_specs=[pl.BlockSpec((1, MEM_LEN, D_MODEL), lambda b: (b, 0, 0)), _full_spec((1, D_MODEL)),
                  _full_spec((D_MODEL, 2 * MEM_WIDTH))],
        out_specs=pl.BlockSpec((1, MEM_LEN, 2 * MEM_WIDTH), lambda b: (b, 0, 0)),
    )(mem, g_mem, w_mkv)


PROJ_WIDTHS = (A_WIDTH, A_WIDTH, SWA_WIDTH, KV_WIDTH, KV_WIDTH, MEM_WIDTH, MIX_WIDTH)
PROJ_OFFSETS = tuple(int(v) for v in np.cumsum((0,) + PROJ_WIDTHS))


def _forward_projection(x2, g_pre, w_in_t):
    n_tok = x2.shape[0]

    def body(x_ref, g_ref, w_ref, h_ref, *out_refs):
        xv = x_ref[...]
        r = lax.rsqrt(jnp.mean(xv * xv, axis=-1, keepdims=True) + EPS)
        h = (xv * r * g_ref[...]).astype(BF16)
        h_ref[...] = h
        proj = _mm_nt(h, w_ref[...])
        for k, ref in enumerate(out_refs):
            ref[...] = proj[:, PROJ_OFFSETS[k]:PROJ_OFFSETS[k + 1]]

    widths = (D_MODEL,) + PROJ_WIDTHS
    h, *parts = pl.pallas_call(
        body, name="forward_projection", grid=(n_tok // PROJ_TILE,),
        out_shape=[jax.ShapeDtypeStruct((n_tok, D_MODEL), BF16)]
        + [jax.ShapeDtypeStruct((n_tok, w), F32) for w in PROJ_WIDTHS],
        in_specs=[pl.BlockSpec((PROJ_TILE, D_MODEL), lambda i: (i, 0)), _full_spec((1, D_MODEL)),
                  _full_spec((IN_WIDTH, D_MODEL))],
        out_specs=[pl.BlockSpec((PROJ_TILE, w), lambda i: (i, 0)) for w in widths],
        compiler_params=pltpu.CompilerParams(vmem_limit_bytes=VMEM_LIMIT),
    )(x2, g_pre, w_in_t)
    return h, parts


def _load_chunk(j, i, sk_ref, sv_ref, skp_ref, svp_ref):
    rows = slice(j * CHUNK, (j + 1) * CHUNK)
    if j == 0:
        k_prev, v_prev, table = skp_ref[...], svp_ref[...], jnp.where(i > 0, 0, 1)
    else:
        prev = slice((j - 1) * CHUNK, j * CHUNK)
        k_prev, v_prev, table = sk_ref[prev, :], sv_ref[prev, :], 0
    k_pairs = _pair_operands(_swa_variants(jnp.concatenate([k_prev, sk_ref[rows, :]], axis=0)))
    v_pairs = _pair_operands(_swa_variants(jnp.concatenate([v_prev, sv_ref[rows, :]], axis=0)))
    return rows, k_pairs, v_pairs, table


def _tile_constants(ws_ref, bs_ref, sink_ref, mkv_ref):
    wm = _causal_weights(ws_ref)
    bs_rows = [jnp.concatenate([bs_ref[g]] * TILE_CHUNKS, axis=0) for g in range(A_GROUPS)]
    sink_col = jnp.max(jnp.concatenate([jnp.full((CHUNK, 128), sink_ref[0, h], F32) for h in range(4)], axis=0),
                       axis=-1, keepdims=True)
    mkv_v = mkv_ref[0]
    mk_pairs = _pair_operands(_mem_variants(mkv_v[:, :MEM_WIDTH]))
    mv_pairs = _pair_operands(_mem_variants(mkv_v[:, MEM_WIDTH:]))
    return wm, bs_rows, sink_col, mk_pairs, mv_pairs


def _forward_mix(parts, mkv, x2, tgt2, v_g, v_b, w_sp, b_sp, sinks, bias, w_out, g_post, n_ex, seq):
    n_tiles_ex = seq // TILE
    n_tok = n_ex * seq
    au, av, sq, sk, sv, mq, z = parts

    def body(au_ref, av_ref, sq_ref, sk_ref, sv_ref, skp_ref, svp_ref, mq_ref, z_ref, mkv_ref, x_ref, tgt_ref,
             vg_ref, vb_ref, ws_ref, bs_ref, sink_ref, bias_ref, wout_ref, gpost_ref,
             dout_ref, do_ref, loss_ref, dgpost_ref):
        b, i = pl.program_id(0), pl.program_id(1)

        @pl.when((b == 0) & (i == 0))
        def _():
            loss_ref[...] = jnp.zeros_like(loss_ref)
            dgpost_ref[...] = jnp.zeros_like(dgpost_ref)

        wm, bs_rows, sink_col, mk_pairs, mv_pairs = _tile_constants(ws_ref, bs_ref, sink_ref, mkv_ref)
        ya, _ = _group_a_forward(au_ref[...], av_ref[...], vg_ref[...], vb_ref[...], wm, bs_rows)
        yb = []
        for j in range(TILE_CHUNKS):
            rows, k_pairs, v_pairs, table = _load_chunk(j, i, sk_ref, sv_ref, skp_ref, svp_ref)
            p, _ = _attention_probs(_halves_bf16(sq_ref[rows, :]), k_pairs, bias_ref[table], sink_col)
            yb.append(_attention_out(p, v_pairs, CHUNK)[0])
        pm, _ = _attention_probs(_halves_bf16(mq_ref[...]), mk_pairs, None, None)
        yc = _attention_out(pm, mv_pairs, TILE)[0]
        ycat = jnp.concatenate(ya + [jnp.concatenate(yb, axis=0), yc], axis=-1)
        zv = z_ref[...]
        y = ycat * (zv * _sigmoid(zv))
        o = _mm(y.astype(BF16), wout_ref[...])
        r2 = lax.rsqrt(jnp.mean(o * o, axis=-1, keepdims=True) + EPS)
        nrm = o * r2
        gp = gpost_ref[...]
        diff = x_ref[...] + nrm * gp - tgt_ref[...]
        loss_ref[...] += jnp.sum(diff * diff) * (0.5 / D_MODEL)
        dout = diff * (1.0 / D_MODEL)
        dout_ref[...] = dout
        dgpost_ref[...] += jnp.sum(dout * nrm, axis=0, keepdims=True)
        dn = dout * gp
        do_ref[...] = r2 * (dn - nrm * jnp.mean(dn * nrm, axis=-1, keepdims=True))

    tile = functools.partial(_tile_specs, n_tiles_ex)
    prev = functools.partial(_prev_chunk_spec, n_tiles_ex)
    return pl.pallas_call(
        body, name="forward_mix", grid=(n_ex, n_tiles_ex),
        out_shape=[jax.ShapeDtypeStruct((n_tok, D_MODEL), F32), jax.ShapeDtypeStruct((n_tok, D_MODEL), F32),
                   jax.ShapeDtypeStruct((1, 128), F32), jax.ShapeDtypeStruct((1, D_MODEL), F32)],
        in_specs=[tile(A_WIDTH), tile(A_WIDTH), tile(SWA_WIDTH), tile(KV_WIDTH), tile(KV_WIDTH),
                  prev(KV_WIDTH), prev(KV_WIDTH), tile(MEM_WIDTH), tile(MIX_WIDTH),
                  pl.BlockSpec((1, MEM_LEN, 2 * MEM_WIDTH), lambda b, i: (b, 0, 0)),
                  tile(D_MODEL), tile(D_MODEL),
                  _full_spec((1, A_WIDTH)), _full_spec((1, A_WIDTH)), _full_spec((A_GROUPS, CHUNK, CHUNK)),
                  _full_spec((A_GROUPS, CHUNK, CHUNK)), SMEM_SPEC, _full_spec((2, 4 * CHUNK, 2 * CHUNK)),
                  _full_spec((MIX_WIDTH, D_MODEL)), _full_spec((1, D_MODEL))],
        out_specs=[tile(D_MODEL), tile(D_MODEL), _full_spec((1, 128)), _full_spec((1, D_MODEL))],
        compiler_params=pltpu.CompilerParams(vmem_limit_bytes=VMEM_LIMIT),
    )(au, av, sq, sk, sv, sk, sv, mq, z, mkv, x2, tgt2, v_g, v_b, w_sp, b_sp, sinks, bias, w_out, g_post)


def _backward_mix(parts, mkv, do, v_g, v_b, w_sp, b_sp, sinks, bias, w_out, n_ex, seq):
    n_tiles_ex = seq // TILE
    n_tok = n_ex * seq
    au, av, sq, sk, sv, mq, z = parts
    col = dict(zip(("au", "av", "sq", "sk", "sv", "mq", "z"),
                   (slice(PROJ_OFFSETS[k], PROJ_OFFSETS[k + 1]) for k in range(len(PROJ_WIDTHS)))))
    before_kv, after_kv = slice(0, col["sk"].start), slice(col["sv"].stop, IN_WIDTH)

    def body(do_ref, au_ref, av_ref, sq_ref, sk_ref, sv_ref, skp_ref, svp_ref, mq_ref, z_ref, mkv_ref,
             vg_ref, vb_ref, ws_ref, bs_ref, sink_ref, bias_ref, wout_ref,
             dproj_ref, dmkv_ref, dwout_ref, dvg_ref, dvb_ref, dws_ref, dbs_ref, dsink_ref, drel_ref,
             carry_dp, carry_k, carry_v):
        b, i = pl.program_id(0), pl.program_id(1)

        @pl.when((b == 0) & (i == 0))
        def _():
            for ref in (dwout_ref, dvg_ref, dvb_ref, dws_ref, dbs_ref, dsink_ref, drel_ref):
                ref[...] = jnp.zeros_like(ref)

        @pl.when(i == 0)
        def _():
            dmkv_ref[...] = jnp.zeros_like(dmkv_ref)
            carry_k[...] = jnp.zeros_like(carry_k)
            carry_v[...] = jnp.zeros_like(carry_v)

        @pl.when(i > 0)
        def _():
            dproj_ref[:, before_kv] = carry_dp[:, before_kv]
            dproj_ref[:, after_kv] = carry_dp[:, after_kv]

        @pl.when(i < n_tiles_ex)
        def _():
            wm, bs_rows, sink_col, mk_pairs, mv_pairs = _tile_constants(ws_ref, bs_ref, sink_ref, mkv_ref)
            vg = vg_ref[...]
            do_b = do_ref[...].astype(BF16)
            dy = _mm_nt(do_b, wout_ref[...])
            zv = z_ref[...]
            sig = _sigmoid(zv)
            sz = zv * sig
            dyc = dy * sz

            au_v, av_v = au_ref[...], av_ref[...]
            ya, res = _group_a_forward(au_v, av_v, vg, vb_ref[...], wm, bs_rows)
            dgu, dgv = [], []
            for g in range(A_GROUPS):
                sl = slice(g * 128, (g + 1) * 128)
                xhat, rstd, vn, s = res["groups"][g]
                dya = dyc[:, sl]
                dgu.append(dya * s)
                ds = dya * res["gu"][:, sl]
                dbs_ref[:, sl] += sum(ds[c * CHUNK:(c + 1) * CHUNK] for c in range(TILE_CHUNKS))
                ds_b = _rows_to_lanes(ds.astype(BF16), TILE_CHUNKS)
                dws_ref[g] += _mm_nt(ds_b, vn)
                dvn = _lanes_to_rows(_mm_tn(wm[g], ds_b), TILE_CHUNKS)
                dvg_ref[:, sl] += jnp.sum(dvn * xhat, axis=0, keepdims=True)
                dvb_ref[:, sl] += jnp.sum(dvn, axis=0, keepdims=True)
                dxh = dvn * vg[:, sl]
                dgv.append(rstd * (dxh - jnp.mean(dxh, axis=-1, keepdims=True)
                                   - xhat * jnp.mean(dxh * xhat, axis=-1, keepdims=True)))
            carry_dp[:, col["au"]] = (jnp.concatenate(dgu, axis=-1) * _gelu_grad(au_v, res["tu"])).astype(BF16)
            carry_dp[:, col["av"]] = (jnp.concatenate(dgv, axis=-1) * _gelu_grad(av_v, res["tv"])).astype(BF16)

            lane4 = lax.broadcasted_iota(jnp.int32, (1, 128), 1)
            dsink_vec = jnp.zeros((1, 128), F32)
            yb, dk_parts, dv_parts = [], [], []
            for j in range(TILE_CHUNKS):
                rows, k_pairs, v_pairs, table = _load_chunk(j, i, sk_ref, sv_ref, skp_ref, svp_ref)
                qp = _halves_bf16(sq_ref[rows, :])
                p, ps = _attention_probs(qp, k_pairs, bias_ref[table], sink_col)
                out, pp = _attention_out(p, v_pairs, CHUNK)
                yb.append(out)
                do_pairs = _halves_bf16(dyc[rows, A_WIDTH:A_WIDTH + SWA_WIDTH])
                dl, delta, dq, dk, dv = _attention_backward(p, pp, do_pairs, qp, k_pairs, v_pairs, CHUNK)
                sink_terms = ps * delta
                for h in range(4):
                    dsink_vec = dsink_vec + jnp.where(lane4 == h, -jnp.sum(sink_terms[h * CHUNK:(h + 1) * CHUNK]), 0.0)
                drel_ref[...] += dl
                carry_dp[rows, col["sq"]] = (dq * QK_SCALE).astype(BF16)
                dk_parts.append(_swa_unvariants(*_split_pair_grads(dk)) * QK_SCALE)
                dv_parts.append(_swa_unvariants(*_split_pair_grads(dv)))

            mqp = _halves_bf16(mq_ref[...])
            pm, _ = _attention_probs(mqp, mk_pairs, None, None)
            yc, ppm = _attention_out(pm, mv_pairs, TILE)
            dc_pairs = _halves_bf16(dyc[:, A_WIDTH + SWA_WIDTH:])
            _, _, dmq, dmk, dmv = _attention_backward(pm, ppm, dc_pairs, mqp, mk_pairs, mv_pairs, TILE)
            carry_dp[:, col["mq"]] = (dmq * QK_SCALE).astype(BF16)
            dmkv_ref[0] += jnp.concatenate([_mem_unvariants(*_split_pair_grads(dmk)) * QK_SCALE,
                                            _mem_unvariants(*_split_pair_grads(dmv))], axis=-1)

            ycat = jnp.concatenate(ya + [jnp.concatenate(yb, axis=0), yc], axis=-1)
            dwout_ref[...] += _mm_tn((ycat * sz).astype(BF16), do_b)
            carry_dp[:, col["z"]] = (dy * ycat * (sig * (1.0 + zv * (1.0 - sig)))).astype(BF16)
            dsink_ref[...] += dsink_vec

            for parts_c, carry, cols in ((dk_parts, carry_k, col["sk"]), (dv_parts, carry_v, col["sv"])):
                @pl.when(i > 0)
                def _():
                    dproj_ref[:, cols] = (carry[...] + jnp.concatenate(
                        [jnp.zeros((TILE - CHUNK, KV_WIDTH), F32), parts_c[0][:CHUNK]], axis=0)).astype(BF16)
                new = [parts_c[0][CHUNK:]]
                for j in range(1, TILE_CHUNKS):
                    new[-1] = new[-1] + parts_c[j][:CHUNK]
                    new.append(parts_c[j][CHUNK:])
                carry[...] = jnp.concatenate(new, axis=0)

        @pl.when(i == n_tiles_ex)
        def _():
            dproj_ref[:, col["sk"]] = carry_k[...].astype(BF16)
            dproj_ref[:, col["sv"]] = carry_v[...].astype(BF16)

    tile = functools.partial(_tile_specs, n_tiles_ex)
    prev = functools.partial(_prev_chunk_spec, n_tiles_ex)
    late = pl.BlockSpec((TILE, IN_WIDTH), lambda b, i: (b * n_tiles_ex + jnp.maximum(i - 1, 0), 0))
    return pl.pallas_call(
        body, name="backward_mix", grid=(n_ex, n_tiles_ex + 1),
        out_shape=[jax.ShapeDtypeStruct((n_tok, IN_WIDTH), BF16),
                   jax.ShapeDtypeStruct((n_ex, MEM_LEN, 2 * MEM_WIDTH), F32),
                   jax.ShapeDtypeStruct((MIX_WIDTH, D_MODEL), F32), jax.ShapeDtypeStruct((1, A_WIDTH), F32),
                   jax.ShapeDtypeStruct((1, A_WIDTH), F32), jax.ShapeDtypeStruct((A_GROUPS, CHUNK, CHUNK), F32),
                   jax.ShapeDtypeStruct((CHUNK, A_WIDTH), F32), jax.ShapeDtypeStruct((1, 128), F32),
                   jax.ShapeDtypeStruct((4 * CHUNK, 2 * CHUNK), F32)],
        in_specs=[tile(D_MODEL), tile(A_WIDTH), tile(A_WIDTH), tile(SWA_WIDTH), tile(KV_WIDTH), tile(KV_WIDTH),
                  prev(KV_WIDTH), prev(KV_WIDTH), tile(MEM_WIDTH), tile(MIX_WIDTH),
                  pl.BlockSpec((1, MEM_LEN, 2 * MEM_WIDTH), lambda b, i: (b, 0, 0)),
                  _full_spec((1, A_WIDTH)), _full_spec((1, A_WIDTH)), _full_spec((A_GROUPS, CHUNK, CHUNK)),
                  _full_spec((A_GROUPS, CHUNK, CHUNK)), SMEM_SPEC, _full_spec((2, 4 * CHUNK, 2 * CHUNK)),
                  _full_spec((MIX_WIDTH, D_MODEL))],
        out_specs=[late, pl.BlockSpec((1, MEM_LEN, 2 * MEM_WIDTH), lambda b, i: (b, 0, 0)),
                   _full_spec((MIX_WIDTH, D_MODEL)), _full_spec((1, A_WIDTH)), _full_spec((1, A_WIDTH)),
                   _full_spec((A_GROUPS, CHUNK, CHUNK)), _full_spec((CHUNK, A_WIDTH)), _full_spec((1, 128)),
                   _full_spec((4 * CHUNK, 2 * CHUNK))],
        scratch_shapes=[pltpu.VMEM((TILE, IN_WIDTH), BF16), pltpu.VMEM((TILE, KV_WIDTH), F32),
                        pltpu.VMEM((TILE, KV_WIDTH), F32)],
        compiler_params=pltpu.CompilerParams(vmem_limit_bytes=VMEM_LIMIT),
    )(do, au, av, sq, sk, sv, sk, sv, mq, z, mkv, v_g, v_b, w_sp, b_sp, sinks, bias, w_out)


BWD_PROJ_TILE = 512


def _backward_projection(x2, dout, dproj, g_pre, w_in_t):
    n_tok = x2.shape[0]
    n_steps = n_tok // BWD_PROJ_TILE

    def body(x_ref, dout_ref, dp_ref, g_ref, w_hbm, dx_ref, dgpre_ref, w_vmem, sem):
        @pl.when(pl.program_id(0) == 0)
        def _():
            load = pltpu.make_async_copy(w_hbm, w_vmem, sem)
            load.start()
            dgpre_ref[...] = jnp.zeros_like(dgpre_ref)
            load.wait()

        xv = x_ref[...]
        r = lax.rsqrt(jnp.mean(xv * xv, axis=-1, keepdims=True) + EPS)
        xn = xv * r
        dh = _mm(dp_ref[...], w_vmem[...])
        dgpre_ref[...] += jnp.sum(dh * xn, axis=0, keepdims=True)
        dhg = dh * g_ref[...]
        dx_ref[...] = r * (dhg - xn * jnp.mean(dhg * xn, axis=-1, keepdims=True)) + dout_ref[...]

    row = lambda w: pl.BlockSpec((BWD_PROJ_TILE, w), lambda i: (i, 0))
    return pl.pallas_call(
        body, name="backward_projection", grid=(n_steps,),
        out_shape=[jax.ShapeDtypeStruct((n_tok, D_MODEL), F32), jax.ShapeDtypeStruct((1, D_MODEL), F32)],
        in_specs=[row(D_MODEL), row(D_MODEL), row(IN_WIDTH), _full_spec((1, D_MODEL)), ANY_SPEC],
        out_specs=[row(D_MODEL), _full_spec((1, D_MODEL))],
        scratch_shapes=[pltpu.VMEM((IN_WIDTH, D_MODEL), BF16), pltpu.SemaphoreType.DMA],
        input_output_aliases={1: 0},
        compiler_params=pltpu.CompilerParams(vmem_limit_bytes=VMEM_LIMIT),
    )(x2, dout, dproj, g_pre, w_in_t)


SHARD_ROWS = IN_WIDTH // N_CHIPS
SHARD_WINDOW = 768
SHARD_HALF = SHARD_ROWS // 2
DWIN_TILE = 2048


def _shard_window_start(shard):
    return (shard * SHARD_ROWS // 128) * 128


def _dwin_reduce(dproj, h, shard_arr):
    n_tok = h.shape[0]
    tile = min(DWIN_TILE, n_tok)
    n_sub = n_tok // tile
    last = N_CHIPS - 1

    def shard_of_slot(s, my_shard):
        return my_shard ^ ((s + 1) % N_CHIPS)

    def body(shard_ref, dp_ref, h_ref, out_hbm, part, recv_d2d, send_ici, recv_ici, mine_buf, other_buf,
             send_sems, recv_sems, local_sems):
        s, t = pl.program_id(0), pl.program_id(1)
        x, y, c = lax.axis_index("x"), lax.axis_index("y"), lax.axis_index("c")
        sibling = (x, y, 1 - c)
        my_rows = pl.ds(pl.multiple_of(c * SHARD_HALF, 8), SHARD_HALF)
        other_rows = pl.ds(pl.multiple_of((1 - c) * SHARD_HALF, 8), SHARD_HALF)

        def to_sibling(k):
            return pltpu.make_async_remote_copy(
                src_ref=part.at[k % 2, other_rows, :], dst_ref=recv_d2d.at[k], send_sem=send_sems.at[k],
                recv_sem=recv_sems.at[k], device_id=sibling, device_id_type=MESH)

        def to_chip(k):
            rel = k + 1
            return pltpu.make_async_remote_copy(
                src_ref=send_ici.at[k], dst_ref=recv_ici.at[k], send_sem=send_sems.at[N_CHIPS + k],
                recv_sem=recv_sems.at[N_CHIPS + k], device_id=(x ^ (rel >> 1), y ^ (rel & 1), c),
                device_id_type=MESH)

        swap = pltpu.make_async_remote_copy(
            src_ref=mine_buf, dst_ref=other_buf, send_sem=send_sems.at[2 * N_CHIPS - 1],
            recv_sem=recv_sems.at[2 * N_CHIPS - 1], device_id=sibling, device_id_type=MESH)

        @pl.when((s > 0) & (t == 0))
        def _():
            k = s - 1
            cp = to_sibling(k)
            cp.wait_recv()
            cp.wait_send()
            send_ici[k] = (part[k % 2, my_rows, :] + recv_d2d[k]).astype(BF16)
            to_chip(k).start()

        r = _mm_tn(dp_ref[...], h_ref[...])
        odd = shard_of_slot(s, shard_ref[0]) % 2
        for parity in range(2):
            rows = r[64 * parity:64 * parity + SHARD_ROWS]

            @pl.when((odd == parity) & (t == 0))
            def _():
                part[s % 2] = rows

            @pl.when((odd == parity) & (t > 0))
            def _():
                part[s % 2] += rows

        @pl.when(t == n_sub - 1)
        def _():
            to_sibling(s).start()

        @pl.when((s == last) & (t == n_sub - 1))
        def _():
            cp = to_sibling(last)
            cp.wait_recv()
            cp.wait_send()
            total = part[last % 2, my_rows, :] + recv_d2d[last]
            for k in range(last):
                to_chip(k).wait_recv()
                total = total + recv_ici[k].astype(F32)
            mine_buf[...] = total
            swap.start()
            out_mine = pltpu.make_async_copy(mine_buf, out_hbm.at[my_rows, :], local_sems.at[0])
            out_mine.start()
            swap.wait_recv()
            out_other = pltpu.make_async_copy(other_buf, out_hbm.at[other_rows, :], local_sems.at[1])
            out_other.start()
            for k in range(last):
                to_chip(k).wait_send()
            swap.wait_send()
            out_mine.wait()
            out_other.wait()

    half = (SHARD_HALF, D_MODEL)
    return pl.pallas_call(
        body, name="dwin_reduce",
        out_shape=jax.ShapeDtypeStruct((SHARD_ROWS, D_MODEL), F32),
        grid_spec=pltpu.PrefetchScalarGridSpec(
            num_scalar_prefetch=1, grid=(N_CHIPS, n_sub),
            in_specs=[pl.BlockSpec((pl.Element(tile), pl.Element(SHARD_WINDOW)),
                                   lambda s, t, m: (t * tile, _shard_window_start(shard_of_slot(s, m[0])))),
                      pl.BlockSpec((tile, D_MODEL), lambda s, t, m: (t, 0))],
            out_specs=ANY_SPEC,
            scratch_shapes=[pltpu.VMEM((2, SHARD_ROWS, D_MODEL), F32), pltpu.VMEM((N_CHIPS,) + half, F32),
                            pltpu.VMEM((last,) + half, BF16), pltpu.VMEM((last,) + half, BF16),
                            pltpu.VMEM(half, F32), pltpu.VMEM(half, F32),
                            pltpu.SemaphoreType.DMA((2 * N_CHIPS,)), pltpu.SemaphoreType.DMA((2 * N_CHIPS,)),
                            pltpu.SemaphoreType.DMA((2,))]),
        compiler_params=pltpu.CompilerParams(vmem_limit_bytes=VMEM_LIMIT),
    )(shard_arr, dproj, h)


def _memkv_backward(mem, dmkv, g_mem, w_mkv):
    n_ex = mem.shape[0]

    def body(mem_ref, d_ref, g_ref, w_ref, dw_ref, dg_ref):
        @pl.when(pl.program_id(0) == 0)
        def _():
            dw_ref[...] = jnp.zeros_like(dw_ref)
            dg_ref[...] = jnp.zeros_like(dg_ref)

        m = mem_ref[0]
        mn = m * lax.rsqrt(jnp.mean(m * m, axis=-1, keepdims=True) + EPS)
        d_b = d_ref[0].astype(BF16)
        dw_ref[...] += _mm_tn((mn * g_ref[...]).astype(BF16), d_b)
        dg_ref[...] += jnp.sum(_mm_nt(d_b, w_ref[...]) * mn, axis=0, keepdims=True)

    return pl.pallas_call(
        body, name="memkv_backward", grid=(n_ex,),
        out_shape=[jax.ShapeDtypeStruct((D_MODEL, 2 * MEM_WIDTH), F32), jax.ShapeDtypeStruct((1, D_MODEL), F32)],
        in_specs=[pl.BlockSpec((1, MEM_LEN, D_MODEL), lambda b: (b, 0, 0)),
                  pl.BlockSpec((1, MEM_LEN, 2 * MEM_WIDTH), lambda b: (b, 0, 0)),
                  _full_spec((1, D_MODEL)), _full_spec((D_MODEL, 2 * MEM_WIDTH))],
        out_specs=[_full_spec((D_MODEL, 2 * MEM_WIDTH)), _full_spec((1, D_MODEL))],
    )(mem, dmkv, g_mem, w_mkv)


def _pack_small_grads(dgpre, dgpost, dgmem, dvg, dvb, dws, dbs, dsink, drel, buckets):
    def body(dgpre_ref, dgpost_ref, dgmem_ref, dvg_ref, dvb_ref, dws_ref, dbs_ref, dsink_ref, drel_ref, bk_ref,
             a_ref, b_ref):
        a_ref[...] = jnp.zeros_like(a_ref)
        b_ref[...] = jnp.zeros_like(b_ref)
        a_ref[0:1, :] = dgpre_ref[...]
        a_ref[1:2, :] = dgpost_ref[...]
        a_ref[2:3, :] = dgmem_ref[...]
        a_ref[3:4, :] = jnp.concatenate([dvg_ref[...], dvb_ref[...]], axis=-1)
        row = lax.broadcasted_iota(jnp.int32, (CHUNK, CHUNK), 0)
        col = lax.broadcasted_iota(jnp.int32, (CHUNK, CHUNK), 1)
        for g in range(A_GROUPS):
            b_ref[ROW_WS + g * CHUNK:ROW_WS + (g + 1) * CHUNK, :] = jnp.where(row >= col, dws_ref[g], 0.0)
            by_token = jnp.transpose(dbs_ref[:, g * 128:(g + 1) * 128])
            b_ref[ROW_BS + g:ROW_BS + g + 1, :] = jnp.sum(by_token, axis=0, keepdims=True)
        b_ref[ROW_SINK:ROW_SINK + 1, :] = dsink_ref[...]
        bk = bk_ref[...]
        rel_row = lax.broadcasted_iota(jnp.int32, (8, 128), 0)
        rel_col = lax.broadcasted_iota(jnp.int32, (8, 128), 1)
        rel = jnp.zeros((8, 128), F32)
        for h in range(4):
            acc = drel_ref[h * CHUNK:(h + 1) * CHUNK, :]
            for b in range(N_BUCKETS):
                rel = jnp.where((rel_row == h) & (rel_col == b), jnp.sum(jnp.where(bk == b, acc, 0.0)), rel)
        b_ref[ROW_REL:ROW_REL + 8, :] = rel

    return pl.pallas_call(
        body, name="pack_small_grads",
        out_shape=[jax.ShapeDtypeStruct((SMALL_A_ROWS, D_MODEL), F32), jax.ShapeDtypeStruct((SMALL_B_ROWS, 128), F32)],
        in_specs=[VMEM_SPEC] * 10, out_specs=[VMEM_SPEC] * 2,
    )(dgpre, dgpost, dgmem, dvg, dvb, dws, dbs, dsink, drel, buckets)


def _exchange_siblings(big, small_a, small_b):
    n = len(big)

    def body(*refs):
        srcs, dsts, (send_sems, recv_sems) = refs[:n + 2], refs[n + 2:2 * n + 4], refs[2 * n + 4:]
        x, y, c = lax.axis_index("x"), lax.axis_index("y"), lax.axis_index("c")
        pairs = [(g.at[:, pl.ds(1 - c, 1)], r) for g, r in zip(srcs[:n], dsts[:n])]
        pairs += list(zip(srcs[n:], dsts[n:]))
        copies = [pltpu.make_async_remote_copy(src_ref=src, dst_ref=dst, send_sem=send_sems.at[k],
                                               recv_sem=recv_sems.at[k], device_id=(x, y, 1 - c),
                                               device_id_type=MESH)
                  for k, (src, dst) in enumerate(pairs)]
        for cp in copies:
            cp.start()
        for cp in copies:
            cp.wait_recv()
        for cp in copies:
            cp.wait_send()

    out_shape = [jax.ShapeDtypeStruct((g.shape[0], 1) + g.shape[2:], F32) for g in big]
    out_shape += [jax.ShapeDtypeStruct(small_a.shape, F32), jax.ShapeDtypeStruct(small_b.shape, F32)]
    return pl.pallas_call(
        body, name="exchange_siblings", out_shape=out_shape,
        in_specs=[ANY_SPEC] * (n + 2), out_specs=[ANY_SPEC] * (n + 2),
        scratch_shapes=[pltpu.SemaphoreType.DMA((n + 2,)), pltpu.SemaphoreType.DMA((n + 2,))],
    )(*big, small_a, small_b)


def _chip_sum(big, recv, small, small_recv, c_arr):
    n = len(big)

    def body(c_ref, *refs):
        g, r, (sa, sb, ra, rb) = refs[:n], refs[n:2 * n], refs[2 * n:2 * n + 4]
        p, (ca, cb) = refs[2 * n + 4:3 * n + 4], refs[3 * n + 4:]
        for w in range(n):
            p[w][...] = (g[w][...] + r[w][...]).astype(BF16)

        @pl.when(pl.program_id(0) == 0)
        def _():
            ca[...] = sa[...] + ra[...]
            cb[...] = sb[...] + rb[...]

    def own(g):
        return pl.BlockSpec((1, 1) + g.shape[2:], lambda j, c_ref: (j, c_ref[0], 0, 0))

    def got(g):
        return pl.BlockSpec((1, 1) + g.shape[2:], lambda j, c_ref: (j, 0, 0, 0))

    def whole(a):
        return pl.BlockSpec(a.shape, lambda j, c_ref: (0, 0))

    return pl.pallas_call(
        body, name="chip_sum",
        out_shape=[jax.ShapeDtypeStruct(r.shape, BF16) for r in recv]
        + [jax.ShapeDtypeStruct(a.shape, F32) for a in small],
        grid_spec=pltpu.PrefetchScalarGridSpec(
            num_scalar_prefetch=1, grid=(N_CHIPS,),
            in_specs=[own(g) for g in big] + [got(g) for g in big] + [whole(a) for a in small + small_recv],
            out_specs=[got(g) for g in big] + [whole(a) for a in small]),
        compiler_params=pltpu.CompilerParams(vmem_limit_bytes=VMEM_LIMIT),
    )(c_arr, *big, *recv, *small, *small_recv)


def _exchange_chips(partials, small):
    n = len(partials)

    def body(*refs):
        p, (ca, cb), r, (ga, gb) = refs[:n], refs[n:n + 2], refs[n + 2:2 * n + 2], refs[2 * n + 2:2 * n + 4]
        send_sems, recv_sems, local_sems = refs[2 * n + 4:]
        x, y, c = lax.axis_index("x"), lax.axis_index("y"), lax.axis_index("c")
        chips = [(1 - x, y), (x, 1 - y), (1 - x, 1 - y)]
        my_chip = 2 * x + y
        own = [pltpu.make_async_copy(ca, ga.at[my_chip], local_sems.at[0]),
               pltpu.make_async_copy(cb, gb.at[my_chip], local_sems.at[1])]
        for cp in own:
            cp.start()
        copies = []
        for j, chip in enumerate(chips):
            pairs = [(p[w].at[2 * chip[0] + chip[1]], r[w].at[j]) for w in range(n)]
            pairs += [(ca, ga.at[my_chip]), (cb, gb.at[my_chip])]
            for w, (src, dst) in enumerate(pairs):
                k = (n + 2) * j + w
                copies.append(pltpu.make_async_remote_copy(
                    src_ref=src, dst_ref=dst, send_sem=send_sems.at[k], recv_sem=recv_sems.at[k],
                    device_id=(*chip, c), device_id_type=MESH))
        for cp in copies:
            cp.start()
        for cp in copies:
            cp.wait_recv()
        for cp in copies:
            cp.wait_send()
        for cp in own:
            cp.wait()

    return pl.pallas_call(
        body, name="exchange_chips",
        out_shape=[jax.ShapeDtypeStruct((3,) + p.shape[1:], BF16) for p in partials]
        + [jax.ShapeDtypeStruct((N_CHIPS,) + a.shape, F32) for a in small],
        in_specs=[ANY_SPEC] * (n + 2), out_specs=[ANY_SPEC] * (n + 2),
        scratch_shapes=[pltpu.SemaphoreType.DMA((3 * (n + 2),)), pltpu.SemaphoreType.DMA((3 * (n + 2),)),
                        pltpu.SemaphoreType.DMA((2,))],
    )(*partials, *small)


def _shard_sum(partials, recv, shard_arr):
    n = len(partials)

    def body(s_ref, *refs):
        for p, r, o in zip(refs[:n], refs[n:2 * n], refs[2 * n:]):
            o[...] = ((p[0, 0].astype(F32) + r[0, 0].astype(F32)) + r[1, 0].astype(F32)) + r[2, 0].astype(F32)

    def own(p):
        return pl.BlockSpec((1,) + p.shape[1:], lambda i, s_ref: (s_ref[0], 0, 0, 0))

    def got(p):
        return pl.BlockSpec((3,) + p.shape[1:], lambda i, s_ref: (0, 0, 0, 0))

    return pl.pallas_call(
        body, name="shard_sum",
        out_shape=[jax.ShapeDtypeStruct(p.shape[2:], F32) for p in partials],
        grid_spec=pltpu.PrefetchScalarGridSpec(
            num_scalar_prefetch=1, grid=(1,),
            in_specs=[own(p) for p in partials] + [got(p) for p in partials],
            out_specs=[pl.BlockSpec(p.shape[2:], lambda i, s_ref: (0, 0)) for p in partials]),
        compiler_params=pltpu.CompilerParams(vmem_limit_bytes=VMEM_LIMIT),
    )(shard_arr, *partials, *recv)


def _exchange_halves(halves):
    n = len(halves)

    def body(*refs):
        send_sems, recv_sems = refs[2 * n:]
        x, y, c = lax.axis_index("x"), lax.axis_index("y"), lax.axis_index("c")
        copies = [pltpu.make_async_remote_copy(src_ref=h, dst_ref=r, send_sem=send_sems.at[w],
                                               recv_sem=recv_sems.at[w], device_id=(x, y, 1 - c),
                                               device_id_type=MESH)
                  for w, (h, r) in enumerate(zip(refs[:n], refs[n:2 * n]))]
        for cp in copies:
            cp.start()
        for cp in copies:
            cp.wait_recv()
        for cp in copies:
            cp.wait_send()

    return pl.pallas_call(
        body, name="exchange_halves",
        out_shape=[jax.ShapeDtypeStruct(h.shape, F32) for h in halves],
        in_specs=[ANY_SPEC] * n, out_specs=[ANY_SPEC] * n,
        scratch_shapes=[pltpu.SemaphoreType.DMA((n,)), pltpu.SemaphoreType.DMA((n,))],
    )(*halves)


def _adamw(w, g, m, v):
    m2 = ADAM_B1 * m + (1.0 - ADAM_B1) * g
    v2 = ADAM_B2 * v + (1.0 - ADAM_B2) * (g * g)
    m_hat = m2 / (1.0 - ADAM_B1 ** ADAM_STEP)
    v_hat = v2 / (1.0 - ADAM_B2 ** ADAM_STEP)
    delta = -ADAM_LR * (m_hat / (jnp.sqrt(v_hat) + ADAM_EPS) + ADAM_WD * w)
    return delta, m2, v2


ADAM_MAX_ROWS = 176


def _adamw_sharded(mine, other, w, m, v, c_arr, name):
    rows, cols = w.shape
    half = rows // 2
    steps = -(-half // ADAM_MAX_ROWS)
    block_rows = half // steps
    assert block_rows * steps == half and block_rows % 8 == 0

    def body(c_ref, mine_ref, other_ref, w_ref, m_ref, v_ref, g_out, d_out, m_out, v_out):
        g = jnp.where(pl.program_id(0) == c_ref[0], mine_ref[...], other_ref[...])
        delta, m2, v2 = _adamw(w_ref[...], g, m_ref[...], v_ref[...])
        g_out[...] = g
        d_out[...] = delta
        m_out[...] = m2
        v_out[...] = v2

    part = pl.BlockSpec((block_rows, cols), lambda h, k, c_ref: (k, 0))
    full = pl.BlockSpec((block_rows, cols), lambda h, k, c_ref: (h * steps + k, 0))
    return pl.pallas_call(
        body, name=name, out_shape=[jax.ShapeDtypeStruct((rows, cols), F32)] * 4,
        grid_spec=pltpu.PrefetchScalarGridSpec(
            num_scalar_prefetch=1, grid=(2, steps), in_specs=[part, part, full, full, full], out_specs=[full] * 4),
    )(c_arr, mine, other, w, m, v)


def _adamw_whole(g, w, m, v, name):
    rows, cols = w.shape
    steps = -(-rows // ADAM_MAX_ROWS)
    block_rows = rows // steps
    assert block_rows * steps == rows and block_rows % 8 == 0

    def body(g_ref, w_ref, m_ref, v_ref, d_out, m_out, v_out):
        delta, m2, v2 = _adamw(w_ref[...], g_ref[...], m_ref[...], v_ref[...])
        d_out[...] = delta
        m_out[...] = m2
        v_out[...] = v2

    block = pl.BlockSpec((block_rows, cols), lambda k: (k, 0))
    out = pl.pallas_call(
        body, name=name, grid=(steps,), out_shape=[jax.ShapeDtypeStruct((rows, cols), F32)] * 3,
        in_specs=[block] * 4, out_specs=[block] * 3,
    )(g, w, m, v)
    return [g] + list(out)


def _adamw_small(ra, rb, weights, moments_m, moments_v):
    n = len(weights)

    def body(*refs):
        ra_ref, rb_ref = refs[0], refs[1]
        w_refs, m_refs, v_refs = refs[2:2 + n], refs[2 + n:2 + 2 * n], refs[2 + 2 * n:2 + 3 * n]
        outs = refs[2 + 3 * n:]
        g_outs, d_outs, m_outs, v_outs = outs[:n], outs[n:2 * n], outs[2 * n:3 * n], outs[3 * n:]
        ga, gb = ra_ref[0], rb_ref[0]
        for chip in range(1, N_CHIPS):
            ga = ga + ra_ref[chip]
            gb = gb + rb_ref[chip]
        grads = [ga[0:1, :], ga[1:2, :], ga[2:3, :], ga[3:4, :A_WIDTH], ga[3:4, A_WIDTH:],
                 gb[ROW_WS:ROW_WS + A_GROUPS * CHUNK, :].reshape(A_GROUPS, CHUNK, CHUNK),
                 gb[ROW_BS:ROW_BS + A_GROUPS, :], gb[ROW_SINK:ROW_SINK + 1, 0:4],
                 gb[ROW_REL:ROW_REL + 4, 0:N_BUCKETS]]
        for k in range(n):
            delta, m2, v2 = _adamw(w_refs[k][...], grads[k], m_refs[k][...], v_refs[k][...])
            g_outs[k][...] = grads[k]
            d_outs[k][...] = delta
            m_outs[k][...] = m2
            v_outs[k][...] = v2

    out_shape = [jax.ShapeDtypeStruct(w.shape, F32) for w in weights] * 4
    return pl.pallas_call(
        body, name="adamw_small", out_shape=out_shape,
        in_specs=[VMEM_SPEC] * (2 + 3 * n), out_specs=[VMEM_SPEC] * (4 * n),
    )(ra, rb, *weights, *moments_m, *moments_v)


def kernel(x, mem, pre_norm_g, post_norm_g, mem_norm_g, w_in, w_mem_kv, v_norm_g, v_norm_b, w_spatial, b_spatial, attn_sinks, rel_bias, w_out, loss_target, m_pre_norm_g, m_post_norm_g, m_mem_norm_g, m_w_in, m_w_mem_kv, m_v_norm_g, m_v_norm_b, m_w_spatial, m_b_spatial, m_attn_sinks, m_rel_bias, m_w_out, v_pre_norm_g, v_post_norm_g, v_mem_norm_g, v_w_in, v_w_mem_kv, v_v_norm_g, v_v_norm_b, v_w_spatial, v_b_spatial, v_attn_sinks, v_rel_bias, v_w_out):
    n_ex, seq, _ = x.shape
    n_tok = n_ex * seq
    x2 = x.reshape(n_tok, D_MODEL)
    tgt2 = loss_target.reshape(n_tok, D_MODEL)
    buckets = jnp.asarray(_bucket_map())
    c_arr = lax.axis_index("c").astype(jnp.int32).reshape(1)
    shard_arr = (2 * lax.axis_index("x") + lax.axis_index("y")).astype(jnp.int32).reshape(1)
    w_sp = w_spatial[0]
    b_sp = jnp.broadcast_to(b_spatial[0][:, :, None], (A_GROUPS, CHUNK, CHUNK))
    w_in_t, m_w_in_t, v_w_in_t = (jnp.transpose(a[0]) for a in (w_in, m_w_in, v_w_in))
    rel_t, m_rel_t, v_rel_t = (jnp.transpose(a) for a in (rel_bias, m_rel_bias, v_rel_bias))

    g_in, g_mkv, g_out = _gather_weights(w_in_t, w_mem_kv[0], w_out[0])
    w_in_b = g_in.reshape(IN_WIDTH, D_MODEL)
    w_mkv_b = g_mkv.reshape(D_MODEL, 2 * MEM_WIDTH)
    w_out_b = g_out.reshape(MIX_WIDTH, D_MODEL)

    bias = _make_bias(rel_t, buckets)
    mkv = _memkv_forward(mem, mem_norm_g, w_mkv_b)
    h_b, parts = _forward_projection(x2, pre_norm_g, w_in_b)
    dout, do, loss_vec, dgpost = _forward_mix(parts, mkv, x2, tgt2, v_norm_g, v_norm_b, w_sp, b_sp, attn_sinks, bias,
                                             w_out_b, post_norm_g, n_ex, seq)

    dproj, dmkv, dwout, dvg, dvb, dws, dbs, dsink, drel = _backward_mix(
        parts, mkv, do, v_norm_g, v_norm_b, w_sp, b_sp, attn_sinks, bias, w_out_b, n_ex, seq)
    dx, dgpre = _backward_projection(x2, dout, dproj, pre_norm_g, w_in_b)
    g_win = _dwin_reduce(dproj, h_b, shard_arr)
    dwmkv, dgmem = _memkv_backward(mem, dmkv, mem_norm_g, w_mkv_b)
    small_a, small_b = _pack_small_grads(dgpre, dgpost, dgmem, dvg, dvb, dws, dbs, dsink, drel, buckets)

    shard_shapes = [w_mem_kv.shape[1:], w_out.shape[1:]]
    big = [g.reshape(N_CHIPS, 2, s[0] // 2, s[1]) for g, s in zip((dwmkv, dwout), shard_shapes)]
    *recv, ra, rb = _exchange_siblings(big, small_a, small_b)
    *partials, ca, cb = _chip_sum(big, recv, [small_a, small_b], [ra, rb], c_arr)
    *recv2, ga, gb = _exchange_chips(partials, [ca, cb])
    mine = _shard_sum(partials, recv2, shard_arr)
    other = _exchange_halves(mine)

    big_out = [_adamw_whole(g_win, w_in_t, m_w_in_t, v_w_in_t, "adamw_w_in"),
               _adamw_sharded(mine[0], other[0], w_mem_kv[0], m_w_mem_kv[0], v_w_mem_kv[0], c_arr, "adamw_w_mem_kv"),
               _adamw_sharded(mine[1], other[1], w_out[0], m_w_out[0], v_w_out[0], c_arr, "adamw_w_out")]
    small_w = [pre_norm_g, post_norm_g, mem_norm_g, v_norm_g, v_norm_b, w_sp, b_spatial[0], attn_sinks, rel_t]
    small_m = [m_pre_norm_g, m_post_norm_g, m_mem_norm_g, m_v_norm_g, m_v_norm_b, m_w_spatial[0], m_b_spatial[0],
               m_attn_sinks, m_rel_t]
    small_v = [v_pre_norm_g, v_post_norm_g, v_mem_norm_g, v_v_norm_g, v_v_norm_b, v_w_spatial[0], v_b_spatial[0],
               v_attn_sinks, v_rel_t]
    small_out = _adamw_small(ga, gb, small_w, small_m, small_v)
    n_small = len(small_w)

    loss = lax.psum(loss_vec[0, 0], ALL_AXES)
    outputs = [loss, dx.reshape(x.shape)]
    for kind in range(4):
        s = small_out[kind * n_small:(kind + 1) * n_small]
        outputs += [s[0], s[1], s[2], jnp.transpose(big_out[0][kind])[None], big_out[1][kind][None], s[3], s[4],
                    s[5][None], s[6][None], s[7], jnp.transpose(s[8]), big_out[2][kind][None]]
    return tuple(outputs)
```

```python
import functools

import numpy as np
import jax
import jax.numpy as jnp
from jax import lax
from jax.experimental import pallas as pl
from jax.experimental.pallas import tpu as pltpu

F32 = jnp.float32
BF16 = jnp.bfloat16
MESH = pl.DeviceIdType.MESH

D_MODEL = 1024
CHUNK = 128
A_WIDTH = 512
A_GROUPS = 4
SWA_WIDTH = 256
KV_WIDTH = 128
MEM_WIDTH = 256
MEM_LEN = 256
MIX_WIDTH = 1024
IN_WIDTH = 2816
N_BUCKETS = 32
MAX_DISTANCE = 128
EPS = 1e-6
NEG = -1e30
QK_SCALE = 0.125
HALF_HEAD_PAIR = 64

ADAM_LR = 0.001
ADAM_B1 = 0.9
ADAM_B2 = 0.999
ADAM_EPS = 1e-08
ADAM_WD = 0.01
ADAM_STEP = 10

N_CHIPS = 4
TILE_CHUNKS = 2
TILE = TILE_CHUNKS * CHUNK
PROJ_TILE = 256
VMEM_LIMIT = 56 * 1024 * 1024

SMALL_A_ROWS = 8
ROW_LOSS = 4
ROW_WS = 0
ROW_BS = 512
ROW_SINK = 520
ROW_REL = 528
SMALL_B_ROWS = 536


def _mm(a, b):
    return lax.dot_general(a, b, (((1,), (0,)), ((), ())), preferred_element_type=F32)


def _mm_nt(a, b):
    return lax.dot_general(a, b, (((1,), (1,)), ((), ())), preferred_element_type=F32)


def _mm_tn(a, b):
    return lax.dot_general(a, b, (((0,), (0,)), ((), ())), preferred_element_type=F32)


def _bucket_map():
    qi = np.arange(CHUNK)[:, None]
    kj = np.arange(2 * CHUNK)[None, :]
    n = np.maximum(qi + CHUNK - kj, 0)
    max_exact = N_BUCKETS // 2
    large = max_exact + (np.log(np.maximum(n, 1) / max_exact) / np.log(MAX_DISTANCE / max_exact)
                         * (N_BUCKETS - max_exact)).astype(np.int32)
    large = np.minimum(large, N_BUCKETS - 1)
    return np.where(n < max_exact, n, large).astype(np.int32)


_GELU_C = 0.7978845608028654
_GELU_A = 0.044715


def _gelu(x):
    t = jnp.tanh(_GELU_C * (x + _GELU_A * x * x * x))
    return 0.5 * x * (1.0 + t), t


def _gelu_grad(x, t):
    return 0.5 * (1.0 + t) + 0.5 * x * (1.0 - t * t) * (_GELU_C * (1.0 + 3.0 * _GELU_A * x * x))


def _sigmoid(x):
    return 1.0 / (1.0 + jnp.exp(-x))


def _lane_lo(shape):
    return lax.broadcasted_iota(jnp.int32, shape, 1) < HALF_HEAD_PAIR


def _swa_variants(t):
    lo = _lane_lo(t.shape)
    tr = pltpu.roll(t, HALF_HEAD_PAIR, 1)
    zero = jnp.zeros_like(t)
    return (jnp.where(lo, t, zero).astype(BF16), jnp.where(lo, zero, tr).astype(BF16),
            jnp.where(lo, tr, zero).astype(BF16), jnp.where(lo, zero, t).astype(BF16))


def _swa_unvariants(d0, d1, d2, d3):
    lo = _lane_lo(d0.shape)
    zero = jnp.zeros_like(d0)
    rolled = jnp.where(lo, zero, d1) + jnp.where(lo, d2, zero)
    return jnp.where(lo, d0, zero) + jnp.where(lo, zero, d3) + pltpu.roll(rolled, HALF_HEAD_PAIR, 1)


def _mem_variants(t):
    out = []
    for pair in range(2):
        tp = t[:, pair * 128:(pair + 1) * 128]
        lo = _lane_lo(tp.shape)
        zero = jnp.zeros_like(tp)
        out.append(jnp.where(lo, tp, zero).astype(BF16))
        out.append(jnp.where(lo, zero, tp).astype(BF16))
    return out


def _mem_unvariants(d0, d1, d2, d3):
    lo = _lane_lo(d0.shape)
    return jnp.concatenate([jnp.where(lo, d0, d1), jnp.where(lo, d2, d3)], axis=-1)


def _softmax(logits, sinks):
    m = jnp.max(logits, axis=-1, keepdims=True)
    if sinks is not None:
        m = jnp.maximum(m, sinks)
    p = jnp.exp(logits - m)
    den = jnp.sum(p, axis=-1, keepdims=True)
    if sinks is None:
        return p * (1.0 / den), None
    es = jnp.exp(sinks - m)
    inv = 1.0 / (den + es)
    return p * inv, es * inv


def _band_valid(with_prev):
    qi = lax.broadcasted_iota(jnp.int32, (CHUNK, 2 * CHUNK), 0)
    kj = lax.broadcasted_iota(jnp.int32, (CHUNK, 2 * CHUNK), 1)
    in_cur = (kj >= CHUNK) & (kj - CHUNK <= qi)
    if not with_prev:
        return in_cur
    return in_cur | ((kj < CHUNK) & (kj > qi))


def _causal_weights(ws_ref):
    row = lax.broadcasted_iota(jnp.int32, (CHUNK, CHUNK), 0)
    col = lax.broadcasted_iota(jnp.int32, (CHUNK, CHUNK), 1)
    return [jnp.where(row >= col, ws_ref[g], 0.0).astype(BF16) for g in range(A_GROUPS)]


def _rows_to_lanes(a, n):
    return jnp.concatenate([a[c * CHUNK:(c + 1) * CHUNK] for c in range(n)], axis=1)


def _lanes_to_rows(a, n):
    w = a.shape[1] // n
    return jnp.concatenate([a[:, c * w:(c + 1) * w] for c in range(n)], axis=0)


def _stack_heads(pair01, pair23):
    return jnp.concatenate([pair01[:, :256], pair01[:, 256:], pair23[:, :256], pair23[:, 256:]], axis=0)


def _pair_heads(s, r):
    return (jnp.concatenate([s[0:r], s[r:2 * r]], axis=1), jnp.concatenate([s[2 * r:3 * r], s[3 * r:4 * r]], axis=1))


def _pair_operands(variants):
    return (jnp.concatenate(variants[0:2], axis=0), jnp.concatenate(variants[2:4], axis=0))


def _split_pair_grads(d_pairs):
    return d_pairs[0][:256], d_pairs[0][256:], d_pairs[1][:256], d_pairs[1][256:]


def _halves_bf16(a):
    return (a[:, :128].astype(BF16), a[:, 128:].astype(BF16))


def _group_a_forward(au, av, vg, vb, wm, bs_rows):
    gu, tu = _gelu(au)
    gv, tv = _gelu(av)
    ya, res = [], []
    for g in range(A_GROUPS):
        sl = slice(g * 128, (g + 1) * 128)
        xg = gv[:, sl]
        xc = xg - jnp.mean(xg, axis=-1, keepdims=True)
        rstd = lax.rsqrt(jnp.mean(xc * xc, axis=-1, keepdims=True) + EPS)
        xhat = xc * rstd
        vn = _rows_to_lanes((xhat * vg[:, sl] + vb[:, sl]).astype(BF16), TILE_CHUNKS)
        s = _lanes_to_rows(_mm(wm[g], vn), TILE_CHUNKS) + bs_rows[g]
        ya.append(gu[:, sl] * s)
        res.append((xhat, rstd, vn, s))
    return ya, dict(gu=gu, tu=tu, tv=tv, groups=res)


def _attention_probs(qp, k_pairs, bias, sink_col):
    logits = _stack_heads(_mm_nt(qp[0], k_pairs[0]), _mm_nt(qp[1], k_pairs[1])) * QK_SCALE
    if bias is not None:
        logits = logits + bias
    return _softmax(logits, sink_col)


def _attention_out(p, v_pairs, r):
    pp = _pair_heads(p.astype(BF16), r)
    return jnp.concatenate([_mm(pp[0], v_pairs[0]), _mm(pp[1], v_pairs[1])], axis=-1), pp


def _attention_backward(p, pp, do_pairs, qp, k_pairs, v_pairs, r):
    dp = _stack_heads(_mm_nt(do_pairs[0], v_pairs[0]), _mm_nt(do_pairs[1], v_pairs[1]))
    delta = jnp.sum(p * dp, axis=-1, keepdims=True)
    dl = p * (dp - delta)
    dlp = _pair_heads(dl.astype(BF16), r)
    dq = jnp.concatenate([_mm(dlp[0], k_pairs[0]), _mm(dlp[1], k_pairs[1])], axis=-1)
    dk = (_mm_tn(dlp[0], qp[0]), _mm_tn(dlp[1], qp[1]))
    dv = (_mm_tn(pp[0], do_pairs[0]), _mm_tn(pp[1], do_pairs[1]))
    return dl, delta, dq, dk, dv


def _tile_specs(n_tiles_ex, width):
    return pl.BlockSpec((TILE, width), lambda b, i: (b * n_tiles_ex + jnp.minimum(i, n_tiles_ex - 1), 0))


def _prev_chunk_spec(n_tiles_ex, width):
    def index(b, i):
        chunk = TILE_CHUNKS * jnp.minimum(i, n_tiles_ex - 1)
        return (b * n_tiles_ex * TILE_CHUNKS + jnp.maximum(chunk - 1, 0), 0)
    return pl.BlockSpec((CHUNK, width), index)


def _full_spec(shape):
    zeros = (0,) * len(shape)
    return pl.BlockSpec(shape, lambda *_: zeros)


SMEM_SPEC = pl.BlockSpec(memory_space=pltpu.SMEM)
ANY_SPEC = pl.BlockSpec(memory_space=pl.ANY)
VMEM_SPEC = pl.BlockSpec(memory_space=pltpu.VMEM)


def _make_bias(rel_bias_t, buckets):
    def body(rel_ref, bk_ref, out_ref):
        bk = bk_ref[...]
        for h in range(4):
            acc = jnp.zeros((CHUNK, 2 * CHUNK), F32)
            for b in range(N_BUCKETS):
                acc = jnp.where(bk == b, rel_ref[h, b], acc)
            for t, with_prev in enumerate((True, False)):
                out_ref[t, h * CHUNK:(h + 1) * CHUNK, :] = jnp.where(_band_valid(with_prev), acc, NEG)

    return pl.pallas_call(
        body, name="make_bias", out_shape=jax.ShapeDtypeStruct((2, 4 * CHUNK, 2 * CHUNK), F32),
        in_specs=[SMEM_SPEC, VMEM_SPEC], out_specs=VMEM_SPEC,
    )(rel_bias_t, buckets)


def _gather_weights(w_in_s, w_mkv_s, w_out_s):
    shapes = [w_in_s.shape, w_mkv_s.shape, w_out_s.shape]
    n_w = len(shapes)

    def body(win_ref, wmkv_ref, wout_ref, gin_ref, gmkv_ref, gout_ref, send_sems, recv_sems):
        x, y, c = lax.axis_index("x"), lax.axis_index("y"), lax.axis_index("c")
        me, sibling = (x, y, c), (x, y, 1 - c)
        chips = [(1 - x, y), (x, 1 - y), (1 - x, 1 - y)]
        ins = [win_ref, wmkv_ref, wout_ref]
        outs = [gin_ref, gmkv_ref, gout_ref]
        my_shard = 2 * x + y
        for w in range(n_w):
            outs[w][my_shard] = ins[w][...].astype(BF16)

        def copy(k, w, shard, half, to):
            rows = shapes[w][0] // 2
            ref = outs[w].at[shard, pl.ds(half * rows, rows), :]
            return pltpu.make_async_remote_copy(src_ref=ref, dst_ref=ref, send_sem=send_sems.at[k],
                                                recv_sem=recv_sems.at[k], device_id=to, device_id_type=MESH)

        pairs = [(w, j) for w in range(n_w) for j in range(3)]
        first = [copy(3 * w + j, w, my_shard, c, (*chips[j], c)) for w, j in pairs]
        for cp in first:
            cp.start()
        passed = []
        for w, j in pairs:
            shard = 2 * chips[j][0] + chips[j][1]
            copy(3 * w + j, w, shard, c, me).wait_recv()
            fwd = copy(9 + 3 * w + j, w, shard, c, sibling)
            fwd.start()
            passed.append(fwd)
        for w, j in pairs:
            shard = 2 * chips[j][0] + chips[j][1]
            copy(9 + 3 * w + j, w, shard, 1 - c, me).wait_recv()
        for cp in first + passed:
            cp.wait_send()

    return pl.pallas_call(
        body, name="gather_weights",
        out_shape=[jax.ShapeDtypeStruct((N_CHIPS,) + s, BF16) for s in shapes],
        in_specs=[VMEM_SPEC] * 3, out_specs=[VMEM_SPEC] * 3,
        scratch_shapes=[pltpu.SemaphoreType.DMA((18,)), pltpu.SemaphoreType.DMA((18,))],
        compiler_params=pltpu.CompilerParams(vmem_limit_bytes=VMEM_LIMIT),
    )(w_in_s, w_mkv_s, w_out_s)


def _memkv_forward(mem, g_mem, w_mkv):
    n_ex = mem.shape[0]

    def body(mem_ref, g_ref, w_ref, out_ref):
        m = mem_ref[0]
        r = lax.rsqrt(jnp.mean(m * m, axis=-1, keepdims=True) + EPS)
        out_ref[0] = _mm((m * r * g_ref[...]).astype(BF16), w_ref[...])

    return pl.pallas_call(
        body, name="memkv_forward", grid=(n_ex,),
        out_shape=jax.ShapeDtypeStruct((n_ex, MEM_LEN, 2 * MEM_WIDTH), F32),
        in_specs=[pl.BlockSpec((1, MEM_LEN, D_MODEL), lambda b: (b, 0, 0)), _full_spec((1, D_MODEL)),
                  _full_spec((D_MODEL, 2 * MEM_WIDTH))],
        out_specs=pl.BlockSpec((1, MEM_LEN, 2 * MEM_WIDTH), lambda b: (b, 0, 0)),
    )(mem, g_mem, w_mkv)


PROJ_WIDTHS = (A_WIDTH, A_WIDTH, SWA_WIDTH, KV_WIDTH, KV_WIDTH, MEM_WIDTH, MIX_WIDTH)
PROJ_OFFSETS = tuple(int(v) for v in np.cumsum((0,) + PROJ_WIDTHS))


def _forward_projection(x2, g_pre, w_in_t):
    n_tok = x2.shape[0]

    def body(x_ref, g_ref, w_ref, h_ref, *out_refs):
        xv = x_ref[...]
        r = lax.rsqrt(jnp.mean(xv * xv, axis=-1, keepdims=True) + EPS)
        h = (xv * r * g_ref[...]).astype(BF16)
        h_ref[...] = h
        proj = _mm_nt(h, w_ref[...])
        for k, ref in enumerate(out_refs):
            ref[...] = proj[:, PROJ_OFFSETS[k]:PROJ_OFFSETS[k + 1]]

    widths = (D_MODEL,) + PROJ_WIDTHS
    h, *parts = pl.pallas_call(
        body, name="forward_projection", grid=(n_tok // PROJ_TILE,),
        out_shape=[jax.ShapeDtypeStruct((n_tok, D_MODEL), BF16)]
        + [jax.ShapeDtypeStruct((n_tok, w), F32) for w in PROJ_WIDTHS],
        in_specs=[pl.BlockSpec((PROJ_TILE, D_MODEL), lambda i: (i, 0)), _full_spec((1, D_MODEL)),
                  _full_spec((IN_WIDTH, D_MODEL))],
        out_specs=[pl.BlockSpec((PROJ_TILE, w), lambda i: (i, 0)) for w in widths],
        compiler_params=pltpu.CompilerParams(vmem_limit_bytes=VMEM_LIMIT),
    )(x2, g_pre, w_in_t)
    return h, parts


def _load_chunk(j, i, sk_ref, sv_ref, skp_ref, svp_ref):
    rows = slice(j * CHUNK, (j + 1) * CHUNK)
    if j == 0:
        k_prev, v_prev, table = skp_ref[...], svp_ref[...], jnp.where(i > 0, 0, 1)
    else:
        prev = slice((j - 1) * CHUNK, j * CHUNK)
        k_prev, v_prev, table = sk_ref[prev, :], sv_ref[prev, :], 0
    k_pairs = _pair_operands(_swa_variants(jnp.concatenate([k_prev, sk_ref[rows, :]], axis=0)))
    v_pairs = _pair_operands(_swa_variants(jnp.concatenate([v_prev, sv_ref[rows, :]], axis=0)))
    return rows, k_pairs, v_pairs, table


def _tile_constants(ws_ref, bs_ref, sink_ref, mkv_ref):
    wm = _causal_weights(ws_ref)
    bs_rows = [jnp.concatenate([bs_ref[g]] * TILE_CHUNKS, axis=0) for g in range(A_GROUPS)]
    sink_col = jnp.max(jnp.concatenate([jnp.full((CHUNK, 128), sink_ref[0, h], F32) for h in range(4)], axis=0),
                       axis=-1, keepdims=True)
    mkv_v = mkv_ref[0]
    mk_pairs = _pair_operands(_mem_variants(mkv_v[:, :MEM_WIDTH]))
    mv_pairs = _pair_operands(_mem_variants(mkv_v[:, MEM_WIDTH:]))
    return wm, bs_rows, sink_col, mk_pairs, mv_pairs


def _forward_mix(parts, mkv, x2, tgt2, v_g, v_b, w_sp, b_sp, sinks, bias, w_out, g_post, n_ex, seq):
    n_tiles_ex = seq // TILE
    n_tok = n_ex * seq
    au, av, sq, sk, sv, mq, z = parts

    def body(au_ref, av_ref, sq_ref, sk_ref, sv_ref, skp_ref, svp_ref, mq_ref, z_ref, mkv_ref, x_ref, tgt_ref,
             vg_ref, vb_ref, ws_ref, bs_ref, sink_ref, bias_ref, wout_ref, gpost_ref,
             dout_ref, do_ref, loss_ref, dgpost_ref):
        b, i = pl.program_id(0), pl.program_id(1)

        @pl.when((b == 0) & (i == 0))
        def _():
            loss_ref[...] = jnp.zeros_like(loss_ref)
            dgpost_ref[...] = jnp.zeros_like(dgpost_ref)

        wm, bs_rows, sink_col, mk_pairs, mv_pairs = _tile_constants(ws_ref, bs_ref, sink_ref, mkv_ref)
        ya, _ = _group_a_forward(au_ref[...], av_ref[...], vg_ref[...], vb_ref[...], wm, bs_rows)
        yb = []
        for j in range(TILE_CHUNKS):
            rows, k_pairs, v_pairs, table = _load_chunk(j, i, sk_ref, sv_ref, skp_ref, svp_ref)
            p, _ = _attention_probs(_halves_bf16(sq_ref[rows, :]), k_pairs, bias_ref[table], sink_col)
            yb.append(_attention_out(p, v_pairs, CHUNK)[0])
        pm, _ = _attention_probs(_halves_bf16(mq_ref[...]), mk_pairs, None, None)
        yc = _attention_out(pm, mv_pairs, TILE)[0]
        ycat = jnp.concatenate(ya + [jnp.concatenate(yb, axis=0), yc], axis=-1)
        zv = z_ref[...]
        y = ycat * (zv * _sigmoid(zv))
        o = _mm(y.astype(BF16), wout_ref[...])
        r2 = lax.rsqrt(jnp.mean(o * o, axis=-1, keepdims=True) + EPS)
        nrm = o * r2
        gp = gpost_ref[...]
        diff = x_ref[...] + nrm * gp - tgt_ref[...]
        loss_ref[...] += jnp.sum(diff * diff) * (0.5 / D_MODEL)
        dout = diff * (1.0 / D_MODEL)
        dout_ref[...] = dout
        dgpost_ref[...] += jnp.sum(dout * nrm, axis=0, keepdims=True)
        dn = dout * gp
        do_ref[...] = r2 * (dn - nrm * jnp.mean(dn * nrm, axis=-1, keepdims=True))

    tile = functools.partial(_tile_specs, n_tiles_ex)
    prev = functools.partial(_prev_chunk_spec, n_tiles_ex)
    return pl.pallas_call(
        body, name="forward_mix", grid=(n_ex, n_tiles_ex),
        out_shape=[jax.ShapeDtypeStruct((n_tok, D_MODEL), F32), jax.ShapeDtypeStruct((n_tok, D_MODEL), F32),
                   jax.ShapeDtypeStruct((1, 128), F32), jax.ShapeDtypeStruct((1, D_MODEL), F32)],
        in_specs=[tile(A_WIDTH), tile(A_WIDTH), tile(SWA_WIDTH), tile(KV_WIDTH), tile(KV_WIDTH),
                  prev(KV_WIDTH), prev(KV_WIDTH), tile(MEM_WIDTH), tile(MIX_WIDTH),
                  pl.BlockSpec((1, MEM_LEN, 2 * MEM_WIDTH), lambda b, i: (b, 0, 0)),
                  tile(D_MODEL), tile(D_MODEL),
                  _full_spec((1, A_WIDTH)), _full_spec((1, A_WIDTH)), _full_spec((A_GROUPS, CHUNK, CHUNK)),
                  _full_spec((A_GROUPS, CHUNK, CHUNK)), SMEM_SPEC, _full_spec((2, 4 * CHUNK, 2 * CHUNK)),
                  _full_spec((MIX_WIDTH, D_MODEL)), _full_spec((1, D_MODEL))],
        out_specs=[tile(D_MODEL), tile(D_MODEL), _full_spec((1, 128)), _full_spec((1, D_MODEL))],
        compiler_params=pltpu.CompilerParams(vmem_limit_bytes=VMEM_LIMIT),
    )(au, av, sq, sk, sv, sk, sv, mq, z, mkv, x2, tgt2, v_g, v_b, w_sp, b_sp, sinks, bias, w_out, g_post)


def _backward_mix(parts, mkv, do, v_g, v_b, w_sp, b_sp, sinks, bias, w_out, n_ex, seq):
    n_tiles_ex = seq // TILE
    n_tok = n_ex * seq
    au, av, sq, sk, sv, mq, z = parts
    col = dict(zip(("au", "av", "sq", "sk", "sv", "mq", "z"),
                   (slice(PROJ_OFFSETS[k], PROJ_OFFSETS[k + 1]) for k in range(len(PROJ_WIDTHS)))))
    before_kv, after_kv = slice(0, col["sk"].start), slice(col["sv"].stop, IN_WIDTH)

    def body(do_ref, au_ref, av_ref, sq_ref, sk_ref, sv_ref, skp_ref, svp_ref, mq_ref, z_ref, mkv_ref,
             vg_ref, vb_ref, ws_ref, bs_ref, sink_ref, bias_ref, wout_ref,
             dproj_ref, dmkv_ref, dwout_ref, dvg_ref, dvb_ref, dws_ref, dbs_ref, dsink_ref, drel_ref,
             carry_dp, carry_k, carry_v):
        b, i = pl.program_id(0), pl.program_id(1)

        @pl.when((b == 0) & (i == 0))
        def _():
            for ref in (dwout_ref, dvg_ref, dvb_ref, dws_ref, dbs_ref, dsink_ref, drel_ref):
                ref[...] = jnp.zeros_like(ref)

        @pl.when(i == 0)
        def _():
            dmkv_ref[...] = jnp.zeros_like(dmkv_ref)
            carry_k[...] = jnp.zeros_like(carry_k)
            carry_v[...] = jnp.zeros_like(carry_v)

        @pl.when(i > 0)
        def _():
            dproj_ref[:, before_kv] = carry_dp[:, before_kv]
            dproj_ref[:, after_kv] = carry_dp[:, after_kv]

        @pl.when(i < n_tiles_ex)
        def _():
            wm, bs_rows, sink_col, mk_pairs, mv_pairs = _tile_constants(ws_ref, bs_ref, sink_ref, mkv_ref)
            vg = vg_ref[...]
            do_b = do_ref[...].astype(BF16)
            dy = _mm_nt(do_b, wout_ref[...])
            zv = z_ref[...]
            sig = _sigmoid(zv)
            sz = zv * sig
            dyc = dy * sz

            au_v, av_v = au_ref[...], av_ref[...]
            ya, res = _group_a_forward(au_v, av_v, vg, vb_ref[...], wm, bs_rows)
            dgu, dgv = [], []
            for g in range(A_GROUPS):
                sl = slice(g * 128, (g + 1) * 128)
                xhat, rstd, vn, s = res["groups"][g]
                dya = dyc[:, sl]
                dgu.append(dya * s)
                ds = dya * res["gu"][:, sl]
                dbs_ref[:, sl] += sum(ds[c * CHUNK:(c + 1) * CHUNK] for c in range(TILE_CHUNKS))
                ds_b = _rows_to_lanes(ds.astype(BF16), TILE_CHUNKS)
                dws_ref[g] += _mm_nt(ds_b, vn)
                dvn = _lanes_to_rows(_mm_tn(wm[g], ds_b), TILE_CHUNKS)
                dvg_ref[:, sl] += jnp.sum(dvn * xhat, axis=0, keepdims=True)
                dvb_ref[:, sl] += jnp.sum(dvn, axis=0, keepdims=True)
                dxh = dvn * vg[:, sl]
                dgv.append(rstd * (dxh - jnp.mean(dxh, axis=-1, keepdims=True)
                                   - xhat * jnp.mean(dxh * xhat, axis=-1, keepdims=True)))
            carry_dp[:, col["au"]] = (jnp.concatenate(dgu, axis=-1) * _gelu_grad(au_v, res["tu"])).astype(BF16)
            carry_dp[:, col["av"]] = (jnp.concatenate(dgv, axis=-1) * _gelu_grad(av_v, res["tv"])).astype(BF16)

            lane4 = lax.broadcasted_iota(jnp.int32, (1, 128), 1)
            dsink_vec = jnp.zeros((1, 128), F32)
            yb, dk_parts, dv_parts = [], [], []
            for j in range(TILE_CHUNKS):
                rows, k_pairs, v_pairs, table = _load_chunk(j, i, sk_ref, sv_ref, skp_ref, svp_ref)
                qp = _halves_bf16(sq_ref[rows, :])
                p, ps = _attention_probs(qp, k_pairs, bias_ref[table], sink_col)
                out, pp = _attention_out(p, v_pairs, CHUNK)
                yb.append(out)
                do_pairs = _halves_bf16(dyc[rows, A_WIDTH:A_WIDTH + SWA_WIDTH])
                dl, delta, dq, dk, dv = _attention_backward(p, pp, do_pairs, qp, k_pairs, v_pairs, CHUNK)
                sink_terms = ps * delta
                for h in range(4):
                    dsink_vec = dsink_vec + jnp.where(lane4 == h, -jnp.sum(sink_terms[h * CHUNK:(h + 1) * CHUNK]), 0.0)
                drel_ref[...] += dl
                carry_dp[rows, col["sq"]] = (dq * QK_SCALE).astype(BF16)
                dk_parts.append(_swa_unvariants(*_split_pair_grads(dk)) * QK_SCALE)
                dv_parts.append(_swa_unvariants(*_split_pair_grads(dv)))

            mqp = _halves_bf16(mq_ref[...])
            pm, _ = _attention_probs(mqp, mk_pairs, None, None)
            yc, ppm = _attention_out(pm, mv_pairs, TILE)
            dc_pairs = _halves_bf16(dyc[:, A_WIDTH + SWA_WIDTH:])
            _, _, dmq, dmk, dmv = _attention_backward(pm, ppm, dc_pairs, mqp, mk_pairs, mv_pairs, TILE)
            carry_dp[:, col["mq"]] = (dmq * QK_SCALE).astype(BF16)
            dmkv_ref[0] += jnp.concatenate([_mem_unvariants(*_split_pair_grads(dmk)) * QK_SCALE,
                                            _mem_unvariants(*_split_pair_grads(dmv))], axis=-1)

            ycat = jnp.concatenate(ya + [jnp.concatenate(yb, axis=0), yc], axis=-1)
            dwout_ref[...] += _mm_tn((ycat * sz).astype(BF16), do_b)
            carry_dp[:, col["z"]] = (dy * ycat * (sig * (1.0 + zv * (1.0 - sig)))).astype(BF16)
            dsink_ref[...] += dsink_vec

            for parts_c, carry, cols in ((dk_parts, carry_k, col["sk"]), (dv_parts, carry_v, col["sv"])):
                @pl.when(i > 0)
                def _():
                    dproj_ref[:, cols] = (carry[...] + jnp.concatenate(
                        [jnp.zeros((TILE - CHUNK, KV_WIDTH), F32), parts_c[0][:CHUNK]], axis=0)).astype(BF16)
                new = [parts_c[0][CHUNK:]]
                for j in range(1, TILE_CHUNKS):
                    new[-1] = new[-1] + parts_c[j][:CHUNK]
                    new.append(parts_c[j][CHUNK:])
                carry[...] = jnp.concatenate(new, axis=0)

        @pl.when(i == n_tiles_ex)
        def _():
            dproj_ref[:, col["sk"]] = carry_k[...].astype(BF16)
            dproj_ref[:, col["sv"]] = carry_v[...].astype(BF16)

    tile = functools.partial(_tile_specs, n_tiles_ex)
    prev = functools.partial(_prev_chunk_spec, n_tiles_ex)
    late = pl.BlockSpec((TILE, IN_WIDTH), lambda b, i: (b * n_tiles_ex + jnp.maximum(i - 1, 0), 0))
    return pl.pallas_call(
        body, name="backward_mix", grid=(n_ex, n_tiles_ex + 1),
        out_shape=[jax.ShapeDtypeStruct((n_tok, IN_WIDTH), BF16),
                   jax.ShapeDtypeStruct((n_ex, MEM_LEN, 2 * MEM_WIDTH), F32),
                   jax.ShapeDtypeStruct((MIX_WIDTH, D_MODEL), F32), jax.ShapeDtypeStruct((1, A_WIDTH), F32),
                   jax.ShapeDtypeStruct((1, A_WIDTH), F32), jax.ShapeDtypeStruct((A_GROUPS, CHUNK, CHUNK), F32),
                   jax.ShapeDtypeStruct((CHUNK, A_WIDTH), F32), jax.ShapeDtypeStruct((1, 128), F32),
                   jax.ShapeDtypeStruct((4 * CHUNK, 2 * CHUNK), F32)],
        in_specs=[tile(D_MODEL), tile(A_WIDTH), tile(A_WIDTH), tile(SWA_WIDTH), tile(KV_WIDTH), tile(KV_WIDTH),
                  prev(KV_WIDTH), prev(KV_WIDTH), tile(MEM_WIDTH), tile(MIX_WIDTH),
                  pl.BlockSpec((1, MEM_LEN, 2 * MEM_WIDTH), lambda b, i: (b, 0, 0)),
                  _full_spec((1, A_WIDTH)), _full_spec((1, A_WIDTH)), _full_spec((A_GROUPS, CHUNK, CHUNK)),
                  _full_spec((A_GROUPS, CHUNK, CHUNK)), SMEM_SPEC, _full_spec((2, 4 * CHUNK, 2 * CHUNK)),
                  _full_spec((MIX_WIDTH, D_MODEL))],
        out_specs=[late, pl.BlockSpec((1, MEM_LEN, 2 * MEM_WIDTH), lambda b, i: (b, 0, 0)),
                   _full_spec((MIX_WIDTH, D_MODEL)), _full_spec((1, A_WIDTH)), _full_spec((1, A_WIDTH)),
                   _full_spec((A_GROUPS, CHUNK, CHUNK)), _full_spec((CHUNK, A_WIDTH)), _full_spec((1, 128)),
                   _full_spec((4 * CHUNK, 2 * CHUNK))],
        scratch_shapes=[pltpu.VMEM((TILE, IN_WIDTH), BF16), pltpu.VMEM((TILE, KV_WIDTH), F32),
                        pltpu.VMEM((TILE, KV_WIDTH), F32)],
        compiler_params=pltpu.CompilerParams(vmem_limit_bytes=VMEM_LIMIT),
    )(do, au, av, sq, sk, sv, sk, sv, mq, z, mkv, v_g, v_b, w_sp, b_sp, sinks, bias, w_out)


BWD_PROJ_TILE = 512


def _backward_projection(x2, dout, dproj, g_pre, w_in_t):
    n_tok = x2.shape[0]
    n_steps = n_tok // BWD_PROJ_TILE

    def body(x_ref, dout_ref, dp_ref, g_ref, w_hbm, dx_ref, dgpre_ref, w_vmem, sem):
        @pl.when(pl.program_id(0) == 0)
        def _():
            load = pltpu.make_async_copy(w_hbm, w_vmem, sem)
            load.start()
            dgpre_ref[...] = jnp.zeros_like(dgpre_ref)
            load.wait()

        xv = x_ref[...]
        r = lax.rsqrt(jnp.mean(xv * xv, axis=-1, keepdims=True) + EPS)
        xn = xv * r
        dh = _mm(dp_ref[...], w_vmem[...])
        dgpre_ref[...] += jnp.sum(dh * xn, axis=0, keepdims=True)
        dhg = dh * g_ref[...]
        dx_ref[...] = r * (dhg - xn * jnp.mean(dhg * xn, axis=-1, keepdims=True)) + dout_ref[...]

    row = lambda w: pl.BlockSpec((BWD_PROJ_TILE, w), lambda i: (i, 0))
    return pl.pallas_call(
        body, name="backward_projection", grid=(n_steps,),
        out_shape=[jax.ShapeDtypeStruct((n_tok, D_MODEL), F32), jax.ShapeDtypeStruct((1, D_MODEL), F32)],
        in_specs=[row(D_MODEL), row(D_MODEL), row(IN_WIDTH), _full_spec((1, D_MODEL)), ANY_SPEC],
        out_specs=[row(D_MODEL), _full_spec((1, D_MODEL))],
        scratch_shapes=[pltpu.VMEM((IN_WIDTH, D_MODEL), BF16), pltpu.SemaphoreType.DMA],
        input_output_aliases={1: 0},
        compiler_params=pltpu.CompilerParams(vmem_limit_bytes=VMEM_LIMIT),
    )(x2, dout, dproj, g_pre, w_in_t)


SHARD_ROWS = IN_WIDTH // N_CHIPS
SHARD_WINDOW = 768
SHARD_HALF = SHARD_ROWS // 2
DWIN_TILE = 1024
N_REL = N_CHIPS - 1


def _shard_window_start(shard):
    return (shard * SHARD_ROWS // 128) * 128


def _reduce_gradients(dproj, h, big, small, shard_arr):
    n_tok = h.shape[0]
    tile = min(DWIN_TILE, n_tok)
    n_sub = n_tok // tile
    last = N_CHIPS - 1
    n_big, n_small = len(big), len(small)
    big_half = [g.shape[2:] for g in big]
    sem_big_d2d = 2 * N_CHIPS
    sem_big_ici = sem_big_d2d + n_big
    sem_big_swap = sem_big_ici + N_REL * n_big
    sem_small_d2d = sem_big_swap + n_big
    sem_small_ici = sem_small_d2d + n_small
    n_sems = sem_small_ici + N_REL * n_small
    loc_small = n_big
    loc_out_win = loc_small + n_small
    loc_out_big = loc_out_win + 2
    loc_out_small = loc_out_big + 2 * n_big
    n_local = loc_out_small + n_small

    def shard_of_slot(s, my_shard):
        return my_shard ^ ((s + 1) % N_CHIPS)

    def body(shard_ref, dp_ref, h_ref, *refs):
        big_hbm, refs = refs[:n_big], refs[n_big:]
        small_hbm, refs = refs[:n_small], refs[n_small:]
        out_hbm, refs = refs[0], refs[1:]
        big_out, refs = refs[:n_big], refs[n_big:]
        small_out, refs = refs[:n_small], refs[n_small:]
        part, recv_d2d, send_ici, recv_ici, mine_buf, other_buf = refs[:6]
        refs = refs[6:]
        big_own, big_recv, big_send, big_land, big_mine, big_other = (
            refs[k * n_big:(k + 1) * n_big] for k in range(6))
        refs = refs[6 * n_big:]
        small_own, small_recv, small_all = (refs[k * n_small:(k + 1) * n_small] for k in range(3))
        send_sems, recv_sems, local_sems = refs[3 * n_small:]

        s, t = pl.program_id(0), pl.program_id(1)
        x, y, c = lax.axis_index("x"), lax.axis_index("y"), lax.axis_index("c")
        my_chip = 2 * x + y
        sibling = (x, y, 1 - c)
        my_rows = pl.ds(pl.multiple_of(c * SHARD_HALF, 8), SHARD_HALF)
        other_rows = pl.ds(pl.multiple_of((1 - c) * SHARD_HALF, 8), SHARD_HALF)

        def remote(src, dst, k, to):
            return pltpu.make_async_remote_copy(src_ref=src, dst_ref=dst, send_sem=send_sems.at[k],
                                                recv_sem=recv_sems.at[k], device_id=to, device_id_type=MESH)

        def chip_at(rel):
            return (x ^ (rel >> 1), y ^ (rel & 1), c)

        def to_sibling(k):
            return remote(part.at[k % 2, other_rows, :], recv_d2d.at[k], k, sibling)

        def to_chip(k):
            return remote(send_ici.at[k], recv_ici.at[k], N_CHIPS + k, chip_at(k + 1))

        swap = remote(mine_buf, other_buf, 2 * N_CHIPS - 1, sibling)
        big_load = [pltpu.make_async_copy(big_hbm[w].at[:, pl.ds(c, 1)], big_own[w], local_sems.at[w])
                    for w in range(n_big)]
        big_to_sibling = [remote(big_hbm[w].at[:, pl.ds(1 - c, 1)], big_recv[w], sem_big_d2d + w, sibling)
                          for w in range(n_big)]
        big_to_chip = [[remote(big_send[w].at[k], big_land[w].at[k], sem_big_ici + N_REL * w + k, chip_at(k + 1))
                        for k in range(N_REL)] for w in range(n_big)]
        big_swap = [remote(big_mine[w], big_other[w], sem_big_swap + w, sibling) for w in range(n_big)]
        small_load = [pltpu.make_async_copy(small_hbm[i], small_own[i], local_sems.at[loc_small + i])
                      for i in range(n_small)]
        small_to_sibling = [remote(small_hbm[i], small_recv[i], sem_small_d2d + i, sibling) for i in range(n_small)]
        small_to_chip = [[remote(small_all[i].at[my_chip], small_all[i].at[my_chip],
                                 sem_small_ici + N_REL * i + k, chip_at(k + 1))
                          for k in range(N_REL)] for i in range(n_small)]

        @pl.when((s == 0) & (t == 0))
        def _():
            for cp in big_load + big_to_sibling + small_load + small_to_sibling:
                cp.start()

        @pl.when((s == 0) & (t == n_sub - 1))
        def _():
            for cp in big_load + small_load:
                cp.wait()
            for cp in big_to_sibling + small_to_sibling:
                cp.wait_recv()
                cp.wait_send()
            for w in range(n_big):
                for k in range(N_REL):
                    shard = my_chip ^ (k + 1)
                    big_send[w][k] = (big_own[w][shard, 0] + big_recv[w][shard, 0]).astype(BF16)
                    big_to_chip[w][k].start()
            for i in range(n_small):
                small_all[i][my_chip] = small_own[i][...] + small_recv[i][...]
                for k in range(N_REL):
                    small_to_chip[i][k].start()

        @pl.when((s > 0) & (t == 0))
        def _():
            k = s - 1
            cp = to_sibling(k)
            cp.wait_recv()
            cp.wait_send()
            send_ici[k] = (part[k % 2, my_rows, :] + recv_d2d[k]).astype(BF16)
            to_chip(k).start()

        r = _mm_tn(dp_ref[...], h_ref[...])
        odd = shard_of_slot(s, shard_ref[0]) % 2
        for parity in range(2):
            rows = r[64 * parity:64 * parity + SHARD_ROWS]

            @pl.when((odd == parity) & (t == 0))
            def _():
                part[s % 2] = rows

            @pl.when((odd == parity) & (t > 0))
            def _():
                part[s % 2] += rows

        @pl.when(t == n_sub - 1)
        def _():
            to_sibling(s).start()

        @pl.when((s == last) & (t == n_sub - 1))
        def _():
            cp = to_sibling(last)
            cp.wait_recv()
            cp.wait_send()
            total = part[last % 2, my_rows, :] + recv_d2d[last]
            for k in range(last):
                to_chip(k).wait_recv()
                total = total + recv_ici[k].astype(F32)
            mine_buf[...] = total
            swap.start()
            out_mine = pltpu.make_async_copy(mine_buf, out_hbm.at[my_rows, :], local_sems.at[0])
            out_mine.start()
            swap.wait_recv()
            out_other = pltpu.make_async_copy(other_buf, out_hbm.at[other_rows, :], local_sems.at[1])
            out_other.start()
            stores = [out_mine, out_other]
            for w in range(n_big):
                rows = big_half[w][0]
                total = big_own[w][my_chip, 0] + big_recv[w][my_chip, 0]
                for k in range(N_REL):
                    big_to_chip[w][k].wait_recv()
                    total = total + big_land[w][k].astype(F32)
                big_mine[w][...] = total
                big_swap[w].start()
                stores.append(pltpu.make_async_copy(
                    big_mine[w], big_out[w].at[pl.ds(pl.multiple_of(c * rows, 8), rows), :],
                    local_sems.at[loc_out_big + 2 * w]))
                stores[-1].start()
            for w in range(n_big):
                rows = big_half[w][0]
                big_swap[w].wait_recv()
                stores.append(pltpu.make_async_copy(
                    big_other[w], big_out[w].at[pl.ds(pl.multiple_of((1 - c) * rows, 8), rows), :],
                    local_sems.at[loc_out_big + 2 * w + 1]))
                stores[-1].start()
            for i in range(n_small):
                for k in range(N_REL):
                    small_to_chip[i][k].wait_recv()
                stores.append(pltpu.make_async_copy(small_all[i], small_out[i], local_sems.at[loc_out_small + i]))
                stores[-1].start()
            for k in range(last):
                to_chip(k).wait_send()
            swap.wait_send()
            for w in range(n_big):
                for k in range(N_REL):
                    big_to_chip[w][k].wait_send()
                big_swap[w].wait_send()
            for i in range(n_small):
                for k in range(N_REL):
                    small_to_chip[i][k].wait_send()
            for cp in stores:
                cp.wait()

    half = (SHARD_HALF, D_MODEL)
    vmem = pltpu.VMEM
    scratch = [vmem((2, SHARD_ROWS, D_MODEL), F32), vmem((N_CHIPS,) + half, F32),
               vmem((N_REL,) + half, BF16), vmem((N_REL,) + half, BF16), vmem(half, F32), vmem(half, F32)]
    scratch += [vmem((N_CHIPS, 1) + hs, F32) for hs in big_half] * 2
    scratch += [vmem((N_REL,) + hs, BF16) for hs in big_half] * 2
    scratch += [vmem(hs, F32) for hs in big_half] * 2
    scratch += [vmem(a.shape, F32) for a in small] * 2 + [vmem((N_CHIPS,) + a.shape, F32) for a in small]
    scratch += [pltpu.SemaphoreType.DMA((n_sems,)), pltpu.SemaphoreType.DMA((n_sems,)),
                pltpu.SemaphoreType.DMA((n_local,))]
    n_hbm = n_big + n_small
    out = pl.pallas_call(
        body, name="reduce_gradients",
        out_shape=[jax.ShapeDtypeStruct((SHARD_ROWS, D_MODEL), F32)]
        + [jax.ShapeDtypeStruct((2 * hs[0], hs[1]), F32) for hs in big_half]
        + [jax.ShapeDtypeStruct((N_CHIPS,) + a.shape, F32) for a in small],
        grid_spec=pltpu.PrefetchScalarGridSpec(
            num_scalar_prefetch=1, grid=(N_CHIPS, n_sub),
            in_specs=[pl.BlockSpec((pl.Element(tile), pl.Element(SHARD_WINDOW)),
                                   lambda s, t, m: (t * tile, _shard_window_start(shard_of_slot(s, m[0])))),
                      pl.BlockSpec((tile, D_MODEL), lambda s, t, m: (t, 0))] + [ANY_SPEC] * n_hbm,
            out_specs=[ANY_SPEC] * (1 + n_hbm),
            scratch_shapes=scratch),
        compiler_params=pltpu.CompilerParams(vmem_limit_bytes=VMEM_LIMIT),
    )(shard_arr, dproj, h, *big, *small)
    return out[:1 + n_big], out[1 + n_big:]


def _memkv_backward(mem, dmkv, g_mem, w_mkv):
    n_ex = mem.shape[0]

    def body(mem_ref, d_ref, g_ref, w_ref, dw_ref, dg_ref):
        @pl.when(pl.program_id(0) == 0)
        def _():
            dw_ref[...] = jnp.zeros_like(dw_ref)
            dg_ref[...] = jnp.zeros_like(dg_ref)

        m = mem_ref[0]
        mn = m * lax.rsqrt(jnp.mean(m * m, axis=-1, keepdims=True) + EPS)
        d_b = d_ref[0].astype(BF16)
        dw_ref[...] += _mm_tn((mn * g_ref[...]).astype(BF16), d_b)
        dg_ref[...] += jnp.sum(_mm_nt(d_b, w_ref[...]) * mn, axis=0, keepdims=True)

    return pl.pallas_call(
        body, name="memkv_backward", grid=(n_ex,),
        out_shape=[jax.ShapeDtypeStruct((D_MODEL, 2 * MEM_WIDTH), F32), jax.ShapeDtypeStruct((1, D_MODEL), F32)],
        in_specs=[pl.BlockSpec((1, MEM_LEN, D_MODEL), lambda b: (b, 0, 0)),
                  pl.BlockSpec((1, MEM_LEN, 2 * MEM_WIDTH), lambda b: (b, 0, 0)),
                  _full_spec((1, D_MODEL)), _full_spec((D_MODEL, 2 * MEM_WIDTH))],
        out_specs=[_full_spec((D_MODEL, 2 * MEM_WIDTH)), _full_spec((1, D_MODEL))],
    )(mem, dmkv, g_mem, w_mkv)


def _pack_small_grads(dgpre, dgpost, dgmem, dvg, dvb, dws, dbs, dsink, drel, loss_vec, buckets):
    def body(dgpre_ref, dgpost_ref, dgmem_ref, dvg_ref, dvb_ref, dws_ref, dbs_ref, dsink_ref, drel_ref, loss_ref,
             bk_ref, a_ref, b_ref):
        a_ref[...] = jnp.zeros_like(a_ref)
        b_ref[...] = jnp.zeros_like(b_ref)
        a_ref[0:1, :] = dgpre_ref[...]
        a_ref[1:2, :] = dgpost_ref[...]
        a_ref[2:3, :] = dgmem_ref[...]
        a_ref[3:4, :] = jnp.concatenate([dvg_ref[...], dvb_ref[...]], axis=-1)
        a_ref[ROW_LOSS:ROW_LOSS + 1, 0:128] = loss_ref[...]
        row = lax.broadcasted_iota(jnp.int32, (CHUNK, CHUNK), 0)
        col = lax.broadcasted_iota(jnp.int32, (CHUNK, CHUNK), 1)
        for g in range(A_GROUPS):
            b_ref[ROW_WS + g * CHUNK:ROW_WS + (g + 1) * CHUNK, :] = jnp.where(row >= col, dws_ref[g], 0.0)
            by_token = jnp.transpose(dbs_ref[:, g * 128:(g + 1) * 128])
            b_ref[ROW_BS + g:ROW_BS + g + 1, :] = jnp.sum(by_token, axis=0, keepdims=True)
        b_ref[ROW_SINK:ROW_SINK + 1, :] = dsink_ref[...]
        bk = bk_ref[...]
        rel_row = lax.broadcasted_iota(jnp.int32, (8, 128), 0)
        rel_col = lax.broadcasted_iota(jnp.int32, (8, 128), 1)
        rel = jnp.zeros((8, 128), F32)
        for h in range(4):
            acc = drel_ref[h * CHUNK:(h + 1) * CHUNK, :]
            for b in range(N_BUCKETS):
                rel = jnp.where((rel_row == h) & (rel_col == b), jnp.sum(jnp.where(bk == b, acc, 0.0)), rel)
        b_ref[ROW_REL:ROW_REL + 8, :] = rel

    return pl.pallas_call(
        body, name="pack_small_grads",
        out_shape=[jax.ShapeDtypeStruct((SMALL_A_ROWS, D_MODEL), F32), jax.ShapeDtypeStruct((SMALL_B_ROWS, 128), F32)],
        in_specs=[VMEM_SPEC] * 11, out_specs=[VMEM_SPEC] * 2,
    )(dgpre, dgpost, dgmem, dvg, dvb, dws, dbs, dsink, drel, loss_vec, buckets)


def _adamw(w, g, m, v):
    m2 = ADAM_B1 * m + (1.0 - ADAM_B1) * g
    v2 = ADAM_B2 * v + (1.0 - ADAM_B2) * (g * g)
    m_hat = m2 / (1.0 - ADAM_B1 ** ADAM_STEP)
    v_hat = v2 / (1.0 - ADAM_B2 ** ADAM_STEP)
    delta = -ADAM_LR * (m_hat / (jnp.sqrt(v_hat) + ADAM_EPS) + ADAM_WD * w)
    return delta, m2, v2


ADAM_MAX_ROWS = 176


def _adamw_whole(g, w, m, v, name):
    rows, cols = w.shape
    steps = -(-rows // ADAM_MAX_ROWS)
    block_rows = rows // steps
    assert block_rows * steps == rows and block_rows % 8 == 0

    def body(g_ref, w_ref, m_ref, v_ref, d_out, m_out, v_out):
        delta, m2, v2 = _adamw(w_ref[...], g_ref[...], m_ref[...], v_ref[...])
        d_out[...] = delta
        m_out[...] = m2
        v_out[...] = v2

    block = pl.BlockSpec((block_rows, cols), lambda k: (k, 0))
    out = pl.pallas_call(
        body, name=name, grid=(steps,), out_shape=[jax.ShapeDtypeStruct((rows, cols), F32)] * 3,
        in_specs=[block] * 4, out_specs=[block] * 3,
    )(g, w, m, v)
    return [g] + list(out)


def _adamw_small(ra, rb, weights, moments_m, moments_v):
    n = len(weights)

    def body(*refs):
        ra_ref, rb_ref = refs[0], refs[1]
        w_refs, m_refs, v_refs = refs[2:2 + n], refs[2 + n:2 + 2 * n], refs[2 + 2 * n:2 + 3 * n]
        outs = refs[2 + 3 * n:]
        g_outs, d_outs, m_outs, v_outs = outs[:n], outs[n:2 * n], outs[2 * n:3 * n], outs[3 * n:4 * n]
        ga, gb = ra_ref[0], rb_ref[0]
        for chip in range(1, N_CHIPS):
            ga = ga + ra_ref[chip]
            gb = gb + rb_ref[chip]
        outs[4 * n][...] = ga[ROW_LOSS:ROW_LOSS + 1, 0:128]
        grads = [ga[0:1, :], ga[1:2, :], ga[2:3, :], ga[3:4, :A_WIDTH], ga[3:4, A_WIDTH:],
                 gb[ROW_WS:ROW_WS + A_GROUPS * CHUNK, :].reshape(A_GROUPS, CHUNK, CHUNK),
                 gb[ROW_BS:ROW_BS + A_GROUPS, :], gb[ROW_SINK:ROW_SINK + 1, 0:4],
                 gb[ROW_REL:ROW_REL + 4, 0:N_BUCKETS]]
        for k in range(n):
            delta, m2, v2 = _adamw(w_refs[k][...], grads[k], m_refs[k][...], v_refs[k][...])
            g_outs[k][...] = grads[k]
            d_outs[k][...] = delta
            m_outs[k][...] = m2
            v_outs[k][...] = v2

    out_shape = [jax.ShapeDtypeStruct(w.shape, F32) for w in weights] * 4 + [jax.ShapeDtypeStruct((1, 128), F32)]
    return pl.pallas_call(
        body, name="adamw_small", out_shape=out_shape,
        in_specs=[VMEM_SPEC] * (2 + 3 * n), out_specs=[VMEM_SPEC] * (4 * n + 1),
    )(ra, rb, *weights, *moments_m, *moments_v)


def kernel(x, mem, pre_norm_g, post_norm_g, mem_norm_g, w_in, w_mem_kv, v_norm_g, v_norm_b, w_spatial, b_spatial, attn_sinks, rel_bias, w_out, loss_target, m_pre_norm_g, m_post_norm_g, m_mem_norm_g, m_w_in, m_w_mem_kv, m_v_norm_g, m_v_norm_b, m_w_spatial, m_b_spatial, m_attn_sinks, m_rel_bias, m_w_out, v_pre_norm_g, v_post_norm_g, v_mem_norm_g, v_w_in, v_w_mem_kv, v_v_norm_g, v_v_norm_b, v_w_spatial, v_b_spatial, v_attn_sinks, v_rel_bias, v_w_out):
    n_ex, seq, _ = x.shape
    n_tok = n_ex * seq
    x2 = x.reshape(n_tok, D_MODEL)
    tgt2 = loss_target.reshape(n_tok, D_MODEL)
    buckets = jnp.asarray(_bucket_map())
    shard_arr = (2 * lax.axis_index("x") + lax.axis_index("y")).astype(jnp.int32).reshape(1)
    w_sp = w_spatial[0]
    b_sp = jnp.broadcast_to(b_spatial[0][:, :, None], (A_GROUPS, CHUNK, CHUNK))
    w_in_t, m_w_in_t, v_w_in_t = (jnp.transpose(a[0]) for a in (w_in, m_w_in, v_w_in))
    rel_t, m_rel_t, v_rel_t = (jnp.transpose(a) for a in (rel_bias, m_rel_bias, v_rel_bias))

    g_in, g_mkv, g_out = _gather_weights(w_in_t, w_mem_kv[0], w_out[0])
    w_in_b = g_in.reshape(IN_WIDTH, D_MODEL)
    w_mkv_b = g_mkv.reshape(D_MODEL, 2 * MEM_WIDTH)
    w_out_b = g_out.reshape(MIX_WIDTH, D_MODEL)

    bias = _make_bias(rel_t, buckets)
    mkv = _memkv_forward(mem, mem_norm_g, w_mkv_b)
    h_b, parts = _forward_projection(x2, pre_norm_g, w_in_b)
    dout, do, loss_vec, dgpost = _forward_mix(parts, mkv, x2, tgt2, v_norm_g, v_norm_b, w_sp, b_sp, attn_sinks, bias,
                                             w_out_b, post_norm_g, n_ex, seq)

    dproj, dmkv, dwout, dvg, dvb, dws, dbs, dsink, drel = _backward_mix(
        parts, mkv, do, v_norm_g, v_norm_b, w_sp, b_sp, attn_sinks, bias, w_out_b, n_ex, seq)
    dx, dgpre = _backward_projection(x2, dout, dproj, pre_norm_g, w_in_b)
    dwmkv, dgmem = _memkv_backward(mem, dmkv, mem_norm_g, w_mkv_b)
    small_a, small_b = _pack_small_grads(dgpre, dgpost, dgmem, dvg, dvb, dws, dbs, dsink, drel, loss_vec, buckets)

    shard_shapes = [w_mem_kv.shape[1:], w_out.shape[1:]]
    big = [g.reshape(N_CHIPS, 2, s[0] // 2, s[1]) for g, s in zip((dwmkv, dwout), shard_shapes)]
    (g_win, g_wmkv, g_wout), (ga, gb) = _reduce_gradients(dproj, h_b, big, [small_a, small_b], shard_arr)

    big_out = [_adamw_whole(g_win, w_in_t, m_w_in_t, v_w_in_t, "adamw_w_in"),
               _adamw_whole(g_wmkv, w_mem_kv[0], m_w_mem_kv[0], v_w_mem_kv[0], "adamw_w_mem_kv"),
               _adamw_whole(g_wout, w_out[0], m_w_out[0], v_w_out[0], "adamw_w_out")]
    small_w = [pre_norm_g, post_norm_g, mem_norm_g, v_norm_g, v_norm_b, w_sp, b_spatial[0], attn_sinks, rel_t]
    small_m = [m_pre_norm_g, m_post_norm_g, m_mem_norm_g, m_v_norm_g, m_v_norm_b, m_w_spatial[0], m_b_spatial[0],
               m_attn_sinks, m_rel_t]
    small_v = [v_pre_norm_g, v_post_norm_g, v_mem_norm_g, v_v_norm_g, v_v_norm_b, v_w_spatial[0], v_b_spatial[0],
               v_attn_sinks, v_rel_t]
    small_out = _adamw_small(ga, gb, small_w, small_m, small_v)
    n_small = len(small_w)

    outputs = [small_out[4 * n_small][0, 0], dx.reshape(x.shape)]
    for kind in range(4):
        s = small_out[kind * n_small:(kind + 1) * n_small]
        outputs += [s[0], s[1], s[2], jnp.transpose(big_out[0][kind])[None], big_out[1][kind][None], s[3], s[4],
                    s[5][None], s[6][None], s[7], jnp.transpose(s[8]), big_out[2][kind][None]]
    return tuple(outputs)
```

```python
import functools

import numpy as np
import jax
import jax.numpy as jnp
from jax import lax
from jax.experimental import pallas as pl
from jax.experimental.pallas import tpu as pltpu

F32 = jnp.float32
BF16 = jnp.bfloat16
MESH = pl.DeviceIdType.MESH

D_MODEL = 1024
CHUNK = 128
A_WIDTH = 512
A_GROUPS = 4
SWA_WIDTH = 256
KV_WIDTH = 128
MEM_WIDTH = 256
MEM_LEN = 256
MIX_WIDTH = 1024
IN_WIDTH = 2816
N_BUCKETS = 32
MAX_DISTANCE = 128
EPS = 1e-6
NEG = -1e30
QK_SCALE = 0.125
HALF_HEAD_PAIR = 64

ADAM_LR = 0.001
ADAM_B1 = 0.9
ADAM_B2 = 0.999
ADAM_EPS = 1e-08
ADAM_WD = 0.01
ADAM_STEP = 10

N_CHIPS = 4
TILE_CHUNKS = 4
TILE = TILE_CHUNKS * CHUNK
PROJ_TILE = 256
VMEM_LIMIT = 56 * 1024 * 1024

SMALL_A_ROWS = 8
ROW_LOSS = 4
ROW_WS = 0
ROW_BS = 512
ROW_SINK = 520
ROW_REL = 528
SMALL_B_ROWS = 536


def _mm(a, b):
    return lax.dot_general(a, b, (((1,), (0,)), ((), ())), preferred_element_type=F32)


def _mm_nt(a, b):
    return lax.dot_general(a, b, (((1,), (1,)), ((), ())), preferred_element_type=F32)


def _mm_tn(a, b):
    return lax.dot_general(a, b, (((0,), (0,)), ((), ())), preferred_element_type=F32)


def _bucket_map():
    qi = np.arange(CHUNK)[:, None]
    kj = np.arange(2 * CHUNK)[None, :]
    n = np.maximum(qi + CHUNK - kj, 0)
    max_exact = N_BUCKETS // 2
    large = max_exact + (np.log(np.maximum(n, 1) / max_exact) / np.log(MAX_DISTANCE / max_exact)
                         * (N_BUCKETS - max_exact)).astype(np.int32)
    large = np.minimum(large, N_BUCKETS - 1)
    return np.where(n < max_exact, n, large).astype(np.int32)


_GELU_C = 0.7978845608028654
_GELU_A = 0.044715


def _gelu(x):
    t = jnp.tanh(_GELU_C * (x + _GELU_A * x * x * x))
    return 0.5 * x * (1.0 + t), t


def _gelu_grad(x, t):
    return 0.5 * (1.0 + t) + 0.5 * x * (1.0 - t * t) * (_GELU_C * (1.0 + 3.0 * _GELU_A * x * x))


def _sigmoid(x):
    return 1.0 / (1.0 + jnp.exp(-x))


def _lane_lo(shape):
    return lax.broadcasted_iota(jnp.int32, shape, 1) < HALF_HEAD_PAIR


def _swa_variants(t):
    lo = _lane_lo(t.shape)
    tr = pltpu.roll(t, HALF_HEAD_PAIR, 1)
    zero = jnp.zeros_like(t)
    return (jnp.where(lo, t, zero).astype(BF16), jnp.where(lo, zero, tr).astype(BF16),
            jnp.where(lo, tr, zero).astype(BF16), jnp.where(lo, zero, t).astype(BF16))


def _swa_unvariants(d0, d1, d2, d3):
    lo = _lane_lo(d0.shape)
    zero = jnp.zeros_like(d0)
    rolled = jnp.where(lo, zero, d1) + jnp.where(lo, d2, zero)
    return jnp.where(lo, d0, zero) + jnp.where(lo, zero, d3) + pltpu.roll(rolled, HALF_HEAD_PAIR, 1)


def _mem_variants(t):
    out = []
    for pair in range(2):
        tp = t[:, pair * 128:(pair + 1) * 128]
        lo = _lane_lo(tp.shape)
        zero = jnp.zeros_like(tp)
        out.append(jnp.where(lo, tp, zero).astype(BF16))
        out.append(jnp.where(lo, zero, tp).astype(BF16))
    return out


def _mem_unvariants(d0, d1, d2, d3):
    lo = _lane_lo(d0.shape)
    return jnp.concatenate([jnp.where(lo, d0, d1), jnp.where(lo, d2, d3)], axis=-1)


def _softmax(logits, sinks):
    m = jnp.max(logits, axis=-1, keepdims=True)
    if sinks is not None:
        m = jnp.maximum(m, sinks)
    p = jnp.exp(logits - m)
    den = jnp.sum(p, axis=-1, keepdims=True)
    if sinks is None:
        return p * (1.0 / den), None
    es = jnp.exp(sinks - m)
    inv = 1.0 / (den + es)
    return p * inv, es * inv


def _band_valid(with_prev):
    qi = lax.broadcasted_iota(jnp.int32, (CHUNK, 2 * CHUNK), 0)
    kj = lax.broadcasted_iota(jnp.int32, (CHUNK, 2 * CHUNK), 1)
    in_cur = (kj >= CHUNK) & (kj - CHUNK <= qi)
    if not with_prev:
        return in_cur
    return in_cur | ((kj < CHUNK) & (kj > qi))


def _causal_weights(ws_ref):
    row = lax.broadcasted_iota(jnp.int32, (CHUNK, CHUNK), 0)
    col = lax.broadcasted_iota(jnp.int32, (CHUNK, CHUNK), 1)
    return [jnp.where(row >= col, ws_ref[g], 0.0).astype(BF16) for g in range(A_GROUPS)]


def _rows_to_lanes(a, n):
    return jnp.concatenate([a[c * CHUNK:(c + 1) * CHUNK] for c in range(n)], axis=1)


def _lanes_to_rows(a, n):
    w = a.shape[1] // n
    return jnp.concatenate([a[:, c * w:(c + 1) * w] for c in range(n)], axis=0)


def _stack_heads(pair01, pair23):
    return jnp.concatenate([pair01[:, :256], pair01[:, 256:], pair23[:, :256], pair23[:, 256:]], axis=0)


def _pair_heads(s, r):
    return (jnp.concatenate([s[0:r], s[r:2 * r]], axis=1), jnp.concatenate([s[2 * r:3 * r], s[3 * r:4 * r]], axis=1))


def _pair_operands(variants):
    return (jnp.concatenate(variants[0:2], axis=0), jnp.concatenate(variants[2:4], axis=0))


def _split_pair_grads(d_pairs):
    return d_pairs[0][:256], d_pairs[0][256:], d_pairs[1][:256], d_pairs[1][256:]


def _halves_bf16(a):
    return (a[:, :128].astype(BF16), a[:, 128:].astype(BF16))


def _group_a_forward(au, av, vg, vb, wm, bs_rows):
    gu, tu = _gelu(au)
    gv, tv = _gelu(av)
    ya, res = [], []
    for g in range(A_GROUPS):
        sl = slice(g * 128, (g + 1) * 128)
        xg = gv[:, sl]
        xc = xg - jnp.mean(xg, axis=-1, keepdims=True)
        rstd = lax.rsqrt(jnp.mean(xc * xc, axis=-1, keepdims=True) + EPS)
        xhat = xc * rstd
        vn = _rows_to_lanes((xhat * vg[:, sl] + vb[:, sl]).astype(BF16), TILE_CHUNKS)
        s = _lanes_to_rows(_mm(wm[g], vn), TILE_CHUNKS) + bs_rows[g]
        ya.append(gu[:, sl] * s)
        res.append((xhat, rstd, vn, s))
    return ya, dict(gu=gu, tu=tu, tv=tv, groups=res)


def _attention_probs(qp, k_pairs, bias, sink_col):
    logits = _stack_heads(_mm_nt(qp[0], k_pairs[0]), _mm_nt(qp[1], k_pairs[1])) * QK_SCALE
    if bias is not None:
        logits = logits + bias
    return _softmax(logits, sink_col)


def _attention_out(p, v_pairs, r):
    pp = _pair_heads(p.astype(BF16), r)
    return jnp.concatenate([_mm(pp[0], v_pairs[0]), _mm(pp[1], v_pairs[1])], axis=-1), pp


def _attention_backward(p, pp, do_pairs, qp, k_pairs, v_pairs, r):
    dp = _stack_heads(_mm_nt(do_pairs[0], v_pairs[0]), _mm_nt(do_pairs[1], v_pairs[1]))
    delta = jnp.sum(p * dp, axis=-1, keepdims=True)
    dl = p * (dp - delta)
    dlp = _pair_heads(dl.astype(BF16), r)
    dq = jnp.concatenate([_mm(dlp[0], k_pairs[0]), _mm(dlp[1], k_pairs[1])], axis=-1)
    dk = (_mm_tn(dlp[0], qp[0]), _mm_tn(dlp[1], qp[1]))
    dv = (_mm_tn(pp[0], do_pairs[0]), _mm_tn(pp[1], do_pairs[1]))
    return dl, delta, dq, dk, dv


def _tile_specs(n_tiles_ex, width):
    return pl.BlockSpec((TILE, width), lambda b, i: (b * n_tiles_ex + jnp.minimum(i, n_tiles_ex - 1), 0))


def _prev_chunk_spec(n_tiles_ex, width):
    def index(b, i):
        chunk = TILE_CHUNKS * jnp.minimum(i, n_tiles_ex - 1)
        return (b * n_tiles_ex * TILE_CHUNKS + jnp.maximum(chunk - 1, 0), 0)
    return pl.BlockSpec((CHUNK, width), index)


def _full_spec(shape):
    zeros = (0,) * len(shape)
    return pl.BlockSpec(shape, lambda *_: zeros)


SMEM_SPEC = pl.BlockSpec(memory_space=pltpu.SMEM)
ANY_SPEC = pl.BlockSpec(memory_space=pl.ANY)
VMEM_SPEC = pl.BlockSpec(memory_space=pltpu.VMEM)


def _make_bias(rel_bias_t, buckets):
    def body(rel_ref, bk_ref, out_ref):
        bk = bk_ref[...]
        for h in range(4):
            acc = jnp.zeros((CHUNK, 2 * CHUNK), F32)
            for b in range(N_BUCKETS):
                acc = jnp.where(bk == b, rel_ref[h, b], acc)
            for t, with_prev in enumerate((True, False)):
                out_ref[t, h * CHUNK:(h + 1) * CHUNK, :] = jnp.where(_band_valid(with_prev), acc, NEG)

    return pl.pallas_call(
        body, name="make_bias", out_shape=jax.ShapeDtypeStruct((2, 4 * CHUNK, 2 * CHUNK), F32),
        in_specs=[SMEM_SPEC, VMEM_SPEC], out_specs=VMEM_SPEC,
    )(rel_bias_t, buckets)


def _memkv_forward(mem, g_mem, w_mkv):
    n_ex = mem.shape[0]

    def body(mem_ref, g_ref, w_ref, out_ref):
        m = mem_ref[0]
        r = lax.rsqrt(jnp.mean(m * m, axis=-1, keepdims=True) + EPS)
        out_ref[0] = _mm((m * r * g_ref[...]).astype(BF16), w_ref[...])

    return pl.pallas_call(
        body, name="memkv_forward", grid=(n_ex,),
        out_shape=jax.ShapeDtypeStruct((n_ex, MEM_LEN, 2 * MEM_WIDTH), F32),
        in_specs=[pl.BlockSpec((1, MEM_LEN, D_MODEL), lambda b: (b, 0, 0)), _full_spec((1, D_MODEL)),
                  _full_spec((D_MODEL, 2 * MEM_WIDTH))],
        out_specs=pl.BlockSpec((1, MEM_LEN, 2 * MEM_WIDTH), lambda b: (b, 0, 0)),
    )(mem, g_mem, w_mkv)


PROJ_WIDTHS = (A_WIDTH, A_WIDTH, SWA_WIDTH, KV_WIDTH, KV_WIDTH, MEM_WIDTH, MIX_WIDTH)
PROJ_OFFSETS = tuple(int(v) for v in np.cumsum((0,) + PROJ_WIDTHS))


HALF_WIDTH = IN_WIDTH // 2
HALF_PARTS = ((0, 1, 2, 3), (4, 5, 6))


def _gather_and_project(x2, g_pre, w_in_s, w_mkv_s, w_out_s, x_arr):
    n_tok = x2.shape[0]
    n_tiles = n_tok // PROJ_TILE
    last = n_tiles - 1
    shapes = [w_in_s.shape, w_mkv_s.shape, w_out_s.shape]
    n_w = len(shapes)

    def body(x_sref, x_ref, g_ref, win_hbm, wmkv_hbm, wout_hbm, h_ref, *refs):
        part_refs, refs = refs[:len(PROJ_WIDTHS)], refs[len(PROJ_WIDTHS):]
        gin_hbm, gmkv_hbm, gout_hbm, wg, stage_in, stage_mkv, stage_out, own_mkv, own_out = refs[:9]
        send_sems, recv_sems, local_sems = refs[9:]
        p, t = pl.program_id(0), pl.program_id(1)
        x, y, c = lax.axis_index("x"), lax.axis_index("y"), lax.axis_index("c")
        me, sibling = (x, y, c), (x, y, 1 - c)
        my_shard = 2 * x + y
        gathered = [wg, gmkv_hbm, gout_hbm]

        def half_rows(w, shard, half):
            rows = shapes[w][0] // 2
            if w == 0:
                return wg.at[pl.ds(pl.multiple_of(shard * shapes[0][0] + half * rows, 16), rows), :]
            return gathered[w].at[shard, pl.ds(half * rows, rows), :]

        def first(w, rel):
            src = half_rows(w, my_shard, c) if w == 0 else (own_mkv, own_out)[w - 1].at[
                pl.ds(c * (shapes[w][0] // 2), shapes[w][0] // 2), :]
            k = 3 * w + rel - 1
            return pltpu.make_async_remote_copy(
                src_ref=src, dst_ref=half_rows(w, my_shard, c), send_sem=send_sems.at[k], recv_sem=recv_sems.at[k],
                device_id=(x ^ (rel >> 1), y ^ (rel & 1), c), device_id_type=MESH)

        def landed(w, rel):
            k = 3 * w + rel - 1
            ref = half_rows(w, my_shard ^ rel, c)
            return pltpu.make_async_remote_copy(src_ref=ref, dst_ref=ref, send_sem=send_sems.at[k],
                                                recv_sem=recv_sems.at[k], device_id=me, device_id_type=MESH)

        def passed(w, rel, half, to):
            k = 9 + 3 * w + rel - 1
            ref = half_rows(w, my_shard ^ rel, half)
            return pltpu.make_async_remote_copy(src_ref=ref, dst_ref=ref, send_sem=send_sems.at[k],
                                                recv_sem=recv_sems.at[k], device_id=to, device_id_type=MESH)

        def pass_on(w, rels):
            for rel in rels:
                landed(w, rel).wait_recv()
                passed(w, rel, c, sibling).start()
            for rel in rels:
                passed(w, rel, 1 - c, me).wait_recv()

        own_stores = [pltpu.make_async_copy(own_mkv, gmkv_hbm.at[my_shard], local_sems.at[3]),
                      pltpu.make_async_copy(own_out, gout_hbm.at[my_shard], local_sems.at[4])]

        @pl.when((p == 0) & (t == 0))
        def _():
            loads = [pltpu.make_async_copy(src, dst, local_sems.at[k]) for k, (src, dst) in enumerate(
                ((win_hbm, stage_in), (wmkv_hbm, stage_mkv), (wout_hbm, stage_out)))]
            for cp in loads:
                cp.start()
            loads[0].wait()
            wg[pl.ds(pl.multiple_of(my_shard * shapes[0][0], 16), shapes[0][0]), :] = stage_in[...].astype(BF16)
            for rel in (1, 2, 3):
                first(0, rel).start()
            loads[1].wait()
            loads[2].wait()
            own_mkv[...] = stage_mkv[...].astype(BF16)
            own_out[...] = stage_out[...].astype(BF16)
            for w in (1, 2):
                for rel in (1, 2, 3):
                    first(w, rel).start()
            for cp in own_stores:
                cp.start()
            pass_on(0, (1,))

        @pl.when((p == 1) & (t == 0))
        def _():
            pass_on(0, (2, 3))

        xv = x_ref[...]
        r = lax.rsqrt(jnp.mean(xv * xv, axis=-1, keepdims=True) + EPS)
        h = (xv * r * g_ref[...]).astype(BF16)

        @pl.when(p == 0)
        def _():
            h_ref[...] = h

        for hh in range(2):
            @pl.when((p ^ x_sref[0]) == hh)
            def _():
                proj = _mm_nt(h, wg[hh * HALF_WIDTH:(hh + 1) * HALF_WIDTH, :])
                for k in HALF_PARTS[hh]:
                    lo = PROJ_OFFSETS[k] - hh * HALF_WIDTH
                    part_refs[k][...] = proj[:, lo:lo + PROJ_WIDTHS[k]]

        @pl.when((p == 1) & (t == last))
        def _():
            store = pltpu.make_async_copy(wg, gin_hbm, local_sems.at[5])
            store.start()
            for w in (1, 2):
                pass_on(w, (1, 2, 3))
            for w in range(n_w):
                for rel in (1, 2, 3):
                    first(w, rel).wait_send()
                    passed(w, rel, c, sibling).wait_send()
            for cp in own_stores:
                cp.wait()
            store.wait()

    def active_in(hh):
        def index(p, t, xs):
            return (jnp.where((p ^ xs[0]) == hh, t, jnp.where(p == 0, 0, last)), 0)
        return index

    part_specs = [pl.BlockSpec((PROJ_TILE, PROJ_WIDTHS[k]), active_in(hh)) for hh in range(2) for k in HALF_PARTS[hh]]
    vmem = pltpu.VMEM
    out = pl.pallas_call(
        body, name="gather_and_project",
        out_shape=[jax.ShapeDtypeStruct((n_tok, D_MODEL), BF16)]
        + [jax.ShapeDtypeStruct((n_tok, w), F32) for w in PROJ_WIDTHS]
        + [jax.ShapeDtypeStruct((N_CHIPS * shapes[0][0], shapes[0][1]), BF16)]
        + [jax.ShapeDtypeStruct((N_CHIPS,) + s, BF16) for s in shapes[1:]],
        grid_spec=pltpu.PrefetchScalarGridSpec(
            num_scalar_prefetch=1, grid=(2, n_tiles),
            in_specs=[pl.BlockSpec((PROJ_TILE, D_MODEL), lambda p, t, xs: (t, 0)),
                      pl.BlockSpec((1, D_MODEL), lambda p, t, xs: (0, 0)), ANY_SPEC, ANY_SPEC, ANY_SPEC],
            out_specs=[pl.BlockSpec((PROJ_TILE, D_MODEL), lambda p, t, xs: (jnp.where(p == 0, t, last), 0))]
            + part_specs + [ANY_SPEC] * 3,
            scratch_shapes=[vmem((N_CHIPS * shapes[0][0], shapes[0][1]), BF16), vmem(shapes[0], F32),
                            vmem(shapes[1], F32), vmem(shapes[2], F32), vmem(shapes[1], BF16), vmem(shapes[2], BF16),
                            pltpu.SemaphoreType.DMA((18,)), pltpu.SemaphoreType.DMA((18,)),
                            pltpu.SemaphoreType.DMA((6,))]),
        compiler_params=pltpu.CompilerParams(vmem_limit_bytes=VMEM_LIMIT),
    )(x_arr, x2, g_pre, w_in_s, w_mkv_s, w_out_s)
    h, parts, weights = out[0], out[1:1 + len(PROJ_WIDTHS)], out[1 + len(PROJ_WIDTHS):]
    return h, list(parts), weights


def _load_chunk(j, i, sk_ref, sv_ref, skp_ref, svp_ref):
    rows = slice(j * CHUNK, (j + 1) * CHUNK)
    if j == 0:
        k_prev, v_prev, table = skp_ref[...], svp_ref[...], jnp.where(i > 0, 0, 1)
    else:
        prev = slice((j - 1) * CHUNK, j * CHUNK)
        k_prev, v_prev, table = sk_ref[prev, :], sv_ref[prev, :], 0
    k_pairs = _pair_operands(_swa_variants(jnp.concatenate([k_prev, sk_ref[rows, :]], axis=0)))
    v_pairs = _pair_operands(_swa_variants(jnp.concatenate([v_prev, sv_ref[rows, :]], axis=0)))
    return rows, k_pairs, v_pairs, table


def _tile_constants(ws_ref, bs_ref, sink_ref, mkv_ref):
    wm = _causal_weights(ws_ref)
    bs_rows = [jnp.concatenate([bs_ref[g]] * TILE_CHUNKS, axis=0) for g in range(A_GROUPS)]
    sink_col = jnp.max(jnp.concatenate([jnp.full((CHUNK, 128), sink_ref[0, h], F32) for h in range(4)], axis=0),
                       axis=-1, keepdims=True)
    mkv_v = mkv_ref[0]
    mk_pairs = _pair_operands(_mem_variants(mkv_v[:, :MEM_WIDTH]))
    mv_pairs = _pair_operands(_mem_variants(mkv_v[:, MEM_WIDTH:]))
    return wm, bs_rows, sink_col, mk_pairs, mv_pairs


def _forward_mix(parts, mkv, x2, tgt2, v_g, v_b, w_sp, b_sp, sinks, bias, w_out, g_post, n_ex, seq):
    n_tiles_ex = seq // TILE
    n_tok = n_ex * seq
    au, av, sq, sk, sv, mq, z = parts

    def body(au_ref, av_ref, sq_ref, sk_ref, sv_ref, skp_ref, svp_ref, mq_ref, z_ref, mkv_ref, x_ref, tgt_ref,
             vg_ref, vb_ref, ws_ref, bs_ref, sink_ref, bias_ref, wout_ref, gpost_ref,
             dout_ref, do_ref, loss_ref, dgpost_ref):
        b, i = pl.program_id(0), pl.program_id(1)

        @pl.when((b == 0) & (i == 0))
        def _():
            loss_ref[...] = jnp.zeros_like(loss_ref)
            dgpost_ref[...] = jnp.zeros_like(dgpost_ref)

        wm, bs_rows, sink_col, mk_pairs, mv_pairs = _tile_constants(ws_ref, bs_ref, sink_ref, mkv_ref)
        ya, _ = _group_a_forward(au_ref[...], av_ref[...], vg_ref[...], vb_ref[...], wm, bs_rows)
        yb = []
        for j in range(TILE_CHUNKS):
            rows, k_pairs, v_pairs, table = _load_chunk(j, i, sk_ref, sv_ref, skp_ref, svp_ref)
            p, _ = _attention_probs(_halves_bf16(sq_ref[rows, :]), k_pairs, bias_ref[table], sink_col)
            yb.append(_attention_out(p, v_pairs, CHUNK)[0])
        pm, _ = _attention_probs(_halves_bf16(mq_ref[...]), mk_pairs, None, None)
        yc = _attention_out(pm, mv_pairs, TILE)[0]
        ycat = jnp.concatenate(ya + [jnp.concatenate(yb, axis=0), yc], axis=-1)
        zv = z_ref[...]
        y = ycat * (zv * _sigmoid(zv))
        o = _mm(y.astype(BF16), wout_ref[...])
        r2 = lax.rsqrt(jnp.mean(o * o, axis=-1, keepdims=True) + EPS)
        nrm = o * r2
        gp = gpost_ref[...]
        diff = x_ref[...] + nrm * gp - tgt_ref[...]
        loss_ref[...] += jnp.sum(diff * diff) * (0.5 / D_MODEL)
        dout = diff * (1.0 / D_MODEL)
        dout_ref[...] = dout
        dgpost_ref[...] += jnp.sum(dout * nrm, axis=0, keepdims=True)
        dn = dout * gp
        do_ref[...] = r2 * (dn - nrm * jnp.mean(dn * nrm, axis=-1, keepdims=True))

    tile = functools.partial(_tile_specs, n_tiles_ex)
    prev = functools.partial(_prev_chunk_spec, n_tiles_ex)
    return pl.pallas_call(
        body, name="forward_mix", grid=(n_ex, n_tiles_ex),
        out_shape=[jax.ShapeDtypeStruct((n_tok, D_MODEL), F32), jax.ShapeDtypeStruct((n_tok, D_MODEL), F32),
                   jax.ShapeDtypeStruct((1, 128), F32), jax.ShapeDtypeStruct((1, D_MODEL), F32)],
        in_specs=[tile(A_WIDTH), tile(A_WIDTH), tile(SWA_WIDTH), tile(KV_WIDTH), tile(KV_WIDTH),
                  prev(KV_WIDTH), prev(KV_WIDTH), tile(MEM_WIDTH), tile(MIX_WIDTH),
                  pl.BlockSpec((1, MEM_LEN, 2 * MEM_WIDTH), lambda b, i: (b, 0, 0)),
                  tile(D_MODEL), tile(D_MODEL),
                  _full_spec((1, A_WIDTH)), _full_spec((1, A_WIDTH)), _full_spec((A_GROUPS, CHUNK, CHUNK)),
                  _full_spec((A_GROUPS, CHUNK, CHUNK)), SMEM_SPEC, _full_spec((2, 4 * CHUNK, 2 * CHUNK)),
                  _full_spec((MIX_WIDTH, D_MODEL)), _full_spec((1, D_MODEL))],
        out_specs=[tile(D_MODEL), tile(D_MODEL), _full_spec((1, 128)), _full_spec((1, D_MODEL))],
        compiler_params=pltpu.CompilerParams(vmem_limit_bytes=VMEM_LIMIT),
    )(au, av, sq, sk, sv, sk, sv, mq, z, mkv, x2, tgt2, v_g, v_b, w_sp, b_sp, sinks, bias, w_out, g_post)


def _backward_mix(parts, mkv, do, v_g, v_b, w_sp, b_sp, sinks, bias, w_out, n_ex, seq):
    n_tiles_ex = seq // TILE
    n_tok = n_ex * seq
    au, av, sq, sk, sv, mq, z = parts
    col = dict(zip(("au", "av", "sq", "sk", "sv", "mq", "z"),
                   (slice(PROJ_OFFSETS[k], PROJ_OFFSETS[k + 1]) for k in range(len(PROJ_WIDTHS)))))
    before_kv, after_kv = slice(0, col["sk"].start), slice(col["sv"].stop, IN_WIDTH)

    def body(do_ref, au_ref, av_ref, sq_ref, sk_ref, sv_ref, skp_ref, svp_ref, mq_ref, z_ref, mkv_ref,
             vg_ref, vb_ref, ws_ref, bs_ref, sink_ref, bias_ref, wout_ref,
             dproj_ref, dmkv_ref, dwout_ref, dvg_ref, dvb_ref, dws_ref, dbs_ref, dsink_ref, drel_ref,
             carry_dp, carry_k, carry_v):
        b, i = pl.program_id(0), pl.program_id(1)

        @pl.when((b == 0) & (i == 0))
        def _():
            for ref in (dwout_ref, dvg_ref, dvb_ref, dws_ref, dbs_ref, dsink_ref, drel_ref):
                ref[...] = jnp.zeros_like(ref)

        @pl.when(i == 0)
        def _():
            dmkv_ref[...] = jnp.zeros_like(dmkv_ref)
            carry_k[...] = jnp.zeros_like(carry_k)
            carry_v[...] = jnp.zeros_like(carry_v)

        @pl.when(i > 0)
        def _():
            dproj_ref[:, before_kv] = carry_dp[:, before_kv]
            dproj_ref[:, after_kv] = carry_dp[:, after_kv]

        @pl.when(i < n_tiles_ex)
        def _():
            wm, bs_rows, sink_col, mk_pairs, mv_pairs = _tile_constants(ws_ref, bs_ref, sink_ref, mkv_ref)
            vg = vg_ref[...]
            do_b = do_ref[...].astype(BF16)
            dy = _mm_nt(do_b, wout_ref[...])
            zv = z_ref[...]
            sig = _sigmoid(zv)
            sz = zv * sig
            dyc = dy * sz

            au_v, av_v = au_ref[...], av_ref[...]
            ya, res = _group_a_forward(au_v, av_v, vg, vb_ref[...], wm, bs_rows)
            dgu, dgv = [], []
            for g in range(A_GROUPS):
                sl = slice(g * 128, (g + 1) * 128)
                xhat, rstd, vn, s = res["groups"][g]
                dya = dyc[:, sl]
                dgu.append(dya * s)
                ds = dya * res["gu"][:, sl]
                dbs_ref[:, sl] += sum(ds[c * CHUNK:(c + 1) * CHUNK] for c in range(TILE_CHUNKS))
                ds_b = _rows_to_lanes(ds.astype(BF16), TILE_CHUNKS)
                dws_ref[g] += _mm_nt(ds_b, vn)
                dvn = _lanes_to_rows(_mm_tn(wm[g], ds_b), TILE_CHUNKS)
                dvg_ref[:, sl] += jnp.sum(dvn * xhat, axis=0, keepdims=True)
                dvb_ref[:, sl] += jnp.sum(dvn, axis=0, keepdims=True)
                dxh = dvn * vg[:, sl]
                dgv.append(rstd * (dxh - jnp.mean(dxh, axis=-1, keepdims=True)
                                   - xhat * jnp.mean(dxh * xhat, axis=-1, keepdims=True)))
            carry_dp[:, col["au"]] = (jnp.concatenate(dgu, axis=-1) * _gelu_grad(au_v, res["tu"])).astype(BF16)
            carry_dp[:, col["av"]] = (jnp.concatenate(dgv, axis=-1) * _gelu_grad(av_v, res["tv"])).astype(BF16)

            lane4 = lax.broadcasted_iota(jnp.int32, (1, 128), 1)
            dsink_vec = jnp.zeros((1, 128), F32)
            yb, dk_parts, dv_parts = [], [], []
            for j in range(TILE_CHUNKS):
                rows, k_pairs, v_pairs, table = _load_chunk(j, i, sk_ref, sv_ref, skp_ref, svp_ref)
                qp = _halves_bf16(sq_ref[rows, :])
                p, ps = _attention_probs(qp, k_pairs, bias_ref[table], sink_col)
                out, pp = _attention_out(p, v_pairs, CHUNK)
                yb.append(out)
                do_pairs = _halves_bf16(dyc[rows, A_WIDTH:A_WIDTH + SWA_WIDTH])
                dl, delta, dq, dk, dv = _attention_backward(p, pp, do_pairs, qp, k_pairs, v_pairs, CHUNK)
                sink_terms = ps * delta
                for h in range(4):
                    dsink_vec = dsink_vec + jnp.where(lane4 == h, -jnp.sum(sink_terms[h * CHUNK:(h + 1) * CHUNK]), 0.0)
                drel_ref[...] += dl
                carry_dp[rows, col["sq"]] = (dq * QK_SCALE).astype(BF16)
                dk_parts.append(_swa_unvariants(*_split_pair_grads(dk)) * QK_SCALE)
                dv_parts.append(_swa_unvariants(*_split_pair_grads(dv)))

            mqp = _halves_bf16(mq_ref[...])
            pm, _ = _attention_probs(mqp, mk_pairs, None, None)
            yc, ppm = _attention_out(pm, mv_pairs, TILE)
            dc_pairs = _halves_bf16(dyc[:, A_WIDTH + SWA_WIDTH:])
            _, _, dmq, dmk, dmv = _attention_backward(pm, ppm, dc_pairs, mqp, mk_pairs, mv_pairs, TILE)
            carry_dp[:, col["mq"]] = (dmq * QK_SCALE).astype(BF16)
            dmkv_ref[0] += jnp.concatenate([_mem_unvariants(*_split_pair_grads(dmk)) * QK_SCALE,
                                            _mem_unvariants(*_split_pair_grads(dmv))], axis=-1)

            ycat = jnp.concatenate(ya + [jnp.concatenate(yb, axis=0), yc], axis=-1)
            dwout_ref[...] += _mm_tn((ycat * sz).astype(BF16), do_b)
            carry_dp[:, col["z"]] = (dy * ycat * (sig * (1.0 + zv * (1.0 - sig)))).astype(BF16)
            dsink_ref[...] += dsink_vec

            for parts_c, carry, cols in ((dk_parts, carry_k, col["sk"]), (dv_parts, carry_v, col["sv"])):
                @pl.when(i > 0)
                def _():
                    dproj_ref[:, cols] = (carry[...] + jnp.concatenate(
                        [jnp.zeros((TILE - CHUNK, KV_WIDTH), F32), parts_c[0][:CHUNK]], axis=0)).astype(BF16)
                new = [parts_c[0][CHUNK:]]
                for j in range(1, TILE_CHUNKS):
                    new[-1] = new[-1] + parts_c[j][:CHUNK]
                    new.append(parts_c[j][CHUNK:])
                carry[...] = jnp.concatenate(new, axis=0)

        @pl.when(i == n_tiles_ex)
        def _():
            dproj_ref[:, col["sk"]] = carry_k[...].astype(BF16)
            dproj_ref[:, col["sv"]] = carry_v[...].astype(BF16)

    tile = functools.partial(_tile_specs, n_tiles_ex)
    prev = functools.partial(_prev_chunk_spec, n_tiles_ex)
    late = pl.BlockSpec((TILE, IN_WIDTH), lambda b, i: (b * n_tiles_ex + jnp.maximum(i - 1, 0), 0))
    return pl.pallas_call(
        body, name="backward_mix", grid=(n_ex, n_tiles_ex + 1),
        out_shape=[jax.ShapeDtypeStruct((n_tok, IN_WIDTH), BF16),
                   jax.ShapeDtypeStruct((n_ex, MEM_LEN, 2 * MEM_WIDTH), F32),
                   jax.ShapeDtypeStruct((MIX_WIDTH, D_MODEL), F32), jax.ShapeDtypeStruct((1, A_WIDTH), F32),
                   jax.ShapeDtypeStruct((1, A_WIDTH), F32), jax.ShapeDtypeStruct((A_GROUPS, CHUNK, CHUNK), F32),
                   jax.ShapeDtypeStruct((CHUNK, A_WIDTH), F32), jax.ShapeDtypeStruct((1, 128), F32),
                   jax.ShapeDtypeStruct((4 * CHUNK, 2 * CHUNK), F32)],
        in_specs=[tile(D_MODEL), tile(A_WIDTH), tile(A_WIDTH), tile(SWA_WIDTH), tile(KV_WIDTH), tile(KV_WIDTH),
                  prev(KV_WIDTH), prev(KV_WIDTH), tile(MEM_WIDTH), tile(MIX_WIDTH),
                  pl.BlockSpec((1, MEM_LEN, 2 * MEM_WIDTH), lambda b, i: (b, 0, 0)),
                  _full_spec((1, A_WIDTH)), _full_spec((1, A_WIDTH)), _full_spec((A_GROUPS, CHUNK, CHUNK)),
                  _full_spec((A_GROUPS, CHUNK, CHUNK)), SMEM_SPEC, _full_spec((2, 4 * CHUNK, 2 * CHUNK)),
                  _full_spec((MIX_WIDTH, D_MODEL))],
        out_specs=[late, pl.BlockSpec((1, MEM_LEN, 2 * MEM_WIDTH), lambda b, i: (b, 0, 0)),
                   _full_spec((MIX_WIDTH, D_MODEL)), _full_spec((1, A_WIDTH)), _full_spec((1, A_WIDTH)),
                   _full_spec((A_GROUPS, CHUNK, CHUNK)), _full_spec((CHUNK, A_WIDTH)), _full_spec((1, 128)),
                   _full_spec((4 * CHUNK, 2 * CHUNK))],
        scratch_shapes=[pltpu.VMEM((TILE, IN_WIDTH), BF16), pltpu.VMEM((TILE, KV_WIDTH), F32),
                        pltpu.VMEM((TILE, KV_WIDTH), F32)],
        compiler_params=pltpu.CompilerParams(vmem_limit_bytes=VMEM_LIMIT),
    )(do, au, av, sq, sk, sv, sk, sv, mq, z, mkv, v_g, v_b, w_sp, b_sp, sinks, bias, w_out)


BWD_PROJ_TILE = 512


def _backward_projection(x2, dout, dproj, g_pre, w_in_t):
    n_tok = x2.shape[0]
    n_steps = n_tok // BWD_PROJ_TILE

    def body(x_ref, dout_ref, dp_ref, g_ref, w_hbm, dx_ref, dgpre_ref, w_vmem, sem):
        @pl.when(pl.program_id(0) == 0)
        def _():
            load = pltpu.make_async_copy(w_hbm, w_vmem, sem)
            load.start()
            dgpre_ref[...] = jnp.zeros_like(dgpre_ref)
            load.wait()

        xv = x_ref[...]
        r = lax.rsqrt(jnp.mean(xv * xv, axis=-1, keepdims=True) + EPS)
        xn = xv * r
        dh = _mm(dp_ref[...], w_vmem[...])
        dgpre_ref[...] += jnp.sum(dh * xn, axis=0, keepdims=True)
        dhg = dh * g_ref[...]
        dx_ref[...] = r * (dhg - xn * jnp.mean(dhg * xn, axis=-1, keepdims=True)) + dout_ref[...]

    row = lambda w: pl.BlockSpec((BWD_PROJ_TILE, w), lambda i: (i, 0))
    return pl.pallas_call(
        body, name="backward_projection", grid=(n_steps,),
        out_shape=[jax.ShapeDtypeStruct((n_tok, D_MODEL), F32), jax.ShapeDtypeStruct((1, D_MODEL), F32)],
        in_specs=[row(D_MODEL), row(D_MODEL), row(IN_WIDTH), _full_spec((1, D_MODEL)), ANY_SPEC],
        out_specs=[row(D_MODEL), _full_spec((1, D_MODEL))],
        scratch_shapes=[pltpu.VMEM((IN_WIDTH, D_MODEL), BF16), pltpu.SemaphoreType.DMA],
        input_output_aliases={1: 0},
        compiler_params=pltpu.CompilerParams(vmem_limit_bytes=VMEM_LIMIT),
    )(x2, dout, dproj, g_pre, w_in_t)


SHARD_ROWS = IN_WIDTH // N_CHIPS
SHARD_WINDOW = 768
SHARD_HALF = SHARD_ROWS // 2
DWIN_TILE = 1024
N_REL = N_CHIPS - 1


def _shard_window_start(shard):
    return (shard * SHARD_ROWS // 128) * 128


def _reduce_gradients(dproj, h, big, small, shard_arr):
    n_tok = h.shape[0]
    tile = min(DWIN_TILE, n_tok)
    n_sub = n_tok // tile
    last = N_CHIPS - 1
    n_big, n_small = len(big), len(small)
    big_half = [g.shape[2:] for g in big]
    sem_big_d2d = 2 * N_CHIPS
    sem_big_ici = sem_big_d2d + n_big
    sem_big_swap = sem_big_ici + N_REL * n_big
    sem_small_d2d = sem_big_swap + n_big
    sem_small_ici = sem_small_d2d + n_small
    n_sems = sem_small_ici + N_REL * n_small
    loc_small = n_big
    loc_out_win = loc_small + n_small
    loc_out_big = loc_out_win + 2
    loc_out_small = loc_out_big + 2 * n_big
    n_local = loc_out_small + n_small

    def shard_of_slot(s, my_shard):
        return my_shard ^ ((s + 1) % N_CHIPS)

    def body(shard_ref, dp_ref, h_ref, *refs):
        big_hbm, refs = refs[:n_big], refs[n_big:]
        small_hbm, refs = refs[:n_small], refs[n_small:]
        out_hbm, refs = refs[0], refs[1:]
        big_out, refs = refs[:n_big], refs[n_big:]
        small_out, refs = refs[:n_small], refs[n_small:]
        part, recv_d2d, send_ici, recv_ici, mine_buf, other_buf = refs[:6]
        refs = refs[6:]
        big_own, big_recv, big_send, big_land, big_mine, big_other = (
            refs[k * n_big:(k + 1) * n_big] for k in range(6))
        refs = refs[6 * n_big:]
        small_own, small_recv, small_all = (refs[k * n_small:(k + 1) * n_small] for k in range(3))
        send_sems, recv_sems, local_sems = refs[3 * n_small:]

        s, t = pl.program_id(0), pl.program_id(1)
        x, y, c = lax.axis_index("x"), lax.axis_index("y"), lax.axis_index("c")
        my_chip = 2 * x + y
        sibling = (x, y, 1 - c)
        my_rows = pl.ds(pl.multiple_of(c * SHARD_HALF, 8), SHARD_HALF)
        other_rows = pl.ds(pl.multiple_of((1 - c) * SHARD_HALF, 8), SHARD_HALF)

        def remote(src, dst, k, to):
            return pltpu.make_async_remote_copy(src_ref=src, dst_ref=dst, send_sem=send_sems.at[k],
                                                recv_sem=recv_sems.at[k], device_id=to, device_id_type=MESH)

        def chip_at(rel):
            return (x ^ (rel >> 1), y ^ (rel & 1), c)

        def to_sibling(k):
            return remote(part.at[k % 2, other_rows, :], recv_d2d.at[k], k, sibling)

        def to_chip(k):
            return remote(send_ici.at[k], recv_ici.at[k], N_CHIPS + k, chip_at(k + 1))

        swap = remote(mine_buf, other_buf, 2 * N_CHIPS - 1, sibling)
        big_load = [pltpu.make_async_copy(big_hbm[w].at[:, pl.ds(c, 1)], big_own[w], local_sems.at[w])
                    for w in range(n_big)]
        big_to_sibling = [remote(big_hbm[w].at[:, pl.ds(1 - c, 1)], big_recv[w], sem_big_d2d + w, sibling)
                          for w in range(n_big)]
        big_to_chip = [[remote(big_send[w].at[k], big_land[w].at[k], sem_big_ici + N_REL * w + k, chip_at(k + 1))
                        for k in range(N_REL)] for w in range(n_big)]
        big_swap = [remote(big_mine[w], big_other[w], sem_big_swap + w, sibling) for w in range(n_big)]
        small_load = [pltpu.make_async_copy(small_hbm[i], small_own[i], local_sems.at[loc_small + i])
                      for i in range(n_small)]
        small_to_sibling = [remote(small_hbm[i], small_recv[i], sem_small_d2d + i, sibling) for i in range(n_small)]
        small_to_chip = [[remote(small_all[i].at[my_chip], small_all[i].at[my_chip],
                                 sem_small_ici + N_REL * i + k, chip_at(k + 1))
                          for k in range(N_REL)] for i in range(n_small)]

        @pl.when((s == 0) & (t == 0))
        def _():
            for cp in big_load + big_to_sibling + small_load + small_to_sibling:
                cp.start()

        @pl.when((s == 0) & (t == n_sub - 1))
        def _():
            for cp in big_load + small_load:
                cp.wait()
            for cp in big_to_sibling + small_to_sibling:
                cp.wait_recv()
                cp.wait_send()
            for w in range(n_big):
                for k in range(N_REL):
                    shard = my_chip ^ (k + 1)
                    big_send[w][k] = (big_own[w][shard, 0] + big_recv[w][shard, 0]).astype(BF16)
                    big_to_chip[w][k].start()
            for i in range(n_small):
                small_all[i][my_chip] = small_own[i][...] + small_recv[i][...]
                for k in range(N_REL):
                    small_to_chip[i][k].start()

        @pl.when((s > 0) & (t == 0))
        def _():
            k = s - 1
            cp = to_sibling(k)
            cp.wait_recv()
            cp.wait_send()
            send_ici[k] = (part[k % 2, my_rows, :] + recv_d2d[k]).astype(BF16)
            to_chip(k).start()

        r = _mm_tn(dp_ref[...], h_ref[...])
        odd = shard_of_slot(s, shard_ref[0]) % 2
        for parity in range(2):
            rows = r[64 * parity:64 * parity + SHARD_ROWS]

            @pl.when((odd == parity) & (t == 0))
            def _():
                part[s % 2] = rows

            @pl.when((odd == parity) & (t > 0))
            def _():
                part[s % 2] += rows

        @pl.when(t == n_sub - 1)
        def _():
            to_sibling(s).start()

        @pl.when((s == last) & (t == n_sub - 1))
        def _():
            cp = to_sibling(last)
            cp.wait_recv()
            cp.wait_send()
            total = part[last % 2, my_rows, :] + recv_d2d[last]
            for k in range(last):
                to_chip(k).wait_recv()
                total = total + recv_ici[k].astype(F32)
            mine_buf[...] = total
            swap.start()
            out_mine = pltpu.make_async_copy(mine_buf, out_hbm.at[my_rows, :], local_sems.at[0])
            out_mine.start()
            swap.wait_recv()
            out_other = pltpu.make_async_copy(other_buf, out_hbm.at[other_rows, :], local_sems.at[1])
            out_other.start()
            stores = [out_mine, out_other]
            for w in range(n_big):
                rows = big_half[w][0]
                total = big_own[w][my_chip, 0] + big_recv[w][my_chip, 0]
                for k in range(N_REL):
                    big_to_chip[w][k].wait_recv()
                    total = total + big_land[w][k].astype(F32)
                big_mine[w][...] = total
                big_swap[w].start()
                stores.append(pltpu.make_async_copy(
                    big_mine[w], big_out[w].at[pl.ds(pl.multiple_of(c * rows, 8), rows), :],
                    local_sems.at[loc_out_big + 2 * w]))
                stores[-1].start()
            for w in range(n_big):
                rows = big_half[w][0]
                big_swap[w].wait_recv()
                stores.append(pltpu.make_async_copy(
                    big_other[w], big_out[w].at[pl.ds(pl.multiple_of((1 - c) * rows, 8), rows), :],
                    local_sems.at[loc_out_big + 2 * w + 1]))
                stores[-1].start()
            for i in range(n_small):
                for k in range(N_REL):
                    small_to_chip[i][k].wait_recv()
                stores.append(pltpu.make_async_copy(small_all[i], small_out[i], local_sems.at[loc_out_small + i]))
                stores[-1].start()
            for k in range(last):
                to_chip(k).wait_send()
            swap.wait_send()
            for w in range(n_big):
                for k in range(N_REL):
                    big_to_chip[w][k].wait_send()
                big_swap[w].wait_send()
            for i in range(n_small):
                for k in range(N_REL):
                    small_to_chip[i][k].wait_send()
            for cp in stores:
                cp.wait()

    half = (SHARD_HALF, D_MODEL)
    vmem = pltpu.VMEM
    scratch = [vmem((2, SHARD_ROWS, D_MODEL), F32), vmem((N_CHIPS,) + half, F32),
               vmem((N_REL,) + half, BF16), vmem((N_REL,) + half, BF16), vmem(half, F32), vmem(half, F32)]
    scratch += [vmem((N_CHIPS, 1) + hs, F32) for hs in big_half] * 2
    scratch += [vmem((N_REL,) + hs, BF16) for hs in big_half] * 2
    scratch += [vmem(hs, F32) for hs in big_half] * 2
    scratch += [vmem(a.shape, F32) for a in small] * 2 + [vmem((N_CHIPS,) + a.shape, F32) for a in small]
    scratch += [pltpu.SemaphoreType.DMA((n_sems,)), pltpu.SemaphoreType.DMA((n_sems,)),
                pltpu.SemaphoreType.DMA((n_local,))]
    n_hbm = n_big + n_small
    out = pl.pallas_call(
        body, name="reduce_gradients",
        out_shape=[jax.ShapeDtypeStruct((SHARD_ROWS, D_MODEL), F32)]
        + [jax.ShapeDtypeStruct((2 * hs[0], hs[1]), F32) for hs in big_half]
        + [jax.ShapeDtypeStruct((N_CHIPS,) + a.shape, F32) for a in small],
        grid_spec=pltpu.PrefetchScalarGridSpec(
            num_scalar_prefetch=1, grid=(N_CHIPS, n_sub),
            in_specs=[pl.BlockSpec((pl.Element(tile), pl.Element(SHARD_WINDOW)),
                                   lambda s, t, m: (t * tile, _shard_window_start(shard_of_slot(s, m[0])))),
                      pl.BlockSpec((tile, D_MODEL), lambda s, t, m: (t, 0))] + [ANY_SPEC] * n_hbm,
            out_specs=[ANY_SPEC] * (1 + n_hbm),
            scratch_shapes=scratch),
        compiler_params=pltpu.CompilerParams(vmem_limit_bytes=VMEM_LIMIT),
    )(shard_arr, dproj, h, *big, *small)
    return out[:1 + n_big], out[1 + n_big:]


def _memkv_backward(mem, dmkv, g_mem, w_mkv):
    n_ex = mem.shape[0]

    def body(mem_ref, d_ref, g_ref, w_ref, dw_ref, dg_ref):
        @pl.when(pl.program_id(0) == 0)
        def _():
            dw_ref[...] = jnp.zeros_like(dw_ref)
            dg_ref[...] = jnp.zeros_like(dg_ref)

        m = mem_ref[0]
        mn = m * lax.rsqrt(jnp.mean(m * m, axis=-1, keepdims=True) + EPS)
        d_b = d_ref[0].astype(BF16)
        dw_ref[...] += _mm_tn((mn * g_ref[...]).astype(BF16), d_b)
        dg_ref[...] += jnp.sum(_mm_nt(d_b, w_ref[...]) * mn, axis=0, keepdims=True)

    return pl.pallas_call(
        body, name="memkv_backward", grid=(n_ex,),
        out_shape=[jax.ShapeDtypeStruct((D_MODEL, 2 * MEM_WIDTH), F32), jax.ShapeDtypeStruct((1, D_MODEL), F32)],
        in_specs=[pl.BlockSpec((1, MEM_LEN, D_MODEL), lambda b: (b, 0, 0)),
                  pl.BlockSpec((1, MEM_LEN, 2 * MEM_WIDTH), lambda b: (b, 0, 0)),
                  _full_spec((1, D_MODEL)), _full_spec((D_MODEL, 2 * MEM_WIDTH))],
        out_specs=[_full_spec((D_MODEL, 2 * MEM_WIDTH)), _full_spec((1, D_MODEL))],
    )(mem, dmkv, g_mem, w_mkv)


def _pack_small_grads(dgpre, dgpost, dgmem, dvg, dvb, dws, dbs, dsink, drel, loss_vec, buckets):
    def body(dgpre_ref, dgpost_ref, dgmem_ref, dvg_ref, dvb_ref, dws_ref, dbs_ref, dsink_ref, drel_ref, loss_ref,
             bk_ref, a_ref, b_ref):
        a_ref[...] = jnp.zeros_like(a_ref)
        b_ref[...] = jnp.zeros_like(b_ref)
        a_ref[0:1, :] = dgpre_ref[...]
        a_ref[1:2, :] = dgpost_ref[...]
        a_ref[2:3, :] = dgmem_ref[...]
        a_ref[3:4, :] = jnp.concatenate([dvg_ref[...], dvb_ref[...]], axis=-1)
        a_ref[ROW_LOSS:ROW_LOSS + 1, 0:128] = loss_ref[...]
        row = lax.broadcasted_iota(jnp.int32, (CHUNK, CHUNK), 0)
        col = lax.broadcasted_iota(jnp.int32, (CHUNK, CHUNK), 1)
        for g in range(A_GROUPS):
            b_ref[ROW_WS + g * CHUNK:ROW_WS + (g + 1) * CHUNK, :] = jnp.where(row >= col, dws_ref[g], 0.0)
            by_token = jnp.transpose(dbs_ref[:, g * 128:(g + 1) * 128])
            b_ref[ROW_BS + g:ROW_BS + g + 1, :] = jnp.sum(by_token, axis=0, keepdims=True)
        b_ref[ROW_SINK:ROW_SINK + 1, :] = dsink_ref[...]
        bk = bk_ref[...]
        rel_row = lax.broadcasted_iota(jnp.int32, (8, 128), 0)
        rel_col = lax.broadcasted_iota(jnp.int32, (8, 128), 1)
        rel = jnp.zeros((8, 128), F32)
        for h in range(4):
            acc = drel_ref[h * CHUNK:(h + 1) * CHUNK, :]
            for b in range(N_BUCKETS):
                rel = jnp.where((rel_row == h) & (rel_col == b), jnp.sum(jnp.where(bk == b, acc, 0.0)), rel)
        b_ref[ROW_REL:ROW_REL + 8, :] = rel

    return pl.pallas_call(
        body, name="pack_small_grads",
        out_shape=[jax.ShapeDtypeStruct((SMALL_A_ROWS, D_MODEL), F32), jax.ShapeDtypeStruct((SMALL_B_ROWS, 128), F32)],
        in_specs=[VMEM_SPEC] * 11, out_specs=[VMEM_SPEC] * 2,
    )(dgpre, dgpost, dgmem, dvg, dvb, dws, dbs, dsink, drel, loss_vec, buckets)


def _adamw(w, g, m, v):
    m2 = ADAM_B1 * m + (1.0 - ADAM_B1) * g
    v2 = ADAM_B2 * v + (1.0 - ADAM_B2) * (g * g)
    m_hat = m2 / (1.0 - ADAM_B1 ** ADAM_STEP)
    v_hat = v2 / (1.0 - ADAM_B2 ** ADAM_STEP)
    delta = -ADAM_LR * (m_hat / (jnp.sqrt(v_hat) + ADAM_EPS) + ADAM_WD * w)
    return delta, m2, v2


ADAM_MAX_ROWS = 176


def _adamw_whole(g, w, m, v, name):
    rows, cols = w.shape
    steps = -(-rows // ADAM_MAX_ROWS)
    block_rows = rows // steps
    assert block_rows * steps == rows and block_rows % 8 == 0

    def body(g_ref, w_ref, m_ref, v_ref, d_out, m_out, v_out):
        delta, m2, v2 = _adamw(w_ref[...], g_ref[...], m_ref[...], v_ref[...])
        d_out[...] = delta
        m_out[...] = m2
        v_out[...] = v2

    block = pl.BlockSpec((block_rows, cols), lambda k: (k, 0))
    out = pl.pallas_call(
        body, name=name, grid=(steps,), out_shape=[jax.ShapeDtypeStruct((rows, cols), F32)] * 3,
        in_specs=[block] * 4, out_specs=[block] * 3,
    )(g, w, m, v)
    return [g] + list(out)


def _adamw_small(ra, rb, weights, moments_m, moments_v):
    n = len(weights)

    def body(*refs):
        ra_ref, rb_ref = refs[0], refs[1]
        w_refs, m_refs, v_refs = refs[2:2 + n], refs[2 + n:2 + 2 * n], refs[2 + 2 * n:2 + 3 * n]
        outs = refs[2 + 3 * n:]
        g_outs, d_outs, m_outs, v_outs = outs[:n], outs[n:2 * n], outs[2 * n:3 * n], outs[3 * n:4 * n]
        ga, gb = ra_ref[0], rb_ref[0]
        for chip in range(1, N_CHIPS):
            ga = ga + ra_ref[chip]
            gb = gb + rb_ref[chip]
        outs[4 * n][...] = ga[ROW_LOSS:ROW_LOSS + 1, 0:128]
        grads = [ga[0:1, :], ga[1:2, :], ga[2:3, :], ga[3:4, :A_WIDTH], ga[3:4, A_WIDTH:],
                 gb[ROW_WS:ROW_WS + A_GROUPS * CHUNK, :].reshape(A_GROUPS, CHUNK, CHUNK),
                 gb[ROW_BS:ROW_BS + A_GROUPS, :], gb[ROW_SINK:ROW_SINK + 1, 0:4],
                 gb[ROW_REL:ROW_REL + 4, 0:N_BUCKETS]]
        for k in range(n):
            delta, m2, v2 = _adamw(w_refs[k][...], grads[k], m_refs[k][...], v_refs[k][...])
            g_outs[k][...] = grads[k]
            d_outs[k][...] = delta
            m_outs[k][...] = m2
            v_outs[k][...] = v2

    out_shape = [jax.ShapeDtypeStruct(w.shape, F32) for w in weights] * 4 + [jax.ShapeDtypeStruct((1, 128), F32)]
    return pl.pallas_call(
        body, name="adamw_small", out_shape=out_shape,
        in_specs=[VMEM_SPEC] * (2 + 3 * n), out_specs=[VMEM_SPEC] * (4 * n + 1),
    )(ra, rb, *weights, *moments_m, *moments_v)


def kernel(x, mem, pre_norm_g, post_norm_g, mem_norm_g, w_in, w_mem_kv, v_norm_g, v_norm_b, w_spatial, b_spatial, attn_sinks, rel_bias, w_out, loss_target, m_pre_norm_g, m_post_norm_g, m_mem_norm_g, m_w_in, m_w_mem_kv, m_v_norm_g, m_v_norm_b, m_w_spatial, m_b_spatial, m_attn_sinks, m_rel_bias, m_w_out, v_pre_norm_g, v_post_norm_g, v_mem_norm_g, v_w_in, v_w_mem_kv, v_v_norm_g, v_v_norm_b, v_w_spatial, v_b_spatial, v_attn_sinks, v_rel_bias, v_w_out):
    n_ex, seq, _ = x.shape
    n_tok = n_ex * seq
    x2 = x.reshape(n_tok, D_MODEL)
    tgt2 = loss_target.reshape(n_tok, D_MODEL)
    buckets = jnp.asarray(_bucket_map())
    shard_arr = (2 * lax.axis_index("x") + lax.axis_index("y")).astype(jnp.int32).reshape(1)
    w_sp = w_spatial[0]
    b_sp = jnp.broadcast_to(b_spatial[0][:, :, None], (A_GROUPS, CHUNK, CHUNK))
    w_in_t, m_w_in_t, v_w_in_t = (jnp.transpose(a[0]) for a in (w_in, m_w_in, v_w_in))
    rel_t, m_rel_t, v_rel_t = (jnp.transpose(a) for a in (rel_bias, m_rel_bias, v_rel_bias))

    x_arr = lax.axis_index("x").astype(jnp.int32).reshape(1)
    h_b, parts, (w_in_b, g_mkv, g_out) = _gather_and_project(x2, pre_norm_g, w_in_t, w_mem_kv[0], w_out[0], x_arr)
    w_mkv_b = g_mkv.reshape(D_MODEL, 2 * MEM_WIDTH)
    w_out_b = g_out.reshape(MIX_WIDTH, D_MODEL)

    bias = _make_bias(rel_t, buckets)
    mkv = _memkv_forward(mem, mem_norm_g, w_mkv_b)
    dout, do, loss_vec, dgpost = _forward_mix(parts, mkv, x2, tgt2, v_norm_g, v_norm_b, w_sp, b_sp, attn_sinks, bias,
                                             w_out_b, post_norm_g, n_ex, seq)

    dproj, dmkv, dwout, dvg, dvb, dws, dbs, dsink, drel = _backward_mix(
        parts, mkv, do, v_norm_g, v_norm_b, w_sp, b_sp, attn_sinks, bias, w_out_b, n_ex, seq)
    dx, dgpre = _backward_projection(x2, dout, dproj, pre_norm_g, w_in_b)
    dwmkv, dgmem = _memkv_backward(mem, dmkv, mem_norm_g, w_mkv_b)
    small_a, small_b = _pack_small_grads(dgpre, dgpost, dgmem, dvg, dvb, dws, dbs, dsink, drel, loss_vec, buckets)

    shard_shapes = [w_mem_kv.shape[1:], w_out.shape[1:]]
    big = [g.reshape(N_CHIPS, 2, s[0] // 2, s[1]) for g, s in zip((dwmkv, dwout), shard_shapes)]
    (g_win, g_wmkv, g_wout), (ga, gb) = _reduce_gradients(dproj, h_b, big, [small_a, small_b], shard_arr)

    big_out = [_adamw_whole(g_win, w_in_t, m_w_in_t, v_w_in_t, "adamw_w_in"),
               _adamw_whole(g_wmkv, w_mem_kv[0], m_w_mem_kv[0], v_w_mem_kv[0], "adamw_w_mem_kv"),
               _adamw_whole(g_wout, w_out[0], m_w_out[0], v_w_out[0], "adamw_w_out")]
    small_w = [pre_norm_g, post_norm_g, mem_norm_g, v_norm_g, v_norm_b, w_sp, b_spatial[0], attn_sinks, rel_t]
    small_m = [m_pre_norm_g, m_post_norm_g, m_mem_norm_g, m_v_norm_g, m_v_norm_b, m_w_spatial[0], m_b_spatial[0],
               m_attn_sinks, m_rel_t]
    small_v = [v_pre_norm_g, v_post_norm_g, v_mem_norm_g, v_v_norm_g, v_v_norm_b, v_w_spatial[0], v_b_spatial[0],
               v_attn_sinks, v_rel_t]
    small_out = _adamw_small(ga, gb, small_w, small_m, small_v)
    n_small = len(small_w)

    outputs = [small_out[4 * n_small][0, 0], dx.reshape(x.shape)]
    for kind in range(4):
        s = small_out[kind * n_small:(kind + 1) * n_small]
        outputs += [s[0], s[1], s[2], jnp.transpose(big_out[0][kind])[None], big_out[1][kind][None], s[3], s[4],
                    s[5][None], s[6][None], s[7], jnp.transpose(s[8]), big_out[2][kind][None]]
    return tuple(outputs)
```

```python
import functools

import numpy as np
import jax
import jax.numpy as jnp
from jax import lax
from jax.experimental import pallas as pl
from jax.experimental.pallas import tpu as pltpu

F32 = jnp.float32
BF16 = jnp.bfloat16
MESH = pl.DeviceIdType.MESH

D_MODEL = 1024
CHUNK = 128
A_WIDTH = 512
A_GROUPS = 4
SWA_WIDTH = 256
KV_WIDTH = 128
MEM_WIDTH = 256
MEM_LEN = 256
MIX_WIDTH = 1024
IN_WIDTH = 2816
N_BUCKETS = 32
MAX_DISTANCE = 128
EPS = 1e-6
NEG = -1e30
QK_SCALE = 0.125
HALF_HEAD_PAIR = 64

ADAM_LR = 0.001
ADAM_B1 = 0.9
ADAM_B2 = 0.999
ADAM_EPS = 1e-08
ADAM_WD = 0.01
ADAM_STEP = 10

N_CHIPS = 4
TILE_CHUNKS = 4
TILE = TILE_CHUNKS * CHUNK
PROJ_TILE = 256
VMEM_LIMIT = 56 * 1024 * 1024

SMALL_A_ROWS = 8
ROW_LOSS = 4
ROW_WS = 0
ROW_BS = 512
ROW_SINK = 520
ROW_REL = 528
SMALL_B_ROWS = 536


def _mm(a, b):
    return lax.dot_general(a, b, (((1,), (0,)), ((), ())), preferred_element_type=F32)


def _mm_nt(a, b):
    return lax.dot_general(a, b, (((1,), (1,)), ((), ())), preferred_element_type=F32)


def _mm_tn(a, b):
    return lax.dot_general(a, b, (((0,), (0,)), ((), ())), preferred_element_type=F32)


def _bucket_map():
    qi = np.arange(CHUNK)[:, None]
    kj = np.arange(2 * CHUNK)[None, :]
    n = np.maximum(qi + CHUNK - kj, 0)
    max_exact = N_BUCKETS // 2
    large = max_exact + (np.log(np.maximum(n, 1) / max_exact) / np.log(MAX_DISTANCE / max_exact)
                         * (N_BUCKETS - max_exact)).astype(np.int32)
    large = np.minimum(large, N_BUCKETS - 1)
    return np.where(n < max_exact, n, large).astype(np.int32)


_GELU_C = 0.7978845608028654
_GELU_A = 0.044715


def _gelu(x):
    t = jnp.tanh(_GELU_C * (x + _GELU_A * x * x * x))
    return 0.5 * x * (1.0 + t), t


def _gelu_grad(x, t):
    return 0.5 * (1.0 + t) + 0.5 * x * (1.0 - t * t) * (_GELU_C * (1.0 + 3.0 * _GELU_A * x * x))


def _sigmoid(x):
    return 1.0 / (1.0 + jnp.exp(-x))


def _lane_lo(shape):
    return lax.broadcasted_iota(jnp.int32, shape, 1) < HALF_HEAD_PAIR


def _swa_variants(t):
    lo = _lane_lo(t.shape)
    tr = pltpu.roll(t, HALF_HEAD_PAIR, 1)
    zero = jnp.zeros_like(t)
    return (jnp.where(lo, t, zero).astype(BF16), jnp.where(lo, zero, tr).astype(BF16),
            jnp.where(lo, tr, zero).astype(BF16), jnp.where(lo, zero, t).astype(BF16))


def _swa_unvariants(d0, d1, d2, d3):
    lo = _lane_lo(d0.shape)
    zero = jnp.zeros_like(d0)
    rolled = jnp.where(lo, zero, d1) + jnp.where(lo, d2, zero)
    return jnp.where(lo, d0, zero) + jnp.where(lo, zero, d3) + pltpu.roll(rolled, HALF_HEAD_PAIR, 1)


def _mem_variants(t):
    out = []
    for pair in range(2):
        tp = t[:, pair * 128:(pair + 1) * 128]
        lo = _lane_lo(tp.shape)
        zero = jnp.zeros_like(tp)
        out.append(jnp.where(lo, tp, zero).astype(BF16))
        out.append(jnp.where(lo, zero, tp).astype(BF16))
    return out


def _mem_unvariants(d0, d1, d2, d3):
    lo = _lane_lo(d0.shape)
    return jnp.concatenate([jnp.where(lo, d0, d1), jnp.where(lo, d2, d3)], axis=-1)


def _softmax(logits, sinks):
    m = jnp.max(logits, axis=-1, keepdims=True)
    if sinks is not None:
        m = jnp.maximum(m, sinks)
    p = jnp.exp(logits - m)
    den = jnp.sum(p, axis=-1, keepdims=True)
    if sinks is None:
        return p * (1.0 / den), None
    es = jnp.exp(sinks - m)
    inv = 1.0 / (den + es)
    return p * inv, es * inv


def _band_valid(with_prev):
    qi = lax.broadcasted_iota(jnp.int32, (CHUNK, 2 * CHUNK), 0)
    kj = lax.broadcasted_iota(jnp.int32, (CHUNK, 2 * CHUNK), 1)
    in_cur = (kj >= CHUNK) & (kj - CHUNK <= qi)
    if not with_prev:
        return in_cur
    return in_cur | ((kj < CHUNK) & (kj > qi))


def _causal_weights(ws_ref):
    row = lax.broadcasted_iota(jnp.int32, (CHUNK, CHUNK), 0)
    col = lax.broadcasted_iota(jnp.int32, (CHUNK, CHUNK), 1)
    return [jnp.where(row >= col, ws_ref[g], 0.0).astype(BF16) for g in range(A_GROUPS)]


def _rows_to_lanes(a, n):
    return jnp.concatenate([a[c * CHUNK:(c + 1) * CHUNK] for c in range(n)], axis=1)


def _lanes_to_rows(a, n):
    w = a.shape[1] // n
    return jnp.concatenate([a[:, c * w:(c + 1) * w] for c in range(n)], axis=0)


def _stack_heads(pair01, pair23):
    return jnp.concatenate([pair01[:, :256], pair01[:, 256:], pair23[:, :256], pair23[:, 256:]], axis=0)


def _pair_heads(s, r):
    return (jnp.concatenate([s[0:r], s[r:2 * r]], axis=1), jnp.concatenate([s[2 * r:3 * r], s[3 * r:4 * r]], axis=1))


def _pair_operands(variants):
    return (jnp.concatenate(variants[0:2], axis=0), jnp.concatenate(variants[2:4], axis=0))


def _split_pair_grads(d_pairs):
    return d_pairs[0][:256], d_pairs[0][256:], d_pairs[1][:256], d_pairs[1][256:]


def _halves_bf16(a):
    return (a[:, :128].astype(BF16), a[:, 128:].astype(BF16))


def _group_a_forward(au, av, vg, vb, wm, bs_rows):
    gu, tu = _gelu(au)
    gv, tv = _gelu(av)
    ya, res = [], []
    for g in range(A_GROUPS):
        sl = slice(g * 128, (g + 1) * 128)
        xg = gv[:, sl]
        xc = xg - jnp.mean(xg, axis=-1, keepdims=True)
        rstd = lax.rsqrt(jnp.mean(xc * xc, axis=-1, keepdims=True) + EPS)
        xhat = xc * rstd
        vn = _rows_to_lanes((xhat * vg[:, sl] + vb[:, sl]).astype(BF16), TILE_CHUNKS)
        s = _lanes_to_rows(_mm(wm[g], vn), TILE_CHUNKS) + bs_rows[g]
        ya.append(gu[:, sl] * s)
        res.append((xhat, rstd, vn, s))
    return ya, dict(gu=gu, tu=tu, tv=tv, groups=res)


def _attention_probs(qp, k_pairs, bias, sink_col):
    logits = _stack_heads(_mm_nt(qp[0], k_pairs[0]), _mm_nt(qp[1], k_pairs[1])) * QK_SCALE
    if bias is not None:
        logits = logits + bias
    return _softmax(logits, sink_col)


def _attention_out(p, v_pairs, r):
    pp = _pair_heads(p.astype(BF16), r)
    return jnp.concatenate([_mm(pp[0], v_pairs[0]), _mm(pp[1], v_pairs[1])], axis=-1), pp


def _attention_backward(p, pp, do_pairs, qp, k_pairs, v_pairs, r):
    dp = _stack_heads(_mm_nt(do_pairs[0], v_pairs[0]), _mm_nt(do_pairs[1], v_pairs[1]))
    delta = jnp.sum(p * dp, axis=-1, keepdims=True)
    dl = p * (dp - delta)
    dlp = _pair_heads(dl.astype(BF16), r)
    dq = jnp.concatenate([_mm(dlp[0], k_pairs[0]), _mm(dlp[1], k_pairs[1])], axis=-1)
    dk = (_mm_tn(dlp[0], qp[0]), _mm_tn(dlp[1], qp[1]))
    dv = (_mm_tn(pp[0], do_pairs[0]), _mm_tn(pp[1], do_pairs[1]))
    return dl, delta, dq, dk, dv


def _tile_specs(n_tiles_ex, width):
    return pl.BlockSpec((TILE, width), lambda b, i: (b * n_tiles_ex + jnp.minimum(i, n_tiles_ex - 1), 0))


def _prev_chunk_spec(n_tiles_ex, width):
    def index(b, i):
        chunk = TILE_CHUNKS * jnp.minimum(i, n_tiles_ex - 1)
        return (b * n_tiles_ex * TILE_CHUNKS + jnp.maximum(chunk - 1, 0), 0)
    return pl.BlockSpec((CHUNK, width), index)


def _full_spec(shape):
    zeros = (0,) * len(shape)
    return pl.BlockSpec(shape, lambda *_: zeros)


SMEM_SPEC = pl.BlockSpec(memory_space=pltpu.SMEM)
ANY_SPEC = pl.BlockSpec(memory_space=pl.ANY)
VMEM_SPEC = pl.BlockSpec(memory_space=pltpu.VMEM)


def _make_bias(rel_bias_t, buckets):
    def body(rel_ref, bk_ref, out_ref):
        bk = bk_ref[...]
        for h in range(4):
            acc = jnp.zeros((CHUNK, 2 * CHUNK), F32)
            for b in range(N_BUCKETS):
                acc = jnp.where(bk == b, rel_ref[h, b], acc)
            for t, with_prev in enumerate((True, False)):
                out_ref[t, h * CHUNK:(h + 1) * CHUNK, :] = jnp.where(_band_valid(with_prev), acc, NEG)

    return pl.pallas_call(
        body, name="make_bias", out_shape=jax.ShapeDtypeStruct((2, 4 * CHUNK, 2 * CHUNK), F32),
        in_specs=[SMEM_SPEC, VMEM_SPEC], out_specs=VMEM_SPEC,
    )(rel_bias_t, buckets)


def _memkv_forward(mem, g_mem, w_mkv):
    n_ex = mem.shape[0]

    def body(mem_ref, g_ref, w_ref, out_ref):
        m = mem_ref[0]
        r = lax.rsqrt(jnp.mean(m * m, axis=-1, keepdims=True) + EPS)
        out_ref[0] = _mm((m * r * g_ref[...]).astype(BF16), w_ref[...])

    return pl.pallas_call(
        body, name="memkv_forward", grid=(n_ex,),
        out_shape=jax.ShapeDtypeStruct((n_ex, MEM_LEN, 2 * MEM_WIDTH), F32),
        in_specs=[pl.BlockSpec((1, MEM_LEN, D_MODEL), lambda b: (b, 0, 0)), _full_spec((1, D_MODEL)),
                  _full_spec((D_MODEL, 2 * MEM_WIDTH))],
        out_specs=pl.BlockSpec((1, MEM_LEN, 2 * MEM_WIDTH), lambda b: (b, 0, 0)),
    )(mem, g_mem, w_mkv)


PROJ_WIDTHS = (A_WIDTH, A_WIDTH, SWA_WIDTH, KV_WIDTH, KV_WIDTH, MEM_WIDTH, MIX_WIDTH)
PROJ_OFFSETS = tuple(int(v) for v in np.cumsum((0,) + PROJ_WIDTHS))


HALF_WIDTH = IN_WIDTH // 2
HALF_PARTS = ((0, 1, 2, 3), (4, 5, 6))


def _gather_and_project(x2, g_pre, w_in_s, w_mkv_s, w_out_s, x_arr):
    n_tok = x2.shape[0]
    n_tiles = n_tok // PROJ_TILE
    last = n_tiles - 1
    shapes = [w_in_s.shape, w_mkv_s.shape, w_out_s.shape]
    n_w = len(shapes)

    def body(x_sref, x_ref, g_ref, win_hbm, wmkv_hbm, wout_hbm, h_ref, *refs):
        part_refs, refs = refs[:len(PROJ_WIDTHS)], refs[len(PROJ_WIDTHS):]
        gin_hbm, gmkv_hbm, gout_hbm, wg, stage_in, stage_mkv, stage_out, own_mkv, own_out = refs[:9]
        send_sems, recv_sems, local_sems = refs[9:]
        p, t = pl.program_id(0), pl.program_id(1)
        x, y, c = lax.axis_index("x"), lax.axis_index("y"), lax.axis_index("c")
        me, sibling = (x, y, c), (x, y, 1 - c)
        my_shard = 2 * x + y
        gathered = [wg, gmkv_hbm, gout_hbm]

        def half_rows(w, shard, half):
            rows = shapes[w][0] // 2
            if w == 0:
                return wg.at[pl.ds(pl.multiple_of(shard * shapes[0][0] + half * rows, 16), rows), :]
            return gathered[w].at[shard, pl.ds(half * rows, rows), :]

        def first(w, rel):
            src = half_rows(w, my_shard, c) if w == 0 else (own_mkv, own_out)[w - 1].at[
                pl.ds(c * (shapes[w][0] // 2), shapes[w][0] // 2), :]
            k = 3 * w + rel - 1
            return pltpu.make_async_remote_copy(
                src_ref=src, dst_ref=half_rows(w, my_shard, c), send_sem=send_sems.at[k], recv_sem=recv_sems.at[k],
                device_id=(x ^ (rel >> 1), y ^ (rel & 1), c), device_id_type=MESH)

        def landed(w, rel):
            k = 3 * w + rel - 1
            ref = half_rows(w, my_shard ^ rel, c)
            return pltpu.make_async_remote_copy(src_ref=ref, dst_ref=ref, send_sem=send_sems.at[k],
                                                recv_sem=recv_sems.at[k], device_id=me, device_id_type=MESH)

        def passed(w, rel, half, to):
            k = 9 + 3 * w + rel - 1
            ref = half_rows(w, my_shard ^ rel, half)
            return pltpu.make_async_remote_copy(src_ref=ref, dst_ref=ref, send_sem=send_sems.at[k],
                                                recv_sem=recv_sems.at[k], device_id=to, device_id_type=MESH)

        def pass_on(w, rels):
            for rel in rels:
                landed(w, rel).wait_recv()
                passed(w, rel, c, sibling).start()
            for rel in rels:
                passed(w, rel, 1 - c, me).wait_recv()

        own_stores = [pltpu.make_async_copy(own_mkv, gmkv_hbm.at[my_shard], local_sems.at[3]),
                      pltpu.make_async_copy(own_out, gout_hbm.at[my_shard], local_sems.at[4])]

        @pl.when((p == 0) & (t == 0))
        def _():
            loads = [pltpu.make_async_copy(src, dst, local_sems.at[k]) for k, (src, dst) in enumerate(
                ((win_hbm, stage_in), (wmkv_hbm, stage_mkv), (wout_hbm, stage_out)))]
            for cp in loads:
                cp.start()
            loads[0].wait()
            wg[pl.ds(pl.multiple_of(my_shard * shapes[0][0], 16), shapes[0][0]), :] = stage_in[...].astype(BF16)
            for rel in (1, 2):
                first(0, rel).start()
            loads[1].wait()
            loads[2].wait()
            own_mkv[...] = stage_mkv[...].astype(BF16)
            own_out[...] = stage_out[...].astype(BF16)
            for cp in own_stores:
                cp.start()
            pass_on(0, (1,))
            first(0, 3).start()

        @pl.when((p == 1) & (t == 0))
        def _():
            pass_on(0, (2, 3))
            for w in (1, 2):
                for rel in (1, 2, 3):
                    first(w, rel).start()

        xv = x_ref[...]
        r = lax.rsqrt(jnp.mean(xv * xv, axis=-1, keepdims=True) + EPS)
        h = (xv * r * g_ref[...]).astype(BF16)

        @pl.when(p == 0)
        def _():
            h_ref[...] = h

        for hh in range(2):
            @pl.when((p ^ x_sref[0]) == hh)
            def _():
                proj = _mm_nt(h, wg[hh * HALF_WIDTH:(hh + 1) * HALF_WIDTH, :])
                for k in HALF_PARTS[hh]:
                    lo = PROJ_OFFSETS[k] - hh * HALF_WIDTH
                    part_refs[k][...] = proj[:, lo:lo + PROJ_WIDTHS[k]]

        @pl.when((p == 1) & (t == last))
        def _():
            store = pltpu.make_async_copy(wg, gin_hbm, local_sems.at[5])
            store.start()
            for w in (1, 2):
                pass_on(w, (1, 2, 3))
            for w in range(n_w):
                for rel in (1, 2, 3):
                    first(w, rel).wait_send()
                    passed(w, rel, c, sibling).wait_send()
            for cp in own_stores:
                cp.wait()
            store.wait()

    def active_in(hh):
        def index(p, t, xs):
            return (jnp.where((p ^ xs[0]) == hh, t, jnp.where(p == 0, 0, last)), 0)
        return index

    part_specs = [pl.BlockSpec((PROJ_TILE, PROJ_WIDTHS[k]), active_in(hh)) for hh in range(2) for k in HALF_PARTS[hh]]
    vmem = pltpu.VMEM
    out = pl.pallas_call(
        body, name="gather_and_project",
        out_shape=[jax.ShapeDtypeStruct((n_tok, D_MODEL), BF16)]
        + [jax.ShapeDtypeStruct((n_tok, w), F32) for w in PROJ_WIDTHS]
        + [jax.ShapeDtypeStruct((N_CHIPS * shapes[0][0], shapes[0][1]), BF16)]
        + [jax.ShapeDtypeStruct((N_CHIPS,) + s, BF16) for s in shapes[1:]],
        grid_spec=pltpu.PrefetchScalarGridSpec(
            num_scalar_prefetch=1, grid=(2, n_tiles),
            in_specs=[pl.BlockSpec((PROJ_TILE, D_MODEL), lambda p, t, xs: (t, 0)),
                      pl.BlockSpec((1, D_MODEL), lambda p, t, xs: (0, 0)), ANY_SPEC, ANY_SPEC, ANY_SPEC],
            out_specs=[pl.BlockSpec((PROJ_TILE, D_MODEL), lambda p, t, xs: (jnp.where(p == 0, t, last), 0))]
            + part_specs + [ANY_SPEC] * 3,
            scratch_shapes=[vmem((N_CHIPS * shapes[0][0], shapes[0][1]), BF16), vmem(shapes[0], F32),
                            vmem(shapes[1], F32), vmem(shapes[2], F32), vmem(shapes[1], BF16), vmem(shapes[2], BF16),
                            pltpu.SemaphoreType.DMA((18,)), pltpu.SemaphoreType.DMA((18,)),
                            pltpu.SemaphoreType.DMA((6,))]),
        compiler_params=pltpu.CompilerParams(vmem_limit_bytes=VMEM_LIMIT),
    )(x_arr, x2, g_pre, w_in_s, w_mkv_s, w_out_s)
    h, parts, weights = out[0], out[1:1 + len(PROJ_WIDTHS)], out[1 + len(PROJ_WIDTHS):]
    return h, list(parts), weights


def _load_chunk(j, i, sk_ref, sv_ref, skp_ref, svp_ref):
    rows = slice(j * CHUNK, (j + 1) * CHUNK)
    if j == 0:
        k_prev, v_prev, table = skp_ref[...], svp_ref[...], jnp.where(i > 0, 0, 1)
    else:
        prev = slice((j - 1) * CHUNK, j * CHUNK)
        k_prev, v_prev, table = sk_ref[prev, :], sv_ref[prev, :], 0
    k_pairs = _pair_operands(_swa_variants(jnp.concatenate([k_prev, sk_ref[rows, :]], axis=0)))
    v_pairs = _pair_operands(_swa_variants(jnp.concatenate([v_prev, sv_ref[rows, :]], axis=0)))
    return rows, k_pairs, v_pairs, table


def _tile_constants(ws_ref, bs_ref, sink_ref, mkv_ref):
    wm = _causal_weights(ws_ref)
    bs_rows = [jnp.concatenate([bs_ref[g]] * TILE_CHUNKS, axis=0) for g in range(A_GROUPS)]
    sink_col = jnp.max(jnp.concatenate([jnp.full((CHUNK, 128), sink_ref[0, h], F32) for h in range(4)], axis=0),
                       axis=-1, keepdims=True)
    mkv_v = mkv_ref[0]
    mk_pairs = _pair_operands(_mem_variants(mkv_v[:, :MEM_WIDTH]))
    mv_pairs = _pair_operands(_mem_variants(mkv_v[:, MEM_WIDTH:]))
    return wm, bs_rows, sink_col, mk_pairs, mv_pairs


def _forward_mix(parts, mkv, x2, tgt2, v_g, v_b, w_sp, b_sp, sinks, bias, w_out, g_post, n_ex, seq):
    n_tiles_ex = seq // TILE
    n_tok = n_ex * seq
    au, av, sq, sk, sv, mq, z = parts

    def body(au_ref, av_ref, sq_ref, sk_ref, sv_ref, skp_ref, svp_ref, mq_ref, z_ref, mkv_ref, x_ref, tgt_ref,
             vg_ref, vb_ref, ws_ref, bs_ref, sink_ref, bias_ref, wout_ref, gpost_ref,
             dout_ref, do_ref, loss_ref, dgpost_ref):
        b, i = pl.program_id(0), pl.program_id(1)

        @pl.when((b == 0) & (i == 0))
        def _():
            loss_ref[...] = jnp.zeros_like(loss_ref)
            dgpost_ref[...] = jnp.zeros_like(dgpost_ref)

        wm, bs_rows, sink_col, mk_pairs, mv_pairs = _tile_constants(ws_ref, bs_ref, sink_ref, mkv_ref)
        ya, _ = _group_a_forward(au_ref[...], av_ref[...], vg_ref[...], vb_ref[...], wm, bs_rows)
        yb = []
        for j in range(TILE_CHUNKS):
            rows, k_pairs, v_pairs, table = _load_chunk(j, i, sk_ref, sv_ref, skp_ref, svp_ref)
            p, _ = _attention_probs(_halves_bf16(sq_ref[rows, :]), k_pairs, bias_ref[table], sink_col)
            yb.append(_attention_out(p, v_pairs, CHUNK)[0])
        pm, _ = _attention_probs(_halves_bf16(mq_ref[...]), mk_pairs, None, None)
        yc = _attention_out(pm, mv_pairs, TILE)[0]
        ycat = jnp.concatenate(ya + [jnp.concatenate(yb, axis=0), yc], axis=-1)
        zv = z_ref[...]
        y = ycat * (zv * _sigmoid(zv))
        o = _mm(y.astype(BF16), wout_ref[...])
        r2 = lax.rsqrt(jnp.mean(o * o, axis=-1, keepdims=True) + EPS)
        nrm = o * r2
        gp = gpost_ref[...]
        diff = x_ref[...] + nrm * gp - tgt_ref[...]
        loss_ref[...] += jnp.sum(diff * diff) * (0.5 / D_MODEL)
        dout = diff * (1.0 / D_MODEL)
        dout_ref[...] = dout
        dgpost_ref[...] += jnp.sum(dout * nrm, axis=0, keepdims=True)
        dn = dout * gp
        do_ref[...] = r2 * (dn - nrm * jnp.mean(dn * nrm, axis=-1, keepdims=True))

    tile = functools.partial(_tile_specs, n_tiles_ex)
    prev = functools.partial(_prev_chunk_spec, n_tiles_ex)
    return pl.pallas_call(
        body, name="forward_mix", grid=(n_ex, n_tiles_ex),
        out_shape=[jax.ShapeDtypeStruct((n_tok, D_MODEL), F32), jax.ShapeDtypeStruct((n_tok, D_MODEL), F32),
                   jax.ShapeDtypeStruct((1, 128), F32), jax.ShapeDtypeStruct((1, D_MODEL), F32)],
        in_specs=[tile(A_WIDTH), tile(A_WIDTH), tile(SWA_WIDTH), tile(KV_WIDTH), tile(KV_WIDTH),
                  prev(KV_WIDTH), prev(KV_WIDTH), tile(MEM_WIDTH), tile(MIX_WIDTH),
                  pl.BlockSpec((1, MEM_LEN, 2 * MEM_WIDTH), lambda b, i: (b, 0, 0)),
                  tile(D_MODEL), tile(D_MODEL),
                  _full_spec((1, A_WIDTH)), _full_spec((1, A_WIDTH)), _full_spec((A_GROUPS, CHUNK, CHUNK)),
                  _full_spec((A_GROUPS, CHUNK, CHUNK)), SMEM_SPEC, _full_spec((2, 4 * CHUNK, 2 * CHUNK)),
                  _full_spec((MIX_WIDTH, D_MODEL)), _full_spec((1, D_MODEL))],
        out_specs=[tile(D_MODEL), tile(D_MODEL), _full_spec((1, 128)), _full_spec((1, D_MODEL))],
        compiler_params=pltpu.CompilerParams(vmem_limit_bytes=VMEM_LIMIT),
    )(au, av, sq, sk, sv, sk, sv, mq, z, mkv, x2, tgt2, v_g, v_b, w_sp, b_sp, sinks, bias, w_out, g_post)


def _backward_mix(parts, mkv, do, v_g, v_b, w_sp, b_sp, sinks, bias, w_out, n_ex, seq):
    n_tiles_ex = seq // TILE
    n_tok = n_ex * seq
    au, av, sq, sk, sv, mq, z = parts
    col = dict(zip(("au", "av", "sq", "sk", "sv", "mq", "z"),
                   (slice(PROJ_OFFSETS[k], PROJ_OFFSETS[k + 1]) for k in range(len(PROJ_WIDTHS)))))
    before_kv, after_kv = slice(0, col["sk"].start), slice(col["sv"].stop, IN_WIDTH)

    def body(do_ref, au_ref, av_ref, sq_ref, sk_ref, sv_ref, skp_ref, svp_ref, mq_ref, z_ref, mkv_ref,
             vg_ref, vb_ref, ws_ref, bs_ref, sink_ref, bias_ref, wout_ref,
             dproj_ref, dmkv_ref, dwout_ref, dvg_ref, dvb_ref, dws_ref, dbs_ref, dsink_ref, drel_ref,
             carry_dp, carry_k, carry_v):
        b, i = pl.program_id(0), pl.program_id(1)

        @pl.when((b == 0) & (i == 0))
        def _():
            for ref in (dwout_ref, dvg_ref, dvb_ref, dws_ref, dbs_ref, dsink_ref, drel_ref):
                ref[...] = jnp.zeros_like(ref)

        @pl.when(i == 0)
        def _():
            dmkv_ref[...] = jnp.zeros_like(dmkv_ref)
            carry_k[...] = jnp.zeros_like(carry_k)
            carry_v[...] = jnp.zeros_like(carry_v)

        @pl.when(i > 0)
        def _():
            dproj_ref[:, before_kv] = carry_dp[:, before_kv]
            dproj_ref[:, after_kv] = carry_dp[:, after_kv]

        @pl.when(i < n_tiles_ex)
        def _():
            wm, bs_rows, sink_col, mk_pairs, mv_pairs = _tile_constants(ws_ref, bs_ref, sink_ref, mkv_ref)
            vg = vg_ref[...]
            do_b = do_ref[...].astype(BF16)
            dy = _mm_nt(do_b, wout_ref[...])
            zv = z_ref[...]
            sig = _sigmoid(zv)
            sz = zv * sig
            dyc = dy * sz

            au_v, av_v = au_ref[...], av_ref[...]
            ya, res = _group_a_forward(au_v, av_v, vg, vb_ref[...], wm, bs_rows)
            dgu, dgv = [], []
            for g in range(A_GROUPS):
                sl = slice(g * 128, (g + 1) * 128)
                xhat, rstd, vn, s = res["groups"][g]
                dya = dyc[:, sl]
                dgu.append(dya * s)
                ds = dya * res["gu"][:, sl]
                dbs_ref[:, sl] += sum(ds[c * CHUNK:(c + 1) * CHUNK] for c in range(TILE_CHUNKS))
                ds_b = _rows_to_lanes(ds.astype(BF16), TILE_CHUNKS)
                dws_ref[g] += _mm_nt(ds_b, vn)
                dvn = _lanes_to_rows(_mm_tn(wm[g], ds_b), TILE_CHUNKS)
                dvg_ref[:, sl] += jnp.sum(dvn * xhat, axis=0, keepdims=True)
                dvb_ref[:, sl] += jnp.sum(dvn, axis=0, keepdims=True)
                dxh = dvn * vg[:, sl]
                dgv.append(rstd * (dxh - jnp.mean(dxh, axis=-1, keepdims=True)
                                   - xhat * jnp.mean(dxh * xhat, axis=-1, keepdims=True)))
            carry_dp[:, col["au"]] = (jnp.concatenate(dgu, axis=-1) * _gelu_grad(au_v, res["tu"])).astype(BF16)
            carry_dp[:, col["av"]] = (jnp.concatenate(dgv, axis=-1) * _gelu_grad(av_v, res["tv"])).astype(BF16)

            lane4 = lax.broadcasted_iota(jnp.int32, (1, 128), 1)
            dsink_vec = jnp.zeros((1, 128), F32)
            yb, dk_parts, dv_parts = [], [], []
            for j in range(TILE_CHUNKS):
                rows, k_pairs, v_pairs, table = _load_chunk(j, i, sk_ref, sv_ref, skp_ref, svp_ref)
                qp = _halves_bf16(sq_ref[rows, :])
                p, ps = _attention_probs(qp, k_pairs, bias_ref[table], sink_col)
                out, pp = _attention_out(p, v_pairs, CHUNK)
                yb.append(out)
                do_pairs = _halves_bf16(dyc[rows, A_WIDTH:A_WIDTH + SWA_WIDTH])
                dl, delta, dq, dk, dv = _attention_backward(p, pp, do_pairs, qp, k_pairs, v_pairs, CHUNK)
                sink_terms = ps * delta
                for h in range(4):
                    dsink_vec = dsink_vec + jnp.where(lane4 == h, -jnp.sum(sink_terms[h * CHUNK:(h + 1) * CHUNK]), 0.0)
                drel_ref[...] += dl
                carry_dp[rows, col["sq"]] = (dq * QK_SCALE).astype(BF16)
                dk_parts.append(_swa_unvariants(*_split_pair_grads(dk)) * QK_SCALE)
                dv_parts.append(_swa_unvariants(*_split_pair_grads(dv)))

            mqp = _halves_bf16(mq_ref[...])
            pm, _ = _attention_probs(mqp, mk_pairs, None, None)
            yc, ppm = _attention_out(pm, mv_pairs, TILE)
            dc_pairs = _halves_bf16(dyc[:, A_WIDTH + SWA_WIDTH:])
            _, _, dmq, dmk, dmv = _attention_backward(pm, ppm, dc_pairs, mqp, mk_pairs, mv_pairs, TILE)
            carry_dp[:, col["mq"]] = (dmq * QK_SCALE).astype(BF16)
            dmkv_ref[0] += jnp.concatenate([_mem_unvariants(*_split_pair_grads(dmk)) * QK_SCALE,
                                            _mem_unvariants(*_split_pair_grads(dmv))], axis=-1)

            ycat = jnp.concatenate(ya + [jnp.concatenate(yb, axis=0), yc], axis=-1)
            dwout_ref[...] += _mm_tn((ycat * sz).astype(BF16), do_b)
            carry_dp[:, col["z"]] = (dy * ycat * (sig * (1.0 + zv * (1.0 - sig)))).astype(BF16)
            dsink_ref[...] += dsink_vec

            for parts_c, carry, cols in ((dk_parts, carry_k, col["sk"]), (dv_parts, carry_v, col["sv"])):
                @pl.when(i > 0)
                def _():
                    dproj_ref[:, cols] = (carry[...] + jnp.concatenate(
                        [jnp.zeros((TILE - CHUNK, KV_WIDTH), F32), parts_c[0][:CHUNK]], axis=0)).astype(BF16)
                new = [parts_c[0][CHUNK:]]
                for j in range(1, TILE_CHUNKS):
                    new[-1] = new[-1] + parts_c[j][:CHUNK]
                    new.append(parts_c[j][CHUNK:])
                carry[...] = jnp.concatenate(new, axis=0)

        @pl.when(i == n_tiles_ex)
        def _():
            dproj_ref[:, col["sk"]] = carry_k[...].astype(BF16)
            dproj_ref[:, col["sv"]] = carry_v[...].astype(BF16)

    tile = functools.partial(_tile_specs, n_tiles_ex)
    prev = functools.partial(_prev_chunk_spec, n_tiles_ex)
    late = pl.BlockSpec((TILE, IN_WIDTH), lambda b, i: (b * n_tiles_ex + jnp.maximum(i - 1, 0), 0))
    return pl.pallas_call(
        body, name="backward_mix", grid=(n_ex, n_tiles_ex + 1),
        out_shape=[jax.ShapeDtypeStruct((n_tok, IN_WIDTH), BF16),
                   jax.ShapeDtypeStruct((n_ex, MEM_LEN, 2 * MEM_WIDTH), F32),
                   jax.ShapeDtypeStruct((MIX_WIDTH, D_MODEL), F32), jax.ShapeDtypeStruct((1, A_WIDTH), F32),
                   jax.ShapeDtypeStruct((1, A_WIDTH), F32), jax.ShapeDtypeStruct((A_GROUPS, CHUNK, CHUNK), F32),
                   jax.ShapeDtypeStruct((CHUNK, A_WIDTH), F32), jax.ShapeDtypeStruct((1, 128), F32),
                   jax.ShapeDtypeStruct((4 * CHUNK, 2 * CHUNK), F32)],
        in_specs=[tile(D_MODEL), tile(A_WIDTH), tile(A_WIDTH), tile(SWA_WIDTH), tile(KV_WIDTH), tile(KV_WIDTH),
                  prev(KV_WIDTH), prev(KV_WIDTH), tile(MEM_WIDTH), tile(MIX_WIDTH),
                  pl.BlockSpec((1, MEM_LEN, 2 * MEM_WIDTH), lambda b, i: (b, 0, 0)),
                  _full_spec((1, A_WIDTH)), _full_spec((1, A_WIDTH)), _full_spec((A_GROUPS, CHUNK, CHUNK)),
                  _full_spec((A_GROUPS, CHUNK, CHUNK)), SMEM_SPEC, _full_spec((2, 4 * CHUNK, 2 * CHUNK)),
                  _full_spec((MIX_WIDTH, D_MODEL))],
        out_specs=[late, pl.BlockSpec((1, MEM_LEN, 2 * MEM_WIDTH), lambda b, i: (b, 0, 0)),
                   _full_spec((MIX_WIDTH, D_MODEL)), _full_spec((1, A_WIDTH)), _full_spec((1, A_WIDTH)),
                   _full_spec((A_GROUPS, CHUNK, CHUNK)), _full_spec((CHUNK, A_WIDTH)), _full_spec((1, 128)),
                   _full_spec((4 * CHUNK, 2 * CHUNK))],
        scratch_shapes=[pltpu.VMEM((TILE, IN_WIDTH), BF16), pltpu.VMEM((TILE, KV_WIDTH), F32),
                        pltpu.VMEM((TILE, KV_WIDTH), F32)],
        compiler_params=pltpu.CompilerParams(vmem_limit_bytes=VMEM_LIMIT),
    )(do, au, av, sq, sk, sv, sk, sv, mq, z, mkv, v_g, v_b, w_sp, b_sp, sinks, bias, w_out)


BWD_PROJ_TILE = 512


def _backward_projection(x2, dout, dproj, g_pre, w_in_t):
    n_tok = x2.shape[0]
    n_steps = n_tok // BWD_PROJ_TILE

    def body(x_ref, dout_ref, dp_ref, g_ref, w_hbm, dx_ref, dgpre_ref, w_vmem, sem):
        @pl.when(pl.program_id(0) == 0)
        def _():
            load = pltpu.make_async_copy(w_hbm, w_vmem, sem)
            load.start()
            dgpre_ref[...] = jnp.zeros_like(dgpre_ref)
            load.wait()

        xv = x_ref[...]
        r = lax.rsqrt(jnp.mean(xv * xv, axis=-1, keepdims=True) + EPS)
        xn = xv * r
        dh = _mm(dp_ref[...], w_vmem[...])
        dgpre_ref[...] += jnp.sum(dh * xn, axis=0, keepdims=True)
        dhg = dh * g_ref[...]
        dx_ref[...] = r * (dhg - xn * jnp.mean(dhg * xn, axis=-1, keepdims=True)) + dout_ref[...]

    row = lambda w: pl.BlockSpec((BWD_PROJ_TILE, w), lambda i: (i, 0))
    return pl.pallas_call(
        body, name="backward_projection", grid=(n_steps,),
        out_shape=[jax.ShapeDtypeStruct((n_tok, D_MODEL), F32), jax.ShapeDtypeStruct((1, D_MODEL), F32)],
        in_specs=[row(D_MODEL), row(D_MODEL), row(IN_WIDTH), _full_spec((1, D_MODEL)), ANY_SPEC],
        out_specs=[row(D_MODEL), _full_spec((1, D_MODEL))],
        scratch_shapes=[pltpu.VMEM((IN_WIDTH, D_MODEL), BF16), pltpu.SemaphoreType.DMA],
        input_output_aliases={1: 0},
        compiler_params=pltpu.CompilerParams(vmem_limit_bytes=VMEM_LIMIT),
    )(x2, dout, dproj, g_pre, w_in_t)


SHARD_ROWS = IN_WIDTH // N_CHIPS
SHARD_WINDOW = 768
SHARD_HALF = SHARD_ROWS // 2
DWIN_TILE = 1024
N_REL = N_CHIPS - 1


def _shard_window_start(shard):
    return (shard * SHARD_ROWS // 128) * 128


def _reduce_gradients(dproj, h, big, small, shard_arr):
    n_tok = h.shape[0]
    tile = min(DWIN_TILE, n_tok)
    n_sub = n_tok // tile
    last = N_CHIPS - 1
    n_big, n_small = len(big), len(small)
    big_half = [g.shape[2:] for g in big]
    sem_big_d2d = 2 * N_CHIPS
    sem_big_ici = sem_big_d2d + n_big
    sem_big_swap = sem_big_ici + N_REL * n_big
    sem_small_d2d = sem_big_swap + n_big
    sem_small_ici = sem_small_d2d + n_small
    n_sems = sem_small_ici + N_REL * n_small
    loc_small = n_big
    loc_out_win = loc_small + n_small
    loc_out_big = loc_out_win + 2
    loc_out_small = loc_out_big + 2 * n_big
    n_local = loc_out_small + n_small

    def shard_of_slot(s, my_shard):
        return my_shard ^ ((s + 1) % N_CHIPS)

    def body(shard_ref, dp_ref, h_ref, *refs):
        big_hbm, refs = refs[:n_big], refs[n_big:]
        small_hbm, refs = refs[:n_small], refs[n_small:]
        out_hbm, refs = refs[0], refs[1:]
        big_out, refs = refs[:n_big], refs[n_big:]
        small_out, refs = refs[:n_small], refs[n_small:]
        part, recv_d2d, send_ici, recv_ici, mine_buf, other_buf = refs[:6]
        refs = refs[6:]
        big_own, big_recv, big_send, big_land, big_mine, big_other = (
            refs[k * n_big:(k + 1) * n_big] for k in range(6))
        refs = refs[6 * n_big:]
        small_own, small_recv, small_all = (refs[k * n_small:(k + 1) * n_small] for k in range(3))
        send_sems, recv_sems, local_sems = refs[3 * n_small:]

        s, t = pl.program_id(0), pl.program_id(1)
        x, y, c = lax.axis_index("x"), lax.axis_index("y"), lax.axis_index("c")
        my_chip = 2 * x + y
        sibling = (x, y, 1 - c)
        my_rows = pl.ds(pl.multiple_of(c * SHARD_HALF, 8), SHARD_HALF)
        other_rows = pl.ds(pl.multiple_of((1 - c) * SHARD_HALF, 8), SHARD_HALF)

        def remote(src, dst, k, to):
            return pltpu.make_async_remote_copy(src_ref=src, dst_ref=dst, send_sem=send_sems.at[k],
                                                recv_sem=recv_sems.at[k], device_id=to, device_id_type=MESH)

        def chip_at(rel):
            return (x ^ (rel >> 1), y ^ (rel & 1), c)

        def to_sibling(k):
            return remote(part.at[k % 2, other_rows, :], recv_d2d.at[k], k, sibling)

        def to_chip(k):
            return remote(send_ici.at[k], recv_ici.at[k], N_CHIPS + k, chip_at(k + 1))

        swap = remote(mine_buf, other_buf, 2 * N_CHIPS - 1, sibling)
        big_load = [pltpu.make_async_copy(big_hbm[w].at[:, pl.ds(c, 1)], big_own[w], local_sems.at[w])
                    for w in range(n_big)]
        big_to_sibling = [remote(big_hbm[w].at[:, pl.ds(1 - c, 1)], big_recv[w], sem_big_d2d + w, sibling)
                          for w in range(n_big)]
        big_to_chip = [[remote(big_send[w].at[k], big_land[w].at[k], sem_big_ici + N_REL * w + k, chip_at(k + 1))
                        for k in range(N_REL)] for w in range(n_big)]
        big_swap = [remote(big_mine[w], big_other[w], sem_big_swap + w, sibling) for w in range(n_big)]
        small_load = [pltpu.make_async_copy(small_hbm[i], small_own[i], local_sems.at[loc_small + i])
                      for i in range(n_small)]
        small_to_sibling = [remote(small_hbm[i], small_recv[i], sem_small_d2d + i, sibling) for i in range(n_small)]
        small_to_chip = [[remote(small_all[i].at[my_chip], small_all[i].at[my_chip],
                                 sem_small_ici + N_REL * i + k, chip_at(k + 1))
                          for k in range(N_REL)] for i in range(n_small)]

        @pl.when((s == 0) & (t == 0))
        def _():
            for cp in big_load + big_to_sibling + small_load + small_to_sibling:
                cp.start()

        @pl.when((s == 0) & (t == n_sub - 1))
        def _():
            for cp in big_load + small_load:
                cp.wait()
            for cp in big_to_sibling + small_to_sibling:
                cp.wait_recv()
                cp.wait_send()
            for w in range(n_big):
                for k in range(N_REL):
                    shard = my_chip ^ (k + 1)
                    big_send[w][k] = (big_own[w][shard, 0] + big_recv[w][shard, 0]).astype(BF16)
                    big_to_chip[w][k].start()
            for i in range(n_small):
                small_all[i][my_chip] = small_own[i][...] + small_recv[i][...]
                for k in range(N_REL):
                    small_to_chip[i][k].start()

        @pl.when((s > 0) & (t == 0))
        def _():
            k = s - 1
            cp = to_sibling(k)
            cp.wait_recv()
            cp.wait_send()
            send_ici[k] = (part[k % 2, my_rows, :] + recv_d2d[k]).astype(BF16)
            to_chip(k).start()

        r = _mm_tn(dp_ref[...], h_ref[...])
        odd = shard_of_slot(s, shard_ref[0]) % 2
        for parity in range(2):
            rows = r[64 * parity:64 * parity + SHARD_ROWS]

            @pl.when((odd == parity) & (t == 0))
            def _():
                part[s % 2] = rows

            @pl.when((odd == parity) & (t > 0))
            def _():
                part[s % 2] += rows

        @pl.when(t == n_sub - 1)
        def _():
            to_sibling(s).start()

        @pl.when((s == last) & (t == n_sub - 1))
        def _():
            cp = to_sibling(last)
            cp.wait_recv()
            cp.wait_send()
            total = part[last % 2, my_rows, :] + recv_d2d[last]
            for k in range(last):
                to_chip(k).wait_recv()
                total = total + recv_ici[k].astype(F32)
            mine_buf[...] = total
            swap.start()
            out_mine = pltpu.make_async_copy(mine_buf, out_hbm.at[my_rows, :], local_sems.at[0])
            out_mine.start()
            swap.wait_recv()
            out_other = pltpu.make_async_copy(other_buf, out_hbm.at[other_rows, :], local_sems.at[1])
            out_other.start()
            stores = [out_mine, out_other]
            for w in range(n_big):
                rows = big_half[w][0]
                total = big_own[w][my_chip, 0] + big_recv[w][my_chip, 0]
                for k in range(N_REL):
                    big_to_chip[w][k].wait_recv()
                    total = total + big_land[w][k].astype(F32)
                big_mine[w][...] = total
                big_swap[w].start()
                stores.append(pltpu.make_async_copy(
                    big_mine[w], big_out[w].at[pl.ds(pl.multiple_of(c * rows, 8), rows), :],
                    local_sems.at[loc_out_big + 2 * w]))
                stores[-1].start()
            for w in range(n_big):
                rows = big_half[w][0]
                big_swap[w].wait_recv()
                stores.append(pltpu.make_async_copy(
                    big_other[w], big_out[w].at[pl.ds(pl.multiple_of((1 - c) * rows, 8), rows), :],
                    local_sems.at[loc_out_big + 2 * w + 1]))
                stores[-1].start()
            for i in range(n_small):
                for k in range(N_REL):
                    small_to_chip[i][k].wait_recv()
                stores.append(pltpu.make_async_copy(small_all[i], small_out[i], local_sems.at[loc_out_small + i]))
                stores[-1].start()
            for k in range(last):
                to_chip(k).wait_send()
            swap.wait_send()
            for w in range(n_big):
                for k in range(N_REL):
                    big_to_chip[w][k].wait_send()
                big_swap[w].wait_send()
            for i in range(n_small):
                for k in range(N_REL):
                    small_to_chip[i][k].wait_send()
            for cp in stores:
                cp.wait()

    half = (SHARD_HALF, D_MODEL)
    vmem = pltpu.VMEM
    scratch = [vmem((2, SHARD_ROWS, D_MODEL), F32), vmem((N_CHIPS,) + half, F32),
               vmem((N_REL,) + half, BF16), vmem((N_REL,) + half, BF16), vmem(half, F32), vmem(half, F32)]
    scratch += [vmem((N_CHIPS, 1) + hs, F32) for hs in big_half] * 2
    scratch += [vmem((N_REL,) + hs, BF16) for hs in big_half] * 2
    scratch += [vmem(hs, F32) for hs in big_half] * 2
    scratch += [vmem(a.shape, F32) for a in small] * 2 + [vmem((N_CHIPS,) + a.shape, F32) for a in small]
    scratch += [pltpu.SemaphoreType.DMA((n_sems,)), pltpu.SemaphoreType.DMA((n_sems,)),
                pltpu.SemaphoreType.DMA((n_local,))]
    n_hbm = n_big + n_small
    out = pl.pallas_call(
        body, name="reduce_gradients",
        out_shape=[jax.ShapeDtypeStruct((SHARD_ROWS, D_MODEL), F32)]
        + [jax.ShapeDtypeStruct((2 * hs[0], hs[1]), F32) for hs in big_half]
        + [jax.ShapeDtypeStruct((N_CHIPS,) + a.shape, F32) for a in small],
        grid_spec=pltpu.PrefetchScalarGridSpec(
            num_scalar_prefetch=1, grid=(N_CHIPS, n_sub),
            in_specs=[pl.BlockSpec((pl.Element(tile), pl.Element(SHARD_WINDOW)),
                                   lambda s, t, m: (t * tile, _shard_window_start(shard_of_slot(s, m[0])))),
                      pl.BlockSpec((tile, D_MODEL), lambda s, t, m: (t, 0))] + [ANY_SPEC] * n_hbm,
            out_specs=[ANY_SPEC] * (1 + n_hbm),
            scratch_shapes=scratch),
        compiler_params=pltpu.CompilerParams(vmem_limit_bytes=VMEM_LIMIT),
    )(shard_arr, dproj, h, *big, *small)
    return out[:1 + n_big], out[1 + n_big:]


def _memkv_backward(mem, dmkv, g_mem, w_mkv):
    n_ex = mem.shape[0]

    def body(mem_ref, d_ref, g_ref, w_ref, dw_ref, dg_ref):
        @pl.when(pl.program_id(0) == 0)
        def _():
            dw_ref[...] = jnp.zeros_like(dw_ref)
            dg_ref[...] = jnp.zeros_like(dg_ref)

        m = mem_ref[0]
        mn = m * lax.rsqrt(jnp.mean(m * m, axis=-1, keepdims=True) + EPS)
        d_b = d_ref[0].astype(BF16)
        dw_ref[...] += _mm_tn((mn * g_ref[...]).astype(BF16), d_b)
        dg_ref[...] += jnp.sum(_mm_nt(d_b, w_ref[...]) * mn, axis=0, keepdims=True)

    return pl.pallas_call(
        body, name="memkv_backward", grid=(n_ex,),
        out_shape=[jax.ShapeDtypeStruct((D_MODEL, 2 * MEM_WIDTH), F32), jax.ShapeDtypeStruct((1, D_MODEL), F32)],
        in_specs=[pl.BlockSpec((1, MEM_LEN, D_MODEL), lambda b: (b, 0, 0)),
                  pl.BlockSpec((1, MEM_LEN, 2 * MEM_WIDTH), lambda b: (b, 0, 0)),
                  _full_spec((1, D_MODEL)), _full_spec((D_MODEL, 2 * MEM_WIDTH))],
        out_specs=[_full_spec((D_MODEL, 2 * MEM_WIDTH)), _full_spec((1, D_MODEL))],
    )(mem, dmkv, g_mem, w_mkv)


def _pack_small_grads(dgpre, dgpost, dgmem, dvg, dvb, dws, dbs, dsink, drel, loss_vec, buckets):
    def body(dgpre_ref, dgpost_ref, dgmem_ref, dvg_ref, dvb_ref, dws_ref, dbs_ref, dsink_ref, drel_ref, loss_ref,
             bk_ref, a_ref, b_ref):
        a_ref[...] = jnp.zeros_like(a_ref)
        b_ref[...] = jnp.zeros_like(b_ref)
        a_ref[0:1, :] = dgpre_ref[...]
        a_ref[1:2, :] = dgpost_ref[...]
        a_ref[2:3, :] = dgmem_ref[...]
        a_ref[3:4, :] = jnp.concatenate([dvg_ref[...], dvb_ref[...]], axis=-1)
        a_ref[ROW_LOSS:ROW_LOSS + 1, 0:128] = loss_ref[...]
        row = lax.broadcasted_iota(jnp.int32, (CHUNK, CHUNK), 0)
        col = lax.broadcasted_iota(jnp.int32, (CHUNK, CHUNK), 1)
        for g in range(A_GROUPS):
            b_ref[ROW_WS + g * CHUNK:ROW_WS + (g + 1) * CHUNK, :] = jnp.where(row >= col, dws_ref[g], 0.0)
            by_token = jnp.transpose(dbs_ref[:, g * 128:(g + 1) * 128])
            b_ref[ROW_BS + g:ROW_BS + g + 1, :] = jnp.sum(by_token, axis=0, keepdims=True)
        b_ref[ROW_SINK:ROW_SINK + 1, :] = dsink_ref[...]
        bk = bk_ref[...]
        rel_row = lax.broadcasted_iota(jnp.int32, (8, 128), 0)
        rel_col = lax.broadcasted_iota(jnp.int32, (8, 128), 1)
        rel = jnp.zeros((8, 128), F32)
        for h in range(4):
            acc = drel_ref[h * CHUNK:(h + 1) * CHUNK, :]
            for b in range(N_BUCKETS):
                rel = jnp.where((rel_row == h) & (rel_col == b), jnp.sum(jnp.where(bk == b, acc, 0.0)), rel)
        b_ref[ROW_REL:ROW_REL + 8, :] = rel

    return pl.pallas_call(
        body, name="pack_small_grads",
        out_shape=[jax.ShapeDtypeStruct((SMALL_A_ROWS, D_MODEL), F32), jax.ShapeDtypeStruct((SMALL_B_ROWS, 128), F32)],
        in_specs=[VMEM_SPEC] * 11, out_specs=[VMEM_SPEC] * 2,
    )(dgpre, dgpost, dgmem, dvg, dvb, dws, dbs, dsink, drel, loss_vec, buckets)


def _adamw(w, g, m, v):
    m2 = ADAM_B1 * m + (1.0 - ADAM_B1) * g
    v2 = ADAM_B2 * v + (1.0 - ADAM_B2) * (g * g)
    m_hat = m2 / (1.0 - ADAM_B1 ** ADAM_STEP)
    v_hat = v2 / (1.0 - ADAM_B2 ** ADAM_STEP)
    delta = -ADAM_LR * (m_hat / (jnp.sqrt(v_hat) + ADAM_EPS) + ADAM_WD * w)
    return delta, m2, v2


ADAM_MAX_ROWS = 176


def _adamw_whole(g, w, m, v, name):
    rows, cols = w.shape
    steps = -(-rows // ADAM_MAX_ROWS)
    block_rows = rows // steps
    assert block_rows * steps == rows and block_rows % 8 == 0

    def body(g_ref, w_ref, m_ref, v_ref, d_out, m_out, v_out):
        delta, m2, v2 = _adamw(w_ref[...], g_ref[...], m_ref[...], v_ref[...])
        d_out[...] = delta
        m_out[...] = m2
        v_out[...] = v2

    block = pl.BlockSpec((block_rows, cols), lambda k: (k, 0))
    out = pl.pallas_call(
        body, name=name, grid=(steps,), out_shape=[jax.ShapeDtypeStruct((rows, cols), F32)] * 3,
        in_specs=[block] * 4, out_specs=[block] * 3,
    )(g, w, m, v)
    return [g] + list(out)


def _adamw_small(ra, rb, weights, moments_m, moments_v):
    n = len(weights)

    def body(*refs):
        ra_ref, rb_ref = refs[0], refs[1]
        w_refs, m_refs, v_refs = refs[2:2 + n], refs[2 + n:2 + 2 * n], refs[2 + 2 * n:2 + 3 * n]
        outs = refs[2 + 3 * n:]
        g_outs, d_outs, m_outs, v_outs = outs[:n], outs[n:2 * n], outs[2 * n:3 * n], outs[3 * n:4 * n]
        ga, gb = ra_ref[0], rb_ref[0]
        for chip in range(1, N_CHIPS):
            ga = ga + ra_ref[chip]
            gb = gb + rb_ref[chip]
        outs[4 * n][...] = ga[ROW_LOSS:ROW_LOSS + 1, 0:128]
        grads = [ga[0:1, :], ga[1:2, :], ga[2:3, :], ga[3:4, :A_WIDTH], ga[3:4, A_WIDTH:],
                 gb[ROW_WS:ROW_WS + A_GROUPS * CHUNK, :].reshape(A_GROUPS, CHUNK, CHUNK),
                 gb[ROW_BS:ROW_BS + A_GROUPS, :], gb[ROW_SINK:ROW_SINK + 1, 0:4],
                 gb[ROW_REL:ROW_REL + 4, 0:N_BUCKETS]]
        for k in range(n):
            delta, m2, v2 = _adamw(w_refs[k][...], grads[k], m_refs[k][...], v_refs[k][...])
            g_outs[k][...] = grads[k]
            d_outs[k][...] = delta
            m_outs[k][...] = m2
            v_outs[k][...] = v2

    out_shape = [jax.ShapeDtypeStruct(w.shape, F32) for w in weights] * 4 + [jax.ShapeDtypeStruct((1, 128), F32)]
    return pl.pallas_call(
        body, name="adamw_small", out_shape=out_shape,
        in_specs=[VMEM_SPEC] * (2 + 3 * n), out_specs=[VMEM_SPEC] * (4 * n + 1),
    )(ra, rb, *weights, *moments_m, *moments_v)


def kernel(x, mem, pre_norm_g, post_norm_g, mem_norm_g, w_in, w_mem_kv, v_norm_g, v_norm_b, w_spatial, b_spatial, attn_sinks, rel_bias, w_out, loss_target, m_pre_norm_g, m_post_norm_g, m_mem_norm_g, m_w_in, m_w_mem_kv, m_v_norm_g, m_v_norm_b, m_w_spatial, m_b_spatial, m_attn_sinks, m_rel_bias, m_w_out, v_pre_norm_g, v_post_norm_g, v_mem_norm_g, v_w_in, v_w_mem_kv, v_v_norm_g, v_v_norm_b, v_w_spatial, v_b_spatial, v_attn_sinks, v_rel_bias, v_w_out):
    n_ex, seq, _ = x.shape
    n_tok = n_ex * seq
    x2 = x.reshape(n_tok, D_MODEL)
    tgt2 = loss_target.reshape(n_tok, D_MODEL)
    buckets = jnp.asarray(_bucket_map())
    shard_arr = (2 * lax.axis_index("x") + lax.axis_index("y")).astype(jnp.int32).reshape(1)
    w_sp = w_spatial[0]
    b_sp = jnp.broadcast_to(b_spatial[0][:, :, None], (A_GROUPS, CHUNK, CHUNK))
    w_in_t, m_w_in_t, v_w_in_t = (jnp.transpose(a[0]) for a in (w_in, m_w_in, v_w_in))
    rel_t, m_rel_t, v_rel_t = (jnp.transpose(a) for a in (rel_bias, m_rel_bias, v_rel_bias))

    x_arr = lax.axis_index("x").astype(jnp.int32).reshape(1)
    h_b, parts, (w_in_b, g_mkv, g_out) = _gather_and_project(x2, pre_norm_g, w_in_t, w_mem_kv[0], w_out[0], x_arr)
    w_mkv_b = g_mkv.reshape(D_MODEL, 2 * MEM_WIDTH)
    w_out_b = g_out.reshape(MIX_WIDTH, D_MODEL)

    bias = _make_bias(rel_t, buckets)
    mkv = _memkv_forward(mem, mem_norm_g, w_mkv_b)
    dout, do, loss_vec, dgpost = _forward_mix(parts, mkv, x2, tgt2, v_norm_g, v_norm_b, w_sp, b_sp, attn_sinks, bias,
                                             w_out_b, post_norm_g, n_ex, seq)

    dproj, dmkv, dwout, dvg, dvb, dws, dbs, dsink, drel = _backward_mix(
        parts, mkv, do, v_norm_g, v_norm_b, w_sp, b_sp, attn_sinks, bias, w_out_b, n_ex, seq)
    dx, dgpre = _backward_projection(x2, dout, dproj, pre_norm_g, w_in_b)
    dwmkv, dgmem = _memkv_backward(mem, dmkv, mem_norm_g, w_mkv_b)
    small_a, small_b = _pack_small_grads(dgpre, dgpost, dgmem, dvg, dvb, dws, dbs, dsink, drel, loss_vec, buckets)

    shard_shapes = [w_mem_kv.shape[1:], w_out.shape[1:]]
    big = [g.reshape(N_CHIPS, 2, s[0] // 2, s[1]) for g, s in zip((dwmkv, dwout), shard_shapes)]
    (g_win, g_wmkv, g_wout), (ga, gb) = _reduce_gradients(dproj, h_b, big, [small_a, small_b], shard_arr)

    big_out = [_adamw_whole(g_win, w_in_t, m_w_in_t, v_w_in_t, "adamw_w_in"),
               _adamw_whole(g_wmkv, w_mem_kv[0], m_w_mem_kv[0], v_w_mem_kv[0], "adamw_w_mem_kv"),
               _adamw_whole(g_wout, w_out[0], m_w_out[0], v_w_out[0], "adamw_w_out")]
    small_w = [pre_norm_g, post_norm_g, mem_norm_g, v_norm_g, v_norm_b, w_sp, b_spatial[0], attn_sinks, rel_t]
    small_m = [m_pre_norm_g, m_post_norm_g, m_mem_norm_g, m_v_norm_g, m_v_norm_b, m_w_spatial[0], m_b_spatial[0],
               m_attn_sinks, m_rel_t]
    small_v = [v_pre_norm_g, v_post_norm_g, v_mem_norm_g, v_v_norm_g, v_v_norm_b, v_w_spatial[0], v_b_spatial[0],
               v_attn_sinks, v_rel_t]
    small_out = _adamw_small(ga, gb, small_w, small_m, small_v)
    n_small = len(small_w)

    outputs = [small_out[4 * n_small][0, 0], dx.reshape(x.shape)]
    for kind in range(4):
        s = small_out[kind * n_small:(kind + 1) * n_small]
        outputs += [s[0], s[1], s[2], jnp.transpose(big_out[0][kind])[None], big_out[1][kind][None], s[3], s[4],
                    s[5][None], s[6][None], s[7], jnp.transpose(s[8]), big_out[2][kind][None]]
    return tuple(outputs)
```

```python
import functools

import numpy as np
import jax
import jax.numpy as jnp
from jax import lax
from jax.experimental import pallas as pl
from jax.experimental.pallas import tpu as pltpu

F32 = jnp.float32
BF16 = jnp.bfloat16
MESH = pl.DeviceIdType.MESH

D_MODEL = 1024
CHUNK = 128
A_WIDTH = 512
A_GROUPS = 4
SWA_WIDTH = 256
KV_WIDTH = 128
MEM_WIDTH = 256
MEM_LEN = 256
MIX_WIDTH = 1024
IN_WIDTH = 2816
N_BUCKETS = 32
MAX_DISTANCE = 128
EPS = 1e-6
NEG = -1e30
QK_SCALE = 0.125
HALF_HEAD_PAIR = 64

ADAM_LR = 0.001
ADAM_B1 = 0.9
ADAM_B2 = 0.999
ADAM_EPS = 1e-08
ADAM_WD = 0.01
ADAM_STEP = 10

N_CHIPS = 4
TILE_CHUNKS = 4
TILE = TILE_CHUNKS * CHUNK
PROJ_TILE = 512
VMEM_LIMIT = 56 * 1024 * 1024

SMALL_A_ROWS = 8
ROW_LOSS = 4
ROW_WS = 0
ROW_BS = 512
ROW_SINK = 520
ROW_REL = 528
SMALL_B_ROWS = 536


def _mm(a, b):
    return lax.dot_general(a, b, (((1,), (0,)), ((), ())), preferred_element_type=F32)


def _mm_nt(a, b):
    return lax.dot_general(a, b, (((1,), (1,)), ((), ())), preferred_element_type=F32)


def _mm_tn(a, b):
    return lax.dot_general(a, b, (((0,), (0,)), ((), ())), preferred_element_type=F32)


def _bucket_map():
    qi = np.arange(CHUNK)[:, None]
    kj = np.arange(2 * CHUNK)[None, :]
    n = np.maximum(qi + CHUNK - kj, 0)
    max_exact = N_BUCKETS // 2
    large = max_exact + (np.log(np.maximum(n, 1) / max_exact) / np.log(MAX_DISTANCE / max_exact)
                         * (N_BUCKETS - max_exact)).astype(np.int32)
    large = np.minimum(large, N_BUCKETS - 1)
    return np.where(n < max_exact, n, large).astype(np.int32)


_GELU_C = 0.7978845608028654
_GELU_A = 0.044715


def _gelu(x):
    t = jnp.tanh(_GELU_C * (x + _GELU_A * x * x * x))
    return 0.5 * x * (1.0 + t), t


def _gelu_grad(x, t):
    return 0.5 * (1.0 + t) + 0.5 * x * (1.0 - t * t) * (_GELU_C * (1.0 + 3.0 * _GELU_A * x * x))


def _sigmoid(x):
    return 1.0 / (1.0 + jnp.exp(-x))


def _lane_lo(shape):
    return lax.broadcasted_iota(jnp.int32, shape, 1) < HALF_HEAD_PAIR


def _swa_variants(t):
    lo = _lane_lo(t.shape)
    tr = pltpu.roll(t, HALF_HEAD_PAIR, 1)
    zero = jnp.zeros_like(t)
    return (jnp.where(lo, t, zero).astype(BF16), jnp.where(lo, zero, tr).astype(BF16),
            jnp.where(lo, tr, zero).astype(BF16), jnp.where(lo, zero, t).astype(BF16))


def _swa_unvariants(d0, d1, d2, d3):
    lo = _lane_lo(d0.shape)
    zero = jnp.zeros_like(d0)
    rolled = jnp.where(lo, zero, d1) + jnp.where(lo, d2, zero)
    return jnp.where(lo, d0, zero) + jnp.where(lo, zero, d3) + pltpu.roll(rolled, HALF_HEAD_PAIR, 1)


def _mem_variants(t):
    out = []
    for pair in range(2):
        tp = t[:, pair * 128:(pair + 1) * 128]
        lo = _lane_lo(tp.shape)
        zero = jnp.zeros_like(tp)
        out.append(jnp.where(lo, tp, zero).astype(BF16))
        out.append(jnp.where(lo, zero, tp).astype(BF16))
    return out


def _mem_unvariants(d0, d1, d2, d3):
    lo = _lane_lo(d0.shape)
    return jnp.concatenate([jnp.where(lo, d0, d1), jnp.where(lo, d2, d3)], axis=-1)


def _softmax(logits, sinks):
    m = jnp.max(logits, axis=-1, keepdims=True)
    if sinks is not None:
        m = jnp.maximum(m, sinks)
    p = jnp.exp(logits - m)
    den = jnp.sum(p, axis=-1, keepdims=True)
    if sinks is None:
        return p * (1.0 / den), None
    es = jnp.exp(sinks - m)
    inv = 1.0 / (den + es)
    return p * inv, es * inv


def _band_valid(with_prev):
    qi = lax.broadcasted_iota(jnp.int32, (CHUNK, 2 * CHUNK), 0)
    kj = lax.broadcasted_iota(jnp.int32, (CHUNK, 2 * CHUNK), 1)
    in_cur = (kj >= CHUNK) & (kj - CHUNK <= qi)
    if not with_prev:
        return in_cur
    return in_cur | ((kj < CHUNK) & (kj > qi))


def _causal_weights(ws_ref):
    row = lax.broadcasted_iota(jnp.int32, (CHUNK, CHUNK), 0)
    col = lax.broadcasted_iota(jnp.int32, (CHUNK, CHUNK), 1)
    return [jnp.where(row >= col, ws_ref[g], 0.0).astype(BF16) for g in range(A_GROUPS)]


def _rows_to_lanes(a, n):
    return jnp.concatenate([a[c * CHUNK:(c + 1) * CHUNK] for c in range(n)], axis=1)


def _lanes_to_rows(a, n):
    w = a.shape[1] // n
    return jnp.concatenate([a[:, c * w:(c + 1) * w] for c in range(n)], axis=0)


def _stack_heads(pair01, pair23):
    return jnp.concatenate([pair01[:, :256], pair01[:, 256:], pair23[:, :256], pair23[:, 256:]], axis=0)


def _pair_heads(s, r):
    return (jnp.concatenate([s[0:r], s[r:2 * r]], axis=1), jnp.concatenate([s[2 * r:3 * r], s[3 * r:4 * r]], axis=1))


def _pair_operands(variants):
    return (jnp.concatenate(variants[0:2], axis=0), jnp.concatenate(variants[2:4], axis=0))


def _split_pair_grads(d_pairs):
    return d_pairs[0][:256], d_pairs[0][256:], d_pairs[1][:256], d_pairs[1][256:]


def _halves_bf16(a):
    return (a[:, :128].astype(BF16), a[:, 128:].astype(BF16))


def _group_a_forward(au, av, vg, vb, wm, bs_rows):
    gu, tu = _gelu(au)
    gv, tv = _gelu(av)
    ya, res = [], []
    for g in range(A_GROUPS):
        sl = slice(g * 128, (g + 1) * 128)
        xg = gv[:, sl]
        xc = xg - jnp.mean(xg, axis=-1, keepdims=True)
        rstd = lax.rsqrt(jnp.mean(xc * xc, axis=-1, keepdims=True) + EPS)
        xhat = xc * rstd
        vn = _rows_to_lanes((xhat * vg[:, sl] + vb[:, sl]).astype(BF16), TILE_CHUNKS)
        s = _lanes_to_rows(_mm(wm[g], vn), TILE_CHUNKS) + bs_rows[g]
        ya.append(gu[:, sl] * s)
        res.append((xhat, rstd, vn, s))
    return ya, dict(gu=gu, tu=tu, tv=tv, groups=res)


def _attention_probs(qp, k_pairs, bias, sink_col):
    logits = _stack_heads(_mm_nt(qp[0], k_pairs[0]), _mm_nt(qp[1], k_pairs[1])) * QK_SCALE
    if bias is not None:
        logits = logits + bias
    return _softmax(logits, sink_col)


def _attention_out(p, v_pairs, r):
    pp = _pair_heads(p.astype(BF16), r)
    return jnp.concatenate([_mm(pp[0], v_pairs[0]), _mm(pp[1], v_pairs[1])], axis=-1), pp


def _attention_backward(p, pp, do_pairs, qp, k_pairs, v_pairs, r):
    dp = _stack_heads(_mm_nt(do_pairs[0], v_pairs[0]), _mm_nt(do_pairs[1], v_pairs[1]))
    delta = jnp.sum(p * dp, axis=-1, keepdims=True)
    dl = p * (dp - delta)
    dlp = _pair_heads(dl.astype(BF16), r)
    dq = jnp.concatenate([_mm(dlp[0], k_pairs[0]), _mm(dlp[1], k_pairs[1])], axis=-1)
    dk = (_mm_tn(dlp[0], qp[0]), _mm_tn(dlp[1], qp[1]))
    dv = (_mm_tn(pp[0], do_pairs[0]), _mm_tn(pp[1], do_pairs[1]))
    return dl, delta, dq, dk, dv


def _tile_specs(n_tiles_ex, width):
    return pl.BlockSpec((TILE, width), lambda b, i: (b * n_tiles_ex + jnp.minimum(i, n_tiles_ex - 1), 0))


def _prev_chunk_spec(n_tiles_ex, width):
    def index(b, i):
        chunk = TILE_CHUNKS * jnp.minimum(i, n_tiles_ex - 1)
        return (b * n_tiles_ex * TILE_CHUNKS + jnp.maximum(chunk - 1, 0), 0)
    return pl.BlockSpec((CHUNK, width), index)


def _full_spec(shape):
    zeros = (0,) * len(shape)
    return pl.BlockSpec(shape, lambda *_: zeros)


SMEM_SPEC = pl.BlockSpec(memory_space=pltpu.SMEM)
ANY_SPEC = pl.BlockSpec(memory_space=pl.ANY)
VMEM_SPEC = pl.BlockSpec(memory_space=pltpu.VMEM)


def _make_bias(rel_bias_t, buckets):
    def body(rel_ref, bk_ref, out_ref):
        bk = bk_ref[...]
        for h in range(4):
            acc = jnp.zeros((CHUNK, 2 * CHUNK), F32)
            for b in range(N_BUCKETS):
                acc = jnp.where(bk == b, rel_ref[h, b], acc)
            for t, with_prev in enumerate((True, False)):
                out_ref[t, h * CHUNK:(h + 1) * CHUNK, :] = jnp.where(_band_valid(with_prev), acc, NEG)

    return pl.pallas_call(
        body, name="make_bias", out_shape=jax.ShapeDtypeStruct((2, 4 * CHUNK, 2 * CHUNK), F32),
        in_specs=[SMEM_SPEC, VMEM_SPEC], out_specs=VMEM_SPEC,
    )(rel_bias_t, buckets)


def _memkv_forward(mem, g_mem, w_mkv):
    n_ex = mem.shape[0]

    def body(mem_ref, g_ref, w_ref, out_ref):
        m = mem_ref[0]
        r = lax.rsqrt(jnp.mean(m * m, axis=-1, keepdims=True) + EPS)
        out_ref[0] = _mm((m * r * g_ref[...]).astype(BF16), w_ref[...])

    return pl.pallas_call(
        body, name="memkv_forward", grid=(n_ex,),
        out_shape=jax.ShapeDtypeStruct((n_ex, MEM_LEN, 2 * MEM_WIDTH), F32),
        in_specs=[pl.BlockSpec((1, MEM_LEN, D_MODEL), lambda b: (b, 0, 0)), _full_spec((1, D_MODEL)),
                  _full_spec((D_MODEL, 2 * MEM_WIDTH))],
        out_specs=pl.BlockSpec((1, MEM_LEN, 2 * MEM_WIDTH), lambda b: (b, 0, 0)),
    )(mem, g_mem, w_mkv)


PROJ_WIDTHS = (A_WIDTH, A_WIDTH, SWA_WIDTH, KV_WIDTH, KV_WIDTH, MEM_WIDTH, MIX_WIDTH)
PROJ_OFFSETS = tuple(int(v) for v in np.cumsum((0,) + PROJ_WIDTHS))


HALF_WIDTH = IN_WIDTH // 2
HALF_PARTS = ((0, 1, 2, 3), (4, 5, 6))


def _gather_and_project(x2, g_pre, w_in_s, w_mkv_s, w_out_s, x_arr):
    n_tok = x2.shape[0]
    n_tiles = n_tok // PROJ_TILE
    last = n_tiles - 1
    shapes = [w_in_s.shape, w_mkv_s.shape, w_out_s.shape]
    n_w = len(shapes)

    def body(x_sref, x_ref, g_ref, win_hbm, wmkv_hbm, wout_hbm, h_ref, *refs):
        part_refs, refs = refs[:len(PROJ_WIDTHS)], refs[len(PROJ_WIDTHS):]
        gin_hbm, gmkv_hbm, gout_hbm, wg, stage_in, stage_mkv, stage_out, own_mkv, own_out = refs[:9]
        send_sems, recv_sems, local_sems = refs[9:]
        p, t = pl.program_id(0), pl.program_id(1)
        x, y, c = lax.axis_index("x"), lax.axis_index("y"), lax.axis_index("c")
        me, sibling = (x, y, c), (x, y, 1 - c)
        my_shard = 2 * x + y
        gathered = [wg, gmkv_hbm, gout_hbm]

        def half_rows(w, shard, half):
            rows = shapes[w][0] // 2
            if w == 0:
                return wg.at[pl.ds(pl.multiple_of(shard * shapes[0][0] + half * rows, 16), rows), :]
            return gathered[w].at[shard, pl.ds(half * rows, rows), :]

        def first(w, rel):
            src = half_rows(w, my_shard, c) if w == 0 else (own_mkv, own_out)[w - 1].at[
                pl.ds(c * (shapes[w][0] // 2), shapes[w][0] // 2), :]
            k = 3 * w + rel - 1
            return pltpu.make_async_remote_copy(
                src_ref=src, dst_ref=half_rows(w, my_shard, c), send_sem=send_sems.at[k], recv_sem=recv_sems.at[k],
                device_id=(x ^ (rel >> 1), y ^ (rel & 1), c), device_id_type=MESH)

        def landed(w, rel):
            k = 3 * w + rel - 1
            ref = half_rows(w, my_shard ^ rel, c)
            return pltpu.make_async_remote_copy(src_ref=ref, dst_ref=ref, send_sem=send_sems.at[k],
                                                recv_sem=recv_sems.at[k], device_id=me, device_id_type=MESH)

        def passed(w, rel, half, to):
            k = 9 + 3 * w + rel - 1
            ref = half_rows(w, my_shard ^ rel, half)
            return pltpu.make_async_remote_copy(src_ref=ref, dst_ref=ref, send_sem=send_sems.at[k],
                                                recv_sem=recv_sems.at[k], device_id=to, device_id_type=MESH)

        def pass_on(w, rels):
            for rel in rels:
                landed(w, rel).wait_recv()
                passed(w, rel, c, sibling).start()
            for rel in rels:
                passed(w, rel, 1 - c, me).wait_recv()

        own_stores = [pltpu.make_async_copy(own_mkv, gmkv_hbm.at[my_shard], local_sems.at[3]),
                      pltpu.make_async_copy(own_out, gout_hbm.at[my_shard], local_sems.at[4])]

        @pl.when((p == 0) & (t == 0))
        def _():
            loads = [pltpu.make_async_copy(src, dst, local_sems.at[k]) for k, (src, dst) in enumerate(
                ((win_hbm, stage_in), (wmkv_hbm, stage_mkv), (wout_hbm, stage_out)))]
            for cp in loads:
                cp.start()
            loads[0].wait()
            wg[pl.ds(pl.multiple_of(my_shard * shapes[0][0], 16), shapes[0][0]), :] = stage_in[...].astype(BF16)
            for rel in (1, 2):
                first(0, rel).start()
            loads[1].wait()
            loads[2].wait()
            own_mkv[...] = stage_mkv[...].astype(BF16)
            own_out[...] = stage_out[...].astype(BF16)
            for cp in own_stores:
                cp.start()
            pass_on(0, (1,))
            first(0, 3).start()

        @pl.when((p == 1) & (t == 0))
        def _():
            pass_on(0, (2, 3))
            for w in (1, 2):
                for rel in (1, 2, 3):
                    first(w, rel).start()

        xv = x_ref[...]
        r = lax.rsqrt(jnp.mean(xv * xv, axis=-1, keepdims=True) + EPS)
        h = (xv * r * g_ref[...]).astype(BF16)

        @pl.when(p == 0)
        def _():
            h_ref[...] = h

        for hh in range(2):
            @pl.when((p ^ x_sref[0]) == hh)
            def _():
                proj = _mm_nt(h, wg[hh * HALF_WIDTH:(hh + 1) * HALF_WIDTH, :])
                for k in HALF_PARTS[hh]:
                    lo = PROJ_OFFSETS[k] - hh * HALF_WIDTH
                    part_refs[k][...] = proj[:, lo:lo + PROJ_WIDTHS[k]]

        @pl.when((p == 1) & (t == last))
        def _():
            store = pltpu.make_async_copy(wg, gin_hbm, local_sems.at[5])
            store.start()
            for w in (1, 2):
                pass_on(w, (1, 2, 3))
            for w in range(n_w):
                for rel in (1, 2, 3):
                    first(w, rel).wait_send()
                    passed(w, rel, c, sibling).wait_send()
            for cp in own_stores:
                cp.wait()
            store.wait()

    def active_in(hh):
        def index(p, t, xs):
            return (jnp.where((p ^ xs[0]) == hh, t, jnp.where(p == 0, 0, last)), 0)
        return index

    part_specs = [pl.BlockSpec((PROJ_TILE, PROJ_WIDTHS[k]), active_in(hh)) for hh in range(2) for k in HALF_PARTS[hh]]
    vmem = pltpu.VMEM
    out = pl.pallas_call(
        body, name="gather_and_project",
        out_shape=[jax.ShapeDtypeStruct((n_tok, D_MODEL), BF16)]
        + [jax.ShapeDtypeStruct((n_tok, w), F32) for w in PROJ_WIDTHS]
        + [jax.ShapeDtypeStruct((N_CHIPS * shapes[0][0], shapes[0][1]), BF16)]
        + [jax.ShapeDtypeStruct((N_CHIPS,) + s, BF16) for s in shapes[1:]],
        grid_spec=pltpu.PrefetchScalarGridSpec(
            num_scalar_prefetch=1, grid=(2, n_tiles),
            in_specs=[pl.BlockSpec((PROJ_TILE, D_MODEL), lambda p, t, xs: (t, 0)),
                      pl.BlockSpec((1, D_MODEL), lambda p, t, xs: (0, 0)), ANY_SPEC, ANY_SPEC, ANY_SPEC],
            out_specs=[pl.BlockSpec((PROJ_TILE, D_MODEL), lambda p, t, xs: (jnp.where(p == 0, t, last), 0))]
            + part_specs + [ANY_SPEC] * 3,
            scratch_shapes=[vmem((N_CHIPS * shapes[0][0], shapes[0][1]), BF16), vmem(shapes[0], F32),
                            vmem(shapes[1], F32), vmem(shapes[2], F32), vmem(shapes[1], BF16), vmem(shapes[2], BF16),
                            pltpu.SemaphoreType.DMA((18,)), pltpu.SemaphoreType.DMA((18,)),
                            pltpu.SemaphoreType.DMA((6,))]),
        compiler_params=pltpu.CompilerParams(vmem_limit_bytes=VMEM_LIMIT),
    )(x_arr, x2, g_pre, w_in_s, w_mkv_s, w_out_s)
    h, parts, weights = out[0], out[1:1 + len(PROJ_WIDTHS)], out[1 + len(PROJ_WIDTHS):]
    return h, list(parts), weights


def _load_chunk(j, i, sk_ref, sv_ref, skp_ref, svp_ref):
    rows = slice(j * CHUNK, (j + 1) * CHUNK)
    if j == 0:
        k_prev, v_prev, table = skp_ref[...], svp_ref[...], jnp.where(i > 0, 0, 1)
    else:
        prev = slice((j - 1) * CHUNK, j * CHUNK)
        k_prev, v_prev, table = sk_ref[prev, :], sv_ref[prev, :], 0
    k_pairs = _pair_operands(_swa_variants(jnp.concatenate([k_prev, sk_ref[rows, :]], axis=0)))
    v_pairs = _pair_operands(_swa_variants(jnp.concatenate([v_prev, sv_ref[rows, :]], axis=0)))
    return rows, k_pairs, v_pairs, table


def _tile_constants(ws_ref, bs_ref, sink_ref, mkv_ref):
    wm = _causal_weights(ws_ref)
    bs_rows = [jnp.concatenate([bs_ref[g]] * TILE_CHUNKS, axis=0) for g in range(A_GROUPS)]
    sink_col = jnp.max(jnp.concatenate([jnp.full((CHUNK, 128), sink_ref[0, h], F32) for h in range(4)], axis=0),
                       axis=-1, keepdims=True)
    mkv_v = mkv_ref[0]
    mk_pairs = _pair_operands(_mem_variants(mkv_v[:, :MEM_WIDTH]))
    mv_pairs = _pair_operands(_mem_variants(mkv_v[:, MEM_WIDTH:]))
    return wm, bs_rows, sink_col, mk_pairs, mv_pairs


def _forward_mix(parts, mkv, x2, tgt2, v_g, v_b, w_sp, b_sp, sinks, bias, w_out, g_post, n_ex, seq):
    n_tiles_ex = seq // TILE
    n_tok = n_ex * seq
    au, av, sq, sk, sv, mq, z = parts

    def body(au_ref, av_ref, sq_ref, sk_ref, sv_ref, skp_ref, svp_ref, mq_ref, z_ref, mkv_ref, x_ref, tgt_ref,
             vg_ref, vb_ref, ws_ref, bs_ref, sink_ref, bias_ref, wout_ref, gpost_ref,
             dout_ref, do_ref, loss_ref, dgpost_ref):
        b, i = pl.program_id(0), pl.program_id(1)

        @pl.when((b == 0) & (i == 0))
        def _():
            loss_ref[...] = jnp.zeros_like(loss_ref)
            dgpost_ref[...] = jnp.zeros_like(dgpost_ref)

        wm, bs_rows, sink_col, mk_pairs, mv_pairs = _tile_constants(ws_ref, bs_ref, sink_ref, mkv_ref)
        ya, _ = _group_a_forward(au_ref[...], av_ref[...], vg_ref[...], vb_ref[...], wm, bs_rows)
        yb = []
        for j in range(TILE_CHUNKS):
            rows, k_pairs, v_pairs, table = _load_chunk(j, i, sk_ref, sv_ref, skp_ref, svp_ref)
            p, _ = _attention_probs(_halves_bf16(sq_ref[rows, :]), k_pairs, bias_ref[table], sink_col)
            yb.append(_attention_out(p, v_pairs, CHUNK)[0])
        pm, _ = _attention_probs(_halves_bf16(mq_ref[...]), mk_pairs, None, None)
        yc = _attention_out(pm, mv_pairs, TILE)[0]
        ycat = jnp.concatenate(ya + [jnp.concatenate(yb, axis=0), yc], axis=-1)
        zv = z_ref[...]
        y = ycat * (zv * _sigmoid(zv))
        o = _mm(y.astype(BF16), wout_ref[...])
        r2 = lax.rsqrt(jnp.mean(o * o, axis=-1, keepdims=True) + EPS)
        nrm = o * r2
        gp = gpost_ref[...]
        diff = x_ref[...] + nrm * gp - tgt_ref[...]
        loss_ref[...] += jnp.sum(diff * diff) * (0.5 / D_MODEL)
        dout = diff * (1.0 / D_MODEL)
        dout_ref[...] = dout
        dgpost_ref[...] += jnp.sum(dout * nrm, axis=0, keepdims=True)
        dn = dout * gp
        do_ref[...] = r2 * (dn - nrm * jnp.mean(dn * nrm, axis=-1, keepdims=True))

    tile = functools.partial(_tile_specs, n_tiles_ex)
    prev = functools.partial(_prev_chunk_spec, n_tiles_ex)
    return pl.pallas_call(
        body, name="forward_mix", grid=(n_ex, n_tiles_ex),
        out_shape=[jax.ShapeDtypeStruct((n_tok, D_MODEL), F32), jax.ShapeDtypeStruct((n_tok, D_MODEL), F32),
                   jax.ShapeDtypeStruct((1, 128), F32), jax.ShapeDtypeStruct((1, D_MODEL), F32)],
        in_specs=[tile(A_WIDTH), tile(A_WIDTH), tile(SWA_WIDTH), tile(KV_WIDTH), tile(KV_WIDTH),
                  prev(KV_WIDTH), prev(KV_WIDTH), tile(MEM_WIDTH), tile(MIX_WIDTH),
                  pl.BlockSpec((1, MEM_LEN, 2 * MEM_WIDTH), lambda b, i: (b, 0, 0)),
                  tile(D_MODEL), tile(D_MODEL),
                  _full_spec((1, A_WIDTH)), _full_spec((1, A_WIDTH)), _full_spec((A_GROUPS, CHUNK, CHUNK)),
                  _full_spec((A_GROUPS, CHUNK, CHUNK)), SMEM_SPEC, _full_spec((2, 4 * CHUNK, 2 * CHUNK)),
                  _full_spec((MIX_WIDTH, D_MODEL)), _full_spec((1, D_MODEL))],
        out_specs=[tile(D_MODEL), tile(D_MODEL), _full_spec((1, 128)), _full_spec((1, D_MODEL))],
        compiler_params=pltpu.CompilerParams(vmem_limit_bytes=VMEM_LIMIT),
    )(au, av, sq, sk, sv, sk, sv, mq, z, mkv, x2, tgt2, v_g, v_b, w_sp, b_sp, sinks, bias, w_out, g_post)


def _backward_mix(parts, mkv, do, v_g, v_b, w_sp, b_sp, sinks, bias, w_out, n_ex, seq):
    n_tiles_ex = seq // TILE
    n_tok = n_ex * seq
    au, av, sq, sk, sv, mq, z = parts
    col = dict(zip(("au", "av", "sq", "sk", "sv", "mq", "z"),
                   (slice(PROJ_OFFSETS[k], PROJ_OFFSETS[k + 1]) for k in range(len(PROJ_WIDTHS)))))
    before_kv, after_kv = slice(0, col["sk"].start), slice(col["sv"].stop, IN_WIDTH)

    def body(do_ref, au_ref, av_ref, sq_ref, sk_ref, sv_ref, skp_ref, svp_ref, mq_ref, z_ref, mkv_ref,
             vg_ref, vb_ref, ws_ref, bs_ref, sink_ref, bias_ref, wout_ref,
             dproj_ref, dmkv_ref, dwout_ref, dvg_ref, dvb_ref, dws_ref, dbs_ref, dsink_ref, drel_ref,
             carry_dp, carry_k, carry_v):
        b, i = pl.program_id(0), pl.program_id(1)

        @pl.when((b == 0) & (i == 0))
        def _():
            for ref in (dwout_ref, dvg_ref, dvb_ref, dws_ref, dbs_ref, dsink_ref, drel_ref):
                ref[...] = jnp.zeros_like(ref)

        @pl.when(i == 0)
        def _():
            dmkv_ref[...] = jnp.zeros_like(dmkv_ref)
            carry_k[...] = jnp.zeros_like(carry_k)
            carry_v[...] = jnp.zeros_like(carry_v)

        @pl.when(i > 0)
        def _():
            dproj_ref[:, before_kv] = carry_dp[:, before_kv]
            dproj_ref[:, after_kv] = carry_dp[:, after_kv]

        @pl.when(i < n_tiles_ex)
        def _():
            wm, bs_rows, sink_col, mk_pairs, mv_pairs = _tile_constants(ws_ref, bs_ref, sink_ref, mkv_ref)
            vg = vg_ref[...]
            do_b = do_ref[...].astype(BF16)
            dy = _mm_nt(do_b, wout_ref[...])
            zv = z_ref[...]
            sig = _sigmoid(zv)
            sz = zv * sig
            dyc = dy * sz

            au_v, av_v = au_ref[...], av_ref[...]
            ya, res = _group_a_forward(au_v, av_v, vg, vb_ref[...], wm, bs_rows)
            dgu, dgv = [], []
            for g in range(A_GROUPS):
                sl = slice(g * 128, (g + 1) * 128)
                xhat, rstd, vn, s = res["groups"][g]
                dya = dyc[:, sl]
                dgu.append(dya * s)
                ds = dya * res["gu"][:, sl]
                dbs_ref[:, sl] += sum(ds[c * CHUNK:(c + 1) * CHUNK] for c in range(TILE_CHUNKS))
                ds_b = _rows_to_lanes(ds.astype(BF16), TILE_CHUNKS)
                dws_ref[g] += _mm_nt(ds_b, vn)
                dvn = _lanes_to_rows(_mm_tn(wm[g], ds_b), TILE_CHUNKS)
                dvg_ref[:, sl] += jnp.sum(dvn * xhat, axis=0, keepdims=True)
                dvb_ref[:, sl] += jnp.sum(dvn, axis=0, keepdims=True)
                dxh = dvn * vg[:, sl]
                dgv.append(rstd * (dxh - jnp.mean(dxh, axis=-1, keepdims=True)
                                   - xhat * jnp.mean(dxh * xhat, axis=-1, keepdims=True)))
            carry_dp[:, col["au"]] = (jnp.concatenate(dgu, axis=-1) * _gelu_grad(au_v, res["tu"])).astype(BF16)
            carry_dp[:, col["av"]] = (jnp.concatenate(dgv, axis=-1) * _gelu_grad(av_v, res["tv"])).astype(BF16)

            lane4 = lax.broadcasted_iota(jnp.int32, (1, 128), 1)
            dsink_vec = jnp.zeros((1, 128), F32)
            yb, dk_parts, dv_parts = [], [], []
            for j in range(TILE_CHUNKS):
                rows, k_pairs, v_pairs, table = _load_chunk(j, i, sk_ref, sv_ref, skp_ref, svp_ref)
                qp = _halves_bf16(sq_ref[rows, :])
                p, ps = _attention_probs(qp, k_pairs, bias_ref[table], sink_col)
                out, pp = _attention_out(p, v_pairs, CHUNK)
                yb.append(out)
                do_pairs = _halves_bf16(dyc[rows, A_WIDTH:A_WIDTH + SWA_WIDTH])
                dl, delta, dq, dk, dv = _attention_backward(p, pp, do_pairs, qp, k_pairs, v_pairs, CHUNK)
                sink_terms = ps * delta
                for h in range(4):
                    dsink_vec = dsink_vec + jnp.where(lane4 == h, -jnp.sum(sink_terms[h * CHUNK:(h + 1) * CHUNK]), 0.0)
                drel_ref[...] += dl
                carry_dp[rows, col["sq"]] = (dq * QK_SCALE).astype(BF16)
                dk_parts.append(_swa_unvariants(*_split_pair_grads(dk)) * QK_SCALE)
                dv_parts.append(_swa_unvariants(*_split_pair_grads(dv)))

            mqp = _halves_bf16(mq_ref[...])
            pm, _ = _attention_probs(mqp, mk_pairs, None, None)
            yc, ppm = _attention_out(pm, mv_pairs, TILE)
            dc_pairs = _halves_bf16(dyc[:, A_WIDTH + SWA_WIDTH:])
            _, _, dmq, dmk, dmv = _attention_backward(pm, ppm, dc_pairs, mqp, mk_pairs, mv_pairs, TILE)
            carry_dp[:, col["mq"]] = (dmq * QK_SCALE).astype(BF16)
            dmkv_ref[0] += jnp.concatenate([_mem_unvariants(*_split_pair_grads(dmk)) * QK_SCALE,
                                            _mem_unvariants(*_split_pair_grads(dmv))], axis=-1)

            ycat = jnp.concatenate(ya + [jnp.concatenate(yb, axis=0), yc], axis=-1)
            dwout_ref[...] += _mm_tn((ycat * sz).astype(BF16), do_b)
            carry_dp[:, col["z"]] = (dy * ycat * (sig * (1.0 + zv * (1.0 - sig)))).astype(BF16)
            dsink_ref[...] += dsink_vec

            for parts_c, carry, cols in ((dk_parts, carry_k, col["sk"]), (dv_parts, carry_v, col["sv"])):
                @pl.when(i > 0)
                def _():
                    dproj_ref[:, cols] = (carry[...] + jnp.concatenate(
                        [jnp.zeros((TILE - CHUNK, KV_WIDTH), F32), parts_c[0][:CHUNK]], axis=0)).astype(BF16)
                new = [parts_c[0][CHUNK:]]
                for j in range(1, TILE_CHUNKS):
                    new[-1] = new[-1] + parts_c[j][:CHUNK]
                    new.append(parts_c[j][CHUNK:])
                carry[...] = jnp.concatenate(new, axis=0)

        @pl.when(i == n_tiles_ex)
        def _():
            dproj_ref[:, col["sk"]] = carry_k[...].astype(BF16)
            dproj_ref[:, col["sv"]] = carry_v[...].astype(BF16)

    tile = functools.partial(_tile_specs, n_tiles_ex)
    prev = functools.partial(_prev_chunk_spec, n_tiles_ex)
    late = pl.BlockSpec((TILE, IN_WIDTH), lambda b, i: (b * n_tiles_ex + jnp.maximum(i - 1, 0), 0))
    return pl.pallas_call(
        body, name="backward_mix", grid=(n_ex, n_tiles_ex + 1),
        out_shape=[jax.ShapeDtypeStruct((n_tok, IN_WIDTH), BF16),
                   jax.ShapeDtypeStruct((n_ex, MEM_LEN, 2 * MEM_WIDTH), F32),
                   jax.ShapeDtypeStruct((MIX_WIDTH, D_MODEL), F32), jax.ShapeDtypeStruct((1, A_WIDTH), F32),
                   jax.ShapeDtypeStruct((1, A_WIDTH), F32), jax.ShapeDtypeStruct((A_GROUPS, CHUNK, CHUNK), F32),
                   jax.ShapeDtypeStruct((CHUNK, A_WIDTH), F32), jax.ShapeDtypeStruct((1, 128), F32),
                   jax.ShapeDtypeStruct((4 * CHUNK, 2 * CHUNK), F32)],
        in_specs=[tile(D_MODEL), tile(A_WIDTH), tile(A_WIDTH), tile(SWA_WIDTH), tile(KV_WIDTH), tile(KV_WIDTH),
                  prev(KV_WIDTH), prev(KV_WIDTH), tile(MEM_WIDTH), tile(MIX_WIDTH),
                  pl.BlockSpec((1, MEM_LEN, 2 * MEM_WIDTH), lambda b, i: (b, 0, 0)),
                  _full_spec((1, A_WIDTH)), _full_spec((1, A_WIDTH)), _full_spec((A_GROUPS, CHUNK, CHUNK)),
                  _full_spec((A_GROUPS, CHUNK, CHUNK)), SMEM_SPEC, _full_spec((2, 4 * CHUNK, 2 * CHUNK)),
                  _full_spec((MIX_WIDTH, D_MODEL))],
        out_specs=[late, pl.BlockSpec((1, MEM_LEN, 2 * MEM_WIDTH), lambda b, i: (b, 0, 0)),
                   _full_spec((MIX_WIDTH, D_MODEL)), _full_spec((1, A_WIDTH)), _full_spec((1, A_WIDTH)),
                   _full_spec((A_GROUPS, CHUNK, CHUNK)), _full_spec((CHUNK, A_WIDTH)), _full_spec((1, 128)),
                   _full_spec((4 * CHUNK, 2 * CHUNK))],
        scratch_shapes=[pltpu.VMEM((TILE, IN_WIDTH), BF16), pltpu.VMEM((TILE, KV_WIDTH), F32),
                        pltpu.VMEM((TILE, KV_WIDTH), F32)],
        compiler_params=pltpu.CompilerParams(vmem_limit_bytes=VMEM_LIMIT),
    )(do, au, av, sq, sk, sv, sk, sv, mq, z, mkv, v_g, v_b, w_sp, b_sp, sinks, bias, w_out)


BWD_PROJ_TILE = 512


def _backward_projection(x2, dout, dproj, g_pre, w_in_t):
    n_tok = x2.shape[0]
    n_steps = n_tok // BWD_PROJ_TILE

    def body(x_ref, dout_ref, dp_ref, g_ref, w_hbm, dx_ref, dgpre_ref, w_vmem, sem):
        @pl.when(pl.program_id(0) == 0)
        def _():
            load = pltpu.make_async_copy(w_hbm, w_vmem, sem)
            load.start()
            dgpre_ref[...] = jnp.zeros_like(dgpre_ref)
            load.wait()

        xv = x_ref[...]
        r = lax.rsqrt(jnp.mean(xv * xv, axis=-1, keepdims=True) + EPS)
        xn = xv * r
        dh = _mm(dp_ref[...], w_vmem[...])
        dgpre_ref[...] += jnp.sum(dh * xn, axis=0, keepdims=True)
        dhg = dh * g_ref[...]
        dx_ref[...] = r * (dhg - xn * jnp.mean(dhg * xn, axis=-1, keepdims=True)) + dout_ref[...]

    row = lambda w: pl.BlockSpec((BWD_PROJ_TILE, w), lambda i: (i, 0))
    return pl.pallas_call(
        body, name="backward_projection", grid=(n_steps,),
        out_shape=[jax.ShapeDtypeStruct((n_tok, D_MODEL), F32), jax.ShapeDtypeStruct((1, D_MODEL), F32)],
        in_specs=[row(D_MODEL), row(D_MODEL), row(IN_WIDTH), _full_spec((1, D_MODEL)), ANY_SPEC],
        out_specs=[row(D_MODEL), _full_spec((1, D_MODEL))],
        scratch_shapes=[pltpu.VMEM((IN_WIDTH, D_MODEL), BF16), pltpu.SemaphoreType.DMA],
        input_output_aliases={1: 0},
        compiler_params=pltpu.CompilerParams(vmem_limit_bytes=VMEM_LIMIT),
    )(x2, dout, dproj, g_pre, w_in_t)


SHARD_ROWS = IN_WIDTH // N_CHIPS
SHARD_WINDOW = 768
SHARD_HALF = SHARD_ROWS // 2
DWIN_TILE = 2048
N_REL = N_CHIPS - 1


def _shard_window_start(shard):
    return (shard * SHARD_ROWS // 128) * 128


def _reduce_gradients(dproj, h, big, small, shard_arr):
    n_tok = h.shape[0]
    tile = min(DWIN_TILE, n_tok)
    n_sub = n_tok // tile
    last = N_CHIPS - 1
    n_big, n_small = len(big), len(small)
    big_half = [g.shape[2:] for g in big]
    sem_big_d2d = 2 * N_CHIPS
    sem_big_ici = sem_big_d2d + n_big
    sem_big_swap = sem_big_ici + N_REL * n_big
    sem_small_d2d = sem_big_swap + n_big
    sem_small_ici = sem_small_d2d + n_small
    n_sems = sem_small_ici + N_REL * n_small
    loc_small = n_big
    loc_out_win = loc_small + n_small
    loc_out_big = loc_out_win + 2
    loc_out_small = loc_out_big + 2 * n_big
    n_local = loc_out_small + n_small

    def shard_of_slot(s, my_shard):
        return my_shard ^ ((s + 1) % N_CHIPS)

    def body(shard_ref, dp_ref, h_hbm, *refs):
        h_vmem, h_sem, refs = refs[-2], refs[-1], refs[:-2]
        big_hbm, refs = refs[:n_big], refs[n_big:]
        small_hbm, refs = refs[:n_small], refs[n_small:]
        out_hbm, refs = refs[0], refs[1:]
        big_out, refs = refs[:n_big], refs[n_big:]
        small_out, refs = refs[:n_small], refs[n_small:]
        part, recv_d2d, send_ici, recv_ici, mine_buf, other_buf = refs[:6]
        refs = refs[6:]
        big_own, big_recv, big_send, big_land, big_mine, big_other = (
            refs[k * n_big:(k + 1) * n_big] for k in range(6))
        refs = refs[6 * n_big:]
        small_own, small_recv, small_all = (refs[k * n_small:(k + 1) * n_small] for k in range(3))
        send_sems, recv_sems, local_sems = refs[3 * n_small:]

        s, t = pl.program_id(0), pl.program_id(1)
        x, y, c = lax.axis_index("x"), lax.axis_index("y"), lax.axis_index("c")
        my_chip = 2 * x + y
        sibling = (x, y, 1 - c)
        my_rows = pl.ds(pl.multiple_of(c * SHARD_HALF, 8), SHARD_HALF)
        other_rows = pl.ds(pl.multiple_of((1 - c) * SHARD_HALF, 8), SHARD_HALF)

        def remote(src, dst, k, to):
            return pltpu.make_async_remote_copy(src_ref=src, dst_ref=dst, send_sem=send_sems.at[k],
                                                recv_sem=recv_sems.at[k], device_id=to, device_id_type=MESH)

        def chip_at(rel):
            return (x ^ (rel >> 1), y ^ (rel & 1), c)

        def to_sibling(k):
            return remote(part.at[k % 2, other_rows, :], recv_d2d.at[k], k, sibling)

        def to_chip(k):
            return remote(send_ici.at[k], recv_ici.at[k], N_CHIPS + k, chip_at(k + 1))

        swap = remote(mine_buf, other_buf, 2 * N_CHIPS - 1, sibling)
        big_load = [pltpu.make_async_copy(big_hbm[w].at[:, pl.ds(c, 1)], big_own[w], local_sems.at[w])
                    for w in range(n_big)]
        big_to_sibling = [remote(big_hbm[w].at[:, pl.ds(1 - c, 1)], big_recv[w], sem_big_d2d + w, sibling)
                          for w in range(n_big)]
        big_to_chip = [[remote(big_send[w].at[k], big_land[w].at[k], sem_big_ici + N_REL * w + k, chip_at(k + 1))
                        for k in range(N_REL)] for w in range(n_big)]
        big_swap = [remote(big_mine[w], big_other[w], sem_big_swap + w, sibling) for w in range(n_big)]
        small_load = [pltpu.make_async_copy(small_hbm[i], small_own[i], local_sems.at[loc_small + i])
                      for i in range(n_small)]
        small_to_sibling = [remote(small_hbm[i], small_recv[i], sem_small_d2d + i, sibling) for i in range(n_small)]
        small_to_chip = [[remote(small_all[i].at[my_chip], small_all[i].at[my_chip],
                                 sem_small_ici + N_REL * i + k, chip_at(k + 1))
                          for k in range(N_REL)] for i in range(n_small)]

        @pl.when((s == 0) & (t == 0))
        def _():
            h_load = pltpu.make_async_copy(h_hbm, h_vmem, h_sem)
            h_load.start()
            for cp in big_load + big_to_sibling + small_load + small_to_sibling:
                cp.start()
            h_load.wait()

        @pl.when((s == 0) & (t == n_sub - 1))
        def _():
            for cp in big_load + small_load:
                cp.wait()
            for cp in big_to_sibling + small_to_sibling:
                cp.wait_recv()
                cp.wait_send()
            for w in range(n_big):
                for k in range(N_REL):
                    shard = my_chip ^ (k + 1)
                    big_send[w][k] = (big_own[w][shard, 0] + big_recv[w][shard, 0]).astype(BF16)
                    big_to_chip[w][k].start()
            for i in range(n_small):
                small_all[i][my_chip] = small_own[i][...] + small_recv[i][...]
                for k in range(N_REL):
                    small_to_chip[i][k].start()

        @pl.when((s > 0) & (t == 0))
        def _():
            k = s - 1
            cp = to_sibling(k)
            cp.wait_recv()
            cp.wait_send()
            send_ici[k] = (part[k % 2, my_rows, :] + recv_d2d[k]).astype(BF16)
            to_chip(k).start()

        r = _mm_tn(dp_ref[...], h_vmem[pl.ds(pl.multiple_of(t * tile, tile), tile), :])
        odd = shard_of_slot(s, shard_ref[0]) % 2
        for parity in range(2):
            rows = r[64 * parity:64 * parity + SHARD_ROWS]

            @pl.when((odd == parity) & (t == 0))
            def _():
                part[s % 2] = rows

            @pl.when((odd == parity) & (t > 0))
            def _():
                part[s % 2] += rows

        @pl.when(t == n_sub - 1)
        def _():
            to_sibling(s).start()

        @pl.when((s == last) & (t == n_sub - 1))
        def _():
            cp = to_sibling(last)
            cp.wait_recv()
            cp.wait_send()
            total = part[last % 2, my_rows, :] + recv_d2d[last]
            for k in range(last):
                to_chip(k).wait_recv()
                total = total + recv_ici[k].astype(F32)
            mine_buf[...] = total
            swap.start()
            out_mine = pltpu.make_async_copy(mine_buf, out_hbm.at[my_rows, :], local_sems.at[0])
            out_mine.start()
            swap.wait_recv()
            out_other = pltpu.make_async_copy(other_buf, out_hbm.at[other_rows, :], local_sems.at[1])
            out_other.start()
            stores = [out_mine, out_other]
            for w in range(n_big):
                rows = big_half[w][0]
                total = big_own[w][my_chip, 0] + big_recv[w][my_chip, 0]
                for k in range(N_REL):
                    big_to_chip[w][k].wait_recv()
                    total = total + big_land[w][k].astype(F32)
                big_mine[w][...] = total
                big_swap[w].start()
                stores.append(pltpu.make_async_copy(
                    big_mine[w], big_out[w].at[pl.ds(pl.multiple_of(c * rows, 8), rows), :],
                    local_sems.at[loc_out_big + 2 * w]))
                stores[-1].start()
            for w in range(n_big):
                rows = big_half[w][0]
                big_swap[w].wait_recv()
                stores.append(pltpu.make_async_copy(
                    big_other[w], big_out[w].at[pl.ds(pl.multiple_of((1 - c) * rows, 8), rows), :],
                    local_sems.at[loc_out_big + 2 * w + 1]))
                stores[-1].start()
            for i in range(n_small):
                for k in range(N_REL):
                    small_to_chip[i][k].wait_recv()
                stores.append(pltpu.make_async_copy(small_all[i], small_out[i], local_sems.at[loc_out_small + i]))
                stores[-1].start()
            for k in range(last):
                to_chip(k).wait_send()
            swap.wait_send()
            for w in range(n_big):
                for k in range(N_REL):
                    big_to_chip[w][k].wait_send()
                big_swap[w].wait_send()
            for i in range(n_small):
                for k in range(N_REL):
                    small_to_chip[i][k].wait_send()
            for cp in stores:
                cp.wait()

    half = (SHARD_HALF, D_MODEL)
    vmem = pltpu.VMEM
    scratch = [vmem((2, SHARD_ROWS, D_MODEL), F32), vmem((N_CHIPS,) + half, F32),
               vmem((N_REL,) + half, BF16), vmem((N_REL,) + half, BF16), vmem(half, F32), vmem(half, F32)]
    scratch += [vmem((N_CHIPS, 1) + hs, F32) for hs in big_half] * 2
    scratch += [vmem((N_REL,) + hs, BF16) for hs in big_half] * 2
    scratch += [vmem(hs, F32) for hs in big_half] * 2
    scratch += [vmem(a.shape, F32) for a in small] * 2 + [vmem((N_CHIPS,) + a.shape, F32) for a in small]
    scratch += [pltpu.SemaphoreType.DMA((n_sems,)), pltpu.SemaphoreType.DMA((n_sems,)),
                pltpu.SemaphoreType.DMA((n_local,)), vmem(h.shape, BF16), pltpu.SemaphoreType.DMA]
    n_hbm = n_big + n_small
    out = pl.pallas_call(
        body, name="reduce_gradients",
        out_shape=[jax.ShapeDtypeStruct((SHARD_ROWS, D_MODEL), F32)]
        + [jax.ShapeDtypeStruct((2 * hs[0], hs[1]), F32) for hs in big_half]
        + [jax.ShapeDtypeStruct((N_CHIPS,) + a.shape, F32) for a in small],
        grid_spec=pltpu.PrefetchScalarGridSpec(
            num_scalar_prefetch=1, grid=(N_CHIPS, n_sub),
            in_specs=[pl.BlockSpec((pl.Element(tile), pl.Element(SHARD_WINDOW)),
                                   lambda s, t, m: (t * tile, _shard_window_start(shard_of_slot(s, m[0])))),
                      ANY_SPEC] + [ANY_SPEC] * n_hbm,
            out_specs=[ANY_SPEC] * (1 + n_hbm),
            scratch_shapes=scratch),
        compiler_params=pltpu.CompilerParams(vmem_limit_bytes=VMEM_LIMIT),
    )(shard_arr, dproj, h, *big, *small)
    return out[:1 + n_big], out[1 + n_big:]


def _memkv_backward(mem, dmkv, g_mem, w_mkv):
    n_ex = mem.shape[0]

    def body(mem_ref, d_ref, g_ref, w_ref, dw_ref, dg_ref):
        @pl.when(pl.program_id(0) == 0)
        def _():
            dw_ref[...] = jnp.zeros_like(dw_ref)
            dg_ref[...] = jnp.zeros_like(dg_ref)

        m = mem_ref[0]
        mn = m * lax.rsqrt(jnp.mean(m * m, axis=-1, keepdims=True) + EPS)
        d_b = d_ref[0].astype(BF16)
        dw_ref[...] += _mm_tn((mn * g_ref[...]).astype(BF16), d_b)
        dg_ref[...] += jnp.sum(_mm_nt(d_b, w_ref[...]) * mn, axis=0, keepdims=True)

    return pl.pallas_call(
        body, name="memkv_backward", grid=(n_ex,),
        out_shape=[jax.ShapeDtypeStruct((D_MODEL, 2 * MEM_WIDTH), F32), jax.ShapeDtypeStruct((1, D_MODEL), F32)],
        in_specs=[pl.BlockSpec((1, MEM_LEN, D_MODEL), lambda b: (b, 0, 0)),
                  pl.BlockSpec((1, MEM_LEN, 2 * MEM_WIDTH), lambda b: (b, 0, 0)),
                  _full_spec((1, D_MODEL)), _full_spec((D_MODEL, 2 * MEM_WIDTH))],
        out_specs=[_full_spec((D_MODEL, 2 * MEM_WIDTH)), _full_spec((1, D_MODEL))],
    )(mem, dmkv, g_mem, w_mkv)


def _pack_small_grads(dgpre, dgpost, dgmem, dvg, dvb, dws, dbs, dsink, drel, loss_vec, buckets):
    def body(dgpre_ref, dgpost_ref, dgmem_ref, dvg_ref, dvb_ref, dws_ref, dbs_ref, dsink_ref, drel_ref, loss_ref,
             bk_ref, a_ref, b_ref):
        a_ref[...] = jnp.zeros_like(a_ref)
        b_ref[...] = jnp.zeros_like(b_ref)
        a_ref[0:1, :] = dgpre_ref[...]
        a_ref[1:2, :] = dgpost_ref[...]
        a_ref[2:3, :] = dgmem_ref[...]
        a_ref[3:4, :] = jnp.concatenate([dvg_ref[...], dvb_ref[...]], axis=-1)
        a_ref[ROW_LOSS:ROW_LOSS + 1, 0:128] = loss_ref[...]
        row = lax.broadcasted_iota(jnp.int32, (CHUNK, CHUNK), 0)
        col = lax.broadcasted_iota(jnp.int32, (CHUNK, CHUNK), 1)
        for g in range(A_GROUPS):
            b_ref[ROW_WS + g * CHUNK:ROW_WS + (g + 1) * CHUNK, :] = jnp.where(row >= col, dws_ref[g], 0.0)
            by_token = jnp.transpose(dbs_ref[:, g * 128:(g + 1) * 128])
            b_ref[ROW_BS + g:ROW_BS + g + 1, :] = jnp.sum(by_token, axis=0, keepdims=True)
        b_ref[ROW_SINK:ROW_SINK + 1, :] = dsink_ref[...]
        bk = bk_ref[...]
        rel_row = lax.broadcasted_iota(jnp.int32, (8, 128), 0)
        rel_col = lax.broadcasted_iota(jnp.int32, (8, 128), 1)
        rel = jnp.zeros((8, 128), F32)
        for h in range(4):
            acc = drel_ref[h * CHUNK:(h + 1) * CHUNK, :]
            for b in range(N_BUCKETS):
                rel = jnp.where((rel_row == h) & (rel_col == b), jnp.sum(jnp.where(bk == b, acc, 0.0)), rel)
        b_ref[ROW_REL:ROW_REL + 8, :] = rel

    return pl.pallas_call(
        body, name="pack_small_grads",
        out_shape=[jax.ShapeDtypeStruct((SMALL_A_ROWS, D_MODEL), F32), jax.ShapeDtypeStruct((SMALL_B_ROWS, 128), F32)],
        in_specs=[VMEM_SPEC] * 11, out_specs=[VMEM_SPEC] * 2,
    )(dgpre, dgpost, dgmem, dvg, dvb, dws, dbs, dsink, drel, loss_vec, buckets)


def _adamw(w, g, m, v):
    m2 = ADAM_B1 * m + (1.0 - ADAM_B1) * g
    v2 = ADAM_B2 * v + (1.0 - ADAM_B2) * (g * g)
    m_hat = m2 / (1.0 - ADAM_B1 ** ADAM_STEP)
    v_hat = v2 / (1.0 - ADAM_B2 ** ADAM_STEP)
    delta = -ADAM_LR * (m_hat / (jnp.sqrt(v_hat) + ADAM_EPS) + ADAM_WD * w)
    return delta, m2, v2


ADAM_MAX_ROWS = 176


def _adamw_whole(g, w, m, v, name):
    rows, cols = w.shape
    steps = -(-rows // ADAM_MAX_ROWS)
    block_rows = rows // steps
    assert block_rows * steps == rows and block_rows % 8 == 0

    def body(g_ref, w_ref, m_ref, v_ref, d_out, m_out, v_out):
        delta, m2, v2 = _adamw(w_ref[...], g_ref[...], m_ref[...], v_ref[...])
        d_out[...] = delta
        m_out[...] = m2
        v_out[...] = v2

    block = pl.BlockSpec((block_rows, cols), lambda k: (k, 0))
    out = pl.pallas_call(
        body, name=name, grid=(steps,), out_shape=[jax.ShapeDtypeStruct((rows, cols), F32)] * 3,
        in_specs=[block] * 4, out_specs=[block] * 3,
    )(g, w, m, v)
    return [g] + list(out)


def _adamw_small(ra, rb, weights, moments_m, moments_v):
    n = len(weights)

    def body(*refs):
        ra_ref, rb_ref = refs[0], refs[1]
        w_refs, m_refs, v_refs = refs[2:2 + n], refs[2 + n:2 + 2 * n], refs[2 + 2 * n:2 + 3 * n]
        outs = refs[2 + 3 * n:]
        g_outs, d_outs, m_outs, v_outs = outs[:n], outs[n:2 * n], outs[2 * n:3 * n], outs[3 * n:4 * n]
        ga, gb = ra_ref[0], rb_ref[0]
        for chip in range(1, N_CHIPS):
            ga = ga + ra_ref[chip]
            gb = gb + rb_ref[chip]
        outs[4 * n][...] = ga[ROW_LOSS:ROW_LOSS + 1, 0:128]
        grads = [ga[0:1, :], ga[1:2, :], ga[2:3, :], ga[3:4, :A_WIDTH], ga[3:4, A_WIDTH:],
                 gb[ROW_WS:ROW_WS + A_GROUPS * CHUNK, :].reshape(A_GROUPS, CHUNK, CHUNK),
                 gb[ROW_BS:ROW_BS + A_GROUPS, :], gb[ROW_SINK:ROW_SINK + 1, 0:4],
                 gb[ROW_REL:ROW_REL + 4, 0:N_BUCKETS]]
        for k in range(n):
            delta, m2, v2 = _adamw(w_refs[k][...], grads[k], m_refs[k][...], v_refs[k][...])
            g_outs[k][...] = grads[k]
            d_outs[k][...] = delta
            m_outs[k][...] = m2
            v_outs[k][...] = v2

    out_shape = [jax.ShapeDtypeStruct(w.shape, F32) for w in weights] * 4 + [jax.ShapeDtypeStruct((1, 128), F32)]
    return pl.pallas_call(
        body, name="adamw_small", out_shape=out_shape,
        in_specs=[VMEM_SPEC] * (2 + 3 * n), out_specs=[VMEM_SPEC] * (4 * n + 1),
    )(ra, rb, *weights, *moments_m, *moments_v)


def kernel(x, mem, pre_norm_g, post_norm_g, mem_norm_g, w_in, w_mem_kv, v_norm_g, v_norm_b, w_spatial, b_spatial, attn_sinks, rel_bias, w_out, loss_target, m_pre_norm_g, m_post_norm_g, m_mem_norm_g, m_w_in, m_w_mem_kv, m_v_norm_g, m_v_norm_b, m_w_spatial, m_b_spatial, m_attn_sinks, m_rel_bias, m_w_out, v_pre_norm_g, v_post_norm_g, v_mem_norm_g, v_w_in, v_w_mem_kv, v_v_norm_g, v_v_norm_b, v_w_spatial, v_b_spatial, v_attn_sinks, v_rel_bias, v_w_out):
    n_ex, seq, _ = x.shape
    n_tok = n_ex * seq
    x2 = x.reshape(n_tok, D_MODEL)
    tgt2 = loss_target.reshape(n_tok, D_MODEL)
    buckets = jnp.asarray(_bucket_map())
    shard_arr = (2 * lax.axis_index("x") + lax.axis_index("y")).astype(jnp.int32).reshape(1)
    w_sp = w_spatial[0]
    b_sp = jnp.broadcast_to(b_spatial[0][:, :, None], (A_GROUPS, CHUNK, CHUNK))
    w_in_t, m_w_in_t, v_w_in_t = (jnp.transpose(a[0]) for a in (w_in, m_w_in, v_w_in))
    rel_t, m_rel_t, v_rel_t = (jnp.transpose(a) for a in (rel_bias, m_rel_bias, v_rel_bias))

    x_arr = lax.axis_index("x").astype(jnp.int32).reshape(1)
    h_b, parts, (w_in_b, g_mkv, g_out) = _gather_and_project(x2, pre_norm_g, w_in_t, w_mem_kv[0], w_out[0], x_arr)
    w_mkv_b = g_mkv.reshape(D_MODEL, 2 * MEM_WIDTH)
    w_out_b = g_out.reshape(MIX_WIDTH, D_MODEL)

    bias = _make_bias(rel_t, buckets)
    mkv = _memkv_forward(mem, mem_norm_g, w_mkv_b)
    dout, do, loss_vec, dgpost = _forward_mix(parts, mkv, x2, tgt2, v_norm_g, v_norm_b, w_sp, b_sp, attn_sinks, bias,
                                             w_out_b, post_norm_g, n_ex, seq)

    dproj, dmkv, dwout, dvg, dvb, dws, dbs, dsink, drel = _backward_mix(
        parts, mkv, do, v_norm_g, v_norm_b, w_sp, b_sp, attn_sinks, bias, w_out_b, n_ex, seq)
    dx, dgpre = _backward_projection(x2, dout, dproj, pre_norm_g, w_in_b)
    dwmkv, dgmem = _memkv_backward(mem, dmkv, mem_norm_g, w_mkv_b)
    small_a, small_b = _pack_small_grads(dgpre, dgpost, dgmem, dvg, dvb, dws, dbs, dsink, drel, loss_vec, buckets)

    shard_shapes = [w_mem_kv.shape[1:], w_out.shape[1:]]
    big = [g.reshape(N_CHIPS, 2, s[0] // 2, s[1]) for g, s in zip((dwmkv, dwout), shard_shapes)]
    (g_win, g_wmkv, g_wout), (ga, gb) = _reduce_gradients(dproj, h_b, big, [small_a, small_b], shard_arr)

    big_out = [_adamw_whole(g_win, w_in_t, m_w_in_t, v_w_in_t, "adamw_w_in"),
               _adamw_whole(g_wmkv, w_mem_kv[0], m_w_mem_kv[0], v_w_mem_kv[0], "adamw_w_mem_kv"),
               _adamw_whole(g_wout, w_out[0], m_w_out[0], v_w_out[0], "adamw_w_out")]
    small_w = [pre_norm_g, post_norm_g, mem_norm_g, v_norm_g, v_norm_b, w_sp, b_spatial[0], attn_sinks, rel_t]
    small_m = [m_pre_norm_g, m_post_norm_g, m_mem_norm_g, m_v_norm_g, m_v_norm_b, m_w_spatial[0], m_b_spatial[0],
               m_attn_sinks, m_rel_t]
    small_v = [v_pre_norm_g, v_post_norm_g, v_mem_norm_g, v_v_norm_g, v_v_norm_b, v_w_spatial[0], v_b_spatial[0],
               v_attn_sinks, v_rel_t]
    small_out = _adamw_small(ga, gb, small_w, small_m, small_v)
    n_small = len(small_w)

    outputs = [small_out[4 * n_small][0, 0], dx.reshape(x.shape)]
    for kind in range(4):
        s = small_out[kind * n_small:(kind + 1) * n_small]
        outputs += [s[0], s[1], s[2], jnp.transpose(big_out[0][kind])[None], big_out[1][kind][None], s[3], s[4],
                    s[5][None], s[6][None], s[7], jnp.transpose(s[8]), big_out[2][kind][None]]
    return tuple(outputs)
```

```python
import functools

import numpy as np
import jax
import jax.numpy as jnp
from jax import lax
from jax.experimental import pallas as pl
from jax.experimental.pallas import tpu as pltpu

F32 = jnp.float32
BF16 = jnp.bfloat16
MESH = pl.DeviceIdType.MESH

D_MODEL = 1024
CHUNK = 128
A_WIDTH = 512
A_GROUPS = 4
SWA_WIDTH = 256
KV_WIDTH = 128
MEM_WIDTH = 256
MEM_LEN = 256
MIX_WIDTH = 1024
IN_WIDTH = 2816
N_BUCKETS = 32
MAX_DISTANCE = 128
EPS = 1e-6
NEG = -1e30
QK_SCALE = 0.125
HALF_HEAD_PAIR = 64

ADAM_LR = 0.001
ADAM_B1 = 0.9
ADAM_B2 = 0.999
ADAM_EPS = 1e-08
ADAM_WD = 0.01
ADAM_STEP = 10

N_CHIPS = 4
TILE_CHUNKS = 2
TILE = TILE_CHUNKS * CHUNK
PROJ_TILE = 512
VMEM_LIMIT = 56 * 1024 * 1024

SMALL_A_ROWS = 8
ROW_LOSS = 4
ROW_WS = 0
ROW_BS = 512
ROW_SINK = 520
ROW_REL = 528
SMALL_B_ROWS = 536


def _mm(a, b):
    return lax.dot_general(a, b, (((1,), (0,)), ((), ())), preferred_element_type=F32)


def _mm_nt(a, b):
    return lax.dot_general(a, b, (((1,), (1,)), ((), ())), preferred_element_type=F32)


def _mm_tn(a, b):
    return lax.dot_general(a, b, (((0,), (0,)), ((), ())), preferred_element_type=F32)


def _bucket_map():
    qi = np.arange(CHUNK)[:, None]
    kj = np.arange(2 * CHUNK)[None, :]
    n = np.maximum(qi + CHUNK - kj, 0)
    max_exact = N_BUCKETS // 2
    large = max_exact + (np.log(np.maximum(n, 1) / max_exact) / np.log(MAX_DISTANCE / max_exact)
                         * (N_BUCKETS - max_exact)).astype(np.int32)
    large = np.minimum(large, N_BUCKETS - 1)
    return np.where(n < max_exact, n, large).astype(np.int32)


_GELU_C = 0.7978845608028654
_GELU_A = 0.044715


def _gelu(x):
    t = jnp.tanh(_GELU_C * (x + _GELU_A * x * x * x))
    return 0.5 * x * (1.0 + t), t


def _gelu_grad(x, t):
    return 0.5 * (1.0 + t) + 0.5 * x * (1.0 - t * t) * (_GELU_C * (1.0 + 3.0 * _GELU_A * x * x))


def _sigmoid(x):
    return 1.0 / (1.0 + jnp.exp(-x))


def _lane_lo(shape):
    return lax.broadcasted_iota(jnp.int32, shape, 1) < HALF_HEAD_PAIR


def _swa_variants(t):
    lo = _lane_lo(t.shape)
    tr = pltpu.roll(t, HALF_HEAD_PAIR, 1)
    zero = jnp.zeros_like(t)
    return (jnp.where(lo, t, zero).astype(BF16), jnp.where(lo, zero, tr).astype(BF16),
            jnp.where(lo, tr, zero).astype(BF16), jnp.where(lo, zero, t).astype(BF16))


def _swa_unvariants(d0, d1, d2, d3):
    lo = _lane_lo(d0.shape)
    zero = jnp.zeros_like(d0)
    rolled = jnp.where(lo, zero, d1) + jnp.where(lo, d2, zero)
    return jnp.where(lo, d0, zero) + jnp.where(lo, zero, d3) + pltpu.roll(rolled, HALF_HEAD_PAIR, 1)


def _mem_variants(t):
    out = []
    for pair in range(2):
        tp = t[:, pair * 128:(pair + 1) * 128]
        lo = _lane_lo(tp.shape)
        zero = jnp.zeros_like(tp)
        out.append(jnp.where(lo, tp, zero).astype(BF16))
        out.append(jnp.where(lo, zero, tp).astype(BF16))
    return out


def _mem_unvariants(d0, d1, d2, d3):
    lo = _lane_lo(d0.shape)
    return jnp.concatenate([jnp.where(lo, d0, d1), jnp.where(lo, d2, d3)], axis=-1)


def _softmax(logits, sinks):
    m = jnp.max(logits, axis=-1, keepdims=True)
    if sinks is not None:
        m = jnp.maximum(m, sinks)
    p = jnp.exp(logits - m)
    den = jnp.sum(p, axis=-1, keepdims=True)
    if sinks is None:
        return p * (1.0 / den), None
    es = jnp.exp(sinks - m)
    inv = 1.0 / (den + es)
    return p * inv, es * inv


def _band_valid(with_prev):
    qi = lax.broadcasted_iota(jnp.int32, (CHUNK, 2 * CHUNK), 0)
    kj = lax.broadcasted_iota(jnp.int32, (CHUNK, 2 * CHUNK), 1)
    in_cur = (kj >= CHUNK) & (kj - CHUNK <= qi)
    if not with_prev:
        return in_cur
    return in_cur | ((kj < CHUNK) & (kj > qi))


def _causal_weights(ws_ref):
    row = lax.broadcasted_iota(jnp.int32, (CHUNK, CHUNK), 0)
    col = lax.broadcasted_iota(jnp.int32, (CHUNK, CHUNK), 1)
    return [jnp.where(row >= col, ws_ref[g], 0.0).astype(BF16) for g in range(A_GROUPS)]


def _rows_to_lanes(a, n):
    return jnp.concatenate([a[c * CHUNK:(c + 1) * CHUNK] for c in range(n)], axis=1)


def _lanes_to_rows(a, n):
    w = a.shape[1] // n
    return jnp.concatenate([a[:, c * w:(c + 1) * w] for c in range(n)], axis=0)


def _stack_heads(pair01, pair23):
    return jnp.concatenate([pair01[:, :256], pair01[:, 256:], pair23[:, :256], pair23[:, 256:]], axis=0)


def _pair_heads(s, r):
    return (jnp.concatenate([s[0:r], s[r:2 * r]], axis=1), jnp.concatenate([s[2 * r:3 * r], s[3 * r:4 * r]], axis=1))


def _pair_operands(variants):
    return (jnp.concatenate(variants[0:2], axis=0), jnp.concatenate(variants[2:4], axis=0))


def _split_pair_grads(d_pairs):
    return d_pairs[0][:256], d_pairs[0][256:], d_pairs[1][:256], d_pairs[1][256:]


def _halves_bf16(a):
    return (a[:, :128].astype(BF16), a[:, 128:].astype(BF16))


def _group_a_forward(au, av, vg, vb, wm, bs_rows):
    gu, tu = _gelu(au)
    gv, tv = _gelu(av)
    ya, res = [], []
    for g in range(A_GROUPS):
        sl = slice(g * 128, (g + 1) * 128)
        xg = gv[:, sl]
        xc = xg - jnp.mean(xg, axis=-1, keepdims=True)
        rstd = lax.rsqrt(jnp.mean(xc * xc, axis=-1, keepdims=True) + EPS)
        xhat = xc * rstd
        vn = _rows_to_lanes((xhat * vg[:, sl] + vb[:, sl]).astype(BF16), TILE_CHUNKS)
        s = _lanes_to_rows(_mm(wm[g], vn), TILE_CHUNKS) + bs_rows[g]
        ya.append(gu[:, sl] * s)
        res.append((xhat, rstd, vn, s))
    return ya, dict(gu=gu, tu=tu, tv=tv, groups=res)


def _attention_probs(qp, k_pairs, bias, sink_col):
    logits = _stack_heads(_mm_nt(qp[0], k_pairs[0]), _mm_nt(qp[1], k_pairs[1])) * QK_SCALE
    if bias is not None:
        logits = logits + bias
    return _softmax(logits, sink_col)


def _attention_out(p, v_pairs, r):
    pp = _pair_heads(p.astype(BF16), r)
    return jnp.concatenate([_mm(pp[0], v_pairs[0]), _mm(pp[1], v_pairs[1])], axis=-1), pp


def _attention_backward(p, pp, do_pairs, qp, k_pairs, v_pairs, r):
    dp = _stack_heads(_mm_nt(do_pairs[0], v_pairs[0]), _mm_nt(do_pairs[1], v_pairs[1]))
    delta = jnp.sum(p * dp, axis=-1, keepdims=True)
    dl = p * (dp - delta)
    dlp = _pair_heads(dl.astype(BF16), r)
    dq = jnp.concatenate([_mm(dlp[0], k_pairs[0]), _mm(dlp[1], k_pairs[1])], axis=-1)
    dk = (_mm_tn(dlp[0], qp[0]), _mm_tn(dlp[1], qp[1]))
    dv = (_mm_tn(pp[0], do_pairs[0]), _mm_tn(pp[1], do_pairs[1]))
    return dl, delta, dq, dk, dv


def _tile_specs(n_tiles_ex, width):
    return pl.BlockSpec((TILE, width), lambda b, i: (b * n_tiles_ex + jnp.minimum(i, n_tiles_ex - 1), 0))


def _prev_chunk_spec(n_tiles_ex, width):
    def index(b, i):
        chunk = TILE_CHUNKS * jnp.minimum(i, n_tiles_ex - 1)
        return (b * n_tiles_ex * TILE_CHUNKS + jnp.maximum(chunk - 1, 0), 0)
    return pl.BlockSpec((CHUNK, width), index)


def _full_spec(shape):
    zeros = (0,) * len(shape)
    return pl.BlockSpec(shape, lambda *_: zeros)


SMEM_SPEC = pl.BlockSpec(memory_space=pltpu.SMEM)
ANY_SPEC = pl.BlockSpec(memory_space=pl.ANY)
VMEM_SPEC = pl.BlockSpec(memory_space=pltpu.VMEM)


def _make_bias(rel_bias_t, buckets):
    def body(rel_ref, bk_ref, out_ref):
        bk = bk_ref[...]
        for h in range(4):
            acc = jnp.zeros((CHUNK, 2 * CHUNK), F32)
            for b in range(N_BUCKETS):
                acc = jnp.where(bk == b, rel_ref[h, b], acc)
            for t, with_prev in enumerate((True, False)):
                out_ref[t, h * CHUNK:(h + 1) * CHUNK, :] = jnp.where(_band_valid(with_prev), acc, NEG)

    return pl.pallas_call(
        body, name="make_bias", out_shape=jax.ShapeDtypeStruct((2, 4 * CHUNK, 2 * CHUNK), F32),
        in_specs=[SMEM_SPEC, VMEM_SPEC], out_specs=VMEM_SPEC,
    )(rel_bias_t, buckets)


def _memkv_forward(mem, g_mem, w_mkv):
    n_ex = mem.shape[0]

    def body(mem_ref, g_ref, w_ref, out_ref):
        m = mem_ref[0]
        r = lax.rsqrt(jnp.mean(m * m, axis=-1, keepdims=True) + EPS)
        out_ref[0] = _mm((m * r * g_ref[...]).astype(BF16), w_ref[...])

    return pl.pallas_call(
        body, name="memkv_forward", grid=(n_ex,),
        out_shape=jax.ShapeDtypeStruct((n_ex, MEM_LEN, 2 * MEM_WIDTH), F32),
        in_specs=[pl.BlockSpec((1, MEM_LEN, D_MODEL), lambda b: (b, 0, 0)), _full_spec((1, D_MODEL)),
                  _full_spec((D_MODEL, 2 * MEM_WIDTH))],
        out_specs=pl.BlockSpec((1, MEM_LEN, 2 * MEM_WIDTH), lambda b: (b, 0, 0)),
    )(mem, g_mem, w_mkv)


PROJ_WIDTHS = (A_WIDTH, A_WIDTH, SWA_WIDTH, KV_WIDTH, KV_WIDTH, MEM_WIDTH, MIX_WIDTH)
PROJ_OFFSETS = tuple(int(v) for v in np.cumsum((0,) + PROJ_WIDTHS))


HALF_WIDTH = IN_WIDTH // 2
HALF_PARTS = ((0, 1, 2, 3), (4, 5, 6))


def _gather_and_project(x2, g_pre, w_in_s, w_mkv_s, w_out_s, x_arr):
    n_tok = x2.shape[0]
    n_tiles = n_tok // PROJ_TILE
    last = n_tiles - 1
    shapes = [w_in_s.shape, w_mkv_s.shape, w_out_s.shape]
    n_w = len(shapes)

    def body(x_sref, x_ref, g_ref, win_hbm, wmkv_hbm, wout_hbm, h_ref, *refs):
        part_refs, refs = refs[:len(PROJ_WIDTHS)], refs[len(PROJ_WIDTHS):]
        gin_hbm, gmkv_hbm, gout_hbm, wg, stage_in, stage_mkv, stage_out, own_mkv, own_out = refs[:9]
        send_sems, recv_sems, local_sems = refs[9:]
        p, t = pl.program_id(0), pl.program_id(1)
        x, y, c = lax.axis_index("x"), lax.axis_index("y"), lax.axis_index("c")
        me, sibling = (x, y, c), (x, y, 1 - c)
        my_shard = 2 * x + y
        gathered = [wg, gmkv_hbm, gout_hbm]

        def half_rows(w, shard, half):
            rows = shapes[w][0] // 2
            if w == 0:
                return wg.at[pl.ds(pl.multiple_of(shard * shapes[0][0] + half * rows, 16), rows), :]
            return gathered[w].at[shard, pl.ds(half * rows, rows), :]

        def first(w, rel):
            src = half_rows(w, my_shard, c) if w == 0 else (own_mkv, own_out)[w - 1].at[
                pl.ds(c * (shapes[w][0] // 2), shapes[w][0] // 2), :]
            k = 3 * w + rel - 1
            return pltpu.make_async_remote_copy(
                src_ref=src, dst_ref=half_rows(w, my_shard, c), send_sem=send_sems.at[k], recv_sem=recv_sems.at[k],
                device_id=(x ^ (rel >> 1), y ^ (rel & 1), c), device_id_type=MESH)

        def landed(w, rel):
            k = 3 * w + rel - 1
            ref = half_rows(w, my_shard ^ rel, c)
            return pltpu.make_async_remote_copy(src_ref=ref, dst_ref=ref, send_sem=send_sems.at[k],
                                                recv_sem=recv_sems.at[k], device_id=me, device_id_type=MESH)

        def passed(w, rel, half, to):
            k = 9 + 3 * w + rel - 1
            ref = half_rows(w, my_shard ^ rel, half)
            return pltpu.make_async_remote_copy(src_ref=ref, dst_ref=ref, send_sem=send_sems.at[k],
                                                recv_sem=recv_sems.at[k], device_id=to, device_id_type=MESH)

        def pass_on(w, rels):
            for rel in rels:
                landed(w, rel).wait_recv()
                passed(w, rel, c, sibling).start()
            for rel in rels:
                passed(w, rel, 1 - c, me).wait_recv()

        own_stores = [pltpu.make_async_copy(own_mkv, gmkv_hbm.at[my_shard], local_sems.at[3]),
                      pltpu.make_async_copy(own_out, gout_hbm.at[my_shard], local_sems.at[4])]

        @pl.when((p == 0) & (t == 0))
        def _():
            loads = [pltpu.make_async_copy(src, dst, local_sems.at[k]) for k, (src, dst) in enumerate(
                ((win_hbm, stage_in), (wmkv_hbm, stage_mkv), (wout_hbm, stage_out)))]
            for cp in loads:
                cp.start()
            loads[0].wait()
            wg[pl.ds(pl.multiple_of(my_shard * shapes[0][0], 16), shapes[0][0]), :] = stage_in[...].astype(BF16)
            for rel in (1, 2):
                first(0, rel).start()
            loads[1].wait()
            loads[2].wait()
            own_mkv[...] = stage_mkv[...].astype(BF16)
            own_out[...] = stage_out[...].astype(BF16)
            for cp in own_stores:
                cp.start()
            pass_on(0, (1,))
            first(0, 3).start()

        @pl.when((p == 1) & (t == 0))
        def _():
            pass_on(0, (2, 3))
            for w in (1, 2):
                for rel in (1, 2, 3):
                    first(w, rel).start()

        xv = x_ref[...]
        r = lax.rsqrt(jnp.mean(xv * xv, axis=-1, keepdims=True) + EPS)
        h = (xv * r * g_ref[...]).astype(BF16)

        @pl.when(p == 0)
        def _():
            h_ref[...] = h

        for hh in range(2):
            @pl.when((p ^ x_sref[0]) == hh)
            def _():
                proj = _mm_nt(h, wg[hh * HALF_WIDTH:(hh + 1) * HALF_WIDTH, :])
                for k in HALF_PARTS[hh]:
                    lo = PROJ_OFFSETS[k] - hh * HALF_WIDTH
                    part_refs[k][...] = proj[:, lo:lo + PROJ_WIDTHS[k]]

        @pl.when((p == 1) & (t == last))
        def _():
            store = pltpu.make_async_copy(wg, gin_hbm, local_sems.at[5])
            store.start()
            for w in (1, 2):
                pass_on(w, (1, 2, 3))
            for w in range(n_w):
                for rel in (1, 2, 3):
                    first(w, rel).wait_send()
                    passed(w, rel, c, sibling).wait_send()
            for cp in own_stores:
                cp.wait()
            store.wait()

    def active_in(hh):
        def index(p, t, xs):
            return (jnp.where((p ^ xs[0]) == hh, t, jnp.where(p == 0, 0, last)), 0)
        return index

    part_specs = [pl.BlockSpec((PROJ_TILE, PROJ_WIDTHS[k]), active_in(hh)) for hh in range(2) for k in HALF_PARTS[hh]]
    vmem = pltpu.VMEM
    out = pl.pallas_call(
        body, name="gather_and_project",
        out_shape=[jax.ShapeDtypeStruct((n_tok, D_MODEL), BF16)]
        + [jax.ShapeDtypeStruct((n_tok, w), F32) for w in PROJ_WIDTHS]
        + [jax.ShapeDtypeStruct((N_CHIPS * shapes[0][0], shapes[0][1]), BF16)]
        + [jax.ShapeDtypeStruct((N_CHIPS,) + s, BF16) for s in shapes[1:]],
        grid_spec=pltpu.PrefetchScalarGridSpec(
            num_scalar_prefetch=1, grid=(2, n_tiles),
            in_specs=[pl.BlockSpec((PROJ_TILE, D_MODEL), lambda p, t, xs: (t, 0)),
                      pl.BlockSpec((1, D_MODEL), lambda p, t, xs: (0, 0)), ANY_SPEC, ANY_SPEC, ANY_SPEC],
            out_specs=[pl.BlockSpec((PROJ_TILE, D_MODEL), lambda p, t, xs: (jnp.where(p == 0, t, last), 0))]
            + part_specs + [ANY_SPEC] * 3,
            scratch_shapes=[vmem((N_CHIPS * shapes[0][0], shapes[0][1]), BF16), vmem(shapes[0], F32),
                            vmem(shapes[1], F32), vmem(shapes[2], F32), vmem(shapes[1], BF16), vmem(shapes[2], BF16),
                            pltpu.SemaphoreType.DMA((18,)), pltpu.SemaphoreType.DMA((18,)),
                            pltpu.SemaphoreType.DMA((6,))]),
        compiler_params=pltpu.CompilerParams(vmem_limit_bytes=VMEM_LIMIT),
    )(x_arr, x2, g_pre, w_in_s, w_mkv_s, w_out_s)
    h, parts, weights = out[0], out[1:1 + len(PROJ_WIDTHS)], out[1 + len(PROJ_WIDTHS):]
    return h, list(parts), weights


def _load_chunk(j, i, sk_ref, sv_ref, skp_ref, svp_ref):
    rows = slice(j * CHUNK, (j + 1) * CHUNK)
    if j == 0:
        k_prev, v_prev, table = skp_ref[...], svp_ref[...], jnp.where(i > 0, 0, 1)
    else:
        prev = slice((j - 1) * CHUNK, j * CHUNK)
        k_prev, v_prev, table = sk_ref[prev, :], sv_ref[prev, :], 0
    k_pairs = _pair_operands(_swa_variants(jnp.concatenate([k_prev, sk_ref[rows, :]], axis=0)))
    v_pairs = _pair_operands(_swa_variants(jnp.concatenate([v_prev, sv_ref[rows, :]], axis=0)))
    return rows, k_pairs, v_pairs, table


def _tile_constants(ws_ref, bs_ref, sink_ref, mkv_ref):
    wm = _causal_weights(ws_ref)
    bs_rows = [jnp.concatenate([bs_ref[g]] * TILE_CHUNKS, axis=0) for g in range(A_GROUPS)]
    sink_col = jnp.max(jnp.concatenate([jnp.full((CHUNK, 128), sink_ref[0, h], F32) for h in range(4)], axis=0),
                       axis=-1, keepdims=True)
    mkv_v = mkv_ref[0]
    mk_pairs = _pair_operands(_mem_variants(mkv_v[:, :MEM_WIDTH]))
    mv_pairs = _pair_operands(_mem_variants(mkv_v[:, MEM_WIDTH:]))
    return wm, bs_rows, sink_col, mk_pairs, mv_pairs


def _mix(parts, mkv, x2, tgt2, v_g, v_b, w_sp, b_sp, sinks, bias, w_out, g_post, n_ex, seq):
    n_tiles_ex = seq // TILE
    n_tok = n_ex * seq
    au, av, sq, sk, sv, mq, z = parts
    col = dict(zip(("au", "av", "sq", "sk", "sv", "mq", "z"),
                   (slice(PROJ_OFFSETS[k], PROJ_OFFSETS[k + 1]) for k in range(len(PROJ_WIDTHS)))))
    before_kv, after_kv = slice(0, col["sk"].start), slice(col["sv"].stop, IN_WIDTH)

    def body(au_ref, av_ref, sq_ref, sk_ref, sv_ref, skp_ref, svp_ref, mq_ref, z_ref, mkv_ref, x_ref, tgt_ref,
             vg_ref, vb_ref, ws_ref, bs_ref, sink_ref, bias_ref, wout_ref, gpost_ref,
             dout_ref, dproj_ref, dmkv_ref, dwout_ref, dvg_ref, dvb_ref, dws_ref, dbs_ref, dsink_ref, drel_ref,
             loss_ref, dgpost_ref, carry_dp, carry_k, carry_v):
        b, i = pl.program_id(0), pl.program_id(1)

        @pl.when((b == 0) & (i == 0))
        def _():
            for ref in (dwout_ref, dvg_ref, dvb_ref, dws_ref, dbs_ref, dsink_ref, drel_ref, loss_ref, dgpost_ref):
                ref[...] = jnp.zeros_like(ref)

        @pl.when(i == 0)
        def _():
            dmkv_ref[...] = jnp.zeros_like(dmkv_ref)
            carry_k[...] = jnp.zeros_like(carry_k)
            carry_v[...] = jnp.zeros_like(carry_v)

        @pl.when(i > 0)
        def _():
            dproj_ref[:, before_kv] = carry_dp[:, before_kv]
            dproj_ref[:, after_kv] = carry_dp[:, after_kv]

        @pl.when(i < n_tiles_ex)
        def _():
            wm, bs_rows, sink_col, mk_pairs, mv_pairs = _tile_constants(ws_ref, bs_ref, sink_ref, mkv_ref)
            vg = vg_ref[...]

            au_v, av_v = au_ref[...], av_ref[...]
            ya, res = _group_a_forward(au_v, av_v, vg, vb_ref[...], wm, bs_rows)
            swa, yb = [], []
            for j in range(TILE_CHUNKS):
                rows, k_pairs, v_pairs, table = _load_chunk(j, i, sk_ref, sv_ref, skp_ref, svp_ref)
                qp = _halves_bf16(sq_ref[rows, :])
                p, ps = _attention_probs(qp, k_pairs, bias_ref[table], sink_col)
                out, pp = _attention_out(p, v_pairs, CHUNK)
                yb.append(out)
                swa.append((rows, k_pairs, v_pairs, qp, p, ps, pp))
            mqp = _halves_bf16(mq_ref[...])
            pm, _ = _attention_probs(mqp, mk_pairs, None, None)
            yc, ppm = _attention_out(pm, mv_pairs, TILE)
            ycat = jnp.concatenate(ya + [jnp.concatenate(yb, axis=0), yc], axis=-1)

            zv = z_ref[...]
            sig = _sigmoid(zv)
            sz = zv * sig
            y_b = (ycat * sz).astype(BF16)
            o = _mm(y_b, wout_ref[...])
            r2 = lax.rsqrt(jnp.mean(o * o, axis=-1, keepdims=True) + EPS)
            nrm = o * r2
            gp = gpost_ref[...]
            diff = x_ref[...] + nrm * gp - tgt_ref[...]
            loss_ref[...] += jnp.sum(diff * diff) * (0.5 / D_MODEL)
            dout = diff * (1.0 / D_MODEL)
            dout_ref[...] = dout
            dgpost_ref[...] += jnp.sum(dout * nrm, axis=0, keepdims=True)
            dn = dout * gp
            do_b = (r2 * (dn - nrm * jnp.mean(dn * nrm, axis=-1, keepdims=True))).astype(BF16)
            dwout_ref[...] += _mm_tn(y_b, do_b)
            dy = _mm_nt(do_b, wout_ref[...])
            carry_dp[:, col["z"]] = (dy * ycat * (sig * (1.0 + zv * (1.0 - sig)))).astype(BF16)
            dyc = dy * sz

            dgu, dgv = [], []
            for g in range(A_GROUPS):
                sl = slice(g * 128, (g + 1) * 128)
                xhat, rstd, vn, s = res["groups"][g]
                dya = dyc[:, sl]
                dgu.append(dya * s)
                ds = dya * res["gu"][:, sl]
                dbs_ref[:, sl] += sum(ds[c * CHUNK:(c + 1) * CHUNK] for c in range(TILE_CHUNKS))
                ds_b = _rows_to_lanes(ds.astype(BF16), TILE_CHUNKS)
                dws_ref[g] += _mm_nt(ds_b, vn)
                dvn = _lanes_to_rows(_mm_tn(wm[g], ds_b), TILE_CHUNKS)
                dvg_ref[:, sl] += jnp.sum(dvn * xhat, axis=0, keepdims=True)
                dvb_ref[:, sl] += jnp.sum(dvn, axis=0, keepdims=True)
                dxh = dvn * vg[:, sl]
                dgv.append(rstd * (dxh - jnp.mean(dxh, axis=-1, keepdims=True)
                                   - xhat * jnp.mean(dxh * xhat, axis=-1, keepdims=True)))
            carry_dp[:, col["au"]] = (jnp.concatenate(dgu, axis=-1) * _gelu_grad(au_v, res["tu"])).astype(BF16)
            carry_dp[:, col["av"]] = (jnp.concatenate(dgv, axis=-1) * _gelu_grad(av_v, res["tv"])).astype(BF16)

            lane4 = lax.broadcasted_iota(jnp.int32, (1, 128), 1)
            dsink_vec = jnp.zeros((1, 128), F32)
            dk_parts, dv_parts = [], []
            for rows, k_pairs, v_pairs, qp, p, ps, pp in swa:
                do_pairs = _halves_bf16(dyc[rows, A_WIDTH:A_WIDTH + SWA_WIDTH])
                dl, delta, dq, dk, dv = _attention_backward(p, pp, do_pairs, qp, k_pairs, v_pairs, CHUNK)
                sink_terms = ps * delta
                for h in range(4):
                    dsink_vec = dsink_vec + jnp.where(lane4 == h, -jnp.sum(sink_terms[h * CHUNK:(h + 1) * CHUNK]), 0.0)
                drel_ref[...] += dl
                carry_dp[rows, col["sq"]] = (dq * QK_SCALE).astype(BF16)
                dk_parts.append(_swa_unvariants(*_split_pair_grads(dk)) * QK_SCALE)
                dv_parts.append(_swa_unvariants(*_split_pair_grads(dv)))
            dsink_ref[...] += dsink_vec

            dc_pairs = _halves_bf16(dyc[:, A_WIDTH + SWA_WIDTH:])
            _, _, dmq, dmk, dmv = _attention_backward(pm, ppm, dc_pairs, mqp, mk_pairs, mv_pairs, TILE)
            carry_dp[:, col["mq"]] = (dmq * QK_SCALE).astype(BF16)
            dmkv_ref[0] += jnp.concatenate([_mem_unvariants(*_split_pair_grads(dmk)) * QK_SCALE,
                                            _mem_unvariants(*_split_pair_grads(dmv))], axis=-1)

            for parts_c, carry, cols in ((dk_parts, carry_k, col["sk"]), (dv_parts, carry_v, col["sv"])):
                @pl.when(i > 0)
                def _():
                    dproj_ref[:, cols] = (carry[...] + jnp.concatenate(
                        [jnp.zeros((TILE - CHUNK, KV_WIDTH), F32), parts_c[0][:CHUNK]], axis=0)).astype(BF16)
                new = [parts_c[0][CHUNK:]]
                for j in range(1, TILE_CHUNKS):
                    new[-1] = new[-1] + parts_c[j][:CHUNK]
                    new.append(parts_c[j][CHUNK:])
                carry[...] = jnp.concatenate(new, axis=0)

        @pl.when(i == n_tiles_ex)
        def _():
            dproj_ref[:, col["sk"]] = carry_k[...].astype(BF16)
            dproj_ref[:, col["sv"]] = carry_v[...].astype(BF16)

    tile = functools.partial(_tile_specs, n_tiles_ex)
    prev = functools.partial(_prev_chunk_spec, n_tiles_ex)
    late = pl.BlockSpec((TILE, IN_WIDTH), lambda b, i: (b * n_tiles_ex + jnp.maximum(i - 1, 0), 0))
    return pl.pallas_call(
        body, name="mix", grid=(n_ex, n_tiles_ex + 1),
        out_shape=[jax.ShapeDtypeStruct((n_tok, D_MODEL), F32), jax.ShapeDtypeStruct((n_tok, IN_WIDTH), BF16),
                   jax.ShapeDtypeStruct((n_ex, MEM_LEN, 2 * MEM_WIDTH), F32),
                   jax.ShapeDtypeStruct((MIX_WIDTH, D_MODEL), F32), jax.ShapeDtypeStruct((1, A_WIDTH), F32),
                   jax.ShapeDtypeStruct((1, A_WIDTH), F32), jax.ShapeDtypeStruct((A_GROUPS, CHUNK, CHUNK), F32),
                   jax.ShapeDtypeStruct((CHUNK, A_WIDTH), F32), jax.ShapeDtypeStruct((1, 128), F32),
                   jax.ShapeDtypeStruct((4 * CHUNK, 2 * CHUNK), F32), jax.ShapeDtypeStruct((1, 128), F32),
                   jax.ShapeDtypeStruct((1, D_MODEL), F32)],
        in_specs=[tile(A_WIDTH), tile(A_WIDTH), tile(SWA_WIDTH), tile(KV_WIDTH), tile(KV_WIDTH),
                  prev(KV_WIDTH), prev(KV_WIDTH), tile(MEM_WIDTH), tile(MIX_WIDTH),
                  pl.BlockSpec((1, MEM_LEN, 2 * MEM_WIDTH), lambda b, i: (b, 0, 0)),
                  tile(D_MODEL), tile(D_MODEL),
                  _full_spec((1, A_WIDTH)), _full_spec((1, A_WIDTH)), _full_spec((A_GROUPS, CHUNK, CHUNK)),
                  _full_spec((A_GROUPS, CHUNK, CHUNK)), SMEM_SPEC, _full_spec((2, 4 * CHUNK, 2 * CHUNK)),
                  _full_spec((MIX_WIDTH, D_MODEL)), _full_spec((1, D_MODEL))],
        out_specs=[tile(D_MODEL), late, pl.BlockSpec((1, MEM_LEN, 2 * MEM_WIDTH), lambda b, i: (b, 0, 0)),
                   _full_spec((MIX_WIDTH, D_MODEL)), _full_spec((1, A_WIDTH)), _full_spec((1, A_WIDTH)),
                   _full_spec((A_GROUPS, CHUNK, CHUNK)), _full_spec((CHUNK, A_WIDTH)), _full_spec((1, 128)),
                   _full_spec((4 * CHUNK, 2 * CHUNK)), _full_spec((1, 128)), _full_spec((1, D_MODEL))],
        scratch_shapes=[pltpu.VMEM((TILE, IN_WIDTH), BF16), pltpu.VMEM((TILE, KV_WIDTH), F32),
                        pltpu.VMEM((TILE, KV_WIDTH), F32)],
        compiler_params=pltpu.CompilerParams(vmem_limit_bytes=VMEM_LIMIT),
    )(au, av, sq, sk, sv, sk, sv, mq, z, mkv, x2, tgt2, v_g, v_b, w_sp, b_sp, sinks, bias, w_out, g_post)


BWD_PROJ_TILE = 512


def _backward_projection(x2, dout, dproj, g_pre, w_in_t):
    n_tok = x2.shape[0]
    n_steps = n_tok // BWD_PROJ_TILE

    def body(x_ref, dout_ref, dp_ref, g_ref, w_hbm, dx_ref, dgpre_ref, w_vmem, sem):
        @pl.when(pl.program_id(0) == 0)
        def _():
            load = pltpu.make_async_copy(w_hbm, w_vmem, sem)
            load.start()
            dgpre_ref[...] = jnp.zeros_like(dgpre_ref)
            load.wait()

        xv = x_ref[...]
        r = lax.rsqrt(jnp.mean(xv * xv, axis=-1, keepdims=True) + EPS)
        xn = xv * r
        dh = _mm(dp_ref[...], w_vmem[...])
        dgpre_ref[...] += jnp.sum(dh * xn, axis=0, keepdims=True)
        dhg = dh * g_ref[...]
        dx_ref[...] = r * (dhg - xn * jnp.mean(dhg * xn, axis=-1, keepdims=True)) + dout_ref[...]

    row = lambda w: pl.BlockSpec((BWD_PROJ_TILE, w), lambda i: (i, 0))
    return pl.pallas_call(
        body, name="backward_projection", grid=(n_steps,),
        out_shape=[jax.ShapeDtypeStruct((n_tok, D_MODEL), F32), jax.ShapeDtypeStruct((1, D_MODEL), F32)],
        in_specs=[row(D_MODEL), row(D_MODEL), row(IN_WIDTH), _full_spec((1, D_MODEL)), ANY_SPEC],
        out_specs=[row(D_MODEL), _full_spec((1, D_MODEL))],
        scratch_shapes=[pltpu.VMEM((IN_WIDTH, D_MODEL), BF16), pltpu.SemaphoreType.DMA],
        input_output_aliases={1: 0},
        compiler_params=pltpu.CompilerParams(vmem_limit_bytes=VMEM_LIMIT),
    )(x2, dout, dproj, g_pre, w_in_t)


SHARD_ROWS = IN_WIDTH // N_CHIPS
SHARD_WINDOW = 768
SHARD_HALF = SHARD_ROWS // 2
DWIN_TILE = 2048
N_REL = N_CHIPS - 1


def _shard_window_start(shard):
    return (shard * SHARD_ROWS // 128) * 128


def _reduce_gradients(dproj, h, big, small, shard_arr):
    n_tok = h.shape[0]
    tile = min(DWIN_TILE, n_tok)
    n_sub = n_tok // tile
    last = N_CHIPS - 1
    n_big, n_small = len(big), len(small)
    big_half = [g.shape[2:] for g in big]
    sem_big_d2d = 2 * N_CHIPS
    sem_big_ici = sem_big_d2d + n_big
    sem_big_swap = sem_big_ici + N_REL * n_big
    sem_small_d2d = sem_big_swap + n_big
    sem_small_ici = sem_small_d2d + n_small
    n_sems = sem_small_ici + N_REL * n_small
    loc_small = n_big
    loc_out_win = loc_small + n_small
    loc_out_big = loc_out_win + 2
    loc_out_small = loc_out_big + 2 * n_big
    n_local = loc_out_small + n_small

    def shard_of_slot(s, my_shard):
        return my_shard ^ ((s + 1) % N_CHIPS)

    def body(shard_ref, dp_ref, h_hbm, *refs):
        h_vmem, h_sem, refs = refs[-2], refs[-1], refs[:-2]
        big_hbm, refs = refs[:n_big], refs[n_big:]
        small_hbm, refs = refs[:n_small], refs[n_small:]
        out_hbm, refs = refs[0], refs[1:]
        big_out, refs = refs[:n_big], refs[n_big:]
        small_out, refs = refs[:n_small], refs[n_small:]
        part, recv_d2d, send_ici, recv_ici, mine_buf, other_buf = refs[:6]
        refs = refs[6:]
        big_own, big_recv, big_send, big_land, big_mine, big_other = (
            refs[k * n_big:(k + 1) * n_big] for k in range(6))
        refs = refs[6 * n_big:]
        small_own, small_recv, small_all = (refs[k * n_small:(k + 1) * n_small] for k in range(3))
        send_sems, recv_sems, local_sems = refs[3 * n_small:]

        s, t = pl.program_id(0), pl.program_id(1)
        x, y, c = lax.axis_index("x"), lax.axis_index("y"), lax.axis_index("c")
        my_chip = 2 * x + y
        sibling = (x, y, 1 - c)
        my_rows = pl.ds(pl.multiple_of(c * SHARD_HALF, 8), SHARD_HALF)
        other_rows = pl.ds(pl.multiple_of((1 - c) * SHARD_HALF, 8), SHARD_HALF)

        def remote(src, dst, k, to):
            return pltpu.make_async_remote_copy(src_ref=src, dst_ref=dst, send_sem=send_sems.at[k],
                                                recv_sem=recv_sems.at[k], device_id=to, device_id_type=MESH)

        def chip_at(rel):
            return (x ^ (rel >> 1), y ^ (rel & 1), c)

        def to_sibling(k):
            return remote(part.at[k % 2, other_rows, :], recv_d2d.at[k], k, sibling)

        def to_chip(k):
            return remote(send_ici.at[k], recv_ici.at[k], N_CHIPS + k, chip_at(k + 1))

        swap = remote(mine_buf, other_buf, 2 * N_CHIPS - 1, sibling)
        big_load = [pltpu.make_async_copy(big_hbm[w].at[:, pl.ds(c, 1)], big_own[w], local_sems.at[w])
                    for w in range(n_big)]
        big_to_sibling = [remote(big_hbm[w].at[:, pl.ds(1 - c, 1)], big_recv[w], sem_big_d2d + w, sibling)
                          for w in range(n_big)]
        big_to_chip = [[remote(big_send[w].at[k], big_land[w].at[k], sem_big_ici + N_REL * w + k, chip_at(k + 1))
                        for k in range(N_REL)] for w in range(n_big)]
        big_swap = [remote(big_mine[w], big_other[w], sem_big_swap + w, sibling) for w in range(n_big)]
        small_load = [pltpu.make_async_copy(small_hbm[i], small_own[i], local_sems.at[loc_small + i])
                      for i in range(n_small)]
        small_to_sibling = [remote(small_hbm[i], small_recv[i], sem_small_d2d + i, sibling) for i in range(n_small)]
        small_to_chip = [[remote(small_all[i].at[my_chip], small_all[i].at[my_chip],
                                 sem_small_ici + N_REL * i + k, chip_at(k + 1))
                          for k in range(N_REL)] for i in range(n_small)]

        @pl.when((s == 0) & (t == 0))
        def _():
            h_load = pltpu.make_async_copy(h_hbm, h_vmem, h_sem)
            h_load.start()
            for cp in big_load + big_to_sibling + small_load + small_to_sibling:
                cp.start()
            h_load.wait()

        @pl.when((s == 0) & (t == n_sub - 1))
        def _():
            for cp in big_load + small_load:
                cp.wait()
            for cp in big_to_sibling + small_to_sibling:
                cp.wait_recv()
                cp.wait_send()
            for w in range(n_big):
                for k in range(N_REL):
                    shard = my_chip ^ (k + 1)
                    big_send[w][k] = (big_own[w][shard, 0] + big_recv[w][shard, 0]).astype(BF16)
                    big_to_chip[w][k].start()
            for i in range(n_small):
                small_all[i][my_chip] = small_own[i][...] + small_recv[i][...]
                for k in range(N_REL):
                    small_to_chip[i][k].start()

        @pl.when((s > 0) & (t == 0))
        def _():
            k = s - 1
            cp = to_sibling(k)
            cp.wait_recv()
            cp.wait_send()
            send_ici[k] = (part[k % 2, my_rows, :] + recv_d2d[k]).astype(BF16)
            to_chip(k).start()

        r = _mm_tn(dp_ref[...], h_vmem[pl.ds(pl.multiple_of(t * tile, tile), tile), :])
        odd = shard_of_slot(s, shard_ref[0]) % 2
        for parity in range(2):
            rows = r[64 * parity:64 * parity + SHARD_ROWS]

            @pl.when((odd == parity) & (t == 0))
            def _():
                part[s % 2] = rows

            @pl.when((odd == parity) & (t > 0))
            def _():
                part[s % 2] += rows

        @pl.when(t == n_sub - 1)
        def _():
            to_sibling(s).start()

        @pl.when((s == last) & (t == n_sub - 1))
        def _():
            cp = to_sibling(last)
            cp.wait_recv()
            cp.wait_send()
            total = part[last % 2, my_rows, :] + recv_d2d[last]
            for k in range(last):
                to_chip(k).wait_recv()
                total = total + recv_ici[k].astype(F32)
            mine_buf[...] = total
            swap.start()
            out_mine = pltpu.make_async_copy(mine_buf, out_hbm.at[my_rows, :], local_sems.at[0])
            out_mine.start()
            swap.wait_recv()
            out_other = pltpu.make_async_copy(other_buf, out_hbm.at[other_rows, :], local_sems.at[1])
            out_other.start()
            stores = [out_mine, out_other]
            for w in range(n_big):
                rows = big_half[w][0]
                total = big_own[w][my_chip, 0] + big_recv[w][my_chip, 0]
                for k in range(N_REL):
                    big_to_chip[w][k].wait_recv()
                    total = total + big_land[w][k].astype(F32)
                big_mine[w][...] = total
                big_swap[w].start()
                stores.append(pltpu.make_async_copy(
                    big_mine[w], big_out[w].at[pl.ds(pl.multiple_of(c * rows, 8), rows), :],
                    local_sems.at[loc_out_big + 2 * w]))
                stores[-1].start()
            for w in range(n_big):
                rows = big_half[w][0]
                big_swap[w].wait_recv()
                stores.append(pltpu.make_async_copy(
                    big_other[w], big_out[w].at[pl.ds(pl.multiple_of((1 - c) * rows, 8), rows), :],
                    local_sems.at[loc_out_big + 2 * w + 1]))
                stores[-1].start()
            for i in range(n_small):
                for k in range(N_REL):
                    small_to_chip[i][k].wait_recv()
                stores.append(pltpu.make_async_copy(small_all[i], small_out[i], local_sems.at[loc_out_small + i]))
                stores[-1].start()
            for k in range(last):
                to_chip(k).wait_send()
            swap.wait_send()
            for w in range(n_big):
                for k in range(N_REL):
                    big_to_chip[w][k].wait_send()
                big_swap[w].wait_send()
            for i in range(n_small):
                for k in range(N_REL):
                    small_to_chip[i][k].wait_send()
            for cp in stores:
                cp.wait()

    half = (SHARD_HALF, D_MODEL)
    vmem = pltpu.VMEM
    scratch = [vmem((2, SHARD_ROWS, D_MODEL), F32), vmem((N_CHIPS,) + half, F32),
               vmem((N_REL,) + half, BF16), vmem((N_REL,) + half, BF16), vmem(half, F32), vmem(half, F32)]
    scratch += [vmem((N_CHIPS, 1) + hs, F32) for hs in big_half] * 2
    scratch += [vmem((N_REL,) + hs, BF16) for hs in big_half] * 2
    scratch += [vmem(hs, F32) for hs in big_half] * 2
    scratch += [vmem(a.shape, F32) for a in small] * 2 + [vmem((N_CHIPS,) + a.shape, F32) for a in small]
    scratch += [pltpu.SemaphoreType.DMA((n_sems,)), pltpu.SemaphoreType.DMA((n_sems,)),
                pltpu.SemaphoreType.DMA((n_local,)), vmem(h.shape, BF16), pltpu.SemaphoreType.DMA]
    n_hbm = n_big + n_small
    out = pl.pallas_call(
        body, name="reduce_gradients",
        out_shape=[jax.ShapeDtypeStruct((SHARD_ROWS, D_MODEL), F32)]
        + [jax.ShapeDtypeStruct((2 * hs[0], hs[1]), F32) for hs in big_half]
        + [jax.ShapeDtypeStruct((N_CHIPS,) + a.shape, F32) for a in small],
        grid_spec=pltpu.PrefetchScalarGridSpec(
            num_scalar_prefetch=1, grid=(N_CHIPS, n_sub),
            in_specs=[pl.BlockSpec((pl.Element(tile), pl.Element(SHARD_WINDOW)),
                                   lambda s, t, m: (t * tile, _shard_window_start(shard_of_slot(s, m[0])))),
                      ANY_SPEC] + [ANY_SPEC] * n_hbm,
            out_specs=[ANY_SPEC] * (1 + n_hbm),
            scratch_shapes=scratch),
        compiler_params=pltpu.CompilerParams(vmem_limit_bytes=VMEM_LIMIT),
    )(shard_arr, dproj, h, *big, *small)
    return out[:1 + n_big], out[1 + n_big:]


def _memkv_backward(mem, dmkv, g_mem, w_mkv):
    n_ex = mem.shape[0]

    def body(mem_ref, d_ref, g_ref, w_ref, dw_ref, dg_ref):
        @pl.when(pl.program_id(0) == 0)
        def _():
            dw_ref[...] = jnp.zeros_like(dw_ref)
            dg_ref[...] = jnp.zeros_like(dg_ref)

        m = mem_ref[0]
        mn = m * lax.rsqrt(jnp.mean(m * m, axis=-1, keepdims=True) + EPS)
        d_b = d_ref[0].astype(BF16)
        dw_ref[...] += _mm_tn((mn * g_ref[...]).astype(BF16), d_b)
        dg_ref[...] += jnp.sum(_mm_nt(d_b, w_ref[...]) * mn, axis=0, keepdims=True)

    return pl.pallas_call(
        body, name="memkv_backward", grid=(n_ex,),
        out_shape=[jax.ShapeDtypeStruct((D_MODEL, 2 * MEM_WIDTH), F32), jax.ShapeDtypeStruct((1, D_MODEL), F32)],
        in_specs=[pl.BlockSpec((1, MEM_LEN, D_MODEL), lambda b: (b, 0, 0)),
                  pl.BlockSpec((1, MEM_LEN, 2 * MEM_WIDTH), lambda b: (b, 0, 0)),
                  _full_spec((1, D_MODEL)), _full_spec((D_MODEL, 2 * MEM_WIDTH))],
        out_specs=[_full_spec((D_MODEL, 2 * MEM_WIDTH)), _full_spec((1, D_MODEL))],
    )(mem, dmkv, g_mem, w_mkv)


def _pack_small_grads(dgpre, dgpost, dgmem, dvg, dvb, dws, dbs, dsink, drel, loss_vec, buckets):
    def body(dgpre_ref, dgpost_ref, dgmem_ref, dvg_ref, dvb_ref, dws_ref, dbs_ref, dsink_ref, drel_ref, loss_ref,
             bk_ref, a_ref, b_ref):
        a_ref[...] = jnp.zeros_like(a_ref)
        b_ref[...] = jnp.zeros_like(b_ref)
        a_ref[0:1, :] = dgpre_ref[...]
        a_ref[1:2, :] = dgpost_ref[...]
        a_ref[2:3, :] = dgmem_ref[...]
        a_ref[3:4, :] = jnp.concatenate([dvg_ref[...], dvb_ref[...]], axis=-1)
        a_ref[ROW_LOSS:ROW_LOSS + 1, 0:128] = loss_ref[...]
        row = lax.broadcasted_iota(jnp.int32, (CHUNK, CHUNK), 0)
        col = lax.broadcasted_iota(jnp.int32, (CHUNK, CHUNK), 1)
        for g in range(A_GROUPS):
            b_ref[ROW_WS + g * CHUNK:ROW_WS + (g + 1) * CHUNK, :] = jnp.where(row >= col, dws_ref[g], 0.0)
            by_token = jnp.transpose(dbs_ref[:, g * 128:(g + 1) * 128])
            b_ref[ROW_BS + g:ROW_BS + g + 1, :] = jnp.sum(by_token, axis=0, keepdims=True)
        b_ref[ROW_SINK:ROW_SINK + 1, :] = dsink_ref[...]
        bk = bk_ref[...]
        rel_row = lax.broadcasted_iota(jnp.int32, (8, 128), 0)
        rel_col = lax.broadcasted_iota(jnp.int32, (8, 128), 1)
        rel = jnp.zeros((8, 128), F32)
        for h in range(4):
            acc = drel_ref[h * CHUNK:(h + 1) * CHUNK, :]
            for b in range(N_BUCKETS):
                rel = jnp.where((rel_row == h) & (rel_col == b), jnp.sum(jnp.where(bk == b, acc, 0.0)), rel)
        b_ref[ROW_REL:ROW_REL + 8, :] = rel

    return pl.pallas_call(
        body, name="pack_small_grads",
        out_shape=[jax.ShapeDtypeStruct((SMALL_A_ROWS, D_MODEL), F32), jax.ShapeDtypeStruct((SMALL_B_ROWS, 128), F32)],
        in_specs=[VMEM_SPEC] * 11, out_specs=[VMEM_SPEC] * 2,
    )(dgpre, dgpost, dgmem, dvg, dvb, dws, dbs, dsink, drel, loss_vec, buckets)


def _adamw(w, g, m, v):
    m2 = ADAM_B1 * m + (1.0 - ADAM_B1) * g
    v2 = ADAM_B2 * v + (1.0 - ADAM_B2) * (g * g)
    m_hat = m2 / (1.0 - ADAM_B1 ** ADAM_STEP)
    v_hat = v2 / (1.0 - ADAM_B2 ** ADAM_STEP)
    delta = -ADAM_LR * (m_hat / (jnp.sqrt(v_hat) + ADAM_EPS) + ADAM_WD * w)
    return delta, m2, v2


ADAM_MAX_ROWS = 176


def _adamw_whole(g, w, m, v, name):
    rows, cols = w.shape
    steps = -(-rows // ADAM_MAX_ROWS)
    block_rows = rows // steps
    assert block_rows * steps == rows and block_rows % 8 == 0

    def body(g_ref, w_ref, m_ref, v_ref, d_out, m_out, v_out):
        delta, m2, v2 = _adamw(w_ref[...], g_ref[...], m_ref[...], v_ref[...])
        d_out[...] = delta
        m_out[...] = m2
        v_out[...] = v2

    block = pl.BlockSpec((block_rows, cols), lambda k: (k, 0))
    out = pl.pallas_call(
        body, name=name, grid=(steps,), out_shape=[jax.ShapeDtypeStruct((rows, cols), F32)] * 3,
        in_specs=[block] * 4, out_specs=[block] * 3,
    )(g, w, m, v)
    return [g] + list(out)


def _adamw_small(ra, rb, weights, moments_m, moments_v):
    n = len(weights)

    def body(*refs):
        ra_ref, rb_ref = refs[0], refs[1]
        w_refs, m_refs, v_refs = refs[2:2 + n], refs[2 + n:2 + 2 * n], refs[2 + 2 * n:2 + 3 * n]
        outs = refs[2 + 3 * n:]
        g_outs, d_outs, m_outs, v_outs = outs[:n], outs[n:2 * n], outs[2 * n:3 * n], outs[3 * n:4 * n]
        ga, gb = ra_ref[0], rb_ref[0]
        for chip in range(1, N_CHIPS):
            ga = ga + ra_ref[chip]
            gb = gb + rb_ref[chip]
        outs[4 * n][...] = ga[ROW_LOSS:ROW_LOSS + 1, 0:128]
        grads = [ga[0:1, :], ga[1:2, :], ga[2:3, :], ga[3:4, :A_WIDTH], ga[3:4, A_WIDTH:],
                 gb[ROW_WS:ROW_WS + A_GROUPS * CHUNK, :].reshape(A_GROUPS, CHUNK, CHUNK),
                 gb[ROW_BS:ROW_BS + A_GROUPS, :], gb[ROW_SINK:ROW_SINK + 1, 0:4],
                 gb[ROW_REL:ROW_REL + 4, 0:N_BUCKETS]]
        for k in range(n):
            delta, m2, v2 = _adamw(w_refs[k][...], grads[k], m_refs[k][...], v_refs[k][...])
            g_outs[k][...] = grads[k]
            d_outs[k][...] = delta
            m_outs[k][...] = m2
            v_outs[k][...] = v2

    out_shape = [jax.ShapeDtypeStruct(w.shape, F32) for w in weights] * 4 + [jax.ShapeDtypeStruct((1, 128), F32)]
    return pl.pallas_call(
        body, name="adamw_small", out_shape=out_shape,
        in_specs=[VMEM_SPEC] * (2 + 3 * n), out_specs=[VMEM_SPEC] * (4 * n + 1),
    )(ra, rb, *weights, *moments_m, *moments_v)


def kernel(x, mem, pre_norm_g, post_norm_g, mem_norm_g, w_in, w_mem_kv, v_norm_g, v_norm_b, w_spatial, b_spatial, attn_sinks, rel_bias, w_out, loss_target, m_pre_norm_g, m_post_norm_g, m_mem_norm_g, m_w_in, m_w_mem_kv, m_v_norm_g, m_v_norm_b, m_w_spatial, m_b_spatial, m_attn_sinks, m_rel_bias, m_w_out, v_pre_norm_g, v_post_norm_g, v_mem_norm_g, v_w_in, v_w_mem_kv, v_v_norm_g, v_v_norm_b, v_w_spatial, v_b_spatial, v_attn_sinks, v_rel_bias, v_w_out):
    n_ex, seq, _ = x.shape
    n_tok = n_ex * seq
    x2 = x.reshape(n_tok, D_MODEL)
    tgt2 = loss_target.reshape(n_tok, D_MODEL)
    buckets = jnp.asarray(_bucket_map())
    shard_arr = (2 * lax.axis_index("x") + lax.axis_index("y")).astype(jnp.int32).reshape(1)
    w_sp = w_spatial[0]
    b_sp = jnp.broadcast_to(b_spatial[0][:, :, None], (A_GROUPS, CHUNK, CHUNK))
    w_in_t, m_w_in_t, v_w_in_t = (jnp.transpose(a[0]) for a in (w_in, m_w_in, v_w_in))
    rel_t, m_rel_t, v_rel_t = (jnp.transpose(a) for a in (rel_bias, m_rel_bias, v_rel_bias))

    x_arr = lax.axis_index("x").astype(jnp.int32).reshape(1)
    h_b, parts, (w_in_b, g_mkv, g_out) = _gather_and_project(x2, pre_norm_g, w_in_t, w_mem_kv[0], w_out[0], x_arr)
    w_mkv_b = g_mkv.reshape(D_MODEL, 2 * MEM_WIDTH)
    w_out_b = g_out.reshape(MIX_WIDTH, D_MODEL)

    bias = _make_bias(rel_t, buckets)
    mkv = _memkv_forward(mem, mem_norm_g, w_mkv_b)
    dout, dproj, dmkv, dwout, dvg, dvb, dws, dbs, dsink, drel, loss_vec, dgpost = _mix(
        parts, mkv, x2, tgt2, v_norm_g, v_norm_b, w_sp, b_sp, attn_sinks, bias, w_out_b, post_norm_g, n_ex, seq)

    dx, dgpre = _backward_projection(x2, dout, dproj, pre_norm_g, w_in_b)
    dwmkv, dgmem = _memkv_backward(mem, dmkv, mem_norm_g, w_mkv_b)
    small_a, small_b = _pack_small_grads(dgpre, dgpost, dgmem, dvg, dvb, dws, dbs, dsink, drel, loss_vec, buckets)

    shard_shapes = [w_mem_kv.shape[1:], w_out.shape[1:]]
    big = [g.reshape(N_CHIPS, 2, s[0] // 2, s[1]) for g, s in zip((dwmkv, dwout), shard_shapes)]
    (g_win, g_wmkv, g_wout), (ga, gb) = _reduce_gradients(dproj, h_b, big, [small_a, small_b], shard_arr)

    big_out = [_adamw_whole(g_win, w_in_t, m_w_in_t, v_w_in_t, "adamw_w_in"),
               _adamw_whole(g_wmkv, w_mem_kv[0], m_w_mem_kv[0], v_w_mem_kv[0], "adamw_w_mem_kv"),
               _adamw_whole(g_wout, w_out[0], m_w_out[0], v_w_out[0], "adamw_w_out")]
    small_w = [pre_norm_g, post_norm_g, mem_norm_g, v_norm_g, v_norm_b, w_sp, b_spatial[0], attn_sinks, rel_t]
    small_m = [m_pre_norm_g, m_post_norm_g, m_mem_norm_g, m_v_norm_g, m_v_norm_b, m_w_spatial[0], m_b_spatial[0],
               m_attn_sinks, m_rel_t]
    small_v = [v_pre_norm_g, v_post_norm_g, v_mem_norm_g, v_v_norm_g, v_v_norm_b, v_w_spatial[0], v_b_spatial[0],
               v_attn_sinks, v_rel_t]
    small_out = _adamw_small(ga, gb, small_w, small_m, small_v)
    n_small = len(small_w)

    outputs = [small_out[4 * n_small][0, 0], dx.reshape(x.shape)]
    for kind in range(4):
        s = small_out[kind * n_small:(kind + 1) * n_small]
        outputs += [s[0], s[1], s[2], jnp.transpose(big_out[0][kind])[None], big_out[1][kind][None], s[3], s[4],
                    s[5][None], s[6][None], s[7], jnp.transpose(s[8]), big_out[2][kind][None]]
    return tuple(outputs)
```

```python
import functools

import numpy as np
import jax
import jax.numpy as jnp
from jax import lax
from jax.experimental import pallas as pl
from jax.experimental.pallas import tpu as pltpu

F32 = jnp.float32
BF16 = jnp.bfloat16
MESH = pl.DeviceIdType.MESH

D_MODEL = 1024
CHUNK = 128
A_WIDTH = 512
A_GROUPS = 4
SWA_WIDTH = 256
KV_WIDTH = 128
MEM_WIDTH = 256
MEM_LEN = 256
MIX_WIDTH = 1024
IN_WIDTH = 2816
N_BUCKETS = 32
MAX_DISTANCE = 128
EPS = 1e-6
NEG = -1e30
QK_SCALE = 0.125
HALF_HEAD_PAIR = 64

ADAM_LR = 0.001
ADAM_B1 = 0.9
ADAM_B2 = 0.999
ADAM_EPS = 1e-08
ADAM_WD = 0.01
ADAM_STEP = 10

N_CHIPS = 4
TILE_CHUNKS = 2
TILE = TILE_CHUNKS * CHUNK
PROJ_TILE = 512
VMEM_LIMIT = 56 * 1024 * 1024

SMALL_A_ROWS = 8
ROW_LOSS = 4
ROW_WS = 0
ROW_BS = 512
ROW_SINK = 520
ROW_REL = 528
SMALL_B_ROWS = 536


def _mm(a, b):
    return lax.dot_general(a, b, (((1,), (0,)), ((), ())), preferred_element_type=F32)


def _mm_nt(a, b):
    return lax.dot_general(a, b, (((1,), (1,)), ((), ())), preferred_element_type=F32)


def _mm_tn(a, b):
    return lax.dot_general(a, b, (((0,), (0,)), ((), ())), preferred_element_type=F32)


def _bucket_map():
    qi = np.arange(CHUNK)[:, None]
    kj = np.arange(2 * CHUNK)[None, :]
    n = np.maximum(qi + CHUNK - kj, 0)
    max_exact = N_BUCKETS // 2
    large = max_exact + (np.log(np.maximum(n, 1) / max_exact) / np.log(MAX_DISTANCE / max_exact)
                         * (N_BUCKETS - max_exact)).astype(np.int32)
    large = np.minimum(large, N_BUCKETS - 1)
    return np.where(n < max_exact, n, large).astype(np.int32)


_GELU_C = 0.7978845608028654
_GELU_A = 0.044715


def _gelu(x):
    t = jnp.tanh(_GELU_C * (x + _GELU_A * x * x * x))
    return 0.5 * x * (1.0 + t), t


def _gelu_grad(x, t):
    return 0.5 * (1.0 + t) + 0.5 * x * (1.0 - t * t) * (_GELU_C * (1.0 + 3.0 * _GELU_A * x * x))


def _sigmoid(x):
    return 1.0 / (1.0 + jnp.exp(-x))


def _lane_lo(shape):
    return lax.broadcasted_iota(jnp.int32, shape, 1) < HALF_HEAD_PAIR


def _swa_variants(t):
    lo = _lane_lo(t.shape)
    tr = pltpu.roll(t, HALF_HEAD_PAIR, 1)
    zero = jnp.zeros_like(t)
    return (jnp.where(lo, t, zero).astype(BF16), jnp.where(lo, zero, tr).astype(BF16),
            jnp.where(lo, tr, zero).astype(BF16), jnp.where(lo, zero, t).astype(BF16))


def _swa_unvariants(d0, d1, d2, d3):
    lo = _lane_lo(d0.shape)
    zero = jnp.zeros_like(d0)
    rolled = jnp.where(lo, zero, d1) + jnp.where(lo, d2, zero)
    return jnp.where(lo, d0, zero) + jnp.where(lo, zero, d3) + pltpu.roll(rolled, HALF_HEAD_PAIR, 1)


def _mem_variants(t):
    out = []
    for pair in range(2):
        tp = t[:, pair * 128:(pair + 1) * 128]
        lo = _lane_lo(tp.shape)
        zero = jnp.zeros_like(tp)
        out.append(jnp.where(lo, tp, zero).astype(BF16))
        out.append(jnp.where(lo, zero, tp).astype(BF16))
    return out


def _mem_unvariants(d0, d1, d2, d3):
    lo = _lane_lo(d0.shape)
    return jnp.concatenate([jnp.where(lo, d0, d1), jnp.where(lo, d2, d3)], axis=-1)


def _softmax(logits, sinks):
    m = jnp.max(logits, axis=-1, keepdims=True)
    if sinks is not None:
        m = jnp.maximum(m, sinks)
    p = jnp.exp(logits - m)
    den = jnp.sum(p, axis=-1, keepdims=True)
    if sinks is None:
        return p * (1.0 / den), None
    es = jnp.exp(sinks - m)
    inv = 1.0 / (den + es)
    return p * inv, es * inv


def _band_valid(with_prev):
    qi = lax.broadcasted_iota(jnp.int32, (CHUNK, 2 * CHUNK), 0)
    kj = lax.broadcasted_iota(jnp.int32, (CHUNK, 2 * CHUNK), 1)
    in_cur = (kj >= CHUNK) & (kj - CHUNK <= qi)
    if not with_prev:
        return in_cur
    return in_cur | ((kj < CHUNK) & (kj > qi))


def _causal_weights(ws_ref):
    row = lax.broadcasted_iota(jnp.int32, (CHUNK, CHUNK), 0)
    col = lax.broadcasted_iota(jnp.int32, (CHUNK, CHUNK), 1)
    return [jnp.where(row >= col, ws_ref[g], 0.0).astype(BF16) for g in range(A_GROUPS)]


def _rows_to_lanes(a, n):
    return jnp.concatenate([a[c * CHUNK:(c + 1) * CHUNK] for c in range(n)], axis=1)


def _lanes_to_rows(a, n):
    w = a.shape[1] // n
    return jnp.concatenate([a[:, c * w:(c + 1) * w] for c in range(n)], axis=0)


def _stack_heads(pair01, pair23):
    return jnp.concatenate([pair01[:, :256], pair01[:, 256:], pair23[:, :256], pair23[:, 256:]], axis=0)


def _pair_heads(s, r):
    return (jnp.concatenate([s[0:r], s[r:2 * r]], axis=1), jnp.concatenate([s[2 * r:3 * r], s[3 * r:4 * r]], axis=1))


def _pair_operands(variants):
    return (jnp.concatenate(variants[0:2], axis=0), jnp.concatenate(variants[2:4], axis=0))


def _split_pair_grads(d_pairs):
    return d_pairs[0][:256], d_pairs[0][256:], d_pairs[1][:256], d_pairs[1][256:]


def _halves_bf16(a):
    return (a[:, :128].astype(BF16), a[:, 128:].astype(BF16))


def _group_a_forward(au, av, vg, vb, wm, bs_rows):
    gu, tu = _gelu(au)
    gv, tv = _gelu(av)
    ya, res = [], []
    for g in range(A_GROUPS):
        sl = slice(g * 128, (g + 1) * 128)
        xg = gv[:, sl]
        xc = xg - jnp.mean(xg, axis=-1, keepdims=True)
        rstd = lax.rsqrt(jnp.mean(xc * xc, axis=-1, keepdims=True) + EPS)
        xhat = xc * rstd
        vn = _rows_to_lanes((xhat * vg[:, sl] + vb[:, sl]).astype(BF16), TILE_CHUNKS)
        s = _lanes_to_rows(_mm(wm[g], vn), TILE_CHUNKS) + bs_rows[g]
        ya.append(gu[:, sl] * s)
        res.append((xhat, rstd, vn, s))
    return ya, dict(gu=gu, tu=tu, tv=tv, groups=res)


def _attention_probs(qp, k_pairs, bias, sink_col):
    logits = _stack_heads(_mm_nt(qp[0], k_pairs[0]), _mm_nt(qp[1], k_pairs[1])) * QK_SCALE
    if bias is not None:
        logits = logits + bias
    return _softmax(logits, sink_col)


def _attention_out(p, v_pairs, r):
    pp = _pair_heads(p.astype(BF16), r)
    return jnp.concatenate([_mm(pp[0], v_pairs[0]), _mm(pp[1], v_pairs[1])], axis=-1), pp


def _attention_backward(p, pp, do_pairs, qp, k_pairs, v_pairs, r):
    dp = _stack_heads(_mm_nt(do_pairs[0], v_pairs[0]), _mm_nt(do_pairs[1], v_pairs[1]))
    delta = jnp.sum(p * dp, axis=-1, keepdims=True)
    dl = p * (dp - delta)
    dlp = _pair_heads(dl.astype(BF16), r)
    dq = jnp.concatenate([_mm(dlp[0], k_pairs[0]), _mm(dlp[1], k_pairs[1])], axis=-1)
    dk = (_mm_tn(dlp[0], qp[0]), _mm_tn(dlp[1], qp[1]))
    dv = (_mm_tn(pp[0], do_pairs[0]), _mm_tn(pp[1], do_pairs[1]))
    return dl, delta, dq, dk, dv


def _tile_specs(n_tiles_ex, width):
    return pl.BlockSpec((TILE, width), lambda b, i: (b * n_tiles_ex + jnp.minimum(i, n_tiles_ex - 1), 0))


def _prev_chunk_spec(n_tiles_ex, width):
    def index(b, i):
        chunk = TILE_CHUNKS * jnp.minimum(i, n_tiles_ex - 1)
        return (b * n_tiles_ex * TILE_CHUNKS + jnp.maximum(chunk - 1, 0), 0)
    return pl.BlockSpec((CHUNK, width), index)


def _full_spec(shape):
    zeros = (0,) * len(shape)
    return pl.BlockSpec(shape, lambda *_: zeros)


SMEM_SPEC = pl.BlockSpec(memory_space=pltpu.SMEM)
ANY_SPEC = pl.BlockSpec(memory_space=pl.ANY)
VMEM_SPEC = pl.BlockSpec(memory_space=pltpu.VMEM)


def _make_bias(rel_bias_t, buckets):
    def body(rel_ref, bk_ref, out_ref):
        bk = bk_ref[...]
        for h in range(4):
            acc = jnp.zeros((CHUNK, 2 * CHUNK), F32)
            for b in range(N_BUCKETS):
                acc = jnp.where(bk == b, rel_ref[h, b], acc)
            for t, with_prev in enumerate((True, False)):
                out_ref[t, h * CHUNK:(h + 1) * CHUNK, :] = jnp.where(_band_valid(with_prev), acc, NEG)

    return pl.pallas_call(
        body, name="make_bias", out_shape=jax.ShapeDtypeStruct((2, 4 * CHUNK, 2 * CHUNK), F32),
        in_specs=[SMEM_SPEC, VMEM_SPEC], out_specs=VMEM_SPEC,
    )(rel_bias_t, buckets)


def _memkv_forward(mem, g_mem, w_mkv):
    n_ex = mem.shape[0]

    def body(mem_ref, g_ref, w_ref, out_ref):
        m = mem_ref[0]
        r = lax.rsqrt(jnp.mean(m * m, axis=-1, keepdims=True) + EPS)
        out_ref[0] = _mm((m * r * g_ref[...]).astype(BF16), w_ref[...])

    return pl.pallas_call(
        body, name="memkv_forward", grid=(n_ex,),
        out_shape=jax.ShapeDtypeStruct((n_ex, MEM_LEN, 2 * MEM_WIDTH), F32),
        in_specs=[pl.BlockSpec((1, MEM_LEN, D_MODEL), lambda b: (b, 0, 0)), _full_spec((1, D_MODEL)),
                  _full_spec((D_MODEL, 2 * MEM_WIDTH))],
        out_specs=pl.BlockSpec((1, MEM_LEN, 2 * MEM_WIDTH), lambda b: (b, 0, 0)),
    )(mem, g_mem, w_mkv)


PROJ_WIDTHS = (A_WIDTH, A_WIDTH, SWA_WIDTH, KV_WIDTH, KV_WIDTH, MEM_WIDTH, MIX_WIDTH)
PROJ_OFFSETS = tuple(int(v) for v in np.cumsum((0,) + PROJ_WIDTHS))


HALF_WIDTH = IN_WIDTH // 2
HALF_PARTS = ((0, 1, 2, 3), (4, 5, 6))


def _gather_and_project(x2, g_pre, w_in_s, w_mkv_s, w_out_s, x_arr):
    n_tok = x2.shape[0]
    n_tiles = n_tok // PROJ_TILE
    last = n_tiles - 1
    shapes = [w_in_s.shape, w_mkv_s.shape, w_out_s.shape]
    n_w = len(shapes)

    def body(x_sref, x_ref, g_ref, win_hbm, wmkv_hbm, wout_hbm, h_ref, *refs):
        part_refs, refs = refs[:len(PROJ_WIDTHS)], refs[len(PROJ_WIDTHS):]
        gin_hbm, gmkv_hbm, gout_hbm, wg, stage_in, stage_mkv, stage_out, own_mkv, own_out = refs[:9]
        send_sems, recv_sems, local_sems = refs[9:]
        p, t = pl.program_id(0), pl.program_id(1)
        x, y, c = lax.axis_index("x"), lax.axis_index("y"), lax.axis_index("c")
        me, sibling = (x, y, c), (x, y, 1 - c)
        my_shard = 2 * x + y
        gathered = [wg, gmkv_hbm, gout_hbm]

        def half_rows(w, shard, half):
            rows = shapes[w][0] // 2
            if w == 0:
                return wg.at[pl.ds(pl.multiple_of(shard * shapes[0][0] + half * rows, 16), rows), :]
            return gathered[w].at[shard, pl.ds(half * rows, rows), :]

        def first(w, rel):
            src = half_rows(w, my_shard, c) if w == 0 else (own_mkv, own_out)[w - 1].at[
                pl.ds(c * (shapes[w][0] // 2), shapes[w][0] // 2), :]
            k = 3 * w + rel - 1
            return pltpu.make_async_remote_copy(
                src_ref=src, dst_ref=half_rows(w, my_shard, c), send_sem=send_sems.at[k], recv_sem=recv_sems.at[k],
                device_id=(x ^ (rel >> 1), y ^ (rel & 1), c), device_id_type=MESH)

        def landed(w, rel):
            k = 3 * w + rel - 1
            ref = half_rows(w, my_shard ^ rel, c)
            return pltpu.make_async_remote_copy(src_ref=ref, dst_ref=ref, send_sem=send_sems.at[k],
                                                recv_sem=recv_sems.at[k], device_id=me, device_id_type=MESH)

        def passed(w, rel, half, to):
            k = 9 + 3 * w + rel - 1
            ref = half_rows(w, my_shard ^ rel, half)
            return pltpu.make_async_remote_copy(src_ref=ref, dst_ref=ref, send_sem=send_sems.at[k],
                                                recv_sem=recv_sems.at[k], device_id=to, device_id_type=MESH)

        def pass_on(w, rels):
            for rel in rels:
                landed(w, rel).wait_recv()
                passed(w, rel, c, sibling).start()
            for rel in rels:
                passed(w, rel, 1 - c, me).wait_recv()

        own_stores = [pltpu.make_async_copy(own_mkv, gmkv_hbm.at[my_shard], local_sems.at[3]),
                      pltpu.make_async_copy(own_out, gout_hbm.at[my_shard], local_sems.at[4])]

        @pl.when((p == 0) & (t == 0))
        def _():
            loads = [pltpu.make_async_copy(src, dst, local_sems.at[k]) for k, (src, dst) in enumerate(
                ((win_hbm, stage_in), (wmkv_hbm, stage_mkv), (wout_hbm, stage_out)))]
            for cp in loads:
                cp.start()
            loads[0].wait()
            wg[pl.ds(pl.multiple_of(my_shard * shapes[0][0], 16), shapes[0][0]), :] = stage_in[...].astype(BF16)
            for rel in (1, 2):
                first(0, rel).start()
            loads[1].wait()
            loads[2].wait()
            own_mkv[...] = stage_mkv[...].astype(BF16)
            own_out[...] = stage_out[...].astype(BF16)
            for cp in own_stores:
                cp.start()
            pass_on(0, (1,))
            first(0, 3).start()

        @pl.when((p == 1) & (t == 0))
        def _():
            pass_on(0, (2, 3))
            for w in (1, 2):
                for rel in (1, 2, 3):
                    first(w, rel).start()

        xv = x_ref[...]
        r = lax.rsqrt(jnp.mean(xv * xv, axis=-1, keepdims=True) + EPS)
        h = (xv * r * g_ref[...]).astype(BF16)

        @pl.when(p == 0)
        def _():
            h_ref[...] = h

        for hh in range(2):
            @pl.when((p ^ x_sref[0]) == hh)
            def _():
                proj = _mm_nt(h, wg[hh * HALF_WIDTH:(hh + 1) * HALF_WIDTH, :])
                for k in HALF_PARTS[hh]:
                    lo = PROJ_OFFSETS[k] - hh * HALF_WIDTH
                    part_refs[k][...] = proj[:, lo:lo + PROJ_WIDTHS[k]]

        @pl.when((p == 1) & (t == last))
        def _():
            store = pltpu.make_async_copy(wg, gin_hbm, local_sems.at[5])
            store.start()
            for w in (1, 2):
                pass_on(w, (1, 2, 3))
            for w in range(n_w):
                for rel in (1, 2, 3):
                    first(w, rel).wait_send()
                    passed(w, rel, c, sibling).wait_send()
            for cp in own_stores:
                cp.wait()
            store.wait()

    def active_in(hh):
        def index(p, t, xs):
            return (jnp.where((p ^ xs[0]) == hh, t, jnp.where(p == 0, 0, last)), 0)
        return index

    part_specs = [pl.BlockSpec((PROJ_TILE, PROJ_WIDTHS[k]), active_in(hh)) for hh in range(2) for k in HALF_PARTS[hh]]
    vmem = pltpu.VMEM
    out = pl.pallas_call(
        body, name="gather_and_project",
        out_shape=[jax.ShapeDtypeStruct((n_tok, D_MODEL), BF16)]
        + [jax.ShapeDtypeStruct((n_tok, w), F32) for w in PROJ_WIDTHS]
        + [jax.ShapeDtypeStruct((N_CHIPS * shapes[0][0], shapes[0][1]), BF16)]
        + [jax.ShapeDtypeStruct((N_CHIPS,) + s, BF16) for s in shapes[1:]],
        grid_spec=pltpu.PrefetchScalarGridSpec(
            num_scalar_prefetch=1, grid=(2, n_tiles),
            in_specs=[pl.BlockSpec((PROJ_TILE, D_MODEL), lambda p, t, xs: (t, 0)),
                      pl.BlockSpec((1, D_MODEL), lambda p, t, xs: (0, 0)), ANY_SPEC, ANY_SPEC, ANY_SPEC],
            out_specs=[pl.BlockSpec((PROJ_TILE, D_MODEL), lambda p, t, xs: (jnp.where(p == 0, t, last), 0))]
            + part_specs + [ANY_SPEC] * 3,
            scratch_shapes=[vmem((N_CHIPS * shapes[0][0], shapes[0][1]), BF16), vmem(shapes[0], F32),
                            vmem(shapes[1], F32), vmem(shapes[2], F32), vmem(shapes[1], BF16), vmem(shapes[2], BF16),
                            pltpu.SemaphoreType.DMA((18,)), pltpu.SemaphoreType.DMA((18,)),
                            pltpu.SemaphoreType.DMA((6,))]),
        compiler_params=pltpu.CompilerParams(vmem_limit_bytes=VMEM_LIMIT),
    )(x_arr, x2, g_pre, w_in_s, w_mkv_s, w_out_s)
    h, parts, weights = out[0], out[1:1 + len(PROJ_WIDTHS)], out[1 + len(PROJ_WIDTHS):]
    return h, list(parts), weights


def _load_chunk(j, i, sk_ref, sv_ref, skp_ref, svp_ref):
    rows = slice(j * CHUNK, (j + 1) * CHUNK)
    if j == 0:
        k_prev, v_prev, table = skp_ref[...], svp_ref[...], jnp.where(i > 0, 0, 1)
    else:
        prev = slice((j - 1) * CHUNK, j * CHUNK)
        k_prev, v_prev, table = sk_ref[prev, :], sv_ref[prev, :], 0
    k_pairs = _pair_operands(_swa_variants(jnp.concatenate([k_prev, sk_ref[rows, :]], axis=0)))
    v_pairs = _pair_operands(_swa_variants(jnp.concatenate([v_prev, sv_ref[rows, :]], axis=0)))
    return rows, k_pairs, v_pairs, table


def _tile_constants(ws_ref, bs_ref, sink_ref, mkv_ref):
    wm = _causal_weights(ws_ref)
    bs_rows = [jnp.concatenate([bs_ref[g]] * TILE_CHUNKS, axis=0) for g in range(A_GROUPS)]
    sink_col = jnp.max(jnp.concatenate([jnp.full((CHUNK, 128), sink_ref[0, h], F32) for h in range(4)], axis=0),
                       axis=-1, keepdims=True)
    mkv_v = mkv_ref[0]
    mk_pairs = _pair_operands(_mem_variants(mkv_v[:, :MEM_WIDTH]))
    mv_pairs = _pair_operands(_mem_variants(mkv_v[:, MEM_WIDTH:]))
    return wm, bs_rows, sink_col, mk_pairs, mv_pairs


def _mix(parts, mkv, x2, tgt2, v_g, v_b, w_sp, b_sp, sinks, bias, w_out, g_post, n_ex, seq):
    n_tiles_ex = seq // TILE
    n_tok = n_ex * seq
    au, av, sq, sk, sv, mq, z = parts
    col = dict(zip(("au", "av", "sq", "sk", "sv", "mq", "z"),
                   (slice(PROJ_OFFSETS[k], PROJ_OFFSETS[k + 1]) for k in range(len(PROJ_WIDTHS)))))
    before_kv, after_kv = slice(0, col["sk"].start), slice(col["sv"].stop, IN_WIDTH)

    def body(au_ref, av_ref, sq_ref, sk_ref, sv_ref, skp_ref, svp_ref, mq_ref, z_ref, mkv_ref, x_ref, tgt_ref,
             vg_ref, vb_ref, ws_ref, bs_ref, sink_ref, bias_ref, wout_ref, gpost_ref,
             dout_ref, dproj_ref, dmkv_ref, dwout_ref, dvg_ref, dvb_ref, dws_ref, dbs_ref, dsink_ref, drel_ref,
             loss_ref, dgpost_ref, carry_dp, carry_k, carry_v):
        b, i = pl.program_id(0), pl.program_id(1)

        @pl.when((b == 0) & (i == 0))
        def _():
            for ref in (dwout_ref, dvg_ref, dvb_ref, dws_ref, dbs_ref, dsink_ref, drel_ref, loss_ref, dgpost_ref):
                ref[...] = jnp.zeros_like(ref)

        @pl.when(i == 0)
        def _():
            dmkv_ref[...] = jnp.zeros_like(dmkv_ref)
            carry_k[...] = jnp.zeros_like(carry_k)
            carry_v[...] = jnp.zeros_like(carry_v)

        @pl.when(i > 0)
        def _():
            dproj_ref[:, before_kv] = carry_dp[:, before_kv]
            dproj_ref[:, after_kv] = carry_dp[:, after_kv]

        @pl.when(i < n_tiles_ex)
        def _():
            wm, bs_rows, sink_col, mk_pairs, mv_pairs = _tile_constants(ws_ref, bs_ref, sink_ref, mkv_ref)
            vg = vg_ref[...]

            au_v, av_v = au_ref[...], av_ref[...]
            ya, res = _group_a_forward(au_v, av_v, vg, vb_ref[...], wm, bs_rows)
            swa, yb = [], []
            for j in range(TILE_CHUNKS):
                rows, k_pairs, v_pairs, table = _load_chunk(j, i, sk_ref, sv_ref, skp_ref, svp_ref)
                qp = _halves_bf16(sq_ref[rows, :])
                p, ps = _attention_probs(qp, k_pairs, bias_ref[table], sink_col)
                out, pp = _attention_out(p, v_pairs, CHUNK)
                yb.append(out)
                swa.append((rows, k_pairs, v_pairs, qp, p, ps, pp))
            mqp = _halves_bf16(mq_ref[...])
            pm, _ = _attention_probs(mqp, mk_pairs, None, None)
            yc, ppm = _attention_out(pm, mv_pairs, TILE)
            ycat = jnp.concatenate(ya + [jnp.concatenate(yb, axis=0), yc], axis=-1)

            zv = z_ref[...]
            sig = _sigmoid(zv)
            sz = zv * sig
            y_b = (ycat * sz).astype(BF16)
            o = _mm(y_b, wout_ref[...])
            r2 = lax.rsqrt(jnp.mean(o * o, axis=-1, keepdims=True) + EPS)
            nrm = o * r2
            gp = gpost_ref[...]
            diff = x_ref[...] + nrm * gp - tgt_ref[...]
            loss_ref[...] += jnp.sum(diff * diff) * (0.5 / D_MODEL)
            dout = diff * (1.0 / D_MODEL)
            dout_ref[...] = dout
            dgpost_ref[...] += jnp.sum(dout * nrm, axis=0, keepdims=True)
            dn = dout * gp
            do_b = (r2 * (dn - nrm * jnp.mean(dn * nrm, axis=-1, keepdims=True))).astype(BF16)
            dwout_ref[...] += _mm_tn(y_b, do_b)
            dy = _mm_nt(do_b, wout_ref[...])
            carry_dp[:, col["z"]] = (dy * ycat * (sig * (1.0 + zv * (1.0 - sig)))).astype(BF16)
            dyc = dy * sz

            dgu, dgv = [], []
            for g in range(A_GROUPS):
                sl = slice(g * 128, (g + 1) * 128)
                xhat, rstd, vn, s = res["groups"][g]
                dya = dyc[:, sl]
                dgu.append(dya * s)
                ds = dya * res["gu"][:, sl]
                dbs_ref[:, sl] += sum(ds[c * CHUNK:(c + 1) * CHUNK] for c in range(TILE_CHUNKS))
                ds_b = _rows_to_lanes(ds.astype(BF16), TILE_CHUNKS)
                dws_ref[g] += _mm_nt(ds_b, vn)
                dvn = _lanes_to_rows(_mm_tn(wm[g], ds_b), TILE_CHUNKS)
                dvg_ref[:, sl] += jnp.sum(dvn * xhat, axis=0, keepdims=True)
                dvb_ref[:, sl] += jnp.sum(dvn, axis=0, keepdims=True)
                dxh = dvn * vg[:, sl]
                dgv.append(rstd * (dxh - jnp.mean(dxh, axis=-1, keepdims=True)
                                   - xhat * jnp.mean(dxh * xhat, axis=-1, keepdims=True)))
            carry_dp[:, col["au"]] = (jnp.concatenate(dgu, axis=-1) * _gelu_grad(au_v, res["tu"])).astype(BF16)
            carry_dp[:, col["av"]] = (jnp.concatenate(dgv, axis=-1) * _gelu_grad(av_v, res["tv"])).astype(BF16)

            lane4 = lax.broadcasted_iota(jnp.int32, (1, 128), 1)
            dsink_vec = jnp.zeros((1, 128), F32)
            dk_parts, dv_parts = [], []
            for rows, k_pairs, v_pairs, qp, p, ps, pp in swa:
                do_pairs = _halves_bf16(dyc[rows, A_WIDTH:A_WIDTH + SWA_WIDTH])
                dl, delta, dq, dk, dv = _attention_backward(p, pp, do_pairs, qp, k_pairs, v_pairs, CHUNK)
                sink_terms = ps * delta
                for h in range(4):
                    dsink_vec = dsink_vec + jnp.where(lane4 == h, -jnp.sum(sink_terms[h * CHUNK:(h + 1) * CHUNK]), 0.0)
                drel_ref[...] += dl
                carry_dp[rows, col["sq"]] = (dq * QK_SCALE).astype(BF16)
                dk_parts.append(_swa_unvariants(*_split_pair_grads(dk)) * QK_SCALE)
                dv_parts.append(_swa_unvariants(*_split_pair_grads(dv)))
            dsink_ref[...] += dsink_vec

            dc_pairs = _halves_bf16(dyc[:, A_WIDTH + SWA_WIDTH:])
            _, _, dmq, dmk, dmv = _attention_backward(pm, ppm, dc_pairs, mqp, mk_pairs, mv_pairs, TILE)
            carry_dp[:, col["mq"]] = (dmq * QK_SCALE).astype(BF16)
            dmkv_ref[0] += jnp.concatenate([_mem_unvariants(*_split_pair_grads(dmk)) * QK_SCALE,
                                            _mem_unvariants(*_split_pair_grads(dmv))], axis=-1)

            for parts_c, carry, cols in ((dk_parts, carry_k, col["sk"]), (dv_parts, carry_v, col["sv"])):
                @pl.when(i > 0)
                def _():
                    dproj_ref[:, cols] = (carry[...] + jnp.concatenate(
                        [jnp.zeros((TILE - CHUNK, KV_WIDTH), F32), parts_c[0][:CHUNK]], axis=0)).astype(BF16)
                new = [parts_c[0][CHUNK:]]
                for j in range(1, TILE_CHUNKS):
                    new[-1] = new[-1] + parts_c[j][:CHUNK]
                    new.append(parts_c[j][CHUNK:])
                carry[...] = jnp.concatenate(new, axis=0)

        @pl.when(i == n_tiles_ex)
        def _():
            dproj_ref[:, col["sk"]] = carry_k[...].astype(BF16)
            dproj_ref[:, col["sv"]] = carry_v[...].astype(BF16)

    tile = functools.partial(_tile_specs, n_tiles_ex)
    prev = functools.partial(_prev_chunk_spec, n_tiles_ex)
    late = pl.BlockSpec((TILE, IN_WIDTH), lambda b, i: (b * n_tiles_ex + jnp.maximum(i - 1, 0), 0))
    return pl.pallas_call(
        body, name="mix", grid=(n_ex, n_tiles_ex + 1),
        out_shape=[jax.ShapeDtypeStruct((n_tok, D_MODEL), F32), jax.ShapeDtypeStruct((n_tok, IN_WIDTH), BF16),
                   jax.ShapeDtypeStruct((n_ex, MEM_LEN, 2 * MEM_WIDTH), F32),
                   jax.ShapeDtypeStruct((MIX_WIDTH, D_MODEL), F32), jax.ShapeDtypeStruct((1, A_WIDTH), F32),
                   jax.ShapeDtypeStruct((1, A_WIDTH), F32), jax.ShapeDtypeStruct((A_GROUPS, CHUNK, CHUNK), F32),
                   jax.ShapeDtypeStruct((CHUNK, A_WIDTH), F32), jax.ShapeDtypeStruct((1, 128), F32),
                   jax.ShapeDtypeStruct((4 * CHUNK, 2 * CHUNK), F32), jax.ShapeDtypeStruct((1, 128), F32),
                   jax.ShapeDtypeStruct((1, D_MODEL), F32)],
        in_specs=[tile(A_WIDTH), tile(A_WIDTH), tile(SWA_WIDTH), tile(KV_WIDTH), tile(KV_WIDTH),
                  prev(KV_WIDTH), prev(KV_WIDTH), tile(MEM_WIDTH), tile(MIX_WIDTH),
                  pl.BlockSpec((1, MEM_LEN, 2 * MEM_WIDTH), lambda b, i: (b, 0, 0)),
                  tile(D_MODEL), tile(D_MODEL),
                  _full_spec((1, A_WIDTH)), _full_spec((1, A_WIDTH)), _full_spec((A_GROUPS, CHUNK, CHUNK)),
                  _full_spec((A_GROUPS, CHUNK, CHUNK)), SMEM_SPEC, _full_spec((2, 4 * CHUNK, 2 * CHUNK)),
                  _full_spec((MIX_WIDTH, D_MODEL)), _full_spec((1, D_MODEL))],
        out_specs=[tile(D_MODEL), late, pl.BlockSpec((1, MEM_LEN, 2 * MEM_WIDTH), lambda b, i: (b, 0, 0)),
                   _full_spec((MIX_WIDTH, D_MODEL)), _full_spec((1, A_WIDTH)), _full_spec((1, A_WIDTH)),
                   _full_spec((A_GROUPS, CHUNK, CHUNK)), _full_spec((CHUNK, A_WIDTH)), _full_spec((1, 128)),
                   _full_spec((4 * CHUNK, 2 * CHUNK)), _full_spec((1, 128)), _full_spec((1, D_MODEL))],
        scratch_shapes=[pltpu.VMEM((TILE, IN_WIDTH), BF16), pltpu.VMEM((TILE, KV_WIDTH), F32),
                        pltpu.VMEM((TILE, KV_WIDTH), F32)],
        compiler_params=pltpu.CompilerParams(vmem_limit_bytes=VMEM_LIMIT),
    )(au, av, sq, sk, sv, sk, sv, mq, z, mkv, x2, tgt2, v_g, v_b, w_sp, b_sp, sinks, bias, w_out, g_post)


BWD_PROJ_TILE = 512


def _backward_projection(x2, dout, dproj, g_pre, w_in_t):
    n_tok = x2.shape[0]
    n_steps = n_tok // BWD_PROJ_TILE

    def body(x_ref, dout_ref, dp_ref, g_ref, w_hbm, dx_ref, dgpre_ref, w_vmem, sem):
        @pl.when(pl.program_id(0) == 0)
        def _():
            load = pltpu.make_async_copy(w_hbm, w_vmem, sem)
            load.start()
            dgpre_ref[...] = jnp.zeros_like(dgpre_ref)
            load.wait()

        xv = x_ref[...]
        r = lax.rsqrt(jnp.mean(xv * xv, axis=-1, keepdims=True) + EPS)
        xn = xv * r
        dh = _mm(dp_ref[...], w_vmem[...])
        dgpre_ref[...] += jnp.sum(dh * xn, axis=0, keepdims=True)
        dhg = dh * g_ref[...]
        dx_ref[...] = r * (dhg - xn * jnp.mean(dhg * xn, axis=-1, keepdims=True)) + dout_ref[...]

    row = lambda w: pl.BlockSpec((BWD_PROJ_TILE, w), lambda i: (i, 0))
    return pl.pallas_call(
        body, name="backward_projection", grid=(n_steps,),
        out_shape=[jax.ShapeDtypeStruct((n_tok, D_MODEL), F32), jax.ShapeDtypeStruct((1, D_MODEL), F32)],
        in_specs=[row(D_MODEL), row(D_MODEL), row(IN_WIDTH), _full_spec((1, D_MODEL)), ANY_SPEC],
        out_specs=[row(D_MODEL), _full_spec((1, D_MODEL))],
        scratch_shapes=[pltpu.VMEM((IN_WIDTH, D_MODEL), BF16), pltpu.SemaphoreType.DMA],
        input_output_aliases={1: 0},
        compiler_params=pltpu.CompilerParams(vmem_limit_bytes=VMEM_LIMIT),
    )(x2, dout, dproj, g_pre, w_in_t)


SHARD_ROWS = IN_WIDTH // N_CHIPS
SHARD_WINDOW = 768
SHARD_HALF = SHARD_ROWS // 2
DWIN_TILE = 2048
N_REL = N_CHIPS - 1


def _shard_window_start(shard):
    return (shard * SHARD_ROWS // 128) * 128


def _reduce_gradients(dproj, h, big, small, shard_arr):
    n_tok = h.shape[0]
    tile = min(DWIN_TILE, n_tok)
    n_sub = n_tok // tile
    last = N_CHIPS - 1
    n_big, n_small = len(big), len(small)
    big_half = [g.shape[2:] for g in big]
    sem_big_d2d = 2 * N_CHIPS
    sem_big_ici = sem_big_d2d + n_big
    sem_big_swap = sem_big_ici + N_REL * n_big
    sem_small_d2d = sem_big_swap + n_big
    sem_small_ici = sem_small_d2d + n_small
    n_sems = sem_small_ici + N_REL * n_small
    loc_small = n_big
    loc_out_win = loc_small + n_small
    loc_out_big = loc_out_win + 2
    loc_out_small = loc_out_big + 2 * n_big
    n_local = loc_out_small + n_small

    def relation_of_slot(s):
        return (s + 2) % N_REL + 1

    def shard_of_slot(s, my_shard):
        return my_shard ^ jnp.where(s == last, 0, relation_of_slot(s))

    def body(shard_ref, dp_ref, h_hbm, *refs):
        h_vmem, h_sem, refs = refs[-2], refs[-1], refs[:-2]
        big_hbm, refs = refs[:n_big], refs[n_big:]
        small_hbm, refs = refs[:n_small], refs[n_small:]
        out_hbm, refs = refs[0], refs[1:]
        big_out, refs = refs[:n_big], refs[n_big:]
        small_out, refs = refs[:n_small], refs[n_small:]
        part, recv_d2d, send_ici, recv_ici, mine_buf, other_buf = refs[:6]
        refs = refs[6:]
        big_own, big_recv, big_send, big_land, big_mine, big_other = (
            refs[k * n_big:(k + 1) * n_big] for k in range(6))
        refs = refs[6 * n_big:]
        small_own, small_recv, small_all = (refs[k * n_small:(k + 1) * n_small] for k in range(3))
        send_sems, recv_sems, local_sems = refs[3 * n_small:]

        s, t = pl.program_id(0), pl.program_id(1)
        x, y, c = lax.axis_index("x"), lax.axis_index("y"), lax.axis_index("c")
        my_chip = 2 * x + y
        sibling = (x, y, 1 - c)
        my_rows = pl.ds(pl.multiple_of(c * SHARD_HALF, 8), SHARD_HALF)
        other_rows = pl.ds(pl.multiple_of((1 - c) * SHARD_HALF, 8), SHARD_HALF)

        def remote(src, dst, k, to):
            return pltpu.make_async_remote_copy(src_ref=src, dst_ref=dst, send_sem=send_sems.at[k],
                                                recv_sem=recv_sems.at[k], device_id=to, device_id_type=MESH)

        def chip_at(rel):
            return (x ^ (rel >> 1), y ^ (rel & 1), c)

        def to_sibling(k):
            return remote(part.at[k % 2, other_rows, :], recv_d2d.at[k], k, sibling)

        def to_chip(k):
            return remote(send_ici.at[k], recv_ici.at[k], N_CHIPS + k, chip_at(relation_of_slot(k)))

        swap = remote(mine_buf, other_buf, 2 * N_CHIPS - 1, sibling)
        big_load = [pltpu.make_async_copy(big_hbm[w].at[:, pl.ds(c, 1)], big_own[w], local_sems.at[w])
                    for w in range(n_big)]
        big_to_sibling = [remote(big_hbm[w].at[:, pl.ds(1 - c, 1)], big_recv[w], sem_big_d2d + w, sibling)
                          for w in range(n_big)]
        big_to_chip = [[remote(big_send[w].at[k], big_land[w].at[k], sem_big_ici + N_REL * w + k, chip_at(k + 1))
                        for k in range(N_REL)] for w in range(n_big)]
        big_swap = [remote(big_mine[w], big_other[w], sem_big_swap + w, sibling) for w in range(n_big)]
        small_load = [pltpu.make_async_copy(small_hbm[i], small_own[i], local_sems.at[loc_small + i])
                      for i in range(n_small)]
        small_to_sibling = [remote(small_hbm[i], small_recv[i], sem_small_d2d + i, sibling) for i in range(n_small)]
        small_to_chip = [[remote(small_all[i].at[my_chip], small_all[i].at[my_chip],
                                 sem_small_ici + N_REL * i + k, chip_at(k + 1))
                          for k in range(N_REL)] for i in range(n_small)]

        @pl.when((s == 0) & (t == 0))
        def _():
            h_load = pltpu.make_async_copy(h_hbm, h_vmem, h_sem)
            h_load.start()
            for cp in big_load + big_to_sibling + small_load + small_to_sibling:
                cp.start()
            h_load.wait()

        @pl.when((s == 0) & (t == n_sub - 1))
        def _():
            for cp in big_load + small_load:
                cp.wait()
            for cp in big_to_sibling + small_to_sibling:
                cp.wait_recv()
                cp.wait_send()
            for w in range(n_big):
                for k in range(N_REL):
                    shard = my_chip ^ (k + 1)
                    big_send[w][k] = (big_own[w][shard, 0] + big_recv[w][shard, 0]).astype(BF16)
                    big_to_chip[w][k].start()
            for i in range(n_small):
                small_all[i][my_chip] = small_own[i][...] + small_recv[i][...]
                for k in range(N_REL):
                    small_to_chip[i][k].start()

        @pl.when((s > 0) & (t == 0))
        def _():
            k = s - 1
            cp = to_sibling(k)
            cp.wait_recv()
            cp.wait_send()
            send_ici[k] = (part[k % 2, my_rows, :] + recv_d2d[k]).astype(BF16)
            to_chip(k).start()

        r = _mm_tn(dp_ref[...], h_vmem[pl.ds(pl.multiple_of(t * tile, tile), tile), :])
        odd = shard_of_slot(s, shard_ref[0]) % 2
        for parity in range(2):
            rows = r[64 * parity:64 * parity + SHARD_ROWS]

            @pl.when((odd == parity) & (t == 0))
            def _():
                part[s % 2] = rows

            @pl.when((odd == parity) & (t > 0))
            def _():
                part[s % 2] += rows

        @pl.when(t == n_sub - 1)
        def _():
            to_sibling(s).start()

        @pl.when((s == last) & (t == n_sub - 1))
        def _():
            cp = to_sibling(last)
            cp.wait_recv()
            cp.wait_send()
            total = part[last % 2, my_rows, :] + recv_d2d[last]
            for k in range(last):
                to_chip(k).wait_recv()
                total = total + recv_ici[k].astype(F32)
            mine_buf[...] = total
            swap.start()
            out_mine = pltpu.make_async_copy(mine_buf, out_hbm.at[my_rows, :], local_sems.at[0])
            out_mine.start()
            swap.wait_recv()
            out_other = pltpu.make_async_copy(other_buf, out_hbm.at[other_rows, :], local_sems.at[1])
            out_other.start()
            stores = [out_mine, out_other]
            for w in range(n_big):
                rows = big_half[w][0]
                total = big_own[w][my_chip, 0] + big_recv[w][my_chip, 0]
                for k in range(N_REL):
                    big_to_chip[w][k].wait_recv()
                    total = total + big_land[w][k].astype(F32)
                big_mine[w][...] = total
                big_swap[w].start()
                stores.append(pltpu.make_async_copy(
                    big_mine[w], big_out[w].at[pl.ds(pl.multiple_of(c * rows, 8), rows), :],
                    local_sems.at[loc_out_big + 2 * w]))
                stores[-1].start()
            for w in range(n_big):
                rows = big_half[w][0]
                big_swap[w].wait_recv()
                stores.append(pltpu.make_async_copy(
                    big_other[w], big_out[w].at[pl.ds(pl.multiple_of((1 - c) * rows, 8), rows), :],
                    local_sems.at[loc_out_big + 2 * w + 1]))
                stores[-1].start()
            for i in range(n_small):
                for k in range(N_REL):
                    small_to_chip[i][k].wait_recv()
                stores.append(pltpu.make_async_copy(small_all[i], small_out[i], local_sems.at[loc_out_small + i]))
                stores[-1].start()
            for k in range(last):
                to_chip(k).wait_send()
            swap.wait_send()
            for w in range(n_big):
                for k in range(N_REL):
                    big_to_chip[w][k].wait_send()
                big_swap[w].wait_send()
            for i in range(n_small):
                for k in range(N_REL):
                    small_to_chip[i][k].wait_send()
            for cp in stores:
                cp.wait()

    half = (SHARD_HALF, D_MODEL)
    vmem = pltpu.VMEM
    scratch = [vmem((2, SHARD_ROWS, D_MODEL), F32), vmem((N_CHIPS,) + half, F32),
               vmem((N_REL,) + half, BF16), vmem((N_REL,) + half, BF16), vmem(half, F32), vmem(half, F32)]
    scratch += [vmem((N_CHIPS, 1) + hs, F32) for hs in big_half] * 2
    scratch += [vmem((N_REL,) + hs, BF16) for hs in big_half] * 2
    scratch += [vmem(hs, F32) for hs in big_half] * 2
    scratch += [vmem(a.shape, F32) for a in small] * 2 + [vmem((N_CHIPS,) + a.shape, F32) for a in small]
    scratch += [pltpu.SemaphoreType.DMA((n_sems,)), pltpu.SemaphoreType.DMA((n_sems,)),
                pltpu.SemaphoreType.DMA((n_local,)), vmem(h.shape, BF16), pltpu.SemaphoreType.DMA]
    n_hbm = n_big + n_small
    out = pl.pallas_call(
        body, name="reduce_gradients",
        out_shape=[jax.ShapeDtypeStruct((SHARD_ROWS, D_MODEL), F32)]
        + [jax.ShapeDtypeStruct((2 * hs[0], hs[1]), F32) for hs in big_half]
        + [jax.ShapeDtypeStruct((N_CHIPS,) + a.shape, F32) for a in small],
        grid_spec=pltpu.PrefetchScalarGridSpec(
            num_scalar_prefetch=1, grid=(N_CHIPS, n_sub),
            in_specs=[pl.BlockSpec((pl.Element(tile), pl.Element(SHARD_WINDOW)),
                                   lambda s, t, m: (t * tile, _shard_window_start(shard_of_slot(s, m[0])))),
                      ANY_SPEC] + [ANY_SPEC] * n_hbm,
            out_specs=[ANY_SPEC] * (1 + n_hbm),
            scratch_shapes=scratch),
        compiler_params=pltpu.CompilerParams(vmem_limit_bytes=VMEM_LIMIT),
    )(shard_arr, dproj, h, *big, *small)
    return out[:1 + n_big], out[1 + n_big:]


def _memkv_backward(mem, dmkv, g_mem, w_mkv):
    n_ex = mem.shape[0]

    def body(mem_ref, d_ref, g_ref, w_ref, dw_ref, dg_ref):
        @pl.when(pl.program_id(0) == 0)
        def _():
            dw_ref[...] = jnp.zeros_like(dw_ref)
            dg_ref[...] = jnp.zeros_like(dg_ref)

        m = mem_ref[0]
        mn = m * lax.rsqrt(jnp.mean(m * m, axis=-1, keepdims=True) + EPS)
        d_b = d_ref[0].astype(BF16)
        dw_ref[...] += _mm_tn((mn * g_ref[...]).astype(BF16), d_b)
        dg_ref[...] += jnp.sum(_mm_nt(d_b, w_ref[...]) * mn, axis=0, keepdims=True)

    return pl.pallas_call(
        body, name="memkv_backward", grid=(n_ex,),
        out_shape=[jax.ShapeDtypeStruct((D_MODEL, 2 * MEM_WIDTH), F32), jax.ShapeDtypeStruct((1, D_MODEL), F32)],
        in_specs=[pl.BlockSpec((1, MEM_LEN, D_MODEL), lambda b: (b, 0, 0)),
                  pl.BlockSpec((1, MEM_LEN, 2 * MEM_WIDTH), lambda b: (b, 0, 0)),
                  _full_spec((1, D_MODEL)), _full_spec((D_MODEL, 2 * MEM_WIDTH))],
        out_specs=[_full_spec((D_MODEL, 2 * MEM_WIDTH)), _full_spec((1, D_MODEL))],
    )(mem, dmkv, g_mem, w_mkv)


def _pack_small_grads(dgpre, dgpost, dgmem, dvg, dvb, dws, dbs, dsink, drel, loss_vec, buckets):
    def body(dgpre_ref, dgpost_ref, dgmem_ref, dvg_ref, dvb_ref, dws_ref, dbs_ref, dsink_ref, drel_ref, loss_ref,
             bk_ref, a_ref, b_ref):
        a_ref[...] = jnp.zeros_like(a_ref)
        b_ref[...] = jnp.zeros_like(b_ref)
        a_ref[0:1, :] = dgpre_ref[...]
        a_ref[1:2, :] = dgpost_ref[...]
        a_ref[2:3, :] = dgmem_ref[...]
        a_ref[3:4, :] = jnp.concatenate([dvg_ref[...], dvb_ref[...]], axis=-1)
        a_ref[ROW_LOSS:ROW_LOSS + 1, 0:128] = loss_ref[...]
        row = lax.broadcasted_iota(jnp.int32, (CHUNK, CHUNK), 0)
        col = lax.broadcasted_iota(jnp.int32, (CHUNK, CHUNK), 1)
        for g in range(A_GROUPS):
            b_ref[ROW_WS + g * CHUNK:ROW_WS + (g + 1) * CHUNK, :] = jnp.where(row >= col, dws_ref[g], 0.0)
            by_token = jnp.transpose(dbs_ref[:, g * 128:(g + 1) * 128])
            b_ref[ROW_BS + g:ROW_BS + g + 1, :] = jnp.sum(by_token, axis=0, keepdims=True)
        b_ref[ROW_SINK:ROW_SINK + 1, :] = dsink_ref[...]
        bk = bk_ref[...]
        rel_row = lax.broadcasted_iota(jnp.int32, (8, 128), 0)
        rel_col = lax.broadcasted_iota(jnp.int32, (8, 128), 1)
        rel = jnp.zeros((8, 128), F32)
        for h in range(4):
            acc = drel_ref[h * CHUNK:(h + 1) * CHUNK, :]
            for b in range(N_BUCKETS):
                rel = jnp.where((rel_row == h) & (rel_col == b), jnp.sum(jnp.where(bk == b, acc, 0.0)), rel)
        b_ref[ROW_REL:ROW_REL + 8, :] = rel

    return pl.pallas_call(
        body, name="pack_small_grads",
        out_shape=[jax.ShapeDtypeStruct((SMALL_A_ROWS, D_MODEL), F32), jax.ShapeDtypeStruct((SMALL_B_ROWS, 128), F32)],
        in_specs=[VMEM_SPEC] * 11, out_specs=[VMEM_SPEC] * 2,
    )(dgpre, dgpost, dgmem, dvg, dvb, dws, dbs, dsink, drel, loss_vec, buckets)


def _adamw(w, g, m, v):
    m2 = ADAM_B1 * m + (1.0 - ADAM_B1) * g
    v2 = ADAM_B2 * v + (1.0 - ADAM_B2) * (g * g)
    m_hat = m2 / (1.0 - ADAM_B1 ** ADAM_STEP)
    v_hat = v2 / (1.0 - ADAM_B2 ** ADAM_STEP)
    delta = -ADAM_LR * (m_hat / (jnp.sqrt(v_hat) + ADAM_EPS) + ADAM_WD * w)
    return delta, m2, v2


ADAM_MAX_ROWS = 176


def _adamw_whole(g, w, m, v, name):
    rows, cols = w.shape
    steps = -(-rows // ADAM_MAX_ROWS)
    block_rows = rows // steps
    assert block_rows * steps == rows and block_rows % 8 == 0

    def body(g_ref, w_ref, m_ref, v_ref, d_out, m_out, v_out):
        delta, m2, v2 = _adamw(w_ref[...], g_ref[...], m_ref[...], v_ref[...])
        d_out[...] = delta
        m_out[...] = m2
        v_out[...] = v2

    block = pl.BlockSpec((block_rows, cols), lambda k: (k, 0))
    out = pl.pallas_call(
        body, name=name, grid=(steps,), out_shape=[jax.ShapeDtypeStruct((rows, cols), F32)] * 3,
        in_specs=[block] * 4, out_specs=[block] * 3,
    )(g, w, m, v)
    return [g] + list(out)


def _adamw_small(ra, rb, weights, moments_m, moments_v):
    n = len(weights)

    def body(*refs):
        ra_ref, rb_ref = refs[0], refs[1]
        w_refs, m_refs, v_refs = refs[2:2 + n], refs[2 + n:2 + 2 * n], refs[2 + 2 * n:2 + 3 * n]
        outs = refs[2 + 3 * n:]
        g_outs, d_outs, m_outs, v_outs = outs[:n], outs[n:2 * n], outs[2 * n:3 * n], outs[3 * n:4 * n]
        ga, gb = ra_ref[0], rb_ref[0]
        for chip in range(1, N_CHIPS):
            ga = ga + ra_ref[chip]
            gb = gb + rb_ref[chip]
        outs[4 * n][...] = ga[ROW_LOSS:ROW_LOSS + 1, 0:128]
        grads = [ga[0:1, :], ga[1:2, :], ga[2:3, :], ga[3:4, :A_WIDTH], ga[3:4, A_WIDTH:],
                 gb[ROW_WS:ROW_WS + A_GROUPS * CHUNK, :].reshape(A_GROUPS, CHUNK, CHUNK),
                 gb[ROW_BS:ROW_BS + A_GROUPS, :], gb[ROW_SINK:ROW_SINK + 1, 0:4],
                 gb[ROW_REL:ROW_REL + 4, 0:N_BUCKETS]]
        for k in range(n):
            delta, m2, v2 = _adamw(w_refs[k][...], grads[k], m_refs[k][...], v_refs[k][...])
            g_outs[k][...] = grads[k]
            d_outs[k][...] = delta
            m_outs[k][...] = m2
            v_outs[k][...] = v2

    out_shape = [jax.ShapeDtypeStruct(w.shape, F32) for w in weights] * 4 + [jax.ShapeDtypeStruct((1, 128), F32)]
    return pl.pallas_call(
        body, name="adamw_small", out_shape=out_shape,
        in_specs=[VMEM_SPEC] * (2 + 3 * n), out_specs=[VMEM_SPEC] * (4 * n + 1),
    )(ra, rb, *weights, *moments_m, *moments_v)


def kernel(x, mem, pre_norm_g, post_norm_g, mem_norm_g, w_in, w_mem_kv, v_norm_g, v_norm_b, w_spatial, b_spatial, attn_sinks, rel_bias, w_out, loss_target, m_pre_norm_g, m_post_norm_g, m_mem_norm_g, m_w_in, m_w_mem_kv, m_v_norm_g, m_v_norm_b, m_w_spatial, m_b_spatial, m_attn_sinks, m_rel_bias, m_w_out, v_pre_norm_g, v_post_norm_g, v_mem_norm_g, v_w_in, v_w_mem_kv, v_v_norm_g, v_v_norm_b, v_w_spatial, v_b_spatial, v_attn_sinks, v_rel_bias, v_w_out):
    n_ex, seq, _ = x.shape
    n_tok = n_ex * seq
    x2 = x.reshape(n_tok, D_MODEL)
    tgt2 = loss_target.reshape(n_tok, D_MODEL)
    buckets = jnp.asarray(_bucket_map())
    shard_arr = (2 * lax.axis_index("x") + lax.axis_index("y")).astype(jnp.int32).reshape(1)
    w_sp = w_spatial[0]
    b_sp = jnp.broadcast_to(b_spatial[0][:, :, None], (A_GROUPS, CHUNK, CHUNK))
    w_in_t, m_w_in_t, v_w_in_t = (jnp.transpose(a[0]) for a in (w_in, m_w_in, v_w_in))
    rel_t, m_rel_t, v_rel_t = (jnp.transpose(a) for a in (rel_bias, m_rel_bias, v_rel_bias))

    x_arr = lax.axis_index("x").astype(jnp.int32).reshape(1)
    h_b, parts, (w_in_b, g_mkv, g_out) = _gather_and_project(x2, pre_norm_g, w_in_t, w_mem_kv[0], w_out[0], x_arr)
    w_mkv_b = g_mkv.reshape(D_MODEL, 2 * MEM_WIDTH)
    w_out_b = g_out.reshape(MIX_WIDTH, D_MODEL)

    bias = _make_bias(rel_t, buckets)
    mkv = _memkv_forward(mem, mem_norm_g, w_mkv_b)
    dout, dproj, dmkv, dwout, dvg, dvb, dws, dbs, dsink, drel, loss_vec, dgpost = _mix(
        parts, mkv, x2, tgt2, v_norm_g, v_norm_b, w_sp, b_sp, attn_sinks, bias, w_out_b, post_norm_g, n_ex, seq)

    dx, dgpre = _backward_projection(x2, dout, dproj, pre_norm_g, w_in_b)
    dwmkv, dgmem = _memkv_backward(mem, dmkv, mem_norm_g, w_mkv_b)
    small_a, small_b = _pack_small_grads(dgpre, dgpost, dgmem, dvg, dvb, dws, dbs, dsink, drel, loss_vec, buckets)

    shard_shapes = [w_mem_kv.shape[1:], w_out.shape[1:]]
    big = [g.reshape(N_CHIPS, 2, s[0] // 2, s[1]) for g, s in zip((dwmkv, dwout), shard_shapes)]
    (g_win, g_wmkv, g_wout), (ga, gb) = _reduce_gradients(dproj, h_b, big, [small_a, small_b], shard_arr)

    big_out = [_adamw_whole(g_win, w_in_t, m_w_in_t, v_w_in_t, "adamw_w_in"),
               _adamw_whole(g_wmkv, w_mem_kv[0], m_w_mem_kv[0], v_w_mem_kv[0], "adamw_w_mem_kv"),
               _adamw_whole(g_wout, w_out[0], m_w_out[0], v_w_out[0], "adamw_w_out")]
    small_w = [pre_norm_g, post_norm_g, mem_norm_g, v_norm_g, v_norm_b, w_sp, b_spatial[0], attn_sinks, rel_t]
    small_m = [m_pre_norm_g, m_post_norm_g, m_mem_norm_g, m_v_norm_g, m_v_norm_b, m_w_spatial[0], m_b_spatial[0],
               m_attn_sinks, m_rel_t]
    small_v = [v_pre_norm_g, v_post_norm_g, v_mem_norm_g, v_v_norm_g, v_v_norm_b, v_w_spatial[0], v_b_spatial[0],
               v_attn_sinks, v_rel_t]
    small_out = _adamw_small(ga, gb, small_w, small_m, small_v)
    n_small = len(small_w)

    outputs = [small_out[4 * n_small][0, 0], dx.reshape(x.shape)]
    for kind in range(4):
        s = small_out[kind * n_small:(kind + 1) * n_small]
        outputs += [s[0], s[1], s[2], jnp.transpose(big_out[0][kind])[None], big_out[1][kind][None], s[3], s[4],
                    s[5][None], s[6][None], s[7], jnp.transpose(s[8]), big_out[2][kind][None]]
    return tuple(outputs)
```

```python
import functools

import numpy as np
import jax
import jax.numpy as jnp
from jax import lax
from jax.experimental import pallas as pl
from jax.experimental.pallas import tpu as pltpu

F32 = jnp.float32
BF16 = jnp.bfloat16
MESH = pl.DeviceIdType.MESH

D_MODEL = 1024
CHUNK = 128
A_WIDTH = 512
A_GROUPS = 4
SWA_WIDTH = 256
KV_WIDTH = 128
MEM_WIDTH = 256
MEM_LEN = 256
MIX_WIDTH = 1024
IN_WIDTH = 2816
N_BUCKETS = 32
MAX_DISTANCE = 128
EPS = 1e-6
NEG = -1e30
QK_SCALE = 0.125
HALF_HEAD_PAIR = 64

ADAM_LR = 0.001
ADAM_B1 = 0.9
ADAM_B2 = 0.999
ADAM_EPS = 1e-08
ADAM_WD = 0.01
ADAM_STEP = 10

N_CHIPS = 4
TILE_CHUNKS = 2
TILE = TILE_CHUNKS * CHUNK
PROJ_TILE = 512
VMEM_LIMIT = 56 * 1024 * 1024

SMALL_A_ROWS = 8
ROW_LOSS = 4
ROW_WS = 0
ROW_BS = 512
ROW_SINK = 520
ROW_REL = 528
SMALL_B_ROWS = 536


def _mm(a, b):
    return lax.dot_general(a, b, (((1,), (0,)), ((), ())), preferred_element_type=F32)


def _mm_nt(a, b):
    return lax.dot_general(a, b, (((1,), (1,)), ((), ())), preferred_element_type=F32)


def _mm_tn(a, b):
    return lax.dot_general(a, b, (((0,), (0,)), ((), ())), preferred_element_type=F32)


def _bucket_map():
    qi = np.arange(CHUNK)[:, None]
    kj = np.arange(2 * CHUNK)[None, :]
    n = np.maximum(qi + CHUNK - kj, 0)
    max_exact = N_BUCKETS // 2
    large = max_exact + (np.log(np.maximum(n, 1) / max_exact) / np.log(MAX_DISTANCE / max_exact)
                         * (N_BUCKETS - max_exact)).astype(np.int32)
    large = np.minimum(large, N_BUCKETS - 1)
    return np.where(n < max_exact, n, large).astype(np.int32)


_GELU_C = 0.7978845608028654
_GELU_A = 0.044715
_GELU_K1 = 2.0 * _GELU_C
_GELU_K2 = 2.0 * _GELU_C * _GELU_A


def _gelu(x):
    x2 = x * x
    s = 1.0 / (1.0 + jnp.exp(x * (-_GELU_K1 - _GELU_K2 * x2)))
    return x * s, (s, x2)


def _gelu_grad(x, saved):
    s, x2 = saved
    return s + x * (s * (1.0 - s)) * (_GELU_K1 + 3.0 * _GELU_K2 * x2)


def _sigmoid(x):
    return 1.0 / (1.0 + jnp.exp(-x))


def _lane_lo(shape):
    return lax.broadcasted_iota(jnp.int32, shape, 1) < HALF_HEAD_PAIR


def _swa_variants(t):
    lo = _lane_lo(t.shape)
    tr = pltpu.roll(t, HALF_HEAD_PAIR, 1)
    zero = jnp.zeros_like(t)
    return (jnp.where(lo, t, zero).astype(BF16), jnp.where(lo, zero, tr).astype(BF16),
            jnp.where(lo, tr, zero).astype(BF16), jnp.where(lo, zero, t).astype(BF16))


def _swa_unvariants(d0, d1, d2, d3):
    lo = _lane_lo(d0.shape)
    zero = jnp.zeros_like(d0)
    rolled = jnp.where(lo, zero, d1) + jnp.where(lo, d2, zero)
    return jnp.where(lo, d0, zero) + jnp.where(lo, zero, d3) + pltpu.roll(rolled, HALF_HEAD_PAIR, 1)


def _mem_variants(t):
    out = []
    for pair in range(2):
        tp = t[:, pair * 128:(pair + 1) * 128]
        lo = _lane_lo(tp.shape)
        zero = jnp.zeros_like(tp)
        out.append(jnp.where(lo, tp, zero).astype(BF16))
        out.append(jnp.where(lo, zero, tp).astype(BF16))
    return out


def _mem_unvariants(d0, d1, d2, d3):
    lo = _lane_lo(d0.shape)
    return jnp.concatenate([jnp.where(lo, d0, d1), jnp.where(lo, d2, d3)], axis=-1)


def _softmax(logits, sinks):
    m = jnp.max(logits, axis=-1, keepdims=True)
    if sinks is not None:
        m = jnp.maximum(m, sinks)
    p = jnp.exp(logits - m)
    den = jnp.sum(p, axis=-1, keepdims=True)
    if sinks is None:
        return p * (1.0 / den), None
    es = jnp.exp(sinks - m)
    inv = 1.0 / (den + es)
    return p * inv, es * inv


def _band_valid(with_prev):
    qi = lax.broadcasted_iota(jnp.int32, (CHUNK, 2 * CHUNK), 0)
    kj = lax.broadcasted_iota(jnp.int32, (CHUNK, 2 * CHUNK), 1)
    in_cur = (kj >= CHUNK) & (kj - CHUNK <= qi)
    if not with_prev:
        return in_cur
    return in_cur | ((kj < CHUNK) & (kj > qi))


def _causal_weights(ws_ref):
    row = lax.broadcasted_iota(jnp.int32, (CHUNK, CHUNK), 0)
    col = lax.broadcasted_iota(jnp.int32, (CHUNK, CHUNK), 1)
    return [jnp.where(row >= col, ws_ref[g], 0.0).astype(BF16) for g in range(A_GROUPS)]


def _rows_to_lanes(a, n):
    return jnp.concatenate([a[c * CHUNK:(c + 1) * CHUNK] for c in range(n)], axis=1)


def _lanes_to_rows(a, n):
    w = a.shape[1] // n
    return jnp.concatenate([a[:, c * w:(c + 1) * w] for c in range(n)], axis=0)


def _stack_heads(pair01, pair23):
    return jnp.concatenate([pair01[:, :256], pair01[:, 256:], pair23[:, :256], pair23[:, 256:]], axis=0)


def _pair_heads(s, r):
    return (jnp.concatenate([s[0:r], s[r:2 * r]], axis=1), jnp.concatenate([s[2 * r:3 * r], s[3 * r:4 * r]], axis=1))


def _pair_operands(variants):
    return (jnp.concatenate(variants[0:2], axis=0), jnp.concatenate(variants[2:4], axis=0))


def _split_pair_grads(d_pairs):
    return d_pairs[0][:256], d_pairs[0][256:], d_pairs[1][:256], d_pairs[1][256:]


def _halves_bf16(a):
    return (a[:, :128].astype(BF16), a[:, 128:].astype(BF16))


def _group_a_forward(au, av, vg, vb, wm, bs_rows):
    gu, tu = _gelu(au)
    gv, tv = _gelu(av)
    ya, res = [], []
    for g in range(A_GROUPS):
        sl = slice(g * 128, (g + 1) * 128)
        xg = gv[:, sl]
        xc = xg - jnp.mean(xg, axis=-1, keepdims=True)
        rstd = lax.rsqrt(jnp.mean(xc * xc, axis=-1, keepdims=True) + EPS)
        xhat = xc * rstd
        vn = _rows_to_lanes((xhat * vg[:, sl] + vb[:, sl]).astype(BF16), TILE_CHUNKS)
        s = _lanes_to_rows(_mm(wm[g], vn), TILE_CHUNKS) + bs_rows[g]
        ya.append(gu[:, sl] * s)
        res.append((xhat, rstd, vn, s))
    return ya, dict(gu=gu, tu=tu, tv=tv, groups=res)


def _attention_probs(qp, k_pairs, bias, sink_col):
    logits = _stack_heads(_mm_nt(qp[0], k_pairs[0]), _mm_nt(qp[1], k_pairs[1]))
    if bias is not None:
        logits = logits + bias
    return _softmax(logits, sink_col)


def _attention_out(p, v_pairs, r):
    pp = _pair_heads(p.astype(BF16), r)
    return jnp.concatenate([_mm(pp[0], v_pairs[0]), _mm(pp[1], v_pairs[1])], axis=-1), pp


def _attention_backward(p, pp, do_pairs, qp, k_pairs, v_pairs, r):
    dp = _stack_heads(_mm_nt(do_pairs[0], v_pairs[0]), _mm_nt(do_pairs[1], v_pairs[1]))
    delta = jnp.sum(p * dp, axis=-1, keepdims=True)
    dl = p * (dp - delta)
    dlp = _pair_heads(dl.astype(BF16), r)
    dq = jnp.concatenate([_mm(dlp[0], k_pairs[0]), _mm(dlp[1], k_pairs[1])], axis=-1)
    dk = (_mm_tn(dlp[0], qp[0]), _mm_tn(dlp[1], qp[1]))
    dv = (_mm_tn(pp[0], do_pairs[0]), _mm_tn(pp[1], do_pairs[1]))
    return dl, delta, dq, dk, dv


def _tile_specs(n_tiles_ex, width):
    return pl.BlockSpec((TILE, width), lambda b, i: (b * n_tiles_ex + jnp.minimum(i, n_tiles_ex - 1), 0))


def _prev_chunk_spec(n_tiles_ex, width):
    def index(b, i):
        chunk = TILE_CHUNKS * jnp.minimum(i, n_tiles_ex - 1)
        return (b * n_tiles_ex * TILE_CHUNKS + jnp.maximum(chunk - 1, 0), 0)
    return pl.BlockSpec((CHUNK, width), index)


def _full_spec(shape):
    zeros = (0,) * len(shape)
    return pl.BlockSpec(shape, lambda *_: zeros)


SMEM_SPEC = pl.BlockSpec(memory_space=pltpu.SMEM)
ANY_SPEC = pl.BlockSpec(memory_space=pl.ANY)
VMEM_SPEC = pl.BlockSpec(memory_space=pltpu.VMEM)


def _make_bias(rel_bias_t, buckets):
    def body(rel_ref, bk_ref, out_ref):
        bk = bk_ref[...]
        for h in range(4):
            acc = jnp.zeros((CHUNK, 2 * CHUNK), F32)
            for b in range(N_BUCKETS):
                acc = jnp.where(bk == b, rel_ref[h, b], acc)
            for t, with_prev in enumerate((True, False)):
                out_ref[t, h * CHUNK:(h + 1) * CHUNK, :] = jnp.where(_band_valid(with_prev), acc, NEG)

    return pl.pallas_call(
        body, name="make_bias", out_shape=jax.ShapeDtypeStruct((2, 4 * CHUNK, 2 * CHUNK), F32),
        in_specs=[SMEM_SPEC, VMEM_SPEC], out_specs=VMEM_SPEC,
    )(rel_bias_t, buckets)


def _memkv_forward(mem, g_mem, w_mkv):
    n_ex = mem.shape[0]

    def body(mem_ref, g_ref, w_ref, out_ref):
        m = mem_ref[0]
        r = lax.rsqrt(jnp.mean(m * m, axis=-1, keepdims=True) + EPS)
        out_ref[0] = _mm((m * r * g_ref[...]).astype(BF16), w_ref[...])

    return pl.pallas_call(
        body, name="memkv_forward", grid=(n_ex,),
        out_shape=jax.ShapeDtypeStruct((n_ex, MEM_LEN, 2 * MEM_WIDTH), F32),
        in_specs=[pl.BlockSpec((1, MEM_LEN, D_MODEL), lambda b: (b, 0, 0)), _full_spec((1, D_MODEL)),
                  _full_spec((D_MODEL, 2 * MEM_WIDTH))],
        out_specs=pl.BlockSpec((1, MEM_LEN, 2 * MEM_WIDTH), lambda b: (b, 0, 0)),
    )(mem, g_mem, w_mkv)


PROJ_WIDTHS = (A_WIDTH, A_WIDTH, SWA_WIDTH, KV_WIDTH, KV_WIDTH, MEM_WIDTH, MIX_WIDTH)
PROJ_OFFSETS = tuple(int(v) for v in np.cumsum((0,) + PROJ_WIDTHS))


HALF_WIDTH = IN_WIDTH // 2
HALF_PARTS = ((0, 1, 2, 3), (4, 5, 6))


def _gather_and_project(x2, g_pre, w_in_s, w_mkv_s, w_out_s, x_arr):
    n_tok = x2.shape[0]
    n_tiles = n_tok // PROJ_TILE
    last = n_tiles - 1
    shapes = [w_in_s.shape, w_mkv_s.shape, w_out_s.shape]
    n_w = len(shapes)

    def body(x_sref, x_ref, g_ref, win_hbm, wmkv_hbm, wout_hbm, h_ref, *refs):
        part_refs, refs = refs[:len(PROJ_WIDTHS)], refs[len(PROJ_WIDTHS):]
        gin_hbm, gmkv_hbm, gout_hbm, wg, stage_in, stage_mkv, stage_out, own_mkv, own_out = refs[:9]
        send_sems, recv_sems, local_sems = refs[9:]
        p, t = pl.program_id(0), pl.program_id(1)
        x, y, c = lax.axis_index("x"), lax.axis_index("y"), lax.axis_index("c")
        me, sibling = (x, y, c), (x, y, 1 - c)
        my_shard = 2 * x + y
        gathered = [wg, gmkv_hbm, gout_hbm]

        def half_rows(w, shard, half):
            rows = shapes[w][0] // 2
            if w == 0:
                return wg.at[pl.ds(pl.multiple_of(shard * shapes[0][0] + half * rows, 16), rows), :]
            return gathered[w].at[shard, pl.ds(half * rows, rows), :]

        def first(w, rel):
            src = half_rows(w, my_shard, c) if w == 0 else (own_mkv, own_out)[w - 1].at[
                pl.ds(c * (shapes[w][0] // 2), shapes[w][0] // 2), :]
            k = 3 * w + rel - 1
            return pltpu.make_async_remote_copy(
                src_ref=src, dst_ref=half_rows(w, my_shard, c), send_sem=send_sems.at[k], recv_sem=recv_sems.at[k],
                device_id=(x ^ (rel >> 1), y ^ (rel & 1), c), device_id_type=MESH)

        def landed(w, rel):
            k = 3 * w + rel - 1
            ref = half_rows(w, my_shard ^ rel, c)
            return pltpu.make_async_remote_copy(src_ref=ref, dst_ref=ref, send_sem=send_sems.at[k],
                                                recv_sem=recv_sems.at[k], device_id=me, device_id_type=MESH)

        def passed(w, rel, half, to):
            k = 9 + 3 * w + rel - 1
            ref = half_rows(w, my_shard ^ rel, half)
            return pltpu.make_async_remote_copy(src_ref=ref, dst_ref=ref, send_sem=send_sems.at[k],
                                                recv_sem=recv_sems.at[k], device_id=to, device_id_type=MESH)

        def pass_on(w, rels):
            for rel in rels:
                landed(w, rel).wait_recv()
                passed(w, rel, c, sibling).start()
            for rel in rels:
                passed(w, rel, 1 - c, me).wait_recv()

        own_stores = [pltpu.make_async_copy(own_mkv, gmkv_hbm.at[my_shard], local_sems.at[3]),
                      pltpu.make_async_copy(own_out, gout_hbm.at[my_shard], local_sems.at[4])]

        @pl.when((p == 0) & (t == 0))
        def _():
            loads = [pltpu.make_async_copy(src, dst, local_sems.at[k]) for k, (src, dst) in enumerate(
                ((win_hbm, stage_in), (wmkv_hbm, stage_mkv), (wout_hbm, stage_out)))]
            for cp in loads:
                cp.start()
            loads[0].wait()
            wg[pl.ds(pl.multiple_of(my_shard * shapes[0][0], 16), shapes[0][0]), :] = stage_in[...].astype(BF16)
            for rel in (1, 2):
                first(0, rel).start()
            loads[1].wait()
            loads[2].wait()
            own_mkv[...] = stage_mkv[...].astype(BF16)
            own_out[...] = stage_out[...].astype(BF16)
            for cp in own_stores:
                cp.start()
            pass_on(0, (1,))
            first(0, 3).start()

        @pl.when((p == 1) & (t == 0))
        def _():
            pass_on(0, (2, 3))
            for w in (1, 2):
                for rel in (1, 2, 3):
                    first(w, rel).start()

        xv = x_ref[...]
        r = lax.rsqrt(jnp.mean(xv * xv, axis=-1, keepdims=True) + EPS)
        h = (xv * r * g_ref[...]).astype(BF16)

        @pl.when(p == 0)
        def _():
            h_ref[...] = h

        for hh in range(2):
            @pl.when((p ^ x_sref[0]) == hh)
            def _():
                proj = _mm_nt(h, wg[hh * HALF_WIDTH:(hh + 1) * HALF_WIDTH, :])
                for k in HALF_PARTS[hh]:
                    lo = PROJ_OFFSETS[k] - hh * HALF_WIDTH
                    part_refs[k][...] = proj[:, lo:lo + PROJ_WIDTHS[k]]

        @pl.when((p == 1) & (t == last))
        def _():
            store = pltpu.make_async_copy(wg, gin_hbm, local_sems.at[5])
            store.start()
            for w in (1, 2):
                pass_on(w, (1, 2, 3))
            for w in range(n_w):
                for rel in (1, 2, 3):
                    first(w, rel).wait_send()
                    passed(w, rel, c, sibling).wait_send()
            for cp in own_stores:
                cp.wait()
            store.wait()

    def active_in(hh):
        def index(p, t, xs):
            return (jnp.where((p ^ xs[0]) == hh, t, jnp.where(p == 0, 0, last)), 0)
        return index

    part_specs = [pl.BlockSpec((PROJ_TILE, PROJ_WIDTHS[k]), active_in(hh)) for hh in range(2) for k in HALF_PARTS[hh]]
    vmem = pltpu.VMEM
    out = pl.pallas_call(
        body, name="gather_and_project",
        out_shape=[jax.ShapeDtypeStruct((n_tok, D_MODEL), BF16)]
        + [jax.ShapeDtypeStruct((n_tok, w), F32) for w in PROJ_WIDTHS]
        + [jax.ShapeDtypeStruct((N_CHIPS * shapes[0][0], shapes[0][1]), BF16)]
        + [jax.ShapeDtypeStruct((N_CHIPS,) + s, BF16) for s in shapes[1:]],
        grid_spec=pltpu.PrefetchScalarGridSpec(
            num_scalar_prefetch=1, grid=(2, n_tiles),
            in_specs=[pl.BlockSpec((PROJ_TILE, D_MODEL), lambda p, t, xs: (t, 0)),
                      pl.BlockSpec((1, D_MODEL), lambda p, t, xs: (0, 0)), ANY_SPEC, ANY_SPEC, ANY_SPEC],
            out_specs=[pl.BlockSpec((PROJ_TILE, D_MODEL), lambda p, t, xs: (jnp.where(p == 0, t, last), 0))]
            + part_specs + [ANY_SPEC] * 3,
            scratch_shapes=[vmem((N_CHIPS * shapes[0][0], shapes[0][1]), BF16), vmem(shapes[0], F32),
                            vmem(shapes[1], F32), vmem(shapes[2], F32), vmem(shapes[1], BF16), vmem(shapes[2], BF16),
                            pltpu.SemaphoreType.DMA((18,)), pltpu.SemaphoreType.DMA((18,)),
                            pltpu.SemaphoreType.DMA((6,))]),
        compiler_params=pltpu.CompilerParams(vmem_limit_bytes=VMEM_LIMIT),
    )(x_arr, x2, g_pre, w_in_s, w_mkv_s, w_out_s)
    h, parts, weights = out[0], out[1:1 + len(PROJ_WIDTHS)], out[1 + len(PROJ_WIDTHS):]
    return h, list(parts), weights


def _load_chunk(j, i, sk_ref, sv_ref, skp_ref, svp_ref):
    rows = slice(j * CHUNK, (j + 1) * CHUNK)
    if j == 0:
        k_prev, v_prev, table = skp_ref[...], svp_ref[...], jnp.where(i > 0, 0, 1)
    else:
        prev = slice((j - 1) * CHUNK, j * CHUNK)
        k_prev, v_prev, table = sk_ref[prev, :], sv_ref[prev, :], 0
    k_pairs = _pair_operands(_swa_variants(jnp.concatenate([k_prev, sk_ref[rows, :]], axis=0)))
    v_pairs = _pair_operands(_swa_variants(jnp.concatenate([v_prev, sv_ref[rows, :]], axis=0)))
    return rows, k_pairs, v_pairs, table


def _tile_constants(ws_ref, bs_ref, sink_ref, mkv_ref):
    wm = _causal_weights(ws_ref)
    bs_rows = [jnp.concatenate([bs_ref[g]] * TILE_CHUNKS, axis=0) for g in range(A_GROUPS)]
    sink_col = jnp.max(jnp.concatenate([jnp.full((CHUNK, 128), sink_ref[0, h], F32) for h in range(4)], axis=0),
                       axis=-1, keepdims=True)
    mkv_v = mkv_ref[0]
    mk_pairs = _pair_operands(_mem_variants(mkv_v[:, :MEM_WIDTH]))
    mv_pairs = _pair_operands(_mem_variants(mkv_v[:, MEM_WIDTH:]))
    return wm, bs_rows, sink_col, mk_pairs, mv_pairs


def _mix(parts, mkv, x2, tgt2, v_g, v_b, w_sp, b_sp, sinks, bias, w_out, g_post, n_ex, seq):
    n_tiles_ex = seq // TILE
    n_tok = n_ex * seq
    au, av, sq, sk, sv, mq, z = parts
    col = dict(zip(("au", "av", "sq", "sk", "sv", "mq", "z"),
                   (slice(PROJ_OFFSETS[k], PROJ_OFFSETS[k + 1]) for k in range(len(PROJ_WIDTHS)))))
    before_kv, after_kv = slice(0, col["sk"].start), slice(col["sv"].stop, IN_WIDTH)

    def body(au_ref, av_ref, sq_ref, sk_ref, sv_ref, skp_ref, svp_ref, mq_ref, z_ref, mkv_ref, x_ref, tgt_ref,
             vg_ref, vb_ref, ws_ref, bs_ref, sink_ref, bias_ref, wout_ref, gpost_ref,
             dout_ref, dproj_ref, dmkv_ref, dwout_ref, dvg_ref, dvb_ref, dws_ref, dbs_ref, dsink_ref, drel_ref,
             loss_ref, dgpost_ref, carry_dp, carry_k, carry_v):
        b, i = pl.program_id(0), pl.program_id(1)

        @pl.when((b == 0) & (i == 0))
        def _():
            for ref in (dwout_ref, dvg_ref, dvb_ref, dws_ref, dbs_ref, dsink_ref, drel_ref, loss_ref, dgpost_ref):
                ref[...] = jnp.zeros_like(ref)

        @pl.when(i == 0)
        def _():
            dmkv_ref[...] = jnp.zeros_like(dmkv_ref)
            carry_k[...] = jnp.zeros_like(carry_k)
            carry_v[...] = jnp.zeros_like(carry_v)

        @pl.when(i > 0)
        def _():
            dproj_ref[:, before_kv] = carry_dp[:, before_kv]
            dproj_ref[:, after_kv] = carry_dp[:, after_kv]

        @pl.when(i < n_tiles_ex)
        def _():
            wm, bs_rows, sink_col, mk_pairs, mv_pairs = _tile_constants(ws_ref, bs_ref, sink_ref, mkv_ref)
            vg = vg_ref[...]

            au_v, av_v = au_ref[...], av_ref[...]
            ya, res = _group_a_forward(au_v, av_v, vg, vb_ref[...], wm, bs_rows)
            swa, yb = [], []
            for j in range(TILE_CHUNKS):
                rows, k_pairs, v_pairs, table = _load_chunk(j, i, sk_ref, sv_ref, skp_ref, svp_ref)
                qp = _halves_bf16(sq_ref[rows, :] * QK_SCALE)
                p, ps = _attention_probs(qp, k_pairs, bias_ref[table], sink_col)
                out, pp = _attention_out(p, v_pairs, CHUNK)
                yb.append(out)
                swa.append((rows, k_pairs, v_pairs, qp, p, ps, pp))
            mqp = _halves_bf16(mq_ref[...] * QK_SCALE)
            pm, _ = _attention_probs(mqp, mk_pairs, None, None)
            yc, ppm = _attention_out(pm, mv_pairs, TILE)
            ycat = jnp.concatenate(ya + [jnp.concatenate(yb, axis=0), yc], axis=-1)

            zv = z_ref[...]
            sig = _sigmoid(zv)
            sz = zv * sig
            y_b = (ycat * sz).astype(BF16)
            o = _mm(y_b, wout_ref[...])
            r2 = lax.rsqrt(jnp.mean(o * o, axis=-1, keepdims=True) + EPS)
            nrm = o * r2
            gp = gpost_ref[...]
            diff = x_ref[...] + nrm * gp - tgt_ref[...]
            loss_ref[...] += jnp.sum(diff * diff) * (0.5 / D_MODEL)
            dout = diff * (1.0 / D_MODEL)
            dout_ref[...] = dout
            dgpost_ref[...] += jnp.sum(dout * nrm, axis=0, keepdims=True)
            dn = dout * gp
            do_b = (r2 * (dn - nrm * jnp.mean(dn * nrm, axis=-1, keepdims=True))).astype(BF16)
            dwout_ref[...] += _mm_tn(y_b, do_b)
            dy = _mm_nt(do_b, wout_ref[...])
            carry_dp[:, col["z"]] = (dy * ycat * (sig * (1.0 + zv * (1.0 - sig)))).astype(BF16)
            dyc = dy * sz

            dgu, dgv = [], []
            for g in range(A_GROUPS):
                sl = slice(g * 128, (g + 1) * 128)
                xhat, rstd, vn, s = res["groups"][g]
                dya = dyc[:, sl]
                dgu.append(dya * s)
                ds = dya * res["gu"][:, sl]
                dbs_ref[:, sl] += sum(ds[c * CHUNK:(c + 1) * CHUNK] for c in range(TILE_CHUNKS))
                ds_b = _rows_to_lanes(ds.astype(BF16), TILE_CHUNKS)
                dws_ref[g] += _mm_nt(ds_b, vn)
                dvn = _lanes_to_rows(_mm_tn(wm[g], ds_b), TILE_CHUNKS)
                dvg_ref[:, sl] += jnp.sum(dvn * xhat, axis=0, keepdims=True)
                dvb_ref[:, sl] += jnp.sum(dvn, axis=0, keepdims=True)
                dxh = dvn * vg[:, sl]
                dgv.append(rstd * (dxh - jnp.mean(dxh, axis=-1, keepdims=True)
                                   - xhat * jnp.mean(dxh * xhat, axis=-1, keepdims=True)))
            carry_dp[:, col["au"]] = (jnp.concatenate(dgu, axis=-1) * _gelu_grad(au_v, res["tu"])).astype(BF16)
            carry_dp[:, col["av"]] = (jnp.concatenate(dgv, axis=-1) * _gelu_grad(av_v, res["tv"])).astype(BF16)

            lane4 = lax.broadcasted_iota(jnp.int32, (1, 128), 1)
            dsink_vec = jnp.zeros((1, 128), F32)
            dk_parts, dv_parts = [], []
            for rows, k_pairs, v_pairs, qp, p, ps, pp in swa:
                do_pairs = _halves_bf16(dyc[rows, A_WIDTH:A_WIDTH + SWA_WIDTH])
                dl, delta, dq, dk, dv = _attention_backward(p, pp, do_pairs, qp, k_pairs, v_pairs, CHUNK)
                sink_terms = ps * delta
                for h in range(4):
                    dsink_vec = dsink_vec + jnp.where(lane4 == h, -jnp.sum(sink_terms[h * CHUNK:(h + 1) * CHUNK]), 0.0)
                drel_ref[...] += dl
                carry_dp[rows, col["sq"]] = (dq * QK_SCALE).astype(BF16)
                dk_parts.append(_swa_unvariants(*_split_pair_grads(dk)))
                dv_parts.append(_swa_unvariants(*_split_pair_grads(dv)))
            dsink_ref[...] += dsink_vec

            dc_pairs = _halves_bf16(dyc[:, A_WIDTH + SWA_WIDTH:])
            _, _, dmq, dmk, dmv = _attention_backward(pm, ppm, dc_pairs, mqp, mk_pairs, mv_pairs, TILE)
            carry_dp[:, col["mq"]] = (dmq * QK_SCALE).astype(BF16)
            dmkv_ref[0] += jnp.concatenate([_mem_unvariants(*_split_pair_grads(dmk)),
                                            _mem_unvariants(*_split_pair_grads(dmv))], axis=-1)

            for parts_c, carry, cols in ((dk_parts, carry_k, col["sk"]), (dv_parts, carry_v, col["sv"])):
                @pl.when(i > 0)
                def _():
                    dproj_ref[:, cols] = (carry[...] + jnp.concatenate(
                        [jnp.zeros((TILE - CHUNK, KV_WIDTH), F32), parts_c[0][:CHUNK]], axis=0)).astype(BF16)
                new = [parts_c[0][CHUNK:]]
                for j in range(1, TILE_CHUNKS):
                    new[-1] = new[-1] + parts_c[j][:CHUNK]
                    new.append(parts_c[j][CHUNK:])
                carry[...] = jnp.concatenate(new, axis=0)

        @pl.when(i == n_tiles_ex)
        def _():
            dproj_ref[:, col["sk"]] = carry_k[...].astype(BF16)
            dproj_ref[:, col["sv"]] = carry_v[...].astype(BF16)

    tile = functools.partial(_tile_specs, n_tiles_ex)
    prev = functools.partial(_prev_chunk_spec, n_tiles_ex)
    late = pl.BlockSpec((TILE, IN_WIDTH), lambda b, i: (b * n_tiles_ex + jnp.maximum(i - 1, 0), 0))
    return pl.pallas_call(
        body, name="mix", grid=(n_ex, n_tiles_ex + 1),
        out_shape=[jax.ShapeDtypeStruct((n_tok, D_MODEL), F32), jax.ShapeDtypeStruct((n_tok, IN_WIDTH), BF16),
                   jax.ShapeDtypeStruct((n_ex, MEM_LEN, 2 * MEM_WIDTH), F32),
                   jax.ShapeDtypeStruct((MIX_WIDTH, D_MODEL), F32), jax.ShapeDtypeStruct((1, A_WIDTH), F32),
                   jax.ShapeDtypeStruct((1, A_WIDTH), F32), jax.ShapeDtypeStruct((A_GROUPS, CHUNK, CHUNK), F32),
                   jax.ShapeDtypeStruct((CHUNK, A_WIDTH), F32), jax.ShapeDtypeStruct((1, 128), F32),
                   jax.ShapeDtypeStruct((4 * CHUNK, 2 * CHUNK), F32), jax.ShapeDtypeStruct((1, 128), F32),
                   jax.ShapeDtypeStruct((1, D_MODEL), F32)],
        in_specs=[tile(A_WIDTH), tile(A_WIDTH), tile(SWA_WIDTH), tile(KV_WIDTH), tile(KV_WIDTH),
                  prev(KV_WIDTH), prev(KV_WIDTH), tile(MEM_WIDTH), tile(MIX_WIDTH),
                  pl.BlockSpec((1, MEM_LEN, 2 * MEM_WIDTH), lambda b, i: (b, 0, 0)),
                  tile(D_MODEL), tile(D_MODEL),
                  _full_spec((1, A_WIDTH)), _full_spec((1, A_WIDTH)), _full_spec((A_GROUPS, CHUNK, CHUNK)),
                  _full_spec((A_GROUPS, CHUNK, CHUNK)), SMEM_SPEC, _full_spec((2, 4 * CHUNK, 2 * CHUNK)),
                  _full_spec((MIX_WIDTH, D_MODEL)), _full_spec((1, D_MODEL))],
        out_specs=[tile(D_MODEL), late, pl.BlockSpec((1, MEM_LEN, 2 * MEM_WIDTH), lambda b, i: (b, 0, 0)),
                   _full_spec((MIX_WIDTH, D_MODEL)), _full_spec((1, A_WIDTH)), _full_spec((1, A_WIDTH)),
                   _full_spec((A_GROUPS, CHUNK, CHUNK)), _full_spec((CHUNK, A_WIDTH)), _full_spec((1, 128)),
                   _full_spec((4 * CHUNK, 2 * CHUNK)), _full_spec((1, 128)), _full_spec((1, D_MODEL))],
        scratch_shapes=[pltpu.VMEM((TILE, IN_WIDTH), BF16), pltpu.VMEM((TILE, KV_WIDTH), F32),
                        pltpu.VMEM((TILE, KV_WIDTH), F32)],
        compiler_params=pltpu.CompilerParams(vmem_limit_bytes=VMEM_LIMIT),
    )(au, av, sq, sk, sv, sk, sv, mq, z, mkv, x2, tgt2, v_g, v_b, w_sp, b_sp, sinks, bias, w_out, g_post)


BWD_PROJ_TILE = 512


def _backward_projection(x2, dout, dproj, g_pre, w_in_t):
    n_tok = x2.shape[0]
    n_steps = n_tok // BWD_PROJ_TILE

    def body(x_ref, dout_ref, dp_ref, g_ref, w_hbm, dx_ref, dgpre_ref, w_vmem, sem):
        @pl.when(pl.program_id(0) == 0)
        def _():
            load = pltpu.make_async_copy(w_hbm, w_vmem, sem)
            load.start()
            dgpre_ref[...] = jnp.zeros_like(dgpre_ref)
            load.wait()

        xv = x_ref[...]
        r = lax.rsqrt(jnp.mean(xv * xv, axis=-1, keepdims=True) + EPS)
        xn = xv * r
        dh = _mm(dp_ref[...], w_vmem[...])
        dgpre_ref[...] += jnp.sum(dh * xn, axis=0, keepdims=True)
        dhg = dh * g_ref[...]
        dx_ref[...] = r * (dhg - xn * jnp.mean(dhg * xn, axis=-1, keepdims=True)) + dout_ref[...]

    row = lambda w: pl.BlockSpec((BWD_PROJ_TILE, w), lambda i: (i, 0))
    return pl.pallas_call(
        body, name="backward_projection", grid=(n_steps,),
        out_shape=[jax.ShapeDtypeStruct((n_tok, D_MODEL), F32), jax.ShapeDtypeStruct((1, D_MODEL), F32)],
        in_specs=[row(D_MODEL), row(D_MODEL), row(IN_WIDTH), _full_spec((1, D_MODEL)), ANY_SPEC],
        out_specs=[row(D_MODEL), _full_spec((1, D_MODEL))],
        scratch_shapes=[pltpu.VMEM((IN_WIDTH, D_MODEL), BF16), pltpu.SemaphoreType.DMA],
        input_output_aliases={1: 0},
        compiler_params=pltpu.CompilerParams(vmem_limit_bytes=VMEM_LIMIT),
    )(x2, dout, dproj, g_pre, w_in_t)


SHARD_ROWS = IN_WIDTH // N_CHIPS
SHARD_WINDOW = 768
SHARD_HALF = SHARD_ROWS // 2
DWIN_TILE = 2048
N_REL = N_CHIPS - 1


def _shard_window_start(shard):
    return (shard * SHARD_ROWS // 128) * 128


def _reduce_gradients(dproj, h, big, small, shard_arr):
    n_tok = h.shape[0]
    tile = min(DWIN_TILE, n_tok)
    n_sub = n_tok // tile
    last = N_CHIPS - 1
    n_big, n_small = len(big), len(small)
    big_half = [g.shape[2:] for g in big]
    sem_big_d2d = 2 * N_CHIPS
    sem_big_ici = sem_big_d2d + n_big
    sem_big_swap = sem_big_ici + N_REL * n_big
    sem_small_d2d = sem_big_swap + n_big
    sem_small_ici = sem_small_d2d + n_small
    n_sems = sem_small_ici + N_REL * n_small
    loc_small = n_big
    loc_out_win = loc_small + n_small
    loc_out_big = loc_out_win + 2
    loc_out_small = loc_out_big + 2 * n_big
    n_local = loc_out_small + n_small

    def relation_of_slot(s):
        return (s + 2) % N_REL + 1

    def shard_of_slot(s, my_shard):
        return my_shard ^ jnp.where(s == last, 0, relation_of_slot(s))

    def body(shard_ref, dp_ref, h_hbm, *refs):
        h_vmem, h_sem, refs = refs[-2], refs[-1], refs[:-2]
        big_hbm, refs = refs[:n_big], refs[n_big:]
        small_hbm, refs = refs[:n_small], refs[n_small:]
        out_hbm, refs = refs[0], refs[1:]
        big_out, refs = refs[:n_big], refs[n_big:]
        small_out, refs = refs[:n_small], refs[n_small:]
        part, recv_d2d, send_ici, recv_ici, mine_buf, other_buf = refs[:6]
        refs = refs[6:]
        big_own, big_recv, big_send, big_land, big_mine, big_other = (
            refs[k * n_big:(k + 1) * n_big] for k in range(6))
        refs = refs[6 * n_big:]
        small_own, small_recv, small_all = (refs[k * n_small:(k + 1) * n_small] for k in range(3))
        send_sems, recv_sems, local_sems = refs[3 * n_small:]

        s, t = pl.program_id(0), pl.program_id(1)
        x, y, c = lax.axis_index("x"), lax.axis_index("y"), lax.axis_index("c")
        my_chip = 2 * x + y
        sibling = (x, y, 1 - c)
        my_rows = pl.ds(pl.multiple_of(c * SHARD_HALF, 8), SHARD_HALF)
        other_rows = pl.ds(pl.multiple_of((1 - c) * SHARD_HALF, 8), SHARD_HALF)

        def remote(src, dst, k, to):
            return pltpu.make_async_remote_copy(src_ref=src, dst_ref=dst, send_sem=send_sems.at[k],
                                                recv_sem=recv_sems.at[k], device_id=to, device_id_type=MESH)

        def chip_at(rel):
            return (x ^ (rel >> 1), y ^ (rel & 1), c)

        def to_sibling(k):
            return remote(part.at[k % 2, other_rows, :], recv_d2d.at[k], k, sibling)

        def to_chip(k):
            return remote(send_ici.at[k], recv_ici.at[k], N_CHIPS + k, chip_at(relation_of_slot(k)))

        swap = remote(mine_buf, other_buf, 2 * N_CHIPS - 1, sibling)
        big_load = [pltpu.make_async_copy(big_hbm[w].at[:, pl.ds(c, 1)], big_own[w], local_sems.at[w])
                    for w in range(n_big)]
        big_to_sibling = [remote(big_hbm[w].at[:, pl.ds(1 - c, 1)], big_recv[w], sem_big_d2d + w, sibling)
                          for w in range(n_big)]
        big_to_chip = [[remote(big_send[w].at[k], big_land[w].at[k], sem_big_ici + N_REL * w + k, chip_at(k + 1))
                        for k in range(N_REL)] for w in range(n_big)]
        big_swap = [remote(big_mine[w], big_other[w], sem_big_swap + w, sibling) for w in range(n_big)]
        small_load = [pltpu.make_async_copy(small_hbm[i], small_own[i], local_sems.at[loc_small + i])
                      for i in range(n_small)]
        small_to_sibling = [remote(small_hbm[i], small_recv[i], sem_small_d2d + i, sibling) for i in range(n_small)]
        small_to_chip = [[remote(small_all[i].at[my_chip], small_all[i].at[my_chip],
                                 sem_small_ici + N_REL * i + k, chip_at(k + 1))
                          for k in range(N_REL)] for i in range(n_small)]

        @pl.when((s == 0) & (t == 0))
        def _():
            h_load = pltpu.make_async_copy(h_hbm, h_vmem, h_sem)
            h_load.start()
            for cp in big_load + big_to_sibling + small_load + small_to_sibling:
                cp.start()
            h_load.wait()

        @pl.when((s == 0) & (t == n_sub - 1))
        def _():
            for cp in big_load + small_load:
                cp.wait()
            for cp in big_to_sibling + small_to_sibling:
                cp.wait_recv()
                cp.wait_send()
            for w in range(n_big):
                for k in range(N_REL):
                    shard = my_chip ^ (k + 1)
                    big_send[w][k] = (big_own[w][shard, 0] + big_recv[w][shard, 0]).astype(BF16)
                    big_to_chip[w][k].start()
            for i in range(n_small):
                small_all[i][my_chip] = small_own[i][...] + small_recv[i][...]
                for k in range(N_REL):
                    small_to_chip[i][k].start()

        @pl.when((s > 0) & (t == 0))
        def _():
            k = s - 1
            cp = to_sibling(k)
            cp.wait_recv()
            cp.wait_send()
            send_ici[k] = (part[k % 2, my_rows, :] + recv_d2d[k]).astype(BF16)
            to_chip(k).start()

        r = _mm_tn(dp_ref[...], h_vmem[pl.ds(pl.multiple_of(t * tile, tile), tile), :])
        odd = shard_of_slot(s, shard_ref[0]) % 2
        for parity in range(2):
            rows = r[64 * parity:64 * parity + SHARD_ROWS]

            @pl.when((odd == parity) & (t == 0))
            def _():
                part[s % 2] = rows

            @pl.when((odd == parity) & (t > 0))
            def _():
                part[s % 2] += rows

        @pl.when(t == n_sub - 1)
        def _():
            to_sibling(s).start()

        @pl.when((s == last) & (t == n_sub - 1))
        def _():
            cp = to_sibling(last)
            cp.wait_recv()
            cp.wait_send()
            total = part[last % 2, my_rows, :] + recv_d2d[last]
            for k in range(last):
                to_chip(k).wait_recv()
                total = total + recv_ici[k].astype(F32)
            mine_buf[...] = total
            swap.start()
            out_mine = pltpu.make_async_copy(mine_buf, out_hbm.at[my_rows, :], local_sems.at[0])
            out_mine.start()
            swap.wait_recv()
            out_other = pltpu.make_async_copy(other_buf, out_hbm.at[other_rows, :], local_sems.at[1])
            out_other.start()
            stores = [out_mine, out_other]
            for w in range(n_big):
                rows = big_half[w][0]
                total = big_own[w][my_chip, 0] + big_recv[w][my_chip, 0]
                for k in range(N_REL):
                    big_to_chip[w][k].wait_recv()
                    total = total + big_land[w][k].astype(F32)
                big_mine[w][...] = total
                big_swap[w].start()
                stores.append(pltpu.make_async_copy(
                    big_mine[w], big_out[w].at[pl.ds(pl.multiple_of(c * rows, 8), rows), :],
                    local_sems.at[loc_out_big + 2 * w]))
                stores[-1].start()
            for w in range(n_big):
                rows = big_half[w][0]
                big_swap[w].wait_recv()
                stores.append(pltpu.make_async_copy(
                    big_other[w], big_out[w].at[pl.ds(pl.multiple_of((1 - c) * rows, 8), rows), :],
                    local_sems.at[loc_out_big + 2 * w + 1]))
                stores[-1].start()
            for i in range(n_small):
                for k in range(N_REL):
                    small_to_chip[i][k].wait_recv()
                stores.append(pltpu.make_async_copy(small_all[i], small_out[i], local_sems.at[loc_out_small + i]))
                stores[-1].start()
            for k in range(last):
                to_chip(k).wait_send()
            swap.wait_send()
            for w in range(n_big):
                for k in range(N_REL):
                    big_to_chip[w][k].wait_send()
                big_swap[w].wait_send()
            for i in range(n_small):
                for k in range(N_REL):
                    small_to_chip[i][k].wait_send()
            for cp in stores:
                cp.wait()

    half = (SHARD_HALF, D_MODEL)
    vmem = pltpu.VMEM
    scratch = [vmem((2, SHARD_ROWS, D_MODEL), F32), vmem((N_CHIPS,) + half, F32),
               vmem((N_REL,) + half, BF16), vmem((N_REL,) + half, BF16), vmem(half, F32), vmem(half, F32)]
    scratch += [vmem((N_CHIPS, 1) + hs, F32) for hs in big_half] * 2
    scratch += [vmem((N_REL,) + hs, BF16) for hs in big_half] * 2
    scratch += [vmem(hs, F32) for hs in big_half] * 2
    scratch += [vmem(a.shape, F32) for a in small] * 2 + [vmem((N_CHIPS,) + a.shape, F32) for a in small]
    scratch += [pltpu.SemaphoreType.DMA((n_sems,)), pltpu.SemaphoreType.DMA((n_sems,)),
                pltpu.SemaphoreType.DMA((n_local,)), vmem(h.shape, BF16), pltpu.SemaphoreType.DMA]
    n_hbm = n_big + n_small
    out = pl.pallas_call(
        body, name="reduce_gradients",
        out_shape=[jax.ShapeDtypeStruct((SHARD_ROWS, D_MODEL), F32)]
        + [jax.ShapeDtypeStruct((2 * hs[0], hs[1]), F32) for hs in big_half]
        + [jax.ShapeDtypeStruct((N_CHIPS,) + a.shape, F32) for a in small],
        grid_spec=pltpu.PrefetchScalarGridSpec(
            num_scalar_prefetch=1, grid=(N_CHIPS, n_sub),
            in_specs=[pl.BlockSpec((pl.Element(tile), pl.Element(SHARD_WINDOW)),
                                   lambda s, t, m: (t * tile, _shard_window_start(shard_of_slot(s, m[0])))),
                      ANY_SPEC] + [ANY_SPEC] * n_hbm,
            out_specs=[ANY_SPEC] * (1 + n_hbm),
            scratch_shapes=scratch),
        compiler_params=pltpu.CompilerParams(vmem_limit_bytes=VMEM_LIMIT),
    )(shard_arr, dproj, h, *big, *small)
    return out[:1 + n_big], out[1 + n_big:]


def _memkv_backward(mem, dmkv, g_mem, w_mkv):
    n_ex = mem.shape[0]

    def body(mem_ref, d_ref, g_ref, w_ref, dw_ref, dg_ref):
        @pl.when(pl.program_id(0) == 0)
        def _():
            dw_ref[...] = jnp.zeros_like(dw_ref)
            dg_ref[...] = jnp.zeros_like(dg_ref)

        m = mem_ref[0]
        mn = m * lax.rsqrt(jnp.mean(m * m, axis=-1, keepdims=True) + EPS)
        d_b = d_ref[0].astype(BF16)
        dw_ref[...] += _mm_tn((mn * g_ref[...]).astype(BF16), d_b)
        dg_ref[...] += jnp.sum(_mm_nt(d_b, w_ref[...]) * mn, axis=0, keepdims=True)

    return pl.pallas_call(
        body, name="memkv_backward", grid=(n_ex,),
        out_shape=[jax.ShapeDtypeStruct((D_MODEL, 2 * MEM_WIDTH), F32), jax.ShapeDtypeStruct((1, D_MODEL), F32)],
        in_specs=[pl.BlockSpec((1, MEM_LEN, D_MODEL), lambda b: (b, 0, 0)),
                  pl.BlockSpec((1, MEM_LEN, 2 * MEM_WIDTH), lambda b: (b, 0, 0)),
                  _full_spec((1, D_MODEL)), _full_spec((D_MODEL, 2 * MEM_WIDTH))],
        out_specs=[_full_spec((D_MODEL, 2 * MEM_WIDTH)), _full_spec((1, D_MODEL))],
    )(mem, dmkv, g_mem, w_mkv)


def _pack_small_grads(dgpre, dgpost, dgmem, dvg, dvb, dws, dbs, dsink, drel, loss_vec, buckets):
    def body(dgpre_ref, dgpost_ref, dgmem_ref, dvg_ref, dvb_ref, dws_ref, dbs_ref, dsink_ref, drel_ref, loss_ref,
             bk_ref, a_ref, b_ref):
        a_ref[...] = jnp.zeros_like(a_ref)
        b_ref[...] = jnp.zeros_like(b_ref)
        a_ref[0:1, :] = dgpre_ref[...]
        a_ref[1:2, :] = dgpost_ref[...]
        a_ref[2:3, :] = dgmem_ref[...]
        a_ref[3:4, :] = jnp.concatenate([dvg_ref[...], dvb_ref[...]], axis=-1)
        a_ref[ROW_LOSS:ROW_LOSS + 1, 0:128] = loss_ref[...]
        row = lax.broadcasted_iota(jnp.int32, (CHUNK, CHUNK), 0)
        col = lax.broadcasted_iota(jnp.int32, (CHUNK, CHUNK), 1)
        for g in range(A_GROUPS):
            b_ref[ROW_WS + g * CHUNK:ROW_WS + (g + 1) * CHUNK, :] = jnp.where(row >= col, dws_ref[g], 0.0)
            by_token = jnp.transpose(dbs_ref[:, g * 128:(g + 1) * 128])
            b_ref[ROW_BS + g:ROW_BS + g + 1, :] = jnp.sum(by_token, axis=0, keepdims=True)
        b_ref[ROW_SINK:ROW_SINK + 1, :] = dsink_ref[...]
        bk = bk_ref[...]
        rel_row = lax.broadcasted_iota(jnp.int32, (8, 128), 0)
        rel_col = lax.broadcasted_iota(jnp.int32, (8, 128), 1)
        rel = jnp.zeros((8, 128), F32)
        for h in range(4):
            acc = drel_ref[h * CHUNK:(h + 1) * CHUNK, :]
            for b in range(N_BUCKETS):
                rel = jnp.where((rel_row == h) & (rel_col == b), jnp.sum(jnp.where(bk == b, acc, 0.0)), rel)
        b_ref[ROW_REL:ROW_REL + 8, :] = rel

    return pl.pallas_call(
        body, name="pack_small_grads",
        out_shape=[jax.ShapeDtypeStruct((SMALL_A_ROWS, D_MODEL), F32), jax.ShapeDtypeStruct((SMALL_B_ROWS, 128), F32)],
        in_specs=[VMEM_SPEC] * 11, out_specs=[VMEM_SPEC] * 2,
    )(dgpre, dgpost, dgmem, dvg, dvb, dws, dbs, dsink, drel, loss_vec, buckets)


def _adamw(w, g, m, v):
    m2 = ADAM_B1 * m + (1.0 - ADAM_B1) * g
    v2 = ADAM_B2 * v + (1.0 - ADAM_B2) * (g * g)
    m_hat = m2 / (1.0 - ADAM_B1 ** ADAM_STEP)
    v_hat = v2 / (1.0 - ADAM_B2 ** ADAM_STEP)
    delta = -ADAM_LR * (m_hat / (jnp.sqrt(v_hat) + ADAM_EPS) + ADAM_WD * w)
    return delta, m2, v2


ADAM_MAX_ROWS = 176


def _adamw_whole(g, w, m, v, name):
    rows, cols = w.shape
    steps = -(-rows // ADAM_MAX_ROWS)
    block_rows = rows // steps
    assert block_rows * steps == rows and block_rows % 8 == 0

    def body(g_ref, w_ref, m_ref, v_ref, d_out, m_out, v_out):
        delta, m2, v2 = _adamw(w_ref[...], g_ref[...], m_ref[...], v_ref[...])
        d_out[...] = delta
        m_out[...] = m2
        v_out[...] = v2

    block = pl.BlockSpec((block_rows, cols), lambda k: (k, 0))
    out = pl.pallas_call(
        body, name=name, grid=(steps,), out_shape=[jax.ShapeDtypeStruct((rows, cols), F32)] * 3,
        in_specs=[block] * 4, out_specs=[block] * 3,
    )(g, w, m, v)
    return [g] + list(out)


def _adamw_small(ra, rb, weights, moments_m, moments_v):
    n = len(weights)

    def body(*refs):
        ra_ref, rb_ref = refs[0], refs[1]
        w_refs, m_refs, v_refs = refs[2:2 + n], refs[2 + n:2 + 2 * n], refs[2 + 2 * n:2 + 3 * n]
        outs = refs[2 + 3 * n:]
        g_outs, d_outs, m_outs, v_outs = outs[:n], outs[n:2 * n], outs[2 * n:3 * n], outs[3 * n:4 * n]
        ga, gb = ra_ref[0], rb_ref[0]
        for chip in range(1, N_CHIPS):
            ga = ga + ra_ref[chip]
            gb = gb + rb_ref[chip]
        outs[4 * n][...] = ga[ROW_LOSS:ROW_LOSS + 1, 0:128]
        grads = [ga[0:1, :], ga[1:2, :], ga[2:3, :], ga[3:4, :A_WIDTH], ga[3:4, A_WIDTH:],
                 gb[ROW_WS:ROW_WS + A_GROUPS * CHUNK, :].reshape(A_GROUPS, CHUNK, CHUNK),
                 gb[ROW_BS:ROW_BS + A_GROUPS, :], gb[ROW_SINK:ROW_SINK + 1, 0:4],
                 gb[ROW_REL:ROW_REL + 4, 0:N_BUCKETS]]
        for k in range(n):
            delta, m2, v2 = _adamw(w_refs[k][...], grads[k], m_refs[k][...], v_refs[k][...])
            g_outs[k][...] = grads[k]
            d_outs[k][...] = delta
            m_outs[k][...] = m2
            v_outs[k][...] = v2

    out_shape = [jax.ShapeDtypeStruct(w.shape, F32) for w in weights] * 4 + [jax.ShapeDtypeStruct((1, 128), F32)]
    return pl.pallas_call(
        body, name="adamw_small", out_shape=out_shape,
        in_specs=[VMEM_SPEC] * (2 + 3 * n), out_specs=[VMEM_SPEC] * (4 * n + 1),
    )(ra, rb, *weights, *moments_m, *moments_v)


def kernel(x, mem, pre_norm_g, post_norm_g, mem_norm_g, w_in, w_mem_kv, v_norm_g, v_norm_b, w_spatial, b_spatial, attn_sinks, rel_bias, w_out, loss_target, m_pre_norm_g, m_post_norm_g, m_mem_norm_g, m_w_in, m_w_mem_kv, m_v_norm_g, m_v_norm_b, m_w_spatial, m_b_spatial, m_attn_sinks, m_rel_bias, m_w_out, v_pre_norm_g, v_post_norm_g, v_mem_norm_g, v_w_in, v_w_mem_kv, v_v_norm_g, v_v_norm_b, v_w_spatial, v_b_spatial, v_attn_sinks, v_rel_bias, v_w_out):
    n_ex, seq, _ = x.shape
    n_tok = n_ex * seq
    x2 = x.reshape(n_tok, D_MODEL)
    tgt2 = loss_target.reshape(n_tok, D_MODEL)
    buckets = jnp.asarray(_bucket_map())
    shard_arr = (2 * lax.axis_index("x") + lax.axis_index("y")).astype(jnp.int32).reshape(1)
    w_sp = w_spatial[0]
    b_sp = jnp.broadcast_to(b_spatial[0][:, :, None], (A_GROUPS, CHUNK, CHUNK))
    w_in_t, m_w_in_t, v_w_in_t = (jnp.transpose(a[0]) for a in (w_in, m_w_in, v_w_in))
    rel_t, m_rel_t, v_rel_t = (jnp.transpose(a) for a in (rel_bias, m_rel_bias, v_rel_bias))

    x_arr = lax.axis_index("x").astype(jnp.int32).reshape(1)
    h_b, parts, (w_in_b, g_mkv, g_out) = _gather_and_project(x2, pre_norm_g, w_in_t, w_mem_kv[0], w_out[0], x_arr)
    w_mkv_b = g_mkv.reshape(D_MODEL, 2 * MEM_WIDTH)
    w_out_b = g_out.reshape(MIX_WIDTH, D_MODEL)

    bias = _make_bias(rel_t, buckets)
    mkv = _memkv_forward(mem, mem_norm_g, w_mkv_b)
    dout, dproj, dmkv, dwout, dvg, dvb, dws, dbs, dsink, drel, loss_vec, dgpost = _mix(
        parts, mkv, x2, tgt2, v_norm_g, v_norm_b, w_sp, b_sp, attn_sinks, bias, w_out_b, post_norm_g, n_ex, seq)

    dx, dgpre = _backward_projection(x2, dout, dproj, pre_norm_g, w_in_b)
    dwmkv, dgmem = _memkv_backward(mem, dmkv, mem_norm_g, w_mkv_b)
    small_a, small_b = _pack_small_grads(dgpre, dgpost, dgmem, dvg, dvb, dws, dbs, dsink, drel, loss_vec, buckets)

    shard_shapes = [w_mem_kv.shape[1:], w_out.shape[1:]]
    big = [g.reshape(N_CHIPS, 2, s[0] // 2, s[1]) for g, s in zip((dwmkv, dwout), shard_shapes)]
    (g_win, g_wmkv, g_wout), (ga, gb) = _reduce_gradients(dproj, h_b, big, [small_a, small_b], shard_arr)

    big_out = [_adamw_whole(g_win, w_in_t, m_w_in_t, v_w_in_t, "adamw_w_in"),
               _adamw_whole(g_wmkv, w_mem_kv[0], m_w_mem_kv[0], v_w_mem_kv[0], "adamw_w_mem_kv"),
               _adamw_whole(g_wout, w_out[0], m_w_out[0], v_w_out[0], "adamw_w_out")]
    small_w = [pre_norm_g, post_norm_g, mem_norm_g, v_norm_g, v_norm_b, w_sp, b_spatial[0], attn_sinks, rel_t]
    small_m = [m_pre_norm_g, m_post_norm_g, m_mem_norm_g, m_v_norm_g, m_v_norm_b, m_w_spatial[0], m_b_spatial[0],
               m_attn_sinks, m_rel_t]
    small_v = [v_pre_norm_g, v_post_norm_g, v_mem_norm_g, v_v_norm_g, v_v_norm_b, v_w_spatial[0], v_b_spatial[0],
               v_attn_sinks, v_rel_t]
    small_out = _adamw_small(ga, gb, small_w, small_m, small_v)
    n_small = len(small_w)

    outputs = [small_out[4 * n_small][0, 0], dx.reshape(x.shape)]
    for kind in range(4):
        s = small_out[kind * n_small:(kind + 1) * n_small]
        outputs += [s[0], s[1], s[2], jnp.transpose(big_out[0][kind])[None], big_out[1][kind][None], s[3], s[4],
                    s[5][None], s[6][None], s[7], jnp.transpose(s[8]), big_out[2][kind][None]]
    return tuple(outputs)
```

```python
import functools

import numpy as np
import jax
import jax.numpy as jnp
from jax import lax
from jax.experimental import pallas as pl
from jax.experimental.pallas import tpu as pltpu

F32 = jnp.float32
BF16 = jnp.bfloat16
MESH = pl.DeviceIdType.MESH

D_MODEL = 1024
CHUNK = 128
A_WIDTH = 512
A_GROUPS = 4
SWA_WIDTH = 256
KV_WIDTH = 128
MEM_WIDTH = 256
MEM_LEN = 256
MIX_WIDTH = 1024
IN_WIDTH = 2816
N_BUCKETS = 32
MAX_DISTANCE = 128
EPS = 1e-6
NEG = -1e30
QK_SCALE = 0.125
HALF_HEAD_PAIR = 64

ADAM_LR = 0.001
ADAM_B1 = 0.9
ADAM_B2 = 0.999
ADAM_EPS = 1e-08
ADAM_WD = 0.01
ADAM_STEP = 10

N_CHIPS = 4
TILE_CHUNKS = 2
TILE = TILE_CHUNKS * CHUNK
PROJ_TILE = 512
VMEM_LIMIT = 56 * 1024 * 1024

SMALL_A_ROWS = 8
ROW_LOSS = 4
ROW_WS = 0
ROW_BS = 512
ROW_SINK = 520
ROW_REL = 528
SMALL_B_ROWS = 536


def _mm(a, b):
    return lax.dot_general(a, b, (((1,), (0,)), ((), ())), preferred_element_type=F32)


def _mm_nt(a, b):
    return lax.dot_general(a, b, (((1,), (1,)), ((), ())), preferred_element_type=F32)


def _mm_tn(a, b):
    return lax.dot_general(a, b, (((0,), (0,)), ((), ())), preferred_element_type=F32)


def _bucket_map():
    qi = np.arange(CHUNK)[:, None]
    kj = np.arange(2 * CHUNK)[None, :]
    n = np.maximum(qi + CHUNK - kj, 0)
    max_exact = N_BUCKETS // 2
    large = max_exact + (np.log(np.maximum(n, 1) / max_exact) / np.log(MAX_DISTANCE / max_exact)
                         * (N_BUCKETS - max_exact)).astype(np.int32)
    large = np.minimum(large, N_BUCKETS - 1)
    return np.where(n < max_exact, n, large).astype(np.int32)


_GELU_C = 0.7978845608028654
_GELU_A = 0.044715
_GELU_K1 = 2.0 * _GELU_C
_GELU_K2 = 2.0 * _GELU_C * _GELU_A


def _gelu(x):
    x2 = x * x
    s = 1.0 / (1.0 + jnp.exp(x * (-_GELU_K1 - _GELU_K2 * x2)))
    return x * s, (s, x2)


def _gelu_grad(x, saved):
    s, x2 = saved
    return s + x * (s * (1.0 - s)) * (_GELU_K1 + 3.0 * _GELU_K2 * x2)


def _sigmoid(x):
    return 1.0 / (1.0 + jnp.exp(-x))


def _lane_lo(shape):
    return lax.broadcasted_iota(jnp.int32, shape, 1) < HALF_HEAD_PAIR


def _swa_variants(t):
    lo = _lane_lo(t.shape)
    tr = pltpu.roll(t, HALF_HEAD_PAIR, 1)
    zero = jnp.zeros_like(t)
    return (jnp.where(lo, t, zero).astype(BF16), jnp.where(lo, zero, tr).astype(BF16),
            jnp.where(lo, tr, zero).astype(BF16), jnp.where(lo, zero, t).astype(BF16))


def _swa_unvariants(d0, d1, d2, d3):
    lo = _lane_lo(d0.shape)
    zero = jnp.zeros_like(d0)
    rolled = jnp.where(lo, zero, d1) + jnp.where(lo, d2, zero)
    return jnp.where(lo, d0, zero) + jnp.where(lo, zero, d3) + pltpu.roll(rolled, HALF_HEAD_PAIR, 1)


def _mem_variants(t):
    out = []
    for pair in range(2):
        tp = t[:, pair * 128:(pair + 1) * 128]
        lo = _lane_lo(tp.shape)
        zero = jnp.zeros_like(tp)
        out.append(jnp.where(lo, tp, zero).astype(BF16))
        out.append(jnp.where(lo, zero, tp).astype(BF16))
    return out


def _mem_unvariants(d0, d1, d2, d3):
    lo = _lane_lo(d0.shape)
    return jnp.concatenate([jnp.where(lo, d0, d1), jnp.where(lo, d2, d3)], axis=-1)


def _softmax(logits, sinks):
    m = jnp.max(logits, axis=-1, keepdims=True)
    if sinks is not None:
        m = jnp.maximum(m, sinks)
    p = jnp.exp(logits - m)
    den = jnp.sum(p, axis=-1, keepdims=True)
    if sinks is None:
        return p * (1.0 / den), None
    es = jnp.exp(sinks - m)
    inv = 1.0 / (den + es)
    return p * inv, es * inv


def _band_valid(with_prev):
    qi = lax.broadcasted_iota(jnp.int32, (CHUNK, 2 * CHUNK), 0)
    kj = lax.broadcasted_iota(jnp.int32, (CHUNK, 2 * CHUNK), 1)
    in_cur = (kj >= CHUNK) & (kj - CHUNK <= qi)
    if not with_prev:
        return in_cur
    return in_cur | ((kj < CHUNK) & (kj > qi))


def _causal_weights(ws_ref):
    row = lax.broadcasted_iota(jnp.int32, (CHUNK, CHUNK), 0)
    col = lax.broadcasted_iota(jnp.int32, (CHUNK, CHUNK), 1)
    return [jnp.where(row >= col, ws_ref[g], 0.0).astype(BF16) for g in range(A_GROUPS)]


def _rows_to_lanes(a, n):
    return jnp.concatenate([a[c * CHUNK:(c + 1) * CHUNK] for c in range(n)], axis=1)


def _lanes_to_rows(a, n):
    w = a.shape[1] // n
    return jnp.concatenate([a[:, c * w:(c + 1) * w] for c in range(n)], axis=0)


def _stack_heads(pair01, pair23):
    return jnp.concatenate([pair01[:, :256], pair01[:, 256:], pair23[:, :256], pair23[:, 256:]], axis=0)


def _pair_heads(s, r):
    return (jnp.concatenate([s[0:r], s[r:2 * r]], axis=1), jnp.concatenate([s[2 * r:3 * r], s[3 * r:4 * r]], axis=1))


def _pair_operands(variants):
    return (jnp.concatenate(variants[0:2], axis=0), jnp.concatenate(variants[2:4], axis=0))


def _split_pair_grads(d_pairs):
    return d_pairs[0][:256], d_pairs[0][256:], d_pairs[1][:256], d_pairs[1][256:]


def _halves_bf16(a):
    return (a[:, :128].astype(BF16), a[:, 128:].astype(BF16))


def _group_a_forward(au, av, vg, vb, wm, bs_rows):
    gu, tu = _gelu(au)
    gv, tv = _gelu(av)
    ya, res = [], []
    for g in range(A_GROUPS):
        sl = slice(g * 128, (g + 1) * 128)
        xg = gv[:, sl]
        xc = xg - jnp.mean(xg, axis=-1, keepdims=True)
        rstd = lax.rsqrt(jnp.mean(xc * xc, axis=-1, keepdims=True) + EPS)
        xhat = xc * rstd
        vn = _rows_to_lanes((xhat * vg[:, sl] + vb[:, sl]).astype(BF16), TILE_CHUNKS)
        s = _lanes_to_rows(_mm(wm[g], vn), TILE_CHUNKS) + bs_rows[g]
        ya.append(gu[:, sl] * s)
        res.append((xhat, rstd, vn, s))
    return ya, dict(gu=gu, tu=tu, tv=tv, groups=res)


def _attention_probs(qp, k_pairs, bias, sink_col):
    logits = _stack_heads(_mm_nt(qp[0], k_pairs[0]), _mm_nt(qp[1], k_pairs[1]))
    if bias is not None:
        logits = logits + bias
    return _softmax(logits, sink_col)


def _attention_out(p, v_pairs, r):
    pp = _pair_heads(p.astype(BF16), r)
    return jnp.concatenate([_mm(pp[0], v_pairs[0]), _mm(pp[1], v_pairs[1])], axis=-1), pp


def _attention_backward(p, pp, do_pairs, qp, k_pairs, v_pairs, r):
    dp = _stack_heads(_mm_nt(do_pairs[0], v_pairs[0]), _mm_nt(do_pairs[1], v_pairs[1]))
    delta = jnp.sum(p * dp, axis=-1, keepdims=True)
    dl = p * (dp - delta)
    dlp = _pair_heads(dl.astype(BF16), r)
    dq = jnp.concatenate([_mm(dlp[0], k_pairs[0]), _mm(dlp[1], k_pairs[1])], axis=-1)
    dk = (_mm_tn(dlp[0], qp[0]), _mm_tn(dlp[1], qp[1]))
    dv = (_mm_tn(pp[0], do_pairs[0]), _mm_tn(pp[1], do_pairs[1]))
    return dl, delta, dq, dk, dv


def _tile_specs(n_tiles_ex, width):
    return pl.BlockSpec((TILE, width), lambda b, i: (b * n_tiles_ex + jnp.minimum(i, n_tiles_ex - 1), 0))


def _prev_chunk_spec(n_tiles_ex, width):
    def index(b, i):
        chunk = TILE_CHUNKS * jnp.minimum(i, n_tiles_ex - 1)
        return (b * n_tiles_ex * TILE_CHUNKS + jnp.maximum(chunk - 1, 0), 0)
    return pl.BlockSpec((CHUNK, width), index)


def _full_spec(shape):
    zeros = (0,) * len(shape)
    return pl.BlockSpec(shape, lambda *_: zeros)


SMEM_SPEC = pl.BlockSpec(memory_space=pltpu.SMEM)
ANY_SPEC = pl.BlockSpec(memory_space=pl.ANY)
VMEM_SPEC = pl.BlockSpec(memory_space=pltpu.VMEM)


def _make_bias(rel_bias_t, buckets):
    def body(rel_ref, bk_ref, out_ref):
        bk = bk_ref[...]
        for h in range(4):
            acc = jnp.zeros((CHUNK, 2 * CHUNK), F32)
            for b in range(N_BUCKETS):
                acc = jnp.where(bk == b, rel_ref[h, b], acc)
            for t, with_prev in enumerate((True, False)):
                out_ref[t, h * CHUNK:(h + 1) * CHUNK, :] = jnp.where(_band_valid(with_prev), acc, NEG)

    return pl.pallas_call(
        body, name="make_bias", out_shape=jax.ShapeDtypeStruct((2, 4 * CHUNK, 2 * CHUNK), F32),
        in_specs=[SMEM_SPEC, VMEM_SPEC], out_specs=VMEM_SPEC,
    )(rel_bias_t, buckets)


def _memkv_forward(mem, g_mem, w_mkv):
    n_ex = mem.shape[0]

    def body(mem_ref, g_ref, w_ref, out_ref):
        m = mem_ref[0]
        r = lax.rsqrt(jnp.mean(m * m, axis=-1, keepdims=True) + EPS)
        out_ref[0] = _mm((m * r * g_ref[...]).astype(BF16), w_ref[...])

    return pl.pallas_call(
        body, name="memkv_forward", grid=(n_ex,),
        out_shape=jax.ShapeDtypeStruct((n_ex, MEM_LEN, 2 * MEM_WIDTH), F32),
        in_specs=[pl.BlockSpec((1, MEM_LEN, D_MODEL), lambda b: (b, 0, 0)), _full_spec((1, D_MODEL)),
                  _full_spec((D_MODEL, 2 * MEM_WIDTH))],
        out_specs=pl.BlockSpec((1, MEM_LEN, 2 * MEM_WIDTH), lambda b: (b, 0, 0)),
    )(mem, g_mem, w_mkv)


PROJ_WIDTHS = (A_WIDTH, A_WIDTH, SWA_WIDTH, KV_WIDTH, KV_WIDTH, MEM_WIDTH, MIX_WIDTH)
PROJ_OFFSETS = tuple(int(v) for v in np.cumsum((0,) + PROJ_WIDTHS))


HALF_WIDTH = IN_WIDTH // 2
HALF_PARTS = ((0, 1, 2, 3), (4, 5, 6))


def _gather_and_project(x2, g_pre, w_in_s, w_mkv_s, w_out_s, x_arr):
    n_tok = x2.shape[0]
    n_tiles = n_tok // PROJ_TILE
    last = n_tiles - 1
    shapes = [w_in_s.shape, w_mkv_s.shape, w_out_s.shape]
    n_w = len(shapes)

    def body(x_sref, x_ref, g_ref, win_hbm, wmkv_hbm, wout_hbm, h_ref, *refs):
        part_refs, refs = refs[:len(PROJ_WIDTHS)], refs[len(PROJ_WIDTHS):]
        gin_hbm, gmkv_hbm, gout_hbm, wg, stage_in, stage_mkv, stage_out, own_mkv, own_out = refs[:9]
        send_sems, recv_sems, local_sems = refs[9:]
        p, t = pl.program_id(0), pl.program_id(1)
        x, y, c = lax.axis_index("x"), lax.axis_index("y"), lax.axis_index("c")
        me, sibling = (x, y, c), (x, y, 1 - c)
        my_shard = 2 * x + y
        gathered = [wg, gmkv_hbm, gout_hbm]

        def half_rows(w, shard, half):
            rows = shapes[w][0] // 2
            if w == 0:
                return wg.at[pl.ds(pl.multiple_of(shard * shapes[0][0] + half * rows, 16), rows), :]
            return gathered[w].at[shard, pl.ds(half * rows, rows), :]

        def first(w, rel):
            src = half_rows(w, my_shard, c) if w == 0 else (own_mkv, own_out)[w - 1].at[
                pl.ds(c * (shapes[w][0] // 2), shapes[w][0] // 2), :]
            k = 3 * w + rel - 1
            return pltpu.make_async_remote_copy(
                src_ref=src, dst_ref=half_rows(w, my_shard, c), send_sem=send_sems.at[k], recv_sem=recv_sems.at[k],
                device_id=(x ^ (rel >> 1), y ^ (rel & 1), c), device_id_type=MESH)

        def landed(w, rel):
            k = 3 * w + rel - 1
            ref = half_rows(w, my_shard ^ rel, c)
            return pltpu.make_async_remote_copy(src_ref=ref, dst_ref=ref, send_sem=send_sems.at[k],
                                                recv_sem=recv_sems.at[k], device_id=me, device_id_type=MESH)

        def passed(w, rel, half, to):
            k = 9 + 3 * w + rel - 1
            ref = half_rows(w, my_shard ^ rel, half)
            return pltpu.make_async_remote_copy(src_ref=ref, dst_ref=ref, send_sem=send_sems.at[k],
                                                recv_sem=recv_sems.at[k], device_id=to, device_id_type=MESH)

        def piece_rows(w, shard, half, piece):
            rows = shapes[w][0] // 4
            start = (2 * half + piece) * rows
            if w == 0:
                return wg.at[pl.ds(pl.multiple_of(shard * shapes[0][0] + start, 16), rows), :]
            return gathered[w].at[shard, pl.ds(start, rows), :]

        def relay(w, piece, incoming):
            k = 3 * w + 2 if piece == 0 else 18 + w
            source_rel, target_rel = (2, 1) if piece == 0 else (1, 2)
            ref = piece_rows(w, my_shard ^ (3 if incoming else source_rel), c, piece)
            to = me if incoming else (x ^ (target_rel >> 1), y ^ (target_rel & 1), c)
            return pltpu.make_async_remote_copy(src_ref=ref, dst_ref=ref, send_sem=send_sems.at[k],
                                                recv_sem=recv_sems.at[k], device_id=to, device_id_type=MESH)

        def pass_on(w, rels):
            for rel in rels:
                if rel == 3:
                    for piece in range(2):
                        relay(w, piece, True).wait_recv()
                else:
                    landed(w, rel).wait_recv()
                passed(w, rel, c, sibling).start()
            for rel in rels:
                passed(w, rel, 1 - c, me).wait_recv()

        own_stores = [pltpu.make_async_copy(own_mkv, gmkv_hbm.at[my_shard], local_sems.at[3]),
                      pltpu.make_async_copy(own_out, gout_hbm.at[my_shard], local_sems.at[4])]

        @pl.when((p == 0) & (t == 0))
        def _():
            loads = [pltpu.make_async_copy(src, dst, local_sems.at[k]) for k, (src, dst) in enumerate(
                ((win_hbm, stage_in), (wmkv_hbm, stage_mkv), (wout_hbm, stage_out)))]
            for cp in loads:
                cp.start()
            loads[0].wait()
            wg[pl.ds(pl.multiple_of(my_shard * shapes[0][0], 16), shapes[0][0]), :] = stage_in[...].astype(BF16)
            for rel in (1, 2):
                first(0, rel).start()
            loads[1].wait()
            loads[2].wait()
            own_mkv[...] = stage_mkv[...].astype(BF16)
            own_out[...] = stage_out[...].astype(BF16)
            for cp in own_stores:
                cp.start()
            pass_on(0, (1, 2))
            for piece in range(2):
                relay(0, piece, False).start()
            for w in (1, 2):
                for rel in (1, 2):
                    first(w, rel).start()

        @pl.when((p == 1) & (t == 0))
        def _():
            pass_on(0, (3,))
            for w in (1, 2):
                pass_on(w, (1, 2))
                for piece in range(2):
                    relay(w, piece, False).start()

        xv = x_ref[...]
        r = lax.rsqrt(jnp.mean(xv * xv, axis=-1, keepdims=True) + EPS)
        h = (xv * r * g_ref[...]).astype(BF16)

        @pl.when(p == 0)
        def _():
            h_ref[...] = h

        for hh in range(2):
            @pl.when((p ^ x_sref[0]) == hh)
            def _():
                proj = _mm_nt(h, wg[hh * HALF_WIDTH:(hh + 1) * HALF_WIDTH, :])
                for k in HALF_PARTS[hh]:
                    lo = PROJ_OFFSETS[k] - hh * HALF_WIDTH
                    part_refs[k][...] = proj[:, lo:lo + PROJ_WIDTHS[k]]

        @pl.when((p == 1) & (t == last))
        def _():
            store = pltpu.make_async_copy(wg, gin_hbm, local_sems.at[5])
            store.start()
            for w in (1, 2):
                pass_on(w, (3,))
            for w in range(n_w):
                for rel in (1, 2):
                    first(w, rel).wait_send()
                for piece in range(2):
                    relay(w, piece, False).wait_send()
                for rel in (1, 2, 3):
                    passed(w, rel, c, sibling).wait_send()
            for cp in own_stores:
                cp.wait()
            store.wait()

    def active_in(hh):
        def index(p, t, xs):
            return (jnp.where((p ^ xs[0]) == hh, t, jnp.where(p == 0, 0, last)), 0)
        return index

    part_specs = [pl.BlockSpec((PROJ_TILE, PROJ_WIDTHS[k]), active_in(hh)) for hh in range(2) for k in HALF_PARTS[hh]]
    vmem = pltpu.VMEM
    out = pl.pallas_call(
        body, name="gather_and_project",
        out_shape=[jax.ShapeDtypeStruct((n_tok, D_MODEL), BF16)]
        + [jax.ShapeDtypeStruct((n_tok, w), F32) for w in PROJ_WIDTHS]
        + [jax.ShapeDtypeStruct((N_CHIPS * shapes[0][0], shapes[0][1]), BF16)]
        + [jax.ShapeDtypeStruct((N_CHIPS,) + s, BF16) for s in shapes[1:]],
        grid_spec=pltpu.PrefetchScalarGridSpec(
            num_scalar_prefetch=1, grid=(2, n_tiles),
            in_specs=[pl.BlockSpec((PROJ_TILE, D_MODEL), lambda p, t, xs: (t, 0)),
                      pl.BlockSpec((1, D_MODEL), lambda p, t, xs: (0, 0)), ANY_SPEC, ANY_SPEC, ANY_SPEC],
            out_specs=[pl.BlockSpec((PROJ_TILE, D_MODEL), lambda p, t, xs: (jnp.where(p == 0, t, last), 0))]
            + part_specs + [ANY_SPEC] * 3,
            scratch_shapes=[vmem((N_CHIPS * shapes[0][0], shapes[0][1]), BF16), vmem(shapes[0], F32),
                            vmem(shapes[1], F32), vmem(shapes[2], F32), vmem(shapes[1], BF16), vmem(shapes[2], BF16),
                            pltpu.SemaphoreType.DMA((21,)), pltpu.SemaphoreType.DMA((21,)),
                            pltpu.SemaphoreType.DMA((6,))]),
        compiler_params=pltpu.CompilerParams(vmem_limit_bytes=VMEM_LIMIT),
    )(x_arr, x2, g_pre, w_in_s, w_mkv_s, w_out_s)
    h, parts, weights = out[0], out[1:1 + len(PROJ_WIDTHS)], out[1 + len(PROJ_WIDTHS):]
    return h, list(parts), weights


def _load_chunk(j, i, sk_ref, sv_ref, skp_ref, svp_ref):
    rows = slice(j * CHUNK, (j + 1) * CHUNK)
    if j == 0:
        k_prev, v_prev, table = skp_ref[...], svp_ref[...], jnp.where(i > 0, 0, 1)
    else:
        prev = slice((j - 1) * CHUNK, j * CHUNK)
        k_prev, v_prev, table = sk_ref[prev, :], sv_ref[prev, :], 0
    k_pairs = _pair_operands(_swa_variants(jnp.concatenate([k_prev, sk_ref[rows, :]], axis=0)))
    v_pairs = _pair_operands(_swa_variants(jnp.concatenate([v_prev, sv_ref[rows, :]], axis=0)))
    return rows, k_pairs, v_pairs, table


def _tile_constants(ws_ref, bs_ref, sink_ref, mkv_ref):
    wm = _causal_weights(ws_ref)
    bs_rows = [jnp.concatenate([bs_ref[g]] * TILE_CHUNKS, axis=0) for g in range(A_GROUPS)]
    sink_col = jnp.max(jnp.concatenate([jnp.full((CHUNK, 128), sink_ref[0, h], F32) for h in range(4)], axis=0),
                       axis=-1, keepdims=True)
    mkv_v = mkv_ref[0]
    mk_pairs = _pair_operands(_mem_variants(mkv_v[:, :MEM_WIDTH]))
    mv_pairs = _pair_operands(_mem_variants(mkv_v[:, MEM_WIDTH:]))
    return wm, bs_rows, sink_col, mk_pairs, mv_pairs


def _mix(parts, mkv, x2, tgt2, v_g, v_b, w_sp, b_sp, sinks, bias, w_out, g_post, n_ex, seq):
    n_tiles_ex = seq // TILE
    n_tok = n_ex * seq
    au, av, sq, sk, sv, mq, z = parts
    col = dict(zip(("au", "av", "sq", "sk", "sv", "mq", "z"),
                   (slice(PROJ_OFFSETS[k], PROJ_OFFSETS[k + 1]) for k in range(len(PROJ_WIDTHS)))))
    before_kv, after_kv = slice(0, col["sk"].start), slice(col["sv"].stop, IN_WIDTH)

    def body(au_ref, av_ref, sq_ref, sk_ref, sv_ref, skp_ref, svp_ref, mq_ref, z_ref, mkv_ref, x_ref, tgt_ref,
             vg_ref, vb_ref, ws_ref, bs_ref, sink_ref, bias_ref, wout_ref, gpost_ref,
             dout_ref, dproj_ref, dmkv_ref, dwout_ref, dvg_ref, dvb_ref, dws_ref, dbs_ref, dsink_ref, drel_ref,
             loss_ref, dgpost_ref, carry_dp, carry_k, carry_v):
        b, i = pl.program_id(0), pl.program_id(1)

        @pl.when((b == 0) & (i == 0))
        def _():
            for ref in (dwout_ref, dvg_ref, dvb_ref, dws_ref, dbs_ref, dsink_ref, drel_ref, loss_ref, dgpost_ref):
                ref[...] = jnp.zeros_like(ref)

        @pl.when(i == 0)
        def _():
            dmkv_ref[...] = jnp.zeros_like(dmkv_ref)
            carry_k[...] = jnp.zeros_like(carry_k)
            carry_v[...] = jnp.zeros_like(carry_v)

        @pl.when(i > 0)
        def _():
            dproj_ref[:, before_kv] = carry_dp[:, before_kv]
            dproj_ref[:, after_kv] = carry_dp[:, after_kv]

        @pl.when(i < n_tiles_ex)
        def _():
            wm, bs_rows, sink_col, mk_pairs, mv_pairs = _tile_constants(ws_ref, bs_ref, sink_ref, mkv_ref)
            vg = vg_ref[...]

            au_v, av_v = au_ref[...], av_ref[...]
            ya, res = _group_a_forward(au_v, av_v, vg, vb_ref[...], wm, bs_rows)
            swa, yb = [], []
            for j in range(TILE_CHUNKS):
                rows, k_pairs, v_pairs, table = _load_chunk(j, i, sk_ref, sv_ref, skp_ref, svp_ref)
                qp = _halves_bf16(sq_ref[rows, :] * QK_SCALE)
                p, ps = _attention_probs(qp, k_pairs, bias_ref[table], sink_col)
                out, pp = _attention_out(p, v_pairs, CHUNK)
                yb.append(out)
                swa.append((rows, k_pairs, v_pairs, qp, p, ps, pp))
            mqp = _halves_bf16(mq_ref[...] * QK_SCALE)
            pm, _ = _attention_probs(mqp, mk_pairs, None, None)
            yc, ppm = _attention_out(pm, mv_pairs, TILE)
            ycat = jnp.concatenate(ya + [jnp.concatenate(yb, axis=0), yc], axis=-1)

            zv = z_ref[...]
            sig = _sigmoid(zv)
            sz = zv * sig
            y_b = (ycat * sz).astype(BF16)
            o = _mm(y_b, wout_ref[...])
            r2 = lax.rsqrt(jnp.mean(o * o, axis=-1, keepdims=True) + EPS)
            nrm = o * r2
            gp = gpost_ref[...]
            diff = x_ref[...] + nrm * gp - tgt_ref[...]
            loss_ref[...] += jnp.sum(diff * diff) * (0.5 / D_MODEL)
            dout = diff * (1.0 / D_MODEL)
            dout_ref[...] = dout
            dgpost_ref[...] += jnp.sum(dout * nrm, axis=0, keepdims=True)
            dn = dout * gp
            do_b = (r2 * (dn - nrm * jnp.mean(dn * nrm, axis=-1, keepdims=True))).astype(BF16)
            dwout_ref[...] += _mm_tn(y_b, do_b)
            dy = _mm_nt(do_b, wout_ref[...])
            carry_dp[:, col["z"]] = (dy * ycat * (sig * (1.0 + zv * (1.0 - sig)))).astype(BF16)
            dyc = dy * sz

            dgu, dgv = [], []
            for g in range(A_GROUPS):
                sl = slice(g * 128, (g + 1) * 128)
                xhat, rstd, vn, s = res["groups"][g]
                dya = dyc[:, sl]
                dgu.append(dya * s)
                ds = dya * res["gu"][:, sl]
                dbs_ref[:, sl] += sum(ds[c * CHUNK:(c + 1) * CHUNK] for c in range(TILE_CHUNKS))
                ds_b = _rows_to_lanes(ds.astype(BF16), TILE_CHUNKS)
                dws_ref[g] += _mm_nt(ds_b, vn)
                dvn = _lanes_to_rows(_mm_tn(wm[g], ds_b), TILE_CHUNKS)
                dvg_ref[:, sl] += jnp.sum(dvn * xhat, axis=0, keepdims=True)
                dvb_ref[:, sl] += jnp.sum(dvn, axis=0, keepdims=True)
                dxh = dvn * vg[:, sl]
                dgv.append(rstd * (dxh - jnp.mean(dxh, axis=-1, keepdims=True)
                                   - xhat * jnp.mean(dxh * xhat, axis=-1, keepdims=True)))
            carry_dp[:, col["au"]] = (jnp.concatenate(dgu, axis=-1) * _gelu_grad(au_v, res["tu"])).astype(BF16)
            carry_dp[:, col["av"]] = (jnp.concatenate(dgv, axis=-1) * _gelu_grad(av_v, res["tv"])).astype(BF16)

            lane4 = lax.broadcasted_iota(jnp.int32, (1, 128), 1)
            dsink_vec = jnp.zeros((1, 128), F32)
            dk_parts, dv_parts = [], []
            for rows, k_pairs, v_pairs, qp, p, ps, pp in swa:
                do_pairs = _halves_bf16(dyc[rows, A_WIDTH:A_WIDTH + SWA_WIDTH])
                dl, delta, dq, dk, dv = _attention_backward(p, pp, do_pairs, qp, k_pairs, v_pairs, CHUNK)
                sink_terms = ps * delta
                for h in range(4):
                    dsink_vec = dsink_vec + jnp.where(lane4 == h, -jnp.sum(sink_terms[h * CHUNK:(h + 1) * CHUNK]), 0.0)
                drel_ref[...] += dl
                carry_dp[rows, col["sq"]] = (dq * QK_SCALE).astype(BF16)
                dk_parts.append(_swa_unvariants(*_split_pair_grads(dk)))
                dv_parts.append(_swa_unvariants(*_split_pair_grads(dv)))
            dsink_ref[...] += dsink_vec

            dc_pairs = _halves_bf16(dyc[:, A_WIDTH + SWA_WIDTH:])
            _, _, dmq, dmk, dmv = _attention_backward(pm, ppm, dc_pairs, mqp, mk_pairs, mv_pairs, TILE)
            carry_dp[:, col["mq"]] = (dmq * QK_SCALE).astype(BF16)
            dmkv_ref[0] += jnp.concatenate([_mem_unvariants(*_split_pair_grads(dmk)),
                                            _mem_unvariants(*_split_pair_grads(dmv))], axis=-1)

            for parts_c, carry, cols in ((dk_parts, carry_k, col["sk"]), (dv_parts, carry_v, col["sv"])):
                @pl.when(i > 0)
                def _():
                    dproj_ref[:, cols] = (carry[...] + jnp.concatenate(
                        [jnp.zeros((TILE - CHUNK, KV_WIDTH), F32), parts_c[0][:CHUNK]], axis=0)).astype(BF16)
                new = [parts_c[0][CHUNK:]]
                for j in range(1, TILE_CHUNKS):
                    new[-1] = new[-1] + parts_c[j][:CHUNK]
                    new.append(parts_c[j][CHUNK:])
                carry[...] = jnp.concatenate(new, axis=0)

        @pl.when(i == n_tiles_ex)
        def _():
            dproj_ref[:, col["sk"]] = carry_k[...].astype(BF16)
            dproj_ref[:, col["sv"]] = carry_v[...].astype(BF16)

    tile = functools.partial(_tile_specs, n_tiles_ex)
    prev = functools.partial(_prev_chunk_spec, n_tiles_ex)
    late = pl.BlockSpec((TILE, IN_WIDTH), lambda b, i: (b * n_tiles_ex + jnp.maximum(i - 1, 0), 0))
    return pl.pallas_call(
        body, name="mix", grid=(n_ex, n_tiles_ex + 1),
        out_shape=[jax.ShapeDtypeStruct((n_tok, D_MODEL), F32), jax.ShapeDtypeStruct((n_tok, IN_WIDTH), BF16),
                   jax.ShapeDtypeStruct((n_ex, MEM_LEN, 2 * MEM_WIDTH), F32),
                   jax.ShapeDtypeStruct((MIX_WIDTH, D_MODEL), F32), jax.ShapeDtypeStruct((1, A_WIDTH), F32),
                   jax.ShapeDtypeStruct((1, A_WIDTH), F32), jax.ShapeDtypeStruct((A_GROUPS, CHUNK, CHUNK), F32),
                   jax.ShapeDtypeStruct((CHUNK, A_WIDTH), F32), jax.ShapeDtypeStruct((1, 128), F32),
                   jax.ShapeDtypeStruct((4 * CHUNK, 2 * CHUNK), F32), jax.ShapeDtypeStruct((1, 128), F32),
                   jax.ShapeDtypeStruct((1, D_MODEL), F32)],
        in_specs=[tile(A_WIDTH), tile(A_WIDTH), tile(SWA_WIDTH), tile(KV_WIDTH), tile(KV_WIDTH),
                  prev(KV_WIDTH), prev(KV_WIDTH), tile(MEM_WIDTH), tile(MIX_WIDTH),
                  pl.BlockSpec((1, MEM_LEN, 2 * MEM_WIDTH), lambda b, i: (b, 0, 0)),
                  tile(D_MODEL), tile(D_MODEL),
                  _full_spec((1, A_WIDTH)), _full_spec((1, A_WIDTH)), _full_spec((A_GROUPS, CHUNK, CHUNK)),
                  _full_spec((A_GROUPS, CHUNK, CHUNK)), SMEM_SPEC, _full_spec((2, 4 * CHUNK, 2 * CHUNK)),
                  _full_spec((MIX_WIDTH, D_MODEL)), _full_spec((1, D_MODEL))],
        out_specs=[tile(D_MODEL), late, pl.BlockSpec((1, MEM_LEN, 2 * MEM_WIDTH), lambda b, i: (b, 0, 0)),
                   _full_spec((MIX_WIDTH, D_MODEL)), _full_spec((1, A_WIDTH)), _full_spec((1, A_WIDTH)),
                   _full_spec((A_GROUPS, CHUNK, CHUNK)), _full_spec((CHUNK, A_WIDTH)), _full_spec((1, 128)),
                   _full_spec((4 * CHUNK, 2 * CHUNK)), _full_spec((1, 128)), _full_spec((1, D_MODEL))],
        scratch_shapes=[pltpu.VMEM((TILE, IN_WIDTH), BF16), pltpu.VMEM((TILE, KV_WIDTH), F32),
                        pltpu.VMEM((TILE, KV_WIDTH), F32)],
        compiler_params=pltpu.CompilerParams(vmem_limit_bytes=VMEM_LIMIT),
    )(au, av, sq, sk, sv, sk, sv, mq, z, mkv, x2, tgt2, v_g, v_b, w_sp, b_sp, sinks, bias, w_out, g_post)


BWD_PROJ_TILE = 512


def _backward_projection(x2, dout, dproj, g_pre, w_in_t):
    n_tok = x2.shape[0]
    n_steps = n_tok // BWD_PROJ_TILE

    def body(x_ref, dout_ref, dp_ref, g_ref, w_hbm, dx_ref, dgpre_ref, w_vmem, sem):
        @pl.when(pl.program_id(0) == 0)
        def _():
            load = pltpu.make_async_copy(w_hbm, w_vmem, sem)
            load.start()
            dgpre_ref[...] = jnp.zeros_like(dgpre_ref)
            load.wait()

        xv = x_ref[...]
        r = lax.rsqrt(jnp.mean(xv * xv, axis=-1, keepdims=True) + EPS)
        xn = xv * r
        dh = _mm(dp_ref[...], w_vmem[...])
        dgpre_ref[...] += jnp.sum(dh * xn, axis=0, keepdims=True)
        dhg = dh * g_ref[...]
        dx_ref[...] = r * (dhg - xn * jnp.mean(dhg * xn, axis=-1, keepdims=True)) + dout_ref[...]

    row = lambda w: pl.BlockSpec((BWD_PROJ_TILE, w), lambda i: (i, 0))
    return pl.pallas_call(
        body, name="backward_projection", grid=(n_steps,),
        out_shape=[jax.ShapeDtypeStruct((n_tok, D_MODEL), F32), jax.ShapeDtypeStruct((1, D_MODEL), F32)],
        in_specs=[row(D_MODEL), row(D_MODEL), row(IN_WIDTH), _full_spec((1, D_MODEL)), ANY_SPEC],
        out_specs=[row(D_MODEL), _full_spec((1, D_MODEL))],
        scratch_shapes=[pltpu.VMEM((IN_WIDTH, D_MODEL), BF16), pltpu.SemaphoreType.DMA],
        input_output_aliases={1: 0},
        compiler_params=pltpu.CompilerParams(vmem_limit_bytes=VMEM_LIMIT),
    )(x2, dout, dproj, g_pre, w_in_t)


SHARD_ROWS = IN_WIDTH // N_CHIPS
SHARD_WINDOW = 768
SHARD_HALF = SHARD_ROWS // 2
DWIN_TILE = 2048
N_REL = N_CHIPS - 1


def _shard_window_start(shard):
    return (shard * SHARD_ROWS // 128) * 128


def _reduce_gradients(dproj, h, big, small, shard_arr):
    n_tok = h.shape[0]
    tile = min(DWIN_TILE, n_tok)
    n_sub = n_tok // tile
    last = N_CHIPS - 1
    n_big, n_small = len(big), len(small)
    big_half = [g.shape[2:] for g in big]
    sem_big_d2d = 2 * N_CHIPS
    sem_big_ici = sem_big_d2d + n_big
    sem_big_swap = sem_big_ici + N_REL * n_big
    sem_small_d2d = sem_big_swap + n_big
    sem_small_ici = sem_small_d2d + n_small
    n_sems = sem_small_ici + N_REL * n_small
    loc_small = n_big
    loc_out_win = loc_small + n_small
    loc_out_big = loc_out_win + 2
    loc_out_small = loc_out_big + 2 * n_big
    n_local = loc_out_small + n_small

    def relation_of_slot(s):
        return (s + 2) % N_REL + 1

    def shard_of_slot(s, my_shard):
        return my_shard ^ jnp.where(s == last, 0, relation_of_slot(s))

    def body(shard_ref, dp_ref, h_hbm, *refs):
        h_vmem, h_sem, refs = refs[-2], refs[-1], refs[:-2]
        big_hbm, refs = refs[:n_big], refs[n_big:]
        small_hbm, refs = refs[:n_small], refs[n_small:]
        out_hbm, refs = refs[0], refs[1:]
        big_out, refs = refs[:n_big], refs[n_big:]
        small_out, refs = refs[:n_small], refs[n_small:]
        part, recv_d2d, send_ici, recv_ici, mine_buf, other_buf = refs[:6]
        refs = refs[6:]
        big_own, big_recv, big_send, big_land, big_mine, big_other = (
            refs[k * n_big:(k + 1) * n_big] for k in range(6))
        refs = refs[6 * n_big:]
        small_own, small_recv, small_all = (refs[k * n_small:(k + 1) * n_small] for k in range(3))
        send_sems, recv_sems, local_sems = refs[3 * n_small:]

        s, t = pl.program_id(0), pl.program_id(1)
        x, y, c = lax.axis_index("x"), lax.axis_index("y"), lax.axis_index("c")
        my_chip = 2 * x + y
        sibling = (x, y, 1 - c)
        my_rows = pl.ds(pl.multiple_of(c * SHARD_HALF, 8), SHARD_HALF)
        other_rows = pl.ds(pl.multiple_of((1 - c) * SHARD_HALF, 8), SHARD_HALF)

        def remote(src, dst, k, to):
            return pltpu.make_async_remote_copy(src_ref=src, dst_ref=dst, send_sem=send_sems.at[k],
                                                recv_sem=recv_sems.at[k], device_id=to, device_id_type=MESH)

        def chip_at(rel):
            return (x ^ (rel >> 1), y ^ (rel & 1), c)

        def to_sibling(k):
            return remote(part.at[k % 2, other_rows, :], recv_d2d.at[k], k, sibling)

        def to_chip(k):
            return remote(send_ici.at[k], recv_ici.at[k], N_CHIPS + k, chip_at(relation_of_slot(k)))

        swap = remote(mine_buf, other_buf, 2 * N_CHIPS - 1, sibling)
        big_load = [pltpu.make_async_copy(big_hbm[w].at[:, pl.ds(c, 1)], big_own[w], local_sems.at[w])
                    for w in range(n_big)]
        big_to_sibling = [remote(big_hbm[w].at[:, pl.ds(1 - c, 1)], big_recv[w], sem_big_d2d + w, sibling)
                          for w in range(n_big)]
        big_to_chip = [[remote(big_send[w].at[k], big_land[w].at[k], sem_big_ici + N_REL * w + k, chip_at(k + 1))
                        for k in range(N_REL)] for w in range(n_big)]
        big_swap = [remote(big_mine[w], big_other[w], sem_big_swap + w, sibling) for w in range(n_big)]
        small_load = [pltpu.make_async_copy(small_hbm[i], small_own[i], local_sems.at[loc_small + i])
                      for i in range(n_small)]
        small_to_sibling = [remote(small_hbm[i], small_recv[i], sem_small_d2d + i, sibling) for i in range(n_small)]
        small_to_chip = [[remote(small_all[i].at[my_chip], small_all[i].at[my_chip],
                                 sem_small_ici + N_REL * i + k, chip_at(k + 1))
                          for k in range(N_REL)] for i in range(n_small)]

        @pl.when((s == 0) & (t == 0))
        def _():
            h_load = pltpu.make_async_copy(h_hbm, h_vmem, h_sem)
            h_load.start()
            for cp in big_load + big_to_sibling + small_load + small_to_sibling:
                cp.start()
            h_load.wait()

        @pl.when((s == 0) & (t == n_sub - 1))
        def _():
            for cp in big_load + small_load:
                cp.wait()
            for cp in big_to_sibling + small_to_sibling:
                cp.wait_recv()
                cp.wait_send()
            for w in range(n_big):
                for k in range(N_REL):
                    shard = my_chip ^ (k + 1)
                    big_send[w][k] = (big_own[w][shard, 0] + big_recv[w][shard, 0]).astype(BF16)
                    big_to_chip[w][k].start()
            for i in range(n_small):
                small_all[i][my_chip] = small_own[i][...] + small_recv[i][...]
                for k in range(N_REL):
                    small_to_chip[i][k].start()

        @pl.when((s > 0) & (t == 0))
        def _():
            k = s - 1
            cp = to_sibling(k)
            cp.wait_recv()
            cp.wait_send()
            send_ici[k] = (part[k % 2, my_rows, :] + recv_d2d[k]).astype(BF16)
            to_chip(k).start()

        r = _mm_tn(dp_ref[...], h_vmem[pl.ds(pl.multiple_of(t * tile, tile), tile), :])
        odd = shard_of_slot(s, shard_ref[0]) % 2
        for parity in range(2):
            rows = r[64 * parity:64 * parity + SHARD_ROWS]

            @pl.when((odd == parity) & (t == 0))
            def _():
                part[s % 2] = rows

            @pl.when((odd == parity) & (t > 0))
            def _():
                part[s % 2] += rows

        @pl.when(t == n_sub - 1)
        def _():
            to_sibling(s).start()

        @pl.when((s == last) & (t == n_sub - 1))
        def _():
            cp = to_sibling(last)
            cp.wait_recv()
            cp.wait_send()
            total = part[last % 2, my_rows, :] + recv_d2d[last]
            for k in range(last):
                to_chip(k).wait_recv()
                total = total + recv_ici[k].astype(F32)
            mine_buf[...] = total
            swap.start()
            out_mine = pltpu.make_async_copy(mine_buf, out_hbm.at[my_rows, :], local_sems.at[0])
            out_mine.start()
            swap.wait_recv()
            out_other = pltpu.make_async_copy(other_buf, out_hbm.at[other_rows, :], local_sems.at[1])
            out_other.start()
            stores = [out_mine, out_other]
            for w in range(n_big):
                rows = big_half[w][0]
                total = big_own[w][my_chip, 0] + big_recv[w][my_chip, 0]
                for k in range(N_REL):
                    big_to_chip[w][k].wait_recv()
                    total = total + big_land[w][k].astype(F32)
                big_mine[w][...] = total
                big_swap[w].start()
                stores.append(pltpu.make_async_copy(
                    big_mine[w], big_out[w].at[pl.ds(pl.multiple_of(c * rows, 8), rows), :],
                    local_sems.at[loc_out_big + 2 * w]))
                stores[-1].start()
            for w in range(n_big):
                rows = big_half[w][0]
                big_swap[w].wait_recv()
                stores.append(pltpu.make_async_copy(
                    big_other[w], big_out[w].at[pl.ds(pl.multiple_of((1 - c) * rows, 8), rows), :],
                    local_sems.at[loc_out_big + 2 * w + 1]))
                stores[-1].start()
            for i in range(n_small):
                for k in range(N_REL):
                    small_to_chip[i][k].wait_recv()
                stores.append(pltpu.make_async_copy(small_all[i], small_out[i], local_sems.at[loc_out_small + i]))
                stores[-1].start()
            for k in range(last):
                to_chip(k).wait_send()
            swap.wait_send()
            for w in range(n_big):
                for k in range(N_REL):
                    big_to_chip[w][k].wait_send()
                big_swap[w].wait_send()
            for i in range(n_small):
                for k in range(N_REL):
                    small_to_chip[i][k].wait_send()
            for cp in stores:
                cp.wait()

    half = (SHARD_HALF, D_MODEL)
    vmem = pltpu.VMEM
    scratch = [vmem((2, SHARD_ROWS, D_MODEL), F32), vmem((N_CHIPS,) + half, F32),
               vmem((N_REL,) + half, BF16), vmem((N_REL,) + half, BF16), vmem(half, F32), vmem(half, F32)]
    scratch += [vmem((N_CHIPS, 1) + hs, F32) for hs in big_half] * 2
    scratch += [vmem((N_REL,) + hs, BF16) for hs in big_half] * 2
    scratch += [vmem(hs, F32) for hs in big_half] * 2
    scratch += [vmem(a.shape, F32) for a in small] * 2 + [vmem((N_CHIPS,) + a.shape, F32) for a in small]
    scratch += [pltpu.SemaphoreType.DMA((n_sems,)), pltpu.SemaphoreType.DMA((n_sems,)),
                pltpu.SemaphoreType.DMA((n_local,)), vmem(h.shape, BF16), pltpu.SemaphoreType.DMA]
    n_hbm = n_big + n_small
    out = pl.pallas_call(
        body, name="reduce_gradients",
        out_shape=[jax.ShapeDtypeStruct((SHARD_ROWS, D_MODEL), F32)]
        + [jax.ShapeDtypeStruct((2 * hs[0], hs[1]), F32) for hs in big_half]
        + [jax.ShapeDtypeStruct((N_CHIPS,) + a.shape, F32) for a in small],
        grid_spec=pltpu.PrefetchScalarGridSpec(
            num_scalar_prefetch=1, grid=(N_CHIPS, n_sub),
            in_specs=[pl.BlockSpec((pl.Element(tile), pl.Element(SHARD_WINDOW)),
                                   lambda s, t, m: (t * tile, _shard_window_start(shard_of_slot(s, m[0])))),
                      ANY_SPEC] + [ANY_SPEC] * n_hbm,
            out_specs=[ANY_SPEC] * (1 + n_hbm),
            scratch_shapes=scratch),
        compiler_params=pltpu.CompilerParams(vmem_limit_bytes=VMEM_LIMIT),
    )(shard_arr, dproj, h, *big, *small)
    return out[:1 + n_big], out[1 + n_big:]


def _memkv_backward(mem, dmkv, g_mem, w_mkv):
    n_ex = mem.shape[0]

    def body(mem_ref, d_ref, g_ref, w_ref, dw_ref, dg_ref):
        @pl.when(pl.program_id(0) == 0)
        def _():
            dw_ref[...] = jnp.zeros_like(dw_ref)
            dg_ref[...] = jnp.zeros_like(dg_ref)

        m = mem_ref[0]
        mn = m * lax.rsqrt(jnp.mean(m * m, axis=-1, keepdims=True) + EPS)
        d_b = d_ref[0].astype(BF16)
        dw_ref[...] += _mm_tn((mn * g_ref[...]).astype(BF16), d_b)
        dg_ref[...] += jnp.sum(_mm_nt(d_b, w_ref[...]) * mn, axis=0, keepdims=True)

    return pl.pallas_call(
        body, name="memkv_backward", grid=(n_ex,),
        out_shape=[jax.ShapeDtypeStruct((D_MODEL, 2 * MEM_WIDTH), F32), jax.ShapeDtypeStruct((1, D_MODEL), F32)],
        in_specs=[pl.BlockSpec((1, MEM_LEN, D_MODEL), lambda b: (b, 0, 0)),
                  pl.BlockSpec((1, MEM_LEN, 2 * MEM_WIDTH), lambda b: (b, 0, 0)),
                  _full_spec((1, D_MODEL)), _full_spec((D_MODEL, 2 * MEM_WIDTH))],
        out_specs=[_full_spec((D_MODEL, 2 * MEM_WIDTH)), _full_spec((1, D_MODEL))],
    )(mem, dmkv, g_mem, w_mkv)


def _pack_small_grads(dgpre, dgpost, dgmem, dvg, dvb, dws, dbs, dsink, drel, loss_vec, buckets):
    def body(dgpre_ref, dgpost_ref, dgmem_ref, dvg_ref, dvb_ref, dws_ref, dbs_ref, dsink_ref, drel_ref, loss_ref,
             bk_ref, a_ref, b_ref):
        a_ref[...] = jnp.zeros_like(a_ref)
        b_ref[...] = jnp.zeros_like(b_ref)
        a_ref[0:1, :] = dgpre_ref[...]
        a_ref[1:2, :] = dgpost_ref[...]
        a_ref[2:3, :] = dgmem_ref[...]
        a_ref[3:4, :] = jnp.concatenate([dvg_ref[...], dvb_ref[...]], axis=-1)
        a_ref[ROW_LOSS:ROW_LOSS + 1, 0:128] = loss_ref[...]
        row = lax.broadcasted_iota(jnp.int32, (CHUNK, CHUNK), 0)
        col = lax.broadcasted_iota(jnp.int32, (CHUNK, CHUNK), 1)
        for g in range(A_GROUPS):
            b_ref[ROW_WS + g * CHUNK:ROW_WS + (g + 1) * CHUNK, :] = jnp.where(row >= col, dws_ref[g], 0.0)
            by_token = jnp.transpose(dbs_ref[:, g * 128:(g + 1) * 128])
            b_ref[ROW_BS + g:ROW_BS + g + 1, :] = jnp.sum(by_token, axis=0, keepdims=True)
        b_ref[ROW_SINK:ROW_SINK + 1, :] = dsink_ref[...]
        bk = bk_ref[...]
        rel_row = lax.broadcasted_iota(jnp.int32, (8, 128), 0)
        rel_col = lax.broadcasted_iota(jnp.int32, (8, 128), 1)
        rel = jnp.zeros((8, 128), F32)
        for h in range(4):
            acc = drel_ref[h * CHUNK:(h + 1) * CHUNK, :]
            for b in range(N_BUCKETS):
                rel = jnp.where((rel_row == h) & (rel_col == b), jnp.sum(jnp.where(bk == b, acc, 0.0)), rel)
        b_ref[ROW_REL:ROW_REL + 8, :] = rel

    return pl.pallas_call(
        body, name="pack_small_grads",
        out_shape=[jax.ShapeDtypeStruct((SMALL_A_ROWS, D_MODEL), F32), jax.ShapeDtypeStruct((SMALL_B_ROWS, 128), F32)],
        in_specs=[VMEM_SPEC] * 11, out_specs=[VMEM_SPEC] * 2,
    )(dgpre, dgpost, dgmem, dvg, dvb, dws, dbs, dsink, drel, loss_vec, buckets)


def _adamw(w, g, m, v):
    m2 = ADAM_B1 * m + (1.0 - ADAM_B1) * g
    v2 = ADAM_B2 * v + (1.0 - ADAM_B2) * (g * g)
    m_hat = m2 / (1.0 - ADAM_B1 ** ADAM_STEP)
    v_hat = v2 / (1.0 - ADAM_B2 ** ADAM_STEP)
    delta = -ADAM_LR * (m_hat / (jnp.sqrt(v_hat) + ADAM_EPS) + ADAM_WD * w)
    return delta, m2, v2


ADAM_MAX_ROWS = 176


def _adamw_whole(g, w, m, v, name):
    rows, cols = w.shape
    steps = -(-rows // ADAM_MAX_ROWS)
    block_rows = rows // steps
    assert block_rows * steps == rows and block_rows % 8 == 0

    def body(g_ref, w_ref, m_ref, v_ref, d_out, m_out, v_out):
        delta, m2, v2 = _adamw(w_ref[...], g_ref[...], m_ref[...], v_ref[...])
        d_out[...] = delta
        m_out[...] = m2
        v_out[...] = v2

    block = pl.BlockSpec((block_rows, cols), lambda k: (k, 0))
    out = pl.pallas_call(
        body, name=name, grid=(steps,), out_shape=[jax.ShapeDtypeStruct((rows, cols), F32)] * 3,
        in_specs=[block] * 4, out_specs=[block] * 3,
    )(g, w, m, v)
    return [g] + list(out)


def _adamw_small(ra, rb, weights, moments_m, moments_v):
    n = len(weights)

    def body(*refs):
        ra_ref, rb_ref = refs[0], refs[1]
        w_refs, m_refs, v_refs = refs[2:2 + n], refs[2 + n:2 + 2 * n], refs[2 + 2 * n:2 + 3 * n]
        outs = refs[2 + 3 * n:]
        g_outs, d_outs, m_outs, v_outs = outs[:n], outs[n:2 * n], outs[2 * n:3 * n], outs[3 * n:4 * n]
        ga, gb = ra_ref[0], rb_ref[0]
        for chip in range(1, N_CHIPS):
            ga = ga + ra_ref[chip]
            gb = gb + rb_ref[chip]
        outs[4 * n][...] = ga[ROW_LOSS:ROW_LOSS + 1, 0:128]
        grads = [ga[0:1, :], ga[1:2, :], ga[2:3, :], ga[3:4, :A_WIDTH], ga[3:4, A_WIDTH:],
                 gb[ROW_WS:ROW_WS + A_GROUPS * CHUNK, :].reshape(A_GROUPS, CHUNK, CHUNK),
                 gb[ROW_BS:ROW_BS + A_GROUPS, :], gb[ROW_SINK:ROW_SINK + 1, 0:4],
                 gb[ROW_REL:ROW_REL + 4, 0:N_BUCKETS]]
        for k in range(n):
            delta, m2, v2 = _adamw(w_refs[k][...], grads[k], m_refs[k][...], v_refs[k][...])
            g_outs[k][...] = grads[k]
            d_outs[k][...] = delta
            m_outs[k][...] = m2
            v_outs[k][...] = v2

    out_shape = [jax.ShapeDtypeStruct(w.shape, F32) for w in weights] * 4 + [jax.ShapeDtypeStruct((1, 128), F32)]
    return pl.pallas_call(
        body, name="adamw_small", out_shape=out_shape,
        in_specs=[VMEM_SPEC] * (2 + 3 * n), out_specs=[VMEM_SPEC] * (4 * n + 1),
    )(ra, rb, *weights, *moments_m, *moments_v)


def kernel(x, mem, pre_norm_g, post_norm_g, mem_norm_g, w_in, w_mem_kv, v_norm_g, v_norm_b, w_spatial, b_spatial, attn_sinks, rel_bias, w_out, loss_target, m_pre_norm_g, m_post_norm_g, m_mem_norm_g, m_w_in, m_w_mem_kv, m_v_norm_g, m_v_norm_b, m_w_spatial, m_b_spatial, m_attn_sinks, m_rel_bias, m_w_out, v_pre_norm_g, v_post_norm_g, v_mem_norm_g, v_w_in, v_w_mem_kv, v_v_norm_g, v_v_norm_b, v_w_spatial, v_b_spatial, v_attn_sinks, v_rel_bias, v_w_out):
    n_ex, seq, _ = x.shape
    n_tok = n_ex * seq
    x2 = x.reshape(n_tok, D_MODEL)
    tgt2 = loss_target.reshape(n_tok, D_MODEL)
    buckets = jnp.asarray(_bucket_map())
    shard_arr = (2 * lax.axis_index("x") + lax.axis_index("y")).astype(jnp.int32).reshape(1)
    w_sp = w_spatial[0]
    b_sp = jnp.broadcast_to(b_spatial[0][:, :, None], (A_GROUPS, CHUNK, CHUNK))
    w_in_t, m_w_in_t, v_w_in_t = (jnp.transpose(a[0]) for a in (w_in, m_w_in, v_w_in))
    rel_t, m_rel_t, v_rel_t = (jnp.transpose(a) for a in (rel_bias, m_rel_bias, v_rel_bias))

    x_arr = lax.axis_index("x").astype(jnp.int32).reshape(1)
    h_b, parts, (w_in_b, g_mkv, g_out) = _gather_and_project(x2, pre_norm_g, w_in_t, w_mem_kv[0], w_out[0], x_arr)
    w_mkv_b = g_mkv.reshape(D_MODEL, 2 * MEM_WIDTH)
    w_out_b = g_out.reshape(MIX_WIDTH, D_MODEL)

    bias = _make_bias(rel_t, buckets)
    mkv = _memkv_forward(mem, mem_norm_g, w_mkv_b)
    dout, dproj, dmkv, dwout, dvg, dvb, dws, dbs, dsink, drel, loss_vec, dgpost = _mix(
        parts, mkv, x2, tgt2, v_norm_g, v_norm_b, w_sp, b_sp, attn_sinks, bias, w_out_b, post_norm_g, n_ex, seq)

    dx, dgpre = _backward_projection(x2, dout, dproj, pre_norm_g, w_in_b)
    dwmkv, dgmem = _memkv_backward(mem, dmkv, mem_norm_g, w_mkv_b)
    small_a, small_b = _pack_small_grads(dgpre, dgpost, dgmem, dvg, dvb, dws, dbs, dsink, drel, loss_vec, buckets)

    shard_shapes = [w_mem_kv.shape[1:], w_out.shape[1:]]
    big = [g.reshape(N_CHIPS, 2, s[0] // 2, s[1]) for g, s in zip((dwmkv, dwout), shard_shapes)]
    (g_win, g_wmkv, g_wout), (ga, gb) = _reduce_gradients(dproj, h_b, big, [small_a, small_b], shard_arr)

    big_out = [_adamw_whole(g_win, w_in_t, m_w_in_t, v_w_in_t, "adamw_w_in"),
               _adamw_whole(g_wmkv, w_mem_kv[0], m_w_mem_kv[0], v_w_mem_kv[0], "adamw_w_mem_kv"),
               _adamw_whole(g_wout, w_out[0], m_w_out[0], v_w_out[0], "adamw_w_out")]
    small_w = [pre_norm_g, post_norm_g, mem_norm_g, v_norm_g, v_norm_b, w_sp, b_spatial[0], attn_sinks, rel_t]
    small_m = [m_pre_norm_g, m_post_norm_g, m_mem_norm_g, m_v_norm_g, m_v_norm_b, m_w_spatial[0], m_b_spatial[0],
               m_attn_sinks, m_rel_t]
    small_v = [v_pre_norm_g, v_post_norm_g, v_mem_norm_g, v_v_norm_g, v_v_norm_b, v_w_spatial[0], v_b_spatial[0],
               v_attn_sinks, v_rel_t]
    small_out = _adamw_small(ga, gb, small_w, small_m, small_v)
    n_small = len(small_w)

    outputs = [small_out[4 * n_small][0, 0], dx.reshape(x.shape)]
    for kind in range(4):
        s = small_out[kind * n_small:(kind + 1) * n_small]
        outputs += [s[0], s[1], s[2], jnp.transpose(big_out[0][kind])[None], big_out[1][kind][None], s[3], s[4],
                    s[5][None], s[6][None], s[7], jnp.transpose(s[8]), big_out[2][kind][None]]
    return tuple(outputs)
```

```python
import functools

import numpy as np
import jax
import jax.numpy as jnp
from jax import lax
from jax.experimental import pallas as pl
from jax.experimental.pallas import tpu as pltpu

F32 = jnp.float32
BF16 = jnp.bfloat16
MESH = pl.DeviceIdType.MESH

D_MODEL = 1024
CHUNK = 128
A_WIDTH = 512
A_GROUPS = 4
SWA_WIDTH = 256
KV_WIDTH = 128
MEM_WIDTH = 256
MEM_LEN = 256
MIX_WIDTH = 1024
IN_WIDTH = 2816
N_BUCKETS = 32
MAX_DISTANCE = 128
EPS = 1e-6
NEG = -1e30
QK_SCALE = 0.125
HALF_HEAD_PAIR = 64

ADAM_LR = 0.001
ADAM_B1 = 0.9
ADAM_B2 = 0.999
ADAM_EPS = 1e-08
ADAM_WD = 0.01
ADAM_STEP = 10

N_CHIPS = 4
TILE_CHUNKS = 2
TILE = TILE_CHUNKS * CHUNK
PROJ_TILE = 512
VMEM_LIMIT = 56 * 1024 * 1024

SMALL_A_ROWS = 8
ROW_LOSS = 4
ROW_WS = 0
ROW_BS = 512
ROW_SINK = 520
ROW_REL = 528
SMALL_B_ROWS = 536


def _mm(a, b):
    return lax.dot_general(a, b, (((1,), (0,)), ((), ())), preferred_element_type=F32)


def _mm_nt(a, b):
    return lax.dot_general(a, b, (((1,), (1,)), ((), ())), preferred_element_type=F32)


def _mm_tn(a, b):
    return lax.dot_general(a, b, (((0,), (0,)), ((), ())), preferred_element_type=F32)


def _bucket_map():
    qi = np.arange(CHUNK)[:, None]
    kj = np.arange(2 * CHUNK)[None, :]
    n = np.maximum(qi + CHUNK - kj, 0)
    max_exact = N_BUCKETS // 2
    large = max_exact + (np.log(np.maximum(n, 1) / max_exact) / np.log(MAX_DISTANCE / max_exact)
                         * (N_BUCKETS - max_exact)).astype(np.int32)
    large = np.minimum(large, N_BUCKETS - 1)
    return np.where(n < max_exact, n, large).astype(np.int32)


_GELU_C = 0.7978845608028654
_GELU_A = 0.044715
_GELU_K1 = 2.0 * _GELU_C
_GELU_K2 = 2.0 * _GELU_C * _GELU_A


def _gelu(x):
    x2 = x * x
    s = 1.0 / (1.0 + jnp.exp(x * (-_GELU_K1 - _GELU_K2 * x2)))
    return x * s, (s, x2)


def _gelu_grad(x, saved):
    s, x2 = saved
    return s + x * (s * (1.0 - s)) * (_GELU_K1 + 3.0 * _GELU_K2 * x2)


def _sigmoid(x):
    return 1.0 / (1.0 + jnp.exp(-x))


def _lane_lo(shape):
    return lax.broadcasted_iota(jnp.int32, shape, 1) < HALF_HEAD_PAIR


def _swa_variants(t):
    lo = _lane_lo(t.shape)
    tr = pltpu.roll(t, HALF_HEAD_PAIR, 1)
    zero = jnp.zeros_like(t)
    return (jnp.where(lo, t, zero).astype(BF16), jnp.where(lo, zero, tr).astype(BF16),
            jnp.where(lo, tr, zero).astype(BF16), jnp.where(lo, zero, t).astype(BF16))


def _swa_unvariants(d0, d1, d2, d3):
    lo = _lane_lo(d0.shape)
    zero = jnp.zeros_like(d0)
    rolled = jnp.where(lo, zero, d1) + jnp.where(lo, d2, zero)
    return jnp.where(lo, d0, zero) + jnp.where(lo, zero, d3) + pltpu.roll(rolled, HALF_HEAD_PAIR, 1)


def _mem_variants(t):
    out = []
    for pair in range(2):
        tp = t[:, pair * 128:(pair + 1) * 128]
        lo = _lane_lo(tp.shape)
        zero = jnp.zeros_like(tp)
        out.append(jnp.where(lo, tp, zero).astype(BF16))
        out.append(jnp.where(lo, zero, tp).astype(BF16))
    return out


def _mem_unvariants(d0, d1, d2, d3):
    lo = _lane_lo(d0.shape)
    return jnp.concatenate([jnp.where(lo, d0, d1), jnp.where(lo, d2, d3)], axis=-1)


def _softmax(logits, sinks):
    m = jnp.max(logits, axis=-1, keepdims=True)
    if sinks is not None:
        m = jnp.maximum(m, sinks)
    p = jnp.exp(logits - m)
    den = jnp.sum(p, axis=-1, keepdims=True)
    if sinks is None:
        return p * (1.0 / den), None
    es = jnp.exp(sinks - m)
    inv = 1.0 / (den + es)
    return p * inv, es * inv


def _band_valid(with_prev):
    qi = lax.broadcasted_iota(jnp.int32, (CHUNK, 2 * CHUNK), 0)
    kj = lax.broadcasted_iota(jnp.int32, (CHUNK, 2 * CHUNK), 1)
    in_cur = (kj >= CHUNK) & (kj - CHUNK <= qi)
    if not with_prev:
        return in_cur
    return in_cur | ((kj < CHUNK) & (kj > qi))


def _causal_weights(ws_ref):
    row = lax.broadcasted_iota(jnp.int32, (CHUNK, CHUNK), 0)
    col = lax.broadcasted_iota(jnp.int32, (CHUNK, CHUNK), 1)
    return [jnp.where(row >= col, ws_ref[g], 0.0).astype(BF16) for g in range(A_GROUPS)]


def _rows_to_lanes(a, n):
    return jnp.concatenate([a[c * CHUNK:(c + 1) * CHUNK] for c in range(n)], axis=1)


def _lanes_to_rows(a, n):
    w = a.shape[1] // n
    return jnp.concatenate([a[:, c * w:(c + 1) * w] for c in range(n)], axis=0)


def _stack_heads(pair01, pair23):
    return jnp.concatenate([pair01[:, :256], pair01[:, 256:], pair23[:, :256], pair23[:, 256:]], axis=0)


def _pair_heads(s, r):
    return (jnp.concatenate([s[0:r], s[r:2 * r]], axis=1), jnp.concatenate([s[2 * r:3 * r], s[3 * r:4 * r]], axis=1))


def _pair_operands(variants):
    return (jnp.concatenate(variants[0:2], axis=0), jnp.concatenate(variants[2:4], axis=0))


def _split_pair_grads(d_pairs):
    return d_pairs[0][:256], d_pairs[0][256:], d_pairs[1][:256], d_pairs[1][256:]


def _halves_bf16(a):
    return (a[:, :128].astype(BF16), a[:, 128:].astype(BF16))


def _group_a_forward(au, av, vg, vb, wm, bs_rows):
    gu, tu = _gelu(au)
    gv, tv = _gelu(av)
    ya, res = [], []
    for g in range(A_GROUPS):
        sl = slice(g * 128, (g + 1) * 128)
        xg = gv[:, sl]
        xc = xg - jnp.mean(xg, axis=-1, keepdims=True)
        rstd = lax.rsqrt(jnp.mean(xc * xc, axis=-1, keepdims=True) + EPS)
        xhat = xc * rstd
        vn = _rows_to_lanes((xhat * vg[:, sl] + vb[:, sl]).astype(BF16), TILE_CHUNKS)
        s = _lanes_to_rows(_mm(wm[g], vn), TILE_CHUNKS) + bs_rows[g]
        ya.append(gu[:, sl] * s)
        res.append((xhat, rstd, vn, s))
    return ya, dict(gu=gu, tu=tu, tv=tv, groups=res)


def _attention_probs(qp, k_pairs, bias, sink_col):
    logits = _stack_heads(_mm_nt(qp[0], k_pairs[0]), _mm_nt(qp[1], k_pairs[1]))
    if bias is not None:
        logits = logits + bias
    return _softmax(logits, sink_col)


def _attention_out(p, v_pairs, r):
    pp = _pair_heads(p.astype(BF16), r)
    return jnp.concatenate([_mm(pp[0], v_pairs[0]), _mm(pp[1], v_pairs[1])], axis=-1), pp


def _attention_backward(p, pp, do_pairs, qp, k_pairs, v_pairs, r):
    dp = _stack_heads(_mm_nt(do_pairs[0], v_pairs[0]), _mm_nt(do_pairs[1], v_pairs[1]))
    delta = jnp.sum(p * dp, axis=-1, keepdims=True)
    dl = p * (dp - delta)
    dlp = _pair_heads(dl.astype(BF16), r)
    dq = jnp.concatenate([_mm(dlp[0], k_pairs[0]), _mm(dlp[1], k_pairs[1])], axis=-1)
    dk = (_mm_tn(dlp[0], qp[0]), _mm_tn(dlp[1], qp[1]))
    dv = (_mm_tn(pp[0], do_pairs[0]), _mm_tn(pp[1], do_pairs[1]))
    return dl, delta, dq, dk, dv


def _tile_specs(n_tiles_ex, width):
    return pl.BlockSpec((TILE, width), lambda b, i: (b * n_tiles_ex + jnp.minimum(i, n_tiles_ex - 1), 0))


def _prev_chunk_spec(n_tiles_ex, width):
    def index(b, i):
        chunk = TILE_CHUNKS * jnp.minimum(i, n_tiles_ex - 1)
        return (b * n_tiles_ex * TILE_CHUNKS + jnp.maximum(chunk - 1, 0), 0)
    return pl.BlockSpec((CHUNK, width), index)


def _full_spec(shape):
    zeros = (0,) * len(shape)
    return pl.BlockSpec(shape, lambda *_: zeros)


SMEM_SPEC = pl.BlockSpec(memory_space=pltpu.SMEM)
ANY_SPEC = pl.BlockSpec(memory_space=pl.ANY)
VMEM_SPEC = pl.BlockSpec(memory_space=pltpu.VMEM)


def _make_bias(rel_bias_t, buckets):
    def body(rel_ref, bk_ref, out_ref):
        bk = bk_ref[...]
        for h in range(4):
            acc = jnp.zeros((CHUNK, 2 * CHUNK), F32)
            for b in range(N_BUCKETS):
                acc = jnp.where(bk == b, rel_ref[h, b], acc)
            for t, with_prev in enumerate((True, False)):
                out_ref[t, h * CHUNK:(h + 1) * CHUNK, :] = jnp.where(_band_valid(with_prev), acc, NEG)

    return pl.pallas_call(
        body, name="make_bias", out_shape=jax.ShapeDtypeStruct((2, 4 * CHUNK, 2 * CHUNK), F32),
        in_specs=[SMEM_SPEC, VMEM_SPEC], out_specs=VMEM_SPEC,
    )(rel_bias_t, buckets)


def _memkv_forward(mem, g_mem, w_mkv):
    n_ex = mem.shape[0]

    def body(mem_ref, g_ref, w_ref, out_ref):
        m = mem_ref[0]
        r = lax.rsqrt(jnp.mean(m * m, axis=-1, keepdims=True) + EPS)
        out_ref[0] = _mm((m * r * g_ref[...]).astype(BF16), w_ref[...])

    return pl.pallas_call(
        body, name="memkv_forward", grid=(n_ex,),
        out_shape=jax.ShapeDtypeStruct((n_ex, MEM_LEN, 2 * MEM_WIDTH), F32),
        in_specs=[pl.BlockSpec((1, MEM_LEN, D_MODEL), lambda b: (b, 0, 0)), _full_spec((1, D_MODEL)),
                  _full_spec((D_MODEL, 2 * MEM_WIDTH))],
        out_specs=pl.BlockSpec((1, MEM_LEN, 2 * MEM_WIDTH), lambda b: (b, 0, 0)),
    )(mem, g_mem, w_mkv)


PROJ_WIDTHS = (A_WIDTH, A_WIDTH, SWA_WIDTH, KV_WIDTH, KV_WIDTH, MEM_WIDTH, MIX_WIDTH)
PROJ_OFFSETS = tuple(int(v) for v in np.cumsum((0,) + PROJ_WIDTHS))


HALF_WIDTH = IN_WIDTH // 2
HALF_PARTS = ((0, 1, 2, 3), (4, 5, 6))


def _gather_and_project(x2, g_pre, w_in_s, w_mkv_s, w_out_s, x_arr):
    n_tok = x2.shape[0]
    n_tiles = n_tok // PROJ_TILE
    last = n_tiles - 1
    shapes = [w_in_s.shape, w_mkv_s.shape, w_out_s.shape]
    n_w = len(shapes)

    def body(x_sref, x_ref, g_ref, win_hbm, wmkv_hbm, wout_hbm, h_ref, *refs):
        part_refs, refs = refs[:len(PROJ_WIDTHS)], refs[len(PROJ_WIDTHS):]
        gin_hbm, gmkv_hbm, gout_hbm, wg, stage_in, stage_mkv, stage_out, own_mkv, own_out = refs[:9]
        send_sems, recv_sems, local_sems = refs[9:]
        p, t = pl.program_id(0), pl.program_id(1)
        x, y, c = lax.axis_index("x"), lax.axis_index("y"), lax.axis_index("c")
        me, sibling = (x, y, c), (x, y, 1 - c)
        my_shard = 2 * x + y
        gathered = [wg, gmkv_hbm, gout_hbm]

        def half_rows(w, shard, half):
            rows = shapes[w][0] // 2
            if w == 0:
                return wg.at[pl.ds(pl.multiple_of(shard * shapes[0][0] + half * rows, 16), rows), :]
            return gathered[w].at[shard, pl.ds(half * rows, rows), :]

        def first(w, rel):
            src = half_rows(w, my_shard, c) if w == 0 else (own_mkv, own_out)[w - 1].at[
                pl.ds(c * (shapes[w][0] // 2), shapes[w][0] // 2), :]
            k = 3 * w + rel - 1
            return pltpu.make_async_remote_copy(
                src_ref=src, dst_ref=half_rows(w, my_shard, c), send_sem=send_sems.at[k], recv_sem=recv_sems.at[k],
                device_id=(x ^ (rel >> 1), y ^ (rel & 1), c), device_id_type=MESH)

        def landed(w, rel):
            k = 3 * w + rel - 1
            ref = half_rows(w, my_shard ^ rel, c)
            return pltpu.make_async_remote_copy(src_ref=ref, dst_ref=ref, send_sem=send_sems.at[k],
                                                recv_sem=recv_sems.at[k], device_id=me, device_id_type=MESH)

        def passed(w, rel, half, to):
            k = 9 + 3 * w + rel - 1
            ref = half_rows(w, my_shard ^ rel, half)
            return pltpu.make_async_remote_copy(src_ref=ref, dst_ref=ref, send_sem=send_sems.at[k],
                                                recv_sem=recv_sems.at[k], device_id=to, device_id_type=MESH)

        def pass_on(w, rels):
            for rel in rels:
                landed(w, rel).wait_recv()
                passed(w, rel, c, sibling).start()
            for rel in rels:
                passed(w, rel, 1 - c, me).wait_recv()

        own_stores = [pltpu.make_async_copy(own_mkv, gmkv_hbm.at[my_shard], local_sems.at[3]),
                      pltpu.make_async_copy(own_out, gout_hbm.at[my_shard], local_sems.at[4])]

        @pl.when((p == 0) & (t == 0))
        def _():
            loads = [pltpu.make_async_copy(src, dst, local_sems.at[k]) for k, (src, dst) in enumerate(
                ((win_hbm, stage_in), (wmkv_hbm, stage_mkv), (wout_hbm, stage_out)))]
            for cp in loads:
                cp.start()
            loads[0].wait()
            wg[pl.ds(pl.multiple_of(my_shard * shapes[0][0], 16), shapes[0][0]), :] = stage_in[...].astype(BF16)
            for rel in (1, 2):
                first(0, rel).start()
            loads[1].wait()
            loads[2].wait()
            own_mkv[...] = stage_mkv[...].astype(BF16)
            own_out[...] = stage_out[...].astype(BF16)
            for cp in own_stores:
                cp.start()
            pass_on(0, (1,))
            first(0, 3).start()

        @pl.when((p == 0) & (t == n_tiles // 2))
        def _():
            for w in (1, 2):
                for rel in (1, 2, 3):
                    first(w, rel).start()

        @pl.when((p == 1) & (t == 0))
        def _():
            pass_on(0, (2, 3))

        xv = x_ref[...]
        r = lax.rsqrt(jnp.mean(xv * xv, axis=-1, keepdims=True) + EPS)
        h = (xv * r * g_ref[...]).astype(BF16)

        @pl.when(p == 0)
        def _():
            h_ref[...] = h

        for hh in range(2):
            @pl.when((p ^ x_sref[0]) == hh)
            def _():
                proj = _mm_nt(h, wg[hh * HALF_WIDTH:(hh + 1) * HALF_WIDTH, :])
                for k in HALF_PARTS[hh]:
                    lo = PROJ_OFFSETS[k] - hh * HALF_WIDTH
                    part_refs[k][...] = proj[:, lo:lo + PROJ_WIDTHS[k]]

        @pl.when((p == 1) & (t == last))
        def _():
            store = pltpu.make_async_copy(wg, gin_hbm, local_sems.at[5])
            store.start()
            for w in (1, 2):
                pass_on(w, (1, 2, 3))
            for w in range(n_w):
                for rel in (1, 2, 3):
                    first(w, rel).wait_send()
                    passed(w, rel, c, sibling).wait_send()
            for cp in own_stores:
                cp.wait()
            store.wait()

    def active_in(hh):
        def index(p, t, xs):
            return (jnp.where((p ^ xs[0]) == hh, t, jnp.where(p == 0, 0, last)), 0)
        return index

    part_specs = [pl.BlockSpec((PROJ_TILE, PROJ_WIDTHS[k]), active_in(hh)) for hh in range(2) for k in HALF_PARTS[hh]]
    vmem = pltpu.VMEM
    out = pl.pallas_call(
        body, name="gather_and_project",
        out_shape=[jax.ShapeDtypeStruct((n_tok, D_MODEL), BF16)]
        + [jax.ShapeDtypeStruct((n_tok, w), F32) for w in PROJ_WIDTHS]
        + [jax.ShapeDtypeStruct((N_CHIPS * shapes[0][0], shapes[0][1]), BF16)]
        + [jax.ShapeDtypeStruct((N_CHIPS,) + s, BF16) for s in shapes[1:]],
        grid_spec=pltpu.PrefetchScalarGridSpec(
            num_scalar_prefetch=1, grid=(2, n_tiles),
            in_specs=[pl.BlockSpec((PROJ_TILE, D_MODEL), lambda p, t, xs: (t, 0)),
                      pl.BlockSpec((1, D_MODEL), lambda p, t, xs: (0, 0)), ANY_SPEC, ANY_SPEC, ANY_SPEC],
            out_specs=[pl.BlockSpec((PROJ_TILE, D_MODEL), lambda p, t, xs: (jnp.where(p == 0, t, last), 0))]
            + part_specs + [ANY_SPEC] * 3,
            scratch_shapes=[vmem((N_CHIPS * shapes[0][0], shapes[0][1]), BF16), vmem(shapes[0], F32),
                            vmem(shapes[1], F32), vmem(shapes[2], F32), vmem(shapes[1], BF16), vmem(shapes[2], BF16),
                            pltpu.SemaphoreType.DMA((18,)), pltpu.SemaphoreType.DMA((18,)),
                            pltpu.SemaphoreType.DMA((6,))]),
        compiler_params=pltpu.CompilerParams(vmem_limit_bytes=VMEM_LIMIT),
    )(x_arr, x2, g_pre, w_in_s, w_mkv_s, w_out_s)
    h, parts, weights = out[0], out[1:1 + len(PROJ_WIDTHS)], out[1 + len(PROJ_WIDTHS):]
    return h, list(parts), weights


def _load_chunk(j, i, sk_ref, sv_ref, skp_ref, svp_ref):
    rows = slice(j * CHUNK, (j + 1) * CHUNK)
    if j == 0:
        k_prev, v_prev, table = skp_ref[...], svp_ref[...], jnp.where(i > 0, 0, 1)
    else:
        prev = slice((j - 1) * CHUNK, j * CHUNK)
        k_prev, v_prev, table = sk_ref[prev, :], sv_ref[prev, :], 0
    k_pairs = _pair_operands(_swa_variants(jnp.concatenate([k_prev, sk_ref[rows, :]], axis=0)))
    v_pairs = _pair_operands(_swa_variants(jnp.concatenate([v_prev, sv_ref[rows, :]], axis=0)))
    return rows, k_pairs, v_pairs, table


def _tile_constants(ws_ref, bs_ref, sink_ref, mkv_ref):
    wm = _causal_weights(ws_ref)
    bs_rows = [jnp.concatenate([bs_ref[g]] * TILE_CHUNKS, axis=0) for g in range(A_GROUPS)]
    sink_col = jnp.max(jnp.concatenate([jnp.full((CHUNK, 128), sink_ref[0, h], F32) for h in range(4)], axis=0),
                       axis=-1, keepdims=True)
    mkv_v = mkv_ref[0]
    mk_pairs = _pair_operands(_mem_variants(mkv_v[:, :MEM_WIDTH]))
    mv_pairs = _pair_operands(_mem_variants(mkv_v[:, MEM_WIDTH:]))
    return wm, bs_rows, sink_col, mk_pairs, mv_pairs


def _mix(parts, mkv, x2, tgt2, v_g, v_b, w_sp, b_sp, sinks, bias, w_out, g_post, n_ex, seq):
    n_tiles_ex = seq // TILE
    n_tok = n_ex * seq
    au, av, sq, sk, sv, mq, z = parts
    col = dict(zip(("au", "av", "sq", "sk", "sv", "mq", "z"),
                   (slice(PROJ_OFFSETS[k], PROJ_OFFSETS[k + 1]) for k in range(len(PROJ_WIDTHS)))))
    before_kv, after_kv = slice(0, col["sk"].start), slice(col["sv"].stop, IN_WIDTH)

    def body(au_ref, av_ref, sq_ref, sk_ref, sv_ref, skp_ref, svp_ref, mq_ref, z_ref, mkv_ref, x_ref, tgt_ref,
             vg_ref, vb_ref, ws_ref, bs_ref, sink_ref, bias_ref, wout_ref, gpost_ref,
             dout_ref, dproj_ref, dmkv_ref, dwout_ref, dvg_ref, dvb_ref, dws_ref, dbs_ref, dsink_ref, drel_ref,
             loss_ref, dgpost_ref, carry_dp, carry_k, carry_v):
        b, i = pl.program_id(0), pl.program_id(1)

        @pl.when((b == 0) & (i == 0))
        def _():
            for ref in (dwout_ref, dvg_ref, dvb_ref, dws_ref, dbs_ref, dsink_ref, drel_ref, loss_ref, dgpost_ref):
                ref[...] = jnp.zeros_like(ref)

        @pl.when(i == 0)
        def _():
            dmkv_ref[...] = jnp.zeros_like(dmkv_ref)
            carry_k[...] = jnp.zeros_like(carry_k)
            carry_v[...] = jnp.zeros_like(carry_v)

        @pl.when(i > 0)
        def _():
            dproj_ref[:, before_kv] = carry_dp[:, before_kv]
            dproj_ref[:, after_kv] = carry_dp[:, after_kv]

        @pl.when(i < n_tiles_ex)
        def _():
            wm, bs_rows, sink_col, mk_pairs, mv_pairs = _tile_constants(ws_ref, bs_ref, sink_ref, mkv_ref)
            vg = vg_ref[...]

            au_v, av_v = au_ref[...], av_ref[...]
            ya, res = _group_a_forward(au_v, av_v, vg, vb_ref[...], wm, bs_rows)
            swa, yb = [], []
            for j in range(TILE_CHUNKS):
                rows, k_pairs, v_pairs, table = _load_chunk(j, i, sk_ref, sv_ref, skp_ref, svp_ref)
                qp = _halves_bf16(sq_ref[rows, :] * QK_SCALE)
                p, ps = _attention_probs(qp, k_pairs, bias_ref[table], sink_col)
                out, pp = _attention_out(p, v_pairs, CHUNK)
                yb.append(out)
                swa.append((rows, k_pairs, v_pairs, qp, p, ps, pp))
            mqp = _halves_bf16(mq_ref[...] * QK_SCALE)
            pm, _ = _attention_probs(mqp, mk_pairs, None, None)
            yc, ppm = _attention_out(pm, mv_pairs, TILE)
            ycat = jnp.concatenate(ya + [jnp.concatenate(yb, axis=0), yc], axis=-1)

            zv = z_ref[...]
            sig = _sigmoid(zv)
            sz = zv * sig
            y_b = (ycat * sz).astype(BF16)
            o = _mm(y_b, wout_ref[...])
            r2 = lax.rsqrt(jnp.mean(o * o, axis=-1, keepdims=True) + EPS)
            nrm = o * r2
            gp = gpost_ref[...]
            diff = x_ref[...] + nrm * gp - tgt_ref[...]
            loss_ref[...] += jnp.sum(diff * diff) * (0.5 / D_MODEL)
            dout = diff * (1.0 / D_MODEL)
            dout_ref[...] = dout
            dgpost_ref[...] += jnp.sum(dout * nrm, axis=0, keepdims=True)
            dn = dout * gp
            do_b = (r2 * (dn - nrm * jnp.mean(dn * nrm, axis=-1, keepdims=True))).astype(BF16)
            dwout_ref[...] += _mm_tn(y_b, do_b)
            dy = _mm_nt(do_b, wout_ref[...])
            carry_dp[:, col["z"]] = (dy * ycat * (sig * (1.0 + zv * (1.0 - sig)))).astype(BF16)
            dyc = dy * sz

            dgu, dgv = [], []
            for g in range(A_GROUPS):
                sl = slice(g * 128, (g + 1) * 128)
                xhat, rstd, vn, s = res["groups"][g]
                dya = dyc[:, sl]
                dgu.append(dya * s)
                ds = dya * res["gu"][:, sl]
                dbs_ref[:, sl] += sum(ds[c * CHUNK:(c + 1) * CHUNK] for c in range(TILE_CHUNKS))
                ds_b = _rows_to_lanes(ds.astype(BF16), TILE_CHUNKS)
                dws_ref[g] += _mm_nt(ds_b, vn)
                dvn = _lanes_to_rows(_mm_tn(wm[g], ds_b), TILE_CHUNKS)
                dvg_ref[:, sl] += jnp.sum(dvn * xhat, axis=0, keepdims=True)
                dvb_ref[:, sl] += jnp.sum(dvn, axis=0, keepdims=True)
                dxh = dvn * vg[:, sl]
                dgv.append(rstd * (dxh - jnp.mean(dxh, axis=-1, keepdims=True)
                                   - xhat * jnp.mean(dxh * xhat, axis=-1, keepdims=True)))
            carry_dp[:, col["au"]] = (jnp.concatenate(dgu, axis=-1) * _gelu_grad(au_v, res["tu"])).astype(BF16)
            carry_dp[:, col["av"]] = (jnp.concatenate(dgv, axis=-1) * _gelu_grad(av_v, res["tv"])).astype(BF16)

            lane4 = lax.broadcasted_iota(jnp.int32, (1, 128), 1)
            dsink_vec = jnp.zeros((1, 128), F32)
            dk_parts, dv_parts = [], []
            for rows, k_pairs, v_pairs, qp, p, ps, pp in swa:
                do_pairs = _halves_bf16(dyc[rows, A_WIDTH:A_WIDTH + SWA_WIDTH])
                dl, delta, dq, dk, dv = _attention_backward(p, pp, do_pairs, qp, k_pairs, v_pairs, CHUNK)
                sink_terms = ps * delta
                for h in range(4):
                    dsink_vec = dsink_vec + jnp.where(lane4 == h, -jnp.sum(sink_terms[h * CHUNK:(h + 1) * CHUNK]), 0.0)
                drel_ref[...] += dl
                carry_dp[rows, col["sq"]] = (dq * QK_SCALE).astype(BF16)
                dk_parts.append(_swa_unvariants(*_split_pair_grads(dk)))
                dv_parts.append(_swa_unvariants(*_split_pair_grads(dv)))
            dsink_ref[...] += dsink_vec

            dc_pairs = _halves_bf16(dyc[:, A_WIDTH + SWA_WIDTH:])
            _, _, dmq, dmk, dmv = _attention_backward(pm, ppm, dc_pairs, mqp, mk_pairs, mv_pairs, TILE)
            carry_dp[:, col["mq"]] = (dmq * QK_SCALE).astype(BF16)
            dmkv_ref[0] += jnp.concatenate([_mem_unvariants(*_split_pair_grads(dmk)),
                                            _mem_unvariants(*_split_pair_grads(dmv))], axis=-1)

            for parts_c, carry, cols in ((dk_parts, carry_k, col["sk"]), (dv_parts, carry_v, col["sv"])):
                @pl.when(i > 0)
                def _():
                    dproj_ref[:, cols] = (carry[...] + jnp.concatenate(
                        [jnp.zeros((TILE - CHUNK, KV_WIDTH), F32), parts_c[0][:CHUNK]], axis=0)).astype(BF16)
                new = [parts_c[0][CHUNK:]]
                for j in range(1, TILE_CHUNKS):
                    new[-1] = new[-1] + parts_c[j][:CHUNK]
                    new.append(parts_c[j][CHUNK:])
                carry[...] = jnp.concatenate(new, axis=0)

        @pl.when(i == n_tiles_ex)
        def _():
            dproj_ref[:, col["sk"]] = carry_k[...].astype(BF16)
            dproj_ref[:, col["sv"]] = carry_v[...].astype(BF16)

    tile = functools.partial(_tile_specs, n_tiles_ex)
    prev = functools.partial(_prev_chunk_spec, n_tiles_ex)
    late = pl.BlockSpec((TILE, IN_WIDTH), lambda b, i: (b * n_tiles_ex + jnp.maximum(i - 1, 0), 0))
    return pl.pallas_call(
        body, name="mix", grid=(n_ex, n_tiles_ex + 1),
        out_shape=[jax.ShapeDtypeStruct((n_tok, D_MODEL), F32), jax.ShapeDtypeStruct((n_tok, IN_WIDTH), BF16),
                   jax.ShapeDtypeStruct((n_ex, MEM_LEN, 2 * MEM_WIDTH), F32),
                   jax.ShapeDtypeStruct((MIX_WIDTH, D_MODEL), F32), jax.ShapeDtypeStruct((1, A_WIDTH), F32),
                   jax.ShapeDtypeStruct((1, A_WIDTH), F32), jax.ShapeDtypeStruct((A_GROUPS, CHUNK, CHUNK), F32),
                   jax.ShapeDtypeStruct((CHUNK, A_WIDTH), F32), jax.ShapeDtypeStruct((1, 128), F32),
                   jax.ShapeDtypeStruct((4 * CHUNK, 2 * CHUNK), F32), jax.ShapeDtypeStruct((1, 128), F32),
                   jax.ShapeDtypeStruct((1, D_MODEL), F32)],
        in_specs=[tile(A_WIDTH), tile(A_WIDTH), tile(SWA_WIDTH), tile(KV_WIDTH), tile(KV_WIDTH),
                  prev(KV_WIDTH), prev(KV_WIDTH), tile(MEM_WIDTH), tile(MIX_WIDTH),
                  pl.BlockSpec((1, MEM_LEN, 2 * MEM_WIDTH), lambda b, i: (b, 0, 0)),
                  tile(D_MODEL), tile(D_MODEL),
                  _full_spec((1, A_WIDTH)), _full_spec((1, A_WIDTH)), _full_spec((A_GROUPS, CHUNK, CHUNK)),
                  _full_spec((A_GROUPS, CHUNK, CHUNK)), SMEM_SPEC, _full_spec((2, 4 * CHUNK, 2 * CHUNK)),
                  _full_spec((MIX_WIDTH, D_MODEL)), _full_spec((1, D_MODEL))],
        out_specs=[tile(D_MODEL), late, pl.BlockSpec((1, MEM_LEN, 2 * MEM_WIDTH), lambda b, i: (b, 0, 0)),
                   _full_spec((MIX_WIDTH, D_MODEL)), _full_spec((1, A_WIDTH)), _full_spec((1, A_WIDTH)),
                   _full_spec((A_GROUPS, CHUNK, CHUNK)), _full_spec((CHUNK, A_WIDTH)), _full_spec((1, 128)),
                   _full_spec((4 * CHUNK, 2 * CHUNK)), _full_spec((1, 128)), _full_spec((1, D_MODEL))],
        scratch_shapes=[pltpu.VMEM((TILE, IN_WIDTH), BF16), pltpu.VMEM((TILE, KV_WIDTH), F32),
                        pltpu.VMEM((TILE, KV_WIDTH), F32)],
        compiler_params=pltpu.CompilerParams(vmem_limit_bytes=VMEM_LIMIT),
    )(au, av, sq, sk, sv, sk, sv, mq, z, mkv, x2, tgt2, v_g, v_b, w_sp, b_sp, sinks, bias, w_out, g_post)


BWD_PROJ_TILE = 512


def _backward_projection(x2, dout, dproj, g_pre, w_in_t):
    n_tok = x2.shape[0]
    n_steps = n_tok // BWD_PROJ_TILE

    def body(x_ref, dout_ref, dp_ref, g_ref, w_hbm, dx_ref, dgpre_ref, w_vmem, sem):
        @pl.when(pl.program_id(0) == 0)
        def _():
            load = pltpu.make_async_copy(w_hbm, w_vmem, sem)
            load.start()
            dgpre_ref[...] = jnp.zeros_like(dgpre_ref)
            load.wait()

        xv = x_ref[...]
        r = lax.rsqrt(jnp.mean(xv * xv, axis=-1, keepdims=True) + EPS)
        xn = xv * r
        dh = _mm(dp_ref[...], w_vmem[...])
        dgpre_ref[...] += jnp.sum(dh * xn, axis=0, keepdims=True)
        dhg = dh * g_ref[...]
        dx_ref[...] = r * (dhg - xn * jnp.mean(dhg * xn, axis=-1, keepdims=True)) + dout_ref[...]

    row = lambda w: pl.BlockSpec((BWD_PROJ_TILE, w), lambda i: (i, 0))
    return pl.pallas_call(
        body, name="backward_projection", grid=(n_steps,),
        out_shape=[jax.ShapeDtypeStruct((n_tok, D_MODEL), F32), jax.ShapeDtypeStruct((1, D_MODEL), F32)],
        in_specs=[row(D_MODEL), row(D_MODEL), row(IN_WIDTH), _full_spec((1, D_MODEL)), ANY_SPEC],
        out_specs=[row(D_MODEL), _full_spec((1, D_MODEL))],
        scratch_shapes=[pltpu.VMEM((IN_WIDTH, D_MODEL), BF16), pltpu.SemaphoreType.DMA],
        input_output_aliases={1: 0},
        compiler_params=pltpu.CompilerParams(vmem_limit_bytes=VMEM_LIMIT),
    )(x2, dout, dproj, g_pre, w_in_t)


SHARD_ROWS = IN_WIDTH // N_CHIPS
SHARD_WINDOW = 768
SHARD_HALF = SHARD_ROWS // 2
DWIN_TILE = 2048
N_REL = N_CHIPS - 1


def _shard_window_start(shard):
    return (shard * SHARD_ROWS // 128) * 128


def _reduce_gradients(dproj, h, big, small, shard_arr):
    n_tok = h.shape[0]
    tile = min(DWIN_TILE, n_tok)
    n_sub = n_tok // tile
    last = N_CHIPS - 1
    n_big, n_small = len(big), len(small)
    big_half = [g.shape[2:] for g in big]
    sem_big_d2d = 2 * N_CHIPS
    sem_big_ici = sem_big_d2d + n_big
    sem_big_swap = sem_big_ici + N_REL * n_big
    sem_small_d2d = sem_big_swap + n_big
    sem_small_ici = sem_small_d2d + n_small
    n_sems = sem_small_ici + N_REL * n_small
    loc_small = n_big
    loc_out_win = loc_small + n_small
    loc_out_big = loc_out_win + 2
    loc_out_small = loc_out_big + 2 * n_big
    n_local = loc_out_small + n_small

    def relation_of_slot(s):
        return (s + 2) % N_REL + 1

    def shard_of_slot(s, my_shard):
        return my_shard ^ jnp.where(s == last, 0, relation_of_slot(s))

    def body(shard_ref, dp_ref, h_hbm, *refs):
        h_vmem, h_sem, refs = refs[-2], refs[-1], refs[:-2]
        big_hbm, refs = refs[:n_big], refs[n_big:]
        small_hbm, refs = refs[:n_small], refs[n_small:]
        out_hbm, refs = refs[0], refs[1:]
        big_out, refs = refs[:n_big], refs[n_big:]
        small_out, refs = refs[:n_small], refs[n_small:]
        part, recv_d2d, send_ici, recv_ici, mine_buf, other_buf = refs[:6]
        refs = refs[6:]
        big_own, big_recv, big_send, big_land, big_mine, big_other = (
            refs[k * n_big:(k + 1) * n_big] for k in range(6))
        refs = refs[6 * n_big:]
        small_own, small_recv, small_all = (refs[k * n_small:(k + 1) * n_small] for k in range(3))
        send_sems, recv_sems, local_sems = refs[3 * n_small:]

        s, t = pl.program_id(0), pl.program_id(1)
        x, y, c = lax.axis_index("x"), lax.axis_index("y"), lax.axis_index("c")
        my_chip = 2 * x + y
        sibling = (x, y, 1 - c)
        my_rows = pl.ds(pl.multiple_of(c * SHARD_HALF, 8), SHARD_HALF)
        other_rows = pl.ds(pl.multiple_of((1 - c) * SHARD_HALF, 8), SHARD_HALF)

        def remote(src, dst, k, to):
            return pltpu.make_async_remote_copy(src_ref=src, dst_ref=dst, send_sem=send_sems.at[k],
                                                recv_sem=recv_sems.at[k], device_id=to, device_id_type=MESH)

        def chip_at(rel):
            return (x ^ (rel >> 1), y ^ (rel & 1), c)

        def to_sibling(k):
            return remote(part.at[k % 2, other_rows, :], recv_d2d.at[k], k, sibling)

        def to_chip(k):
            return remote(send_ici.at[k], recv_ici.at[k], N_CHIPS + k, chip_at(relation_of_slot(k)))

        swap = remote(mine_buf, other_buf, 2 * N_CHIPS - 1, sibling)
        big_load = [pltpu.make_async_copy(big_hbm[w].at[:, pl.ds(c, 1)], big_own[w], local_sems.at[w])
                    for w in range(n_big)]
        big_to_sibling = [remote(big_hbm[w].at[:, pl.ds(1 - c, 1)], big_recv[w], sem_big_d2d + w, sibling)
                          for w in range(n_big)]
        big_to_chip = [[remote(big_send[w].at[k], big_land[w].at[k], sem_big_ici + N_REL * w + k, chip_at(k + 1))
                        for k in range(N_REL)] for w in range(n_big)]
        big_swap = [remote(big_mine[w], big_other[w], sem_big_swap + w, sibling) for w in range(n_big)]
        small_load = [pltpu.make_async_copy(small_hbm[i], small_own[i], local_sems.at[loc_small + i])
                      for i in range(n_small)]
        small_to_sibling = [remote(small_hbm[i], small_recv[i], sem_small_d2d + i, sibling) for i in range(n_small)]
        small_to_chip = [[remote(small_all[i].at[my_chip], small_all[i].at[my_chip],
                                 sem_small_ici + N_REL * i + k, chip_at(k + 1))
                          for k in range(N_REL)] for i in range(n_small)]

        @pl.when((s == 0) & (t == 0))
        def _():
            h_load = pltpu.make_async_copy(h_hbm, h_vmem, h_sem)
            h_load.start()
            for cp in big_load + big_to_sibling + small_load + small_to_sibling:
                cp.start()
            h_load.wait()

        @pl.when((s == 0) & (t == n_sub - 1))
        def _():
            for cp in big_load + small_load:
                cp.wait()
            for cp in big_to_sibling + small_to_sibling:
                cp.wait_recv()
                cp.wait_send()
            for w in range(n_big):
                for k in range(N_REL):
                    shard = my_chip ^ (k + 1)
                    big_send[w][k] = (big_own[w][shard, 0] + big_recv[w][shard, 0]).astype(BF16)
                    big_to_chip[w][k].start()
            for i in range(n_small):
                small_all[i][my_chip] = small_own[i][...] + small_recv[i][...]
                for k in range(N_REL):
                    small_to_chip[i][k].start()

        @pl.when((s > 0) & (t == 0))
        def _():
            k = s - 1
            cp = to_sibling(k)
            cp.wait_recv()
            cp.wait_send()
            send_ici[k] = (part[k % 2, my_rows, :] + recv_d2d[k]).astype(BF16)
            to_chip(k).start()

        def big_rows(w, half):
            rows = big_half[w][0]
            return big_out[w].at[pl.ds(pl.multiple_of(half * rows, 8), rows), :]

        big_store_mine = [pltpu.make_async_copy(big_mine[w], big_rows(w, c), local_sems.at[loc_out_big + 2 * w])
                          for w in range(n_big)]
        big_store_other = [pltpu.make_async_copy(big_other[w], big_rows(w, 1 - c),
                                                 local_sems.at[loc_out_big + 2 * w + 1]) for w in range(n_big)]
        small_store = [pltpu.make_async_copy(small_all[i], small_out[i], local_sems.at[loc_out_small + i])
                       for i in range(n_small)]

        @pl.when((s == last) & (t == 0))
        def _():
            for w in range(n_big):
                total = big_own[w][my_chip, 0] + big_recv[w][my_chip, 0]
                for k in range(N_REL):
                    big_to_chip[w][k].wait_recv()
                    total = total + big_land[w][k].astype(F32)
                big_mine[w][...] = total
                big_swap[w].start()
                big_store_mine[w].start()
            for i in range(n_small):
                for k in range(N_REL):
                    small_to_chip[i][k].wait_recv()
                small_store[i].start()

        r = _mm_tn(dp_ref[...], h_vmem[pl.ds(pl.multiple_of(t * tile, tile), tile), :])
        odd = shard_of_slot(s, shard_ref[0]) % 2
        for parity in range(2):
            rows = r[64 * parity:64 * parity + SHARD_ROWS]

            @pl.when((odd == parity) & (t == 0))
            def _():
                part[s % 2] = rows

            @pl.when((odd == parity) & (t > 0))
            def _():
                part[s % 2] += rows

        @pl.when(t == n_sub - 1)
        def _():
            to_sibling(s).start()

        @pl.when((s == last) & (t == n_sub - 1))
        def _():
            cp = to_sibling(last)
            cp.wait_recv()
            cp.wait_send()
            total = part[last % 2, my_rows, :] + recv_d2d[last]
            for k in range(last):
                to_chip(k).wait_recv()
                total = total + recv_ici[k].astype(F32)
            mine_buf[...] = total
            swap.start()
            out_mine = pltpu.make_async_copy(mine_buf, out_hbm.at[my_rows, :], local_sems.at[0])
            out_mine.start()
            swap.wait_recv()
            out_other = pltpu.make_async_copy(other_buf, out_hbm.at[other_rows, :], local_sems.at[1])
            out_other.start()
            for w in range(n_big):
                big_swap[w].wait_recv()
                big_store_other[w].start()
            stores = [out_mine, out_other] + big_store_mine + big_store_other + small_store
            for k in range(last):
                to_chip(k).wait_send()
            swap.wait_send()
            for w in range(n_big):
                for k in range(N_REL):
                    big_to_chip[w][k].wait_send()
                big_swap[w].wait_send()
            for i in range(n_small):
                for k in range(N_REL):
                    small_to_chip[i][k].wait_send()
            for cp in stores:
                cp.wait()

    half = (SHARD_HALF, D_MODEL)
    vmem = pltpu.VMEM
    scratch = [vmem((2, SHARD_ROWS, D_MODEL), F32), vmem((N_CHIPS,) + half, F32),
               vmem((N_REL,) + half, BF16), vmem((N_REL,) + half, BF16), vmem(half, F32), vmem(half, F32)]
    scratch += [vmem((N_CHIPS, 1) + hs, F32) for hs in big_half] * 2
    scratch += [vmem((N_REL,) + hs, BF16) for hs in big_half] * 2
    scratch += [vmem(hs, F32) for hs in big_half] * 2
    scratch += [vmem(a.shape, F32) for a in small] * 2 + [vmem((N_CHIPS,) + a.shape, F32) for a in small]
    scratch += [pltpu.SemaphoreType.DMA((n_sems,)), pltpu.SemaphoreType.DMA((n_sems,)),
                pltpu.SemaphoreType.DMA((n_local,)), vmem(h.shape, BF16), pltpu.SemaphoreType.DMA]
    n_hbm = n_big + n_small
    out = pl.pallas_call(
        body, name="reduce_gradients",
        out_shape=[jax.ShapeDtypeStruct((SHARD_ROWS, D_MODEL), F32)]
        + [jax.ShapeDtypeStruct((2 * hs[0], hs[1]), F32) for hs in big_half]
        + [jax.ShapeDtypeStruct((N_CHIPS,) + a.shape, F32) for a in small],
        grid_spec=pltpu.PrefetchScalarGridSpec(
            num_scalar_prefetch=1, grid=(N_CHIPS, n_sub),
            in_specs=[pl.BlockSpec((pl.Element(tile), pl.Element(SHARD_WINDOW)),
                                   lambda s, t, m: (t * tile, _shard_window_start(shard_of_slot(s, m[0])))),
                      ANY_SPEC] + [ANY_SPEC] * n_hbm,
            out_specs=[ANY_SPEC] * (1 + n_hbm),
            scratch_shapes=scratch),
        compiler_params=pltpu.CompilerParams(vmem_limit_bytes=VMEM_LIMIT),
    )(shard_arr, dproj, h, *big, *small)
    return out[:1 + n_big], out[1 + n_big:]


def _memkv_backward(mem, dmkv, g_mem, w_mkv):
    n_ex = mem.shape[0]

    def body(mem_ref, d_ref, g_ref, w_ref, dw_ref, dg_ref):
        @pl.when(pl.program_id(0) == 0)
        def _():
            dw_ref[...] = jnp.zeros_like(dw_ref)
            dg_ref[...] = jnp.zeros_like(dg_ref)

        m = mem_ref[0]
        mn = m * lax.rsqrt(jnp.mean(m * m, axis=-1, keepdims=True) + EPS)
        d_b = d_ref[0].astype(BF16)
        dw_ref[...] += _mm_tn((mn * g_ref[...]).astype(BF16), d_b)
        dg_ref[...] += jnp.sum(_mm_nt(d_b, w_ref[...]) * mn, axis=0, keepdims=True)

    return pl.pallas_call(
        body, name="memkv_backward", grid=(n_ex,),
        out_shape=[jax.ShapeDtypeStruct((D_MODEL, 2 * MEM_WIDTH), F32), jax.ShapeDtypeStruct((1, D_MODEL), F32)],
        in_specs=[pl.BlockSpec((1, MEM_LEN, D_MODEL), lambda b: (b, 0, 0)),
                  pl.BlockSpec((1, MEM_LEN, 2 * MEM_WIDTH), lambda b: (b, 0, 0)),
                  _full_spec((1, D_MODEL)), _full_spec((D_MODEL, 2 * MEM_WIDTH))],
        out_specs=[_full_spec((D_MODEL, 2 * MEM_WIDTH)), _full_spec((1, D_MODEL))],
    )(mem, dmkv, g_mem, w_mkv)


def _pack_small_grads(dgpre, dgpost, dgmem, dvg, dvb, dws, dbs, dsink, drel, loss_vec, buckets):
    def body(dgpre_ref, dgpost_ref, dgmem_ref, dvg_ref, dvb_ref, dws_ref, dbs_ref, dsink_ref, drel_ref, loss_ref,
             bk_ref, a_ref, b_ref):
        a_ref[...] = jnp.zeros_like(a_ref)
        b_ref[...] = jnp.zeros_like(b_ref)
        a_ref[0:1, :] = dgpre_ref[...]
        a_ref[1:2, :] = dgpost_ref[...]
        a_ref[2:3, :] = dgmem_ref[...]
        a_ref[3:4, :] = jnp.concatenate([dvg_ref[...], dvb_ref[...]], axis=-1)
        a_ref[ROW_LOSS:ROW_LOSS + 1, 0:128] = loss_ref[...]
        row = lax.broadcasted_iota(jnp.int32, (CHUNK, CHUNK), 0)
        col = lax.broadcasted_iota(jnp.int32, (CHUNK, CHUNK), 1)
        for g in range(A_GROUPS):
            b_ref[ROW_WS + g * CHUNK:ROW_WS + (g + 1) * CHUNK, :] = jnp.where(row >= col, dws_ref[g], 0.0)
            by_token = jnp.transpose(dbs_ref[:, g * 128:(g + 1) * 128])
            b_ref[ROW_BS + g:ROW_BS + g + 1, :] = jnp.sum(by_token, axis=0, keepdims=True)
        b_ref[ROW_SINK:ROW_SINK + 1, :] = dsink_ref[...]
        bk = bk_ref[...]
        rel_row = lax.broadcasted_iota(jnp.int32, (8, 128), 0)
        rel_col = lax.broadcasted_iota(jnp.int32, (8, 128), 1)
        rel = jnp.zeros((8, 128), F32)
        for h in range(4):
            acc = drel_ref[h * CHUNK:(h + 1) * CHUNK, :]
            for b in range(N_BUCKETS):
                rel = jnp.where((rel_row == h) & (rel_col == b), jnp.sum(jnp.where(bk == b, acc, 0.0)), rel)
        b_ref[ROW_REL:ROW_REL + 8, :] = rel

    return pl.pallas_call(
        body, name="pack_small_grads",
        out_shape=[jax.ShapeDtypeStruct((SMALL_A_ROWS, D_MODEL), F32), jax.ShapeDtypeStruct((SMALL_B_ROWS, 128), F32)],
        in_specs=[VMEM_SPEC] * 11, out_specs=[VMEM_SPEC] * 2,
    )(dgpre, dgpost, dgmem, dvg, dvb, dws, dbs, dsink, drel, loss_vec, buckets)


def _adamw(w, g, m, v):
    m2 = ADAM_B1 * m + (1.0 - ADAM_B1) * g
    v2 = ADAM_B2 * v + (1.0 - ADAM_B2) * (g * g)
    m_hat = m2 / (1.0 - ADAM_B1 ** ADAM_STEP)
    v_hat = v2 / (1.0 - ADAM_B2 ** ADAM_STEP)
    delta = -ADAM_LR * (m_hat / (jnp.sqrt(v_hat) + ADAM_EPS) + ADAM_WD * w)
    return delta, m2, v2


ADAM_MAX_ROWS = 176


def _adamw_whole(g, w, m, v, name):
    rows, cols = w.shape
    steps = -(-rows // ADAM_MAX_ROWS)
    block_rows = rows // steps
    assert block_rows * steps == rows and block_rows % 8 == 0

    def body(g_ref, w_ref, m_ref, v_ref, d_out, m_out, v_out):
        delta, m2, v2 = _adamw(w_ref[...], g_ref[...], m_ref[...], v_ref[...])
        d_out[...] = delta
        m_out[...] = m2
        v_out[...] = v2

    block = pl.BlockSpec((block_rows, cols), lambda k: (k, 0))
    out = pl.pallas_call(
        body, name=name, grid=(steps,), out_shape=[jax.ShapeDtypeStruct((rows, cols), F32)] * 3,
        in_specs=[block] * 4, out_specs=[block] * 3,
    )(g, w, m, v)
    return [g] + list(out)


def _adamw_small(ra, rb, weights, moments_m, moments_v):
    n = len(weights)

    def body(*refs):
        ra_ref, rb_ref = refs[0], refs[1]
        w_refs, m_refs, v_refs = refs[2:2 + n], refs[2 + n:2 + 2 * n], refs[2 + 2 * n:2 + 3 * n]
        outs = refs[2 + 3 * n:]
        g_outs, d_outs, m_outs, v_outs = outs[:n], outs[n:2 * n], outs[2 * n:3 * n], outs[3 * n:4 * n]
        ga, gb = ra_ref[0], rb_ref[0]
        for chip in range(1, N_CHIPS):
            ga = ga + ra_ref[chip]
            gb = gb + rb_ref[chip]
        outs[4 * n][...] = ga[ROW_LOSS:ROW_LOSS + 1, 0:128]
        grads = [ga[0:1, :], ga[1:2, :], ga[2:3, :], ga[3:4, :A_WIDTH], ga[3:4, A_WIDTH:],
                 gb[ROW_WS:ROW_WS + A_GROUPS * CHUNK, :].reshape(A_GROUPS, CHUNK, CHUNK),
                 gb[ROW_BS:ROW_BS + A_GROUPS, :], gb[ROW_SINK:ROW_SINK + 1, 0:4],
                 gb[ROW_REL:ROW_REL + 4, 0:N_BUCKETS]]
        for k in range(n):
            delta, m2, v2 = _adamw(w_refs[k][...], grads[k], m_refs[k][...], v_refs[k][...])
            g_outs[k][...] = grads[k]
            d_outs[k][...] = delta
            m_outs[k][...] = m2
            v_outs[k][...] = v2

    out_shape = [jax.ShapeDtypeStruct(w.shape, F32) for w in weights] * 4 + [jax.ShapeDtypeStruct((1, 128), F32)]
    return pl.pallas_call(
        body, name="adamw_small", out_shape=out_shape,
        in_specs=[VMEM_SPEC] * (2 + 3 * n), out_specs=[VMEM_SPEC] * (4 * n + 1),
    )(ra, rb, *weights, *moments_m, *moments_v)


def kernel(x, mem, pre_norm_g, post_norm_g, mem_norm_g, w_in, w_mem_kv, v_norm_g, v_norm_b, w_spatial, b_spatial, attn_sinks, rel_bias, w_out, loss_target, m_pre_norm_g, m_post_norm_g, m_mem_norm_g, m_w_in, m_w_mem_kv, m_v_norm_g, m_v_norm_b, m_w_spatial, m_b_spatial, m_attn_sinks, m_rel_bias, m_w_out, v_pre_norm_g, v_post_norm_g, v_mem_norm_g, v_w_in, v_w_mem_kv, v_v_norm_g, v_v_norm_b, v_w_spatial, v_b_spatial, v_attn_sinks, v_rel_bias, v_w_out):
    n_ex, seq, _ = x.shape
    n_tok = n_ex * seq
    x2 = x.reshape(n_tok, D_MODEL)
    tgt2 = loss_target.reshape(n_tok, D_MODEL)
    buckets = jnp.asarray(_bucket_map())
    shard_arr = (2 * lax.axis_index("x") + lax.axis_index("y")).astype(jnp.int32).reshape(1)
    w_sp = w_spatial[0]
    b_sp = jnp.broadcast_to(b_spatial[0][:, :, None], (A_GROUPS, CHUNK, CHUNK))
    w_in_t, m_w_in_t, v_w_in_t = (jnp.transpose(a[0]) for a in (w_in, m_w_in, v_w_in))
    rel_t, m_rel_t, v_rel_t = (jnp.transpose(a) for a in (rel_bias, m_rel_bias, v_rel_bias))

    x_arr = lax.axis_index("x").astype(jnp.int32).reshape(1)
    h_b, parts, (w_in_b, g_mkv, g_out) = _gather_and_project(x2, pre_norm_g, w_in_t, w_mem_kv[0], w_out[0], x_arr)
    w_mkv_b = g_mkv.reshape(D_MODEL, 2 * MEM_WIDTH)
    w_out_b = g_out.reshape(MIX_WIDTH, D_MODEL)

    bias = _make_bias(rel_t, buckets)
    mkv = _memkv_forward(mem, mem_norm_g, w_mkv_b)
    dout, dproj, dmkv, dwout, dvg, dvb, dws, dbs, dsink, drel, loss_vec, dgpost = _mix(
        parts, mkv, x2, tgt2, v_norm_g, v_norm_b, w_sp, b_sp, attn_sinks, bias, w_out_b, post_norm_g, n_ex, seq)

    dx, dgpre = _backward_projection(x2, dout, dproj, pre_norm_g, w_in_b)
    dwmkv, dgmem = _memkv_backward(mem, dmkv, mem_norm_g, w_mkv_b)
    small_a, small_b = _pack_small_grads(dgpre, dgpost, dgmem, dvg, dvb, dws, dbs, dsink, drel, loss_vec, buckets)

    shard_shapes = [w_mem_kv.shape[1:], w_out.shape[1:]]
    big = [g.reshape(N_CHIPS, 2, s[0] // 2, s[1]) for g, s in zip((dwmkv, dwout), shard_shapes)]
    (g_win, g_wmkv, g_wout), (ga, gb) = _reduce_gradients(dproj, h_b, big, [small_a, small_b], shard_arr)

    big_out = [_adamw_whole(g_win, w_in_t, m_w_in_t, v_w_in_t, "adamw_w_in"),
               _adamw_whole(g_wmkv, w_mem_kv[0], m_w_mem_kv[0], v_w_mem_kv[0], "adamw_w_mem_kv"),
               _adamw_whole(g_wout, w_out[0], m_w_out[0], v_w_out[0], "adamw_w_out")]
    small_w = [pre_norm_g, post_norm_g, mem_norm_g, v_norm_g, v_norm_b, w_sp, b_spatial[0], attn_sinks, rel_t]
    small_m = [m_pre_norm_g, m_post_norm_g, m_mem_norm_g, m_v_norm_g, m_v_norm_b, m_w_spatial[0], m_b_spatial[0],
               m_attn_sinks, m_rel_t]
    small_v = [v_pre_norm_g, v_post_norm_g, v_mem_norm_g, v_v_norm_g, v_v_norm_b, v_w_spatial[0], v_b_spatial[0],
               v_attn_sinks, v_rel_t]
    small_out = _adamw_small(ga, gb, small_w, small_m, small_v)
    n_small = len(small_w)

    outputs = [small_out[4 * n_small][0, 0], dx.reshape(x.shape)]
    for kind in range(4):
        s = small_out[kind * n_small:(kind + 1) * n_small]
        outputs += [s[0], s[1], s[2], jnp.transpose(big_out[0][kind])[None], big_out[1][kind][None], s[3], s[4],
                    s[5][None], s[6][None], s[7], jnp.transpose(s[8]), big_out[2][kind][None]]
    return tuple(outputs)
```

```python
import functools

import numpy as np
import jax
import jax.numpy as jnp
from jax import lax
from jax.experimental import pallas as pl
from jax.experimental.pallas import tpu as pltpu

F32 = jnp.float32
BF16 = jnp.bfloat16
MESH = pl.DeviceIdType.MESH

D_MODEL = 1024
CHUNK = 128
A_WIDTH = 512
A_GROUPS = 4
SWA_WIDTH = 256
KV_WIDTH = 128
MEM_WIDTH = 256
MEM_LEN = 256
MIX_WIDTH = 1024
IN_WIDTH = 2816
N_BUCKETS = 32
MAX_DISTANCE = 128
EPS = 1e-6
NEG = -1e30
QK_SCALE = 0.125
HALF_HEAD_PAIR = 64

ADAM_LR = 0.001
ADAM_B1 = 0.9
ADAM_B2 = 0.999
ADAM_EPS = 1e-08
ADAM_WD = 0.01
ADAM_STEP = 10

N_CHIPS = 4
TILE_CHUNKS = 2
TILE = TILE_CHUNKS * CHUNK
PROJ_TILE = 512
VMEM_LIMIT = 56 * 1024 * 1024

SMALL_A_ROWS = 8
ROW_LOSS = 4
ROW_WS = 0
ROW_BS = 512
ROW_SINK = 520
ROW_REL = 528
SMALL_B_ROWS = 536


def _mm(a, b):
    return lax.dot_general(a, b, (((1,), (0,)), ((), ())), preferred_element_type=F32)


def _mm_nt(a, b):
    return lax.dot_general(a, b, (((1,), (1,)), ((), ())), preferred_element_type=F32)


def _mm_tn(a, b):
    return lax.dot_general(a, b, (((0,), (0,)), ((), ())), preferred_element_type=F32)


def _bucket_map():
    qi = np.arange(CHUNK)[:, None]
    kj = np.arange(2 * CHUNK)[None, :]
    n = np.maximum(qi + CHUNK - kj, 0)
    max_exact = N_BUCKETS // 2
    large = max_exact + (np.log(np.maximum(n, 1) / max_exact) / np.log(MAX_DISTANCE / max_exact)
                         * (N_BUCKETS - max_exact)).astype(np.int32)
    large = np.minimum(large, N_BUCKETS - 1)
    return np.where(n < max_exact, n, large).astype(np.int32)


_GELU_C = 0.7978845608028654
_GELU_A = 0.044715
_GELU_K1 = 2.0 * _GELU_C
_GELU_K2 = 2.0 * _GELU_C * _GELU_A


def _gelu(x):
    x2 = x * x
    s = 1.0 / (1.0 + jnp.exp(x * (-_GELU_K1 - _GELU_K2 * x2)))
    return x * s, (s, x2)


def _gelu_grad(x, saved):
    s, x2 = saved
    return s + x * (s * (1.0 - s)) * (_GELU_K1 + 3.0 * _GELU_K2 * x2)


def _sigmoid(x):
    return 1.0 / (1.0 + jnp.exp(-x))


def _lane_lo(shape):
    return lax.broadcasted_iota(jnp.int32, shape, 1) < HALF_HEAD_PAIR


def _swa_variants(t):
    lo = _lane_lo(t.shape)
    tr = pltpu.roll(t, HALF_HEAD_PAIR, 1)
    zero = jnp.zeros_like(t)
    return (jnp.where(lo, t, zero).astype(BF16), jnp.where(lo, zero, tr).astype(BF16),
            jnp.where(lo, tr, zero).astype(BF16), jnp.where(lo, zero, t).astype(BF16))


def _swa_unvariants(d0, d1, d2, d3):
    lo = _lane_lo(d0.shape)
    zero = jnp.zeros_like(d0)
    rolled = jnp.where(lo, zero, d1) + jnp.where(lo, d2, zero)
    return jnp.where(lo, d0, zero) + jnp.where(lo, zero, d3) + pltpu.roll(rolled, HALF_HEAD_PAIR, 1)


def _mem_variants(t):
    out = []
    for pair in range(2):
        tp = t[:, pair * 128:(pair + 1) * 128]
        lo = _lane_lo(tp.shape)
        zero = jnp.zeros_like(tp)
        out.append(jnp.where(lo, tp, zero).astype(BF16))
        out.append(jnp.where(lo, zero, tp).astype(BF16))
    return out


def _mem_unvariants(d0, d1, d2, d3):
    lo = _lane_lo(d0.shape)
    return jnp.concatenate([jnp.where(lo, d0, d1), jnp.where(lo, d2, d3)], axis=-1)


def _softmax(logits, sinks):
    m = jnp.max(logits, axis=-1, keepdims=True)
    if sinks is not None:
        m = jnp.maximum(m, sinks)
    p = jnp.exp(logits - m)
    den = jnp.sum(p, axis=-1, keepdims=True)
    if sinks is None:
        return p * (1.0 / den), None
    es = jnp.exp(sinks - m)
    inv = 1.0 / (den + es)
    return p * inv, es * inv


def _band_valid(with_prev):
    qi = lax.broadcasted_iota(jnp.int32, (CHUNK, 2 * CHUNK), 0)
    kj = lax.broadcasted_iota(jnp.int32, (CHUNK, 2 * CHUNK), 1)
    in_cur = (kj >= CHUNK) & (kj - CHUNK <= qi)
    if not with_prev:
        return in_cur
    return in_cur | ((kj < CHUNK) & (kj > qi))


def _causal_weights(ws_ref):
    row = lax.broadcasted_iota(jnp.int32, (CHUNK, CHUNK), 0)
    col = lax.broadcasted_iota(jnp.int32, (CHUNK, CHUNK), 1)
    return [jnp.where(row >= col, ws_ref[g], 0.0).astype(BF16) for g in range(A_GROUPS)]


def _rows_to_lanes(a, n):
    return jnp.concatenate([a[c * CHUNK:(c + 1) * CHUNK] for c in range(n)], axis=1)


def _lanes_to_rows(a, n):
    w = a.shape[1] // n
    return jnp.concatenate([a[:, c * w:(c + 1) * w] for c in range(n)], axis=0)


def _stack_heads(pair01, pair23):
    return jnp.concatenate([pair01[:, :256], pair01[:, 256:], pair23[:, :256], pair23[:, 256:]], axis=0)


def _pair_heads(s, r):
    return (jnp.concatenate([s[0:r], s[r:2 * r]], axis=1), jnp.concatenate([s[2 * r:3 * r], s[3 * r:4 * r]], axis=1))


def _pair_operands(variants):
    return (jnp.concatenate(variants[0:2], axis=0), jnp.concatenate(variants[2:4], axis=0))


def _split_pair_grads(d_pairs):
    return d_pairs[0][:256], d_pairs[0][256:], d_pairs[1][:256], d_pairs[1][256:]


def _halves_bf16(a):
    return (a[:, :128].astype(BF16), a[:, 128:].astype(BF16))


def _group_a_forward(au, av, vg, vb, wm, bs_rows):
    gu, tu = _gelu(au)
    gv, tv = _gelu(av)
    ya, res = [], []
    for g in range(A_GROUPS):
        sl = slice(g * 128, (g + 1) * 128)
        xg = gv[:, sl]
        xc = xg - jnp.mean(xg, axis=-1, keepdims=True)
        rstd = lax.rsqrt(jnp.mean(xc * xc, axis=-1, keepdims=True) + EPS)
        xhat = xc * rstd
        vn = _rows_to_lanes((xhat * vg[:, sl] + vb[:, sl]).astype(BF16), TILE_CHUNKS)
        s = _lanes_to_rows(_mm(wm[g], vn), TILE_CHUNKS) + bs_rows[g]
        ya.append(gu[:, sl] * s)
        res.append((xhat, rstd, vn, s))
    return ya, dict(gu=gu, tu=tu, tv=tv, groups=res)


def _attention_probs(qp, k_pairs, bias, sink_col):
    logits = _stack_heads(_mm_nt(qp[0], k_pairs[0]), _mm_nt(qp[1], k_pairs[1]))
    if bias is not None:
        logits = logits + bias
    return _softmax(logits, sink_col)


def _attention_out(p, v_pairs, r):
    pp = _pair_heads(p.astype(BF16), r)
    return jnp.concatenate([_mm(pp[0], v_pairs[0]), _mm(pp[1], v_pairs[1])], axis=-1), pp


def _attention_backward(p, pp, do_pairs, qp, k_pairs, v_pairs, r):
    dp = _stack_heads(_mm_nt(do_pairs[0], v_pairs[0]), _mm_nt(do_pairs[1], v_pairs[1]))
    delta = jnp.sum(p * dp, axis=-1, keepdims=True)
    dl = p * (dp - delta)
    dlp = _pair_heads(dl.astype(BF16), r)
    dq = jnp.concatenate([_mm(dlp[0], k_pairs[0]), _mm(dlp[1], k_pairs[1])], axis=-1)
    dk = (_mm_tn(dlp[0], qp[0]), _mm_tn(dlp[1], qp[1]))
    dv = (_mm_tn(pp[0], do_pairs[0]), _mm_tn(pp[1], do_pairs[1]))
    return dl, delta, dq, dk, dv


def _tile_specs(n_tiles_ex, width):
    return pl.BlockSpec((TILE, width), lambda b, i: (b * n_tiles_ex + jnp.minimum(i, n_tiles_ex - 1), 0))


def _prev_chunk_spec(n_tiles_ex, width):
    def index(b, i):
        chunk = TILE_CHUNKS * jnp.minimum(i, n_tiles_ex - 1)
        return (b * n_tiles_ex * TILE_CHUNKS + jnp.maximum(chunk - 1, 0), 0)
    return pl.BlockSpec((CHUNK, width), index)


def _full_spec(shape):
    zeros = (0,) * len(shape)
    return pl.BlockSpec(shape, lambda *_: zeros)


SMEM_SPEC = pl.BlockSpec(memory_space=pltpu.SMEM)
ANY_SPEC = pl.BlockSpec(memory_space=pl.ANY)
VMEM_SPEC = pl.BlockSpec(memory_space=pltpu.VMEM)


def _make_bias(rel_bias_t, buckets):
    def body(rel_ref, bk_ref, out_ref):
        bk = bk_ref[...]
        for h in range(4):
            acc = jnp.zeros((CHUNK, 2 * CHUNK), F32)
            for b in range(N_BUCKETS):
                acc = jnp.where(bk == b, rel_ref[h, b], acc)
            for t, with_prev in enumerate((True, False)):
                out_ref[t, h * CHUNK:(h + 1) * CHUNK, :] = jnp.where(_band_valid(with_prev), acc, NEG)

    return pl.pallas_call(
        body, name="make_bias", out_shape=jax.ShapeDtypeStruct((2, 4 * CHUNK, 2 * CHUNK), F32),
        in_specs=[SMEM_SPEC, VMEM_SPEC], out_specs=VMEM_SPEC,
    )(rel_bias_t, buckets)


def _memkv_forward(mem, g_mem, w_mkv):
    n_ex = mem.shape[0]

    def body(mem_ref, g_ref, w_ref, out_ref):
        m = mem_ref[0]
        r = lax.rsqrt(jnp.mean(m * m, axis=-1, keepdims=True) + EPS)
        out_ref[0] = _mm((m * r * g_ref[...]).astype(BF16), w_ref[...])

    return pl.pallas_call(
        body, name="memkv_forward", grid=(n_ex,),
        out_shape=jax.ShapeDtypeStruct((n_ex, MEM_LEN, 2 * MEM_WIDTH), F32),
        in_specs=[pl.BlockSpec((1, MEM_LEN, D_MODEL), lambda b: (b, 0, 0)), _full_spec((1, D_MODEL)),
                  _full_spec((D_MODEL, 2 * MEM_WIDTH))],
        out_specs=pl.BlockSpec((1, MEM_LEN, 2 * MEM_WIDTH), lambda b: (b, 0, 0)),
    )(mem, g_mem, w_mkv)


PROJ_WIDTHS = (A_WIDTH, A_WIDTH, SWA_WIDTH, KV_WIDTH, KV_WIDTH, MEM_WIDTH, MIX_WIDTH)
PROJ_OFFSETS = tuple(int(v) for v in np.cumsum((0,) + PROJ_WIDTHS))


HALF_WIDTH = IN_WIDTH // 2
HALF_PARTS = ((0, 1, 2, 3), (4, 5, 6))


def _gather_and_project(x2, g_pre, w_in_s, w_mkv_s, w_out_s, x_arr):
    n_tok = x2.shape[0]
    n_tiles = n_tok // PROJ_TILE
    last = n_tiles - 1
    shapes = [w_in_s.shape, w_mkv_s.shape, w_out_s.shape]
    n_w = len(shapes)

    def body(x_sref, x_ref, g_ref, win_hbm, wmkv_hbm, wout_hbm, h_ref, *refs):
        part_refs, refs = refs[:len(PROJ_WIDTHS)], refs[len(PROJ_WIDTHS):]
        gin_hbm, gmkv_hbm, gout_hbm, wg, stage_in, stage_mkv, stage_out, own_mkv, own_out, h_all = refs[:10]
        send_sems, recv_sems, local_sems = refs[10:]
        p, t = pl.program_id(0), pl.program_id(1)
        x, y, c = lax.axis_index("x"), lax.axis_index("y"), lax.axis_index("c")
        me, sibling = (x, y, c), (x, y, 1 - c)
        my_shard = 2 * x + y
        gathered = [wg, gmkv_hbm, gout_hbm]

        def half_rows(w, shard, half):
            rows = shapes[w][0] // 2
            if w == 0:
                return wg.at[pl.ds(pl.multiple_of(shard * shapes[0][0] + half * rows, 16), rows), :]
            return gathered[w].at[shard, pl.ds(half * rows, rows), :]

        def first(w, rel):
            src = half_rows(w, my_shard, c) if w == 0 else (own_mkv, own_out)[w - 1].at[
                pl.ds(c * (shapes[w][0] // 2), shapes[w][0] // 2), :]
            k = 3 * w + rel - 1
            return pltpu.make_async_remote_copy(
                src_ref=src, dst_ref=half_rows(w, my_shard, c), send_sem=send_sems.at[k], recv_sem=recv_sems.at[k],
                device_id=(x ^ (rel >> 1), y ^ (rel & 1), c), device_id_type=MESH)

        def landed(w, rel):
            k = 3 * w + rel - 1
            ref = half_rows(w, my_shard ^ rel, c)
            return pltpu.make_async_remote_copy(src_ref=ref, dst_ref=ref, send_sem=send_sems.at[k],
                                                recv_sem=recv_sems.at[k], device_id=me, device_id_type=MESH)

        def passed(w, rel, half, to):
            k = 9 + 3 * w + rel - 1
            ref = half_rows(w, my_shard ^ rel, half)
            return pltpu.make_async_remote_copy(src_ref=ref, dst_ref=ref, send_sem=send_sems.at[k],
                                                recv_sem=recv_sems.at[k], device_id=to, device_id_type=MESH)

        def pass_on(w, rels):
            for rel in rels:
                landed(w, rel).wait_recv()
                passed(w, rel, c, sibling).start()
            for rel in rels:
                passed(w, rel, 1 - c, me).wait_recv()

        own_stores = [pltpu.make_async_copy(own_mkv, gmkv_hbm.at[my_shard], local_sems.at[3]),
                      pltpu.make_async_copy(own_out, gout_hbm.at[my_shard], local_sems.at[4])]

        @pl.when((p == 0) & (t == 0))
        def _():
            loads = [pltpu.make_async_copy(src, dst, local_sems.at[k]) for k, (src, dst) in enumerate(
                ((win_hbm, stage_in), (wmkv_hbm, stage_mkv), (wout_hbm, stage_out)))]
            for cp in loads:
                cp.start()
            loads[0].wait()
            wg[pl.ds(pl.multiple_of(my_shard * shapes[0][0], 16), shapes[0][0]), :] = stage_in[...].astype(BF16)
            for rel in (1, 2):
                first(0, rel).start()
            loads[1].wait()
            loads[2].wait()
            own_mkv[...] = stage_mkv[...].astype(BF16)
            own_out[...] = stage_out[...].astype(BF16)
            for cp in own_stores:
                cp.start()
            pass_on(0, (1,))
            first(0, 3).start()

        @pl.when((p == 0) & (t == n_tiles // 2))
        def _():
            for w in (1, 2):
                for rel in (1, 2, 3):
                    first(w, rel).start()

        @pl.when((p == 1) & (t == 0))
        def _():
            pass_on(0, (2, 3))

        which_half = p ^ x_sref[0]
        tile_rows = pl.ds(pl.multiple_of(t * PROJ_TILE, PROJ_TILE), PROJ_TILE)

        def project(h):
            proj = _mm_nt(h, wg[pl.ds(pl.multiple_of(which_half * HALF_WIDTH, 16), HALF_WIDTH), :])
            for hh in range(2):
                @pl.when(which_half == hh)
                def _():
                    for k in HALF_PARTS[hh]:
                        lo = PROJ_OFFSETS[k] - hh * HALF_WIDTH
                        part_refs[k][...] = proj[:, lo:lo + PROJ_WIDTHS[k]]

        @pl.when(p == 0)
        def _():
            xv = x_ref[...]
            r = lax.rsqrt(jnp.mean(xv * xv, axis=-1, keepdims=True) + EPS)
            h = (xv * r * g_ref[...]).astype(BF16)
            h_ref[...] = h
            h_all[tile_rows, :] = h
            project(h)

        @pl.when(p == 1)
        def _():
            project(h_all[tile_rows, :])

        @pl.when((p == 1) & (t == last))
        def _():
            store = pltpu.make_async_copy(wg, gin_hbm, local_sems.at[5])
            store.start()
            for w in (1, 2):
                pass_on(w, (1, 2, 3))
            for w in range(n_w):
                for rel in (1, 2, 3):
                    first(w, rel).wait_send()
                    passed(w, rel, c, sibling).wait_send()
            for cp in own_stores:
                cp.wait()
            store.wait()

    def active_in(hh):
        def index(p, t, xs):
            return (jnp.where((p ^ xs[0]) == hh, t, jnp.where(p == 0, 0, last)), 0)
        return index

    part_specs = [pl.BlockSpec((PROJ_TILE, PROJ_WIDTHS[k]), active_in(hh)) for hh in range(2) for k in HALF_PARTS[hh]]
    vmem = pltpu.VMEM
    out = pl.pallas_call(
        body, name="gather_and_project",
        out_shape=[jax.ShapeDtypeStruct((n_tok, D_MODEL), BF16)]
        + [jax.ShapeDtypeStruct((n_tok, w), F32) for w in PROJ_WIDTHS]
        + [jax.ShapeDtypeStruct((N_CHIPS * shapes[0][0], shapes[0][1]), BF16)]
        + [jax.ShapeDtypeStruct((N_CHIPS,) + s, BF16) for s in shapes[1:]],
        grid_spec=pltpu.PrefetchScalarGridSpec(
            num_scalar_prefetch=1, grid=(2, n_tiles),
            in_specs=[pl.BlockSpec((PROJ_TILE, D_MODEL), lambda p, t, xs: (jnp.where(p == 0, t, last), 0)),
                      pl.BlockSpec((1, D_MODEL), lambda p, t, xs: (0, 0)), ANY_SPEC, ANY_SPEC, ANY_SPEC],
            out_specs=[pl.BlockSpec((PROJ_TILE, D_MODEL), lambda p, t, xs: (jnp.where(p == 0, t, last), 0))]
            + part_specs + [ANY_SPEC] * 3,
            scratch_shapes=[vmem((N_CHIPS * shapes[0][0], shapes[0][1]), BF16), vmem(shapes[0], F32),
                            vmem(shapes[1], F32), vmem(shapes[2], F32), vmem(shapes[1], BF16), vmem(shapes[2], BF16),
                            vmem((n_tok, D_MODEL), BF16),
                            pltpu.SemaphoreType.DMA((18,)), pltpu.SemaphoreType.DMA((18,)),
                            pltpu.SemaphoreType.DMA((6,))]),
        compiler_params=pltpu.CompilerParams(vmem_limit_bytes=VMEM_LIMIT),
    )(x_arr, x2, g_pre, w_in_s, w_mkv_s, w_out_s)
    h, parts, weights = out[0], out[1:1 + len(PROJ_WIDTHS)], out[1 + len(PROJ_WIDTHS):]
    return h, list(parts), weights


def _load_chunk(j, i, sk_ref, sv_ref, skp_ref, svp_ref):
    rows = slice(j * CHUNK, (j + 1) * CHUNK)
    if j == 0:
        k_prev, v_prev, table = skp_ref[...], svp_ref[...], jnp.where(i > 0, 0, 1)
    else:
        prev = slice((j - 1) * CHUNK, j * CHUNK)
        k_prev, v_prev, table = sk_ref[prev, :], sv_ref[prev, :], 0
    k_pairs = _pair_operands(_swa_variants(jnp.concatenate([k_prev, sk_ref[rows, :]], axis=0)))
    v_pairs = _pair_operands(_swa_variants(jnp.concatenate([v_prev, sv_ref[rows, :]], axis=0)))
    return rows, k_pairs, v_pairs, table


def _tile_constants(ws_ref, bs_ref, sink_ref, mkv_ref):
    wm = _causal_weights(ws_ref)
    bs_rows = [jnp.concatenate([bs_ref[g]] * TILE_CHUNKS, axis=0) for g in range(A_GROUPS)]
    sink_col = jnp.max(jnp.concatenate([jnp.full((CHUNK, 128), sink_ref[0, h], F32) for h in range(4)], axis=0),
                       axis=-1, keepdims=True)
    mkv_v = mkv_ref[0]
    mk_pairs = _pair_operands(_mem_variants(mkv_v[:, :MEM_WIDTH]))
    mv_pairs = _pair_operands(_mem_variants(mkv_v[:, MEM_WIDTH:]))
    return wm, bs_rows, sink_col, mk_pairs, mv_pairs


def _mix(parts, mkv, x2, tgt2, v_g, v_b, w_sp, b_sp, sinks, bias, w_out, g_post, n_ex, seq):
    n_tiles_ex = seq // TILE
    n_tok = n_ex * seq
    au, av, sq, sk, sv, mq, z = parts
    col = dict(zip(("au", "av", "sq", "sk", "sv", "mq", "z"),
                   (slice(PROJ_OFFSETS[k], PROJ_OFFSETS[k + 1]) for k in range(len(PROJ_WIDTHS)))))
    before_kv, after_kv = slice(0, col["sk"].start), slice(col["sv"].stop, IN_WIDTH)

    def body(au_ref, av_ref, sq_ref, sk_ref, sv_ref, skp_ref, svp_ref, mq_ref, z_ref, mkv_ref, x_ref, tgt_ref,
             vg_ref, vb_ref, ws_ref, bs_ref, sink_ref, bias_ref, wout_ref, gpost_ref,
             dout_ref, dproj_ref, dmkv_ref, dwout_ref, dvg_ref, dvb_ref, dws_ref, dbs_ref, dsink_ref, drel_ref,
             loss_ref, dgpost_ref, carry_dp, carry_k, carry_v):
        b, i = pl.program_id(0), pl.program_id(1)

        @pl.when((b == 0) & (i == 0))
        def _():
            for ref in (dwout_ref, dvg_ref, dvb_ref, dws_ref, dbs_ref, dsink_ref, drel_ref, loss_ref, dgpost_ref):
                ref[...] = jnp.zeros_like(ref)

        @pl.when(i == 0)
        def _():
            dmkv_ref[...] = jnp.zeros_like(dmkv_ref)
            carry_k[...] = jnp.zeros_like(carry_k)
            carry_v[...] = jnp.zeros_like(carry_v)

        @pl.when(i > 0)
        def _():
            dproj_ref[:, before_kv] = carry_dp[:, before_kv]
            dproj_ref[:, after_kv] = carry_dp[:, after_kv]

        @pl.when(i < n_tiles_ex)
        def _():
            wm, bs_rows, sink_col, mk_pairs, mv_pairs = _tile_constants(ws_ref, bs_ref, sink_ref, mkv_ref)
            vg = vg_ref[...]

            au_v, av_v = au_ref[...], av_ref[...]
            ya, res = _group_a_forward(au_v, av_v, vg, vb_ref[...], wm, bs_rows)
            swa, yb = [], []
            for j in range(TILE_CHUNKS):
                rows, k_pairs, v_pairs, table = _load_chunk(j, i, sk_ref, sv_ref, skp_ref, svp_ref)
                qp = _halves_bf16(sq_ref[rows, :] * QK_SCALE)
                p, ps = _attention_probs(qp, k_pairs, bias_ref[table], sink_col)
                out, pp = _attention_out(p, v_pairs, CHUNK)
                yb.append(out)
                swa.append((rows, k_pairs, v_pairs, qp, p, ps, pp))
            mqp = _halves_bf16(mq_ref[...] * QK_SCALE)
            pm, _ = _attention_probs(mqp, mk_pairs, None, None)
            yc, ppm = _attention_out(pm, mv_pairs, TILE)
            ycat = jnp.concatenate(ya + [jnp.concatenate(yb, axis=0), yc], axis=-1)

            zv = z_ref[...]
            sig = _sigmoid(zv)
            sz = zv * sig
            y_b = (ycat * sz).astype(BF16)
            o = _mm(y_b, wout_ref[...])
            r2 = lax.rsqrt(jnp.mean(o * o, axis=-1, keepdims=True) + EPS)
            nrm = o * r2
            gp = gpost_ref[...]
            diff = x_ref[...] + nrm * gp - tgt_ref[...]
            loss_ref[...] += jnp.sum(diff * diff) * (0.5 / D_MODEL)
            dout = diff * (1.0 / D_MODEL)
            dout_ref[...] = dout
            dgpost_ref[...] += jnp.sum(dout * nrm, axis=0, keepdims=True)
            dn = dout * gp
            do_b = (r2 * (dn - nrm * jnp.mean(dn * nrm, axis=-1, keepdims=True))).astype(BF16)
            dwout_ref[...] += _mm_tn(y_b, do_b)
            dy = _mm_nt(do_b, wout_ref[...])
            carry_dp[:, col["z"]] = (dy * ycat * (sig * (1.0 + zv * (1.0 - sig)))).astype(BF16)
            dyc = dy * sz

            dgu, dgv = [], []
            for g in range(A_GROUPS):
                sl = slice(g * 128, (g + 1) * 128)
                xhat, rstd, vn, s = res["groups"][g]
                dya = dyc[:, sl]
                dgu.append(dya * s)
                ds = dya * res["gu"][:, sl]
                dbs_ref[:, sl] += sum(ds[c * CHUNK:(c + 1) * CHUNK] for c in range(TILE_CHUNKS))
                ds_b = _rows_to_lanes(ds.astype(BF16), TILE_CHUNKS)
                dws_ref[g] += _mm_nt(ds_b, vn)
                dvn = _lanes_to_rows(_mm_tn(wm[g], ds_b), TILE_CHUNKS)
                dvg_ref[:, sl] += jnp.sum(dvn * xhat, axis=0, keepdims=True)
                dvb_ref[:, sl] += jnp.sum(dvn, axis=0, keepdims=True)
                dxh = dvn * vg[:, sl]
                dgv.append(rstd * (dxh - jnp.mean(dxh, axis=-1, keepdims=True)
                                   - xhat * jnp.mean(dxh * xhat, axis=-1, keepdims=True)))
            carry_dp[:, col["au"]] = (jnp.concatenate(dgu, axis=-1) * _gelu_grad(au_v, res["tu"])).astype(BF16)
            carry_dp[:, col["av"]] = (jnp.concatenate(dgv, axis=-1) * _gelu_grad(av_v, res["tv"])).astype(BF16)

            lane4 = lax.broadcasted_iota(jnp.int32, (1, 128), 1)
            dsink_vec = jnp.zeros((1, 128), F32)
            dk_parts, dv_parts = [], []
            for rows, k_pairs, v_pairs, qp, p, ps, pp in swa:
                do_pairs = _halves_bf16(dyc[rows, A_WIDTH:A_WIDTH + SWA_WIDTH])
                dl, delta, dq, dk, dv = _attention_backward(p, pp, do_pairs, qp, k_pairs, v_pairs, CHUNK)
                sink_terms = ps * delta
                for h in range(4):
                    dsink_vec = dsink_vec + jnp.where(lane4 == h, -jnp.sum(sink_terms[h * CHUNK:(h + 1) * CHUNK]), 0.0)
                drel_ref[...] += dl
                carry_dp[rows, col["sq"]] = (dq * QK_SCALE).astype(BF16)
                dk_parts.append(_swa_unvariants(*_split_pair_grads(dk)))
                dv_parts.append(_swa_unvariants(*_split_pair_grads(dv)))
            dsink_ref[...] += dsink_vec

            dc_pairs = _halves_bf16(dyc[:, A_WIDTH + SWA_WIDTH:])
            _, _, dmq, dmk, dmv = _attention_backward(pm, ppm, dc_pairs, mqp, mk_pairs, mv_pairs, TILE)
            carry_dp[:, col["mq"]] = (dmq * QK_SCALE).astype(BF16)
            dmkv_ref[0] += jnp.concatenate([_mem_unvariants(*_split_pair_grads(dmk)),
                                            _mem_unvariants(*_split_pair_grads(dmv))], axis=-1)

            for parts_c, carry, cols in ((dk_parts, carry_k, col["sk"]), (dv_parts, carry_v, col["sv"])):
                @pl.when(i > 0)
                def _():
                    dproj_ref[:, cols] = (carry[...] + jnp.concatenate(
                        [jnp.zeros((TILE - CHUNK, KV_WIDTH), F32), parts_c[0][:CHUNK]], axis=0)).astype(BF16)
                new = [parts_c[0][CHUNK:]]
                for j in range(1, TILE_CHUNKS):
                    new[-1] = new[-1] + parts_c[j][:CHUNK]
                    new.append(parts_c[j][CHUNK:])
                carry[...] = jnp.concatenate(new, axis=0)

        @pl.when(i == n_tiles_ex)
        def _():
            dproj_ref[:, col["sk"]] = carry_k[...].astype(BF16)
            dproj_ref[:, col["sv"]] = carry_v[...].astype(BF16)

    tile = functools.partial(_tile_specs, n_tiles_ex)
    prev = functools.partial(_prev_chunk_spec, n_tiles_ex)
    late = pl.BlockSpec((TILE, IN_WIDTH), lambda b, i: (b * n_tiles_ex + jnp.maximum(i - 1, 0), 0))
    return pl.pallas_call(
        body, name="mix", grid=(n_ex, n_tiles_ex + 1),
        out_shape=[jax.ShapeDtypeStruct((n_tok, D_MODEL), F32), jax.ShapeDtypeStruct((n_tok, IN_WIDTH), BF16),
                   jax.ShapeDtypeStruct((n_ex, MEM_LEN, 2 * MEM_WIDTH), F32),
                   jax.ShapeDtypeStruct((MIX_WIDTH, D_MODEL), F32), jax.ShapeDtypeStruct((1, A_WIDTH), F32),
                   jax.ShapeDtypeStruct((1, A_WIDTH), F32), jax.ShapeDtypeStruct((A_GROUPS, CHUNK, CHUNK), F32),
                   jax.ShapeDtypeStruct((CHUNK, A_WIDTH), F32), jax.ShapeDtypeStruct((1, 128), F32),
                   jax.ShapeDtypeStruct((4 * CHUNK, 2 * CHUNK), F32), jax.ShapeDtypeStruct((1, 128), F32),
                   jax.ShapeDtypeStruct((1, D_MODEL), F32)],
        in_specs=[tile(A_WIDTH), tile(A_WIDTH), tile(SWA_WIDTH), tile(KV_WIDTH), tile(KV_WIDTH),
                  prev(KV_WIDTH), prev(KV_WIDTH), tile(MEM_WIDTH), tile(MIX_WIDTH),
                  pl.BlockSpec((1, MEM_LEN, 2 * MEM_WIDTH), lambda b, i: (b, 0, 0)),
                  tile(D_MODEL), tile(D_MODEL),
                  _full_spec((1, A_WIDTH)), _full_spec((1, A_WIDTH)), _full_spec((A_GROUPS, CHUNK, CHUNK)),
                  _full_spec((A_GROUPS, CHUNK, CHUNK)), SMEM_SPEC, _full_spec((2, 4 * CHUNK, 2 * CHUNK)),
                  _full_spec((MIX_WIDTH, D_MODEL)), _full_spec((1, D_MODEL))],
        out_specs=[tile(D_MODEL), late, pl.BlockSpec((1, MEM_LEN, 2 * MEM_WIDTH), lambda b, i: (b, 0, 0)),
                   _full_spec((MIX_WIDTH, D_MODEL)), _full_spec((1, A_WIDTH)), _full_spec((1, A_WIDTH)),
                   _full_spec((A_GROUPS, CHUNK, CHUNK)), _full_spec((CHUNK, A_WIDTH)), _full_spec((1, 128)),
                   _full_spec((4 * CHUNK, 2 * CHUNK)), _full_spec((1, 128)), _full_spec((1, D_MODEL))],
        scratch_shapes=[pltpu.VMEM((TILE, IN_WIDTH), BF16), pltpu.VMEM((TILE, KV_WIDTH), F32),
                        pltpu.VMEM((TILE, KV_WIDTH), F32)],
        compiler_params=pltpu.CompilerParams(vmem_limit_bytes=VMEM_LIMIT),
    )(au, av, sq, sk, sv, sk, sv, mq, z, mkv, x2, tgt2, v_g, v_b, w_sp, b_sp, sinks, bias, w_out, g_post)


BWD_PROJ_TILE = 512


def _backward_projection(x2, dout, dproj, g_pre, w_in_t):
    n_tok = x2.shape[0]
    n_steps = n_tok // BWD_PROJ_TILE

    def body(x_ref, dout_ref, dp_ref, g_ref, w_hbm, dx_ref, dgpre_ref, w_vmem, sem):
        @pl.when(pl.program_id(0) == 0)
        def _():
            load = pltpu.make_async_copy(w_hbm, w_vmem, sem)
            load.start()
            dgpre_ref[...] = jnp.zeros_like(dgpre_ref)
            load.wait()

        xv = x_ref[...]
        r = lax.rsqrt(jnp.mean(xv * xv, axis=-1, keepdims=True) + EPS)
        xn = xv * r
        dh = _mm(dp_ref[...], w_vmem[...])
        dgpre_ref[...] += jnp.sum(dh * xn, axis=0, keepdims=True)
        dhg = dh * g_ref[...]
        dx_ref[...] = r * (dhg - xn * jnp.mean(dhg * xn, axis=-1, keepdims=True)) + dout_ref[...]

    row = lambda w: pl.BlockSpec((BWD_PROJ_TILE, w), lambda i: (i, 0))
    return pl.pallas_call(
        body, name="backward_projection", grid=(n_steps,),
        out_shape=[jax.ShapeDtypeStruct((n_tok, D_MODEL), F32), jax.ShapeDtypeStruct((1, D_MODEL), F32)],
        in_specs=[row(D_MODEL), row(D_MODEL), row(IN_WIDTH), _full_spec((1, D_MODEL)), ANY_SPEC],
        out_specs=[row(D_MODEL), _full_spec((1, D_MODEL))],
        scratch_shapes=[pltpu.VMEM((IN_WIDTH, D_MODEL), BF16), pltpu.SemaphoreType.DMA],
        input_output_aliases={1: 0},
        compiler_params=pltpu.CompilerParams(vmem_limit_bytes=VMEM_LIMIT),
    )(x2, dout, dproj, g_pre, w_in_t)


SHARD_ROWS = IN_WIDTH // N_CHIPS
SHARD_WINDOW = 768
SHARD_HALF = SHARD_ROWS // 2
DWIN_TILE = 2048
N_REL = N_CHIPS - 1


def _shard_window_start(shard):
    return (shard * SHARD_ROWS // 128) * 128


def _reduce_gradients(dproj, h, big, small, shard_arr):
    n_tok = h.shape[0]
    tile = min(DWIN_TILE, n_tok)
    n_sub = n_tok // tile
    last = N_CHIPS - 1
    n_big, n_small = len(big), len(small)
    big_half = [g.shape[2:] for g in big]
    sem_big_d2d = 2 * N_CHIPS
    sem_big_ici = sem_big_d2d + n_big
    sem_big_swap = sem_big_ici + N_REL * n_big
    sem_small_d2d = sem_big_swap + n_big
    sem_small_ici = sem_small_d2d + n_small
    n_sems = sem_small_ici + N_REL * n_small
    loc_small = n_big
    loc_out_win = loc_small + n_small
    loc_out_big = loc_out_win + 2
    loc_out_small = loc_out_big + 2 * n_big
    n_local = loc_out_small + n_small

    def relation_of_slot(s):
        return (s + 2) % N_REL + 1

    def shard_of_slot(s, my_shard):
        return my_shard ^ jnp.where(s == last, 0, relation_of_slot(s))

    def body(shard_ref, dp_ref, h_hbm, *refs):
        h_vmem, h_sem, refs = refs[-2], refs[-1], refs[:-2]
        big_hbm, refs = refs[:n_big], refs[n_big:]
        small_hbm, refs = refs[:n_small], refs[n_small:]
        out_hbm, refs = refs[0], refs[1:]
        big_out, refs = refs[:n_big], refs[n_big:]
        small_out, refs = refs[:n_small], refs[n_small:]
        part, recv_d2d, send_ici, recv_ici, mine_buf, other_buf = refs[:6]
        refs = refs[6:]
        big_own, big_recv, big_send, big_land, big_mine, big_other = (
            refs[k * n_big:(k + 1) * n_big] for k in range(6))
        refs = refs[6 * n_big:]
        small_own, small_recv, small_all = (refs[k * n_small:(k + 1) * n_small] for k in range(3))
        send_sems, recv_sems, local_sems = refs[3 * n_small:]

        s, t = pl.program_id(0), pl.program_id(1)
        x, y, c = lax.axis_index("x"), lax.axis_index("y"), lax.axis_index("c")
        my_chip = 2 * x + y
        sibling = (x, y, 1 - c)
        my_rows = pl.ds(pl.multiple_of(c * SHARD_HALF, 8), SHARD_HALF)
        other_rows = pl.ds(pl.multiple_of((1 - c) * SHARD_HALF, 8), SHARD_HALF)

        def remote(src, dst, k, to):
            return pltpu.make_async_remote_copy(src_ref=src, dst_ref=dst, send_sem=send_sems.at[k],
                                                recv_sem=recv_sems.at[k], device_id=to, device_id_type=MESH)

        def chip_at(rel):
            return (x ^ (rel >> 1), y ^ (rel & 1), c)

        def to_sibling(k):
            return remote(part.at[k % 2, other_rows, :], recv_d2d.at[k], k, sibling)

        def to_chip(k):
            return remote(send_ici.at[k], recv_ici.at[k], N_CHIPS + k, chip_at(relation_of_slot(k)))

        swap = remote(mine_buf, other_buf, 2 * N_CHIPS - 1, sibling)
        big_load = [pltpu.make_async_copy(big_hbm[w].at[:, pl.ds(c, 1)], big_own[w], local_sems.at[w])
                    for w in range(n_big)]
        big_to_sibling = [remote(big_hbm[w].at[:, pl.ds(1 - c, 1)], big_recv[w], sem_big_d2d + w, sibling)
                          for w in range(n_big)]
        big_to_chip = [[remote(big_send[w].at[k], big_land[w].at[k], sem_big_ici + N_REL * w + k, chip_at(k + 1))
                        for k in range(N_REL)] for w in range(n_big)]
        big_swap = [remote(big_mine[w], big_other[w], sem_big_swap + w, sibling) for w in range(n_big)]
        small_load = [pltpu.make_async_copy(small_hbm[i], small_own[i], local_sems.at[loc_small + i])
                      for i in range(n_small)]
        small_to_sibling = [remote(small_hbm[i], small_recv[i], sem_small_d2d + i, sibling) for i in range(n_small)]
        small_to_chip = [[remote(small_all[i].at[my_chip], small_all[i].at[my_chip],
                                 sem_small_ici + N_REL * i + k, chip_at(k + 1))
                          for k in range(N_REL)] for i in range(n_small)]

        @pl.when((s == 0) & (t == 0))
        def _():
            h_load = pltpu.make_async_copy(h_hbm, h_vmem, h_sem)
            h_load.start()
            for cp in big_load + big_to_sibling + small_load + small_to_sibling:
                cp.start()
            h_load.wait()

        @pl.when((s == 0) & (t == n_sub - 1))
        def _():
            for cp in big_load + small_load:
                cp.wait()
            for cp in big_to_sibling + small_to_sibling:
                cp.wait_recv()
                cp.wait_send()
            for w in range(n_big):
                for k in range(N_REL):
                    shard = my_chip ^ (k + 1)
                    big_send[w][k] = (big_own[w][shard, 0] + big_recv[w][shard, 0]).astype(BF16)
                    big_to_chip[w][k].start()
            for i in range(n_small):
                small_all[i][my_chip] = small_own[i][...] + small_recv[i][...]
                for k in range(N_REL):
                    small_to_chip[i][k].start()

        @pl.when((s > 0) & (t == 0))
        def _():
            k = s - 1
            cp = to_sibling(k)
            cp.wait_recv()
            cp.wait_send()
            send_ici[k] = (part[k % 2, my_rows, :] + recv_d2d[k]).astype(BF16)
            to_chip(k).start()

        def big_rows(w, half):
            rows = big_half[w][0]
            return big_out[w].at[pl.ds(pl.multiple_of(half * rows, 8), rows), :]

        big_store_mine = [pltpu.make_async_copy(big_mine[w], big_rows(w, c), local_sems.at[loc_out_big + 2 * w])
                          for w in range(n_big)]
        big_store_other = [pltpu.make_async_copy(big_other[w], big_rows(w, 1 - c),
                                                 local_sems.at[loc_out_big + 2 * w + 1]) for w in range(n_big)]
        small_store = [pltpu.make_async_copy(small_all[i], small_out[i], local_sems.at[loc_out_small + i])
                       for i in range(n_small)]

        @pl.when((s == last) & (t == 0))
        def _():
            for w in range(n_big):
                total = big_own[w][my_chip, 0] + big_recv[w][my_chip, 0]
                for k in range(N_REL):
                    big_to_chip[w][k].wait_recv()
                    total = total + big_land[w][k].astype(F32)
                big_mine[w][...] = total
                big_swap[w].start()
                big_store_mine[w].start()
            for i in range(n_small):
                for k in range(N_REL):
                    small_to_chip[i][k].wait_recv()
                small_store[i].start()

        r = _mm_tn(dp_ref[...], h_vmem[pl.ds(pl.multiple_of(t * tile, tile), tile), :])
        odd = shard_of_slot(s, shard_ref[0]) % 2
        for parity in range(2):
            rows = r[64 * parity:64 * parity + SHARD_ROWS]

            @pl.when((odd == parity) & (t == 0))
            def _():
                part[s % 2] = rows

            @pl.when((odd == parity) & (t > 0))
            def _():
                part[s % 2] += rows

        @pl.when(t == n_sub - 1)
        def _():
            to_sibling(s).start()

        @pl.when((s == last) & (t == n_sub - 1))
        def _():
            cp = to_sibling(last)
            cp.wait_recv()
            cp.wait_send()
            total = part[last % 2, my_rows, :] + recv_d2d[last]
            for k in range(last):
                to_chip(k).wait_recv()
                total = total + recv_ici[k].astype(F32)
            mine_buf[...] = total
            swap.start()
            out_mine = pltpu.make_async_copy(mine_buf, out_hbm.at[my_rows, :], local_sems.at[0])
            out_mine.start()
            swap.wait_recv()
            out_other = pltpu.make_async_copy(other_buf, out_hbm.at[other_rows, :], local_sems.at[1])
            out_other.start()
            for w in range(n_big):
                big_swap[w].wait_recv()
                big_store_other[w].start()
            stores = [out_mine, out_other] + big_store_mine + big_store_other + small_store
            for k in range(last):
                to_chip(k).wait_send()
            swap.wait_send()
            for w in range(n_big):
                for k in range(N_REL):
                    big_to_chip[w][k].wait_send()
                big_swap[w].wait_send()
            for i in range(n_small):
                for k in range(N_REL):
                    small_to_chip[i][k].wait_send()
            for cp in stores:
                cp.wait()

    half = (SHARD_HALF, D_MODEL)
    vmem = pltpu.VMEM
    scratch = [vmem((2, SHARD_ROWS, D_MODEL), F32), vmem((N_CHIPS,) + half, F32),
               vmem((N_REL,) + half, BF16), vmem((N_REL,) + half, BF16), vmem(half, F32), vmem(half, F32)]
    scratch += [vmem((N_CHIPS, 1) + hs, F32) for hs in big_half] * 2
    scratch += [vmem((N_REL,) + hs, BF16) for hs in big_half] * 2
    scratch += [vmem(hs, F32) for hs in big_half] * 2
    scratch += [vmem(a.shape, F32) for a in small] * 2 + [vmem((N_CHIPS,) + a.shape, F32) for a in small]
    scratch += [pltpu.SemaphoreType.DMA((n_sems,)), pltpu.SemaphoreType.DMA((n_sems,)),
                pltpu.SemaphoreType.DMA((n_local,)), vmem(h.shape, BF16), pltpu.SemaphoreType.DMA]
    n_hbm = n_big + n_small
    out = pl.pallas_call(
        body, name="reduce_gradients",
        out_shape=[jax.ShapeDtypeStruct((SHARD_ROWS, D_MODEL), F32)]
        + [jax.ShapeDtypeStruct((2 * hs[0], hs[1]), F32) for hs in big_half]
        + [jax.ShapeDtypeStruct((N_CHIPS,) + a.shape, F32) for a in small],
        grid_spec=pltpu.PrefetchScalarGridSpec(
            num_scalar_prefetch=1, grid=(N_CHIPS, n_sub),
            in_specs=[pl.BlockSpec((pl.Element(tile), pl.Element(SHARD_WINDOW)),
                                   lambda s, t, m: (t * tile, _shard_window_start(shard_of_slot(s, m[0])))),
                      ANY_SPEC] + [ANY_SPEC] * n_hbm,
            out_specs=[ANY_SPEC] * (1 + n_hbm),
            scratch_shapes=scratch),
        compiler_params=pltpu.CompilerParams(vmem_limit_bytes=VMEM_LIMIT),
    )(shard_arr, dproj, h, *big, *small)
    return out[:1 + n_big], out[1 + n_big:]


def _memkv_backward(mem, dmkv, g_mem, w_mkv):
    n_ex = mem.shape[0]

    def body(mem_ref, d_ref, g_ref, w_ref, dw_ref, dg_ref):
        @pl.when(pl.program_id(0) == 0)
        def _():
            dw_ref[...] = jnp.zeros_like(dw_ref)
            dg_ref[...] = jnp.zeros_like(dg_ref)

        m = mem_ref[0]
        mn = m * lax.rsqrt(jnp.mean(m * m, axis=-1, keepdims=True) + EPS)
        d_b = d_ref[0].astype(BF16)
        dw_ref[...] += _mm_tn((mn * g_ref[...]).astype(BF16), d_b)
        dg_ref[...] += jnp.sum(_mm_nt(d_b, w_ref[...]) * mn, axis=0, keepdims=True)

    return pl.pallas_call(
        body, name="memkv_backward", grid=(n_ex,),
        out_shape=[jax.ShapeDtypeStruct((D_MODEL, 2 * MEM_WIDTH), F32), jax.ShapeDtypeStruct((1, D_MODEL), F32)],
        in_specs=[pl.BlockSpec((1, MEM_LEN, D_MODEL), lambda b: (b, 0, 0)),
                  pl.BlockSpec((1, MEM_LEN, 2 * MEM_WIDTH), lambda b: (b, 0, 0)),
                  _full_spec((1, D_MODEL)), _full_spec((D_MODEL, 2 * MEM_WIDTH))],
        out_specs=[_full_spec((D_MODEL, 2 * MEM_WIDTH)), _full_spec((1, D_MODEL))],
    )(mem, dmkv, g_mem, w_mkv)


def _pack_small_grads(dgpre, dgpost, dgmem, dvg, dvb, dws, dbs, dsink, drel, loss_vec, buckets):
    def body(dgpre_ref, dgpost_ref, dgmem_ref, dvg_ref, dvb_ref, dws_ref, dbs_ref, dsink_ref, drel_ref, loss_ref,
             bk_ref, a_ref, b_ref):
        a_ref[...] = jnp.zeros_like(a_ref)
        b_ref[...] = jnp.zeros_like(b_ref)
        a_ref[0:1, :] = dgpre_ref[...]
        a_ref[1:2, :] = dgpost_ref[...]
        a_ref[2:3, :] = dgmem_ref[...]
        a_ref[3:4, :] = jnp.concatenate([dvg_ref[...], dvb_ref[...]], axis=-1)
        a_ref[ROW_LOSS:ROW_LOSS + 1, 0:128] = loss_ref[...]
        row = lax.broadcasted_iota(jnp.int32, (CHUNK, CHUNK), 0)
        col = lax.broadcasted_iota(jnp.int32, (CHUNK, CHUNK), 1)
        for g in range(A_GROUPS):
            b_ref[ROW_WS + g * CHUNK:ROW_WS + (g + 1) * CHUNK, :] = jnp.where(row >= col, dws_ref[g], 0.0)
            by_token = jnp.transpose(dbs_ref[:, g * 128:(g + 1) * 128])
            b_ref[ROW_BS + g:ROW_BS + g + 1, :] = jnp.sum(by_token, axis=0, keepdims=True)
        b_ref[ROW_SINK:ROW_SINK + 1, :] = dsink_ref[...]
        bk = bk_ref[...]
        rel_row = lax.broadcasted_iota(jnp.int32, (8, 128), 0)
        rel_col = lax.broadcasted_iota(jnp.int32, (8, 128), 1)
        rel = jnp.zeros((8, 128), F32)
        for h in range(4):
            acc = drel_ref[h * CHUNK:(h + 1) * CHUNK, :]
            for b in range(N_BUCKETS):
                rel = jnp.where((rel_row == h) & (rel_col == b), jnp.sum(jnp.where(bk == b, acc, 0.0)), rel)
        b_ref[ROW_REL:ROW_REL + 8, :] = rel

    return pl.pallas_call(
        body, name="pack_small_grads",
        out_shape=[jax.ShapeDtypeStruct((SMALL_A_ROWS, D_MODEL), F32), jax.ShapeDtypeStruct((SMALL_B_ROWS, 128), F32)],
        in_specs=[VMEM_SPEC] * 11, out_specs=[VMEM_SPEC] * 2,
    )(dgpre, dgpost, dgmem, dvg, dvb, dws, dbs, dsink, drel, loss_vec, buckets)


def _adamw(w, g, m, v):
    m2 = ADAM_B1 * m + (1.0 - ADAM_B1) * g
    v2 = ADAM_B2 * v + (1.0 - ADAM_B2) * (g * g)
    m_hat = m2 / (1.0 - ADAM_B1 ** ADAM_STEP)
    v_hat = v2 / (1.0 - ADAM_B2 ** ADAM_STEP)
    delta = -ADAM_LR * (m_hat / (jnp.sqrt(v_hat) + ADAM_EPS) + ADAM_WD * w)
    return delta, m2, v2


ADAM_MAX_ROWS = 176


def _adamw_whole(g, w, m, v, name):
    rows, cols = w.shape
    steps = -(-rows // ADAM_MAX_ROWS)
    block_rows = rows // steps
    assert block_rows * steps == rows and block_rows % 8 == 0

    def body(g_ref, w_ref, m_ref, v_ref, d_out, m_out, v_out):
        delta, m2, v2 = _adamw(w_ref[...], g_ref[...], m_ref[...], v_ref[...])
        d_out[...] = delta
        m_out[...] = m2
        v_out[...] = v2

    block = pl.BlockSpec((block_rows, cols), lambda k: (k, 0))
    out = pl.pallas_call(
        body, name=name, grid=(steps,), out_shape=[jax.ShapeDtypeStruct((rows, cols), F32)] * 3,
        in_specs=[block] * 4, out_specs=[block] * 3,
    )(g, w, m, v)
    return [g] + list(out)


def _adamw_small(ra, rb, weights, moments_m, moments_v):
    n = len(weights)

    def body(*refs):
        ra_ref, rb_ref = refs[0], refs[1]
        w_refs, m_refs, v_refs = refs[2:2 + n], refs[2 + n:2 + 2 * n], refs[2 + 2 * n:2 + 3 * n]
        outs = refs[2 + 3 * n:]
        g_outs, d_outs, m_outs, v_outs = outs[:n], outs[n:2 * n], outs[2 * n:3 * n], outs[3 * n:4 * n]
        ga, gb = ra_ref[0], rb_ref[0]
        for chip in range(1, N_CHIPS):
            ga = ga + ra_ref[chip]
            gb = gb + rb_ref[chip]
        outs[4 * n][...] = ga[ROW_LOSS:ROW_LOSS + 1, 0:128]
        grads = [ga[0:1, :], ga[1:2, :], ga[2:3, :], ga[3:4, :A_WIDTH], ga[3:4, A_WIDTH:],
                 gb[ROW_WS:ROW_WS + A_GROUPS * CHUNK, :].reshape(A_GROUPS, CHUNK, CHUNK),
                 gb[ROW_BS:ROW_BS + A_GROUPS, :], gb[ROW_SINK:ROW_SINK + 1, 0:4],
                 gb[ROW_REL:ROW_REL + 4, 0:N_BUCKETS]]
        for k in range(n):
            delta, m2, v2 = _adamw(w_refs[k][...], grads[k], m_refs[k][...], v_refs[k][...])
            g_outs[k][...] = grads[k]
            d_outs[k][...] = delta
            m_outs[k][...] = m2
            v_outs[k][...] = v2

    out_shape = [jax.ShapeDtypeStruct(w.shape, F32) for w in weights] * 4 + [jax.ShapeDtypeStruct((1, 128), F32)]
    return pl.pallas_call(
        body, name="adamw_small", out_shape=out_shape,
        in_specs=[VMEM_SPEC] * (2 + 3 * n), out_specs=[VMEM_SPEC] * (4 * n + 1),
    )(ra, rb, *weights, *moments_m, *moments_v)


def kernel(x, mem, pre_norm_g, post_norm_g, mem_norm_g, w_in, w_mem_kv, v_norm_g, v_norm_b, w_spatial, b_spatial, attn_sinks, rel_bias, w_out, loss_target, m_pre_norm_g, m_post_norm_g, m_mem_norm_g, m_w_in, m_w_mem_kv, m_v_norm_g, m_v_norm_b, m_w_spatial, m_b_spatial, m_attn_sinks, m_rel_bias, m_w_out, v_pre_norm_g, v_post_norm_g, v_mem_norm_g, v_w_in, v_w_mem_kv, v_v_norm_g, v_v_norm_b, v_w_spatial, v_b_spatial, v_attn_sinks, v_rel_bias, v_w_out):
    n_ex, seq, _ = x.shape
    n_tok = n_ex * seq
    x2 = x.reshape(n_tok, D_MODEL)
    tgt2 = loss_target.reshape(n_tok, D_MODEL)
    buckets = jnp.asarray(_bucket_map())
    shard_arr = (2 * lax.axis_index("x") + lax.axis_index("y")).astype(jnp.int32).reshape(1)
    w_sp = w_spatial[0]
    b_sp = jnp.broadcast_to(b_spatial[0][:, :, None], (A_GROUPS, CHUNK, CHUNK))
    w_in_t, m_w_in_t, v_w_in_t = (jnp.transpose(a[0]) for a in (w_in, m_w_in, v_w_in))
    rel_t, m_rel_t, v_rel_t = (jnp.transpose(a) for a in (rel_bias, m_rel_bias, v_rel_bias))

    x_arr = lax.axis_index("x").astype(jnp.int32).reshape(1)
    h_b, parts, (w_in_b, g_mkv, g_out) = _gather_and_project(x2, pre_norm_g, w_in_t, w_mem_kv[0], w_out[0], x_arr)
    w_mkv_b = g_mkv.reshape(D_MODEL, 2 * MEM_WIDTH)
    w_out_b = g_out.reshape(MIX_WIDTH, D_MODEL)

    bias = _make_bias(rel_t, buckets)
    mkv = _memkv_forward(mem, mem_norm_g, w_mkv_b)
    dout, dproj, dmkv, dwout, dvg, dvb, dws, dbs, dsink, drel, loss_vec, dgpost = _mix(
        parts, mkv, x2, tgt2, v_norm_g, v_norm_b, w_sp, b_sp, attn_sinks, bias, w_out_b, post_norm_g, n_ex, seq)

    dx, dgpre = _backward_projection(x2, dout, dproj, pre_norm_g, w_in_b)
    dwmkv, dgmem = _memkv_backward(mem, dmkv, mem_norm_g, w_mkv_b)
    small_a, small_b = _pack_small_grads(dgpre, dgpost, dgmem, dvg, dvb, dws, dbs, dsink, drel, loss_vec, buckets)

    shard_shapes = [w_mem_kv.shape[1:], w_out.shape[1:]]
    big = [g.reshape(N_CHIPS, 2, s[0] // 2, s[1]) for g, s in zip((dwmkv, dwout), shard_shapes)]
    (g_win, g_wmkv, g_wout), (ga, gb) = _reduce_gradients(dproj, h_b, big, [small_a, small_b], shard_arr)

    big_out = [_adamw_whole(g_win, w_in_t, m_w_in_t, v_w_in_t, "adamw_w_in"),
               _adamw_whole(g_wmkv, w_mem_kv[0], m_w_mem_kv[0], v_w_mem_kv[0], "adamw_w_mem_kv"),
               _adamw_whole(g_wout, w_out[0], m_w_out[0], v_w_out[0], "adamw_w_out")]
    small_w = [pre_norm_g, post_norm_g, mem_norm_g, v_norm_g, v_norm_b, w_sp, b_spatial[0], attn_sinks, rel_t]
    small_m = [m_pre_norm_g, m_post_norm_g, m_mem_norm_g, m_v_norm_g, m_v_norm_b, m_w_spatial[0], m_b_spatial[0],
               m_attn_sinks, m_rel_t]
    small_v = [v_pre_norm_g, v_post_norm_g, v_mem_norm_g, v_v_norm_g, v_v_norm_b, v_w_spatial[0], v_b_spatial[0],
               v_attn_sinks, v_rel_t]
    small_out = _adamw_small(ga, gb, small_w, small_m, small_v)
    n_small = len(small_w)

    outputs = [small_out[4 * n_small][0, 0], dx.reshape(x.shape)]
    for kind in range(4):
        s = small_out[kind * n_small:(kind + 1) * n_small]
        outputs += [s[0], s[1], s[2], jnp.transpose(big_out[0][kind])[None], big_out[1][kind][None], s[3], s[4],
                    s[5][None], s[6][None], s[7], jnp.transpose(s[8]), big_out[2][kind][None]]
    return tuple(outputs)
```

```python
import functools

import numpy as np
import jax
import jax.numpy as jnp
from jax import lax
from jax.experimental import pallas as pl
from jax.experimental.pallas import tpu as pltpu

F32 = jnp.float32
BF16 = jnp.bfloat16
MESH = pl.DeviceIdType.MESH

D_MODEL = 1024
CHUNK = 128
A_WIDTH = 512
A_GROUPS = 4
SWA_WIDTH = 256
KV_WIDTH = 128
MEM_WIDTH = 256
MEM_LEN = 256
MIX_WIDTH = 1024
IN_WIDTH = 2816
N_BUCKETS = 32
MAX_DISTANCE = 128
EPS = 1e-6
NEG = -1e30
QK_SCALE = 0.125
HALF_HEAD_PAIR = 64

ADAM_LR = 0.001
ADAM_B1 = 0.9
ADAM_B2 = 0.999
ADAM_EPS = 1e-08
ADAM_WD = 0.01
ADAM_STEP = 10

N_CHIPS = 4
TILE_CHUNKS = 2
TILE = TILE_CHUNKS * CHUNK
PROJ_TILE = 512
VMEM_LIMIT = 56 * 1024 * 1024

SMALL_A_ROWS = 8
ROW_LOSS = 4
ROW_WS = 0
ROW_BS = 512
ROW_SINK = 520
ROW_REL = 528
SMALL_B_ROWS = 536


def _mm(a, b):
    return lax.dot_general(a, b, (((1,), (0,)), ((), ())), preferred_element_type=F32)


def _mm_nt(a, b):
    return lax.dot_general(a, b, (((1,), (1,)), ((), ())), preferred_element_type=F32)


def _mm_tn(a, b):
    return lax.dot_general(a, b, (((0,), (0,)), ((), ())), preferred_element_type=F32)


def _bucket_map():
    qi = np.arange(CHUNK)[:, None]
    kj = np.arange(2 * CHUNK)[None, :]
    n = np.maximum(qi + CHUNK - kj, 0)
    max_exact = N_BUCKETS // 2
    large = max_exact + (np.log(np.maximum(n, 1) / max_exact) / np.log(MAX_DISTANCE / max_exact)
                         * (N_BUCKETS - max_exact)).astype(np.int32)
    large = np.minimum(large, N_BUCKETS - 1)
    return np.where(n < max_exact, n, large).astype(np.int32)


_GELU_C = 0.7978845608028654
_GELU_A = 0.044715
_GELU_K1 = 2.0 * _GELU_C
_GELU_K2 = 2.0 * _GELU_C * _GELU_A


def _gelu(x):
    x2 = x * x
    s = 1.0 / (1.0 + jnp.exp(x * (-_GELU_K1 - _GELU_K2 * x2)))
    return x * s, (s, x2)


def _gelu_grad(x, saved):
    s, x2 = saved
    return s + x * (s * (1.0 - s)) * (_GELU_K1 + 3.0 * _GELU_K2 * x2)


def _sigmoid(x):
    return 1.0 / (1.0 + jnp.exp(-x))


def _lane_lo(shape):
    return lax.broadcasted_iota(jnp.int32, shape, 1) < HALF_HEAD_PAIR


def _swa_variants(t):
    lo = _lane_lo(t.shape)
    tr = pltpu.roll(t, HALF_HEAD_PAIR, 1)
    zero = jnp.zeros_like(t)
    return (jnp.where(lo, t, zero).astype(BF16), jnp.where(lo, zero, tr).astype(BF16),
            jnp.where(lo, tr, zero).astype(BF16), jnp.where(lo, zero, t).astype(BF16))


def _swa_unvariants(d0, d1, d2, d3):
    lo = _lane_lo(d0.shape)
    zero = jnp.zeros_like(d0)
    rolled = jnp.where(lo, zero, d1) + jnp.where(lo, d2, zero)
    return jnp.where(lo, d0, zero) + jnp.where(lo, zero, d3) + pltpu.roll(rolled, HALF_HEAD_PAIR, 1)


def _mem_variants(t):
    out = []
    for pair in range(2):
        tp = t[:, pair * 128:(pair + 1) * 128]
        lo = _lane_lo(tp.shape)
        zero = jnp.zeros_like(tp)
        out.append(jnp.where(lo, tp, zero).astype(BF16))
        out.append(jnp.where(lo, zero, tp).astype(BF16))
    return out


def _mem_unvariants(d0, d1, d2, d3):
    lo = _lane_lo(d0.shape)
    return jnp.concatenate([jnp.where(lo, d0, d1), jnp.where(lo, d2, d3)], axis=-1)


def _softmax(logits, sinks):
    m = jnp.max(logits, axis=-1, keepdims=True)
    if sinks is not None:
        m = jnp.maximum(m, sinks)
    p = jnp.exp(logits - m)
    den = jnp.sum(p, axis=-1, keepdims=True)
    if sinks is None:
        return p * (1.0 / den), None
    es = jnp.exp(sinks - m)
    inv = 1.0 / (den + es)
    return p * inv, es * inv


def _band_valid(with_prev):
    qi = lax.broadcasted_iota(jnp.int32, (CHUNK, 2 * CHUNK), 0)
    kj = lax.broadcasted_iota(jnp.int32, (CHUNK, 2 * CHUNK), 1)
    in_cur = (kj >= CHUNK) & (kj - CHUNK <= qi)
    if not with_prev:
        return in_cur
    return in_cur | ((kj < CHUNK) & (kj > qi))


def _causal_weights(ws_ref):
    row = lax.broadcasted_iota(jnp.int32, (CHUNK, CHUNK), 0)
    col = lax.broadcasted_iota(jnp.int32, (CHUNK, CHUNK), 1)
    return [jnp.where(row >= col, ws_ref[g], 0.0).astype(BF16) for g in range(A_GROUPS)]


def _rows_to_lanes(a, n):
    return jnp.concatenate([a[c * CHUNK:(c + 1) * CHUNK] for c in range(n)], axis=1)


def _lanes_to_rows(a, n):
    w = a.shape[1] // n
    return jnp.concatenate([a[:, c * w:(c + 1) * w] for c in range(n)], axis=0)


def _stack_heads(pair01, pair23):
    return jnp.concatenate([pair01[:, :256], pair01[:, 256:], pair23[:, :256], pair23[:, 256:]], axis=0)


def _pair_heads(s, r):
    return (jnp.concatenate([s[0:r], s[r:2 * r]], axis=1), jnp.concatenate([s[2 * r:3 * r], s[3 * r:4 * r]], axis=1))


def _pair_operands(variants):
    return (jnp.concatenate(variants[0:2], axis=0), jnp.concatenate(variants[2:4], axis=0))


def _split_pair_grads(d_pairs):
    return d_pairs[0][:256], d_pairs[0][256:], d_pairs[1][:256], d_pairs[1][256:]


def _halves_bf16(a):
    return (a[:, :128].astype(BF16), a[:, 128:].astype(BF16))


def _group_a_forward(au, av, vg, vb, wm, bs_rows):
    gu, tu = _gelu(au)
    gv, tv = _gelu(av)
    ya, res = [], []
    for g in range(A_GROUPS):
        sl = slice(g * 128, (g + 1) * 128)
        xg = gv[:, sl]
        xc = xg - jnp.mean(xg, axis=-1, keepdims=True)
        rstd = lax.rsqrt(jnp.mean(xc * xc, axis=-1, keepdims=True) + EPS)
        xhat = xc * rstd
        vn = _rows_to_lanes((xhat * vg[:, sl] + vb[:, sl]).astype(BF16), TILE_CHUNKS)
        s = _lanes_to_rows(_mm(wm[g], vn), TILE_CHUNKS) + bs_rows[g]
        ya.append(gu[:, sl] * s)
        res.append((xhat, rstd, vn, s))
    return ya, dict(gu=gu, tu=tu, tv=tv, groups=res)


def _attention_probs(qp, k_pairs, bias, sink_col):
    logits = _stack_heads(_mm_nt(qp[0], k_pairs[0]), _mm_nt(qp[1], k_pairs[1]))
    if bias is not None:
        logits = logits + bias
    return _softmax(logits, sink_col)


def _attention_out(p, v_pairs, r):
    pp = _pair_heads(p.astype(BF16), r)
    return jnp.concatenate([_mm(pp[0], v_pairs[0]), _mm(pp[1], v_pairs[1])], axis=-1), pp


def _attention_backward(p, pp, do_pairs, qp, k_pairs, v_pairs, r):
    dp = _stack_heads(_mm_nt(do_pairs[0], v_pairs[0]), _mm_nt(do_pairs[1], v_pairs[1]))
    delta = jnp.sum(p * dp, axis=-1, keepdims=True)
    dl = p * (dp - delta)
    dlp = _pair_heads(dl.astype(BF16), r)
    dq = jnp.concatenate([_mm(dlp[0], k_pairs[0]), _mm(dlp[1], k_pairs[1])], axis=-1)
    dk = (_mm_tn(dlp[0], qp[0]), _mm_tn(dlp[1], qp[1]))
    dv = (_mm_tn(pp[0], do_pairs[0]), _mm_tn(pp[1], do_pairs[1]))
    return dl, delta, dq, dk, dv


def _tile_specs(n_tiles_ex, width):
    return pl.BlockSpec((TILE, width), lambda b, i: (b * n_tiles_ex + jnp.minimum(i, n_tiles_ex - 1), 0))


def _prev_chunk_spec(n_tiles_ex, width):
    def index(b, i):
        chunk = TILE_CHUNKS * jnp.minimum(i, n_tiles_ex - 1)
        return (b * n_tiles_ex * TILE_CHUNKS + jnp.maximum(chunk - 1, 0), 0)
    return pl.BlockSpec((CHUNK, width), index)


def _full_spec(shape):
    zeros = (0,) * len(shape)
    return pl.BlockSpec(shape, lambda *_: zeros)


SMEM_SPEC = pl.BlockSpec(memory_space=pltpu.SMEM)
ANY_SPEC = pl.BlockSpec(memory_space=pl.ANY)
VMEM_SPEC = pl.BlockSpec(memory_space=pltpu.VMEM)


def _fill_bias(rel_ref, bk_ref, out_ref):
    bk = bk_ref[...]
    for h in range(4):
        acc = jnp.zeros((CHUNK, 2 * CHUNK), F32)
        for b in range(N_BUCKETS):
            acc = jnp.where(bk == b, rel_ref[h, b], acc)
        for t, with_prev in enumerate((True, False)):
            out_ref[t, h * CHUNK:(h + 1) * CHUNK, :] = jnp.where(_band_valid(with_prev), acc, NEG)


def _memkv_forward(mem, g_mem, w_mkv):
    n_ex = mem.shape[0]

    def body(mem_ref, g_ref, w_ref, out_ref):
        m = mem_ref[0]
        r = lax.rsqrt(jnp.mean(m * m, axis=-1, keepdims=True) + EPS)
        out_ref[0] = _mm((m * r * g_ref[...]).astype(BF16), w_ref[...])

    return pl.pallas_call(
        body, name="memkv_forward", grid=(n_ex,),
        out_shape=jax.ShapeDtypeStruct((n_ex, MEM_LEN, 2 * MEM_WIDTH), F32),
        in_specs=[pl.BlockSpec((1, MEM_LEN, D_MODEL), lambda b: (b, 0, 0)), _full_spec((1, D_MODEL)),
                  _full_spec((D_MODEL, 2 * MEM_WIDTH))],
        out_specs=pl.BlockSpec((1, MEM_LEN, 2 * MEM_WIDTH), lambda b: (b, 0, 0)),
    )(mem, g_mem, w_mkv)


PROJ_WIDTHS = (A_WIDTH, A_WIDTH, SWA_WIDTH, KV_WIDTH, KV_WIDTH, MEM_WIDTH, MIX_WIDTH)
PROJ_OFFSETS = tuple(int(v) for v in np.cumsum((0,) + PROJ_WIDTHS))


HALF_WIDTH = IN_WIDTH // 2
HALF_PARTS = ((0, 1, 2, 3), (4, 5, 6))


def _gather_and_project(x2, g_pre, w_in_s, w_mkv_s, w_out_s, rel_bias_t, buckets, x_arr):
    n_tok = x2.shape[0]
    n_tiles = n_tok // PROJ_TILE
    last = n_tiles - 1
    shapes = [w_in_s.shape, w_mkv_s.shape, w_out_s.shape]
    n_w = len(shapes)

    def body(x_sref, x_ref, g_ref, win_hbm, wmkv_hbm, wout_hbm, rel_ref, bk_ref, h_ref, *refs):
        part_refs, refs = refs[:len(PROJ_WIDTHS)], refs[len(PROJ_WIDTHS):]
        bias_ref, refs = refs[0], refs[1:]
        gin_hbm, gmkv_hbm, gout_hbm, wg, stage_in, stage_mkv, stage_out, own_mkv, own_out, h_all = refs[:10]
        send_sems, recv_sems, local_sems = refs[10:]
        p, t = pl.program_id(0), pl.program_id(1)
        x, y, c = lax.axis_index("x"), lax.axis_index("y"), lax.axis_index("c")
        me, sibling = (x, y, c), (x, y, 1 - c)
        my_shard = 2 * x + y
        gathered = [wg, gmkv_hbm, gout_hbm]

        def half_rows(w, shard, half):
            rows = shapes[w][0] // 2
            if w == 0:
                return wg.at[pl.ds(pl.multiple_of(shard * shapes[0][0] + half * rows, 16), rows), :]
            return gathered[w].at[shard, pl.ds(half * rows, rows), :]

        def first(w, rel):
            src = half_rows(w, my_shard, c) if w == 0 else (own_mkv, own_out)[w - 1].at[
                pl.ds(c * (shapes[w][0] // 2), shapes[w][0] // 2), :]
            k = 3 * w + rel - 1
            return pltpu.make_async_remote_copy(
                src_ref=src, dst_ref=half_rows(w, my_shard, c), send_sem=send_sems.at[k], recv_sem=recv_sems.at[k],
                device_id=(x ^ (rel >> 1), y ^ (rel & 1), c), device_id_type=MESH)

        def landed(w, rel):
            k = 3 * w + rel - 1
            ref = half_rows(w, my_shard ^ rel, c)
            return pltpu.make_async_remote_copy(src_ref=ref, dst_ref=ref, send_sem=send_sems.at[k],
                                                recv_sem=recv_sems.at[k], device_id=me, device_id_type=MESH)

        def passed(w, rel, half, to):
            k = 9 + 3 * w + rel - 1
            ref = half_rows(w, my_shard ^ rel, half)
            return pltpu.make_async_remote_copy(src_ref=ref, dst_ref=ref, send_sem=send_sems.at[k],
                                                recv_sem=recv_sems.at[k], device_id=to, device_id_type=MESH)

        def pass_on(w, rels):
            for rel in rels:
                landed(w, rel).wait_recv()
                passed(w, rel, c, sibling).start()
            for rel in rels:
                passed(w, rel, 1 - c, me).wait_recv()

        own_stores = [pltpu.make_async_copy(own_mkv, gmkv_hbm.at[my_shard], local_sems.at[3]),
                      pltpu.make_async_copy(own_out, gout_hbm.at[my_shard], local_sems.at[4])]

        @pl.when((p == 0) & (t == 0))
        def _():
            loads = [pltpu.make_async_copy(src, dst, local_sems.at[k]) for k, (src, dst) in enumerate(
                ((win_hbm, stage_in), (wmkv_hbm, stage_mkv), (wout_hbm, stage_out)))]
            for cp in loads:
                cp.start()
            loads[0].wait()
            wg[pl.ds(pl.multiple_of(my_shard * shapes[0][0], 16), shapes[0][0]), :] = stage_in[...].astype(BF16)
            for rel in (1, 2):
                first(0, rel).start()
            loads[1].wait()
            loads[2].wait()
            own_mkv[...] = stage_mkv[...].astype(BF16)
            own_out[...] = stage_out[...].astype(BF16)
            for cp in own_stores:
                cp.start()
            _fill_bias(rel_ref, bk_ref, bias_ref)
            pass_on(0, (1,))
            first(0, 3).start()

        @pl.when((p == 0) & (t == n_tiles // 2))
        def _():
            for w in (1, 2):
                for rel in (1, 2, 3):
                    first(w, rel).start()

        @pl.when((p == 1) & (t == 0))
        def _():
            pass_on(0, (2, 3))

        which_half = p ^ x_sref[0]
        tile_rows = pl.ds(pl.multiple_of(t * PROJ_TILE, PROJ_TILE), PROJ_TILE)

        def project(h):
            proj = _mm_nt(h, wg[pl.ds(pl.multiple_of(which_half * HALF_WIDTH, 16), HALF_WIDTH), :])
            for hh in range(2):
                @pl.when(which_half == hh)
                def _():
                    for k in HALF_PARTS[hh]:
                        lo = PROJ_OFFSETS[k] - hh * HALF_WIDTH
                        part_refs[k][...] = proj[:, lo:lo + PROJ_WIDTHS[k]]

        @pl.when(p == 0)
        def _():
            xv = x_ref[...]
            r = lax.rsqrt(jnp.mean(xv * xv, axis=-1, keepdims=True) + EPS)
            h = (xv * r * g_ref[...]).astype(BF16)
            h_ref[...] = h
            h_all[tile_rows, :] = h
            project(h)

        @pl.when(p == 1)
        def _():
            project(h_all[tile_rows, :])

        @pl.when((p == 1) & (t == last))
        def _():
            store = pltpu.make_async_copy(wg, gin_hbm, local_sems.at[5])
            store.start()
            for w in (1, 2):
                pass_on(w, (1, 2, 3))
            for w in range(n_w):
                for rel in (1, 2, 3):
                    first(w, rel).wait_send()
                    passed(w, rel, c, sibling).wait_send()
            for cp in own_stores:
                cp.wait()
            store.wait()

    def active_in(hh):
        def index(p, t, xs):
            return (jnp.where((p ^ xs[0]) == hh, t, jnp.where(p == 0, 0, last)), 0)
        return index

    part_specs = [pl.BlockSpec((PROJ_TILE, PROJ_WIDTHS[k]), active_in(hh)) for hh in range(2) for k in HALF_PARTS[hh]]
    vmem = pltpu.VMEM
    out = pl.pallas_call(
        body, name="gather_and_project",
        out_shape=[jax.ShapeDtypeStruct((n_tok, D_MODEL), BF16)]
        + [jax.ShapeDtypeStruct((n_tok, w), F32) for w in PROJ_WIDTHS]
        + [jax.ShapeDtypeStruct((2, 4 * CHUNK, 2 * CHUNK), F32)]
        + [jax.ShapeDtypeStruct((N_CHIPS * shapes[0][0], shapes[0][1]), BF16)]
        + [jax.ShapeDtypeStruct((N_CHIPS,) + s, BF16) for s in shapes[1:]],
        grid_spec=pltpu.PrefetchScalarGridSpec(
            num_scalar_prefetch=1, grid=(2, n_tiles),
            in_specs=[pl.BlockSpec((PROJ_TILE, D_MODEL), lambda p, t, xs: (jnp.where(p == 0, t, last), 0)),
                      pl.BlockSpec((1, D_MODEL), lambda p, t, xs: (0, 0)), ANY_SPEC, ANY_SPEC, ANY_SPEC, SMEM_SPEC,
                      pl.BlockSpec(buckets.shape, lambda p, t, xs: (0, 0))],
            out_specs=[pl.BlockSpec((PROJ_TILE, D_MODEL), lambda p, t, xs: (jnp.where(p == 0, t, last), 0))]
            + part_specs + [pl.BlockSpec((2, 4 * CHUNK, 2 * CHUNK), lambda p, t, xs: (0, 0, 0))] + [ANY_SPEC] * 3,
            scratch_shapes=[vmem((N_CHIPS * shapes[0][0], shapes[0][1]), BF16), vmem(shapes[0], F32),
                            vmem(shapes[1], F32), vmem(shapes[2], F32), vmem(shapes[1], BF16), vmem(shapes[2], BF16),
                            vmem((n_tok, D_MODEL), BF16),
                            pltpu.SemaphoreType.DMA((18,)), pltpu.SemaphoreType.DMA((18,)),
                            pltpu.SemaphoreType.DMA((6,))]),
        compiler_params=pltpu.CompilerParams(vmem_limit_bytes=VMEM_LIMIT),
    )(x_arr, x2, g_pre, w_in_s, w_mkv_s, w_out_s, rel_bias_t, buckets)
    n_parts = len(PROJ_WIDTHS)
    return out[0], list(out[1:1 + n_parts]), out[2 + n_parts:], out[1 + n_parts]


def _load_chunk(j, i, sk_ref, sv_ref, skp_ref, svp_ref):
    rows = slice(j * CHUNK, (j + 1) * CHUNK)
    if j == 0:
        k_prev, v_prev, table = skp_ref[...], svp_ref[...], jnp.where(i > 0, 0, 1)
    else:
        prev = slice((j - 1) * CHUNK, j * CHUNK)
        k_prev, v_prev, table = sk_ref[prev, :], sv_ref[prev, :], 0
    k_pairs = _pair_operands(_swa_variants(jnp.concatenate([k_prev, sk_ref[rows, :]], axis=0)))
    v_pairs = _pair_operands(_swa_variants(jnp.concatenate([v_prev, sv_ref[rows, :]], axis=0)))
    return rows, k_pairs, v_pairs, table


def _tile_constants(ws_ref, bs_ref, sink_ref, mkv_ref):
    wm = _causal_weights(ws_ref)
    bs_rows = [jnp.concatenate([bs_ref[g]] * TILE_CHUNKS, axis=0) for g in range(A_GROUPS)]
    sink_col = jnp.max(jnp.concatenate([jnp.full((CHUNK, 128), sink_ref[0, h], F32) for h in range(4)], axis=0),
                       axis=-1, keepdims=True)
    mkv_v = mkv_ref[0]
    mk_pairs = _pair_operands(_mem_variants(mkv_v[:, :MEM_WIDTH]))
    mv_pairs = _pair_operands(_mem_variants(mkv_v[:, MEM_WIDTH:]))
    return wm, bs_rows, sink_col, mk_pairs, mv_pairs


def _mix(parts, mkv, x2, tgt2, v_g, v_b, w_sp, b_sp, sinks, bias, w_out, g_post, n_ex, seq):
    n_tiles_ex = seq // TILE
    n_tok = n_ex * seq
    au, av, sq, sk, sv, mq, z = parts
    col = dict(zip(("au", "av", "sq", "sk", "sv", "mq", "z"),
                   (slice(PROJ_OFFSETS[k], PROJ_OFFSETS[k + 1]) for k in range(len(PROJ_WIDTHS)))))
    before_kv, after_kv = slice(0, col["sk"].start), slice(col["sv"].stop, IN_WIDTH)

    def body(au_ref, av_ref, sq_ref, sk_ref, sv_ref, skp_ref, svp_ref, mq_ref, z_ref, mkv_ref, x_ref, tgt_ref,
             vg_ref, vb_ref, ws_ref, bs_ref, sink_ref, bias_ref, wout_ref, gpost_ref,
             dout_ref, dproj_ref, dmkv_ref, dwout_ref, dvg_ref, dvb_ref, dws_ref, dbs_ref, dsink_ref, drel_ref,
             loss_ref, dgpost_ref, carry_dp, carry_k, carry_v):
        b, i = pl.program_id(0), pl.program_id(1)

        @pl.when((b == 0) & (i == 0))
        def _():
            for ref in (dwout_ref, dvg_ref, dvb_ref, dws_ref, dbs_ref, dsink_ref, drel_ref, loss_ref, dgpost_ref):
                ref[...] = jnp.zeros_like(ref)

        @pl.when(i == 0)
        def _():
            dmkv_ref[...] = jnp.zeros_like(dmkv_ref)
            carry_k[...] = jnp.zeros_like(carry_k)
            carry_v[...] = jnp.zeros_like(carry_v)

        @pl.when(i > 0)
        def _():
            dproj_ref[:, before_kv] = carry_dp[:, before_kv]
            dproj_ref[:, after_kv] = carry_dp[:, after_kv]

        @pl.when(i < n_tiles_ex)
        def _():
            wm, bs_rows, sink_col, mk_pairs, mv_pairs = _tile_constants(ws_ref, bs_ref, sink_ref, mkv_ref)
            vg = vg_ref[...]

            au_v, av_v = au_ref[...], av_ref[...]
            ya, res = _group_a_forward(au_v, av_v, vg, vb_ref[...], wm, bs_rows)
            swa, yb = [], []
            for j in range(TILE_CHUNKS):
                rows, k_pairs, v_pairs, table = _load_chunk(j, i, sk_ref, sv_ref, skp_ref, svp_ref)
                qp = _halves_bf16(sq_ref[rows, :] * QK_SCALE)
                p, ps = _attention_probs(qp, k_pairs, bias_ref[table], sink_col)
                out, pp = _attention_out(p, v_pairs, CHUNK)
                yb.append(out)
                swa.append((rows, k_pairs, v_pairs, qp, p, ps, pp))
            mqp = _halves_bf16(mq_ref[...] * QK_SCALE)
            pm, _ = _attention_probs(mqp, mk_pairs, None, None)
            yc, ppm = _attention_out(pm, mv_pairs, TILE)
            ycat = jnp.concatenate(ya + [jnp.concatenate(yb, axis=0), yc], axis=-1)

            zv = z_ref[...]
            sig = _sigmoid(zv)
            sz = zv * sig
            y_b = (ycat * sz).astype(BF16)
            o = _mm(y_b, wout_ref[...])
            r2 = lax.rsqrt(jnp.mean(o * o, axis=-1, keepdims=True) + EPS)
            nrm = o * r2
            gp = gpost_ref[...]
            diff = x_ref[...] + nrm * gp - tgt_ref[...]
            loss_ref[...] += jnp.sum(diff * diff) * (0.5 / D_MODEL)
            dout = diff * (1.0 / D_MODEL)
            dout_ref[...] = dout
            dgpost_ref[...] += jnp.sum(dout * nrm, axis=0, keepdims=True)
            dn = dout * gp
            do_b = (r2 * (dn - nrm * jnp.mean(dn * nrm, axis=-1, keepdims=True))).astype(BF16)
            dwout_ref[...] += _mm_tn(y_b, do_b)
            dy = _mm_nt(do_b, wout_ref[...])
            carry_dp[:, col["z"]] = (dy * ycat * (sig * (1.0 + zv * (1.0 - sig)))).astype(BF16)
            dyc = dy * sz

            dgu, dgv = [], []
            for g in range(A_GROUPS):
                sl = slice(g * 128, (g + 1) * 128)
                xhat, rstd, vn, s = res["groups"][g]
                dya = dyc[:, sl]
                dgu.append(dya * s)
                ds = dya * res["gu"][:, sl]
                dbs_ref[:, sl] += sum(ds[c * CHUNK:(c + 1) * CHUNK] for c in range(TILE_CHUNKS))
                ds_b = _rows_to_lanes(ds.astype(BF16), TILE_CHUNKS)
                dws_ref[g] += _mm_nt(ds_b, vn)
                dvn = _lanes_to_rows(_mm_tn(wm[g], ds_b), TILE_CHUNKS)
                dvg_ref[:, sl] += jnp.sum(dvn * xhat, axis=0, keepdims=True)
                dvb_ref[:, sl] += jnp.sum(dvn, axis=0, keepdims=True)
                dxh = dvn * vg[:, sl]
                dgv.append(rstd * (dxh - jnp.mean(dxh, axis=-1, keepdims=True)
                                   - xhat * jnp.mean(dxh * xhat, axis=-1, keepdims=True)))
            carry_dp[:, col["au"]] = (jnp.concatenate(dgu, axis=-1) * _gelu_grad(au_v, res["tu"])).astype(BF16)
            carry_dp[:, col["av"]] = (jnp.concatenate(dgv, axis=-1) * _gelu_grad(av_v, res["tv"])).astype(BF16)

            lane4 = lax.broadcasted_iota(jnp.int32, (1, 128), 1)
            dsink_vec = jnp.zeros((1, 128), F32)
            dk_parts, dv_parts = [], []
            for rows, k_pairs, v_pairs, qp, p, ps, pp in swa:
                do_pairs = _halves_bf16(dyc[rows, A_WIDTH:A_WIDTH + SWA_WIDTH])
                dl, delta, dq, dk, dv = _attention_backward(p, pp, do_pairs, qp, k_pairs, v_pairs, CHUNK)
                sink_terms = ps * delta
                for h in range(4):
                    dsink_vec = dsink_vec + jnp.where(lane4 == h, -jnp.sum(sink_terms[h * CHUNK:(h + 1) * CHUNK]), 0.0)
                drel_ref[...] += dl
                carry_dp[rows, col["sq"]] = (dq * QK_SCALE).astype(BF16)
                dk_parts.append(_swa_unvariants(*_split_pair_grads(dk)))
                dv_parts.append(_swa_unvariants(*_split_pair_grads(dv)))
            dsink_ref[...] += dsink_vec

            dc_pairs = _halves_bf16(dyc[:, A_WIDTH + SWA_WIDTH:])
            _, _, dmq, dmk, dmv = _attention_backward(pm, ppm, dc_pairs, mqp, mk_pairs, mv_pairs, TILE)
            carry_dp[:, col["mq"]] = (dmq * QK_SCALE).astype(BF16)
            dmkv_ref[0] += jnp.concatenate([_mem_unvariants(*_split_pair_grads(dmk)),
                                            _mem_unvariants(*_split_pair_grads(dmv))], axis=-1)

            for parts_c, carry, cols in ((dk_parts, carry_k, col["sk"]), (dv_parts, carry_v, col["sv"])):
                @pl.when(i > 0)
                def _():
                    dproj_ref[:, cols] = (carry[...] + jnp.concatenate(
                        [jnp.zeros((TILE - CHUNK, KV_WIDTH), F32), parts_c[0][:CHUNK]], axis=0)).astype(BF16)
                new = [parts_c[0][CHUNK:]]
                for j in range(1, TILE_CHUNKS):
                    new[-1] = new[-1] + parts_c[j][:CHUNK]
                    new.append(parts_c[j][CHUNK:])
                carry[...] = jnp.concatenate(new, axis=0)

        @pl.when(i == n_tiles_ex)
        def _():
            dproj_ref[:, col["sk"]] = carry_k[...].astype(BF16)
            dproj_ref[:, col["sv"]] = carry_v[...].astype(BF16)

    tile = functools.partial(_tile_specs, n_tiles_ex)
    prev = functools.partial(_prev_chunk_spec, n_tiles_ex)
    late = pl.BlockSpec((TILE, IN_WIDTH), lambda b, i: (b * n_tiles_ex + jnp.maximum(i - 1, 0), 0))
    return pl.pallas_call(
        body, name="mix", grid=(n_ex, n_tiles_ex + 1),
        out_shape=[jax.ShapeDtypeStruct((n_tok, D_MODEL), F32), jax.ShapeDtypeStruct((n_tok, IN_WIDTH), BF16),
                   jax.ShapeDtypeStruct((n_ex, MEM_LEN, 2 * MEM_WIDTH), F32),
                   jax.ShapeDtypeStruct((MIX_WIDTH, D_MODEL), F32), jax.ShapeDtypeStruct((1, A_WIDTH), F32),
                   jax.ShapeDtypeStruct((1, A_WIDTH), F32), jax.ShapeDtypeStruct((A_GROUPS, CHUNK, CHUNK), F32),
                   jax.ShapeDtypeStruct((CHUNK, A_WIDTH), F32), jax.ShapeDtypeStruct((1, 128), F32),
                   jax.ShapeDtypeStruct((4 * CHUNK, 2 * CHUNK), F32), jax.ShapeDtypeStruct((1, 128), F32),
                   jax.ShapeDtypeStruct((1, D_MODEL), F32)],
        in_specs=[tile(A_WIDTH), tile(A_WIDTH), tile(SWA_WIDTH), tile(KV_WIDTH), tile(KV_WIDTH),
                  prev(KV_WIDTH), prev(KV_WIDTH), tile(MEM_WIDTH), tile(MIX_WIDTH),
                  pl.BlockSpec((1, MEM_LEN, 2 * MEM_WIDTH), lambda b, i: (b, 0, 0)),
                  tile(D_MODEL), tile(D_MODEL),
                  _full_spec((1, A_WIDTH)), _full_spec((1, A_WIDTH)), _full_spec((A_GROUPS, CHUNK, CHUNK)),
                  _full_spec((A_GROUPS, CHUNK, CHUNK)), SMEM_SPEC, _full_spec((2, 4 * CHUNK, 2 * CHUNK)),
                  _full_spec((MIX_WIDTH, D_MODEL)), _full_spec((1, D_MODEL))],
        out_specs=[tile(D_MODEL), late, pl.BlockSpec((1, MEM_LEN, 2 * MEM_WIDTH), lambda b, i: (b, 0, 0)),
                   _full_spec((MIX_WIDTH, D_MODEL)), _full_spec((1, A_WIDTH)), _full_spec((1, A_WIDTH)),
                   _full_spec((A_GROUPS, CHUNK, CHUNK)), _full_spec((CHUNK, A_WIDTH)), _full_spec((1, 128)),
                   _full_spec((4 * CHUNK, 2 * CHUNK)), _full_spec((1, 128)), _full_spec((1, D_MODEL))],
        scratch_shapes=[pltpu.VMEM((TILE, IN_WIDTH), BF16), pltpu.VMEM((TILE, KV_WIDTH), F32),
                        pltpu.VMEM((TILE, KV_WIDTH), F32)],
        compiler_params=pltpu.CompilerParams(vmem_limit_bytes=VMEM_LIMIT),
    )(au, av, sq, sk, sv, sk, sv, mq, z, mkv, x2, tgt2, v_g, v_b, w_sp, b_sp, sinks, bias, w_out, g_post)


BWD_PROJ_TILE = 512


def _backward_projection(x2, dout, dproj, g_pre, w_in_t):
    n_tok = x2.shape[0]
    n_steps = n_tok // BWD_PROJ_TILE

    def body(x_ref, dout_ref, dp_ref, g_ref, w_hbm, dx_ref, dgpre_ref, w_vmem, sem):
        @pl.when(pl.program_id(0) == 0)
        def _():
            load = pltpu.make_async_copy(w_hbm, w_vmem, sem)
            load.start()
            dgpre_ref[...] = jnp.zeros_like(dgpre_ref)
            load.wait()

        xv = x_ref[...]
        r = lax.rsqrt(jnp.mean(xv * xv, axis=-1, keepdims=True) + EPS)
        xn = xv * r
        dh = _mm(dp_ref[...], w_vmem[...])
        dgpre_ref[...] += jnp.sum(dh * xn, axis=0, keepdims=True)
        dhg = dh * g_ref[...]
        dx_ref[...] = r * (dhg - xn * jnp.mean(dhg * xn, axis=-1, keepdims=True)) + dout_ref[...]

    row = lambda w: pl.BlockSpec((BWD_PROJ_TILE, w), lambda i: (i, 0))
    return pl.pallas_call(
        body, name="backward_projection", grid=(n_steps,),
        out_shape=[jax.ShapeDtypeStruct((n_tok, D_MODEL), F32), jax.ShapeDtypeStruct((1, D_MODEL), F32)],
        in_specs=[row(D_MODEL), row(D_MODEL), row(IN_WIDTH), _full_spec((1, D_MODEL)), ANY_SPEC],
        out_specs=[row(D_MODEL), _full_spec((1, D_MODEL))],
        scratch_shapes=[pltpu.VMEM((IN_WIDTH, D_MODEL), BF16), pltpu.SemaphoreType.DMA],
        input_output_aliases={1: 0},
        compiler_params=pltpu.CompilerParams(vmem_limit_bytes=VMEM_LIMIT),
    )(x2, dout, dproj, g_pre, w_in_t)


SHARD_ROWS = IN_WIDTH // N_CHIPS
SHARD_WINDOW = 768
SHARD_HALF = SHARD_ROWS // 2
DWIN_TILE = 2048
N_REL = N_CHIPS - 1


def _shard_window_start(shard):
    return (shard * SHARD_ROWS // 128) * 128


def _reduce_gradients(dproj, h, big, small, shard_arr):
    n_tok = h.shape[0]
    tile = min(DWIN_TILE, n_tok)
    n_sub = n_tok // tile
    last = N_CHIPS - 1
    n_big, n_small = len(big), len(small)
    big_half = [g.shape[2:] for g in big]
    sem_big_d2d = 2 * N_CHIPS
    sem_big_ici = sem_big_d2d + n_big
    sem_big_swap = sem_big_ici + N_REL * n_big
    sem_small_d2d = sem_big_swap + n_big
    sem_small_ici = sem_small_d2d + n_small
    n_sems = sem_small_ici + N_REL * n_small
    loc_small = n_big
    loc_out_win = loc_small + n_small
    loc_out_big = loc_out_win + 2
    loc_out_small = loc_out_big + 2 * n_big
    n_local = loc_out_small + n_small

    def relation_of_slot(s):
        return (s + 2) % N_REL + 1

    def shard_of_slot(s, my_shard):
        return my_shard ^ jnp.where(s == last, 0, relation_of_slot(s))

    def body(shard_ref, dp_ref, h_hbm, *refs):
        h_vmem, h_sem, refs = refs[-2], refs[-1], refs[:-2]
        big_hbm, refs = refs[:n_big], refs[n_big:]
        small_hbm, refs = refs[:n_small], refs[n_small:]
        out_hbm, refs = refs[0], refs[1:]
        big_out, refs = refs[:n_big], refs[n_big:]
        small_out, refs = refs[:n_small], refs[n_small:]
        part, recv_d2d, send_ici, recv_ici, mine_buf, other_buf = refs[:6]
        refs = refs[6:]
        big_own, big_recv, big_send, big_land, big_mine, big_other = (
            refs[k * n_big:(k + 1) * n_big] for k in range(6))
        refs = refs[6 * n_big:]
        small_own, small_recv, small_all = (refs[k * n_small:(k + 1) * n_small] for k in range(3))
        send_sems, recv_sems, local_sems = refs[3 * n_small:]

        s, t = pl.program_id(0), pl.program_id(1)
        x, y, c = lax.axis_index("x"), lax.axis_index("y"), lax.axis_index("c")
        my_chip = 2 * x + y
        sibling = (x, y, 1 - c)
        my_rows = pl.ds(pl.multiple_of(c * SHARD_HALF, 8), SHARD_HALF)
        other_rows = pl.ds(pl.multiple_of((1 - c) * SHARD_HALF, 8), SHARD_HALF)

        def remote(src, dst, k, to):
            return pltpu.make_async_remote_copy(src_ref=src, dst_ref=dst, send_sem=send_sems.at[k],
                                                recv_sem=recv_sems.at[k], device_id=to, device_id_type=MESH)

        def chip_at(rel):
            return (x ^ (rel >> 1), y ^ (rel & 1), c)

        def to_sibling(k):
            return remote(part.at[k % 2, other_rows, :], recv_d2d.at[k], k, sibling)

        def to_chip(k):
            return remote(send_ici.at[k], recv_ici.at[k], N_CHIPS + k, chip_at(relation_of_slot(k)))

        swap = remote(mine_buf, other_buf, 2 * N_CHIPS - 1, sibling)
        big_load = [pltpu.make_async_copy(big_hbm[w].at[:, pl.ds(c, 1)], big_own[w], local_sems.at[w])
                    for w in range(n_big)]
        big_to_sibling = [remote(big_hbm[w].at[:, pl.ds(1 - c, 1)], big_recv[w], sem_big_d2d + w, sibling)
                          for w in range(n_big)]
        big_to_chip = [[remote(big_send[w].at[k], big_land[w].at[k], sem_big_ici + N_REL * w + k, chip_at(k + 1))
                        for k in range(N_REL)] for w in range(n_big)]
        big_swap = [remote(big_mine[w], big_other[w], sem_big_swap + w, sibling) for w in range(n_big)]
        small_load = [pltpu.make_async_copy(small_hbm[i], small_own[i], local_sems.at[loc_small + i])
                      for i in range(n_small)]
        small_to_sibling = [remote(small_hbm[i], small_recv[i], sem_small_d2d + i, sibling) for i in range(n_small)]
        small_to_chip = [[remote(small_all[i].at[my_chip], small_all[i].at[my_chip],
                                 sem_small_ici + N_REL * i + k, chip_at(k + 1))
                          for k in range(N_REL)] for i in range(n_small)]

        @pl.when((s == 0) & (t == 0))
        def _():
            h_load = pltpu.make_async_copy(h_hbm, h_vmem, h_sem)
            h_load.start()
            for cp in big_load + big_to_sibling + small_load + small_to_sibling:
                cp.start()
            h_load.wait()

        @pl.when((s == 0) & (t == n_sub - 1))
        def _():
            for cp in big_load + small_load:
                cp.wait()
            for cp in big_to_sibling + small_to_sibling:
                cp.wait_recv()
                cp.wait_send()
            for w in range(n_big):
                for k in range(N_REL):
                    shard = my_chip ^ (k + 1)
                    big_send[w][k] = (big_own[w][shard, 0] + big_recv[w][shard, 0]).astype(BF16)
                    big_to_chip[w][k].start()
            for i in range(n_small):
                small_all[i][my_chip] = small_own[i][...] + small_recv[i][...]
                for k in range(N_REL):
                    small_to_chip[i][k].start()

        @pl.when((s > 0) & (t == jnp.where(s == last, 0, min(1, n_sub - 1))))
        def _():
            k = s - 1
            cp = to_sibling(k)
            cp.wait_recv()
            cp.wait_send()
            send_ici[k] = (part[k % 2, my_rows, :] + recv_d2d[k]).astype(BF16)
            to_chip(k).start()

        def big_rows(w, half):
            rows = big_half[w][0]
            return big_out[w].at[pl.ds(pl.multiple_of(half * rows, 8), rows), :]

        big_store_mine = [pltpu.make_async_copy(big_mine[w], big_rows(w, c), local_sems.at[loc_out_big + 2 * w])
                          for w in range(n_big)]
        big_store_other = [pltpu.make_async_copy(big_other[w], big_rows(w, 1 - c),
                                                 local_sems.at[loc_out_big + 2 * w + 1]) for w in range(n_big)]
        small_store = [pltpu.make_async_copy(small_all[i], small_out[i], local_sems.at[loc_out_small + i])
                       for i in range(n_small)]

        @pl.when((s == last) & (t == 0))
        def _():
            for w in range(n_big):
                total = big_own[w][my_chip, 0] + big_recv[w][my_chip, 0]
                for k in range(N_REL):
                    big_to_chip[w][k].wait_recv()
                    total = total + big_land[w][k].astype(F32)
                big_mine[w][...] = total
                big_swap[w].start()
                big_store_mine[w].start()
            for i in range(n_small):
                for k in range(N_REL):
                    small_to_chip[i][k].wait_recv()
                small_store[i].start()

        r = _mm_tn(dp_ref[...], h_vmem[pl.ds(pl.multiple_of(t * tile, tile), tile), :])
        odd = shard_of_slot(s, shard_ref[0]) % 2
        for parity in range(2):
            rows = r[64 * parity:64 * parity + SHARD_ROWS]

            @pl.when((odd == parity) & (t == 0))
            def _():
                part[s % 2] = rows

            @pl.when((odd == parity) & (t > 0))
            def _():
                part[s % 2] += rows

        @pl.when(t == n_sub - 1)
        def _():
            to_sibling(s).start()

        @pl.when((s == last) & (t == n_sub - 1))
        def _():
            cp = to_sibling(last)
            cp.wait_recv()
            cp.wait_send()
            total = part[last % 2, my_rows, :] + recv_d2d[last]
            for k in range(last):
                to_chip(k).wait_recv()
                total = total + recv_ici[k].astype(F32)
            mine_buf[...] = total
            swap.start()
            out_mine = pltpu.make_async_copy(mine_buf, out_hbm.at[my_rows, :], local_sems.at[0])
            out_mine.start()
            swap.wait_recv()
            out_other = pltpu.make_async_copy(other_buf, out_hbm.at[other_rows, :], local_sems.at[1])
            out_other.start()
            for w in range(n_big):
                big_swap[w].wait_recv()
                big_store_other[w].start()
            stores = [out_mine, out_other] + big_store_mine + big_store_other + small_store
            for k in range(last):
                to_chip(k).wait_send()
            swap.wait_send()
            for w in range(n_big):
                for k in range(N_REL):
                    big_to_chip[w][k].wait_send()
                big_swap[w].wait_send()
            for i in range(n_small):
                for k in range(N_REL):
                    small_to_chip[i][k].wait_send()
            for cp in stores:
                cp.wait()

    half = (SHARD_HALF, D_MODEL)
    vmem = pltpu.VMEM
    scratch = [vmem((2, SHARD_ROWS, D_MODEL), F32), vmem((N_CHIPS,) + half, F32),
               vmem((N_REL,) + half, BF16), vmem((N_REL,) + half, BF16), vmem(half, F32), vmem(half, F32)]
    scratch += [vmem((N_CHIPS, 1) + hs, F32) for hs in big_half] * 2
    scratch += [vmem((N_REL,) + hs, BF16) for hs in big_half] * 2
    scratch += [vmem(hs, F32) for hs in big_half] * 2
    scratch += [vmem(a.shape, F32) for a in small] * 2 + [vmem((N_CHIPS,) + a.shape, F32) for a in small]
    scratch += [pltpu.SemaphoreType.DMA((n_sems,)), pltpu.SemaphoreType.DMA((n_sems,)),
                pltpu.SemaphoreType.DMA((n_local,)), vmem(h.shape, BF16), pltpu.SemaphoreType.DMA]
    n_hbm = n_big + n_small
    out = pl.pallas_call(
        body, name="reduce_gradients",
        out_shape=[jax.ShapeDtypeStruct((SHARD_ROWS, D_MODEL), F32)]
        + [jax.ShapeDtypeStruct((2 * hs[0], hs[1]), F32) for hs in big_half]
        + [jax.ShapeDtypeStruct((N_CHIPS,) + a.shape, F32) for a in small],
        grid_spec=pltpu.PrefetchScalarGridSpec(
            num_scalar_prefetch=1, grid=(N_CHIPS, n_sub),
            in_specs=[pl.BlockSpec((pl.Element(tile), pl.Element(SHARD_WINDOW)),
                                   lambda s, t, m: (t * tile, _shard_window_start(shard_of_slot(s, m[0])))),
                      ANY_SPEC] + [ANY_SPEC] * n_hbm,
            out_specs=[ANY_SPEC] * (1 + n_hbm),
            scratch_shapes=scratch),
        compiler_params=pltpu.CompilerParams(vmem_limit_bytes=VMEM_LIMIT),
    )(shard_arr, dproj, h, *big, *small)
    return out[:1 + n_big], out[1 + n_big:]


def _memkv_backward(mem, dmkv, g_mem, w_mkv):
    n_ex = mem.shape[0]

    def body(mem_ref, d_ref, g_ref, w_ref, dw_ref, dg_ref):
        @pl.when(pl.program_id(0) == 0)
        def _():
            dw_ref[...] = jnp.zeros_like(dw_ref)
            dg_ref[...] = jnp.zeros_like(dg_ref)

        m = mem_ref[0]
        mn = m * lax.rsqrt(jnp.mean(m * m, axis=-1, keepdims=True) + EPS)
        d_b = d_ref[0].astype(BF16)
        dw_ref[...] += _mm_tn((mn * g_ref[...]).astype(BF16), d_b)
        dg_ref[...] += jnp.sum(_mm_nt(d_b, w_ref[...]) * mn, axis=0, keepdims=True)

    return pl.pallas_call(
        body, name="memkv_backward", grid=(n_ex,),
        out_shape=[jax.ShapeDtypeStruct((D_MODEL, 2 * MEM_WIDTH), F32), jax.ShapeDtypeStruct((1, D_MODEL), F32)],
        in_specs=[pl.BlockSpec((1, MEM_LEN, D_MODEL), lambda b: (b, 0, 0)),
                  pl.BlockSpec((1, MEM_LEN, 2 * MEM_WIDTH), lambda b: (b, 0, 0)),
                  _full_spec((1, D_MODEL)), _full_spec((D_MODEL, 2 * MEM_WIDTH))],
        out_specs=[_full_spec((D_MODEL, 2 * MEM_WIDTH)), _full_spec((1, D_MODEL))],
    )(mem, dmkv, g_mem, w_mkv)


def _pack_small_grads(dgpre, dgpost, dgmem, dvg, dvb, dws, dbs, dsink, drel, loss_vec, buckets):
    def body(dgpre_ref, dgpost_ref, dgmem_ref, dvg_ref, dvb_ref, dws_ref, dbs_ref, dsink_ref, drel_ref, loss_ref,
             bk_ref, a_ref, b_ref):
        a_ref[...] = jnp.zeros_like(a_ref)
        b_ref[...] = jnp.zeros_like(b_ref)
        a_ref[0:1, :] = dgpre_ref[...]
        a_ref[1:2, :] = dgpost_ref[...]
        a_ref[2:3, :] = dgmem_ref[...]
        a_ref[3:4, :] = jnp.concatenate([dvg_ref[...], dvb_ref[...]], axis=-1)
        a_ref[ROW_LOSS:ROW_LOSS + 1, 0:128] = loss_ref[...]
        row = lax.broadcasted_iota(jnp.int32, (CHUNK, CHUNK), 0)
        col = lax.broadcasted_iota(jnp.int32, (CHUNK, CHUNK), 1)
        for g in range(A_GROUPS):
            b_ref[ROW_WS + g * CHUNK:ROW_WS + (g + 1) * CHUNK, :] = jnp.where(row >= col, dws_ref[g], 0.0)
            by_token = jnp.transpose(dbs_ref[:, g * 128:(g + 1) * 128])
            b_ref[ROW_BS + g:ROW_BS + g + 1, :] = jnp.sum(by_token, axis=0, keepdims=True)
        b_ref[ROW_SINK:ROW_SINK + 1, :] = dsink_ref[...]
        bk = bk_ref[...]
        rel_row = lax.broadcasted_iota(jnp.int32, (8, 128), 0)
        rel_col = lax.broadcasted_iota(jnp.int32, (8, 128), 1)
        rel = jnp.zeros((8, 128), F32)
        for h in range(4):
            acc = drel_ref[h * CHUNK:(h + 1) * CHUNK, :]
            for b in range(N_BUCKETS):
                rel = jnp.where((rel_row == h) & (rel_col == b), jnp.sum(jnp.where(bk == b, acc, 0.0)), rel)
        b_ref[ROW_REL:ROW_REL + 8, :] = rel

    return pl.pallas_call(
        body, name="pack_small_grads",
        out_shape=[jax.ShapeDtypeStruct((SMALL_A_ROWS, D_MODEL), F32), jax.ShapeDtypeStruct((SMALL_B_ROWS, 128), F32)],
        in_specs=[VMEM_SPEC] * 11, out_specs=[VMEM_SPEC] * 2,
    )(dgpre, dgpost, dgmem, dvg, dvb, dws, dbs, dsink, drel, loss_vec, buckets)


def _adamw(w, g, m, v):
    m2 = ADAM_B1 * m + (1.0 - ADAM_B1) * g
    v2 = ADAM_B2 * v + (1.0 - ADAM_B2) * (g * g)
    m_hat = m2 / (1.0 - ADAM_B1 ** ADAM_STEP)
    v_hat = v2 / (1.0 - ADAM_B2 ** ADAM_STEP)
    delta = -ADAM_LR * (m_hat / (jnp.sqrt(v_hat) + ADAM_EPS) + ADAM_WD * w)
    return delta, m2, v2


ADAM_MAX_ROWS = 176


def _adamw_whole(g, w, m, v, name):
    rows, cols = w.shape
    steps = -(-rows // ADAM_MAX_ROWS)
    block_rows = rows // steps
    assert block_rows * steps == rows and block_rows % 8 == 0

    def body(g_ref, w_ref, m_ref, v_ref, d_out, m_out, v_out):
        delta, m2, v2 = _adamw(w_ref[...], g_ref[...], m_ref[...], v_ref[...])
        d_out[...] = delta
        m_out[...] = m2
        v_out[...] = v2

    block = pl.BlockSpec((block_rows, cols), lambda k: (k, 0))
    out = pl.pallas_call(
        body, name=name, grid=(steps,), out_shape=[jax.ShapeDtypeStruct((rows, cols), F32)] * 3,
        in_specs=[block] * 4, out_specs=[block] * 3,
    )(g, w, m, v)
    return [g] + list(out)


def _adamw_small(ra, rb, weights, moments_m, moments_v):
    n = len(weights)

    def body(*refs):
        ra_ref, rb_ref = refs[0], refs[1]
        w_refs, m_refs, v_refs = refs[2:2 + n], refs[2 + n:2 + 2 * n], refs[2 + 2 * n:2 + 3 * n]
        outs = refs[2 + 3 * n:]
        g_outs, d_outs, m_outs, v_outs = outs[:n], outs[n:2 * n], outs[2 * n:3 * n], outs[3 * n:4 * n]
        ga, gb = ra_ref[0], rb_ref[0]
        for chip in range(1, N_CHIPS):
            ga = ga + ra_ref[chip]
            gb = gb + rb_ref[chip]
        outs[4 * n][...] = ga[ROW_LOSS:ROW_LOSS + 1, 0:128]
        grads = [ga[0:1, :], ga[1:2, :], ga[2:3, :], ga[3:4, :A_WIDTH], ga[3:4, A_WIDTH:],
                 gb[ROW_WS:ROW_WS + A_GROUPS * CHUNK, :].reshape(A_GROUPS, CHUNK, CHUNK),
                 gb[ROW_BS:ROW_BS + A_GROUPS, :], gb[ROW_SINK:ROW_SINK + 1, 0:4],
                 gb[ROW_REL:ROW_REL + 4, 0:N_BUCKETS]]
        for k in range(n):
            delta, m2, v2 = _adamw(w_refs[k][...], grads[k], m_refs[k][...], v_refs[k][...])
            g_outs[k][...] = grads[k]
            d_outs[k][...] = delta
            m_outs[k][...] = m2
            v_outs[k][...] = v2

    out_shape = [jax.ShapeDtypeStruct(w.shape, F32) for w in weights] * 4 + [jax.ShapeDtypeStruct((1, 128), F32)]
    return pl.pallas_call(
        body, name="adamw_small", out_shape=out_shape,
        in_specs=[VMEM_SPEC] * (2 + 3 * n), out_specs=[VMEM_SPEC] * (4 * n + 1),
    )(ra, rb, *weights, *moments_m, *moments_v)


def kernel(x, mem, pre_norm_g, post_norm_g, mem_norm_g, w_in, w_mem_kv, v_norm_g, v_norm_b, w_spatial, b_spatial, attn_sinks, rel_bias, w_out, loss_target, m_pre_norm_g, m_post_norm_g, m_mem_norm_g, m_w_in, m_w_mem_kv, m_v_norm_g, m_v_norm_b, m_w_spatial, m_b_spatial, m_attn_sinks, m_rel_bias, m_w_out, v_pre_norm_g, v_post_norm_g, v_mem_norm_g, v_w_in, v_w_mem_kv, v_v_norm_g, v_v_norm_b, v_w_spatial, v_b_spatial, v_attn_sinks, v_rel_bias, v_w_out):
    n_ex, seq, _ = x.shape
    n_tok = n_ex * seq
    x2 = x.reshape(n_tok, D_MODEL)
    tgt2 = loss_target.reshape(n_tok, D_MODEL)
    buckets = jnp.asarray(_bucket_map())
    shard_arr = (2 * lax.axis_index("x") + lax.axis_index("y")).astype(jnp.int32).reshape(1)
    w_sp = w_spatial[0]
    b_sp = jnp.broadcast_to(b_spatial[0][:, :, None], (A_GROUPS, CHUNK, CHUNK))
    w_in_t, m_w_in_t, v_w_in_t = (jnp.transpose(a[0]) for a in (w_in, m_w_in, v_w_in))
    rel_t, m_rel_t, v_rel_t = (jnp.transpose(a) for a in (rel_bias, m_rel_bias, v_rel_bias))

    x_arr = lax.axis_index("x").astype(jnp.int32).reshape(1)
    h_b, parts, (w_in_b, g_mkv, g_out), bias = _gather_and_project(
        x2, pre_norm_g, w_in_t, w_mem_kv[0], w_out[0], rel_t, buckets, x_arr)
    w_mkv_b = g_mkv.reshape(D_MODEL, 2 * MEM_WIDTH)
    w_out_b = g_out.reshape(MIX_WIDTH, D_MODEL)

    mkv = _memkv_forward(mem, mem_norm_g, w_mkv_b)
    dout, dproj, dmkv, dwout, dvg, dvb, dws, dbs, dsink, drel, loss_vec, dgpost = _mix(
        parts, mkv, x2, tgt2, v_norm_g, v_norm_b, w_sp, b_sp, attn_sinks, bias, w_out_b, post_norm_g, n_ex, seq)

    dx, dgpre = _backward_projection(x2, dout, dproj, pre_norm_g, w_in_b)
    dwmkv, dgmem = _memkv_backward(mem, dmkv, mem_norm_g, w_mkv_b)
    small_a, small_b = _pack_small_grads(dgpre, dgpost, dgmem, dvg, dvb, dws, dbs, dsink, drel, loss_vec, buckets)

    shard_shapes = [w_mem_kv.shape[1:], w_out.shape[1:]]
    big = [g.reshape(N_CHIPS, 2, s[0] // 2, s[1]) for g, s in zip((dwmkv, dwout), shard_shapes)]
    (g_win, g_wmkv, g_wout), (ga, gb) = _reduce_gradients(dproj, h_b, big, [small_a, small_b], shard_arr)

    big_out = [_adamw_whole(g_win, w_in_t, m_w_in_t, v_w_in_t, "adamw_w_in"),
               _adamw_whole(g_wmkv, w_mem_kv[0], m_w_mem_kv[0], v_w_mem_kv[0], "adamw_w_mem_kv"),
               _adamw_whole(g_wout, w_out[0], m_w_out[0], v_w_out[0], "adamw_w_out")]
    small_w = [pre_norm_g, post_norm_g, mem_norm_g, v_norm_g, v_norm_b, w_sp, b_spatial[0], attn_sinks, rel_t]
    small_m = [m_pre_norm_g, m_post_norm_g, m_mem_norm_g, m_v_norm_g, m_v_norm_b, m_w_spatial[0], m_b_spatial[0],
               m_attn_sinks, m_rel_t]
    small_v = [v_pre_norm_g, v_post_norm_g, v_mem_norm_g, v_v_norm_g, v_v_norm_b, v_w_spatial[0], v_b_spatial[0],
               v_attn_sinks, v_rel_t]
    small_out = _adamw_small(ga, gb, small_w, small_m, small_v)
    n_small = len(small_w)

    outputs = [small_out[4 * n_small][0, 0], dx.reshape(x.shape)]
    for kind in range(4):
        s = small_out[kind * n_small:(kind + 1) * n_small]
        outputs += [s[0], s[1], s[2], jnp.transpose(big_out[0][kind])[None], big_out[1][kind][None], s[3], s[4],
                    s[5][None], s[6][None], s[7], jnp.transpose(s[8]), big_out[2][kind][None]]
    return tuple(outputs)
```

```python
import functools

import numpy as np
import jax
import jax.numpy as jnp
from jax import lax
from jax.experimental import pallas as pl
from jax.experimental.pallas import tpu as pltpu

F32 = jnp.float32
BF16 = jnp.bfloat16
MESH = pl.DeviceIdType.MESH

D_MODEL = 1024
CHUNK = 128
A_WIDTH = 512
A_GROUPS = 4
SWA_WIDTH = 256
KV_WIDTH = 128
MEM_WIDTH = 256
MEM_LEN = 256
MIX_WIDTH = 1024
IN_WIDTH = 2816
N_BUCKETS = 32
MAX_DISTANCE = 128
EPS = 1e-6
NEG = -1e30
QK_SCALE = 0.125
HALF_HEAD_PAIR = 64

ADAM_LR = 0.001
ADAM_B1 = 0.9
ADAM_B2 = 0.999
ADAM_EPS = 1e-08
ADAM_WD = 0.01
ADAM_STEP = 10

N_CHIPS = 4
TILE_CHUNKS = 2
TILE = TILE_CHUNKS * CHUNK
PROJ_TILE = 512
VMEM_LIMIT = 56 * 1024 * 1024

SMALL_A_ROWS = 8
ROW_LOSS = 4
ROW_WS = 0
ROW_BS = 512
ROW_SINK = 520
ROW_REL = 528
SMALL_B_ROWS = 536


def _mm(a, b):
    return lax.dot_general(a, b, (((1,), (0,)), ((), ())), preferred_element_type=F32)


def _mm_nt(a, b):
    return lax.dot_general(a, b, (((1,), (1,)), ((), ())), preferred_element_type=F32)


def _mm_tn(a, b):
    return lax.dot_general(a, b, (((0,), (0,)), ((), ())), preferred_element_type=F32)


def _bucket_map():
    qi = np.arange(CHUNK)[:, None]
    kj = np.arange(2 * CHUNK)[None, :]
    n = np.maximum(qi + CHUNK - kj, 0)
    max_exact = N_BUCKETS // 2
    large = max_exact + (np.log(np.maximum(n, 1) / max_exact) / np.log(MAX_DISTANCE / max_exact)
                         * (N_BUCKETS - max_exact)).astype(np.int32)
    large = np.minimum(large, N_BUCKETS - 1)
    return np.where(n < max_exact, n, large).astype(np.int32)


_GELU_C = 0.7978845608028654
_GELU_A = 0.044715
_GELU_K1 = 2.0 * _GELU_C
_GELU_K2 = 2.0 * _GELU_C * _GELU_A


def _gelu(x):
    x2 = x * x
    s = 1.0 / (1.0 + jnp.exp(x * (-_GELU_K1 - _GELU_K2 * x2)))
    return x * s, (s, x2)


def _gelu_grad(x, saved):
    s, x2 = saved
    return s + x * (s * (1.0 - s)) * (_GELU_K1 + 3.0 * _GELU_K2 * x2)


def _sigmoid(x):
    return 1.0 / (1.0 + jnp.exp(-x))


def _lane_lo(shape):
    return lax.broadcasted_iota(jnp.int32, shape, 1) < HALF_HEAD_PAIR


def _swa_variants(t):
    lo = _lane_lo(t.shape)
    tr = pltpu.roll(t, HALF_HEAD_PAIR, 1)
    zero = jnp.zeros_like(t)
    return (jnp.where(lo, t, zero).astype(BF16), jnp.where(lo, zero, tr).astype(BF16),
            jnp.where(lo, tr, zero).astype(BF16), jnp.where(lo, zero, t).astype(BF16))


def _swa_unvariants(d0, d1, d2, d3):
    lo = _lane_lo(d0.shape)
    zero = jnp.zeros_like(d0)
    rolled = jnp.where(lo, zero, d1) + jnp.where(lo, d2, zero)
    return jnp.where(lo, d0, zero) + jnp.where(lo, zero, d3) + pltpu.roll(rolled, HALF_HEAD_PAIR, 1)


def _mem_variants(t):
    out = []
    for pair in range(2):
        tp = t[:, pair * 128:(pair + 1) * 128]
        lo = _lane_lo(tp.shape)
        zero = jnp.zeros_like(tp)
        out.append(jnp.where(lo, tp, zero).astype(BF16))
        out.append(jnp.where(lo, zero, tp).astype(BF16))
    return out


def _mem_unvariants(d0, d1, d2, d3):
    lo = _lane_lo(d0.shape)
    return jnp.concatenate([jnp.where(lo, d0, d1), jnp.where(lo, d2, d3)], axis=-1)


def _softmax(logits, sinks):
    m = jnp.max(logits, axis=-1, keepdims=True)
    if sinks is not None:
        m = jnp.maximum(m, sinks)
    p = jnp.exp(logits - m)
    den = jnp.sum(p, axis=-1, keepdims=True)
    if sinks is None:
        return p * (1.0 / den), None
    es = jnp.exp(sinks - m)
    inv = 1.0 / (den + es)
    return p * inv, es * inv


def _band_valid(with_prev):
    qi = lax.broadcasted_iota(jnp.int32, (CHUNK, 2 * CHUNK), 0)
    kj = lax.broadcasted_iota(jnp.int32, (CHUNK, 2 * CHUNK), 1)
    in_cur = (kj >= CHUNK) & (kj - CHUNK <= qi)
    if not with_prev:
        return in_cur
    return in_cur | ((kj < CHUNK) & (kj > qi))


def _causal_weights(ws_ref):
    row = lax.broadcasted_iota(jnp.int32, (CHUNK, CHUNK), 0)
    col = lax.broadcasted_iota(jnp.int32, (CHUNK, CHUNK), 1)
    return [jnp.where(row >= col, ws_ref[g], 0.0).astype(BF16) for g in range(A_GROUPS)]


def _rows_to_lanes(a, n):
    return jnp.concatenate([a[c * CHUNK:(c + 1) * CHUNK] for c in range(n)], axis=1)


def _lanes_to_rows(a, n):
    w = a.shape[1] // n
    return jnp.concatenate([a[:, c * w:(c + 1) * w] for c in range(n)], axis=0)


def _stack_heads(pair01, pair23):
    return jnp.concatenate([pair01[:, :256], pair01[:, 256:], pair23[:, :256], pair23[:, 256:]], axis=0)


def _pair_heads(s, r):
    return (jnp.concatenate([s[0:r], s[r:2 * r]], axis=1), jnp.concatenate([s[2 * r:3 * r], s[3 * r:4 * r]], axis=1))


def _pair_operands(variants):
    return (jnp.concatenate(variants[0:2], axis=0), jnp.concatenate(variants[2:4], axis=0))


def _split_pair_grads(d_pairs):
    return d_pairs[0][:256], d_pairs[0][256:], d_pairs[1][:256], d_pairs[1][256:]


def _halves_bf16(a):
    return (a[:, :128].astype(BF16), a[:, 128:].astype(BF16))


def _group_a_forward(au, av, vg, vb, wm, bs_rows):
    gu, tu = _gelu(au)
    gv, tv = _gelu(av)
    ya, res = [], []
    for g in range(A_GROUPS):
        sl = slice(g * 128, (g + 1) * 128)
        xg = gv[:, sl]
        xc = xg - jnp.mean(xg, axis=-1, keepdims=True)
        rstd = lax.rsqrt(jnp.mean(xc * xc, axis=-1, keepdims=True) + EPS)
        xhat = xc * rstd
        vn = _rows_to_lanes((xhat * vg[:, sl] + vb[:, sl]).astype(BF16), TILE_CHUNKS)
        s = _lanes_to_rows(_mm(wm[g], vn), TILE_CHUNKS) + bs_rows[g]
        ya.append(gu[:, sl] * s)
        res.append((xhat, rstd, vn, s))
    return ya, dict(gu=gu, tu=tu, tv=tv, groups=res)


def _attention_probs(qp, k_pairs, bias, sink_col):
    logits = _stack_heads(_mm_nt(qp[0], k_pairs[0]), _mm_nt(qp[1], k_pairs[1]))
    if bias is not None:
        logits = logits + bias
    return _softmax(logits, sink_col)


def _attention_out(p, v_pairs, r):
    pp = _pair_heads(p.astype(BF16), r)
    return jnp.concatenate([_mm(pp[0], v_pairs[0]), _mm(pp[1], v_pairs[1])], axis=-1), pp


def _attention_backward(p, pp, do_pairs, qp, k_pairs, v_pairs, r):
    dp = _stack_heads(_mm_nt(do_pairs[0], v_pairs[0]), _mm_nt(do_pairs[1], v_pairs[1]))
    delta = jnp.sum(p * dp, axis=-1, keepdims=True)
    dl = p * (dp - delta)
    dlp = _pair_heads(dl.astype(BF16), r)
    dq = jnp.concatenate([_mm(dlp[0], k_pairs[0]), _mm(dlp[1], k_pairs[1])], axis=-1)
    dk = (_mm_tn(dlp[0], qp[0]), _mm_tn(dlp[1], qp[1]))
    dv = (_mm_tn(pp[0], do_pairs[0]), _mm_tn(pp[1], do_pairs[1]))
    return dl, delta, dq, dk, dv


def _tile_specs(n_tiles_ex, width):
    return pl.BlockSpec((TILE, width), lambda b, i: (b * n_tiles_ex + jnp.minimum(i, n_tiles_ex - 1), 0))


def _prev_chunk_spec(n_tiles_ex, width):
    def index(b, i):
        chunk = TILE_CHUNKS * jnp.minimum(i, n_tiles_ex - 1)
        return (b * n_tiles_ex * TILE_CHUNKS + jnp.maximum(chunk - 1, 0), 0)
    return pl.BlockSpec((CHUNK, width), index)


def _full_spec(shape):
    zeros = (0,) * len(shape)
    return pl.BlockSpec(shape, lambda *_: zeros)


SMEM_SPEC = pl.BlockSpec(memory_space=pltpu.SMEM)
ANY_SPEC = pl.BlockSpec(memory_space=pl.ANY)
VMEM_SPEC = pl.BlockSpec(memory_space=pltpu.VMEM)


def _fill_bias(rel_ref, bk_ref, out_ref):
    bk = bk_ref[...]
    for h in range(4):
        acc = jnp.zeros((CHUNK, 2 * CHUNK), F32)
        for b in range(N_BUCKETS):
            acc = jnp.where(bk == b, rel_ref[h, b], acc)
        for t, with_prev in enumerate((True, False)):
            out_ref[t, h * CHUNK:(h + 1) * CHUNK, :] = jnp.where(_band_valid(with_prev), acc, NEG)


def _memkv_forward(mem, g_mem, w_mkv):
    n_ex = mem.shape[0]

    def body(mem_ref, g_ref, w_ref, out_ref):
        m = mem_ref[0]
        r = lax.rsqrt(jnp.mean(m * m, axis=-1, keepdims=True) + EPS)
        out_ref[0] = _mm((m * r * g_ref[...]).astype(BF16), w_ref[...])

    return pl.pallas_call(
        body, name="memkv_forward", grid=(n_ex,),
        out_shape=jax.ShapeDtypeStruct((n_ex, MEM_LEN, 2 * MEM_WIDTH), F32),
        in_specs=[pl.BlockSpec((1, MEM_LEN, D_MODEL), lambda b: (b, 0, 0)), _full_spec((1, D_MODEL)),
                  _full_spec((D_MODEL, 2 * MEM_WIDTH))],
        out_specs=pl.BlockSpec((1, MEM_LEN, 2 * MEM_WIDTH), lambda b: (b, 0, 0)),
    )(mem, g_mem, w_mkv)


PROJ_WIDTHS = (A_WIDTH, A_WIDTH, SWA_WIDTH, KV_WIDTH, KV_WIDTH, MEM_WIDTH, MIX_WIDTH)
PROJ_OFFSETS = tuple(int(v) for v in np.cumsum((0,) + PROJ_WIDTHS))


HALF_WIDTH = IN_WIDTH // 2
HALF_PARTS = ((0, 1, 2, 3), (4, 5, 6))


def _gather_and_project(x2, g_pre, w_in_s, w_mkv_s, w_out_s, rel_bias_t, buckets, x_arr):
    n_tok = x2.shape[0]
    n_tiles = n_tok // PROJ_TILE
    last = n_tiles - 1
    shapes = [w_in_s.shape, w_mkv_s.shape, w_out_s.shape]
    n_w = len(shapes)

    def body(x_sref, x_ref, g_ref, win_hbm, wmkv_hbm, wout_hbm, rel_ref, bk_ref, h_ref, *refs):
        part_refs, refs = refs[:len(PROJ_WIDTHS)], refs[len(PROJ_WIDTHS):]
        bias_ref, refs = refs[0], refs[1:]
        gin_hbm, gmkv_hbm, gout_hbm, wg, stage_in, stage_mkv, stage_out, own_mkv, own_out, h_all = refs[:10]
        send_sems, recv_sems, local_sems = refs[10:]
        p, t = pl.program_id(0), pl.program_id(1)
        x, y, c = lax.axis_index("x"), lax.axis_index("y"), lax.axis_index("c")
        me, sibling = (x, y, c), (x, y, 1 - c)
        my_shard = 2 * x + y
        gathered = [wg, gmkv_hbm, gout_hbm]

        def half_rows(w, shard, half):
            rows = shapes[w][0] // 2
            if w == 0:
                return wg.at[pl.ds(pl.multiple_of(shard * shapes[0][0] + half * rows, 16), rows), :]
            return gathered[w].at[shard, pl.ds(half * rows, rows), :]

        def first(w, rel):
            src = half_rows(w, my_shard, c) if w == 0 else (own_mkv, own_out)[w - 1].at[
                pl.ds(c * (shapes[w][0] // 2), shapes[w][0] // 2), :]
            k = 3 * w + rel - 1
            return pltpu.make_async_remote_copy(
                src_ref=src, dst_ref=half_rows(w, my_shard, c), send_sem=send_sems.at[k], recv_sem=recv_sems.at[k],
                device_id=(x ^ (rel >> 1), y ^ (rel & 1), c), device_id_type=MESH)

        def landed(w, rel):
            k = 3 * w + rel - 1
            ref = half_rows(w, my_shard ^ rel, c)
            return pltpu.make_async_remote_copy(src_ref=ref, dst_ref=ref, send_sem=send_sems.at[k],
                                                recv_sem=recv_sems.at[k], device_id=me, device_id_type=MESH)

        def passed(w, rel, half, to):
            k = 9 + 3 * w + rel - 1
            ref = half_rows(w, my_shard ^ rel, half)
            return pltpu.make_async_remote_copy(src_ref=ref, dst_ref=ref, send_sem=send_sems.at[k],
                                                recv_sem=recv_sems.at[k], device_id=to, device_id_type=MESH)

        def pass_on(w, rels):
            for rel in rels:
                landed(w, rel).wait_recv()
                passed(w, rel, c, sibling).start()
            for rel in rels:
                passed(w, rel, 1 - c, me).wait_recv()

        own_stores = [pltpu.make_async_copy(own_mkv, gmkv_hbm.at[my_shard], local_sems.at[3]),
                      pltpu.make_async_copy(own_out, gout_hbm.at[my_shard], local_sems.at[4])]

        @pl.when((p == 0) & (t == 0))
        def _():
            loads = [pltpu.make_async_copy(src, dst, local_sems.at[k]) for k, (src, dst) in enumerate(
                ((win_hbm, stage_in), (wmkv_hbm, stage_mkv), (wout_hbm, stage_out)))]
            for cp in loads:
                cp.start()
            loads[0].wait()
            wg[pl.ds(pl.multiple_of(my_shard * shapes[0][0], 16), shapes[0][0]), :] = stage_in[...].astype(BF16)
            for rel in (1, 2):
                first(0, rel).start()
            loads[1].wait()
            loads[2].wait()
            own_mkv[...] = stage_mkv[...].astype(BF16)
            own_out[...] = stage_out[...].astype(BF16)
            for cp in own_stores:
                cp.start()
            _fill_bias(rel_ref, bk_ref, bias_ref)
            pass_on(0, (1,))
            first(0, 3).start()

        @pl.when((p == 0) & (t == n_tiles // 2))
        def _():
            for w in (1, 2):
                for rel in (1, 2, 3):
                    first(w, rel).start()

        @pl.when((p == 1) & (t == 0))
        def _():
            pass_on(0, (2, 3))

        which_half = p ^ x_sref[0]
        tile_rows = pl.ds(pl.multiple_of(t * PROJ_TILE, PROJ_TILE), PROJ_TILE)

        def project(h):
            proj = _mm_nt(h, wg[pl.ds(pl.multiple_of(which_half * HALF_WIDTH, 16), HALF_WIDTH), :])
            for hh in range(2):
                @pl.when(which_half == hh)
                def _():
                    for k in HALF_PARTS[hh]:
                        lo = PROJ_OFFSETS[k] - hh * HALF_WIDTH
                        part_refs[k][...] = proj[:, lo:lo + PROJ_WIDTHS[k]]

        @pl.when(p == 0)
        def _():
            xv = x_ref[...]
            r = lax.rsqrt(jnp.mean(xv * xv, axis=-1, keepdims=True) + EPS)
            h = (xv * r * g_ref[...]).astype(BF16)
            h_ref[...] = h
            h_all[tile_rows, :] = h
            project(h)

        @pl.when(p == 1)
        def _():
            project(h_all[tile_rows, :])

        @pl.when((p == 1) & (t == last))
        def _():
            store = pltpu.make_async_copy(wg, gin_hbm, local_sems.at[5])
            store.start()
            for w in (1, 2):
                pass_on(w, (1, 2, 3))
            for w in range(n_w):
                for rel in (1, 2, 3):
                    first(w, rel).wait_send()
                    passed(w, rel, c, sibling).wait_send()
            for cp in own_stores:
                cp.wait()
            store.wait()

    def active_in(hh):
        def index(p, t, xs):
            return (jnp.where((p ^ xs[0]) == hh, t, jnp.where(p == 0, 0, last)), 0)
        return index

    part_specs = [pl.BlockSpec((PROJ_TILE, PROJ_WIDTHS[k]), active_in(hh)) for hh in range(2) for k in HALF_PARTS[hh]]
    vmem = pltpu.VMEM
    out = pl.pallas_call(
        body, name="gather_and_project",
        out_shape=[jax.ShapeDtypeStruct((n_tok, D_MODEL), BF16)]
        + [jax.ShapeDtypeStruct((n_tok, w), F32) for w in PROJ_WIDTHS]
        + [jax.ShapeDtypeStruct((2, 4 * CHUNK, 2 * CHUNK), F32)]
        + [jax.ShapeDtypeStruct((N_CHIPS * shapes[0][0], shapes[0][1]), BF16)]
        + [jax.ShapeDtypeStruct((N_CHIPS,) + s, BF16) for s in shapes[1:]],
        grid_spec=pltpu.PrefetchScalarGridSpec(
            num_scalar_prefetch=1, grid=(2, n_tiles),
            in_specs=[pl.BlockSpec((PROJ_TILE, D_MODEL), lambda p, t, xs: (jnp.where(p == 0, t, last), 0)),
                      pl.BlockSpec((1, D_MODEL), lambda p, t, xs: (0, 0)), ANY_SPEC, ANY_SPEC, ANY_SPEC, SMEM_SPEC,
                      pl.BlockSpec(buckets.shape, lambda p, t, xs: (0, 0))],
            out_specs=[pl.BlockSpec((PROJ_TILE, D_MODEL), lambda p, t, xs: (jnp.where(p == 0, t, last), 0))]
            + part_specs + [pl.BlockSpec((2, 4 * CHUNK, 2 * CHUNK), lambda p, t, xs: (0, 0, 0))] + [ANY_SPEC] * 3,
            scratch_shapes=[vmem((N_CHIPS * shapes[0][0], shapes[0][1]), BF16), vmem(shapes[0], F32),
                            vmem(shapes[1], F32), vmem(shapes[2], F32), vmem(shapes[1], BF16), vmem(shapes[2], BF16),
                            vmem((n_tok, D_MODEL), BF16),
                            pltpu.SemaphoreType.DMA((18,)), pltpu.SemaphoreType.DMA((18,)),
                            pltpu.SemaphoreType.DMA((6,))]),
        compiler_params=pltpu.CompilerParams(vmem_limit_bytes=VMEM_LIMIT),
    )(x_arr, x2, g_pre, w_in_s, w_mkv_s, w_out_s, rel_bias_t, buckets)
    n_parts = len(PROJ_WIDTHS)
    return out[0], list(out[1:1 + n_parts]), out[2 + n_parts:], out[1 + n_parts]


def _load_chunk(j, i, sk_ref, sv_ref, skp_ref, svp_ref):
    rows = slice(j * CHUNK, (j + 1) * CHUNK)
    if j == 0:
        k_prev, v_prev, table = skp_ref[...], svp_ref[...], jnp.where(i > 0, 0, 1)
    else:
        prev = slice((j - 1) * CHUNK, j * CHUNK)
        k_prev, v_prev, table = sk_ref[prev, :], sv_ref[prev, :], 0
    k_pairs = _pair_operands(_swa_variants(jnp.concatenate([k_prev, sk_ref[rows, :]], axis=0)))
    v_pairs = _pair_operands(_swa_variants(jnp.concatenate([v_prev, sv_ref[rows, :]], axis=0)))
    return rows, k_pairs, v_pairs, table


def _tile_constants(ws_ref, bs_ref, sink_ref, mkv_ref):
    wm = _causal_weights(ws_ref)
    bs_rows = [jnp.concatenate([bs_ref[g]] * TILE_CHUNKS, axis=0) for g in range(A_GROUPS)]
    sink_col = jnp.max(jnp.concatenate([jnp.full((CHUNK, 128), sink_ref[0, h], F32) for h in range(4)], axis=0),
                       axis=-1, keepdims=True)
    mkv_v = mkv_ref[0]
    mk_pairs = _pair_operands(_mem_variants(mkv_v[:, :MEM_WIDTH]))
    mv_pairs = _pair_operands(_mem_variants(mkv_v[:, MEM_WIDTH:]))
    return wm, bs_rows, sink_col, mk_pairs, mv_pairs


def _mix(parts, mkv, x2, tgt2, v_g, v_b, w_sp, b_sp, sinks, bias, w_out, g_post, n_ex, seq):
    n_tiles_ex = seq // TILE
    n_tok = n_ex * seq
    au, av, sq, sk, sv, mq, z = parts
    col = dict(zip(("au", "av", "sq", "sk", "sv", "mq", "z"),
                   (slice(PROJ_OFFSETS[k], PROJ_OFFSETS[k + 1]) for k in range(len(PROJ_WIDTHS)))))
    before_kv, after_kv = slice(0, col["sk"].start), slice(col["sv"].stop, IN_WIDTH)

    def body(au_ref, av_ref, sq_ref, sk_ref, sv_ref, skp_ref, svp_ref, mq_ref, z_ref, mkv_ref, x_ref, tgt_ref,
             vg_ref, vb_ref, ws_ref, bs_ref, sink_ref, bias_ref, wout_ref, gpost_ref,
             dout_ref, dproj_ref, dmkv_ref, dwout_ref, dvg_ref, dvb_ref, dws_ref, dbs_ref, dsink_ref, drel_ref,
             loss_ref, dgpost_ref, carry_dp, carry_k, carry_v):
        b, i = pl.program_id(0), pl.program_id(1)

        @pl.when((b == 0) & (i == 0))
        def _():
            for ref in (dwout_ref, dvg_ref, dvb_ref, dws_ref, dbs_ref, dsink_ref, drel_ref, loss_ref, dgpost_ref):
                ref[...] = jnp.zeros_like(ref)

        @pl.when(i == 0)
        def _():
            dmkv_ref[...] = jnp.zeros_like(dmkv_ref)
            carry_k[...] = jnp.zeros_like(carry_k)
            carry_v[...] = jnp.zeros_like(carry_v)

        @pl.when(i > 0)
        def _():
            dproj_ref[:, before_kv] = carry_dp[:, before_kv]
            dproj_ref[:, after_kv] = carry_dp[:, after_kv]

        @pl.when(i < n_tiles_ex)
        def _():
            wm, bs_rows, sink_col, mk_pairs, mv_pairs = _tile_constants(ws_ref, bs_ref, sink_ref, mkv_ref)
            vg = vg_ref[...]

            au_v, av_v = au_ref[...], av_ref[...]
            ya, res = _group_a_forward(au_v, av_v, vg, vb_ref[...], wm, bs_rows)
            swa, yb = [], []
            for j in range(TILE_CHUNKS):
                rows, k_pairs, v_pairs, table = _load_chunk(j, i, sk_ref, sv_ref, skp_ref, svp_ref)
                qp = _halves_bf16(sq_ref[rows, :] * QK_SCALE)
                p, ps = _attention_probs(qp, k_pairs, bias_ref[table], sink_col)
                out, pp = _attention_out(p, v_pairs, CHUNK)
                yb.append(out)
                swa.append((rows, k_pairs, v_pairs, qp, p, ps, pp))
            mqp = _halves_bf16(mq_ref[...] * QK_SCALE)
            pm, _ = _attention_probs(mqp, mk_pairs, None, None)
            yc, ppm = _attention_out(pm, mv_pairs, TILE)
            ycat = jnp.concatenate(ya + [jnp.concatenate(yb, axis=0), yc], axis=-1)

            zv = z_ref[...]
            sig = _sigmoid(zv)
            sz = zv * sig
            y_b = (ycat * sz).astype(BF16)
            o = _mm(y_b, wout_ref[...])
            r2 = lax.rsqrt(jnp.mean(o * o, axis=-1, keepdims=True) + EPS)
            nrm = o * r2
            gp = gpost_ref[...]
            diff = x_ref[...] + nrm * gp - tgt_ref[...]
            loss_ref[...] += jnp.sum(diff * diff) * (0.5 / D_MODEL)
            dout = diff * (1.0 / D_MODEL)
            dout_ref[...] = dout
            dgpost_ref[...] += jnp.sum(dout * nrm, axis=0, keepdims=True)
            dn = dout * gp
            do_b = (r2 * (dn - nrm * jnp.mean(dn * nrm, axis=-1, keepdims=True))).astype(BF16)
            dwout_ref[...] += _mm_tn(y_b, do_b)
            dy = _mm_nt(do_b, wout_ref[...])
            carry_dp[:, col["z"]] = (dy * ycat * (sig * (1.0 + zv * (1.0 - sig)))).astype(BF16)
            dyc = dy * sz

            dgu, dgv = [], []
            for g in range(A_GROUPS):
                sl = slice(g * 128, (g + 1) * 128)
                xhat, rstd, vn, s = res["groups"][g]
                dya = dyc[:, sl]
                dgu.append(dya * s)
                ds = dya * res["gu"][:, sl]
                dbs_ref[:, sl] += sum(ds[c * CHUNK:(c + 1) * CHUNK] for c in range(TILE_CHUNKS))
                ds_b = _rows_to_lanes(ds.astype(BF16), TILE_CHUNKS)
                dws_ref[g] += _mm_nt(ds_b, vn)
                dvn = _lanes_to_rows(_mm_tn(wm[g], ds_b), TILE_CHUNKS)
                dvg_ref[:, sl] += jnp.sum(dvn * xhat, axis=0, keepdims=True)
                dvb_ref[:, sl] += jnp.sum(dvn, axis=0, keepdims=True)
                dxh = dvn * vg[:, sl]
                dgv.append(rstd * (dxh - jnp.mean(dxh, axis=-1, keepdims=True)
                                   - xhat * jnp.mean(dxh * xhat, axis=-1, keepdims=True)))
            carry_dp[:, col["au"]] = (jnp.concatenate(dgu, axis=-1) * _gelu_grad(au_v, res["tu"])).astype(BF16)
            carry_dp[:, col["av"]] = (jnp.concatenate(dgv, axis=-1) * _gelu_grad(av_v, res["tv"])).astype(BF16)

            lane4 = lax.broadcasted_iota(jnp.int32, (1, 128), 1)
            dsink_vec = jnp.zeros((1, 128), F32)
            dk_parts, dv_parts = [], []
            for rows, k_pairs, v_pairs, qp, p, ps, pp in swa:
                do_pairs = _halves_bf16(dyc[rows, A_WIDTH:A_WIDTH + SWA_WIDTH])
                dl, delta, dq, dk, dv = _attention_backward(p, pp, do_pairs, qp, k_pairs, v_pairs, CHUNK)
                sink_terms = ps * delta
                for h in range(4):
                    dsink_vec = dsink_vec + jnp.where(lane4 == h, -jnp.sum(sink_terms[h * CHUNK:(h + 1) * CHUNK]), 0.0)
                drel_ref[...] += dl
                carry_dp[rows, col["sq"]] = (dq * QK_SCALE).astype(BF16)
                dk_parts.append(_swa_unvariants(*_split_pair_grads(dk)))
                dv_parts.append(_swa_unvariants(*_split_pair_grads(dv)))
            dsink_ref[...] += dsink_vec

            dc_pairs = _halves_bf16(dyc[:, A_WIDTH + SWA_WIDTH:])
            _, _, dmq, dmk, dmv = _attention_backward(pm, ppm, dc_pairs, mqp, mk_pairs, mv_pairs, TILE)
            carry_dp[:, col["mq"]] = (dmq * QK_SCALE).astype(BF16)
            dmkv_ref[0] += jnp.concatenate([_mem_unvariants(*_split_pair_grads(dmk)),
                                            _mem_unvariants(*_split_pair_grads(dmv))], axis=-1)

            for parts_c, carry, cols in ((dk_parts, carry_k, col["sk"]), (dv_parts, carry_v, col["sv"])):
                @pl.when(i > 0)
                def _():
                    dproj_ref[:, cols] = (carry[...] + jnp.concatenate(
                        [jnp.zeros((TILE - CHUNK, KV_WIDTH), F32), parts_c[0][:CHUNK]], axis=0)).astype(BF16)
                new = [parts_c[0][CHUNK:]]
                for j in range(1, TILE_CHUNKS):
                    new[-1] = new[-1] + parts_c[j][:CHUNK]
                    new.append(parts_c[j][CHUNK:])
                carry[...] = jnp.concatenate(new, axis=0)

        @pl.when(i == n_tiles_ex)
        def _():
            dproj_ref[:, col["sk"]] = carry_k[...].astype(BF16)
            dproj_ref[:, col["sv"]] = carry_v[...].astype(BF16)

    tile = functools.partial(_tile_specs, n_tiles_ex)
    prev = functools.partial(_prev_chunk_spec, n_tiles_ex)
    late = pl.BlockSpec((TILE, IN_WIDTH), lambda b, i: (b * n_tiles_ex + jnp.maximum(i - 1, 0), 0))
    return pl.pallas_call(
        body, name="mix", grid=(n_ex, n_tiles_ex + 1),
        out_shape=[jax.ShapeDtypeStruct((n_tok, D_MODEL), F32), jax.ShapeDtypeStruct((n_tok, IN_WIDTH), BF16),
                   jax.ShapeDtypeStruct((n_ex, MEM_LEN, 2 * MEM_WIDTH), F32),
                   jax.ShapeDtypeStruct((MIX_WIDTH, D_MODEL), F32), jax.ShapeDtypeStruct((1, A_WIDTH), F32),
                   jax.ShapeDtypeStruct((1, A_WIDTH), F32), jax.ShapeDtypeStruct((A_GROUPS, CHUNK, CHUNK), F32),
                   jax.ShapeDtypeStruct((CHUNK, A_WIDTH), F32), jax.ShapeDtypeStruct((1, 128), F32),
                   jax.ShapeDtypeStruct((4 * CHUNK, 2 * CHUNK), F32), jax.ShapeDtypeStruct((1, 128), F32),
                   jax.ShapeDtypeStruct((1, D_MODEL), F32)],
        in_specs=[tile(A_WIDTH), tile(A_WIDTH), tile(SWA_WIDTH), tile(KV_WIDTH), tile(KV_WIDTH),
                  prev(KV_WIDTH), prev(KV_WIDTH), tile(MEM_WIDTH), tile(MIX_WIDTH),
                  pl.BlockSpec((1, MEM_LEN, 2 * MEM_WIDTH), lambda b, i: (b, 0, 0)),
                  tile(D_MODEL), tile(D_MODEL),
                  _full_spec((1, A_WIDTH)), _full_spec((1, A_WIDTH)), _full_spec((A_GROUPS, CHUNK, CHUNK)),
                  _full_spec((A_GROUPS, CHUNK, CHUNK)), SMEM_SPEC, _full_spec((2, 4 * CHUNK, 2 * CHUNK)),
                  _full_spec((MIX_WIDTH, D_MODEL)), _full_spec((1, D_MODEL))],
        out_specs=[tile(D_MODEL), late, pl.BlockSpec((1, MEM_LEN, 2 * MEM_WIDTH), lambda b, i: (b, 0, 0)),
                   _full_spec((MIX_WIDTH, D_MODEL)), _full_spec((1, A_WIDTH)), _full_spec((1, A_WIDTH)),
                   _full_spec((A_GROUPS, CHUNK, CHUNK)), _full_spec((CHUNK, A_WIDTH)), _full_spec((1, 128)),
                   _full_spec((4 * CHUNK, 2 * CHUNK)), _full_spec((1, 128)), _full_spec((1, D_MODEL))],
        scratch_shapes=[pltpu.VMEM((TILE, IN_WIDTH), BF16), pltpu.VMEM((TILE, KV_WIDTH), F32),
                        pltpu.VMEM((TILE, KV_WIDTH), F32)],
        compiler_params=pltpu.CompilerParams(vmem_limit_bytes=VMEM_LIMIT),
    )(au, av, sq, sk, sv, sk, sv, mq, z, mkv, x2, tgt2, v_g, v_b, w_sp, b_sp, sinks, bias, w_out, g_post)


BWD_PROJ_TILE = 512


def _backward_projection(x2, dout, dproj, g_pre, w_in_t):
    n_tok = x2.shape[0]
    n_steps = n_tok // BWD_PROJ_TILE

    def body(x_ref, dout_ref, dp_ref, g_ref, w_hbm, dx_ref, dgpre_ref, w_vmem, sem):
        @pl.when(pl.program_id(0) == 0)
        def _():
            load = pltpu.make_async_copy(w_hbm, w_vmem, sem)
            load.start()
            dgpre_ref[...] = jnp.zeros_like(dgpre_ref)
            load.wait()

        xv = x_ref[...]
        r = lax.rsqrt(jnp.mean(xv * xv, axis=-1, keepdims=True) + EPS)
        xn = xv * r
        dh = _mm(dp_ref[...], w_vmem[...])
        dgpre_ref[...] += jnp.sum(dh * xn, axis=0, keepdims=True)
        dhg = dh * g_ref[...]
        dx_ref[...] = r * (dhg - xn * jnp.mean(dhg * xn, axis=-1, keepdims=True)) + dout_ref[...]

    row = lambda w: pl.BlockSpec((BWD_PROJ_TILE, w), lambda i: (i, 0))
    return pl.pallas_call(
        body, name="backward_projection", grid=(n_steps,),
        out_shape=[jax.ShapeDtypeStruct((n_tok, D_MODEL), F32), jax.ShapeDtypeStruct((1, D_MODEL), F32)],
        in_specs=[row(D_MODEL), row(D_MODEL), row(IN_WIDTH), _full_spec((1, D_MODEL)), ANY_SPEC],
        out_specs=[row(D_MODEL), _full_spec((1, D_MODEL))],
        scratch_shapes=[pltpu.VMEM((IN_WIDTH, D_MODEL), BF16), pltpu.SemaphoreType.DMA],
        input_output_aliases={1: 0},
        compiler_params=pltpu.CompilerParams(vmem_limit_bytes=VMEM_LIMIT),
    )(x2, dout, dproj, g_pre, w_in_t)


SHARD_ROWS = IN_WIDTH // N_CHIPS
SHARD_WINDOW = 768
SHARD_HALF = SHARD_ROWS // 2
DWIN_TILE = 2048
N_REL = N_CHIPS - 1


def _shard_window_start(shard):
    return (shard * SHARD_ROWS // 128) * 128


def _reduce_gradients(dproj, h, big, small, shard_arr):
    n_tok = h.shape[0]
    tile = min(DWIN_TILE, n_tok)
    n_sub = n_tok // tile
    last = N_CHIPS - 1
    n_big, n_small = len(big), len(small)
    big_half = [g.shape[2:] for g in big]
    sem_big_d2d = 2 * N_CHIPS
    sem_big_ici = sem_big_d2d + n_big
    sem_big_swap = sem_big_ici + N_REL * n_big
    sem_small_d2d = sem_big_swap + n_big
    sem_small_ici = sem_small_d2d + n_small
    n_sems = sem_small_ici + N_REL * n_small
    loc_small = n_big
    loc_out_win = loc_small + n_small
    loc_out_big = loc_out_win + 2
    loc_out_small = loc_out_big + 2 * n_big
    n_local = loc_out_small + n_small

    def relation_of_slot(s):
        return (s + 2) % N_REL + 1

    def shard_of_slot(s, my_shard):
        return my_shard ^ jnp.where(s == last, 0, relation_of_slot(s))

    def body(shard_ref, dp_ref, h_hbm, *refs):
        h_vmem, h_sem, refs = refs[-2], refs[-1], refs[:-2]
        big_hbm, refs = refs[:n_big], refs[n_big:]
        small_hbm, refs = refs[:n_small], refs[n_small:]
        out_hbm, refs = refs[0], refs[1:]
        big_out, refs = refs[:n_big], refs[n_big:]
        small_out, refs = refs[:n_small], refs[n_small:]
        part, recv_d2d, send_ici, recv_ici, mine_buf, other_buf = refs[:6]
        refs = refs[6:]
        big_own, big_recv, big_send, big_land, big_mine, big_other = (
            refs[k * n_big:(k + 1) * n_big] for k in range(6))
        refs = refs[6 * n_big:]
        small_own, small_recv, small_all = (refs[k * n_small:(k + 1) * n_small] for k in range(3))
        send_sems, recv_sems, local_sems = refs[3 * n_small:]

        s, t = pl.program_id(0), pl.program_id(1)
        x, y, c = lax.axis_index("x"), lax.axis_index("y"), lax.axis_index("c")
        my_chip = 2 * x + y
        sibling = (x, y, 1 - c)
        my_rows = pl.ds(pl.multiple_of(c * SHARD_HALF, 8), SHARD_HALF)
        other_rows = pl.ds(pl.multiple_of((1 - c) * SHARD_HALF, 8), SHARD_HALF)

        def remote(src, dst, k, to):
            return pltpu.make_async_remote_copy(src_ref=src, dst_ref=dst, send_sem=send_sems.at[k],
                                                recv_sem=recv_sems.at[k], device_id=to, device_id_type=MESH)

        def chip_at(rel):
            return (x ^ (rel >> 1), y ^ (rel & 1), c)

        def to_sibling(k):
            return remote(part.at[k % 2, other_rows, :], recv_d2d.at[k], k, sibling)

        def to_chip(k):
            return remote(send_ici.at[k], recv_ici.at[k], N_CHIPS + k, chip_at(relation_of_slot(k)))

        swap = remote(mine_buf, other_buf, 2 * N_CHIPS - 1, sibling)
        big_load = [pltpu.make_async_copy(big_hbm[w].at[:, pl.ds(c, 1)], big_own[w], local_sems.at[w])
                    for w in range(n_big)]
        big_to_sibling = [remote(big_hbm[w].at[:, pl.ds(1 - c, 1)], big_recv[w], sem_big_d2d + w, sibling)
                          for w in range(n_big)]
        big_to_chip = [[remote(big_send[w].at[k], big_land[w].at[k], sem_big_ici + N_REL * w + k, chip_at(k + 1))
                        for k in range(N_REL)] for w in range(n_big)]
        big_swap = [remote(big_mine[w], big_other[w], sem_big_swap + w, sibling) for w in range(n_big)]
        small_load = [pltpu.make_async_copy(small_hbm[i], small_own[i], local_sems.at[loc_small + i])
                      for i in range(n_small)]
        small_to_sibling = [remote(small_hbm[i], small_recv[i], sem_small_d2d + i, sibling) for i in range(n_small)]
        small_to_chip = [[remote(small_all[i].at[my_chip], small_all[i].at[my_chip],
                                 sem_small_ici + N_REL * i + k, chip_at(k + 1))
                          for k in range(N_REL)] for i in range(n_small)]

        @pl.when((s == 0) & (t == 0))
        def _():
            h_load = pltpu.make_async_copy(h_hbm, h_vmem, h_sem)
            h_load.start()
            for cp in big_load + big_to_sibling + small_load + small_to_sibling:
                cp.start()
            h_load.wait()

        @pl.when((s == 0) & (t == n_sub - 1))
        def _():
            for cp in big_load + small_load:
                cp.wait()
            for cp in big_to_sibling + small_to_sibling:
                cp.wait_recv()
                cp.wait_send()
            for w in range(n_big):
                for k in range(N_REL):
                    shard = my_chip ^ (k + 1)
                    big_send[w][k] = (big_own[w][shard, 0] + big_recv[w][shard, 0]).astype(BF16)
                    big_to_chip[w][k].start()
            for i in range(n_small):
                small_all[i][my_chip] = small_own[i][...] + small_recv[i][...]
                for k in range(N_REL):
                    small_to_chip[i][k].start()

        @pl.when((s > 0) & (t == jnp.where(s == last, 0, min(1, n_sub - 1))))
        def _():
            k = s - 1
            cp = to_sibling(k)
            cp.wait_recv()
            cp.wait_send()
            send_ici[k] = (part[k % 2, my_rows, :] + recv_d2d[k]).astype(BF16)
            to_chip(k).start()

        def big_rows(w, half):
            rows = big_half[w][0]
            return big_out[w].at[pl.ds(pl.multiple_of(half * rows, 8), rows), :]

        big_store_mine = [pltpu.make_async_copy(big_mine[w], big_rows(w, c), local_sems.at[loc_out_big + 2 * w])
                          for w in range(n_big)]
        big_store_other = [pltpu.make_async_copy(big_other[w], big_rows(w, 1 - c),
                                                 local_sems.at[loc_out_big + 2 * w + 1]) for w in range(n_big)]
        small_store = [pltpu.make_async_copy(small_all[i], small_out[i], local_sems.at[loc_out_small + i])
                       for i in range(n_small)]

        @pl.when((s == last) & (t == 0))
        def _():
            for w in range(n_big):
                total = big_own[w][my_chip, 0] + big_recv[w][my_chip, 0]
                for k in range(N_REL):
                    big_to_chip[w][k].wait_recv()
                    total = total + big_land[w][k].astype(F32)
                big_mine[w][...] = total
                big_swap[w].start()
                big_store_mine[w].start()
            for i in range(n_small):
                for k in range(N_REL):
                    small_to_chip[i][k].wait_recv()
                small_store[i].start()

        r = _mm_tn(dp_ref[...], h_vmem[pl.ds(pl.multiple_of(t * tile, tile), tile), :])
        odd = shard_of_slot(s, shard_ref[0]) % 2
        for parity in range(2):
            rows = r[64 * parity:64 * parity + SHARD_ROWS]

            @pl.when((odd == parity) & (t == 0))
            def _():
                part[s % 2] = rows

            @pl.when((odd == parity) & (t > 0))
            def _():
                part[s % 2] += rows

        @pl.when(t == n_sub - 1)
        def _():
            to_sibling(s).start()

        @pl.when((s == last) & (t == n_sub - 1))
        def _():
            cp = to_sibling(last)
            cp.wait_recv()
            cp.wait_send()
            total = part[last % 2, my_rows, :] + recv_d2d[last]
            for k in range(last):
                to_chip(k).wait_recv()
                total = total + recv_ici[k].astype(F32)
            mine_buf[...] = total
            swap.start()
            out_mine = pltpu.make_async_copy(mine_buf, out_hbm.at[my_rows, :], local_sems.at[0])
            out_mine.start()
            swap.wait_recv()
            out_other = pltpu.make_async_copy(other_buf, out_hbm.at[other_rows, :], local_sems.at[1])
            out_other.start()
            for w in range(n_big):
                big_swap[w].wait_recv()
                big_store_other[w].start()
            stores = [out_mine, out_other] + big_store_mine + big_store_other + small_store
            for k in range(last):
                to_chip(k).wait_send()
            swap.wait_send()
            for w in range(n_big):
                for k in range(N_REL):
                    big_to_chip[w][k].wait_send()
                big_swap[w].wait_send()
            for i in range(n_small):
                for k in range(N_REL):
                    small_to_chip[i][k].wait_send()
            for cp in stores:
                cp.wait()

    half = (SHARD_HALF, D_MODEL)
    vmem = pltpu.VMEM
    scratch = [vmem((2, SHARD_ROWS, D_MODEL), F32), vmem((N_CHIPS,) + half, F32),
               vmem((N_REL,) + half, BF16), vmem((N_REL,) + half, BF16), vmem(half, F32), vmem(half, F32)]
    scratch += [vmem((N_CHIPS, 1) + hs, F32) for hs in big_half] * 2
    scratch += [vmem((N_REL,) + hs, BF16) for hs in big_half] * 2
    scratch += [vmem(hs, F32) for hs in big_half] * 2
    scratch += [vmem(a.shape, F32) for a in small] * 2 + [vmem((N_CHIPS,) + a.shape, F32) for a in small]
    scratch += [pltpu.SemaphoreType.DMA((n_sems,)), pltpu.SemaphoreType.DMA((n_sems,)),
                pltpu.SemaphoreType.DMA((n_local,)), vmem(h.shape, BF16), pltpu.SemaphoreType.DMA]
    n_hbm = n_big + n_small
    out = pl.pallas_call(
        body, name="reduce_gradients",
        out_shape=[jax.ShapeDtypeStruct((SHARD_ROWS, D_MODEL), F32)]
        + [jax.ShapeDtypeStruct((2 * hs[0], hs[1]), F32) for hs in big_half]
        + [jax.ShapeDtypeStruct((N_CHIPS,) + a.shape, F32) for a in small],
        grid_spec=pltpu.PrefetchScalarGridSpec(
            num_scalar_prefetch=1, grid=(N_CHIPS, n_sub),
            in_specs=[pl.BlockSpec((pl.Element(tile), pl.Element(SHARD_WINDOW)),
                                   lambda s, t, m: (t * tile, _shard_window_start(shard_of_slot(s, m[0])))),
                      ANY_SPEC] + [ANY_SPEC] * n_hbm,
            out_specs=[ANY_SPEC] * (1 + n_hbm),
            scratch_shapes=scratch),
        compiler_params=pltpu.CompilerParams(vmem_limit_bytes=VMEM_LIMIT),
    )(shard_arr, dproj, h, *big, *small)
    return out[:1 + n_big], out[1 + n_big:]


def _memkv_backward(mem, dmkv, g_mem, w_mkv):
    n_ex = mem.shape[0]

    def body(mem_ref, d_ref, g_ref, w_ref, dw_ref, dg_ref):
        @pl.when(pl.program_id(0) == 0)
        def _():
            dw_ref[...] = jnp.zeros_like(dw_ref)
            dg_ref[...] = jnp.zeros_like(dg_ref)

        m = mem_ref[0]
        mn = m * lax.rsqrt(jnp.mean(m * m, axis=-1, keepdims=True) + EPS)
        d_b = d_ref[0].astype(BF16)
        dw_ref[...] += _mm_tn((mn * g_ref[...]).astype(BF16), d_b)
        dg_ref[...] += jnp.sum(_mm_nt(d_b, w_ref[...]) * mn, axis=0, keepdims=True)

    return pl.pallas_call(
        body, name="memkv_backward", grid=(n_ex,),
        out_shape=[jax.ShapeDtypeStruct((D_MODEL, 2 * MEM_WIDTH), F32), jax.ShapeDtypeStruct((1, D_MODEL), F32)],
        in_specs=[pl.BlockSpec((1, MEM_LEN, D_MODEL), lambda b: (b, 0, 0)),
                  pl.BlockSpec((1, MEM_LEN, 2 * MEM_WIDTH), lambda b: (b, 0, 0)),
                  _full_spec((1, D_MODEL)), _full_spec((D_MODEL, 2 * MEM_WIDTH))],
        out_specs=[_full_spec((D_MODEL, 2 * MEM_WIDTH)), _full_spec((1, D_MODEL))],
    )(mem, dmkv, g_mem, w_mkv)


def _pack_small_grads(dgpre, dgpost, dgmem, dvg, dvb, dws, dbs, dsink, drel, loss_vec, buckets):
    def body(dgpre_ref, dgpost_ref, dgmem_ref, dvg_ref, dvb_ref, dws_ref, dbs_ref, dsink_ref, drel_ref, loss_ref,
             bk_ref, a_ref, b_ref):
        a_ref[...] = jnp.zeros_like(a_ref)
        b_ref[...] = jnp.zeros_like(b_ref)
        a_ref[0:1, :] = dgpre_ref[...]
        a_ref[1:2, :] = dgpost_ref[...]
        a_ref[2:3, :] = dgmem_ref[...]
        a_ref[3:4, :] = jnp.concatenate([dvg_ref[...], dvb_ref[...]], axis=-1)
        a_ref[ROW_LOSS:ROW_LOSS + 1, 0:128] = loss_ref[...]
        row = lax.broadcasted_iota(jnp.int32, (CHUNK, CHUNK), 0)
        col = lax.broadcasted_iota(jnp.int32, (CHUNK, CHUNK), 1)
        for g in range(A_GROUPS):
            b_ref[ROW_WS + g * CHUNK:ROW_WS + (g + 1) * CHUNK, :] = jnp.where(row >= col, dws_ref[g], 0.0)
            by_token = jnp.transpose(dbs_ref[:, g * 128:(g + 1) * 128])
            b_ref[ROW_BS + g:ROW_BS + g + 1, :] = jnp.sum(by_token, axis=0, keepdims=True)
        b_ref[ROW_SINK:ROW_SINK + 1, :] = dsink_ref[...]
        bk = bk_ref[...]
        rel_row = lax.broadcasted_iota(jnp.int32, (8, 128), 0)
        rel_col = lax.broadcasted_iota(jnp.int32, (8, 128), 1)
        rel = jnp.zeros((8, 128), F32)
        for h in range(4):
            acc = drel_ref[h * CHUNK:(h + 1) * CHUNK, :]
            for b in range(N_BUCKETS):
                rel = jnp.where((rel_row == h) & (rel_col == b), jnp.sum(jnp.where(bk == b, acc, 0.0)), rel)
        b_ref[ROW_REL:ROW_REL + 8, :] = rel

    return pl.pallas_call(
        body, name="pack_small_grads",
        out_shape=[jax.ShapeDtypeStruct((SMALL_A_ROWS, D_MODEL), F32), jax.ShapeDtypeStruct((SMALL_B_ROWS, 128), F32)],
        in_specs=[VMEM_SPEC] * 11, out_specs=[VMEM_SPEC] * 2,
    )(dgpre, dgpost, dgmem, dvg, dvb, dws, dbs, dsink, drel, loss_vec, buckets)


def _adamw(w, g, m, v):
    m2 = ADAM_B1 * m + (1.0 - ADAM_B1) * g
    v2 = ADAM_B2 * v + (1.0 - ADAM_B2) * (g * g)
    m_hat = m2 / (1.0 - ADAM_B1 ** ADAM_STEP)
    v_hat = v2 / (1.0 - ADAM_B2 ** ADAM_STEP)
    delta = -ADAM_LR * (m_hat / (jnp.sqrt(v_hat) + ADAM_EPS) + ADAM_WD * w)
    return delta, m2, v2


ADAM_STEPS = 4


def _adamw_all(shard_grads, shard_w, shard_m, shard_v, ra, rb, small_w, small_m, small_v):
    n_sh, n = len(shard_w), len(small_w)

    def body(*refs):
        sh_in, refs = refs[:4 * n_sh], refs[4 * n_sh:]
        ra_ref, rb_ref, refs = refs[0], refs[1], refs[2:]
        w_refs, m_refs, v_refs, refs = refs[:n], refs[n:2 * n], refs[2 * n:3 * n], refs[3 * n:]
        sh_out, outs = refs[:4 * n_sh], refs[4 * n_sh:]
        for k in range(n_sh):
            g = sh_in[k][...]
            delta, m2, v2 = _adamw(sh_in[n_sh + k][...], g, sh_in[2 * n_sh + k][...], sh_in[3 * n_sh + k][...])
            for ref, val in zip(sh_out[4 * k:4 * k + 4], (g, delta, m2, v2)):
                ref[...] = val

        @pl.when(pl.program_id(0) == 0)
        def _():
            g_outs, d_outs, m_outs, v_outs = outs[:n], outs[n:2 * n], outs[2 * n:3 * n], outs[3 * n:4 * n]
            ga, gb = ra_ref[0], rb_ref[0]
            for chip in range(1, N_CHIPS):
                ga = ga + ra_ref[chip]
                gb = gb + rb_ref[chip]
            outs[4 * n][...] = ga[ROW_LOSS:ROW_LOSS + 1, 0:128]
            grads = [ga[0:1, :], ga[1:2, :], ga[2:3, :], ga[3:4, :A_WIDTH], ga[3:4, A_WIDTH:],
                     gb[ROW_WS:ROW_WS + A_GROUPS * CHUNK, :].reshape(A_GROUPS, CHUNK, CHUNK),
                     gb[ROW_BS:ROW_BS + A_GROUPS, :], gb[ROW_SINK:ROW_SINK + 1, 0:4],
                     gb[ROW_REL:ROW_REL + 4, 0:N_BUCKETS]]
            for k in range(n):
                delta, m2, v2 = _adamw(w_refs[k][...], grads[k], m_refs[k][...], v_refs[k][...])
                g_outs[k][...] = grads[k]
                d_outs[k][...] = delta
                m_outs[k][...] = m2
                v_outs[k][...] = v2

    def rows_block(a):
        assert a.shape[0] % (8 * ADAM_STEPS) == 0
        return pl.BlockSpec((a.shape[0] // ADAM_STEPS, a.shape[1]), lambda i: (i, 0))

    sh_specs = [rows_block(w) for w in shard_w]
    small_in = [ra, rb, *small_w, *small_m, *small_v]
    small_out_shapes = [jax.ShapeDtypeStruct(w.shape, F32) for w in small_w] * 4 + [jax.ShapeDtypeStruct((1, 128), F32)]
    out = pl.pallas_call(
        body, name="adamw_all", grid=(ADAM_STEPS,),
        out_shape=[jax.ShapeDtypeStruct(w.shape, F32) for w in shard_w for _ in range(4)] + small_out_shapes,
        in_specs=sh_specs * 4 + [_full_spec(a.shape) for a in small_in],
        out_specs=[spec for spec in sh_specs for _ in range(4)] + [_full_spec(s.shape) for s in small_out_shapes],
        compiler_params=pltpu.CompilerParams(vmem_limit_bytes=VMEM_LIMIT),
    )(*shard_grads, *shard_w, *shard_m, *shard_v, *small_in)
    return [out[4 * k:4 * k + 4] for k in range(n_sh)], out[4 * n_sh:]


def kernel(x, mem, pre_norm_g, post_norm_g, mem_norm_g, w_in, w_mem_kv, v_norm_g, v_norm_b, w_spatial, b_spatial, attn_sinks, rel_bias, w_out, loss_target, m_pre_norm_g, m_post_norm_g, m_mem_norm_g, m_w_in, m_w_mem_kv, m_v_norm_g, m_v_norm_b, m_w_spatial, m_b_spatial, m_attn_sinks, m_rel_bias, m_w_out, v_pre_norm_g, v_post_norm_g, v_mem_norm_g, v_w_in, v_w_mem_kv, v_v_norm_g, v_v_norm_b, v_w_spatial, v_b_spatial, v_attn_sinks, v_rel_bias, v_w_out):
    n_ex, seq, _ = x.shape
    n_tok = n_ex * seq
    x2 = x.reshape(n_tok, D_MODEL)
    tgt2 = loss_target.reshape(n_tok, D_MODEL)
    buckets = jnp.asarray(_bucket_map())
    shard_arr = (2 * lax.axis_index("x") + lax.axis_index("y")).astype(jnp.int32).reshape(1)
    w_sp = w_spatial[0]
    b_sp = jnp.broadcast_to(b_spatial[0][:, :, None], (A_GROUPS, CHUNK, CHUNK))
    w_in_t, m_w_in_t, v_w_in_t = (jnp.transpose(a[0]) for a in (w_in, m_w_in, v_w_in))
    rel_t, m_rel_t, v_rel_t = (jnp.transpose(a) for a in (rel_bias, m_rel_bias, v_rel_bias))

    x_arr = lax.axis_index("x").astype(jnp.int32).reshape(1)
    h_b, parts, (w_in_b, g_mkv, g_out), bias = _gather_and_project(
        x2, pre_norm_g, w_in_t, w_mem_kv[0], w_out[0], rel_t, buckets, x_arr)
    w_mkv_b = g_mkv.reshape(D_MODEL, 2 * MEM_WIDTH)
    w_out_b = g_out.reshape(MIX_WIDTH, D_MODEL)

    mkv = _memkv_forward(mem, mem_norm_g, w_mkv_b)
    dout, dproj, dmkv, dwout, dvg, dvb, dws, dbs, dsink, drel, loss_vec, dgpost = _mix(
        parts, mkv, x2, tgt2, v_norm_g, v_norm_b, w_sp, b_sp, attn_sinks, bias, w_out_b, post_norm_g, n_ex, seq)

    dx, dgpre = _backward_projection(x2, dout, dproj, pre_norm_g, w_in_b)
    dwmkv, dgmem = _memkv_backward(mem, dmkv, mem_norm_g, w_mkv_b)
    small_a, small_b = _pack_small_grads(dgpre, dgpost, dgmem, dvg, dvb, dws, dbs, dsink, drel, loss_vec, buckets)

    shard_shapes = [w_mem_kv.shape[1:], w_out.shape[1:]]
    big = [g.reshape(N_CHIPS, 2, s[0] // 2, s[1]) for g, s in zip((dwmkv, dwout), shard_shapes)]
    (g_win, g_wmkv, g_wout), (ga, gb) = _reduce_gradients(dproj, h_b, big, [small_a, small_b], shard_arr)

    small_w = [pre_norm_g, post_norm_g, mem_norm_g, v_norm_g, v_norm_b, w_sp, b_spatial[0], attn_sinks, rel_t]
    small_m = [m_pre_norm_g, m_post_norm_g, m_mem_norm_g, m_v_norm_g, m_v_norm_b, m_w_spatial[0], m_b_spatial[0],
               m_attn_sinks, m_rel_t]
    small_v = [v_pre_norm_g, v_post_norm_g, v_mem_norm_g, v_v_norm_g, v_v_norm_b, v_w_spatial[0], v_b_spatial[0],
               v_attn_sinks, v_rel_t]
    big_out, small_out = _adamw_all(
        [g_win, g_wmkv, g_wout], [w_in_t, w_mem_kv[0], w_out[0]], [m_w_in_t, m_w_mem_kv[0], m_w_out[0]],
        [v_w_in_t, v_w_mem_kv[0], v_w_out[0]], ga, gb, small_w, small_m, small_v)
    n_small = len(small_w)

    outputs = [small_out[4 * n_small][0, 0], dx.reshape(x.shape)]
    for kind in range(4):
        s = small_out[kind * n_small:(kind + 1) * n_small]
        outputs += [s[0], s[1], s[2], jnp.transpose(big_out[0][kind])[None], big_out[1][kind][None], s[3], s[4],
                    s[5][None], s[6][None], s[7], jnp.transpose(s[8]), big_out[2][kind][None]]
    return tuple(outputs)
```

```python
import functools

import numpy as np
import jax
import jax.numpy as jnp
from jax import lax
from jax.experimental import pallas as pl
from jax.experimental.pallas import tpu as pltpu

F32 = jnp.float32
BF16 = jnp.bfloat16
MESH = pl.DeviceIdType.MESH

D_MODEL = 1024
CHUNK = 128
A_WIDTH = 512
A_GROUPS = 4
SWA_WIDTH = 256
KV_WIDTH = 128
MEM_WIDTH = 256
MEM_LEN = 256
MIX_WIDTH = 1024
IN_WIDTH = 2816
N_BUCKETS = 32
MAX_DISTANCE = 128
EPS = 1e-6
NEG = -1e30
QK_SCALE = 0.125
HALF_HEAD_PAIR = 64

ADAM_LR = 0.001
ADAM_B1 = 0.9
ADAM_B2 = 0.999
ADAM_EPS = 1e-08
ADAM_WD = 0.01
ADAM_STEP = 10

N_CHIPS = 4
TILE_CHUNKS = 2
TILE = TILE_CHUNKS * CHUNK
PROJ_TILE = 512
VMEM_LIMIT = 56 * 1024 * 1024

SMALL_A_ROWS = 8
ROW_LOSS = 4
ROW_WS = 0
ROW_BS = 512
ROW_SINK = 520
ROW_REL = 528
SMALL_B_ROWS = 536


def _mm(a, b):
    return lax.dot_general(a, b, (((1,), (0,)), ((), ())), preferred_element_type=F32)


def _mm_nt(a, b):
    return lax.dot_general(a, b, (((1,), (1,)), ((), ())), preferred_element_type=F32)


def _mm_tn(a, b):
    return lax.dot_general(a, b, (((0,), (0,)), ((), ())), preferred_element_type=F32)


def _bucket_map():
    qi = np.arange(CHUNK)[:, None]
    kj = np.arange(2 * CHUNK)[None, :]
    n = np.maximum(qi + CHUNK - kj, 0)
    max_exact = N_BUCKETS // 2
    large = max_exact + (np.log(np.maximum(n, 1) / max_exact) / np.log(MAX_DISTANCE / max_exact)
                         * (N_BUCKETS - max_exact)).astype(np.int32)
    large = np.minimum(large, N_BUCKETS - 1)
    return np.where(n < max_exact, n, large).astype(np.int32)


_GELU_C = 0.7978845608028654
_GELU_A = 0.044715
_GELU_K1 = 2.0 * _GELU_C
_GELU_K2 = 2.0 * _GELU_C * _GELU_A


def _gelu(x):
    x2 = x * x
    s = 1.0 / (1.0 + jnp.exp(x * (-_GELU_K1 - _GELU_K2 * x2)))
    return x * s, (s, x2)


def _gelu_grad(x, saved):
    s, x2 = saved
    return s + x * (s * (1.0 - s)) * (_GELU_K1 + 3.0 * _GELU_K2 * x2)


def _sigmoid(x):
    return 1.0 / (1.0 + jnp.exp(-x))


def _lane_lo(shape):
    return lax.broadcasted_iota(jnp.int32, shape, 1) < HALF_HEAD_PAIR


def _swa_variants(t):
    lo = _lane_lo(t.shape)
    tr = pltpu.roll(t, HALF_HEAD_PAIR, 1)
    zero = jnp.zeros_like(t)
    return (jnp.where(lo, t, zero).astype(BF16), jnp.where(lo, zero, tr).astype(BF16),
            jnp.where(lo, tr, zero).astype(BF16), jnp.where(lo, zero, t).astype(BF16))


def _swa_unvariants(d0, d1, d2, d3):
    lo = _lane_lo(d0.shape)
    zero = jnp.zeros_like(d0)
    rolled = jnp.where(lo, zero, d1) + jnp.where(lo, d2, zero)
    return jnp.where(lo, d0, zero) + jnp.where(lo, zero, d3) + pltpu.roll(rolled, HALF_HEAD_PAIR, 1)


def _mem_variants(t):
    out = []
    for pair in range(2):
        tp = t[:, pair * 128:(pair + 1) * 128]
        lo = _lane_lo(tp.shape)
        zero = jnp.zeros_like(tp)
        out.append(jnp.where(lo, tp, zero).astype(BF16))
        out.append(jnp.where(lo, zero, tp).astype(BF16))
    return out


def _mem_unvariants(d0, d1, d2, d3):
    lo = _lane_lo(d0.shape)
    return jnp.concatenate([jnp.where(lo, d0, d1), jnp.where(lo, d2, d3)], axis=-1)


def _softmax(logits, sinks):
    m = jnp.max(logits, axis=-1, keepdims=True)
    if sinks is not None:
        m = jnp.maximum(m, sinks)
    p = jnp.exp(logits - m)
    den = jnp.sum(p, axis=-1, keepdims=True)
    if sinks is None:
        return p * (1.0 / den), None
    es = jnp.exp(sinks - m)
    inv = 1.0 / (den + es)
    return p * inv, es * inv


def _band_valid(with_prev):
    qi = lax.broadcasted_iota(jnp.int32, (CHUNK, 2 * CHUNK), 0)
    kj = lax.broadcasted_iota(jnp.int32, (CHUNK, 2 * CHUNK), 1)
    in_cur = (kj >= CHUNK) & (kj - CHUNK <= qi)
    if not with_prev:
        return in_cur
    return in_cur | ((kj < CHUNK) & (kj > qi))


def _causal_weights(ws_ref):
    row = lax.broadcasted_iota(jnp.int32, (CHUNK, CHUNK), 0)
    col = lax.broadcasted_iota(jnp.int32, (CHUNK, CHUNK), 1)
    return [jnp.where(row >= col, ws_ref[g], 0.0).astype(BF16) for g in range(A_GROUPS)]


def _rows_to_lanes(a, n):
    return jnp.concatenate([a[c * CHUNK:(c + 1) * CHUNK] for c in range(n)], axis=1)


def _lanes_to_rows(a, n):
    w = a.shape[1] // n
    return jnp.concatenate([a[:, c * w:(c + 1) * w] for c in range(n)], axis=0)


def _stack_heads(pair01, pair23):
    return jnp.concatenate([pair01[:, :256], pair01[:, 256:], pair23[:, :256], pair23[:, 256:]], axis=0)


def _pair_heads(s, r):
    return (jnp.concatenate([s[0:r], s[r:2 * r]], axis=1), jnp.concatenate([s[2 * r:3 * r], s[3 * r:4 * r]], axis=1))


def _pair_operands(variants):
    return (jnp.concatenate(variants[0:2], axis=0), jnp.concatenate(variants[2:4], axis=0))


def _split_pair_grads(d_pairs):
    return d_pairs[0][:256], d_pairs[0][256:], d_pairs[1][:256], d_pairs[1][256:]


def _halves_bf16(a):
    return (a[:, :128].astype(BF16), a[:, 128:].astype(BF16))


def _group_a_forward(au, av, vg, vb, wm, bs_rows):
    gu, tu = _gelu(au)
    gv, tv = _gelu(av)
    ya, res = [], []
    for g in range(A_GROUPS):
        sl = slice(g * 128, (g + 1) * 128)
        xg = gv[:, sl]
        xc = xg - jnp.mean(xg, axis=-1, keepdims=True)
        rstd = lax.rsqrt(jnp.mean(xc * xc, axis=-1, keepdims=True) + EPS)
        xhat = xc * rstd
        vn = _rows_to_lanes((xhat * vg[:, sl] + vb[:, sl]).astype(BF16), TILE_CHUNKS)
        s = _lanes_to_rows(_mm(wm[g], vn), TILE_CHUNKS) + bs_rows[g]
        ya.append(gu[:, sl] * s)
        res.append((xhat, rstd, vn, s))
    return ya, dict(gu=gu, tu=tu, tv=tv, groups=res)


def _attention_probs(qp, k_pairs, bias, sink_col):
    logits = _stack_heads(_mm_nt(qp[0], k_pairs[0]), _mm_nt(qp[1], k_pairs[1]))
    if bias is not None:
        logits = logits + bias
    return _softmax(logits, sink_col)


def _attention_out(p, v_pairs, r):
    pp = _pair_heads(p.astype(BF16), r)
    return jnp.concatenate([_mm(pp[0], v_pairs[0]), _mm(pp[1], v_pairs[1])], axis=-1), pp


def _attention_backward(p, pp, do_pairs, qp, k_pairs, v_pairs, r):
    dp = _stack_heads(_mm_nt(do_pairs[0], v_pairs[0]), _mm_nt(do_pairs[1], v_pairs[1]))
    delta = jnp.sum(p * dp, axis=-1, keepdims=True)
    dl = p * (dp - delta)
    dlp = _pair_heads(dl.astype(BF16), r)
    dq = jnp.concatenate([_mm(dlp[0], k_pairs[0]), _mm(dlp[1], k_pairs[1])], axis=-1)
    dk = (_mm_tn(dlp[0], qp[0]), _mm_tn(dlp[1], qp[1]))
    dv = (_mm_tn(pp[0], do_pairs[0]), _mm_tn(pp[1], do_pairs[1]))
    return dl, delta, dq, dk, dv


def _tile_specs(n_tiles_ex, width):
    return pl.BlockSpec((TILE, width), lambda b, i: (b * n_tiles_ex + jnp.minimum(i, n_tiles_ex - 1), 0))


def _prev_chunk_spec(n_tiles_ex, width):
    def index(b, i):
        chunk = TILE_CHUNKS * jnp.minimum(i, n_tiles_ex - 1)
        return (b * n_tiles_ex * TILE_CHUNKS + jnp.maximum(chunk - 1, 0), 0)
    return pl.BlockSpec((CHUNK, width), index)


def _full_spec(shape):
    zeros = (0,) * len(shape)
    return pl.BlockSpec(shape, lambda *_: zeros)


SMEM_SPEC = pl.BlockSpec(memory_space=pltpu.SMEM)
ANY_SPEC = pl.BlockSpec(memory_space=pl.ANY)
VMEM_SPEC = pl.BlockSpec(memory_space=pltpu.VMEM)


def _fill_bias(rel_ref, bk_ref, out_ref):
    bk = bk_ref[...]
    for h in range(4):
        acc = jnp.zeros((CHUNK, 2 * CHUNK), F32)
        for b in range(N_BUCKETS):
            acc = jnp.where(bk == b, rel_ref[h, b], acc)
        for t, with_prev in enumerate((True, False)):
            out_ref[t, h * CHUNK:(h + 1) * CHUNK, :] = jnp.where(_band_valid(with_prev), acc, NEG)


def _memkv_forward(mem, g_mem, w_mkv):
    n_ex = mem.shape[0]

    def body(mem_ref, g_ref, w_ref, out_ref):
        m = mem_ref[0]
        r = lax.rsqrt(jnp.mean(m * m, axis=-1, keepdims=True) + EPS)
        out_ref[0] = _mm((m * r * g_ref[...]).astype(BF16), w_ref[...])

    return pl.pallas_call(
        body, name="memkv_forward", grid=(n_ex,),
        out_shape=jax.ShapeDtypeStruct((n_ex, MEM_LEN, 2 * MEM_WIDTH), F32),
        in_specs=[pl.BlockSpec((1, MEM_LEN, D_MODEL), lambda b: (b, 0, 0)), _full_spec((1, D_MODEL)),
                  _full_spec((D_MODEL, 2 * MEM_WIDTH))],
        out_specs=pl.BlockSpec((1, MEM_LEN, 2 * MEM_WIDTH), lambda b: (b, 0, 0)),
    )(mem, g_mem, w_mkv)


PROJ_WIDTHS = (A_WIDTH, A_WIDTH, SWA_WIDTH, KV_WIDTH, KV_WIDTH, MEM_WIDTH, MIX_WIDTH)
PROJ_OFFSETS = tuple(int(v) for v in np.cumsum((0,) + PROJ_WIDTHS))


MXU_TILE = 256
HALF_WIDTH = IN_WIDTH // 2
PHASE_COLS = (HALF_WIDTH // MXU_TILE * MXU_TILE, IN_WIDTH - HALF_WIDTH // MXU_TILE * MXU_TILE)


def _phase_columns(phase, chip_x):
    if phase == 0:
        return 0 if chip_x == 0 else IN_WIDTH - PHASE_COLS[0]
    return PHASE_COLS[0] if chip_x == 0 else 0


def _phase_parts(phase, chip_x):
    start = _phase_columns(phase, chip_x)
    return [(k, PROJ_OFFSETS[k] - start) for k in range(len(PROJ_WIDTHS))
            if start <= PROJ_OFFSETS[k] and PROJ_OFFSETS[k + 1] <= start + PHASE_COLS[phase]]


def _gather_and_project(x2, g_pre, w_in_s, w_mkv_s, w_out_s, rel_bias_t, buckets, x_arr):
    n_tok = x2.shape[0]
    n_tiles = n_tok // PROJ_TILE
    last = n_tiles - 1
    shapes = [w_in_s.shape, w_mkv_s.shape, w_out_s.shape]
    n_w = len(shapes)

    def body(x_sref, x_ref, g_ref, win_hbm, wmkv_hbm, wout_hbm, rel_ref, bk_ref, h_ref, *refs):
        part_refs, refs = refs[:len(PROJ_WIDTHS)], refs[len(PROJ_WIDTHS):]
        bias_ref, refs = refs[0], refs[1:]
        gin_hbm, gmkv_hbm, gout_hbm, wg, stage_in, stage_mkv, stage_out, own_mkv, own_out, h_all = refs[:10]
        send_sems, recv_sems, local_sems = refs[10:]
        p, t = pl.program_id(0), pl.program_id(1)
        x, y, c = lax.axis_index("x"), lax.axis_index("y"), lax.axis_index("c")
        me, sibling = (x, y, c), (x, y, 1 - c)
        my_shard = 2 * x + y
        gathered = [wg, gmkv_hbm, gout_hbm]

        def half_rows(w, shard, half):
            rows = shapes[w][0] // 2
            if w == 0:
                return wg.at[pl.ds(pl.multiple_of(shard * shapes[0][0] + half * rows, 16), rows), :]
            return gathered[w].at[shard, pl.ds(half * rows, rows), :]

        def first(w, rel):
            src = half_rows(w, my_shard, c) if w == 0 else (own_mkv, own_out)[w - 1].at[
                pl.ds(c * (shapes[w][0] // 2), shapes[w][0] // 2), :]
            k = 3 * w + rel - 1
            return pltpu.make_async_remote_copy(
                src_ref=src, dst_ref=half_rows(w, my_shard, c), send_sem=send_sems.at[k], recv_sem=recv_sems.at[k],
                device_id=(x ^ (rel >> 1), y ^ (rel & 1), c), device_id_type=MESH)

        def landed(w, rel):
            k = 3 * w + rel - 1
            ref = half_rows(w, my_shard ^ rel, c)
            return pltpu.make_async_remote_copy(src_ref=ref, dst_ref=ref, send_sem=send_sems.at[k],
                                                recv_sem=recv_sems.at[k], device_id=me, device_id_type=MESH)

        def passed(w, rel, half, to):
            k = 9 + 3 * w + rel - 1
            ref = half_rows(w, my_shard ^ rel, half)
            return pltpu.make_async_remote_copy(src_ref=ref, dst_ref=ref, send_sem=send_sems.at[k],
                                                recv_sem=recv_sems.at[k], device_id=to, device_id_type=MESH)

        def pass_on(w, rels):
            for rel in rels:
                landed(w, rel).wait_recv()
                passed(w, rel, c, sibling).start()
            for rel in rels:
                passed(w, rel, 1 - c, me).wait_recv()

        own_stores = [pltpu.make_async_copy(own_mkv, gmkv_hbm.at[my_shard], local_sems.at[3]),
                      pltpu.make_async_copy(own_out, gout_hbm.at[my_shard], local_sems.at[4])]

        @pl.when((p == 0) & (t == 0))
        def _():
            loads = [pltpu.make_async_copy(src, dst, local_sems.at[k]) for k, (src, dst) in enumerate(
                ((win_hbm, stage_in), (wmkv_hbm, stage_mkv), (wout_hbm, stage_out)))]
            for cp in loads:
                cp.start()
            loads[0].wait()
            wg[pl.ds(pl.multiple_of(my_shard * shapes[0][0], 16), shapes[0][0]), :] = stage_in[...].astype(BF16)
            for rel in (1, 2):
                first(0, rel).start()
            loads[1].wait()
            loads[2].wait()
            own_mkv[...] = stage_mkv[...].astype(BF16)
            own_out[...] = stage_out[...].astype(BF16)
            for cp in own_stores:
                cp.start()
            _fill_bias(rel_ref, bk_ref, bias_ref)
            pass_on(0, (1,))
            first(0, 3).start()

        @pl.when((p == 0) & (t == n_tiles // 2))
        def _():
            for w in (1, 2):
                for rel in (1, 2, 3):
                    first(w, rel).start()

        @pl.when((p == 1) & (t == 0))
        def _():
            pass_on(0, (2, 3))

        tile_rows = pl.ds(pl.multiple_of(t * PROJ_TILE, PROJ_TILE), PROJ_TILE)

        def project(h, phase):
            start = jnp.where(x_sref[0] == 0, _phase_columns(phase, 0), _phase_columns(phase, 1))
            proj = _mm_nt(h, wg[pl.ds(pl.multiple_of(start, MXU_TILE), PHASE_COLS[phase]), :])
            for chip_x in range(2):
                @pl.when(x_sref[0] == chip_x)
                def _():
                    for k, lo in _phase_parts(phase, chip_x):
                        part_refs[k][...] = proj[:, lo:lo + PROJ_WIDTHS[k]]

        @pl.when(p == 0)
        def _():
            xv = x_ref[...]
            r = lax.rsqrt(jnp.mean(xv * xv, axis=-1, keepdims=True) + EPS)
            h = (xv * r * g_ref[...]).astype(BF16)
            h_ref[...] = h
            h_all[tile_rows, :] = h
            project(h, 0)

        @pl.when(p == 1)
        def _():
            project(h_all[tile_rows, :], 1)

        @pl.when((p == 1) & (t == last))
        def _():
            store = pltpu.make_async_copy(wg, gin_hbm, local_sems.at[5])
            store.start()
            for w in (1, 2):
                pass_on(w, (1, 2, 3))
            for w in range(n_w):
                for rel in (1, 2, 3):
                    first(w, rel).wait_send()
                    passed(w, rel, c, sibling).wait_send()
            for cp in own_stores:
                cp.wait()
            store.wait()

    def written_in(k):
        phase_on = [next(ph for ph in range(2) if k in dict(_phase_parts(ph, chip_x))) for chip_x in range(2)]

        def index(p, t, xs):
            phase = jnp.where(xs[0] == 0, phase_on[0], phase_on[1])
            return (jnp.where(p == phase, t, jnp.where(p < phase, 0, last)), 0)
        return index

    part_specs = [pl.BlockSpec((PROJ_TILE, PROJ_WIDTHS[k]), written_in(k)) for k in range(len(PROJ_WIDTHS))]
    vmem = pltpu.VMEM
    out = pl.pallas_call(
        body, name="gather_and_project",
        out_shape=[jax.ShapeDtypeStruct((n_tok, D_MODEL), BF16)]
        + [jax.ShapeDtypeStruct((n_tok, w), F32) for w in PROJ_WIDTHS]
        + [jax.ShapeDtypeStruct((2, 4 * CHUNK, 2 * CHUNK), F32)]
        + [jax.ShapeDtypeStruct((N_CHIPS * shapes[0][0], shapes[0][1]), BF16)]
        + [jax.ShapeDtypeStruct((N_CHIPS,) + s, BF16) for s in shapes[1:]],
        grid_spec=pltpu.PrefetchScalarGridSpec(
            num_scalar_prefetch=1, grid=(2, n_tiles),
            in_specs=[pl.BlockSpec((PROJ_TILE, D_MODEL), lambda p, t, xs: (jnp.where(p == 0, t, last), 0)),
                      pl.BlockSpec((1, D_MODEL), lambda p, t, xs: (0, 0)), ANY_SPEC, ANY_SPEC, ANY_SPEC, SMEM_SPEC,
                      pl.BlockSpec(buckets.shape, lambda p, t, xs: (0, 0))],
            out_specs=[pl.BlockSpec((PROJ_TILE, D_MODEL), lambda p, t, xs: (jnp.where(p == 0, t, last), 0))]
            + part_specs + [pl.BlockSpec((2, 4 * CHUNK, 2 * CHUNK), lambda p, t, xs: (0, 0, 0))] + [ANY_SPEC] * 3,
            scratch_shapes=[vmem((N_CHIPS * shapes[0][0], shapes[0][1]), BF16), vmem(shapes[0], F32),
                            vmem(shapes[1], F32), vmem(shapes[2], F32), vmem(shapes[1], BF16), vmem(shapes[2], BF16),
                            vmem((n_tok, D_MODEL), BF16),
                            pltpu.SemaphoreType.DMA((18,)), pltpu.SemaphoreType.DMA((18,)),
                            pltpu.SemaphoreType.DMA((6,))]),
        compiler_params=pltpu.CompilerParams(vmem_limit_bytes=VMEM_LIMIT),
    )(x_arr, x2, g_pre, w_in_s, w_mkv_s, w_out_s, rel_bias_t, buckets)
    n_parts = len(PROJ_WIDTHS)
    return out[0], list(out[1:1 + n_parts]), out[2 + n_parts:], out[1 + n_parts]


def _load_chunk(j, i, sk_ref, sv_ref, skp_ref, svp_ref):
    rows = slice(j * CHUNK, (j + 1) * CHUNK)
    if j == 0:
        k_prev, v_prev, table = skp_ref[...], svp_ref[...], jnp.where(i > 0, 0, 1)
    else:
        prev = slice((j - 1) * CHUNK, j * CHUNK)
        k_prev, v_prev, table = sk_ref[prev, :], sv_ref[prev, :], 0
    k_pairs = _pair_operands(_swa_variants(jnp.concatenate([k_prev, sk_ref[rows, :]], axis=0)))
    v_pairs = _pair_operands(_swa_variants(jnp.concatenate([v_prev, sv_ref[rows, :]], axis=0)))
    return rows, k_pairs, v_pairs, table


def _tile_constants(ws_ref, bs_ref, sink_ref, mkv_ref):
    wm = _causal_weights(ws_ref)
    bs_rows = [jnp.concatenate([bs_ref[g]] * TILE_CHUNKS, axis=0) for g in range(A_GROUPS)]
    sink_col = jnp.max(jnp.concatenate([jnp.full((CHUNK, 128), sink_ref[0, h], F32) for h in range(4)], axis=0),
                       axis=-1, keepdims=True)
    mkv_v = mkv_ref[0]
    mk_pairs = _pair_operands(_mem_variants(mkv_v[:, :MEM_WIDTH]))
    mv_pairs = _pair_operands(_mem_variants(mkv_v[:, MEM_WIDTH:]))
    return wm, bs_rows, sink_col, mk_pairs, mv_pairs


def _mix(parts, mkv, x2, tgt2, v_g, v_b, w_sp, b_sp, sinks, bias, w_out, g_post, n_ex, seq):
    n_tiles_ex = seq // TILE
    n_tok = n_ex * seq
    au, av, sq, sk, sv, mq, z = parts
    col = dict(zip(("au", "av", "sq", "sk", "sv", "mq", "z"),
                   (slice(PROJ_OFFSETS[k], PROJ_OFFSETS[k + 1]) for k in range(len(PROJ_WIDTHS)))))
    before_kv, after_kv = slice(0, col["sk"].start), slice(col["sv"].stop, IN_WIDTH)

    def body(au_ref, av_ref, sq_ref, sk_ref, sv_ref, skp_ref, svp_ref, mq_ref, z_ref, mkv_ref, x_ref, tgt_ref,
             vg_ref, vb_ref, ws_ref, bs_ref, sink_ref, bias_ref, wout_ref, gpost_ref,
             dout_ref, dproj_ref, dmkv_ref, dwout_ref, dvg_ref, dvb_ref, dws_ref, dbs_ref, dsink_ref, drel_ref,
             loss_ref, dgpost_ref, carry_dp, carry_k, carry_v):
        b, i = pl.program_id(0), pl.program_id(1)

        @pl.when((b == 0) & (i == 0))
        def _():
            for ref in (dwout_ref, dvg_ref, dvb_ref, dws_ref, dbs_ref, dsink_ref, drel_ref, loss_ref, dgpost_ref):
                ref[...] = jnp.zeros_like(ref)

        @pl.when(i == 0)
        def _():
            dmkv_ref[...] = jnp.zeros_like(dmkv_ref)
            carry_k[...] = jnp.zeros_like(carry_k)
            carry_v[...] = jnp.zeros_like(carry_v)

        @pl.when(i > 0)
        def _():
            dproj_ref[:, before_kv] = carry_dp[:, before_kv]
            dproj_ref[:, after_kv] = carry_dp[:, after_kv]

        @pl.when(i < n_tiles_ex)
        def _():
            wm, bs_rows, sink_col, mk_pairs, mv_pairs = _tile_constants(ws_ref, bs_ref, sink_ref, mkv_ref)
            vg = vg_ref[...]

            au_v, av_v = au_ref[...], av_ref[...]
            ya, res = _group_a_forward(au_v, av_v, vg, vb_ref[...], wm, bs_rows)
            swa, yb = [], []
            for j in range(TILE_CHUNKS):
                rows, k_pairs, v_pairs, table = _load_chunk(j, i, sk_ref, sv_ref, skp_ref, svp_ref)
                qp = _halves_bf16(sq_ref[rows, :] * QK_SCALE)
                p, ps = _attention_probs(qp, k_pairs, bias_ref[table], sink_col)
                out, pp = _attention_out(p, v_pairs, CHUNK)
                yb.append(out)
                swa.append((rows, k_pairs, v_pairs, qp, p, ps, pp))
            mqp = _halves_bf16(mq_ref[...] * QK_SCALE)
            pm, _ = _attention_probs(mqp, mk_pairs, None, None)
            yc, ppm = _attention_out(pm, mv_pairs, TILE)
            ycat = jnp.concatenate(ya + [jnp.concatenate(yb, axis=0), yc], axis=-1)

            zv = z_ref[...]
            sig = _sigmoid(zv)
            sz = zv * sig
            y_b = (ycat * sz).astype(BF16)
            o = _mm(y_b, wout_ref[...])
            r2 = lax.rsqrt(jnp.mean(o * o, axis=-1, keepdims=True) + EPS)
            nrm = o * r2
            gp = gpost_ref[...]
            diff = x_ref[...] + nrm * gp - tgt_ref[...]
            loss_ref[...] += jnp.sum(diff * diff) * (0.5 / D_MODEL)
            dout = diff * (1.0 / D_MODEL)
            dout_ref[...] = dout
            dgpost_ref[...] += jnp.sum(dout * nrm, axis=0, keepdims=True)
            dn = dout * gp
            do_b = (r2 * (dn - nrm * jnp.mean(dn * nrm, axis=-1, keepdims=True))).astype(BF16)
            dwout_ref[...] += _mm_tn(y_b, do_b)
            dy = _mm_nt(do_b, wout_ref[...])
            carry_dp[:, col["z"]] = (dy * ycat * (sig * (1.0 + zv * (1.0 - sig)))).astype(BF16)
            dyc = dy * sz

            dgu, dgv = [], []
            for g in range(A_GROUPS):
                sl = slice(g * 128, (g + 1) * 128)
                xhat, rstd, vn, s = res["groups"][g]
                dya = dyc[:, sl]
                dgu.append(dya * s)
                ds = dya * res["gu"][:, sl]
                dbs_ref[:, sl] += sum(ds[c * CHUNK:(c + 1) * CHUNK] for c in range(TILE_CHUNKS))
                ds_b = _rows_to_lanes(ds.astype(BF16), TILE_CHUNKS)
                dws_ref[g] += _mm_nt(ds_b, vn)
                dvn = _lanes_to_rows(_mm_tn(wm[g], ds_b), TILE_CHUNKS)
                dvg_ref[:, sl] += jnp.sum(dvn * xhat, axis=0, keepdims=True)
                dvb_ref[:, sl] += jnp.sum(dvn, axis=0, keepdims=True)
                dxh = dvn * vg[:, sl]
                dgv.append(rstd * (dxh - jnp.mean(dxh, axis=-1, keepdims=True)
                                   - xhat * jnp.mean(dxh * xhat, axis=-1, keepdims=True)))
            carry_dp[:, col["au"]] = (jnp.concatenate(dgu, axis=-1) * _gelu_grad(au_v, res["tu"])).astype(BF16)
            carry_dp[:, col["av"]] = (jnp.concatenate(dgv, axis=-1) * _gelu_grad(av_v, res["tv"])).astype(BF16)

            lane4 = lax.broadcasted_iota(jnp.int32, (1, 128), 1)
            dsink_vec = jnp.zeros((1, 128), F32)
            dk_parts, dv_parts = [], []
            for rows, k_pairs, v_pairs, qp, p, ps, pp in swa:
                do_pairs = _halves_bf16(dyc[rows, A_WIDTH:A_WIDTH + SWA_WIDTH])
                dl, delta, dq, dk, dv = _attention_backward(p, pp, do_pairs, qp, k_pairs, v_pairs, CHUNK)
                sink_terms = ps * delta
                for h in range(4):
                    dsink_vec = dsink_vec + jnp.where(lane4 == h, -jnp.sum(sink_terms[h * CHUNK:(h + 1) * CHUNK]), 0.0)
                drel_ref[...] += dl
                carry_dp[rows, col["sq"]] = (dq * QK_SCALE).astype(BF16)
                dk_parts.append(_swa_unvariants(*_split_pair_grads(dk)))
                dv_parts.append(_swa_unvariants(*_split_pair_grads(dv)))
            dsink_ref[...] += dsink_vec

            dc_pairs = _halves_bf16(dyc[:, A_WIDTH + SWA_WIDTH:])
            _, _, dmq, dmk, dmv = _attention_backward(pm, ppm, dc_pairs, mqp, mk_pairs, mv_pairs, TILE)
            carry_dp[:, col["mq"]] = (dmq * QK_SCALE).astype(BF16)
            dmkv_ref[0] += jnp.concatenate([_mem_unvariants(*_split_pair_grads(dmk)),
                                            _mem_unvariants(*_split_pair_grads(dmv))], axis=-1)

            for parts_c, carry, cols in ((dk_parts, carry_k, col["sk"]), (dv_parts, carry_v, col["sv"])):
                @pl.when(i > 0)
                def _():
                    dproj_ref[:, cols] = (carry[...] + jnp.concatenate(
                        [jnp.zeros((TILE - CHUNK, KV_WIDTH), F32), parts_c[0][:CHUNK]], axis=0)).astype(BF16)
                new = [parts_c[0][CHUNK:]]
                for j in range(1, TILE_CHUNKS):
                    new[-1] = new[-1] + parts_c[j][:CHUNK]
                    new.append(parts_c[j][CHUNK:])
                carry[...] = jnp.concatenate(new, axis=0)

        @pl.when(i == n_tiles_ex)
        def _():
            dproj_ref[:, col["sk"]] = carry_k[...].astype(BF16)
            dproj_ref[:, col["sv"]] = carry_v[...].astype(BF16)

    tile = functools.partial(_tile_specs, n_tiles_ex)
    prev = functools.partial(_prev_chunk_spec, n_tiles_ex)
    late = pl.BlockSpec((TILE, IN_WIDTH), lambda b, i: (b * n_tiles_ex + jnp.maximum(i - 1, 0), 0))
    return pl.pallas_call(
        body, name="mix", grid=(n_ex, n_tiles_ex + 1),
        out_shape=[jax.ShapeDtypeStruct((n_tok, D_MODEL), F32), jax.ShapeDtypeStruct((n_tok, IN_WIDTH), BF16),
                   jax.ShapeDtypeStruct((n_ex, MEM_LEN, 2 * MEM_WIDTH), F32),
                   jax.ShapeDtypeStruct((MIX_WIDTH, D_MODEL), F32), jax.ShapeDtypeStruct((1, A_WIDTH), F32),
                   jax.ShapeDtypeStruct((1, A_WIDTH), F32), jax.ShapeDtypeStruct((A_GROUPS, CHUNK, CHUNK), F32),
                   jax.ShapeDtypeStruct((CHUNK, A_WIDTH), F32), jax.ShapeDtypeStruct((1, 128), F32),
                   jax.ShapeDtypeStruct((4 * CHUNK, 2 * CHUNK), F32), jax.ShapeDtypeStruct((1, 128), F32),
                   jax.ShapeDtypeStruct((1, D_MODEL), F32)],
        in_specs=[tile(A_WIDTH), tile(A_WIDTH), tile(SWA_WIDTH), tile(KV_WIDTH), tile(KV_WIDTH),
                  prev(KV_WIDTH), prev(KV_WIDTH), tile(MEM_WIDTH), tile(MIX_WIDTH),
                  pl.BlockSpec((1, MEM_LEN, 2 * MEM_WIDTH), lambda b, i: (b, 0, 0)),
                  tile(D_MODEL), tile(D_MODEL),
                  _full_spec((1, A_WIDTH)), _full_spec((1, A_WIDTH)), _full_spec((A_GROUPS, CHUNK, CHUNK)),
                  _full_spec((A_GROUPS, CHUNK, CHUNK)), SMEM_SPEC, _full_spec((2, 4 * CHUNK, 2 * CHUNK)),
                  _full_spec((MIX_WIDTH, D_MODEL)), _full_spec((1, D_MODEL))],
        out_specs=[tile(D_MODEL), late, pl.BlockSpec((1, MEM_LEN, 2 * MEM_WIDTH), lambda b, i: (b, 0, 0)),
                   _full_spec((MIX_WIDTH, D_MODEL)), _full_spec((1, A_WIDTH)), _full_spec((1, A_WIDTH)),
                   _full_spec((A_GROUPS, CHUNK, CHUNK)), _full_spec((CHUNK, A_WIDTH)), _full_spec((1, 128)),
                   _full_spec((4 * CHUNK, 2 * CHUNK)), _full_spec((1, 128)), _full_spec((1, D_MODEL))],
        scratch_shapes=[pltpu.VMEM((TILE, IN_WIDTH), BF16), pltpu.VMEM((TILE, KV_WIDTH), F32),
                        pltpu.VMEM((TILE, KV_WIDTH), F32)],
        compiler_params=pltpu.CompilerParams(vmem_limit_bytes=VMEM_LIMIT),
    )(au, av, sq, sk, sv, sk, sv, mq, z, mkv, x2, tgt2, v_g, v_b, w_sp, b_sp, sinks, bias, w_out, g_post)


BWD_PROJ_TILE = 512


def _backward_projection(x2, dout, dproj, g_pre, w_in_t):
    n_tok = x2.shape[0]
    n_steps = n_tok // BWD_PROJ_TILE

    def body(x_ref, dout_ref, dp_ref, g_ref, w_hbm, dx_ref, dgpre_ref, w_vmem, sem):
        @pl.when(pl.program_id(0) == 0)
        def _():
            load = pltpu.make_async_copy(w_hbm, w_vmem, sem)
            load.start()
            dgpre_ref[...] = jnp.zeros_like(dgpre_ref)
            load.wait()

        xv = x_ref[...]
        r = lax.rsqrt(jnp.mean(xv * xv, axis=-1, keepdims=True) + EPS)
        xn = xv * r
        dh = _mm(dp_ref[...], w_vmem[...])
        dgpre_ref[...] += jnp.sum(dh * xn, axis=0, keepdims=True)
        dhg = dh * g_ref[...]
        dx_ref[...] = r * (dhg - xn * jnp.mean(dhg * xn, axis=-1, keepdims=True)) + dout_ref[...]

    row = lambda w: pl.BlockSpec((BWD_PROJ_TILE, w), lambda i: (i, 0))
    return pl.pallas_call(
        body, name="backward_projection", grid=(n_steps,),
        out_shape=[jax.ShapeDtypeStruct((n_tok, D_MODEL), F32), jax.ShapeDtypeStruct((1, D_MODEL), F32)],
        in_specs=[row(D_MODEL), row(D_MODEL), row(IN_WIDTH), _full_spec((1, D_MODEL)), ANY_SPEC],
        out_specs=[row(D_MODEL), _full_spec((1, D_MODEL))],
        scratch_shapes=[pltpu.VMEM((IN_WIDTH, D_MODEL), BF16), pltpu.SemaphoreType.DMA],
        input_output_aliases={1: 0},
        compiler_params=pltpu.CompilerParams(vmem_limit_bytes=VMEM_LIMIT),
    )(x2, dout, dproj, g_pre, w_in_t)


SHARD_ROWS = IN_WIDTH // N_CHIPS
SHARD_WINDOW = 768
SHARD_HALF = SHARD_ROWS // 2
DWIN_TILE = 2048
N_REL = N_CHIPS - 1


def _shard_window_start(shard):
    return (shard * SHARD_ROWS // 128) * 128


def _reduce_gradients(dproj, h, big, small, shard_arr):
    n_tok = h.shape[0]
    tile = min(DWIN_TILE, n_tok)
    n_sub = n_tok // tile
    last = N_CHIPS - 1
    n_big, n_small = len(big), len(small)
    big_half = [g.shape[2:] for g in big]
    sem_big_d2d = 2 * N_CHIPS
    sem_big_ici = sem_big_d2d + n_big
    sem_big_swap = sem_big_ici + N_REL * n_big
    sem_small_d2d = sem_big_swap + n_big
    sem_small_ici = sem_small_d2d + n_small
    n_sems = sem_small_ici + N_REL * n_small
    loc_small = n_big
    loc_out_win = loc_small + n_small
    loc_out_big = loc_out_win + 2
    loc_out_small = loc_out_big + 2 * n_big
    n_local = loc_out_small + n_small

    def relation_of_slot(s):
        return (s + 2) % N_REL + 1

    def shard_of_slot(s, my_shard):
        return my_shard ^ jnp.where(s == last, 0, relation_of_slot(s))

    def body(shard_ref, dp_ref, h_hbm, *refs):
        h_vmem, h_sem, refs = refs[-2], refs[-1], refs[:-2]
        big_hbm, refs = refs[:n_big], refs[n_big:]
        small_hbm, refs = refs[:n_small], refs[n_small:]
        out_hbm, refs = refs[0], refs[1:]
        big_out, refs = refs[:n_big], refs[n_big:]
        small_out, refs = refs[:n_small], refs[n_small:]
        part, recv_d2d, send_ici, recv_ici, mine_buf, other_buf = refs[:6]
        refs = refs[6:]
        big_own, big_recv, big_send, big_land, big_mine, big_other = (
            refs[k * n_big:(k + 1) * n_big] for k in range(6))
        refs = refs[6 * n_big:]
        small_own, small_recv, small_all = (refs[k * n_small:(k + 1) * n_small] for k in range(3))
        send_sems, recv_sems, local_sems = refs[3 * n_small:]

        s, t = pl.program_id(0), pl.program_id(1)
        x, y, c = lax.axis_index("x"), lax.axis_index("y"), lax.axis_index("c")
        my_chip = 2 * x + y
        sibling = (x, y, 1 - c)
        my_rows = pl.ds(pl.multiple_of(c * SHARD_HALF, 8), SHARD_HALF)
        other_rows = pl.ds(pl.multiple_of((1 - c) * SHARD_HALF, 8), SHARD_HALF)

        def remote(src, dst, k, to):
            return pltpu.make_async_remote_copy(src_ref=src, dst_ref=dst, send_sem=send_sems.at[k],
                                                recv_sem=recv_sems.at[k], device_id=to, device_id_type=MESH)

        def chip_at(rel):
            return (x ^ (rel >> 1), y ^ (rel & 1), c)

        def to_sibling(k):
            return remote(part.at[k % 2, other_rows, :], recv_d2d.at[k], k, sibling)

        def to_chip(k):
            return remote(send_ici.at[k], recv_ici.at[k], N_CHIPS + k, chip_at(relation_of_slot(k)))

        swap = remote(mine_buf, other_buf, 2 * N_CHIPS - 1, sibling)
        big_load = [pltpu.make_async_copy(big_hbm[w].at[:, pl.ds(c, 1)], big_own[w], local_sems.at[w])
                    for w in range(n_big)]
        big_to_sibling = [remote(big_hbm[w].at[:, pl.ds(1 - c, 1)], big_recv[w], sem_big_d2d + w, sibling)
                          for w in range(n_big)]
        big_to_chip = [[remote(big_send[w].at[k], big_land[w].at[k], sem_big_ici + N_REL * w + k, chip_at(k + 1))
                        for k in range(N_REL)] for w in range(n_big)]
        big_swap = [remote(big_mine[w], big_other[w], sem_big_swap + w, sibling) for w in range(n_big)]
        small_load = [pltpu.make_async_copy(small_hbm[i], small_own[i], local_sems.at[loc_small + i])
                      for i in range(n_small)]
        small_to_sibling = [remote(small_hbm[i], small_recv[i], sem_small_d2d + i, sibling) for i in range(n_small)]
        small_to_chip = [[remote(small_all[i].at[my_chip], small_all[i].at[my_chip],
                                 sem_small_ici + N_REL * i + k, chip_at(k + 1))
                          for k in range(N_REL)] for i in range(n_small)]

        @pl.when((s == 0) & (t == 0))
        def _():
            h_load = pltpu.make_async_copy(h_hbm, h_vmem, h_sem)
            h_load.start()
            for cp in big_load + big_to_sibling + small_load + small_to_sibling:
                cp.start()
            h_load.wait()

        @pl.when((s == 0) & (t == n_sub - 1))
        def _():
            for cp in big_load + small_load:
                cp.wait()
            for cp in big_to_sibling + small_to_sibling:
                cp.wait_recv()
                cp.wait_send()
            for w in range(n_big):
                for k in range(N_REL):
                    shard = my_chip ^ (k + 1)
                    big_send[w][k] = (big_own[w][shard, 0] + big_recv[w][shard, 0]).astype(BF16)
                    big_to_chip[w][k].start()
            for i in range(n_small):
                small_all[i][my_chip] = small_own[i][...] + small_recv[i][...]
                for k in range(N_REL):
                    small_to_chip[i][k].start()

        @pl.when((s > 0) & (t == jnp.where(s == last, 0, min(1, n_sub - 1))))
        def _():
            k = s - 1
            cp = to_sibling(k)
            cp.wait_recv()
            cp.wait_send()
            send_ici[k] = (part[k % 2, my_rows, :] + recv_d2d[k]).astype(BF16)
            to_chip(k).start()

        def big_rows(w, half):
            rows = big_half[w][0]
            return big_out[w].at[pl.ds(pl.multiple_of(half * rows, 8), rows), :]

        big_store_mine = [pltpu.make_async_copy(big_mine[w], big_rows(w, c), local_sems.at[loc_out_big + 2 * w])
                          for w in range(n_big)]
        big_store_other = [pltpu.make_async_copy(big_other[w], big_rows(w, 1 - c),
                                                 local_sems.at[loc_out_big + 2 * w + 1]) for w in range(n_big)]
        small_store = [pltpu.make_async_copy(small_all[i], small_out[i], local_sems.at[loc_out_small + i])
                       for i in range(n_small)]

        @pl.when((s == last) & (t == 0))
        def _():
            for w in range(n_big):
                total = big_own[w][my_chip, 0] + big_recv[w][my_chip, 0]
                for k in range(N_REL):
                    big_to_chip[w][k].wait_recv()
                    total = total + big_land[w][k].astype(F32)
                big_mine[w][...] = total
                big_swap[w].start()
                big_store_mine[w].start()
            for i in range(n_small):
                for k in range(N_REL):
                    small_to_chip[i][k].wait_recv()
                small_store[i].start()

        r = _mm_tn(dp_ref[...], h_vmem[pl.ds(pl.multiple_of(t * tile, tile), tile), :])
        odd = shard_of_slot(s, shard_ref[0]) % 2
        for parity in range(2):
            rows = r[64 * parity:64 * parity + SHARD_ROWS]

            @pl.when((odd == parity) & (t == 0))
            def _():
                part[s % 2] = rows

            @pl.when((odd == parity) & (t > 0))
            def _():
                part[s % 2] += rows

        @pl.when(t == n_sub - 1)
        def _():
            to_sibling(s).start()

        @pl.when((s == last) & (t == n_sub - 1))
        def _():
            cp = to_sibling(last)
            cp.wait_recv()
            cp.wait_send()
            total = part[last % 2, my_rows, :] + recv_d2d[last]
            for k in range(last):
                to_chip(k).wait_recv()
                total = total + recv_ici[k].astype(F32)
            mine_buf[...] = total
            swap.start()
            out_mine = pltpu.make_async_copy(mine_buf, out_hbm.at[my_rows, :], local_sems.at[0])
            out_mine.start()
            swap.wait_recv()
            out_other = pltpu.make_async_copy(other_buf, out_hbm.at[other_rows, :], local_sems.at[1])
            out_other.start()
            for w in range(n_big):
                big_swap[w].wait_recv()
                big_store_other[w].start()
            stores = [out_mine, out_other] + big_store_mine + big_store_other + small_store
            for k in range(last):
                to_chip(k).wait_send()
            swap.wait_send()
            for w in range(n_big):
                for k in range(N_REL):
                    big_to_chip[w][k].wait_send()
                big_swap[w].wait_send()
            for i in range(n_small):
                for k in range(N_REL):
                    small_to_chip[i][k].wait_send()
            for cp in stores:
                cp.wait()

    half = (SHARD_HALF, D_MODEL)
    vmem = pltpu.VMEM
    scratch = [vmem((2, SHARD_ROWS, D_MODEL), F32), vmem((N_CHIPS,) + half, F32),
               vmem((N_REL,) + half, BF16), vmem((N_REL,) + half, BF16), vmem(half, F32), vmem(half, F32)]
    scratch += [vmem((N_CHIPS, 1) + hs, F32) for hs in big_half] * 2
    scratch += [vmem((N_REL,) + hs, BF16) for hs in big_half] * 2
    scratch += [vmem(hs, F32) for hs in big_half] * 2
    scratch += [vmem(a.shape, F32) for a in small] * 2 + [vmem((N_CHIPS,) + a.shape, F32) for a in small]
    scratch += [pltpu.SemaphoreType.DMA((n_sems,)), pltpu.SemaphoreType.DMA((n_sems,)),
                pltpu.SemaphoreType.DMA((n_local,)), vmem(h.shape, BF16), pltpu.SemaphoreType.DMA]
    n_hbm = n_big + n_small
    out = pl.pallas_call(
        body, name="reduce_gradients",
        out_shape=[jax.ShapeDtypeStruct((SHARD_ROWS, D_MODEL), F32)]
        + [jax.ShapeDtypeStruct((2 * hs[0], hs[1]), F32) for hs in big_half]
        + [jax.ShapeDtypeStruct((N_CHIPS,) + a.shape, F32) for a in small],
        grid_spec=pltpu.PrefetchScalarGridSpec(
            num_scalar_prefetch=1, grid=(N_CHIPS, n_sub),
            in_specs=[pl.BlockSpec((pl.Element(tile), pl.Element(SHARD_WINDOW)),
                                   lambda s, t, m: (t * tile, _shard_window_start(shard_of_slot(s, m[0])))),
                      ANY_SPEC] + [ANY_SPEC] * n_hbm,
            out_specs=[ANY_SPEC] * (1 + n_hbm),
            scratch_shapes=scratch),
        compiler_params=pltpu.CompilerParams(vmem_limit_bytes=VMEM_LIMIT),
    )(shard_arr, dproj, h, *big, *small)
    return out[:1 + n_big], out[1 + n_big:]


def _memkv_backward(mem, dmkv, g_mem, w_mkv):
    n_ex = mem.shape[0]

    def body(mem_ref, d_ref, g_ref, w_ref, dw_ref, dg_ref):
        @pl.when(pl.program_id(0) == 0)
        def _():
            dw_ref[...] = jnp.zeros_like(dw_ref)
            dg_ref[...] = jnp.zeros_like(dg_ref)

        m = mem_ref[0]
        mn = m * lax.rsqrt(jnp.mean(m * m, axis=-1, keepdims=True) + EPS)
        d_b = d_ref[0].astype(BF16)
        dw_ref[...] += _mm_tn((mn * g_ref[...]).astype(BF16), d_b)
        dg_ref[...] += jnp.sum(_mm_nt(d_b, w_ref[...]) * mn, axis=0, keepdims=True)

    return pl.pallas_call(
        body, name="memkv_backward", grid=(n_ex,),
        out_shape=[jax.ShapeDtypeStruct((D_MODEL, 2 * MEM_WIDTH), F32), jax.ShapeDtypeStruct((1, D_MODEL), F32)],
        in_specs=[pl.BlockSpec((1, MEM_LEN, D_MODEL), lambda b: (b, 0, 0)),
                  pl.BlockSpec((1, MEM_LEN, 2 * MEM_WIDTH), lambda b: (b, 0, 0)),
                  _full_spec((1, D_MODEL)), _full_spec((D_MODEL, 2 * MEM_WIDTH))],
        out_specs=[_full_spec((D_MODEL, 2 * MEM_WIDTH)), _full_spec((1, D_MODEL))],
    )(mem, dmkv, g_mem, w_mkv)


def _pack_small_grads(dgpre, dgpost, dgmem, dvg, dvb, dws, dbs, dsink, drel, loss_vec, buckets):
    def body(dgpre_ref, dgpost_ref, dgmem_ref, dvg_ref, dvb_ref, dws_ref, dbs_ref, dsink_ref, drel_ref, loss_ref,
             bk_ref, a_ref, b_ref):
        a_ref[...] = jnp.zeros_like(a_ref)
        b_ref[...] = jnp.zeros_like(b_ref)
        a_ref[0:1, :] = dgpre_ref[...]
        a_ref[1:2, :] = dgpost_ref[...]
        a_ref[2:3, :] = dgmem_ref[...]
        a_ref[3:4, :] = jnp.concatenate([dvg_ref[...], dvb_ref[...]], axis=-1)
        a_ref[ROW_LOSS:ROW_LOSS + 1, 0:128] = loss_ref[...]
        row = lax.broadcasted_iota(jnp.int32, (CHUNK, CHUNK), 0)
        col = lax.broadcasted_iota(jnp.int32, (CHUNK, CHUNK), 1)
        for g in range(A_GROUPS):
            b_ref[ROW_WS + g * CHUNK:ROW_WS + (g + 1) * CHUNK, :] = jnp.where(row >= col, dws_ref[g], 0.0)
            by_token = jnp.transpose(dbs_ref[:, g * 128:(g + 1) * 128])
            b_ref[ROW_BS + g:ROW_BS + g + 1, :] = jnp.sum(by_token, axis=0, keepdims=True)
        b_ref[ROW_SINK:ROW_SINK + 1, :] = dsink_ref[...]
        bk = bk_ref[...]
        rel_row = lax.broadcasted_iota(jnp.int32, (8, 128), 0)
        rel_col = lax.broadcasted_iota(jnp.int32, (8, 128), 1)
        rel = jnp.zeros((8, 128), F32)
        for h in range(4):
            acc = drel_ref[h * CHUNK:(h + 1) * CHUNK, :]
            for b in range(N_BUCKETS):
                rel = jnp.where((rel_row == h) & (rel_col == b), jnp.sum(jnp.where(bk == b, acc, 0.0)), rel)
        b_ref[ROW_REL:ROW_REL + 8, :] = rel

    return pl.pallas_call(
        body, name="pack_small_grads",
        out_shape=[jax.ShapeDtypeStruct((SMALL_A_ROWS, D_MODEL), F32), jax.ShapeDtypeStruct((SMALL_B_ROWS, 128), F32)],
        in_specs=[VMEM_SPEC] * 11, out_specs=[VMEM_SPEC] * 2,
    )(dgpre, dgpost, dgmem, dvg, dvb, dws, dbs, dsink, drel, loss_vec, buckets)


def _adamw(w, g, m, v):
    m2 = ADAM_B1 * m + (1.0 - ADAM_B1) * g
    v2 = ADAM_B2 * v + (1.0 - ADAM_B2) * (g * g)
    m_hat = m2 / (1.0 - ADAM_B1 ** ADAM_STEP)
    v_hat = v2 / (1.0 - ADAM_B2 ** ADAM_STEP)
    delta = -ADAM_LR * (m_hat / (jnp.sqrt(v_hat) + ADAM_EPS) + ADAM_WD * w)
    return delta, m2, v2


ADAM_STEPS = 4


def _adamw_all(shard_grads, shard_w, shard_m, shard_v, ra, rb, small_w, small_m, small_v):
    n_sh, n = len(shard_w), len(small_w)

    def body(*refs):
        sh_in, refs = refs[:4 * n_sh], refs[4 * n_sh:]
        ra_ref, rb_ref, refs = refs[0], refs[1], refs[2:]
        w_refs, m_refs, v_refs, refs = refs[:n], refs[n:2 * n], refs[2 * n:3 * n], refs[3 * n:]
        sh_out, outs = refs[:4 * n_sh], refs[4 * n_sh:]
        for k in range(n_sh):
            g = sh_in[k][...]
            delta, m2, v2 = _adamw(sh_in[n_sh + k][...], g, sh_in[2 * n_sh + k][...], sh_in[3 * n_sh + k][...])
            for ref, val in zip(sh_out[4 * k:4 * k + 4], (g, delta, m2, v2)):
                ref[...] = val

        @pl.when(pl.program_id(0) == 0)
        def _():
            g_outs, d_outs, m_outs, v_outs = outs[:n], outs[n:2 * n], outs[2 * n:3 * n], outs[3 * n:4 * n]
            ga, gb = ra_ref[0], rb_ref[0]
            for chip in range(1, N_CHIPS):
                ga = ga + ra_ref[chip]
                gb = gb + rb_ref[chip]
            outs[4 * n][...] = ga[ROW_LOSS:ROW_LOSS + 1, 0:128]
            grads = [ga[0:1, :], ga[1:2, :], ga[2:3, :], ga[3:4, :A_WIDTH], ga[3:4, A_WIDTH:],
                     gb[ROW_WS:ROW_WS + A_GROUPS * CHUNK, :].reshape(A_GROUPS, CHUNK, CHUNK),
                     gb[ROW_BS:ROW_BS + A_GROUPS, :], gb[ROW_SINK:ROW_SINK + 1, 0:4],
                     gb[ROW_REL:ROW_REL + 4, 0:N_BUCKETS]]
            for k in range(n):
                delta, m2, v2 = _adamw(w_refs[k][...], grads[k], m_refs[k][...], v_refs[k][...])
                g_outs[k][...] = grads[k]
                d_outs[k][...] = delta
                m_outs[k][...] = m2
                v_outs[k][...] = v2

    def rows_block(a):
        assert a.shape[0] % (8 * ADAM_STEPS) == 0
        return pl.BlockSpec((a.shape[0] // ADAM_STEPS, a.shape[1]), lambda i: (i, 0))

    sh_specs = [rows_block(w) for w in shard_w]
    small_in = [ra, rb, *small_w, *small_m, *small_v]
    small_out_shapes = [jax.ShapeDtypeStruct(w.shape, F32) for w in small_w] * 4 + [jax.ShapeDtypeStruct((1, 128), F32)]
    out = pl.pallas_call(
        body, name="adamw_all", grid=(ADAM_STEPS,),
        out_shape=[jax.ShapeDtypeStruct(w.shape, F32) for w in shard_w for _ in range(4)] + small_out_shapes,
        in_specs=sh_specs * 4 + [_full_spec(a.shape) for a in small_in],
        out_specs=[spec for spec in sh_specs for _ in range(4)] + [_full_spec(s.shape) for s in small_out_shapes],
        compiler_params=pltpu.CompilerParams(vmem_limit_bytes=VMEM_LIMIT),
    )(*shard_grads, *shard_w, *shard_m, *shard_v, *small_in)
    return [out[4 * k:4 * k + 4] for k in range(n_sh)], out[4 * n_sh:]


def kernel(x, mem, pre_norm_g, post_norm_g, mem_norm_g, w_in, w_mem_kv, v_norm_g, v_norm_b, w_spatial, b_spatial, attn_sinks, rel_bias, w_out, loss_target, m_pre_norm_g, m_post_norm_g, m_mem_norm_g, m_w_in, m_w_mem_kv, m_v_norm_g, m_v_norm_b, m_w_spatial, m_b_spatial, m_attn_sinks, m_rel_bias, m_w_out, v_pre_norm_g, v_post_norm_g, v_mem_norm_g, v_w_in, v_w_mem_kv, v_v_norm_g, v_v_norm_b, v_w_spatial, v_b_spatial, v_attn_sinks, v_rel_bias, v_w_out):
    n_ex, seq, _ = x.shape
    n_tok = n_ex * seq
    x2 = x.reshape(n_tok, D_MODEL)
    tgt2 = loss_target.reshape(n_tok, D_MODEL)
    buckets = jnp.asarray(_bucket_map())
    shard_arr = (2 * lax.axis_index("x") + lax.axis_index("y")).astype(jnp.int32).reshape(1)
    w_sp = w_spatial[0]
    b_sp = jnp.broadcast_to(b_spatial[0][:, :, None], (A_GROUPS, CHUNK, CHUNK))
    w_in_t, m_w_in_t, v_w_in_t = (jnp.transpose(a[0]) for a in (w_in, m_w_in, v_w_in))
    rel_t, m_rel_t, v_rel_t = (jnp.transpose(a) for a in (rel_bias, m_rel_bias, v_rel_bias))

    x_arr = lax.axis_index("x").astype(jnp.int32).reshape(1)
    h_b, parts, (w_in_b, g_mkv, g_out), bias = _gather_and_project(
        x2, pre_norm_g, w_in_t, w_mem_kv[0], w_out[0], rel_t, buckets, x_arr)
    w_mkv_b = g_mkv.reshape(D_MODEL, 2 * MEM_WIDTH)
    w_out_b = g_out.reshape(MIX_WIDTH, D_MODEL)

    mkv = _memkv_forward(mem, mem_norm_g, w_mkv_b)
    dout, dproj, dmkv, dwout, dvg, dvb, dws, dbs, dsink, drel, loss_vec, dgpost = _mix(
        parts, mkv, x2, tgt2, v_norm_g, v_norm_b, w_sp, b_sp, attn_sinks, bias, w_out_b, post_norm_g, n_ex, seq)

    dx, dgpre = _backward_projection(x2, dout, dproj, pre_norm_g, w_in_b)
    dwmkv, dgmem = _memkv_backward(mem, dmkv, mem_norm_g, w_mkv_b)
    small_a, small_b = _pack_small_grads(dgpre, dgpost, dgmem, dvg, dvb, dws, dbs, dsink, drel, loss_vec, buckets)

    shard_shapes = [w_mem_kv.shape[1:], w_out.shape[1:]]
    big = [g.reshape(N_CHIPS, 2, s[0] // 2, s[1]) for g, s in zip((dwmkv, dwout), shard_shapes)]
    (g_win, g_wmkv, g_wout), (ga, gb) = _reduce_gradients(dproj, h_b, big, [small_a, small_b], shard_arr)

    small_w = [pre_norm_g, post_norm_g, mem_norm_g, v_norm_g, v_norm_b, w_sp, b_spatial[0], attn_sinks, rel_t]
    small_m = [m_pre_norm_g, m_post_norm_g, m_mem_norm_g, m_v_norm_g, m_v_norm_b, m_w_spatial[0], m_b_spatial[0],
               m_attn_sinks, m_rel_t]
    small_v = [v_pre_norm_g, v_post_norm_g, v_mem_norm_g, v_v_norm_g, v_v_norm_b, v_w_spatial[0], v_b_spatial[0],
               v_attn_sinks, v_rel_t]
    big_out, small_out = _adamw_all(
        [g_win, g_wmkv, g_wout], [w_in_t, w_mem_kv[0], w_out[0]], [m_w_in_t, m_w_mem_kv[0], m_w_out[0]],
        [v_w_in_t, v_w_mem_kv[0], v_w_out[0]], ga, gb, small_w, small_m, small_v)
    n_small = len(small_w)

    outputs = [small_out[4 * n_small][0, 0], dx.reshape(x.shape)]
    for kind in range(4):
        s = small_out[kind * n_small:(kind + 1) * n_small]
        outputs += [s[0], s[1], s[2], jnp.transpose(big_out[0][kind])[None], big_out[1][kind][None], s[3], s[4],
                    s[5][None], s[6][None], s[7], jnp.transpose(s[8]), big_out[2][kind][None]]
    return tuple(outputs)
```

```python
import functools

import numpy as np
import jax
import jax.numpy as jnp
from jax import lax
from jax.experimental import pallas as pl
from jax.experimental.pallas import tpu as pltpu

F32 = jnp.float32
BF16 = jnp.bfloat16
MESH = pl.DeviceIdType.MESH

D_MODEL = 1024
CHUNK = 128
A_WIDTH = 512
A_GROUPS = 4
SWA_WIDTH = 256
KV_WIDTH = 128
MEM_WIDTH = 256
MEM_LEN = 256
MIX_WIDTH = 1024
IN_WIDTH = 2816
N_BUCKETS = 32
MAX_DISTANCE = 128
EPS = 1e-6
NEG = -1e30
QK_SCALE = 0.125
HALF_HEAD_PAIR = 64

ADAM_LR = 0.001
ADAM_B1 = 0.9
ADAM_B2 = 0.999
ADAM_EPS = 1e-08
ADAM_WD = 0.01
ADAM_STEP = 10

N_CHIPS = 4
TILE_CHUNKS = 2
TILE = TILE_CHUNKS * CHUNK
PROJ_TILE = 512
VMEM_LIMIT = 56 * 1024 * 1024

SMALL_A_ROWS = 8
ROW_LOSS = 4
ROW_WS = 0
ROW_BS = 512
ROW_SINK = 520
ROW_REL = 528
SMALL_B_ROWS = 536


def _mm(a, b):
    return lax.dot_general(a, b, (((1,), (0,)), ((), ())), preferred_element_type=F32)


def _mm_nt(a, b):
    return lax.dot_general(a, b, (((1,), (1,)), ((), ())), preferred_element_type=F32)


def _mm_tn(a, b):
    return lax.dot_general(a, b, (((0,), (0,)), ((), ())), preferred_element_type=F32)


def _bucket_map():
    qi = np.arange(CHUNK)[:, None]
    kj = np.arange(2 * CHUNK)[None, :]
    n = np.maximum(qi + CHUNK - kj, 0)
    max_exact = N_BUCKETS // 2
    large = max_exact + (np.log(np.maximum(n, 1) / max_exact) / np.log(MAX_DISTANCE / max_exact)
                         * (N_BUCKETS - max_exact)).astype(np.int32)
    large = np.minimum(large, N_BUCKETS - 1)
    return np.where(n < max_exact, n, large).astype(np.int32)


_GELU_C = 0.7978845608028654
_GELU_A = 0.044715
_GELU_K1 = 2.0 * _GELU_C
_GELU_K2 = 2.0 * _GELU_C * _GELU_A


def _gelu(x):
    x2 = x * x
    s = 1.0 / (1.0 + jnp.exp(x * (-_GELU_K1 - _GELU_K2 * x2)))
    return x * s, (s, x2)


def _gelu_grad(x, saved):
    s, x2 = saved
    return s + x * (s * (1.0 - s)) * (_GELU_K1 + 3.0 * _GELU_K2 * x2)


def _sigmoid(x):
    return 1.0 / (1.0 + jnp.exp(-x))


def _lane_lo(shape):
    return lax.broadcasted_iota(jnp.int32, shape, 1) < HALF_HEAD_PAIR


def _swa_variants(t):
    lo = _lane_lo(t.shape)
    tr = pltpu.roll(t, HALF_HEAD_PAIR, 1)
    zero = jnp.zeros_like(t)
    return (jnp.where(lo, t, zero).astype(BF16), jnp.where(lo, zero, tr).astype(BF16),
            jnp.where(lo, tr, zero).astype(BF16), jnp.where(lo, zero, t).astype(BF16))


def _swa_unvariants(d0, d1, d2, d3):
    lo = _lane_lo(d0.shape)
    zero = jnp.zeros_like(d0)
    rolled = jnp.where(lo, zero, d1) + jnp.where(lo, d2, zero)
    return jnp.where(lo, d0, zero) + jnp.where(lo, zero, d3) + pltpu.roll(rolled, HALF_HEAD_PAIR, 1)


def _mem_variants(t):
    out = []
    for pair in range(2):
        tp = t[:, pair * 128:(pair + 1) * 128]
        lo = _lane_lo(tp.shape)
        zero = jnp.zeros_like(tp)
        out.append(jnp.where(lo, tp, zero).astype(BF16))
        out.append(jnp.where(lo, zero, tp).astype(BF16))
    return out


def _mem_unvariants(d0, d1, d2, d3):
    lo = _lane_lo(d0.shape)
    return jnp.concatenate([jnp.where(lo, d0, d1), jnp.where(lo, d2, d3)], axis=-1)


def _softmax(logits, sinks):
    m = jnp.max(logits, axis=-1, keepdims=True)
    if sinks is not None:
        m = jnp.maximum(m, sinks)
    p = jnp.exp(logits - m)
    den = jnp.sum(p, axis=-1, keepdims=True)
    if sinks is None:
        return p * (1.0 / den), None
    es = jnp.exp(sinks - m)
    inv = 1.0 / (den + es)
    return p * inv, es * inv


def _band_valid(with_prev):
    qi = lax.broadcasted_iota(jnp.int32, (CHUNK, 2 * CHUNK), 0)
    kj = lax.broadcasted_iota(jnp.int32, (CHUNK, 2 * CHUNK), 1)
    in_cur = (kj >= CHUNK) & (kj - CHUNK <= qi)
    if not with_prev:
        return in_cur
    return in_cur | ((kj < CHUNK) & (kj > qi))


def _causal_weights(ws_ref):
    row = lax.broadcasted_iota(jnp.int32, (CHUNK, CHUNK), 0)
    col = lax.broadcasted_iota(jnp.int32, (CHUNK, CHUNK), 1)
    return [jnp.where(row >= col, ws_ref[g], 0.0).astype(BF16) for g in range(A_GROUPS)]


def _rows_to_lanes(a, n):
    return jnp.concatenate([a[c * CHUNK:(c + 1) * CHUNK] for c in range(n)], axis=1)


def _lanes_to_rows(a, n):
    w = a.shape[1] // n
    return jnp.concatenate([a[:, c * w:(c + 1) * w] for c in range(n)], axis=0)


def _stack_heads(pair01, pair23):
    return jnp.concatenate([pair01[:, :256], pair01[:, 256:], pair23[:, :256], pair23[:, 256:]], axis=0)


def _pair_heads(s, r):
    return (jnp.concatenate([s[0:r], s[r:2 * r]], axis=1), jnp.concatenate([s[2 * r:3 * r], s[3 * r:4 * r]], axis=1))


def _pair_operands(variants):
    return (jnp.concatenate(variants[0:2], axis=0), jnp.concatenate(variants[2:4], axis=0))


def _split_pair_grads(d_pairs):
    return d_pairs[0][:256], d_pairs[0][256:], d_pairs[1][:256], d_pairs[1][256:]


def _halves_bf16(a):
    return (a[:, :128].astype(BF16), a[:, 128:].astype(BF16))


def _group_a_forward(au, av, vg, vb, wm, bs_rows):
    gu, tu = _gelu(au)
    gv, tv = _gelu(av)
    ya, res = [], []
    for g in range(A_GROUPS):
        sl = slice(g * 128, (g + 1) * 128)
        xg = gv[:, sl]
        xc = xg - jnp.mean(xg, axis=-1, keepdims=True)
        rstd = lax.rsqrt(jnp.mean(xc * xc, axis=-1, keepdims=True) + EPS)
        xhat = xc * rstd
        vn = _rows_to_lanes((xhat * vg[:, sl] + vb[:, sl]).astype(BF16), TILE_CHUNKS)
        s = _lanes_to_rows(_mm(wm[g], vn), TILE_CHUNKS) + bs_rows[g]
        ya.append(gu[:, sl] * s)
        res.append((xhat, rstd, vn, s))
    return ya, dict(gu=gu, tu=tu, tv=tv, groups=res)


def _attention_logits(qp, k_pairs):
    return _stack_heads(_mm_nt(qp[0], k_pairs[0]), _mm_nt(qp[1], k_pairs[1]))


def _attention_out(p, v_pairs, r):
    pp = _pair_heads(p.astype(BF16), r)
    return jnp.concatenate([_mm(pp[0], v_pairs[0]), _mm(pp[1], v_pairs[1])], axis=-1), pp


def _attention_dprobs(do_pairs, v_pairs):
    return _stack_heads(_mm_nt(do_pairs[0], v_pairs[0]), _mm_nt(do_pairs[1], v_pairs[1]))


def _softmax_backward(p, dp):
    delta = jnp.sum(p * dp, axis=-1, keepdims=True)
    return p * (dp - delta), delta


def _attention_grads(dl, pp, do_pairs, qp, k_pairs, r):
    dlp = _pair_heads(dl.astype(BF16), r)
    dq = jnp.concatenate([_mm(dlp[0], k_pairs[0]), _mm(dlp[1], k_pairs[1])], axis=-1)
    dk = (_mm_tn(dlp[0], qp[0]), _mm_tn(dlp[1], qp[1]))
    dv = (_mm_tn(pp[0], do_pairs[0]), _mm_tn(pp[1], do_pairs[1]))
    return dq, dk, dv


def _tile_specs(n_tiles_ex, width):
    return pl.BlockSpec((TILE, width), lambda b, i: (b * n_tiles_ex + jnp.minimum(i, n_tiles_ex - 1), 0))


def _prev_chunk_spec(n_tiles_ex, width):
    def index(b, i):
        chunk = TILE_CHUNKS * jnp.minimum(i, n_tiles_ex - 1)
        return (b * n_tiles_ex * TILE_CHUNKS + jnp.maximum(chunk - 1, 0), 0)
    return pl.BlockSpec((CHUNK, width), index)


def _full_spec(shape):
    zeros = (0,) * len(shape)
    return pl.BlockSpec(shape, lambda *_: zeros)


SMEM_SPEC = pl.BlockSpec(memory_space=pltpu.SMEM)
ANY_SPEC = pl.BlockSpec(memory_space=pl.ANY)
VMEM_SPEC = pl.BlockSpec(memory_space=pltpu.VMEM)


def _fill_bias(rel_ref, bk_ref, out_ref):
    bk = bk_ref[...]
    for h in range(4):
        acc = jnp.zeros((CHUNK, 2 * CHUNK), F32)
        for b in range(N_BUCKETS):
            acc = jnp.where(bk == b, rel_ref[h, b], acc)
        for t, with_prev in enumerate((True, False)):
            out_ref[t, h * CHUNK:(h + 1) * CHUNK, :] = jnp.where(_band_valid(with_prev), acc, NEG)


def _memkv_forward(mem, g_mem, w_mkv):
    n_ex = mem.shape[0]

    def body(mem_ref, g_ref, w_ref, out_ref):
        m = mem_ref[0]
        r = lax.rsqrt(jnp.mean(m * m, axis=-1, keepdims=True) + EPS)
        out_ref[0] = _mm((m * r * g_ref[...]).astype(BF16), w_ref[...])

    return pl.pallas_call(
        body, name="memkv_forward", grid=(n_ex,),
        out_shape=jax.ShapeDtypeStruct((n_ex, MEM_LEN, 2 * MEM_WIDTH), F32),
        in_specs=[pl.BlockSpec((1, MEM_LEN, D_MODEL), lambda b: (b, 0, 0)), _full_spec((1, D_MODEL)),
                  _full_spec((D_MODEL, 2 * MEM_WIDTH))],
        out_specs=pl.BlockSpec((1, MEM_LEN, 2 * MEM_WIDTH), lambda b: (b, 0, 0)),
    )(mem, g_mem, w_mkv)


PROJ_WIDTHS = (A_WIDTH, A_WIDTH, SWA_WIDTH, KV_WIDTH, KV_WIDTH, MEM_WIDTH, MIX_WIDTH)
PROJ_OFFSETS = tuple(int(v) for v in np.cumsum((0,) + PROJ_WIDTHS))


MXU_TILE = 256
HALF_WIDTH = IN_WIDTH // 2
PHASE_COLS = (HALF_WIDTH // MXU_TILE * MXU_TILE, IN_WIDTH - HALF_WIDTH // MXU_TILE * MXU_TILE)


def _phase_columns(phase, chip_x):
    if phase == 0:
        return 0 if chip_x == 0 else IN_WIDTH - PHASE_COLS[0]
    return PHASE_COLS[0] if chip_x == 0 else 0


def _phase_parts(phase, chip_x):
    start = _phase_columns(phase, chip_x)
    return [(k, PROJ_OFFSETS[k] - start) for k in range(len(PROJ_WIDTHS))
            if start <= PROJ_OFFSETS[k] and PROJ_OFFSETS[k + 1] <= start + PHASE_COLS[phase]]


def _gather_and_project(x2, g_pre, w_in_s, w_mkv_s, w_out_s, rel_bias_t, buckets, x_arr):
    n_tok = x2.shape[0]
    n_tiles = n_tok // PROJ_TILE
    last = n_tiles - 1
    shapes = [w_in_s.shape, w_mkv_s.shape, w_out_s.shape]
    n_w = len(shapes)

    def body(x_sref, x_ref, g_ref, win_hbm, wmkv_hbm, wout_hbm, rel_ref, bk_ref, h_ref, *refs):
        part_refs, refs = refs[:len(PROJ_WIDTHS)], refs[len(PROJ_WIDTHS):]
        bias_ref, refs = refs[0], refs[1:]
        gin_hbm, gmkv_hbm, gout_hbm, wg, stage_in, stage_mkv, stage_out, own_mkv, own_out, h_all = refs[:10]
        send_sems, recv_sems, local_sems = refs[10:]
        p, t = pl.program_id(0), pl.program_id(1)
        x, y, c = lax.axis_index("x"), lax.axis_index("y"), lax.axis_index("c")
        me, sibling = (x, y, c), (x, y, 1 - c)
        my_shard = 2 * x + y
        gathered = [wg, gmkv_hbm, gout_hbm]

        def half_rows(w, shard, half):
            rows = shapes[w][0] // 2
            if w == 0:
                return wg.at[pl.ds(pl.multiple_of(shard * shapes[0][0] + half * rows, 16), rows), :]
            return gathered[w].at[shard, pl.ds(half * rows, rows), :]

        def first(w, rel):
            src = half_rows(w, my_shard, c) if w == 0 else (own_mkv, own_out)[w - 1].at[
                pl.ds(c * (shapes[w][0] // 2), shapes[w][0] // 2), :]
            k = 3 * w + rel - 1
            return pltpu.make_async_remote_copy(
                src_ref=src, dst_ref=half_rows(w, my_shard, c), send_sem=send_sems.at[k], recv_sem=recv_sems.at[k],
                device_id=(x ^ (rel >> 1), y ^ (rel & 1), c), device_id_type=MESH)

        def landed(w, rel):
            k = 3 * w + rel - 1
            ref = half_rows(w, my_shard ^ rel, c)
            return pltpu.make_async_remote_copy(src_ref=ref, dst_ref=ref, send_sem=send_sems.at[k],
                                                recv_sem=recv_sems.at[k], device_id=me, device_id_type=MESH)

        def passed(w, rel, half, to):
            k = 9 + 3 * w + rel - 1
            ref = half_rows(w, my_shard ^ rel, half)
            return pltpu.make_async_remote_copy(src_ref=ref, dst_ref=ref, send_sem=send_sems.at[k],
                                                recv_sem=recv_sems.at[k], device_id=to, device_id_type=MESH)

        def pass_on(w, rels):
            for rel in rels:
                landed(w, rel).wait_recv()
                passed(w, rel, c, sibling).start()
            for rel in rels:
                passed(w, rel, 1 - c, me).wait_recv()

        own_stores = [pltpu.make_async_copy(own_mkv, gmkv_hbm.at[my_shard], local_sems.at[3]),
                      pltpu.make_async_copy(own_out, gout_hbm.at[my_shard], local_sems.at[4])]

        @pl.when((p == 0) & (t == 0))
        def _():
            loads = [pltpu.make_async_copy(src, dst, local_sems.at[k]) for k, (src, dst) in enumerate(
                ((win_hbm, stage_in), (wmkv_hbm, stage_mkv), (wout_hbm, stage_out)))]
            for cp in loads:
                cp.start()
            loads[0].wait()
            wg[pl.ds(pl.multiple_of(my_shard * shapes[0][0], 16), shapes[0][0]), :] = stage_in[...].astype(BF16)
            for rel in (1, 2):
                first(0, rel).start()
            loads[1].wait()
            loads[2].wait()
            own_mkv[...] = stage_mkv[...].astype(BF16)
            own_out[...] = stage_out[...].astype(BF16)
            for cp in own_stores:
                cp.start()
            _fill_bias(rel_ref, bk_ref, bias_ref)
            pass_on(0, (1,))
            first(0, 3).start()

        @pl.when((p == 0) & (t == n_tiles // 2))
        def _():
            for w in (1, 2):
                for rel in (1, 2, 3):
                    first(w, rel).start()

        @pl.when((p == 1) & (t == 0))
        def _():
            pass_on(0, (2, 3))

        tile_rows = pl.ds(pl.multiple_of(t * PROJ_TILE, PROJ_TILE), PROJ_TILE)

        def project(h, phase):
            start = jnp.where(x_sref[0] == 0, _phase_columns(phase, 0), _phase_columns(phase, 1))
            proj = _mm_nt(h, wg[pl.ds(pl.multiple_of(start, MXU_TILE), PHASE_COLS[phase]), :])
            for chip_x in range(2):
                @pl.when(x_sref[0] == chip_x)
                def _():
                    for k, lo in _phase_parts(phase, chip_x):
                        part_refs[k][...] = proj[:, lo:lo + PROJ_WIDTHS[k]]

        @pl.when(p == 0)
        def _():
            xv = x_ref[...]
            r = lax.rsqrt(jnp.mean(xv * xv, axis=-1, keepdims=True) + EPS)
            h = (xv * r * g_ref[...]).astype(BF16)
            h_ref[...] = h
            h_all[tile_rows, :] = h
            project(h, 0)

        @pl.when(p == 1)
        def _():
            project(h_all[tile_rows, :], 1)

        @pl.when((p == 1) & (t == last))
        def _():
            store = pltpu.make_async_copy(wg, gin_hbm, local_sems.at[5])
            store.start()
            for w in (1, 2):
                pass_on(w, (1, 2, 3))
            for w in range(n_w):
                for rel in (1, 2, 3):
                    first(w, rel).wait_send()
                    passed(w, rel, c, sibling).wait_send()
            for cp in own_stores:
                cp.wait()
            store.wait()

    def written_in(k):
        phase_on = [next(ph for ph in range(2) if k in dict(_phase_parts(ph, chip_x))) for chip_x in range(2)]

        def index(p, t, xs):
            phase = jnp.where(xs[0] == 0, phase_on[0], phase_on[1])
            return (jnp.where(p == phase, t, jnp.where(p < phase, 0, last)), 0)
        return index

    part_specs = [pl.BlockSpec((PROJ_TILE, PROJ_WIDTHS[k]), written_in(k)) for k in range(len(PROJ_WIDTHS))]
    vmem = pltpu.VMEM
    out = pl.pallas_call(
        body, name="gather_and_project",
        out_shape=[jax.ShapeDtypeStruct((n_tok, D_MODEL), BF16)]
        + [jax.ShapeDtypeStruct((n_tok, w), F32) for w in PROJ_WIDTHS]
        + [jax.ShapeDtypeStruct((2, 4 * CHUNK, 2 * CHUNK), F32)]
        + [jax.ShapeDtypeStruct((N_CHIPS * shapes[0][0], shapes[0][1]), BF16)]
        + [jax.ShapeDtypeStruct((N_CHIPS,) + s, BF16) for s in shapes[1:]],
        grid_spec=pltpu.PrefetchScalarGridSpec(
            num_scalar_prefetch=1, grid=(2, n_tiles),
            in_specs=[pl.BlockSpec((PROJ_TILE, D_MODEL), lambda p, t, xs: (jnp.where(p == 0, t, last), 0)),
                      pl.BlockSpec((1, D_MODEL), lambda p, t, xs: (0, 0)), ANY_SPEC, ANY_SPEC, ANY_SPEC, SMEM_SPEC,
                      pl.BlockSpec(buckets.shape, lambda p, t, xs: (0, 0))],
            out_specs=[pl.BlockSpec((PROJ_TILE, D_MODEL), lambda p, t, xs: (jnp.where(p == 0, t, last), 0))]
            + part_specs + [pl.BlockSpec((2, 4 * CHUNK, 2 * CHUNK), lambda p, t, xs: (0, 0, 0))] + [ANY_SPEC] * 3,
            scratch_shapes=[vmem((N_CHIPS * shapes[0][0], shapes[0][1]), BF16), vmem(shapes[0], F32),
                            vmem(shapes[1], F32), vmem(shapes[2], F32), vmem(shapes[1], BF16), vmem(shapes[2], BF16),
                            vmem((n_tok, D_MODEL), BF16),
                            pltpu.SemaphoreType.DMA((18,)), pltpu.SemaphoreType.DMA((18,)),
                            pltpu.SemaphoreType.DMA((6,))]),
        compiler_params=pltpu.CompilerParams(vmem_limit_bytes=VMEM_LIMIT),
    )(x_arr, x2, g_pre, w_in_s, w_mkv_s, w_out_s, rel_bias_t, buckets)
    n_parts = len(PROJ_WIDTHS)
    return out[0], list(out[1:1 + n_parts]), out[2 + n_parts:], out[1 + n_parts]


def _load_chunk(j, i, sk_ref, sv_ref, skp_ref, svp_ref):
    rows = slice(j * CHUNK, (j + 1) * CHUNK)
    if j == 0:
        k_prev, v_prev, table = skp_ref[...], svp_ref[...], jnp.where(i > 0, 0, 1)
    else:
        prev = slice((j - 1) * CHUNK, j * CHUNK)
        k_prev, v_prev, table = sk_ref[prev, :], sv_ref[prev, :], 0
    k_pairs = _pair_operands(_swa_variants(jnp.concatenate([k_prev, sk_ref[rows, :]], axis=0)))
    v_pairs = _pair_operands(_swa_variants(jnp.concatenate([v_prev, sv_ref[rows, :]], axis=0)))
    return rows, k_pairs, v_pairs, table


def _tile_constants(ws_ref, bs_ref, sink_ref, mkv_ref):
    wm = _causal_weights(ws_ref)
    bs_rows = [jnp.concatenate([bs_ref[g]] * TILE_CHUNKS, axis=0) for g in range(A_GROUPS)]
    sink_col = jnp.max(jnp.concatenate([jnp.full((CHUNK, 128), sink_ref[0, h], F32) for h in range(4)] * TILE_CHUNKS,
                                       axis=0), axis=-1, keepdims=True)
    mkv_v = mkv_ref[0]
    mk_pairs = _pair_operands(_mem_variants(mkv_v[:, :MEM_WIDTH]))
    mv_pairs = _pair_operands(_mem_variants(mkv_v[:, MEM_WIDTH:]))
    return wm, bs_rows, sink_col, mk_pairs, mv_pairs


def _mix(parts, mkv, x2, tgt2, v_g, v_b, w_sp, b_sp, sinks, bias, w_out, g_post, n_ex, seq):
    n_tiles_ex = seq // TILE
    n_tok = n_ex * seq
    au, av, sq, sk, sv, mq, z = parts
    col = dict(zip(("au", "av", "sq", "sk", "sv", "mq", "z"),
                   (slice(PROJ_OFFSETS[k], PROJ_OFFSETS[k + 1]) for k in range(len(PROJ_WIDTHS)))))
    before_kv, after_kv = slice(0, col["sk"].start), slice(col["sv"].stop, IN_WIDTH)

    def body(au_ref, av_ref, sq_ref, sk_ref, sv_ref, skp_ref, svp_ref, mq_ref, z_ref, mkv_ref, x_ref, tgt_ref,
             vg_ref, vb_ref, ws_ref, bs_ref, sink_ref, bias_ref, wout_ref, gpost_ref,
             dout_ref, dproj_ref, dmkv_ref, dwout_ref, dvg_ref, dvb_ref, dws_ref, dbs_ref, dsink_ref, drel_ref,
             loss_ref, dgpost_ref, carry_dp, carry_k, carry_v):
        b, i = pl.program_id(0), pl.program_id(1)

        @pl.when((b == 0) & (i == 0))
        def _():
            for ref in (dwout_ref, dvg_ref, dvb_ref, dws_ref, dbs_ref, dsink_ref, drel_ref, loss_ref, dgpost_ref):
                ref[...] = jnp.zeros_like(ref)

        @pl.when(i == 0)
        def _():
            dmkv_ref[...] = jnp.zeros_like(dmkv_ref)
            carry_k[...] = jnp.zeros_like(carry_k)
            carry_v[...] = jnp.zeros_like(carry_v)

        @pl.when(i > 0)
        def _():
            dproj_ref[:, before_kv] = carry_dp[:, before_kv]
            dproj_ref[:, after_kv] = carry_dp[:, after_kv]

        @pl.when(i < n_tiles_ex)
        def _():
            wm, bs_rows, sink_col, mk_pairs, mv_pairs = _tile_constants(ws_ref, bs_ref, sink_ref, mkv_ref)
            vg = vg_ref[...]

            au_v, av_v = au_ref[...], av_ref[...]
            ya, res = _group_a_forward(au_v, av_v, vg, vb_ref[...], wm, bs_rows)
            swa, logits, yb = [], [], []
            for j in range(TILE_CHUNKS):
                rows, k_pairs, v_pairs, table = _load_chunk(j, i, sk_ref, sv_ref, skp_ref, svp_ref)
                qp = _halves_bf16(sq_ref[rows, :] * QK_SCALE)
                logits.append(_attention_logits(qp, k_pairs) + bias_ref[table])
                swa.append([rows, k_pairs, v_pairs, qp])
            p_swa, sink_p = _softmax(jnp.concatenate(logits, axis=0), sink_col)
            for j in range(TILE_CHUNKS):
                out, pp = _attention_out(p_swa[j * 4 * CHUNK:(j + 1) * 4 * CHUNK], swa[j][2], CHUNK)
                yb.append(out)
                swa[j].append(pp)
            mqp = _halves_bf16(mq_ref[...] * QK_SCALE)
            pm, _ = _softmax(_attention_logits(mqp, mk_pairs), None)
            yc, ppm = _attention_out(pm, mv_pairs, TILE)
            ycat = jnp.concatenate(ya + [jnp.concatenate(yb, axis=0), yc], axis=-1)

            zv = z_ref[...]
            sig = _sigmoid(zv)
            sz = zv * sig
            y_b = (ycat * sz).astype(BF16)
            o = _mm(y_b, wout_ref[...])
            r2 = lax.rsqrt(jnp.mean(o * o, axis=-1, keepdims=True) + EPS)
            nrm = o * r2
            gp = gpost_ref[...]
            diff = x_ref[...] + nrm * gp - tgt_ref[...]
            loss_ref[...] += jnp.sum(diff * diff) * (0.5 / D_MODEL)
            dout = diff * (1.0 / D_MODEL)
            dout_ref[...] = dout
            dgpost_ref[...] += jnp.sum(dout * nrm, axis=0, keepdims=True)
            dn = dout * gp
            do_b = (r2 * (dn - nrm * jnp.mean(dn * nrm, axis=-1, keepdims=True))).astype(BF16)
            dwout_ref[...] += _mm_tn(y_b, do_b)
            dy = _mm_nt(do_b, wout_ref[...])
            carry_dp[:, col["z"]] = (dy * ycat * (sig * (1.0 + zv * (1.0 - sig)))).astype(BF16)
            dyc = dy * sz

            dgu, dgv = [], []
            for g in range(A_GROUPS):
                sl = slice(g * 128, (g + 1) * 128)
                xhat, rstd, vn, s = res["groups"][g]
                dya = dyc[:, sl]
                dgu.append(dya * s)
                ds = dya * res["gu"][:, sl]
                dbs_ref[:, sl] += sum(ds[c * CHUNK:(c + 1) * CHUNK] for c in range(TILE_CHUNKS))
                ds_b = _rows_to_lanes(ds.astype(BF16), TILE_CHUNKS)
                dws_ref[g] += _mm_nt(ds_b, vn)
                dvn = _lanes_to_rows(_mm_tn(wm[g], ds_b), TILE_CHUNKS)
                dvg_ref[:, sl] += jnp.sum(dvn * xhat, axis=0, keepdims=True)
                dvb_ref[:, sl] += jnp.sum(dvn, axis=0, keepdims=True)
                dxh = dvn * vg[:, sl]
                dgv.append(rstd * (dxh - jnp.mean(dxh, axis=-1, keepdims=True)
                                   - xhat * jnp.mean(dxh * xhat, axis=-1, keepdims=True)))
            carry_dp[:, col["au"]] = (jnp.concatenate(dgu, axis=-1) * _gelu_grad(au_v, res["tu"])).astype(BF16)
            carry_dp[:, col["av"]] = (jnp.concatenate(dgv, axis=-1) * _gelu_grad(av_v, res["tv"])).astype(BF16)

            do_pairs = [_halves_bf16(dyc[rows, A_WIDTH:A_WIDTH + SWA_WIDTH]) for rows, *_ in swa]
            dl_swa, delta = _softmax_backward(p_swa, jnp.concatenate(
                [_attention_dprobs(do_pairs[j], swa[j][2]) for j in range(TILE_CHUNKS)], axis=0))
            sink_terms = sink_p * delta
            lane4 = lax.broadcasted_iota(jnp.int32, (1, 128), 1)
            dsink_vec = jnp.zeros((1, 128), F32)
            for h in range(4):
                head_sum = sum(jnp.sum(sink_terms[(4 * j + h) * CHUNK:(4 * j + h + 1) * CHUNK])
                               for j in range(TILE_CHUNKS))
                dsink_vec = dsink_vec + jnp.where(lane4 == h, -head_sum, 0.0)
            dsink_ref[...] += dsink_vec
            drel_ref[...] += sum(dl_swa[j * 4 * CHUNK:(j + 1) * 4 * CHUNK] for j in range(TILE_CHUNKS))
            dk_parts, dv_parts = [], []
            for j, (rows, k_pairs, v_pairs, qp, pp) in enumerate(swa):
                dq, dk, dv = _attention_grads(dl_swa[j * 4 * CHUNK:(j + 1) * 4 * CHUNK], pp, do_pairs[j], qp, k_pairs,
                                              CHUNK)
                carry_dp[rows, col["sq"]] = (dq * QK_SCALE).astype(BF16)
                dk_parts.append(_swa_unvariants(*_split_pair_grads(dk)))
                dv_parts.append(_swa_unvariants(*_split_pair_grads(dv)))

            dc_pairs = _halves_bf16(dyc[:, A_WIDTH + SWA_WIDTH:])
            dl_mem, _ = _softmax_backward(pm, _attention_dprobs(dc_pairs, mv_pairs))
            dmq, dmk, dmv = _attention_grads(dl_mem, ppm, dc_pairs, mqp, mk_pairs, TILE)
            carry_dp[:, col["mq"]] = (dmq * QK_SCALE).astype(BF16)
            dmkv_ref[0] += jnp.concatenate([_mem_unvariants(*_split_pair_grads(dmk)),
                                            _mem_unvariants(*_split_pair_grads(dmv))], axis=-1)

            for parts_c, carry, cols in ((dk_parts, carry_k, col["sk"]), (dv_parts, carry_v, col["sv"])):
                @pl.when(i > 0)
                def _():
                    dproj_ref[:, cols] = (carry[...] + jnp.concatenate(
                        [jnp.zeros((TILE - CHUNK, KV_WIDTH), F32), parts_c[0][:CHUNK]], axis=0)).astype(BF16)
                new = [parts_c[0][CHUNK:]]
                for j in range(1, TILE_CHUNKS):
                    new[-1] = new[-1] + parts_c[j][:CHUNK]
                    new.append(parts_c[j][CHUNK:])
                carry[...] = jnp.concatenate(new, axis=0)

        @pl.when(i == n_tiles_ex)
        def _():
            dproj_ref[:, col["sk"]] = carry_k[...].astype(BF16)
            dproj_ref[:, col["sv"]] = carry_v[...].astype(BF16)

    tile = functools.partial(_tile_specs, n_tiles_ex)
    prev = functools.partial(_prev_chunk_spec, n_tiles_ex)
    late = pl.BlockSpec((TILE, IN_WIDTH), lambda b, i: (b * n_tiles_ex + jnp.maximum(i - 1, 0), 0))
    return pl.pallas_call(
        body, name="mix", grid=(n_ex, n_tiles_ex + 1),
        out_shape=[jax.ShapeDtypeStruct((n_tok, D_MODEL), F32), jax.ShapeDtypeStruct((n_tok, IN_WIDTH), BF16),
                   jax.ShapeDtypeStruct((n_ex, MEM_LEN, 2 * MEM_WIDTH), F32),
                   jax.ShapeDtypeStruct((MIX_WIDTH, D_MODEL), F32), jax.ShapeDtypeStruct((1, A_WIDTH), F32),
                   jax.ShapeDtypeStruct((1, A_WIDTH), F32), jax.ShapeDtypeStruct((A_GROUPS, CHUNK, CHUNK), F32),
                   jax.ShapeDtypeStruct((CHUNK, A_WIDTH), F32), jax.ShapeDtypeStruct((1, 128), F32),
                   jax.ShapeDtypeStruct((4 * CHUNK, 2 * CHUNK), F32), jax.ShapeDtypeStruct((1, 128), F32),
                   jax.ShapeDtypeStruct((1, D_MODEL), F32)],
        in_specs=[tile(A_WIDTH), tile(A_WIDTH), tile(SWA_WIDTH), tile(KV_WIDTH), tile(KV_WIDTH),
                  prev(KV_WIDTH), prev(KV_WIDTH), tile(MEM_WIDTH), tile(MIX_WIDTH),
                  pl.BlockSpec((1, MEM_LEN, 2 * MEM_WIDTH), lambda b, i: (b, 0, 0)),
                  tile(D_MODEL), tile(D_MODEL),
                  _full_spec((1, A_WIDTH)), _full_spec((1, A_WIDTH)), _full_spec((A_GROUPS, CHUNK, CHUNK)),
                  _full_spec((A_GROUPS, CHUNK, CHUNK)), SMEM_SPEC, _full_spec((2, 4 * CHUNK, 2 * CHUNK)),
                  _full_spec((MIX_WIDTH, D_MODEL)), _full_spec((1, D_MODEL))],
        out_specs=[tile(D_MODEL), late, pl.BlockSpec((1, MEM_LEN, 2 * MEM_WIDTH), lambda b, i: (b, 0, 0)),
                   _full_spec((MIX_WIDTH, D_MODEL)), _full_spec((1, A_WIDTH)), _full_spec((1, A_WIDTH)),
                   _full_spec((A_GROUPS, CHUNK, CHUNK)), _full_spec((CHUNK, A_WIDTH)), _full_spec((1, 128)),
                   _full_spec((4 * CHUNK, 2 * CHUNK)), _full_spec((1, 128)), _full_spec((1, D_MODEL))],
        scratch_shapes=[pltpu.VMEM((TILE, IN_WIDTH), BF16), pltpu.VMEM((TILE, KV_WIDTH), F32),
                        pltpu.VMEM((TILE, KV_WIDTH), F32)],
        compiler_params=pltpu.CompilerParams(vmem_limit_bytes=VMEM_LIMIT),
    )(au, av, sq, sk, sv, sk, sv, mq, z, mkv, x2, tgt2, v_g, v_b, w_sp, b_sp, sinks, bias, w_out, g_post)


BWD_PROJ_TILE = 512


def _backward_projection(x2, dout, dproj, g_pre, w_in_t):
    n_tok = x2.shape[0]
    n_steps = n_tok // BWD_PROJ_TILE

    def body(x_ref, dout_ref, dp_ref, g_ref, w_hbm, dx_ref, dgpre_ref, w_vmem, sem):
        @pl.when(pl.program_id(0) == 0)
        def _():
            load = pltpu.make_async_copy(w_hbm, w_vmem, sem)
            load.start()
            dgpre_ref[...] = jnp.zeros_like(dgpre_ref)
            load.wait()

        xv = x_ref[...]
        r = lax.rsqrt(jnp.mean(xv * xv, axis=-1, keepdims=True) + EPS)
        xn = xv * r
        dh = _mm(dp_ref[...], w_vmem[...])
        dgpre_ref[...] += jnp.sum(dh * xn, axis=0, keepdims=True)
        dhg = dh * g_ref[...]
        dx_ref[...] = r * (dhg - xn * jnp.mean(dhg * xn, axis=-1, keepdims=True)) + dout_ref[...]

    row = lambda w: pl.BlockSpec((BWD_PROJ_TILE, w), lambda i: (i, 0))
    return pl.pallas_call(
        body, name="backward_projection", grid=(n_steps,),
        out_shape=[jax.ShapeDtypeStruct((n_tok, D_MODEL), F32), jax.ShapeDtypeStruct((1, D_MODEL), F32)],
        in_specs=[row(D_MODEL), row(D_MODEL), row(IN_WIDTH), _full_spec((1, D_MODEL)), ANY_SPEC],
        out_specs=[row(D_MODEL), _full_spec((1, D_MODEL))],
        scratch_shapes=[pltpu.VMEM((IN_WIDTH, D_MODEL), BF16), pltpu.SemaphoreType.DMA],
        input_output_aliases={1: 0},
        compiler_params=pltpu.CompilerParams(vmem_limit_bytes=VMEM_LIMIT),
    )(x2, dout, dproj, g_pre, w_in_t)


SHARD_ROWS = IN_WIDTH // N_CHIPS
SHARD_WINDOW = 768
SHARD_HALF = SHARD_ROWS // 2
DWIN_TILE = 2048
N_REL = N_CHIPS - 1


def _shard_window_start(shard):
    return (shard * SHARD_ROWS // 128) * 128


def _reduce_gradients(dproj, h, big, small, shard_arr):
    n_tok = h.shape[0]
    tile = min(DWIN_TILE, n_tok)
    n_sub = n_tok // tile
    last = N_CHIPS - 1
    n_big, n_small = len(big), len(small)
    big_half = [g.shape[2:] for g in big]
    sem_big_d2d = 2 * N_CHIPS
    sem_big_ici = sem_big_d2d + n_big
    sem_big_swap = sem_big_ici + N_REL * n_big
    sem_small_d2d = sem_big_swap + n_big
    sem_small_ici = sem_small_d2d + n_small
    n_sems = sem_small_ici + N_REL * n_small
    loc_small = n_big
    loc_out_win = loc_small + n_small
    loc_out_big = loc_out_win + 2
    loc_out_small = loc_out_big + 2 * n_big
    n_local = loc_out_small + n_small

    def relation_of_slot(s):
        return (s + 2) % N_REL + 1

    def shard_of_slot(s, my_shard):
        return my_shard ^ jnp.where(s == last, 0, relation_of_slot(s))

    def body(shard_ref, dp_ref, h_hbm, *refs):
        h_vmem, h_sem, refs = refs[-2], refs[-1], refs[:-2]
        big_hbm, refs = refs[:n_big], refs[n_big:]
        small_hbm, refs = refs[:n_small], refs[n_small:]
        out_hbm, refs = refs[0], refs[1:]
        big_out, refs = refs[:n_big], refs[n_big:]
        small_out, refs = refs[:n_small], refs[n_small:]
        part, recv_d2d, send_ici, recv_ici, mine_buf, other_buf = refs[:6]
        refs = refs[6:]
        big_own, big_recv, big_send, big_land, big_mine, big_other = (
            refs[k * n_big:(k + 1) * n_big] for k in range(6))
        refs = refs[6 * n_big:]
        small_own, small_recv, small_all = (refs[k * n_small:(k + 1) * n_small] for k in range(3))
        send_sems, recv_sems, local_sems = refs[3 * n_small:]

        s, t = pl.program_id(0), pl.program_id(1)
        x, y, c = lax.axis_index("x"), lax.axis_index("y"), lax.axis_index("c")
        my_chip = 2 * x + y
        sibling = (x, y, 1 - c)
        my_rows = pl.ds(pl.multiple_of(c * SHARD_HALF, 8), SHARD_HALF)
        other_rows = pl.ds(pl.multiple_of((1 - c) * SHARD_HALF, 8), SHARD_HALF)

        def remote(src, dst, k, to):
            return pltpu.make_async_remote_copy(src_ref=src, dst_ref=dst, send_sem=send_sems.at[k],
                                                recv_sem=recv_sems.at[k], device_id=to, device_id_type=MESH)

        def chip_at(rel):
            return (x ^ (rel >> 1), y ^ (rel & 1), c)

        def to_sibling(k):
            return remote(part.at[k % 2, other_rows, :], recv_d2d.at[k], k, sibling)

        def to_chip(k):
            return remote(send_ici.at[k], recv_ici.at[k], N_CHIPS + k, chip_at(relation_of_slot(k)))

        swap = remote(mine_buf, other_buf, 2 * N_CHIPS - 1, sibling)
        big_load = [pltpu.make_async_copy(big_hbm[w].at[:, pl.ds(c, 1)], big_own[w], local_sems.at[w])
                    for w in range(n_big)]
        big_to_sibling = [remote(big_hbm[w].at[:, pl.ds(1 - c, 1)], big_recv[w], sem_big_d2d + w, sibling)
                          for w in range(n_big)]
        big_to_chip = [[remote(big_send[w].at[k], big_land[w].at[k], sem_big_ici + N_REL * w + k, chip_at(k + 1))
                        for k in range(N_REL)] for w in range(n_big)]
        big_swap = [remote(big_mine[w], big_other[w], sem_big_swap + w, sibling) for w in range(n_big)]
        small_load = [pltpu.make_async_copy(small_hbm[i], small_own[i], local_sems.at[loc_small + i])
                      for i in range(n_small)]
        small_to_sibling = [remote(small_hbm[i], small_recv[i], sem_small_d2d + i, sibling) for i in range(n_small)]
        small_to_chip = [[remote(small_all[i].at[my_chip], small_all[i].at[my_chip],
                                 sem_small_ici + N_REL * i + k, chip_at(k + 1))
                          for k in range(N_REL)] for i in range(n_small)]

        @pl.when((s == 0) & (t == 0))
        def _():
            h_load = pltpu.make_async_copy(h_hbm, h_vmem, h_sem)
            h_load.start()
            for cp in big_load + big_to_sibling + small_load + small_to_sibling:
                cp.start()
            h_load.wait()

        @pl.when((s == 0) & (t == n_sub - 1))
        def _():
            for cp in big_load + small_load:
                cp.wait()
            for cp in big_to_sibling + small_to_sibling:
                cp.wait_recv()
                cp.wait_send()
            for w in range(n_big):
                for k in range(N_REL):
                    shard = my_chip ^ (k + 1)
                    big_send[w][k] = (big_own[w][shard, 0] + big_recv[w][shard, 0]).astype(BF16)
                    big_to_chip[w][k].start()
            for i in range(n_small):
                small_all[i][my_chip] = small_own[i][...] + small_recv[i][...]
                for k in range(N_REL):
                    small_to_chip[i][k].start()

        @pl.when((s > 0) & (t == jnp.where(s == last, 0, min(1, n_sub - 1))))
        def _():
            k = s - 1
            cp = to_sibling(k)
            cp.wait_recv()
            cp.wait_send()
            send_ici[k] = (part[k % 2, my_rows, :] + recv_d2d[k]).astype(BF16)
            to_chip(k).start()

        def big_rows(w, half):
            rows = big_half[w][0]
            return big_out[w].at[pl.ds(pl.multiple_of(half * rows, 8), rows), :]

        big_store_mine = [pltpu.make_async_copy(big_mine[w], big_rows(w, c), local_sems.at[loc_out_big + 2 * w])
                          for w in range(n_big)]
        big_store_other = [pltpu.make_async_copy(big_other[w], big_rows(w, 1 - c),
                                                 local_sems.at[loc_out_big + 2 * w + 1]) for w in range(n_big)]
        small_store = [pltpu.make_async_copy(small_all[i], small_out[i], local_sems.at[loc_out_small + i])
                       for i in range(n_small)]

        @pl.when((s == last) & (t == 0))
        def _():
            for w in range(n_big):
                total = big_own[w][my_chip, 0] + big_recv[w][my_chip, 0]
                for k in range(N_REL):
                    big_to_chip[w][k].wait_recv()
                    total = total + big_land[w][k].astype(F32)
                big_mine[w][...] = total
                big_swap[w].start()
                big_store_mine[w].start()
            for i in range(n_small):
                for k in range(N_REL):
                    small_to_chip[i][k].wait_recv()
                small_store[i].start()

        r = _mm_tn(dp_ref[...], h_vmem[pl.ds(pl.multiple_of(t * tile, tile), tile), :])
        odd = shard_of_slot(s, shard_ref[0]) % 2
        for parity in range(2):
            rows = r[64 * parity:64 * parity + SHARD_ROWS]

            @pl.when((odd == parity) & (t == 0))
            def _():
                part[s % 2] = rows

            @pl.when((odd == parity) & (t > 0))
            def _():
                part[s % 2] += rows

        @pl.when(t == n_sub - 1)
        def _():
            to_sibling(s).start()

        @pl.when((s == last) & (t == n_sub - 1))
        def _():
            cp = to_sibling(last)
            cp.wait_recv()
            cp.wait_send()
            total = part[last % 2, my_rows, :] + recv_d2d[last]
            for k in range(last):
                to_chip(k).wait_recv()
                total = total + recv_ici[k].astype(F32)
            mine_buf[...] = total
            swap.start()
            out_mine = pltpu.make_async_copy(mine_buf, out_hbm.at[my_rows, :], local_sems.at[0])
            out_mine.start()
            swap.wait_recv()
            out_other = pltpu.make_async_copy(other_buf, out_hbm.at[other_rows, :], local_sems.at[1])
            out_other.start()
            for w in range(n_big):
                big_swap[w].wait_recv()
                big_store_other[w].start()
            stores = [out_mine, out_other] + big_store_mine + big_store_other + small_store
            for k in range(last):
                to_chip(k).wait_send()
            swap.wait_send()
            for w in range(n_big):
                for k in range(N_REL):
                    big_to_chip[w][k].wait_send()
                big_swap[w].wait_send()
            for i in range(n_small):
                for k in range(N_REL):
                    small_to_chip[i][k].wait_send()
            for cp in stores:
                cp.wait()

    half = (SHARD_HALF, D_MODEL)
    vmem = pltpu.VMEM
    scratch = [vmem((2, SHARD_ROWS, D_MODEL), F32), vmem((N_CHIPS,) + half, F32),
               vmem((N_REL,) + half, BF16), vmem((N_REL,) + half, BF16), vmem(half, F32), vmem(half, F32)]
    scratch += [vmem((N_CHIPS, 1) + hs, F32) for hs in big_half] * 2
    scratch += [vmem((N_REL,) + hs, BF16) for hs in big_half] * 2
    scratch += [vmem(hs, F32) for hs in big_half] * 2
    scratch += [vmem(a.shape, F32) for a in small] * 2 + [vmem((N_CHIPS,) + a.shape, F32) for a in small]
    scratch += [pltpu.SemaphoreType.DMA((n_sems,)), pltpu.SemaphoreType.DMA((n_sems,)),
                pltpu.SemaphoreType.DMA((n_local,)), vmem(h.shape, BF16), pltpu.SemaphoreType.DMA]
    n_hbm = n_big + n_small
    out = pl.pallas_call(
        body, name="reduce_gradients",
        out_shape=[jax.ShapeDtypeStruct((SHARD_ROWS, D_MODEL), F32)]
        + [jax.ShapeDtypeStruct((2 * hs[0], hs[1]), F32) for hs in big_half]
        + [jax.ShapeDtypeStruct((N_CHIPS,) + a.shape, F32) for a in small],
        grid_spec=pltpu.PrefetchScalarGridSpec(
            num_scalar_prefetch=1, grid=(N_CHIPS, n_sub),
            in_specs=[pl.BlockSpec((pl.Element(tile), pl.Element(SHARD_WINDOW)),
                                   lambda s, t, m: (t * tile, _shard_window_start(shard_of_slot(s, m[0])))),
                      ANY_SPEC] + [ANY_SPEC] * n_hbm,
            out_specs=[ANY_SPEC] * (1 + n_hbm),
            scratch_shapes=scratch),
        compiler_params=pltpu.CompilerParams(vmem_limit_bytes=VMEM_LIMIT),
    )(shard_arr, dproj, h, *big, *small)
    return out[:1 + n_big], out[1 + n_big:]


def _memkv_backward(mem, dmkv, g_mem, w_mkv):
    n_ex = mem.shape[0]

    def body(mem_ref, d_ref, g_ref, w_ref, dw_ref, dg_ref):
        @pl.when(pl.program_id(0) == 0)
        def _():
            dw_ref[...] = jnp.zeros_like(dw_ref)
            dg_ref[...] = jnp.zeros_like(dg_ref)

        m = mem_ref[0]
        mn = m * lax.rsqrt(jnp.mean(m * m, axis=-1, keepdims=True) + EPS)
        d_b = d_ref[0].astype(BF16)
        dw_ref[...] += _mm_tn((mn * g_ref[...]).astype(BF16), d_b)
        dg_ref[...] += jnp.sum(_mm_nt(d_b, w_ref[...]) * mn, axis=0, keepdims=True)

    return pl.pallas_call(
        body, name="memkv_backward", grid=(n_ex,),
        out_shape=[jax.ShapeDtypeStruct((D_MODEL, 2 * MEM_WIDTH), F32), jax.ShapeDtypeStruct((1, D_MODEL), F32)],
        in_specs=[pl.BlockSpec((1, MEM_LEN, D_MODEL), lambda b: (b, 0, 0)),
                  pl.BlockSpec((1, MEM_LEN, 2 * MEM_WIDTH), lambda b: (b, 0, 0)),
                  _full_spec((1, D_MODEL)), _full_spec((D_MODEL, 2 * MEM_WIDTH))],
        out_specs=[_full_spec((D_MODEL, 2 * MEM_WIDTH)), _full_spec((1, D_MODEL))],
    )(mem, dmkv, g_mem, w_mkv)


def _pack_small_grads(dgpre, dgpost, dgmem, dvg, dvb, dws, dbs, dsink, drel, loss_vec, buckets):
    def body(dgpre_ref, dgpost_ref, dgmem_ref, dvg_ref, dvb_ref, dws_ref, dbs_ref, dsink_ref, drel_ref, loss_ref,
             bk_ref, a_ref, b_ref):
        a_ref[...] = jnp.zeros_like(a_ref)
        b_ref[...] = jnp.zeros_like(b_ref)
        a_ref[0:1, :] = dgpre_ref[...]
        a_ref[1:2, :] = dgpost_ref[...]
        a_ref[2:3, :] = dgmem_ref[...]
        a_ref[3:4, :] = jnp.concatenate([dvg_ref[...], dvb_ref[...]], axis=-1)
        a_ref[ROW_LOSS:ROW_LOSS + 1, 0:128] = loss_ref[...]
        row = lax.broadcasted_iota(jnp.int32, (CHUNK, CHUNK), 0)
        col = lax.broadcasted_iota(jnp.int32, (CHUNK, CHUNK), 1)
        for g in range(A_GROUPS):
            b_ref[ROW_WS + g * CHUNK:ROW_WS + (g + 1) * CHUNK, :] = jnp.where(row >= col, dws_ref[g], 0.0)
            by_token = jnp.transpose(dbs_ref[:, g * 128:(g + 1) * 128])
            b_ref[ROW_BS + g:ROW_BS + g + 1, :] = jnp.sum(by_token, axis=0, keepdims=True)
        b_ref[ROW_SINK:ROW_SINK + 1, :] = dsink_ref[...]
        bk = bk_ref[...]
        rel_row = lax.broadcasted_iota(jnp.int32, (8, 128), 0)
        rel_col = lax.broadcasted_iota(jnp.int32, (8, 128), 1)
        rel = jnp.zeros((8, 128), F32)
        for h in range(4):
            acc = drel_ref[h * CHUNK:(h + 1) * CHUNK, :]
            for b in range(N_BUCKETS):
                rel = jnp.where((rel_row == h) & (rel_col == b), jnp.sum(jnp.where(bk == b, acc, 0.0)), rel)
        b_ref[ROW_REL:ROW_REL + 8, :] = rel

    return pl.pallas_call(
        body, name="pack_small_grads",
        out_shape=[jax.ShapeDtypeStruct((SMALL_A_ROWS, D_MODEL), F32), jax.ShapeDtypeStruct((SMALL_B_ROWS, 128), F32)],
        in_specs=[VMEM_SPEC] * 11, out_specs=[VMEM_SPEC] * 2,
    )(dgpre, dgpost, dgmem, dvg, dvb, dws, dbs, dsink, drel, loss_vec, buckets)


def _adamw(w, g, m, v):
    m2 = ADAM_B1 * m + (1.0 - ADAM_B1) * g
    v2 = ADAM_B2 * v + (1.0 - ADAM_B2) * (g * g)
    m_hat = m2 / (1.0 - ADAM_B1 ** ADAM_STEP)
    v_hat = v2 / (1.0 - ADAM_B2 ** ADAM_STEP)
    delta = -ADAM_LR * (m_hat / (jnp.sqrt(v_hat) + ADAM_EPS) + ADAM_WD * w)
    return delta, m2, v2


ADAM_STEPS = 4


def _adamw_all(shard_grads, shard_w, shard_m, shard_v, ra, rb, small_w, small_m, small_v):
    n_sh, n = len(shard_w), len(small_w)

    def body(*refs):
        sh_in, refs = refs[:4 * n_sh], refs[4 * n_sh:]
        ra_ref, rb_ref, refs = refs[0], refs[1], refs[2:]
        w_refs, m_refs, v_refs, refs = refs[:n], refs[n:2 * n], refs[2 * n:3 * n], refs[3 * n:]
        sh_out, outs = refs[:4 * n_sh], refs[4 * n_sh:]
        for k in range(n_sh):
            g = sh_in[k][...]
            delta, m2, v2 = _adamw(sh_in[n_sh + k][...], g, sh_in[2 * n_sh + k][...], sh_in[3 * n_sh + k][...])
            for ref, val in zip(sh_out[4 * k:4 * k + 4], (g, delta, m2, v2)):
                ref[...] = val

        @pl.when(pl.program_id(0) == 0)
        def _():
            g_outs, d_outs, m_outs, v_outs = outs[:n], outs[n:2 * n], outs[2 * n:3 * n], outs[3 * n:4 * n]
            ga, gb = ra_ref[0], rb_ref[0]
            for chip in range(1, N_CHIPS):
                ga = ga + ra_ref[chip]
                gb = gb + rb_ref[chip]
            outs[4 * n][...] = ga[ROW_LOSS:ROW_LOSS + 1, 0:128]
            grads = [ga[0:1, :], ga[1:2, :], ga[2:3, :], ga[3:4, :A_WIDTH], ga[3:4, A_WIDTH:],
                     gb[ROW_WS:ROW_WS + A_GROUPS * CHUNK, :].reshape(A_GROUPS, CHUNK, CHUNK),
                     gb[ROW_BS:ROW_BS + A_GROUPS, :], gb[ROW_SINK:ROW_SINK + 1, 0:4],
                     gb[ROW_REL:ROW_REL + 4, 0:N_BUCKETS]]
            for k in range(n):
                delta, m2, v2 = _adamw(w_refs[k][...], grads[k], m_refs[k][...], v_refs[k][...])
                g_outs[k][...] = grads[k]
                d_outs[k][...] = delta
                m_outs[k][...] = m2
                v_outs[k][...] = v2

    def rows_block(a):
        assert a.shape[0] % (8 * ADAM_STEPS) == 0
        return pl.BlockSpec((a.shape[0] // ADAM_STEPS, a.shape[1]), lambda i: (i, 0))

    sh_specs = [rows_block(w) for w in shard_w]
    small_in = [ra, rb, *small_w, *small_m, *small_v]
    small_out_shapes = [jax.ShapeDtypeStruct(w.shape, F32) for w in small_w] * 4 + [jax.ShapeDtypeStruct((1, 128), F32)]
    out = pl.pallas_call(
        body, name="adamw_all", grid=(ADAM_STEPS,),
        out_shape=[jax.ShapeDtypeStruct(w.shape, F32) for w in shard_w for _ in range(4)] + small_out_shapes,
        in_specs=sh_specs * 4 + [_full_spec(a.shape) for a in small_in],
        out_specs=[spec for spec in sh_specs for _ in range(4)] + [_full_spec(s.shape) for s in small_out_shapes],
        compiler_params=pltpu.CompilerParams(vmem_limit_bytes=VMEM_LIMIT),
    )(*shard_grads, *shard_w, *shard_m, *shard_v, *small_in)
    return [out[4 * k:4 * k + 4] for k in range(n_sh)], out[4 * n_sh:]


def kernel(x, mem, pre_norm_g, post_norm_g, mem_norm_g, w_in, w_mem_kv, v_norm_g, v_norm_b, w_spatial, b_spatial, attn_sinks, rel_bias, w_out, loss_target, m_pre_norm_g, m_post_norm_g, m_mem_norm_g, m_w_in, m_w_mem_kv, m_v_norm_g, m_v_norm_b, m_w_spatial, m_b_spatial, m_attn_sinks, m_rel_bias, m_w_out, v_pre_norm_g, v_post_norm_g, v_mem_norm_g, v_w_in, v_w_mem_kv, v_v_norm_g, v_v_norm_b, v_w_spatial, v_b_spatial, v_attn_sinks, v_rel_bias, v_w_out):
    n_ex, seq, _ = x.shape
    n_tok = n_ex * seq
    x2 = x.reshape(n_tok, D_MODEL)
    tgt2 = loss_target.reshape(n_tok, D_MODEL)
    buckets = jnp.asarray(_bucket_map())
    shard_arr = (2 * lax.axis_index("x") + lax.axis_index("y")).astype(jnp.int32).reshape(1)
    w_sp = w_spatial[0]
    b_sp = jnp.broadcast_to(b_spatial[0][:, :, None], (A_GROUPS, CHUNK, CHUNK))
    w_in_t, m_w_in_t, v_w_in_t = (jnp.transpose(a[0]) for a in (w_in, m_w_in, v_w_in))
    rel_t, m_rel_t, v_rel_t = (jnp.transpose(a) for a in (rel_bias, m_rel_bias, v_rel_bias))

    x_arr = lax.axis_index("x").astype(jnp.int32).reshape(1)
    h_b, parts, (w_in_b, g_mkv, g_out), bias = _gather_and_project(
        x2, pre_norm_g, w_in_t, w_mem_kv[0], w_out[0], rel_t, buckets, x_arr)
    w_mkv_b = g_mkv.reshape(D_MODEL, 2 * MEM_WIDTH)
    w_out_b = g_out.reshape(MIX_WIDTH, D_MODEL)

    mkv = _memkv_forward(mem, mem_norm_g, w_mkv_b)
    dout, dproj, dmkv, dwout, dvg, dvb, dws, dbs, dsink, drel, loss_vec, dgpost = _mix(
        parts, mkv, x2, tgt2, v_norm_g, v_norm_b, w_sp, b_sp, attn_sinks, bias, w_out_b, post_norm_g, n_ex, seq)

    dx, dgpre = _backward_projection(x2, dout, dproj, pre_norm_g, w_in_b)
    dwmkv, dgmem = _memkv_backward(mem, dmkv, mem_norm_g, w_mkv_b)
    small_a, small_b = _pack_small_grads(dgpre, dgpost, dgmem, dvg, dvb, dws, dbs, dsink, drel, loss_vec, buckets)

    shard_shapes = [w_mem_kv.shape[1:], w_out.shape[1:]]
    big = [g.reshape(N_CHIPS, 2, s[0] // 2, s[1]) for g, s in zip((dwmkv, dwout), shard_shapes)]
    (g_win, g_wmkv, g_wout), (ga, gb) = _reduce_gradients(dproj, h_b, big, [small_a, small_b], shard_arr)

    small_w = [pre_norm_g, post_norm_g, mem_norm_g, v_norm_g, v_norm_b, w_sp, b_spatial[0], attn_sinks, rel_t]
    small_m = [m_pre_norm_g, m_post_norm_g, m_mem_norm_g, m_v_norm_g, m_v_norm_b, m_w_spatial[0], m_b_spatial[0],
               m_attn_sinks, m_rel_t]
    small_v = [v_pre_norm_g, v_post_norm_g, v_mem_norm_g, v_v_norm_g, v_v_norm_b, v_w_spatial[0], v_b_spatial[0],
               v_attn_sinks, v_rel_t]
    big_out, small_out = _adamw_all(
        [g_win, g_wmkv, g_wout], [w_in_t, w_mem_kv[0], w_out[0]], [m_w_in_t, m_w_mem_kv[0], m_w_out[0]],
        [v_w_in_t, v_w_mem_kv[0], v_w_out[0]], ga, gb, small_w, small_m, small_v)
    n_small = len(small_w)

    outputs = [small_out[4 * n_small][0, 0], dx.reshape(x.shape)]
    for kind in range(4):
        s = small_out[kind * n_small:(kind + 1) * n_small]
        outputs += [s[0], s[1], s[2], jnp.transpose(big_out[0][kind])[None], big_out[1][kind][None], s[3], s[4],
                    s[5][None], s[6][None], s[7], jnp.transpose(s[8]), big_out[2][kind][None]]
    return tuple(outputs)
```

```python
import functools

import numpy as np
import jax
import jax.numpy as jnp
from jax import lax
from jax.experimental import pallas as pl
from jax.experimental.pallas import tpu as pltpu

F32 = jnp.float32
BF16 = jnp.bfloat16
MESH = pl.DeviceIdType.MESH

D_MODEL = 1024
CHUNK = 128
A_WIDTH = 512
A_GROUPS = 4
SWA_WIDTH = 256
KV_WIDTH = 128
MEM_WIDTH = 256
MEM_LEN = 256
MIX_WIDTH = 1024
IN_WIDTH = 2816
N_BUCKETS = 32
MAX_DISTANCE = 128
EPS = 1e-6
NEG = -1e30
QK_SCALE = 0.125
HALF_HEAD_PAIR = 64

ADAM_LR = 0.001
ADAM_B1 = 0.9
ADAM_B2 = 0.999
ADAM_EPS = 1e-08
ADAM_WD = 0.01
ADAM_STEP = 10

N_CHIPS = 4
TILE_CHUNKS = 2
TILE = TILE_CHUNKS * CHUNK
PROJ_TILE = 512
VMEM_LIMIT = 56 * 1024 * 1024

SMALL_A_ROWS = 8
ROW_LOSS = 4
ROW_WS = 0
ROW_BS = 512
ROW_SINK = 520
ROW_REL = 528
SMALL_B_ROWS = 536


def _mm(a, b):
    return lax.dot_general(a, b, (((1,), (0,)), ((), ())), preferred_element_type=F32)


def _mm_nt(a, b):
    return lax.dot_general(a, b, (((1,), (1,)), ((), ())), preferred_element_type=F32)


def _mm_tn(a, b):
    return lax.dot_general(a, b, (((0,), (0,)), ((), ())), preferred_element_type=F32)


def _bucket_map():
    qi = np.arange(CHUNK)[:, None]
    kj = np.arange(2 * CHUNK)[None, :]
    n = np.maximum(qi + CHUNK - kj, 0)
    max_exact = N_BUCKETS // 2
    large = max_exact + (np.log(np.maximum(n, 1) / max_exact) / np.log(MAX_DISTANCE / max_exact)
                         * (N_BUCKETS - max_exact)).astype(np.int32)
    large = np.minimum(large, N_BUCKETS - 1)
    return np.where(n < max_exact, n, large).astype(np.int32)


_GELU_C = 0.7978845608028654
_GELU_A = 0.044715
_GELU_K1 = 2.0 * _GELU_C
_GELU_K2 = 2.0 * _GELU_C * _GELU_A


def _gelu(x):
    x2 = x * x
    s = 1.0 / (1.0 + jnp.exp(x * (-_GELU_K1 - _GELU_K2 * x2)))
    return x * s, (s, x2)


def _gelu_grad(x, saved):
    s, x2 = saved
    return s + x * (s * (1.0 - s)) * (_GELU_K1 + 3.0 * _GELU_K2 * x2)


def _sigmoid(x):
    return 1.0 / (1.0 + jnp.exp(-x))


def _lane_lo(shape):
    return lax.broadcasted_iota(jnp.int32, shape, 1) < HALF_HEAD_PAIR


def _swa_variants(t):
    lo = _lane_lo(t.shape)
    tr = pltpu.roll(t, HALF_HEAD_PAIR, 1)
    zero = jnp.zeros_like(t)
    return (jnp.where(lo, t, zero).astype(BF16), jnp.where(lo, zero, tr).astype(BF16),
            jnp.where(lo, tr, zero).astype(BF16), jnp.where(lo, zero, t).astype(BF16))


def _swa_unvariants(d0, d1, d2, d3):
    lo = _lane_lo(d0.shape)
    zero = jnp.zeros_like(d0)
    rolled = jnp.where(lo, zero, d1) + jnp.where(lo, d2, zero)
    return jnp.where(lo, d0, zero) + jnp.where(lo, zero, d3) + pltpu.roll(rolled, HALF_HEAD_PAIR, 1)


def _mem_variants(t):
    out = []
    for pair in range(2):
        tp = t[:, pair * 128:(pair + 1) * 128]
        lo = _lane_lo(tp.shape)
        zero = jnp.zeros_like(tp)
        out.append(jnp.where(lo, tp, zero).astype(BF16))
        out.append(jnp.where(lo, zero, tp).astype(BF16))
    return out


def _mem_unvariants(d0, d1, d2, d3):
    lo = _lane_lo(d0.shape)
    return jnp.concatenate([jnp.where(lo, d0, d1), jnp.where(lo, d2, d3)], axis=-1)


def _softmax(logits, sinks):
    m = jnp.max(logits, axis=-1, keepdims=True)
    if sinks is not None:
        m = jnp.maximum(m, sinks)
    p = jnp.exp(logits - m)
    den = jnp.sum(p, axis=-1, keepdims=True)
    if sinks is None:
        return p * (1.0 / den), None
    es = jnp.exp(sinks - m)
    inv = 1.0 / (den + es)
    return p * inv, es * inv


def _band_valid(with_prev):
    qi = lax.broadcasted_iota(jnp.int32, (CHUNK, 2 * CHUNK), 0)
    kj = lax.broadcasted_iota(jnp.int32, (CHUNK, 2 * CHUNK), 1)
    in_cur = (kj >= CHUNK) & (kj - CHUNK <= qi)
    if not with_prev:
        return in_cur
    return in_cur | ((kj < CHUNK) & (kj > qi))


def _causal_weights(ws_ref):
    row = lax.broadcasted_iota(jnp.int32, (CHUNK, CHUNK), 0)
    col = lax.broadcasted_iota(jnp.int32, (CHUNK, CHUNK), 1)
    return [jnp.where(row >= col, ws_ref[g], 0.0).astype(BF16) for g in range(A_GROUPS)]


def _rows_to_lanes(a, n):
    return jnp.concatenate([a[c * CHUNK:(c + 1) * CHUNK] for c in range(n)], axis=1)


def _lanes_to_rows(a, n):
    w = a.shape[1] // n
    return jnp.concatenate([a[:, c * w:(c + 1) * w] for c in range(n)], axis=0)


def _stack_heads(pair01, pair23):
    return jnp.concatenate([pair01[:, :256], pair01[:, 256:], pair23[:, :256], pair23[:, 256:]], axis=0)


def _pair_heads(s, r):
    return (jnp.concatenate([s[0:r], s[r:2 * r]], axis=1), jnp.concatenate([s[2 * r:3 * r], s[3 * r:4 * r]], axis=1))


def _pair_operands(variants):
    return (jnp.concatenate(variants[0:2], axis=0), jnp.concatenate(variants[2:4], axis=0))


def _split_pair_grads(d_pairs):
    return d_pairs[0][:256], d_pairs[0][256:], d_pairs[1][:256], d_pairs[1][256:]


def _halves_bf16(a):
    return (a[:, :128].astype(BF16), a[:, 128:].astype(BF16))


def _group_a_forward(au, av, vg, vb, wm, bs_rows):
    gu, tu = _gelu(au)
    gv, tv = _gelu(av)
    ya, res = [], []
    for g in range(A_GROUPS):
        sl = slice(g * 128, (g + 1) * 128)
        xg = gv[:, sl]
        xc = xg - jnp.mean(xg, axis=-1, keepdims=True)
        rstd = lax.rsqrt(jnp.mean(xc * xc, axis=-1, keepdims=True) + EPS)
        xhat = xc * rstd
        vn = _rows_to_lanes((xhat * vg[:, sl] + vb[:, sl]).astype(BF16), TILE_CHUNKS)
        s = _lanes_to_rows(_mm(wm[g], vn), TILE_CHUNKS) + bs_rows[g]
        ya.append(gu[:, sl] * s)
        res.append((xhat, rstd, vn, s))
    return ya, dict(gu=gu, tu=tu, tv=tv, groups=res)


def _attention_logits(qp, k_pairs):
    return _stack_heads(_mm_nt(qp[0], k_pairs[0]), _mm_nt(qp[1], k_pairs[1]))


def _attention_out(p, v_pairs, r):
    pp = _pair_heads(p.astype(BF16), r)
    return jnp.concatenate([_mm(pp[0], v_pairs[0]), _mm(pp[1], v_pairs[1])], axis=-1), pp


def _attention_dprobs(do_pairs, v_pairs):
    return _stack_heads(_mm_nt(do_pairs[0], v_pairs[0]), _mm_nt(do_pairs[1], v_pairs[1]))


def _softmax_backward(p, dp):
    delta = jnp.sum(p * dp, axis=-1, keepdims=True)
    return p * (dp - delta), delta


def _attention_grads(dl, pp, do_pairs, qp, k_pairs, r):
    dlp = _pair_heads(dl.astype(BF16), r)
    dq = jnp.concatenate([_mm(dlp[0], k_pairs[0]), _mm(dlp[1], k_pairs[1])], axis=-1)
    dk = (_mm_tn(dlp[0], qp[0]), _mm_tn(dlp[1], qp[1]))
    dv = (_mm_tn(pp[0], do_pairs[0]), _mm_tn(pp[1], do_pairs[1]))
    return dq, dk, dv


def _tile_specs(n_tiles_ex, width):
    return pl.BlockSpec((TILE, width), lambda b, i: (b * n_tiles_ex + jnp.minimum(i, n_tiles_ex - 1), 0))


def _prev_chunk_spec(n_tiles_ex, width):
    def index(b, i):
        chunk = TILE_CHUNKS * jnp.minimum(i, n_tiles_ex - 1)
        return (b * n_tiles_ex * TILE_CHUNKS + jnp.maximum(chunk - 1, 0), 0)
    return pl.BlockSpec((CHUNK, width), index)


def _full_spec(shape):
    zeros = (0,) * len(shape)
    return pl.BlockSpec(shape, lambda *_: zeros)


SMEM_SPEC = pl.BlockSpec(memory_space=pltpu.SMEM)
ANY_SPEC = pl.BlockSpec(memory_space=pl.ANY)
VMEM_SPEC = pl.BlockSpec(memory_space=pltpu.VMEM)


def _fill_bias(rel_ref, bk_ref, out_ref):
    bk = bk_ref[...]
    for h in range(4):
        acc = jnp.zeros((CHUNK, 2 * CHUNK), F32)
        for b in range(N_BUCKETS):
            acc = jnp.where(bk == b, rel_ref[h, b], acc)
        for t, with_prev in enumerate((True, False)):
            out_ref[t, h * CHUNK:(h + 1) * CHUNK, :] = jnp.where(_band_valid(with_prev), acc, NEG)


def _memkv_forward(mem, g_mem, w_mkv):
    n_ex = mem.shape[0]

    def body(mem_ref, g_ref, w_ref, out_ref):
        m = mem_ref[0]
        r = lax.rsqrt(jnp.mean(m * m, axis=-1, keepdims=True) + EPS)
        out_ref[0] = _mm((m * r * g_ref[...]).astype(BF16), w_ref[...])

    return pl.pallas_call(
        body, name="memkv_forward", grid=(n_ex,),
        out_shape=jax.ShapeDtypeStruct((n_ex, MEM_LEN, 2 * MEM_WIDTH), F32),
        in_specs=[pl.BlockSpec((1, MEM_LEN, D_MODEL), lambda b: (b, 0, 0)), _full_spec((1, D_MODEL)),
                  _full_spec((D_MODEL, 2 * MEM_WIDTH))],
        out_specs=pl.BlockSpec((1, MEM_LEN, 2 * MEM_WIDTH), lambda b: (b, 0, 0)),
    )(mem, g_mem, w_mkv)


PROJ_WIDTHS = (A_WIDTH, A_WIDTH, SWA_WIDTH, KV_WIDTH, KV_WIDTH, MEM_WIDTH, MIX_WIDTH)
PROJ_OFFSETS = tuple(int(v) for v in np.cumsum((0,) + PROJ_WIDTHS))


MXU_TILE = 256
HALF_WIDTH = IN_WIDTH // 2
PHASE_COLS = (HALF_WIDTH // MXU_TILE * MXU_TILE, IN_WIDTH - HALF_WIDTH // MXU_TILE * MXU_TILE)


def _phase_columns(phase, chip_x):
    if phase == 0:
        return 0 if chip_x == 0 else IN_WIDTH - PHASE_COLS[0]
    return PHASE_COLS[0] if chip_x == 0 else 0


def _phase_parts(phase, chip_x):
    start = _phase_columns(phase, chip_x)
    return [(k, PROJ_OFFSETS[k] - start) for k in range(len(PROJ_WIDTHS))
            if start <= PROJ_OFFSETS[k] and PROJ_OFFSETS[k + 1] <= start + PHASE_COLS[phase]]


def _gather_and_project(x2, g_pre, w_in_s, w_mkv_s, w_out_s, rel_bias_t, buckets, x_arr):
    n_tok = x2.shape[0]
    n_tiles = n_tok // PROJ_TILE
    last = n_tiles - 1
    shapes = [w_in_s.shape, w_mkv_s.shape, w_out_s.shape]
    n_w = len(shapes)

    def body(x_sref, x_ref, g_ref, win_hbm, wmkv_hbm, wout_hbm, rel_ref, bk_ref, h_ref, *refs):
        part_refs, refs = refs[:len(PROJ_WIDTHS)], refs[len(PROJ_WIDTHS):]
        bias_ref, refs = refs[0], refs[1:]
        gin_hbm, gmkv_hbm, gout_hbm, wg, stage_in, stage_mkv, stage_out, own_mkv, own_out, h_all = refs[:10]
        send_sems, recv_sems, local_sems = refs[10:]
        p, t = pl.program_id(0), pl.program_id(1)
        x, y, c = lax.axis_index("x"), lax.axis_index("y"), lax.axis_index("c")
        me, sibling = (x, y, c), (x, y, 1 - c)
        my_shard = 2 * x + y
        gathered = [wg, gmkv_hbm, gout_hbm]

        def half_rows(w, shard, half):
            rows = shapes[w][0] // 2
            if w == 0:
                return wg.at[pl.ds(pl.multiple_of(shard * shapes[0][0] + half * rows, 16), rows), :]
            return gathered[w].at[shard, pl.ds(half * rows, rows), :]

        def first(w, rel):
            src = half_rows(w, my_shard, c) if w == 0 else (own_mkv, own_out)[w - 1].at[
                pl.ds(c * (shapes[w][0] // 2), shapes[w][0] // 2), :]
            k = 3 * w + rel - 1
            return pltpu.make_async_remote_copy(
                src_ref=src, dst_ref=half_rows(w, my_shard, c), send_sem=send_sems.at[k], recv_sem=recv_sems.at[k],
                device_id=(x ^ (rel >> 1), y ^ (rel & 1), c), device_id_type=MESH)

        def landed(w, rel):
            k = 3 * w + rel - 1
            ref = half_rows(w, my_shard ^ rel, c)
            return pltpu.make_async_remote_copy(src_ref=ref, dst_ref=ref, send_sem=send_sems.at[k],
                                                recv_sem=recv_sems.at[k], device_id=me, device_id_type=MESH)

        def passed(w, rel, half, to):
            k = 9 + 3 * w + rel - 1
            ref = half_rows(w, my_shard ^ rel, half)
            return pltpu.make_async_remote_copy(src_ref=ref, dst_ref=ref, send_sem=send_sems.at[k],
                                                recv_sem=recv_sems.at[k], device_id=to, device_id_type=MESH)

        def pass_on(w, rels):
            for rel in rels:
                landed(w, rel).wait_recv()
                passed(w, rel, c, sibling).start()
            for rel in rels:
                passed(w, rel, 1 - c, me).wait_recv()

        own_stores = [pltpu.make_async_copy(own_mkv, gmkv_hbm.at[my_shard], local_sems.at[3]),
                      pltpu.make_async_copy(own_out, gout_hbm.at[my_shard], local_sems.at[4])]

        @pl.when((p == 0) & (t == 0))
        def _():
            loads = [pltpu.make_async_copy(src, dst, local_sems.at[k]) for k, (src, dst) in enumerate(
                ((win_hbm, stage_in), (wmkv_hbm, stage_mkv), (wout_hbm, stage_out)))]
            for cp in loads:
                cp.start()
            loads[0].wait()
            wg[pl.ds(pl.multiple_of(my_shard * shapes[0][0], 16), shapes[0][0]), :] = stage_in[...].astype(BF16)
            for rel in (1, 2):
                first(0, rel).start()
            loads[1].wait()
            loads[2].wait()
            own_mkv[...] = stage_mkv[...].astype(BF16)
            own_out[...] = stage_out[...].astype(BF16)
            for cp in own_stores:
                cp.start()
            _fill_bias(rel_ref, bk_ref, bias_ref)
            pass_on(0, (1,))
            first(0, 3).start()

        @pl.when((p == 0) & (t == n_tiles // 2))
        def _():
            for w in (1, 2):
                for rel in (1, 2, 3):
                    first(w, rel).start()

        store = pltpu.make_async_copy(wg, gin_hbm, local_sems.at[5])

        @pl.when((p == 1) & (t == 0))
        def _():
            pass_on(0, (2, 3))
            store.start()

        @pl.when((p == 1) & (t == n_tiles // 2))
        def _():
            for w in (1, 2):
                pass_on(w, (1, 2, 3))

        tile_rows = pl.ds(pl.multiple_of(t * PROJ_TILE, PROJ_TILE), PROJ_TILE)

        def project(h, phase):
            start = jnp.where(x_sref[0] == 0, _phase_columns(phase, 0), _phase_columns(phase, 1))
            proj = _mm_nt(h, wg[pl.ds(pl.multiple_of(start, MXU_TILE), PHASE_COLS[phase]), :])
            for chip_x in range(2):
                @pl.when(x_sref[0] == chip_x)
                def _():
                    for k, lo in _phase_parts(phase, chip_x):
                        part_refs[k][...] = proj[:, lo:lo + PROJ_WIDTHS[k]]

        @pl.when(p == 0)
        def _():
            xv = x_ref[...]
            r = lax.rsqrt(jnp.mean(xv * xv, axis=-1, keepdims=True) + EPS)
            h = (xv * r * g_ref[...]).astype(BF16)
            h_ref[...] = h
            h_all[tile_rows, :] = h
            project(h, 0)

        @pl.when(p == 1)
        def _():
            project(h_all[tile_rows, :], 1)

        @pl.when((p == 1) & (t == last))
        def _():
            for w in range(n_w):
                for rel in (1, 2, 3):
                    first(w, rel).wait_send()
                    passed(w, rel, c, sibling).wait_send()
            for cp in own_stores:
                cp.wait()
            store.wait()

    def written_in(k):
        phase_on = [next(ph for ph in range(2) if k in dict(_phase_parts(ph, chip_x))) for chip_x in range(2)]

        def index(p, t, xs):
            phase = jnp.where(xs[0] == 0, phase_on[0], phase_on[1])
            return (jnp.where(p == phase, t, jnp.where(p < phase, 0, last)), 0)
        return index

    part_specs = [pl.BlockSpec((PROJ_TILE, PROJ_WIDTHS[k]), written_in(k)) for k in range(len(PROJ_WIDTHS))]
    vmem = pltpu.VMEM
    out = pl.pallas_call(
        body, name="gather_and_project",
        out_shape=[jax.ShapeDtypeStruct((n_tok, D_MODEL), BF16)]
        + [jax.ShapeDtypeStruct((n_tok, w), F32) for w in PROJ_WIDTHS]
        + [jax.ShapeDtypeStruct((2, 4 * CHUNK, 2 * CHUNK), F32)]
        + [jax.ShapeDtypeStruct((N_CHIPS * shapes[0][0], shapes[0][1]), BF16)]
        + [jax.ShapeDtypeStruct((N_CHIPS,) + s, BF16) for s in shapes[1:]],
        grid_spec=pltpu.PrefetchScalarGridSpec(
            num_scalar_prefetch=1, grid=(2, n_tiles),
            in_specs=[pl.BlockSpec((PROJ_TILE, D_MODEL), lambda p, t, xs: (jnp.where(p == 0, t, last), 0)),
                      pl.BlockSpec((1, D_MODEL), lambda p, t, xs: (0, 0)), ANY_SPEC, ANY_SPEC, ANY_SPEC, SMEM_SPEC,
                      pl.BlockSpec(buckets.shape, lambda p, t, xs: (0, 0))],
            out_specs=[pl.BlockSpec((PROJ_TILE, D_MODEL), lambda p, t, xs: (jnp.where(p == 0, t, last), 0))]
            + part_specs + [pl.BlockSpec((2, 4 * CHUNK, 2 * CHUNK), lambda p, t, xs: (0, 0, 0))] + [ANY_SPEC] * 3,
            scratch_shapes=[vmem((N_CHIPS * shapes[0][0], shapes[0][1]), BF16), vmem(shapes[0], F32),
                            vmem(shapes[1], F32), vmem(shapes[2], F32), vmem(shapes[1], BF16), vmem(shapes[2], BF16),
                            vmem((n_tok, D_MODEL), BF16),
                            pltpu.SemaphoreType.DMA((18,)), pltpu.SemaphoreType.DMA((18,)),
                            pltpu.SemaphoreType.DMA((6,))]),
        compiler_params=pltpu.CompilerParams(vmem_limit_bytes=VMEM_LIMIT),
    )(x_arr, x2, g_pre, w_in_s, w_mkv_s, w_out_s, rel_bias_t, buckets)
    n_parts = len(PROJ_WIDTHS)
    return out[0], list(out[1:1 + n_parts]), out[2 + n_parts:], out[1 + n_parts]


def _load_chunk(j, i, sk_ref, sv_ref, skp_ref, svp_ref):
    rows = slice(j * CHUNK, (j + 1) * CHUNK)
    if j == 0:
        k_prev, v_prev, table = skp_ref[...], svp_ref[...], jnp.where(i > 0, 0, 1)
    else:
        prev = slice((j - 1) * CHUNK, j * CHUNK)
        k_prev, v_prev, table = sk_ref[prev, :], sv_ref[prev, :], 0
    k_pairs = _pair_operands(_swa_variants(jnp.concatenate([k_prev, sk_ref[rows, :]], axis=0)))
    v_pairs = _pair_operands(_swa_variants(jnp.concatenate([v_prev, sv_ref[rows, :]], axis=0)))
    return rows, k_pairs, v_pairs, table


def _tile_constants(ws_ref, bs_ref, sink_ref, mkv_ref):
    wm = _causal_weights(ws_ref)
    bs_rows = [jnp.concatenate([bs_ref[g]] * TILE_CHUNKS, axis=0) for g in range(A_GROUPS)]
    sink_col = jnp.max(jnp.concatenate([jnp.full((CHUNK, 128), sink_ref[0, h], F32) for h in range(4)] * TILE_CHUNKS,
                                       axis=0), axis=-1, keepdims=True)
    mkv_v = mkv_ref[0]
    mk_pairs = _pair_operands(_mem_variants(mkv_v[:, :MEM_WIDTH]))
    mv_pairs = _pair_operands(_mem_variants(mkv_v[:, MEM_WIDTH:]))
    return wm, bs_rows, sink_col, mk_pairs, mv_pairs


def _mix(parts, mkv, x2, tgt2, v_g, v_b, w_sp, b_sp, sinks, bias, w_out, g_post, n_ex, seq):
    n_tiles_ex = seq // TILE
    n_tok = n_ex * seq
    au, av, sq, sk, sv, mq, z = parts
    col = dict(zip(("au", "av", "sq", "sk", "sv", "mq", "z"),
                   (slice(PROJ_OFFSETS[k], PROJ_OFFSETS[k + 1]) for k in range(len(PROJ_WIDTHS)))))
    before_kv, after_kv = slice(0, col["sk"].start), slice(col["sv"].stop, IN_WIDTH)

    def body(au_ref, av_ref, sq_ref, sk_ref, sv_ref, skp_ref, svp_ref, mq_ref, z_ref, mkv_ref, x_ref, tgt_ref,
             vg_ref, vb_ref, ws_ref, bs_ref, sink_ref, bias_ref, wout_ref, gpost_ref,
             dout_ref, dproj_ref, dmkv_ref, dwout_ref, dvg_ref, dvb_ref, dws_ref, dbs_ref, dsink_ref, drel_ref,
             loss_ref, dgpost_ref, carry_dp, carry_k, carry_v):
        b, i = pl.program_id(0), pl.program_id(1)

        @pl.when((b == 0) & (i == 0))
        def _():
            for ref in (dwout_ref, dvg_ref, dvb_ref, dws_ref, dbs_ref, dsink_ref, drel_ref, loss_ref, dgpost_ref):
                ref[...] = jnp.zeros_like(ref)

        @pl.when(i == 0)
        def _():
            dmkv_ref[...] = jnp.zeros_like(dmkv_ref)
            carry_k[...] = jnp.zeros_like(carry_k)
            carry_v[...] = jnp.zeros_like(carry_v)

        @pl.when(i > 0)
        def _():
            dproj_ref[:, before_kv] = carry_dp[:, before_kv]
            dproj_ref[:, after_kv] = carry_dp[:, after_kv]

        @pl.when(i < n_tiles_ex)
        def _():
            wm, bs_rows, sink_col, mk_pairs, mv_pairs = _tile_constants(ws_ref, bs_ref, sink_ref, mkv_ref)
            vg = vg_ref[...]

            au_v, av_v = au_ref[...], av_ref[...]
            ya, res = _group_a_forward(au_v, av_v, vg, vb_ref[...], wm, bs_rows)
            swa, logits, yb = [], [], []
            for j in range(TILE_CHUNKS):
                rows, k_pairs, v_pairs, table = _load_chunk(j, i, sk_ref, sv_ref, skp_ref, svp_ref)
                qp = _halves_bf16(sq_ref[rows, :] * QK_SCALE)
                logits.append(_attention_logits(qp, k_pairs) + bias_ref[table])
                swa.append([rows, k_pairs, v_pairs, qp])
            p_swa, sink_p = _softmax(jnp.concatenate(logits, axis=0), sink_col)
            for j in range(TILE_CHUNKS):
                out, pp = _attention_out(p_swa[j * 4 * CHUNK:(j + 1) * 4 * CHUNK], swa[j][2], CHUNK)
                yb.append(out)
                swa[j].append(pp)
            mqp = _halves_bf16(mq_ref[...] * QK_SCALE)
            pm, _ = _softmax(_attention_logits(mqp, mk_pairs), None)
            yc, ppm = _attention_out(pm, mv_pairs, TILE)
            ycat = jnp.concatenate(ya + [jnp.concatenate(yb, axis=0), yc], axis=-1)

            zv = z_ref[...]
            sig = _sigmoid(zv)
            sz = zv * sig
            y_b = (ycat * sz).astype(BF16)
            o = _mm(y_b, wout_ref[...])
            r2 = lax.rsqrt(jnp.mean(o * o, axis=-1, keepdims=True) + EPS)
            nrm = o * r2
            gp = gpost_ref[...]
            diff = x_ref[...] + nrm * gp - tgt_ref[...]
            loss_ref[...] += jnp.sum(diff * diff) * (0.5 / D_MODEL)
            dout = diff * (1.0 / D_MODEL)
            dout_ref[...] = dout
            dgpost_ref[...] += jnp.sum(dout * nrm, axis=0, keepdims=True)
            dn = dout * gp
            do_b = (r2 * (dn - nrm * jnp.mean(dn * nrm, axis=-1, keepdims=True))).astype(BF16)
            dwout_ref[...] += _mm_tn(y_b, do_b)
            dy = _mm_nt(do_b, wout_ref[...])
            carry_dp[:, col["z"]] = (dy * ycat * (sig * (1.0 + zv * (1.0 - sig)))).astype(BF16)
            dyc = dy * sz

            dgu, dgv = [], []
            for g in range(A_GROUPS):
                sl = slice(g * 128, (g + 1) * 128)
                xhat, rstd, vn, s = res["groups"][g]
                dya = dyc[:, sl]
                dgu.append(dya * s)
                ds = dya * res["gu"][:, sl]
                dbs_ref[:, sl] += sum(ds[c * CHUNK:(c + 1) * CHUNK] for c in range(TILE_CHUNKS))
                ds_b = _rows_to_lanes(ds.astype(BF16), TILE_CHUNKS)
                dws_ref[g] += _mm_nt(ds_b, vn)
                dvn = _lanes_to_rows(_mm_tn(wm[g], ds_b), TILE_CHUNKS)
                dvg_ref[:, sl] += jnp.sum(dvn * xhat, axis=0, keepdims=True)
                dvb_ref[:, sl] += jnp.sum(dvn, axis=0, keepdims=True)
                dxh = dvn * vg[:, sl]
                dgv.append(rstd * (dxh - jnp.mean(dxh, axis=-1, keepdims=True)
                                   - xhat * jnp.mean(dxh * xhat, axis=-1, keepdims=True)))
            carry_dp[:, col["au"]] = (jnp.concatenate(dgu, axis=-1) * _gelu_grad(au_v, res["tu"])).astype(BF16)
            carry_dp[:, col["av"]] = (jnp.concatenate(dgv, axis=-1) * _gelu_grad(av_v, res["tv"])).astype(BF16)

            do_pairs = [_halves_bf16(dyc[rows, A_WIDTH:A_WIDTH + SWA_WIDTH]) for rows, *_ in swa]
            dl_swa, delta = _softmax_backward(p_swa, jnp.concatenate(
                [_attention_dprobs(do_pairs[j], swa[j][2]) for j in range(TILE_CHUNKS)], axis=0))
            sink_terms = sink_p * delta
            lane4 = lax.broadcasted_iota(jnp.int32, (1, 128), 1)
            dsink_vec = jnp.zeros((1, 128), F32)
            for h in range(4):
                head_sum = sum(jnp.sum(sink_terms[(4 * j + h) * CHUNK:(4 * j + h + 1) * CHUNK])
                               for j in range(TILE_CHUNKS))
                dsink_vec = dsink_vec + jnp.where(lane4 == h, -head_sum, 0.0)
            dsink_ref[...] += dsink_vec
            drel_ref[...] += sum(dl_swa[j * 4 * CHUNK:(j + 1) * 4 * CHUNK] for j in range(TILE_CHUNKS))
            dk_parts, dv_parts = [], []
            for j, (rows, k_pairs, v_pairs, qp, pp) in enumerate(swa):
                dq, dk, dv = _attention_grads(dl_swa[j * 4 * CHUNK:(j + 1) * 4 * CHUNK], pp, do_pairs[j], qp, k_pairs,
                                              CHUNK)
                carry_dp[rows, col["sq"]] = (dq * QK_SCALE).astype(BF16)
                dk_parts.append(_swa_unvariants(*_split_pair_grads(dk)))
                dv_parts.append(_swa_unvariants(*_split_pair_grads(dv)))

            dc_pairs = _halves_bf16(dyc[:, A_WIDTH + SWA_WIDTH:])
            dl_mem, _ = _softmax_backward(pm, _attention_dprobs(dc_pairs, mv_pairs))
            dmq, dmk, dmv = _attention_grads(dl_mem, ppm, dc_pairs, mqp, mk_pairs, TILE)
            carry_dp[:, col["mq"]] = (dmq * QK_SCALE).astype(BF16)
            dmkv_ref[0] += jnp.concatenate([_mem_unvariants(*_split_pair_grads(dmk)),
                                            _mem_unvariants(*_split_pair_grads(dmv))], axis=-1)

            for parts_c, carry, cols in ((dk_parts, carry_k, col["sk"]), (dv_parts, carry_v, col["sv"])):
                @pl.when(i > 0)
                def _():
                    dproj_ref[:, cols] = (carry[...] + jnp.concatenate(
                        [jnp.zeros((TILE - CHUNK, KV_WIDTH), F32), parts_c[0][:CHUNK]], axis=0)).astype(BF16)
                new = [parts_c[0][CHUNK:]]
                for j in range(1, TILE_CHUNKS):
                    new[-1] = new[-1] + parts_c[j][:CHUNK]
                    new.append(parts_c[j][CHUNK:])
                carry[...] = jnp.concatenate(new, axis=0)

        @pl.when(i == n_tiles_ex)
        def _():
            dproj_ref[:, col["sk"]] = carry_k[...].astype(BF16)
            dproj_ref[:, col["sv"]] = carry_v[...].astype(BF16)

    tile = functools.partial(_tile_specs, n_tiles_ex)
    prev = functools.partial(_prev_chunk_spec, n_tiles_ex)
    late = pl.BlockSpec((TILE, IN_WIDTH), lambda b, i: (b * n_tiles_ex + jnp.maximum(i - 1, 0), 0))
    return pl.pallas_call(
        body, name="mix", grid=(n_ex, n_tiles_ex + 1),
        out_shape=[jax.ShapeDtypeStruct((n_tok, D_MODEL), F32), jax.ShapeDtypeStruct((n_tok, IN_WIDTH), BF16),
                   jax.ShapeDtypeStruct((n_ex, MEM_LEN, 2 * MEM_WIDTH), F32),
                   jax.ShapeDtypeStruct((MIX_WIDTH, D_MODEL), F32), jax.ShapeDtypeStruct((1, A_WIDTH), F32),
                   jax.ShapeDtypeStruct((1, A_WIDTH), F32), jax.ShapeDtypeStruct((A_GROUPS, CHUNK, CHUNK), F32),
                   jax.ShapeDtypeStruct((CHUNK, A_WIDTH), F32), jax.ShapeDtypeStruct((1, 128), F32),
                   jax.ShapeDtypeStruct((4 * CHUNK, 2 * CHUNK), F32), jax.ShapeDtypeStruct((1, 128), F32),
                   jax.ShapeDtypeStruct((1, D_MODEL), F32)],
        in_specs=[tile(A_WIDTH), tile(A_WIDTH), tile(SWA_WIDTH), tile(KV_WIDTH), tile(KV_WIDTH),
                  prev(KV_WIDTH), prev(KV_WIDTH), tile(MEM_WIDTH), tile(MIX_WIDTH),
                  pl.BlockSpec((1, MEM_LEN, 2 * MEM_WIDTH), lambda b, i: (b, 0, 0)),
                  tile(D_MODEL), tile(D_MODEL),
                  _full_spec((1, A_WIDTH)), _full_spec((1, A_WIDTH)), _full_spec((A_GROUPS, CHUNK, CHUNK)),
                  _full_spec((A_GROUPS, CHUNK, CHUNK)), SMEM_SPEC, _full_spec((2, 4 * CHUNK, 2 * CHUNK)),
                  _full_spec((MIX_WIDTH, D_MODEL)), _full_spec((1, D_MODEL))],
        out_specs=[tile(D_MODEL), late, pl.BlockSpec((1, MEM_LEN, 2 * MEM_WIDTH), lambda b, i: (b, 0, 0)),
                   _full_spec((MIX_WIDTH, D_MODEL)), _full_spec((1, A_WIDTH)), _full_spec((1, A_WIDTH)),
                   _full_spec((A_GROUPS, CHUNK, CHUNK)), _full_spec((CHUNK, A_WIDTH)), _full_spec((1, 128)),
                   _full_spec((4 * CHUNK, 2 * CHUNK)), _full_spec((1, 128)), _full_spec((1, D_MODEL))],
        scratch_shapes=[pltpu.VMEM((TILE, IN_WIDTH), BF16), pltpu.VMEM((TILE, KV_WIDTH), F32),
                        pltpu.VMEM((TILE, KV_WIDTH), F32)],
        compiler_params=pltpu.CompilerParams(vmem_limit_bytes=VMEM_LIMIT),
    )(au, av, sq, sk, sv, sk, sv, mq, z, mkv, x2, tgt2, v_g, v_b, w_sp, b_sp, sinks, bias, w_out, g_post)


BWD_PROJ_TILE = 512


def _backward_projection(x2, dout, dproj, g_pre, w_in_t):
    n_tok = x2.shape[0]
    n_steps = n_tok // BWD_PROJ_TILE

    def body(x_ref, dout_ref, dp_ref, g_ref, w_hbm, dx_ref, dgpre_ref, w_vmem, sem):
        @pl.when(pl.program_id(0) == 0)
        def _():
            load = pltpu.make_async_copy(w_hbm, w_vmem, sem)
            load.start()
            dgpre_ref[...] = jnp.zeros_like(dgpre_ref)
            load.wait()

        xv = x_ref[...]
        r = lax.rsqrt(jnp.mean(xv * xv, axis=-1, keepdims=True) + EPS)
        xn = xv * r
        dh = _mm(dp_ref[...], w_vmem[...])
        dgpre_ref[...] += jnp.sum(dh * xn, axis=0, keepdims=True)
        dhg = dh * g_ref[...]
        dx_ref[...] = r * (dhg - xn * jnp.mean(dhg * xn, axis=-1, keepdims=True)) + dout_ref[...]

    row = lambda w: pl.BlockSpec((BWD_PROJ_TILE, w), lambda i: (i, 0))
    return pl.pallas_call(
        body, name="backward_projection", grid=(n_steps,),
        out_shape=[jax.ShapeDtypeStruct((n_tok, D_MODEL), F32), jax.ShapeDtypeStruct((1, D_MODEL), F32)],
        in_specs=[row(D_MODEL), row(D_MODEL), row(IN_WIDTH), _full_spec((1, D_MODEL)), ANY_SPEC],
        out_specs=[row(D_MODEL), _full_spec((1, D_MODEL))],
        scratch_shapes=[pltpu.VMEM((IN_WIDTH, D_MODEL), BF16), pltpu.SemaphoreType.DMA],
        input_output_aliases={1: 0},
        compiler_params=pltpu.CompilerParams(vmem_limit_bytes=VMEM_LIMIT),
    )(x2, dout, dproj, g_pre, w_in_t)


SHARD_ROWS = IN_WIDTH // N_CHIPS
SHARD_WINDOW = 768
SHARD_HALF = SHARD_ROWS // 2
DWIN_TILE = 2048
N_REL = N_CHIPS - 1


def _shard_window_start(shard):
    return (shard * SHARD_ROWS // 128) * 128


def _reduce_gradients(dproj, h, big, small, shard_arr):
    n_tok = h.shape[0]
    tile = min(DWIN_TILE, n_tok)
    n_sub = n_tok // tile
    last = N_CHIPS - 1
    n_big, n_small = len(big), len(small)
    big_half = [g.shape[2:] for g in big]
    sem_big_d2d = 2 * N_CHIPS
    sem_big_ici = sem_big_d2d + n_big
    sem_big_swap = sem_big_ici + N_REL * n_big
    sem_small_d2d = sem_big_swap + n_big
    sem_small_ici = sem_small_d2d + n_small
    n_sems = sem_small_ici + N_REL * n_small
    loc_small = n_big
    loc_out_win = loc_small + n_small
    loc_out_big = loc_out_win + 2
    loc_out_small = loc_out_big + 2 * n_big
    n_local = loc_out_small + n_small

    def relation_of_slot(s):
        return (s + 2) % N_REL + 1

    def shard_of_slot(s, my_shard):
        return my_shard ^ jnp.where(s == last, 0, relation_of_slot(s))

    def body(shard_ref, dp_ref, h_hbm, *refs):
        h_vmem, h_sem, refs = refs[-2], refs[-1], refs[:-2]
        big_hbm, refs = refs[:n_big], refs[n_big:]
        small_hbm, refs = refs[:n_small], refs[n_small:]
        out_hbm, refs = refs[0], refs[1:]
        big_out, refs = refs[:n_big], refs[n_big:]
        small_out, refs = refs[:n_small], refs[n_small:]
        part, recv_d2d, send_ici, recv_ici, mine_buf, other_buf = refs[:6]
        refs = refs[6:]
        big_own, big_recv, big_send, big_land, big_mine, big_other = (
            refs[k * n_big:(k + 1) * n_big] for k in range(6))
        refs = refs[6 * n_big:]
        small_own, small_recv, small_all = (refs[k * n_small:(k + 1) * n_small] for k in range(3))
        send_sems, recv_sems, local_sems = refs[3 * n_small:]

        s, t = pl.program_id(0), pl.program_id(1)
        x, y, c = lax.axis_index("x"), lax.axis_index("y"), lax.axis_index("c")
        my_chip = 2 * x + y
        sibling = (x, y, 1 - c)
        my_rows = pl.ds(pl.multiple_of(c * SHARD_HALF, 8), SHARD_HALF)
        other_rows = pl.ds(pl.multiple_of((1 - c) * SHARD_HALF, 8), SHARD_HALF)

        def remote(src, dst, k, to):
            return pltpu.make_async_remote_copy(src_ref=src, dst_ref=dst, send_sem=send_sems.at[k],
                                                recv_sem=recv_sems.at[k], device_id=to, device_id_type=MESH)

        def chip_at(rel):
            return (x ^ (rel >> 1), y ^ (rel & 1), c)

        def to_sibling(k):
            return remote(part.at[k % 2, other_rows, :], recv_d2d.at[k], k, sibling)

        def to_chip(k):
            return remote(send_ici.at[k], recv_ici.at[k], N_CHIPS + k, chip_at(relation_of_slot(k)))

        swap = remote(mine_buf, other_buf, 2 * N_CHIPS - 1, sibling)
        big_load = [pltpu.make_async_copy(big_hbm[w].at[:, pl.ds(c, 1)], big_own[w], local_sems.at[w])
                    for w in range(n_big)]
        big_to_sibling = [remote(big_hbm[w].at[:, pl.ds(1 - c, 1)], big_recv[w], sem_big_d2d + w, sibling)
                          for w in range(n_big)]
        big_to_chip = [[remote(big_send[w].at[k], big_land[w].at[k], sem_big_ici + N_REL * w + k, chip_at(k + 1))
                        for k in range(N_REL)] for w in range(n_big)]
        big_swap = [remote(big_mine[w], big_other[w], sem_big_swap + w, sibling) for w in range(n_big)]
        small_load = [pltpu.make_async_copy(small_hbm[i], small_own[i], local_sems.at[loc_small + i])
                      for i in range(n_small)]
        small_to_sibling = [remote(small_hbm[i], small_recv[i], sem_small_d2d + i, sibling) for i in range(n_small)]
        small_to_chip = [[remote(small_all[i].at[my_chip], small_all[i].at[my_chip],
                                 sem_small_ici + N_REL * i + k, chip_at(k + 1))
                          for k in range(N_REL)] for i in range(n_small)]

        @pl.when((s == 0) & (t == 0))
        def _():
            h_load = pltpu.make_async_copy(h_hbm, h_vmem, h_sem)
            h_load.start()
            for cp in big_load + big_to_sibling + small_load + small_to_sibling:
                cp.start()
            h_load.wait()

        @pl.when((s == 0) & (t == n_sub - 1))
        def _():
            for cp in big_load + small_load:
                cp.wait()
            for cp in big_to_sibling + small_to_sibling:
                cp.wait_recv()
                cp.wait_send()
            for w in range(n_big):
                for k in range(N_REL):
                    shard = my_chip ^ (k + 1)
                    big_send[w][k] = (big_own[w][shard, 0] + big_recv[w][shard, 0]).astype(BF16)
                    big_to_chip[w][k].start()
            for i in range(n_small):
                small_all[i][my_chip] = small_own[i][...] + small_recv[i][...]
                for k in range(N_REL):
                    small_to_chip[i][k].start()

        @pl.when((s > 0) & (t == jnp.where(s == last, 0, min(1, n_sub - 1))))
        def _():
            k = s - 1
            cp = to_sibling(k)
            cp.wait_recv()
            cp.wait_send()
            send_ici[k] = (part[k % 2, my_rows, :] + recv_d2d[k]).astype(BF16)
            to_chip(k).start()

        def big_rows(w, half):
            rows = big_half[w][0]
            return big_out[w].at[pl.ds(pl.multiple_of(half * rows, 8), rows), :]

        big_store_mine = [pltpu.make_async_copy(big_mine[w], big_rows(w, c), local_sems.at[loc_out_big + 2 * w])
                          for w in range(n_big)]
        big_store_other = [pltpu.make_async_copy(big_other[w], big_rows(w, 1 - c),
                                                 local_sems.at[loc_out_big + 2 * w + 1]) for w in range(n_big)]
        small_store = [pltpu.make_async_copy(small_all[i], small_out[i], local_sems.at[loc_out_small + i])
                       for i in range(n_small)]

        @pl.when((s == last) & (t == 0))
        def _():
            for w in range(n_big):
                total = big_own[w][my_chip, 0] + big_recv[w][my_chip, 0]
                for k in range(N_REL):
                    big_to_chip[w][k].wait_recv()
                    total = total + big_land[w][k].astype(F32)
                big_mine[w][...] = total
                big_swap[w].start()
                big_store_mine[w].start()
            for i in range(n_small):
                for k in range(N_REL):
                    small_to_chip[i][k].wait_recv()
                small_store[i].start()

        r = _mm_tn(dp_ref[...], h_vmem[pl.ds(pl.multiple_of(t * tile, tile), tile), :])
        odd = shard_of_slot(s, shard_ref[0]) % 2
        for parity in range(2):
            rows = r[64 * parity:64 * parity + SHARD_ROWS]

            @pl.when((odd == parity) & (t == 0))
            def _():
                part[s % 2] = rows

            @pl.when((odd == parity) & (t > 0))
            def _():
                part[s % 2] += rows

        @pl.when(t == n_sub - 1)
        def _():
            to_sibling(s).start()

        @pl.when((s == last) & (t == n_sub - 1))
        def _():
            cp = to_sibling(last)
            cp.wait_recv()
            cp.wait_send()
            total = part[last % 2, my_rows, :] + recv_d2d[last]
            for k in range(last):
                to_chip(k).wait_recv()
                total = total + recv_ici[k].astype(F32)
            mine_buf[...] = total
            swap.start()
            out_mine = pltpu.make_async_copy(mine_buf, out_hbm.at[my_rows, :], local_sems.at[0])
            out_mine.start()
            swap.wait_recv()
            out_other = pltpu.make_async_copy(other_buf, out_hbm.at[other_rows, :], local_sems.at[1])
            out_other.start()
            for w in range(n_big):
                big_swap[w].wait_recv()
                big_store_other[w].start()
            stores = [out_mine, out_other] + big_store_mine + big_store_other + small_store
            for k in range(last):
                to_chip(k).wait_send()
            swap.wait_send()
            for w in range(n_big):
                for k in range(N_REL):
                    big_to_chip[w][k].wait_send()
                big_swap[w].wait_send()
            for i in range(n_small):
                for k in range(N_REL):
                    small_to_chip[i][k].wait_send()
            for cp in stores:
                cp.wait()

    half = (SHARD_HALF, D_MODEL)
    vmem = pltpu.VMEM
    scratch = [vmem((2, SHARD_ROWS, D_MODEL), F32), vmem((N_CHIPS,) + half, F32),
               vmem((N_REL,) + half, BF16), vmem((N_REL,) + half, BF16), vmem(half, F32), vmem(half, F32)]
    scratch += [vmem((N_CHIPS, 1) + hs, F32) for hs in big_half] * 2
    scratch += [vmem((N_REL,) + hs, BF16) for hs in big_half] * 2
    scratch += [vmem(hs, F32) for hs in big_half] * 2
    scratch += [vmem(a.shape, F32) for a in small] * 2 + [vmem((N_CHIPS,) + a.shape, F32) for a in small]
    scratch += [pltpu.SemaphoreType.DMA((n_sems,)), pltpu.SemaphoreType.DMA((n_sems,)),
                pltpu.SemaphoreType.DMA((n_local,)), vmem(h.shape, BF16), pltpu.SemaphoreType.DMA]
    n_hbm = n_big + n_small
    out = pl.pallas_call(
        body, name="reduce_gradients",
        out_shape=[jax.ShapeDtypeStruct((SHARD_ROWS, D_MODEL), F32)]
        + [jax.ShapeDtypeStruct((2 * hs[0], hs[1]), F32) for hs in big_half]
        + [jax.ShapeDtypeStruct((N_CHIPS,) + a.shape, F32) for a in small],
        grid_spec=pltpu.PrefetchScalarGridSpec(
            num_scalar_prefetch=1, grid=(N_CHIPS, n_sub),
            in_specs=[pl.BlockSpec((pl.Element(tile), pl.Element(SHARD_WINDOW)),
                                   lambda s, t, m: (t * tile, _shard_window_start(shard_of_slot(s, m[0])))),
                      ANY_SPEC] + [ANY_SPEC] * n_hbm,
            out_specs=[ANY_SPEC] * (1 + n_hbm),
            scratch_shapes=scratch),
        compiler_params=pltpu.CompilerParams(vmem_limit_bytes=VMEM_LIMIT),
    )(shard_arr, dproj, h, *big, *small)
    return out[:1 + n_big], out[1 + n_big:]


def _memkv_backward(mem, dmkv, g_mem, w_mkv):
    n_ex = mem.shape[0]

    def body(mem_ref, d_ref, g_ref, w_ref, dw_ref, dg_ref):
        @pl.when(pl.program_id(0) == 0)
        def _():
            dw_ref[...] = jnp.zeros_like(dw_ref)
            dg_ref[...] = jnp.zeros_like(dg_ref)

        m = mem_ref[0]
        mn = m * lax.rsqrt(jnp.mean(m * m, axis=-1, keepdims=True) + EPS)
        d_b = d_ref[0].astype(BF16)
        dw_ref[...] += _mm_tn((mn * g_ref[...]).astype(BF16), d_b)
        dg_ref[...] += jnp.sum(_mm_nt(d_b, w_ref[...]) * mn, axis=0, keepdims=True)

    return pl.pallas_call(
        body, name="memkv_backward", grid=(n_ex,),
        out_shape=[jax.ShapeDtypeStruct((D_MODEL, 2 * MEM_WIDTH), F32), jax.ShapeDtypeStruct((1, D_MODEL), F32)],
        in_specs=[pl.BlockSpec((1, MEM_LEN, D_MODEL), lambda b: (b, 0, 0)),
                  pl.BlockSpec((1, MEM_LEN, 2 * MEM_WIDTH), lambda b: (b, 0, 0)),
                  _full_spec((1, D_MODEL)), _full_spec((D_MODEL, 2 * MEM_WIDTH))],
        out_specs=[_full_spec((D_MODEL, 2 * MEM_WIDTH)), _full_spec((1, D_MODEL))],
    )(mem, dmkv, g_mem, w_mkv)


def _pack_small_grads(dgpre, dgpost, dgmem, dvg, dvb, dws, dbs, dsink, drel, loss_vec, buckets):
    def body(dgpre_ref, dgpost_ref, dgmem_ref, dvg_ref, dvb_ref, dws_ref, dbs_ref, dsink_ref, drel_ref, loss_ref,
             bk_ref, a_ref, b_ref):
        a_ref[...] = jnp.zeros_like(a_ref)
        b_ref[...] = jnp.zeros_like(b_ref)
        a_ref[0:1, :] = dgpre_ref[...]
        a_ref[1:2, :] = dgpost_ref[...]
        a_ref[2:3, :] = dgmem_ref[...]
        a_ref[3:4, :] = jnp.concatenate([dvg_ref[...], dvb_ref[...]], axis=-1)
        a_ref[ROW_LOSS:ROW_LOSS + 1, 0:128] = loss_ref[...]
        row = lax.broadcasted_iota(jnp.int32, (CHUNK, CHUNK), 0)
        col = lax.broadcasted_iota(jnp.int32, (CHUNK, CHUNK), 1)
        for g in range(A_GROUPS):
            b_ref[ROW_WS + g * CHUNK:ROW_WS + (g + 1) * CHUNK, :] = jnp.where(row >= col, dws_ref[g], 0.0)
            by_token = jnp.transpose(dbs_ref[:, g * 128:(g + 1) * 128])
            b_ref[ROW_BS + g:ROW_BS + g + 1, :] = jnp.sum(by_token, axis=0, keepdims=True)
        b_ref[ROW_SINK:ROW_SINK + 1, :] = dsink_ref[...]
        bk = bk_ref[...]
        rel_row = lax.broadcasted_iota(jnp.int32, (8, 128), 0)
        rel_col = lax.broadcasted_iota(jnp.int32, (8, 128), 1)
        rel = jnp.zeros((8, 128), F32)
        for h in range(4):
            acc = drel_ref[h * CHUNK:(h + 1) * CHUNK, :]
            for b in range(N_BUCKETS):
                rel = jnp.where((rel_row == h) & (rel_col == b), jnp.sum(jnp.where(bk == b, acc, 0.0)), rel)
        b_ref[ROW_REL:ROW_REL + 8, :] = rel

    return pl.pallas_call(
        body, name="pack_small_grads",
        out_shape=[jax.ShapeDtypeStruct((SMALL_A_ROWS, D_MODEL), F32), jax.ShapeDtypeStruct((SMALL_B_ROWS, 128), F32)],
        in_specs=[VMEM_SPEC] * 11, out_specs=[VMEM_SPEC] * 2,
    )(dgpre, dgpost, dgmem, dvg, dvb, dws, dbs, dsink, drel, loss_vec, buckets)


def _adamw(w, g, m, v):
    m2 = ADAM_B1 * m + (1.0 - ADAM_B1) * g
    v2 = ADAM_B2 * v + (1.0 - ADAM_B2) * (g * g)
    m_hat = m2 / (1.0 - ADAM_B1 ** ADAM_STEP)
    v_hat = v2 / (1.0 - ADAM_B2 ** ADAM_STEP)
    delta = -ADAM_LR * (m_hat / (jnp.sqrt(v_hat) + ADAM_EPS) + ADAM_WD * w)
    return delta, m2, v2


ADAM_STEPS = 4


def _adamw_all(shard_grads, shard_w, shard_m, shard_v, ra, rb, small_w, small_m, small_v):
    n_sh, n = len(shard_w), len(small_w)

    def body(*refs):
        sh_in, refs = refs[:4 * n_sh], refs[4 * n_sh:]
        ra_ref, rb_ref, refs = refs[0], refs[1], refs[2:]
        w_refs, m_refs, v_refs, refs = refs[:n], refs[n:2 * n], refs[2 * n:3 * n], refs[3 * n:]
        sh_out, outs = refs[:4 * n_sh], refs[4 * n_sh:]
        for k in range(n_sh):
            g = sh_in[k][...]
            delta, m2, v2 = _adamw(sh_in[n_sh + k][...], g, sh_in[2 * n_sh + k][...], sh_in[3 * n_sh + k][...])
            for ref, val in zip(sh_out[4 * k:4 * k + 4], (g, delta, m2, v2)):
                ref[...] = val

        @pl.when(pl.program_id(0) == 0)
        def _():
            g_outs, d_outs, m_outs, v_outs = outs[:n], outs[n:2 * n], outs[2 * n:3 * n], outs[3 * n:4 * n]
            ga, gb = ra_ref[0], rb_ref[0]
            for chip in range(1, N_CHIPS):
                ga = ga + ra_ref[chip]
                gb = gb + rb_ref[chip]
            outs[4 * n][...] = ga[ROW_LOSS:ROW_LOSS + 1, 0:128]
            grads = [ga[0:1, :], ga[1:2, :], ga[2:3, :], ga[3:4, :A_WIDTH], ga[3:4, A_WIDTH:],
                     gb[ROW_WS:ROW_WS + A_GROUPS * CHUNK, :].reshape(A_GROUPS, CHUNK, CHUNK),
                     gb[ROW_BS:ROW_BS + A_GROUPS, :], gb[ROW_SINK:ROW_SINK + 1, 0:4],
                     gb[ROW_REL:ROW_REL + 4, 0:N_BUCKETS]]
            for k in range(n):
                delta, m2, v2 = _adamw(w_refs[k][...], grads[k], m_refs[k][...], v_refs[k][...])
                g_outs[k][...] = grads[k]
                d_outs[k][...] = delta
                m_outs[k][...] = m2
                v_outs[k][...] = v2

    def rows_block(a):
        assert a.shape[0] % (8 * ADAM_STEPS) == 0
        return pl.BlockSpec((a.shape[0] // ADAM_STEPS, a.shape[1]), lambda i: (i, 0))

    sh_specs = [rows_block(w) for w in shard_w]
    small_in = [ra, rb, *small_w, *small_m, *small_v]
    small_out_shapes = [jax.ShapeDtypeStruct(w.shape, F32) for w in small_w] * 4 + [jax.ShapeDtypeStruct((1, 128), F32)]
    out = pl.pallas_call(
        body, name="adamw_all", grid=(ADAM_STEPS,),
        out_shape=[jax.ShapeDtypeStruct(w.shape, F32) for w in shard_w for _ in range(4)] + small_out_shapes,
        in_specs=sh_specs * 4 + [_full_spec(a.shape) for a in small_in],
        out_specs=[spec for spec in sh_specs for _ in range(4)] + [_full_spec(s.shape) for s in small_out_shapes],
        compiler_params=pltpu.CompilerParams(vmem_limit_bytes=VMEM_LIMIT),
    )(*shard_grads, *shard_w, *shard_m, *shard_v, *small_in)
    return [out[4 * k:4 * k + 4] for k in range(n_sh)], out[4 * n_sh:]


def kernel(x, mem, pre_norm_g, post_norm_g, mem_norm_g, w_in, w_mem_kv, v_norm_g, v_norm_b, w_spatial, b_spatial, attn_sinks, rel_bias, w_out, loss_target, m_pre_norm_g, m_post_norm_g, m_mem_norm_g, m_w_in, m_w_mem_kv, m_v_norm_g, m_v_norm_b, m_w_spatial, m_b_spatial, m_attn_sinks, m_rel_bias, m_w_out, v_pre_norm_g, v_post_norm_g, v_mem_norm_g, v_w_in, v_w_mem_kv, v_v_norm_g, v_v_norm_b, v_w_spatial, v_b_spatial, v_attn_sinks, v_rel_bias, v_w_out):
    n_ex, seq, _ = x.shape
    n_tok = n_ex * seq
    x2 = x.reshape(n_tok, D_MODEL)
    tgt2 = loss_target.reshape(n_tok, D_MODEL)
    buckets = jnp.asarray(_bucket_map())
    shard_arr = (2 * lax.axis_index("x") + lax.axis_index("y")).astype(jnp.int32).reshape(1)
    w_sp = w_spatial[0]
    b_sp = jnp.broadcast_to(b_spatial[0][:, :, None], (A_GROUPS, CHUNK, CHUNK))
    w_in_t, m_w_in_t, v_w_in_t = (jnp.transpose(a[0]) for a in (w_in, m_w_in, v_w_in))
    rel_t, m_rel_t, v_rel_t = (jnp.transpose(a) for a in (rel_bias, m_rel_bias, v_rel_bias))

    x_arr = lax.axis_index("x").astype(jnp.int32).reshape(1)
    h_b, parts, (w_in_b, g_mkv, g_out), bias = _gather_and_project(
        x2, pre_norm_g, w_in_t, w_mem_kv[0], w_out[0], rel_t, buckets, x_arr)
    w_mkv_b = g_mkv.reshape(D_MODEL, 2 * MEM_WIDTH)
    w_out_b = g_out.reshape(MIX_WIDTH, D_MODEL)

    mkv = _memkv_forward(mem, mem_norm_g, w_mkv_b)
    dout, dproj, dmkv, dwout, dvg, dvb, dws, dbs, dsink, drel, loss_vec, dgpost = _mix(
        parts, mkv, x2, tgt2, v_norm_g, v_norm_b, w_sp, b_sp, attn_sinks, bias, w_out_b, post_norm_g, n_ex, seq)

    dx, dgpre = _backward_projection(x2, dout, dproj, pre_norm_g, w_in_b)
    dwmkv, dgmem = _memkv_backward(mem, dmkv, mem_norm_g, w_mkv_b)
    small_a, small_b = _pack_small_grads(dgpre, dgpost, dgmem, dvg, dvb, dws, dbs, dsink, drel, loss_vec, buckets)

    shard_shapes = [w_mem_kv.shape[1:], w_out.shape[1:]]
    big = [g.reshape(N_CHIPS, 2, s[0] // 2, s[1]) for g, s in zip((dwmkv, dwout), shard_shapes)]
    (g_win, g_wmkv, g_wout), (ga, gb) = _reduce_gradients(dproj, h_b, big, [small_a, small_b], shard_arr)

    small_w = [pre_norm_g, post_norm_g, mem_norm_g, v_norm_g, v_norm_b, w_sp, b_spatial[0], attn_sinks, rel_t]
    small_m = [m_pre_norm_g, m_post_norm_g, m_mem_norm_g, m_v_norm_g, m_v_norm_b, m_w_spatial[0], m_b_spatial[0],
               m_attn_sinks, m_rel_t]
    small_v = [v_pre_norm_g, v_post_norm_g, v_mem_norm_g, v_v_norm_g, v_v_norm_b, v_w_spatial[0], v_b_spatial[0],
               v_attn_sinks, v_rel_t]
    big_out, small_out = _adamw_all(
        [g_win, g_wmkv, g_wout], [w_in_t, w_mem_kv[0], w_out[0]], [m_w_in_t, m_w_mem_kv[0], m_w_out[0]],
        [v_w_in_t, v_w_mem_kv[0], v_w_out[0]], ga, gb, small_w, small_m, small_v)
    n_small = len(small_w)

    outputs = [small_out[4 * n_small][0, 0], dx.reshape(x.shape)]
    for kind in range(4):
        s = small_out[kind * n_small:(kind + 1) * n_small]
        outputs += [s[0], s[1], s[2], jnp.transpose(big_out[0][kind])[None], big_out[1][kind][None], s[3], s[4],
                    s[5][None], s[6][None], s[7], jnp.transpose(s[8]), big_out[2][kind][None]]
    return tuple(outputs)
```

```python
import functools

import numpy as np
import jax
import jax.numpy as jnp
from jax import lax
from jax.experimental import pallas as pl
from jax.experimental.pallas import tpu as pltpu

F32 = jnp.float32
BF16 = jnp.bfloat16
MESH = pl.DeviceIdType.MESH

D_MODEL = 1024
CHUNK = 128
A_WIDTH = 512
A_GROUPS = 4
SWA_WIDTH = 256
KV_WIDTH = 128
MEM_WIDTH = 256
MEM_LEN = 256
MIX_WIDTH = 1024
IN_WIDTH = 2816
N_BUCKETS = 32
MAX_DISTANCE = 128
EPS = 1e-6
NEG = -1e30
QK_SCALE = 0.125
HALF_HEAD_PAIR = 64

ADAM_LR = 0.001
ADAM_B1 = 0.9
ADAM_B2 = 0.999
ADAM_EPS = 1e-08
ADAM_WD = 0.01
ADAM_STEP = 10

N_CHIPS = 4
TILE_CHUNKS = 2
TILE = TILE_CHUNKS * CHUNK
PROJ_TILE = 512
VMEM_LIMIT = 56 * 1024 * 1024

SMALL_A_ROWS = 8
ROW_LOSS = 4
ROW_WS = 0
ROW_BS = 512
ROW_SINK = 520
ROW_REL = 528
SMALL_B_ROWS = 536


def _mm(a, b):
    return lax.dot_general(a, b, (((1,), (0,)), ((), ())), preferred_element_type=F32)


def _mm_nt(a, b):
    return lax.dot_general(a, b, (((1,), (1,)), ((), ())), preferred_element_type=F32)


def _mm_tn(a, b):
    return lax.dot_general(a, b, (((0,), (0,)), ((), ())), preferred_element_type=F32)


def _bucket_map():
    qi = np.arange(CHUNK)[:, None]
    kj = np.arange(2 * CHUNK)[None, :]
    n = np.maximum(qi + CHUNK - kj, 0)
    max_exact = N_BUCKETS // 2
    large = max_exact + (np.log(np.maximum(n, 1) / max_exact) / np.log(MAX_DISTANCE / max_exact)
                         * (N_BUCKETS - max_exact)).astype(np.int32)
    large = np.minimum(large, N_BUCKETS - 1)
    return np.where(n < max_exact, n, large).astype(np.int32)


_GELU_C = 0.7978845608028654
_GELU_A = 0.044715
_GELU_K1 = 2.0 * _GELU_C
_GELU_K2 = 2.0 * _GELU_C * _GELU_A


def _gelu(x):
    x2 = x * x
    s = 1.0 / (1.0 + jnp.exp(x * (-_GELU_K1 - _GELU_K2 * x2)))
    return x * s, (s, x2)


def _gelu_grad(x, saved):
    s, x2 = saved
    return s + x * (s * (1.0 - s)) * (_GELU_K1 + 3.0 * _GELU_K2 * x2)


def _sigmoid(x):
    return 1.0 / (1.0 + jnp.exp(-x))


def _lane_lo(shape):
    return lax.broadcasted_iota(jnp.int32, shape, 1) < HALF_HEAD_PAIR


def _swa_variants(t):
    lo = _lane_lo(t.shape)
    tr = pltpu.roll(t, HALF_HEAD_PAIR, 1)
    zero = jnp.zeros_like(t)
    return (jnp.where(lo, t, zero).astype(BF16), jnp.where(lo, zero, tr).astype(BF16),
            jnp.where(lo, tr, zero).astype(BF16), jnp.where(lo, zero, t).astype(BF16))


def _swa_unvariants(d0, d1, d2, d3):
    lo = _lane_lo(d0.shape)
    zero = jnp.zeros_like(d0)
    rolled = jnp.where(lo, zero, d1) + jnp.where(lo, d2, zero)
    return jnp.where(lo, d0, zero) + jnp.where(lo, zero, d3) + pltpu.roll(rolled, HALF_HEAD_PAIR, 1)


def _mem_variants(t):
    out = []
    for pair in range(2):
        tp = t[:, pair * 128:(pair + 1) * 128]
        lo = _lane_lo(tp.shape)
        zero = jnp.zeros_like(tp)
        out.append(jnp.where(lo, tp, zero).astype(BF16))
        out.append(jnp.where(lo, zero, tp).astype(BF16))
    return out


def _mem_unvariants(d0, d1, d2, d3):
    lo = _lane_lo(d0.shape)
    return jnp.concatenate([jnp.where(lo, d0, d1), jnp.where(lo, d2, d3)], axis=-1)


def _softmax(logits, sinks):
    m = jnp.max(logits, axis=-1, keepdims=True)
    if sinks is not None:
        m = jnp.maximum(m, sinks)
    p = jnp.exp(logits - m)
    den = jnp.sum(p, axis=-1, keepdims=True)
    if sinks is None:
        return p * (1.0 / den), None
    es = jnp.exp(sinks - m)
    inv = 1.0 / (den + es)
    return p * inv, es * inv


def _band_valid(with_prev):
    qi = lax.broadcasted_iota(jnp.int32, (CHUNK, 2 * CHUNK), 0)
    kj = lax.broadcasted_iota(jnp.int32, (CHUNK, 2 * CHUNK), 1)
    in_cur = (kj >= CHUNK) & (kj - CHUNK <= qi)
    if not with_prev:
        return in_cur
    return in_cur | ((kj < CHUNK) & (kj > qi))


def _causal_weights(ws_ref):
    row = lax.broadcasted_iota(jnp.int32, (CHUNK, CHUNK), 0)
    col = lax.broadcasted_iota(jnp.int32, (CHUNK, CHUNK), 1)
    return [jnp.where(row >= col, ws_ref[g], 0.0).astype(BF16) for g in range(A_GROUPS)]


def _rows_to_lanes(a, n):
    return jnp.concatenate([a[c * CHUNK:(c + 1) * CHUNK] for c in range(n)], axis=1)


def _lanes_to_rows(a, n):
    w = a.shape[1] // n
    return jnp.concatenate([a[:, c * w:(c + 1) * w] for c in range(n)], axis=0)


def _stack_heads(pair01, pair23):
    return jnp.concatenate([pair01[:, :256], pair01[:, 256:], pair23[:, :256], pair23[:, 256:]], axis=0)


def _pair_heads(s, r):
    return (jnp.concatenate([s[0:r], s[r:2 * r]], axis=1), jnp.concatenate([s[2 * r:3 * r], s[3 * r:4 * r]], axis=1))


def _pair_operands(variants):
    return (jnp.concatenate(variants[0:2], axis=0), jnp.concatenate(variants[2:4], axis=0))


def _split_pair_grads(d_pairs):
    return d_pairs[0][:256], d_pairs[0][256:], d_pairs[1][:256], d_pairs[1][256:]


def _halves_bf16(a):
    return (a[:, :128].astype(BF16), a[:, 128:].astype(BF16))


def _group_a_forward(au, av, vg, vb, wm, bs_rows):
    gu, tu = _gelu(au)
    gv, tv = _gelu(av)
    ya, res = [], []
    for g in range(A_GROUPS):
        sl = slice(g * 128, (g + 1) * 128)
        xg = gv[:, sl]
        xc = xg - jnp.mean(xg, axis=-1, keepdims=True)
        rstd = lax.rsqrt(jnp.mean(xc * xc, axis=-1, keepdims=True) + EPS)
        xhat = xc * rstd
        vn = _rows_to_lanes((xhat * vg[:, sl] + vb[:, sl]).astype(BF16), TILE_CHUNKS)
        s = _lanes_to_rows(_mm(wm[g], vn), TILE_CHUNKS) + bs_rows[g]
        ya.append(gu[:, sl] * s)
        res.append((xhat, rstd, vn, s))
    return ya, dict(gu=gu, tu=tu, tv=tv, groups=res)


def _attention_logits(qp, k_pairs):
    return _stack_heads(_mm_nt(qp[0], k_pairs[0]), _mm_nt(qp[1], k_pairs[1]))


def _attention_out(p, v_pairs, r):
    pp = _pair_heads(p.astype(BF16), r)
    return jnp.concatenate([_mm(pp[0], v_pairs[0]), _mm(pp[1], v_pairs[1])], axis=-1), pp


def _attention_dprobs(do_pairs, v_pairs):
    return _stack_heads(_mm_nt(do_pairs[0], v_pairs[0]), _mm_nt(do_pairs[1], v_pairs[1]))


def _softmax_backward(p, dp):
    delta = jnp.sum(p * dp, axis=-1, keepdims=True)
    return p * (dp - delta), delta


def _attention_grads(dl, pp, do_pairs, qp, k_pairs, r):
    dlp = _pair_heads(dl.astype(BF16), r)
    dq = jnp.concatenate([_mm(dlp[0], k_pairs[0]), _mm(dlp[1], k_pairs[1])], axis=-1)
    dk = (_mm_tn(dlp[0], qp[0]), _mm_tn(dlp[1], qp[1]))
    dv = (_mm_tn(pp[0], do_pairs[0]), _mm_tn(pp[1], do_pairs[1]))
    return dq, dk, dv


def _tile_specs(n_tiles_ex, width):
    return pl.BlockSpec((TILE, width), lambda b, i: (b * n_tiles_ex + jnp.minimum(i, n_tiles_ex - 1), 0))


def _prev_chunk_spec(n_tiles_ex, width):
    def index(b, i):
        chunk = TILE_CHUNKS * jnp.minimum(i, n_tiles_ex - 1)
        return (b * n_tiles_ex * TILE_CHUNKS + jnp.maximum(chunk - 1, 0), 0)
    return pl.BlockSpec((CHUNK, width), index)


def _full_spec(shape):
    zeros = (0,) * len(shape)
    return pl.BlockSpec(shape, lambda *_: zeros)


SMEM_SPEC = pl.BlockSpec(memory_space=pltpu.SMEM)
ANY_SPEC = pl.BlockSpec(memory_space=pl.ANY)
VMEM_SPEC = pl.BlockSpec(memory_space=pltpu.VMEM)


def _fill_bias(rel_ref, bk_ref, out_ref):
    bk = bk_ref[...]
    for h in range(4):
        acc = jnp.zeros((CHUNK, 2 * CHUNK), F32)
        for b in range(N_BUCKETS):
            acc = jnp.where(bk == b, rel_ref[h, b], acc)
        for t, with_prev in enumerate((True, False)):
            out_ref[t, h * CHUNK:(h + 1) * CHUNK, :] = jnp.where(_band_valid(with_prev), acc, NEG)


PROJ_WIDTHS = (A_WIDTH, A_WIDTH, SWA_WIDTH, KV_WIDTH, KV_WIDTH, MEM_WIDTH, MIX_WIDTH)
PROJ_OFFSETS = tuple(int(v) for v in np.cumsum((0,) + PROJ_WIDTHS))


MXU_TILE = 256
HALF_WIDTH = IN_WIDTH // 2
PHASE_COLS = (HALF_WIDTH // MXU_TILE * MXU_TILE, IN_WIDTH - HALF_WIDTH // MXU_TILE * MXU_TILE)


def _phase_columns(phase, chip_x):
    if phase == 0:
        return 0 if chip_x == 0 else IN_WIDTH - PHASE_COLS[0]
    return PHASE_COLS[0] if chip_x == 0 else 0


def _phase_parts(phase, chip_x):
    start = _phase_columns(phase, chip_x)
    return [(k, PROJ_OFFSETS[k] - start) for k in range(len(PROJ_WIDTHS))
            if start <= PROJ_OFFSETS[k] and PROJ_OFFSETS[k + 1] <= start + PHASE_COLS[phase]]


def _gather_and_project(x2, g_pre, w_in_s, w_mkv_s, w_out_s, rel_bias_t, buckets, x_arr):
    n_tok = x2.shape[0]
    n_tiles = n_tok // PROJ_TILE
    last = n_tiles - 1
    shapes = [w_in_s.shape, w_mkv_s.shape, w_out_s.shape]
    n_w = len(shapes)

    def body(x_sref, x_ref, g_ref, win_hbm, wmkv_hbm, wout_hbm, rel_ref, bk_ref, h_ref, *refs):
        part_refs, refs = refs[:len(PROJ_WIDTHS)], refs[len(PROJ_WIDTHS):]
        bias_ref, refs = refs[0], refs[1:]
        gin_hbm, gmkv_hbm, gout_hbm, wg, stage_in, stage_mkv, stage_out, own_mkv, own_out, h_all = refs[:10]
        send_sems, recv_sems, local_sems = refs[10:]
        p, t = pl.program_id(0), pl.program_id(1)
        x, y, c = lax.axis_index("x"), lax.axis_index("y"), lax.axis_index("c")
        me, sibling = (x, y, c), (x, y, 1 - c)
        my_shard = 2 * x + y
        gathered = [wg, gmkv_hbm, gout_hbm]

        def half_rows(w, shard, half):
            rows = shapes[w][0] // 2
            if w == 0:
                return wg.at[pl.ds(pl.multiple_of(shard * shapes[0][0] + half * rows, 16), rows), :]
            return gathered[w].at[shard, pl.ds(half * rows, rows), :]

        def first(w, rel):
            src = half_rows(w, my_shard, c) if w == 0 else (own_mkv, own_out)[w - 1].at[
                pl.ds(c * (shapes[w][0] // 2), shapes[w][0] // 2), :]
            k = 3 * w + rel - 1
            return pltpu.make_async_remote_copy(
                src_ref=src, dst_ref=half_rows(w, my_shard, c), send_sem=send_sems.at[k], recv_sem=recv_sems.at[k],
                device_id=(x ^ (rel >> 1), y ^ (rel & 1), c), device_id_type=MESH)

        def landed(w, rel):
            k = 3 * w + rel - 1
            ref = half_rows(w, my_shard ^ rel, c)
            return pltpu.make_async_remote_copy(src_ref=ref, dst_ref=ref, send_sem=send_sems.at[k],
                                                recv_sem=recv_sems.at[k], device_id=me, device_id_type=MESH)

        def passed(w, rel, half, to):
            k = 9 + 3 * w + rel - 1
            ref = half_rows(w, my_shard ^ rel, half)
            return pltpu.make_async_remote_copy(src_ref=ref, dst_ref=ref, send_sem=send_sems.at[k],
                                                recv_sem=recv_sems.at[k], device_id=to, device_id_type=MESH)

        def pass_on(w, rels):
            for rel in rels:
                landed(w, rel).wait_recv()
                passed(w, rel, c, sibling).start()
            for rel in rels:
                passed(w, rel, 1 - c, me).wait_recv()

        own_stores = [pltpu.make_async_copy(own_mkv, gmkv_hbm.at[my_shard], local_sems.at[3]),
                      pltpu.make_async_copy(own_out, gout_hbm.at[my_shard], local_sems.at[4])]

        @pl.when((p == 0) & (t == 0))
        def _():
            loads = [pltpu.make_async_copy(src, dst, local_sems.at[k]) for k, (src, dst) in enumerate(
                ((win_hbm, stage_in), (wmkv_hbm, stage_mkv), (wout_hbm, stage_out)))]
            for cp in loads:
                cp.start()
            loads[0].wait()
            wg[pl.ds(pl.multiple_of(my_shard * shapes[0][0], 16), shapes[0][0]), :] = stage_in[...].astype(BF16)
            for rel in (1, 2):
                first(0, rel).start()
            loads[1].wait()
            loads[2].wait()
            own_mkv[...] = stage_mkv[...].astype(BF16)
            own_out[...] = stage_out[...].astype(BF16)
            for cp in own_stores:
                cp.start()
            _fill_bias(rel_ref, bk_ref, bias_ref)
            pass_on(0, (1,))
            first(0, 3).start()

        @pl.when((p == 0) & (t == n_tiles // 2))
        def _():
            for w in (1, 2):
                for rel in (1, 2, 3):
                    first(w, rel).start()

        store = pltpu.make_async_copy(wg, gin_hbm, local_sems.at[5])

        @pl.when((p == 1) & (t == 0))
        def _():
            pass_on(0, (2, 3))
            store.start()

        @pl.when((p == 1) & (t == n_tiles // 2))
        def _():
            for w in (1, 2):
                pass_on(w, (1, 2, 3))

        tile_rows = pl.ds(pl.multiple_of(t * PROJ_TILE, PROJ_TILE), PROJ_TILE)

        def project(h, phase):
            start = jnp.where(x_sref[0] == 0, _phase_columns(phase, 0), _phase_columns(phase, 1))
            proj = _mm_nt(h, wg[pl.ds(pl.multiple_of(start, MXU_TILE), PHASE_COLS[phase]), :])
            for chip_x in range(2):
                @pl.when(x_sref[0] == chip_x)
                def _():
                    for k, lo in _phase_parts(phase, chip_x):
                        part_refs[k][...] = proj[:, lo:lo + PROJ_WIDTHS[k]]

        @pl.when(p == 0)
        def _():
            xv = x_ref[...]
            r = lax.rsqrt(jnp.mean(xv * xv, axis=-1, keepdims=True) + EPS)
            h = (xv * r * g_ref[...]).astype(BF16)
            h_ref[...] = h
            h_all[tile_rows, :] = h
            project(h, 0)

        @pl.when(p == 1)
        def _():
            project(h_all[tile_rows, :], 1)

        @pl.when((p == 1) & (t == last))
        def _():
            for w in range(n_w):
                for rel in (1, 2, 3):
                    first(w, rel).wait_send()
                    passed(w, rel, c, sibling).wait_send()
            for cp in own_stores:
                cp.wait()
            store.wait()

    def written_in(k):
        phase_on = [next(ph for ph in range(2) if k in dict(_phase_parts(ph, chip_x))) for chip_x in range(2)]

        def index(p, t, xs):
            phase = jnp.where(xs[0] == 0, phase_on[0], phase_on[1])
            return (jnp.where(p == phase, t, jnp.where(p < phase, 0, last)), 0)
        return index

    part_specs = [pl.BlockSpec((PROJ_TILE, PROJ_WIDTHS[k]), written_in(k)) for k in range(len(PROJ_WIDTHS))]
    vmem = pltpu.VMEM
    out = pl.pallas_call(
        body, name="gather_and_project",
        out_shape=[jax.ShapeDtypeStruct((n_tok, D_MODEL), BF16)]
        + [jax.ShapeDtypeStruct((n_tok, w), F32) for w in PROJ_WIDTHS]
        + [jax.ShapeDtypeStruct((2, 4 * CHUNK, 2 * CHUNK), F32)]
        + [jax.ShapeDtypeStruct((N_CHIPS * shapes[0][0], shapes[0][1]), BF16)]
        + [jax.ShapeDtypeStruct((N_CHIPS,) + s, BF16) for s in shapes[1:]],
        grid_spec=pltpu.PrefetchScalarGridSpec(
            num_scalar_prefetch=1, grid=(2, n_tiles),
            in_specs=[pl.BlockSpec((PROJ_TILE, D_MODEL), lambda p, t, xs: (jnp.where(p == 0, t, last), 0)),
                      pl.BlockSpec((1, D_MODEL), lambda p, t, xs: (0, 0)), ANY_SPEC, ANY_SPEC, ANY_SPEC, SMEM_SPEC,
                      pl.BlockSpec(buckets.shape, lambda p, t, xs: (0, 0))],
            out_specs=[pl.BlockSpec((PROJ_TILE, D_MODEL), lambda p, t, xs: (jnp.where(p == 0, t, last), 0))]
            + part_specs + [pl.BlockSpec((2, 4 * CHUNK, 2 * CHUNK), lambda p, t, xs: (0, 0, 0))] + [ANY_SPEC] * 3,
            scratch_shapes=[vmem((N_CHIPS * shapes[0][0], shapes[0][1]), BF16), vmem(shapes[0], F32),
                            vmem(shapes[1], F32), vmem(shapes[2], F32), vmem(shapes[1], BF16), vmem(shapes[2], BF16),
                            vmem((n_tok, D_MODEL), BF16),
                            pltpu.SemaphoreType.DMA((18,)), pltpu.SemaphoreType.DMA((18,)),
                            pltpu.SemaphoreType.DMA((6,))]),
        compiler_params=pltpu.CompilerParams(vmem_limit_bytes=VMEM_LIMIT),
    )(x_arr, x2, g_pre, w_in_s, w_mkv_s, w_out_s, rel_bias_t, buckets)
    n_parts = len(PROJ_WIDTHS)
    return out[0], list(out[1:1 + n_parts]), out[2 + n_parts:], out[1 + n_parts]


def _load_chunk(j, i, sk_ref, sv_ref, skp_ref, svp_ref):
    rows = slice(j * CHUNK, (j + 1) * CHUNK)
    if j == 0:
        k_prev, v_prev, table = skp_ref[...], svp_ref[...], jnp.where(i > 0, 0, 1)
    else:
        prev = slice((j - 1) * CHUNK, j * CHUNK)
        k_prev, v_prev, table = sk_ref[prev, :], sv_ref[prev, :], 0
    k_pairs = _pair_operands(_swa_variants(jnp.concatenate([k_prev, sk_ref[rows, :]], axis=0)))
    v_pairs = _pair_operands(_swa_variants(jnp.concatenate([v_prev, sv_ref[rows, :]], axis=0)))
    return rows, k_pairs, v_pairs, table


def _tile_constants(ws_ref, bs_ref, sink_ref, mkv_v):
    wm = _causal_weights(ws_ref)
    bs_rows = [jnp.concatenate([bs_ref[g]] * TILE_CHUNKS, axis=0) for g in range(A_GROUPS)]
    sink_col = jnp.max(jnp.concatenate([jnp.full((CHUNK, 128), sink_ref[0, h], F32) for h in range(4)] * TILE_CHUNKS,
                                       axis=0), axis=-1, keepdims=True)
    mk_pairs = _pair_operands(_mem_variants(mkv_v[:, :MEM_WIDTH]))
    mv_pairs = _pair_operands(_mem_variants(mkv_v[:, MEM_WIDTH:]))
    return wm, bs_rows, sink_col, mk_pairs, mv_pairs


def _mix(parts, mem, x2, tgt2, v_g, v_b, w_sp, b_sp, sinks, bias, w_out, g_post, g_mem, w_mkv, n_ex, seq):
    n_tiles_ex = seq // TILE
    n_tok = n_ex * seq
    au, av, sq, sk, sv, mq, z = parts
    col = dict(zip(("au", "av", "sq", "sk", "sv", "mq", "z"),
                   (slice(PROJ_OFFSETS[k], PROJ_OFFSETS[k + 1]) for k in range(len(PROJ_WIDTHS)))))
    before_kv, after_kv = slice(0, col["sk"].start), slice(col["sv"].stop, IN_WIDTH)

    def body(au_ref, av_ref, sq_ref, sk_ref, sv_ref, skp_ref, svp_ref, mq_ref, z_ref, mem_ref, x_ref, tgt_ref,
             vg_ref, vb_ref, ws_ref, bs_ref, sink_ref, bias_ref, wout_ref, gpost_ref, gmem_ref, wmkv_ref,
             dout_ref, dproj_ref, dwmkv_ref, dgmem_ref, dwout_ref, dvg_ref, dvb_ref, dws_ref, dbs_ref, dsink_ref,
             drel_ref, loss_ref, dgpost_ref, carry_dp, carry_k, carry_v, memn_s, mkv_s, dmkv_s):
        b, i = pl.program_id(0), pl.program_id(1)

        @pl.when((b == 0) & (i == 0))
        def _():
            for ref in (dwmkv_ref, dgmem_ref, dwout_ref, dvg_ref, dvb_ref, dws_ref, dbs_ref, dsink_ref, drel_ref,
                        loss_ref, dgpost_ref):
                ref[...] = jnp.zeros_like(ref)

        def normalized_mem():
            m = mem_ref[0]
            return m * lax.rsqrt(jnp.mean(m * m, axis=-1, keepdims=True) + EPS)

        @pl.when(i == 0)
        def _():
            memn_s[...] = (normalized_mem() * gmem_ref[...]).astype(BF16)
            mkv_s[...] = _mm(memn_s[...], wmkv_ref[...])
            dmkv_s[...] = jnp.zeros_like(dmkv_s)
            carry_k[...] = jnp.zeros_like(carry_k)
            carry_v[...] = jnp.zeros_like(carry_v)

        @pl.when(i > 0)
        def _():
            dproj_ref[:, before_kv] = carry_dp[:, before_kv]
            dproj_ref[:, after_kv] = carry_dp[:, after_kv]

        @pl.when(i < n_tiles_ex)
        def _():
            wm, bs_rows, sink_col, mk_pairs, mv_pairs = _tile_constants(ws_ref, bs_ref, sink_ref, mkv_s[...])
            vg = vg_ref[...]

            au_v, av_v = au_ref[...], av_ref[...]
            ya, res = _group_a_forward(au_v, av_v, vg, vb_ref[...], wm, bs_rows)
            swa, logits, yb = [], [], []
            for j in range(TILE_CHUNKS):
                rows, k_pairs, v_pairs, table = _load_chunk(j, i, sk_ref, sv_ref, skp_ref, svp_ref)
                qp = _halves_bf16(sq_ref[rows, :] * QK_SCALE)
                logits.append(_attention_logits(qp, k_pairs) + bias_ref[table])
                swa.append([rows, k_pairs, v_pairs, qp])
            p_swa, sink_p = _softmax(jnp.concatenate(logits, axis=0), sink_col)
            for j in range(TILE_CHUNKS):
                out, pp = _attention_out(p_swa[j * 4 * CHUNK:(j + 1) * 4 * CHUNK], swa[j][2], CHUNK)
                yb.append(out)
                swa[j].append(pp)
            mqp = _halves_bf16(mq_ref[...] * QK_SCALE)
            pm, _ = _softmax(_attention_logits(mqp, mk_pairs), None)
            yc, ppm = _attention_out(pm, mv_pairs, TILE)
            ycat = jnp.concatenate(ya + [jnp.concatenate(yb, axis=0), yc], axis=-1)

            zv = z_ref[...]
            sig = _sigmoid(zv)
            sz = zv * sig
            y_b = (ycat * sz).astype(BF16)
            o = _mm(y_b, wout_ref[...])
            r2 = lax.rsqrt(jnp.mean(o * o, axis=-1, keepdims=True) + EPS)
            nrm = o * r2
            gp = gpost_ref[...]
            diff = x_ref[...] + nrm * gp - tgt_ref[...]
            loss_ref[...] += jnp.sum(diff * diff) * (0.5 / D_MODEL)
            dout = diff * (1.0 / D_MODEL)
            dout_ref[...] = dout
            dgpost_ref[...] += jnp.sum(dout * nrm, axis=0, keepdims=True)
            dn = dout * gp
            do_b = (r2 * (dn - nrm * jnp.mean(dn * nrm, axis=-1, keepdims=True))).astype(BF16)
            dwout_ref[...] += _mm_tn(y_b, do_b)
            dy = _mm_nt(do_b, wout_ref[...])
            carry_dp[:, col["z"]] = (dy * ycat * (sig * (1.0 + zv * (1.0 - sig)))).astype(BF16)
            dyc = dy * sz

            dgu, dgv = [], []
            for g in range(A_GROUPS):
                sl = slice(g * 128, (g + 1) * 128)
                xhat, rstd, vn, s = res["groups"][g]
                dya = dyc[:, sl]
                dgu.append(dya * s)
                ds = dya * res["gu"][:, sl]
                dbs_ref[:, sl] += sum(ds[c * CHUNK:(c + 1) * CHUNK] for c in range(TILE_CHUNKS))
                ds_b = _rows_to_lanes(ds.astype(BF16), TILE_CHUNKS)
                dws_ref[g] += _mm_nt(ds_b, vn)
                dvn = _lanes_to_rows(_mm_tn(wm[g], ds_b), TILE_CHUNKS)
                dvg_ref[:, sl] += jnp.sum(dvn * xhat, axis=0, keepdims=True)
                dvb_ref[:, sl] += jnp.sum(dvn, axis=0, keepdims=True)
                dxh = dvn * vg[:, sl]
                dgv.append(rstd * (dxh - jnp.mean(dxh, axis=-1, keepdims=True)
                                   - xhat * jnp.mean(dxh * xhat, axis=-1, keepdims=True)))
            carry_dp[:, col["au"]] = (jnp.concatenate(dgu, axis=-1) * _gelu_grad(au_v, res["tu"])).astype(BF16)
            carry_dp[:, col["av"]] = (jnp.concatenate(dgv, axis=-1) * _gelu_grad(av_v, res["tv"])).astype(BF16)

            do_pairs = [_halves_bf16(dyc[rows, A_WIDTH:A_WIDTH + SWA_WIDTH]) for rows, *_ in swa]
            dl_swa, delta = _softmax_backward(p_swa, jnp.concatenate(
                [_attention_dprobs(do_pairs[j], swa[j][2]) for j in range(TILE_CHUNKS)], axis=0))
            sink_terms = sink_p * delta
            lane4 = lax.broadcasted_iota(jnp.int32, (1, 128), 1)
            dsink_vec = jnp.zeros((1, 128), F32)
            for h in range(4):
                head_sum = sum(jnp.sum(sink_terms[(4 * j + h) * CHUNK:(4 * j + h + 1) * CHUNK])
                               for j in range(TILE_CHUNKS))
                dsink_vec = dsink_vec + jnp.where(lane4 == h, -head_sum, 0.0)
            dsink_ref[...] += dsink_vec
            drel_ref[...] += sum(dl_swa[j * 4 * CHUNK:(j + 1) * 4 * CHUNK] for j in range(TILE_CHUNKS))
            dk_parts, dv_parts = [], []
            for j, (rows, k_pairs, v_pairs, qp, pp) in enumerate(swa):
                dq, dk, dv = _attention_grads(dl_swa[j * 4 * CHUNK:(j + 1) * 4 * CHUNK], pp, do_pairs[j], qp, k_pairs,
                                              CHUNK)
                carry_dp[rows, col["sq"]] = (dq * QK_SCALE).astype(BF16)
                dk_parts.append(_swa_unvariants(*_split_pair_grads(dk)))
                dv_parts.append(_swa_unvariants(*_split_pair_grads(dv)))

            dc_pairs = _halves_bf16(dyc[:, A_WIDTH + SWA_WIDTH:])
            dl_mem, _ = _softmax_backward(pm, _attention_dprobs(dc_pairs, mv_pairs))
            dmq, dmk, dmv = _attention_grads(dl_mem, ppm, dc_pairs, mqp, mk_pairs, TILE)
            carry_dp[:, col["mq"]] = (dmq * QK_SCALE).astype(BF16)
            dmkv_s[...] += jnp.concatenate([_mem_unvariants(*_split_pair_grads(dmk)),
                                            _mem_unvariants(*_split_pair_grads(dmv))], axis=-1)

            for parts_c, carry, cols in ((dk_parts, carry_k, col["sk"]), (dv_parts, carry_v, col["sv"])):
                @pl.when(i > 0)
                def _():
                    dproj_ref[:, cols] = (carry[...] + jnp.concatenate(
                        [jnp.zeros((TILE - CHUNK, KV_WIDTH), F32), parts_c[0][:CHUNK]], axis=0)).astype(BF16)
                new = [parts_c[0][CHUNK:]]
                for j in range(1, TILE_CHUNKS):
                    new[-1] = new[-1] + parts_c[j][:CHUNK]
                    new.append(parts_c[j][CHUNK:])
                carry[...] = jnp.concatenate(new, axis=0)

        @pl.when(i == n_tiles_ex)
        def _():
            dproj_ref[:, col["sk"]] = carry_k[...].astype(BF16)
            dproj_ref[:, col["sv"]] = carry_v[...].astype(BF16)
            d_b = dmkv_s[...].astype(BF16)
            dwmkv_ref[...] += _mm_tn(memn_s[...], d_b)
            dgmem_ref[...] += jnp.sum(_mm_nt(d_b, wmkv_ref[...]) * normalized_mem(), axis=0, keepdims=True)

    tile = functools.partial(_tile_specs, n_tiles_ex)
    prev = functools.partial(_prev_chunk_spec, n_tiles_ex)
    late = pl.BlockSpec((TILE, IN_WIDTH), lambda b, i: (b * n_tiles_ex + jnp.maximum(i - 1, 0), 0))
    return pl.pallas_call(
        body, name="mix", grid=(n_ex, n_tiles_ex + 1),
        out_shape=[jax.ShapeDtypeStruct((n_tok, D_MODEL), F32), jax.ShapeDtypeStruct((n_tok, IN_WIDTH), BF16),
                   jax.ShapeDtypeStruct((D_MODEL, 2 * MEM_WIDTH), F32), jax.ShapeDtypeStruct((1, D_MODEL), F32),
                   jax.ShapeDtypeStruct((MIX_WIDTH, D_MODEL), F32), jax.ShapeDtypeStruct((1, A_WIDTH), F32),
                   jax.ShapeDtypeStruct((1, A_WIDTH), F32), jax.ShapeDtypeStruct((A_GROUPS, CHUNK, CHUNK), F32),
                   jax.ShapeDtypeStruct((CHUNK, A_WIDTH), F32), jax.ShapeDtypeStruct((1, 128), F32),
                   jax.ShapeDtypeStruct((4 * CHUNK, 2 * CHUNK), F32), jax.ShapeDtypeStruct((1, 128), F32),
                   jax.ShapeDtypeStruct((1, D_MODEL), F32)],
        in_specs=[tile(A_WIDTH), tile(A_WIDTH), tile(SWA_WIDTH), tile(KV_WIDTH), tile(KV_WIDTH),
                  prev(KV_WIDTH), prev(KV_WIDTH), tile(MEM_WIDTH), tile(MIX_WIDTH),
                  pl.BlockSpec((1, MEM_LEN, D_MODEL), lambda b, i: (b, 0, 0)),
                  tile(D_MODEL), tile(D_MODEL),
                  _full_spec((1, A_WIDTH)), _full_spec((1, A_WIDTH)), _full_spec((A_GROUPS, CHUNK, CHUNK)),
                  _full_spec((A_GROUPS, CHUNK, CHUNK)), SMEM_SPEC, _full_spec((2, 4 * CHUNK, 2 * CHUNK)),
                  _full_spec((MIX_WIDTH, D_MODEL)), _full_spec((1, D_MODEL)), _full_spec((1, D_MODEL)),
                  _full_spec((D_MODEL, 2 * MEM_WIDTH))],
        out_specs=[tile(D_MODEL), late, _full_spec((D_MODEL, 2 * MEM_WIDTH)), _full_spec((1, D_MODEL)),
                   _full_spec((MIX_WIDTH, D_MODEL)), _full_spec((1, A_WIDTH)), _full_spec((1, A_WIDTH)),
                   _full_spec((A_GROUPS, CHUNK, CHUNK)), _full_spec((CHUNK, A_WIDTH)), _full_spec((1, 128)),
                   _full_spec((4 * CHUNK, 2 * CHUNK)), _full_spec((1, 128)), _full_spec((1, D_MODEL))],
        scratch_shapes=[pltpu.VMEM((TILE, IN_WIDTH), BF16), pltpu.VMEM((TILE, KV_WIDTH), F32),
                        pltpu.VMEM((TILE, KV_WIDTH), F32), pltpu.VMEM((MEM_LEN, D_MODEL), BF16),
                        pltpu.VMEM((MEM_LEN, 2 * MEM_WIDTH), F32), pltpu.VMEM((MEM_LEN, 2 * MEM_WIDTH), F32)],
        compiler_params=pltpu.CompilerParams(vmem_limit_bytes=VMEM_LIMIT),
    )(au, av, sq, sk, sv, sk, sv, mq, z, mem, x2, tgt2, v_g, v_b, w_sp, b_sp, sinks, bias, w_out, g_post, g_mem,
      w_mkv)


BWD_PROJ_TILE = 1024


def _backward_projection(x2, dout, dproj, g_pre, w_in_t):
    n_tok = x2.shape[0]
    n_steps = n_tok // BWD_PROJ_TILE

    def body(x_ref, dout_ref, dp_ref, g_ref, w_hbm, dx_ref, dgpre_ref, w_vmem, sem):
        @pl.when(pl.program_id(0) == 0)
        def _():
            load = pltpu.make_async_copy(w_hbm, w_vmem, sem)
            load.start()
            dgpre_ref[...] = jnp.zeros_like(dgpre_ref)
            load.wait()

        xv = x_ref[...]
        r = lax.rsqrt(jnp.mean(xv * xv, axis=-1, keepdims=True) + EPS)
        xn = xv * r
        dh = _mm(dp_ref[...], w_vmem[...])
        dgpre_ref[...] += jnp.sum(dh * xn, axis=0, keepdims=True)
        dhg = dh * g_ref[...]
        dx_ref[...] = r * (dhg - xn * jnp.mean(dhg * xn, axis=-1, keepdims=True)) + dout_ref[...]

    row = lambda w: pl.BlockSpec((BWD_PROJ_TILE, w), lambda i: (i, 0))
    return pl.pallas_call(
        body, name="backward_projection", grid=(n_steps,),
        out_shape=[jax.ShapeDtypeStruct((n_tok, D_MODEL), F32), jax.ShapeDtypeStruct((1, D_MODEL), F32)],
        in_specs=[row(D_MODEL), row(D_MODEL), row(IN_WIDTH), _full_spec((1, D_MODEL)), ANY_SPEC],
        out_specs=[row(D_MODEL), _full_spec((1, D_MODEL))],
        scratch_shapes=[pltpu.VMEM((IN_WIDTH, D_MODEL), BF16), pltpu.SemaphoreType.DMA],
        input_output_aliases={1: 0},
        compiler_params=pltpu.CompilerParams(vmem_limit_bytes=VMEM_LIMIT),
    )(x2, dout, dproj, g_pre, w_in_t)


SHARD_ROWS = IN_WIDTH // N_CHIPS
SHARD_WINDOW = 768
SHARD_HALF = SHARD_ROWS // 2
DWIN_TILE = 2048
N_REL = N_CHIPS - 1


def _shard_window_start(shard):
    return (shard * SHARD_ROWS // 128) * 128


def _reduce_gradients(dproj, h, big, small, shard_arr):
    n_tok = h.shape[0]
    tile = min(DWIN_TILE, n_tok)
    n_sub = n_tok // tile
    last = N_CHIPS - 1
    n_big, n_small = len(big), len(small)
    big_half = [g.shape[2:] for g in big]
    sem_big_d2d = 2 * N_CHIPS
    sem_big_ici = sem_big_d2d + n_big
    sem_big_swap = sem_big_ici + N_REL * n_big
    sem_small_d2d = sem_big_swap + n_big
    sem_small_ici = sem_small_d2d + n_small
    n_sems = sem_small_ici + N_REL * n_small
    loc_small = n_big
    loc_out_win = loc_small + n_small
    loc_out_big = loc_out_win + 2
    loc_out_small = loc_out_big + 2 * n_big
    n_local = loc_out_small + n_small

    def relation_of_slot(s):
        return (s + 2) % N_REL + 1

    def shard_of_slot(s, my_shard):
        return my_shard ^ jnp.where(s == last, 0, relation_of_slot(s))

    def body(shard_ref, dp_ref, h_hbm, *refs):
        h_vmem, h_sem, refs = refs[-2], refs[-1], refs[:-2]
        big_hbm, refs = refs[:n_big], refs[n_big:]
        small_hbm, refs = refs[:n_small], refs[n_small:]
        out_hbm, refs = refs[0], refs[1:]
        big_out, refs = refs[:n_big], refs[n_big:]
        small_out, refs = refs[:n_small], refs[n_small:]
        part, recv_d2d, send_ici, recv_ici, mine_buf, other_buf = refs[:6]
        refs = refs[6:]
        big_own, big_recv, big_send, big_land, big_mine, big_other = (
            refs[k * n_big:(k + 1) * n_big] for k in range(6))
        refs = refs[6 * n_big:]
        small_own, small_recv, small_all = (refs[k * n_small:(k + 1) * n_small] for k in range(3))
        send_sems, recv_sems, local_sems = refs[3 * n_small:]

        s, t = pl.program_id(0), pl.program_id(1)
        x, y, c = lax.axis_index("x"), lax.axis_index("y"), lax.axis_index("c")
        my_chip = 2 * x + y
        sibling = (x, y, 1 - c)
        my_rows = pl.ds(pl.multiple_of(c * SHARD_HALF, 8), SHARD_HALF)
        other_rows = pl.ds(pl.multiple_of((1 - c) * SHARD_HALF, 8), SHARD_HALF)

        def remote(src, dst, k, to):
            return pltpu.make_async_remote_copy(src_ref=src, dst_ref=dst, send_sem=send_sems.at[k],
                                                recv_sem=recv_sems.at[k], device_id=to, device_id_type=MESH)

        def chip_at(rel):
            return (x ^ (rel >> 1), y ^ (rel & 1), c)

        def to_sibling(k):
            return remote(part.at[k % 2, other_rows, :], recv_d2d.at[k], k, sibling)

        def to_chip(k):
            return remote(send_ici.at[k], recv_ici.at[k], N_CHIPS + k, chip_at(relation_of_slot(k)))

        swap = remote(mine_buf, other_buf, 2 * N_CHIPS - 1, sibling)
        big_load = [pltpu.make_async_copy(big_hbm[w].at[:, pl.ds(c, 1)], big_own[w], local_sems.at[w])
                    for w in range(n_big)]
        big_to_sibling = [remote(big_hbm[w].at[:, pl.ds(1 - c, 1)], big_recv[w], sem_big_d2d + w, sibling)
                          for w in range(n_big)]
        big_to_chip = [[remote(big_send[w].at[k], big_land[w].at[k], sem_big_ici + N_REL * w + k, chip_at(k + 1))
                        for k in range(N_REL)] for w in range(n_big)]
        big_swap = [remote(big_mine[w], big_other[w], sem_big_swap + w, sibling) for w in range(n_big)]
        small_load = [pltpu.make_async_copy(small_hbm[i], small_own[i], local_sems.at[loc_small + i])
                      for i in range(n_small)]
        small_to_sibling = [remote(small_hbm[i], small_recv[i], sem_small_d2d + i, sibling) for i in range(n_small)]
        small_to_chip = [[remote(small_all[i].at[my_chip], small_all[i].at[my_chip],
                                 sem_small_ici + N_REL * i + k, chip_at(k + 1))
                          for k in range(N_REL)] for i in range(n_small)]

        @pl.when((s == 0) & (t == 0))
        def _():
            h_load = pltpu.make_async_copy(h_hbm, h_vmem, h_sem)
            h_load.start()
            for cp in big_load + big_to_sibling + small_load + small_to_sibling:
                cp.start()
            h_load.wait()

        @pl.when((s == 0) & (t == n_sub - 1))
        def _():
            for cp in big_load + small_load:
                cp.wait()
            for cp in big_to_sibling + small_to_sibling:
                cp.wait_recv()
                cp.wait_send()
            for w in range(n_big):
                for k in range(N_REL):
                    shard = my_chip ^ (k + 1)
                    big_send[w][k] = (big_own[w][shard, 0] + big_recv[w][shard, 0]).astype(BF16)
                    big_to_chip[w][k].start()
            for i in range(n_small):
                small_all[i][my_chip] = small_own[i][...] + small_recv[i][...]
                for k in range(N_REL):
                    small_to_chip[i][k].start()

        @pl.when((s > 0) & (t == jnp.where(s == last, 0, min(1, n_sub - 1))))
        def _():
            k = s - 1
            cp = to_sibling(k)
            cp.wait_recv()
            cp.wait_send()
            send_ici[k] = (part[k % 2, my_rows, :] + recv_d2d[k]).astype(BF16)
            to_chip(k).start()

        def big_rows(w, half):
            rows = big_half[w][0]
            return big_out[w].at[pl.ds(pl.multiple_of(half * rows, 8), rows), :]

        big_store_mine = [pltpu.make_async_copy(big_mine[w], big_rows(w, c), local_sems.at[loc_out_big + 2 * w])
                          for w in range(n_big)]
        big_store_other = [pltpu.make_async_copy(big_other[w], big_rows(w, 1 - c),
                                                 local_sems.at[loc_out_big + 2 * w + 1]) for w in range(n_big)]
        small_store = [pltpu.make_async_copy(small_all[i], small_out[i], local_sems.at[loc_out_small + i])
                       for i in range(n_small)]

        @pl.when((s == last) & (t == 0))
        def _():
            for w in range(n_big):
                total = big_own[w][my_chip, 0] + big_recv[w][my_chip, 0]
                for k in range(N_REL):
                    big_to_chip[w][k].wait_recv()
                    total = total + big_land[w][k].astype(F32)
                big_mine[w][...] = total
                big_swap[w].start()
                big_store_mine[w].start()
            for i in range(n_small):
                for k in range(N_REL):
                    small_to_chip[i][k].wait_recv()
                small_store[i].start()

        r = _mm_tn(dp_ref[...], h_vmem[pl.ds(pl.multiple_of(t * tile, tile), tile), :])
        odd = shard_of_slot(s, shard_ref[0]) % 2
        for parity in range(2):
            rows = r[64 * parity:64 * parity + SHARD_ROWS]

            @pl.when((odd == parity) & (t == 0))
            def _():
                part[s % 2] = rows

            @pl.when((odd == parity) & (t > 0))
            def _():
                part[s % 2] += rows

        @pl.when(t == n_sub - 1)
        def _():
            to_sibling(s).start()

        @pl.when((s == last) & (t == n_sub - 1))
        def _():
            cp = to_sibling(last)
            cp.wait_recv()
            cp.wait_send()
            total = part[last % 2, my_rows, :] + recv_d2d[last]
            for k in range(last):
                to_chip(k).wait_recv()
                total = total + recv_ici[k].astype(F32)
            mine_buf[...] = total
            swap.start()
            out_mine = pltpu.make_async_copy(mine_buf, out_hbm.at[my_rows, :], local_sems.at[0])
            out_mine.start()
            swap.wait_recv()
            out_other = pltpu.make_async_copy(other_buf, out_hbm.at[other_rows, :], local_sems.at[1])
            out_other.start()
            for w in range(n_big):
                big_swap[w].wait_recv()
                big_store_other[w].start()
            stores = [out_mine, out_other] + big_store_mine + big_store_other + small_store
            for k in range(last):
                to_chip(k).wait_send()
            swap.wait_send()
            for w in range(n_big):
                for k in range(N_REL):
                    big_to_chip[w][k].wait_send()
                big_swap[w].wait_send()
            for i in range(n_small):
                for k in range(N_REL):
                    small_to_chip[i][k].wait_send()
            for cp in stores:
                cp.wait()

    half = (SHARD_HALF, D_MODEL)
    vmem = pltpu.VMEM
    scratch = [vmem((2, SHARD_ROWS, D_MODEL), F32), vmem((N_CHIPS,) + half, F32),
               vmem((N_REL,) + half, BF16), vmem((N_REL,) + half, BF16), vmem(half, F32), vmem(half, F32)]
    scratch += [vmem((N_CHIPS, 1) + hs, F32) for hs in big_half] * 2
    scratch += [vmem((N_REL,) + hs, BF16) for hs in big_half] * 2
    scratch += [vmem(hs, F32) for hs in big_half] * 2
    scratch += [vmem(a.shape, F32) for a in small] * 2 + [vmem((N_CHIPS,) + a.shape, F32) for a in small]
    scratch += [pltpu.SemaphoreType.DMA((n_sems,)), pltpu.SemaphoreType.DMA((n_sems,)),
                pltpu.SemaphoreType.DMA((n_local,)), vmem(h.shape, BF16), pltpu.SemaphoreType.DMA]
    n_hbm = n_big + n_small
    out = pl.pallas_call(
        body, name="reduce_gradients",
        out_shape=[jax.ShapeDtypeStruct((SHARD_ROWS, D_MODEL), F32)]
        + [jax.ShapeDtypeStruct((2 * hs[0], hs[1]), F32) for hs in big_half]
        + [jax.ShapeDtypeStruct((N_CHIPS,) + a.shape, F32) for a in small],
        grid_spec=pltpu.PrefetchScalarGridSpec(
            num_scalar_prefetch=1, grid=(N_CHIPS, n_sub),
            in_specs=[pl.BlockSpec((pl.Element(tile), pl.Element(SHARD_WINDOW)),
                                   lambda s, t, m: (t * tile, _shard_window_start(shard_of_slot(s, m[0])))),
                      ANY_SPEC] + [ANY_SPEC] * n_hbm,
            out_specs=[ANY_SPEC] * (1 + n_hbm),
            scratch_shapes=scratch),
        compiler_params=pltpu.CompilerParams(vmem_limit_bytes=VMEM_LIMIT),
    )(shard_arr, dproj, h, *big, *small)
    return out[:1 + n_big], out[1 + n_big:]


def _pack_small_grads(dgpre, dgpost, dgmem, dvg, dvb, dws, dbs, dsink, drel, loss_vec, buckets):
    def body(dgpre_ref, dgpost_ref, dgmem_ref, dvg_ref, dvb_ref, dws_ref, dbs_ref, dsink_ref, drel_ref, loss_ref,
             bk_ref, a_ref, b_ref):
        a_ref[...] = jnp.zeros_like(a_ref)
        b_ref[...] = jnp.zeros_like(b_ref)
        a_ref[0:1, :] = dgpre_ref[...]
        a_ref[1:2, :] = dgpost_ref[...]
        a_ref[2:3, :] = dgmem_ref[...]
        a_ref[3:4, :] = jnp.concatenate([dvg_ref[...], dvb_ref[...]], axis=-1)
        a_ref[ROW_LOSS:ROW_LOSS + 1, 0:128] = loss_ref[...]
        row = lax.broadcasted_iota(jnp.int32, (CHUNK, CHUNK), 0)
        col = lax.broadcasted_iota(jnp.int32, (CHUNK, CHUNK), 1)
        for g in range(A_GROUPS):
            b_ref[ROW_WS + g * CHUNK:ROW_WS + (g + 1) * CHUNK, :] = jnp.where(row >= col, dws_ref[g], 0.0)
            by_token = jnp.transpose(dbs_ref[:, g * 128:(g + 1) * 128])
            b_ref[ROW_BS + g:ROW_BS + g + 1, :] = jnp.sum(by_token, axis=0, keepdims=True)
        b_ref[ROW_SINK:ROW_SINK + 1, :] = dsink_ref[...]
        bk = bk_ref[...]
        rel_row = lax.broadcasted_iota(jnp.int32, (8, 128), 0)
        rel_col = lax.broadcasted_iota(jnp.int32, (8, 128), 1)
        rel = jnp.zeros((8, 128), F32)
        for h in range(4):
            acc = drel_ref[h * CHUNK:(h + 1) * CHUNK, :]
            for b in range(N_BUCKETS):
                rel = jnp.where((rel_row == h) & (rel_col == b), jnp.sum(jnp.where(bk == b, acc, 0.0)), rel)
        b_ref[ROW_REL:ROW_REL + 8, :] = rel

    return pl.pallas_call(
        body, name="pack_small_grads",
        out_shape=[jax.ShapeDtypeStruct((SMALL_A_ROWS, D_MODEL), F32), jax.ShapeDtypeStruct((SMALL_B_ROWS, 128), F32)],
        in_specs=[VMEM_SPEC] * 11, out_specs=[VMEM_SPEC] * 2,
    )(dgpre, dgpost, dgmem, dvg, dvb, dws, dbs, dsink, drel, loss_vec, buckets)


def _adamw(w, g, m, v):
    m2 = ADAM_B1 * m + (1.0 - ADAM_B1) * g
    v2 = ADAM_B2 * v + (1.0 - ADAM_B2) * (g * g)
    m_hat = m2 / (1.0 - ADAM_B1 ** ADAM_STEP)
    v_hat = v2 / (1.0 - ADAM_B2 ** ADAM_STEP)
    delta = -ADAM_LR * (m_hat / (jnp.sqrt(v_hat) + ADAM_EPS) + ADAM_WD * w)
    return delta, m2, v2


ADAM_STEPS = 4


def _adamw_all(shard_grads, shard_w, shard_m, shard_v, ra, rb, small_w, small_m, small_v):
    n_sh, n = len(shard_w), len(small_w)

    def body(*refs):
        sh_in, refs = refs[:4 * n_sh], refs[4 * n_sh:]
        ra_ref, rb_ref, refs = refs[0], refs[1], refs[2:]
        w_refs, m_refs, v_refs, refs = refs[:n], refs[n:2 * n], refs[2 * n:3 * n], refs[3 * n:]
        sh_out, outs = refs[:4 * n_sh], refs[4 * n_sh:]
        for k in range(n_sh):
            g = sh_in[k][...]
            delta, m2, v2 = _adamw(sh_in[n_sh + k][...], g, sh_in[2 * n_sh + k][...], sh_in[3 * n_sh + k][...])
            for ref, val in zip(sh_out[4 * k:4 * k + 4], (g, delta, m2, v2)):
                ref[...] = val

        @pl.when(pl.program_id(0) == 0)
        def _():
            g_outs, d_outs, m_outs, v_outs = outs[:n], outs[n:2 * n], outs[2 * n:3 * n], outs[3 * n:4 * n]
            ga, gb = ra_ref[0], rb_ref[0]
            for chip in range(1, N_CHIPS):
                ga = ga + ra_ref[chip]
                gb = gb + rb_ref[chip]
            outs[4 * n][...] = ga[ROW_LOSS:ROW_LOSS + 1, 0:128]
            grads = [ga[0:1, :], ga[1:2, :], ga[2:3, :], ga[3:4, :A_WIDTH], ga[3:4, A_WIDTH:],
                     gb[ROW_WS:ROW_WS + A_GROUPS * CHUNK, :].reshape(A_GROUPS, CHUNK, CHUNK),
                     gb[ROW_BS:ROW_BS + A_GROUPS, :], gb[ROW_SINK:ROW_SINK + 1, 0:4],
                     gb[ROW_REL:ROW_REL + 4, 0:N_BUCKETS]]
            for k in range(n):
                delta, m2, v2 = _adamw(w_refs[k][...], grads[k], m_refs[k][...], v_refs[k][...])
                g_outs[k][...] = grads[k]
                d_outs[k][...] = delta
                m_outs[k][...] = m2
                v_outs[k][...] = v2

    def rows_block(a):
        assert a.shape[0] % (8 * ADAM_STEPS) == 0
        return pl.BlockSpec((a.shape[0] // ADAM_STEPS, a.shape[1]), lambda i: (i, 0))

    sh_specs = [rows_block(w) for w in shard_w]
    small_in = [ra, rb, *small_w, *small_m, *small_v]
    small_out_shapes = [jax.ShapeDtypeStruct(w.shape, F32) for w in small_w] * 4 + [jax.ShapeDtypeStruct((1, 128), F32)]
    out = pl.pallas_call(
        body, name="adamw_all", grid=(ADAM_STEPS,),
        out_shape=[jax.ShapeDtypeStruct(w.shape, F32) for w in shard_w for _ in range(4)] + small_out_shapes,
        in_specs=sh_specs * 4 + [_full_spec(a.shape) for a in small_in],
        out_specs=[spec for spec in sh_specs for _ in range(4)] + [_full_spec(s.shape) for s in small_out_shapes],
        compiler_params=pltpu.CompilerParams(vmem_limit_bytes=VMEM_LIMIT),
    )(*shard_grads, *shard_w, *shard_m, *shard_v, *small_in)
    return [out[4 * k:4 * k + 4] for k in range(n_sh)], out[4 * n_sh:]


def kernel(x, mem, pre_norm_g, post_norm_g, mem_norm_g, w_in, w_mem_kv, v_norm_g, v_norm_b, w_spatial, b_spatial, attn_sinks, rel_bias, w_out, loss_target, m_pre_norm_g, m_post_norm_g, m_mem_norm_g, m_w_in, m_w_mem_kv, m_v_norm_g, m_v_norm_b, m_w_spatial, m_b_spatial, m_attn_sinks, m_rel_bias, m_w_out, v_pre_norm_g, v_post_norm_g, v_mem_norm_g, v_w_in, v_w_mem_kv, v_v_norm_g, v_v_norm_b, v_w_spatial, v_b_spatial, v_attn_sinks, v_rel_bias, v_w_out):
    n_ex, seq, _ = x.shape
    n_tok = n_ex * seq
    x2 = x.reshape(n_tok, D_MODEL)
    tgt2 = loss_target.reshape(n_tok, D_MODEL)
    buckets = jnp.asarray(_bucket_map())
    shard_arr = (2 * lax.axis_index("x") + lax.axis_index("y")).astype(jnp.int32).reshape(1)
    w_sp = w_spatial[0]
    b_sp = jnp.broadcast_to(b_spatial[0][:, :, None], (A_GROUPS, CHUNK, CHUNK))
    w_in_t, m_w_in_t, v_w_in_t = (jnp.transpose(a[0]) for a in (w_in, m_w_in, v_w_in))
    rel_t, m_rel_t, v_rel_t = (jnp.transpose(a) for a in (rel_bias, m_rel_bias, v_rel_bias))

    x_arr = lax.axis_index("x").astype(jnp.int32).reshape(1)
    h_b, parts, (w_in_b, g_mkv, g_out), bias = _gather_and_project(
        x2, pre_norm_g, w_in_t, w_mem_kv[0], w_out[0], rel_t, buckets, x_arr)
    w_mkv_b = g_mkv.reshape(D_MODEL, 2 * MEM_WIDTH)
    w_out_b = g_out.reshape(MIX_WIDTH, D_MODEL)

    dout, dproj, dwmkv, dgmem, dwout, dvg, dvb, dws, dbs, dsink, drel, loss_vec, dgpost = _mix(
        parts, mem, x2, tgt2, v_norm_g, v_norm_b, w_sp, b_sp, attn_sinks, bias, w_out_b, post_norm_g, mem_norm_g,
        w_mkv_b, n_ex, seq)

    dx, dgpre = _backward_projection(x2, dout, dproj, pre_norm_g, w_in_b)
    small_a, small_b = _pack_small_grads(dgpre, dgpost, dgmem, dvg, dvb, dws, dbs, dsink, drel, loss_vec, buckets)

    shard_shapes = [w_mem_kv.shape[1:], w_out.shape[1:]]
    big = [g.reshape(N_CHIPS, 2, s[0] // 2, s[1]) for g, s in zip((dwmkv, dwout), shard_shapes)]
    (g_win, g_wmkv, g_wout), (ga, gb) = _reduce_gradients(dproj, h_b, big, [small_a, small_b], shard_arr)

    small_w = [pre_norm_g, post_norm_g, mem_norm_g, v_norm_g, v_norm_b, w_sp, b_spatial[0], attn_sinks, rel_t]
    small_m = [m_pre_norm_g, m_post_norm_g, m_mem_norm_g, m_v_norm_g, m_v_norm_b, m_w_spatial[0], m_b_spatial[0],
               m_attn_sinks, m_rel_t]
    small_v = [v_pre_norm_g, v_post_norm_g, v_mem_norm_g, v_v_norm_g, v_v_norm_b, v_w_spatial[0], v_b_spatial[0],
               v_attn_sinks, v_rel_t]
    big_out, small_out = _adamw_all(
        [g_win, g_wmkv, g_wout], [w_in_t, w_mem_kv[0], w_out[0]], [m_w_in_t, m_w_mem_kv[0], m_w_out[0]],
        [v_w_in_t, v_w_mem_kv[0], v_w_out[0]], ga, gb, small_w, small_m, small_v)
    n_small = len(small_w)

    outputs = [small_out[4 * n_small][0, 0], dx.reshape(x.shape)]
    for kind in range(4):
        s = small_out[kind * n_small:(kind + 1) * n_small]
        outputs += [s[0], s[1], s[2], jnp.transpose(big_out[0][kind])[None], big_out[1][kind][None], s[3], s[4],
                    s[5][None], s[6][None], s[7], jnp.transpose(s[8]), big_out[2][kind][None]]
    return tuple(outputs)
```

```python
import functools

import numpy as np
import jax
import jax.numpy as jnp
from jax import lax
from jax.experimental import pallas as pl
from jax.experimental.pallas import tpu as pltpu

F32 = jnp.float32
BF16 = jnp.bfloat16
MESH = pl.DeviceIdType.MESH

D_MODEL = 1024
CHUNK = 128
A_WIDTH = 512
A_GROUPS = 4
SWA_WIDTH = 256
KV_WIDTH = 128
MEM_WIDTH = 256
MEM_LEN = 256
MIX_WIDTH = 1024
IN_WIDTH = 2816
N_BUCKETS = 32
MAX_DISTANCE = 128
EPS = 1e-6
NEG = -1e30
QK_SCALE = 0.125
HALF_HEAD_PAIR = 64

ADAM_LR = 0.001
ADAM_B1 = 0.9
ADAM_B2 = 0.999
ADAM_EPS = 1e-08
ADAM_WD = 0.01
ADAM_STEP = 10

N_CHIPS = 4
TILE_CHUNKS = 2
TILE = TILE_CHUNKS * CHUNK
PROJ_TILE = 512
VMEM_LIMIT = 56 * 1024 * 1024

SMALL_A_ROWS = 8
ROW_LOSS = 4
ROW_WS = 0
ROW_BS = 512
ROW_SINK = 520
ROW_REL = 528
SMALL_B_ROWS = 536


def _mm(a, b):
    return lax.dot_general(a, b, (((1,), (0,)), ((), ())), preferred_element_type=F32)


def _mm_nt(a, b):
    return lax.dot_general(a, b, (((1,), (1,)), ((), ())), preferred_element_type=F32)


def _mm_tn(a, b):
    return lax.dot_general(a, b, (((0,), (0,)), ((), ())), preferred_element_type=F32)


def _bucket_map():
    qi = np.arange(CHUNK)[:, None]
    kj = np.arange(2 * CHUNK)[None, :]
    n = np.maximum(qi + CHUNK - kj, 0)
    max_exact = N_BUCKETS // 2
    large = max_exact + (np.log(np.maximum(n, 1) / max_exact) / np.log(MAX_DISTANCE / max_exact)
                         * (N_BUCKETS - max_exact)).astype(np.int32)
    large = np.minimum(large, N_BUCKETS - 1)
    return np.where(n < max_exact, n, large).astype(np.int32)


_GELU_C = 0.7978845608028654
_GELU_A = 0.044715
_GELU_K1 = 2.0 * _GELU_C
_GELU_K2 = 2.0 * _GELU_C * _GELU_A


def _gelu(x):
    x2 = x * x
    s = 1.0 / (1.0 + jnp.exp(x * (-_GELU_K1 - _GELU_K2 * x2)))
    return x * s, (s, x2)


def _gelu_grad(x, saved):
    s, x2 = saved
    return s + x * (s * (1.0 - s)) * (_GELU_K1 + 3.0 * _GELU_K2 * x2)


def _sigmoid(x):
    return 1.0 / (1.0 + jnp.exp(-x))


def _lane_lo(shape):
    return lax.broadcasted_iota(jnp.int32, shape, 1) < HALF_HEAD_PAIR


def _swa_variants(t):
    lo = _lane_lo(t.shape)
    tr = pltpu.roll(t, HALF_HEAD_PAIR, 1)
    zero = jnp.zeros_like(t)
    return (jnp.where(lo, t, zero).astype(BF16), jnp.where(lo, zero, tr).astype(BF16),
            jnp.where(lo, tr, zero).astype(BF16), jnp.where(lo, zero, t).astype(BF16))


def _swa_unvariants(d0, d1, d2, d3):
    lo = _lane_lo(d0.shape)
    zero = jnp.zeros_like(d0)
    rolled = jnp.where(lo, zero, d1) + jnp.where(lo, d2, zero)
    return jnp.where(lo, d0, zero) + jnp.where(lo, zero, d3) + pltpu.roll(rolled, HALF_HEAD_PAIR, 1)


def _mem_variants(t):
    out = []
    for pair in range(2):
        tp = t[:, pair * 128:(pair + 1) * 128]
        lo = _lane_lo(tp.shape)
        zero = jnp.zeros_like(tp)
        out.append(jnp.where(lo, tp, zero).astype(BF16))
        out.append(jnp.where(lo, zero, tp).astype(BF16))
    return out


def _mem_unvariants(d0, d1, d2, d3):
    lo = _lane_lo(d0.shape)
    return jnp.concatenate([jnp.where(lo, d0, d1), jnp.where(lo, d2, d3)], axis=-1)


def _softmax(logits, sinks):
    m = jnp.max(logits, axis=-1, keepdims=True)
    if sinks is not None:
        m = jnp.maximum(m, sinks)
    p = jnp.exp(logits - m)
    den = jnp.sum(p, axis=-1, keepdims=True)
    if sinks is None:
        return p * (1.0 / den), None
    es = jnp.exp(sinks - m)
    inv = 1.0 / (den + es)
    return p * inv, es * inv


def _band_valid(with_prev):
    qi = lax.broadcasted_iota(jnp.int32, (CHUNK, 2 * CHUNK), 0)
    kj = lax.broadcasted_iota(jnp.int32, (CHUNK, 2 * CHUNK), 1)
    in_cur = (kj >= CHUNK) & (kj - CHUNK <= qi)
    if not with_prev:
        return in_cur
    return in_cur | ((kj < CHUNK) & (kj > qi))


def _causal_weights(ws_ref):
    row = lax.broadcasted_iota(jnp.int32, (CHUNK, CHUNK), 0)
    col = lax.broadcasted_iota(jnp.int32, (CHUNK, CHUNK), 1)
    return [jnp.where(row >= col, ws_ref[g], 0.0).astype(BF16) for g in range(A_GROUPS)]


def _rows_to_lanes(a, n):
    return jnp.concatenate([a[c * CHUNK:(c + 1) * CHUNK] for c in range(n)], axis=1)


def _lanes_to_rows(a, n):
    w = a.shape[1] // n
    return jnp.concatenate([a[:, c * w:(c + 1) * w] for c in range(n)], axis=0)


def _stack_heads(pair01, pair23):
    return jnp.concatenate([pair01[:, :256], pair01[:, 256:], pair23[:, :256], pair23[:, 256:]], axis=0)


def _pair_heads(s, r):
    return (jnp.concatenate([s[0:r], s[r:2 * r]], axis=1), jnp.concatenate([s[2 * r:3 * r], s[3 * r:4 * r]], axis=1))


def _pair_operands(variants):
    return (jnp.concatenate(variants[0:2], axis=0), jnp.concatenate(variants[2:4], axis=0))


def _split_pair_grads(d_pairs):
    return d_pairs[0][:256], d_pairs[0][256:], d_pairs[1][:256], d_pairs[1][256:]


def _halves_bf16(a):
    return (a[:, :128].astype(BF16), a[:, 128:].astype(BF16))


def _group_a_forward(au, av, vg, vb, wm, bs_rows):
    gu, tu = _gelu(au)
    gv, tv = _gelu(av)
    ya, res = [], []
    for g in range(A_GROUPS):
        sl = slice(g * 128, (g + 1) * 128)
        xg = gv[:, sl]
        xc = xg - jnp.mean(xg, axis=-1, keepdims=True)
        rstd = lax.rsqrt(jnp.mean(xc * xc, axis=-1, keepdims=True) + EPS)
        xhat = xc * rstd
        vn = _rows_to_lanes((xhat * vg[:, sl] + vb[:, sl]).astype(BF16), TILE_CHUNKS)
        s = _lanes_to_rows(_mm(wm[g], vn), TILE_CHUNKS) + bs_rows[g]
        ya.append(gu[:, sl] * s)
        res.append((xhat, rstd, vn, s))
    return ya, dict(gu=gu, tu=tu, tv=tv, groups=res)


def _attention_logits(qp, k_pairs):
    return _stack_heads(_mm_nt(qp[0], k_pairs[0]), _mm_nt(qp[1], k_pairs[1]))


def _attention_out(p, v_pairs, r):
    pp = _pair_heads(p.astype(BF16), r)
    return jnp.concatenate([_mm(pp[0], v_pairs[0]), _mm(pp[1], v_pairs[1])], axis=-1), pp


def _attention_dprobs(do_pairs, v_pairs):
    return _stack_heads(_mm_nt(do_pairs[0], v_pairs[0]), _mm_nt(do_pairs[1], v_pairs[1]))


def _softmax_backward(p, dp):
    delta = jnp.sum(p * dp, axis=-1, keepdims=True)
    return p * (dp - delta), delta


def _attention_grads(dl, pp, do_pairs, qp, k_pairs, r):
    dlp = _pair_heads(dl.astype(BF16), r)
    dq = jnp.concatenate([_mm(dlp[0], k_pairs[0]), _mm(dlp[1], k_pairs[1])], axis=-1)
    dk = (_mm_tn(dlp[0], qp[0]), _mm_tn(dlp[1], qp[1]))
    dv = (_mm_tn(pp[0], do_pairs[0]), _mm_tn(pp[1], do_pairs[1]))
    return dq, dk, dv


def _tile_specs(n_tiles_ex, width):
    return pl.BlockSpec((TILE, width), lambda b, i: (b * n_tiles_ex + jnp.minimum(i, n_tiles_ex - 1), 0))


def _prev_chunk_spec(n_tiles_ex, width):
    def index(b, i):
        chunk = TILE_CHUNKS * jnp.minimum(i, n_tiles_ex - 1)
        return (b * n_tiles_ex * TILE_CHUNKS + jnp.maximum(chunk - 1, 0), 0)
    return pl.BlockSpec((CHUNK, width), index)


def _full_spec(shape):
    zeros = (0,) * len(shape)
    return pl.BlockSpec(shape, lambda *_: zeros)


SMEM_SPEC = pl.BlockSpec(memory_space=pltpu.SMEM)
ANY_SPEC = pl.BlockSpec(memory_space=pl.ANY)
VMEM_SPEC = pl.BlockSpec(memory_space=pltpu.VMEM)


def _fill_bias(rel_ref, bk_ref, out_ref):
    bk = bk_ref[...]
    for h in range(4):
        acc = jnp.zeros((CHUNK, 2 * CHUNK), F32)
        for b in range(N_BUCKETS):
            acc = jnp.where(bk == b, rel_ref[h, b], acc)
        for t, with_prev in enumerate((True, False)):
            out_ref[t, h * CHUNK:(h + 1) * CHUNK, :] = jnp.where(_band_valid(with_prev), acc, NEG)


PROJ_WIDTHS = (A_WIDTH, A_WIDTH, SWA_WIDTH, KV_WIDTH, KV_WIDTH, MEM_WIDTH, MIX_WIDTH)
PROJ_OFFSETS = tuple(int(v) for v in np.cumsum((0,) + PROJ_WIDTHS))


MXU_TILE = 256
HALF_WIDTH = IN_WIDTH // 2
PHASE_COLS = (HALF_WIDTH // MXU_TILE * MXU_TILE, IN_WIDTH - HALF_WIDTH // MXU_TILE * MXU_TILE)


def _phase_columns(phase, chip_x):
    if phase == 0:
        return 0 if chip_x == 0 else IN_WIDTH - PHASE_COLS[0]
    return PHASE_COLS[0] if chip_x == 0 else 0


def _phase_parts(phase, chip_x):
    start = _phase_columns(phase, chip_x)
    return [(k, PROJ_OFFSETS[k] - start) for k in range(len(PROJ_WIDTHS))
            if start <= PROJ_OFFSETS[k] and PROJ_OFFSETS[k + 1] <= start + PHASE_COLS[phase]]


def _gather_and_project(x2, g_pre, w_in_s, w_mkv_s, w_out_s, rel_bias_t, buckets, x_arr):
    n_tok = x2.shape[0]
    n_tiles = n_tok // PROJ_TILE
    last = n_tiles - 1
    shapes = [w_in_s.shape, w_mkv_s.shape, w_out_s.shape]
    n_w = len(shapes)

    def body(x_sref, x_ref, g_ref, win_hbm, wmkv_hbm, wout_hbm, rel_ref, bk_ref, h_ref, *refs):
        part_refs, refs = refs[:len(PROJ_WIDTHS)], refs[len(PROJ_WIDTHS):]
        bias_ref, refs = refs[0], refs[1:]
        gin_hbm, gmkv_hbm, gout_hbm, wg, stage_in, stage_mkv, stage_out, own_mkv, own_out, h_all = refs[:10]
        send_sems, recv_sems, local_sems = refs[10:]
        p, t = pl.program_id(0), pl.program_id(1)
        x, y, c = lax.axis_index("x"), lax.axis_index("y"), lax.axis_index("c")
        me, sibling = (x, y, c), (x, y, 1 - c)
        my_shard = 2 * x + y
        gathered = [wg, gmkv_hbm, gout_hbm]

        def half_rows(w, shard, half):
            rows = shapes[w][0] // 2
            if w == 0:
                return wg.at[pl.ds(pl.multiple_of(shard * shapes[0][0] + half * rows, 16), rows), :]
            return gathered[w].at[shard, pl.ds(half * rows, rows), :]

        def first(w, rel):
            src = half_rows(w, my_shard, c) if w == 0 else (own_mkv, own_out)[w - 1].at[
                pl.ds(c * (shapes[w][0] // 2), shapes[w][0] // 2), :]
            k = 3 * w + rel - 1
            return pltpu.make_async_remote_copy(
                src_ref=src, dst_ref=half_rows(w, my_shard, c), send_sem=send_sems.at[k], recv_sem=recv_sems.at[k],
                device_id=(x ^ (rel >> 1), y ^ (rel & 1), c), device_id_type=MESH)

        def landed(w, rel):
            k = 3 * w + rel - 1
            ref = half_rows(w, my_shard ^ rel, c)
            return pltpu.make_async_remote_copy(src_ref=ref, dst_ref=ref, send_sem=send_sems.at[k],
                                                recv_sem=recv_sems.at[k], device_id=me, device_id_type=MESH)

        def passed(w, rel, half, to):
            k = 9 + 3 * w + rel - 1
            ref = half_rows(w, my_shard ^ rel, half)
            return pltpu.make_async_remote_copy(src_ref=ref, dst_ref=ref, send_sem=send_sems.at[k],
                                                recv_sem=recv_sems.at[k], device_id=to, device_id_type=MESH)

        def pass_on(w, rels):
            for rel in rels:
                landed(w, rel).wait_recv()
                passed(w, rel, c, sibling).start()
            for rel in rels:
                passed(w, rel, 1 - c, me).wait_recv()

        own_stores = [pltpu.make_async_copy(own_mkv, gmkv_hbm.at[my_shard], local_sems.at[3]),
                      pltpu.make_async_copy(own_out, gout_hbm.at[my_shard], local_sems.at[4])]

        @pl.when((p == 0) & (t == 0))
        def _():
            loads = [pltpu.make_async_copy(src, dst, local_sems.at[k]) for k, (src, dst) in enumerate(
                ((win_hbm, stage_in), (wmkv_hbm, stage_mkv), (wout_hbm, stage_out)))]
            for cp in loads:
                cp.start()
            loads[0].wait()
            wg[pl.ds(pl.multiple_of(my_shard * shapes[0][0], 16), shapes[0][0]), :] = stage_in[...].astype(BF16)
            for rel in (1, 2):
                first(0, rel).start()
            loads[1].wait()
            loads[2].wait()
            own_mkv[...] = stage_mkv[...].astype(BF16)
            own_out[...] = stage_out[...].astype(BF16)
            for cp in own_stores:
                cp.start()
            _fill_bias(rel_ref, bk_ref, bias_ref)
            pass_on(0, (1,))
            first(0, 3).start()

        @pl.when((p == 0) & (t == n_tiles // 2))
        def _():
            for w in (1, 2):
                for rel in (1, 2, 3):
                    first(w, rel).start()

        store = pltpu.make_async_copy(wg, gin_hbm, local_sems.at[5])

        @pl.when((p == 1) & (t == 0))
        def _():
            pass_on(0, (2, 3))
            store.start()

        @pl.when((p == 1) & (t == n_tiles // 2))
        def _():
            for w in (1, 2):
                pass_on(w, (1, 2, 3))

        tile_rows = pl.ds(pl.multiple_of(t * PROJ_TILE, PROJ_TILE), PROJ_TILE)

        def project(h, phase):
            start = jnp.where(x_sref[0] == 0, _phase_columns(phase, 0), _phase_columns(phase, 1))
            proj = _mm_nt(h, wg[pl.ds(pl.multiple_of(start, MXU_TILE), PHASE_COLS[phase]), :])
            for chip_x in range(2):
                @pl.when(x_sref[0] == chip_x)
                def _():
                    for k, lo in _phase_parts(phase, chip_x):
                        part_refs[k][...] = proj[:, lo:lo + PROJ_WIDTHS[k]]

        @pl.when(p == 0)
        def _():
            xv = x_ref[...]
            r = lax.rsqrt(jnp.mean(xv * xv, axis=-1, keepdims=True) + EPS)
            h = (xv * r * g_ref[...]).astype(BF16)
            h_ref[...] = h
            h_all[tile_rows, :] = h
            project(h, 0)

        @pl.when(p == 1)
        def _():
            project(h_all[tile_rows, :], 1)

        @pl.when((p == 1) & (t == last))
        def _():
            for w in range(n_w):
                for rel in (1, 2, 3):
                    first(w, rel).wait_send()
                    passed(w, rel, c, sibling).wait_send()
            for cp in own_stores:
                cp.wait()
            store.wait()

    def written_in(k):
        phase_on = [next(ph for ph in range(2) if k in dict(_phase_parts(ph, chip_x))) for chip_x in range(2)]

        def index(p, t, xs):
            phase = jnp.where(xs[0] == 0, phase_on[0], phase_on[1])
            return (jnp.where(p == phase, t, jnp.where(p < phase, 0, last)), 0)
        return index

    part_specs = [pl.BlockSpec((PROJ_TILE, PROJ_WIDTHS[k]), written_in(k)) for k in range(len(PROJ_WIDTHS))]
    vmem = pltpu.VMEM
    out = pl.pallas_call(
        body, name="gather_and_project",
        out_shape=[jax.ShapeDtypeStruct((n_tok, D_MODEL), BF16)]
        + [jax.ShapeDtypeStruct((n_tok, w), F32) for w in PROJ_WIDTHS]
        + [jax.ShapeDtypeStruct((2, 4 * CHUNK, 2 * CHUNK), F32)]
        + [jax.ShapeDtypeStruct((N_CHIPS * shapes[0][0], shapes[0][1]), BF16)]
        + [jax.ShapeDtypeStruct((N_CHIPS,) + s, BF16) for s in shapes[1:]],
        grid_spec=pltpu.PrefetchScalarGridSpec(
            num_scalar_prefetch=1, grid=(2, n_tiles),
            in_specs=[pl.BlockSpec((PROJ_TILE, D_MODEL), lambda p, t, xs: (jnp.where(p == 0, t, last), 0)),
                      pl.BlockSpec((1, D_MODEL), lambda p, t, xs: (0, 0)), ANY_SPEC, ANY_SPEC, ANY_SPEC, SMEM_SPEC,
                      pl.BlockSpec(buckets.shape, lambda p, t, xs: (0, 0))],
            out_specs=[pl.BlockSpec((PROJ_TILE, D_MODEL), lambda p, t, xs: (jnp.where(p == 0, t, last), 0))]
            + part_specs + [pl.BlockSpec((2, 4 * CHUNK, 2 * CHUNK), lambda p, t, xs: (0, 0, 0))] + [ANY_SPEC] * 3,
            scratch_shapes=[vmem((N_CHIPS * shapes[0][0], shapes[0][1]), BF16), vmem(shapes[0], F32),
                            vmem(shapes[1], F32), vmem(shapes[2], F32), vmem(shapes[1], BF16), vmem(shapes[2], BF16),
                            vmem((n_tok, D_MODEL), BF16),
                            pltpu.SemaphoreType.DMA((18,)), pltpu.SemaphoreType.DMA((18,)),
                            pltpu.SemaphoreType.DMA((6,))]),
        compiler_params=pltpu.CompilerParams(vmem_limit_bytes=VMEM_LIMIT),
    )(x_arr, x2, g_pre, w_in_s, w_mkv_s, w_out_s, rel_bias_t, buckets)
    n_parts = len(PROJ_WIDTHS)
    return out[0], list(out[1:1 + n_parts]), out[2 + n_parts:], out[1 + n_parts]


def _load_chunk(j, i, sk_ref, sv_ref, skp_ref, svp_ref):
    rows = slice(j * CHUNK, (j + 1) * CHUNK)
    if j == 0:
        k_prev, v_prev, table = skp_ref[...], svp_ref[...], jnp.where(i > 0, 0, 1)
    else:
        prev = slice((j - 1) * CHUNK, j * CHUNK)
        k_prev, v_prev, table = sk_ref[prev, :], sv_ref[prev, :], 0
    k_pairs = _pair_operands(_swa_variants(jnp.concatenate([k_prev, sk_ref[rows, :]], axis=0)))
    v_pairs = _pair_operands(_swa_variants(jnp.concatenate([v_prev, sv_ref[rows, :]], axis=0)))
    return rows, k_pairs, v_pairs, table


def _tile_constants(ws_ref, bs_ref, sink_ref, mkv_v):
    wm = _causal_weights(ws_ref)
    bs_rows = [jnp.concatenate([bs_ref[g]] * TILE_CHUNKS, axis=0) for g in range(A_GROUPS)]
    sink_col = jnp.max(jnp.concatenate([jnp.full((CHUNK, 128), sink_ref[0, h], F32) for h in range(4)] * TILE_CHUNKS,
                                       axis=0), axis=-1, keepdims=True)
    mk_pairs = _pair_operands(_mem_variants(mkv_v[:, :MEM_WIDTH]))
    mv_pairs = _pair_operands(_mem_variants(mkv_v[:, MEM_WIDTH:]))
    return wm, bs_rows, sink_col, mk_pairs, mv_pairs


def _mix(parts, mem, x2, tgt2, v_g, v_b, w_sp, b_sp, sinks, bias, w_out, g_post, g_mem, w_mkv, n_ex, seq):
    n_tiles_ex = seq // TILE
    n_tok = n_ex * seq
    au, av, sq, sk, sv, mq, z = parts
    col = dict(zip(("au", "av", "sq", "sk", "sv", "mq", "z"),
                   (slice(PROJ_OFFSETS[k], PROJ_OFFSETS[k + 1]) for k in range(len(PROJ_WIDTHS)))))
    before_kv, after_kv = slice(0, col["sk"].start), slice(col["sv"].stop, IN_WIDTH)

    def body(au_ref, av_ref, sq_ref, sk_ref, sv_ref, skp_ref, svp_ref, mq_ref, z_ref, mem_ref, x_ref, tgt_ref,
             vg_ref, vb_ref, ws_ref, bs_ref, sink_ref, bias_ref, wout_ref, gpost_ref, gmem_ref, wmkv_ref,
             dout_ref, dproj_ref, dwmkv_ref, dgmem_ref, dwout_ref, dvg_ref, dvb_ref, dws_ref, dbs_ref, dsink_ref,
             drel_ref, loss_ref, dgpost_ref, carry_dp, carry_k, carry_v, memn_s, mkv_s, dmkv_s):
        b, i = pl.program_id(0), pl.program_id(1)

        @pl.when((b == 0) & (i == 0))
        def _():
            for ref in (dwmkv_ref, dgmem_ref, dwout_ref, dvg_ref, dvb_ref, dws_ref, dbs_ref, dsink_ref, drel_ref,
                        loss_ref, dgpost_ref):
                ref[...] = jnp.zeros_like(ref)

        def normalized_mem():
            m = mem_ref[0]
            return m * lax.rsqrt(jnp.mean(m * m, axis=-1, keepdims=True) + EPS)

        @pl.when(i == 0)
        def _():
            memn_s[...] = (normalized_mem() * gmem_ref[...]).astype(BF16)
            mkv_s[...] = _mm(memn_s[...], wmkv_ref[...])
            dmkv_s[...] = jnp.zeros_like(dmkv_s)
            carry_k[...] = jnp.zeros_like(carry_k)
            carry_v[...] = jnp.zeros_like(carry_v)

        @pl.when(i > 0)
        def _():
            dproj_ref[:, before_kv] = carry_dp[:, before_kv]
            dproj_ref[:, after_kv] = carry_dp[:, after_kv]

        @pl.when(i < n_tiles_ex)
        def _():
            wm, bs_rows, sink_col, mk_pairs, mv_pairs = _tile_constants(ws_ref, bs_ref, sink_ref, mkv_s[...])
            vg = vg_ref[...]

            au_v, av_v = au_ref[...], av_ref[...]
            ya, res = _group_a_forward(au_v, av_v, vg, vb_ref[...], wm, bs_rows)
            swa, logits, yb = [], [], []
            for j in range(TILE_CHUNKS):
                rows, k_pairs, v_pairs, table = _load_chunk(j, i, sk_ref, sv_ref, skp_ref, svp_ref)
                qp = _halves_bf16(sq_ref[rows, :] * QK_SCALE)
                logits.append(_attention_logits(qp, k_pairs) + bias_ref[table])
                swa.append([rows, k_pairs, v_pairs, qp])
            p_swa, sink_p = _softmax(jnp.concatenate(logits, axis=0), sink_col)
            for j in range(TILE_CHUNKS):
                out, pp = _attention_out(p_swa[j * 4 * CHUNK:(j + 1) * 4 * CHUNK], swa[j][2], CHUNK)
                yb.append(out)
                swa[j].append(pp)
            mqp = _halves_bf16(mq_ref[...] * QK_SCALE)
            pm, _ = _softmax(_attention_logits(mqp, mk_pairs), None)
            yc, ppm = _attention_out(pm, mv_pairs, TILE)
            ycat = jnp.concatenate(ya + [jnp.concatenate(yb, axis=0), yc], axis=-1)

            zv = z_ref[...]
            sig = _sigmoid(zv)
            sz = zv * sig
            y_b = (ycat * sz).astype(BF16)
            o = _mm(y_b, wout_ref[...])
            r2 = lax.rsqrt(jnp.mean(o * o, axis=-1, keepdims=True) + EPS)
            nrm = o * r2
            gp = gpost_ref[...]
            diff = x_ref[...] + nrm * gp - tgt_ref[...]
            loss_ref[...] += jnp.sum(diff * diff) * (0.5 / D_MODEL)
            dout = diff * (1.0 / D_MODEL)
            dout_ref[...] = dout
            dgpost_ref[...] += jnp.sum(dout * nrm, axis=0, keepdims=True)
            dn = dout * gp
            do_b = (r2 * (dn - nrm * jnp.mean(dn * nrm, axis=-1, keepdims=True))).astype(BF16)
            dwout_ref[...] += _mm_tn(y_b, do_b)
            dy = _mm_nt(do_b, wout_ref[...])
            carry_dp[:, col["z"]] = (dy * ycat * (sig * (1.0 + zv * (1.0 - sig)))).astype(BF16)
            dyc = dy * sz

            dgu, dgv = [], []
            for g in range(A_GROUPS):
                sl = slice(g * 128, (g + 1) * 128)
                xhat, rstd, vn, s = res["groups"][g]
                dya = dyc[:, sl]
                dgu.append(dya * s)
                ds = dya * res["gu"][:, sl]
                dbs_ref[:, sl] += sum(ds[c * CHUNK:(c + 1) * CHUNK] for c in range(TILE_CHUNKS))
                ds_b = _rows_to_lanes(ds.astype(BF16), TILE_CHUNKS)
                dws_ref[g] += _mm_nt(ds_b, vn)
                dvn = _lanes_to_rows(_mm_tn(wm[g], ds_b), TILE_CHUNKS)
                dvg_ref[:, sl] += jnp.sum(dvn * xhat, axis=0, keepdims=True)
                dvb_ref[:, sl] += jnp.sum(dvn, axis=0, keepdims=True)
                dxh = dvn * vg[:, sl]
                dgv.append(rstd * (dxh - jnp.mean(dxh, axis=-1, keepdims=True)
                                   - xhat * jnp.mean(dxh * xhat, axis=-1, keepdims=True)))
            carry_dp[:, col["au"]] = (jnp.concatenate(dgu, axis=-1) * _gelu_grad(au_v, res["tu"])).astype(BF16)
            carry_dp[:, col["av"]] = (jnp.concatenate(dgv, axis=-1) * _gelu_grad(av_v, res["tv"])).astype(BF16)

            do_pairs = [_halves_bf16(dyc[rows, A_WIDTH:A_WIDTH + SWA_WIDTH]) for rows, *_ in swa]
            dl_swa, delta = _softmax_backward(p_swa, jnp.concatenate(
                [_attention_dprobs(do_pairs[j], swa[j][2]) for j in range(TILE_CHUNKS)], axis=0))
            sink_terms = sink_p * delta
            lane4 = lax.broadcasted_iota(jnp.int32, (1, 128), 1)
            dsink_vec = jnp.zeros((1, 128), F32)
            for h in range(4):
                head_sum = sum(jnp.sum(sink_terms[(4 * j + h) * CHUNK:(4 * j + h + 1) * CHUNK])
                               for j in range(TILE_CHUNKS))
                dsink_vec = dsink_vec + jnp.where(lane4 == h, -head_sum, 0.0)
            dsink_ref[...] += dsink_vec
            drel_ref[...] += sum(dl_swa[j * 4 * CHUNK:(j + 1) * 4 * CHUNK] for j in range(TILE_CHUNKS))
            dk_parts, dv_parts = [], []
            for j, (rows, k_pairs, v_pairs, qp, pp) in enumerate(swa):
                dq, dk, dv = _attention_grads(dl_swa[j * 4 * CHUNK:(j + 1) * 4 * CHUNK], pp, do_pairs[j], qp, k_pairs,
                                              CHUNK)
                carry_dp[rows, col["sq"]] = (dq * QK_SCALE).astype(BF16)
                dk_parts.append(_swa_unvariants(*_split_pair_grads(dk)))
                dv_parts.append(_swa_unvariants(*_split_pair_grads(dv)))

            dc_pairs = _halves_bf16(dyc[:, A_WIDTH + SWA_WIDTH:])
            dl_mem, _ = _softmax_backward(pm, _attention_dprobs(dc_pairs, mv_pairs))
            dmq, dmk, dmv = _attention_grads(dl_mem, ppm, dc_pairs, mqp, mk_pairs, TILE)
            carry_dp[:, col["mq"]] = (dmq * QK_SCALE).astype(BF16)
            dmkv_s[...] += jnp.concatenate([_mem_unvariants(*_split_pair_grads(dmk)),
                                            _mem_unvariants(*_split_pair_grads(dmv))], axis=-1)

            for parts_c, carry, cols in ((dk_parts, carry_k, col["sk"]), (dv_parts, carry_v, col["sv"])):
                @pl.when(i > 0)
                def _():
                    dproj_ref[:, cols] = (carry[...] + jnp.concatenate(
                        [jnp.zeros((TILE - CHUNK, KV_WIDTH), F32), parts_c[0][:CHUNK]], axis=0)).astype(BF16)
                new = [parts_c[0][CHUNK:]]
                for j in range(1, TILE_CHUNKS):
                    new[-1] = new[-1] + parts_c[j][:CHUNK]
                    new.append(parts_c[j][CHUNK:])
                carry[...] = jnp.concatenate(new, axis=0)

        @pl.when(i == n_tiles_ex)
        def _():
            dproj_ref[:, col["sk"]] = carry_k[...].astype(BF16)
            dproj_ref[:, col["sv"]] = carry_v[...].astype(BF16)
            d_b = dmkv_s[...].astype(BF16)
            dwmkv_ref[...] += _mm_tn(memn_s[...], d_b)
            dgmem_ref[...] += jnp.sum(_mm_nt(d_b, wmkv_ref[...]) * normalized_mem(), axis=0, keepdims=True)

    tile = functools.partial(_tile_specs, n_tiles_ex)
    prev = functools.partial(_prev_chunk_spec, n_tiles_ex)
    late = pl.BlockSpec((TILE, IN_WIDTH), lambda b, i: (b * n_tiles_ex + jnp.maximum(i - 1, 0), 0))
    return pl.pallas_call(
        body, name="mix", grid=(n_ex, n_tiles_ex + 1),
        out_shape=[jax.ShapeDtypeStruct((n_tok, D_MODEL), F32), jax.ShapeDtypeStruct((n_tok, IN_WIDTH), BF16),
                   jax.ShapeDtypeStruct((D_MODEL, 2 * MEM_WIDTH), F32), jax.ShapeDtypeStruct((1, D_MODEL), F32),
                   jax.ShapeDtypeStruct((MIX_WIDTH, D_MODEL), F32), jax.ShapeDtypeStruct((1, A_WIDTH), F32),
                   jax.ShapeDtypeStruct((1, A_WIDTH), F32), jax.ShapeDtypeStruct((A_GROUPS, CHUNK, CHUNK), F32),
                   jax.ShapeDtypeStruct((CHUNK, A_WIDTH), F32), jax.ShapeDtypeStruct((1, 128), F32),
                   jax.ShapeDtypeStruct((4 * CHUNK, 2 * CHUNK), F32), jax.ShapeDtypeStruct((1, 128), F32),
                   jax.ShapeDtypeStruct((1, D_MODEL), F32)],
        in_specs=[tile(A_WIDTH), tile(A_WIDTH), tile(SWA_WIDTH), tile(KV_WIDTH), tile(KV_WIDTH),
                  prev(KV_WIDTH), prev(KV_WIDTH), tile(MEM_WIDTH), tile(MIX_WIDTH),
                  pl.BlockSpec((1, MEM_LEN, D_MODEL), lambda b, i: (b, 0, 0)),
                  tile(D_MODEL), tile(D_MODEL),
                  _full_spec((1, A_WIDTH)), _full_spec((1, A_WIDTH)), _full_spec((A_GROUPS, CHUNK, CHUNK)),
                  _full_spec((A_GROUPS, CHUNK, CHUNK)), SMEM_SPEC, _full_spec((2, 4 * CHUNK, 2 * CHUNK)),
                  _full_spec((MIX_WIDTH, D_MODEL)), _full_spec((1, D_MODEL)), _full_spec((1, D_MODEL)),
                  _full_spec((D_MODEL, 2 * MEM_WIDTH))],
        out_specs=[tile(D_MODEL), late, _full_spec((D_MODEL, 2 * MEM_WIDTH)), _full_spec((1, D_MODEL)),
                   _full_spec((MIX_WIDTH, D_MODEL)), _full_spec((1, A_WIDTH)), _full_spec((1, A_WIDTH)),
                   _full_spec((A_GROUPS, CHUNK, CHUNK)), _full_spec((CHUNK, A_WIDTH)), _full_spec((1, 128)),
                   _full_spec((4 * CHUNK, 2 * CHUNK)), _full_spec((1, 128)), _full_spec((1, D_MODEL))],
        scratch_shapes=[pltpu.VMEM((TILE, IN_WIDTH), BF16), pltpu.VMEM((TILE, KV_WIDTH), F32),
                        pltpu.VMEM((TILE, KV_WIDTH), F32), pltpu.VMEM((MEM_LEN, D_MODEL), BF16),
                        pltpu.VMEM((MEM_LEN, 2 * MEM_WIDTH), F32), pltpu.VMEM((MEM_LEN, 2 * MEM_WIDTH), F32)],
        compiler_params=pltpu.CompilerParams(vmem_limit_bytes=VMEM_LIMIT),
    )(au, av, sq, sk, sv, sk, sv, mq, z, mem, x2, tgt2, v_g, v_b, w_sp, b_sp, sinks, bias, w_out, g_post, g_mem,
      w_mkv)


BWD_PROJ_TILE = 512


def _fill_small_grads(dgpre_ref, dgpost_ref, dgmem_ref, dvg_ref, dvb_ref, dws_ref, dbs_ref, dsink_ref, drel_ref,
                      loss_ref, bk_ref, a_ref, b_ref):
    a_ref[...] = jnp.zeros_like(a_ref)
    b_ref[...] = jnp.zeros_like(b_ref)
    a_ref[0:1, :] = dgpre_ref[...]
    a_ref[1:2, :] = dgpost_ref[...]
    a_ref[2:3, :] = dgmem_ref[...]
    a_ref[3:4, :] = jnp.concatenate([dvg_ref[...], dvb_ref[...]], axis=-1)
    a_ref[ROW_LOSS:ROW_LOSS + 1, 0:128] = loss_ref[...]
    row = lax.broadcasted_iota(jnp.int32, (CHUNK, CHUNK), 0)
    col = lax.broadcasted_iota(jnp.int32, (CHUNK, CHUNK), 1)
    for g in range(A_GROUPS):
        b_ref[ROW_WS + g * CHUNK:ROW_WS + (g + 1) * CHUNK, :] = jnp.where(row >= col, dws_ref[g], 0.0)
        by_token = jnp.transpose(dbs_ref[:, g * 128:(g + 1) * 128])
        b_ref[ROW_BS + g:ROW_BS + g + 1, :] = jnp.sum(by_token, axis=0, keepdims=True)
    b_ref[ROW_SINK:ROW_SINK + 1, :] = dsink_ref[...]
    bk = bk_ref[...]
    rel_row = lax.broadcasted_iota(jnp.int32, (8, 128), 0)
    rel_col = lax.broadcasted_iota(jnp.int32, (8, 128), 1)
    rel = jnp.zeros((8, 128), F32)
    for h in range(4):
        acc = drel_ref[h * CHUNK:(h + 1) * CHUNK, :]
        for b in range(N_BUCKETS):
            rel = jnp.where((rel_row == h) & (rel_col == b), jnp.sum(jnp.where(bk == b, acc, 0.0)), rel)
    b_ref[ROW_REL:ROW_REL + 8, :] = rel


def _backward_projection(x2, dout, dproj, g_pre, w_in_t, small_parts):
    n_tok = x2.shape[0]
    n_steps = n_tok // BWD_PROJ_TILE
    n_small = len(small_parts)

    def body(x_ref, dout_ref, dp_ref, g_ref, w_hbm, *refs):
        small_refs, (dx_ref, a_ref, b_ref, w_vmem, dgpre, sem) = refs[:n_small], refs[n_small:]
        step = pl.program_id(0)

        @pl.when(step == 0)
        def _():
            load = pltpu.make_async_copy(w_hbm, w_vmem, sem)
            load.start()
            dgpre[...] = jnp.zeros_like(dgpre)
            load.wait()

        xv = x_ref[...]
        r = lax.rsqrt(jnp.mean(xv * xv, axis=-1, keepdims=True) + EPS)
        xn = xv * r
        dh = _mm(dp_ref[...], w_vmem[...])
        dgpre[...] += jnp.sum(dh * xn, axis=0, keepdims=True)
        dhg = dh * g_ref[...]
        dx_ref[...] = r * (dhg - xn * jnp.mean(dhg * xn, axis=-1, keepdims=True)) + dout_ref[...]

        @pl.when(step == n_steps - 1)
        def _():
            _fill_small_grads(dgpre, *small_refs, a_ref, b_ref)

    row = lambda w: pl.BlockSpec((BWD_PROJ_TILE, w), lambda i: (i, 0))
    return pl.pallas_call(
        body, name="backward_projection", grid=(n_steps,),
        out_shape=[jax.ShapeDtypeStruct((n_tok, D_MODEL), F32), jax.ShapeDtypeStruct((SMALL_A_ROWS, D_MODEL), F32),
                   jax.ShapeDtypeStruct((SMALL_B_ROWS, 128), F32)],
        in_specs=[row(D_MODEL), row(D_MODEL), row(IN_WIDTH), _full_spec((1, D_MODEL)), ANY_SPEC]
        + [_full_spec(a.shape) for a in small_parts],
        out_specs=[row(D_MODEL), _full_spec((SMALL_A_ROWS, D_MODEL)), _full_spec((SMALL_B_ROWS, 128))],
        scratch_shapes=[pltpu.VMEM((IN_WIDTH, D_MODEL), BF16), pltpu.VMEM((1, D_MODEL), F32),
                        pltpu.SemaphoreType.DMA],
        input_output_aliases={1: 0},
        compiler_params=pltpu.CompilerParams(vmem_limit_bytes=VMEM_LIMIT),
    )(x2, dout, dproj, g_pre, w_in_t, *small_parts)


SHARD_ROWS = IN_WIDTH // N_CHIPS
SHARD_WINDOW = 768
SHARD_HALF = SHARD_ROWS // 2
DWIN_TILE = 2048
N_REL = N_CHIPS - 1


def _shard_window_start(shard):
    return (shard * SHARD_ROWS // 128) * 128


def _reduce_gradients(dproj, h, big, small, shard_arr):
    n_tok = h.shape[0]
    tile = min(DWIN_TILE, n_tok)
    n_sub = n_tok // tile
    last = N_CHIPS - 1
    n_big, n_small = len(big), len(small)
    big_half = [g.shape[2:] for g in big]
    sem_big_d2d = 2 * N_CHIPS
    sem_big_ici = sem_big_d2d + n_big
    sem_big_swap = sem_big_ici + N_REL * n_big
    sem_small_d2d = sem_big_swap + n_big
    sem_small_ici = sem_small_d2d + n_small
    n_sems = sem_small_ici + N_REL * n_small
    loc_small = n_big
    loc_out_win = loc_small + n_small
    loc_out_big = loc_out_win + 2
    loc_out_small = loc_out_big + 2 * n_big
    n_local = loc_out_small + n_small

    def relation_of_slot(s):
        return (s + 2) % N_REL + 1

    def shard_of_slot(s, my_shard):
        return my_shard ^ jnp.where(s == last, 0, relation_of_slot(s))

    def body(shard_ref, dp_ref, h_hbm, *refs):
        h_vmem, h_sem, refs = refs[-2], refs[-1], refs[:-2]
        big_hbm, refs = refs[:n_big], refs[n_big:]
        small_hbm, refs = refs[:n_small], refs[n_small:]
        out_hbm, refs = refs[0], refs[1:]
        big_out, refs = refs[:n_big], refs[n_big:]
        small_out, refs = refs[:n_small], refs[n_small:]
        part, recv_d2d, send_ici, recv_ici, mine_buf, other_buf = refs[:6]
        refs = refs[6:]
        big_own, big_recv, big_send, big_land, big_mine, big_other = (
            refs[k * n_big:(k + 1) * n_big] for k in range(6))
        refs = refs[6 * n_big:]
        small_own, small_recv, small_all = (refs[k * n_small:(k + 1) * n_small] for k in range(3))
        send_sems, recv_sems, local_sems = refs[3 * n_small:]

        s, t = pl.program_id(0), pl.program_id(1)
        x, y, c = lax.axis_index("x"), lax.axis_index("y"), lax.axis_index("c")
        my_chip = 2 * x + y
        sibling = (x, y, 1 - c)
        my_rows = pl.ds(pl.multiple_of(c * SHARD_HALF, 8), SHARD_HALF)
        other_rows = pl.ds(pl.multiple_of((1 - c) * SHARD_HALF, 8), SHARD_HALF)

        def remote(src, dst, k, to):
            return pltpu.make_async_remote_copy(src_ref=src, dst_ref=dst, send_sem=send_sems.at[k],
                                                recv_sem=recv_sems.at[k], device_id=to, device_id_type=MESH)

        def chip_at(rel):
            return (x ^ (rel >> 1), y ^ (rel & 1), c)

        def to_sibling(k):
            return remote(part.at[k % 2, other_rows, :], recv_d2d.at[k], k, sibling)

        def to_chip(k):
            return remote(send_ici.at[k], recv_ici.at[k], N_CHIPS + k, chip_at(relation_of_slot(k)))

        swap = remote(mine_buf, other_buf, 2 * N_CHIPS - 1, sibling)
        big_load = [pltpu.make_async_copy(big_hbm[w].at[:, pl.ds(c, 1)], big_own[w], local_sems.at[w])
                    for w in range(n_big)]
        big_to_sibling = [remote(big_hbm[w].at[:, pl.ds(1 - c, 1)], big_recv[w], sem_big_d2d + w, sibling)
                          for w in range(n_big)]
        big_to_chip = [[remote(big_send[w].at[k], big_land[w].at[k], sem_big_ici + N_REL * w + k, chip_at(k + 1))
                        for k in range(N_REL)] for w in range(n_big)]
        big_swap = [remote(big_mine[w], big_other[w], sem_big_swap + w, sibling) for w in range(n_big)]
        small_load = [pltpu.make_async_copy(small_hbm[i], small_own[i], local_sems.at[loc_small + i])
                      for i in range(n_small)]
        small_to_sibling = [remote(small_hbm[i], small_recv[i], sem_small_d2d + i, sibling) for i in range(n_small)]
        small_to_chip = [[remote(small_all[i].at[my_chip], small_all[i].at[my_chip],
                                 sem_small_ici + N_REL * i + k, chip_at(k + 1))
                          for k in range(N_REL)] for i in range(n_small)]

        @pl.when((s == 0) & (t == 0))
        def _():
            h_load = pltpu.make_async_copy(h_hbm, h_vmem, h_sem)
            h_load.start()
            for cp in big_load + big_to_sibling + small_load + small_to_sibling:
                cp.start()
            h_load.wait()

        @pl.when((s == 0) & (t == n_sub - 1))
        def _():
            for cp in big_load + small_load:
                cp.wait()
            for cp in big_to_sibling + small_to_sibling:
                cp.wait_recv()
                cp.wait_send()
            for w in range(n_big):
                for k in range(N_REL):
                    shard = my_chip ^ (k + 1)
                    big_send[w][k] = (big_own[w][shard, 0] + big_recv[w][shard, 0]).astype(BF16)
                    big_to_chip[w][k].start()
            for i in range(n_small):
                small_all[i][my_chip] = small_own[i][...] + small_recv[i][...]
                for k in range(N_REL):
                    small_to_chip[i][k].start()

        @pl.when((s > 0) & (t == jnp.where(s == last, 0, min(1, n_sub - 1))))
        def _():
            k = s - 1
            cp = to_sibling(k)
            cp.wait_recv()
            cp.wait_send()
            send_ici[k] = (part[k % 2, my_rows, :] + recv_d2d[k]).astype(BF16)
            to_chip(k).start()

        def big_rows(w, half):
            rows = big_half[w][0]
            return big_out[w].at[pl.ds(pl.multiple_of(half * rows, 8), rows), :]

        big_store_mine = [pltpu.make_async_copy(big_mine[w], big_rows(w, c), local_sems.at[loc_out_big + 2 * w])
                          for w in range(n_big)]
        big_store_other = [pltpu.make_async_copy(big_other[w], big_rows(w, 1 - c),
                                                 local_sems.at[loc_out_big + 2 * w + 1]) for w in range(n_big)]
        small_store = [pltpu.make_async_copy(small_all[i], small_out[i], local_sems.at[loc_out_small + i])
                       for i in range(n_small)]

        @pl.when((s == last) & (t == 0))
        def _():
            for w in range(n_big):
                total = big_own[w][my_chip, 0] + big_recv[w][my_chip, 0]
                for k in range(N_REL):
                    big_to_chip[w][k].wait_recv()
                    total = total + big_land[w][k].astype(F32)
                big_mine[w][...] = total
                big_swap[w].start()
                big_store_mine[w].start()
            for i in range(n_small):
                for k in range(N_REL):
                    small_to_chip[i][k].wait_recv()
                small_store[i].start()

        r = _mm_tn(dp_ref[...], h_vmem[pl.ds(pl.multiple_of(t * tile, tile), tile), :])
        odd = shard_of_slot(s, shard_ref[0]) % 2
        for parity in range(2):
            rows = r[64 * parity:64 * parity + SHARD_ROWS]

            @pl.when((odd == parity) & (t == 0))
            def _():
                part[s % 2] = rows

            @pl.when((odd == parity) & (t > 0))
            def _():
                part[s % 2] += rows

        @pl.when(t == n_sub - 1)
        def _():
            to_sibling(s).start()

        @pl.when((s == last) & (t == n_sub - 1))
        def _():
            cp = to_sibling(last)
            cp.wait_recv()
            cp.wait_send()
            total = part[last % 2, my_rows, :] + recv_d2d[last]
            for k in range(last):
                to_chip(k).wait_recv()
                total = total + recv_ici[k].astype(F32)
            mine_buf[...] = total
            swap.start()
            out_mine = pltpu.make_async_copy(mine_buf, out_hbm.at[my_rows, :], local_sems.at[0])
            out_mine.start()
            swap.wait_recv()
            out_other = pltpu.make_async_copy(other_buf, out_hbm.at[other_rows, :], local_sems.at[1])
            out_other.start()
            for w in range(n_big):
                big_swap[w].wait_recv()
                big_store_other[w].start()
            stores = [out_mine, out_other] + big_store_mine + big_store_other + small_store
            for k in range(last):
                to_chip(k).wait_send()
            swap.wait_send()
            for w in range(n_big):
                for k in range(N_REL):
                    big_to_chip[w][k].wait_send()
                big_swap[w].wait_send()
            for i in range(n_small):
                for k in range(N_REL):
                    small_to_chip[i][k].wait_send()
            for cp in stores:
                cp.wait()

    half = (SHARD_HALF, D_MODEL)
    vmem = pltpu.VMEM
    scratch = [vmem((2, SHARD_ROWS, D_MODEL), F32), vmem((N_CHIPS,) + half, F32),
               vmem((N_REL,) + half, BF16), vmem((N_REL,) + half, BF16), vmem(half, F32), vmem(half, F32)]
    scratch += [vmem((N_CHIPS, 1) + hs, F32) for hs in big_half] * 2
    scratch += [vmem((N_REL,) + hs, BF16) for hs in big_half] * 2
    scratch += [vmem(hs, F32) for hs in big_half] * 2
    scratch += [vmem(a.shape, F32) for a in small] * 2 + [vmem((N_CHIPS,) + a.shape, F32) for a in small]
    scratch += [pltpu.SemaphoreType.DMA((n_sems,)), pltpu.SemaphoreType.DMA((n_sems,)),
                pltpu.SemaphoreType.DMA((n_local,)), vmem(h.shape, BF16), pltpu.SemaphoreType.DMA]
    n_hbm = n_big + n_small
    out = pl.pallas_call(
        body, name="reduce_gradients",
        out_shape=[jax.ShapeDtypeStruct((SHARD_ROWS, D_MODEL), F32)]
        + [jax.ShapeDtypeStruct((2 * hs[0], hs[1]), F32) for hs in big_half]
        + [jax.ShapeDtypeStruct((N_CHIPS,) + a.shape, F32) for a in small],
        grid_spec=pltpu.PrefetchScalarGridSpec(
            num_scalar_prefetch=1, grid=(N_CHIPS, n_sub),
            in_specs=[pl.BlockSpec((pl.Element(tile), pl.Element(SHARD_WINDOW)),
                                   lambda s, t, m: (t * tile, _shard_window_start(shard_of_slot(s, m[0])))),
                      ANY_SPEC] + [ANY_SPEC] * n_hbm,
            out_specs=[ANY_SPEC] * (1 + n_hbm),
            scratch_shapes=scratch),
        compiler_params=pltpu.CompilerParams(vmem_limit_bytes=VMEM_LIMIT),
    )(shard_arr, dproj, h, *big, *small)
    return out[:1 + n_big], out[1 + n_big:]


def _adamw(w, g, m, v):
    m2 = ADAM_B1 * m + (1.0 - ADAM_B1) * g
    v2 = ADAM_B2 * v + (1.0 - ADAM_B2) * (g * g)
    m_hat = m2 / (1.0 - ADAM_B1 ** ADAM_STEP)
    v_hat = v2 / (1.0 - ADAM_B2 ** ADAM_STEP)
    delta = -ADAM_LR * (m_hat / (jnp.sqrt(v_hat) + ADAM_EPS) + ADAM_WD * w)
    return delta, m2, v2


ADAM_STEPS = 4


def _adamw_all(shard_grads, shard_w, shard_m, shard_v, ra, rb, small_w, small_m, small_v):
    n_sh, n = len(shard_w), len(small_w)

    def body(*refs):
        sh_in, refs = refs[:4 * n_sh], refs[4 * n_sh:]
        ra_ref, rb_ref, refs = refs[0], refs[1], refs[2:]
        w_refs, m_refs, v_refs, refs = refs[:n], refs[n:2 * n], refs[2 * n:3 * n], refs[3 * n:]
        sh_out, outs = refs[:4 * n_sh], refs[4 * n_sh:]
        for k in range(n_sh):
            g = sh_in[k][...]
            delta, m2, v2 = _adamw(sh_in[n_sh + k][...], g, sh_in[2 * n_sh + k][...], sh_in[3 * n_sh + k][...])
            for ref, val in zip(sh_out[4 * k:4 * k + 4], (g, delta, m2, v2)):
                ref[...] = val

        @pl.when(pl.program_id(0) == 0)
        def _():
            g_outs, d_outs, m_outs, v_outs = outs[:n], outs[n:2 * n], outs[2 * n:3 * n], outs[3 * n:4 * n]
            ga, gb = ra_ref[0], rb_ref[0]
            for chip in range(1, N_CHIPS):
                ga = ga + ra_ref[chip]
                gb = gb + rb_ref[chip]
            outs[4 * n][...] = ga[ROW_LOSS:ROW_LOSS + 1, 0:128]
            grads = [ga[0:1, :], ga[1:2, :], ga[2:3, :], ga[3:4, :A_WIDTH], ga[3:4, A_WIDTH:],
                     gb[ROW_WS:ROW_WS + A_GROUPS * CHUNK, :].reshape(A_GROUPS, CHUNK, CHUNK),
                     gb[ROW_BS:ROW_BS + A_GROUPS, :], gb[ROW_SINK:ROW_SINK + 1, 0:4],
                     gb[ROW_REL:ROW_REL + 4, 0:N_BUCKETS]]
            for k in range(n):
                delta, m2, v2 = _adamw(w_refs[k][...], grads[k], m_refs[k][...], v_refs[k][...])
                g_outs[k][...] = grads[k]
                d_outs[k][...] = delta
                m_outs[k][...] = m2
                v_outs[k][...] = v2

    def rows_block(a):
        assert a.shape[0] % (8 * ADAM_STEPS) == 0
        return pl.BlockSpec((a.shape[0] // ADAM_STEPS, a.shape[1]), lambda i: (i, 0))

    sh_specs = [rows_block(w) for w in shard_w]
    small_in = [ra, rb, *small_w, *small_m, *small_v]
    small_out_shapes = [jax.ShapeDtypeStruct(w.shape, F32) for w in small_w] * 4 + [jax.ShapeDtypeStruct((1, 128), F32)]
    out = pl.pallas_call(
        body, name="adamw_all", grid=(ADAM_STEPS,),
        out_shape=[jax.ShapeDtypeStruct(w.shape, F32) for w in shard_w for _ in range(4)] + small_out_shapes,
        in_specs=sh_specs * 4 + [_full_spec(a.shape) for a in small_in],
        out_specs=[spec for spec in sh_specs for _ in range(4)] + [_full_spec(s.shape) for s in small_out_shapes],
        compiler_params=pltpu.CompilerParams(vmem_limit_bytes=VMEM_LIMIT),
    )(*shard_grads, *shard_w, *shard_m, *shard_v, *small_in)
    return [out[4 * k:4 * k + 4] for k in range(n_sh)], out[4 * n_sh:]


def kernel(x, mem, pre_norm_g, post_norm_g, mem_norm_g, w_in, w_mem_kv, v_norm_g, v_norm_b, w_spatial, b_spatial, attn_sinks, rel_bias, w_out, loss_target, m_pre_norm_g, m_post_norm_g, m_mem_norm_g, m_w_in, m_w_mem_kv, m_v_norm_g, m_v_norm_b, m_w_spatial, m_b_spatial, m_attn_sinks, m_rel_bias, m_w_out, v_pre_norm_g, v_post_norm_g, v_mem_norm_g, v_w_in, v_w_mem_kv, v_v_norm_g, v_v_norm_b, v_w_spatial, v_b_spatial, v_attn_sinks, v_rel_bias, v_w_out):
    n_ex, seq, _ = x.shape
    n_tok = n_ex * seq
    x2 = x.reshape(n_tok, D_MODEL)
    tgt2 = loss_target.reshape(n_tok, D_MODEL)
    buckets = jnp.asarray(_bucket_map())
    shard_arr = (2 * lax.axis_index("x") + lax.axis_index("y")).astype(jnp.int32).reshape(1)
    w_sp = w_spatial[0]
    b_sp = jnp.broadcast_to(b_spatial[0][:, :, None], (A_GROUPS, CHUNK, CHUNK))
    w_in_t, m_w_in_t, v_w_in_t = (jnp.transpose(a[0]) for a in (w_in, m_w_in, v_w_in))
    rel_t, m_rel_t, v_rel_t = (jnp.transpose(a) for a in (rel_bias, m_rel_bias, v_rel_bias))

    x_arr = lax.axis_index("x").astype(jnp.int32).reshape(1)
    h_b, parts, (w_in_b, g_mkv, g_out), bias = _gather_and_project(
        x2, pre_norm_g, w_in_t, w_mem_kv[0], w_out[0], rel_t, buckets, x_arr)
    w_mkv_b = g_mkv.reshape(D_MODEL, 2 * MEM_WIDTH)
    w_out_b = g_out.reshape(MIX_WIDTH, D_MODEL)

    dout, dproj, dwmkv, dgmem, dwout, dvg, dvb, dws, dbs, dsink, drel, loss_vec, dgpost = _mix(
        parts, mem, x2, tgt2, v_norm_g, v_norm_b, w_sp, b_sp, attn_sinks, bias, w_out_b, post_norm_g, mem_norm_g,
        w_mkv_b, n_ex, seq)

    dx, small_a, small_b = _backward_projection(
        x2, dout, dproj, pre_norm_g, w_in_b, [dgpost, dgmem, dvg, dvb, dws, dbs, dsink, drel, loss_vec, buckets])

    shard_shapes = [w_mem_kv.shape[1:], w_out.shape[1:]]
    big = [g.reshape(N_CHIPS, 2, s[0] // 2, s[1]) for g, s in zip((dwmkv, dwout), shard_shapes)]
    (g_win, g_wmkv, g_wout), (ga, gb) = _reduce_gradients(dproj, h_b, big, [small_a, small_b], shard_arr)

    small_w = [pre_norm_g, post_norm_g, mem_norm_g, v_norm_g, v_norm_b, w_sp, b_spatial[0], attn_sinks, rel_t]
    small_m = [m_pre_norm_g, m_post_norm_g, m_mem_norm_g, m_v_norm_g, m_v_norm_b, m_w_spatial[0], m_b_spatial[0],
               m_attn_sinks, m_rel_t]
    small_v = [v_pre_norm_g, v_post_norm_g, v_mem_norm_g, v_v_norm_g, v_v_norm_b, v_w_spatial[0], v_b_spatial[0],
               v_attn_sinks, v_rel_t]
    big_out, small_out = _adamw_all(
        [g_win, g_wmkv, g_wout], [w_in_t, w_mem_kv[0], w_out[0]], [m_w_in_t, m_w_mem_kv[0], m_w_out[0]],
        [v_w_in_t, v_w_mem_kv[0], v_w_out[0]], ga, gb, small_w, small_m, small_v)
    n_small = len(small_w)

    outputs = [small_out[4 * n_small][0, 0], dx.reshape(x.shape)]
    for kind in range(4):
        s = small_out[kind * n_small:(kind + 1) * n_small]
        outputs += [s[0], s[1], s[2], jnp.transpose(big_out[0][kind])[None], big_out[1][kind][None], s[3], s[4],
                    s[5][None], s[6][None], s[7], jnp.transpose(s[8]), big_out[2][kind][None]]
    return tuple(outputs)
```

```python
import functools

import numpy as np
import jax
import jax.numpy as jnp
from jax import lax
from jax.experimental import pallas as pl
from jax.experimental.pallas import tpu as pltpu

F32 = jnp.float32
BF16 = jnp.bfloat16
MESH = pl.DeviceIdType.MESH

D_MODEL = 1024
CHUNK = 128
A_WIDTH = 512
A_GROUPS = 4
SWA_WIDTH = 256
KV_WIDTH = 128
MEM_WIDTH = 256
MEM_LEN = 256
MIX_WIDTH = 1024
IN_WIDTH = 2816
N_BUCKETS = 32
MAX_DISTANCE = 128
EPS = 1e-6
NEG = -1e30
QK_SCALE = 0.125
HALF_HEAD_PAIR = 64

ADAM_LR = 0.001
ADAM_B1 = 0.9
ADAM_B2 = 0.999
ADAM_EPS = 1e-08
ADAM_WD = 0.01
ADAM_STEP = 10

N_CHIPS = 4
TILE_CHUNKS = 2
TILE = TILE_CHUNKS * CHUNK
PROJ_TILE = 512
VMEM_LIMIT = 56 * 1024 * 1024

SMALL_A_ROWS = 8
ROW_LOSS = 4
ROW_WS = 0
ROW_BS = 512
ROW_SINK = 520
ROW_REL = 528
SMALL_B_ROWS = 536


def _mm(a, b):
    return lax.dot_general(a, b, (((1,), (0,)), ((), ())), preferred_element_type=F32)


def _mm_nt(a, b):
    return lax.dot_general(a, b, (((1,), (1,)), ((), ())), preferred_element_type=F32)


def _mm_tn(a, b):
    return lax.dot_general(a, b, (((0,), (0,)), ((), ())), preferred_element_type=F32)


def _bucket_map():
    qi = np.arange(CHUNK)[:, None]
    kj = np.arange(2 * CHUNK)[None, :]
    n = np.maximum(qi + CHUNK - kj, 0)
    max_exact = N_BUCKETS // 2
    large = max_exact + (np.log(np.maximum(n, 1) / max_exact) / np.log(MAX_DISTANCE / max_exact)
                         * (N_BUCKETS - max_exact)).astype(np.int32)
    large = np.minimum(large, N_BUCKETS - 1)
    return np.where(n < max_exact, n, large).astype(np.int32)


_GELU_C = 0.7978845608028654
_GELU_A = 0.044715
_GELU_K1 = 2.0 * _GELU_C
_GELU_K2 = 2.0 * _GELU_C * _GELU_A


def _gelu(x):
    x2 = x * x
    s = 1.0 / (1.0 + jnp.exp(x * (-_GELU_K1 - _GELU_K2 * x2)))
    return x * s, (s, x2)


def _gelu_grad(x, saved):
    s, x2 = saved
    return s + x * (s * (1.0 - s)) * (_GELU_K1 + 3.0 * _GELU_K2 * x2)


def _sigmoid(x):
    return 1.0 / (1.0 + jnp.exp(-x))


def _lane_lo(shape):
    return lax.broadcasted_iota(jnp.int32, shape, 1) < HALF_HEAD_PAIR


def _swa_variants(t):
    lo = _lane_lo(t.shape)
    tr = pltpu.roll(t, HALF_HEAD_PAIR, 1)
    zero = jnp.zeros_like(t)
    return (jnp.where(lo, t, zero).astype(BF16), jnp.where(lo, zero, tr).astype(BF16),
            jnp.where(lo, tr, zero).astype(BF16), jnp.where(lo, zero, t).astype(BF16))


def _swa_unvariants(d0, d1, d2, d3):
    lo = _lane_lo(d0.shape)
    zero = jnp.zeros_like(d0)
    rolled = jnp.where(lo, zero, d1) + jnp.where(lo, d2, zero)
    return jnp.where(lo, d0, zero) + jnp.where(lo, zero, d3) + pltpu.roll(rolled, HALF_HEAD_PAIR, 1)


def _mem_variants(t):
    out = []
    for pair in range(2):
        tp = t[:, pair * 128:(pair + 1) * 128]
        lo = _lane_lo(tp.shape)
        zero = jnp.zeros_like(tp)
        out.append(jnp.where(lo, tp, zero).astype(BF16))
        out.append(jnp.where(lo, zero, tp).astype(BF16))
    return out


def _mem_unvariants(d0, d1, d2, d3):
    lo = _lane_lo(d0.shape)
    return jnp.concatenate([jnp.where(lo, d0, d1), jnp.where(lo, d2, d3)], axis=-1)


def _softmax(logits, sinks):
    m = jnp.max(logits, axis=-1, keepdims=True)
    if sinks is not None:
        m = jnp.maximum(m, sinks)
    p = jnp.exp(logits - m)
    den = jnp.sum(p, axis=-1, keepdims=True)
    if sinks is None:
        return p * (1.0 / den), None
    es = jnp.exp(sinks - m)
    inv = 1.0 / (den + es)
    return p * inv, es * inv


def _band_valid(with_prev):
    qi = lax.broadcasted_iota(jnp.int32, (CHUNK, 2 * CHUNK), 0)
    kj = lax.broadcasted_iota(jnp.int32, (CHUNK, 2 * CHUNK), 1)
    in_cur = (kj >= CHUNK) & (kj - CHUNK <= qi)
    if not with_prev:
        return in_cur
    return in_cur | ((kj < CHUNK) & (kj > qi))


def _causal_weights(ws_ref):
    row = lax.broadcasted_iota(jnp.int32, (CHUNK, CHUNK), 0)
    col = lax.broadcasted_iota(jnp.int32, (CHUNK, CHUNK), 1)
    return [jnp.where(row >= col, ws_ref[g], 0.0).astype(BF16) for g in range(A_GROUPS)]


def _rows_to_lanes(a, n):
    return jnp.concatenate([a[c * CHUNK:(c + 1) * CHUNK] for c in range(n)], axis=1)


def _lanes_to_rows(a, n):
    w = a.shape[1] // n
    return jnp.concatenate([a[:, c * w:(c + 1) * w] for c in range(n)], axis=0)


def _stack_heads(pair01, pair23):
    return jnp.concatenate([pair01[:, :256], pair01[:, 256:], pair23[:, :256], pair23[:, 256:]], axis=0)


def _pair_heads(s, r):
    return (jnp.concatenate([s[0:r], s[r:2 * r]], axis=1), jnp.concatenate([s[2 * r:3 * r], s[3 * r:4 * r]], axis=1))


def _pair_operands(variants):
    return (jnp.concatenate(variants[0:2], axis=0), jnp.concatenate(variants[2:4], axis=0))


def _split_pair_grads(d_pairs):
    return d_pairs[0][:256], d_pairs[0][256:], d_pairs[1][:256], d_pairs[1][256:]


def _halves_bf16(a):
    return (a[:, :128].astype(BF16), a[:, 128:].astype(BF16))


def _group_a_forward(au, av, vg, vb, wm, bs_rows):
    gu, tu = _gelu(au)
    gv, tv = _gelu(av)
    ya, res = [], []
    for g in range(A_GROUPS):
        sl = slice(g * 128, (g + 1) * 128)
        xg = gv[:, sl]
        xc = xg - jnp.mean(xg, axis=-1, keepdims=True)
        rstd = lax.rsqrt(jnp.mean(xc * xc, axis=-1, keepdims=True) + EPS)
        xhat = xc * rstd
        vn = _rows_to_lanes((xhat * vg[:, sl] + vb[:, sl]).astype(BF16), TILE_CHUNKS)
        s = _lanes_to_rows(_mm(wm[g], vn), TILE_CHUNKS) + bs_rows[g]
        ya.append(gu[:, sl] * s)
        res.append((xhat, rstd, vn, s))
    return ya, dict(gu=gu, tu=tu, tv=tv, groups=res)


def _attention_logits(qp, k_pairs):
    return _stack_heads(_mm_nt(qp[0], k_pairs[0]), _mm_nt(qp[1], k_pairs[1]))


def _attention_out(p, v_pairs, r):
    pp = _pair_heads(p.astype(BF16), r)
    return jnp.concatenate([_mm(pp[0], v_pairs[0]), _mm(pp[1], v_pairs[1])], axis=-1), pp


def _attention_dprobs(do_pairs, v_pairs):
    return _stack_heads(_mm_nt(do_pairs[0], v_pairs[0]), _mm_nt(do_pairs[1], v_pairs[1]))


def _softmax_backward(p, dp):
    delta = jnp.sum(p * dp, axis=-1, keepdims=True)
    return p * (dp - delta), delta


def _attention_grads(dl, pp, do_pairs, qp, k_pairs, r):
    dlp = _pair_heads(dl.astype(BF16), r)
    dq = jnp.concatenate([_mm(dlp[0], k_pairs[0]), _mm(dlp[1], k_pairs[1])], axis=-1)
    dk = (_mm_tn(dlp[0], qp[0]), _mm_tn(dlp[1], qp[1]))
    dv = (_mm_tn(pp[0], do_pairs[0]), _mm_tn(pp[1], do_pairs[1]))
    return dq, dk, dv


def _tile_specs(n_tiles_ex, width):
    return pl.BlockSpec((TILE, width), lambda b, i: (b * n_tiles_ex + jnp.minimum(i, n_tiles_ex - 1), 0))


def _prev_chunk_spec(n_tiles_ex, width):
    def index(b, i):
        chunk = TILE_CHUNKS * jnp.minimum(i, n_tiles_ex - 1)
        return (b * n_tiles_ex * TILE_CHUNKS + jnp.maximum(chunk - 1, 0), 0)
    return pl.BlockSpec((CHUNK, width), index)


def _full_spec(shape):
    zeros = (0,) * len(shape)
    return pl.BlockSpec(shape, lambda *_: zeros)


SMEM_SPEC = pl.BlockSpec(memory_space=pltpu.SMEM)
ANY_SPEC = pl.BlockSpec(memory_space=pl.ANY)
VMEM_SPEC = pl.BlockSpec(memory_space=pltpu.VMEM)


def _fill_bias(rel_ref, bk_ref, out_ref):
    bk = bk_ref[...]
    for h in range(4):
        acc = jnp.zeros((CHUNK, 2 * CHUNK), F32)
        for b in range(N_BUCKETS):
            acc = jnp.where(bk == b, rel_ref[h, b], acc)
        for t, with_prev in enumerate((True, False)):
            out_ref[t, h * CHUNK:(h + 1) * CHUNK, :] = jnp.where(_band_valid(with_prev), acc, NEG)


PROJ_WIDTHS = (A_WIDTH, A_WIDTH, SWA_WIDTH, KV_WIDTH, KV_WIDTH, MEM_WIDTH, MIX_WIDTH)
PROJ_OFFSETS = tuple(int(v) for v in np.cumsum((0,) + PROJ_WIDTHS))


MXU_TILE = 256
HALF_WIDTH = IN_WIDTH // 2
PHASE_COLS = (HALF_WIDTH // MXU_TILE * MXU_TILE, IN_WIDTH - HALF_WIDTH // MXU_TILE * MXU_TILE)


def _phase_columns(phase, chip_x):
    if phase == 0:
        return 0 if chip_x == 0 else IN_WIDTH - PHASE_COLS[0]
    return PHASE_COLS[0] if chip_x == 0 else 0


def _phase_parts(phase, chip_x):
    start = _phase_columns(phase, chip_x)
    return [(k, PROJ_OFFSETS[k] - start) for k in range(len(PROJ_WIDTHS))
            if start <= PROJ_OFFSETS[k] and PROJ_OFFSETS[k + 1] <= start + PHASE_COLS[phase]]


def _gather_and_project(x2, g_pre, w_in_s, w_mkv_s, w_out_s, rel_bias_t, buckets, b_spatial, x_arr):
    n_tok = x2.shape[0]
    n_tiles = n_tok // PROJ_TILE
    last = n_tiles - 1
    shapes = [w_in_s.shape, w_mkv_s.shape, w_out_s.shape]
    n_w = len(shapes)

    def body(x_sref, x_ref, g_ref, win_hbm, wmkv_hbm, wout_hbm, rel_ref, bk_ref, bsp_ref, h_ref, *refs):
        part_refs, refs = refs[:len(PROJ_WIDTHS)], refs[len(PROJ_WIDTHS):]
        bias_ref, bs_ref, refs = refs[0], refs[1], refs[2:]
        gin_hbm, gmkv_hbm, gout_hbm, wg, stage_in, stage_mkv, stage_out, own_mkv, own_out, h_all = refs[:10]
        send_sems, recv_sems, local_sems = refs[10:]
        p, t = pl.program_id(0), pl.program_id(1)
        x, y, c = lax.axis_index("x"), lax.axis_index("y"), lax.axis_index("c")
        me, sibling = (x, y, c), (x, y, 1 - c)
        my_shard = 2 * x + y
        gathered = [wg, gmkv_hbm, gout_hbm]

        def half_rows(w, shard, half):
            rows = shapes[w][0] // 2
            if w == 0:
                return wg.at[pl.ds(pl.multiple_of(shard * shapes[0][0] + half * rows, 16), rows), :]
            return gathered[w].at[shard, pl.ds(half * rows, rows), :]

        def first(w, rel):
            src = half_rows(w, my_shard, c) if w == 0 else (own_mkv, own_out)[w - 1].at[
                pl.ds(c * (shapes[w][0] // 2), shapes[w][0] // 2), :]
            k = 3 * w + rel - 1
            return pltpu.make_async_remote_copy(
                src_ref=src, dst_ref=half_rows(w, my_shard, c), send_sem=send_sems.at[k], recv_sem=recv_sems.at[k],
                device_id=(x ^ (rel >> 1), y ^ (rel & 1), c), device_id_type=MESH)

        def landed(w, rel):
            k = 3 * w + rel - 1
            ref = half_rows(w, my_shard ^ rel, c)
            return pltpu.make_async_remote_copy(src_ref=ref, dst_ref=ref, send_sem=send_sems.at[k],
                                                recv_sem=recv_sems.at[k], device_id=me, device_id_type=MESH)

        def passed(w, rel, half, to):
            k = 9 + 3 * w + rel - 1
            ref = half_rows(w, my_shard ^ rel, half)
            return pltpu.make_async_remote_copy(src_ref=ref, dst_ref=ref, send_sem=send_sems.at[k],
                                                recv_sem=recv_sems.at[k], device_id=to, device_id_type=MESH)

        def pass_on(w, rels):
            for rel in rels:
                landed(w, rel).wait_recv()
                passed(w, rel, c, sibling).start()
            for rel in rels:
                passed(w, rel, 1 - c, me).wait_recv()

        own_stores = [pltpu.make_async_copy(own_mkv, gmkv_hbm.at[my_shard], local_sems.at[3]),
                      pltpu.make_async_copy(own_out, gout_hbm.at[my_shard], local_sems.at[4])]

        @pl.when((p == 0) & (t == 0))
        def _():
            loads = [pltpu.make_async_copy(src, dst, local_sems.at[k]) for k, (src, dst) in enumerate(
                ((win_hbm, stage_in), (wmkv_hbm, stage_mkv), (wout_hbm, stage_out)))]
            for cp in loads:
                cp.start()
            loads[0].wait()
            wg[pl.ds(pl.multiple_of(my_shard * shapes[0][0], 16), shapes[0][0]), :] = stage_in[...].astype(BF16)
            for rel in (1, 2):
                first(0, rel).start()
            loads[1].wait()
            loads[2].wait()
            own_mkv[...] = stage_mkv[...].astype(BF16)
            own_out[...] = stage_out[...].astype(BF16)
            for cp in own_stores:
                cp.start()
            _fill_bias(rel_ref, bk_ref, bias_ref)
            for g in range(A_GROUPS):
                bs_ref[g] = jnp.transpose(jnp.broadcast_to(bsp_ref[g:g + 1, :], (CHUNK, CHUNK)))
            pass_on(0, (1,))
            first(0, 3).start()

        @pl.when((p == 0) & (t == n_tiles // 2))
        def _():
            for w in (1, 2):
                for rel in (1, 2, 3):
                    first(w, rel).start()

        store = pltpu.make_async_copy(wg, gin_hbm, local_sems.at[5])

        @pl.when((p == 1) & (t == 0))
        def _():
            pass_on(0, (2, 3))
            store.start()

        @pl.when((p == 1) & (t == n_tiles // 2))
        def _():
            for w in (1, 2):
                pass_on(w, (1, 2, 3))

        tile_rows = pl.ds(pl.multiple_of(t * PROJ_TILE, PROJ_TILE), PROJ_TILE)

        def project(h, phase):
            start = jnp.where(x_sref[0] == 0, _phase_columns(phase, 0), _phase_columns(phase, 1))
            proj = _mm_nt(h, wg[pl.ds(pl.multiple_of(start, MXU_TILE), PHASE_COLS[phase]), :])
            for chip_x in range(2):
                @pl.when(x_sref[0] == chip_x)
                def _():
                    for k, lo in _phase_parts(phase, chip_x):
                        part_refs[k][...] = proj[:, lo:lo + PROJ_WIDTHS[k]]

        @pl.when(p == 0)
        def _():
            xv = x_ref[...]
            r = lax.rsqrt(jnp.mean(xv * xv, axis=-1, keepdims=True) + EPS)
            h = (xv * r * g_ref[...]).astype(BF16)
            h_ref[...] = h
            h_all[tile_rows, :] = h
            project(h, 0)

        @pl.when(p == 1)
        def _():
            project(h_all[tile_rows, :], 1)

        @pl.when((p == 1) & (t == last))
        def _():
            for w in range(n_w):
                for rel in (1, 2, 3):
                    first(w, rel).wait_send()
                    passed(w, rel, c, sibling).wait_send()
            for cp in own_stores:
                cp.wait()
            store.wait()

    def written_in(k):
        phase_on = [next(ph for ph in range(2) if k in dict(_phase_parts(ph, chip_x))) for chip_x in range(2)]

        def index(p, t, xs):
            phase = jnp.where(xs[0] == 0, phase_on[0], phase_on[1])
            return (jnp.where(p == phase, t, jnp.where(p < phase, 0, last)), 0)
        return index

    part_specs = [pl.BlockSpec((PROJ_TILE, PROJ_WIDTHS[k]), written_in(k)) for k in range(len(PROJ_WIDTHS))]
    vmem = pltpu.VMEM
    out = pl.pallas_call(
        body, name="gather_and_project",
        out_shape=[jax.ShapeDtypeStruct((n_tok, D_MODEL), BF16)]
        + [jax.ShapeDtypeStruct((n_tok, w), F32) for w in PROJ_WIDTHS]
        + [jax.ShapeDtypeStruct((2, 4 * CHUNK, 2 * CHUNK), F32), jax.ShapeDtypeStruct((A_GROUPS, CHUNK, CHUNK), F32)]
        + [jax.ShapeDtypeStruct((N_CHIPS * shapes[0][0], shapes[0][1]), BF16)]
        + [jax.ShapeDtypeStruct((N_CHIPS,) + s, BF16) for s in shapes[1:]],
        grid_spec=pltpu.PrefetchScalarGridSpec(
            num_scalar_prefetch=1, grid=(2, n_tiles),
            in_specs=[pl.BlockSpec((PROJ_TILE, D_MODEL), lambda p, t, xs: (jnp.where(p == 0, t, last), 0)),
                      pl.BlockSpec((1, D_MODEL), lambda p, t, xs: (0, 0)), ANY_SPEC, ANY_SPEC, ANY_SPEC, SMEM_SPEC,
                      pl.BlockSpec(buckets.shape, lambda p, t, xs: (0, 0)),
                      pl.BlockSpec(b_spatial.shape, lambda p, t, xs: (0, 0))],
            out_specs=[pl.BlockSpec((PROJ_TILE, D_MODEL), lambda p, t, xs: (jnp.where(p == 0, t, last), 0))]
            + part_specs + [pl.BlockSpec((2, 4 * CHUNK, 2 * CHUNK), lambda p, t, xs: (0, 0, 0)),
                            pl.BlockSpec((A_GROUPS, CHUNK, CHUNK), lambda p, t, xs: (0, 0, 0))] + [ANY_SPEC] * 3,
            scratch_shapes=[vmem((N_CHIPS * shapes[0][0], shapes[0][1]), BF16), vmem(shapes[0], F32),
                            vmem(shapes[1], F32), vmem(shapes[2], F32), vmem(shapes[1], BF16), vmem(shapes[2], BF16),
                            vmem((n_tok, D_MODEL), BF16),
                            pltpu.SemaphoreType.DMA((18,)), pltpu.SemaphoreType.DMA((18,)),
                            pltpu.SemaphoreType.DMA((6,))]),
        compiler_params=pltpu.CompilerParams(vmem_limit_bytes=VMEM_LIMIT),
    )(x_arr, x2, g_pre, w_in_s, w_mkv_s, w_out_s, rel_bias_t, buckets, b_spatial)
    n_parts = len(PROJ_WIDTHS)
    return out[0], list(out[1:1 + n_parts]), out[3 + n_parts:], out[1 + n_parts], out[2 + n_parts]


def _load_chunk(j, i, sk_ref, sv_ref, skp_ref, svp_ref):
    rows = slice(j * CHUNK, (j + 1) * CHUNK)
    if j == 0:
        k_prev, v_prev, table = skp_ref[...], svp_ref[...], jnp.where(i > 0, 0, 1)
    else:
        prev = slice((j - 1) * CHUNK, j * CHUNK)
        k_prev, v_prev, table = sk_ref[prev, :], sv_ref[prev, :], 0
    k_pairs = _pair_operands(_swa_variants(jnp.concatenate([k_prev, sk_ref[rows, :]], axis=0)))
    v_pairs = _pair_operands(_swa_variants(jnp.concatenate([v_prev, sv_ref[rows, :]], axis=0)))
    return rows, k_pairs, v_pairs, table


def _tile_constants(ws_ref, bs_ref, sink_ref, mkv_v):
    wm = _causal_weights(ws_ref)
    bs_rows = [jnp.concatenate([bs_ref[g]] * TILE_CHUNKS, axis=0) for g in range(A_GROUPS)]
    sink_col = jnp.max(jnp.concatenate([jnp.full((CHUNK, 128), sink_ref[0, h], F32) for h in range(4)] * TILE_CHUNKS,
                                       axis=0), axis=-1, keepdims=True)
    mk_pairs = _pair_operands(_mem_variants(mkv_v[:, :MEM_WIDTH]))
    mv_pairs = _pair_operands(_mem_variants(mkv_v[:, MEM_WIDTH:]))
    return wm, bs_rows, sink_col, mk_pairs, mv_pairs


def _mix(parts, mem, x2, tgt2, v_g, v_b, w_sp, b_sp, sinks, bias, w_out, g_post, g_mem, w_mkv, n_ex, seq):
    n_tiles_ex = seq // TILE
    n_tok = n_ex * seq
    au, av, sq, sk, sv, mq, z = parts
    col = dict(zip(("au", "av", "sq", "sk", "sv", "mq", "z"),
                   (slice(PROJ_OFFSETS[k], PROJ_OFFSETS[k + 1]) for k in range(len(PROJ_WIDTHS)))))
    before_kv, after_kv = slice(0, col["sk"].start), slice(col["sv"].stop, IN_WIDTH)

    def body(au_ref, av_ref, sq_ref, sk_ref, sv_ref, skp_ref, svp_ref, mq_ref, z_ref, mem_ref, x_ref, tgt_ref,
             vg_ref, vb_ref, ws_ref, bs_ref, sink_ref, bias_ref, wout_ref, gpost_ref, gmem_ref, wmkv_ref,
             dout_ref, dproj_ref, dwmkv_ref, dgmem_ref, dwout_ref, dvg_ref, dvb_ref, dws_ref, dbs_ref, dsink_ref,
             drel_ref, loss_ref, dgpost_ref, carry_dp, carry_k, carry_v, memn_s, mkv_s, dmkv_s):
        b, i = pl.program_id(0), pl.program_id(1)

        @pl.when((b == 0) & (i == 0))
        def _():
            for ref in (dwmkv_ref, dgmem_ref, dwout_ref, dvg_ref, dvb_ref, dws_ref, dbs_ref, dsink_ref, drel_ref,
                        loss_ref, dgpost_ref):
                ref[...] = jnp.zeros_like(ref)

        def normalized_mem():
            m = mem_ref[0]
            return m * lax.rsqrt(jnp.mean(m * m, axis=-1, keepdims=True) + EPS)

        @pl.when(i == 0)
        def _():
            memn_s[...] = (normalized_mem() * gmem_ref[...]).astype(BF16)
            mkv_s[...] = _mm(memn_s[...], wmkv_ref[...])
            dmkv_s[...] = jnp.zeros_like(dmkv_s)
            carry_k[...] = jnp.zeros_like(carry_k)
            carry_v[...] = jnp.zeros_like(carry_v)

        @pl.when(i > 0)
        def _():
            dproj_ref[:, before_kv] = carry_dp[:, before_kv]
            dproj_ref[:, after_kv] = carry_dp[:, after_kv]

        @pl.when(i < n_tiles_ex)
        def _():
            wm, bs_rows, sink_col, mk_pairs, mv_pairs = _tile_constants(ws_ref, bs_ref, sink_ref, mkv_s[...])
            vg = vg_ref[...]

            au_v, av_v = au_ref[...], av_ref[...]
            ya, res = _group_a_forward(au_v, av_v, vg, vb_ref[...], wm, bs_rows)
            swa, logits, yb = [], [], []
            for j in range(TILE_CHUNKS):
                rows, k_pairs, v_pairs, table = _load_chunk(j, i, sk_ref, sv_ref, skp_ref, svp_ref)
                qp = _halves_bf16(sq_ref[rows, :] * QK_SCALE)
                logits.append(_attention_logits(qp, k_pairs) + bias_ref[table])
                swa.append([rows, k_pairs, v_pairs, qp])
            p_swa, sink_p = _softmax(jnp.concatenate(logits, axis=0), sink_col)
            for j in range(TILE_CHUNKS):
                out, pp = _attention_out(p_swa[j * 4 * CHUNK:(j + 1) * 4 * CHUNK], swa[j][2], CHUNK)
                yb.append(out)
                swa[j].append(pp)
            mqp = _halves_bf16(mq_ref[...] * QK_SCALE)
            pm, _ = _softmax(_attention_logits(mqp, mk_pairs), None)
            yc, ppm = _attention_out(pm, mv_pairs, TILE)
            ycat = jnp.concatenate(ya + [jnp.concatenate(yb, axis=0), yc], axis=-1)

            zv = z_ref[...]
            sig = _sigmoid(zv)
            sz = zv * sig
            y_b = (ycat * sz).astype(BF16)
            o = _mm(y_b, wout_ref[...])
            r2 = lax.rsqrt(jnp.mean(o * o, axis=-1, keepdims=True) + EPS)
            nrm = o * r2
            gp = gpost_ref[...]
            diff = x_ref[...] + nrm * gp - tgt_ref[...]
            loss_ref[...] += jnp.sum(diff * diff) * (0.5 / D_MODEL)
            dout = diff * (1.0 / D_MODEL)
            dout_ref[...] = dout
            dgpost_ref[...] += jnp.sum(dout * nrm, axis=0, keepdims=True)
            dn = dout * gp
            do_b = (r2 * (dn - nrm * jnp.mean(dn * nrm, axis=-1, keepdims=True))).astype(BF16)
            dwout_ref[...] += _mm_tn(y_b, do_b)
            dy = _mm_nt(do_b, wout_ref[...])
            carry_dp[:, col["z"]] = (dy * ycat * (sig * (1.0 + zv * (1.0 - sig)))).astype(BF16)
            dyc = dy * sz

            dgu, dgv = [], []
            for g in range(A_GROUPS):
                sl = slice(g * 128, (g + 1) * 128)
                xhat, rstd, vn, s = res["groups"][g]
                dya = dyc[:, sl]
                dgu.append(dya * s)
                ds = dya * res["gu"][:, sl]
                dbs_ref[:, sl] += sum(ds[c * CHUNK:(c + 1) * CHUNK] for c in range(TILE_CHUNKS))
                ds_b = _rows_to_lanes(ds.astype(BF16), TILE_CHUNKS)
                dws_ref[g] += _mm_nt(ds_b, vn)
                dvn = _lanes_to_rows(_mm_tn(wm[g], ds_b), TILE_CHUNKS)
                dvg_ref[:, sl] += jnp.sum(dvn * xhat, axis=0, keepdims=True)
                dvb_ref[:, sl] += jnp.sum(dvn, axis=0, keepdims=True)
                dxh = dvn * vg[:, sl]
                dgv.append(rstd * (dxh - jnp.mean(dxh, axis=-1, keepdims=True)
                                   - xhat * jnp.mean(dxh * xhat, axis=-1, keepdims=True)))
            carry_dp[:, col["au"]] = (jnp.concatenate(dgu, axis=-1) * _gelu_grad(au_v, res["tu"])).astype(BF16)
            carry_dp[:, col["av"]] = (jnp.concatenate(dgv, axis=-1) * _gelu_grad(av_v, res["tv"])).astype(BF16)

            do_pairs = [_halves_bf16(dyc[rows, A_WIDTH:A_WIDTH + SWA_WIDTH]) for rows, *_ in swa]
            dl_swa, delta = _softmax_backward(p_swa, jnp.concatenate(
                [_attention_dprobs(do_pairs[j], swa[j][2]) for j in range(TILE_CHUNKS)], axis=0))
            sink_terms = sink_p * delta
            lane4 = lax.broadcasted_iota(jnp.int32, (1, 128), 1)
            dsink_vec = jnp.zeros((1, 128), F32)
            for h in range(4):
                head_sum = sum(jnp.sum(sink_terms[(4 * j + h) * CHUNK:(4 * j + h + 1) * CHUNK])
                               for j in range(TILE_CHUNKS))
                dsink_vec = dsink_vec + jnp.where(lane4 == h, -head_sum, 0.0)
            dsink_ref[...] += dsink_vec
            drel_ref[...] += sum(dl_swa[j * 4 * CHUNK:(j + 1) * 4 * CHUNK] for j in range(TILE_CHUNKS))
            dk_parts, dv_parts = [], []
            for j, (rows, k_pairs, v_pairs, qp, pp) in enumerate(swa):
                dq, dk, dv = _attention_grads(dl_swa[j * 4 * CHUNK:(j + 1) * 4 * CHUNK], pp, do_pairs[j], qp, k_pairs,
                                              CHUNK)
                carry_dp[rows, col["sq"]] = (dq * QK_SCALE).astype(BF16)
                dk_parts.append(_swa_unvariants(*_split_pair_grads(dk)))
                dv_parts.append(_swa_unvariants(*_split_pair_grads(dv)))

            dc_pairs = _halves_bf16(dyc[:, A_WIDTH + SWA_WIDTH:])
            dl_mem, _ = _softmax_backward(pm, _attention_dprobs(dc_pairs, mv_pairs))
            dmq, dmk, dmv = _attention_grads(dl_mem, ppm, dc_pairs, mqp, mk_pairs, TILE)
            carry_dp[:, col["mq"]] = (dmq * QK_SCALE).astype(BF16)
            dmkv_s[...] += jnp.concatenate([_mem_unvariants(*_split_pair_grads(dmk)),
                                            _mem_unvariants(*_split_pair_grads(dmv))], axis=-1)

            for parts_c, carry, cols in ((dk_parts, carry_k, col["sk"]), (dv_parts, carry_v, col["sv"])):
                @pl.when(i > 0)
                def _():
                    dproj_ref[:, cols] = (carry[...] + jnp.concatenate(
                        [jnp.zeros((TILE - CHUNK, KV_WIDTH), F32), parts_c[0][:CHUNK]], axis=0)).astype(BF16)
                new = [parts_c[0][CHUNK:]]
                for j in range(1, TILE_CHUNKS):
                    new[-1] = new[-1] + parts_c[j][:CHUNK]
                    new.append(parts_c[j][CHUNK:])
                carry[...] = jnp.concatenate(new, axis=0)

        @pl.when(i == n_tiles_ex)
        def _():
            dproj_ref[:, col["sk"]] = carry_k[...].astype(BF16)
            dproj_ref[:, col["sv"]] = carry_v[...].astype(BF16)
            d_b = dmkv_s[...].astype(BF16)
            dwmkv_ref[...] += _mm_tn(memn_s[...], d_b)
            dgmem_ref[...] += jnp.sum(_mm_nt(d_b, wmkv_ref[...]) * normalized_mem(), axis=0, keepdims=True)

    tile = functools.partial(_tile_specs, n_tiles_ex)
    prev = functools.partial(_prev_chunk_spec, n_tiles_ex)
    late = pl.BlockSpec((TILE, IN_WIDTH), lambda b, i: (b * n_tiles_ex + jnp.maximum(i - 1, 0), 0))
    return pl.pallas_call(
        body, name="mix", grid=(n_ex, n_tiles_ex + 1),
        out_shape=[jax.ShapeDtypeStruct((n_tok, D_MODEL), F32), jax.ShapeDtypeStruct((n_tok, IN_WIDTH), BF16),
                   jax.ShapeDtypeStruct((D_MODEL, 2 * MEM_WIDTH), F32), jax.ShapeDtypeStruct((1, D_MODEL), F32),
                   jax.ShapeDtypeStruct((MIX_WIDTH, D_MODEL), F32), jax.ShapeDtypeStruct((1, A_WIDTH), F32),
                   jax.ShapeDtypeStruct((1, A_WIDTH), F32), jax.ShapeDtypeStruct((A_GROUPS, CHUNK, CHUNK), F32),
                   jax.ShapeDtypeStruct((CHUNK, A_WIDTH), F32), jax.ShapeDtypeStruct((1, 128), F32),
                   jax.ShapeDtypeStruct((4 * CHUNK, 2 * CHUNK), F32), jax.ShapeDtypeStruct((1, 128), F32),
                   jax.ShapeDtypeStruct((1, D_MODEL), F32)],
        in_specs=[tile(A_WIDTH), tile(A_WIDTH), tile(SWA_WIDTH), tile(KV_WIDTH), tile(KV_WIDTH),
                  prev(KV_WIDTH), prev(KV_WIDTH), tile(MEM_WIDTH), tile(MIX_WIDTH),
                  pl.BlockSpec((1, MEM_LEN, D_MODEL), lambda b, i: (b, 0, 0)),
                  tile(D_MODEL), tile(D_MODEL),
                  _full_spec((1, A_WIDTH)), _full_spec((1, A_WIDTH)), _full_spec((A_GROUPS, CHUNK, CHUNK)),
                  _full_spec((A_GROUPS, CHUNK, CHUNK)), SMEM_SPEC, _full_spec((2, 4 * CHUNK, 2 * CHUNK)),
                  _full_spec((MIX_WIDTH, D_MODEL)), _full_spec((1, D_MODEL)), _full_spec((1, D_MODEL)),
                  _full_spec((D_MODEL, 2 * MEM_WIDTH))],
        out_specs=[tile(D_MODEL), late, _full_spec((D_MODEL, 2 * MEM_WIDTH)), _full_spec((1, D_MODEL)),
                   _full_spec((MIX_WIDTH, D_MODEL)), _full_spec((1, A_WIDTH)), _full_spec((1, A_WIDTH)),
                   _full_spec((A_GROUPS, CHUNK, CHUNK)), _full_spec((CHUNK, A_WIDTH)), _full_spec((1, 128)),
                   _full_spec((4 * CHUNK, 2 * CHUNK)), _full_spec((1, 128)), _full_spec((1, D_MODEL))],
        scratch_shapes=[pltpu.VMEM((TILE, IN_WIDTH), BF16), pltpu.VMEM((TILE, KV_WIDTH), F32),
                        pltpu.VMEM((TILE, KV_WIDTH), F32), pltpu.VMEM((MEM_LEN, D_MODEL), BF16),
                        pltpu.VMEM((MEM_LEN, 2 * MEM_WIDTH), F32), pltpu.VMEM((MEM_LEN, 2 * MEM_WIDTH), F32)],
        compiler_params=pltpu.CompilerParams(vmem_limit_bytes=VMEM_LIMIT),
    )(au, av, sq, sk, sv, sk, sv, mq, z, mem, x2, tgt2, v_g, v_b, w_sp, b_sp, sinks, bias, w_out, g_post, g_mem,
      w_mkv)


BWD_PROJ_TILE = 512


def _fill_small_grads(dgpre_ref, dgpost_ref, dgmem_ref, dvg_ref, dvb_ref, dws_ref, dbs_ref, dsink_ref, drel_ref,
                      loss_ref, bk_ref, a_ref, b_ref):
    a_ref[...] = jnp.zeros_like(a_ref)
    b_ref[...] = jnp.zeros_like(b_ref)
    a_ref[0:1, :] = dgpre_ref[...]
    a_ref[1:2, :] = dgpost_ref[...]
    a_ref[2:3, :] = dgmem_ref[...]
    a_ref[3:4, :] = jnp.concatenate([dvg_ref[...], dvb_ref[...]], axis=-1)
    a_ref[ROW_LOSS:ROW_LOSS + 1, 0:128] = loss_ref[...]
    row = lax.broadcasted_iota(jnp.int32, (CHUNK, CHUNK), 0)
    col = lax.broadcasted_iota(jnp.int32, (CHUNK, CHUNK), 1)
    for g in range(A_GROUPS):
        b_ref[ROW_WS + g * CHUNK:ROW_WS + (g + 1) * CHUNK, :] = jnp.where(row >= col, dws_ref[g], 0.0)
        by_token = jnp.transpose(dbs_ref[:, g * 128:(g + 1) * 128])
        b_ref[ROW_BS + g:ROW_BS + g + 1, :] = jnp.sum(by_token, axis=0, keepdims=True)
    b_ref[ROW_SINK:ROW_SINK + 1, :] = dsink_ref[...]
    bk = bk_ref[...]
    rel_row = lax.broadcasted_iota(jnp.int32, (8, 128), 0)
    rel_col = lax.broadcasted_iota(jnp.int32, (8, 128), 1)
    rel = jnp.zeros((8, 128), F32)
    for h in range(4):
        acc = drel_ref[h * CHUNK:(h + 1) * CHUNK, :]
        for b in range(N_BUCKETS):
            rel = jnp.where((rel_row == h) & (rel_col == b), jnp.sum(jnp.where(bk == b, acc, 0.0)), rel)
    b_ref[ROW_REL:ROW_REL + 8, :] = rel


def _backward_projection(x2, dout, dproj, g_pre, w_in_t, small_parts):
    n_tok = x2.shape[0]
    n_steps = n_tok // BWD_PROJ_TILE
    n_small = len(small_parts)

    def body(x_ref, dout_ref, dp_ref, g_ref, w_hbm, *refs):
        small_refs, (dx_ref, a_ref, b_ref, w_vmem, dgpre, sem) = refs[:n_small], refs[n_small:]
        step = pl.program_id(0)

        @pl.when(step == 0)
        def _():
            load = pltpu.make_async_copy(w_hbm, w_vmem, sem)
            load.start()
            dgpre[...] = jnp.zeros_like(dgpre)
            load.wait()

        xv = x_ref[...]
        r = lax.rsqrt(jnp.mean(xv * xv, axis=-1, keepdims=True) + EPS)
        xn = xv * r
        dh = _mm(dp_ref[...], w_vmem[...])
        dgpre[...] += jnp.sum(dh * xn, axis=0, keepdims=True)
        dhg = dh * g_ref[...]
        dx_ref[...] = r * (dhg - xn * jnp.mean(dhg * xn, axis=-1, keepdims=True)) + dout_ref[...]

        @pl.when(step == n_steps - 1)
        def _():
            _fill_small_grads(dgpre, *small_refs, a_ref, b_ref)

    row = lambda w: pl.BlockSpec((BWD_PROJ_TILE, w), lambda i: (i, 0))
    return pl.pallas_call(
        body, name="backward_projection", grid=(n_steps,),
        out_shape=[jax.ShapeDtypeStruct((n_tok, D_MODEL), F32), jax.ShapeDtypeStruct((SMALL_A_ROWS, D_MODEL), F32),
                   jax.ShapeDtypeStruct((SMALL_B_ROWS, 128), F32)],
        in_specs=[row(D_MODEL), row(D_MODEL), row(IN_WIDTH), _full_spec((1, D_MODEL)), ANY_SPEC]
        + [_full_spec(a.shape) for a in small_parts],
        out_specs=[row(D_MODEL), _full_spec((SMALL_A_ROWS, D_MODEL)), _full_spec((SMALL_B_ROWS, 128))],
        scratch_shapes=[pltpu.VMEM((IN_WIDTH, D_MODEL), BF16), pltpu.VMEM((1, D_MODEL), F32),
                        pltpu.SemaphoreType.DMA],
        input_output_aliases={1: 0},
        compiler_params=pltpu.CompilerParams(vmem_limit_bytes=VMEM_LIMIT),
    )(x2, dout, dproj, g_pre, w_in_t, *small_parts)


SHARD_ROWS = IN_WIDTH // N_CHIPS
SHARD_WINDOW = 768
SHARD_HALF = SHARD_ROWS // 2
DWIN_TILE = 2048
N_REL = N_CHIPS - 1


def _shard_window_start(shard):
    return (shard * SHARD_ROWS // 128) * 128


def _reduce_gradients(dproj, h, big, small, shard_arr):
    n_tok = h.shape[0]
    tile = min(DWIN_TILE, n_tok)
    n_sub = n_tok // tile
    last = N_CHIPS - 1
    n_big, n_small = len(big), len(small)
    big_half = [g.shape[2:] for g in big]
    sem_big_d2d = 2 * N_CHIPS
    sem_big_ici = sem_big_d2d + n_big
    sem_big_swap = sem_big_ici + N_REL * n_big
    sem_small_d2d = sem_big_swap + n_big
    sem_small_ici = sem_small_d2d + n_small
    n_sems = sem_small_ici + N_REL * n_small
    loc_small = n_big
    loc_out_win = loc_small + n_small
    loc_out_big = loc_out_win + 2
    loc_out_small = loc_out_big + 2 * n_big
    n_local = loc_out_small + n_small

    def relation_of_slot(s):
        return (s + 2) % N_REL + 1

    def shard_of_slot(s, my_shard):
        return my_shard ^ jnp.where(s == last, 0, relation_of_slot(s))

    def body(shard_ref, dp_ref, h_hbm, *refs):
        h_vmem, h_sem, refs = refs[-2], refs[-1], refs[:-2]
        big_hbm, refs = refs[:n_big], refs[n_big:]
        small_hbm, refs = refs[:n_small], refs[n_small:]
        out_hbm, refs = refs[0], refs[1:]
        big_out, refs = refs[:n_big], refs[n_big:]
        small_out, refs = refs[:n_small], refs[n_small:]
        part, recv_d2d, send_ici, recv_ici, mine_buf, other_buf = refs[:6]
        refs = refs[6:]
        big_own, big_recv, big_send, big_land, big_mine, big_other = (
            refs[k * n_big:(k + 1) * n_big] for k in range(6))
        refs = refs[6 * n_big:]
        small_own, small_recv, small_all = (refs[k * n_small:(k + 1) * n_small] for k in range(3))
        send_sems, recv_sems, local_sems = refs[3 * n_small:]

        s, t = pl.program_id(0), pl.program_id(1)
        x, y, c = lax.axis_index("x"), lax.axis_index("y"), lax.axis_index("c")
        my_chip = 2 * x + y
        sibling = (x, y, 1 - c)
        my_rows = pl.ds(pl.multiple_of(c * SHARD_HALF, 8), SHARD_HALF)
        other_rows = pl.ds(pl.multiple_of((1 - c) * SHARD_HALF, 8), SHARD_HALF)

        def remote(src, dst, k, to):
            return pltpu.make_async_remote_copy(src_ref=src, dst_ref=dst, send_sem=send_sems.at[k],
                                                recv_sem=recv_sems.at[k], device_id=to, device_id_type=MESH)

        def chip_at(rel):
            return (x ^ (rel >> 1), y ^ (rel & 1), c)

        def to_sibling(k):
            return remote(part.at[k % 2, other_rows, :], recv_d2d.at[k], k, sibling)

        def to_chip(k):
            return remote(send_ici.at[k], recv_ici.at[k], N_CHIPS + k, chip_at(relation_of_slot(k)))

        swap = remote(mine_buf, other_buf, 2 * N_CHIPS - 1, sibling)
        big_load = [pltpu.make_async_copy(big_hbm[w].at[:, pl.ds(c, 1)], big_own[w], local_sems.at[w])
                    for w in range(n_big)]
        big_to_sibling = [remote(big_hbm[w].at[:, pl.ds(1 - c, 1)], big_recv[w], sem_big_d2d + w, sibling)
                          for w in range(n_big)]
        big_to_chip = [[remote(big_send[w].at[k], big_land[w].at[k], sem_big_ici + N_REL * w + k, chip_at(k + 1))
                        for k in range(N_REL)] for w in range(n_big)]
        big_swap = [remote(big_mine[w], big_other[w], sem_big_swap + w, sibling) for w in range(n_big)]
        small_load = [pltpu.make_async_copy(small_hbm[i], small_own[i], local_sems.at[loc_small + i])
                      for i in range(n_small)]
        small_to_sibling = [remote(small_hbm[i], small_recv[i], sem_small_d2d + i, sibling) for i in range(n_small)]
        small_to_chip = [[remote(small_all[i].at[my_chip], small_all[i].at[my_chip],
                                 sem_small_ici + N_REL * i + k, chip_at(k + 1))
                          for k in range(N_REL)] for i in range(n_small)]

        @pl.when((s == 0) & (t == 0))
        def _():
            h_load = pltpu.make_async_copy(h_hbm, h_vmem, h_sem)
            h_load.start()
            for cp in big_load + big_to_sibling + small_load + small_to_sibling:
                cp.start()
            h_load.wait()

        @pl.when((s == 0) & (t == n_sub - 1))
        def _():
            for cp in big_load + small_load:
                cp.wait()
            for cp in big_to_sibling + small_to_sibling:
                cp.wait_recv()
                cp.wait_send()
            for w in range(n_big):
                for k in range(N_REL):
                    shard = my_chip ^ (k + 1)
                    big_send[w][k] = (big_own[w][shard, 0] + big_recv[w][shard, 0]).astype(BF16)
                    big_to_chip[w][k].start()
            for i in range(n_small):
                small_all[i][my_chip] = small_own[i][...] + small_recv[i][...]
                for k in range(N_REL):
                    small_to_chip[i][k].start()

        @pl.when((s > 0) & (t == jnp.where(s == last, 0, min(1, n_sub - 1))))
        def _():
            k = s - 1
            cp = to_sibling(k)
            cp.wait_recv()
            cp.wait_send()
            send_ici[k] = (part[k % 2, my_rows, :] + recv_d2d[k]).astype(BF16)
            to_chip(k).start()

        def big_rows(w, half):
            rows = big_half[w][0]
            return big_out[w].at[pl.ds(pl.multiple_of(half * rows, 8), rows), :]

        big_store_mine = [pltpu.make_async_copy(big_mine[w], big_rows(w, c), local_sems.at[loc_out_big + 2 * w])
                          for w in range(n_big)]
        big_store_other = [pltpu.make_async_copy(big_other[w], big_rows(w, 1 - c),
                                                 local_sems.at[loc_out_big + 2 * w + 1]) for w in range(n_big)]
        small_store = [pltpu.make_async_copy(small_all[i], small_out[i], local_sems.at[loc_out_small + i])
                       for i in range(n_small)]

        @pl.when((s == last) & (t == 0))
        def _():
            for w in range(n_big):
                total = big_own[w][my_chip, 0] + big_recv[w][my_chip, 0]
                for k in range(N_REL):
                    big_to_chip[w][k].wait_recv()
                    total = total + big_land[w][k].astype(F32)
                big_mine[w][...] = total
                big_swap[w].start()
                big_store_mine[w].start()
            for i in range(n_small):
                for k in range(N_REL):
                    small_to_chip[i][k].wait_recv()
                small_store[i].start()

        r = _mm_tn(dp_ref[...], h_vmem[pl.ds(pl.multiple_of(t * tile, tile), tile), :])
        odd = shard_of_slot(s, shard_ref[0]) % 2
        for parity in range(2):
            rows = r[64 * parity:64 * parity + SHARD_ROWS]

            @pl.when((odd == parity) & (t == 0))
            def _():
                part[s % 2] = rows

            @pl.when((odd == parity) & (t > 0))
            def _():
                part[s % 2] += rows

        @pl.when(t == n_sub - 1)
        def _():
            to_sibling(s).start()

        @pl.when((s == last) & (t == n_sub - 1))
        def _():
            cp = to_sibling(last)
            cp.wait_recv()
            cp.wait_send()
            total = part[last % 2, my_rows, :] + recv_d2d[last]
            for k in range(last):
                to_chip(k).wait_recv()
                total = total + recv_ici[k].astype(F32)
            mine_buf[...] = total
            swap.start()
            out_mine = pltpu.make_async_copy(mine_buf, out_hbm.at[my_rows, :], local_sems.at[0])
            out_mine.start()
            swap.wait_recv()
            out_other = pltpu.make_async_copy(other_buf, out_hbm.at[other_rows, :], local_sems.at[1])
            out_other.start()
            for w in range(n_big):
                big_swap[w].wait_recv()
                big_store_other[w].start()
            stores = [out_mine, out_other] + big_store_mine + big_store_other + small_store
            for k in range(last):
                to_chip(k).wait_send()
            swap.wait_send()
            for w in range(n_big):
                for k in range(N_REL):
                    big_to_chip[w][k].wait_send()
                big_swap[w].wait_send()
            for i in range(n_small):
                for k in range(N_REL):
                    small_to_chip[i][k].wait_send()
            for cp in stores:
                cp.wait()

    half = (SHARD_HALF, D_MODEL)
    vmem = pltpu.VMEM
    scratch = [vmem((2, SHARD_ROWS, D_MODEL), F32), vmem((N_CHIPS,) + half, F32),
               vmem((N_REL,) + half, BF16), vmem((N_REL,) + half, BF16), vmem(half, F32), vmem(half, F32)]
    scratch += [vmem((N_CHIPS, 1) + hs, F32) for hs in big_half] * 2
    scratch += [vmem((N_REL,) + hs, BF16) for hs in big_half] * 2
    scratch += [vmem(hs, F32) for hs in big_half] * 2
    scratch += [vmem(a.shape, F32) for a in small] * 2 + [vmem((N_CHIPS,) + a.shape, F32) for a in small]
    scratch += [pltpu.SemaphoreType.DMA((n_sems,)), pltpu.SemaphoreType.DMA((n_sems,)),
                pltpu.SemaphoreType.DMA((n_local,)), vmem(h.shape, BF16), pltpu.SemaphoreType.DMA]
    n_hbm = n_big + n_small
    out = pl.pallas_call(
        body, name="reduce_gradients",
        out_shape=[jax.ShapeDtypeStruct((SHARD_ROWS, D_MODEL), F32)]
        + [jax.ShapeDtypeStruct((2 * hs[0], hs[1]), F32) for hs in big_half]
        + [jax.ShapeDtypeStruct((N_CHIPS,) + a.shape, F32) for a in small],
        grid_spec=pltpu.PrefetchScalarGridSpec(
            num_scalar_prefetch=1, grid=(N_CHIPS, n_sub),
            in_specs=[pl.BlockSpec((pl.Element(tile), pl.Element(SHARD_WINDOW)),
                                   lambda s, t, m: (t * tile, _shard_window_start(shard_of_slot(s, m[0])))),
                      ANY_SPEC] + [ANY_SPEC] * n_hbm,
            out_specs=[ANY_SPEC] * (1 + n_hbm),
            scratch_shapes=scratch),
        compiler_params=pltpu.CompilerParams(vmem_limit_bytes=VMEM_LIMIT),
    )(shard_arr, dproj, h, *big, *small)
    return out[:1 + n_big], out[1 + n_big:]


def _adamw(w, g, m, v):
    m2 = ADAM_B1 * m + (1.0 - ADAM_B1) * g
    v2 = ADAM_B2 * v + (1.0 - ADAM_B2) * (g * g)
    m_hat = m2 / (1.0 - ADAM_B1 ** ADAM_STEP)
    v_hat = v2 / (1.0 - ADAM_B2 ** ADAM_STEP)
    delta = -ADAM_LR * (m_hat / (jnp.sqrt(v_hat) + ADAM_EPS) + ADAM_WD * w)
    return delta, m2, v2


ADAM_STEPS = 4


def _adamw_all(shard_grads, shard_w, shard_m, shard_v, ra, rb, small_w, small_m, small_v):
    n_sh, n = len(shard_w), len(small_w)

    def body(*refs):
        sh_in, refs = refs[:4 * n_sh], refs[4 * n_sh:]
        ra_ref, rb_ref, refs = refs[0], refs[1], refs[2:]
        w_refs, m_refs, v_refs, refs = refs[:n], refs[n:2 * n], refs[2 * n:3 * n], refs[3 * n:]
        sh_out, outs = refs[:4 * n_sh], refs[4 * n_sh:]
        for k in range(n_sh):
            g = sh_in[k][...]
            delta, m2, v2 = _adamw(sh_in[n_sh + k][...], g, sh_in[2 * n_sh + k][...], sh_in[3 * n_sh + k][...])
            for ref, val in zip(sh_out[4 * k:4 * k + 4], (g, delta, m2, v2)):
                ref[...] = val

        @pl.when(pl.program_id(0) == 0)
        def _():
            g_outs, d_outs, m_outs, v_outs = outs[:n], outs[n:2 * n], outs[2 * n:3 * n], outs[3 * n:4 * n]
            ga, gb = ra_ref[0], rb_ref[0]
            for chip in range(1, N_CHIPS):
                ga = ga + ra_ref[chip]
                gb = gb + rb_ref[chip]
            outs[4 * n][...] = ga[ROW_LOSS:ROW_LOSS + 1, 0:128]
            grads = [ga[0:1, :], ga[1:2, :], ga[2:3, :], ga[3:4, :A_WIDTH], ga[3:4, A_WIDTH:],
                     gb[ROW_WS:ROW_WS + A_GROUPS * CHUNK, :].reshape(A_GROUPS, CHUNK, CHUNK),
                     gb[ROW_BS:ROW_BS + A_GROUPS, :], gb[ROW_SINK:ROW_SINK + 1, 0:4],
                     gb[ROW_REL:ROW_REL + 4, 0:N_BUCKETS]]
            for k in range(n):
                delta, m2, v2 = _adamw(w_refs[k][...], grads[k], m_refs[k][...], v_refs[k][...])
                g_outs[k][...] = grads[k]
                d_outs[k][...] = delta
                m_outs[k][...] = m2
                v_outs[k][...] = v2

    def rows_block(a):
        assert a.shape[0] % (8 * ADAM_STEPS) == 0
        return pl.BlockSpec((a.shape[0] // ADAM_STEPS, a.shape[1]), lambda i: (i, 0))

    sh_specs = [rows_block(w) for w in shard_w]
    small_in = [ra, rb, *small_w, *small_m, *small_v]
    small_out_shapes = [jax.ShapeDtypeStruct(w.shape, F32) for w in small_w] * 4 + [jax.ShapeDtypeStruct((1, 128), F32)]
    out = pl.pallas_call(
        body, name="adamw_all", grid=(ADAM_STEPS,),
        out_shape=[jax.ShapeDtypeStruct(w.shape, F32) for w in shard_w for _ in range(4)] + small_out_shapes,
        in_specs=sh_specs * 4 + [_full_spec(a.shape) for a in small_in],
        out_specs=[spec for spec in sh_specs for _ in range(4)] + [_full_spec(s.shape) for s in small_out_shapes],
        compiler_params=pltpu.CompilerParams(vmem_limit_bytes=VMEM_LIMIT),
    )(*shard_grads, *shard_w, *shard_m, *shard_v, *small_in)
    return [out[4 * k:4 * k + 4] for k in range(n_sh)], out[4 * n_sh:]


def kernel(x, mem, pre_norm_g, post_norm_g, mem_norm_g, w_in, w_mem_kv, v_norm_g, v_norm_b, w_spatial, b_spatial, attn_sinks, rel_bias, w_out, loss_target, m_pre_norm_g, m_post_norm_g, m_mem_norm_g, m_w_in, m_w_mem_kv, m_v_norm_g, m_v_norm_b, m_w_spatial, m_b_spatial, m_attn_sinks, m_rel_bias, m_w_out, v_pre_norm_g, v_post_norm_g, v_mem_norm_g, v_w_in, v_w_mem_kv, v_v_norm_g, v_v_norm_b, v_w_spatial, v_b_spatial, v_attn_sinks, v_rel_bias, v_w_out):
    n_ex, seq, _ = x.shape
    n_tok = n_ex * seq
    x2 = x.reshape(n_tok, D_MODEL)
    tgt2 = loss_target.reshape(n_tok, D_MODEL)
    buckets = jnp.asarray(_bucket_map())
    shard_arr = (2 * lax.axis_index("x") + lax.axis_index("y")).astype(jnp.int32).reshape(1)
    w_sp = w_spatial[0]
    w_in_t, m_w_in_t, v_w_in_t = (jnp.transpose(a[0]) for a in (w_in, m_w_in, v_w_in))
    rel_t, m_rel_t, v_rel_t = (jnp.transpose(a) for a in (rel_bias, m_rel_bias, v_rel_bias))

    x_arr = lax.axis_index("x").astype(jnp.int32).reshape(1)
    h_b, parts, (w_in_b, g_mkv, g_out), bias, b_sp = _gather_and_project(
        x2, pre_norm_g, w_in_t, w_mem_kv[0], w_out[0], rel_t, buckets, b_spatial[0], x_arr)
    w_mkv_b = g_mkv.reshape(D_MODEL, 2 * MEM_WIDTH)
    w_out_b = g_out.reshape(MIX_WIDTH, D_MODEL)

    dout, dproj, dwmkv, dgmem, dwout, dvg, dvb, dws, dbs, dsink, drel, loss_vec, dgpost = _mix(
        parts, mem, x2, tgt2, v_norm_g, v_norm_b, w_sp, b_sp, attn_sinks, bias, w_out_b, post_norm_g, mem_norm_g,
        w_mkv_b, n_ex, seq)

    dx, small_a, small_b = _backward_projection(
        x2, dout, dproj, pre_norm_g, w_in_b, [dgpost, dgmem, dvg, dvb, dws, dbs, dsink, drel, loss_vec, buckets])

    shard_shapes = [w_mem_kv.shape[1:], w_out.shape[1:]]
    big = [g.reshape(N_CHIPS, 2, s[0] // 2, s[1]) for g, s in zip((dwmkv, dwout), shard_shapes)]
    (g_win, g_wmkv, g_wout), (ga, gb) = _reduce_gradients(dproj, h_b, big, [small_a, small_b], shard_arr)

    small_w = [pre_norm_g, post_norm_g, mem_norm_g, v_norm_g, v_norm_b, w_sp, b_spatial[0], attn_sinks, rel_t]
    small_m = [m_pre_norm_g, m_post_norm_g, m_mem_norm_g, m_v_norm_g, m_v_norm_b, m_w_spatial[0], m_b_spatial[0],
               m_attn_sinks, m_rel_t]
    small_v = [v_pre_norm_g, v_post_norm_g, v_mem_norm_g, v_v_norm_g, v_v_norm_b, v_w_spatial[0], v_b_spatial[0],
               v_attn_sinks, v_rel_t]
    big_out, small_out = _adamw_all(
        [g_win, g_wmkv, g_wout], [w_in_t, w_mem_kv[0], w_out[0]], [m_w_in_t, m_w_mem_kv[0], m_w_out[0]],
        [v_w_in_t, v_w_mem_kv[0], v_w_out[0]], ga, gb, small_w, small_m, small_v)
    n_small = len(small_w)

    outputs = [small_out[4 * n_small][0, 0], dx.reshape(x.shape)]
    for kind in range(4):
        s = small_out[kind * n_small:(kind + 1) * n_small]
        outputs += [s[0], s[1], s[2], jnp.transpose(big_out[0][kind])[None], big_out[1][kind][None], s[3], s[4],
                    s[5][None], s[6][None], s[7], jnp.transpose(s[8]), big_out[2][kind][None]]
    return tuple(outputs)
```

```python
import functools

import numpy as np
import jax
import jax.numpy as jnp
from jax import lax
from jax.experimental import pallas as pl
from jax.experimental.pallas import tpu as pltpu

F32 = jnp.float32
BF16 = jnp.bfloat16
MESH = pl.DeviceIdType.MESH

D_MODEL = 1024
CHUNK = 128
A_WIDTH = 512
A_GROUPS = 4
SWA_WIDTH = 256
KV_WIDTH = 128
MEM_WIDTH = 256
MEM_LEN = 256
MIX_WIDTH = 1024
IN_WIDTH = 2816
N_BUCKETS = 32
MAX_DISTANCE = 128
EPS = 1e-6
NEG = -1e30
QK_SCALE = 0.125
HALF_HEAD_PAIR = 64

ADAM_LR = 0.001
ADAM_B1 = 0.9
ADAM_B2 = 0.999
ADAM_EPS = 1e-08
ADAM_WD = 0.01
ADAM_STEP = 10

N_CHIPS = 4
TILE_CHUNKS = 2
TILE = TILE_CHUNKS * CHUNK
PROJ_TILE = 512
VMEM_LIMIT = 56 * 1024 * 1024

SMALL_A_ROWS = 8
ROW_LOSS = 4
ROW_WS = 0
ROW_BS = 512
ROW_SINK = 520
ROW_REL = 528
SMALL_B_ROWS = 536


def _mm(a, b):
    return lax.dot_general(a, b, (((1,), (0,)), ((), ())), preferred_element_type=F32)


def _mm_nt(a, b):
    return lax.dot_general(a, b, (((1,), (1,)), ((), ())), preferred_element_type=F32)


def _mm_tn(a, b):
    return lax.dot_general(a, b, (((0,), (0,)), ((), ())), preferred_element_type=F32)


def _bucket_map():
    qi = np.arange(CHUNK)[:, None]
    kj = np.arange(2 * CHUNK)[None, :]
    n = np.maximum(qi + CHUNK - kj, 0)
    max_exact = N_BUCKETS // 2
    large = max_exact + (np.log(np.maximum(n, 1) / max_exact) / np.log(MAX_DISTANCE / max_exact)
                         * (N_BUCKETS - max_exact)).astype(np.int32)
    large = np.minimum(large, N_BUCKETS - 1)
    return np.where(n < max_exact, n, large).astype(np.int32)


_GELU_C = 0.7978845608028654
_GELU_A = 0.044715
_GELU_K1 = 2.0 * _GELU_C
_GELU_K2 = 2.0 * _GELU_C * _GELU_A


def _gelu(x):
    x2 = x * x
    s = 1.0 / (1.0 + jnp.exp(x * (-_GELU_K1 - _GELU_K2 * x2)))
    return x * s, (s, x2)


def _gelu_grad(x, saved):
    s, x2 = saved
    return s + x * (s * (1.0 - s)) * (_GELU_K1 + 3.0 * _GELU_K2 * x2)


def _sigmoid(x):
    return 1.0 / (1.0 + jnp.exp(-x))


def _lane_lo(shape):
    return lax.broadcasted_iota(jnp.int32, shape, 1) < HALF_HEAD_PAIR


def _swa_variants(t):
    lo = _lane_lo(t.shape)
    tr = pltpu.roll(t, HALF_HEAD_PAIR, 1)
    zero = jnp.zeros_like(t)
    return (jnp.where(lo, t, zero).astype(BF16), jnp.where(lo, zero, tr).astype(BF16),
            jnp.where(lo, tr, zero).astype(BF16), jnp.where(lo, zero, t).astype(BF16))


def _swa_unvariants(d0, d1, d2, d3):
    lo = _lane_lo(d0.shape)
    zero = jnp.zeros_like(d0)
    rolled = jnp.where(lo, zero, d1) + jnp.where(lo, d2, zero)
    return jnp.where(lo, d0, zero) + jnp.where(lo, zero, d3) + pltpu.roll(rolled, HALF_HEAD_PAIR, 1)


def _mem_variants(t):
    out = []
    for pair in range(2):
        tp = t[:, pair * 128:(pair + 1) * 128]
        lo = _lane_lo(tp.shape)
        zero = jnp.zeros_like(tp)
        out.append(jnp.where(lo, tp, zero).astype(BF16))
        out.append(jnp.where(lo, zero, tp).astype(BF16))
    return out


def _mem_unvariants(d0, d1, d2, d3):
    lo = _lane_lo(d0.shape)
    return jnp.concatenate([jnp.where(lo, d0, d1), jnp.where(lo, d2, d3)], axis=-1)


def _softmax(logits, sinks):
    m = jnp.max(logits, axis=-1, keepdims=True)
    if sinks is not None:
        m = jnp.maximum(m, sinks)
    p = jnp.exp(logits - m)
    den = jnp.sum(p, axis=-1, keepdims=True)
    if sinks is None:
        return p * (1.0 / den), None
    es = jnp.exp(sinks - m)
    inv = 1.0 / (den + es)
    return p * inv, es * inv


def _band_valid(with_prev):
    qi = lax.broadcasted_iota(jnp.int32, (CHUNK, 2 * CHUNK), 0)
    kj = lax.broadcasted_iota(jnp.int32, (CHUNK, 2 * CHUNK), 1)
    in_cur = (kj >= CHUNK) & (kj - CHUNK <= qi)
    if not with_prev:
        return in_cur
    return in_cur | ((kj < CHUNK) & (kj > qi))


def _causal_weights(ws_ref):
    row = lax.broadcasted_iota(jnp.int32, (CHUNK, CHUNK), 0)
    col = lax.broadcasted_iota(jnp.int32, (CHUNK, CHUNK), 1)
    return [jnp.where(row >= col, ws_ref[g], 0.0).astype(BF16) for g in range(A_GROUPS)]


def _rows_to_lanes(a, n):
    return jnp.concatenate([a[c * CHUNK:(c + 1) * CHUNK] for c in range(n)], axis=1)


def _lanes_to_rows(a, n):
    w = a.shape[1] // n
    return jnp.concatenate([a[:, c * w:(c + 1) * w] for c in range(n)], axis=0)


def _stack_heads(pair01, pair23):
    return jnp.concatenate([pair01[:, :256], pair01[:, 256:], pair23[:, :256], pair23[:, 256:]], axis=0)


def _pair_heads(s, r):
    return (jnp.concatenate([s[0:r], s[r:2 * r]], axis=1), jnp.concatenate([s[2 * r:3 * r], s[3 * r:4 * r]], axis=1))


def _pair_operands(variants):
    return (jnp.concatenate(variants[0:2], axis=0), jnp.concatenate(variants[2:4], axis=0))


def _split_pair_grads(d_pairs):
    return d_pairs[0][:256], d_pairs[0][256:], d_pairs[1][:256], d_pairs[1][256:]


def _halves_bf16(a):
    return (a[:, :128].astype(BF16), a[:, 128:].astype(BF16))


def _group_a_forward(au, av, vg, vb, wm, bs_rows):
    gu, tu = _gelu(au)
    gv, tv = _gelu(av)
    ya, res = [], []
    for g in range(A_GROUPS):
        sl = slice(g * 128, (g + 1) * 128)
        xg = gv[:, sl]
        xc = xg - jnp.mean(xg, axis=-1, keepdims=True)
        rstd = lax.rsqrt(jnp.mean(xc * xc, axis=-1, keepdims=True) + EPS)
        xhat = xc * rstd
        vn = _rows_to_lanes((xhat * vg[:, sl] + vb[:, sl]).astype(BF16), TILE_CHUNKS)
        s = _lanes_to_rows(_mm(wm[g], vn), TILE_CHUNKS) + bs_rows[g]
        ya.append(gu[:, sl] * s)
        res.append((xhat, rstd, vn, s))
    return ya, dict(gu=gu, tu=tu, tv=tv, groups=res)


def _attention_logits(qp, k_pairs):
    return _stack_heads(_mm_nt(qp[0], k_pairs[0]), _mm_nt(qp[1], k_pairs[1]))


def _attention_out(p, v_pairs, r):
    pp = _pair_heads(p.astype(BF16), r)
    return jnp.concatenate([_mm(pp[0], v_pairs[0]), _mm(pp[1], v_pairs[1])], axis=-1), pp


def _attention_dprobs(do_pairs, v_pairs):
    return _stack_heads(_mm_nt(do_pairs[0], v_pairs[0]), _mm_nt(do_pairs[1], v_pairs[1]))


def _softmax_backward(p, dp):
    delta = jnp.sum(p * dp, axis=-1, keepdims=True)
    return p * (dp - delta), delta


def _attention_grads(dl, pp, do_pairs, qp, k_pairs, r):
    dlp = _pair_heads(dl.astype(BF16), r)
    dq = jnp.concatenate([_mm(dlp[0], k_pairs[0]), _mm(dlp[1], k_pairs[1])], axis=-1)
    dk = (_mm_tn(dlp[0], qp[0]), _mm_tn(dlp[1], qp[1]))
    dv = (_mm_tn(pp[0], do_pairs[0]), _mm_tn(pp[1], do_pairs[1]))
    return dq, dk, dv


def _tile_specs(n_tiles_ex, width):
    return pl.BlockSpec((TILE, width), lambda b, i: (b * n_tiles_ex + jnp.minimum(i, n_tiles_ex - 1), 0))


def _prev_chunk_spec(n_tiles_ex, width):
    def index(b, i):
        chunk = TILE_CHUNKS * jnp.minimum(i, n_tiles_ex - 1)
        return (b * n_tiles_ex * TILE_CHUNKS + jnp.maximum(chunk - 1, 0), 0)
    return pl.BlockSpec((CHUNK, width), index)


def _full_spec(shape):
    zeros = (0,) * len(shape)
    return pl.BlockSpec(shape, lambda *_: zeros)


SMEM_SPEC = pl.BlockSpec(memory_space=pltpu.SMEM)
ANY_SPEC = pl.BlockSpec(memory_space=pl.ANY)
VMEM_SPEC = pl.BlockSpec(memory_space=pltpu.VMEM)


def _fill_bias(rel_ref, bk_ref, out_ref):
    bk = bk_ref[...]
    for h in range(4):
        acc = jnp.zeros((CHUNK, 2 * CHUNK), F32)
        for b in range(N_BUCKETS):
            acc = jnp.where(bk == b, rel_ref[h, b], acc)
        for t, with_prev in enumerate((True, False)):
            out_ref[t, h * CHUNK:(h + 1) * CHUNK, :] = jnp.where(_band_valid(with_prev), acc, NEG)


PROJ_WIDTHS = (A_WIDTH, A_WIDTH, SWA_WIDTH, KV_WIDTH, KV_WIDTH, MEM_WIDTH, MIX_WIDTH)
PROJ_OFFSETS = tuple(int(v) for v in np.cumsum((0,) + PROJ_WIDTHS))


MXU_TILE = 256
HALF_WIDTH = IN_WIDTH // 2
PHASE_COLS = (HALF_WIDTH // MXU_TILE * MXU_TILE, IN_WIDTH - HALF_WIDTH // MXU_TILE * MXU_TILE)


def _phase_columns(phase, chip_x):
    if phase == 0:
        return 0 if chip_x == 0 else IN_WIDTH - PHASE_COLS[0]
    return PHASE_COLS[0] if chip_x == 0 else 0


def _phase_parts(phase, chip_x):
    start = _phase_columns(phase, chip_x)
    return [(k, PROJ_OFFSETS[k] - start) for k in range(len(PROJ_WIDTHS))
            if start <= PROJ_OFFSETS[k] and PROJ_OFFSETS[k + 1] <= start + PHASE_COLS[phase]]


def _gather_and_project(x2, g_pre, w_in_s, w_mkv_s, w_out_s, rel_bias_t, buckets, b_spatial, x_arr):
    n_tok = x2.shape[0]
    n_tiles = n_tok // PROJ_TILE
    last = n_tiles - 1
    shapes = [w_in_s.shape, w_mkv_s.shape, w_out_s.shape]
    n_w = len(shapes)

    def body(x_sref, x_ref, g_ref, win_hbm, wmkv_hbm, wout_hbm, rel_ref, bk_ref, bsp_ref, h_ref, *refs):
        part_refs, refs = refs[:len(PROJ_WIDTHS)], refs[len(PROJ_WIDTHS):]
        bias_ref, bs_ref, refs = refs[0], refs[1], refs[2:]
        gin_hbm, gmkv_hbm, gout_hbm, wg, stage_in, stage_mkv, stage_out, own_mkv, own_out, h_all = refs[:10]
        send_sems, recv_sems, local_sems = refs[10:]
        p, t = pl.program_id(0), pl.program_id(1)
        x, y, c = lax.axis_index("x"), lax.axis_index("y"), lax.axis_index("c")
        me, sibling = (x, y, c), (x, y, 1 - c)
        my_shard = 2 * x + y
        gathered = [wg, gmkv_hbm, gout_hbm]

        def half_rows(w, shard, half):
            rows = shapes[w][0] // 2
            if w == 0:
                return wg.at[pl.ds(pl.multiple_of(shard * shapes[0][0] + half * rows, 16), rows), :]
            return gathered[w].at[shard, pl.ds(half * rows, rows), :]

        def first(w, rel):
            src = half_rows(w, my_shard, c) if w == 0 else (own_mkv, own_out)[w - 1].at[
                pl.ds(c * (shapes[w][0] // 2), shapes[w][0] // 2), :]
            k = 3 * w + rel - 1
            return pltpu.make_async_remote_copy(
                src_ref=src, dst_ref=half_rows(w, my_shard, c), send_sem=send_sems.at[k], recv_sem=recv_sems.at[k],
                device_id=(x ^ (rel >> 1), y ^ (rel & 1), c), device_id_type=MESH)

        def landed(w, rel):
            k = 3 * w + rel - 1
            ref = half_rows(w, my_shard ^ rel, c)
            return pltpu.make_async_remote_copy(src_ref=ref, dst_ref=ref, send_sem=send_sems.at[k],
                                                recv_sem=recv_sems.at[k], device_id=me, device_id_type=MESH)

        def passed(w, rel, half, to):
            k = 9 + 3 * w + rel - 1
            ref = half_rows(w, my_shard ^ rel, half)
            return pltpu.make_async_remote_copy(src_ref=ref, dst_ref=ref, send_sem=send_sems.at[k],
                                                recv_sem=recv_sems.at[k], device_id=to, device_id_type=MESH)

        def pass_on(w, rels):
            for rel in rels:
                landed(w, rel).wait_recv()
                passed(w, rel, c, sibling).start()
            for rel in rels:
                passed(w, rel, 1 - c, me).wait_recv()

        own_stores = [pltpu.make_async_copy(own_mkv, gmkv_hbm.at[my_shard], local_sems.at[3]),
                      pltpu.make_async_copy(own_out, gout_hbm.at[my_shard], local_sems.at[4])]

        @pl.when((p == 0) & (t == 0))
        def _():
            loads = [pltpu.make_async_copy(src, dst, local_sems.at[k]) for k, (src, dst) in enumerate(
                ((win_hbm, stage_in), (wmkv_hbm, stage_mkv), (wout_hbm, stage_out)))]
            for cp in loads:
                cp.start()
            loads[0].wait()
            wg[pl.ds(pl.multiple_of(my_shard * shapes[0][0], 16), shapes[0][0]), :] = stage_in[...].astype(BF16)
            for rel in (1, 2):
                first(0, rel).start()
            loads[1].wait()
            loads[2].wait()
            own_mkv[...] = stage_mkv[...].astype(BF16)
            own_out[...] = stage_out[...].astype(BF16)
            for cp in own_stores:
                cp.start()
            _fill_bias(rel_ref, bk_ref, bias_ref)
            for g in range(A_GROUPS):
                bs_ref[g] = jnp.transpose(jnp.broadcast_to(bsp_ref[g:g + 1, :], (CHUNK, CHUNK)))
            pass_on(0, (1,))
            first(0, 3).start()

        @pl.when((p == 0) & (t == n_tiles // 2))
        def _():
            for w in (1, 2):
                for rel in (1, 2, 3):
                    first(w, rel).start()

        store = pltpu.make_async_copy(wg, gin_hbm, local_sems.at[5])

        @pl.when((p == 1) & (t == 0))
        def _():
            pass_on(0, (2, 3))
            store.start()

        @pl.when((p == 1) & (t == n_tiles // 2))
        def _():
            for w in (1, 2):
                pass_on(w, (1, 2, 3))

        tile_rows = pl.ds(pl.multiple_of(t * PROJ_TILE, PROJ_TILE), PROJ_TILE)

        def project(h, phase):
            start = jnp.where(x_sref[0] == 0, _phase_columns(phase, 0), _phase_columns(phase, 1))
            proj = _mm_nt(h, wg[pl.ds(pl.multiple_of(start, MXU_TILE), PHASE_COLS[phase]), :])
            for chip_x in range(2):
                @pl.when(x_sref[0] == chip_x)
                def _():
                    for k, lo in _phase_parts(phase, chip_x):
                        part_refs[k][...] = proj[:, lo:lo + PROJ_WIDTHS[k]].astype(BF16)

        @pl.when(p == 0)
        def _():
            xv = x_ref[...]
            r = lax.rsqrt(jnp.mean(xv * xv, axis=-1, keepdims=True) + EPS)
            h = (xv * r * g_ref[...]).astype(BF16)
            h_ref[...] = h
            h_all[tile_rows, :] = h
            project(h, 0)

        @pl.when(p == 1)
        def _():
            project(h_all[tile_rows, :], 1)

        @pl.when((p == 1) & (t == last))
        def _():
            for w in range(n_w):
                for rel in (1, 2, 3):
                    first(w, rel).wait_send()
                    passed(w, rel, c, sibling).wait_send()
            for cp in own_stores:
                cp.wait()
            store.wait()

    def written_in(k):
        phase_on = [next(ph for ph in range(2) if k in dict(_phase_parts(ph, chip_x))) for chip_x in range(2)]

        def index(p, t, xs):
            phase = jnp.where(xs[0] == 0, phase_on[0], phase_on[1])
            return (jnp.where(p == phase, t, jnp.where(p < phase, 0, last)), 0)
        return index

    part_specs = [pl.BlockSpec((PROJ_TILE, PROJ_WIDTHS[k]), written_in(k)) for k in range(len(PROJ_WIDTHS))]
    vmem = pltpu.VMEM
    out = pl.pallas_call(
        body, name="gather_and_project",
        out_shape=[jax.ShapeDtypeStruct((n_tok, D_MODEL), BF16)]
        + [jax.ShapeDtypeStruct((n_tok, w), BF16) for w in PROJ_WIDTHS]
        + [jax.ShapeDtypeStruct((2, 4 * CHUNK, 2 * CHUNK), F32), jax.ShapeDtypeStruct((A_GROUPS, CHUNK, CHUNK), F32)]
        + [jax.ShapeDtypeStruct((N_CHIPS * shapes[0][0], shapes[0][1]), BF16)]
        + [jax.ShapeDtypeStruct((N_CHIPS,) + s, BF16) for s in shapes[1:]],
        grid_spec=pltpu.PrefetchScalarGridSpec(
            num_scalar_prefetch=1, grid=(2, n_tiles),
            in_specs=[pl.BlockSpec((PROJ_TILE, D_MODEL), lambda p, t, xs: (jnp.where(p == 0, t, last), 0)),
                      pl.BlockSpec((1, D_MODEL), lambda p, t, xs: (0, 0)), ANY_SPEC, ANY_SPEC, ANY_SPEC, SMEM_SPEC,
                      pl.BlockSpec(buckets.shape, lambda p, t, xs: (0, 0)),
                      pl.BlockSpec(b_spatial.shape, lambda p, t, xs: (0, 0))],
            out_specs=[pl.BlockSpec((PROJ_TILE, D_MODEL), lambda p, t, xs: (jnp.where(p == 0, t, last), 0))]
            + part_specs + [pl.BlockSpec((2, 4 * CHUNK, 2 * CHUNK), lambda p, t, xs: (0, 0, 0)),
                            pl.BlockSpec((A_GROUPS, CHUNK, CHUNK), lambda p, t, xs: (0, 0, 0))] + [ANY_SPEC] * 3,
            scratch_shapes=[vmem((N_CHIPS * shapes[0][0], shapes[0][1]), BF16), vmem(shapes[0], F32),
                            vmem(shapes[1], F32), vmem(shapes[2], F32), vmem(shapes[1], BF16), vmem(shapes[2], BF16),
                            vmem((n_tok, D_MODEL), BF16),
                            pltpu.SemaphoreType.DMA((18,)), pltpu.SemaphoreType.DMA((18,)),
                            pltpu.SemaphoreType.DMA((6,))]),
        compiler_params=pltpu.CompilerParams(vmem_limit_bytes=VMEM_LIMIT),
    )(x_arr, x2, g_pre, w_in_s, w_mkv_s, w_out_s, rel_bias_t, buckets, b_spatial)
    n_parts = len(PROJ_WIDTHS)
    return out[0], list(out[1:1 + n_parts]), out[3 + n_parts:], out[1 + n_parts], out[2 + n_parts]


def _load_chunk(j, i, sk_ref, sv_ref, skp_ref, svp_ref):
    rows = slice(j * CHUNK, (j + 1) * CHUNK)
    if j == 0:
        k_prev, v_prev, table = skp_ref[...], svp_ref[...], jnp.where(i > 0, 0, 1)
    else:
        prev = slice((j - 1) * CHUNK, j * CHUNK)
        k_prev, v_prev, table = sk_ref[prev, :], sv_ref[prev, :], 0
    k_pairs = _pair_operands(_swa_variants(jnp.concatenate([k_prev, sk_ref[rows, :]], axis=0).astype(F32)))
    v_pairs = _pair_operands(_swa_variants(jnp.concatenate([v_prev, sv_ref[rows, :]], axis=0).astype(F32)))
    return rows, k_pairs, v_pairs, table


def _tile_constants(ws_ref, bs_ref, sink_ref, mkv_v):
    wm = _causal_weights(ws_ref)
    bs_rows = [jnp.concatenate([bs_ref[g]] * TILE_CHUNKS, axis=0) for g in range(A_GROUPS)]
    sink_col = jnp.max(jnp.concatenate([jnp.full((CHUNK, 128), sink_ref[0, h], F32) for h in range(4)] * TILE_CHUNKS,
                                       axis=0), axis=-1, keepdims=True)
    mk_pairs = _pair_operands(_mem_variants(mkv_v[:, :MEM_WIDTH]))
    mv_pairs = _pair_operands(_mem_variants(mkv_v[:, MEM_WIDTH:]))
    return wm, bs_rows, sink_col, mk_pairs, mv_pairs


def _mix(parts, mem, x2, tgt2, v_g, v_b, w_sp, b_sp, sinks, bias, w_out, g_post, g_mem, w_mkv, n_ex, seq):
    n_tiles_ex = seq // TILE
    n_tok = n_ex * seq
    au, av, sq, sk, sv, mq, z = parts
    col = dict(zip(("au", "av", "sq", "sk", "sv", "mq", "z"),
                   (slice(PROJ_OFFSETS[k], PROJ_OFFSETS[k + 1]) for k in range(len(PROJ_WIDTHS)))))
    before_kv, after_kv = slice(0, col["sk"].start), slice(col["sv"].stop, IN_WIDTH)

    def body(au_ref, av_ref, sq_ref, sk_ref, sv_ref, skp_ref, svp_ref, mq_ref, z_ref, mem_ref, x_ref, tgt_ref,
             vg_ref, vb_ref, ws_ref, bs_ref, sink_ref, bias_ref, wout_ref, gpost_ref, gmem_ref, wmkv_ref,
             dout_ref, dproj_ref, dwmkv_ref, dgmem_ref, dwout_ref, dvg_ref, dvb_ref, dws_ref, dbs_ref, dsink_ref,
             drel_ref, loss_ref, dgpost_ref, carry_dp, carry_k, carry_v, memn_s, mkv_s, dmkv_s):
        b, i = pl.program_id(0), pl.program_id(1)

        @pl.when((b == 0) & (i == 0))
        def _():
            for ref in (dwmkv_ref, dgmem_ref, dwout_ref, dvg_ref, dvb_ref, dws_ref, dbs_ref, dsink_ref, drel_ref,
                        loss_ref, dgpost_ref):
                ref[...] = jnp.zeros_like(ref)

        def normalized_mem():
            m = mem_ref[0]
            return m * lax.rsqrt(jnp.mean(m * m, axis=-1, keepdims=True) + EPS)

        @pl.when(i == 0)
        def _():
            memn_s[...] = (normalized_mem() * gmem_ref[...]).astype(BF16)
            mkv_s[...] = _mm(memn_s[...], wmkv_ref[...])
            dmkv_s[...] = jnp.zeros_like(dmkv_s)
            carry_k[...] = jnp.zeros_like(carry_k)
            carry_v[...] = jnp.zeros_like(carry_v)

        @pl.when(i > 0)
        def _():
            dproj_ref[:, before_kv] = carry_dp[:, before_kv]
            dproj_ref[:, after_kv] = carry_dp[:, after_kv]

        @pl.when(i < n_tiles_ex)
        def _():
            wm, bs_rows, sink_col, mk_pairs, mv_pairs = _tile_constants(ws_ref, bs_ref, sink_ref, mkv_s[...])
            vg = vg_ref[...]

            au_v, av_v = au_ref[...].astype(F32), av_ref[...].astype(F32)
            ya, res = _group_a_forward(au_v, av_v, vg, vb_ref[...], wm, bs_rows)
            swa, logits, yb = [], [], []
            for j in range(TILE_CHUNKS):
                rows, k_pairs, v_pairs, table = _load_chunk(j, i, sk_ref, sv_ref, skp_ref, svp_ref)
                qp = _halves_bf16(sq_ref[rows, :] * QK_SCALE)
                logits.append(_attention_logits(qp, k_pairs) + bias_ref[table])
                swa.append([rows, k_pairs, v_pairs, qp])
            p_swa, sink_p = _softmax(jnp.concatenate(logits, axis=0), sink_col)
            for j in range(TILE_CHUNKS):
                out, pp = _attention_out(p_swa[j * 4 * CHUNK:(j + 1) * 4 * CHUNK], swa[j][2], CHUNK)
                yb.append(out)
                swa[j].append(pp)
            mqp = _halves_bf16(mq_ref[...] * QK_SCALE)
            pm, _ = _softmax(_attention_logits(mqp, mk_pairs), None)
            yc, ppm = _attention_out(pm, mv_pairs, TILE)
            ycat = jnp.concatenate(ya + [jnp.concatenate(yb, axis=0), yc], axis=-1)

            zv = z_ref[...].astype(F32)
            sig = _sigmoid(zv)
            sz = zv * sig
            y_b = (ycat * sz).astype(BF16)
            o = _mm(y_b, wout_ref[...])
            r2 = lax.rsqrt(jnp.mean(o * o, axis=-1, keepdims=True) + EPS)
            nrm = o * r2
            gp = gpost_ref[...]
            diff = x_ref[...] + nrm * gp - tgt_ref[...]
            loss_ref[...] += jnp.sum(diff * diff) * (0.5 / D_MODEL)
            dout = diff * (1.0 / D_MODEL)
            dout_ref[...] = dout
            dgpost_ref[...] += jnp.sum(dout * nrm, axis=0, keepdims=True)
            dn = dout * gp
            do_b = (r2 * (dn - nrm * jnp.mean(dn * nrm, axis=-1, keepdims=True))).astype(BF16)
            dwout_ref[...] += _mm_tn(y_b, do_b)
            dy = _mm_nt(do_b, wout_ref[...])
            carry_dp[:, col["z"]] = (dy * ycat * (sig * (1.0 + zv * (1.0 - sig)))).astype(BF16)
            dyc = dy * sz

            dgu, dgv = [], []
            for g in range(A_GROUPS):
                sl = slice(g * 128, (g + 1) * 128)
                xhat, rstd, vn, s = res["groups"][g]
                dya = dyc[:, sl]
                dgu.append(dya * s)
                ds = dya * res["gu"][:, sl]
                dbs_ref[:, sl] += sum(ds[c * CHUNK:(c + 1) * CHUNK] for c in range(TILE_CHUNKS))
                ds_b = _rows_to_lanes(ds.astype(BF16), TILE_CHUNKS)
                dws_ref[g] += _mm_nt(ds_b, vn)
                dvn = _lanes_to_rows(_mm_tn(wm[g], ds_b), TILE_CHUNKS)
                dvg_ref[:, sl] += jnp.sum(dvn * xhat, axis=0, keepdims=True)
                dvb_ref[:, sl] += jnp.sum(dvn, axis=0, keepdims=True)
                dxh = dvn * vg[:, sl]
                dgv.append(rstd * (dxh - jnp.mean(dxh, axis=-1, keepdims=True)
                                   - xhat * jnp.mean(dxh * xhat, axis=-1, keepdims=True)))
            carry_dp[:, col["au"]] = (jnp.concatenate(dgu, axis=-1) * _gelu_grad(au_v, res["tu"])).astype(BF16)
            carry_dp[:, col["av"]] = (jnp.concatenate(dgv, axis=-1) * _gelu_grad(av_v, res["tv"])).astype(BF16)

            do_pairs = [_halves_bf16(dyc[rows, A_WIDTH:A_WIDTH + SWA_WIDTH]) for rows, *_ in swa]
            dl_swa, delta = _softmax_backward(p_swa, jnp.concatenate(
                [_attention_dprobs(do_pairs[j], swa[j][2]) for j in range(TILE_CHUNKS)], axis=0))
            sink_terms = sink_p * delta
            lane4 = lax.broadcasted_iota(jnp.int32, (1, 128), 1)
            dsink_vec = jnp.zeros((1, 128), F32)
            for h in range(4):
                head_sum = sum(jnp.sum(sink_terms[(4 * j + h) * CHUNK:(4 * j + h + 1) * CHUNK])
                               for j in range(TILE_CHUNKS))
                dsink_vec = dsink_vec + jnp.where(lane4 == h, -head_sum, 0.0)
            dsink_ref[...] += dsink_vec
            drel_ref[...] += sum(dl_swa[j * 4 * CHUNK:(j + 1) * 4 * CHUNK] for j in range(TILE_CHUNKS))
            dk_parts, dv_parts = [], []
            for j, (rows, k_pairs, v_pairs, qp, pp) in enumerate(swa):
                dq, dk, dv = _attention_grads(dl_swa[j * 4 * CHUNK:(j + 1) * 4 * CHUNK], pp, do_pairs[j], qp, k_pairs,
                                              CHUNK)
                carry_dp[rows, col["sq"]] = (dq * QK_SCALE).astype(BF16)
                dk_parts.append(_swa_unvariants(*_split_pair_grads(dk)))
                dv_parts.append(_swa_unvariants(*_split_pair_grads(dv)))

            dc_pairs = _halves_bf16(dyc[:, A_WIDTH + SWA_WIDTH:])
            dl_mem, _ = _softmax_backward(pm, _attention_dprobs(dc_pairs, mv_pairs))
            dmq, dmk, dmv = _attention_grads(dl_mem, ppm, dc_pairs, mqp, mk_pairs, TILE)
            carry_dp[:, col["mq"]] = (dmq * QK_SCALE).astype(BF16)
            dmkv_s[...] += jnp.concatenate([_mem_unvariants(*_split_pair_grads(dmk)),
                                            _mem_unvariants(*_split_pair_grads(dmv))], axis=-1)

            for parts_c, carry, cols in ((dk_parts, carry_k, col["sk"]), (dv_parts, carry_v, col["sv"])):
                @pl.when(i > 0)
                def _():
                    dproj_ref[:, cols] = (carry[...] + jnp.concatenate(
                        [jnp.zeros((TILE - CHUNK, KV_WIDTH), F32), parts_c[0][:CHUNK]], axis=0)).astype(BF16)
                new = [parts_c[0][CHUNK:]]
                for j in range(1, TILE_CHUNKS):
                    new[-1] = new[-1] + parts_c[j][:CHUNK]
                    new.append(parts_c[j][CHUNK:])
                carry[...] = jnp.concatenate(new, axis=0)

        @pl.when(i == n_tiles_ex)
        def _():
            dproj_ref[:, col["sk"]] = carry_k[...].astype(BF16)
            dproj_ref[:, col["sv"]] = carry_v[...].astype(BF16)
            d_b = dmkv_s[...].astype(BF16)
            dwmkv_ref[...] += _mm_tn(memn_s[...], d_b)
            dgmem_ref[...] += jnp.sum(_mm_nt(d_b, wmkv_ref[...]) * normalized_mem(), axis=0, keepdims=True)

    tile = functools.partial(_tile_specs, n_tiles_ex)
    prev = functools.partial(_prev_chunk_spec, n_tiles_ex)
    late = pl.BlockSpec((TILE, IN_WIDTH), lambda b, i: (b * n_tiles_ex + jnp.maximum(i - 1, 0), 0))
    return pl.pallas_call(
        body, name="mix", grid=(n_ex, n_tiles_ex + 1),
        out_shape=[jax.ShapeDtypeStruct((n_tok, D_MODEL), F32), jax.ShapeDtypeStruct((n_tok, IN_WIDTH), BF16),
                   jax.ShapeDtypeStruct((D_MODEL, 2 * MEM_WIDTH), F32), jax.ShapeDtypeStruct((1, D_MODEL), F32),
                   jax.ShapeDtypeStruct((MIX_WIDTH, D_MODEL), F32), jax.ShapeDtypeStruct((1, A_WIDTH), F32),
                   jax.ShapeDtypeStruct((1, A_WIDTH), F32), jax.ShapeDtypeStruct((A_GROUPS, CHUNK, CHUNK), F32),
                   jax.ShapeDtypeStruct((CHUNK, A_WIDTH), F32), jax.ShapeDtypeStruct((1, 128), F32),
                   jax.ShapeDtypeStruct((4 * CHUNK, 2 * CHUNK), F32), jax.ShapeDtypeStruct((1, 128), F32),
                   jax.ShapeDtypeStruct((1, D_MODEL), F32)],
        in_specs=[tile(A_WIDTH), tile(A_WIDTH), tile(SWA_WIDTH), tile(KV_WIDTH), tile(KV_WIDTH),
                  prev(KV_WIDTH), prev(KV_WIDTH), tile(MEM_WIDTH), tile(MIX_WIDTH),
                  pl.BlockSpec((1, MEM_LEN, D_MODEL), lambda b, i: (b, 0, 0)),
                  tile(D_MODEL), tile(D_MODEL),
                  _full_spec((1, A_WIDTH)), _full_spec((1, A_WIDTH)), _full_spec((A_GROUPS, CHUNK, CHUNK)),
                  _full_spec((A_GROUPS, CHUNK, CHUNK)), SMEM_SPEC, _full_spec((2, 4 * CHUNK, 2 * CHUNK)),
                  _full_spec((MIX_WIDTH, D_MODEL)), _full_spec((1, D_MODEL)), _full_spec((1, D_MODEL)),
                  _full_spec((D_MODEL, 2 * MEM_WIDTH))],
        out_specs=[tile(D_MODEL), late, _full_spec((D_MODEL, 2 * MEM_WIDTH)), _full_spec((1, D_MODEL)),
                   _full_spec((MIX_WIDTH, D_MODEL)), _full_spec((1, A_WIDTH)), _full_spec((1, A_WIDTH)),
                   _full_spec((A_GROUPS, CHUNK, CHUNK)), _full_spec((CHUNK, A_WIDTH)), _full_spec((1, 128)),
                   _full_spec((4 * CHUNK, 2 * CHUNK)), _full_spec((1, 128)), _full_spec((1, D_MODEL))],
        scratch_shapes=[pltpu.VMEM((TILE, IN_WIDTH), BF16), pltpu.VMEM((TILE, KV_WIDTH), F32),
                        pltpu.VMEM((TILE, KV_WIDTH), F32), pltpu.VMEM((MEM_LEN, D_MODEL), BF16),
                        pltpu.VMEM((MEM_LEN, 2 * MEM_WIDTH), F32), pltpu.VMEM((MEM_LEN, 2 * MEM_WIDTH), F32)],
        compiler_params=pltpu.CompilerParams(vmem_limit_bytes=VMEM_LIMIT),
    )(au, av, sq, sk, sv, sk, sv, mq, z, mem, x2, tgt2, v_g, v_b, w_sp, b_sp, sinks, bias, w_out, g_post, g_mem,
      w_mkv)


BWD_PROJ_TILE = 512


def _fill_small_grads(dgpre_ref, dgpost_ref, dgmem_ref, dvg_ref, dvb_ref, dws_ref, dbs_ref, dsink_ref, drel_ref,
                      loss_ref, bk_ref, a_ref, b_ref):
    a_ref[...] = jnp.zeros_like(a_ref)
    b_ref[...] = jnp.zeros_like(b_ref)
    a_ref[0:1, :] = dgpre_ref[...]
    a_ref[1:2, :] = dgpost_ref[...]
    a_ref[2:3, :] = dgmem_ref[...]
    a_ref[3:4, :] = jnp.concatenate([dvg_ref[...], dvb_ref[...]], axis=-1)
    a_ref[ROW_LOSS:ROW_LOSS + 1, 0:128] = loss_ref[...]
    row = lax.broadcasted_iota(jnp.int32, (CHUNK, CHUNK), 0)
    col = lax.broadcasted_iota(jnp.int32, (CHUNK, CHUNK), 1)
    for g in range(A_GROUPS):
        b_ref[ROW_WS + g * CHUNK:ROW_WS + (g + 1) * CHUNK, :] = jnp.where(row >= col, dws_ref[g], 0.0)
        by_token = jnp.transpose(dbs_ref[:, g * 128:(g + 1) * 128])
        b_ref[ROW_BS + g:ROW_BS + g + 1, :] = jnp.sum(by_token, axis=0, keepdims=True)
    b_ref[ROW_SINK:ROW_SINK + 1, :] = dsink_ref[...]
    bk = bk_ref[...]
    rel_row = lax.broadcasted_iota(jnp.int32, (8, 128), 0)
    rel_col = lax.broadcasted_iota(jnp.int32, (8, 128), 1)
    rel = jnp.zeros((8, 128), F32)
    for h in range(4):
        acc = drel_ref[h * CHUNK:(h + 1) * CHUNK, :]
        for b in range(N_BUCKETS):
            rel = jnp.where((rel_row == h) & (rel_col == b), jnp.sum(jnp.where(bk == b, acc, 0.0)), rel)
    b_ref[ROW_REL:ROW_REL + 8, :] = rel


def _backward_projection(x2, dout, dproj, g_pre, w_in_t, small_parts):
    n_tok = x2.shape[0]
    n_steps = n_tok // BWD_PROJ_TILE
    n_small = len(small_parts)

    def body(x_ref, dout_ref, dp_ref, g_ref, w_hbm, *refs):
        small_refs, (dx_ref, a_ref, b_ref, w_vmem, dgpre, sem) = refs[:n_small], refs[n_small:]
        step = pl.program_id(0)

        @pl.when(step == 0)
        def _():
            load = pltpu.make_async_copy(w_hbm, w_vmem, sem)
            load.start()
            dgpre[...] = jnp.zeros_like(dgpre)
            load.wait()

        xv = x_ref[...]
        r = lax.rsqrt(jnp.mean(xv * xv, axis=-1, keepdims=True) + EPS)
        xn = xv * r
        dh = _mm(dp_ref[...], w_vmem[...])
        dgpre[...] += jnp.sum(dh * xn, axis=0, keepdims=True)
        dhg = dh * g_ref[...]
        dx_ref[...] = r * (dhg - xn * jnp.mean(dhg * xn, axis=-1, keepdims=True)) + dout_ref[...]

        @pl.when(step == n_steps - 1)
        def _():
            _fill_small_grads(dgpre, *small_refs, a_ref, b_ref)

    row = lambda w: pl.BlockSpec((BWD_PROJ_TILE, w), lambda i: (i, 0))
    return pl.pallas_call(
        body, name="backward_projection", grid=(n_steps,),
        out_shape=[jax.ShapeDtypeStruct((n_tok, D_MODEL), F32), jax.ShapeDtypeStruct((SMALL_A_ROWS, D_MODEL), F32),
                   jax.ShapeDtypeStruct((SMALL_B_ROWS, 128), F32)],
        in_specs=[row(D_MODEL), row(D_MODEL), row(IN_WIDTH), _full_spec((1, D_MODEL)), ANY_SPEC]
        + [_full_spec(a.shape) for a in small_parts],
        out_specs=[row(D_MODEL), _full_spec((SMALL_A_ROWS, D_MODEL)), _full_spec((SMALL_B_ROWS, 128))],
        scratch_shapes=[pltpu.VMEM((IN_WIDTH, D_MODEL), BF16), pltpu.VMEM((1, D_MODEL), F32),
                        pltpu.SemaphoreType.DMA],
        input_output_aliases={1: 0},
        compiler_params=pltpu.CompilerParams(vmem_limit_bytes=VMEM_LIMIT),
    )(x2, dout, dproj, g_pre, w_in_t, *small_parts)


SHARD_ROWS = IN_WIDTH // N_CHIPS
SHARD_WINDOW = 768
SHARD_HALF = SHARD_ROWS // 2
DWIN_TILE = 2048
N_REL = N_CHIPS - 1


def _shard_window_start(shard):
    return (shard * SHARD_ROWS // 128) * 128


def _reduce_gradients(dproj, h, big, small, shard_arr):
    n_tok = h.shape[0]
    tile = min(DWIN_TILE, n_tok)
    n_sub = n_tok // tile
    last = N_CHIPS - 1
    n_big, n_small = len(big), len(small)
    big_half = [g.shape[2:] for g in big]
    sem_big_d2d = 2 * N_CHIPS
    sem_big_ici = sem_big_d2d + n_big
    sem_big_swap = sem_big_ici + N_REL * n_big
    sem_small_d2d = sem_big_swap + n_big
    sem_small_ici = sem_small_d2d + n_small
    n_sems = sem_small_ici + N_REL * n_small
    loc_small = n_big
    loc_out_win = loc_small + n_small
    loc_out_big = loc_out_win + 2
    loc_out_small = loc_out_big + 2 * n_big
    n_local = loc_out_small + n_small

    def relation_of_slot(s):
        return (s + 2) % N_REL + 1

    def shard_of_slot(s, my_shard):
        return my_shard ^ jnp.where(s == last, 0, relation_of_slot(s))

    def body(shard_ref, dp_ref, h_hbm, *refs):
        h_vmem, h_sem, refs = refs[-2], refs[-1], refs[:-2]
        big_hbm, refs = refs[:n_big], refs[n_big:]
        small_hbm, refs = refs[:n_small], refs[n_small:]
        out_hbm, refs = refs[0], refs[1:]
        big_out, refs = refs[:n_big], refs[n_big:]
        small_out, refs = refs[:n_small], refs[n_small:]
        part, recv_d2d, send_ici, recv_ici, mine_buf, other_buf = refs[:6]
        refs = refs[6:]
        big_own, big_recv, big_send, big_land, big_mine, big_other = (
            refs[k * n_big:(k + 1) * n_big] for k in range(6))
        refs = refs[6 * n_big:]
        small_own, small_recv, small_all = (refs[k * n_small:(k + 1) * n_small] for k in range(3))
        send_sems, recv_sems, local_sems = refs[3 * n_small:]

        s, t = pl.program_id(0), pl.program_id(1)
        x, y, c = lax.axis_index("x"), lax.axis_index("y"), lax.axis_index("c")
        my_chip = 2 * x + y
        sibling = (x, y, 1 - c)
        my_rows = pl.ds(pl.multiple_of(c * SHARD_HALF, 8), SHARD_HALF)
        other_rows = pl.ds(pl.multiple_of((1 - c) * SHARD_HALF, 8), SHARD_HALF)

        def remote(src, dst, k, to):
            return pltpu.make_async_remote_copy(src_ref=src, dst_ref=dst, send_sem=send_sems.at[k],
                                                recv_sem=recv_sems.at[k], device_id=to, device_id_type=MESH)

        def chip_at(rel):
            return (x ^ (rel >> 1), y ^ (rel & 1), c)

        def to_sibling(k):
            return remote(part.at[k % 2, other_rows, :], recv_d2d.at[k], k, sibling)

        def to_chip(k):
            return remote(send_ici.at[k], recv_ici.at[k], N_CHIPS + k, chip_at(relation_of_slot(k)))

        swap = remote(mine_buf, other_buf, 2 * N_CHIPS - 1, sibling)
        big_load = [pltpu.make_async_copy(big_hbm[w].at[:, pl.ds(c, 1)], big_own[w], local_sems.at[w])
                    for w in range(n_big)]
        big_to_sibling = [remote(big_hbm[w].at[:, pl.ds(1 - c, 1)], big_recv[w], sem_big_d2d + w, sibling)
                          for w in range(n_big)]
        big_to_chip = [[remote(big_send[w].at[k], big_land[w].at[k], sem_big_ici + N_REL * w + k, chip_at(k + 1))
                        for k in range(N_REL)] for w in range(n_big)]
        big_swap = [remote(big_mine[w], big_other[w], sem_big_swap + w, sibling) for w in range(n_big)]
        small_load = [pltpu.make_async_copy(small_hbm[i], small_own[i], local_sems.at[loc_small + i])
                      for i in range(n_small)]
        small_to_sibling = [remote(small_hbm[i], small_recv[i], sem_small_d2d + i, sibling) for i in range(n_small)]
        small_to_chip = [[remote(small_all[i].at[my_chip], small_all[i].at[my_chip],
                                 sem_small_ici + N_REL * i + k, chip_at(k + 1))
                          for k in range(N_REL)] for i in range(n_small)]

        @pl.when((s == 0) & (t == 0))
        def _():
            h_load = pltpu.make_async_copy(h_hbm, h_vmem, h_sem)
            h_load.start()
            for cp in big_load + big_to_sibling + small_load + small_to_sibling:
                cp.start()
            h_load.wait()

        @pl.when((s == 0) & (t == n_sub - 1))
        def _():
            for cp in big_load + small_load:
                cp.wait()
            for cp in big_to_sibling + small_to_sibling:
                cp.wait_recv()
                cp.wait_send()
            for w in range(n_big):
                for k in range(N_REL):
                    shard = my_chip ^ (k + 1)
                    big_send[w][k] = (big_own[w][shard, 0] + big_recv[w][shard, 0]).astype(BF16)
                    big_to_chip[w][k].start()
            for i in range(n_small):
                small_all[i][my_chip] = small_own[i][...] + small_recv[i][...]
                for k in range(N_REL):
                    small_to_chip[i][k].start()

        @pl.when((s > 0) & (t == jnp.where(s == last, 0, min(1, n_sub - 1))))
        def _():
            k = s - 1
            cp = to_sibling(k)
            cp.wait_recv()
            cp.wait_send()
            send_ici[k] = (part[k % 2, my_rows, :] + recv_d2d[k]).astype(BF16)
            to_chip(k).start()

        def big_rows(w, half):
            rows = big_half[w][0]
            return big_out[w].at[pl.ds(pl.multiple_of(half * rows, 8), rows), :]

        big_store_mine = [pltpu.make_async_copy(big_mine[w], big_rows(w, c), local_sems.at[loc_out_big + 2 * w])
                          for w in range(n_big)]
        big_store_other = [pltpu.make_async_copy(big_other[w], big_rows(w, 1 - c),
                                                 local_sems.at[loc_out_big + 2 * w + 1]) for w in range(n_big)]
        small_store = [pltpu.make_async_copy(small_all[i], small_out[i], local_sems.at[loc_out_small + i])
                       for i in range(n_small)]

        @pl.when((s == last) & (t == 0))
        def _():
            for w in range(n_big):
                total = big_own[w][my_chip, 0] + big_recv[w][my_chip, 0]
                for k in range(N_REL):
                    big_to_chip[w][k].wait_recv()
                    total = total + big_land[w][k].astype(F32)
                big_mine[w][...] = total
                big_swap[w].start()
                big_store_mine[w].start()
            for i in range(n_small):
                for k in range(N_REL):
                    small_to_chip[i][k].wait_recv()
                small_store[i].start()

        r = _mm_tn(dp_ref[...], h_vmem[pl.ds(pl.multiple_of(t * tile, tile), tile), :])
        odd = shard_of_slot(s, shard_ref[0]) % 2
        for parity in range(2):
            rows = r[64 * parity:64 * parity + SHARD_ROWS]

            @pl.when((odd == parity) & (t == 0))
            def _():
                part[s % 2] = rows

            @pl.when((odd == parity) & (t > 0))
            def _():
                part[s % 2] += rows

        @pl.when(t == n_sub - 1)
        def _():
            to_sibling(s).start()

        @pl.when((s == last) & (t == n_sub - 1))
        def _():
            cp = to_sibling(last)
            cp.wait_recv()
            cp.wait_send()
            total = part[last % 2, my_rows, :] + recv_d2d[last]
            for k in range(last):
                to_chip(k).wait_recv()
                total = total + recv_ici[k].astype(F32)
            mine_buf[...] = total
            swap.start()
            out_mine = pltpu.make_async_copy(mine_buf, out_hbm.at[my_rows, :], local_sems.at[0])
            out_mine.start()
            swap.wait_recv()
            out_other = pltpu.make_async_copy(other_buf, out_hbm.at[other_rows, :], local_sems.at[1])
            out_other.start()
            for w in range(n_big):
                big_swap[w].wait_recv()
                big_store_other[w].start()
            stores = [out_mine, out_other] + big_store_mine + big_store_other + small_store
            for k in range(last):
                to_chip(k).wait_send()
            swap.wait_send()
            for w in range(n_big):
                for k in range(N_REL):
                    big_to_chip[w][k].wait_send()
                big_swap[w].wait_send()
            for i in range(n_small):
                for k in range(N_REL):
                    small_to_chip[i][k].wait_send()
            for cp in stores:
                cp.wait()

    half = (SHARD_HALF, D_MODEL)
    vmem = pltpu.VMEM
    scratch = [vmem((2, SHARD_ROWS, D_MODEL), F32), vmem((N_CHIPS,) + half, F32),
               vmem((N_REL,) + half, BF16), vmem((N_REL,) + half, BF16), vmem(half, F32), vmem(half, F32)]
    scratch += [vmem((N_CHIPS, 1) + hs, F32) for hs in big_half] * 2
    scratch += [vmem((N_REL,) + hs, BF16) for hs in big_half] * 2
    scratch += [vmem(hs, F32) for hs in big_half] * 2
    scratch += [vmem(a.shape, F32) for a in small] * 2 + [vmem((N_CHIPS,) + a.shape, F32) for a in small]
    scratch += [pltpu.SemaphoreType.DMA((n_sems,)), pltpu.SemaphoreType.DMA((n_sems,)),
                pltpu.SemaphoreType.DMA((n_local,)), vmem(h.shape, BF16), pltpu.SemaphoreType.DMA]
    n_hbm = n_big + n_small
    out = pl.pallas_call(
        body, name="reduce_gradients",
        out_shape=[jax.ShapeDtypeStruct((SHARD_ROWS, D_MODEL), F32)]
        + [jax.ShapeDtypeStruct((2 * hs[0], hs[1]), F32) for hs in big_half]
        + [jax.ShapeDtypeStruct((N_CHIPS,) + a.shape, F32) for a in small],
        grid_spec=pltpu.PrefetchScalarGridSpec(
            num_scalar_prefetch=1, grid=(N_CHIPS, n_sub),
            in_specs=[pl.BlockSpec((pl.Element(tile), pl.Element(SHARD_WINDOW)),
                                   lambda s, t, m: (t * tile, _shard_window_start(shard_of_slot(s, m[0])))),
                      ANY_SPEC] + [ANY_SPEC] * n_hbm,
            out_specs=[ANY_SPEC] * (1 + n_hbm),
            scratch_shapes=scratch),
        compiler_params=pltpu.CompilerParams(vmem_limit_bytes=VMEM_LIMIT),
    )(shard_arr, dproj, h, *big, *small)
    return out[:1 + n_big], out[1 + n_big:]


def _adamw(w, g, m, v):
    m2 = ADAM_B1 * m + (1.0 - ADAM_B1) * g
    v2 = ADAM_B2 * v + (1.0 - ADAM_B2) * (g * g)
    m_hat = m2 / (1.0 - ADAM_B1 ** ADAM_STEP)
    v_hat = v2 / (1.0 - ADAM_B2 ** ADAM_STEP)
    delta = -ADAM_LR * (m_hat / (jnp.sqrt(v_hat) + ADAM_EPS) + ADAM_WD * w)
    return delta, m2, v2


ADAM_STEPS = 4


def _adamw_all(shard_grads, shard_w, shard_m, shard_v, ra, rb, small_w, small_m, small_v):
    n_sh, n = len(shard_w), len(small_w)

    def body(*refs):
        sh_in, refs = refs[:4 * n_sh], refs[4 * n_sh:]
        ra_ref, rb_ref, refs = refs[0], refs[1], refs[2:]
        w_refs, m_refs, v_refs, refs = refs[:n], refs[n:2 * n], refs[2 * n:3 * n], refs[3 * n:]
        sh_out, outs = refs[:4 * n_sh], refs[4 * n_sh:]
        for k in range(n_sh):
            g = sh_in[k][...]
            delta, m2, v2 = _adamw(sh_in[n_sh + k][...], g, sh_in[2 * n_sh + k][...], sh_in[3 * n_sh + k][...])
            for ref, val in zip(sh_out[4 * k:4 * k + 4], (g, delta, m2, v2)):
                ref[...] = val

        @pl.when(pl.program_id(0) == 0)
        def _():
            g_outs, d_outs, m_outs, v_outs = outs[:n], outs[n:2 * n], outs[2 * n:3 * n], outs[3 * n:4 * n]
            ga, gb = ra_ref[0], rb_ref[0]
            for chip in range(1, N_CHIPS):
                ga = ga + ra_ref[chip]
                gb = gb + rb_ref[chip]
            outs[4 * n][...] = ga[ROW_LOSS:ROW_LOSS + 1, 0:128]
            grads = [ga[0:1, :], ga[1:2, :], ga[2:3, :], ga[3:4, :A_WIDTH], ga[3:4, A_WIDTH:],
                     gb[ROW_WS:ROW_WS + A_GROUPS * CHUNK, :].reshape(A_GROUPS, CHUNK, CHUNK),
                     gb[ROW_BS:ROW_BS + A_GROUPS, :], gb[ROW_SINK:ROW_SINK + 1, 0:4],
                     gb[ROW_REL:ROW_REL + 4, 0:N_BUCKETS]]
            for k in range(n):
                delta, m2, v2 = _adamw(w_refs[k][...], grads[k], m_refs[k][...], v_refs[k][...])
                g_outs[k][...] = grads[k]
                d_outs[k][...] = delta
                m_outs[k][...] = m2
                v_outs[k][...] = v2

    def rows_block(a):
        assert a.shape[0] % (8 * ADAM_STEPS) == 0
        return pl.BlockSpec((a.shape[0] // ADAM_STEPS, a.shape[1]), lambda i: (i, 0))

    sh_specs = [rows_block(w) for w in shard_w]
    small_in = [ra, rb, *small_w, *small_m, *small_v]
    small_out_shapes = [jax.ShapeDtypeStruct(w.shape, F32) for w in small_w] * 4 + [jax.ShapeDtypeStruct((1, 128), F32)]
    out = pl.pallas_call(
        body, name="adamw_all", grid=(ADAM_STEPS,),
        out_shape=[jax.ShapeDtypeStruct(w.shape, F32) for w in shard_w for _ in range(4)] + small_out_shapes,
        in_specs=sh_specs * 4 + [_full_spec(a.shape) for a in small_in],
        out_specs=[spec for spec in sh_specs for _ in range(4)] + [_full_spec(s.shape) for s in small_out_shapes],
        compiler_params=pltpu.CompilerParams(vmem_limit_bytes=VMEM_LIMIT),
    )(*shard_grads, *shard_w, *shard_m, *shard_v, *small_in)
    return [out[4 * k:4 * k + 4] for k in range(n_sh)], out[4 * n_sh:]


def kernel(x, mem, pre_norm_g, post_norm_g, mem_norm_g, w_in, w_mem_kv, v_norm_g, v_norm_b, w_spatial, b_spatial, attn_sinks, rel_bias, w_out, loss_target, m_pre_norm_g, m_post_norm_g, m_mem_norm_g, m_w_in, m_w_mem_kv, m_v_norm_g, m_v_norm_b, m_w_spatial, m_b_spatial, m_attn_sinks, m_rel_bias, m_w_out, v_pre_norm_g, v_post_norm_g, v_mem_norm_g, v_w_in, v_w_mem_kv, v_v_norm_g, v_v_norm_b, v_w_spatial, v_b_spatial, v_attn_sinks, v_rel_bias, v_w_out):
    n_ex, seq, _ = x.shape
    n_tok = n_ex * seq
    x2 = x.reshape(n_tok, D_MODEL)
    tgt2 = loss_target.reshape(n_tok, D_MODEL)
    buckets = jnp.asarray(_bucket_map())
    shard_arr = (2 * lax.axis_index("x") + lax.axis_index("y")).astype(jnp.int32).reshape(1)
    w_sp = w_spatial[0]
    w_in_t, m_w_in_t, v_w_in_t = (jnp.transpose(a[0]) for a in (w_in, m_w_in, v_w_in))
    rel_t, m_rel_t, v_rel_t = (jnp.transpose(a) for a in (rel_bias, m_rel_bias, v_rel_bias))

    x_arr = lax.axis_index("x").astype(jnp.int32).reshape(1)
    h_b, parts, (w_in_b, g_mkv, g_out), bias, b_sp = _gather_and_project(
        x2, pre_norm_g, w_in_t, w_mem_kv[0], w_out[0], rel_t, buckets, b_spatial[0], x_arr)
    w_mkv_b = g_mkv.reshape(D_MODEL, 2 * MEM_WIDTH)
    w_out_b = g_out.reshape(MIX_WIDTH, D_MODEL)

    dout, dproj, dwmkv, dgmem, dwout, dvg, dvb, dws, dbs, dsink, drel, loss_vec, dgpost = _mix(
        parts, mem, x2, tgt2, v_norm_g, v_norm_b, w_sp, b_sp, attn_sinks, bias, w_out_b, post_norm_g, mem_norm_g,
        w_mkv_b, n_ex, seq)

    dx, small_a, small_b = _backward_projection(
        x2, dout, dproj, pre_norm_g, w_in_b, [dgpost, dgmem, dvg, dvb, dws, dbs, dsink, drel, loss_vec, buckets])

    shard_shapes = [w_mem_kv.shape[1:], w_out.shape[1:]]
    big = [g.reshape(N_CHIPS, 2, s[0] // 2, s[1]) for g, s in zip((dwmkv, dwout), shard_shapes)]
    (g_win, g_wmkv, g_wout), (ga, gb) = _reduce_gradients(dproj, h_b, big, [small_a, small_b], shard_arr)

    small_w = [pre_norm_g, post_norm_g, mem_norm_g, v_norm_g, v_norm_b, w_sp, b_spatial[0], attn_sinks, rel_t]
    small_m = [m_pre_norm_g, m_post_norm_g, m_mem_norm_g, m_v_norm_g, m_v_norm_b, m_w_spatial[0], m_b_spatial[0],
               m_attn_sinks, m_rel_t]
    small_v = [v_pre_norm_g, v_post_norm_g, v_mem_norm_g, v_v_norm_g, v_v_norm_b, v_w_spatial[0], v_b_spatial[0],
               v_attn_sinks, v_rel_t]
    big_out, small_out = _adamw_all(
        [g_win, g_wmkv, g_wout], [w_in_t, w_mem_kv[0], w_out[0]], [m_w_in_t, m_w_mem_kv[0], m_w_out[0]],
        [v_w_in_t, v_w_mem_kv[0], v_w_out[0]], ga, gb, small_w, small_m, small_v)
    n_small = len(small_w)

    outputs = [small_out[4 * n_small][0, 0], dx.reshape(x.shape)]
    for kind in range(4):
        s = small_out[kind * n_small:(kind + 1) * n_small]
        outputs += [s[0], s[1], s[2], jnp.transpose(big_out[0][kind])[None], big_out[1][kind][None], s[3], s[4],
                    s[5][None], s[6][None], s[7], jnp.transpose(s[8]), big_out[2][kind][None]]
    return tuple(outputs)
```

```python
import functools

import numpy as np
import jax
import jax.numpy as jnp
from jax import lax
from jax.experimental import pallas as pl
from jax.experimental.pallas import tpu as pltpu

F32 = jnp.float32
BF16 = jnp.bfloat16
MESH = pl.DeviceIdType.MESH

D_MODEL = 1024
CHUNK = 128
A_WIDTH = 512
A_GROUPS = 4
SWA_WIDTH = 256
KV_WIDTH = 128
MEM_WIDTH = 256
MEM_LEN = 256
MIX_WIDTH = 1024
IN_WIDTH = 2816
N_BUCKETS = 32
MAX_DISTANCE = 128
EPS = 1e-6
NEG = -1e30
QK_SCALE = 0.125
HALF_HEAD_PAIR = 64

ADAM_LR = 0.001
ADAM_B1 = 0.9
ADAM_B2 = 0.999
ADAM_EPS = 1e-08
ADAM_WD = 0.01
ADAM_STEP = 10

N_CHIPS = 4
TILE_CHUNKS = 2
TILE = TILE_CHUNKS * CHUNK
PROJ_TILE = 1024
VMEM_LIMIT = 56 * 1024 * 1024

SMALL_A_ROWS = 8
ROW_LOSS = 4
ROW_WS = 0
ROW_BS = 512
ROW_SINK = 520
ROW_REL = 528
SMALL_B_ROWS = 536


def _mm(a, b):
    return lax.dot_general(a, b, (((1,), (0,)), ((), ())), preferred_element_type=F32)


def _mm_nt(a, b):
    return lax.dot_general(a, b, (((1,), (1,)), ((), ())), preferred_element_type=F32)


def _mm_tn(a, b):
    return lax.dot_general(a, b, (((0,), (0,)), ((), ())), preferred_element_type=F32)


def _bucket_map():
    qi = np.arange(CHUNK)[:, None]
    kj = np.arange(2 * CHUNK)[None, :]
    n = np.maximum(qi + CHUNK - kj, 0)
    max_exact = N_BUCKETS // 2
    large = max_exact + (np.log(np.maximum(n, 1) / max_exact) / np.log(MAX_DISTANCE / max_exact)
                         * (N_BUCKETS - max_exact)).astype(np.int32)
    large = np.minimum(large, N_BUCKETS - 1)
    return np.where(n < max_exact, n, large).astype(np.int32)


_GELU_C = 0.7978845608028654
_GELU_A = 0.044715
_GELU_K1 = 2.0 * _GELU_C
_GELU_K2 = 2.0 * _GELU_C * _GELU_A


def _gelu(x):
    x2 = x * x
    s = 1.0 / (1.0 + jnp.exp(x * (-_GELU_K1 - _GELU_K2 * x2)))
    return x * s, (s, x2)


def _gelu_grad(x, saved):
    s, x2 = saved
    return s + x * (s * (1.0 - s)) * (_GELU_K1 + 3.0 * _GELU_K2 * x2)


def _sigmoid(x):
    return 1.0 / (1.0 + jnp.exp(-x))


def _lane_lo(shape):
    return lax.broadcasted_iota(jnp.int32, shape, 1) < HALF_HEAD_PAIR


def _swa_variants(t):
    lo = _lane_lo(t.shape)
    tr = pltpu.roll(t, HALF_HEAD_PAIR, 1)
    zero = jnp.zeros_like(t)
    return (jnp.where(lo, t, zero).astype(BF16), jnp.where(lo, zero, tr).astype(BF16),
            jnp.where(lo, tr, zero).astype(BF16), jnp.where(lo, zero, t).astype(BF16))


def _swa_unvariants(d0, d1, d2, d3):
    lo = _lane_lo(d0.shape)
    zero = jnp.zeros_like(d0)
    rolled = jnp.where(lo, zero, d1) + jnp.where(lo, d2, zero)
    return jnp.where(lo, d0, zero) + jnp.where(lo, zero, d3) + pltpu.roll(rolled, HALF_HEAD_PAIR, 1)


def _mem_variants(t):
    out = []
    for pair in range(2):
        tp = t[:, pair * 128:(pair + 1) * 128]
        lo = _lane_lo(tp.shape)
        zero = jnp.zeros_like(tp)
        out.append(jnp.where(lo, tp, zero).astype(BF16))
        out.append(jnp.where(lo, zero, tp).astype(BF16))
    return out


def _mem_unvariants(d0, d1, d2, d3):
    lo = _lane_lo(d0.shape)
    return jnp.concatenate([jnp.where(lo, d0, d1), jnp.where(lo, d2, d3)], axis=-1)


def _softmax(logits, sinks):
    m = jnp.max(logits, axis=-1, keepdims=True)
    if sinks is not None:
        m = jnp.maximum(m, sinks)
    p = jnp.exp(logits - m)
    den = jnp.sum(p, axis=-1, keepdims=True)
    if sinks is None:
        return p * (1.0 / den), None
    es = jnp.exp(sinks - m)
    inv = 1.0 / (den + es)
    return p * inv, es * inv


def _band_valid(with_prev):
    qi = lax.broadcasted_iota(jnp.int32, (CHUNK, 2 * CHUNK), 0)
    kj = lax.broadcasted_iota(jnp.int32, (CHUNK, 2 * CHUNK), 1)
    in_cur = (kj >= CHUNK) & (kj - CHUNK <= qi)
    if not with_prev:
        return in_cur
    return in_cur | ((kj < CHUNK) & (kj > qi))


def _causal_weights(ws_ref):
    row = lax.broadcasted_iota(jnp.int32, (CHUNK, CHUNK), 0)
    col = lax.broadcasted_iota(jnp.int32, (CHUNK, CHUNK), 1)
    return [jnp.where(row >= col, ws_ref[g], 0.0).astype(BF16) for g in range(A_GROUPS)]


def _rows_to_lanes(a, n):
    return jnp.concatenate([a[c * CHUNK:(c + 1) * CHUNK] for c in range(n)], axis=1)


def _lanes_to_rows(a, n):
    w = a.shape[1] // n
    return jnp.concatenate([a[:, c * w:(c + 1) * w] for c in range(n)], axis=0)


def _stack_heads(pair01, pair23):
    return jnp.concatenate([pair01[:, :256], pair01[:, 256:], pair23[:, :256], pair23[:, 256:]], axis=0)


def _pair_heads(s, r):
    return (jnp.concatenate([s[0:r], s[r:2 * r]], axis=1), jnp.concatenate([s[2 * r:3 * r], s[3 * r:4 * r]], axis=1))


def _pair_operands(variants):
    return (jnp.concatenate(variants[0:2], axis=0), jnp.concatenate(variants[2:4], axis=0))


def _split_pair_grads(d_pairs):
    return d_pairs[0][:256], d_pairs[0][256:], d_pairs[1][:256], d_pairs[1][256:]


def _halves_bf16(a):
    return (a[:, :128].astype(BF16), a[:, 128:].astype(BF16))


def _group_a_forward(au, av, vg, vb, wm, bs_rows):
    gu, tu = _gelu(au)
    gv, tv = _gelu(av)
    ya, res = [], []
    for g in range(A_GROUPS):
        sl = slice(g * 128, (g + 1) * 128)
        xg = gv[:, sl]
        xc = xg - jnp.mean(xg, axis=-1, keepdims=True)
        rstd = lax.rsqrt(jnp.mean(xc * xc, axis=-1, keepdims=True) + EPS)
        xhat = xc * rstd
        vn = _rows_to_lanes((xhat * vg[:, sl] + vb[:, sl]).astype(BF16), TILE_CHUNKS)
        s = _lanes_to_rows(_mm(wm[g], vn), TILE_CHUNKS) + bs_rows[g]
        ya.append(gu[:, sl] * s)
        res.append((xhat, rstd, vn, s))
    return ya, dict(gu=gu, tu=tu, tv=tv, groups=res)


def _attention_logits(qp, k_pairs):
    return _stack_heads(_mm_nt(qp[0], k_pairs[0]), _mm_nt(qp[1], k_pairs[1]))


def _attention_out(p, v_pairs, r):
    pp = _pair_heads(p.astype(BF16), r)
    return jnp.concatenate([_mm(pp[0], v_pairs[0]), _mm(pp[1], v_pairs[1])], axis=-1), pp


def _attention_dprobs(do_pairs, v_pairs):
    return _stack_heads(_mm_nt(do_pairs[0], v_pairs[0]), _mm_nt(do_pairs[1], v_pairs[1]))


def _softmax_backward(p, dp):
    delta = jnp.sum(p * dp, axis=-1, keepdims=True)
    return p * (dp - delta), delta


def _attention_grads(dl, pp, do_pairs, qp, k_pairs, r):
    dlp = _pair_heads(dl.astype(BF16), r)
    dq = jnp.concatenate([_mm(dlp[0], k_pairs[0]), _mm(dlp[1], k_pairs[1])], axis=-1)
    dk = (_mm_tn(dlp[0], qp[0]), _mm_tn(dlp[1], qp[1]))
    dv = (_mm_tn(pp[0], do_pairs[0]), _mm_tn(pp[1], do_pairs[1]))
    return dq, dk, dv


def _tile_specs(n_tiles_ex, width):
    return pl.BlockSpec((TILE, width), lambda b, i: (b * n_tiles_ex + jnp.minimum(i, n_tiles_ex - 1), 0))


def _prev_chunk_spec(n_tiles_ex, width):
    def index(b, i):
        chunk = TILE_CHUNKS * jnp.minimum(i, n_tiles_ex - 1)
        return (b * n_tiles_ex * TILE_CHUNKS + jnp.maximum(chunk - 1, 0), 0)
    return pl.BlockSpec((CHUNK, width), index)


def _full_spec(shape):
    zeros = (0,) * len(shape)
    return pl.BlockSpec(shape, lambda *_: zeros)


SMEM_SPEC = pl.BlockSpec(memory_space=pltpu.SMEM)
ANY_SPEC = pl.BlockSpec(memory_space=pl.ANY)


def _fill_bias(rel_ref, bk_ref, out_ref):
    bk = bk_ref[...]
    for h in range(4):
        acc = jnp.zeros((CHUNK, 2 * CHUNK), F32)
        for b in range(N_BUCKETS):
            acc = jnp.where(bk == b, rel_ref[h, b], acc)
        for t, with_prev in enumerate((True, False)):
            out_ref[t, h * CHUNK:(h + 1) * CHUNK, :] = jnp.where(_band_valid(with_prev), acc, NEG)


PROJ_WIDTHS = (A_WIDTH, A_WIDTH, SWA_WIDTH, KV_WIDTH, KV_WIDTH, MEM_WIDTH, MIX_WIDTH)
PROJ_OFFSETS = tuple(int(v) for v in np.cumsum((0,) + PROJ_WIDTHS))


MXU_TILE = 256
HALF_WIDTH = IN_WIDTH // 2
PHASE_COLS = (HALF_WIDTH // MXU_TILE * MXU_TILE, IN_WIDTH - HALF_WIDTH // MXU_TILE * MXU_TILE)


def _phase_columns(phase, chip_x):
    if phase == 0:
        return 0 if chip_x == 0 else IN_WIDTH - PHASE_COLS[0]
    return PHASE_COLS[0] if chip_x == 0 else 0


def _phase_parts(phase, chip_x):
    start = _phase_columns(phase, chip_x)
    return [(k, PROJ_OFFSETS[k] - start) for k in range(len(PROJ_WIDTHS))
            if start <= PROJ_OFFSETS[k] and PROJ_OFFSETS[k + 1] <= start + PHASE_COLS[phase]]


def _gather_and_project(x2, g_pre, w_in_s, w_mkv_s, w_out_s, rel_bias_t, buckets, b_spatial, x_arr):
    n_tok = x2.shape[0]
    n_tiles = n_tok // PROJ_TILE
    last = n_tiles - 1
    shapes = [w_in_s.shape, w_mkv_s.shape, w_out_s.shape]
    n_w = len(shapes)

    def body(x_sref, x_ref, g_ref, win_hbm, wmkv_hbm, wout_hbm, rel_ref, bk_ref, bsp_ref, h_ref, *refs):
        part_refs, refs = refs[:len(PROJ_WIDTHS)], refs[len(PROJ_WIDTHS):]
        bias_ref, bs_ref, refs = refs[0], refs[1], refs[2:]
        gin_hbm, gmkv_hbm, gout_hbm, wg, stage_in, stage_mkv, stage_out, own_mkv, own_out, h_all = refs[:10]
        send_sems, recv_sems, local_sems = refs[10:]
        p, t = pl.program_id(0), pl.program_id(1)
        x, y, c = lax.axis_index("x"), lax.axis_index("y"), lax.axis_index("c")
        me, sibling = (x, y, c), (x, y, 1 - c)
        my_shard = 2 * x + y
        gathered = [wg, gmkv_hbm, gout_hbm]

        def half_rows(w, shard, half):
            rows = shapes[w][0] // 2
            if w == 0:
                return wg.at[pl.ds(pl.multiple_of(shard * shapes[0][0] + half * rows, 16), rows), :]
            return gathered[w].at[shard, pl.ds(half * rows, rows), :]

        def first(w, rel):
            src = half_rows(w, my_shard, c) if w == 0 else (own_mkv, own_out)[w - 1].at[
                pl.ds(c * (shapes[w][0] // 2), shapes[w][0] // 2), :]
            k = 3 * w + rel - 1
            return pltpu.make_async_remote_copy(
                src_ref=src, dst_ref=half_rows(w, my_shard, c), send_sem=send_sems.at[k], recv_sem=recv_sems.at[k],
                device_id=(x ^ (rel >> 1), y ^ (rel & 1), c), device_id_type=MESH)

        def landed(w, rel):
            k = 3 * w + rel - 1
            ref = half_rows(w, my_shard ^ rel, c)
            return pltpu.make_async_remote_copy(src_ref=ref, dst_ref=ref, send_sem=send_sems.at[k],
                                                recv_sem=recv_sems.at[k], device_id=me, device_id_type=MESH)

        def passed(w, rel, half, to):
            k = 9 + 3 * w + rel - 1
            ref = half_rows(w, my_shard ^ rel, half)
            return pltpu.make_async_remote_copy(src_ref=ref, dst_ref=ref, send_sem=send_sems.at[k],
                                                recv_sem=recv_sems.at[k], device_id=to, device_id_type=MESH)

        def pass_on(w, rels):
            for rel in rels:
                landed(w, rel).wait_recv()
                passed(w, rel, c, sibling).start()
            for rel in rels:
                passed(w, rel, 1 - c, me).wait_recv()

        own_stores = [pltpu.make_async_copy(own_mkv, gmkv_hbm.at[my_shard], local_sems.at[3]),
                      pltpu.make_async_copy(own_out, gout_hbm.at[my_shard], local_sems.at[4])]

        @pl.when((p == 0) & (t == 0))
        def _():
            loads = [pltpu.make_async_copy(src, dst, local_sems.at[k]) for k, (src, dst) in enumerate(
                ((win_hbm, stage_in), (wmkv_hbm, stage_mkv), (wout_hbm, stage_out)))]
            for cp in loads:
                cp.start()
            loads[0].wait()
            wg[pl.ds(pl.multiple_of(my_shard * shapes[0][0], 16), shapes[0][0]), :] = stage_in[...].astype(BF16)
            for rel in (1, 2):
                first(0, rel).start()
            loads[1].wait()
            loads[2].wait()
            own_mkv[...] = stage_mkv[...].astype(BF16)
            own_out[...] = stage_out[...].astype(BF16)
            for cp in own_stores:
                cp.start()
            _fill_bias(rel_ref, bk_ref, bias_ref)
            for g in range(A_GROUPS):
                bs_ref[g] = jnp.transpose(jnp.broadcast_to(bsp_ref[g:g + 1, :], (CHUNK, CHUNK)))
            pass_on(0, (1,))
            first(0, 3).start()

        @pl.when((p == 0) & (t == n_tiles // 2))
        def _():
            for w in (1, 2):
                for rel in (1, 2, 3):
                    first(w, rel).start()

        store = pltpu.make_async_copy(wg, gin_hbm, local_sems.at[5])

        @pl.when((p == 1) & (t == 0))
        def _():
            pass_on(0, (2, 3))
            store.start()

        @pl.when((p == 1) & (t == n_tiles // 2))
        def _():
            for w in (1, 2):
                pass_on(w, (1, 2, 3))

        tile_rows = pl.ds(pl.multiple_of(t * PROJ_TILE, PROJ_TILE), PROJ_TILE)

        def project(h, phase):
            start = jnp.where(x_sref[0] == 0, _phase_columns(phase, 0), _phase_columns(phase, 1))
            proj = _mm_nt(h, wg[pl.ds(pl.multiple_of(start, MXU_TILE), PHASE_COLS[phase]), :])
            for chip_x in range(2):
                @pl.when(x_sref[0] == chip_x)
                def _():
                    for k, lo in _phase_parts(phase, chip_x):
                        part_refs[k][...] = proj[:, lo:lo + PROJ_WIDTHS[k]].astype(BF16)

        @pl.when(p == 0)
        def _():
            xv = x_ref[...]
            r = lax.rsqrt(jnp.mean(xv * xv, axis=-1, keepdims=True) + EPS)
            h = (xv * r * g_ref[...]).astype(BF16)
            h_ref[...] = h
            h_all[tile_rows, :] = h
            project(h, 0)

        @pl.when(p == 1)
        def _():
            project(h_all[tile_rows, :], 1)

        @pl.when((p == 1) & (t == last))
        def _():
            for w in range(n_w):
                for rel in (1, 2, 3):
                    first(w, rel).wait_send()
                    passed(w, rel, c, sibling).wait_send()
            for cp in own_stores:
                cp.wait()
            store.wait()

    def written_in(k):
        phase_on = [next(ph for ph in range(2) if k in dict(_phase_parts(ph, chip_x))) for chip_x in range(2)]

        def index(p, t, xs):
            phase = jnp.where(xs[0] == 0, phase_on[0], phase_on[1])
            return (jnp.where(p == phase, t, jnp.where(p < phase, 0, last)), 0)
        return index

    part_specs = [pl.BlockSpec((PROJ_TILE, PROJ_WIDTHS[k]), written_in(k)) for k in range(len(PROJ_WIDTHS))]
    vmem = pltpu.VMEM
    out = pl.pallas_call(
        body, name="gather_and_project",
        out_shape=[jax.ShapeDtypeStruct((n_tok, D_MODEL), BF16)]
        + [jax.ShapeDtypeStruct((n_tok, w), BF16) for w in PROJ_WIDTHS]
        + [jax.ShapeDtypeStruct((2, 4 * CHUNK, 2 * CHUNK), F32), jax.ShapeDtypeStruct((A_GROUPS, CHUNK, CHUNK), F32)]
        + [jax.ShapeDtypeStruct((N_CHIPS * shapes[0][0], shapes[0][1]), BF16)]
        + [jax.ShapeDtypeStruct((N_CHIPS,) + s, BF16) for s in shapes[1:]],
        grid_spec=pltpu.PrefetchScalarGridSpec(
            num_scalar_prefetch=1, grid=(2, n_tiles),
            in_specs=[pl.BlockSpec((PROJ_TILE, D_MODEL), lambda p, t, xs: (jnp.where(p == 0, t, last), 0)),
                      pl.BlockSpec((1, D_MODEL), lambda p, t, xs: (0, 0)), ANY_SPEC, ANY_SPEC, ANY_SPEC, SMEM_SPEC,
                      pl.BlockSpec(buckets.shape, lambda p, t, xs: (0, 0)),
                      pl.BlockSpec(b_spatial.shape, lambda p, t, xs: (0, 0))],
            out_specs=[pl.BlockSpec((PROJ_TILE, D_MODEL), lambda p, t, xs: (jnp.where(p == 0, t, last), 0))]
            + part_specs + [pl.BlockSpec((2, 4 * CHUNK, 2 * CHUNK), lambda p, t, xs: (0, 0, 0)),
                            pl.BlockSpec((A_GROUPS, CHUNK, CHUNK), lambda p, t, xs: (0, 0, 0))] + [ANY_SPEC] * 3,
            scratch_shapes=[vmem((N_CHIPS * shapes[0][0], shapes[0][1]), BF16), vmem(shapes[0], F32),
                            vmem(shapes[1], F32), vmem(shapes[2], F32), vmem(shapes[1], BF16), vmem(shapes[2], BF16),
                            vmem((n_tok, D_MODEL), BF16),
                            pltpu.SemaphoreType.DMA((18,)), pltpu.SemaphoreType.DMA((18,)),
                            pltpu.SemaphoreType.DMA((6,))]),
        compiler_params=pltpu.CompilerParams(vmem_limit_bytes=VMEM_LIMIT),
    )(x_arr, x2, g_pre, w_in_s, w_mkv_s, w_out_s, rel_bias_t, buckets, b_spatial)
    n_parts = len(PROJ_WIDTHS)
    return out[0], list(out[1:1 + n_parts]), out[3 + n_parts:], out[1 + n_parts], out[2 + n_parts]


def _load_chunk(j, i, sk_ref, sv_ref, skp_ref, svp_ref):
    rows = slice(j * CHUNK, (j + 1) * CHUNK)
    if j == 0:
        k_prev, v_prev, table = skp_ref[...], svp_ref[...], jnp.where(i > 0, 0, 1)
    else:
        prev = slice((j - 1) * CHUNK, j * CHUNK)
        k_prev, v_prev, table = sk_ref[prev, :], sv_ref[prev, :], 0
    k_pairs = _pair_operands(_swa_variants(jnp.concatenate([k_prev, sk_ref[rows, :]], axis=0).astype(F32)))
    v_pairs = _pair_operands(_swa_variants(jnp.concatenate([v_prev, sv_ref[rows, :]], axis=0).astype(F32)))
    return rows, k_pairs, v_pairs, table


def _tile_constants(ws_ref, bs_ref, sink_ref, mkv_v):
    wm = _causal_weights(ws_ref)
    bs_rows = [jnp.concatenate([bs_ref[g]] * TILE_CHUNKS, axis=0) for g in range(A_GROUPS)]
    sink_col = jnp.max(jnp.concatenate([jnp.full((CHUNK, 128), sink_ref[0, h], F32) for h in range(4)] * TILE_CHUNKS,
                                       axis=0), axis=-1, keepdims=True)
    mk_pairs = _pair_operands(_mem_variants(mkv_v[:, :MEM_WIDTH]))
    mv_pairs = _pair_operands(_mem_variants(mkv_v[:, MEM_WIDTH:]))
    return wm, bs_rows, sink_col, mk_pairs, mv_pairs


def _mix(parts, mem, x2, tgt2, v_g, v_b, w_sp, b_sp, sinks, bias, w_out, g_post, g_mem, w_mkv, n_ex, seq):
    n_tiles_ex = seq // TILE
    n_tok = n_ex * seq
    au, av, sq, sk, sv, mq, z = parts
    col = dict(zip(("au", "av", "sq", "sk", "sv", "mq", "z"),
                   (slice(PROJ_OFFSETS[k], PROJ_OFFSETS[k + 1]) for k in range(len(PROJ_WIDTHS)))))
    before_kv, after_kv = slice(0, col["sk"].start), slice(col["sv"].stop, IN_WIDTH)

    def body(au_ref, av_ref, sq_ref, sk_ref, sv_ref, skp_ref, svp_ref, mq_ref, z_ref, mem_ref, x_ref, tgt_ref,
             vg_ref, vb_ref, ws_ref, bs_ref, sink_ref, bias_ref, wout_ref, gpost_ref, gmem_ref, wmkv_ref,
             dout_ref, dproj_ref, dwmkv_ref, dgmem_ref, dwout_ref, dvg_ref, dvb_ref, dws_ref, dbs_ref, dsink_ref,
             drel_ref, loss_ref, dgpost_ref, carry_dp, carry_k, carry_v, memn_s, mkv_s, dmkv_s):
        b, i = pl.program_id(0), pl.program_id(1)

        @pl.when((b == 0) & (i == 0))
        def _():
            for ref in (dwmkv_ref, dgmem_ref, dwout_ref, dvg_ref, dvb_ref, dws_ref, dbs_ref, dsink_ref, drel_ref,
                        loss_ref, dgpost_ref):
                ref[...] = jnp.zeros_like(ref)

        def normalized_mem():
            m = mem_ref[0]
            return m * lax.rsqrt(jnp.mean(m * m, axis=-1, keepdims=True) + EPS)

        @pl.when(i == 0)
        def _():
            memn_s[...] = (normalized_mem() * gmem_ref[...]).astype(BF16)
            mkv_s[...] = _mm(memn_s[...], wmkv_ref[...])
            dmkv_s[...] = jnp.zeros_like(dmkv_s)
            carry_k[...] = jnp.zeros_like(carry_k)
            carry_v[...] = jnp.zeros_like(carry_v)

        @pl.when(i > 0)
        def _():
            dproj_ref[:, before_kv] = carry_dp[:, before_kv]
            dproj_ref[:, after_kv] = carry_dp[:, after_kv]

        @pl.when(i < n_tiles_ex)
        def _():
            wm, bs_rows, sink_col, mk_pairs, mv_pairs = _tile_constants(ws_ref, bs_ref, sink_ref, mkv_s[...])
            vg = vg_ref[...]

            au_v, av_v = au_ref[...].astype(F32), av_ref[...].astype(F32)
            ya, res = _group_a_forward(au_v, av_v, vg, vb_ref[...], wm, bs_rows)
            swa, logits, yb = [], [], []
            for j in range(TILE_CHUNKS):
                rows, k_pairs, v_pairs, table = _load_chunk(j, i, sk_ref, sv_ref, skp_ref, svp_ref)
                qp = _halves_bf16(sq_ref[rows, :] * QK_SCALE)
                logits.append(_attention_logits(qp, k_pairs) + bias_ref[table])
                swa.append([rows, k_pairs, v_pairs, qp])
            p_swa, sink_p = _softmax(jnp.concatenate(logits, axis=0), sink_col)
            for j in range(TILE_CHUNKS):
                out, pp = _attention_out(p_swa[j * 4 * CHUNK:(j + 1) * 4 * CHUNK], swa[j][2], CHUNK)
                yb.append(out)
                swa[j].append(pp)
            mqp = _halves_bf16(mq_ref[...] * QK_SCALE)
            pm, _ = _softmax(_attention_logits(mqp, mk_pairs), None)
            yc, ppm = _attention_out(pm, mv_pairs, TILE)
            ycat = jnp.concatenate(ya + [jnp.concatenate(yb, axis=0), yc], axis=-1)

            zv = z_ref[...].astype(F32)
            sig = _sigmoid(zv)
            sz = zv * sig
            y_b = (ycat * sz).astype(BF16)
            o = _mm(y_b, wout_ref[...])
            r2 = lax.rsqrt(jnp.mean(o * o, axis=-1, keepdims=True) + EPS)
            nrm = o * r2
            gp = gpost_ref[...]
            diff = x_ref[...] + nrm * gp - tgt_ref[...]
            loss_ref[...] += jnp.sum(diff * diff) * (0.5 / D_MODEL)
            dout = diff * (1.0 / D_MODEL)
            dout_ref[...] = dout
            dgpost_ref[...] += jnp.sum(dout * nrm, axis=0, keepdims=True)
            dn = dout * gp
            do_b = (r2 * (dn - nrm * jnp.mean(dn * nrm, axis=-1, keepdims=True))).astype(BF16)
            dwout_ref[...] += _mm_tn(y_b, do_b)
            dy = _mm_nt(do_b, wout_ref[...])
            carry_dp[:, col["z"]] = (dy * ycat * (sig * (1.0 + zv * (1.0 - sig)))).astype(BF16)
            dyc = dy * sz

            dgu, dgv = [], []
            for g in range(A_GROUPS):
                sl = slice(g * 128, (g + 1) * 128)
                xhat, rstd, vn, s = res["groups"][g]
                dya = dyc[:, sl]
                dgu.append(dya * s)
                ds = dya * res["gu"][:, sl]
                dbs_ref[:, sl] += sum(ds[c * CHUNK:(c + 1) * CHUNK] for c in range(TILE_CHUNKS))
                ds_b = _rows_to_lanes(ds.astype(BF16), TILE_CHUNKS)
                dws_ref[g] += _mm_nt(ds_b, vn)
                dvn = _lanes_to_rows(_mm_tn(wm[g], ds_b), TILE_CHUNKS)
                dvg_ref[:, sl] += jnp.sum(dvn * xhat, axis=0, keepdims=True)
                dvb_ref[:, sl] += jnp.sum(dvn, axis=0, keepdims=True)
                dxh = dvn * vg[:, sl]
                dgv.append(rstd * (dxh - jnp.mean(dxh, axis=-1, keepdims=True)
                                   - xhat * jnp.mean(dxh * xhat, axis=-1, keepdims=True)))
            carry_dp[:, col["au"]] = (jnp.concatenate(dgu, axis=-1) * _gelu_grad(au_v, res["tu"])).astype(BF16)
            carry_dp[:, col["av"]] = (jnp.concatenate(dgv, axis=-1) * _gelu_grad(av_v, res["tv"])).astype(BF16)

            do_pairs = [_halves_bf16(dyc[rows, A_WIDTH:A_WIDTH + SWA_WIDTH]) for rows, *_ in swa]
            dl_swa, delta = _softmax_backward(p_swa, jnp.concatenate(
                [_attention_dprobs(do_pairs[j], swa[j][2]) for j in range(TILE_CHUNKS)], axis=0))
            sink_terms = sink_p * delta
            lane4 = lax.broadcasted_iota(jnp.int32, (1, 128), 1)
            dsink_vec = jnp.zeros((1, 128), F32)
            for h in range(4):
                head_sum = sum(jnp.sum(sink_terms[(4 * j + h) * CHUNK:(4 * j + h + 1) * CHUNK])
                               for j in range(TILE_CHUNKS))
                dsink_vec = dsink_vec + jnp.where(lane4 == h, -head_sum, 0.0)
            dsink_ref[...] += dsink_vec
            drel_ref[...] += sum(dl_swa[j * 4 * CHUNK:(j + 1) * 4 * CHUNK] for j in range(TILE_CHUNKS))
            dk_parts, dv_parts = [], []
            for j, (rows, k_pairs, v_pairs, qp, pp) in enumerate(swa):
                dq, dk, dv = _attention_grads(dl_swa[j * 4 * CHUNK:(j + 1) * 4 * CHUNK], pp, do_pairs[j], qp, k_pairs,
                                              CHUNK)
                carry_dp[rows, col["sq"]] = (dq * QK_SCALE).astype(BF16)
                dk_parts.append(_swa_unvariants(*_split_pair_grads(dk)))
                dv_parts.append(_swa_unvariants(*_split_pair_grads(dv)))

            dc_pairs = _halves_bf16(dyc[:, A_WIDTH + SWA_WIDTH:])
            dl_mem, _ = _softmax_backward(pm, _attention_dprobs(dc_pairs, mv_pairs))
            dmq, dmk, dmv = _attention_grads(dl_mem, ppm, dc_pairs, mqp, mk_pairs, TILE)
            carry_dp[:, col["mq"]] = (dmq * QK_SCALE).astype(BF16)
            dmkv_s[...] += jnp.concatenate([_mem_unvariants(*_split_pair_grads(dmk)),
                                            _mem_unvariants(*_split_pair_grads(dmv))], axis=-1)

            for parts_c, carry, cols in ((dk_parts, carry_k, col["sk"]), (dv_parts, carry_v, col["sv"])):
                @pl.when(i > 0)
                def _():
                    dproj_ref[:, cols] = (carry[...] + jnp.concatenate(
                        [jnp.zeros((TILE - CHUNK, KV_WIDTH), F32), parts_c[0][:CHUNK]], axis=0)).astype(BF16)
                new = [parts_c[0][CHUNK:]]
                for j in range(1, TILE_CHUNKS):
                    new[-1] = new[-1] + parts_c[j][:CHUNK]
                    new.append(parts_c[j][CHUNK:])
                carry[...] = jnp.concatenate(new, axis=0)

        @pl.when(i == n_tiles_ex)
        def _():
            dproj_ref[:, col["sk"]] = carry_k[...].astype(BF16)
            dproj_ref[:, col["sv"]] = carry_v[...].astype(BF16)
            d_b = dmkv_s[...].astype(BF16)
            dwmkv_ref[...] += _mm_tn(memn_s[...], d_b)
            dgmem_ref[...] += jnp.sum(_mm_nt(d_b, wmkv_ref[...]) * normalized_mem(), axis=0, keepdims=True)

    tile = functools.partial(_tile_specs, n_tiles_ex)
    prev = functools.partial(_prev_chunk_spec, n_tiles_ex)
    late = pl.BlockSpec((TILE, IN_WIDTH), lambda b, i: (b * n_tiles_ex + jnp.maximum(i - 1, 0), 0))
    return pl.pallas_call(
        body, name="mix", grid=(n_ex, n_tiles_ex + 1),
        out_shape=[jax.ShapeDtypeStruct((n_tok, D_MODEL), F32), jax.ShapeDtypeStruct((n_tok, IN_WIDTH), BF16),
                   jax.ShapeDtypeStruct((D_MODEL, 2 * MEM_WIDTH), F32), jax.ShapeDtypeStruct((1, D_MODEL), F32),
                   jax.ShapeDtypeStruct((MIX_WIDTH, D_MODEL), F32), jax.ShapeDtypeStruct((1, A_WIDTH), F32),
                   jax.ShapeDtypeStruct((1, A_WIDTH), F32), jax.ShapeDtypeStruct((A_GROUPS, CHUNK, CHUNK), F32),
                   jax.ShapeDtypeStruct((CHUNK, A_WIDTH), F32), jax.ShapeDtypeStruct((1, 128), F32),
                   jax.ShapeDtypeStruct((4 * CHUNK, 2 * CHUNK), F32), jax.ShapeDtypeStruct((1, 128), F32),
                   jax.ShapeDtypeStruct((1, D_MODEL), F32)],
        in_specs=[tile(A_WIDTH), tile(A_WIDTH), tile(SWA_WIDTH), tile(KV_WIDTH), tile(KV_WIDTH),
                  prev(KV_WIDTH), prev(KV_WIDTH), tile(MEM_WIDTH), tile(MIX_WIDTH),
                  pl.BlockSpec((1, MEM_LEN, D_MODEL), lambda b, i: (b, 0, 0)),
                  tile(D_MODEL), tile(D_MODEL),
                  _full_spec((1, A_WIDTH)), _full_spec((1, A_WIDTH)), _full_spec((A_GROUPS, CHUNK, CHUNK)),
                  _full_spec((A_GROUPS, CHUNK, CHUNK)), SMEM_SPEC, _full_spec((2, 4 * CHUNK, 2 * CHUNK)),
                  _full_spec((MIX_WIDTH, D_MODEL)), _full_spec((1, D_MODEL)), _full_spec((1, D_MODEL)),
                  _full_spec((D_MODEL, 2 * MEM_WIDTH))],
        out_specs=[tile(D_MODEL), late, _full_spec((D_MODEL, 2 * MEM_WIDTH)), _full_spec((1, D_MODEL)),
                   _full_spec((MIX_WIDTH, D_MODEL)), _full_spec((1, A_WIDTH)), _full_spec((1, A_WIDTH)),
                   _full_spec((A_GROUPS, CHUNK, CHUNK)), _full_spec((CHUNK, A_WIDTH)), _full_spec((1, 128)),
                   _full_spec((4 * CHUNK, 2 * CHUNK)), _full_spec((1, 128)), _full_spec((1, D_MODEL))],
        scratch_shapes=[pltpu.VMEM((TILE, IN_WIDTH), BF16), pltpu.VMEM((TILE, KV_WIDTH), F32),
                        pltpu.VMEM((TILE, KV_WIDTH), F32), pltpu.VMEM((MEM_LEN, D_MODEL), BF16),
                        pltpu.VMEM((MEM_LEN, 2 * MEM_WIDTH), F32), pltpu.VMEM((MEM_LEN, 2 * MEM_WIDTH), F32)],
        compiler_params=pltpu.CompilerParams(vmem_limit_bytes=VMEM_LIMIT),
    )(au, av, sq, sk, sv, sk, sv, mq, z, mem, x2, tgt2, v_g, v_b, w_sp, b_sp, sinks, bias, w_out, g_post, g_mem,
      w_mkv)


BWD_PROJ_TILE = 512


def _fill_small_grads(dgpre_ref, dgpost_ref, dgmem_ref, dvg_ref, dvb_ref, dws_ref, dbs_ref, dsink_ref, drel_ref,
                      loss_ref, bk_ref, a_ref, b_ref):
    a_ref[...] = jnp.zeros_like(a_ref)
    b_ref[...] = jnp.zeros_like(b_ref)
    a_ref[0:1, :] = dgpre_ref[...]
    a_ref[1:2, :] = dgpost_ref[...]
    a_ref[2:3, :] = dgmem_ref[...]
    a_ref[3:4, :] = jnp.concatenate([dvg_ref[...], dvb_ref[...]], axis=-1)
    a_ref[ROW_LOSS:ROW_LOSS + 1, 0:128] = loss_ref[...]
    row = lax.broadcasted_iota(jnp.int32, (CHUNK, CHUNK), 0)
    col = lax.broadcasted_iota(jnp.int32, (CHUNK, CHUNK), 1)
    for g in range(A_GROUPS):
        b_ref[ROW_WS + g * CHUNK:ROW_WS + (g + 1) * CHUNK, :] = jnp.where(row >= col, dws_ref[g], 0.0)
        by_token = jnp.transpose(dbs_ref[:, g * 128:(g + 1) * 128])
        b_ref[ROW_BS + g:ROW_BS + g + 1, :] = jnp.sum(by_token, axis=0, keepdims=True)
    b_ref[ROW_SINK:ROW_SINK + 1, :] = dsink_ref[...]
    bk = bk_ref[...]
    rel_row = lax.broadcasted_iota(jnp.int32, (8, 128), 0)
    rel_col = lax.broadcasted_iota(jnp.int32, (8, 128), 1)
    rel = jnp.zeros((8, 128), F32)
    for h in range(4):
        acc = drel_ref[h * CHUNK:(h + 1) * CHUNK, :]
        for b in range(N_BUCKETS):
            rel = jnp.where((rel_row == h) & (rel_col == b), jnp.sum(jnp.where(bk == b, acc, 0.0)), rel)
    b_ref[ROW_REL:ROW_REL + 8, :] = rel


def _backward_projection(x2, dout, dproj, g_pre, w_in_t, small_parts):
    n_tok = x2.shape[0]
    n_steps = n_tok // BWD_PROJ_TILE
    n_small = len(small_parts)

    def body(x_ref, dout_ref, dp_ref, g_ref, w_hbm, *refs):
        small_refs, (dx_ref, a_ref, b_ref, w_vmem, dgpre, sem) = refs[:n_small], refs[n_small:]
        step = pl.program_id(0)

        @pl.when(step == 0)
        def _():
            load = pltpu.make_async_copy(w_hbm, w_vmem, sem)
            load.start()
            dgpre[...] = jnp.zeros_like(dgpre)
            load.wait()

        xv = x_ref[...]
        r = lax.rsqrt(jnp.mean(xv * xv, axis=-1, keepdims=True) + EPS)
        xn = xv * r
        dh = _mm(dp_ref[...], w_vmem[...])
        dgpre[...] += jnp.sum(dh * xn, axis=0, keepdims=True)
        dhg = dh * g_ref[...]
        dx_ref[...] = r * (dhg - xn * jnp.mean(dhg * xn, axis=-1, keepdims=True)) + dout_ref[...]

        @pl.when(step == n_steps - 1)
        def _():
            _fill_small_grads(dgpre, *small_refs, a_ref, b_ref)

    row = lambda w: pl.BlockSpec((BWD_PROJ_TILE, w), lambda i: (i, 0))
    return pl.pallas_call(
        body, name="backward_projection", grid=(n_steps,),
        out_shape=[jax.ShapeDtypeStruct((n_tok, D_MODEL), F32), jax.ShapeDtypeStruct((SMALL_A_ROWS, D_MODEL), F32),
                   jax.ShapeDtypeStruct((SMALL_B_ROWS, 128), F32)],
        in_specs=[row(D_MODEL), row(D_MODEL), row(IN_WIDTH), _full_spec((1, D_MODEL)), ANY_SPEC]
        + [_full_spec(a.shape) for a in small_parts],
        out_specs=[row(D_MODEL), _full_spec((SMALL_A_ROWS, D_MODEL)), _full_spec((SMALL_B_ROWS, 128))],
        scratch_shapes=[pltpu.VMEM((IN_WIDTH, D_MODEL), BF16), pltpu.VMEM((1, D_MODEL), F32),
                        pltpu.SemaphoreType.DMA],
        input_output_aliases={1: 0},
        compiler_params=pltpu.CompilerParams(vmem_limit_bytes=VMEM_LIMIT),
    )(x2, dout, dproj, g_pre, w_in_t, *small_parts)


SHARD_ROWS = IN_WIDTH // N_CHIPS
SHARD_WINDOW = 768
SHARD_HALF = SHARD_ROWS // 2
DWIN_TILE = 2048
N_REL = N_CHIPS - 1


def _shard_window_start(shard):
    return (shard * SHARD_ROWS // 128) * 128


def _reduce_gradients(dproj, h, big, small, shard_arr):
    n_tok = h.shape[0]
    tile = min(DWIN_TILE, n_tok)
    n_sub = n_tok // tile
    last = N_CHIPS - 1
    n_big, n_small = len(big), len(small)
    big_half = [g.shape[2:] for g in big]
    sem_big_d2d = 2 * N_CHIPS
    sem_big_ici = sem_big_d2d + n_big
    sem_big_swap = sem_big_ici + N_REL * n_big
    sem_small_d2d = sem_big_swap + n_big
    sem_small_ici = sem_small_d2d + n_small
    n_sems = sem_small_ici + N_REL * n_small
    loc_small = n_big
    loc_out_win = loc_small + n_small
    loc_out_big = loc_out_win + 2
    loc_out_small = loc_out_big + 2 * n_big
    n_local = loc_out_small + n_small

    def relation_of_slot(s):
        return (s + 2) % N_REL + 1

    def shard_of_slot(s, my_shard):
        return my_shard ^ jnp.where(s == last, 0, relation_of_slot(s))

    def body(shard_ref, dp_ref, h_hbm, *refs):
        h_vmem, h_sem, refs = refs[-2], refs[-1], refs[:-2]
        big_hbm, refs = refs[:n_big], refs[n_big:]
        small_hbm, refs = refs[:n_small], refs[n_small:]
        out_hbm, refs = refs[0], refs[1:]
        big_out, refs = refs[:n_big], refs[n_big:]
        small_out, refs = refs[:n_small], refs[n_small:]
        part, recv_d2d, send_ici, recv_ici, mine_buf, other_buf = refs[:6]
        refs = refs[6:]
        big_own, big_recv, big_send, big_land, big_mine, big_other = (
            refs[k * n_big:(k + 1) * n_big] for k in range(6))
        refs = refs[6 * n_big:]
        small_own, small_recv, small_all = (refs[k * n_small:(k + 1) * n_small] for k in range(3))
        send_sems, recv_sems, local_sems = refs[3 * n_small:]

        s, t = pl.program_id(0), pl.program_id(1)
        x, y, c = lax.axis_index("x"), lax.axis_index("y"), lax.axis_index("c")
        my_chip = 2 * x + y
        sibling = (x, y, 1 - c)
        my_rows = pl.ds(pl.multiple_of(c * SHARD_HALF, 8), SHARD_HALF)
        other_rows = pl.ds(pl.multiple_of((1 - c) * SHARD_HALF, 8), SHARD_HALF)

        def remote(src, dst, k, to):
            return pltpu.make_async_remote_copy(src_ref=src, dst_ref=dst, send_sem=send_sems.at[k],
                                                recv_sem=recv_sems.at[k], device_id=to, device_id_type=MESH)

        def chip_at(rel):
            return (x ^ (rel >> 1), y ^ (rel & 1), c)

        def to_sibling(k):
            return remote(part.at[k % 2, other_rows, :], recv_d2d.at[k], k, sibling)

        def to_chip(k):
            return remote(send_ici.at[k], recv_ici.at[k], N_CHIPS + k, chip_at(relation_of_slot(k)))

        swap = remote(mine_buf, other_buf, 2 * N_CHIPS - 1, sibling)
        big_load = [pltpu.make_async_copy(big_hbm[w].at[:, pl.ds(c, 1)], big_own[w], local_sems.at[w])
                    for w in range(n_big)]
        big_to_sibling = [remote(big_hbm[w].at[:, pl.ds(1 - c, 1)], big_recv[w], sem_big_d2d + w, sibling)
                          for w in range(n_big)]
        big_to_chip = [[remote(big_send[w].at[k], big_land[w].at[k], sem_big_ici + N_REL * w + k, chip_at(k + 1))
                        for k in range(N_REL)] for w in range(n_big)]
        big_swap = [remote(big_mine[w], big_other[w], sem_big_swap + w, sibling) for w in range(n_big)]
        small_load = [pltpu.make_async_copy(small_hbm[i], small_own[i], local_sems.at[loc_small + i])
                      for i in range(n_small)]
        small_to_sibling = [remote(small_hbm[i], small_recv[i], sem_small_d2d + i, sibling) for i in range(n_small)]
        small_to_chip = [[remote(small_all[i].at[my_chip], small_all[i].at[my_chip],
                                 sem_small_ici + N_REL * i + k, chip_at(k + 1))
                          for k in range(N_REL)] for i in range(n_small)]

        @pl.when((s == 0) & (t == 0))
        def _():
            h_load = pltpu.make_async_copy(h_hbm, h_vmem, h_sem)
            h_load.start()
            for cp in big_load + big_to_sibling + small_load + small_to_sibling:
                cp.start()
            h_load.wait()

        @pl.when((s == 0) & (t == n_sub - 1))
        def _():
            for cp in big_load + small_load:
                cp.wait()
            for cp in big_to_sibling + small_to_sibling:
                cp.wait_recv()
                cp.wait_send()
            for w in range(n_big):
                for k in range(N_REL):
                    shard = my_chip ^ (k + 1)
                    big_send[w][k] = (big_own[w][shard, 0] + big_recv[w][shard, 0]).astype(BF16)
                    big_to_chip[w][k].start()
            for i in range(n_small):
                small_all[i][my_chip] = small_own[i][...] + small_recv[i][...]
                for k in range(N_REL):
                    small_to_chip[i][k].start()

        @pl.when((s > 0) & (t == jnp.where(s == last, 0, min(1, n_sub - 1))))
        def _():
            k = s - 1
            cp = to_sibling(k)
            cp.wait_recv()
            cp.wait_send()
            send_ici[k] = (part[k % 2, my_rows, :] + recv_d2d[k]).astype(BF16)
            to_chip(k).start()

        def big_rows(w, half):
            rows = big_half[w][0]
            return big_out[w].at[pl.ds(pl.multiple_of(half * rows, 8), rows), :]

        big_store_mine = [pltpu.make_async_copy(big_mine[w], big_rows(w, c), local_sems.at[loc_out_big + 2 * w])
                          for w in range(n_big)]
        big_store_other = [pltpu.make_async_copy(big_other[w], big_rows(w, 1 - c),
                                                 local_sems.at[loc_out_big + 2 * w + 1]) for w in range(n_big)]
        small_store = [pltpu.make_async_copy(small_all[i], small_out[i], local_sems.at[loc_out_small + i])
                       for i in range(n_small)]

        @pl.when((s == last) & (t == 0))
        def _():
            for w in range(n_big):
                total = big_own[w][my_chip, 0] + big_recv[w][my_chip, 0]
                for k in range(N_REL):
                    big_to_chip[w][k].wait_recv()
                    total = total + big_land[w][k].astype(F32)
                big_mine[w][...] = total
                big_swap[w].start()
                big_store_mine[w].start()
            for i in range(n_small):
                for k in range(N_REL):
                    small_to_chip[i][k].wait_recv()
                small_store[i].start()

        r = _mm_tn(dp_ref[...], h_vmem[pl.ds(pl.multiple_of(t * tile, tile), tile), :])
        odd = shard_of_slot(s, shard_ref[0]) % 2
        for parity in range(2):
            rows = r[64 * parity:64 * parity + SHARD_ROWS]

            @pl.when((odd == parity) & (t == 0))
            def _():
                part[s % 2] = rows

            @pl.when((odd == parity) & (t > 0))
            def _():
                part[s % 2] += rows

        @pl.when(t == n_sub - 1)
        def _():
            to_sibling(s).start()

        @pl.when((s == last) & (t == n_sub - 1))
        def _():
            cp = to_sibling(last)
            cp.wait_recv()
            cp.wait_send()
            total = part[last % 2, my_rows, :] + recv_d2d[last]
            for k in range(last):
                to_chip(k).wait_recv()
                total = total + recv_ici[k].astype(F32)
            mine_buf[...] = total
            swap.start()
            out_mine = pltpu.make_async_copy(mine_buf, out_hbm.at[my_rows, :], local_sems.at[0])
            out_mine.start()
            swap.wait_recv()
            out_other = pltpu.make_async_copy(other_buf, out_hbm.at[other_rows, :], local_sems.at[1])
            out_other.start()
            for w in range(n_big):
                big_swap[w].wait_recv()
                big_store_other[w].start()
            stores = [out_mine, out_other] + big_store_mine + big_store_other + small_store
            for k in range(last):
                to_chip(k).wait_send()
            swap.wait_send()
            for w in range(n_big):
                for k in range(N_REL):
                    big_to_chip[w][k].wait_send()
                big_swap[w].wait_send()
            for i in range(n_small):
                for k in range(N_REL):
                    small_to_chip[i][k].wait_send()
            for cp in stores:
                cp.wait()

    half = (SHARD_HALF, D_MODEL)
    vmem = pltpu.VMEM
    scratch = [vmem((2, SHARD_ROWS, D_MODEL), F32), vmem((N_CHIPS,) + half, F32),
               vmem((N_REL,) + half, BF16), vmem((N_REL,) + half, BF16), vmem(half, F32), vmem(half, F32)]
    scratch += [vmem((N_CHIPS, 1) + hs, F32) for hs in big_half] * 2
    scratch += [vmem((N_REL,) + hs, BF16) for hs in big_half] * 2
    scratch += [vmem(hs, F32) for hs in big_half] * 2
    scratch += [vmem(a.shape, F32) for a in small] * 2 + [vmem((N_CHIPS,) + a.shape, F32) for a in small]
    scratch += [pltpu.SemaphoreType.DMA((n_sems,)), pltpu.SemaphoreType.DMA((n_sems,)),
                pltpu.SemaphoreType.DMA((n_local,)), vmem(h.shape, BF16), pltpu.SemaphoreType.DMA]
    n_hbm = n_big + n_small
    out = pl.pallas_call(
        body, name="reduce_gradients",
        out_shape=[jax.ShapeDtypeStruct((SHARD_ROWS, D_MODEL), F32)]
        + [jax.ShapeDtypeStruct((2 * hs[0], hs[1]), F32) for hs in big_half]
        + [jax.ShapeDtypeStruct((N_CHIPS,) + a.shape, F32) for a in small],
        grid_spec=pltpu.PrefetchScalarGridSpec(
            num_scalar_prefetch=1, grid=(N_CHIPS, n_sub),
            in_specs=[pl.BlockSpec((pl.Element(tile), pl.Element(SHARD_WINDOW)),
                                   lambda s, t, m: (t * tile, _shard_window_start(shard_of_slot(s, m[0])))),
                      ANY_SPEC] + [ANY_SPEC] * n_hbm,
            out_specs=[ANY_SPEC] * (1 + n_hbm),
            scratch_shapes=scratch),
        compiler_params=pltpu.CompilerParams(vmem_limit_bytes=VMEM_LIMIT),
    )(shard_arr, dproj, h, *big, *small)
    return out[:1 + n_big], out[1 + n_big:]


def _adamw(w, g, m, v):
    m2 = ADAM_B1 * m + (1.0 - ADAM_B1) * g
    v2 = ADAM_B2 * v + (1.0 - ADAM_B2) * (g * g)
    m_hat = m2 / (1.0 - ADAM_B1 ** ADAM_STEP)
    v_hat = v2 / (1.0 - ADAM_B2 ** ADAM_STEP)
    delta = -ADAM_LR * (m_hat / (jnp.sqrt(v_hat) + ADAM_EPS) + ADAM_WD * w)
    return delta, m2, v2


ADAM_STEPS = 4


def _adamw_all(shard_grads, shard_w, shard_m, shard_v, ra, rb, small_w, small_m, small_v):
    n_sh, n = len(shard_w), len(small_w)

    def body(*refs):
        sh_in, refs = refs[:4 * n_sh], refs[4 * n_sh:]
        ra_ref, rb_ref, refs = refs[0], refs[1], refs[2:]
        w_refs, m_refs, v_refs, refs = refs[:n], refs[n:2 * n], refs[2 * n:3 * n], refs[3 * n:]
        sh_out, outs = refs[:4 * n_sh], refs[4 * n_sh:]
        for k in range(n_sh):
            g = sh_in[k][...]
            delta, m2, v2 = _adamw(sh_in[n_sh + k][...], g, sh_in[2 * n_sh + k][...], sh_in[3 * n_sh + k][...])
            for ref, val in zip(sh_out[4 * k:4 * k + 4], (g, delta, m2, v2)):
                ref[...] = val

        @pl.when(pl.program_id(0) == 0)
        def _():
            g_outs, d_outs, m_outs, v_outs = outs[:n], outs[n:2 * n], outs[2 * n:3 * n], outs[3 * n:4 * n]
            ga, gb = ra_ref[0], rb_ref[0]
            for chip in range(1, N_CHIPS):
                ga = ga + ra_ref[chip]
                gb = gb + rb_ref[chip]
            outs[4 * n][...] = ga[ROW_LOSS:ROW_LOSS + 1, 0:128]
            grads = [ga[0:1, :], ga[1:2, :], ga[2:3, :], ga[3:4, :A_WIDTH], ga[3:4, A_WIDTH:],
                     gb[ROW_WS:ROW_WS + A_GROUPS * CHUNK, :].reshape(A_GROUPS, CHUNK, CHUNK),
                     gb[ROW_BS:ROW_BS + A_GROUPS, :], gb[ROW_SINK:ROW_SINK + 1, 0:4],
                     gb[ROW_REL:ROW_REL + 4, 0:N_BUCKETS]]
            for k in range(n):
                delta, m2, v2 = _adamw(w_refs[k][...], grads[k], m_refs[k][...], v_refs[k][...])
                g_outs[k][...] = grads[k]
                d_outs[k][...] = delta
                m_outs[k][...] = m2
                v_outs[k][...] = v2

    def rows_block(a):
        assert a.shape[0] % (8 * ADAM_STEPS) == 0
        return pl.BlockSpec((a.shape[0] // ADAM_STEPS, a.shape[1]), lambda i: (i, 0))

    sh_specs = [rows_block(w) for w in shard_w]
    small_in = [ra, rb, *small_w, *small_m, *small_v]
    small_out_shapes = [jax.ShapeDtypeStruct(w.shape, F32) for w in small_w] * 4 + [jax.ShapeDtypeStruct((1, 128), F32)]
    out = pl.pallas_call(
        body, name="adamw_all", grid=(ADAM_STEPS,),
        out_shape=[jax.ShapeDtypeStruct(w.shape, F32) for w in shard_w for _ in range(4)] + small_out_shapes,
        in_specs=sh_specs * 4 + [_full_spec(a.shape) for a in small_in],
        out_specs=[spec for spec in sh_specs for _ in range(4)] + [_full_spec(s.shape) for s in small_out_shapes],
        compiler_params=pltpu.CompilerParams(vmem_limit_bytes=VMEM_LIMIT),
    )(*shard_grads, *shard_w, *shard_m, *shard_v, *small_in)
    return [out[4 * k:4 * k + 4] for k in range(n_sh)], out[4 * n_sh:]


def kernel(x, mem, pre_norm_g, post_norm_g, mem_norm_g, w_in, w_mem_kv, v_norm_g, v_norm_b, w_spatial, b_spatial, attn_sinks, rel_bias, w_out, loss_target, m_pre_norm_g, m_post_norm_g, m_mem_norm_g, m_w_in, m_w_mem_kv, m_v_norm_g, m_v_norm_b, m_w_spatial, m_b_spatial, m_attn_sinks, m_rel_bias, m_w_out, v_pre_norm_g, v_post_norm_g, v_mem_norm_g, v_w_in, v_w_mem_kv, v_v_norm_g, v_v_norm_b, v_w_spatial, v_b_spatial, v_attn_sinks, v_rel_bias, v_w_out):
    n_ex, seq, _ = x.shape
    n_tok = n_ex * seq
    x2 = x.reshape(n_tok, D_MODEL)
    tgt2 = loss_target.reshape(n_tok, D_MODEL)
    buckets = jnp.asarray(_bucket_map())
    shard_arr = (2 * lax.axis_index("x") + lax.axis_index("y")).astype(jnp.int32).reshape(1)
    w_sp = w_spatial[0]
    w_in_t, m_w_in_t, v_w_in_t = (jnp.transpose(a[0]) for a in (w_in, m_w_in, v_w_in))
    rel_t, m_rel_t, v_rel_t = (jnp.transpose(a) for a in (rel_bias, m_rel_bias, v_rel_bias))

    x_arr = lax.axis_index("x").astype(jnp.int32).reshape(1)
    h_b, parts, (w_in_b, g_mkv, g_out), bias, b_sp = _gather_and_project(
        x2, pre_norm_g, w_in_t, w_mem_kv[0], w_out[0], rel_t, buckets, b_spatial[0], x_arr)
    w_mkv_b = g_mkv.reshape(D_MODEL, 2 * MEM_WIDTH)
    w_out_b = g_out.reshape(MIX_WIDTH, D_MODEL)

    dout, dproj, dwmkv, dgmem, dwout, dvg, dvb, dws, dbs, dsink, drel, loss_vec, dgpost = _mix(
        parts, mem, x2, tgt2, v_norm_g, v_norm_b, w_sp, b_sp, attn_sinks, bias, w_out_b, post_norm_g, mem_norm_g,
        w_mkv_b, n_ex, seq)

    dx, small_a, small_b = _backward_projection(
        x2, dout, dproj, pre_norm_g, w_in_b, [dgpost, dgmem, dvg, dvb, dws, dbs, dsink, drel, loss_vec, buckets])

    shard_shapes = [w_mem_kv.shape[1:], w_out.shape[1:]]
    big = [g.reshape(N_CHIPS, 2, s[0] // 2, s[1]) for g, s in zip((dwmkv, dwout), shard_shapes)]
    (g_win, g_wmkv, g_wout), (ga, gb) = _reduce_gradients(dproj, h_b, big, [small_a, small_b], shard_arr)

    small_w = [pre_norm_g, post_norm_g, mem_norm_g, v_norm_g, v_norm_b, w_sp, b_spatial[0], attn_sinks, rel_t]
    small_m = [m_pre_norm_g, m_post_norm_g, m_mem_norm_g, m_v_norm_g, m_v_norm_b, m_w_spatial[0], m_b_spatial[0],
               m_attn_sinks, m_rel_t]
    small_v = [v_pre_norm_g, v_post_norm_g, v_mem_norm_g, v_v_norm_g, v_v_norm_b, v_w_spatial[0], v_b_spatial[0],
               v_attn_sinks, v_rel_t]
    big_out, small_out = _adamw_all(
        [g_win, g_wmkv, g_wout], [w_in_t, w_mem_kv[0], w_out[0]], [m_w_in_t, m_w_mem_kv[0], m_w_out[0]],
        [v_w_in_t, v_w_mem_kv[0], v_w_out[0]], ga, gb, small_w, small_m, small_v)
    n_small = len(small_w)

    outputs = [small_out[4 * n_small][0, 0], dx.reshape(x.shape)]
    for kind in range(4):
        s = small_out[kind * n_small:(kind + 1) * n_small]
        outputs += [s[0], s[1], s[2], jnp.transpose(big_out[0][kind])[None], big_out[1][kind][None], s[3], s[4],
                    s[5][None], s[6][None], s[7], jnp.transpose(s[8]), big_out[2][kind][None]]
    return tuple(outputs)
```

```python
import functools

import numpy as np
import jax
import jax.numpy as jnp
from jax import lax
from jax.experimental import pallas as pl
from jax.experimental.pallas import tpu as pltpu

F32 = jnp.float32
BF16 = jnp.bfloat16
MESH = pl.DeviceIdType.MESH

D_MODEL = 1024
CHUNK = 128
A_WIDTH = 512
A_GROUPS = 4
SWA_WIDTH = 256
KV_WIDTH = 128
MEM_WIDTH = 256
MEM_LEN = 256
MIX_WIDTH = 1024
IN_WIDTH = 2816
N_BUCKETS = 32
MAX_DISTANCE = 128
EPS = 1e-6
NEG = -1e30
QK_SCALE = 0.125
HALF_HEAD_PAIR = 64

ADAM_LR = 0.001
ADAM_B1 = 0.9
ADAM_B2 = 0.999
ADAM_EPS = 1e-08
ADAM_WD = 0.01
ADAM_STEP = 10

N_CHIPS = 4
TILE_CHUNKS = 2
TILE = TILE_CHUNKS * CHUNK
PROJ_TILE = 512
VMEM_LIMIT = 56 * 1024 * 1024

SMALL_A_ROWS = 8
ROW_LOSS = 4
ROW_WS = 0
ROW_BS = 512
ROW_SINK = 520
ROW_REL = 528
SMALL_B_ROWS = 536


def _mm(a, b):
    return lax.dot_general(a, b, (((1,), (0,)), ((), ())), preferred_element_type=F32)


def _mm_nt(a, b):
    return lax.dot_general(a, b, (((1,), (1,)), ((), ())), preferred_element_type=F32)


def _mm_tn(a, b):
    return lax.dot_general(a, b, (((0,), (0,)), ((), ())), preferred_element_type=F32)


def _bucket_map():
    qi = np.arange(CHUNK)[:, None]
    kj = np.arange(2 * CHUNK)[None, :]
    n = np.maximum(qi + CHUNK - kj, 0)
    max_exact = N_BUCKETS // 2
    large = max_exact + (np.log(np.maximum(n, 1) / max_exact) / np.log(MAX_DISTANCE / max_exact)
                         * (N_BUCKETS - max_exact)).astype(np.int32)
    large = np.minimum(large, N_BUCKETS - 1)
    return np.where(n < max_exact, n, large).astype(np.int32)


_GELU_C = 0.7978845608028654
_GELU_A = 0.044715
_GELU_K1 = 2.0 * _GELU_C
_GELU_K2 = 2.0 * _GELU_C * _GELU_A


def _gelu(x):
    x2 = x * x
    s = 1.0 / (1.0 + jnp.exp(x * (-_GELU_K1 - _GELU_K2 * x2)))
    return x * s, (s, x2)


def _gelu_grad(x, saved):
    s, x2 = saved
    return s + x * (s * (1.0 - s)) * (_GELU_K1 + 3.0 * _GELU_K2 * x2)


def _sigmoid(x):
    return 1.0 / (1.0 + jnp.exp(-x))


def _lane_lo(shape):
    return lax.broadcasted_iota(jnp.int32, shape, 1) < HALF_HEAD_PAIR


def _swa_variants(t):
    lo = _lane_lo(t.shape)
    tr = pltpu.roll(t, HALF_HEAD_PAIR, 1)
    zero = jnp.zeros_like(t)
    return (jnp.where(lo, t, zero).astype(BF16), jnp.where(lo, zero, tr).astype(BF16),
            jnp.where(lo, tr, zero).astype(BF16), jnp.where(lo, zero, t).astype(BF16))


def _swa_unvariants(d0, d1, d2, d3):
    lo = _lane_lo(d0.shape)
    zero = jnp.zeros_like(d0)
    rolled = jnp.where(lo, zero, d1) + jnp.where(lo, d2, zero)
    return jnp.where(lo, d0, zero) + jnp.where(lo, zero, d3) + pltpu.roll(rolled, HALF_HEAD_PAIR, 1)


def _mem_variants(t):
    out = []
    for pair in range(2):
        tp = t[:, pair * 128:(pair + 1) * 128]
        lo = _lane_lo(tp.shape)
        zero = jnp.zeros_like(tp)
        out.append(jnp.where(lo, tp, zero).astype(BF16))
        out.append(jnp.where(lo, zero, tp).astype(BF16))
    return out


def _mem_unvariants(d0, d1, d2, d3):
    lo = _lane_lo(d0.shape)
    return jnp.concatenate([jnp.where(lo, d0, d1), jnp.where(lo, d2, d3)], axis=-1)


def _softmax(logits, sinks):
    m = jnp.max(logits, axis=-1, keepdims=True)
    if sinks is not None:
        m = jnp.maximum(m, sinks)
    p = jnp.exp(logits - m)
    den = jnp.sum(p, axis=-1, keepdims=True)
    if sinks is None:
        return p * (1.0 / den), None
    es = jnp.exp(sinks - m)
    inv = 1.0 / (den + es)
    return p * inv, es * inv


def _band_valid(with_prev):
    qi = lax.broadcasted_iota(jnp.int32, (CHUNK, 2 * CHUNK), 0)
    kj = lax.broadcasted_iota(jnp.int32, (CHUNK, 2 * CHUNK), 1)
    in_cur = (kj >= CHUNK) & (kj - CHUNK <= qi)
    if not with_prev:
        return in_cur
    return in_cur | ((kj < CHUNK) & (kj > qi))


def _causal_weights(ws_ref):
    row = lax.broadcasted_iota(jnp.int32, (CHUNK, CHUNK), 0)
    col = lax.broadcasted_iota(jnp.int32, (CHUNK, CHUNK), 1)
    return [jnp.where(row >= col, ws_ref[g], 0.0).astype(BF16) for g in range(A_GROUPS)]


def _rows_to_lanes(a, n):
    return jnp.concatenate([a[c * CHUNK:(c + 1) * CHUNK] for c in range(n)], axis=1)


def _lanes_to_rows(a, n):
    w = a.shape[1] // n
    return jnp.concatenate([a[:, c * w:(c + 1) * w] for c in range(n)], axis=0)


def _stack_heads(pair01, pair23):
    return jnp.concatenate([pair01[:, :256], pair01[:, 256:], pair23[:, :256], pair23[:, 256:]], axis=0)


def _pair_heads(s, r):
    return (jnp.concatenate([s[0:r], s[r:2 * r]], axis=1), jnp.concatenate([s[2 * r:3 * r], s[3 * r:4 * r]], axis=1))


def _pair_operands(variants):
    return (jnp.concatenate(variants[0:2], axis=0), jnp.concatenate(variants[2:4], axis=0))


def _split_pair_grads(d_pairs):
    return d_pairs[0][:256], d_pairs[0][256:], d_pairs[1][:256], d_pairs[1][256:]


def _halves_bf16(a):
    return (a[:, :128].astype(BF16), a[:, 128:].astype(BF16))


def _group_a_forward(au, av, vg, vb, wm, bs_rows):
    gu, tu = _gelu(au)
    gv, tv = _gelu(av)
    ya, res = [], []
    for g in range(A_GROUPS):
        sl = slice(g * 128, (g + 1) * 128)
        xg = gv[:, sl]
        xc = xg - jnp.mean(xg, axis=-1, keepdims=True)
        rstd = lax.rsqrt(jnp.mean(xc * xc, axis=-1, keepdims=True) + EPS)
        xhat = xc * rstd
        vn = _rows_to_lanes((xhat * vg[:, sl] + vb[:, sl]).astype(BF16), TILE_CHUNKS)
        s = _lanes_to_rows(_mm(wm[g], vn), TILE_CHUNKS) + bs_rows[g]
        ya.append(gu[:, sl] * s)
        res.append((xhat, rstd, vn, s))
    return ya, dict(gu=gu, tu=tu, tv=tv, groups=res)


def _attention_logits(qp, k_pairs):
    return _stack_heads(_mm_nt(qp[0], k_pairs[0]), _mm_nt(qp[1], k_pairs[1]))


def _attention_out(p, v_pairs, r):
    pp = _pair_heads(p.astype(BF16), r)
    return jnp.concatenate([_mm(pp[0], v_pairs[0]), _mm(pp[1], v_pairs[1])], axis=-1), pp


def _attention_dprobs(do_pairs, v_pairs):
    return _stack_heads(_mm_nt(do_pairs[0], v_pairs[0]), _mm_nt(do_pairs[1], v_pairs[1]))


def _softmax_backward(p, dp):
    delta = jnp.sum(p * dp, axis=-1, keepdims=True)
    return p * (dp - delta), delta


def _attention_grads(dl, pp, do_pairs, qp, k_pairs, r):
    dlp = _pair_heads(dl.astype(BF16), r)
    dq = jnp.concatenate([_mm(dlp[0], k_pairs[0]), _mm(dlp[1], k_pairs[1])], axis=-1)
    dk = (_mm_tn(dlp[0], qp[0]), _mm_tn(dlp[1], qp[1]))
    dv = (_mm_tn(pp[0], do_pairs[0]), _mm_tn(pp[1], do_pairs[1]))
    return dq, dk, dv


def _tile_specs(n_tiles_ex, width):
    return pl.BlockSpec((TILE, width), lambda b, i: (b * n_tiles_ex + jnp.minimum(i, n_tiles_ex - 1), 0))


def _prev_chunk_spec(n_tiles_ex, width):
    def index(b, i):
        chunk = TILE_CHUNKS * jnp.minimum(i, n_tiles_ex - 1)
        return (b * n_tiles_ex * TILE_CHUNKS + jnp.maximum(chunk - 1, 0), 0)
    return pl.BlockSpec((CHUNK, width), index)


def _full_spec(shape):
    zeros = (0,) * len(shape)
    return pl.BlockSpec(shape, lambda *_: zeros)


SMEM_SPEC = pl.BlockSpec(memory_space=pltpu.SMEM)
ANY_SPEC = pl.BlockSpec(memory_space=pl.ANY)


def _fill_bias(rel_ref, bk_ref, out_ref):
    bk = bk_ref[...]
    for h in range(4):
        acc = jnp.zeros((CHUNK, 2 * CHUNK), F32)
        for b in range(N_BUCKETS):
            acc = jnp.where(bk == b, rel_ref[h, b], acc)
        for t, with_prev in enumerate((True, False)):
            out_ref[t, h * CHUNK:(h + 1) * CHUNK, :] = jnp.where(_band_valid(with_prev), acc, NEG)


PROJ_WIDTHS = (A_WIDTH, A_WIDTH, SWA_WIDTH, KV_WIDTH, KV_WIDTH, MEM_WIDTH, MIX_WIDTH)
PROJ_OFFSETS = tuple(int(v) for v in np.cumsum((0,) + PROJ_WIDTHS))


MXU_TILE = 256
HALF_WIDTH = IN_WIDTH // 2
PHASE_COLS = (HALF_WIDTH // MXU_TILE * MXU_TILE, IN_WIDTH - HALF_WIDTH // MXU_TILE * MXU_TILE)


def _phase_columns(phase, chip_x):
    if phase == 0:
        return 0 if chip_x == 0 else IN_WIDTH - PHASE_COLS[0]
    return PHASE_COLS[0] if chip_x == 0 else 0


def _phase_parts(phase, chip_x):
    start = _phase_columns(phase, chip_x)
    return [(k, PROJ_OFFSETS[k] - start) for k in range(len(PROJ_WIDTHS))
            if start <= PROJ_OFFSETS[k] and PROJ_OFFSETS[k + 1] <= start + PHASE_COLS[phase]]


def _gather_and_project(x2, g_pre, w_in_s, w_mkv_s, w_out_s, rel_bias_t, buckets, b_spatial, x_arr):
    n_tok = x2.shape[0]
    n_tiles = n_tok // PROJ_TILE
    last = n_tiles - 1
    shapes = [w_in_s.shape, w_mkv_s.shape, w_out_s.shape]
    n_w = len(shapes)

    def body(x_sref, x_ref, g_ref, win_hbm, wmkv_hbm, wout_hbm, rel_ref, bk_ref, bsp_ref, h_ref, *refs):
        part_refs, refs = refs[:len(PROJ_WIDTHS)], refs[len(PROJ_WIDTHS):]
        bias_ref, bs_ref, refs = refs[0], refs[1], refs[2:]
        gin_hbm, gmkv_hbm, gout_hbm, wg, stage_in, stage_mkv, stage_out, own_mkv, own_out, h_all = refs[:10]
        send_sems, recv_sems, local_sems = refs[10:]
        p, t = pl.program_id(0), pl.program_id(1)
        x, y, c = lax.axis_index("x"), lax.axis_index("y"), lax.axis_index("c")
        me, sibling = (x, y, c), (x, y, 1 - c)
        my_shard = 2 * x + y
        gathered = [wg, gmkv_hbm, gout_hbm]

        def half_rows(w, shard, half):
            rows = shapes[w][0] // 2
            if w == 0:
                return wg.at[pl.ds(pl.multiple_of(shard * shapes[0][0] + half * rows, 16), rows), :]
            return gathered[w].at[shard, pl.ds(half * rows, rows), :]

        def first(w, rel):
            src = half_rows(w, my_shard, c) if w == 0 else (own_mkv, own_out)[w - 1].at[
                pl.ds(c * (shapes[w][0] // 2), shapes[w][0] // 2), :]
            k = 3 * w + rel - 1
            return pltpu.make_async_remote_copy(
                src_ref=src, dst_ref=half_rows(w, my_shard, c), send_sem=send_sems.at[k], recv_sem=recv_sems.at[k],
                device_id=(x ^ (rel >> 1), y ^ (rel & 1), c), device_id_type=MESH)

        def landed(w, rel):
            k = 3 * w + rel - 1
            ref = half_rows(w, my_shard ^ rel, c)
            return pltpu.make_async_remote_copy(src_ref=ref, dst_ref=ref, send_sem=send_sems.at[k],
                                                recv_sem=recv_sems.at[k], device_id=me, device_id_type=MESH)

        def passed(w, rel, half, to):
            k = 9 + 3 * w + rel - 1
            ref = half_rows(w, my_shard ^ rel, half)
            return pltpu.make_async_remote_copy(src_ref=ref, dst_ref=ref, send_sem=send_sems.at[k],
                                                recv_sem=recv_sems.at[k], device_id=to, device_id_type=MESH)

        def pass_on(w, rels):
            for rel in rels:
                landed(w, rel).wait_recv()
                passed(w, rel, c, sibling).start()
            for rel in rels:
                passed(w, rel, 1 - c, me).wait_recv()

        own_stores = [pltpu.make_async_copy(own_mkv, gmkv_hbm.at[my_shard], local_sems.at[3]),
                      pltpu.make_async_copy(own_out, gout_hbm.at[my_shard], local_sems.at[4])]

        @pl.when((p == 0) & (t == 0))
        def _():
            half_rows_in = shapes[0][0] // 2
            halves = [pl.ds(pl.multiple_of(hc * half_rows_in, 8), half_rows_in) for hc in (c, 1 - c)]
            loads = [pltpu.make_async_copy(win_hbm.at[halves[0], :], stage_in.at[halves[0], :], local_sems.at[0]),
                     pltpu.make_async_copy(wmkv_hbm, stage_mkv, local_sems.at[1]),
                     pltpu.make_async_copy(wout_hbm, stage_out, local_sems.at[2]),
                     pltpu.make_async_copy(win_hbm.at[halves[1], :], stage_in.at[halves[1], :], local_sems.at[6])]
            for cp in (loads[0], loads[3], loads[1], loads[2]):
                cp.start()
            loads[0].wait()
            half_rows(0, my_shard, c)[...] = stage_in[halves[0], :].astype(BF16)
            for rel in (1, 2):
                first(0, rel).start()
            loads[3].wait()
            half_rows(0, my_shard, 1 - c)[...] = stage_in[halves[1], :].astype(BF16)
            loads[1].wait()
            loads[2].wait()
            own_mkv[...] = stage_mkv[...].astype(BF16)
            own_out[...] = stage_out[...].astype(BF16)
            for cp in own_stores:
                cp.start()
            _fill_bias(rel_ref, bk_ref, bias_ref)
            for g in range(A_GROUPS):
                bs_ref[g] = jnp.transpose(jnp.broadcast_to(bsp_ref[g:g + 1, :], (CHUNK, CHUNK)))
            pass_on(0, (1,))
            first(0, 3).start()

        @pl.when((p == 0) & (t == n_tiles // 2))
        def _():
            for w in (1, 2):
                for rel in (1, 2, 3):
                    first(w, rel).start()

        store = pltpu.make_async_copy(wg, gin_hbm, local_sems.at[5])

        @pl.when((p == 1) & (t == 0))
        def _():
            pass_on(0, (2, 3))
            store.start()

        @pl.when((p == 1) & (t == n_tiles // 2))
        def _():
            for w in (1, 2):
                pass_on(w, (1, 2, 3))

        tile_rows = pl.ds(pl.multiple_of(t * PROJ_TILE, PROJ_TILE), PROJ_TILE)

        def project(h, phase):
            start = jnp.where(x_sref[0] == 0, _phase_columns(phase, 0), _phase_columns(phase, 1))
            proj = _mm_nt(h, wg[pl.ds(pl.multiple_of(start, MXU_TILE), PHASE_COLS[phase]), :])
            for chip_x in range(2):
                @pl.when(x_sref[0] == chip_x)
                def _():
                    for k, lo in _phase_parts(phase, chip_x):
                        part_refs[k][...] = proj[:, lo:lo + PROJ_WIDTHS[k]].astype(BF16)

        @pl.when(p == 0)
        def _():
            xv = x_ref[...]
            r = lax.rsqrt(jnp.mean(xv * xv, axis=-1, keepdims=True) + EPS)
            h = (xv * r * g_ref[...]).astype(BF16)
            h_ref[...] = h
            h_all[tile_rows, :] = h
            project(h, 0)

        @pl.when(p == 1)
        def _():
            project(h_all[tile_rows, :], 1)

        @pl.when((p == 1) & (t == last))
        def _():
            for w in range(n_w):
                for rel in (1, 2, 3):
                    first(w, rel).wait_send()
                    passed(w, rel, c, sibling).wait_send()
            for cp in own_stores:
                cp.wait()
            store.wait()

    def written_in(k):
        phase_on = [next(ph for ph in range(2) if k in dict(_phase_parts(ph, chip_x))) for chip_x in range(2)]

        def index(p, t, xs):
            phase = jnp.where(xs[0] == 0, phase_on[0], phase_on[1])
            return (jnp.where(p == phase, t, jnp.where(p < phase, 0, last)), 0)
        return index

    part_specs = [pl.BlockSpec((PROJ_TILE, PROJ_WIDTHS[k]), written_in(k)) for k in range(len(PROJ_WIDTHS))]
    vmem = pltpu.VMEM
    out = pl.pallas_call(
        body, name="gather_and_project",
        out_shape=[jax.ShapeDtypeStruct((n_tok, D_MODEL), BF16)]
        + [jax.ShapeDtypeStruct((n_tok, w), BF16) for w in PROJ_WIDTHS]
        + [jax.ShapeDtypeStruct((2, 4 * CHUNK, 2 * CHUNK), F32), jax.ShapeDtypeStruct((A_GROUPS, CHUNK, CHUNK), F32)]
        + [jax.ShapeDtypeStruct((N_CHIPS * shapes[0][0], shapes[0][1]), BF16)]
        + [jax.ShapeDtypeStruct((N_CHIPS,) + s, BF16) for s in shapes[1:]],
        grid_spec=pltpu.PrefetchScalarGridSpec(
            num_scalar_prefetch=1, grid=(2, n_tiles),
            in_specs=[pl.BlockSpec((PROJ_TILE, D_MODEL), lambda p, t, xs: (jnp.where(p == 0, t, last), 0)),
                      pl.BlockSpec((1, D_MODEL), lambda p, t, xs: (0, 0)), ANY_SPEC, ANY_SPEC, ANY_SPEC, SMEM_SPEC,
                      pl.BlockSpec(buckets.shape, lambda p, t, xs: (0, 0)),
                      pl.BlockSpec(b_spatial.shape, lambda p, t, xs: (0, 0))],
            out_specs=[pl.BlockSpec((PROJ_TILE, D_MODEL), lambda p, t, xs: (jnp.where(p == 0, t, last), 0))]
            + part_specs + [pl.BlockSpec((2, 4 * CHUNK, 2 * CHUNK), lambda p, t, xs: (0, 0, 0)),
                            pl.BlockSpec((A_GROUPS, CHUNK, CHUNK), lambda p, t, xs: (0, 0, 0))] + [ANY_SPEC] * 3,
            scratch_shapes=[vmem((N_CHIPS * shapes[0][0], shapes[0][1]), BF16), vmem(shapes[0], F32),
                            vmem(shapes[1], F32), vmem(shapes[2], F32), vmem(shapes[1], BF16), vmem(shapes[2], BF16),
                            vmem((n_tok, D_MODEL), BF16),
                            pltpu.SemaphoreType.DMA((18,)), pltpu.SemaphoreType.DMA((18,)),
                            pltpu.SemaphoreType.DMA((7,))]),
        compiler_params=pltpu.CompilerParams(vmem_limit_bytes=VMEM_LIMIT),
    )(x_arr, x2, g_pre, w_in_s, w_mkv_s, w_out_s, rel_bias_t, buckets, b_spatial)
    n_parts = len(PROJ_WIDTHS)
    return out[0], list(out[1:1 + n_parts]), out[3 + n_parts:], out[1 + n_parts], out[2 + n_parts]


def _load_chunk(j, i, sk_ref, sv_ref, skp_ref, svp_ref):
    rows = slice(j * CHUNK, (j + 1) * CHUNK)
    if j == 0:
        k_prev, v_prev, table = skp_ref[...], svp_ref[...], jnp.where(i > 0, 0, 1)
    else:
        prev = slice((j - 1) * CHUNK, j * CHUNK)
        k_prev, v_prev, table = sk_ref[prev, :], sv_ref[prev, :], 0
    k_pairs = _pair_operands(_swa_variants(jnp.concatenate([k_prev, sk_ref[rows, :]], axis=0).astype(F32)))
    v_pairs = _pair_operands(_swa_variants(jnp.concatenate([v_prev, sv_ref[rows, :]], axis=0).astype(F32)))
    return rows, k_pairs, v_pairs, table


def _tile_constants(ws_ref, bs_ref, sink_ref, mkv_v):
    wm = _causal_weights(ws_ref)
    bs_rows = [jnp.concatenate([bs_ref[g]] * TILE_CHUNKS, axis=0) for g in range(A_GROUPS)]
    sink_col = jnp.max(jnp.concatenate([jnp.full((CHUNK, 128), sink_ref[0, h], F32) for h in range(4)] * TILE_CHUNKS,
                                       axis=0), axis=-1, keepdims=True)
    mk_pairs = _pair_operands(_mem_variants(mkv_v[:, :MEM_WIDTH]))
    mv_pairs = _pair_operands(_mem_variants(mkv_v[:, MEM_WIDTH:]))
    return wm, bs_rows, sink_col, mk_pairs, mv_pairs


def _mix(parts, mem, x2, tgt2, v_g, v_b, w_sp, b_sp, sinks, bias, w_out, g_post, g_mem, w_mkv, n_ex, seq):
    n_tiles_ex = seq // TILE
    n_tok = n_ex * seq
    au, av, sq, sk, sv, mq, z = parts
    col = dict(zip(("au", "av", "sq", "sk", "sv", "mq", "z"),
                   (slice(PROJ_OFFSETS[k], PROJ_OFFSETS[k + 1]) for k in range(len(PROJ_WIDTHS)))))
    before_kv, after_kv = slice(0, col["sk"].start), slice(col["sv"].stop, IN_WIDTH)

    def body(au_ref, av_ref, sq_ref, sk_ref, sv_ref, skp_ref, svp_ref, mq_ref, z_ref, mem_ref, x_ref, tgt_ref,
             vg_ref, vb_ref, ws_ref, bs_ref, sink_ref, bias_ref, wout_ref, gpost_ref, gmem_ref, wmkv_ref,
             dout_ref, dproj_ref, dwmkv_ref, dgmem_ref, dwout_ref, dvg_ref, dvb_ref, dws_ref, dbs_ref, dsink_ref,
             drel_ref, loss_ref, dgpost_ref, carry_dp, carry_k, carry_v, memn_s, mkv_s, dmkv_s):
        b, i = pl.program_id(0), pl.program_id(1)

        @pl.when((b == 0) & (i == 0))
        def _():
            for ref in (dwmkv_ref, dgmem_ref, dwout_ref, dvg_ref, dvb_ref, dws_ref, dbs_ref, dsink_ref, drel_ref,
                        loss_ref, dgpost_ref):
                ref[...] = jnp.zeros_like(ref)

        def normalized_mem():
            m = mem_ref[0]
            return m * lax.rsqrt(jnp.mean(m * m, axis=-1, keepdims=True) + EPS)

        @pl.when(i == 0)
        def _():
            memn_s[...] = (normalized_mem() * gmem_ref[...]).astype(BF16)
            mkv_s[...] = _mm(memn_s[...], wmkv_ref[...])
            dmkv_s[...] = jnp.zeros_like(dmkv_s)
            carry_k[...] = jnp.zeros_like(carry_k)
            carry_v[...] = jnp.zeros_like(carry_v)

        @pl.when(i > 0)
        def _():
            dproj_ref[:, before_kv] = carry_dp[:, before_kv]
            dproj_ref[:, after_kv] = carry_dp[:, after_kv]

        @pl.when(i < n_tiles_ex)
        def _():
            wm, bs_rows, sink_col, mk_pairs, mv_pairs = _tile_constants(ws_ref, bs_ref, sink_ref, mkv_s[...])
            vg = vg_ref[...]

            au_v, av_v = au_ref[...].astype(F32), av_ref[...].astype(F32)
            ya, res = _group_a_forward(au_v, av_v, vg, vb_ref[...], wm, bs_rows)
            swa, logits, yb = [], [], []
            for j in range(TILE_CHUNKS):
                rows, k_pairs, v_pairs, table = _load_chunk(j, i, sk_ref, sv_ref, skp_ref, svp_ref)
                qp = _halves_bf16(sq_ref[rows, :] * QK_SCALE)
                logits.append(_attention_logits(qp, k_pairs) + bias_ref[table])
                swa.append([rows, k_pairs, v_pairs, qp])
            p_swa, sink_p = _softmax(jnp.concatenate(logits, axis=0), sink_col)
            for j in range(TILE_CHUNKS):
                out, pp = _attention_out(p_swa[j * 4 * CHUNK:(j + 1) * 4 * CHUNK], swa[j][2], CHUNK)
                yb.append(out)
                swa[j].append(pp)
            mqp = _halves_bf16(mq_ref[...] * QK_SCALE)
            pm, _ = _softmax(_attention_logits(mqp, mk_pairs), None)
            yc, ppm = _attention_out(pm, mv_pairs, TILE)
            ycat = jnp.concatenate(ya + [jnp.concatenate(yb, axis=0), yc], axis=-1)

            zv = z_ref[...].astype(F32)
            sig = _sigmoid(zv)
            sz = zv * sig
            y_b = (ycat * sz).astype(BF16)
            o = _mm(y_b, wout_ref[...])
            r2 = lax.rsqrt(jnp.mean(o * o, axis=-1, keepdims=True) + EPS)
            nrm = o * r2
            gp = gpost_ref[...]
            diff = x_ref[...] + nrm * gp - tgt_ref[...]
            loss_ref[...] += jnp.sum(diff * diff) * (0.5 / D_MODEL)
            dout = diff * (1.0 / D_MODEL)
            dout_ref[...] = dout
            dgpost_ref[...] += jnp.sum(dout * nrm, axis=0, keepdims=True)
            dn = dout * gp
            do_b = (r2 * (dn - nrm * jnp.mean(dn * nrm, axis=-1, keepdims=True))).astype(BF16)
            dwout_ref[...] += _mm_tn(y_b, do_b)
            dy = _mm_nt(do_b, wout_ref[...])
            carry_dp[:, col["z"]] = (dy * ycat * (sig * (1.0 + zv * (1.0 - sig)))).astype(BF16)
            dyc = dy * sz

            dgu, dgv = [], []
            for g in range(A_GROUPS):
                sl = slice(g * 128, (g + 1) * 128)
                xhat, rstd, vn, s = res["groups"][g]
                dya = dyc[:, sl]
                dgu.append(dya * s)
                ds = dya * res["gu"][:, sl]
                dbs_ref[:, sl] += sum(ds[c * CHUNK:(c + 1) * CHUNK] for c in range(TILE_CHUNKS))
                ds_b = _rows_to_lanes(ds.astype(BF16), TILE_CHUNKS)
                dws_ref[g] += _mm_nt(ds_b, vn)
                dvn = _lanes_to_rows(_mm_tn(wm[g], ds_b), TILE_CHUNKS)
                dvg_ref[:, sl] += jnp.sum(dvn * xhat, axis=0, keepdims=True)
                dvb_ref[:, sl] += jnp.sum(dvn, axis=0, keepdims=True)
                dxh = dvn * vg[:, sl]
                dgv.append(rstd * (dxh - jnp.mean(dxh, axis=-1, keepdims=True)
                                   - xhat * jnp.mean(dxh * xhat, axis=-1, keepdims=True)))
            carry_dp[:, col["au"]] = (jnp.concatenate(dgu, axis=-1) * _gelu_grad(au_v, res["tu"])).astype(BF16)
            carry_dp[:, col["av"]] = (jnp.concatenate(dgv, axis=-1) * _gelu_grad(av_v, res["tv"])).astype(BF16)

            do_pairs = [_halves_bf16(dyc[rows, A_WIDTH:A_WIDTH + SWA_WIDTH]) for rows, *_ in swa]
            dl_swa, delta = _softmax_backward(p_swa, jnp.concatenate(
                [_attention_dprobs(do_pairs[j], swa[j][2]) for j in range(TILE_CHUNKS)], axis=0))
            sink_terms = sink_p * delta
            lane4 = lax.broadcasted_iota(jnp.int32, (1, 128), 1)
            dsink_vec = jnp.zeros((1, 128), F32)
            for h in range(4):
                head_sum = sum(jnp.sum(sink_terms[(4 * j + h) * CHUNK:(4 * j + h + 1) * CHUNK])
                               for j in range(TILE_CHUNKS))
                dsink_vec = dsink_vec + jnp.where(lane4 == h, -head_sum, 0.0)
            dsink_ref[...] += dsink_vec
            drel_ref[...] += sum(dl_swa[j * 4 * CHUNK:(j + 1) * 4 * CHUNK] for j in range(TILE_CHUNKS))
            dk_parts, dv_parts = [], []
            for j, (rows, k_pairs, v_pairs, qp, pp) in enumerate(swa):
                dq, dk, dv = _attention_grads(dl_swa[j * 4 * CHUNK:(j + 1) * 4 * CHUNK], pp, do_pairs[j], qp, k_pairs,
                                              CHUNK)
                carry_dp[rows, col["sq"]] = (dq * QK_SCALE).astype(BF16)
                dk_parts.append(_swa_unvariants(*_split_pair_grads(dk)))
                dv_parts.append(_swa_unvariants(*_split_pair_grads(dv)))

            dc_pairs = _halves_bf16(dyc[:, A_WIDTH + SWA_WIDTH:])
            dl_mem, _ = _softmax_backward(pm, _attention_dprobs(dc_pairs, mv_pairs))
            dmq, dmk, dmv = _attention_grads(dl_mem, ppm, dc_pairs, mqp, mk_pairs, TILE)
            carry_dp[:, col["mq"]] = (dmq * QK_SCALE).astype(BF16)
            dmkv_s[...] += jnp.concatenate([_mem_unvariants(*_split_pair_grads(dmk)),
                                            _mem_unvariants(*_split_pair_grads(dmv))], axis=-1)

            for parts_c, carry, cols in ((dk_parts, carry_k, col["sk"]), (dv_parts, carry_v, col["sv"])):
                @pl.when(i > 0)
                def _():
                    dproj_ref[:, cols] = (carry[...] + jnp.concatenate(
                        [jnp.zeros((TILE - CHUNK, KV_WIDTH), F32), parts_c[0][:CHUNK]], axis=0)).astype(BF16)
                new = [parts_c[0][CHUNK:]]
                for j in range(1, TILE_CHUNKS):
                    new[-1] = new[-1] + parts_c[j][:CHUNK]
                    new.append(parts_c[j][CHUNK:])
                carry[...] = jnp.concatenate(new, axis=0)

        @pl.when(i == n_tiles_ex)
        def _():
            dproj_ref[:, col["sk"]] = carry_k[...].astype(BF16)
            dproj_ref[:, col["sv"]] = carry_v[...].astype(BF16)
            d_b = dmkv_s[...].astype(BF16)
            dwmkv_ref[...] += _mm_tn(memn_s[...], d_b)
            dgmem_ref[...] += jnp.sum(_mm_nt(d_b, wmkv_ref[...]) * normalized_mem(), axis=0, keepdims=True)

    tile = functools.partial(_tile_specs, n_tiles_ex)
    prev = functools.partial(_prev_chunk_spec, n_tiles_ex)
    late = pl.BlockSpec((TILE, IN_WIDTH), lambda b, i: (b * n_tiles_ex + jnp.maximum(i - 1, 0), 0))
    return pl.pallas_call(
        body, name="mix", grid=(n_ex, n_tiles_ex + 1),
        out_shape=[jax.ShapeDtypeStruct((n_tok, D_MODEL), F32), jax.ShapeDtypeStruct((n_tok, IN_WIDTH), BF16),
                   jax.ShapeDtypeStruct((D_MODEL, 2 * MEM_WIDTH), F32), jax.ShapeDtypeStruct((1, D_MODEL), F32),
                   jax.ShapeDtypeStruct((MIX_WIDTH, D_MODEL), F32), jax.ShapeDtypeStruct((1, A_WIDTH), F32),
                   jax.ShapeDtypeStruct((1, A_WIDTH), F32), jax.ShapeDtypeStruct((A_GROUPS, CHUNK, CHUNK), F32),
                   jax.ShapeDtypeStruct((CHUNK, A_WIDTH), F32), jax.ShapeDtypeStruct((1, 128), F32),
                   jax.ShapeDtypeStruct((4 * CHUNK, 2 * CHUNK), F32), jax.ShapeDtypeStruct((1, 128), F32),
                   jax.ShapeDtypeStruct((1, D_MODEL), F32)],
        in_specs=[tile(A_WIDTH), tile(A_WIDTH), tile(SWA_WIDTH), tile(KV_WIDTH), tile(KV_WIDTH),
                  prev(KV_WIDTH), prev(KV_WIDTH), tile(MEM_WIDTH), tile(MIX_WIDTH),
                  pl.BlockSpec((1, MEM_LEN, D_MODEL), lambda b, i: (b, 0, 0)),
                  tile(D_MODEL), tile(D_MODEL),
                  _full_spec((1, A_WIDTH)), _full_spec((1, A_WIDTH)), _full_spec((A_GROUPS, CHUNK, CHUNK)),
                  _full_spec((A_GROUPS, CHUNK, CHUNK)), SMEM_SPEC, _full_spec((2, 4 * CHUNK, 2 * CHUNK)),
                  _full_spec((MIX_WIDTH, D_MODEL)), _full_spec((1, D_MODEL)), _full_spec((1, D_MODEL)),
                  _full_spec((D_MODEL, 2 * MEM_WIDTH))],
        out_specs=[tile(D_MODEL), late, _full_spec((D_MODEL, 2 * MEM_WIDTH)), _full_spec((1, D_MODEL)),
                   _full_spec((MIX_WIDTH, D_MODEL)), _full_spec((1, A_WIDTH)), _full_spec((1, A_WIDTH)),
                   _full_spec((A_GROUPS, CHUNK, CHUNK)), _full_spec((CHUNK, A_WIDTH)), _full_spec((1, 128)),
                   _full_spec((4 * CHUNK, 2 * CHUNK)), _full_spec((1, 128)), _full_spec((1, D_MODEL))],
        scratch_shapes=[pltpu.VMEM((TILE, IN_WIDTH), BF16), pltpu.VMEM((TILE, KV_WIDTH), F32),
                        pltpu.VMEM((TILE, KV_WIDTH), F32), pltpu.VMEM((MEM_LEN, D_MODEL), BF16),
                        pltpu.VMEM((MEM_LEN, 2 * MEM_WIDTH), F32), pltpu.VMEM((MEM_LEN, 2 * MEM_WIDTH), F32)],
        compiler_params=pltpu.CompilerParams(vmem_limit_bytes=VMEM_LIMIT),
    )(au, av, sq, sk, sv, sk, sv, mq, z, mem, x2, tgt2, v_g, v_b, w_sp, b_sp, sinks, bias, w_out, g_post, g_mem,
      w_mkv)


BWD_PROJ_TILE = 512


def _fill_small_grads(dgpre_ref, dgpost_ref, dgmem_ref, dvg_ref, dvb_ref, dws_ref, dbs_ref, dsink_ref, drel_ref,
                      loss_ref, bk_ref, a_ref, b_ref):
    a_ref[...] = jnp.zeros_like(a_ref)
    b_ref[...] = jnp.zeros_like(b_ref)
    a_ref[0:1, :] = dgpre_ref[...]
    a_ref[1:2, :] = dgpost_ref[...]
    a_ref[2:3, :] = dgmem_ref[...]
    a_ref[3:4, :] = jnp.concatenate([dvg_ref[...], dvb_ref[...]], axis=-1)
    a_ref[ROW_LOSS:ROW_LOSS + 1, 0:128] = loss_ref[...]
    row = lax.broadcasted_iota(jnp.int32, (CHUNK, CHUNK), 0)
    col = lax.broadcasted_iota(jnp.int32, (CHUNK, CHUNK), 1)
    for g in range(A_GROUPS):
        b_ref[ROW_WS + g * CHUNK:ROW_WS + (g + 1) * CHUNK, :] = jnp.where(row >= col, dws_ref[g], 0.0)
        by_token = jnp.transpose(dbs_ref[:, g * 128:(g + 1) * 128])
        b_ref[ROW_BS + g:ROW_BS + g + 1, :] = jnp.sum(by_token, axis=0, keepdims=True)
    b_ref[ROW_SINK:ROW_SINK + 1, :] = dsink_ref[...]
    bk = bk_ref[...]
    rel_row = lax.broadcasted_iota(jnp.int32, (8, 128), 0)
    rel_col = lax.broadcasted_iota(jnp.int32, (8, 128), 1)
    rel = jnp.zeros((8, 128), F32)
    for h in range(4):
        acc = drel_ref[h * CHUNK:(h + 1) * CHUNK, :]
        for b in range(N_BUCKETS):
            rel = jnp.where((rel_row == h) & (rel_col == b), jnp.sum(jnp.where(bk == b, acc, 0.0)), rel)
    b_ref[ROW_REL:ROW_REL + 8, :] = rel


def _backward_projection(x2, dout, dproj, g_pre, w_in_t, small_parts):
    n_tok = x2.shape[0]
    n_steps = n_tok // BWD_PROJ_TILE
    n_small = len(small_parts)

    def body(x_ref, dout_ref, dp_ref, g_ref, w_hbm, *refs):
        small_refs, (dx_ref, a_ref, b_ref, w_vmem, dgpre, sem) = refs[:n_small], refs[n_small:]
        step = pl.program_id(0)

        @pl.when(step == 0)
        def _():
            load = pltpu.make_async_copy(w_hbm, w_vmem, sem)
            load.start()
            dgpre[...] = jnp.zeros_like(dgpre)
            load.wait()

        xv = x_ref[...]
        r = lax.rsqrt(jnp.mean(xv * xv, axis=-1, keepdims=True) + EPS)
        xn = xv * r
        dh = _mm(dp_ref[...], w_vmem[...])
        dgpre[...] += jnp.sum(dh * xn, axis=0, keepdims=True)
        dhg = dh * g_ref[...]
        dx_ref[...] = r * (dhg - xn * jnp.mean(dhg * xn, axis=-1, keepdims=True)) + dout_ref[...]

        @pl.when(step == n_steps - 1)
        def _():
            _fill_small_grads(dgpre, *small_refs, a_ref, b_ref)

    row = lambda w: pl.BlockSpec((BWD_PROJ_TILE, w), lambda i: (i, 0))
    return pl.pallas_call(
        body, name="backward_projection", grid=(n_steps,),
        out_shape=[jax.ShapeDtypeStruct((n_tok, D_MODEL), F32), jax.ShapeDtypeStruct((SMALL_A_ROWS, D_MODEL), F32),
                   jax.ShapeDtypeStruct((SMALL_B_ROWS, 128), F32)],
        in_specs=[row(D_MODEL), row(D_MODEL), row(IN_WIDTH), _full_spec((1, D_MODEL)), ANY_SPEC]
        + [_full_spec(a.shape) for a in small_parts],
        out_specs=[row(D_MODEL), _full_spec((SMALL_A_ROWS, D_MODEL)), _full_spec((SMALL_B_ROWS, 128))],
        scratch_shapes=[pltpu.VMEM((IN_WIDTH, D_MODEL), BF16), pltpu.VMEM((1, D_MODEL), F32),
                        pltpu.SemaphoreType.DMA],
        input_output_aliases={1: 0},
        compiler_params=pltpu.CompilerParams(vmem_limit_bytes=VMEM_LIMIT),
    )(x2, dout, dproj, g_pre, w_in_t, *small_parts)


SHARD_ROWS = IN_WIDTH // N_CHIPS
SHARD_WINDOW = 768
SHARD_HALF = SHARD_ROWS // 2
DWIN_TILE = 2048
N_REL = N_CHIPS - 1


def _shard_window_start(shard):
    return (shard * SHARD_ROWS // 128) * 128


def _reduce_gradients(dproj, h, big, small, shard_arr):
    n_tok = h.shape[0]
    tile = min(DWIN_TILE, n_tok)
    n_sub = n_tok // tile
    last = N_CHIPS - 1
    n_big, n_small = len(big), len(small)
    big_half = [g.shape[2:] for g in big]
    sem_big_d2d = 2 * N_CHIPS
    sem_big_ici = sem_big_d2d + n_big
    sem_big_swap = sem_big_ici + N_REL * n_big
    sem_small_d2d = sem_big_swap + n_big
    sem_small_ici = sem_small_d2d + n_small
    n_sems = sem_small_ici + N_REL * n_small
    loc_small = n_big
    loc_out_win = loc_small + n_small
    loc_out_big = loc_out_win + 2
    loc_out_small = loc_out_big + 2 * n_big
    n_local = loc_out_small + n_small

    def relation_of_slot(s):
        return (s + 2) % N_REL + 1

    def shard_of_slot(s, my_shard):
        return my_shard ^ jnp.where(s == last, 0, relation_of_slot(s))

    def body(shard_ref, dp_ref, h_hbm, *refs):
        h_vmem, h_sem, refs = refs[-2], refs[-1], refs[:-2]
        big_hbm, refs = refs[:n_big], refs[n_big:]
        small_hbm, refs = refs[:n_small], refs[n_small:]
        out_hbm, refs = refs[0], refs[1:]
        big_out, refs = refs[:n_big], refs[n_big:]
        small_out, refs = refs[:n_small], refs[n_small:]
        part, recv_d2d, send_ici, recv_ici, mine_buf, other_buf = refs[:6]
        refs = refs[6:]
        big_own, big_recv, big_send, big_land, big_mine, big_other = (
            refs[k * n_big:(k + 1) * n_big] for k in range(6))
        refs = refs[6 * n_big:]
        small_own, small_recv, small_all = (refs[k * n_small:(k + 1) * n_small] for k in range(3))
        send_sems, recv_sems, local_sems = refs[3 * n_small:]

        s, t = pl.program_id(0), pl.program_id(1)
        x, y, c = lax.axis_index("x"), lax.axis_index("y"), lax.axis_index("c")
        my_chip = 2 * x + y
        sibling = (x, y, 1 - c)
        my_rows = pl.ds(pl.multiple_of(c * SHARD_HALF, 8), SHARD_HALF)
        other_rows = pl.ds(pl.multiple_of((1 - c) * SHARD_HALF, 8), SHARD_HALF)

        def remote(src, dst, k, to):
            return pltpu.make_async_remote_copy(src_ref=src, dst_ref=dst, send_sem=send_sems.at[k],
                                                recv_sem=recv_sems.at[k], device_id=to, device_id_type=MESH)

        def chip_at(rel):
            return (x ^ (rel >> 1), y ^ (rel & 1), c)

        def to_sibling(k):
            return remote(part.at[k % 2, other_rows, :], recv_d2d.at[k], k, sibling)

        def to_chip(k):
            return remote(send_ici.at[k], recv_ici.at[k], N_CHIPS + k, chip_at(relation_of_slot(k)))

        swap = remote(mine_buf, other_buf, 2 * N_CHIPS - 1, sibling)
        big_load = [pltpu.make_async_copy(big_hbm[w].at[:, pl.ds(c, 1)], big_own[w], local_sems.at[w])
                    for w in range(n_big)]
        big_to_sibling = [remote(big_hbm[w].at[:, pl.ds(1 - c, 1)], big_recv[w], sem_big_d2d + w, sibling)
                          for w in range(n_big)]
        big_to_chip = [[remote(big_send[w].at[k], big_land[w].at[k], sem_big_ici + N_REL * w + k, chip_at(k + 1))
                        for k in range(N_REL)] for w in range(n_big)]
        big_swap = [remote(big_mine[w], big_other[w], sem_big_swap + w, sibling) for w in range(n_big)]
        small_load = [pltpu.make_async_copy(small_hbm[i], small_own[i], local_sems.at[loc_small + i])
                      for i in range(n_small)]
        small_to_sibling = [remote(small_hbm[i], small_recv[i], sem_small_d2d + i, sibling) for i in range(n_small)]
        small_to_chip = [[remote(small_all[i].at[my_chip], small_all[i].at[my_chip],
                                 sem_small_ici + N_REL * i + k, chip_at(k + 1))
                          for k in range(N_REL)] for i in range(n_small)]

        h_loads = [pltpu.make_async_copy(h_hbm.at[rows, :], h_vmem.at[rows, :], h_sem.at[k]) for k, rows in enumerate(
            [pl.ds(0, tile)] + ([pl.ds(tile, n_tok - tile)] if n_sub > 1 else []))]

        @pl.when((s == 0) & (t == 0))
        def _():
            for cp in h_loads + big_load + big_to_sibling + small_load + small_to_sibling:
                cp.start()
            h_loads[0].wait()

        if n_sub > 1:
            @pl.when((s == 0) & (t == 1))
            def _():
                h_loads[1].wait()

        @pl.when((s == 0) & (t == n_sub - 1))
        def _():
            for cp in big_load + small_load:
                cp.wait()
            for cp in big_to_sibling + small_to_sibling:
                cp.wait_recv()
                cp.wait_send()
            for w in range(n_big):
                for k in range(N_REL):
                    shard = my_chip ^ (k + 1)
                    big_send[w][k] = (big_own[w][shard, 0] + big_recv[w][shard, 0]).astype(BF16)
                    big_to_chip[w][k].start()
            for i in range(n_small):
                small_all[i][my_chip] = small_own[i][...] + small_recv[i][...]
                for k in range(N_REL):
                    small_to_chip[i][k].start()

        @pl.when((s > 0) & (t == jnp.where(s == last, 0, min(1, n_sub - 1))))
        def _():
            k = s - 1
            cp = to_sibling(k)
            cp.wait_recv()
            cp.wait_send()
            send_ici[k] = (part[k % 2, my_rows, :] + recv_d2d[k]).astype(BF16)
            to_chip(k).start()

        def big_rows(w, half):
            rows = big_half[w][0]
            return big_out[w].at[pl.ds(pl.multiple_of(half * rows, 8), rows), :]

        big_store_mine = [pltpu.make_async_copy(big_mine[w], big_rows(w, c), local_sems.at[loc_out_big + 2 * w])
                          for w in range(n_big)]
        big_store_other = [pltpu.make_async_copy(big_other[w], big_rows(w, 1 - c),
                                                 local_sems.at[loc_out_big + 2 * w + 1]) for w in range(n_big)]
        small_store = [pltpu.make_async_copy(small_all[i], small_out[i], local_sems.at[loc_out_small + i])
                       for i in range(n_small)]

        @pl.when((s == last) & (t == 0))
        def _():
            for w in range(n_big):
                total = big_own[w][my_chip, 0] + big_recv[w][my_chip, 0]
                for k in range(N_REL):
                    big_to_chip[w][k].wait_recv()
                    total = total + big_land[w][k].astype(F32)
                big_mine[w][...] = total
                big_swap[w].start()
                big_store_mine[w].start()
            for i in range(n_small):
                for k in range(N_REL):
                    small_to_chip[i][k].wait_recv()
                small_store[i].start()

        r = _mm_tn(dp_ref[...], h_vmem[pl.ds(pl.multiple_of(t * tile, tile), tile), :])
        odd = shard_of_slot(s, shard_ref[0]) % 2
        for parity in range(2):
            rows = r[64 * parity:64 * parity + SHARD_ROWS]

            @pl.when((odd == parity) & (t == 0))
            def _():
                part[s % 2] = rows

            @pl.when((odd == parity) & (t > 0))
            def _():
                part[s % 2] += rows

        @pl.when(t == n_sub - 1)
        def _():
            to_sibling(s).start()

        @pl.when((s == last) & (t == n_sub - 1))
        def _():
            cp = to_sibling(last)
            cp.wait_recv()
            cp.wait_send()
            total = part[last % 2, my_rows, :] + recv_d2d[last]
            for k in range(last):
                to_chip(k).wait_recv()
                total = total + recv_ici[k].astype(F32)
            mine_buf[...] = total
            swap.start()
            out_mine = pltpu.make_async_copy(mine_buf, out_hbm.at[my_rows, :], local_sems.at[0])
            out_mine.start()
            swap.wait_recv()
            out_other = pltpu.make_async_copy(other_buf, out_hbm.at[other_rows, :], local_sems.at[1])
            out_other.start()
            for w in range(n_big):
                big_swap[w].wait_recv()
                big_store_other[w].start()
            stores = [out_mine, out_other] + big_store_mine + big_store_other + small_store
            for k in range(last):
                to_chip(k).wait_send()
            swap.wait_send()
            for w in range(n_big):
                for k in range(N_REL):
                    big_to_chip[w][k].wait_send()
                big_swap[w].wait_send()
            for i in range(n_small):
                for k in range(N_REL):
                    small_to_chip[i][k].wait_send()
            for cp in stores:
                cp.wait()

    half = (SHARD_HALF, D_MODEL)
    vmem = pltpu.VMEM
    scratch = [vmem((2, SHARD_ROWS, D_MODEL), F32), vmem((N_CHIPS,) + half, F32),
               vmem((N_REL,) + half, BF16), vmem((N_REL,) + half, BF16), vmem(half, F32), vmem(half, F32)]
    scratch += [vmem((N_CHIPS, 1) + hs, F32) for hs in big_half] * 2
    scratch += [vmem((N_REL,) + hs, BF16) for hs in big_half] * 2
    scratch += [vmem(hs, F32) for hs in big_half] * 2
    scratch += [vmem(a.shape, F32) for a in small] * 2 + [vmem((N_CHIPS,) + a.shape, F32) for a in small]
    scratch += [pltpu.SemaphoreType.DMA((n_sems,)), pltpu.SemaphoreType.DMA((n_sems,)),
                pltpu.SemaphoreType.DMA((n_local,)), vmem(h.shape, BF16), pltpu.SemaphoreType.DMA((2,))]
    n_hbm = n_big + n_small
    out = pl.pallas_call(
        body, name="reduce_gradients",
        out_shape=[jax.ShapeDtypeStruct((SHARD_ROWS, D_MODEL), F32)]
        + [jax.ShapeDtypeStruct((2 * hs[0], hs[1]), F32) for hs in big_half]
        + [jax.ShapeDtypeStruct((N_CHIPS,) + a.shape, F32) for a in small],
        grid_spec=pltpu.PrefetchScalarGridSpec(
            num_scalar_prefetch=1, grid=(N_CHIPS, n_sub),
            in_specs=[pl.BlockSpec((pl.Element(tile), pl.Element(SHARD_WINDOW)),
                                   lambda s, t, m: (t * tile, _shard_window_start(shard_of_slot(s, m[0])))),
                      ANY_SPEC] + [ANY_SPEC] * n_hbm,
            out_specs=[ANY_SPEC] * (1 + n_hbm),
            scratch_shapes=scratch),
        compiler_params=pltpu.CompilerParams(vmem_limit_bytes=VMEM_LIMIT),
    )(shard_arr, dproj, h, *big, *small)
    return out[:1 + n_big], out[1 + n_big:]


def _adamw(w, g, m, v):
    m2 = ADAM_B1 * m + (1.0 - ADAM_B1) * g
    v2 = ADAM_B2 * v + (1.0 - ADAM_B2) * (g * g)
    m_hat = m2 / (1.0 - ADAM_B1 ** ADAM_STEP)
    v_hat = v2 / (1.0 - ADAM_B2 ** ADAM_STEP)
    delta = -ADAM_LR * (m_hat / (jnp.sqrt(v_hat) + ADAM_EPS) + ADAM_WD * w)
    return delta, m2, v2


ADAM_STEPS = 4


def _adamw_all(shard_grads, shard_w, shard_m, shard_v, ra, rb, small_w, small_m, small_v):
    n_sh, n = len(shard_w), len(small_w)

    def body(*refs):
        sh_in, refs = refs[:4 * n_sh], refs[4 * n_sh:]
        ra_ref, rb_ref, refs = refs[0], refs[1], refs[2:]
        w_refs, m_refs, v_refs, refs = refs[:n], refs[n:2 * n], refs[2 * n:3 * n], refs[3 * n:]
        sh_out, outs = refs[:4 * n_sh], refs[4 * n_sh:]
        for k in range(n_sh):
            g = sh_in[k][...]
            delta, m2, v2 = _adamw(sh_in[n_sh + k][...], g, sh_in[2 * n_sh + k][...], sh_in[3 * n_sh + k][...])
            for ref, val in zip(sh_out[4 * k:4 * k + 4], (g, delta, m2, v2)):
                ref[...] = val

        @pl.when(pl.program_id(0) == 0)
        def _():
            g_outs, d_outs, m_outs, v_outs = outs[:n], outs[n:2 * n], outs[2 * n:3 * n], outs[3 * n:4 * n]
            ga, gb = ra_ref[0], rb_ref[0]
            for chip in range(1, N_CHIPS):
                ga = ga + ra_ref[chip]
                gb = gb + rb_ref[chip]
            outs[4 * n][...] = ga[ROW_LOSS:ROW_LOSS + 1, 0:128]
            grads = [ga[0:1, :], ga[1:2, :], ga[2:3, :], ga[3:4, :A_WIDTH], ga[3:4, A_WIDTH:],
                     gb[ROW_WS:ROW_WS + A_GROUPS * CHUNK, :].reshape(A_GROUPS, CHUNK, CHUNK),
                     gb[ROW_BS:ROW_BS + A_GROUPS, :], gb[ROW_SINK:ROW_SINK + 1, 0:4],
                     gb[ROW_REL:ROW_REL + 4, 0:N_BUCKETS]]
            for k in range(n):
                delta, m2, v2 = _adamw(w_refs[k][...], grads[k], m_refs[k][...], v_refs[k][...])
                g_outs[k][...] = grads[k]
                d_outs[k][...] = delta
                m_outs[k][...] = m2
                v_outs[k][...] = v2

    def rows_block(a):
        assert a.shape[0] % (8 * ADAM_STEPS) == 0
        return pl.BlockSpec((a.shape[0] // ADAM_STEPS, a.shape[1]), lambda i: (i, 0))

    sh_specs = [rows_block(w) for w in shard_w]
    small_in = [ra, rb, *small_w, *small_m, *small_v]
    small_out_shapes = [jax.ShapeDtypeStruct(w.shape, F32) for w in small_w] * 4 + [jax.ShapeDtypeStruct((1, 128), F32)]
    out = pl.pallas_call(
        body, name="adamw_all", grid=(ADAM_STEPS,),
        out_shape=[jax.ShapeDtypeStruct(w.shape, F32) for w in shard_w for _ in range(4)] + small_out_shapes,
        in_specs=sh_specs * 4 + [_full_spec(a.shape) for a in small_in],
        out_specs=[spec for spec in sh_specs for _ in range(4)] + [_full_spec(s.shape) for s in small_out_shapes],
        compiler_params=pltpu.CompilerParams(vmem_limit_bytes=VMEM_LIMIT),
    )(*shard_grads, *shard_w, *shard_m, *shard_v, *small_in)
    return [out[4 * k:4 * k + 4] for k in range(n_sh)], out[4 * n_sh:]


def kernel(x, mem, pre_norm_g, post_norm_g, mem_norm_g, w_in, w_mem_kv, v_norm_g, v_norm_b, w_spatial, b_spatial, attn_sinks, rel_bias, w_out, loss_target, m_pre_norm_g, m_post_norm_g, m_mem_norm_g, m_w_in, m_w_mem_kv, m_v_norm_g, m_v_norm_b, m_w_spatial, m_b_spatial, m_attn_sinks, m_rel_bias, m_w_out, v_pre_norm_g, v_post_norm_g, v_mem_norm_g, v_w_in, v_w_mem_kv, v_v_norm_g, v_v_norm_b, v_w_spatial, v_b_spatial, v_attn_sinks, v_rel_bias, v_w_out):
    n_ex, seq, _ = x.shape
    n_tok = n_ex * seq
    x2 = x.reshape(n_tok, D_MODEL)
    tgt2 = loss_target.reshape(n_tok, D_MODEL)
    buckets = jnp.asarray(_bucket_map())
    shard_arr = (2 * lax.axis_index("x") + lax.axis_index("y")).astype(jnp.int32).reshape(1)
    w_sp = w_spatial[0]
    w_in_t, m_w_in_t, v_w_in_t = (jnp.transpose(a[0]) for a in (w_in, m_w_in, v_w_in))
    rel_t, m_rel_t, v_rel_t = (jnp.transpose(a) for a in (rel_bias, m_rel_bias, v_rel_bias))

    x_arr = lax.axis_index("x").astype(jnp.int32).reshape(1)
    h_b, parts, (w_in_b, g_mkv, g_out), bias, b_sp = _gather_and_project(
        x2, pre_norm_g, w_in_t, w_mem_kv[0], w_out[0], rel_t, buckets, b_spatial[0], x_arr)
    w_mkv_b = g_mkv.reshape(D_MODEL, 2 * MEM_WIDTH)
    w_out_b = g_out.reshape(MIX_WIDTH, D_MODEL)

    dout, dproj, dwmkv, dgmem, dwout, dvg, dvb, dws, dbs, dsink, drel, loss_vec, dgpost = _mix(
        parts, mem, x2, tgt2, v_norm_g, v_norm_b, w_sp, b_sp, attn_sinks, bias, w_out_b, post_norm_g, mem_norm_g,
        w_mkv_b, n_ex, seq)

    dx, small_a, small_b = _backward_projection(
        x2, dout, dproj, pre_norm_g, w_in_b, [dgpost, dgmem, dvg, dvb, dws, dbs, dsink, drel, loss_vec, buckets])

    shard_shapes = [w_mem_kv.shape[1:], w_out.shape[1:]]
    big = [g.reshape(N_CHIPS, 2, s[0] // 2, s[1]) for g, s in zip((dwmkv, dwout), shard_shapes)]
    (g_win, g_wmkv, g_wout), (ga, gb) = _reduce_gradients(dproj, h_b, big, [small_a, small_b], shard_arr)

    small_w = [pre_norm_g, post_norm_g, mem_norm_g, v_norm_g, v_norm_b, w_sp, b_spatial[0], attn_sinks, rel_t]
    small_m = [m_pre_norm_g, m_post_norm_g, m_mem_norm_g, m_v_norm_g, m_v_norm_b, m_w_spatial[0], m_b_spatial[0],
               m_attn_sinks, m_rel_t]
    small_v = [v_pre_norm_g, v_post_norm_g, v_mem_norm_g, v_v_norm_g, v_v_norm_b, v_w_spatial[0], v_b_spatial[0],
               v_attn_sinks, v_rel_t]
    big_out, small_out = _adamw_all(
        [g_win, g_wmkv, g_wout], [w_in_t, w_mem_kv[0], w_out[0]], [m_w_in_t, m_w_mem_kv[0], m_w_out[0]],
        [v_w_in_t, v_w_mem_kv[0], v_w_out[0]], ga, gb, small_w, small_m, small_v)
    n_small = len(small_w)

    outputs = [small_out[4 * n_small][0, 0], dx.reshape(x.shape)]
    for kind in range(4):
        s = small_out[kind * n_small:(kind + 1) * n_small]
        outputs += [s[0], s[1], s[2], jnp.transpose(big_out[0][kind])[None], big_out[1][kind][None], s[3], s[4],
                    s[5][None], s[6][None], s[7], jnp.transpose(s[8]), big_out[2][kind][None]]
    return tuple(outputs)
```

```python
import functools

import numpy as np
import jax
import jax.numpy as jnp
from jax import lax
from jax.experimental import pallas as pl
from jax.experimental.pallas import tpu as pltpu

F32 = jnp.float32
BF16 = jnp.bfloat16
MESH = pl.DeviceIdType.MESH

D_MODEL = 1024
CHUNK = 128
A_WIDTH = 512
A_GROUPS = 4
SWA_WIDTH = 256
KV_WIDTH = 128
MEM_WIDTH = 256
MEM_LEN = 256
MIX_WIDTH = 1024
IN_WIDTH = 2816
N_BUCKETS = 32
MAX_DISTANCE = 128
EPS = 1e-6
NEG = -1e30
QK_SCALE = 0.125
HALF_HEAD_PAIR = 64

ADAM_LR = 0.001
ADAM_B1 = 0.9
ADAM_B2 = 0.999
ADAM_EPS = 1e-08
ADAM_WD = 0.01
ADAM_STEP = 10

N_CHIPS = 4
TILE_CHUNKS = 2
TILE = TILE_CHUNKS * CHUNK
PROJ_TILE = 512
VMEM_LIMIT = 56 * 1024 * 1024

SMALL_A_ROWS = 8
ROW_LOSS = 4
ROW_WS = 0
ROW_BS = 512
ROW_SINK = 520
ROW_REL = 528
SMALL_B_ROWS = 536


def _mm(a, b):
    return lax.dot_general(a, b, (((1,), (0,)), ((), ())), preferred_element_type=F32)


def _mm_nt(a, b):
    return lax.dot_general(a, b, (((1,), (1,)), ((), ())), preferred_element_type=F32)


def _mm_tn(a, b):
    return lax.dot_general(a, b, (((0,), (0,)), ((), ())), preferred_element_type=F32)


def _bucket_map():
    qi = np.arange(CHUNK)[:, None]
    kj = np.arange(2 * CHUNK)[None, :]
    n = np.maximum(qi + CHUNK - kj, 0)
    max_exact = N_BUCKETS // 2
    large = max_exact + (np.log(np.maximum(n, 1) / max_exact) / np.log(MAX_DISTANCE / max_exact)
                         * (N_BUCKETS - max_exact)).astype(np.int32)
    large = np.minimum(large, N_BUCKETS - 1)
    return np.where(n < max_exact, n, large).astype(np.int32)


_GELU_C = 0.7978845608028654
_GELU_A = 0.044715
_GELU_K1 = 2.0 * _GELU_C
_GELU_K2 = 2.0 * _GELU_C * _GELU_A


def _gelu(x):
    x2 = x * x
    s = 1.0 / (1.0 + jnp.exp(x * (-_GELU_K1 - _GELU_K2 * x2)))
    return x * s, (s, x2)


def _gelu_grad(x, saved):
    s, x2 = saved
    return s + x * (s * (1.0 - s)) * (_GELU_K1 + 3.0 * _GELU_K2 * x2)


def _sigmoid(x):
    return 1.0 / (1.0 + jnp.exp(-x))


def _lane_lo(shape):
    return lax.broadcasted_iota(jnp.int32, shape, 1) < HALF_HEAD_PAIR


def _swa_variants(t):
    lo = _lane_lo(t.shape)
    tr = pltpu.roll(t, HALF_HEAD_PAIR, 1)
    zero = jnp.zeros_like(t)
    return (jnp.where(lo, t, zero).astype(BF16), jnp.where(lo, zero, tr).astype(BF16),
            jnp.where(lo, tr, zero).astype(BF16), jnp.where(lo, zero, t).astype(BF16))


def _swa_unvariants(d0, d1, d2, d3):
    lo = _lane_lo(d0.shape)
    zero = jnp.zeros_like(d0)
    rolled = jnp.where(lo, zero, d1) + jnp.where(lo, d2, zero)
    return jnp.where(lo, d0, zero) + jnp.where(lo, zero, d3) + pltpu.roll(rolled, HALF_HEAD_PAIR, 1)


def _mem_variants(t):
    out = []
    for pair in range(2):
        tp = t[:, pair * 128:(pair + 1) * 128]
        lo = _lane_lo(tp.shape)
        zero = jnp.zeros_like(tp)
        out.append(jnp.where(lo, tp, zero).astype(BF16))
        out.append(jnp.where(lo, zero, tp).astype(BF16))
    return out


def _mem_unvariants(d0, d1, d2, d3):
    lo = _lane_lo(d0.shape)
    return jnp.concatenate([jnp.where(lo, d0, d1), jnp.where(lo, d2, d3)], axis=-1)


def _softmax(logits, sinks):
    m = jnp.max(logits, axis=-1, keepdims=True)
    if sinks is not None:
        m = jnp.maximum(m, sinks)
    p = jnp.exp(logits - m)
    den = jnp.sum(p, axis=-1, keepdims=True)
    if sinks is None:
        return p * (1.0 / den), None
    es = jnp.exp(sinks - m)
    inv = 1.0 / (den + es)
    return p * inv, es * inv


def _band_valid(with_prev):
    qi = lax.broadcasted_iota(jnp.int32, (CHUNK, 2 * CHUNK), 0)
    kj = lax.broadcasted_iota(jnp.int32, (CHUNK, 2 * CHUNK), 1)
    in_cur = (kj >= CHUNK) & (kj - CHUNK <= qi)
    if not with_prev:
        return in_cur
    return in_cur | ((kj < CHUNK) & (kj > qi))


def _causal_weights(ws_ref):
    row = lax.broadcasted_iota(jnp.int32, (CHUNK, CHUNK), 0)
    col = lax.broadcasted_iota(jnp.int32, (CHUNK, CHUNK), 1)
    return [jnp.where(row >= col, ws_ref[g], 0.0).astype(BF16) for g in range(A_GROUPS)]


def _rows_to_lanes(a, n):
    return jnp.concatenate([a[c * CHUNK:(c + 1) * CHUNK] for c in range(n)], axis=1)


def _lanes_to_rows(a, n):
    w = a.shape[1] // n
    return jnp.concatenate([a[:, c * w:(c + 1) * w] for c in range(n)], axis=0)


def _stack_heads(pair01, pair23):
    return jnp.concatenate([pair01[:, :256], pair01[:, 256:], pair23[:, :256], pair23[:, 256:]], axis=0)


def _pair_heads(s, r):
    return (jnp.concatenate([s[0:r], s[r:2 * r]], axis=1), jnp.concatenate([s[2 * r:3 * r], s[3 * r:4 * r]], axis=1))


def _pair_operands(variants):
    return (jnp.concatenate(variants[0:2], axis=0), jnp.concatenate(variants[2:4], axis=0))


def _split_pair_grads(d_pairs):
    return d_pairs[0][:256], d_pairs[0][256:], d_pairs[1][:256], d_pairs[1][256:]


def _halves_bf16(a):
    return (a[:, :128].astype(BF16), a[:, 128:].astype(BF16))


def _group_a_forward(au, av, vg, vb, wm, bs_rows):
    gu, tu = _gelu(au)
    gv, tv = _gelu(av)
    ya, res = [], []
    for g in range(A_GROUPS):
        sl = slice(g * 128, (g + 1) * 128)
        xg = gv[:, sl]
        xc = xg - jnp.mean(xg, axis=-1, keepdims=True)
        rstd = lax.rsqrt(jnp.mean(xc * xc, axis=-1, keepdims=True) + EPS)
        xhat = xc * rstd
        vn = _rows_to_lanes((xhat * vg[:, sl] + vb[:, sl]).astype(BF16), TILE_CHUNKS)
        s = _lanes_to_rows(_mm(wm[g], vn), TILE_CHUNKS) + bs_rows[g]
        ya.append(gu[:, sl] * s)
        res.append((xhat, rstd, vn, s))
    return ya, dict(gu=gu, tu=tu, tv=tv, groups=res)


def _attention_logits(qp, k_pairs):
    return _stack_heads(_mm_nt(qp[0], k_pairs[0]), _mm_nt(qp[1], k_pairs[1]))


def _attention_out(p, v_pairs, r):
    pp = _pair_heads(p.astype(BF16), r)
    return jnp.concatenate([_mm(pp[0], v_pairs[0]), _mm(pp[1], v_pairs[1])], axis=-1), pp


def _attention_dprobs(do_pairs, v_pairs):
    return _stack_heads(_mm_nt(do_pairs[0], v_pairs[0]), _mm_nt(do_pairs[1], v_pairs[1]))


def _softmax_backward(p, dp):
    delta = jnp.sum(p * dp, axis=-1, keepdims=True)
    return p * (dp - delta), delta


def _attention_grads(dl, pp, do_pairs, qp, k_pairs, r):
    dlp = _pair_heads(dl.astype(BF16), r)
    dq = jnp.concatenate([_mm(dlp[0], k_pairs[0]), _mm(dlp[1], k_pairs[1])], axis=-1)
    dk = (_mm_tn(dlp[0], qp[0]), _mm_tn(dlp[1], qp[1]))
    dv = (_mm_tn(pp[0], do_pairs[0]), _mm_tn(pp[1], do_pairs[1]))
    return dq, dk, dv


def _tile_specs(n_tiles_ex, width):
    return pl.BlockSpec((TILE, width), lambda b, i: (b * n_tiles_ex + jnp.minimum(i, n_tiles_ex - 1), 0))


def _prev_chunk_spec(n_tiles_ex, width):
    def index(b, i):
        chunk = TILE_CHUNKS * jnp.minimum(i, n_tiles_ex - 1)
        return (b * n_tiles_ex * TILE_CHUNKS + jnp.maximum(chunk - 1, 0), 0)
    return pl.BlockSpec((CHUNK, width), index)


def _full_spec(shape):
    zeros = (0,) * len(shape)
    return pl.BlockSpec(shape, lambda *_: zeros)


SMEM_SPEC = pl.BlockSpec(memory_space=pltpu.SMEM)
ANY_SPEC = pl.BlockSpec(memory_space=pl.ANY)


def _fill_bias(rel_ref, bk_ref, out_ref):
    bk = bk_ref[...]
    for h in range(4):
        acc = jnp.zeros((CHUNK, 2 * CHUNK), F32)
        for b in range(N_BUCKETS):
            acc = jnp.where(bk == b, rel_ref[h, b], acc)
        for t, with_prev in enumerate((True, False)):
            out_ref[t, h * CHUNK:(h + 1) * CHUNK, :] = jnp.where(_band_valid(with_prev), acc, NEG)


PROJ_WIDTHS = (A_WIDTH, A_WIDTH, SWA_WIDTH, KV_WIDTH, KV_WIDTH, MEM_WIDTH, MIX_WIDTH)
PROJ_OFFSETS = tuple(int(v) for v in np.cumsum((0,) + PROJ_WIDTHS))


MXU_TILE = 256
HALF_WIDTH = IN_WIDTH // 2
PHASE_COLS = (HALF_WIDTH // MXU_TILE * MXU_TILE, IN_WIDTH - HALF_WIDTH // MXU_TILE * MXU_TILE)


def _phase_columns(phase, chip_x):
    if phase == 0:
        return 0 if chip_x == 0 else IN_WIDTH - PHASE_COLS[0]
    return PHASE_COLS[0] if chip_x == 0 else 0


def _phase_parts(phase, chip_x):
    start = _phase_columns(phase, chip_x)
    return [(k, PROJ_OFFSETS[k] - start) for k in range(len(PROJ_WIDTHS))
            if start <= PROJ_OFFSETS[k] and PROJ_OFFSETS[k + 1] <= start + PHASE_COLS[phase]]


def _gather_and_project(x2, g_pre, w_in_s, w_mkv_s, w_out_s, rel_bias_t, buckets, b_spatial, x_arr):
    n_tok = x2.shape[0]
    n_tiles = n_tok // PROJ_TILE
    last = n_tiles - 1
    shapes = [w_in_s.shape, w_mkv_s.shape, w_out_s.shape]
    n_w = len(shapes)

    def body(x_sref, x_ref, g_ref, win_hbm, wmkv_hbm, wout_hbm, rel_ref, bk_ref, bsp_ref, h_ref, *refs):
        part_refs, refs = refs[:len(PROJ_WIDTHS)], refs[len(PROJ_WIDTHS):]
        bias_ref, bs_ref, refs = refs[0], refs[1], refs[2:]
        gin_hbm, gmkv_hbm, gout_hbm, wg, stage_in, stage_mkv, stage_out, own_mkv, own_out, h_all = refs[:10]
        send_sems, recv_sems, local_sems = refs[10:]
        p, t = pl.program_id(0), pl.program_id(1)
        x, y, c = lax.axis_index("x"), lax.axis_index("y"), lax.axis_index("c")
        me, sibling = (x, y, c), (x, y, 1 - c)
        my_shard = 2 * x + y
        gathered = [wg, gmkv_hbm, gout_hbm]

        def half_rows(w, shard, half):
            rows = shapes[w][0] // 2
            if w == 0:
                return wg.at[pl.ds(pl.multiple_of(shard * shapes[0][0] + half * rows, 16), rows), :]
            return gathered[w].at[shard, pl.ds(half * rows, rows), :]

        def first(w, rel):
            src = half_rows(w, my_shard, c) if w == 0 else (own_mkv, own_out)[w - 1].at[
                pl.ds(c * (shapes[w][0] // 2), shapes[w][0] // 2), :]
            k = 3 * w + rel - 1
            return pltpu.make_async_remote_copy(
                src_ref=src, dst_ref=half_rows(w, my_shard, c), send_sem=send_sems.at[k], recv_sem=recv_sems.at[k],
                device_id=(x ^ (rel >> 1), y ^ (rel & 1), c), device_id_type=MESH)

        def landed(w, rel):
            k = 3 * w + rel - 1
            ref = half_rows(w, my_shard ^ rel, c)
            return pltpu.make_async_remote_copy(src_ref=ref, dst_ref=ref, send_sem=send_sems.at[k],
                                                recv_sem=recv_sems.at[k], device_id=me, device_id_type=MESH)

        def passed(w, rel, half, to):
            k = 9 + 3 * w + rel - 1
            ref = half_rows(w, my_shard ^ rel, half)
            return pltpu.make_async_remote_copy(src_ref=ref, dst_ref=ref, send_sem=send_sems.at[k],
                                                recv_sem=recv_sems.at[k], device_id=to, device_id_type=MESH)

        def pass_on(w, rels):
            for rel in rels:
                landed(w, rel).wait_recv()
                passed(w, rel, c, sibling).start()
            for rel in rels:
                passed(w, rel, 1 - c, me).wait_recv()

        own_stores = [pltpu.make_async_copy(own_mkv, gmkv_hbm.at[my_shard], local_sems.at[3]),
                      pltpu.make_async_copy(own_out, gout_hbm.at[my_shard], local_sems.at[4])]

        @pl.when((p == 0) & (t == 0))
        def _():
            half_rows_in = shapes[0][0] // 2
            halves = [pl.ds(pl.multiple_of(hc * half_rows_in, 8), half_rows_in) for hc in (c, 1 - c)]
            loads = [pltpu.make_async_copy(win_hbm.at[halves[0], :], stage_in.at[halves[0], :], local_sems.at[0]),
                     pltpu.make_async_copy(wmkv_hbm, stage_mkv, local_sems.at[1]),
                     pltpu.make_async_copy(wout_hbm, stage_out, local_sems.at[2]),
                     pltpu.make_async_copy(win_hbm.at[halves[1], :], stage_in.at[halves[1], :], local_sems.at[6])]
            for cp in (loads[0], loads[3], loads[1], loads[2]):
                cp.start()
            loads[0].wait()
            half_rows(0, my_shard, c)[...] = stage_in[halves[0], :].astype(BF16)
            for rel in (1, 2):
                first(0, rel).start()
            loads[3].wait()
            half_rows(0, my_shard, 1 - c)[...] = stage_in[halves[1], :].astype(BF16)
            loads[1].wait()
            loads[2].wait()
            own_mkv[...] = stage_mkv[...].astype(BF16)
            own_out[...] = stage_out[...].astype(BF16)
            for cp in own_stores:
                cp.start()
            _fill_bias(rel_ref, bk_ref, bias_ref)
            for g in range(A_GROUPS):
                bs_ref[g] = jnp.transpose(jnp.broadcast_to(bsp_ref[g:g + 1, :], (CHUNK, CHUNK)))
            pass_on(0, (1,))
            first(0, 3).start()

        @pl.when((p == 0) & (t == n_tiles // 2))
        def _():
            for w in (1, 2):
                for rel in (1, 2, 3):
                    first(w, rel).start()

        store = pltpu.make_async_copy(wg, gin_hbm, local_sems.at[5])

        @pl.when((p == 1) & (t == 0))
        def _():
            pass_on(0, (2, 3))
            store.start()

        @pl.when((p == 1) & (t == n_tiles // 2))
        def _():
            for w in (1, 2):
                pass_on(w, (1, 2, 3))

        tile_rows = pl.ds(pl.multiple_of(t * PROJ_TILE, PROJ_TILE), PROJ_TILE)

        def project(h, phase):
            start = jnp.where(x_sref[0] == 0, _phase_columns(phase, 0), _phase_columns(phase, 1))
            proj = _mm_nt(h, wg[pl.ds(pl.multiple_of(start, MXU_TILE), PHASE_COLS[phase]), :])
            for chip_x in range(2):
                @pl.when(x_sref[0] == chip_x)
                def _():
                    for k, lo in _phase_parts(phase, chip_x):
                        part_refs[k][...] = proj[:, lo:lo + PROJ_WIDTHS[k]].astype(BF16)

        @pl.when(p == 0)
        def _():
            xv = x_ref[...]
            r = lax.rsqrt(jnp.mean(xv * xv, axis=-1, keepdims=True) + EPS)
            h = (xv * r * g_ref[...]).astype(BF16)
            h_ref[...] = h
            h_all[tile_rows, :] = h
            project(h, 0)

        @pl.when(p == 1)
        def _():
            project(h_all[tile_rows, :], 1)

        @pl.when((p == 1) & (t == last))
        def _():
            for w in range(n_w):
                for rel in (1, 2, 3):
                    first(w, rel).wait_send()
                    passed(w, rel, c, sibling).wait_send()
            for cp in own_stores:
                cp.wait()
            store.wait()

    def written_in(k):
        phase_on = [next(ph for ph in range(2) if k in dict(_phase_parts(ph, chip_x))) for chip_x in range(2)]

        def index(p, t, xs):
            phase = jnp.where(xs[0] == 0, phase_on[0], phase_on[1])
            return (jnp.where(p == phase, t, jnp.where(p < phase, 0, last)), 0)
        return index

    part_specs = [pl.BlockSpec((PROJ_TILE, PROJ_WIDTHS[k]), written_in(k)) for k in range(len(PROJ_WIDTHS))]
    vmem = pltpu.VMEM
    out = pl.pallas_call(
        body, name="gather_and_project",
        out_shape=[jax.ShapeDtypeStruct((n_tok, D_MODEL), BF16)]
        + [jax.ShapeDtypeStruct((n_tok, w), BF16) for w in PROJ_WIDTHS]
        + [jax.ShapeDtypeStruct((2, 4 * CHUNK, 2 * CHUNK), F32), jax.ShapeDtypeStruct((A_GROUPS, CHUNK, CHUNK), F32)]
        + [jax.ShapeDtypeStruct((N_CHIPS * shapes[0][0], shapes[0][1]), BF16)]
        + [jax.ShapeDtypeStruct((N_CHIPS,) + s, BF16) for s in shapes[1:]],
        grid_spec=pltpu.PrefetchScalarGridSpec(
            num_scalar_prefetch=1, grid=(2, n_tiles),
            in_specs=[pl.BlockSpec((PROJ_TILE, D_MODEL), lambda p, t, xs: (jnp.where(p == 0, t, last), 0)),
                      pl.BlockSpec((1, D_MODEL), lambda p, t, xs: (0, 0)), ANY_SPEC, ANY_SPEC, ANY_SPEC, SMEM_SPEC,
                      pl.BlockSpec(buckets.shape, lambda p, t, xs: (0, 0)),
                      pl.BlockSpec(b_spatial.shape, lambda p, t, xs: (0, 0))],
            out_specs=[pl.BlockSpec((PROJ_TILE, D_MODEL), lambda p, t, xs: (jnp.where(p == 0, t, last), 0))]
            + part_specs + [pl.BlockSpec((2, 4 * CHUNK, 2 * CHUNK), lambda p, t, xs: (0, 0, 0)),
                            pl.BlockSpec((A_GROUPS, CHUNK, CHUNK), lambda p, t, xs: (0, 0, 0))] + [ANY_SPEC] * 3,
            scratch_shapes=[vmem((N_CHIPS * shapes[0][0], shapes[0][1]), BF16), vmem(shapes[0], F32),
                            vmem(shapes[1], F32), vmem(shapes[2], F32), vmem(shapes[1], BF16), vmem(shapes[2], BF16),
                            vmem((n_tok, D_MODEL), BF16),
                            pltpu.SemaphoreType.DMA((18,)), pltpu.SemaphoreType.DMA((18,)),
                            pltpu.SemaphoreType.DMA((7,))]),
        compiler_params=pltpu.CompilerParams(vmem_limit_bytes=VMEM_LIMIT),
    )(x_arr, x2, g_pre, w_in_s, w_mkv_s, w_out_s, rel_bias_t, buckets, b_spatial)
    n_parts = len(PROJ_WIDTHS)
    return out[0], list(out[1:1 + n_parts]), out[3 + n_parts:], out[1 + n_parts], out[2 + n_parts]


def _load_chunk(j, i, sk_ref, sv_ref, skp_ref, svp_ref):
    rows = slice(j * CHUNK, (j + 1) * CHUNK)
    if j == 0:
        k_prev, v_prev, table = skp_ref[...], svp_ref[...], jnp.where(i > 0, 0, 1)
    else:
        prev = slice((j - 1) * CHUNK, j * CHUNK)
        k_prev, v_prev, table = sk_ref[prev, :], sv_ref[prev, :], 0
    k_pairs = _pair_operands(_swa_variants(jnp.concatenate([k_prev, sk_ref[rows, :]], axis=0).astype(F32)))
    v_pairs = _pair_operands(_swa_variants(jnp.concatenate([v_prev, sv_ref[rows, :]], axis=0).astype(F32)))
    return rows, k_pairs, v_pairs, table


def _tile_constants(ws_ref, bs_ref, sink_ref):
    wm = _causal_weights(ws_ref)
    bs_rows = [jnp.concatenate([bs_ref[g]] * TILE_CHUNKS, axis=0) for g in range(A_GROUPS)]
    sink_col = jnp.max(jnp.concatenate([jnp.full((CHUNK, 128), sink_ref[0, h], F32) for h in range(4)] * TILE_CHUNKS,
                                       axis=0), axis=-1, keepdims=True)
    return wm, bs_rows, sink_col


def _mix(parts, mem, x2, tgt2, v_g, v_b, w_sp, b_sp, sinks, bias, w_out, g_post, g_mem, w_mkv, n_ex, seq):
    n_tiles_ex = seq // TILE
    n_tok = n_ex * seq
    au, av, sq, sk, sv, mq, z = parts
    col = dict(zip(("au", "av", "sq", "sk", "sv", "mq", "z"),
                   (slice(PROJ_OFFSETS[k], PROJ_OFFSETS[k + 1]) for k in range(len(PROJ_WIDTHS)))))
    before_kv, after_kv = slice(0, col["sk"].start), slice(col["sv"].stop, IN_WIDTH)

    def body(au_ref, av_ref, sq_ref, sk_ref, sv_ref, skp_ref, svp_ref, mq_ref, z_ref, mem_ref, x_ref, tgt_ref,
             vg_ref, vb_ref, ws_ref, bs_ref, sink_ref, bias_ref, wout_ref, gpost_ref, gmem_ref, wmkv_ref,
             dout_ref, dproj_ref, dwmkv_ref, dgmem_ref, dwout_ref, dvg_ref, dvb_ref, dws_ref, dbs_ref, dsink_ref,
             drel_ref, loss_ref, dgpost_ref, carry_dp, carry_k, carry_v, memn_s, mem_ops, dmkv_s):
        b, i = pl.program_id(0), pl.program_id(1)

        @pl.when((b == 0) & (i == 0))
        def _():
            for ref in (dwmkv_ref, dgmem_ref, dwout_ref, dvg_ref, dvb_ref, dws_ref, dbs_ref, dsink_ref, drel_ref,
                        loss_ref, dgpost_ref):
                ref[...] = jnp.zeros_like(ref)

        def normalized_mem():
            m = mem_ref[0]
            return m * lax.rsqrt(jnp.mean(m * m, axis=-1, keepdims=True) + EPS)

        @pl.when(i == 0)
        def _():
            memn_s[...] = (normalized_mem() * gmem_ref[...]).astype(BF16)
            mkv = _mm(memn_s[...], wmkv_ref[...])
            for k, pair in enumerate(_pair_operands(_mem_variants(mkv[:, :MEM_WIDTH]))
                                     + _pair_operands(_mem_variants(mkv[:, MEM_WIDTH:]))):
                mem_ops[k] = pair
            dmkv_s[...] = jnp.zeros_like(dmkv_s)
            carry_k[...] = jnp.zeros_like(carry_k)
            carry_v[...] = jnp.zeros_like(carry_v)

        @pl.when(i > 0)
        def _():
            dproj_ref[:, before_kv] = carry_dp[:, before_kv]
            dproj_ref[:, after_kv] = carry_dp[:, after_kv]

        @pl.when(i < n_tiles_ex)
        def _():
            wm, bs_rows, sink_col = _tile_constants(ws_ref, bs_ref, sink_ref)
            mk_pairs, mv_pairs = (mem_ops[0], mem_ops[1]), (mem_ops[2], mem_ops[3])
            vg = vg_ref[...]

            au_v, av_v = au_ref[...].astype(F32), av_ref[...].astype(F32)
            ya, res = _group_a_forward(au_v, av_v, vg, vb_ref[...], wm, bs_rows)
            swa, logits, yb = [], [], []
            for j in range(TILE_CHUNKS):
                rows, k_pairs, v_pairs, table = _load_chunk(j, i, sk_ref, sv_ref, skp_ref, svp_ref)
                qp = _halves_bf16(sq_ref[rows, :] * QK_SCALE)
                logits.append(_attention_logits(qp, k_pairs) + bias_ref[table])
                swa.append([rows, k_pairs, v_pairs, qp])
            p_swa, sink_p = _softmax(jnp.concatenate(logits, axis=0), sink_col)
            for j in range(TILE_CHUNKS):
                out, pp = _attention_out(p_swa[j * 4 * CHUNK:(j + 1) * 4 * CHUNK], swa[j][2], CHUNK)
                yb.append(out)
                swa[j].append(pp)
            mqp = _halves_bf16(mq_ref[...] * QK_SCALE)
            pm, _ = _softmax(_attention_logits(mqp, mk_pairs), None)
            yc, ppm = _attention_out(pm, mv_pairs, TILE)
            ycat = jnp.concatenate(ya + [jnp.concatenate(yb, axis=0), yc], axis=-1)

            zv = z_ref[...].astype(F32)
            sig = _sigmoid(zv)
            sz = zv * sig
            y_b = (ycat * sz).astype(BF16)
            o = _mm(y_b, wout_ref[...])
            r2 = lax.rsqrt(jnp.mean(o * o, axis=-1, keepdims=True) + EPS)
            nrm = o * r2
            gp = gpost_ref[...]
            diff = x_ref[...] + nrm * gp - tgt_ref[...]
            loss_ref[...] += jnp.sum(diff * diff) * (0.5 / D_MODEL)
            dout = diff * (1.0 / D_MODEL)
            dout_ref[...] = dout
            dgpost_ref[...] += jnp.sum(dout * nrm, axis=0, keepdims=True)
            dn = dout * gp
            do_b = (r2 * (dn - nrm * jnp.mean(dn * nrm, axis=-1, keepdims=True))).astype(BF16)
            dwout_ref[...] += _mm_tn(y_b, do_b)
            dy = _mm_nt(do_b, wout_ref[...])
            carry_dp[:, col["z"]] = (dy * ycat * (sig + sz * (1.0 - sig))).astype(BF16)
            dyc = dy * sz

            dgu, dgv = [], []
            for g in range(A_GROUPS):
                sl = slice(g * 128, (g + 1) * 128)
                xhat, rstd, vn, s = res["groups"][g]
                dya = dyc[:, sl]
                dgu.append(dya * s)
                ds = dya * res["gu"][:, sl]
                dbs_ref[:, sl] += sum(ds[c * CHUNK:(c + 1) * CHUNK] for c in range(TILE_CHUNKS))
                ds_b = _rows_to_lanes(ds.astype(BF16), TILE_CHUNKS)
                dws_ref[g] += _mm_nt(ds_b, vn)
                dvn = _lanes_to_rows(_mm_tn(wm[g], ds_b), TILE_CHUNKS)
                dvg_ref[:, sl] += jnp.sum(dvn * xhat, axis=0, keepdims=True)
                dvb_ref[:, sl] += jnp.sum(dvn, axis=0, keepdims=True)
                dxh = dvn * vg[:, sl]
                dgv.append(rstd * (dxh - jnp.mean(dxh, axis=-1, keepdims=True)
                                   - xhat * jnp.mean(dxh * xhat, axis=-1, keepdims=True)))
            carry_dp[:, col["au"]] = (jnp.concatenate(dgu, axis=-1) * _gelu_grad(au_v, res["tu"])).astype(BF16)
            carry_dp[:, col["av"]] = (jnp.concatenate(dgv, axis=-1) * _gelu_grad(av_v, res["tv"])).astype(BF16)

            do_pairs = [_halves_bf16(dyc[rows, A_WIDTH:A_WIDTH + SWA_WIDTH]) for rows, *_ in swa]
            dl_swa, delta = _softmax_backward(p_swa, jnp.concatenate(
                [_attention_dprobs(do_pairs[j], swa[j][2]) for j in range(TILE_CHUNKS)], axis=0))
            sink_terms = sink_p * delta
            lane4 = lax.broadcasted_iota(jnp.int32, (1, 128), 1)
            dsink_vec = jnp.zeros((1, 128), F32)
            for h in range(4):
                head_sum = sum(jnp.sum(sink_terms[(4 * j + h) * CHUNK:(4 * j + h + 1) * CHUNK])
                               for j in range(TILE_CHUNKS))
                dsink_vec = dsink_vec + jnp.where(lane4 == h, -head_sum, 0.0)
            dsink_ref[...] += dsink_vec
            drel_ref[...] += sum(dl_swa[j * 4 * CHUNK:(j + 1) * 4 * CHUNK] for j in range(TILE_CHUNKS))
            dk_parts, dv_parts = [], []
            for j, (rows, k_pairs, v_pairs, qp, pp) in enumerate(swa):
                dq, dk, dv = _attention_grads(dl_swa[j * 4 * CHUNK:(j + 1) * 4 * CHUNK], pp, do_pairs[j], qp, k_pairs,
                                              CHUNK)
                carry_dp[rows, col["sq"]] = (dq * QK_SCALE).astype(BF16)
                dk_parts.append(_swa_unvariants(*_split_pair_grads(dk)))
                dv_parts.append(_swa_unvariants(*_split_pair_grads(dv)))

            dc_pairs = _halves_bf16(dyc[:, A_WIDTH + SWA_WIDTH:])
            dl_mem, _ = _softmax_backward(pm, _attention_dprobs(dc_pairs, mv_pairs))
            dmq, dmk, dmv = _attention_grads(dl_mem, ppm, dc_pairs, mqp, mk_pairs, TILE)
            carry_dp[:, col["mq"]] = (dmq * QK_SCALE).astype(BF16)
            dmkv_s[...] += jnp.concatenate([_mem_unvariants(*_split_pair_grads(dmk)),
                                            _mem_unvariants(*_split_pair_grads(dmv))], axis=-1)

            for parts_c, carry, cols in ((dk_parts, carry_k, col["sk"]), (dv_parts, carry_v, col["sv"])):
                @pl.when(i > 0)
                def _():
                    dproj_ref[:, cols] = (carry[...] + jnp.concatenate(
                        [jnp.zeros((TILE - CHUNK, KV_WIDTH), F32), parts_c[0][:CHUNK]], axis=0)).astype(BF16)
                new = [parts_c[0][CHUNK:]]
                for j in range(1, TILE_CHUNKS):
                    new[-1] = new[-1] + parts_c[j][:CHUNK]
                    new.append(parts_c[j][CHUNK:])
                carry[...] = jnp.concatenate(new, axis=0)

        @pl.when(i == n_tiles_ex)
        def _():
            dproj_ref[:, col["sk"]] = carry_k[...].astype(BF16)
            dproj_ref[:, col["sv"]] = carry_v[...].astype(BF16)
            d_b = dmkv_s[...].astype(BF16)
            dwmkv_ref[...] += _mm_tn(memn_s[...], d_b)
            dgmem_ref[...] += jnp.sum(_mm_nt(d_b, wmkv_ref[...]) * normalized_mem(), axis=0, keepdims=True)

    tile = functools.partial(_tile_specs, n_tiles_ex)
    prev = functools.partial(_prev_chunk_spec, n_tiles_ex)
    late = pl.BlockSpec((TILE, IN_WIDTH), lambda b, i: (b * n_tiles_ex + jnp.maximum(i - 1, 0), 0))
    return pl.pallas_call(
        body, name="mix", grid=(n_ex, n_tiles_ex + 1),
        out_shape=[jax.ShapeDtypeStruct((n_tok, D_MODEL), F32), jax.ShapeDtypeStruct((n_tok, IN_WIDTH), BF16),
                   jax.ShapeDtypeStruct((D_MODEL, 2 * MEM_WIDTH), F32), jax.ShapeDtypeStruct((1, D_MODEL), F32),
                   jax.ShapeDtypeStruct((MIX_WIDTH, D_MODEL), F32), jax.ShapeDtypeStruct((1, A_WIDTH), F32),
                   jax.ShapeDtypeStruct((1, A_WIDTH), F32), jax.ShapeDtypeStruct((A_GROUPS, CHUNK, CHUNK), F32),
                   jax.ShapeDtypeStruct((CHUNK, A_WIDTH), F32), jax.ShapeDtypeStruct((1, 128), F32),
                   jax.ShapeDtypeStruct((4 * CHUNK, 2 * CHUNK), F32), jax.ShapeDtypeStruct((1, 128), F32),
                   jax.ShapeDtypeStruct((1, D_MODEL), F32)],
        in_specs=[tile(A_WIDTH), tile(A_WIDTH), tile(SWA_WIDTH), tile(KV_WIDTH), tile(KV_WIDTH),
                  prev(KV_WIDTH), prev(KV_WIDTH), tile(MEM_WIDTH), tile(MIX_WIDTH),
                  pl.BlockSpec((1, MEM_LEN, D_MODEL), lambda b, i: (b, 0, 0)),
                  tile(D_MODEL), tile(D_MODEL),
                  _full_spec((1, A_WIDTH)), _full_spec((1, A_WIDTH)), _full_spec((A_GROUPS, CHUNK, CHUNK)),
                  _full_spec((A_GROUPS, CHUNK, CHUNK)), SMEM_SPEC, _full_spec((2, 4 * CHUNK, 2 * CHUNK)),
                  _full_spec((MIX_WIDTH, D_MODEL)), _full_spec((1, D_MODEL)), _full_spec((1, D_MODEL)),
                  _full_spec((D_MODEL, 2 * MEM_WIDTH))],
        out_specs=[tile(D_MODEL), late, _full_spec((D_MODEL, 2 * MEM_WIDTH)), _full_spec((1, D_MODEL)),
                   _full_spec((MIX_WIDTH, D_MODEL)), _full_spec((1, A_WIDTH)), _full_spec((1, A_WIDTH)),
                   _full_spec((A_GROUPS, CHUNK, CHUNK)), _full_spec((CHUNK, A_WIDTH)), _full_spec((1, 128)),
                   _full_spec((4 * CHUNK, 2 * CHUNK)), _full_spec((1, 128)), _full_spec((1, D_MODEL))],
        scratch_shapes=[pltpu.VMEM((TILE, IN_WIDTH), BF16), pltpu.VMEM((TILE, KV_WIDTH), F32),
                        pltpu.VMEM((TILE, KV_WIDTH), F32), pltpu.VMEM((MEM_LEN, D_MODEL), BF16),
                        pltpu.VMEM((4, 2 * MEM_LEN, 128), BF16), pltpu.VMEM((MEM_LEN, 2 * MEM_WIDTH), F32)],
        compiler_params=pltpu.CompilerParams(vmem_limit_bytes=VMEM_LIMIT),
    )(au, av, sq, sk, sv, sk, sv, mq, z, mem, x2, tgt2, v_g, v_b, w_sp, b_sp, sinks, bias, w_out, g_post, g_mem,
      w_mkv)


BWD_PROJ_TILE = 512


def _fill_small_grads(dgpre_ref, dgpost_ref, dgmem_ref, dvg_ref, dvb_ref, dws_ref, dbs_ref, dsink_ref, drel_ref,
                      loss_ref, bk_ref, a_ref, b_ref):
    a_ref[...] = jnp.zeros_like(a_ref)
    b_ref[...] = jnp.zeros_like(b_ref)
    a_ref[0:1, :] = dgpre_ref[...]
    a_ref[1:2, :] = dgpost_ref[...]
    a_ref[2:3, :] = dgmem_ref[...]
    a_ref[3:4, :] = jnp.concatenate([dvg_ref[...], dvb_ref[...]], axis=-1)
    a_ref[ROW_LOSS:ROW_LOSS + 1, 0:128] = loss_ref[...]
    row = lax.broadcasted_iota(jnp.int32, (CHUNK, CHUNK), 0)
    col = lax.broadcasted_iota(jnp.int32, (CHUNK, CHUNK), 1)
    for g in range(A_GROUPS):
        b_ref[ROW_WS + g * CHUNK:ROW_WS + (g + 1) * CHUNK, :] = jnp.where(row >= col, dws_ref[g], 0.0)
        by_token = jnp.transpose(dbs_ref[:, g * 128:(g + 1) * 128])
        b_ref[ROW_BS + g:ROW_BS + g + 1, :] = jnp.sum(by_token, axis=0, keepdims=True)
    b_ref[ROW_SINK:ROW_SINK + 1, :] = dsink_ref[...]
    bk = bk_ref[...]
    rel_row = lax.broadcasted_iota(jnp.int32, (8, 128), 0)
    rel_col = lax.broadcasted_iota(jnp.int32, (8, 128), 1)
    rel = jnp.zeros((8, 128), F32)
    for h in range(4):
        acc = drel_ref[h * CHUNK:(h + 1) * CHUNK, :]
        for b in range(N_BUCKETS):
            rel = jnp.where((rel_row == h) & (rel_col == b), jnp.sum(jnp.where(bk == b, acc, 0.0)), rel)
    b_ref[ROW_REL:ROW_REL + 8, :] = rel


def _backward_projection(x2, dout, dproj, g_pre, w_in_t, small_parts):
    n_tok = x2.shape[0]
    n_steps = n_tok // BWD_PROJ_TILE
    n_small = len(small_parts)

    def body(x_ref, dout_ref, dp_ref, g_ref, w_hbm, *refs):
        small_refs, (dx_ref, a_ref, b_ref, w_vmem, dgpre, sem) = refs[:n_small], refs[n_small:]
        step = pl.program_id(0)

        @pl.when(step == 0)
        def _():
            load = pltpu.make_async_copy(w_hbm, w_vmem, sem)
            load.start()
            dgpre[...] = jnp.zeros_like(dgpre)
            load.wait()

        xv = x_ref[...]
        r = lax.rsqrt(jnp.mean(xv * xv, axis=-1, keepdims=True) + EPS)
        xn = xv * r
        dh = _mm(dp_ref[...], w_vmem[...])
        dgpre[...] += jnp.sum(dh * xn, axis=0, keepdims=True)
        dhg = dh * g_ref[...]
        dx_ref[...] = r * (dhg - xn * jnp.mean(dhg * xn, axis=-1, keepdims=True)) + dout_ref[...]

        @pl.when(step == n_steps - 1)
        def _():
            _fill_small_grads(dgpre, *small_refs, a_ref, b_ref)

    row = lambda w: pl.BlockSpec((BWD_PROJ_TILE, w), lambda i: (i, 0))
    return pl.pallas_call(
        body, name="backward_projection", grid=(n_steps,),
        out_shape=[jax.ShapeDtypeStruct((n_tok, D_MODEL), F32), jax.ShapeDtypeStruct((SMALL_A_ROWS, D_MODEL), F32),
                   jax.ShapeDtypeStruct((SMALL_B_ROWS, 128), F32)],
        in_specs=[row(D_MODEL), row(D_MODEL), row(IN_WIDTH), _full_spec((1, D_MODEL)), ANY_SPEC]
        + [_full_spec(a.shape) for a in small_parts],
        out_specs=[row(D_MODEL), _full_spec((SMALL_A_ROWS, D_MODEL)), _full_spec((SMALL_B_ROWS, 128))],
        scratch_shapes=[pltpu.VMEM((IN_WIDTH, D_MODEL), BF16), pltpu.VMEM((1, D_MODEL), F32),
                        pltpu.SemaphoreType.DMA],
        input_output_aliases={1: 0},
        compiler_params=pltpu.CompilerParams(vmem_limit_bytes=VMEM_LIMIT),
    )(x2, dout, dproj, g_pre, w_in_t, *small_parts)


SHARD_ROWS = IN_WIDTH // N_CHIPS
SHARD_WINDOW = 768
SHARD_HALF = SHARD_ROWS // 2
DWIN_TILE = 2048
N_REL = N_CHIPS - 1


def _shard_window_start(shard):
    return (shard * SHARD_ROWS // 128) * 128


def _reduce_gradients(dproj, h, big, small, shard_arr):
    n_tok = h.shape[0]
    tile = min(DWIN_TILE, n_tok)
    n_sub = n_tok // tile
    last = N_CHIPS - 1
    n_big, n_small = len(big), len(small)
    big_half = [g.shape[2:] for g in big]
    sem_big_d2d = 2 * N_CHIPS
    sem_big_ici = sem_big_d2d + n_big
    sem_big_swap = sem_big_ici + N_REL * n_big
    sem_small_d2d = sem_big_swap + n_big
    sem_small_ici = sem_small_d2d + n_small
    n_sems = sem_small_ici + N_REL * n_small
    loc_small = n_big
    loc_out_win = loc_small + n_small
    loc_out_big = loc_out_win + 2
    loc_out_small = loc_out_big + 2 * n_big
    n_local = loc_out_small + n_small

    def relation_of_slot(s):
        return (s + 2) % N_REL + 1

    def shard_of_slot(s, my_shard):
        return my_shard ^ jnp.where(s == last, 0, relation_of_slot(s))

    def body(shard_ref, dp_ref, h_hbm, *refs):
        h_vmem, h_sem, refs = refs[-2], refs[-1], refs[:-2]
        big_hbm, refs = refs[:n_big], refs[n_big:]
        small_hbm, refs = refs[:n_small], refs[n_small:]
        out_hbm, refs = refs[0], refs[1:]
        big_out, refs = refs[:n_big], refs[n_big:]
        small_out, refs = refs[:n_small], refs[n_small:]
        part, recv_d2d, send_ici, recv_ici, mine_buf, other_buf = refs[:6]
        refs = refs[6:]
        big_own, big_recv, big_send, big_land, big_mine, big_other = (
            refs[k * n_big:(k + 1) * n_big] for k in range(6))
        refs = refs[6 * n_big:]
        small_own, small_recv, small_all = (refs[k * n_small:(k + 1) * n_small] for k in range(3))
        send_sems, recv_sems, local_sems = refs[3 * n_small:]

        s, t = pl.program_id(0), pl.program_id(1)
        x, y, c = lax.axis_index("x"), lax.axis_index("y"), lax.axis_index("c")
        my_chip = 2 * x + y
        sibling = (x, y, 1 - c)
        my_rows = pl.ds(pl.multiple_of(c * SHARD_HALF, 8), SHARD_HALF)
        other_rows = pl.ds(pl.multiple_of((1 - c) * SHARD_HALF, 8), SHARD_HALF)

        def remote(src, dst, k, to):
            return pltpu.make_async_remote_copy(src_ref=src, dst_ref=dst, send_sem=send_sems.at[k],
                                                recv_sem=recv_sems.at[k], device_id=to, device_id_type=MESH)

        def chip_at(rel):
            return (x ^ (rel >> 1), y ^ (rel & 1), c)

        def to_sibling(k):
            return remote(part.at[k % 2, other_rows, :], recv_d2d.at[k], k, sibling)

        def to_chip(k):
            return remote(send_ici.at[k], recv_ici.at[k], N_CHIPS + k, chip_at(relation_of_slot(k)))

        swap = remote(mine_buf, other_buf, 2 * N_CHIPS - 1, sibling)
        big_load = [pltpu.make_async_copy(big_hbm[w].at[:, pl.ds(c, 1)], big_own[w], local_sems.at[w])
                    for w in range(n_big)]
        big_to_sibling = [remote(big_hbm[w].at[:, pl.ds(1 - c, 1)], big_recv[w], sem_big_d2d + w, sibling)
                          for w in range(n_big)]
        big_to_chip = [[remote(big_send[w].at[k], big_land[w].at[k], sem_big_ici + N_REL * w + k, chip_at(k + 1))
                        for k in range(N_REL)] for w in range(n_big)]
        big_swap = [remote(big_mine[w], big_other[w], sem_big_swap + w, sibling) for w in range(n_big)]
        small_load = [pltpu.make_async_copy(small_hbm[i], small_own[i], local_sems.at[loc_small + i])
                      for i in range(n_small)]
        small_to_sibling = [remote(small_hbm[i], small_recv[i], sem_small_d2d + i, sibling) for i in range(n_small)]
        small_to_chip = [[remote(small_all[i].at[my_chip], small_all[i].at[my_chip],
                                 sem_small_ici + N_REL * i + k, chip_at(k + 1))
                          for k in range(N_REL)] for i in range(n_small)]

        h_loads = [pltpu.make_async_copy(h_hbm.at[rows, :], h_vmem.at[rows, :], h_sem.at[k]) for k, rows in enumerate(
            [pl.ds(0, tile)] + ([pl.ds(tile, n_tok - tile)] if n_sub > 1 else []))]

        @pl.when((s == 0) & (t == 0))
        def _():
            for cp in h_loads + big_load + big_to_sibling + small_load + small_to_sibling:
                cp.start()
            h_loads[0].wait()

        if n_sub > 1:
            @pl.when((s == 0) & (t == 1))
            def _():
                h_loads[1].wait()

        @pl.when((s == 0) & (t == n_sub - 1))
        def _():
            for cp in big_load + small_load:
                cp.wait()
            for cp in big_to_sibling + small_to_sibling:
                cp.wait_recv()
                cp.wait_send()
            for w in range(n_big):
                for k in range(N_REL):
                    shard = my_chip ^ (k + 1)
                    big_send[w][k] = (big_own[w][shard, 0] + big_recv[w][shard, 0]).astype(BF16)
                    big_to_chip[w][k].start()
            for i in range(n_small):
                small_all[i][my_chip] = small_own[i][...] + small_recv[i][...]
                for k in range(N_REL):
                    small_to_chip[i][k].start()

        @pl.when((s > 0) & (t == jnp.where(s == last, 0, min(1, n_sub - 1))))
        def _():
            k = s - 1
            cp = to_sibling(k)
            cp.wait_recv()
            cp.wait_send()
            send_ici[k] = (part[k % 2, my_rows, :] + recv_d2d[k]).astype(BF16)
            to_chip(k).start()

        def big_rows(w, half):
            rows = big_half[w][0]
            return big_out[w].at[pl.ds(pl.multiple_of(half * rows, 8), rows), :]

        big_store_mine = [pltpu.make_async_copy(big_mine[w], big_rows(w, c), local_sems.at[loc_out_big + 2 * w])
                          for w in range(n_big)]
        big_store_other = [pltpu.make_async_copy(big_other[w], big_rows(w, 1 - c),
                                                 local_sems.at[loc_out_big + 2 * w + 1]) for w in range(n_big)]
        small_store = [pltpu.make_async_copy(small_all[i], small_out[i], local_sems.at[loc_out_small + i])
                       for i in range(n_small)]

        @pl.when((s == last) & (t == 0))
        def _():
            for w in range(n_big):
                total = big_own[w][my_chip, 0] + big_recv[w][my_chip, 0]
                for k in range(N_REL):
                    big_to_chip[w][k].wait_recv()
                    total = total + big_land[w][k].astype(F32)
                big_mine[w][...] = total
                big_swap[w].start()
                big_store_mine[w].start()
            for i in range(n_small):
                for k in range(N_REL):
                    small_to_chip[i][k].wait_recv()
                small_store[i].start()

        r = _mm_tn(dp_ref[...], h_vmem[pl.ds(pl.multiple_of(t * tile, tile), tile), :])
        odd = shard_of_slot(s, shard_ref[0]) % 2
        for parity in range(2):
            rows = r[64 * parity:64 * parity + SHARD_ROWS]

            @pl.when((odd == parity) & (t == 0))
            def _():
                part[s % 2] = rows

            @pl.when((odd == parity) & (t > 0))
            def _():
                part[s % 2] += rows

        @pl.when(t == n_sub - 1)
        def _():
            to_sibling(s).start()

        @pl.when((s == last) & (t == n_sub - 1))
        def _():
            cp = to_sibling(last)
            cp.wait_recv()
            cp.wait_send()
            total = part[last % 2, my_rows, :] + recv_d2d[last]
            for k in range(last):
                to_chip(k).wait_recv()
                total = total + recv_ici[k].astype(F32)
            mine_buf[...] = total
            swap.start()
            out_mine = pltpu.make_async_copy(mine_buf, out_hbm.at[my_rows, :], local_sems.at[0])
            out_mine.start()
            swap.wait_recv()
            out_other = pltpu.make_async_copy(other_buf, out_hbm.at[other_rows, :], local_sems.at[1])
            out_other.start()
            for w in range(n_big):
                big_swap[w].wait_recv()
                big_store_other[w].start()
            stores = [out_mine, out_other] + big_store_mine + big_store_other + small_store
            for k in range(last):
                to_chip(k).wait_send()
            swap.wait_send()
            for w in range(n_big):
                for k in range(N_REL):
                    big_to_chip[w][k].wait_send()
                big_swap[w].wait_send()
            for i in range(n_small):
                for k in range(N_REL):
                    small_to_chip[i][k].wait_send()
            for cp in stores:
                cp.wait()

    half = (SHARD_HALF, D_MODEL)
    vmem = pltpu.VMEM
    scratch = [vmem((2, SHARD_ROWS, D_MODEL), F32), vmem((N_CHIPS,) + half, F32),
               vmem((N_REL,) + half, BF16), vmem((N_REL,) + half, BF16), vmem(half, F32), vmem(half, F32)]
    scratch += [vmem((N_CHIPS, 1) + hs, F32) for hs in big_half] * 2
    scratch += [vmem((N_REL,) + hs, BF16) for hs in big_half] * 2
    scratch += [vmem(hs, F32) for hs in big_half] * 2
    scratch += [vmem(a.shape, F32) for a in small] * 2 + [vmem((N_CHIPS,) + a.shape, F32) for a in small]
    scratch += [pltpu.SemaphoreType.DMA((n_sems,)), pltpu.SemaphoreType.DMA((n_sems,)),
                pltpu.SemaphoreType.DMA((n_local,)), vmem(h.shape, BF16), pltpu.SemaphoreType.DMA((2,))]
    n_hbm = n_big + n_small
    out = pl.pallas_call(
        body, name="reduce_gradients",
        out_shape=[jax.ShapeDtypeStruct((SHARD_ROWS, D_MODEL), F32)]
        + [jax.ShapeDtypeStruct((2 * hs[0], hs[1]), F32) for hs in big_half]
        + [jax.ShapeDtypeStruct((N_CHIPS,) + a.shape, F32) for a in small],
        grid_spec=pltpu.PrefetchScalarGridSpec(
            num_scalar_prefetch=1, grid=(N_CHIPS, n_sub),
            in_specs=[pl.BlockSpec((pl.Element(tile), pl.Element(SHARD_WINDOW)),
                                   lambda s, t, m: (t * tile, _shard_window_start(shard_of_slot(s, m[0])))),
                      ANY_SPEC] + [ANY_SPEC] * n_hbm,
            out_specs=[ANY_SPEC] * (1 + n_hbm),
            scratch_shapes=scratch),
        compiler_params=pltpu.CompilerParams(vmem_limit_bytes=VMEM_LIMIT),
    )(shard_arr, dproj, h, *big, *small)
    return out[:1 + n_big], out[1 + n_big:]


def _adamw(w, g, m, v):
    m2 = ADAM_B1 * m + (1.0 - ADAM_B1) * g
    v2 = ADAM_B2 * v + (1.0 - ADAM_B2) * (g * g)
    m_hat = m2 / (1.0 - ADAM_B1 ** ADAM_STEP)
    v_hat = v2 / (1.0 - ADAM_B2 ** ADAM_STEP)
    delta = -ADAM_LR * (m_hat / (jnp.sqrt(v_hat) + ADAM_EPS) + ADAM_WD * w)
    return delta, m2, v2


ADAM_STEPS = 2


def _adamw_all(shard_grads, shard_w, shard_m, shard_v, ra, rb, small_w, small_m, small_v):
    n_sh, n = len(shard_w), len(small_w)

    def body(*refs):
        sh_in, refs = refs[:4 * n_sh], refs[4 * n_sh:]
        ra_ref, rb_ref, refs = refs[0], refs[1], refs[2:]
        w_refs, m_refs, v_refs, refs = refs[:n], refs[n:2 * n], refs[2 * n:3 * n], refs[3 * n:]
        sh_out, outs = refs[:4 * n_sh], refs[4 * n_sh:]
        for k in range(n_sh):
            g = sh_in[k][...]
            delta, m2, v2 = _adamw(sh_in[n_sh + k][...], g, sh_in[2 * n_sh + k][...], sh_in[3 * n_sh + k][...])
            for ref, val in zip(sh_out[4 * k:4 * k + 4], (g, delta, m2, v2)):
                ref[...] = val

        @pl.when(pl.program_id(0) == 0)
        def _():
            g_outs, d_outs, m_outs, v_outs = outs[:n], outs[n:2 * n], outs[2 * n:3 * n], outs[3 * n:4 * n]
            ga, gb = ra_ref[0], rb_ref[0]
            for chip in range(1, N_CHIPS):
                ga = ga + ra_ref[chip]
                gb = gb + rb_ref[chip]
            outs[4 * n][...] = ga[ROW_LOSS:ROW_LOSS + 1, 0:128]
            grads = [ga[0:1, :], ga[1:2, :], ga[2:3, :], ga[3:4, :A_WIDTH], ga[3:4, A_WIDTH:],
                     gb[ROW_WS:ROW_WS + A_GROUPS * CHUNK, :].reshape(A_GROUPS, CHUNK, CHUNK),
                     gb[ROW_BS:ROW_BS + A_GROUPS, :], gb[ROW_SINK:ROW_SINK + 1, 0:4],
                     gb[ROW_REL:ROW_REL + 4, 0:N_BUCKETS]]
            for k in range(n):
                delta, m2, v2 = _adamw(w_refs[k][...], grads[k], m_refs[k][...], v_refs[k][...])
                g_outs[k][...] = grads[k]
                d_outs[k][...] = delta
                m_outs[k][...] = m2
                v_outs[k][...] = v2

    def rows_block(a):
        assert a.shape[0] % (8 * ADAM_STEPS) == 0
        return pl.BlockSpec((a.shape[0] // ADAM_STEPS, a.shape[1]), lambda i: (i, 0))

    sh_specs = [rows_block(w) for w in shard_w]
    small_in = [ra, rb, *small_w, *small_m, *small_v]
    small_out_shapes = [jax.ShapeDtypeStruct(w.shape, F32) for w in small_w] * 4 + [jax.ShapeDtypeStruct((1, 128), F32)]
    out = pl.pallas_call(
        body, name="adamw_all", grid=(ADAM_STEPS,),
        out_shape=[jax.ShapeDtypeStruct(w.shape, F32) for w in shard_w for _ in range(4)] + small_out_shapes,
        in_specs=sh_specs * 4 + [_full_spec(a.shape) for a in small_in],
        out_specs=[spec for spec in sh_specs for _ in range(4)] + [_full_spec(s.shape) for s in small_out_shapes],
        compiler_params=pltpu.CompilerParams(vmem_limit_bytes=VMEM_LIMIT),
    )(*shard_grads, *shard_w, *shard_m, *shard_v, *small_in)
    return [out[4 * k:4 * k + 4] for k in range(n_sh)], out[4 * n_sh:]


def kernel(x, mem, pre_norm_g, post_norm_g, mem_norm_g, w_in, w_mem_kv, v_norm_g, v_norm_b, w_spatial, b_spatial, attn_sinks, rel_bias, w_out, loss_target, m_pre_norm_g, m_post_norm_g, m_mem_norm_g, m_w_in, m_w_mem_kv, m_v_norm_g, m_v_norm_b, m_w_spatial, m_b_spatial, m_attn_sinks, m_rel_bias, m_w_out, v_pre_norm_g, v_post_norm_g, v_mem_norm_g, v_w_in, v_w_mem_kv, v_v_norm_g, v_v_norm_b, v_w_spatial, v_b_spatial, v_attn_sinks, v_rel_bias, v_w_out):
    n_ex, seq, _ = x.shape
    n_tok = n_ex * seq
    x2 = x.reshape(n_tok, D_MODEL)
    tgt2 = loss_target.reshape(n_tok, D_MODEL)
    buckets = jnp.asarray(_bucket_map())
    shard_arr = (2 * lax.axis_index("x") + lax.axis_index("y")).astype(jnp.int32).reshape(1)
    w_sp = w_spatial[0]
    w_in_t, m_w_in_t, v_w_in_t = (jnp.transpose(a[0]) for a in (w_in, m_w_in, v_w_in))
    rel_t, m_rel_t, v_rel_t = (jnp.transpose(a) for a in (rel_bias, m_rel_bias, v_rel_bias))

    x_arr = lax.axis_index("x").astype(jnp.int32).reshape(1)
    h_b, parts, (w_in_b, g_mkv, g_out), bias, b_sp = _gather_and_project(
        x2, pre_norm_g, w_in_t, w_mem_kv[0], w_out[0], rel_t, buckets, b_spatial[0], x_arr)
    w_mkv_b = g_mkv.reshape(D_MODEL, 2 * MEM_WIDTH)
    w_out_b = g_out.reshape(MIX_WIDTH, D_MODEL)

    dout, dproj, dwmkv, dgmem, dwout, dvg, dvb, dws, dbs, dsink, drel, loss_vec, dgpost = _mix(
        parts, mem, x2, tgt2, v_norm_g, v_norm_b, w_sp, b_sp, attn_sinks, bias, w_out_b, post_norm_g, mem_norm_g,
        w_mkv_b, n_ex, seq)

    dx, small_a, small_b = _backward_projection(
        x2, dout, dproj, pre_norm_g, w_in_b, [dgpost, dgmem, dvg, dvb, dws, dbs, dsink, drel, loss_vec, buckets])

    shard_shapes = [w_mem_kv.shape[1:], w_out.shape[1:]]
    big = [g.reshape(N_CHIPS, 2, s[0] // 2, s[1]) for g, s in zip((dwmkv, dwout), shard_shapes)]
    (g_win, g_wmkv, g_wout), (ga, gb) = _reduce_gradients(dproj, h_b, big, [small_a, small_b], shard_arr)

    small_w = [pre_norm_g, post_norm_g, mem_norm_g, v_norm_g, v_norm_b, w_sp, b_spatial[0], attn_sinks, rel_t]
    small_m = [m_pre_norm_g, m_post_norm_g, m_mem_norm_g, m_v_norm_g, m_v_norm_b, m_w_spatial[0], m_b_spatial[0],
               m_attn_sinks, m_rel_t]
    small_v = [v_pre_norm_g, v_post_norm_g, v_mem_norm_g, v_v_norm_g, v_v_norm_b, v_w_spatial[0], v_b_spatial[0],
               v_attn_sinks, v_rel_t]
    big_out, small_out = _adamw_all(
        [g_win, g_wmkv, g_wout], [w_in_t, w_mem_kv[0], w_out[0]], [m_w_in_t, m_w_mem_kv[0], m_w_out[0]],
        [v_w_in_t, v_w_mem_kv[0], v_w_out[0]], ga, gb, small_w, small_m, small_v)
    n_small = len(small_w)

    outputs = [small_out[4 * n_small][0, 0], dx.reshape(x.shape)]
    for kind in range(4):
        s = small_out[kind * n_small:(kind + 1) * n_small]
        outputs += [s[0], s[1], s[2], jnp.transpose(big_out[0][kind])[None], big_out[1][kind][None], s[3], s[4],
                    s[5][None], s[6][None], s[7], jnp.transpose(s[8]), big_out[2][kind][None]]
    return tuple(outputs)
```

```python
import functools

import numpy as np
import jax
import jax.numpy as jnp
from jax import lax
from jax.experimental import pallas as pl
from jax.experimental.pallas import tpu as pltpu

F32 = jnp.float32
BF16 = jnp.bfloat16
MESH = pl.DeviceIdType.MESH

D_MODEL = 1024
CHUNK = 128
A_WIDTH = 512
A_GROUPS = 4
SWA_WIDTH = 256
KV_WIDTH = 128
MEM_WIDTH = 256
MEM_LEN = 256
MIX_WIDTH = 1024
IN_WIDTH = 2816
N_BUCKETS = 32
MAX_DISTANCE = 128
EPS = 1e-6
NEG = -1e30
QK_SCALE = 0.125
HALF_HEAD_PAIR = 64

ADAM_LR = 0.001
ADAM_B1 = 0.9
ADAM_B2 = 0.999
ADAM_EPS = 1e-08
ADAM_WD = 0.01
ADAM_STEP = 10

N_CHIPS = 4
TILE_CHUNKS = 2
TILE = TILE_CHUNKS * CHUNK
PROJ_TILE = 512
VMEM_LIMIT = 56 * 1024 * 1024

SMALL_A_ROWS = 8
ROW_LOSS = 4
ROW_WS = 0
ROW_BS = 512
ROW_SINK = 520
ROW_REL = 528
SMALL_B_ROWS = 536


def _mm(a, b):
    return lax.dot_general(a, b, (((1,), (0,)), ((), ())), preferred_element_type=F32)


def _mm_nt(a, b):
    return lax.dot_general(a, b, (((1,), (1,)), ((), ())), preferred_element_type=F32)


def _mm_tn(a, b):
    return lax.dot_general(a, b, (((0,), (0,)), ((), ())), preferred_element_type=F32)


def _bucket_map():
    qi = np.arange(CHUNK)[:, None]
    kj = np.arange(2 * CHUNK)[None, :]
    n = np.maximum(qi + CHUNK - kj, 0)
    max_exact = N_BUCKETS // 2
    large = max_exact + (np.log(np.maximum(n, 1) / max_exact) / np.log(MAX_DISTANCE / max_exact)
                         * (N_BUCKETS - max_exact)).astype(np.int32)
    large = np.minimum(large, N_BUCKETS - 1)
    return np.where(n < max_exact, n, large).astype(np.int32)


_GELU_C = 0.7978845608028654
_GELU_A = 0.044715
_GELU_K1 = 2.0 * _GELU_C
_GELU_K2 = 2.0 * _GELU_C * _GELU_A


def _gelu(x):
    x2 = x * x
    s = 1.0 / (1.0 + jnp.exp(x * (-_GELU_K1 - _GELU_K2 * x2)))
    return x * s, (s, x2)


def _gelu_grad(x, saved):
    s, x2 = saved
    return s + x * (s * (1.0 - s)) * (_GELU_K1 + 3.0 * _GELU_K2 * x2)


def _sigmoid(x):
    return 1.0 / (1.0 + jnp.exp(-x))


def _lane_lo(shape):
    return lax.broadcasted_iota(jnp.int32, shape, 1) < HALF_HEAD_PAIR


def _swa_variants(t):
    lo = _lane_lo(t.shape)
    tr = pltpu.roll(t, HALF_HEAD_PAIR, 1)
    zero = jnp.zeros_like(t)
    return (jnp.where(lo, t, zero).astype(BF16), jnp.where(lo, zero, tr).astype(BF16),
            jnp.where(lo, tr, zero).astype(BF16), jnp.where(lo, zero, t).astype(BF16))


def _swa_unvariants(d0, d1, d2, d3):
    lo = _lane_lo(d0.shape)
    zero = jnp.zeros_like(d0)
    rolled = jnp.where(lo, zero, d1) + jnp.where(lo, d2, zero)
    return jnp.where(lo, d0, zero) + jnp.where(lo, zero, d3) + pltpu.roll(rolled, HALF_HEAD_PAIR, 1)


def _mem_variants(t):
    out = []
    for pair in range(2):
        tp = t[:, pair * 128:(pair + 1) * 128]
        lo = _lane_lo(tp.shape)
        zero = jnp.zeros_like(tp)
        out.append(jnp.where(lo, tp, zero).astype(BF16))
        out.append(jnp.where(lo, zero, tp).astype(BF16))
    return out


def _mem_unvariants(d0, d1, d2, d3):
    lo = _lane_lo(d0.shape)
    return jnp.concatenate([jnp.where(lo, d0, d1), jnp.where(lo, d2, d3)], axis=-1)


def _softmax(logits, sinks):
    m = jnp.max(logits, axis=-1, keepdims=True)
    if sinks is not None:
        m = jnp.maximum(m, sinks)
    p = jnp.exp(logits - m)
    den = jnp.sum(p, axis=-1, keepdims=True)
    if sinks is None:
        return p * (1.0 / den), None
    es = jnp.exp(sinks - m)
    inv = 1.0 / (den + es)
    return p * inv, es * inv


def _band_valid(with_prev):
    qi = lax.broadcasted_iota(jnp.int32, (CHUNK, 2 * CHUNK), 0)
    kj = lax.broadcasted_iota(jnp.int32, (CHUNK, 2 * CHUNK), 1)
    in_cur = (kj >= CHUNK) & (kj - CHUNK <= qi)
    if not with_prev:
        return in_cur
    return in_cur | ((kj < CHUNK) & (kj > qi))


def _causal_weights(ws_ref):
    row = lax.broadcasted_iota(jnp.int32, (CHUNK, CHUNK), 0)
    col = lax.broadcasted_iota(jnp.int32, (CHUNK, CHUNK), 1)
    return [jnp.where(row >= col, ws_ref[g], 0.0).astype(BF16) for g in range(A_GROUPS)]


def _rows_to_lanes(a, n):
    return jnp.concatenate([a[c * CHUNK:(c + 1) * CHUNK] for c in range(n)], axis=1)


def _lanes_to_rows(a, n):
    w = a.shape[1] // n
    return jnp.concatenate([a[:, c * w:(c + 1) * w] for c in range(n)], axis=0)


def _stack_heads(pair01, pair23):
    return jnp.concatenate([pair01[:, :256], pair01[:, 256:], pair23[:, :256], pair23[:, 256:]], axis=0)


def _pair_heads(s, r):
    return (jnp.concatenate([s[0:r], s[r:2 * r]], axis=1), jnp.concatenate([s[2 * r:3 * r], s[3 * r:4 * r]], axis=1))


def _pair_operands(variants):
    return (jnp.concatenate(variants[0:2], axis=0), jnp.concatenate(variants[2:4], axis=0))


def _split_pair_grads(d_pairs):
    return d_pairs[0][:256], d_pairs[0][256:], d_pairs[1][:256], d_pairs[1][256:]


def _halves_bf16(a):
    return (a[:, :128].astype(BF16), a[:, 128:].astype(BF16))


def _group_a_forward(au, av, vg, vb, wm, bs_rows):
    gu, tu = _gelu(au)
    gv, tv = _gelu(av)
    ya, res = [], []
    for g in range(A_GROUPS):
        sl = slice(g * 128, (g + 1) * 128)
        xg = gv[:, sl]
        xc = xg - jnp.mean(xg, axis=-1, keepdims=True)
        rstd = lax.rsqrt(jnp.mean(xc * xc, axis=-1, keepdims=True) + EPS)
        xhat = xc * rstd
        vn = _rows_to_lanes((xhat * vg[:, sl] + vb[:, sl]).astype(BF16), TILE_CHUNKS)
        s = _lanes_to_rows(_mm(wm[g], vn), TILE_CHUNKS) + bs_rows[g]
        ya.append(gu[:, sl] * s)
        res.append((xhat, rstd, vn, s))
    return ya, dict(gu=gu, tu=tu, tv=tv, groups=res)


def _attention_logits(qp, k_pairs):
    return _stack_heads(_mm_nt(qp[0], k_pairs[0]), _mm_nt(qp[1], k_pairs[1]))


def _attention_out(p, v_pairs, r):
    pp = _pair_heads(p.astype(BF16), r)
    return jnp.concatenate([_mm(pp[0], v_pairs[0]), _mm(pp[1], v_pairs[1])], axis=-1), pp


def _attention_dprobs(do_pairs, v_pairs):
    return _stack_heads(_mm_nt(do_pairs[0], v_pairs[0]), _mm_nt(do_pairs[1], v_pairs[1]))


def _softmax_backward(p, dp):
    delta = jnp.sum(p * dp, axis=-1, keepdims=True)
    return p * (dp - delta), delta


def _attention_grads(dl, pp, do_pairs, qp, k_pairs, r):
    dlp = _pair_heads(dl.astype(BF16), r)
    dq = jnp.concatenate([_mm(dlp[0], k_pairs[0]), _mm(dlp[1], k_pairs[1])], axis=-1)
    dk = (_mm_tn(dlp[0], qp[0]), _mm_tn(dlp[1], qp[1]))
    dv = (_mm_tn(pp[0], do_pairs[0]), _mm_tn(pp[1], do_pairs[1]))
    return dq, dk, dv


def _tile_specs(n_tiles_ex, width):
    return pl.BlockSpec((TILE, width), lambda b, i: (b * n_tiles_ex + jnp.minimum(i, n_tiles_ex - 1), 0))


def _prev_chunk_spec(n_tiles_ex, width):
    def index(b, i):
        chunk = TILE_CHUNKS * jnp.minimum(i, n_tiles_ex - 1)
        return (b * n_tiles_ex * TILE_CHUNKS + jnp.maximum(chunk - 1, 0), 0)
    return pl.BlockSpec((CHUNK, width), index)


def _full_spec(shape):
    zeros = (0,) * len(shape)
    return pl.BlockSpec(shape, lambda *_: zeros)


SMEM_SPEC = pl.BlockSpec(memory_space=pltpu.SMEM)
ANY_SPEC = pl.BlockSpec(memory_space=pl.ANY)


def _fill_bias(rel_ref, bk_ref, out_ref):
    bk = bk_ref[...]
    for h in range(4):
        acc = jnp.zeros((CHUNK, 2 * CHUNK), F32)
        for b in range(N_BUCKETS):
            acc = jnp.where(bk == b, rel_ref[h, b], acc)
        for t, with_prev in enumerate((True, False)):
            out_ref[t, h * CHUNK:(h + 1) * CHUNK, :] = jnp.where(_band_valid(with_prev), acc, NEG)


PROJ_WIDTHS = (A_WIDTH, A_WIDTH, SWA_WIDTH, KV_WIDTH, KV_WIDTH, MEM_WIDTH, MIX_WIDTH)
PROJ_OFFSETS = tuple(int(v) for v in np.cumsum((0,) + PROJ_WIDTHS))


MXU_TILE = 256
HALF_WIDTH = IN_WIDTH // 2
PHASE_COLS = (HALF_WIDTH // MXU_TILE * MXU_TILE, IN_WIDTH - HALF_WIDTH // MXU_TILE * MXU_TILE)


def _phase_columns(phase, chip_x):
    if phase == 0:
        return 0 if chip_x == 0 else IN_WIDTH - PHASE_COLS[0]
    return PHASE_COLS[0] if chip_x == 0 else 0


def _phase_parts(phase, chip_x):
    start = _phase_columns(phase, chip_x)
    return [(k, PROJ_OFFSETS[k] - start) for k in range(len(PROJ_WIDTHS))
            if start <= PROJ_OFFSETS[k] and PROJ_OFFSETS[k + 1] <= start + PHASE_COLS[phase]]


def _gather_and_project(x2, g_pre, w_in_s, w_mkv_s, w_out_s, rel_bias_t, buckets, b_spatial, x_arr):
    n_tok = x2.shape[0]
    n_tiles = n_tok // PROJ_TILE
    last = n_tiles - 1
    shapes = [w_in_s.shape, w_mkv_s.shape, w_out_s.shape]
    n_w = len(shapes)

    def body(x_sref, x_ref, g_ref, win_hbm, wmkv_hbm, wout_hbm, rel_ref, bk_ref, bsp_ref, h_ref, *refs):
        part_refs, refs = refs[:len(PROJ_WIDTHS)], refs[len(PROJ_WIDTHS):]
        bias_ref, bs_ref, refs = refs[0], refs[1], refs[2:]
        gin_hbm, gmkv_hbm, gout_hbm, wg, stage_in, stage_mkv, stage_out, own_mkv, own_out, h_all = refs[:10]
        send_sems, recv_sems, local_sems = refs[10:]
        p, t = pl.program_id(0), pl.program_id(1)
        x, y, c = lax.axis_index("x"), lax.axis_index("y"), lax.axis_index("c")
        me, sibling = (x, y, c), (x, y, 1 - c)
        my_shard = 2 * x + y
        gathered = [wg, gmkv_hbm, gout_hbm]

        def half_rows(w, shard, half):
            rows = shapes[w][0] // 2
            if w == 0:
                return wg.at[pl.ds(pl.multiple_of(shard * shapes[0][0] + half * rows, 16), rows), :]
            return gathered[w].at[shard, pl.ds(half * rows, rows), :]

        def first(w, rel):
            src = half_rows(w, my_shard, c) if w == 0 else (own_mkv, own_out)[w - 1].at[
                pl.ds(c * (shapes[w][0] // 2), shapes[w][0] // 2), :]
            k = 3 * w + rel - 1
            return pltpu.make_async_remote_copy(
                src_ref=src, dst_ref=half_rows(w, my_shard, c), send_sem=send_sems.at[k], recv_sem=recv_sems.at[k],
                device_id=(x ^ (rel >> 1), y ^ (rel & 1), c), device_id_type=MESH)

        def landed(w, rel):
            k = 3 * w + rel - 1
            ref = half_rows(w, my_shard ^ rel, c)
            return pltpu.make_async_remote_copy(src_ref=ref, dst_ref=ref, send_sem=send_sems.at[k],
                                                recv_sem=recv_sems.at[k], device_id=me, device_id_type=MESH)

        def passed(w, rel, half, to):
            k = 9 + 3 * w + rel - 1
            ref = half_rows(w, my_shard ^ rel, half)
            return pltpu.make_async_remote_copy(src_ref=ref, dst_ref=ref, send_sem=send_sems.at[k],
                                                recv_sem=recv_sems.at[k], device_id=to, device_id_type=MESH)

        def pass_on(w, rels):
            for rel in rels:
                landed(w, rel).wait_recv()
                passed(w, rel, c, sibling).start()
            for rel in rels:
                passed(w, rel, 1 - c, me).wait_recv()

        own_stores = [pltpu.make_async_copy(own_mkv, gmkv_hbm.at[my_shard], local_sems.at[3]),
                      pltpu.make_async_copy(own_out, gout_hbm.at[my_shard], local_sems.at[4])]

        @pl.when((p == 0) & (t == 0))
        def _():
            half_rows_in = shapes[0][0] // 2
            halves = [pl.ds(pl.multiple_of(hc * half_rows_in, 8), half_rows_in) for hc in (c, 1 - c)]
            loads = [pltpu.make_async_copy(win_hbm.at[halves[0], :], stage_in.at[halves[0], :], local_sems.at[0]),
                     pltpu.make_async_copy(wmkv_hbm, stage_mkv, local_sems.at[1]),
                     pltpu.make_async_copy(wout_hbm, stage_out, local_sems.at[2]),
                     pltpu.make_async_copy(win_hbm.at[halves[1], :], stage_in.at[halves[1], :], local_sems.at[6])]
            for cp in (loads[0], loads[3], loads[1], loads[2]):
                cp.start()
            loads[0].wait()
            half_rows(0, my_shard, c)[...] = stage_in[halves[0], :].astype(BF16)
            for rel in (1, 2):
                first(0, rel).start()
            loads[3].wait()
            half_rows(0, my_shard, 1 - c)[...] = stage_in[halves[1], :].astype(BF16)
            loads[1].wait()
            loads[2].wait()
            own_mkv[...] = stage_mkv[...].astype(BF16)
            own_out[...] = stage_out[...].astype(BF16)
            for cp in own_stores:
                cp.start()
            _fill_bias(rel_ref, bk_ref, bias_ref)
            for g in range(A_GROUPS):
                bs_ref[g] = jnp.transpose(jnp.broadcast_to(bsp_ref[g:g + 1, :], (CHUNK, CHUNK)))
            pass_on(0, (1,))
            first(0, 3).start()

        @pl.when((p == 0) & (t == n_tiles // 2))
        def _():
            for w in (1, 2):
                for rel in (1, 2, 3):
                    first(w, rel).start()

        store = pltpu.make_async_copy(wg, gin_hbm, local_sems.at[5])

        @pl.when((p == 1) & (t == 0))
        def _():
            pass_on(0, (2, 3))
            store.start()

        @pl.when((p == 1) & (t == n_tiles // 2))
        def _():
            for w in (1, 2):
                pass_on(w, (1, 2, 3))

        tile_rows = pl.ds(pl.multiple_of(t * PROJ_TILE, PROJ_TILE), PROJ_TILE)

        def project(h, phase):
            start = jnp.where(x_sref[0] == 0, _phase_columns(phase, 0), _phase_columns(phase, 1))
            proj = _mm_nt(h, wg[pl.ds(pl.multiple_of(start, MXU_TILE), PHASE_COLS[phase]), :])
            for chip_x in range(2):
                @pl.when(x_sref[0] == chip_x)
                def _():
                    for k, lo in _phase_parts(phase, chip_x):
                        part_refs[k][...] = proj[:, lo:lo + PROJ_WIDTHS[k]].astype(BF16)

        @pl.when(p == 0)
        def _():
            xv = x_ref[...]
            r = lax.rsqrt(jnp.mean(xv * xv, axis=-1, keepdims=True) + EPS)
            h = (xv * r * g_ref[...]).astype(BF16)
            h_ref[...] = h
            h_all[tile_rows, :] = h
            project(h, 0)

        @pl.when(p == 1)
        def _():
            project(h_all[tile_rows, :], 1)

        @pl.when((p == 1) & (t == last))
        def _():
            for w in range(n_w):
                for rel in (1, 2, 3):
                    first(w, rel).wait_send()
                    passed(w, rel, c, sibling).wait_send()
            for cp in own_stores:
                cp.wait()
            store.wait()

    def written_in(k):
        phase_on = [next(ph for ph in range(2) if k in dict(_phase_parts(ph, chip_x))) for chip_x in range(2)]

        def index(p, t, xs):
            phase = jnp.where(xs[0] == 0, phase_on[0], phase_on[1])
            return (jnp.where(p == phase, t, jnp.where(p < phase, 0, last)), 0)
        return index

    part_specs = [pl.BlockSpec((PROJ_TILE, PROJ_WIDTHS[k]), written_in(k)) for k in range(len(PROJ_WIDTHS))]
    vmem = pltpu.VMEM
    out = pl.pallas_call(
        body, name="gather_and_project",
        out_shape=[jax.ShapeDtypeStruct((n_tok, D_MODEL), BF16)]
        + [jax.ShapeDtypeStruct((n_tok, w), BF16) for w in PROJ_WIDTHS]
        + [jax.ShapeDtypeStruct((2, 4 * CHUNK, 2 * CHUNK), F32), jax.ShapeDtypeStruct((A_GROUPS, CHUNK, CHUNK), F32)]
        + [jax.ShapeDtypeStruct((N_CHIPS * shapes[0][0], shapes[0][1]), BF16)]
        + [jax.ShapeDtypeStruct((N_CHIPS,) + s, BF16) for s in shapes[1:]],
        grid_spec=pltpu.PrefetchScalarGridSpec(
            num_scalar_prefetch=1, grid=(2, n_tiles),
            in_specs=[pl.BlockSpec((PROJ_TILE, D_MODEL), lambda p, t, xs: (jnp.where(p == 0, t, last), 0)),
                      pl.BlockSpec((1, D_MODEL), lambda p, t, xs: (0, 0)), ANY_SPEC, ANY_SPEC, ANY_SPEC, SMEM_SPEC,
                      pl.BlockSpec(buckets.shape, lambda p, t, xs: (0, 0)),
                      pl.BlockSpec(b_spatial.shape, lambda p, t, xs: (0, 0))],
            out_specs=[pl.BlockSpec((PROJ_TILE, D_MODEL), lambda p, t, xs: (jnp.where(p == 0, t, last), 0))]
            + part_specs + [pl.BlockSpec((2, 4 * CHUNK, 2 * CHUNK), lambda p, t, xs: (0, 0, 0)),
                            pl.BlockSpec((A_GROUPS, CHUNK, CHUNK), lambda p, t, xs: (0, 0, 0))] + [ANY_SPEC] * 3,
            scratch_shapes=[vmem((N_CHIPS * shapes[0][0], shapes[0][1]), BF16), vmem(shapes[0], F32),
                            vmem(shapes[1], F32), vmem(shapes[2], F32), vmem(shapes[1], BF16), vmem(shapes[2], BF16),
                            vmem((n_tok, D_MODEL), BF16),
                            pltpu.SemaphoreType.DMA((18,)), pltpu.SemaphoreType.DMA((18,)),
                            pltpu.SemaphoreType.DMA((7,))]),
        compiler_params=pltpu.CompilerParams(vmem_limit_bytes=VMEM_LIMIT),
    )(x_arr, x2, g_pre, w_in_s, w_mkv_s, w_out_s, rel_bias_t, buckets, b_spatial)
    n_parts = len(PROJ_WIDTHS)
    return out[0], list(out[1:1 + n_parts]), out[3 + n_parts:], out[1 + n_parts], out[2 + n_parts]


def _load_chunk(j, i, sk_ref, sv_ref, skp_ref, svp_ref):
    rows = slice(j * CHUNK, (j + 1) * CHUNK)
    if j == 0:
        k_prev, v_prev, table = skp_ref[...], svp_ref[...], jnp.where(i > 0, 0, 1)
    else:
        prev = slice((j - 1) * CHUNK, j * CHUNK)
        k_prev, v_prev, table = sk_ref[prev, :], sv_ref[prev, :], 0
    k_pairs = _pair_operands(_swa_variants(jnp.concatenate([k_prev, sk_ref[rows, :]], axis=0).astype(F32)))
    v_pairs = _pair_operands(_swa_variants(jnp.concatenate([v_prev, sv_ref[rows, :]], axis=0).astype(F32)))
    return rows, k_pairs, v_pairs, table


def _tile_constants(ws_ref, bs_ref, sink_ref):
    wm = _causal_weights(ws_ref)
    bs_rows = [jnp.concatenate([bs_ref[g]] * TILE_CHUNKS, axis=0) for g in range(A_GROUPS)]
    sink_col = jnp.max(jnp.concatenate([jnp.full((CHUNK, 128), sink_ref[0, h], F32) for h in range(4)] * TILE_CHUNKS,
                                       axis=0), axis=-1, keepdims=True)
    return wm, bs_rows, sink_col


def _mix(parts, mem, x2, tgt2, v_g, v_b, w_sp, b_sp, sinks, bias, w_out, g_post, g_mem, w_mkv, g_pre, w_in_t, buckets,
         n_ex, seq):
    n_tiles_ex = seq // TILE
    n_tok = n_ex * seq
    au, av, sq, sk, sv, mq, z = parts
    col = dict(zip(("au", "av", "sq", "sk", "sv", "mq", "z"),
                   (slice(PROJ_OFFSETS[k], PROJ_OFFSETS[k + 1]) for k in range(len(PROJ_WIDTHS)))))
    before_kv, after_kv = slice(0, col["sk"].start), slice(col["sv"].stop, IN_WIDTH)

    def body(au_ref, av_ref, sq_ref, sk_ref, sv_ref, skp_ref, svp_ref, mq_ref, z_ref, mem_ref, x_ref, tgt_ref,
             vg_ref, vb_ref, ws_ref, bs_ref, sink_ref, bias_ref, wout_ref, gpost_ref, gmem_ref, wmkv_ref,
             xl_ref, gpre_ref, bk_ref, win_hbm,
             dx_ref, dproj_ref, dwmkv_ref, dwout_ref, a_ref, b_ref,
             carry_dp, carry_k, carry_v, memn_s, mem_ops, dmkv_s, carry_dout, win_s, win_sem,
             dgpre_ref, dgpost_ref, dgmem_ref, dvg_ref, dvb_ref, dws_ref, dbs_ref, dsink_ref, drel_ref, loss_ref):
        b, i = pl.program_id(0), pl.program_id(1)
        win_load = pltpu.make_async_copy(win_hbm, win_s, win_sem)

        @pl.when((b == 0) & (i == 0))
        def _():
            win_load.start()
            for ref in (dwmkv_ref, dwout_ref, dgpre_ref, dgpost_ref, dgmem_ref, dvg_ref, dvb_ref, dws_ref, dbs_ref,
                        dsink_ref, drel_ref, loss_ref):
                ref[...] = jnp.zeros_like(ref)

        def normalized_mem():
            m = mem_ref[0]
            return m * lax.rsqrt(jnp.mean(m * m, axis=-1, keepdims=True) + EPS)

        @pl.when(i == 0)
        def _():
            memn_s[...] = (normalized_mem() * gmem_ref[...]).astype(BF16)
            mkv = _mm(memn_s[...], wmkv_ref[...])
            for k, pair in enumerate(_pair_operands(_mem_variants(mkv[:, :MEM_WIDTH]))
                                     + _pair_operands(_mem_variants(mkv[:, MEM_WIDTH:]))):
                mem_ops[k] = pair
            dmkv_s[...] = jnp.zeros_like(dmkv_s)
            carry_k[...] = jnp.zeros_like(carry_k)
            carry_v[...] = jnp.zeros_like(carry_v)

        @pl.when(i > 0)
        def _():
            dproj_ref[:, before_kv] = carry_dp[:, before_kv]
            dproj_ref[:, after_kv] = carry_dp[:, after_kv]

        @pl.when(i < n_tiles_ex)
        def _():
            wm, bs_rows, sink_col = _tile_constants(ws_ref, bs_ref, sink_ref)
            mk_pairs, mv_pairs = (mem_ops[0], mem_ops[1]), (mem_ops[2], mem_ops[3])
            vg = vg_ref[...]

            au_v, av_v = au_ref[...].astype(F32), av_ref[...].astype(F32)
            ya, res = _group_a_forward(au_v, av_v, vg, vb_ref[...], wm, bs_rows)
            swa, logits, yb = [], [], []
            for j in range(TILE_CHUNKS):
                rows, k_pairs, v_pairs, table = _load_chunk(j, i, sk_ref, sv_ref, skp_ref, svp_ref)
                qp = _halves_bf16(sq_ref[rows, :] * QK_SCALE)
                logits.append(_attention_logits(qp, k_pairs) + bias_ref[table])
                swa.append([rows, k_pairs, v_pairs, qp])
            p_swa, sink_p = _softmax(jnp.concatenate(logits, axis=0), sink_col)
            for j in range(TILE_CHUNKS):
                out, pp = _attention_out(p_swa[j * 4 * CHUNK:(j + 1) * 4 * CHUNK], swa[j][2], CHUNK)
                yb.append(out)
                swa[j].append(pp)
            mqp = _halves_bf16(mq_ref[...] * QK_SCALE)
            pm, _ = _softmax(_attention_logits(mqp, mk_pairs), None)
            yc, ppm = _attention_out(pm, mv_pairs, TILE)
            ycat = jnp.concatenate(ya + [jnp.concatenate(yb, axis=0), yc], axis=-1)

            zv = z_ref[...].astype(F32)
            sig = _sigmoid(zv)
            sz = zv * sig
            y_b = (ycat * sz).astype(BF16)
            o = _mm(y_b, wout_ref[...])
            r2 = lax.rsqrt(jnp.mean(o * o, axis=-1, keepdims=True) + EPS)
            nrm = o * r2
            gp = gpost_ref[...]
            diff = x_ref[...] + nrm * gp - tgt_ref[...]
            loss_ref[...] += jnp.sum(diff * diff) * (0.5 / D_MODEL)
            dout = diff * (1.0 / D_MODEL)
            carry_dout[lax.rem(i, 2)] = dout
            dgpost_ref[...] += jnp.sum(dout * nrm, axis=0, keepdims=True)
            dn = dout * gp
            do_b = (r2 * (dn - nrm * jnp.mean(dn * nrm, axis=-1, keepdims=True))).astype(BF16)
            dwout_ref[...] += _mm_tn(y_b, do_b)
            dy = _mm_nt(do_b, wout_ref[...])
            carry_dp[:, col["z"]] = (dy * ycat * (sig + sz * (1.0 - sig))).astype(BF16)
            dyc = dy * sz

            dgu, dgv = [], []
            for g in range(A_GROUPS):
                sl = slice(g * 128, (g + 1) * 128)
                xhat, rstd, vn, s = res["groups"][g]
                dya = dyc[:, sl]
                dgu.append(dya * s)
                ds = dya * res["gu"][:, sl]
                dbs_ref[:, sl] += sum(ds[c * CHUNK:(c + 1) * CHUNK] for c in range(TILE_CHUNKS))
                ds_b = _rows_to_lanes(ds.astype(BF16), TILE_CHUNKS)
                dws_ref[g] += _mm_nt(ds_b, vn)
                dvn = _lanes_to_rows(_mm_tn(wm[g], ds_b), TILE_CHUNKS)
                dvg_ref[:, sl] += jnp.sum(dvn * xhat, axis=0, keepdims=True)
                dvb_ref[:, sl] += jnp.sum(dvn, axis=0, keepdims=True)
                dxh = dvn * vg[:, sl]
                dgv.append(rstd * (dxh - jnp.mean(dxh, axis=-1, keepdims=True)
                                   - xhat * jnp.mean(dxh * xhat, axis=-1, keepdims=True)))
            carry_dp[:, col["au"]] = (jnp.concatenate(dgu, axis=-1) * _gelu_grad(au_v, res["tu"])).astype(BF16)
            carry_dp[:, col["av"]] = (jnp.concatenate(dgv, axis=-1) * _gelu_grad(av_v, res["tv"])).astype(BF16)

            do_pairs = [_halves_bf16(dyc[rows, A_WIDTH:A_WIDTH + SWA_WIDTH]) for rows, *_ in swa]
            dl_swa, delta = _softmax_backward(p_swa, jnp.concatenate(
                [_attention_dprobs(do_pairs[j], swa[j][2]) for j in range(TILE_CHUNKS)], axis=0))
            sink_terms = sink_p * delta
            lane4 = lax.broadcasted_iota(jnp.int32, (1, 128), 1)
            dsink_vec = jnp.zeros((1, 128), F32)
            for h in range(4):
                head_sum = sum(jnp.sum(sink_terms[(4 * j + h) * CHUNK:(4 * j + h + 1) * CHUNK])
                               for j in range(TILE_CHUNKS))
                dsink_vec = dsink_vec + jnp.where(lane4 == h, -head_sum, 0.0)
            dsink_ref[...] += dsink_vec
            drel_ref[...] += sum(dl_swa[j * 4 * CHUNK:(j + 1) * 4 * CHUNK] for j in range(TILE_CHUNKS))
            dk_parts, dv_parts = [], []
            for j, (rows, k_pairs, v_pairs, qp, pp) in enumerate(swa):
                dq, dk, dv = _attention_grads(dl_swa[j * 4 * CHUNK:(j + 1) * 4 * CHUNK], pp, do_pairs[j], qp, k_pairs,
                                              CHUNK)
                carry_dp[rows, col["sq"]] = (dq * QK_SCALE).astype(BF16)
                dk_parts.append(_swa_unvariants(*_split_pair_grads(dk)))
                dv_parts.append(_swa_unvariants(*_split_pair_grads(dv)))

            dc_pairs = _halves_bf16(dyc[:, A_WIDTH + SWA_WIDTH:])
            dl_mem, _ = _softmax_backward(pm, _attention_dprobs(dc_pairs, mv_pairs))
            dmq, dmk, dmv = _attention_grads(dl_mem, ppm, dc_pairs, mqp, mk_pairs, TILE)
            carry_dp[:, col["mq"]] = (dmq * QK_SCALE).astype(BF16)
            dmkv_s[...] += jnp.concatenate([_mem_unvariants(*_split_pair_grads(dmk)),
                                            _mem_unvariants(*_split_pair_grads(dmv))], axis=-1)

            for parts_c, carry, cols in ((dk_parts, carry_k, col["sk"]), (dv_parts, carry_v, col["sv"])):
                @pl.when(i > 0)
                def _():
                    dproj_ref[:, cols] = (carry[...] + jnp.concatenate(
                        [jnp.zeros((TILE - CHUNK, KV_WIDTH), F32), parts_c[0][:CHUNK]], axis=0)).astype(BF16)
                new = [parts_c[0][CHUNK:]]
                for j in range(1, TILE_CHUNKS):
                    new[-1] = new[-1] + parts_c[j][:CHUNK]
                    new.append(parts_c[j][CHUNK:])
                carry[...] = jnp.concatenate(new, axis=0)

        @pl.when(i == n_tiles_ex)
        def _():
            dproj_ref[:, col["sk"]] = carry_k[...].astype(BF16)
            dproj_ref[:, col["sv"]] = carry_v[...].astype(BF16)
            d_b = dmkv_s[...].astype(BF16)
            dwmkv_ref[...] += _mm_tn(memn_s[...], d_b)
            dgmem_ref[...] += jnp.sum(_mm_nt(d_b, wmkv_ref[...]) * normalized_mem(), axis=0, keepdims=True)

        @pl.when((b == 0) & (i == 1))
        def _():
            win_load.wait()

        @pl.when(i > 0)
        def _():
            xv = xl_ref[...]
            r = lax.rsqrt(jnp.mean(xv * xv, axis=-1, keepdims=True) + EPS)
            xn = xv * r
            dh = _mm(dproj_ref[...], win_s[...])
            dgpre_ref[...] += jnp.sum(dh * xn, axis=0, keepdims=True)
            dhg = dh * gpre_ref[...]
            dx_ref[...] = (r * (dhg - xn * jnp.mean(dhg * xn, axis=-1, keepdims=True))
                           + carry_dout[lax.rem(i + 1, 2)])

        @pl.when((b == n_ex - 1) & (i == n_tiles_ex))
        def _():
            _fill_small_grads(dgpre_ref, dgpost_ref, dgmem_ref, dvg_ref, dvb_ref, dws_ref, dbs_ref, dsink_ref,
                              drel_ref, loss_ref, bk_ref, a_ref, b_ref)

    tile = functools.partial(_tile_specs, n_tiles_ex)
    prev = functools.partial(_prev_chunk_spec, n_tiles_ex)
    late = lambda width: pl.BlockSpec((TILE, width), lambda b, i: (b * n_tiles_ex + jnp.maximum(i - 1, 0), 0))
    vmem_f32 = lambda *shape: pltpu.VMEM(shape, F32)
    return pl.pallas_call(
        body, name="mix", grid=(n_ex, n_tiles_ex + 1),
        out_shape=[jax.ShapeDtypeStruct((n_tok, D_MODEL), F32), jax.ShapeDtypeStruct((n_tok, IN_WIDTH), BF16),
                   jax.ShapeDtypeStruct((D_MODEL, 2 * MEM_WIDTH), F32), jax.ShapeDtypeStruct((MIX_WIDTH, D_MODEL), F32),
                   jax.ShapeDtypeStruct((SMALL_A_ROWS, D_MODEL), F32), jax.ShapeDtypeStruct((SMALL_B_ROWS, 128), F32)],
        in_specs=[tile(A_WIDTH), tile(A_WIDTH), tile(SWA_WIDTH), tile(KV_WIDTH), tile(KV_WIDTH),
                  prev(KV_WIDTH), prev(KV_WIDTH), tile(MEM_WIDTH), tile(MIX_WIDTH),
                  pl.BlockSpec((1, MEM_LEN, D_MODEL), lambda b, i: (b, 0, 0)),
                  tile(D_MODEL), tile(D_MODEL),
                  _full_spec((1, A_WIDTH)), _full_spec((1, A_WIDTH)), _full_spec((A_GROUPS, CHUNK, CHUNK)),
                  _full_spec((A_GROUPS, CHUNK, CHUNK)), SMEM_SPEC, _full_spec((2, 4 * CHUNK, 2 * CHUNK)),
                  _full_spec((MIX_WIDTH, D_MODEL)), _full_spec((1, D_MODEL)), _full_spec((1, D_MODEL)),
                  _full_spec((D_MODEL, 2 * MEM_WIDTH)),
                  late(D_MODEL), _full_spec((1, D_MODEL)), _full_spec((CHUNK, 2 * CHUNK)), ANY_SPEC],
        out_specs=[late(D_MODEL), late(IN_WIDTH), _full_spec((D_MODEL, 2 * MEM_WIDTH)),
                   _full_spec((MIX_WIDTH, D_MODEL)), _full_spec((SMALL_A_ROWS, D_MODEL)),
                   _full_spec((SMALL_B_ROWS, 128))],
        scratch_shapes=[pltpu.VMEM((TILE, IN_WIDTH), BF16), pltpu.VMEM((TILE, KV_WIDTH), F32),
                        pltpu.VMEM((TILE, KV_WIDTH), F32), pltpu.VMEM((MEM_LEN, D_MODEL), BF16),
                        pltpu.VMEM((4, 2 * MEM_LEN, 128), BF16), pltpu.VMEM((MEM_LEN, 2 * MEM_WIDTH), F32),
                        pltpu.VMEM((2, TILE, D_MODEL), F32), pltpu.VMEM((IN_WIDTH, D_MODEL), BF16),
                        pltpu.SemaphoreType.DMA,
                        vmem_f32(1, D_MODEL), vmem_f32(1, D_MODEL), vmem_f32(1, D_MODEL), vmem_f32(1, A_WIDTH),
                        vmem_f32(1, A_WIDTH), vmem_f32(A_GROUPS, CHUNK, CHUNK), vmem_f32(CHUNK, A_WIDTH),
                        vmem_f32(1, 128), vmem_f32(4 * CHUNK, 2 * CHUNK), vmem_f32(1, 128)],
        compiler_params=pltpu.CompilerParams(vmem_limit_bytes=VMEM_LIMIT),
    )(au, av, sq, sk, sv, sk, sv, mq, z, mem, x2, tgt2, v_g, v_b, w_sp, b_sp, sinks, bias, w_out, g_post, g_mem,
      w_mkv, x2, g_pre, buckets, w_in_t)


def _fill_small_grads(dgpre_ref, dgpost_ref, dgmem_ref, dvg_ref, dvb_ref, dws_ref, dbs_ref, dsink_ref, drel_ref,
                      loss_ref, bk_ref, a_ref, b_ref):
    a_ref[...] = jnp.zeros_like(a_ref)
    b_ref[...] = jnp.zeros_like(b_ref)
    a_ref[0:1, :] = dgpre_ref[...]
    a_ref[1:2, :] = dgpost_ref[...]
    a_ref[2:3, :] = dgmem_ref[...]
    a_ref[3:4, :] = jnp.concatenate([dvg_ref[...], dvb_ref[...]], axis=-1)
    a_ref[ROW_LOSS:ROW_LOSS + 1, 0:128] = loss_ref[...]
    row = lax.broadcasted_iota(jnp.int32, (CHUNK, CHUNK), 0)
    col = lax.broadcasted_iota(jnp.int32, (CHUNK, CHUNK), 1)
    for g in range(A_GROUPS):
        b_ref[ROW_WS + g * CHUNK:ROW_WS + (g + 1) * CHUNK, :] = jnp.where(row >= col, dws_ref[g], 0.0)
        by_token = jnp.transpose(dbs_ref[:, g * 128:(g + 1) * 128])
        b_ref[ROW_BS + g:ROW_BS + g + 1, :] = jnp.sum(by_token, axis=0, keepdims=True)
    b_ref[ROW_SINK:ROW_SINK + 1, :] = dsink_ref[...]
    bk = bk_ref[...]
    rel_row = lax.broadcasted_iota(jnp.int32, (8, 128), 0)
    rel_col = lax.broadcasted_iota(jnp.int32, (8, 128), 1)
    rel = jnp.zeros((8, 128), F32)
    for h in range(4):
        acc = drel_ref[h * CHUNK:(h + 1) * CHUNK, :]
        for b in range(N_BUCKETS):
            rel = jnp.where((rel_row == h) & (rel_col == b), jnp.sum(jnp.where(bk == b, acc, 0.0)), rel)
    b_ref[ROW_REL:ROW_REL + 8, :] = rel


SHARD_ROWS = IN_WIDTH // N_CHIPS
SHARD_WINDOW = 768
SHARD_HALF = SHARD_ROWS // 2
DWIN_TILE = 2048
N_REL = N_CHIPS - 1


def _shard_window_start(shard):
    return (shard * SHARD_ROWS // 128) * 128


def _reduce_gradients(dproj, h, big, small, shard_arr):
    n_tok = h.shape[0]
    tile = min(DWIN_TILE, n_tok)
    n_sub = n_tok // tile
    last = N_CHIPS - 1
    n_big, n_small = len(big), len(small)
    big_half = [g.shape[2:] for g in big]
    sem_big_d2d = 2 * N_CHIPS
    sem_big_ici = sem_big_d2d + n_big
    sem_big_swap = sem_big_ici + N_REL * n_big
    sem_small_d2d = sem_big_swap + n_big
    sem_small_ici = sem_small_d2d + n_small
    n_sems = sem_small_ici + N_REL * n_small
    loc_small = n_big
    loc_out_win = loc_small + n_small
    loc_out_big = loc_out_win + 2
    loc_out_small = loc_out_big + 2 * n_big
    n_local = loc_out_small + n_small

    def relation_of_slot(s):
        return (s + 2) % N_REL + 1

    def shard_of_slot(s, my_shard):
        return my_shard ^ jnp.where(s == last, 0, relation_of_slot(s))

    def body(shard_ref, dp_ref, h_hbm, *refs):
        h_vmem, h_sem, refs = refs[-2], refs[-1], refs[:-2]
        big_hbm, refs = refs[:n_big], refs[n_big:]
        small_hbm, refs = refs[:n_small], refs[n_small:]
        out_hbm, refs = refs[0], refs[1:]
        big_out, refs = refs[:n_big], refs[n_big:]
        small_out, refs = refs[:n_small], refs[n_small:]
        part, recv_d2d, send_ici, recv_ici, mine_buf, other_buf = refs[:6]
        refs = refs[6:]
        big_own, big_recv, big_send, big_land, big_mine, big_other = (
            refs[k * n_big:(k + 1) * n_big] for k in range(6))
        refs = refs[6 * n_big:]
        small_own, small_recv, small_all = (refs[k * n_small:(k + 1) * n_small] for k in range(3))
        send_sems, recv_sems, local_sems = refs[3 * n_small:]

        s, t = pl.program_id(0), pl.program_id(1)
        x, y, c = lax.axis_index("x"), lax.axis_index("y"), lax.axis_index("c")
        my_chip = 2 * x + y
        sibling = (x, y, 1 - c)
        my_rows = pl.ds(pl.multiple_of(c * SHARD_HALF, 8), SHARD_HALF)
        other_rows = pl.ds(pl.multiple_of((1 - c) * SHARD_HALF, 8), SHARD_HALF)

        def remote(src, dst, k, to):
            return pltpu.make_async_remote_copy(src_ref=src, dst_ref=dst, send_sem=send_sems.at[k],
                                                recv_sem=recv_sems.at[k], device_id=to, device_id_type=MESH)

        def chip_at(rel):
            return (x ^ (rel >> 1), y ^ (rel & 1), c)

        def to_sibling(k):
            return remote(part.at[k % 2, other_rows, :], recv_d2d.at[k], k, sibling)

        def to_chip(k):
            return remote(send_ici.at[k], recv_ici.at[k], N_CHIPS + k, chip_at(relation_of_slot(k)))

        swap = remote(mine_buf, other_buf, 2 * N_CHIPS - 1, sibling)
        big_load = [pltpu.make_async_copy(big_hbm[w].at[:, pl.ds(c, 1)], big_own[w], local_sems.at[w])
                    for w in range(n_big)]
        big_to_sibling = [remote(big_hbm[w].at[:, pl.ds(1 - c, 1)], big_recv[w], sem_big_d2d + w, sibling)
                          for w in range(n_big)]
        big_to_chip = [[remote(big_send[w].at[k], big_land[w].at[k], sem_big_ici + N_REL * w + k, chip_at(k + 1))
                        for k in range(N_REL)] for w in range(n_big)]
        big_swap = [remote(big_mine[w], big_other[w], sem_big_swap + w, sibling) for w in range(n_big)]
        small_load = [pltpu.make_async_copy(small_hbm[i], small_own[i], local_sems.at[loc_small + i])
                      for i in range(n_small)]
        small_to_sibling = [remote(small_hbm[i], small_recv[i], sem_small_d2d + i, sibling) for i in range(n_small)]
        small_to_chip = [[remote(small_all[i].at[my_chip], small_all[i].at[my_chip],
                                 sem_small_ici + N_REL * i + k, chip_at(k + 1))
                          for k in range(N_REL)] for i in range(n_small)]

        h_loads = [pltpu.make_async_copy(h_hbm.at[rows, :], h_vmem.at[rows, :], h_sem.at[k]) for k, rows in enumerate(
            [pl.ds(0, tile)] + ([pl.ds(tile, n_tok - tile)] if n_sub > 1 else []))]

        @pl.when((s == 0) & (t == 0))
        def _():
            for cp in h_loads + big_load + big_to_sibling + small_load + small_to_sibling:
                cp.start()
            h_loads[0].wait()

        if n_sub > 1:
            @pl.when((s == 0) & (t == 1))
            def _():
                h_loads[1].wait()

        @pl.when((s == 0) & (t == n_sub - 1))
        def _():
            for cp in big_load + small_load:
                cp.wait()
            for cp in big_to_sibling + small_to_sibling:
                cp.wait_recv()
                cp.wait_send()
            for w in range(n_big):
                for k in range(N_REL):
                    shard = my_chip ^ (k + 1)
                    big_send[w][k] = (big_own[w][shard, 0] + big_recv[w][shard, 0]).astype(BF16)
                    big_to_chip[w][k].start()
            for i in range(n_small):
                small_all[i][my_chip] = small_own[i][...] + small_recv[i][...]
                for k in range(N_REL):
                    small_to_chip[i][k].start()

        @pl.when((s > 0) & (t == jnp.where(s == last, 0, min(1, n_sub - 1))))
        def _():
            k = s - 1
            cp = to_sibling(k)
            cp.wait_recv()
            cp.wait_send()
            send_ici[k] = (part[k % 2, my_rows, :] + recv_d2d[k]).astype(BF16)
            to_chip(k).start()

        def big_rows(w, half):
            rows = big_half[w][0]
            return big_out[w].at[pl.ds(pl.multiple_of(half * rows, 8), rows), :]

        big_store_mine = [pltpu.make_async_copy(big_mine[w], big_rows(w, c), local_sems.at[loc_out_big + 2 * w])
                          for w in range(n_big)]
        big_store_other = [pltpu.make_async_copy(big_other[w], big_rows(w, 1 - c),
                                                 local_sems.at[loc_out_big + 2 * w + 1]) for w in range(n_big)]
        small_store = [pltpu.make_async_copy(small_all[i], small_out[i], local_sems.at[loc_out_small + i])
                       for i in range(n_small)]

        @pl.when((s == last) & (t == 0))
        def _():
            for w in range(n_big):
                total = big_own[w][my_chip, 0] + big_recv[w][my_chip, 0]
                for k in range(N_REL):
                    big_to_chip[w][k].wait_recv()
                    total = total + big_land[w][k].astype(F32)
                big_mine[w][...] = total
                big_swap[w].start()
                big_store_mine[w].start()
            for i in range(n_small):
                for k in range(N_REL):
                    small_to_chip[i][k].wait_recv()
                small_store[i].start()

        r = _mm_tn(dp_ref[...], h_vmem[pl.ds(pl.multiple_of(t * tile, tile), tile), :])
        odd = shard_of_slot(s, shard_ref[0]) % 2
        for parity in range(2):
            rows = r[64 * parity:64 * parity + SHARD_ROWS]

            @pl.when((odd == parity) & (t == 0))
            def _():
                part[s % 2] = rows

            @pl.when((odd == parity) & (t > 0))
            def _():
                part[s % 2] += rows

        @pl.when(t == n_sub - 1)
        def _():
            to_sibling(s).start()

        @pl.when((s == last) & (t == n_sub - 1))
        def _():
            cp = to_sibling(last)
            cp.wait_recv()
            cp.wait_send()
            total = part[last % 2, my_rows, :] + recv_d2d[last]
            for k in range(last):
                to_chip(k).wait_recv()
                total = total + recv_ici[k].astype(F32)
            mine_buf[...] = total
            swap.start()
            out_mine = pltpu.make_async_copy(mine_buf, out_hbm.at[my_rows, :], local_sems.at[0])
            out_mine.start()
            swap.wait_recv()
            out_other = pltpu.make_async_copy(other_buf, out_hbm.at[other_rows, :], local_sems.at[1])
            out_other.start()
            for w in range(n_big):
                big_swap[w].wait_recv()
                big_store_other[w].start()
            stores = [out_mine, out_other] + big_store_mine + big_store_other + small_store
            for k in range(last):
                to_chip(k).wait_send()
            swap.wait_send()
            for w in range(n_big):
                for k in range(N_REL):
                    big_to_chip[w][k].wait_send()
                big_swap[w].wait_send()
            for i in range(n_small):
                for k in range(N_REL):
                    small_to_chip[i][k].wait_send()
            for cp in stores:
                cp.wait()

    half = (SHARD_HALF, D_MODEL)
    vmem = pltpu.VMEM
    scratch = [vmem((2, SHARD_ROWS, D_MODEL), F32), vmem((N_CHIPS,) + half, F32),
               vmem((N_REL,) + half, BF16), vmem((N_REL,) + half, BF16), vmem(half, F32), vmem(half, F32)]
    scratch += [vmem((N_CHIPS, 1) + hs, F32) for hs in big_half] * 2
    scratch += [vmem((N_REL,) + hs, BF16) for hs in big_half] * 2
    scratch += [vmem(hs, F32) for hs in big_half] * 2
    scratch += [vmem(a.shape, F32) for a in small] * 2 + [vmem((N_CHIPS,) + a.shape, F32) for a in small]
    scratch += [pltpu.SemaphoreType.DMA((n_sems,)), pltpu.SemaphoreType.DMA((n_sems,)),
                pltpu.SemaphoreType.DMA((n_local,)), vmem(h.shape, BF16), pltpu.SemaphoreType.DMA((2,))]
    n_hbm = n_big + n_small
    out = pl.pallas_call(
        body, name="reduce_gradients",
        out_shape=[jax.ShapeDtypeStruct((SHARD_ROWS, D_MODEL), F32)]
        + [jax.ShapeDtypeStruct((2 * hs[0], hs[1]), F32) for hs in big_half]
        + [jax.ShapeDtypeStruct((N_CHIPS,) + a.shape, F32) for a in small],
        grid_spec=pltpu.PrefetchScalarGridSpec(
            num_scalar_prefetch=1, grid=(N_CHIPS, n_sub),
            in_specs=[pl.BlockSpec((pl.Element(tile), pl.Element(SHARD_WINDOW)),
                                   lambda s, t, m: (t * tile, _shard_window_start(shard_of_slot(s, m[0])))),
                      ANY_SPEC] + [ANY_SPEC] * n_hbm,
            out_specs=[ANY_SPEC] * (1 + n_hbm),
            scratch_shapes=scratch),
        compiler_params=pltpu.CompilerParams(vmem_limit_bytes=VMEM_LIMIT),
    )(shard_arr, dproj, h, *big, *small)
    return out[:1 + n_big], out[1 + n_big:]


def _adamw(w, g, m, v):
    m2 = ADAM_B1 * m + (1.0 - ADAM_B1) * g
    v2 = ADAM_B2 * v + (1.0 - ADAM_B2) * (g * g)
    m_hat = m2 / (1.0 - ADAM_B1 ** ADAM_STEP)
    v_hat = v2 / (1.0 - ADAM_B2 ** ADAM_STEP)
    delta = -ADAM_LR * (m_hat / (jnp.sqrt(v_hat) + ADAM_EPS) + ADAM_WD * w)
    return delta, m2, v2


ADAM_STEPS = 2


def _adamw_all(shard_grads, shard_w, shard_m, shard_v, ra, rb, small_w, small_m, small_v):
    n_sh, n = len(shard_w), len(small_w)

    def body(*refs):
        sh_in, refs = refs[:4 * n_sh], refs[4 * n_sh:]
        ra_ref, rb_ref, refs = refs[0], refs[1], refs[2:]
        w_refs, m_refs, v_refs, refs = refs[:n], refs[n:2 * n], refs[2 * n:3 * n], refs[3 * n:]
        sh_out, outs = refs[:4 * n_sh], refs[4 * n_sh:]
        for k in range(n_sh):
            g = sh_in[k][...]
            delta, m2, v2 = _adamw(sh_in[n_sh + k][...], g, sh_in[2 * n_sh + k][...], sh_in[3 * n_sh + k][...])
            for ref, val in zip(sh_out[4 * k:4 * k + 4], (g, delta, m2, v2)):
                ref[...] = val

        @pl.when(pl.program_id(0) == 0)
        def _():
            g_outs, d_outs, m_outs, v_outs = outs[:n], outs[n:2 * n], outs[2 * n:3 * n], outs[3 * n:4 * n]
            ga, gb = ra_ref[0], rb_ref[0]
            for chip in range(1, N_CHIPS):
                ga = ga + ra_ref[chip]
                gb = gb + rb_ref[chip]
            outs[4 * n][...] = ga[ROW_LOSS:ROW_LOSS + 1, 0:128]
            grads = [ga[0:1, :], ga[1:2, :], ga[2:3, :], ga[3:4, :A_WIDTH], ga[3:4, A_WIDTH:],
                     gb[ROW_WS:ROW_WS + A_GROUPS * CHUNK, :].reshape(A_GROUPS, CHUNK, CHUNK),
                     gb[ROW_BS:ROW_BS + A_GROUPS, :], gb[ROW_SINK:ROW_SINK + 1, 0:4],
                     gb[ROW_REL:ROW_REL + 4, 0:N_BUCKETS]]
            for k in range(n):
                delta, m2, v2 = _adamw(w_refs[k][...], grads[k], m_refs[k][...], v_refs[k][...])
                g_outs[k][...] = grads[k]
                d_outs[k][...] = delta
                m_outs[k][...] = m2
                v_outs[k][...] = v2

    def rows_block(a):
        assert a.shape[0] % (8 * ADAM_STEPS) == 0
        return pl.BlockSpec((a.shape[0] // ADAM_STEPS, a.shape[1]), lambda i: (i, 0))

    sh_specs = [rows_block(w) for w in shard_w]
    small_in = [ra, rb, *small_w, *small_m, *small_v]
    small_out_shapes = [jax.ShapeDtypeStruct(w.shape, F32) for w in small_w] * 4 + [jax.ShapeDtypeStruct((1, 128), F32)]
    out = pl.pallas_call(
        body, name="adamw_all", grid=(ADAM_STEPS,),
        out_shape=[jax.ShapeDtypeStruct(w.shape, F32) for w in shard_w for _ in range(4)] + small_out_shapes,
        in_specs=sh_specs * 4 + [_full_spec(a.shape) for a in small_in],
        out_specs=[spec for spec in sh_specs for _ in range(4)] + [_full_spec(s.shape) for s in small_out_shapes],
        compiler_params=pltpu.CompilerParams(vmem_limit_bytes=VMEM_LIMIT),
    )(*shard_grads, *shard_w, *shard_m, *shard_v, *small_in)
    return [out[4 * k:4 * k + 4] for k in range(n_sh)], out[4 * n_sh:]


def kernel(x, mem, pre_norm_g, post_norm_g, mem_norm_g, w_in, w_mem_kv, v_norm_g, v_norm_b, w_spatial, b_spatial, attn_sinks, rel_bias, w_out, loss_target, m_pre_norm_g, m_post_norm_g, m_mem_norm_g, m_w_in, m_w_mem_kv, m_v_norm_g, m_v_norm_b, m_w_spatial, m_b_spatial, m_attn_sinks, m_rel_bias, m_w_out, v_pre_norm_g, v_post_norm_g, v_mem_norm_g, v_w_in, v_w_mem_kv, v_v_norm_g, v_v_norm_b, v_w_spatial, v_b_spatial, v_attn_sinks, v_rel_bias, v_w_out):
    n_ex, seq, _ = x.shape
    n_tok = n_ex * seq
    x2 = x.reshape(n_tok, D_MODEL)
    tgt2 = loss_target.reshape(n_tok, D_MODEL)
    buckets = jnp.asarray(_bucket_map())
    shard_arr = (2 * lax.axis_index("x") + lax.axis_index("y")).astype(jnp.int32).reshape(1)
    w_sp = w_spatial[0]
    w_in_t, m_w_in_t, v_w_in_t = (jnp.transpose(a[0]) for a in (w_in, m_w_in, v_w_in))
    rel_t, m_rel_t, v_rel_t = (jnp.transpose(a) for a in (rel_bias, m_rel_bias, v_rel_bias))

    x_arr = lax.axis_index("x").astype(jnp.int32).reshape(1)
    h_b, parts, (w_in_b, g_mkv, g_out), bias, b_sp = _gather_and_project(
        x2, pre_norm_g, w_in_t, w_mem_kv[0], w_out[0], rel_t, buckets, b_spatial[0], x_arr)
    w_mkv_b = g_mkv.reshape(D_MODEL, 2 * MEM_WIDTH)
    w_out_b = g_out.reshape(MIX_WIDTH, D_MODEL)

    dx, dproj, dwmkv, dwout, small_a, small_b = _mix(
        parts, mem, x2, tgt2, v_norm_g, v_norm_b, w_sp, b_sp, attn_sinks, bias, w_out_b, post_norm_g, mem_norm_g,
        w_mkv_b, pre_norm_g, w_in_b, buckets, n_ex, seq)

    shard_shapes = [w_mem_kv.shape[1:], w_out.shape[1:]]
    big = [g.reshape(N_CHIPS, 2, s[0] // 2, s[1]) for g, s in zip((dwmkv, dwout), shard_shapes)]
    (g_win, g_wmkv, g_wout), (ga, gb) = _reduce_gradients(dproj, h_b, big, [small_a, small_b], shard_arr)

    small_w = [pre_norm_g, post_norm_g, mem_norm_g, v_norm_g, v_norm_b, w_sp, b_spatial[0], attn_sinks, rel_t]
    small_m = [m_pre_norm_g, m_post_norm_g, m_mem_norm_g, m_v_norm_g, m_v_norm_b, m_w_spatial[0], m_b_spatial[0],
               m_attn_sinks, m_rel_t]
    small_v = [v_pre_norm_g, v_post_norm_g, v_mem_norm_g, v_v_norm_g, v_v_norm_b, v_w_spatial[0], v_b_spatial[0],
               v_attn_sinks, v_rel_t]
    big_out, small_out = _adamw_all(
        [g_win, g_wmkv, g_wout], [w_in_t, w_mem_kv[0], w_out[0]], [m_w_in_t, m_w_mem_kv[0], m_w_out[0]],
        [v_w_in_t, v_w_mem_kv[0], v_w_out[0]], ga, gb, small_w, small_m, small_v)
    n_small = len(small_w)

    outputs = [small_out[4 * n_small][0, 0], dx.reshape(x.shape)]
    for kind in range(4):
        s = small_out[kind * n_small:(kind + 1) * n_small]
        outputs += [s[0], s[1], s[2], jnp.transpose(big_out[0][kind])[None], big_out[1][kind][None], s[3], s[4],
                    s[5][None], s[6][None], s[7], jnp.transpose(s[8]), big_out[2][kind][None]]
    return tuple(outputs)
```

```python
import functools

import numpy as np
import jax
import jax.numpy as jnp
from jax import lax
from jax.experimental import pallas as pl
from jax.experimental.pallas import tpu as pltpu

F32 = jnp.float32
BF16 = jnp.bfloat16
MESH = pl.DeviceIdType.MESH

D_MODEL = 1024
CHUNK = 128
A_WIDTH = 512
A_GROUPS = 4
SWA_WIDTH = 256
KV_WIDTH = 128
MEM_WIDTH = 256
MEM_LEN = 256
MIX_WIDTH = 1024
IN_WIDTH = 2816
N_BUCKETS = 32
MAX_DISTANCE = 128
EPS = 1e-6
NEG = -1e30
QK_SCALE = 0.125
HALF_HEAD_PAIR = 64

ADAM_LR = 0.001
ADAM_B1 = 0.9
ADAM_B2 = 0.999
ADAM_EPS = 1e-08
ADAM_WD = 0.01
ADAM_STEP = 10

N_CHIPS = 4
TILE_CHUNKS = 2
TILE = TILE_CHUNKS * CHUNK
PROJ_TILE = 512
VMEM_LIMIT = 56 * 1024 * 1024

SMALL_A_ROWS = 8
ROW_LOSS = 4
ROW_WS = 0
ROW_BS = 512
ROW_SINK = 520
ROW_REL = 528
SMALL_B_ROWS = 536


def _mm(a, b):
    return lax.dot_general(a, b, (((1,), (0,)), ((), ())), preferred_element_type=F32)


def _mm_nt(a, b):
    return lax.dot_general(a, b, (((1,), (1,)), ((), ())), preferred_element_type=F32)


def _mm_tn(a, b):
    return lax.dot_general(a, b, (((0,), (0,)), ((), ())), preferred_element_type=F32)


def _bucket_map():
    qi = np.arange(CHUNK)[:, None]
    kj = np.arange(2 * CHUNK)[None, :]
    n = np.maximum(qi + CHUNK - kj, 0)
    max_exact = N_BUCKETS // 2
    large = max_exact + (np.log(np.maximum(n, 1) / max_exact) / np.log(MAX_DISTANCE / max_exact)
                         * (N_BUCKETS - max_exact)).astype(np.int32)
    large = np.minimum(large, N_BUCKETS - 1)
    return np.where(n < max_exact, n, large).astype(np.int32)


_GELU_C = 0.7978845608028654
_GELU_A = 0.044715
_GELU_K1 = 2.0 * _GELU_C
_GELU_K2 = 2.0 * _GELU_C * _GELU_A


def _gelu(x):
    x2 = x * x
    s = 1.0 / (1.0 + jnp.exp(x * (-_GELU_K1 - _GELU_K2 * x2)))
    return x * s, (s, x2)


def _gelu_grad(x, saved):
    s, x2 = saved
    return s + x * (s * (1.0 - s)) * (_GELU_K1 + 3.0 * _GELU_K2 * x2)


def _sigmoid(x):
    return 1.0 / (1.0 + jnp.exp(-x))


def _lane_lo(shape):
    return lax.broadcasted_iota(jnp.int32, shape, 1) < HALF_HEAD_PAIR


def _swa_variants(t):
    lo = _lane_lo(t.shape)
    tr = pltpu.roll(t, HALF_HEAD_PAIR, 1)
    zero = jnp.zeros_like(t)
    return (jnp.where(lo, t, zero).astype(BF16), jnp.where(lo, zero, tr).astype(BF16),
            jnp.where(lo, tr, zero).astype(BF16), jnp.where(lo, zero, t).astype(BF16))


def _swa_unvariants(d0, d1, d2, d3):
    lo = _lane_lo(d0.shape)
    zero = jnp.zeros_like(d0)
    rolled = jnp.where(lo, zero, d1) + jnp.where(lo, d2, zero)
    return jnp.where(lo, d0, zero) + jnp.where(lo, zero, d3) + pltpu.roll(rolled, HALF_HEAD_PAIR, 1)


def _mem_variants(t):
    out = []
    for pair in range(2):
        tp = t[:, pair * 128:(pair + 1) * 128]
        lo = _lane_lo(tp.shape)
        zero = jnp.zeros_like(tp)
        out.append(jnp.where(lo, tp, zero).astype(BF16))
        out.append(jnp.where(lo, zero, tp).astype(BF16))
    return out


def _mem_unvariants(d0, d1, d2, d3):
    lo = _lane_lo(d0.shape)
    return jnp.concatenate([jnp.where(lo, d0, d1), jnp.where(lo, d2, d3)], axis=-1)


def _softmax(logits, sinks):
    m = jnp.max(logits, axis=-1, keepdims=True)
    if sinks is not None:
        m = jnp.maximum(m, sinks)
    p = jnp.exp(logits - m)
    den = jnp.sum(p, axis=-1, keepdims=True)
    if sinks is None:
        return p * (1.0 / den), None
    es = jnp.exp(sinks - m)
    inv = 1.0 / (den + es)
    return p * inv, es * inv


def _band_valid(with_prev):
    qi = lax.broadcasted_iota(jnp.int32, (CHUNK, 2 * CHUNK), 0)
    kj = lax.broadcasted_iota(jnp.int32, (CHUNK, 2 * CHUNK), 1)
    in_cur = (kj >= CHUNK) & (kj - CHUNK <= qi)
    if not with_prev:
        return in_cur
    return in_cur | ((kj < CHUNK) & (kj > qi))


def _causal_weights(ws_ref):
    row = lax.broadcasted_iota(jnp.int32, (CHUNK, CHUNK), 0)
    col = lax.broadcasted_iota(jnp.int32, (CHUNK, CHUNK), 1)
    return [jnp.where(row >= col, ws_ref[g], 0.0).astype(BF16) for g in range(A_GROUPS)]


def _rows_to_lanes(a, n):
    return jnp.concatenate([a[c * CHUNK:(c + 1) * CHUNK] for c in range(n)], axis=1)


def _lanes_to_rows(a, n):
    w = a.shape[1] // n
    return jnp.concatenate([a[:, c * w:(c + 1) * w] for c in range(n)], axis=0)


def _stack_heads(pair01, pair23):
    return jnp.concatenate([pair01[:, :256], pair01[:, 256:], pair23[:, :256], pair23[:, 256:]], axis=0)


def _pair_heads(s, r):
    return (jnp.concatenate([s[0:r], s[r:2 * r]], axis=1), jnp.concatenate([s[2 * r:3 * r], s[3 * r:4 * r]], axis=1))


def _pair_operands(variants):
    return (jnp.concatenate(variants[0:2], axis=0), jnp.concatenate(variants[2:4], axis=0))


def _split_pair_grads(d_pairs):
    return d_pairs[0][:256], d_pairs[0][256:], d_pairs[1][:256], d_pairs[1][256:]


def _halves_bf16(a):
    return (a[:, :128].astype(BF16), a[:, 128:].astype(BF16))


def _group_a_forward(au, av, vg, vb, wm, bs_rows):
    gu, tu = _gelu(au)
    gv, tv = _gelu(av)
    ya, res = [], []
    for g in range(A_GROUPS):
        sl = slice(g * 128, (g + 1) * 128)
        xg = gv[:, sl]
        xc = xg - jnp.mean(xg, axis=-1, keepdims=True)
        rstd = lax.rsqrt(jnp.mean(xc * xc, axis=-1, keepdims=True) + EPS)
        xhat = xc * rstd
        vn = _rows_to_lanes((xhat * vg[:, sl] + vb[:, sl]).astype(BF16), TILE_CHUNKS)
        s = _lanes_to_rows(_mm(wm[g], vn), TILE_CHUNKS) + bs_rows[g]
        ya.append(gu[:, sl] * s)
        res.append((xhat, rstd, vn, s))
    return ya, dict(gu=gu, tu=tu, tv=tv, groups=res)


def _attention_logits(qp, k_pairs):
    return _stack_heads(_mm_nt(qp[0], k_pairs[0]), _mm_nt(qp[1], k_pairs[1]))


def _attention_out(p, v_pairs, r):
    pp = _pair_heads(p.astype(BF16), r)
    return jnp.concatenate([_mm(pp[0], v_pairs[0]), _mm(pp[1], v_pairs[1])], axis=-1), pp


def _attention_dprobs(do_pairs, v_pairs):
    return _stack_heads(_mm_nt(do_pairs[0], v_pairs[0]), _mm_nt(do_pairs[1], v_pairs[1]))


def _softmax_backward(p, dp):
    delta = jnp.sum(p * dp, axis=-1, keepdims=True)
    return p * (dp - delta), delta


def _attention_grads(dl, pp, do_pairs, qp, k_pairs, r):
    dlp = _pair_heads(dl.astype(BF16), r)
    dq = jnp.concatenate([_mm(dlp[0], k_pairs[0]), _mm(dlp[1], k_pairs[1])], axis=-1)
    dk = (_mm_tn(dlp[0], qp[0]), _mm_tn(dlp[1], qp[1]))
    dv = (_mm_tn(pp[0], do_pairs[0]), _mm_tn(pp[1], do_pairs[1]))
    return dq, dk, dv


def _tile_specs(n_tiles_ex, width):
    return pl.BlockSpec((TILE, width), lambda b, i: (b * n_tiles_ex + jnp.minimum(i, n_tiles_ex - 1), 0))


def _prev_chunk_spec(n_tiles_ex, width):
    def index(b, i):
        chunk = TILE_CHUNKS * jnp.minimum(i, n_tiles_ex - 1)
        return (b * n_tiles_ex * TILE_CHUNKS + jnp.maximum(chunk - 1, 0), 0)
    return pl.BlockSpec((CHUNK, width), index)


def _full_spec(shape):
    zeros = (0,) * len(shape)
    return pl.BlockSpec(shape, lambda *_: zeros)


SMEM_SPEC = pl.BlockSpec(memory_space=pltpu.SMEM)
ANY_SPEC = pl.BlockSpec(memory_space=pl.ANY)


def _fill_bias(rel_ref, bk_ref, out_ref):
    bk = bk_ref[...]
    for h in range(4):
        acc = jnp.zeros((CHUNK, 2 * CHUNK), F32)
        for b in range(N_BUCKETS):
            acc = jnp.where(bk == b, rel_ref[h, b], acc)
        for t, with_prev in enumerate((True, False)):
            out_ref[t, h * CHUNK:(h + 1) * CHUNK, :] = jnp.where(_band_valid(with_prev), acc, NEG)


PROJ_WIDTHS = (A_WIDTH, A_WIDTH, SWA_WIDTH, KV_WIDTH, KV_WIDTH, MEM_WIDTH, MIX_WIDTH)
PROJ_OFFSETS = tuple(int(v) for v in np.cumsum((0,) + PROJ_WIDTHS))


MXU_TILE = 256
HALF_WIDTH = IN_WIDTH // 2
PHASE_COLS = (HALF_WIDTH // MXU_TILE * MXU_TILE, IN_WIDTH - HALF_WIDTH // MXU_TILE * MXU_TILE)


def _phase_columns(phase, chip_x):
    if phase == 0:
        return 0 if chip_x == 0 else IN_WIDTH - PHASE_COLS[0]
    return PHASE_COLS[0] if chip_x == 0 else 0


def _phase_parts(phase, chip_x):
    start = _phase_columns(phase, chip_x)
    return [(k, PROJ_OFFSETS[k] - start) for k in range(len(PROJ_WIDTHS))
            if start <= PROJ_OFFSETS[k] and PROJ_OFFSETS[k + 1] <= start + PHASE_COLS[phase]]


def _gather_and_project(x2, g_pre, w_in_s, w_mkv_s, w_out_s, rel_bias_t, buckets, b_spatial, x_arr):
    n_tok = x2.shape[0]
    n_tiles = n_tok // PROJ_TILE
    last = n_tiles - 1
    shapes = [w_in_s.shape, w_mkv_s.shape, w_out_s.shape]
    n_w = len(shapes)

    def body(x_sref, x_ref, g_ref, win_hbm, wmkv_hbm, wout_hbm, rel_ref, bk_ref, bsp_ref, h_ref, *refs):
        part_refs, refs = refs[:len(PROJ_WIDTHS)], refs[len(PROJ_WIDTHS):]
        bias_ref, bs_ref, refs = refs[0], refs[1], refs[2:]
        gin_hbm, gmkv_hbm, gout_hbm, wg, stage_in, stage_mkv, stage_out, own_mkv, own_out, h_all = refs[:10]
        send_sems, recv_sems, local_sems = refs[10:]
        p, t = pl.program_id(0), pl.program_id(1)
        x, y, c = lax.axis_index("x"), lax.axis_index("y"), lax.axis_index("c")
        me, sibling = (x, y, c), (x, y, 1 - c)
        my_shard = 2 * x + y
        gathered = [wg, gmkv_hbm, gout_hbm]

        def half_rows(w, shard, half):
            rows = shapes[w][0] // 2
            if w == 0:
                return wg.at[pl.ds(pl.multiple_of(shard * shapes[0][0] + half * rows, 16), rows), :]
            return gathered[w].at[shard, pl.ds(half * rows, rows), :]

        def first(w, rel):
            src = half_rows(w, my_shard, c) if w == 0 else (own_mkv, own_out)[w - 1].at[
                pl.ds(c * (shapes[w][0] // 2), shapes[w][0] // 2), :]
            k = 3 * w + rel - 1
            return pltpu.make_async_remote_copy(
                src_ref=src, dst_ref=half_rows(w, my_shard, c), send_sem=send_sems.at[k], recv_sem=recv_sems.at[k],
                device_id=(x ^ (rel >> 1), y ^ (rel & 1), c), device_id_type=MESH)

        def landed(w, rel):
            k = 3 * w + rel - 1
            ref = half_rows(w, my_shard ^ rel, c)
            return pltpu.make_async_remote_copy(src_ref=ref, dst_ref=ref, send_sem=send_sems.at[k],
                                                recv_sem=recv_sems.at[k], device_id=me, device_id_type=MESH)

        def passed(w, rel, half, to):
            k = 9 + 3 * w + rel - 1
            ref = half_rows(w, my_shard ^ rel, half)
            return pltpu.make_async_remote_copy(src_ref=ref, dst_ref=ref, send_sem=send_sems.at[k],
                                                recv_sem=recv_sems.at[k], device_id=to, device_id_type=MESH)

        def pass_on(w, rels):
            for rel in rels:
                landed(w, rel).wait_recv()
                passed(w, rel, c, sibling).start()
            for rel in rels:
                passed(w, rel, 1 - c, me).wait_recv()

        own_stores = [pltpu.make_async_copy(own_mkv, gmkv_hbm.at[my_shard], local_sems.at[3]),
                      pltpu.make_async_copy(own_out, gout_hbm.at[my_shard], local_sems.at[4])]

        @pl.when((p == 0) & (t == 0))
        def _():
            half_rows_in = shapes[0][0] // 2
            halves = [pl.ds(pl.multiple_of(hc * half_rows_in, 8), half_rows_in) for hc in (c, 1 - c)]
            loads = [pltpu.make_async_copy(win_hbm.at[halves[0], :], stage_in.at[halves[0], :], local_sems.at[0]),
                     pltpu.make_async_copy(wmkv_hbm, stage_mkv, local_sems.at[1]),
                     pltpu.make_async_copy(wout_hbm, stage_out, local_sems.at[2]),
                     pltpu.make_async_copy(win_hbm.at[halves[1], :], stage_in.at[halves[1], :], local_sems.at[6])]
            for cp in (loads[0], loads[3], loads[1], loads[2]):
                cp.start()
            loads[0].wait()
            half_rows(0, my_shard, c)[...] = stage_in[halves[0], :].astype(BF16)
            for rel in (1, 2):
                first(0, rel).start()
            loads[3].wait()
            half_rows(0, my_shard, 1 - c)[...] = stage_in[halves[1], :].astype(BF16)
            loads[1].wait()
            loads[2].wait()
            own_mkv[...] = stage_mkv[...].astype(BF16)
            own_out[...] = stage_out[...].astype(BF16)
            for cp in own_stores:
                cp.start()
            _fill_bias(rel_ref, bk_ref, bias_ref)
            for g in range(A_GROUPS):
                bs_ref[g] = jnp.transpose(jnp.broadcast_to(bsp_ref[g:g + 1, :], (CHUNK, CHUNK)))
            pass_on(0, (1,))
            first(0, 3).start()

        @pl.when((p == 0) & (t == n_tiles // 2))
        def _():
            for w in (1, 2):
                for rel in (1, 2, 3):
                    first(w, rel).start()

        store = pltpu.make_async_copy(wg, gin_hbm, local_sems.at[5])

        @pl.when((p == 1) & (t == 0))
        def _():
            pass_on(0, (2, 3))
            store.start()

        @pl.when((p == 1) & (t == n_tiles // 2))
        def _():
            for w in (1, 2):
                pass_on(w, (1, 2, 3))

        tile_rows = pl.ds(pl.multiple_of(t * PROJ_TILE, PROJ_TILE), PROJ_TILE)

        def project(h, phase):
            start = jnp.where(x_sref[0] == 0, _phase_columns(phase, 0), _phase_columns(phase, 1))
            proj = _mm_nt(h, wg[pl.ds(pl.multiple_of(start, MXU_TILE), PHASE_COLS[phase]), :])
            for chip_x in range(2):
                @pl.when(x_sref[0] == chip_x)
                def _():
                    for k, lo in _phase_parts(phase, chip_x):
                        part_refs[k][...] = proj[:, lo:lo + PROJ_WIDTHS[k]].astype(BF16)

        @pl.when(p == 0)
        def _():
            xv = x_ref[...]
            r = lax.rsqrt(jnp.mean(xv * xv, axis=-1, keepdims=True) + EPS)
            h = (xv * r * g_ref[...]).astype(BF16)
            h_ref[...] = h
            h_all[tile_rows, :] = h
            project(h, 0)

        @pl.when(p == 1)
        def _():
            project(h_all[tile_rows, :], 1)

        @pl.when((p == 1) & (t == last))
        def _():
            for w in range(n_w):
                for rel in (1, 2, 3):
                    first(w, rel).wait_send()
                    passed(w, rel, c, sibling).wait_send()
            for cp in own_stores:
                cp.wait()
            store.wait()

    def written_in(k):
        phase_on = [next(ph for ph in range(2) if k in dict(_phase_parts(ph, chip_x))) for chip_x in range(2)]

        def index(p, t, xs):
            phase = jnp.where(xs[0] == 0, phase_on[0], phase_on[1])
            return (jnp.where(p == phase, t, jnp.where(p < phase, 0, last)), 0)
        return index

    part_specs = [pl.BlockSpec((PROJ_TILE, PROJ_WIDTHS[k]), written_in(k)) for k in range(len(PROJ_WIDTHS))]
    vmem = pltpu.VMEM
    out = pl.pallas_call(
        body, name="gather_and_project",
        out_shape=[jax.ShapeDtypeStruct((n_tok, D_MODEL), BF16)]
        + [jax.ShapeDtypeStruct((n_tok, w), BF16) for w in PROJ_WIDTHS]
        + [jax.ShapeDtypeStruct((2, 4 * CHUNK, 2 * CHUNK), F32), jax.ShapeDtypeStruct((A_GROUPS, CHUNK, CHUNK), F32)]
        + [jax.ShapeDtypeStruct((N_CHIPS * shapes[0][0], shapes[0][1]), BF16)]
        + [jax.ShapeDtypeStruct((N_CHIPS,) + s, BF16) for s in shapes[1:]],
        grid_spec=pltpu.PrefetchScalarGridSpec(
            num_scalar_prefetch=1, grid=(2, n_tiles),
            in_specs=[pl.BlockSpec((PROJ_TILE, D_MODEL), lambda p, t, xs: (jnp.where(p == 0, t, last), 0)),
                      pl.BlockSpec((1, D_MODEL), lambda p, t, xs: (0, 0)), ANY_SPEC, ANY_SPEC, ANY_SPEC, SMEM_SPEC,
                      pl.BlockSpec(buckets.shape, lambda p, t, xs: (0, 0)),
                      pl.BlockSpec(b_spatial.shape, lambda p, t, xs: (0, 0))],
            out_specs=[pl.BlockSpec((PROJ_TILE, D_MODEL), lambda p, t, xs: (jnp.where(p == 0, t, last), 0))]
            + part_specs + [pl.BlockSpec((2, 4 * CHUNK, 2 * CHUNK), lambda p, t, xs: (0, 0, 0)),
                            pl.BlockSpec((A_GROUPS, CHUNK, CHUNK), lambda p, t, xs: (0, 0, 0))] + [ANY_SPEC] * 3,
            scratch_shapes=[vmem((N_CHIPS * shapes[0][0], shapes[0][1]), BF16), vmem(shapes[0], F32),
                            vmem(shapes[1], F32), vmem(shapes[2], F32), vmem(shapes[1], BF16), vmem(shapes[2], BF16),
                            vmem((n_tok, D_MODEL), BF16),
                            pltpu.SemaphoreType.DMA((18,)), pltpu.SemaphoreType.DMA((18,)),
                            pltpu.SemaphoreType.DMA((7,))]),
        compiler_params=pltpu.CompilerParams(vmem_limit_bytes=VMEM_LIMIT),
    )(x_arr, x2, g_pre, w_in_s, w_mkv_s, w_out_s, rel_bias_t, buckets, b_spatial)
    n_parts = len(PROJ_WIDTHS)
    return out[0], list(out[1:1 + n_parts]), out[3 + n_parts:], out[1 + n_parts], out[2 + n_parts]


def _load_chunk(j, i, sk_ref, sv_ref, skp_ref, svp_ref):
    rows = slice(j * CHUNK, (j + 1) * CHUNK)
    if j == 0:
        k_prev, v_prev, table = skp_ref[...], svp_ref[...], jnp.where(i > 0, 0, 1)
    else:
        prev = slice((j - 1) * CHUNK, j * CHUNK)
        k_prev, v_prev, table = sk_ref[prev, :], sv_ref[prev, :], 0
    k_pairs = _pair_operands(_swa_variants(jnp.concatenate([k_prev, sk_ref[rows, :]], axis=0).astype(F32)))
    v_pairs = _pair_operands(_swa_variants(jnp.concatenate([v_prev, sv_ref[rows, :]], axis=0).astype(F32)))
    return rows, k_pairs, v_pairs, table


def _tile_constants(ws_ref, bs_ref, sink_ref):
    wm = _causal_weights(ws_ref)
    bs_rows = [jnp.concatenate([bs_ref[g]] * TILE_CHUNKS, axis=0) for g in range(A_GROUPS)]
    sink_col = jnp.max(jnp.concatenate([jnp.full((CHUNK, 128), sink_ref[0, h], F32) for h in range(4)] * TILE_CHUNKS,
                                       axis=0), axis=-1, keepdims=True)
    return wm, bs_rows, sink_col


def _mix(parts, mem, x2, tgt2, v_g, v_b, w_sp, b_sp, sinks, bias, w_out, g_post, g_mem, w_mkv, g_pre, w_in_t, buckets,
         n_ex, seq):
    n_tiles_ex = seq // TILE
    n_tok = n_ex * seq
    au, av, sq, sk, sv, mq, z = parts
    col = dict(zip(("au", "av", "sq", "sk", "sv", "mq", "z"),
                   (slice(PROJ_OFFSETS[k], PROJ_OFFSETS[k + 1]) for k in range(len(PROJ_WIDTHS)))))
    before_kv, after_kv = slice(0, col["sk"].start), slice(col["sv"].stop, IN_WIDTH)
    kv_cols = slice(col["sk"].start, col["sv"].stop)
    gated = slice(col["au"].start, col["av"].stop)
    cut_a, cut_z = (s.start + 3 * (s.stop - s.start) // 4 for s in (gated, col["z"]))
    back_cols = ((slice(gated.start, cut_a),), (slice(col["z"].start, cut_z),),
                 (slice(cut_a, gated.stop), slice(cut_z, col["z"].stop)), (col["sq"], col["mq"]))
    assert sum(s.stop - s.start for part in back_cols for s in part) == IN_WIDTH - 2 * KV_WIDTH

    def body(au_ref, av_ref, sq_ref, sk_ref, sv_ref, skp_ref, svp_ref, mq_ref, z_ref, mem_ref, x_ref, tgt_ref,
             vg_ref, vb_ref, ws_ref, bs_ref, sink_ref, bias_ref, wout_ref, gpost_ref, gmem_ref, wmkv_ref,
             xl_ref, gpre_ref, bk_ref, win_hbm,
             dx_ref, dproj_ref, dwmkv_ref, dwout_ref, a_ref, b_ref,
             carry_dp, carry_k, carry_v, memn_s, mem_ops, dmkv_s, carry_dout, win_s, win_sem, dh_s,
             dgpre_ref, dgpost_ref, dgmem_ref, dvg_ref, dvb_ref, dws_ref, dbs_ref, dsink_ref, drel_ref, loss_ref):
        b, i = pl.program_id(0), pl.program_id(1)
        win_load = pltpu.make_async_copy(win_hbm, win_s, win_sem)

        @pl.when((b == 0) & (i == 0))
        def _():
            win_load.start()
            for ref in (dwmkv_ref, dwout_ref, dgpre_ref, dgpost_ref, dgmem_ref, dvg_ref, dvb_ref, dws_ref, dbs_ref,
                        dsink_ref, drel_ref, loss_ref, carry_dp):
                ref[...] = jnp.zeros_like(ref)

        def normalized_mem():
            m = mem_ref[0]
            return m * lax.rsqrt(jnp.mean(m * m, axis=-1, keepdims=True) + EPS)

        @pl.when(i == 0)
        def _():
            memn_s[...] = (normalized_mem() * gmem_ref[...]).astype(BF16)
            mkv = _mm(memn_s[...], wmkv_ref[...])
            for k, pair in enumerate(_pair_operands(_mem_variants(mkv[:, :MEM_WIDTH]))
                                     + _pair_operands(_mem_variants(mkv[:, MEM_WIDTH:]))):
                mem_ops[k] = pair
            dmkv_s[...] = jnp.zeros_like(dmkv_s)
            carry_k[...] = jnp.zeros_like(carry_k)
            carry_v[...] = jnp.zeros_like(carry_v)

        @pl.when((b == 0) & (i == 0))
        def _():
            win_load.wait()

        @pl.when(i > 0)
        def _():
            dproj_ref[:, before_kv] = carry_dp[:, before_kv]
            dproj_ref[:, after_kv] = carry_dp[:, after_kv]

        def project_back(part):
            return sum(_mm(carry_dp[:, s], win_s[s, :]) for s in back_cols[part])

        @pl.when(i < n_tiles_ex)
        def _():
            dh_s[...] = project_back(0)
            wm, bs_rows, sink_col = _tile_constants(ws_ref, bs_ref, sink_ref)
            mk_pairs, mv_pairs = (mem_ops[0], mem_ops[1]), (mem_ops[2], mem_ops[3])
            vg = vg_ref[...]

            au_v, av_v = au_ref[...].astype(F32), av_ref[...].astype(F32)
            ya, res = _group_a_forward(au_v, av_v, vg, vb_ref[...], wm, bs_rows)
            swa, logits, yb = [], [], []
            for j in range(TILE_CHUNKS):
                rows, k_pairs, v_pairs, table = _load_chunk(j, i, sk_ref, sv_ref, skp_ref, svp_ref)
                qp = _halves_bf16(sq_ref[rows, :] * QK_SCALE)
                logits.append(_attention_logits(qp, k_pairs) + bias_ref[table])
                swa.append([rows, k_pairs, v_pairs, qp])
            dh_s[...] += project_back(1)
            p_swa, sink_p = _softmax(jnp.concatenate(logits, axis=0), sink_col)
            for j in range(TILE_CHUNKS):
                out, pp = _attention_out(p_swa[j * 4 * CHUNK:(j + 1) * 4 * CHUNK], swa[j][2], CHUNK)
                yb.append(out)
                swa[j].append(pp)
            mqp = _halves_bf16(mq_ref[...] * QK_SCALE)
            pm, _ = _softmax(_attention_logits(mqp, mk_pairs), None)
            yc, ppm = _attention_out(pm, mv_pairs, TILE)
            ycat = jnp.concatenate(ya + [jnp.concatenate(yb, axis=0), yc], axis=-1)

            dh_s[...] += project_back(2)
            zv = z_ref[...].astype(F32)
            sig = _sigmoid(zv)
            sz = zv * sig
            y_b = (ycat * sz).astype(BF16)
            o = _mm(y_b, wout_ref[...])
            r2 = lax.rsqrt(jnp.mean(o * o, axis=-1, keepdims=True) + EPS)
            nrm = o * r2
            gp = gpost_ref[...]
            diff = x_ref[...] + nrm * gp - tgt_ref[...]
            loss_ref[...] += jnp.sum(diff * diff) * (0.5 / D_MODEL)
            dout = diff * (1.0 / D_MODEL)
            carry_dout[lax.rem(i, 2)] = dout
            dgpost_ref[...] += jnp.sum(dout * nrm, axis=0, keepdims=True)
            dn = dout * gp
            do_b = (r2 * (dn - nrm * jnp.mean(dn * nrm, axis=-1, keepdims=True))).astype(BF16)
            dwout_ref[...] += _mm_tn(y_b, do_b)
            dy = _mm_nt(do_b, wout_ref[...])
            carry_dp[:, col["z"]] = (dy * ycat * (sig + sz * (1.0 - sig))).astype(BF16)
            dyc = dy * sz

            dgu, dgv = [], []
            for g in range(A_GROUPS):
                sl = slice(g * 128, (g + 1) * 128)
                xhat, rstd, vn, s = res["groups"][g]
                dya = dyc[:, sl]
                dgu.append(dya * s)
                ds = dya * res["gu"][:, sl]
                dbs_ref[:, sl] += sum(ds[c * CHUNK:(c + 1) * CHUNK] for c in range(TILE_CHUNKS))
                ds_b = _rows_to_lanes(ds.astype(BF16), TILE_CHUNKS)
                dws_ref[g] += _mm_nt(ds_b, vn)
                dvn = _lanes_to_rows(_mm_tn(wm[g], ds_b), TILE_CHUNKS)
                dvg_ref[:, sl] += jnp.sum(dvn * xhat, axis=0, keepdims=True)
                dvb_ref[:, sl] += jnp.sum(dvn, axis=0, keepdims=True)
                dxh = dvn * vg[:, sl]
                dgv.append(rstd * (dxh - jnp.mean(dxh, axis=-1, keepdims=True)
                                   - xhat * jnp.mean(dxh * xhat, axis=-1, keepdims=True)))
            carry_dp[:, col["au"]] = (jnp.concatenate(dgu, axis=-1) * _gelu_grad(au_v, res["tu"])).astype(BF16)
            carry_dp[:, col["av"]] = (jnp.concatenate(dgv, axis=-1) * _gelu_grad(av_v, res["tv"])).astype(BF16)

            do_pairs = [_halves_bf16(dyc[rows, A_WIDTH:A_WIDTH + SWA_WIDTH]) for rows, *_ in swa]
            dp_swa = jnp.concatenate(
                [_attention_dprobs(do_pairs[j], swa[j][2]) for j in range(TILE_CHUNKS)], axis=0)
            dh_s[...] += project_back(3)
            dl_swa, delta = _softmax_backward(p_swa, dp_swa)
            sink_terms = sink_p * delta
            lane4 = lax.broadcasted_iota(jnp.int32, (1, 128), 1)
            dsink_vec = jnp.zeros((1, 128), F32)
            for h in range(4):
                head_sum = sum(jnp.sum(sink_terms[(4 * j + h) * CHUNK:(4 * j + h + 1) * CHUNK])
                               for j in range(TILE_CHUNKS))
                dsink_vec = dsink_vec + jnp.where(lane4 == h, -head_sum, 0.0)
            dsink_ref[...] += dsink_vec
            drel_ref[...] += sum(dl_swa[j * 4 * CHUNK:(j + 1) * 4 * CHUNK] for j in range(TILE_CHUNKS))
            dk_parts, dv_parts = [], []
            for j, (rows, k_pairs, v_pairs, qp, pp) in enumerate(swa):
                dq, dk, dv = _attention_grads(dl_swa[j * 4 * CHUNK:(j + 1) * 4 * CHUNK], pp, do_pairs[j], qp, k_pairs,
                                              CHUNK)
                carry_dp[rows, col["sq"]] = (dq * QK_SCALE).astype(BF16)
                dk_parts.append(_swa_unvariants(*_split_pair_grads(dk)))
                dv_parts.append(_swa_unvariants(*_split_pair_grads(dv)))

            dc_pairs = _halves_bf16(dyc[:, A_WIDTH + SWA_WIDTH:])
            dl_mem, _ = _softmax_backward(pm, _attention_dprobs(dc_pairs, mv_pairs))
            dmq, dmk, dmv = _attention_grads(dl_mem, ppm, dc_pairs, mqp, mk_pairs, TILE)
            carry_dp[:, col["mq"]] = (dmq * QK_SCALE).astype(BF16)
            dmkv_s[...] += jnp.concatenate([_mem_unvariants(*_split_pair_grads(dmk)),
                                            _mem_unvariants(*_split_pair_grads(dmv))], axis=-1)

            for parts_c, carry, cols in ((dk_parts, carry_k, col["sk"]), (dv_parts, carry_v, col["sv"])):
                @pl.when(i > 0)
                def _():
                    dproj_ref[:, cols] = (carry[...] + jnp.concatenate(
                        [jnp.zeros((TILE - CHUNK, KV_WIDTH), F32), parts_c[0][:CHUNK]], axis=0)).astype(BF16)
                new = [parts_c[0][CHUNK:]]
                for j in range(1, TILE_CHUNKS):
                    new[-1] = new[-1] + parts_c[j][:CHUNK]
                    new.append(parts_c[j][CHUNK:])
                carry[...] = jnp.concatenate(new, axis=0)

        @pl.when(i == n_tiles_ex)
        def _():
            dproj_ref[:, col["sk"]] = carry_k[...].astype(BF16)
            dproj_ref[:, col["sv"]] = carry_v[...].astype(BF16)
            d_b = dmkv_s[...].astype(BF16)
            dwmkv_ref[...] += _mm_tn(memn_s[...], d_b)
            dgmem_ref[...] += jnp.sum(_mm_nt(d_b, wmkv_ref[...]) * normalized_mem(), axis=0, keepdims=True)
            dh_s[...] = sum(project_back(part) for part in range(len(back_cols)))

        @pl.when(i > 0)
        def _():
            xv = xl_ref[...]
            r = lax.rsqrt(jnp.mean(xv * xv, axis=-1, keepdims=True) + EPS)
            xn = xv * r
            dh = dh_s[...] + _mm(dproj_ref[:, kv_cols], win_s[kv_cols, :])
            dgpre_ref[...] += jnp.sum(dh * xn, axis=0, keepdims=True)
            dhg = dh * gpre_ref[...]
            dx_ref[...] = (r * (dhg - xn * jnp.mean(dhg * xn, axis=-1, keepdims=True))
                           + carry_dout[lax.rem(i + 1, 2)])

        @pl.when((b == n_ex - 1) & (i == n_tiles_ex))
        def _():
            _fill_small_grads(dgpre_ref, dgpost_ref, dgmem_ref, dvg_ref, dvb_ref, dws_ref, dbs_ref, dsink_ref,
                              drel_ref, loss_ref, bk_ref, a_ref, b_ref)

    tile = functools.partial(_tile_specs, n_tiles_ex)
    prev = functools.partial(_prev_chunk_spec, n_tiles_ex)
    late = lambda width: pl.BlockSpec((TILE, width), lambda b, i: (b * n_tiles_ex + jnp.maximum(i - 1, 0), 0))
    vmem_f32 = lambda *shape: pltpu.VMEM(shape, F32)
    return pl.pallas_call(
        body, name="mix", grid=(n_ex, n_tiles_ex + 1),
        out_shape=[jax.ShapeDtypeStruct((n_tok, D_MODEL), F32), jax.ShapeDtypeStruct((n_tok, IN_WIDTH), BF16),
                   jax.ShapeDtypeStruct((D_MODEL, 2 * MEM_WIDTH), F32), jax.ShapeDtypeStruct((MIX_WIDTH, D_MODEL), F32),
                   jax.ShapeDtypeStruct((SMALL_A_ROWS, D_MODEL), F32), jax.ShapeDtypeStruct((SMALL_B_ROWS, 128), F32)],
        in_specs=[tile(A_WIDTH), tile(A_WIDTH), tile(SWA_WIDTH), tile(KV_WIDTH), tile(KV_WIDTH),
                  prev(KV_WIDTH), prev(KV_WIDTH), tile(MEM_WIDTH), tile(MIX_WIDTH),
                  pl.BlockSpec((1, MEM_LEN, D_MODEL), lambda b, i: (b, 0, 0)),
                  tile(D_MODEL), tile(D_MODEL),
                  _full_spec((1, A_WIDTH)), _full_spec((1, A_WIDTH)), _full_spec((A_GROUPS, CHUNK, CHUNK)),
                  _full_spec((A_GROUPS, CHUNK, CHUNK)), SMEM_SPEC, _full_spec((2, 4 * CHUNK, 2 * CHUNK)),
                  _full_spec((MIX_WIDTH, D_MODEL)), _full_spec((1, D_MODEL)), _full_spec((1, D_MODEL)),
                  _full_spec((D_MODEL, 2 * MEM_WIDTH)),
                  late(D_MODEL), _full_spec((1, D_MODEL)), _full_spec((CHUNK, 2 * CHUNK)), ANY_SPEC],
        out_specs=[late(D_MODEL), late(IN_WIDTH), _full_spec((D_MODEL, 2 * MEM_WIDTH)),
                   _full_spec((MIX_WIDTH, D_MODEL)), _full_spec((SMALL_A_ROWS, D_MODEL)),
                   _full_spec((SMALL_B_ROWS, 128))],
        scratch_shapes=[pltpu.VMEM((TILE, IN_WIDTH), BF16), pltpu.VMEM((TILE, KV_WIDTH), F32),
                        pltpu.VMEM((TILE, KV_WIDTH), F32), pltpu.VMEM((MEM_LEN, D_MODEL), BF16),
                        pltpu.VMEM((4, 2 * MEM_LEN, 128), BF16), pltpu.VMEM((MEM_LEN, 2 * MEM_WIDTH), F32),
                        pltpu.VMEM((2, TILE, D_MODEL), F32), pltpu.VMEM((IN_WIDTH, D_MODEL), BF16),
                        pltpu.SemaphoreType.DMA, vmem_f32(TILE, D_MODEL),
                        vmem_f32(1, D_MODEL), vmem_f32(1, D_MODEL), vmem_f32(1, D_MODEL), vmem_f32(1, A_WIDTH),
                        vmem_f32(1, A_WIDTH), vmem_f32(A_GROUPS, CHUNK, CHUNK), vmem_f32(CHUNK, A_WIDTH),
                        vmem_f32(1, 128), vmem_f32(4 * CHUNK, 2 * CHUNK), vmem_f32(1, 128)],
        compiler_params=pltpu.CompilerParams(vmem_limit_bytes=VMEM_LIMIT),
    )(au, av, sq, sk, sv, sk, sv, mq, z, mem, x2, tgt2, v_g, v_b, w_sp, b_sp, sinks, bias, w_out, g_post, g_mem,
      w_mkv, x2, g_pre, buckets, w_in_t)


def _fill_small_grads(dgpre_ref, dgpost_ref, dgmem_ref, dvg_ref, dvb_ref, dws_ref, dbs_ref, dsink_ref, drel_ref,
                      loss_ref, bk_ref, a_ref, b_ref):
    a_ref[...] = jnp.zeros_like(a_ref)
    b_ref[...] = jnp.zeros_like(b_ref)
    a_ref[0:1, :] = dgpre_ref[...]
    a_ref[1:2, :] = dgpost_ref[...]
    a_ref[2:3, :] = dgmem_ref[...]
    a_ref[3:4, :] = jnp.concatenate([dvg_ref[...], dvb_ref[...]], axis=-1)
    a_ref[ROW_LOSS:ROW_LOSS + 1, 0:128] = loss_ref[...]
    row = lax.broadcasted_iota(jnp.int32, (CHUNK, CHUNK), 0)
    col = lax.broadcasted_iota(jnp.int32, (CHUNK, CHUNK), 1)
    for g in range(A_GROUPS):
        b_ref[ROW_WS + g * CHUNK:ROW_WS + (g + 1) * CHUNK, :] = jnp.where(row >= col, dws_ref[g], 0.0)
        by_token = jnp.transpose(dbs_ref[:, g * 128:(g + 1) * 128])
        b_ref[ROW_BS + g:ROW_BS + g + 1, :] = jnp.sum(by_token, axis=0, keepdims=True)
    b_ref[ROW_SINK:ROW_SINK + 1, :] = dsink_ref[...]
    bk = bk_ref[...]
    rel_row = lax.broadcasted_iota(jnp.int32, (8, 128), 0)
    rel_col = lax.broadcasted_iota(jnp.int32, (8, 128), 1)
    rel = jnp.zeros((8, 128), F32)
    for h in range(4):
        acc = drel_ref[h * CHUNK:(h + 1) * CHUNK, :]
        for b in range(N_BUCKETS):
            rel = jnp.where((rel_row == h) & (rel_col == b), jnp.sum(jnp.where(bk == b, acc, 0.0)), rel)
    b_ref[ROW_REL:ROW_REL + 8, :] = rel


SHARD_ROWS = IN_WIDTH // N_CHIPS
SHARD_WINDOW = 768
SHARD_HALF = SHARD_ROWS // 2
DWIN_TILE = 2048
N_REL = N_CHIPS - 1


def _shard_window_start(shard):
    return (shard * SHARD_ROWS // 128) * 128


def _reduce_gradients(dproj, h, big, small, shard_arr):
    n_tok = h.shape[0]
    tile = min(DWIN_TILE, n_tok)
    n_sub = n_tok // tile
    last = N_CHIPS - 1
    n_big, n_small = len(big), len(small)
    big_half = [g.shape[2:] for g in big]
    sem_big_d2d = 2 * N_CHIPS
    sem_big_ici = sem_big_d2d + n_big
    sem_big_swap = sem_big_ici + N_REL * n_big
    sem_small_d2d = sem_big_swap + n_big
    sem_small_ici = sem_small_d2d + n_small
    n_sems = sem_small_ici + N_REL * n_small
    loc_small = n_big
    loc_out_win = loc_small + n_small
    loc_out_big = loc_out_win + 2
    loc_out_small = loc_out_big + 2 * n_big
    n_local = loc_out_small + n_small

    def relation_of_slot(s):
        return (s + 2) % N_REL + 1

    def shard_of_slot(s, my_shard):
        return my_shard ^ jnp.where(s == last, 0, relation_of_slot(s))

    def body(shard_ref, dp_ref, h_hbm, *refs):
        h_vmem, h_sem, refs = refs[-2], refs[-1], refs[:-2]
        big_hbm, refs = refs[:n_big], refs[n_big:]
        small_hbm, refs = refs[:n_small], refs[n_small:]
        out_hbm, refs = refs[0], refs[1:]
        big_out, refs = refs[:n_big], refs[n_big:]
        small_out, refs = refs[:n_small], refs[n_small:]
        part, recv_d2d, send_ici, recv_ici, mine_buf, other_buf = refs[:6]
        refs = refs[6:]
        big_own, big_recv, big_send, big_land, big_mine, big_other = (
            refs[k * n_big:(k + 1) * n_big] for k in range(6))
        refs = refs[6 * n_big:]
        small_own, small_recv, small_all = (refs[k * n_small:(k + 1) * n_small] for k in range(3))
        send_sems, recv_sems, local_sems = refs[3 * n_small:]

        s, t = pl.program_id(0), pl.program_id(1)
        x, y, c = lax.axis_index("x"), lax.axis_index("y"), lax.axis_index("c")
        my_chip = 2 * x + y
        sibling = (x, y, 1 - c)
        my_rows = pl.ds(pl.multiple_of(c * SHARD_HALF, 8), SHARD_HALF)
        other_rows = pl.ds(pl.multiple_of((1 - c) * SHARD_HALF, 8), SHARD_HALF)

        def remote(src, dst, k, to):
            return pltpu.make_async_remote_copy(src_ref=src, dst_ref=dst, send_sem=send_sems.at[k],
                                                recv_sem=recv_sems.at[k], device_id=to, device_id_type=MESH)

        def chip_at(rel):
            return (x ^ (rel >> 1), y ^ (rel & 1), c)

        def to_sibling(k):
            return remote(part.at[k % 2, other_rows, :], recv_d2d.at[k], k, sibling)

        def to_chip(k):
            return remote(send_ici.at[k], recv_ici.at[k], N_CHIPS + k, chip_at(relation_of_slot(k)))

        swap = remote(mine_buf, other_buf, 2 * N_CHIPS - 1, sibling)
        big_load = [pltpu.make_async_copy(big_hbm[w].at[:, pl.ds(c, 1)], big_own[w], local_sems.at[w])
                    for w in range(n_big)]
        big_to_sibling = [remote(big_hbm[w].at[:, pl.ds(1 - c, 1)], big_recv[w], sem_big_d2d + w, sibling)
                          for w in range(n_big)]
        big_to_chip = [[remote(big_send[w].at[k], big_land[w].at[k], sem_big_ici + N_REL * w + k, chip_at(k + 1))
                        for k in range(N_REL)] for w in range(n_big)]
        big_swap = [remote(big_mine[w], big_other[w], sem_big_swap + w, sibling) for w in range(n_big)]
        small_load = [pltpu.make_async_copy(small_hbm[i], small_own[i], local_sems.at[loc_small + i])
                      for i in range(n_small)]
        small_to_sibling = [remote(small_hbm[i], small_recv[i], sem_small_d2d + i, sibling) for i in range(n_small)]
        small_to_chip = [[remote(small_all[i].at[my_chip], small_all[i].at[my_chip],
                                 sem_small_ici + N_REL * i + k, chip_at(k + 1))
                          for k in range(N_REL)] for i in range(n_small)]

        h_loads = [pltpu.make_async_copy(h_hbm.at[rows, :], h_vmem.at[rows, :], h_sem.at[k]) for k, rows in enumerate(
            [pl.ds(0, tile)] + ([pl.ds(tile, n_tok - tile)] if n_sub > 1 else []))]

        @pl.when((s == 0) & (t == 0))
        def _():
            for cp in h_loads + big_load + big_to_sibling + small_load + small_to_sibling:
                cp.start()
            h_loads[0].wait()

        if n_sub > 1:
            @pl.when((s == 0) & (t == 1))
            def _():
                h_loads[1].wait()

        @pl.when((s == 0) & (t == n_sub - 1))
        def _():
            for cp in big_load + small_load:
                cp.wait()
            for cp in big_to_sibling + small_to_sibling:
                cp.wait_recv()
                cp.wait_send()
            for w in range(n_big):
                for k in range(N_REL):
                    shard = my_chip ^ (k + 1)
                    big_send[w][k] = (big_own[w][shard, 0] + big_recv[w][shard, 0]).astype(BF16)
                    big_to_chip[w][k].start()
            for i in range(n_small):
                small_all[i][my_chip] = small_own[i][...] + small_recv[i][...]
                for k in range(N_REL):
                    small_to_chip[i][k].start()

        @pl.when((s > 0) & (t == jnp.where(s == last, 0, min(1, n_sub - 1))))
        def _():
            k = s - 1
            cp = to_sibling(k)
            cp.wait_recv()
            cp.wait_send()
            send_ici[k] = (part[k % 2, my_rows, :] + recv_d2d[k]).astype(BF16)
            to_chip(k).start()

        def big_rows(w, half):
            rows = big_half[w][0]
            return big_out[w].at[pl.ds(pl.multiple_of(half * rows, 8), rows), :]

        big_store_mine = [pltpu.make_async_copy(big_mine[w], big_rows(w, c), local_sems.at[loc_out_big + 2 * w])
                          for w in range(n_big)]
        big_store_other = [pltpu.make_async_copy(big_other[w], big_rows(w, 1 - c),
                                                 local_sems.at[loc_out_big + 2 * w + 1]) for w in range(n_big)]
        small_store = [pltpu.make_async_copy(small_all[i], small_out[i], local_sems.at[loc_out_small + i])
                       for i in range(n_small)]

        @pl.when((s == last) & (t == 0))
        def _():
            for w in range(n_big):
                total = big_own[w][my_chip, 0] + big_recv[w][my_chip, 0]
                for k in range(N_REL):
                    big_to_chip[w][k].wait_recv()
                    total = total + big_land[w][k].astype(F32)
                big_mine[w][...] = total
                big_swap[w].start()
                big_store_mine[w].start()
            for i in range(n_small):
                for k in range(N_REL):
                    small_to_chip[i][k].wait_recv()
                small_store[i].start()

        r = _mm_tn(dp_ref[...], h_vmem[pl.ds(pl.multiple_of(t * tile, tile), tile), :])
        odd = shard_of_slot(s, shard_ref[0]) % 2
        for parity in range(2):
            rows = r[64 * parity:64 * parity + SHARD_ROWS]

            @pl.when((odd == parity) & (t == 0))
            def _():
                part[s % 2] = rows

            @pl.when((odd == parity) & (t > 0))
            def _():
                part[s % 2] += rows

        @pl.when(t == n_sub - 1)
        def _():
            to_sibling(s).start()

        @pl.when((s == last) & (t == n_sub - 1))
        def _():
            cp = to_sibling(last)
            cp.wait_recv()
            cp.wait_send()
            total = part[last % 2, my_rows, :] + recv_d2d[last]
            for k in range(last):
                to_chip(k).wait_recv()
                total = total + recv_ici[k].astype(F32)
            mine_buf[...] = total
            swap.start()
            out_mine = pltpu.make_async_copy(mine_buf, out_hbm.at[my_rows, :], local_sems.at[0])
            out_mine.start()
            swap.wait_recv()
            out_other = pltpu.make_async_copy(other_buf, out_hbm.at[other_rows, :], local_sems.at[1])
            out_other.start()
            for w in range(n_big):
                big_swap[w].wait_recv()
                big_store_other[w].start()
            stores = [out_mine, out_other] + big_store_mine + big_store_other + small_store
            for k in range(last):
                to_chip(k).wait_send()
            swap.wait_send()
            for w in range(n_big):
                for k in range(N_REL):
                    big_to_chip[w][k].wait_send()
                big_swap[w].wait_send()
            for i in range(n_small):
                for k in range(N_REL):
                    small_to_chip[i][k].wait_send()
            for cp in stores:
                cp.wait()

    half = (SHARD_HALF, D_MODEL)
    vmem = pltpu.VMEM
    scratch = [vmem((2, SHARD_ROWS, D_MODEL), F32), vmem((N_CHIPS,) + half, F32),
               vmem((N_REL,) + half, BF16), vmem((N_REL,) + half, BF16), vmem(half, F32), vmem(half, F32)]
    scratch += [vmem((N_CHIPS, 1) + hs, F32) for hs in big_half] * 2
    scratch += [vmem((N_REL,) + hs, BF16) for hs in big_half] * 2
    scratch += [vmem(hs, F32) for hs in big_half] * 2
    scratch += [vmem(a.shape, F32) for a in small] * 2 + [vmem((N_CHIPS,) + a.shape, F32) for a in small]
    scratch += [pltpu.SemaphoreType.DMA((n_sems,)), pltpu.SemaphoreType.DMA((n_sems,)),
                pltpu.SemaphoreType.DMA((n_local,)), vmem(h.shape, BF16), pltpu.SemaphoreType.DMA((2,))]
    n_hbm = n_big + n_small
    out = pl.pallas_call(
        body, name="reduce_gradients",
        out_shape=[jax.ShapeDtypeStruct((SHARD_ROWS, D_MODEL), F32)]
        + [jax.ShapeDtypeStruct((2 * hs[0], hs[1]), F32) for hs in big_half]
        + [jax.ShapeDtypeStruct((N_CHIPS,) + a.shape, F32) for a in small],
        grid_spec=pltpu.PrefetchScalarGridSpec(
            num_scalar_prefetch=1, grid=(N_CHIPS, n_sub),
            in_specs=[pl.BlockSpec((pl.Element(tile), pl.Element(SHARD_WINDOW)),
                                   lambda s, t, m: (t * tile, _shard_window_start(shard_of_slot(s, m[0])))),
                      ANY_SPEC] + [ANY_SPEC] * n_hbm,
            out_specs=[ANY_SPEC] * (1 + n_hbm),
            scratch_shapes=scratch),
        compiler_params=pltpu.CompilerParams(vmem_limit_bytes=VMEM_LIMIT),
    )(shard_arr, dproj, h, *big, *small)
    return out[:1 + n_big], out[1 + n_big:]


def _adamw(w, g, m, v):
    m2 = ADAM_B1 * m + (1.0 - ADAM_B1) * g
    v2 = ADAM_B2 * v + (1.0 - ADAM_B2) * (g * g)
    m_hat = m2 / (1.0 - ADAM_B1 ** ADAM_STEP)
    v_hat = v2 / (1.0 - ADAM_B2 ** ADAM_STEP)
    delta = -ADAM_LR * (m_hat / (jnp.sqrt(v_hat) + ADAM_EPS) + ADAM_WD * w)
    return delta, m2, v2


ADAM_STEPS = 2


def _adamw_all(shard_grads, shard_w, shard_m, shard_v, ra, rb, small_w, small_m, small_v):
    n_sh, n = len(shard_w), len(small_w)

    def body(*refs):
        sh_in, refs = refs[:4 * n_sh], refs[4 * n_sh:]
        ra_ref, rb_ref, refs = refs[0], refs[1], refs[2:]
        w_refs, m_refs, v_refs, refs = refs[:n], refs[n:2 * n], refs[2 * n:3 * n], refs[3 * n:]
        sh_out, outs = refs[:4 * n_sh], refs[4 * n_sh:]
        for k in range(n_sh):
            g = sh_in[k][...]
            delta, m2, v2 = _adamw(sh_in[n_sh + k][...], g, sh_in[2 * n_sh + k][...], sh_in[3 * n_sh + k][...])
            for ref, val in zip(sh_out[4 * k:4 * k + 4], (g, delta, m2, v2)):
                ref[...] = val

        @pl.when(pl.program_id(0) == 0)
        def _():
            g_outs, d_outs, m_outs, v_outs = outs[:n], outs[n:2 * n], outs[2 * n:3 * n], outs[3 * n:4 * n]
            ga, gb = ra_ref[0], rb_ref[0]
            for chip in range(1, N_CHIPS):
                ga = ga + ra_ref[chip]
                gb = gb + rb_ref[chip]
            outs[4 * n][...] = ga[ROW_LOSS:ROW_LOSS + 1, 0:128]
            grads = [ga[0:1, :], ga[1:2, :], ga[2:3, :], ga[3:4, :A_WIDTH], ga[3:4, A_WIDTH:],
                     gb[ROW_WS:ROW_WS + A_GROUPS * CHUNK, :].reshape(A_GROUPS, CHUNK, CHUNK),
                     gb[ROW_BS:ROW_BS + A_GROUPS, :], gb[ROW_SINK:ROW_SINK + 1, 0:4],
                     gb[ROW_REL:ROW_REL + 4, 0:N_BUCKETS]]
            for k in range(n):
                delta, m2, v2 = _adamw(w_refs[k][...], grads[k], m_refs[k][...], v_refs[k][...])
                g_outs[k][...] = grads[k]
                d_outs[k][...] = delta
                m_outs[k][...] = m2
                v_outs[k][...] = v2

    def rows_block(a):
        assert a.shape[0] % (8 * ADAM_STEPS) == 0
        return pl.BlockSpec((a.shape[0] // ADAM_STEPS, a.shape[1]), lambda i: (i, 0))

    sh_specs = [rows_block(w) for w in shard_w]
    small_in = [ra, rb, *small_w, *small_m, *small_v]
    small_out_shapes = [jax.ShapeDtypeStruct(w.shape, F32) for w in small_w] * 4 + [jax.ShapeDtypeStruct((1, 128), F32)]
    out = pl.pallas_call(
        body, name="adamw_all", grid=(ADAM_STEPS,),
        out_shape=[jax.ShapeDtypeStruct(w.shape, F32) for w in shard_w for _ in range(4)] + small_out_shapes,
        in_specs=sh_specs * 4 + [_full_spec(a.shape) for a in small_in],
        out_specs=[spec for spec in sh_specs for _ in range(4)] + [_full_spec(s.shape) for s in small_out_shapes],
        compiler_params=pltpu.CompilerParams(vmem_limit_bytes=VMEM_LIMIT),
    )(*shard_grads, *shard_w, *shard_m, *shard_v, *small_in)
    return [out[4 * k:4 * k + 4] for k in range(n_sh)], out[4 * n_sh:]


def kernel(x, mem, pre_norm_g, post_norm_g, mem_norm_g, w_in, w_mem_kv, v_norm_g, v_norm_b, w_spatial, b_spatial, attn_sinks, rel_bias, w_out, loss_target, m_pre_norm_g, m_post_norm_g, m_mem_norm_g, m_w_in, m_w_mem_kv, m_v_norm_g, m_v_norm_b, m_w_spatial, m_b_spatial, m_attn_sinks, m_rel_bias, m_w_out, v_pre_norm_g, v_post_norm_g, v_mem_norm_g, v_w_in, v_w_mem_kv, v_v_norm_g, v_v_norm_b, v_w_spatial, v_b_spatial, v_attn_sinks, v_rel_bias, v_w_out):
    n_ex, seq, _ = x.shape
    n_tok = n_ex * seq
    x2 = x.reshape(n_tok, D_MODEL)
    tgt2 = loss_target.reshape(n_tok, D_MODEL)
    buckets = jnp.asarray(_bucket_map())
    shard_arr = (2 * lax.axis_index("x") + lax.axis_index("y")).astype(jnp.int32).reshape(1)
    w_sp = w_spatial[0]
    w_in_t, m_w_in_t, v_w_in_t = (jnp.transpose(a[0]) for a in (w_in, m_w_in, v_w_in))
    rel_t, m_rel_t, v_rel_t = (jnp.transpose(a) for a in (rel_bias, m_rel_bias, v_rel_bias))

    x_arr = lax.axis_index("x").astype(jnp.int32).reshape(1)
    h_b, parts, (w_in_b, g_mkv, g_out), bias, b_sp = _gather_and_project(
        x2, pre_norm_g, w_in_t, w_mem_kv[0], w_out[0], rel_t, buckets, b_spatial[0], x_arr)
    w_mkv_b = g_mkv.reshape(D_MODEL, 2 * MEM_WIDTH)
    w_out_b = g_out.reshape(MIX_WIDTH, D_MODEL)

    dx, dproj, dwmkv, dwout, small_a, small_b = _mix(
        parts, mem, x2, tgt2, v_norm_g, v_norm_b, w_sp, b_sp, attn_sinks, bias, w_out_b, post_norm_g, mem_norm_g,
        w_mkv_b, pre_norm_g, w_in_b, buckets, n_ex, seq)

    shard_shapes = [w_mem_kv.shape[1:], w_out.shape[1:]]
    big = [g.reshape(N_CHIPS, 2, s[0] // 2, s[1]) for g, s in zip((dwmkv, dwout), shard_shapes)]
    (g_win, g_wmkv, g_wout), (ga, gb) = _reduce_gradients(dproj, h_b, big, [small_a, small_b], shard_arr)

    small_w = [pre_norm_g, post_norm_g, mem_norm_g, v_norm_g, v_norm_b, w_sp, b_spatial[0], attn_sinks, rel_t]
    small_m = [m_pre_norm_g, m_post_norm_g, m_mem_norm_g, m_v_norm_g, m_v_norm_b, m_w_spatial[0], m_b_spatial[0],
               m_attn_sinks, m_rel_t]
    small_v = [v_pre_norm_g, v_post_norm_g, v_mem_norm_g, v_v_norm_g, v_v_norm_b, v_w_spatial[0], v_b_spatial[0],
               v_attn_sinks, v_rel_t]
    big_out, small_out = _adamw_all(
        [g_win, g_wmkv, g_wout], [w_in_t, w_mem_kv[0], w_out[0]], [m_w_in_t, m_w_mem_kv[0], m_w_out[0]],
        [v_w_in_t, v_w_mem_kv[0], v_w_out[0]], ga, gb, small_w, small_m, small_v)
    n_small = len(small_w)

    outputs = [small_out[4 * n_small][0, 0], dx.reshape(x.shape)]
    for kind in range(4):
        s = small_out[kind * n_small:(kind + 1) * n_small]
        outputs += [s[0], s[1], s[2], jnp.transpose(big_out[0][kind])[None], big_out[1][kind][None], s[3], s[4],
                    s[5][None], s[6][None], s[7], jnp.transpose(s[8]), big_out[2][kind][None]]
    return tuple(outputs)
```

```python
import functools

import numpy as np
import jax
import jax.numpy as jnp
from jax import lax
from jax.experimental import pallas as pl
from jax.experimental.pallas import tpu as pltpu

F32 = jnp.float32
BF16 = jnp.bfloat16
MESH = pl.DeviceIdType.MESH

D_MODEL = 1024
CHUNK = 128
A_WIDTH = 512
A_GROUPS = 4
SWA_WIDTH = 256
KV_WIDTH = 128
MEM_WIDTH = 256
MEM_LEN = 256
MIX_WIDTH = 1024
IN_WIDTH = 2816
N_BUCKETS = 32
MAX_DISTANCE = 128
EPS = 1e-6
NEG = -1e30
QK_SCALE = 0.125
HALF_HEAD_PAIR = 64

ADAM_LR = 0.001
ADAM_B1 = 0.9
ADAM_B2 = 0.999
ADAM_EPS = 1e-08
ADAM_WD = 0.01
ADAM_STEP = 10

N_CHIPS = 4
TILE_CHUNKS = 2
TILE = TILE_CHUNKS * CHUNK
PROJ_TILE = 512
VMEM_LIMIT = 56 * 1024 * 1024

SMALL_A_ROWS = 8
ROW_LOSS = 4
ROW_WS = 0
ROW_BS = 512
ROW_SINK = 520
ROW_REL = 528
SMALL_B_ROWS = 536


def _mm(a, b):
    return lax.dot_general(a, b, (((1,), (0,)), ((), ())), preferred_element_type=F32)


def _mm_nt(a, b):
    return lax.dot_general(a, b, (((1,), (1,)), ((), ())), preferred_element_type=F32)


def _mm_tn(a, b):
    return lax.dot_general(a, b, (((0,), (0,)), ((), ())), preferred_element_type=F32)


def _bucket_map():
    qi = np.arange(CHUNK)[:, None]
    kj = np.arange(2 * CHUNK)[None, :]
    n = np.maximum(qi + CHUNK - kj, 0)
    max_exact = N_BUCKETS // 2
    large = max_exact + (np.log(np.maximum(n, 1) / max_exact) / np.log(MAX_DISTANCE / max_exact)
                         * (N_BUCKETS - max_exact)).astype(np.int32)
    large = np.minimum(large, N_BUCKETS - 1)
    return np.where(n < max_exact, n, large).astype(np.int32)


_GELU_C = 0.7978845608028654
_GELU_A = 0.044715
_GELU_K1 = 2.0 * _GELU_C
_GELU_K2 = 2.0 * _GELU_C * _GELU_A


def _gelu(x):
    x2 = x * x
    s = 1.0 / (1.0 + jnp.exp(x * (-_GELU_K1 - _GELU_K2 * x2)))
    return x * s, (s, x2)


def _gelu_grad(x, saved):
    s, x2 = saved
    return s + x * (s * (1.0 - s)) * (_GELU_K1 + 3.0 * _GELU_K2 * x2)


def _sigmoid(x):
    return 1.0 / (1.0 + jnp.exp(-x))


def _lane_lo(shape):
    return lax.broadcasted_iota(jnp.int32, shape, 1) < HALF_HEAD_PAIR


def _swa_variants(t):
    lo = _lane_lo(t.shape)
    tr = pltpu.roll(t, HALF_HEAD_PAIR, 1)
    zero = jnp.zeros_like(t)
    return (jnp.where(lo, t, zero).astype(BF16), jnp.where(lo, zero, tr).astype(BF16),
            jnp.where(lo, tr, zero).astype(BF16), jnp.where(lo, zero, t).astype(BF16))


def _swa_unvariants(d0, d1, d2, d3):
    lo = _lane_lo(d0.shape)
    zero = jnp.zeros_like(d0)
    rolled = jnp.where(lo, zero, d1) + jnp.where(lo, d2, zero)
    return jnp.where(lo, d0, zero) + jnp.where(lo, zero, d3) + pltpu.roll(rolled, HALF_HEAD_PAIR, 1)


def _mem_variants(t):
    out = []
    for pair in range(2):
        tp = t[:, pair * 128:(pair + 1) * 128]
        lo = _lane_lo(tp.shape)
        zero = jnp.zeros_like(tp)
        out.append(jnp.where(lo, tp, zero).astype(BF16))
        out.append(jnp.where(lo, zero, tp).astype(BF16))
    return out


def _mem_unvariants(d0, d1, d2, d3):
    lo = _lane_lo(d0.shape)
    return jnp.concatenate([jnp.where(lo, d0, d1), jnp.where(lo, d2, d3)], axis=-1)


def _softmax(logits, sinks):
    m = jnp.max(logits, axis=-1, keepdims=True)
    if sinks is not None:
        m = jnp.maximum(m, sinks)
    p = jnp.exp(logits - m)
    den = jnp.sum(p, axis=-1, keepdims=True)
    if sinks is None:
        return p * (1.0 / den), None
    es = jnp.exp(sinks - m)
    inv = 1.0 / (den + es)
    return p * inv, es * inv


def _band_valid(with_prev):
    qi = lax.broadcasted_iota(jnp.int32, (CHUNK, 2 * CHUNK), 0)
    kj = lax.broadcasted_iota(jnp.int32, (CHUNK, 2 * CHUNK), 1)
    in_cur = (kj >= CHUNK) & (kj - CHUNK <= qi)
    if not with_prev:
        return in_cur
    return in_cur | ((kj < CHUNK) & (kj > qi))


def _causal_weights(ws_ref):
    row = lax.broadcasted_iota(jnp.int32, (CHUNK, CHUNK), 0)
    col = lax.broadcasted_iota(jnp.int32, (CHUNK, CHUNK), 1)
    return [jnp.where(row >= col, ws_ref[g], 0.0).astype(BF16) for g in range(A_GROUPS)]


def _rows_to_lanes(a, n):
    return jnp.concatenate([a[c * CHUNK:(c + 1) * CHUNK] for c in range(n)], axis=1)


def _lanes_to_rows(a, n):
    w = a.shape[1] // n
    return jnp.concatenate([a[:, c * w:(c + 1) * w] for c in range(n)], axis=0)


def _stack_heads(pair01, pair23):
    return jnp.concatenate([pair01[:, :256], pair01[:, 256:], pair23[:, :256], pair23[:, 256:]], axis=0)


def _pair_heads(s, r):
    return (jnp.concatenate([s[0:r], s[r:2 * r]], axis=1), jnp.concatenate([s[2 * r:3 * r], s[3 * r:4 * r]], axis=1))


def _pair_operands(variants):
    return (jnp.concatenate(variants[0:2], axis=0), jnp.concatenate(variants[2:4], axis=0))


def _split_pair_grads(d_pairs):
    return d_pairs[0][:256], d_pairs[0][256:], d_pairs[1][:256], d_pairs[1][256:]


def _halves_bf16(a):
    return (a[:, :128].astype(BF16), a[:, 128:].astype(BF16))


def _group_a_forward(au, av, vg, vb, wm, bs_rows):
    gu, tu = _gelu(au)
    gv, tv = _gelu(av)
    ya, res = [], []
    for g in range(A_GROUPS):
        sl = slice(g * 128, (g + 1) * 128)
        xg = gv[:, sl]
        xc = xg - jnp.mean(xg, axis=-1, keepdims=True)
        rstd = lax.rsqrt(jnp.mean(xc * xc, axis=-1, keepdims=True) + EPS)
        xhat = xc * rstd
        vn = _rows_to_lanes((xhat * vg[:, sl] + vb[:, sl]).astype(BF16), TILE_CHUNKS)
        s = _lanes_to_rows(_mm(wm[g], vn), TILE_CHUNKS) + bs_rows[g]
        ya.append(gu[:, sl] * s)
        res.append((xhat, rstd, vn, s))
    return ya, dict(gu=gu, tu=tu, tv=tv, groups=res)


def _attention_logits(qp, k_pairs):
    return _stack_heads(_mm_nt(qp[0], k_pairs[0]), _mm_nt(qp[1], k_pairs[1]))


def _attention_out(p, v_pairs, r):
    pp = _pair_heads(p.astype(BF16), r)
    return jnp.concatenate([_mm(pp[0], v_pairs[0]), _mm(pp[1], v_pairs[1])], axis=-1), pp


def _attention_dprobs(do_pairs, v_pairs):
    return _stack_heads(_mm_nt(do_pairs[0], v_pairs[0]), _mm_nt(do_pairs[1], v_pairs[1]))


def _softmax_backward(p, dp):
    delta = jnp.sum(p * dp, axis=-1, keepdims=True)
    return p * (dp - delta), delta


def _attention_grads(dl, pp, do_pairs, qp, k_pairs, r):
    dlp = _pair_heads(dl.astype(BF16), r)
    dq = jnp.concatenate([_mm(dlp[0], k_pairs[0]), _mm(dlp[1], k_pairs[1])], axis=-1)
    dk = (_mm_tn(dlp[0], qp[0]), _mm_tn(dlp[1], qp[1]))
    dv = (_mm_tn(pp[0], do_pairs[0]), _mm_tn(pp[1], do_pairs[1]))
    return dq, dk, dv


def _tile_specs(n_tiles_ex, width):
    return pl.BlockSpec((TILE, width), lambda b, i: (b * n_tiles_ex + jnp.minimum(i, n_tiles_ex - 1), 0))


def _prev_chunk_spec(n_tiles_ex, width):
    def index(b, i):
        chunk = TILE_CHUNKS * jnp.minimum(i, n_tiles_ex - 1)
        return (b * n_tiles_ex * TILE_CHUNKS + jnp.maximum(chunk - 1, 0), 0)
    return pl.BlockSpec((CHUNK, width), index)


def _full_spec(shape):
    zeros = (0,) * len(shape)
    return pl.BlockSpec(shape, lambda *_: zeros)


SMEM_SPEC = pl.BlockSpec(memory_space=pltpu.SMEM)
ANY_SPEC = pl.BlockSpec(memory_space=pl.ANY)


def _fill_bias(rel_ref, bk_ref, out_ref):
    bk = bk_ref[...]
    for h in range(4):
        acc = jnp.zeros((CHUNK, 2 * CHUNK), F32)
        for b in range(N_BUCKETS):
            acc = jnp.where(bk == b, rel_ref[h, b], acc)
        for t, with_prev in enumerate((True, False)):
            out_ref[t, h * CHUNK:(h + 1) * CHUNK, :] = jnp.where(_band_valid(with_prev), acc, NEG)


PROJ_WIDTHS = (A_WIDTH, A_WIDTH, SWA_WIDTH, KV_WIDTH, KV_WIDTH, MEM_WIDTH, MIX_WIDTH)
PROJ_OFFSETS = tuple(int(v) for v in np.cumsum((0,) + PROJ_WIDTHS))


MXU_TILE = 256
HALF_WIDTH = IN_WIDTH // 2
PHASE_COLS = (HALF_WIDTH // MXU_TILE * MXU_TILE, IN_WIDTH - HALF_WIDTH // MXU_TILE * MXU_TILE)


def _phase_columns(phase, chip_x):
    if phase == 0:
        return 0 if chip_x == 0 else IN_WIDTH - PHASE_COLS[0]
    return PHASE_COLS[0] if chip_x == 0 else 0


def _phase_parts(phase, chip_x):
    start = _phase_columns(phase, chip_x)
    return [(k, PROJ_OFFSETS[k] - start) for k in range(len(PROJ_WIDTHS))
            if start <= PROJ_OFFSETS[k] and PROJ_OFFSETS[k + 1] <= start + PHASE_COLS[phase]]


def _gather_and_project(x2, g_pre, w_in_s, w_mkv_s, w_out_s, rel_bias_t, buckets, b_spatial, x_arr):
    n_tok = x2.shape[0]
    n_tiles = n_tok // PROJ_TILE
    last = n_tiles - 1
    shapes = [w_in_s.shape, w_mkv_s.shape, w_out_s.shape]
    n_w = len(shapes)

    def body(x_sref, x_ref, g_ref, win_hbm, wmkv_hbm, wout_hbm, rel_ref, bk_ref, bsp_ref, h_ref, *refs):
        part_refs, refs = refs[:len(PROJ_WIDTHS)], refs[len(PROJ_WIDTHS):]
        bias_ref, bs_ref, refs = refs[0], refs[1], refs[2:]
        gin_hbm, gmkv_hbm, gout_hbm, wg, stage_in, stage_mkv, stage_out, own_mkv, own_out, h_all = refs[:10]
        send_sems, recv_sems, local_sems = refs[10:]
        p, t = pl.program_id(0), pl.program_id(1)
        x, y, c = lax.axis_index("x"), lax.axis_index("y"), lax.axis_index("c")
        me, sibling = (x, y, c), (x, y, 1 - c)
        my_shard = 2 * x + y
        gathered = [wg, gmkv_hbm, gout_hbm]

        def half_rows(w, shard, half):
            rows = shapes[w][0] // 2
            if w == 0:
                return wg.at[pl.ds(pl.multiple_of(shard * shapes[0][0] + half * rows, 16), rows), :]
            return gathered[w].at[shard, pl.ds(half * rows, rows), :]

        def first(w, rel):
            src = half_rows(w, my_shard, c) if w == 0 else (own_mkv, own_out)[w - 1].at[
                pl.ds(c * (shapes[w][0] // 2), shapes[w][0] // 2), :]
            k = 3 * w + rel - 1
            return pltpu.make_async_remote_copy(
                src_ref=src, dst_ref=half_rows(w, my_shard, c), send_sem=send_sems.at[k], recv_sem=recv_sems.at[k],
                device_id=(x ^ (rel >> 1), y ^ (rel & 1), c), device_id_type=MESH)

        def landed(w, rel):
            k = 3 * w + rel - 1
            ref = half_rows(w, my_shard ^ rel, c)
            return pltpu.make_async_remote_copy(src_ref=ref, dst_ref=ref, send_sem=send_sems.at[k],
                                                recv_sem=recv_sems.at[k], device_id=me, device_id_type=MESH)

        def passed(w, rel, half, to):
            k = 9 + 3 * w + rel - 1
            ref = half_rows(w, my_shard ^ rel, half)
            return pltpu.make_async_remote_copy(src_ref=ref, dst_ref=ref, send_sem=send_sems.at[k],
                                                recv_sem=recv_sems.at[k], device_id=to, device_id_type=MESH)

        def pass_on(w, rels):
            for rel in rels:
                landed(w, rel).wait_recv()
                passed(w, rel, c, sibling).start()
            for rel in rels:
                passed(w, rel, 1 - c, me).wait_recv()

        own_stores = [pltpu.make_async_copy(own_mkv, gmkv_hbm.at[my_shard], local_sems.at[3]),
                      pltpu.make_async_copy(own_out, gout_hbm.at[my_shard], local_sems.at[4])]

        @pl.when((p == 0) & (t == 0))
        def _():
            half_rows_in = shapes[0][0] // 2
            halves = [pl.ds(pl.multiple_of(hc * half_rows_in, 8), half_rows_in) for hc in (c, 1 - c)]
            loads = [pltpu.make_async_copy(win_hbm.at[halves[0], :], stage_in.at[halves[0], :], local_sems.at[0]),
                     pltpu.make_async_copy(wmkv_hbm, stage_mkv, local_sems.at[1]),
                     pltpu.make_async_copy(wout_hbm, stage_out, local_sems.at[2]),
                     pltpu.make_async_copy(win_hbm.at[halves[1], :], stage_in.at[halves[1], :], local_sems.at[6])]
            for cp in (loads[0], loads[3], loads[1], loads[2]):
                cp.start()
            loads[0].wait()
            half_rows(0, my_shard, c)[...] = stage_in[halves[0], :].astype(BF16)
            for rel in (1, 2):
                first(0, rel).start()
            loads[3].wait()
            half_rows(0, my_shard, 1 - c)[...] = stage_in[halves[1], :].astype(BF16)
            loads[1].wait()
            loads[2].wait()
            own_mkv[...] = stage_mkv[...].astype(BF16)
            own_out[...] = stage_out[...].astype(BF16)
            for cp in own_stores:
                cp.start()
            _fill_bias(rel_ref, bk_ref, bias_ref)
            for g in range(A_GROUPS):
                bs_ref[g] = jnp.transpose(jnp.broadcast_to(bsp_ref[g:g + 1, :], (CHUNK, CHUNK)))
            pass_on(0, (1,))
            first(0, 3).start()

        @pl.when((p == 0) & (t == n_tiles // 2))
        def _():
            for w in (1, 2):
                for rel in (1, 2, 3):
                    first(w, rel).start()

        store = pltpu.make_async_copy(wg, gin_hbm, local_sems.at[5])

        @pl.when((p == 1) & (t == 0))
        def _():
            pass_on(0, (2, 3))
            store.start()

        @pl.when((p == 1) & (t == n_tiles // 2))
        def _():
            for w in (1, 2):
                pass_on(w, (1, 2, 3))

        tile_rows = pl.ds(pl.multiple_of(t * PROJ_TILE, PROJ_TILE), PROJ_TILE)

        def project(h, phase):
            start = jnp.where(x_sref[0] == 0, _phase_columns(phase, 0), _phase_columns(phase, 1))
            proj = _mm_nt(h, wg[pl.ds(pl.multiple_of(start, MXU_TILE), PHASE_COLS[phase]), :])
            for chip_x in range(2):
                @pl.when(x_sref[0] == chip_x)
                def _():
                    for k, lo in _phase_parts(phase, chip_x):
                        part_refs[k][...] = proj[:, lo:lo + PROJ_WIDTHS[k]].astype(BF16)

        @pl.when(p == 0)
        def _():
            xv = x_ref[...]
            r = lax.rsqrt(jnp.mean(xv * xv, axis=-1, keepdims=True) + EPS)
            h = (xv * r * g_ref[...]).astype(BF16)
            h_ref[...] = h
            h_all[tile_rows, :] = h
            project(h, 0)

        @pl.when(p == 1)
        def _():
            project(h_all[tile_rows, :], 1)

        @pl.when((p == 1) & (t == last))
        def _():
            for w in range(n_w):
                for rel in (1, 2, 3):
                    first(w, rel).wait_send()
                    passed(w, rel, c, sibling).wait_send()
            for cp in own_stores:
                cp.wait()
            store.wait()

    def written_in(k):
        phase_on = [next(ph for ph in range(2) if k in dict(_phase_parts(ph, chip_x))) for chip_x in range(2)]

        def index(p, t, xs):
            phase = jnp.where(xs[0] == 0, phase_on[0], phase_on[1])
            return (jnp.where(p == phase, t, jnp.where(p < phase, 0, last)), 0)
        return index

    part_specs = [pl.BlockSpec((PROJ_TILE, PROJ_WIDTHS[k]), written_in(k)) for k in range(len(PROJ_WIDTHS))]
    vmem = pltpu.VMEM
    out = pl.pallas_call(
        body, name="gather_and_project",
        out_shape=[jax.ShapeDtypeStruct((n_tok, D_MODEL), BF16)]
        + [jax.ShapeDtypeStruct((n_tok, w), BF16) for w in PROJ_WIDTHS]
        + [jax.ShapeDtypeStruct((2, 4 * CHUNK, 2 * CHUNK), F32), jax.ShapeDtypeStruct((A_GROUPS, CHUNK, CHUNK), F32)]
        + [jax.ShapeDtypeStruct((N_CHIPS * shapes[0][0], shapes[0][1]), BF16)]
        + [jax.ShapeDtypeStruct((N_CHIPS,) + s, BF16) for s in shapes[1:]],
        grid_spec=pltpu.PrefetchScalarGridSpec(
            num_scalar_prefetch=1, grid=(2, n_tiles),
            in_specs=[pl.BlockSpec((PROJ_TILE, D_MODEL), lambda p, t, xs: (jnp.where(p == 0, t, last), 0)),
                      pl.BlockSpec((1, D_MODEL), lambda p, t, xs: (0, 0)), ANY_SPEC, ANY_SPEC, ANY_SPEC, SMEM_SPEC,
                      pl.BlockSpec(buckets.shape, lambda p, t, xs: (0, 0)),
                      pl.BlockSpec(b_spatial.shape, lambda p, t, xs: (0, 0))],
            out_specs=[pl.BlockSpec((PROJ_TILE, D_MODEL), lambda p, t, xs: (jnp.where(p == 0, t, last), 0))]
            + part_specs + [pl.BlockSpec((2, 4 * CHUNK, 2 * CHUNK), lambda p, t, xs: (0, 0, 0)),
                            pl.BlockSpec((A_GROUPS, CHUNK, CHUNK), lambda p, t, xs: (0, 0, 0))] + [ANY_SPEC] * 3,
            scratch_shapes=[vmem((N_CHIPS * shapes[0][0], shapes[0][1]), BF16), vmem(shapes[0], F32),
                            vmem(shapes[1], F32), vmem(shapes[2], F32), vmem(shapes[1], BF16), vmem(shapes[2], BF16),
                            vmem((n_tok, D_MODEL), BF16),
                            pltpu.SemaphoreType.DMA((18,)), pltpu.SemaphoreType.DMA((18,)),
                            pltpu.SemaphoreType.DMA((7,))]),
        compiler_params=pltpu.CompilerParams(vmem_limit_bytes=VMEM_LIMIT),
    )(x_arr, x2, g_pre, w_in_s, w_mkv_s, w_out_s, rel_bias_t, buckets, b_spatial)
    n_parts = len(PROJ_WIDTHS)
    return out[0], list(out[1:1 + n_parts]), out[3 + n_parts:], out[1 + n_parts], out[2 + n_parts]


def _load_chunk(j, i, sk_ref, sv_ref, skp_ref, svp_ref):
    rows = slice(j * CHUNK, (j + 1) * CHUNK)
    if j == 0:
        k_prev, v_prev, table = skp_ref[...], svp_ref[...], jnp.where(i > 0, 0, 1)
    else:
        prev = slice((j - 1) * CHUNK, j * CHUNK)
        k_prev, v_prev, table = sk_ref[prev, :], sv_ref[prev, :], 0
    k_pairs = _pair_operands(_swa_variants(jnp.concatenate([k_prev, sk_ref[rows, :]], axis=0).astype(F32)))
    v_pairs = _pair_operands(_swa_variants(jnp.concatenate([v_prev, sv_ref[rows, :]], axis=0).astype(F32)))
    return rows, k_pairs, v_pairs, table


def _tile_constants(ws_ref, bs_ref, sink_ref):
    wm = _causal_weights(ws_ref)
    bs_rows = [jnp.concatenate([bs_ref[g]] * TILE_CHUNKS, axis=0) for g in range(A_GROUPS)]
    sink_col = jnp.max(jnp.concatenate([jnp.full((CHUNK, 128), sink_ref[0, h], F32) for h in range(4)] * TILE_CHUNKS,
                                       axis=0), axis=-1, keepdims=True)
    return wm, bs_rows, sink_col


def _mix(parts, mem, x2, tgt2, v_g, v_b, w_sp, b_sp, sinks, bias, w_out, g_post, g_mem, w_mkv, g_pre, w_in_t, buckets,
         n_ex, seq):
    n_tiles_ex = seq // TILE
    n_tok = n_ex * seq
    au, av, sq, sk, sv, mq, z = parts
    col = dict(zip(("au", "av", "sq", "sk", "sv", "mq", "z"),
                   (slice(PROJ_OFFSETS[k], PROJ_OFFSETS[k + 1]) for k in range(len(PROJ_WIDTHS)))))
    before_kv, after_kv = slice(0, col["sk"].start), slice(col["sv"].stop, IN_WIDTH)
    kv_cols = slice(col["sk"].start, col["sv"].stop)
    gated = slice(col["au"].start, col["av"].stop)
    cut_a, cut_z = (s.start + 3 * (s.stop - s.start) // 4 for s in (gated, col["z"]))
    back_cols = ((slice(gated.start, cut_a),), (slice(col["z"].start, cut_z),),
                 (slice(cut_a, gated.stop), slice(cut_z, col["z"].stop)), (col["sq"], col["mq"]))
    assert sum(s.stop - s.start for part in back_cols for s in part) == IN_WIDTH - 2 * KV_WIDTH

    def body(au_ref, av_ref, sq_ref, sk_ref, sv_ref, skp_ref, svp_ref, mq_ref, z_ref, mem_ref, x_ref, tgt_ref,
             vg_ref, vb_ref, ws_ref, bs_ref, sink_ref, bias_ref, wout_ref, gpost_ref, gmem_ref, wmkv_ref,
             xl_ref, gpre_ref, bk_ref, win_hbm,
             dx_ref, dproj_ref, dwmkv_ref, dwout_ref, a_ref, b_ref,
             carry_dp, carry_k, carry_v, memn_s, mem_ops, dmkv_s, carry_dout, win_s, win_sem, dh_s,
             dgpre_ref, dgpost_ref, dgmem_ref, dvg_ref, dvb_ref, dws_ref, dbs_ref, dsink_ref, drel_ref, loss_ref):
        b, i = pl.program_id(0), pl.program_id(1)
        win_load = pltpu.make_async_copy(win_hbm, win_s, win_sem)

        @pl.when((b == 0) & (i == 0))
        def _():
            win_load.start()
            for ref in (dwmkv_ref, dwout_ref, dgpre_ref, dgpost_ref, dgmem_ref, dvg_ref, dvb_ref, dws_ref, dbs_ref,
                        dsink_ref, drel_ref, loss_ref, carry_dp):
                ref[...] = jnp.zeros_like(ref)

        def normalized_mem():
            m = mem_ref[0]
            return m * lax.rsqrt(jnp.mean(m * m, axis=-1, keepdims=True) + EPS)

        @pl.when(i == 0)
        def _():
            memn_s[...] = (normalized_mem() * gmem_ref[...]).astype(BF16)
            mkv = _mm(memn_s[...], wmkv_ref[...])
            for k, pair in enumerate(_pair_operands(_mem_variants(mkv[:, :MEM_WIDTH]))
                                     + _pair_operands(_mem_variants(mkv[:, MEM_WIDTH:]))):
                mem_ops[k] = pair
            dmkv_s[...] = jnp.zeros_like(dmkv_s)
            carry_k[...] = jnp.zeros_like(carry_k)
            carry_v[...] = jnp.zeros_like(carry_v)

        @pl.when((b == 0) & (i == 0))
        def _():
            win_load.wait()

        @pl.when(i > 0)
        def _():
            dproj_ref[:, before_kv] = carry_dp[:, before_kv]
            dproj_ref[:, after_kv] = carry_dp[:, after_kv]

        def project_back(part):
            return sum(_mm(carry_dp[:, s], win_s[s, :]) for s in back_cols[part])

        @pl.when(i < n_tiles_ex)
        def _():
            dh_s[...] = project_back(0)
            wm, bs_rows, sink_col = _tile_constants(ws_ref, bs_ref, sink_ref)
            mk_pairs, mv_pairs = (mem_ops[0], mem_ops[1]), (mem_ops[2], mem_ops[3])
            vg = vg_ref[...]

            au_v, av_v = au_ref[...].astype(F32), av_ref[...].astype(F32)
            ya, res = _group_a_forward(au_v, av_v, vg, vb_ref[...], wm, bs_rows)
            swa, logits, yb = [], [], []
            for j in range(TILE_CHUNKS):
                rows, k_pairs, v_pairs, table = _load_chunk(j, i, sk_ref, sv_ref, skp_ref, svp_ref)
                qp = _halves_bf16(sq_ref[rows, :] * QK_SCALE)
                logits.append(_attention_logits(qp, k_pairs) + bias_ref[table])
                swa.append([rows, k_pairs, v_pairs, qp])
            dh_s[...] += project_back(1)
            p_swa, sink_p = _softmax(jnp.concatenate(logits, axis=0), sink_col)
            for j in range(TILE_CHUNKS):
                out, pp = _attention_out(p_swa[j * 4 * CHUNK:(j + 1) * 4 * CHUNK], swa[j][2], CHUNK)
                yb.append(out)
                swa[j].append(pp)
            mqp = _halves_bf16(mq_ref[...] * QK_SCALE)
            pm, _ = _softmax(_attention_logits(mqp, mk_pairs), None)
            yc, ppm = _attention_out(pm, mv_pairs, TILE)
            ycat = jnp.concatenate(ya + [jnp.concatenate(yb, axis=0), yc], axis=-1)

            zv = z_ref[...].astype(F32)
            sig = _sigmoid(zv)
            sz = zv * sig
            y_b = (ycat * sz).astype(BF16)
            o = _mm(y_b, wout_ref[...])
            dh_s[...] += project_back(2)
            r2 =lax.rsqrt(jnp.mean(o * o, axis=-1, keepdims=True) + EPS)
            nrm = o * r2
            gp = gpost_ref[...]
            diff = x_ref[...] + nrm * gp - tgt_ref[...]
            loss_ref[...] += jnp.sum(diff * diff) * (0.5 / D_MODEL)
            dout = diff * (1.0 / D_MODEL)
            carry_dout[lax.rem(i, 2)] = dout
            dgpost_ref[...] += jnp.sum(dout * nrm, axis=0, keepdims=True)
            dn = dout * gp
            do_b =(r2 * (dn - nrm * jnp.mean(dn * nrm, axis=-1, keepdims=True))).astype(BF16)
            dy = _mm_nt(do_b, wout_ref[...])
            carry_dp[:, col["z"]] = (dy * ycat * (sig + sz * (1.0 - sig))).astype(BF16)
            dyc = dy * sz

            dgu, dgv = [], []
            for g in range(A_GROUPS):
                sl = slice(g * 128, (g + 1) * 128)
                xhat, rstd, vn, s = res["groups"][g]
                dya = dyc[:, sl]
                dgu.append(dya * s)
                ds = dya * res["gu"][:, sl]
                dbs_ref[:, sl] += sum(ds[c * CHUNK:(c + 1) * CHUNK] for c in range(TILE_CHUNKS))
                ds_b = _rows_to_lanes(ds.astype(BF16), TILE_CHUNKS)
                dws_ref[g] += _mm_nt(ds_b, vn)
                dvn = _lanes_to_rows(_mm_tn(wm[g], ds_b), TILE_CHUNKS)
                dvg_ref[:, sl] += jnp.sum(dvn * xhat, axis=0, keepdims=True)
                dvb_ref[:, sl] += jnp.sum(dvn, axis=0, keepdims=True)
                dxh = dvn * vg[:, sl]
                dgv.append(rstd * (dxh - jnp.mean(dxh, axis=-1, keepdims=True)
                                   - xhat * jnp.mean(dxh * xhat, axis=-1, keepdims=True)))
            carry_dp[:, col["au"]] =(jnp.concatenate(dgu, axis=-1) * _gelu_grad(au_v, res["tu"])).astype(BF16)
            carry_dp[:, col["av"]] = (jnp.concatenate(dgv, axis=-1) * _gelu_grad(av_v, res["tv"])).astype(BF16)

            do_pairs = [_halves_bf16(dyc[rows, A_WIDTH:A_WIDTH + SWA_WIDTH]) for rows, *_ in swa]
            dp_swa = jnp.concatenate(
                [_attention_dprobs(do_pairs[j], swa[j][2]) for j in range(TILE_CHUNKS)], axis=0)
            dh_s[...] += project_back(3)
            dl_swa, delta = _softmax_backward(p_swa, dp_swa)
            sink_terms = sink_p * delta
            lane4 = lax.broadcasted_iota(jnp.int32, (1, 128), 1)
            dsink_vec = jnp.zeros((1, 128), F32)
            for h in range(4):
                head_sum = sum(jnp.sum(sink_terms[(4 * j + h) * CHUNK:(4 * j + h + 1) * CHUNK])
                               for j in range(TILE_CHUNKS))
                dsink_vec = dsink_vec + jnp.where(lane4 == h, -head_sum, 0.0)
            dsink_ref[...] += dsink_vec
            drel_ref[...] += sum(dl_swa[j * 4 * CHUNK:(j + 1) * 4 * CHUNK] for j in range(TILE_CHUNKS))
            dk_parts, dv_parts = [], []
            for j, (rows, k_pairs, v_pairs, qp, pp) in enumerate(swa):
                dq, dk, dv = _attention_grads(dl_swa[j * 4 * CHUNK:(j + 1) * 4 * CHUNK], pp, do_pairs[j], qp, k_pairs,
                                              CHUNK)
                carry_dp[rows, col["sq"]] = (dq * QK_SCALE).astype(BF16)
                dk_parts.append(_swa_unvariants(*_split_pair_grads(dk)))
                dv_parts.append(_swa_unvariants(*_split_pair_grads(dv)))

            dc_pairs = _halves_bf16(dyc[:, A_WIDTH + SWA_WIDTH:])
            dp_mem = _attention_dprobs(dc_pairs, mv_pairs)
            dwout_ref[...] += _mm_tn(y_b, do_b)
            dl_mem, _ = _softmax_backward(pm, dp_mem)
            dmq, dmk, dmv = _attention_grads(dl_mem, ppm, dc_pairs, mqp, mk_pairs, TILE)
            carry_dp[:, col["mq"]] = (dmq * QK_SCALE).astype(BF16)
            dmkv_s[...] += jnp.concatenate([_mem_unvariants(*_split_pair_grads(dmk)),
                                            _mem_unvariants(*_split_pair_grads(dmv))], axis=-1)

            for parts_c, carry, cols in ((dk_parts, carry_k, col["sk"]), (dv_parts, carry_v, col["sv"])):
                @pl.when(i > 0)
                def _():
                    dproj_ref[:, cols] = (carry[...] + jnp.concatenate(
                        [jnp.zeros((TILE - CHUNK, KV_WIDTH), F32), parts_c[0][:CHUNK]], axis=0)).astype(BF16)
                new = [parts_c[0][CHUNK:]]
                for j in range(1, TILE_CHUNKS):
                    new[-1] = new[-1] + parts_c[j][:CHUNK]
                    new.append(parts_c[j][CHUNK:])
                carry[...] = jnp.concatenate(new, axis=0)

        @pl.when(i == n_tiles_ex)
        def _():
            dproj_ref[:, col["sk"]] = carry_k[...].astype(BF16)
            dproj_ref[:, col["sv"]] = carry_v[...].astype(BF16)
            d_b = dmkv_s[...].astype(BF16)
            dwmkv_ref[...] += _mm_tn(memn_s[...], d_b)
            dgmem_ref[...] += jnp.sum(_mm_nt(d_b, wmkv_ref[...]) * normalized_mem(), axis=0, keepdims=True)
            dh_s[...] = sum(project_back(part) for part in range(len(back_cols)))

        @pl.when(i > 0)
        def _():
            xv = xl_ref[...]
            r = lax.rsqrt(jnp.mean(xv * xv, axis=-1, keepdims=True) + EPS)
            xn = xv * r
            dh = dh_s[...] + _mm(dproj_ref[:, kv_cols], win_s[kv_cols, :])
            dgpre_ref[...] += jnp.sum(dh * xn, axis=0, keepdims=True)
            dhg = dh * gpre_ref[...]
            dx_ref[...] = (r * (dhg - xn * jnp.mean(dhg * xn, axis=-1, keepdims=True))
                           + carry_dout[lax.rem(i + 1, 2)])

        @pl.when((b == n_ex - 1) & (i == n_tiles_ex))
        def _():
            _fill_small_grads(dgpre_ref, dgpost_ref, dgmem_ref, dvg_ref, dvb_ref, dws_ref, dbs_ref, dsink_ref,
                              drel_ref, loss_ref, bk_ref, a_ref, b_ref)

    tile = functools.partial(_tile_specs, n_tiles_ex)
    prev = functools.partial(_prev_chunk_spec, n_tiles_ex)
    late = lambda width: pl.BlockSpec((TILE, width), lambda b, i: (b * n_tiles_ex + jnp.maximum(i - 1, 0), 0))
    vmem_f32 = lambda *shape: pltpu.VMEM(shape, F32)
    return pl.pallas_call(
        body, name="mix", grid=(n_ex, n_tiles_ex + 1),
        out_shape=[jax.ShapeDtypeStruct((n_tok, D_MODEL), F32), jax.ShapeDtypeStruct((n_tok, IN_WIDTH), BF16),
                   jax.ShapeDtypeStruct((D_MODEL, 2 * MEM_WIDTH), F32), jax.ShapeDtypeStruct((MIX_WIDTH, D_MODEL), F32),
                   jax.ShapeDtypeStruct((SMALL_A_ROWS, D_MODEL), F32), jax.ShapeDtypeStruct((SMALL_B_ROWS, 128), F32)],
        in_specs=[tile(A_WIDTH), tile(A_WIDTH), tile(SWA_WIDTH), tile(KV_WIDTH), tile(KV_WIDTH),
                  prev(KV_WIDTH), prev(KV_WIDTH), tile(MEM_WIDTH), tile(MIX_WIDTH),
                  pl.BlockSpec((1, MEM_LEN, D_MODEL), lambda b, i: (b, 0, 0)),
                  tile(D_MODEL), tile(D_MODEL),
                  _full_spec((1, A_WIDTH)), _full_spec((1, A_WIDTH)), _full_spec((A_GROUPS, CHUNK, CHUNK)),
                  _full_spec((A_GROUPS, CHUNK, CHUNK)), SMEM_SPEC, _full_spec((2, 4 * CHUNK, 2 * CHUNK)),
                  _full_spec((MIX_WIDTH, D_MODEL)), _full_spec((1, D_MODEL)), _full_spec((1, D_MODEL)),
                  _full_spec((D_MODEL, 2 * MEM_WIDTH)),
                  late(D_MODEL), _full_spec((1, D_MODEL)), _full_spec((CHUNK, 2 * CHUNK)), ANY_SPEC],
        out_specs=[late(D_MODEL), late(IN_WIDTH), _full_spec((D_MODEL, 2 * MEM_WIDTH)),
                   _full_spec((MIX_WIDTH, D_MODEL)), _full_spec((SMALL_A_ROWS, D_MODEL)),
                   _full_spec((SMALL_B_ROWS, 128))],
        scratch_shapes=[pltpu.VMEM((TILE, IN_WIDTH), BF16), pltpu.VMEM((TILE, KV_WIDTH), F32),
                        pltpu.VMEM((TILE, KV_WIDTH), F32), pltpu.VMEM((MEM_LEN, D_MODEL), BF16),
                        pltpu.VMEM((4, 2 * MEM_LEN, 128), BF16), pltpu.VMEM((MEM_LEN, 2 * MEM_WIDTH), F32),
                        pltpu.VMEM((2, TILE, D_MODEL), F32), pltpu.VMEM((IN_WIDTH, D_MODEL), BF16),
                        pltpu.SemaphoreType.DMA, vmem_f32(TILE, D_MODEL),
                        vmem_f32(1, D_MODEL), vmem_f32(1, D_MODEL), vmem_f32(1, D_MODEL), vmem_f32(1, A_WIDTH),
                        vmem_f32(1, A_WIDTH), vmem_f32(A_GROUPS, CHUNK, CHUNK), vmem_f32(CHUNK, A_WIDTH),
                        vmem_f32(1, 128), vmem_f32(4 * CHUNK, 2 * CHUNK), vmem_f32(1, 128)],
        compiler_params=pltpu.CompilerParams(vmem_limit_bytes=VMEM_LIMIT),
    )(au, av, sq, sk, sv, sk, sv, mq, z, mem, x2, tgt2, v_g, v_b, w_sp, b_sp, sinks, bias, w_out, g_post, g_mem,
      w_mkv, x2, g_pre, buckets, w_in_t)


def _fill_small_grads(dgpre_ref, dgpost_ref, dgmem_ref, dvg_ref, dvb_ref, dws_ref, dbs_ref, dsink_ref, drel_ref,
                      loss_ref, bk_ref, a_ref, b_ref):
    a_ref[...] = jnp.zeros_like(a_ref)
    b_ref[...] = jnp.zeros_like(b_ref)
    a_ref[0:1, :] = dgpre_ref[...]
    a_ref[1:2, :] = dgpost_ref[...]
    a_ref[2:3, :] = dgmem_ref[...]
    a_ref[3:4, :] = jnp.concatenate([dvg_ref[...], dvb_ref[...]], axis=-1)
    a_ref[ROW_LOSS:ROW_LOSS + 1, 0:128] = loss_ref[...]
    row = lax.broadcasted_iota(jnp.int32, (CHUNK, CHUNK), 0)
    col = lax.broadcasted_iota(jnp.int32, (CHUNK, CHUNK), 1)
    for g in range(A_GROUPS):
        b_ref[ROW_WS + g * CHUNK:ROW_WS + (g + 1) * CHUNK, :] = jnp.where(row >= col, dws_ref[g], 0.0)
        by_token = jnp.transpose(dbs_ref[:, g * 128:(g + 1) * 128])
        b_ref[ROW_BS + g:ROW_BS + g + 1, :] = jnp.sum(by_token, axis=0, keepdims=True)
    b_ref[ROW_SINK:ROW_SINK + 1, :] = dsink_ref[...]
    bk = bk_ref[...]
    rel_row = lax.broadcasted_iota(jnp.int32, (8, 128), 0)
    rel_col = lax.broadcasted_iota(jnp.int32, (8, 128), 1)
    rel = jnp.zeros((8, 128), F32)
    for h in range(4):
        acc = drel_ref[h * CHUNK:(h + 1) * CHUNK, :]
        for b in range(N_BUCKETS):
            rel = jnp.where((rel_row == h) & (rel_col == b), jnp.sum(jnp.where(bk == b, acc, 0.0)), rel)
    b_ref[ROW_REL:ROW_REL + 8, :] = rel


SHARD_ROWS = IN_WIDTH // N_CHIPS
SHARD_WINDOW = 768
SHARD_HALF = SHARD_ROWS // 2
DWIN_TILE = 2048
N_REL = N_CHIPS - 1


def _shard_window_start(shard):
    return (shard * SHARD_ROWS // 128) * 128


def _reduce_gradients(dproj, h, big, small, shard_arr):
    n_tok = h.shape[0]
    tile = min(DWIN_TILE, n_tok)
    n_sub = n_tok // tile
    last = N_CHIPS - 1
    n_big, n_small = len(big), len(small)
    big_half = [g.shape[2:] for g in big]
    sem_big_d2d = 2 * N_CHIPS
    sem_big_ici = sem_big_d2d + n_big
    sem_big_swap = sem_big_ici + N_REL * n_big
    sem_small_d2d = sem_big_swap + n_big
    sem_small_ici = sem_small_d2d + n_small
    n_sems = sem_small_ici + N_REL * n_small
    loc_small = n_big
    loc_out_win = loc_small + n_small
    loc_out_big = loc_out_win + 2
    loc_out_small = loc_out_big + 2 * n_big
    n_local = loc_out_small + n_small

    def relation_of_slot(s):
        return (s + 2) % N_REL + 1

    def shard_of_slot(s, my_shard):
        return my_shard ^ jnp.where(s == last, 0, relation_of_slot(s))

    def body(shard_ref, dp_ref, h_hbm, *refs):
        h_vmem, h_sem, refs = refs[-2], refs[-1], refs[:-2]
        big_hbm, refs = refs[:n_big], refs[n_big:]
        small_hbm, refs = refs[:n_small], refs[n_small:]
        out_hbm, refs = refs[0], refs[1:]
        big_out, refs = refs[:n_big], refs[n_big:]
        small_out, refs = refs[:n_small], refs[n_small:]
        part, recv_d2d, send_ici, recv_ici, mine_buf, other_buf = refs[:6]
        refs = refs[6:]
        big_own, big_recv, big_send, big_land, big_mine, big_other = (
            refs[k * n_big:(k + 1) * n_big] for k in range(6))
        refs = refs[6 * n_big:]
        small_own, small_recv, small_all = (refs[k * n_small:(k + 1) * n_small] for k in range(3))
        send_sems, recv_sems, local_sems = refs[3 * n_small:]

        s, t = pl.program_id(0), pl.program_id(1)
        x, y, c = lax.axis_index("x"), lax.axis_index("y"), lax.axis_index("c")
        my_chip = 2 * x + y
        sibling = (x, y, 1 - c)
        my_rows = pl.ds(pl.multiple_of(c * SHARD_HALF, 8), SHARD_HALF)
        other_rows = pl.ds(pl.multiple_of((1 - c) * SHARD_HALF, 8), SHARD_HALF)

        def remote(src, dst, k, to):
            return pltpu.make_async_remote_copy(src_ref=src, dst_ref=dst, send_sem=send_sems.at[k],
                                                recv_sem=recv_sems.at[k], device_id=to, device_id_type=MESH)

        def chip_at(rel):
            return (x ^ (rel >> 1), y ^ (rel & 1), c)

        def to_sibling(k):
            return remote(part.at[k % 2, other_rows, :], recv_d2d.at[k], k, sibling)

        def to_chip(k):
            return remote(send_ici.at[k], recv_ici.at[k], N_CHIPS + k, chip_at(relation_of_slot(k)))

        swap = remote(mine_buf, other_buf, 2 * N_CHIPS - 1, sibling)
        big_load = [pltpu.make_async_copy(big_hbm[w].at[:, pl.ds(c, 1)], big_own[w], local_sems.at[w])
                    for w in range(n_big)]
        big_to_sibling = [remote(big_hbm[w].at[:, pl.ds(1 - c, 1)], big_recv[w], sem_big_d2d + w, sibling)
                          for w in range(n_big)]
        big_to_chip = [[remote(big_send[w].at[k], big_land[w].at[k], sem_big_ici + N_REL * w + k, chip_at(k + 1))
                        for k in range(N_REL)] for w in range(n_big)]
        big_swap = [remote(big_mine[w], big_other[w], sem_big_swap + w, sibling) for w in range(n_big)]
        small_load = [pltpu.make_async_copy(small_hbm[i], small_own[i], local_sems.at[loc_small + i])
                      for i in range(n_small)]
        small_to_sibling = [remote(small_hbm[i], small_recv[i], sem_small_d2d + i, sibling) for i in range(n_small)]
        small_to_chip = [[remote(small_all[i].at[my_chip], small_all[i].at[my_chip],
                                 sem_small_ici + N_REL * i + k, chip_at(k + 1))
                          for k in range(N_REL)] for i in range(n_small)]

        h_loads = [pltpu.make_async_copy(h_hbm.at[rows, :], h_vmem.at[rows, :], h_sem.at[k]) for k, rows in enumerate(
            [pl.ds(0, tile)] + ([pl.ds(tile, n_tok - tile)] if n_sub > 1 else []))]

        @pl.when((s == 0) & (t == 0))
        def _():
            for cp in h_loads + big_load + big_to_sibling + small_load + small_to_sibling:
                cp.start()
            h_loads[0].wait()

        if n_sub > 1:
            @pl.when((s == 0) & (t == 1))
            def _():
                h_loads[1].wait()

        @pl.when((s == 0) & (t == n_sub - 1))
        def _():
            for cp in big_load + small_load:
                cp.wait()
            for cp in big_to_sibling + small_to_sibling:
                cp.wait_recv()
                cp.wait_send()
            for w in range(n_big):
                for k in range(N_REL):
                    shard = my_chip ^ (k + 1)
                    big_send[w][k] = (big_own[w][shard, 0] + big_recv[w][shard, 0]).astype(BF16)
                    big_to_chip[w][k].start()
            for i in range(n_small):
                small_all[i][my_chip] = small_own[i][...] + small_recv[i][...]
                for k in range(N_REL):
                    small_to_chip[i][k].start()

        @pl.when((s > 0) & (t == jnp.where(s == last, 0, min(1, n_sub - 1))))
        def _():
            k = s - 1
            cp = to_sibling(k)
            cp.wait_recv()
            cp.wait_send()
            send_ici[k] = (part[k % 2, my_rows, :] + recv_d2d[k]).astype(BF16)
            to_chip(k).start()

        def big_rows(w, half):
            rows = big_half[w][0]
            return big_out[w].at[pl.ds(pl.multiple_of(half * rows, 8), rows), :]

        big_store_mine = [pltpu.make_async_copy(big_mine[w], big_rows(w, c), local_sems.at[loc_out_big + 2 * w])
                          for w in range(n_big)]
        big_store_other = [pltpu.make_async_copy(big_other[w], big_rows(w, 1 - c),
                                                 local_sems.at[loc_out_big + 2 * w + 1]) for w in range(n_big)]
        small_store = [pltpu.make_async_copy(small_all[i], small_out[i], local_sems.at[loc_out_small + i])
                       for i in range(n_small)]

        @pl.when((s == last) & (t == 0))
        def _():
            for w in range(n_big):
                total = big_own[w][my_chip, 0] + big_recv[w][my_chip, 0]
                for k in range(N_REL):
                    big_to_chip[w][k].wait_recv()
                    total = total + big_land[w][k].astype(F32)
                big_mine[w][...] = total
                big_swap[w].start()
                big_store_mine[w].start()
            for i in range(n_small):
                for k in range(N_REL):
                    small_to_chip[i][k].wait_recv()
                small_store[i].start()

        r = _mm_tn(dp_ref[...], h_vmem[pl.ds(pl.multiple_of(t * tile, tile), tile), :])
        odd = shard_of_slot(s, shard_ref[0]) % 2
        for parity in range(2):
            rows = r[64 * parity:64 * parity + SHARD_ROWS]

            @pl.when((odd == parity) & (t == 0))
            def _():
                part[s % 2] = rows

            @pl.when((odd == parity) & (t > 0))
            def _():
                part[s % 2] += rows

        @pl.when(t == n_sub - 1)
        def _():
            to_sibling(s).start()

        @pl.when((s == last) & (t == n_sub - 1))
        def _():
            cp = to_sibling(last)
            cp.wait_recv()
            cp.wait_send()
            total = part[last % 2, my_rows, :] + recv_d2d[last]
            for k in range(last):
                to_chip(k).wait_recv()
                total = total + recv_ici[k].astype(F32)
            mine_buf[...] = total
            swap.start()
            out_mine = pltpu.make_async_copy(mine_buf, out_hbm.at[my_rows, :], local_sems.at[0])
            out_mine.start()
            swap.wait_recv()
            out_other = pltpu.make_async_copy(other_buf, out_hbm.at[other_rows, :], local_sems.at[1])
            out_other.start()
            for w in range(n_big):
                big_swap[w].wait_recv()
                big_store_other[w].start()
            stores = [out_mine, out_other] + big_store_mine + big_store_other + small_store
            for k in range(last):
                to_chip(k).wait_send()
            swap.wait_send()
            for w in range(n_big):
                for k in range(N_REL):
                    big_to_chip[w][k].wait_send()
                big_swap[w].wait_send()
            for i in range(n_small):
                for k in range(N_REL):
                    small_to_chip[i][k].wait_send()
            for cp in stores:
                cp.wait()

    half = (SHARD_HALF, D_MODEL)
    vmem = pltpu.VMEM
    scratch = [vmem((2, SHARD_ROWS, D_MODEL), F32), vmem((N_CHIPS,) + half, F32),
               vmem((N_REL,) + half, BF16), vmem((N_REL,) + half, BF16), vmem(half, F32), vmem(half, F32)]
    scratch += [vmem((N_CHIPS, 1) + hs, F32) for hs in big_half] * 2
    scratch += [vmem((N_REL,) + hs, BF16) for hs in big_half] * 2
    scratch += [vmem(hs, F32) for hs in big_half] * 2
    scratch += [vmem(a.shape, F32) for a in small] * 2 + [vmem((N_CHIPS,) + a.shape, F32) for a in small]
    scratch += [pltpu.SemaphoreType.DMA((n_sems,)), pltpu.SemaphoreType.DMA((n_sems,)),
                pltpu.SemaphoreType.DMA((n_local,)), vmem(h.shape, BF16), pltpu.SemaphoreType.DMA((2,))]
    n_hbm = n_big + n_small
    out = pl.pallas_call(
        body, name="reduce_gradients",
        out_shape=[jax.ShapeDtypeStruct((SHARD_ROWS, D_MODEL), F32)]
        + [jax.ShapeDtypeStruct((2 * hs[0], hs[1]), F32) for hs in big_half]
        + [jax.ShapeDtypeStruct((N_CHIPS,) + a.shape, F32) for a in small],
        grid_spec=pltpu.PrefetchScalarGridSpec(
            num_scalar_prefetch=1, grid=(N_CHIPS, n_sub),
            in_specs=[pl.BlockSpec((pl.Element(tile), pl.Element(SHARD_WINDOW)),
                                   lambda s, t, m: (t * tile, _shard_window_start(shard_of_slot(s, m[0])))),
                      ANY_SPEC] + [ANY_SPEC] * n_hbm,
            out_specs=[ANY_SPEC] * (1 + n_hbm),
            scratch_shapes=scratch),
        compiler_params=pltpu.CompilerParams(vmem_limit_bytes=VMEM_LIMIT),
    )(shard_arr, dproj, h, *big, *small)
    return out[:1 + n_big], out[1 + n_big:]


def _adamw(w, g, m, v):
    m2 = ADAM_B1 * m + (1.0 - ADAM_B1) * g
    v2 = ADAM_B2 * v + (1.0 - ADAM_B2) * (g * g)
    m_hat = m2 / (1.0 - ADAM_B1 ** ADAM_STEP)
    v_hat = v2 / (1.0 - ADAM_B2 ** ADAM_STEP)
    delta = -ADAM_LR * (m_hat / (jnp.sqrt(v_hat) + ADAM_EPS) + ADAM_WD * w)
    return delta, m2, v2


ADAM_STEPS = 2


def _adamw_all(shard_grads, shard_w, shard_m, shard_v, ra, rb, small_w, small_m, small_v):
    n_sh, n = len(shard_w), len(small_w)

    def body(*refs):
        sh_in, refs = refs[:4 * n_sh], refs[4 * n_sh:]
        ra_ref, rb_ref, refs = refs[0], refs[1], refs[2:]
        w_refs, m_refs, v_refs, refs = refs[:n], refs[n:2 * n], refs[2 * n:3 * n], refs[3 * n:]
        sh_out, outs = refs[:4 * n_sh], refs[4 * n_sh:]
        for k in range(n_sh):
            g = sh_in[k][...]
            delta, m2, v2 = _adamw(sh_in[n_sh + k][...], g, sh_in[2 * n_sh + k][...], sh_in[3 * n_sh + k][...])
            for ref, val in zip(sh_out[4 * k:4 * k + 4], (g, delta, m2, v2)):
                ref[...] = val

        @pl.when(pl.program_id(0) == 0)
        def _():
            g_outs, d_outs, m_outs, v_outs = outs[:n], outs[n:2 * n], outs[2 * n:3 * n], outs[3 * n:4 * n]
            ga, gb = ra_ref[0], rb_ref[0]
            for chip in range(1, N_CHIPS):
                ga = ga + ra_ref[chip]
                gb = gb + rb_ref[chip]
            outs[4 * n][...] = ga[ROW_LOSS:ROW_LOSS + 1, 0:128]
            grads = [ga[0:1, :], ga[1:2, :], ga[2:3, :], ga[3:4, :A_WIDTH], ga[3:4, A_WIDTH:],
                     gb[ROW_WS:ROW_WS + A_GROUPS * CHUNK, :].reshape(A_GROUPS, CHUNK, CHUNK),
                     gb[ROW_BS:ROW_BS + A_GROUPS, :], gb[ROW_SINK:ROW_SINK + 1, 0:4],
                     gb[ROW_REL:ROW_REL + 4, 0:N_BUCKETS]]
            for k in range(n):
                delta, m2, v2 = _adamw(w_refs[k][...], grads[k], m_refs[k][...], v_refs[k][...])
                g_outs[k][...] = grads[k]
                d_outs[k][...] = delta
                m_outs[k][...] = m2
                v_outs[k][...] = v2

    def rows_block(a):
        assert a.shape[0] % (8 * ADAM_STEPS) == 0
        return pl.BlockSpec((a.shape[0] // ADAM_STEPS, a.shape[1]), lambda i: (i, 0))

    sh_specs = [rows_block(w) for w in shard_w]
    small_in = [ra, rb, *small_w, *small_m, *small_v]
    small_out_shapes = [jax.ShapeDtypeStruct(w.shape, F32) for w in small_w] * 4 + [jax.ShapeDtypeStruct((1, 128), F32)]
    out = pl.pallas_call(
        body, name="adamw_all", grid=(ADAM_STEPS,),
        out_shape=[jax.ShapeDtypeStruct(w.shape, F32) for w in shard_w for _ in range(4)] + small_out_shapes,
        in_specs=sh_specs * 4 + [_full_spec(a.shape) for a in small_in],
        out_specs=[spec for spec in sh_specs for _ in range(4)] + [_full_spec(s.shape) for s in small_out_shapes],
        compiler_params=pltpu.CompilerParams(vmem_limit_bytes=VMEM_LIMIT),
    )(*shard_grads, *shard_w, *shard_m, *shard_v, *small_in)
    return [out[4 * k:4 * k + 4] for k in range(n_sh)], out[4 * n_sh:]


def kernel(x, mem, pre_norm_g, post_norm_g, mem_norm_g, w_in, w_mem_kv, v_norm_g, v_norm_b, w_spatial, b_spatial, attn_sinks, rel_bias, w_out, loss_target, m_pre_norm_g, m_post_norm_g, m_mem_norm_g, m_w_in, m_w_mem_kv, m_v_norm_g, m_v_norm_b, m_w_spatial, m_b_spatial, m_attn_sinks, m_rel_bias, m_w_out, v_pre_norm_g, v_post_norm_g, v_mem_norm_g, v_w_in, v_w_mem_kv, v_v_norm_g, v_v_norm_b, v_w_spatial, v_b_spatial, v_attn_sinks, v_rel_bias, v_w_out):
    n_ex, seq, _ = x.shape
    n_tok = n_ex * seq
    x2 = x.reshape(n_tok, D_MODEL)
    tgt2 = loss_target.reshape(n_tok, D_MODEL)
    buckets = jnp.asarray(_bucket_map())
    shard_arr = (2 * lax.axis_index("x") + lax.axis_index("y")).astype(jnp.int32).reshape(1)
    w_sp = w_spatial[0]
    w_in_t, m_w_in_t, v_w_in_t = (jnp.transpose(a[0]) for a in (w_in, m_w_in, v_w_in))
    rel_t, m_rel_t, v_rel_t = (jnp.transpose(a) for a in (rel_bias, m_rel_bias, v_rel_bias))

    x_arr = lax.axis_index("x").astype(jnp.int32).reshape(1)
    h_b, parts, (w_in_b, g_mkv, g_out), bias, b_sp = _gather_and_project(
        x2, pre_norm_g, w_in_t, w_mem_kv[0], w_out[0], rel_t, buckets, b_spatial[0], x_arr)
    w_mkv_b = g_mkv.reshape(D_MODEL, 2 * MEM_WIDTH)
    w_out_b = g_out.reshape(MIX_WIDTH, D_MODEL)

    dx, dproj, dwmkv, dwout, small_a, small_b = _mix(
        parts, mem, x2, tgt2, v_norm_g, v_norm_b, w_sp, b_sp, attn_sinks, bias, w_out_b, post_norm_g, mem_norm_g,
        w_mkv_b, pre_norm_g, w_in_b, buckets, n_ex, seq)

    shard_shapes = [w_mem_kv.shape[1:], w_out.shape[1:]]
    big = [g.reshape(N_CHIPS, 2, s[0] // 2, s[1]) for g, s in zip((dwmkv, dwout), shard_shapes)]
    (g_win, g_wmkv, g_wout), (ga, gb) = _reduce_gradients(dproj, h_b, big, [small_a, small_b], shard_arr)

    small_w = [pre_norm_g, post_norm_g, mem_norm_g, v_norm_g, v_norm_b, w_sp, b_spatial[0], attn_sinks, rel_t]
    small_m = [m_pre_norm_g, m_post_norm_g, m_mem_norm_g, m_v_norm_g, m_v_norm_b, m_w_spatial[0], m_b_spatial[0],
               m_attn_sinks, m_rel_t]
    small_v = [v_pre_norm_g, v_post_norm_g, v_mem_norm_g, v_v_norm_g, v_v_norm_b, v_w_spatial[0], v_b_spatial[0],
               v_attn_sinks, v_rel_t]
    big_out, small_out = _adamw_all(
        [g_win, g_wmkv, g_wout], [w_in_t, w_mem_kv[0], w_out[0]], [m_w_in_t, m_w_mem_kv[0], m_w_out[0]],
        [v_w_in_t, v_w_mem_kv[0], v_w_out[0]], ga, gb, small_w, small_m, small_v)
    n_small = len(small_w)

    outputs = [small_out[4 * n_small][0, 0], dx.reshape(x.shape)]
    for kind in range(4):
        s = small_out[kind * n_small:(kind + 1) * n_small]
        outputs += [s[0], s[1], s[2], jnp.transpose(big_out[0][kind])[None], big_out[1][kind][None], s[3], s[4],
                    s[5][None], s[6][None], s[7], jnp.transpose(s[8]), big_out[2][kind][None]]
    return tuple(outputs)
```

```python
import functools

import numpy as np
import jax
import jax.numpy as jnp
from jax import lax
from jax.experimental import pallas as pl
from jax.experimental.pallas import tpu as pltpu

F32 = jnp.float32
BF16 = jnp.bfloat16
MESH = pl.DeviceIdType.MESH

D_MODEL = 1024
CHUNK = 128
A_WIDTH = 512
A_GROUPS = 4
SWA_WIDTH = 256
KV_WIDTH = 128
MEM_WIDTH = 256
MEM_LEN = 256
MIX_WIDTH = 1024
IN_WIDTH = 2816
N_BUCKETS = 32
MAX_DISTANCE = 128
EPS = 1e-6
NEG = -1e30
QK_SCALE = 0.125
HALF_HEAD_PAIR = 64

ADAM_LR = 0.001
ADAM_B1 = 0.9
ADAM_B2 = 0.999
ADAM_EPS = 1e-08
ADAM_WD = 0.01
ADAM_STEP = 10

N_CHIPS = 4
TILE_CHUNKS = 2
TILE = TILE_CHUNKS * CHUNK
PROJ_TILE = 512
VMEM_LIMIT = 56 * 1024 * 1024

SMALL_A_ROWS = 8
ROW_LOSS = 4
ROW_WS = 0
ROW_BS = 512
ROW_SINK = 520
ROW_REL = 528
SMALL_B_ROWS = 536


def _mm(a, b):
    return lax.dot_general(a, b, (((1,), (0,)), ((), ())), preferred_element_type=F32)


def _mm_nt(a, b):
    return lax.dot_general(a, b, (((1,), (1,)), ((), ())), preferred_element_type=F32)


def _mm_tn(a, b):
    return lax.dot_general(a, b, (((0,), (0,)), ((), ())), preferred_element_type=F32)


def _bucket_map():
    qi = np.arange(CHUNK)[:, None]
    kj = np.arange(2 * CHUNK)[None, :]
    n = np.maximum(qi + CHUNK - kj, 0)
    max_exact = N_BUCKETS // 2
    large = max_exact + (np.log(np.maximum(n, 1) / max_exact) / np.log(MAX_DISTANCE / max_exact)
                         * (N_BUCKETS - max_exact)).astype(np.int32)
    large = np.minimum(large, N_BUCKETS - 1)
    return np.where(n < max_exact, n, large).astype(np.int32)


_GELU_C = 0.7978845608028654
_GELU_A = 0.044715
_GELU_K1 = 2.0 * _GELU_C
_GELU_K2 = 2.0 * _GELU_C * _GELU_A


def _gelu(x):
    x2 = x * x
    s = 1.0 / (1.0 + jnp.exp(x * (-_GELU_K1 - _GELU_K2 * x2)))
    return x * s, (s, x2)


def _gelu_grad(x, saved):
    s, x2 = saved
    return s + x * (s * (1.0 - s)) * (_GELU_K1 + 3.0 * _GELU_K2 * x2)


def _sigmoid(x):
    return 1.0 / (1.0 + jnp.exp(-x))


def _lane_lo(shape):
    return lax.broadcasted_iota(jnp.int32, shape, 1) < HALF_HEAD_PAIR


def _swa_variants(t):
    lo = _lane_lo(t.shape)
    tr = pltpu.roll(t, HALF_HEAD_PAIR, 1)
    zero = jnp.zeros_like(t)
    return (jnp.where(lo, t, zero).astype(BF16), jnp.where(lo, zero, tr).astype(BF16),
            jnp.where(lo, tr, zero).astype(BF16), jnp.where(lo, zero, t).astype(BF16))


def _swa_unvariants(d0, d1, d2, d3):
    lo = _lane_lo(d0.shape)
    zero = jnp.zeros_like(d0)
    rolled = jnp.where(lo, zero, d1) + jnp.where(lo, d2, zero)
    return jnp.where(lo, d0, zero) + jnp.where(lo, zero, d3) + pltpu.roll(rolled, HALF_HEAD_PAIR, 1)


def _mem_variants(t):
    out = []
    for pair in range(2):
        tp = t[:, pair * 128:(pair + 1) * 128]
        lo = _lane_lo(tp.shape)
        zero = jnp.zeros_like(tp)
        out.append(jnp.where(lo, tp, zero).astype(BF16))
        out.append(jnp.where(lo, zero, tp).astype(BF16))
    return out


def _mem_unvariants(d0, d1, d2, d3):
    lo = _lane_lo(d0.shape)
    return jnp.concatenate([jnp.where(lo, d0, d1), jnp.where(lo, d2, d3)], axis=-1)


def _softmax(logits, sinks):
    m = jnp.max(logits, axis=-1, keepdims=True)
    if sinks is not None:
        m = jnp.maximum(m, sinks)
    p = jnp.exp(logits - m)
    den = jnp.sum(p, axis=-1, keepdims=True)
    if sinks is None:
        return p * (1.0 / den), None
    es = jnp.exp(sinks - m)
    inv = 1.0 / (den + es)
    return p * inv, es * inv


def _band_valid(with_prev):
    qi = lax.broadcasted_iota(jnp.int32, (CHUNK, 2 * CHUNK), 0)
    kj = lax.broadcasted_iota(jnp.int32, (CHUNK, 2 * CHUNK), 1)
    in_cur = (kj >= CHUNK) & (kj - CHUNK <= qi)
    if not with_prev:
        return in_cur
    return in_cur | ((kj < CHUNK) & (kj > qi))


def _causal_weights(ws_ref):
    row = lax.broadcasted_iota(jnp.int32, (CHUNK, CHUNK), 0)
    col = lax.broadcasted_iota(jnp.int32, (CHUNK, CHUNK), 1)
    return [jnp.where(row >= col, ws_ref[g], 0.0).astype(BF16) for g in range(A_GROUPS)]


def _rows_to_lanes(a, n):
    return jnp.concatenate([a[c * CHUNK:(c + 1) * CHUNK] for c in range(n)], axis=1)


def _lanes_to_rows(a, n):
    w = a.shape[1] // n
    return jnp.concatenate([a[:, c * w:(c + 1) * w] for c in range(n)], axis=0)


def _stack_heads(pair01, pair23):
    return jnp.concatenate([pair01[:, :256], pair01[:, 256:], pair23[:, :256], pair23[:, 256:]], axis=0)


def _pair_heads(s, r):
    return (jnp.concatenate([s[0:r], s[r:2 * r]], axis=1), jnp.concatenate([s[2 * r:3 * r], s[3 * r:4 * r]], axis=1))


def _pair_operands(variants):
    return (jnp.concatenate(variants[0:2], axis=0), jnp.concatenate(variants[2:4], axis=0))


def _split_pair_grads(d_pairs):
    return d_pairs[0][:256], d_pairs[0][256:], d_pairs[1][:256], d_pairs[1][256:]


def _halves_bf16(a):
    return (a[:, :128].astype(BF16), a[:, 128:].astype(BF16))


def _group_a_forward(au, av, vg, vb, wm, bs_rows):
    gu, tu = _gelu(au)
    gv, tv = _gelu(av)
    ya, res = [], []
    for g in range(A_GROUPS):
        sl = slice(g * 128, (g + 1) * 128)
        xg = gv[:, sl]
        xc = xg - jnp.mean(xg, axis=-1, keepdims=True)
        rstd = lax.rsqrt(jnp.mean(xc * xc, axis=-1, keepdims=True) + EPS)
        xhat = xc * rstd
        vn = _rows_to_lanes((xhat * vg[:, sl] + vb[:, sl]).astype(BF16), TILE_CHUNKS)
        s = _lanes_to_rows(_mm(wm[g], vn), TILE_CHUNKS) + bs_rows[g]
        ya.append(gu[:, sl] * s)
        res.append((xhat, rstd, vn, s))
    return ya, dict(gu=gu, tu=tu, tv=tv, groups=res)


def _attention_logits(qp, k_pairs):
    return _stack_heads(_mm_nt(qp[0], k_pairs[0]), _mm_nt(qp[1], k_pairs[1]))


def _attention_out(p, v_pairs, r):
    pp = _pair_heads(p.astype(BF16), r)
    return jnp.concatenate([_mm(pp[0], v_pairs[0]), _mm(pp[1], v_pairs[1])], axis=-1), pp


def _attention_dprobs(do_pairs, v_pairs):
    return _stack_heads(_mm_nt(do_pairs[0], v_pairs[0]), _mm_nt(do_pairs[1], v_pairs[1]))


def _softmax_backward(p, dp):
    delta = jnp.sum(p * dp, axis=-1, keepdims=True)
    return p * (dp - delta), delta


def _attention_grads(dl, pp, do_pairs, qp, k_pairs, r):
    dlp = _pair_heads(dl.astype(BF16), r)
    dq = jnp.concatenate([_mm(dlp[0], k_pairs[0]), _mm(dlp[1], k_pairs[1])], axis=-1)
    dk = (_mm_tn(dlp[0], qp[0]), _mm_tn(dlp[1], qp[1]))
    dv = (_mm_tn(pp[0], do_pairs[0]), _mm_tn(pp[1], do_pairs[1]))
    return dq, dk, dv


def _tile_specs(n_tiles_ex, width):
    return pl.BlockSpec((TILE, width), lambda b, i: (b * n_tiles_ex + jnp.minimum(i, n_tiles_ex - 1), 0))


def _prev_chunk_spec(n_tiles_ex, width):
    def index(b, i):
        chunk = TILE_CHUNKS * jnp.minimum(i, n_tiles_ex - 1)
        return (b * n_tiles_ex * TILE_CHUNKS + jnp.maximum(chunk - 1, 0), 0)
    return pl.BlockSpec((CHUNK, width), index)


def _full_spec(shape):
    zeros = (0,) * len(shape)
    return pl.BlockSpec(shape, lambda *_: zeros)


SMEM_SPEC = pl.BlockSpec(memory_space=pltpu.SMEM)
ANY_SPEC = pl.BlockSpec(memory_space=pl.ANY)


def _fill_bias(rel_ref, bk_ref, out_ref):
    bk = bk_ref[...]
    for h in range(4):
        acc = jnp.zeros((CHUNK, 2 * CHUNK), F32)
        for b in range(N_BUCKETS):
            acc = jnp.where(bk == b, rel_ref[h, b], acc)
        for t, with_prev in enumerate((True, False)):
            out_ref[t, h * CHUNK:(h + 1) * CHUNK, :] = jnp.where(_band_valid(with_prev), acc, NEG)


PROJ_WIDTHS = (A_WIDTH, A_WIDTH, SWA_WIDTH, KV_WIDTH, KV_WIDTH, MEM_WIDTH, MIX_WIDTH)
PROJ_OFFSETS = tuple(int(v) for v in np.cumsum((0,) + PROJ_WIDTHS))


MXU_TILE = 256
HALF_WIDTH = IN_WIDTH // 2
PHASE_COLS = (HALF_WIDTH // MXU_TILE * MXU_TILE, IN_WIDTH - HALF_WIDTH // MXU_TILE * MXU_TILE)


def _phase_columns(phase, chip_x):
    if phase == 0:
        return 0 if chip_x == 0 else IN_WIDTH - PHASE_COLS[0]
    return PHASE_COLS[0] if chip_x == 0 else 0


def _phase_parts(phase, chip_x):
    start = _phase_columns(phase, chip_x)
    return [(k, PROJ_OFFSETS[k] - start) for k in range(len(PROJ_WIDTHS))
            if start <= PROJ_OFFSETS[k] and PROJ_OFFSETS[k + 1] <= start + PHASE_COLS[phase]]


def _gather_and_project(x2, g_pre, w_in_s, w_mkv_s, w_out_s, rel_bias_t, buckets, b_spatial, x_arr):
    n_tok = x2.shape[0]
    n_tiles = n_tok // PROJ_TILE
    last = n_tiles - 1
    shapes = [w_in_s.shape, w_mkv_s.shape, w_out_s.shape]
    n_w = len(shapes)

    def body(x_sref, x_ref, g_ref, win_hbm, wmkv_hbm, wout_hbm, rel_ref, bk_ref, bsp_ref, h_ref, *refs):
        part_refs, refs = refs[:len(PROJ_WIDTHS)], refs[len(PROJ_WIDTHS):]
        bias_ref, bs_ref, refs = refs[0], refs[1], refs[2:]
        gin_hbm, gmkv_hbm, gout_hbm, wg, stage_in, stage_mkv, stage_out, own_mkv, own_out, h_all = refs[:10]
        send_sems, recv_sems, local_sems = refs[10:]
        p, t = pl.program_id(0), pl.program_id(1)
        x, y, c = lax.axis_index("x"), lax.axis_index("y"), lax.axis_index("c")
        me, sibling = (x, y, c), (x, y, 1 - c)
        my_shard = 2 * x + y
        gathered = [wg, gmkv_hbm, gout_hbm]

        def half_rows(w, shard, half):
            rows = shapes[w][0] // 2
            if w == 0:
                return wg.at[pl.ds(pl.multiple_of(shard * shapes[0][0] + half * rows, 16), rows), :]
            return gathered[w].at[shard, pl.ds(half * rows, rows), :]

        def first(w, rel):
            src = half_rows(w, my_shard, c) if w == 0 else (own_mkv, own_out)[w - 1].at[
                pl.ds(c * (shapes[w][0] // 2), shapes[w][0] // 2), :]
            k = 3 * w + rel - 1
            return pltpu.make_async_remote_copy(
                src_ref=src, dst_ref=half_rows(w, my_shard, c), send_sem=send_sems.at[k], recv_sem=recv_sems.at[k],
                device_id=(x ^ (rel >> 1), y ^ (rel & 1), c), device_id_type=MESH)

        def landed(w, rel):
            k = 3 * w + rel - 1
            ref = half_rows(w, my_shard ^ rel, c)
            return pltpu.make_async_remote_copy(src_ref=ref, dst_ref=ref, send_sem=send_sems.at[k],
                                                recv_sem=recv_sems.at[k], device_id=me, device_id_type=MESH)

        def passed(w, rel, half, to):
            k = 9 + 3 * w + rel - 1
            ref = half_rows(w, my_shard ^ rel, half)
            return pltpu.make_async_remote_copy(src_ref=ref, dst_ref=ref, send_sem=send_sems.at[k],
                                                recv_sem=recv_sems.at[k], device_id=to, device_id_type=MESH)

        def pass_on(w, rels):
            for rel in rels:
                landed(w, rel).wait_recv()
                passed(w, rel, c, sibling).start()
            for rel in rels:
                passed(w, rel, 1 - c, me).wait_recv()

        own_stores = [pltpu.make_async_copy(own_mkv, gmkv_hbm.at[my_shard], local_sems.at[3]),
                      pltpu.make_async_copy(own_out, gout_hbm.at[my_shard], local_sems.at[4])]

        @pl.when((p == 0) & (t == 0))
        def _():
            half_rows_in = shapes[0][0] // 2
            halves = [pl.ds(pl.multiple_of(hc * half_rows_in, 8), half_rows_in) for hc in (c, 1 - c)]
            loads = [pltpu.make_async_copy(win_hbm.at[halves[0], :], stage_in.at[halves[0], :], local_sems.at[0]),
                     pltpu.make_async_copy(wmkv_hbm, stage_mkv, local_sems.at[1]),
                     pltpu.make_async_copy(wout_hbm, stage_out, local_sems.at[2]),
                     pltpu.make_async_copy(win_hbm.at[halves[1], :], stage_in.at[halves[1], :], local_sems.at[6])]
            for cp in (loads[0], loads[3], loads[1], loads[2]):
                cp.start()
            loads[0].wait()
            half_rows(0, my_shard, c)[...] = stage_in[halves[0], :].astype(BF16)
            for rel in (1, 2):
                first(0, rel).start()
            loads[3].wait()
            half_rows(0, my_shard, 1 - c)[...] = stage_in[halves[1], :].astype(BF16)
            loads[1].wait()
            loads[2].wait()
            own_mkv[...] = stage_mkv[...].astype(BF16)
            own_out[...] = stage_out[...].astype(BF16)
            for cp in own_stores:
                cp.start()
            _fill_bias(rel_ref, bk_ref, bias_ref)
            for g in range(A_GROUPS):
                bs_ref[g] = jnp.transpose(jnp.broadcast_to(bsp_ref[g:g + 1, :], (CHUNK, CHUNK)))
            pass_on(0, (1,))
            first(0, 3).start()

        @pl.when((p == 0) & (t == n_tiles // 2))
        def _():
            for w in (1, 2):
                for rel in (1, 2, 3):
                    first(w, rel).start()

        store = pltpu.make_async_copy(wg, gin_hbm, local_sems.at[5])

        @pl.when((p == 1) & (t == 0))
        def _():
            pass_on(0, (2, 3))
            store.start()

        @pl.when((p == 1) & (t == n_tiles // 2))
        def _():
            for w in (1, 2):
                pass_on(w, (1, 2, 3))

        tile_rows = pl.ds(pl.multiple_of(t * PROJ_TILE, PROJ_TILE), PROJ_TILE)

        def project(h, phase):
            start = jnp.where(x_sref[0] == 0, _phase_columns(phase, 0), _phase_columns(phase, 1))
            proj = _mm_nt(h, wg[pl.ds(pl.multiple_of(start, MXU_TILE), PHASE_COLS[phase]), :])
            for chip_x in range(2):
                @pl.when(x_sref[0] == chip_x)
                def _():
                    for k, lo in _phase_parts(phase, chip_x):
                        part_refs[k][...] = proj[:, lo:lo + PROJ_WIDTHS[k]].astype(BF16)

        @pl.when(p == 0)
        def _():
            xv = x_ref[...]
            r = lax.rsqrt(jnp.mean(xv * xv, axis=-1, keepdims=True) + EPS)
            h = (xv * r * g_ref[...]).astype(BF16)
            h_ref[...] = h
            h_all[tile_rows, :] = h
            project(h, 0)

        @pl.when(p == 1)
        def _():
            project(h_all[tile_rows, :], 1)

        @pl.when((p == 1) & (t == last))
        def _():
            for w in range(n_w):
                for rel in (1, 2, 3):
                    first(w, rel).wait_send()
                    passed(w, rel, c, sibling).wait_send()
            for cp in own_stores:
                cp.wait()
            store.wait()

    def written_in(k):
        phase_on = [next(ph for ph in range(2) if k in dict(_phase_parts(ph, chip_x))) for chip_x in range(2)]

        def index(p, t, xs):
            phase = jnp.where(xs[0] == 0, phase_on[0], phase_on[1])
            return (jnp.where(p == phase, t, jnp.where(p < phase, 0, last)), 0)
        return index

    part_specs = [pl.BlockSpec((PROJ_TILE, PROJ_WIDTHS[k]), written_in(k)) for k in range(len(PROJ_WIDTHS))]
    vmem = pltpu.VMEM
    out = pl.pallas_call(
        body, name="gather_and_project",
        out_shape=[jax.ShapeDtypeStruct((n_tok, D_MODEL), BF16)]
        + [jax.ShapeDtypeStruct((n_tok, w), BF16) for w in PROJ_WIDTHS]
        + [jax.ShapeDtypeStruct((2, 4 * CHUNK, 2 * CHUNK), F32), jax.ShapeDtypeStruct((A_GROUPS, CHUNK, CHUNK), F32)]
        + [jax.ShapeDtypeStruct((N_CHIPS * shapes[0][0], shapes[0][1]), BF16)]
        + [jax.ShapeDtypeStruct((N_CHIPS,) + s, BF16) for s in shapes[1:]],
        grid_spec=pltpu.PrefetchScalarGridSpec(
            num_scalar_prefetch=1, grid=(2, n_tiles),
            in_specs=[pl.BlockSpec((PROJ_TILE, D_MODEL), lambda p, t, xs: (jnp.where(p == 0, t, last), 0)),
                      pl.BlockSpec((1, D_MODEL), lambda p, t, xs: (0, 0)), ANY_SPEC, ANY_SPEC, ANY_SPEC, SMEM_SPEC,
                      pl.BlockSpec(buckets.shape, lambda p, t, xs: (0, 0)),
                      pl.BlockSpec(b_spatial.shape, lambda p, t, xs: (0, 0))],
            out_specs=[pl.BlockSpec((PROJ_TILE, D_MODEL), lambda p, t, xs: (jnp.where(p == 0, t, last), 0))]
            + part_specs + [pl.BlockSpec((2, 4 * CHUNK, 2 * CHUNK), lambda p, t, xs: (0, 0, 0)),
                            pl.BlockSpec((A_GROUPS, CHUNK, CHUNK), lambda p, t, xs: (0, 0, 0))] + [ANY_SPEC] * 3,
            scratch_shapes=[vmem((N_CHIPS * shapes[0][0], shapes[0][1]), BF16), vmem(shapes[0], F32),
                            vmem(shapes[1], F32), vmem(shapes[2], F32), vmem(shapes[1], BF16), vmem(shapes[2], BF16),
                            vmem((n_tok, D_MODEL), BF16),
                            pltpu.SemaphoreType.DMA((18,)), pltpu.SemaphoreType.DMA((18,)),
                            pltpu.SemaphoreType.DMA((7,))]),
        compiler_params=pltpu.CompilerParams(vmem_limit_bytes=VMEM_LIMIT),
    )(x_arr, x2, g_pre, w_in_s, w_mkv_s, w_out_s, rel_bias_t, buckets, b_spatial)
    n_parts = len(PROJ_WIDTHS)
    return out[0], list(out[1:1 + n_parts]), out[3 + n_parts:], out[1 + n_parts], out[2 + n_parts]


def _load_chunk(j, i, sk_ref, sv_ref, skp_ref, svp_ref):
    rows = slice(j * CHUNK, (j + 1) * CHUNK)
    if j == 0:
        k_prev, v_prev, table = skp_ref[...], svp_ref[...], jnp.where(i > 0, 0, 1)
    else:
        prev = slice((j - 1) * CHUNK, j * CHUNK)
        k_prev, v_prev, table = sk_ref[prev, :], sv_ref[prev, :], 0
    k_pairs = _pair_operands(_swa_variants(jnp.concatenate([k_prev, sk_ref[rows, :]], axis=0).astype(F32)))
    v_pairs = _pair_operands(_swa_variants(jnp.concatenate([v_prev, sv_ref[rows, :]], axis=0).astype(F32)))
    return rows, k_pairs, v_pairs, table


def _tile_constants(ws_ref, bs_ref, sink_ref):
    wm = _causal_weights(ws_ref)
    bs_rows = [jnp.concatenate([bs_ref[g]] * TILE_CHUNKS, axis=0) for g in range(A_GROUPS)]
    sink_col = jnp.max(jnp.concatenate([jnp.full((CHUNK, 128), sink_ref[0, h], F32) for h in range(4)] * TILE_CHUNKS,
                                       axis=0), axis=-1, keepdims=True)
    return wm, bs_rows, sink_col


def _mix(parts, mem, x2, tgt2, v_g, v_b, w_sp, b_sp, sinks, bias, w_out, g_post, g_mem, w_mkv, g_pre, w_in_t, buckets,
         n_ex, seq):
    n_tiles_ex = seq // TILE
    n_tok = n_ex * seq
    au, av, sq, sk, sv, mq, z = parts
    col = dict(zip(("au", "av", "sq", "sk", "sv", "mq", "z"),
                   (slice(PROJ_OFFSETS[k], PROJ_OFFSETS[k + 1]) for k in range(len(PROJ_WIDTHS)))))
    before_kv, after_kv = slice(0, col["sk"].start), slice(col["sv"].stop, IN_WIDTH)
    kv_cols = slice(col["sk"].start, col["sv"].stop)
    gated = slice(col["au"].start, col["av"].stop)
    cut_a, cut_z = (s.start + 3 * (s.stop - s.start) // 4 for s in (gated, col["z"]))
    back_cols = ((slice(gated.start, cut_a),), (slice(col["z"].start, cut_z),),
                 (slice(cut_a, gated.stop), slice(cut_z, col["z"].stop)), (col["sq"], col["mq"]))
    assert sum(s.stop - s.start for part in back_cols for s in part) == IN_WIDTH - 2 * KV_WIDTH

    def body(au_ref, av_ref, sq_ref, sk_ref, sv_ref, skp_ref, svp_ref, mq_ref, z_ref, mem_ref, x_ref, tgt_ref,
             vg_ref, vb_ref, ws_ref, bs_ref, sink_ref, bias_ref, wout_ref, gpost_ref, gmem_ref, wmkv_ref,
             xl_ref, gpre_ref, bk_ref, win_hbm,
             dx_ref, dproj_ref, dwmkv_ref, dwout_ref, a_ref, b_ref,
             carry_dp, carry_k, carry_v, memn_s, mem_ops, dmkv_s, carry_dout, win_s, win_sem, dh_s,
             dgpre_ref, dgpost_ref, dgmem_ref, dvg_ref, dvb_ref, dws_ref, dbs_ref, dsink_ref, drel_ref, loss_ref):
        b, i = pl.program_id(0), pl.program_id(1)
        win_load = pltpu.make_async_copy(win_hbm, win_s, win_sem)

        @pl.when((b == 0) & (i == 0))
        def _():
            win_load.start()
            for ref in (dwmkv_ref, dwout_ref, dgpre_ref, dgpost_ref, dgmem_ref, dvg_ref, dvb_ref, dws_ref, dbs_ref,
                        dsink_ref, drel_ref, loss_ref, carry_dp):
                ref[...] = jnp.zeros_like(ref)

        def normalized_mem():
            m = mem_ref[0]
            return m * lax.rsqrt(jnp.mean(m * m, axis=-1, keepdims=True) + EPS)

        @pl.when(i == 0)
        def _():
            memn_s[...] = (normalized_mem() * gmem_ref[...]).astype(BF16)
            mkv = _mm(memn_s[...], wmkv_ref[...])
            for k, pair in enumerate(_pair_operands(_mem_variants(mkv[:, :MEM_WIDTH]))
                                     + _pair_operands(_mem_variants(mkv[:, MEM_WIDTH:]))):
                mem_ops[k] = pair
            dmkv_s[...] = jnp.zeros_like(dmkv_s)
            carry_k[...] = jnp.zeros_like(carry_k)
            carry_v[...] = jnp.zeros_like(carry_v)

        @pl.when((b == 0) & (i == 0))
        def _():
            win_load.wait()

        @pl.when(i > 0)
        def _():
            dproj_ref[:, before_kv] = carry_dp[:, before_kv]
            dproj_ref[:, after_kv] = carry_dp[:, after_kv]

        def project_back(part):
            return sum(_mm(carry_dp[:, s], win_s[s, :]) for s in back_cols[part])

        @pl.when(i < n_tiles_ex)
        def _():
            dh_s[...] = project_back(0)
            wm, bs_rows, sink_col = _tile_constants(ws_ref, bs_ref, sink_ref)
            mk_pairs, mv_pairs = (mem_ops[0], mem_ops[1]), (mem_ops[2], mem_ops[3])
            vg = vg_ref[...]

            au_v, av_v = au_ref[...].astype(F32), av_ref[...].astype(F32)
            ya, res = _group_a_forward(au_v, av_v, vg, vb_ref[...], wm, bs_rows)
            swa, logits, yb = [], [], []
            for j in range(TILE_CHUNKS):
                rows, k_pairs, v_pairs, table = _load_chunk(j, i, sk_ref, sv_ref, skp_ref, svp_ref)
                qp = _halves_bf16(sq_ref[rows, :] * QK_SCALE)
                logits.append(_attention_logits(qp, k_pairs) + bias_ref[table])
                swa.append([rows, k_pairs, v_pairs, qp])
            dh_s[...] += project_back(1)
            mqp = _halves_bf16(mq_ref[...] * QK_SCALE)
            logits_mem = _attention_logits(mqp, mk_pairs)
            p_swa, sink_p = _softmax(jnp.concatenate(logits, axis=0), sink_col)
            for j in range(TILE_CHUNKS):
                out, pp = _attention_out(p_swa[j * 4 * CHUNK:(j + 1) * 4 * CHUNK], swa[j][2], CHUNK)
                yb.append(out)
                swa[j].append(pp)
            pm, _ = _softmax(logits_mem, None)
            yc, ppm = _attention_out(pm, mv_pairs, TILE)
            ycat = jnp.concatenate(ya + [jnp.concatenate(yb, axis=0), yc], axis=-1)

            zv = z_ref[...].astype(F32)
            sig = _sigmoid(zv)
            sz = zv * sig
            y_b = (ycat * sz).astype(BF16)
            halves = (slice(0, TILE // 2), slice(TILE // 2, TILE))
            o_halves = [_mm(y_b[rows], wout_ref[...]) for rows in halves]
            dh_s[...] += project_back(2)
            gp = gpost_ref[...]
            do_halves = []
            for rows, o in zip(halves, o_halves):
                r2 = lax.rsqrt(jnp.mean(o * o, axis=-1, keepdims=True) + EPS)
                nrm = o * r2
                diff = x_ref[rows, :] + nrm * gp - tgt_ref[rows, :]
                loss_ref[...] += jnp.sum(diff * diff) * (0.5 / D_MODEL)
                dout = diff * (1.0 / D_MODEL)
                carry_dout[lax.rem(i, 2), rows, :] = dout
                dgpost_ref[...] += jnp.sum(dout * nrm, axis=0, keepdims=True)
                dn = dout * gp
                do_halves.append((r2 * (dn - nrm * jnp.mean(dn * nrm, axis=-1, keepdims=True))).astype(BF16))
            do_b = jnp.concatenate(do_halves, axis=0)
            dy = _mm_nt(do_b, wout_ref[...])
            carry_dp[:, col["z"]] = (dy * ycat * (sig + sz * (1.0 - sig))).astype(BF16)
            dyc = dy * sz

            dgu, dgv = [], []
            for g in range(A_GROUPS):
                sl = slice(g * 128, (g + 1) * 128)
                xhat, rstd, vn, s = res["groups"][g]
                dya = dyc[:, sl]
                dgu.append(dya * s)
                ds = dya * res["gu"][:, sl]
                dbs_ref[:, sl] += sum(ds[c * CHUNK:(c + 1) * CHUNK] for c in range(TILE_CHUNKS))
                ds_b = _rows_to_lanes(ds.astype(BF16), TILE_CHUNKS)
                dws_ref[g] += _mm_nt(ds_b, vn)
                dvn = _lanes_to_rows(_mm_tn(wm[g], ds_b), TILE_CHUNKS)
                dvg_ref[:, sl] += jnp.sum(dvn * xhat, axis=0, keepdims=True)
                dvb_ref[:, sl] += jnp.sum(dvn, axis=0, keepdims=True)
                dxh = dvn * vg[:, sl]
                dgv.append(rstd * (dxh - jnp.mean(dxh, axis=-1, keepdims=True)
                                   - xhat * jnp.mean(dxh * xhat, axis=-1, keepdims=True)))
            carry_dp[:, col["au"]] =(jnp.concatenate(dgu, axis=-1) * _gelu_grad(au_v, res["tu"])).astype(BF16)
            carry_dp[:, col["av"]] = (jnp.concatenate(dgv, axis=-1) * _gelu_grad(av_v, res["tv"])).astype(BF16)

            do_pairs = [_halves_bf16(dyc[rows, A_WIDTH:A_WIDTH + SWA_WIDTH]) for rows, *_ in swa]
            dp_swa = jnp.concatenate(
                [_attention_dprobs(do_pairs[j], swa[j][2]) for j in range(TILE_CHUNKS)], axis=0)
            dh_s[...] += project_back(3)
            dl_swa, delta = _softmax_backward(p_swa, dp_swa)
            sink_terms = sink_p * delta
            lane4 = lax.broadcasted_iota(jnp.int32, (1, 128), 1)
            dsink_vec = jnp.zeros((1, 128), F32)
            for h in range(4):
                head_sum = sum(jnp.sum(sink_terms[(4 * j + h) * CHUNK:(4 * j + h + 1) * CHUNK])
                               for j in range(TILE_CHUNKS))
                dsink_vec = dsink_vec + jnp.where(lane4 == h, -head_sum, 0.0)
            dsink_ref[...] += dsink_vec
            drel_ref[...] += sum(dl_swa[j * 4 * CHUNK:(j + 1) * 4 * CHUNK] for j in range(TILE_CHUNKS))
            dk_parts, dv_parts = [], []
            for j, (rows, k_pairs, v_pairs, qp, pp) in enumerate(swa):
                dq, dk, dv = _attention_grads(dl_swa[j * 4 * CHUNK:(j + 1) * 4 * CHUNK], pp, do_pairs[j], qp, k_pairs,
                                              CHUNK)
                carry_dp[rows, col["sq"]] = (dq * QK_SCALE).astype(BF16)
                dk_parts.append(_swa_unvariants(*_split_pair_grads(dk)))
                dv_parts.append(_swa_unvariants(*_split_pair_grads(dv)))

            dc_pairs = _halves_bf16(dyc[:, A_WIDTH + SWA_WIDTH:])
            dp_mem = _attention_dprobs(dc_pairs, mv_pairs)
            dwout_ref[...] += _mm_tn(y_b, do_b)
            dl_mem, _ = _softmax_backward(pm, dp_mem)
            dmq, dmk, dmv = _attention_grads(dl_mem, ppm, dc_pairs, mqp, mk_pairs, TILE)
            carry_dp[:, col["mq"]] = (dmq * QK_SCALE).astype(BF16)
            dmkv_s[...] += jnp.concatenate([_mem_unvariants(*_split_pair_grads(dmk)),
                                            _mem_unvariants(*_split_pair_grads(dmv))], axis=-1)

            for parts_c, carry, cols in ((dk_parts, carry_k, col["sk"]), (dv_parts, carry_v, col["sv"])):
                @pl.when(i > 0)
                def _():
                    dproj_ref[:, cols] = (carry[...] + jnp.concatenate(
                        [jnp.zeros((TILE - CHUNK, KV_WIDTH), F32), parts_c[0][:CHUNK]], axis=0)).astype(BF16)
                new = [parts_c[0][CHUNK:]]
                for j in range(1, TILE_CHUNKS):
                    new[-1] = new[-1] + parts_c[j][:CHUNK]
                    new.append(parts_c[j][CHUNK:])
                carry[...] = jnp.concatenate(new, axis=0)

        @pl.when(i == n_tiles_ex)
        def _():
            dproj_ref[:, col["sk"]] = carry_k[...].astype(BF16)
            dproj_ref[:, col["sv"]] = carry_v[...].astype(BF16)
            d_b = dmkv_s[...].astype(BF16)
            dwmkv_ref[...] += _mm_tn(memn_s[...], d_b)
            dgmem_ref[...] += jnp.sum(_mm_nt(d_b, wmkv_ref[...]) * normalized_mem(), axis=0, keepdims=True)
            dh_s[...] = sum(project_back(part) for part in range(len(back_cols)))

        @pl.when(i > 0)
        def _():
            xv = xl_ref[...]
            r = lax.rsqrt(jnp.mean(xv * xv, axis=-1, keepdims=True) + EPS)
            xn = xv * r
            dh = dh_s[...] + _mm(dproj_ref[:, kv_cols], win_s[kv_cols, :])
            dgpre_ref[...] += jnp.sum(dh * xn, axis=0, keepdims=True)
            dhg = dh * gpre_ref[...]
            dx_ref[...] = (r * (dhg - xn * jnp.mean(dhg * xn, axis=-1, keepdims=True))
                           + carry_dout[lax.rem(i + 1, 2)])

        @pl.when((b == n_ex - 1) & (i == n_tiles_ex))
        def _():
            _fill_small_grads(dgpre_ref, dgpost_ref, dgmem_ref, dvg_ref, dvb_ref, dws_ref, dbs_ref, dsink_ref,
                              drel_ref, loss_ref, bk_ref, a_ref, b_ref)

    tile = functools.partial(_tile_specs, n_tiles_ex)
    prev = functools.partial(_prev_chunk_spec, n_tiles_ex)
    late = lambda width: pl.BlockSpec((TILE, width), lambda b, i: (b * n_tiles_ex + jnp.maximum(i - 1, 0), 0))
    vmem_f32 = lambda *shape: pltpu.VMEM(shape, F32)
    return pl.pallas_call(
        body, name="mix", grid=(n_ex, n_tiles_ex + 1),
        out_shape=[jax.ShapeDtypeStruct((n_tok, D_MODEL), F32), jax.ShapeDtypeStruct((n_tok, IN_WIDTH), BF16),
                   jax.ShapeDtypeStruct((D_MODEL, 2 * MEM_WIDTH), F32), jax.ShapeDtypeStruct((MIX_WIDTH, D_MODEL), F32),
                   jax.ShapeDtypeStruct((SMALL_A_ROWS, D_MODEL), F32), jax.ShapeDtypeStruct((SMALL_B_ROWS, 128), F32)],
        in_specs=[tile(A_WIDTH), tile(A_WIDTH), tile(SWA_WIDTH), tile(KV_WIDTH), tile(KV_WIDTH),
                  prev(KV_WIDTH), prev(KV_WIDTH), tile(MEM_WIDTH), tile(MIX_WIDTH),
                  pl.BlockSpec((1, MEM_LEN, D_MODEL), lambda b, i: (b, 0, 0)),
                  tile(D_MODEL), tile(D_MODEL),
                  _full_spec((1, A_WIDTH)), _full_spec((1, A_WIDTH)), _full_spec((A_GROUPS, CHUNK, CHUNK)),
                  _full_spec((A_GROUPS, CHUNK, CHUNK)), SMEM_SPEC, _full_spec((2, 4 * CHUNK, 2 * CHUNK)),
                  _full_spec((MIX_WIDTH, D_MODEL)), _full_spec((1, D_MODEL)), _full_spec((1, D_MODEL)),
                  _full_spec((D_MODEL, 2 * MEM_WIDTH)),
                  late(D_MODEL), _full_spec((1, D_MODEL)), _full_spec((CHUNK, 2 * CHUNK)), ANY_SPEC],
        out_specs=[late(D_MODEL), late(IN_WIDTH), _full_spec((D_MODEL, 2 * MEM_WIDTH)),
                   _full_spec((MIX_WIDTH, D_MODEL)), _full_spec((SMALL_A_ROWS, D_MODEL)),
                   _full_spec((SMALL_B_ROWS, 128))],
        scratch_shapes=[pltpu.VMEM((TILE, IN_WIDTH), BF16), pltpu.VMEM((TILE, KV_WIDTH), F32),
                        pltpu.VMEM((TILE, KV_WIDTH), F32), pltpu.VMEM((MEM_LEN, D_MODEL), BF16),
                        pltpu.VMEM((4, 2 * MEM_LEN, 128), BF16), pltpu.VMEM((MEM_LEN, 2 * MEM_WIDTH), F32),
                        pltpu.VMEM((2, TILE, D_MODEL), F32), pltpu.VMEM((IN_WIDTH, D_MODEL), BF16),
                        pltpu.SemaphoreType.DMA, vmem_f32(TILE, D_MODEL),
                        vmem_f32(1, D_MODEL), vmem_f32(1, D_MODEL), vmem_f32(1, D_MODEL), vmem_f32(1, A_WIDTH),
                        vmem_f32(1, A_WIDTH), vmem_f32(A_GROUPS, CHUNK, CHUNK), vmem_f32(CHUNK, A_WIDTH),
                        vmem_f32(1, 128), vmem_f32(4 * CHUNK, 2 * CHUNK), vmem_f32(1, 128)],
        compiler_params=pltpu.CompilerParams(vmem_limit_bytes=VMEM_LIMIT),
    )(au, av, sq, sk, sv, sk, sv, mq, z, mem, x2, tgt2, v_g, v_b, w_sp, b_sp, sinks, bias, w_out, g_post, g_mem,
      w_mkv, x2, g_pre, buckets, w_in_t)


def _fill_small_grads(dgpre_ref, dgpost_ref, dgmem_ref, dvg_ref, dvb_ref, dws_ref, dbs_ref, dsink_ref, drel_ref,
                      loss_ref, bk_ref, a_ref, b_ref):
    a_ref[...] = jnp.zeros_like(a_ref)
    b_ref[...] = jnp.zeros_like(b_ref)
    a_ref[0:1, :] = dgpre_ref[...]
    a_ref[1:2, :] = dgpost_ref[...]
    a_ref[2:3, :] = dgmem_ref[...]
    a_ref[3:4, :] = jnp.concatenate([dvg_ref[...], dvb_ref[...]], axis=-1)
    a_ref[ROW_LOSS:ROW_LOSS + 1, 0:128] = loss_ref[...]
    row = lax.broadcasted_iota(jnp.int32, (CHUNK, CHUNK), 0)
    col = lax.broadcasted_iota(jnp.int32, (CHUNK, CHUNK), 1)
    for g in range(A_GROUPS):
        b_ref[ROW_WS + g * CHUNK:ROW_WS + (g + 1) * CHUNK, :] = jnp.where(row >= col, dws_ref[g], 0.0)
        by_token = jnp.transpose(dbs_ref[:, g * 128:(g + 1) * 128])
        b_ref[ROW_BS + g:ROW_BS + g + 1, :] = jnp.sum(by_token, axis=0, keepdims=True)
    b_ref[ROW_SINK:ROW_SINK + 1, :] = dsink_ref[...]
    bk = bk_ref[...]
    rel_row = lax.broadcasted_iota(jnp.int32, (8, 128), 0)
    rel_col = lax.broadcasted_iota(jnp.int32, (8, 128), 1)
    rel = jnp.zeros((8, 128), F32)
    for h in range(4):
        acc = drel_ref[h * CHUNK:(h + 1) * CHUNK, :]
        for b in range(N_BUCKETS):
            rel = jnp.where((rel_row == h) & (rel_col == b), jnp.sum(jnp.where(bk == b, acc, 0.0)), rel)
    b_ref[ROW_REL:ROW_REL + 8, :] = rel


SHARD_ROWS = IN_WIDTH // N_CHIPS
SHARD_WINDOW = 768
SHARD_HALF = SHARD_ROWS // 2
DWIN_TILE = 2048
N_REL = N_CHIPS - 1


def _shard_window_start(shard):
    return (shard * SHARD_ROWS // 128) * 128


def _reduce_gradients(dproj, h, big, small, shard_arr):
    n_tok = h.shape[0]
    tile = min(DWIN_TILE, n_tok)
    n_sub = n_tok // tile
    last = N_CHIPS - 1
    n_big, n_small = len(big), len(small)
    big_half = [g.shape[2:] for g in big]
    sem_big_d2d = 2 * N_CHIPS
    sem_big_ici = sem_big_d2d + n_big
    sem_big_swap = sem_big_ici + N_REL * n_big
    sem_small_d2d = sem_big_swap + n_big
    sem_small_ici = sem_small_d2d + n_small
    n_sems = sem_small_ici + N_REL * n_small
    loc_small = n_big
    loc_out_win = loc_small + n_small
    loc_out_big = loc_out_win + 2
    loc_out_small = loc_out_big + 2 * n_big
    n_local = loc_out_small + n_small

    def relation_of_slot(s):
        return (s + 2) % N_REL + 1

    def shard_of_slot(s, my_shard):
        return my_shard ^ jnp.where(s == last, 0, relation_of_slot(s))

    def body(shard_ref, dp_ref, h_hbm, *refs):
        h_vmem, h_sem, refs = refs[-2], refs[-1], refs[:-2]
        big_hbm, refs = refs[:n_big], refs[n_big:]
        small_hbm, refs = refs[:n_small], refs[n_small:]
        out_hbm, refs = refs[0], refs[1:]
        big_out, refs = refs[:n_big], refs[n_big:]
        small_out, refs = refs[:n_small], refs[n_small:]
        part, recv_d2d, send_ici, recv_ici, mine_buf, other_buf = refs[:6]
        refs = refs[6:]
        big_own, big_recv, big_send, big_land, big_mine, big_other = (
            refs[k * n_big:(k + 1) * n_big] for k in range(6))
        refs = refs[6 * n_big:]
        small_own, small_recv, small_all = (refs[k * n_small:(k + 1) * n_small] for k in range(3))
        send_sems, recv_sems, local_sems = refs[3 * n_small:]

        s, t = pl.program_id(0), pl.program_id(1)
        x, y, c = lax.axis_index("x"), lax.axis_index("y"), lax.axis_index("c")
        my_chip = 2 * x + y
        sibling = (x, y, 1 - c)
        my_rows = pl.ds(pl.multiple_of(c * SHARD_HALF, 8), SHARD_HALF)
        other_rows = pl.ds(pl.multiple_of((1 - c) * SHARD_HALF, 8), SHARD_HALF)

        def remote(src, dst, k, to):
            return pltpu.make_async_remote_copy(src_ref=src, dst_ref=dst, send_sem=send_sems.at[k],
                                                recv_sem=recv_sems.at[k], device_id=to, device_id_type=MESH)

        def chip_at(rel):
            return (x ^ (rel >> 1), y ^ (rel & 1), c)

        def to_sibling(k):
            return remote(part.at[k % 2, other_rows, :], recv_d2d.at[k], k, sibling)

        def to_chip(k):
            return remote(send_ici.at[k], recv_ici.at[k], N_CHIPS + k, chip_at(relation_of_slot(k)))

        swap = remote(mine_buf, other_buf, 2 * N_CHIPS - 1, sibling)
        big_load = [pltpu.make_async_copy(big_hbm[w].at[:, pl.ds(c, 1)], big_own[w], local_sems.at[w])
                    for w in range(n_big)]
        big_to_sibling = [remote(big_hbm[w].at[:, pl.ds(1 - c, 1)], big_recv[w], sem_big_d2d + w, sibling)
                          for w in range(n_big)]
        big_to_chip = [[remote(big_send[w].at[k], big_land[w].at[k], sem_big_ici + N_REL * w + k, chip_at(k + 1))
                        for k in range(N_REL)] for w in range(n_big)]
        big_swap = [remote(big_mine[w], big_other[w], sem_big_swap + w, sibling) for w in range(n_big)]
        small_load = [pltpu.make_async_copy(small_hbm[i], small_own[i], local_sems.at[loc_small + i])
                      for i in range(n_small)]
        small_to_sibling = [remote(small_hbm[i], small_recv[i], sem_small_d2d + i, sibling) for i in range(n_small)]
        small_to_chip = [[remote(small_all[i].at[my_chip], small_all[i].at[my_chip],
                                 sem_small_ici + N_REL * i + k, chip_at(k + 1))
                          for k in range(N_REL)] for i in range(n_small)]

        h_loads = [pltpu.make_async_copy(h_hbm.at[rows, :], h_vmem.at[rows, :], h_sem.at[k]) for k, rows in enumerate(
            [pl.ds(0, tile)] + ([pl.ds(tile, n_tok - tile)] if n_sub > 1 else []))]

        @pl.when((s == 0) & (t == 0))
        def _():
            for cp in h_loads + big_load + big_to_sibling + small_load + small_to_sibling:
                cp.start()
            h_loads[0].wait()

        if n_sub > 1:
            @pl.when((s == 0) & (t == 1))
            def _():
                h_loads[1].wait()

        @pl.when((s == 0) & (t == n_sub - 1))
        def _():
            for cp in big_load + small_load:
                cp.wait()
            for cp in big_to_sibling + small_to_sibling:
                cp.wait_recv()
                cp.wait_send()
            for w in range(n_big):
                for k in range(N_REL):
                    shard = my_chip ^ (k + 1)
                    big_send[w][k] = (big_own[w][shard, 0] + big_recv[w][shard, 0]).astype(BF16)
                    big_to_chip[w][k].start()
            for i in range(n_small):
                small_all[i][my_chip] = small_own[i][...] + small_recv[i][...]
                for k in range(N_REL):
                    small_to_chip[i][k].start()

        @pl.when((s > 0) & (t == jnp.where(s == last, 0, min(1, n_sub - 1))))
        def _():
            k = s - 1
            cp = to_sibling(k)
            cp.wait_recv()
            cp.wait_send()
            send_ici[k] = (part[k % 2, my_rows, :] + recv_d2d[k]).astype(BF16)
            to_chip(k).start()

        def big_rows(w, half):
            rows = big_half[w][0]
            return big_out[w].at[pl.ds(pl.multiple_of(half * rows, 8), rows), :]

        big_store_mine = [pltpu.make_async_copy(big_mine[w], big_rows(w, c), local_sems.at[loc_out_big + 2 * w])
                          for w in range(n_big)]
        big_store_other = [pltpu.make_async_copy(big_other[w], big_rows(w, 1 - c),
                                                 local_sems.at[loc_out_big + 2 * w + 1]) for w in range(n_big)]
        small_store = [pltpu.make_async_copy(small_all[i], small_out[i], local_sems.at[loc_out_small + i])
                       for i in range(n_small)]

        @pl.when((s == last) & (t == 0))
        def _():
            for w in range(n_big):
                total = big_own[w][my_chip, 0] + big_recv[w][my_chip, 0]
                for k in range(N_REL):
                    big_to_chip[w][k].wait_recv()
                    total = total + big_land[w][k].astype(F32)
                big_mine[w][...] = total
                big_swap[w].start()
                big_store_mine[w].start()
            for i in range(n_small):
                for k in range(N_REL):
                    small_to_chip[i][k].wait_recv()
                small_store[i].start()

        r = _mm_tn(dp_ref[...], h_vmem[pl.ds(pl.multiple_of(t * tile, tile), tile), :])
        odd = shard_of_slot(s, shard_ref[0]) % 2
        for parity in range(2):
            rows = r[64 * parity:64 * parity + SHARD_ROWS]

            @pl.when((odd == parity) & (t == 0))
            def _():
                part[s % 2] = rows

            @pl.when((odd == parity) & (t > 0))
            def _():
                part[s % 2] += rows

        @pl.when(t == n_sub - 1)
        def _():
            to_sibling(s).start()

        @pl.when((s == last) & (t == n_sub - 1))
        def _():
            cp = to_sibling(last)
            cp.wait_recv()
            cp.wait_send()
            total = part[last % 2, my_rows, :] + recv_d2d[last]
            for k in range(last):
                to_chip(k).wait_recv()
                total = total + recv_ici[k].astype(F32)
            mine_buf[...] = total
            swap.start()
            out_mine = pltpu.make_async_copy(mine_buf, out_hbm.at[my_rows, :], local_sems.at[0])
            out_mine.start()
            swap.wait_recv()
            out_other = pltpu.make_async_copy(other_buf, out_hbm.at[other_rows, :], local_sems.at[1])
            out_other.start()
            for w in range(n_big):
                big_swap[w].wait_recv()
                big_store_other[w].start()
            stores = [out_mine, out_other] + big_store_mine + big_store_other + small_store
            for k in range(last):
                to_chip(k).wait_send()
            swap.wait_send()
            for w in range(n_big):
                for k in range(N_REL):
                    big_to_chip[w][k].wait_send()
                big_swap[w].wait_send()
            for i in range(n_small):
                for k in range(N_REL):
                    small_to_chip[i][k].wait_send()
            for cp in stores:
                cp.wait()

    half = (SHARD_HALF, D_MODEL)
    vmem = pltpu.VMEM
    scratch = [vmem((2, SHARD_ROWS, D_MODEL), F32), vmem((N_CHIPS,) + half, F32),
               vmem((N_REL,) + half, BF16), vmem((N_REL,) + half, BF16), vmem(half, F32), vmem(half, F32)]
    scratch += [vmem((N_CHIPS, 1) + hs, F32) for hs in big_half] * 2
    scratch += [vmem((N_REL,) + hs, BF16) for hs in big_half] * 2
    scratch += [vmem(hs, F32) for hs in big_half] * 2
    scratch += [vmem(a.shape, F32) for a in small] * 2 + [vmem((N_CHIPS,) + a.shape, F32) for a in small]
    scratch += [pltpu.SemaphoreType.DMA((n_sems,)), pltpu.SemaphoreType.DMA((n_sems,)),
                pltpu.SemaphoreType.DMA((n_local,)), vmem(h.shape, BF16), pltpu.SemaphoreType.DMA((2,))]
    n_hbm = n_big + n_small
    out = pl.pallas_call(
        body, name="reduce_gradients",
        out_shape=[jax.ShapeDtypeStruct((SHARD_ROWS, D_MODEL), F32)]
        + [jax.ShapeDtypeStruct((2 * hs[0], hs[1]), F32) for hs in big_half]
        + [jax.ShapeDtypeStruct((N_CHIPS,) + a.shape, F32) for a in small],
        grid_spec=pltpu.PrefetchScalarGridSpec(
            num_scalar_prefetch=1, grid=(N_CHIPS, n_sub),
            in_specs=[pl.BlockSpec((pl.Element(tile), pl.Element(SHARD_WINDOW)),
                                   lambda s, t, m: (t * tile, _shard_window_start(shard_of_slot(s, m[0])))),
                      ANY_SPEC] + [ANY_SPEC] * n_hbm,
            out_specs=[ANY_SPEC] * (1 + n_hbm),
            scratch_shapes=scratch),
        compiler_params=pltpu.CompilerParams(vmem_limit_bytes=VMEM_LIMIT),
    )(shard_arr, dproj, h, *big, *small)
    return out[:1 + n_big], out[1 + n_big:]


def _adamw(w, g, m, v):
    m2 = ADAM_B1 * m + (1.0 - ADAM_B1) * g
    v2 = ADAM_B2 * v + (1.0 - ADAM_B2) * (g * g)
    m_hat = m2 / (1.0 - ADAM_B1 ** ADAM_STEP)
    v_hat = v2 / (1.0 - ADAM_B2 ** ADAM_STEP)
    delta = -ADAM_LR * (m_hat / (jnp.sqrt(v_hat) + ADAM_EPS) + ADAM_WD * w)
    return delta, m2, v2


ADAM_STEPS = 2


def _adamw_all(shard_grads, shard_w, shard_m, shard_v, ra, rb, small_w, small_m, small_v):
    n_sh, n = len(shard_w), len(small_w)

    def body(*refs):
        sh_in, refs = refs[:4 * n_sh], refs[4 * n_sh:]
        ra_ref, rb_ref, refs = refs[0], refs[1], refs[2:]
        w_refs, m_refs, v_refs, refs = refs[:n], refs[n:2 * n], refs[2 * n:3 * n], refs[3 * n:]
        sh_out, outs = refs[:4 * n_sh], refs[4 * n_sh:]
        for k in range(n_sh):
            g = sh_in[k][...]
            delta, m2, v2 = _adamw(sh_in[n_sh + k][...], g, sh_in[2 * n_sh + k][...], sh_in[3 * n_sh + k][...])
            for ref, val in zip(sh_out[4 * k:4 * k + 4], (g, delta, m2, v2)):
                ref[...] = val

        @pl.when(pl.program_id(0) == 0)
        def _():
            g_outs, d_outs, m_outs, v_outs = outs[:n], outs[n:2 * n], outs[2 * n:3 * n], outs[3 * n:4 * n]
            ga, gb = ra_ref[0], rb_ref[0]
            for chip in range(1, N_CHIPS):
                ga = ga + ra_ref[chip]
                gb = gb + rb_ref[chip]
            outs[4 * n][...] = ga[ROW_LOSS:ROW_LOSS + 1, 0:128]
            grads = [ga[0:1, :], ga[1:2, :], ga[2:3, :], ga[3:4, :A_WIDTH], ga[3:4, A_WIDTH:],
                     gb[ROW_WS:ROW_WS + A_GROUPS * CHUNK, :].reshape(A_GROUPS, CHUNK, CHUNK),
                     gb[ROW_BS:ROW_BS + A_GROUPS, :], gb[ROW_SINK:ROW_SINK + 1, 0:4],
                     gb[ROW_REL:ROW_REL + 4, 0:N_BUCKETS]]
            for k in range(n):
                delta, m2, v2 = _adamw(w_refs[k][...], grads[k], m_refs[k][...], v_refs[k][...])
                g_outs[k][...] = grads[k]
                d_outs[k][...] = delta
                m_outs[k][...] = m2
                v_outs[k][...] = v2

    def rows_block(a):
        assert a.shape[0] % (8 * ADAM_STEPS) == 0
        return pl.BlockSpec((a.shape[0] // ADAM_STEPS, a.shape[1]), lambda i: (i, 0))

    sh_specs = [rows_block(w) for w in shard_w]
    small_in = [ra, rb, *small_w, *small_m, *small_v]
    small_out_shapes = [jax.ShapeDtypeStruct(w.shape, F32) for w in small_w] * 4 + [jax.ShapeDtypeStruct((1, 128), F32)]
    out = pl.pallas_call(
        body, name="adamw_all", grid=(ADAM_STEPS,),
        out_shape=[jax.ShapeDtypeStruct(w.shape, F32) for w in shard_w for _ in range(4)] + small_out_shapes,
        in_specs=sh_specs * 4 + [_full_spec(a.shape) for a in small_in],
        out_specs=[spec for spec in sh_specs for _ in range(4)] + [_full_spec(s.shape) for s in small_out_shapes],
        compiler_params=pltpu.CompilerParams(vmem_limit_bytes=VMEM_LIMIT),
    )(*shard_grads, *shard_w, *shard_m, *shard_v, *small_in)
    return [out[4 * k:4 * k + 4] for k in range(n_sh)], out[4 * n_sh:]


def kernel(x, mem, pre_norm_g, post_norm_g, mem_norm_g, w_in, w_mem_kv, v_norm_g, v_norm_b, w_spatial, b_spatial, attn_sinks, rel_bias, w_out, loss_target, m_pre_norm_g, m_post_norm_g, m_mem_norm_g, m_w_in, m_w_mem_kv, m_v_norm_g, m_v_norm_b, m_w_spatial, m_b_spatial, m_attn_sinks, m_rel_bias, m_w_out, v_pre_norm_g, v_post_norm_g, v_mem_norm_g, v_w_in, v_w_mem_kv, v_v_norm_g, v_v_norm_b, v_w_spatial, v_b_spatial, v_attn_sinks, v_rel_bias, v_w_out):
    n_ex, seq, _ = x.shape
    n_tok = n_ex * seq
    x2 = x.reshape(n_tok, D_MODEL)
    tgt2 = loss_target.reshape(n_tok, D_MODEL)
    buckets = jnp.asarray(_bucket_map())
    shard_arr = (2 * lax.axis_index("x") + lax.axis_index("y")).astype(jnp.int32).reshape(1)
    w_sp = w_spatial[0]
    w_in_t, m_w_in_t, v_w_in_t = (jnp.transpose(a[0]) for a in (w_in, m_w_in, v_w_in))
    rel_t, m_rel_t, v_rel_t = (jnp.transpose(a) for a in (rel_bias, m_rel_bias, v_rel_bias))

    x_arr = lax.axis_index("x").astype(jnp.int32).reshape(1)
    h_b, parts, (w_in_b, g_mkv, g_out), bias, b_sp = _gather_and_project(
        x2, pre_norm_g, w_in_t, w_mem_kv[0], w_out[0], rel_t, buckets, b_spatial[0], x_arr)
    w_mkv_b = g_mkv.reshape(D_MODEL, 2 * MEM_WIDTH)
    w_out_b = g_out.reshape(MIX_WIDTH, D_MODEL)

    dx, dproj, dwmkv, dwout, small_a, small_b = _mix(
        parts, mem, x2, tgt2, v_norm_g, v_norm_b, w_sp, b_sp, attn_sinks, bias, w_out_b, post_norm_g, mem_norm_g,
        w_mkv_b, pre_norm_g, w_in_b, buckets, n_ex, seq)

    shard_shapes = [w_mem_kv.shape[1:], w_out.shape[1:]]
    big = [g.reshape(N_CHIPS, 2, s[0] // 2, s[1]) for g, s in zip((dwmkv, dwout), shard_shapes)]
    (g_win, g_wmkv, g_wout), (ga, gb) = _reduce_gradients(dproj, h_b, big, [small_a, small_b], shard_arr)

    small_w = [pre_norm_g, post_norm_g, mem_norm_g, v_norm_g, v_norm_b, w_sp, b_spatial[0], attn_sinks, rel_t]
    small_m = [m_pre_norm_g, m_post_norm_g, m_mem_norm_g, m_v_norm_g, m_v_norm_b, m_w_spatial[0], m_b_spatial[0],
               m_attn_sinks, m_rel_t]
    small_v = [v_pre_norm_g, v_post_norm_g, v_mem_norm_g, v_v_norm_g, v_v_norm_b, v_w_spatial[0], v_b_spatial[0],
               v_attn_sinks, v_rel_t]
    big_out, small_out = _adamw_all(
        [g_win, g_wmkv, g_wout], [w_in_t, w_mem_kv[0], w_out[0]], [m_w_in_t, m_w_mem_kv[0], m_w_out[0]],
        [v_w_in_t, v_w_mem_kv[0], v_w_out[0]], ga, gb, small_w, small_m, small_v)
    n_small = len(small_w)

    outputs = [small_out[4 * n_small][0, 0], dx.reshape(x.shape)]
    for kind in range(4):
        s = small_out[kind * n_small:(kind + 1) * n_small]
        outputs += [s[0], s[1], s[2], jnp.transpose(big_out[0][kind])[None], big_out[1][kind][None], s[3], s[4],
                    s[5][None], s[6][None], s[7], jnp.transpose(s[8]), big_out[2][kind][None]]
    return tuple(outputs)
```

```python
import functools

import numpy as np
import jax
import jax.numpy as jnp
from jax import lax
from jax.experimental import pallas as pl
from jax.experimental.pallas import tpu as pltpu

F32 = jnp.float32
BF16 = jnp.bfloat16
MESH = pl.DeviceIdType.MESH

D_MODEL = 1024
CHUNK = 128
A_WIDTH = 512
A_GROUPS = 4
SWA_WIDTH = 256
KV_WIDTH = 128
MEM_WIDTH = 256
MEM_LEN = 256
MIX_WIDTH = 1024
IN_WIDTH = 2816
N_BUCKETS = 32
MAX_DISTANCE = 128
EPS = 1e-6
NEG = -1e30
QK_SCALE = 0.125
HALF_HEAD_PAIR = 64

ADAM_LR = 0.001
ADAM_B1 = 0.9
ADAM_B2 = 0.999
ADAM_EPS = 1e-08
ADAM_WD = 0.01
ADAM_STEP = 10

N_CHIPS = 4
TILE_CHUNKS = 2
TILE = TILE_CHUNKS * CHUNK
PROJ_TILE = 512
VMEM_LIMIT = 56 * 1024 * 1024

SMALL_A_ROWS = 8
ROW_LOSS = 4
ROW_WS = 0
ROW_BS = 512
ROW_SINK = 520
ROW_REL = 528
SMALL_B_ROWS = 536


def _mm(a, b):
    return lax.dot_general(a, b, (((1,), (0,)), ((), ())), preferred_element_type=F32)


def _mm_nt(a, b):
    return lax.dot_general(a, b, (((1,), (1,)), ((), ())), preferred_element_type=F32)


def _mm_tn(a, b):
    return lax.dot_general(a, b, (((0,), (0,)), ((), ())), preferred_element_type=F32)


def _bucket_map():
    qi = np.arange(CHUNK)[:, None]
    kj = np.arange(2 * CHUNK)[None, :]
    n = np.maximum(qi + CHUNK - kj, 0)
    max_exact = N_BUCKETS // 2
    large = max_exact + (np.log(np.maximum(n, 1) / max_exact) / np.log(MAX_DISTANCE / max_exact)
                         * (N_BUCKETS - max_exact)).astype(np.int32)
    large = np.minimum(large, N_BUCKETS - 1)
    return np.where(n < max_exact, n, large).astype(np.int32)


_GELU_C = 0.7978845608028654
_GELU_A = 0.044715
_GELU_K1 = 2.0 * _GELU_C
_GELU_K2 = 2.0 * _GELU_C * _GELU_A


def _gelu(x):
    x2 = x * x
    s = 1.0 / (1.0 + jnp.exp(x * (-_GELU_K1 - _GELU_K2 * x2)))
    return x * s, (s, x2)


def _gelu_grad(x, saved):
    s, x2 = saved
    return s + x * (s * (1.0 - s)) * (_GELU_K1 + 3.0 * _GELU_K2 * x2)


def _sigmoid(x):
    return 1.0 / (1.0 + jnp.exp(-x))


def _lane_lo(shape):
    return lax.broadcasted_iota(jnp.int32, shape, 1) < HALF_HEAD_PAIR


def _swa_variants(t):
    lo = _lane_lo(t.shape)
    tr = pltpu.roll(t, HALF_HEAD_PAIR, 1)
    zero = jnp.zeros_like(t)
    return (jnp.where(lo, t, zero).astype(BF16), jnp.where(lo, zero, tr).astype(BF16),
            jnp.where(lo, tr, zero).astype(BF16), jnp.where(lo, zero, t).astype(BF16))


def _swa_unvariants(d0, d1, d2, d3):
    lo = _lane_lo(d0.shape)
    zero = jnp.zeros_like(d0)
    rolled = jnp.where(lo, zero, d1) + jnp.where(lo, d2, zero)
    return jnp.where(lo, d0, zero) + jnp.where(lo, zero, d3) + pltpu.roll(rolled, HALF_HEAD_PAIR, 1)


def _mem_variants(t):
    out = []
    for pair in range(2):
        tp = t[:, pair * 128:(pair + 1) * 128]
        lo = _lane_lo(tp.shape)
        zero = jnp.zeros_like(tp)
        out.append(jnp.where(lo, tp, zero).astype(BF16))
        out.append(jnp.where(lo, zero, tp).astype(BF16))
    return out


def _mem_unvariants(d0, d1, d2, d3):
    lo = _lane_lo(d0.shape)
    return jnp.concatenate([jnp.where(lo, d0, d1), jnp.where(lo, d2, d3)], axis=-1)


def _softmax(logits, sinks):
    m = jnp.max(logits, axis=-1, keepdims=True)
    if sinks is not None:
        m = jnp.maximum(m, sinks)
    p = jnp.exp(logits - m)
    den = jnp.sum(p, axis=-1, keepdims=True)
    if sinks is None:
        return p * (1.0 / den), None
    es = jnp.exp(sinks - m)
    inv = 1.0 / (den + es)
    return p * inv, es * inv


def _band_valid(with_prev):
    qi = lax.broadcasted_iota(jnp.int32, (CHUNK, 2 * CHUNK), 0)
    kj = lax.broadcasted_iota(jnp.int32, (CHUNK, 2 * CHUNK), 1)
    in_cur = (kj >= CHUNK) & (kj - CHUNK <= qi)
    if not with_prev:
        return in_cur
    return in_cur | ((kj < CHUNK) & (kj > qi))


def _causal_weights(ws_ref):
    row = lax.broadcasted_iota(jnp.int32, (CHUNK, CHUNK), 0)
    col = lax.broadcasted_iota(jnp.int32, (CHUNK, CHUNK), 1)
    return [jnp.where(row >= col, ws_ref[g], 0.0).astype(BF16) for g in range(A_GROUPS)]


def _rows_to_lanes(a, n):
    return jnp.concatenate([a[c * CHUNK:(c + 1) * CHUNK] for c in range(n)], axis=1)


def _lanes_to_rows(a, n):
    w = a.shape[1] // n
    return jnp.concatenate([a[:, c * w:(c + 1) * w] for c in range(n)], axis=0)


def _stack_heads(pair01, pair23):
    return jnp.concatenate([pair01[:, :256], pair01[:, 256:], pair23[:, :256], pair23[:, 256:]], axis=0)


def _pair_heads(s, r):
    return (jnp.concatenate([s[0:r], s[r:2 * r]], axis=1), jnp.concatenate([s[2 * r:3 * r], s[3 * r:4 * r]], axis=1))


def _pair_operands(variants):
    return (jnp.concatenate(variants[0:2], axis=0), jnp.concatenate(variants[2:4], axis=0))


def _split_pair_grads(d_pairs):
    return d_pairs[0][:256], d_pairs[0][256:], d_pairs[1][:256], d_pairs[1][256:]


def _halves_bf16(a):
    return (a[:, :128].astype(BF16), a[:, 128:].astype(BF16))


def _group_a_forward(au, av, vg, vb, wm, bs_rows):
    gu, tu = _gelu(au)
    gv, tv = _gelu(av)
    ya, res = [], []
    for g in range(A_GROUPS):
        sl = slice(g * 128, (g + 1) * 128)
        xg = gv[:, sl]
        xc = xg - jnp.mean(xg, axis=-1, keepdims=True)
        rstd = lax.rsqrt(jnp.mean(xc * xc, axis=-1, keepdims=True) + EPS)
        xhat = xc * rstd
        vn = _rows_to_lanes((xhat * vg[:, sl] + vb[:, sl]).astype(BF16), TILE_CHUNKS)
        s = _lanes_to_rows(_mm(wm[g], vn), TILE_CHUNKS) + bs_rows[g]
        ya.append(gu[:, sl] * s)
        res.append((xhat, rstd, vn, s))
    return ya, dict(gu=gu, tu=tu, tv=tv, groups=res)


def _attention_logits(qp, k_pairs):
    return _stack_heads(_mm_nt(qp[0], k_pairs[0]), _mm_nt(qp[1], k_pairs[1]))


def _attention_out(p, v_pairs, r):
    pp = _pair_heads(p.astype(BF16), r)
    return jnp.concatenate([_mm(pp[0], v_pairs[0]), _mm(pp[1], v_pairs[1])], axis=-1), pp


def _attention_dprobs(do_pairs, v_pairs):
    return _stack_heads(_mm_nt(do_pairs[0], v_pairs[0]), _mm_nt(do_pairs[1], v_pairs[1]))


def _softmax_backward(p, dp):
    delta = jnp.sum(p * dp, axis=-1, keepdims=True)
    return p * (dp - delta), delta


def _attention_grads(dl, pp, do_pairs, qp, k_pairs, r):
    dlp = _pair_heads(dl.astype(BF16), r)
    dq = jnp.concatenate([_mm(dlp[0], k_pairs[0]), _mm(dlp[1], k_pairs[1])], axis=-1)
    dk = (_mm_tn(dlp[0], qp[0]), _mm_tn(dlp[1], qp[1]))
    dv = (_mm_tn(pp[0], do_pairs[0]), _mm_tn(pp[1], do_pairs[1]))
    return dq, dk, dv


def _tile_specs(n_tiles_ex, width):
    return pl.BlockSpec((TILE, width), lambda b, i: (b * n_tiles_ex + jnp.minimum(i, n_tiles_ex - 1), 0))


def _prev_chunk_spec(n_tiles_ex, width):
    def index(b, i):
        chunk = TILE_CHUNKS * jnp.minimum(i, n_tiles_ex - 1)
        return (b * n_tiles_ex * TILE_CHUNKS + jnp.maximum(chunk - 1, 0), 0)
    return pl.BlockSpec((CHUNK, width), index)


def _full_spec(shape):
    zeros = (0,) * len(shape)
    return pl.BlockSpec(shape, lambda *_: zeros)


SMEM_SPEC = pl.BlockSpec(memory_space=pltpu.SMEM)
ANY_SPEC = pl.BlockSpec(memory_space=pl.ANY)


def _fill_bias(rel_ref, bk_ref, out_ref):
    bk = bk_ref[...]
    for h in range(4):
        acc = jnp.zeros((CHUNK, 2 * CHUNK), F32)
        for b in range(N_BUCKETS):
            acc = jnp.where(bk == b, rel_ref[h, b], acc)
        for t, with_prev in enumerate((True, False)):
            out_ref[t, h * CHUNK:(h + 1) * CHUNK, :] = jnp.where(_band_valid(with_prev), acc, NEG)


PROJ_WIDTHS = (A_WIDTH, A_WIDTH, SWA_WIDTH, KV_WIDTH, KV_WIDTH, MEM_WIDTH, MIX_WIDTH)
PROJ_OFFSETS = tuple(int(v) for v in np.cumsum((0,) + PROJ_WIDTHS))


MXU_TILE = 256
HALF_WIDTH = IN_WIDTH // 2
PHASE_COLS = (HALF_WIDTH // MXU_TILE * MXU_TILE, IN_WIDTH - HALF_WIDTH // MXU_TILE * MXU_TILE)


def _phase_columns(phase, chip_x):
    if phase == 0:
        return 0 if chip_x == 0 else IN_WIDTH - PHASE_COLS[0]
    return PHASE_COLS[0] if chip_x == 0 else 0


def _phase_parts(phase, chip_x):
    start = _phase_columns(phase, chip_x)
    return [(k, PROJ_OFFSETS[k] - start) for k in range(len(PROJ_WIDTHS))
            if start <= PROJ_OFFSETS[k] and PROJ_OFFSETS[k + 1] <= start + PHASE_COLS[phase]]


def _gather_and_project(x2, g_pre, w_in_s, w_mkv_s, w_out_s, rel_bias_t, buckets, b_spatial, x_arr):
    n_tok = x2.shape[0]
    n_tiles = n_tok // PROJ_TILE
    last = n_tiles - 1
    shapes = [w_in_s.shape, w_mkv_s.shape, w_out_s.shape]
    n_w = len(shapes)

    def body(x_sref, x_ref, g_ref, win_hbm, wmkv_hbm, wout_hbm, rel_ref, bk_ref, bsp_ref, h_ref, *refs):
        part_refs, refs = refs[:len(PROJ_WIDTHS)], refs[len(PROJ_WIDTHS):]
        bias_ref, bs_ref, refs = refs[0], refs[1], refs[2:]
        gin_hbm, gmkv_hbm, gout_hbm, wg, stage_in, stage_mkv, stage_out, own_mkv, own_out, h_all = refs[:10]
        send_sems, recv_sems, local_sems = refs[10:]
        p, t = pl.program_id(0), pl.program_id(1)
        x, y, c = lax.axis_index("x"), lax.axis_index("y"), lax.axis_index("c")
        me, sibling = (x, y, c), (x, y, 1 - c)
        my_shard = 2 * x + y
        gathered = [wg, gmkv_hbm, gout_hbm]

        def half_rows(w, shard, half):
            rows = shapes[w][0] // 2
            if w == 0:
                return wg.at[pl.ds(pl.multiple_of(shard * shapes[0][0] + half * rows, 16), rows), :]
            return gathered[w].at[shard, pl.ds(half * rows, rows), :]

        def first(w, rel):
            src = half_rows(w, my_shard, c) if w == 0 else (own_mkv, own_out)[w - 1].at[
                pl.ds(c * (shapes[w][0] // 2), shapes[w][0] // 2), :]
            k = 3 * w + rel - 1
            return pltpu.make_async_remote_copy(
                src_ref=src, dst_ref=half_rows(w, my_shard, c), send_sem=send_sems.at[k], recv_sem=recv_sems.at[k],
                device_id=(x ^ (rel >> 1), y ^ (rel & 1), c), device_id_type=MESH)

        def landed(w, rel):
            k = 3 * w + rel - 1
            ref = half_rows(w, my_shard ^ rel, c)
            return pltpu.make_async_remote_copy(src_ref=ref, dst_ref=ref, send_sem=send_sems.at[k],
                                                recv_sem=recv_sems.at[k], device_id=me, device_id_type=MESH)

        def passed(w, rel, half, to):
            k = 9 + 3 * w + rel - 1
            ref = half_rows(w, my_shard ^ rel, half)
            return pltpu.make_async_remote_copy(src_ref=ref, dst_ref=ref, send_sem=send_sems.at[k],
                                                recv_sem=recv_sems.at[k], device_id=to, device_id_type=MESH)

        def pass_on(w, rels):
            for rel in rels:
                landed(w, rel).wait_recv()
                passed(w, rel, c, sibling).start()
            for rel in rels:
                passed(w, rel, 1 - c, me).wait_recv()

        own_stores = [pltpu.make_async_copy(own_mkv, gmkv_hbm.at[my_shard], local_sems.at[3]),
                      pltpu.make_async_copy(own_out, gout_hbm.at[my_shard], local_sems.at[4])]

        @pl.when((p == 0) & (t == 0))
        def _():
            half_rows_in = shapes[0][0] // 2
            halves = [pl.ds(pl.multiple_of(hc * half_rows_in, 8), half_rows_in) for hc in (c, 1 - c)]
            loads = [pltpu.make_async_copy(win_hbm.at[halves[0], :], stage_in.at[halves[0], :], local_sems.at[0]),
                     pltpu.make_async_copy(wmkv_hbm, stage_mkv, local_sems.at[1]),
                     pltpu.make_async_copy(wout_hbm, stage_out, local_sems.at[2]),
                     pltpu.make_async_copy(win_hbm.at[halves[1], :], stage_in.at[halves[1], :], local_sems.at[6])]
            for cp in (loads[0], loads[3], loads[1], loads[2]):
                cp.start()
            loads[0].wait()
            half_rows(0, my_shard, c)[...] = stage_in[halves[0], :].astype(BF16)
            for rel in (1, 2):
                first(0, rel).start()
            loads[3].wait()
            half_rows(0, my_shard, 1 - c)[...] = stage_in[halves[1], :].astype(BF16)
            loads[1].wait()
            loads[2].wait()
            own_mkv[...] = stage_mkv[...].astype(BF16)
            own_out[...] = stage_out[...].astype(BF16)
            for cp in own_stores:
                cp.start()
            _fill_bias(rel_ref, bk_ref, bias_ref)
            for g in range(A_GROUPS):
                bs_ref[g] = jnp.transpose(jnp.broadcast_to(bsp_ref[g:g + 1, :], (CHUNK, CHUNK)))
            pass_on(0, (1,))
            first(0, 3).start()

        @pl.when((p == 0) & (t == n_tiles // 2))
        def _():
            for w in (1, 2):
                for rel in (1, 2, 3):
                    first(w, rel).start()

        store = pltpu.make_async_copy(wg, gin_hbm, local_sems.at[5])

        @pl.when((p == 1) & (t == 0))
        def _():
            pass_on(0, (2, 3))
            store.start()

        @pl.when((p == 1) & (t == n_tiles // 2))
        def _():
            for w in (1, 2):
                pass_on(w, (1, 2, 3))

        tile_rows = pl.ds(pl.multiple_of(t * PROJ_TILE, PROJ_TILE), PROJ_TILE)

        def project(h, phase):
            start = jnp.where(x_sref[0] == 0, _phase_columns(phase, 0), _phase_columns(phase, 1))
            proj = _mm_nt(h, wg[pl.ds(pl.multiple_of(start, MXU_TILE), PHASE_COLS[phase]), :])
            for chip_x in range(2):
                @pl.when(x_sref[0] == chip_x)
                def _():
                    for k, lo in _phase_parts(phase, chip_x):
                        part_refs[k][...] = proj[:, lo:lo + PROJ_WIDTHS[k]].astype(BF16)

        @pl.when(p == 0)
        def _():
            xv = x_ref[...]
            r = lax.rsqrt(jnp.mean(xv * xv, axis=-1, keepdims=True) + EPS)
            h = (xv * r * g_ref[...]).astype(BF16)
            h_ref[...] = h
            h_all[tile_rows, :] = h
            project(h, 0)

        @pl.when(p == 1)
        def _():
            project(h_all[tile_rows, :], 1)

        @pl.when((p == 1) & (t == last))
        def _():
            for w in range(n_w):
                for rel in (1, 2, 3):
                    first(w, rel).wait_send()
                    passed(w, rel, c, sibling).wait_send()
            for cp in own_stores:
                cp.wait()
            store.wait()

    def written_in(k):
        phase_on = [next(ph for ph in range(2) if k in dict(_phase_parts(ph, chip_x))) for chip_x in range(2)]

        def index(p, t, xs):
            phase = jnp.where(xs[0] == 0, phase_on[0], phase_on[1])
            return (jnp.where(p == phase, t, jnp.where(p < phase, 0, last)), 0)
        return index

    part_specs = [pl.BlockSpec((PROJ_TILE, PROJ_WIDTHS[k]), written_in(k)) for k in range(len(PROJ_WIDTHS))]
    vmem = pltpu.VMEM
    out = pl.pallas_call(
        body, name="gather_and_project",
        out_shape=[jax.ShapeDtypeStruct((n_tok, D_MODEL), BF16)]
        + [jax.ShapeDtypeStruct((n_tok, w), BF16) for w in PROJ_WIDTHS]
        + [jax.ShapeDtypeStruct((2, 4 * CHUNK, 2 * CHUNK), F32), jax.ShapeDtypeStruct((A_GROUPS, CHUNK, CHUNK), F32)]
        + [jax.ShapeDtypeStruct((N_CHIPS * shapes[0][0], shapes[0][1]), BF16)]
        + [jax.ShapeDtypeStruct((N_CHIPS,) + s, BF16) for s in shapes[1:]],
        grid_spec=pltpu.PrefetchScalarGridSpec(
            num_scalar_prefetch=1, grid=(2, n_tiles),
            in_specs=[pl.BlockSpec((PROJ_TILE, D_MODEL), lambda p, t, xs: (jnp.where(p == 0, t, last), 0)),
                      pl.BlockSpec((1, D_MODEL), lambda p, t, xs: (0, 0)), ANY_SPEC, ANY_SPEC, ANY_SPEC, SMEM_SPEC,
                      pl.BlockSpec(buckets.shape, lambda p, t, xs: (0, 0)),
                      pl.BlockSpec(b_spatial.shape, lambda p, t, xs: (0, 0))],
            out_specs=[pl.BlockSpec((PROJ_TILE, D_MODEL), lambda p, t, xs: (jnp.where(p == 0, t, last), 0))]
            + part_specs + [pl.BlockSpec((2, 4 * CHUNK, 2 * CHUNK), lambda p, t, xs: (0, 0, 0)),
                            pl.BlockSpec((A_GROUPS, CHUNK, CHUNK), lambda p, t, xs: (0, 0, 0))] + [ANY_SPEC] * 3,
            scratch_shapes=[vmem((N_CHIPS * shapes[0][0], shapes[0][1]), BF16), vmem(shapes[0], F32),
                            vmem(shapes[1], F32), vmem(shapes[2], F32), vmem(shapes[1], BF16), vmem(shapes[2], BF16),
                            vmem((n_tok, D_MODEL), BF16),
                            pltpu.SemaphoreType.DMA((18,)), pltpu.SemaphoreType.DMA((18,)),
                            pltpu.SemaphoreType.DMA((7,))]),
        compiler_params=pltpu.CompilerParams(vmem_limit_bytes=VMEM_LIMIT),
    )(x_arr, x2, g_pre, w_in_s, w_mkv_s, w_out_s, rel_bias_t, buckets, b_spatial)
    n_parts = len(PROJ_WIDTHS)
    return out[0], list(out[1:1 + n_parts]), out[3 + n_parts:], out[1 + n_parts], out[2 + n_parts]


def _load_chunk(j, i, sk_ref, sv_ref, skp_ref, svp_ref):
    rows = slice(j * CHUNK, (j + 1) * CHUNK)
    if j == 0:
        k_prev, v_prev, table = skp_ref[...], svp_ref[...], jnp.where(i > 0, 0, 1)
    else:
        prev = slice((j - 1) * CHUNK, j * CHUNK)
        k_prev, v_prev, table = sk_ref[prev, :], sv_ref[prev, :], 0
    k_pairs = _pair_operands(_swa_variants(jnp.concatenate([k_prev, sk_ref[rows, :]], axis=0).astype(F32)))
    v_pairs = _pair_operands(_swa_variants(jnp.concatenate([v_prev, sv_ref[rows, :]], axis=0).astype(F32)))
    return rows, k_pairs, v_pairs, table


def _tile_constants(ws_ref, bs_ref, sink_ref):
    wm = _causal_weights(ws_ref)
    bs_rows = [jnp.concatenate([bs_ref[g]] * TILE_CHUNKS, axis=0) for g in range(A_GROUPS)]
    sink_col = jnp.max(jnp.concatenate([jnp.full((CHUNK, 128), sink_ref[0, h], F32) for h in range(4)] * TILE_CHUNKS,
                                       axis=0), axis=-1, keepdims=True)
    return wm, bs_rows, sink_col


WIN_LOADS = 4


def _mix(parts, mem, x2, tgt2, v_g, v_b, w_sp, b_sp, sinks, bias, w_out, g_post, g_mem, w_mkv, g_pre, w_in_t, buckets,
         n_ex, seq):
    n_tiles_ex = seq // TILE
    n_tok = n_ex * seq
    au, av, sq, sk, sv, mq, z = parts
    col = dict(zip(("au", "av", "sq", "sk", "sv", "mq", "z"),
                   (slice(PROJ_OFFSETS[k], PROJ_OFFSETS[k + 1]) for k in range(len(PROJ_WIDTHS)))))
    before_kv, after_kv = slice(0, col["sk"].start), slice(col["sv"].stop, IN_WIDTH)
    kv_cols = slice(col["sk"].start, col["sv"].stop)
    gated = slice(col["au"].start, col["av"].stop)
    cut_a, cut_z = (s.start + 3 * (s.stop - s.start) // 4 for s in (gated, col["z"]))
    back_cols = ((slice(gated.start, cut_a),), (slice(col["z"].start, cut_z),),
                 (slice(cut_a, gated.stop), slice(cut_z, col["z"].stop)), (col["sq"], col["mq"]))
    assert sum(s.stop - s.start for part in back_cols for s in part) == IN_WIDTH - 2 * KV_WIDTH

    def body(au_ref, av_ref, sq_ref, sk_ref, sv_ref, skp_ref, svp_ref, mq_ref, z_ref, mem_ref, x_ref, tgt_ref,
             vg_ref, vb_ref, ws_ref, bs_ref, sink_ref, bias_ref, wout_ref, gpost_ref, gmem_ref, wmkv_ref,
             xl_ref, gpre_ref, bk_ref, win_hbm,
             dx_ref, dproj_ref, dwmkv_ref, dwout_ref, a_ref, b_ref,
             carry_dp, carry_k, carry_v, memn_s, mem_ops, dmkv_s, carry_dout, win_s, win_sem, dh_s,
             dgpre_ref, dgpost_ref, dgmem_ref, dvg_ref, dvb_ref, dws_ref, dbs_ref, dsink_ref, drel_ref, loss_ref):
        b, i = pl.program_id(0), pl.program_id(1)
        win_blocks = [pl.ds(k * (IN_WIDTH // WIN_LOADS), IN_WIDTH // WIN_LOADS) for k in range(WIN_LOADS)]
        win_loads = [pltpu.make_async_copy(win_hbm.at[rows, :], win_s.at[rows, :], win_sem.at[k])
                     for k, rows in enumerate(win_blocks)]

        @pl.when((b == 0) & (i == 0))
        def _():
            for load in win_loads:
                load.start()
            for ref in (dwmkv_ref, dwout_ref, dgpre_ref, dgpost_ref, dgmem_ref, dvg_ref, dvb_ref, dws_ref, dbs_ref,
                        dsink_ref, drel_ref, loss_ref, carry_dp):
                ref[...] = jnp.zeros_like(ref)

        def normalized_mem():
            m = mem_ref[0]
            return m * lax.rsqrt(jnp.mean(m * m, axis=-1, keepdims=True) + EPS)

        @pl.when(i == 0)
        def _():
            memn_s[...] = (normalized_mem() * gmem_ref[...]).astype(BF16)
            mkv = _mm(memn_s[...], wmkv_ref[...])
            for k, pair in enumerate(_pair_operands(_mem_variants(mkv[:, :MEM_WIDTH]))
                                     + _pair_operands(_mem_variants(mkv[:, MEM_WIDTH:]))):
                mem_ops[k] = pair
            dmkv_s[...] = jnp.zeros_like(dmkv_s)
            carry_k[...] = jnp.zeros_like(carry_k)
            carry_v[...] = jnp.zeros_like(carry_v)

        @pl.when((b == 0) & (i == 0))
        def _():
            for load in win_loads:
                load.wait()

        @pl.when(i > 0)
        def _():
            dproj_ref[:, before_kv] = carry_dp[:, before_kv]
            dproj_ref[:, after_kv] = carry_dp[:, after_kv]

        def project_back(part):
            return sum(_mm(carry_dp[:, s], win_s[s, :]) for s in back_cols[part])

        @pl.when(i < n_tiles_ex)
        def _():
            dh_s[...] = project_back(0)
            wm, bs_rows, sink_col = _tile_constants(ws_ref, bs_ref, sink_ref)
            mk_pairs, mv_pairs = (mem_ops[0], mem_ops[1]), (mem_ops[2], mem_ops[3])
            vg = vg_ref[...]

            au_v, av_v = au_ref[...].astype(F32), av_ref[...].astype(F32)
            ya, res = _group_a_forward(au_v, av_v, vg, vb_ref[...], wm, bs_rows)
            swa, logits, yb = [], [], []
            for j in range(TILE_CHUNKS):
                rows, k_pairs, v_pairs, table = _load_chunk(j, i, sk_ref, sv_ref, skp_ref, svp_ref)
                qp = _halves_bf16(sq_ref[rows, :] * QK_SCALE)
                logits.append(_attention_logits(qp, k_pairs) + bias_ref[table])
                swa.append([rows, k_pairs, v_pairs, qp])
            dh_s[...] += project_back(1)
            mqp = _halves_bf16(mq_ref[...] * QK_SCALE)
            logits_mem = _attention_logits(mqp, mk_pairs)
            p_swa, sink_p = _softmax(jnp.concatenate(logits, axis=0), sink_col)
            for j in range(TILE_CHUNKS):
                out, pp = _attention_out(p_swa[j * 4 * CHUNK:(j + 1) * 4 * CHUNK], swa[j][2], CHUNK)
                yb.append(out)
                swa[j].append(pp)
            pm, _ = _softmax(logits_mem, None)
            yc, ppm = _attention_out(pm, mv_pairs, TILE)
            ycat = jnp.concatenate(ya + [jnp.concatenate(yb, axis=0), yc], axis=-1)

            zv = z_ref[...].astype(F32)
            sig = _sigmoid(zv)
            sz = zv * sig
            y_b = (ycat * sz).astype(BF16)
            halves = (slice(0, TILE // 2), slice(TILE // 2, TILE))
            o_halves = [_mm(y_b[rows], wout_ref[...]) for rows in halves]
            dh_s[...] += project_back(2)
            gp = gpost_ref[...]
            do_halves = []
            for rows, o in zip(halves, o_halves):
                r2 = lax.rsqrt(jnp.mean(o * o, axis=-1, keepdims=True) + EPS)
                nrm = o * r2
                diff = x_ref[rows, :] + nrm * gp - tgt_ref[rows, :]
                loss_ref[...] += jnp.sum(diff * diff) * (0.5 / D_MODEL)
                dout = diff * (1.0 / D_MODEL)
                carry_dout[lax.rem(i, 2), rows, :] = dout
                dgpost_ref[...] += jnp.sum(dout * nrm, axis=0, keepdims=True)
                dn = dout * gp
                do_halves.append((r2 * (dn - nrm * jnp.mean(dn * nrm, axis=-1, keepdims=True))).astype(BF16))
            do_b = jnp.concatenate(do_halves, axis=0)
            dy = _mm_nt(do_b, wout_ref[...])
            carry_dp[:, col["z"]] = (dy * ycat * (sig + sz * (1.0 - sig))).astype(BF16)
            dyc = dy * sz

            dgu, dgv = [], []
            for g in range(A_GROUPS):
                sl = slice(g * 128, (g + 1) * 128)
                xhat, rstd, vn, s = res["groups"][g]
                dya = dyc[:, sl]
                dgu.append(dya * s)
                ds = dya * res["gu"][:, sl]
                dbs_ref[:, sl] += sum(ds[c * CHUNK:(c + 1) * CHUNK] for c in range(TILE_CHUNKS))
                ds_b = _rows_to_lanes(ds.astype(BF16), TILE_CHUNKS)
                dws_ref[g] += _mm_nt(ds_b, vn)
                dvn = _lanes_to_rows(_mm_tn(wm[g], ds_b), TILE_CHUNKS)
                dvg_ref[:, sl] += jnp.sum(dvn * xhat, axis=0, keepdims=True)
                dvb_ref[:, sl] += jnp.sum(dvn, axis=0, keepdims=True)
                dxh = dvn * vg[:, sl]
                dgv.append(rstd * (dxh - jnp.mean(dxh, axis=-1, keepdims=True)
                                   - xhat * jnp.mean(dxh * xhat, axis=-1, keepdims=True)))
            carry_dp[:, col["au"]] =(jnp.concatenate(dgu, axis=-1) * _gelu_grad(au_v, res["tu"])).astype(BF16)
            carry_dp[:, col["av"]] = (jnp.concatenate(dgv, axis=-1) * _gelu_grad(av_v, res["tv"])).astype(BF16)

            do_pairs = [_halves_bf16(dyc[rows, A_WIDTH:A_WIDTH + SWA_WIDTH]) for rows, *_ in swa]
            dp_swa = jnp.concatenate(
                [_attention_dprobs(do_pairs[j], swa[j][2]) for j in range(TILE_CHUNKS)], axis=0)
            dh_s[...] += project_back(3)
            dl_swa, delta = _softmax_backward(p_swa, dp_swa)
            sink_terms = sink_p * delta
            lane4 = lax.broadcasted_iota(jnp.int32, (1, 128), 1)
            dsink_vec = jnp.zeros((1, 128), F32)
            for h in range(4):
                head_sum = sum(jnp.sum(sink_terms[(4 * j + h) * CHUNK:(4 * j + h + 1) * CHUNK])
                               for j in range(TILE_CHUNKS))
                dsink_vec = dsink_vec + jnp.where(lane4 == h, -head_sum, 0.0)
            dsink_ref[...] += dsink_vec
            drel_ref[...] += sum(dl_swa[j * 4 * CHUNK:(j + 1) * 4 * CHUNK] for j in range(TILE_CHUNKS))
            dk_parts, dv_parts = [], []
            for j, (rows, k_pairs, v_pairs, qp, pp) in enumerate(swa):
                dq, dk, dv = _attention_grads(dl_swa[j * 4 * CHUNK:(j + 1) * 4 * CHUNK], pp, do_pairs[j], qp, k_pairs,
                                              CHUNK)
                carry_dp[rows, col["sq"]] = (dq * QK_SCALE).astype(BF16)
                dk_parts.append(_swa_unvariants(*_split_pair_grads(dk)))
                dv_parts.append(_swa_unvariants(*_split_pair_grads(dv)))

            dc_pairs = _halves_bf16(dyc[:, A_WIDTH + SWA_WIDTH:])
            dp_mem = _attention_dprobs(dc_pairs, mv_pairs)
            dwout_ref[...] += _mm_tn(y_b, do_b)
            dl_mem, _ = _softmax_backward(pm, dp_mem)
            dmq, dmk, dmv = _attention_grads(dl_mem, ppm, dc_pairs, mqp, mk_pairs, TILE)
            carry_dp[:, col["mq"]] = (dmq * QK_SCALE).astype(BF16)
            dmkv_s[...] += jnp.concatenate([_mem_unvariants(*_split_pair_grads(dmk)),
                                            _mem_unvariants(*_split_pair_grads(dmv))], axis=-1)

            for parts_c, carry, cols in ((dk_parts, carry_k, col["sk"]), (dv_parts, carry_v, col["sv"])):
                @pl.when(i > 0)
                def _():
                    dproj_ref[:, cols] = (carry[...] + jnp.concatenate(
                        [jnp.zeros((TILE - CHUNK, KV_WIDTH), F32), parts_c[0][:CHUNK]], axis=0)).astype(BF16)
                new = [parts_c[0][CHUNK:]]
                for j in range(1, TILE_CHUNKS):
                    new[-1] = new[-1] + parts_c[j][:CHUNK]
                    new.append(parts_c[j][CHUNK:])
                carry[...] = jnp.concatenate(new, axis=0)

        @pl.when(i == n_tiles_ex)
        def _():
            dproj_ref[:, col["sk"]] = carry_k[...].astype(BF16)
            dproj_ref[:, col["sv"]] = carry_v[...].astype(BF16)
            d_b = dmkv_s[...].astype(BF16)
            dwmkv_ref[...] += _mm_tn(memn_s[...], d_b)
            dgmem_ref[...] += jnp.sum(_mm_nt(d_b, wmkv_ref[...]) * normalized_mem(), axis=0, keepdims=True)
            dh_s[...] = sum(project_back(part) for part in range(len(back_cols)))

        @pl.when(i > 0)
        def _():
            xv = xl_ref[...]
            r = lax.rsqrt(jnp.mean(xv * xv, axis=-1, keepdims=True) + EPS)
            xn = xv * r
            dh = dh_s[...] + _mm(dproj_ref[:, kv_cols], win_s[kv_cols, :])
            dgpre_ref[...] += jnp.sum(dh * xn, axis=0, keepdims=True)
            dhg = dh * gpre_ref[...]
            dx_ref[...] = (r * (dhg - xn * jnp.mean(dhg * xn, axis=-1, keepdims=True))
                           + carry_dout[lax.rem(i + 1, 2)])

        @pl.when((b == n_ex - 1) & (i == n_tiles_ex))
        def _():
            _fill_small_grads(dgpre_ref, dgpost_ref, dgmem_ref, dvg_ref, dvb_ref, dws_ref, dbs_ref, dsink_ref,
                              drel_ref, loss_ref, bk_ref, a_ref, b_ref)

    tile = functools.partial(_tile_specs, n_tiles_ex)
    prev = functools.partial(_prev_chunk_spec, n_tiles_ex)
    late = lambda width: pl.BlockSpec((TILE, width), lambda b, i: (b * n_tiles_ex + jnp.maximum(i - 1, 0), 0))
    vmem_f32 = lambda *shape: pltpu.VMEM(shape, F32)
    return pl.pallas_call(
        body, name="mix", grid=(n_ex, n_tiles_ex + 1),
        out_shape=[jax.ShapeDtypeStruct((n_tok, D_MODEL), F32), jax.ShapeDtypeStruct((n_tok, IN_WIDTH), BF16),
                   jax.ShapeDtypeStruct((D_MODEL, 2 * MEM_WIDTH), F32), jax.ShapeDtypeStruct((MIX_WIDTH, D_MODEL), F32),
                   jax.ShapeDtypeStruct((SMALL_A_ROWS, D_MODEL), F32), jax.ShapeDtypeStruct((SMALL_B_ROWS, 128), F32)],
        in_specs=[tile(A_WIDTH), tile(A_WIDTH), tile(SWA_WIDTH), tile(KV_WIDTH), tile(KV_WIDTH),
                  prev(KV_WIDTH), prev(KV_WIDTH), tile(MEM_WIDTH), tile(MIX_WIDTH),
                  pl.BlockSpec((1, MEM_LEN, D_MODEL), lambda b, i: (b, 0, 0)),
                  tile(D_MODEL), tile(D_MODEL),
                  _full_spec((1, A_WIDTH)), _full_spec((1, A_WIDTH)), _full_spec((A_GROUPS, CHUNK, CHUNK)),
                  _full_spec((A_GROUPS, CHUNK, CHUNK)), SMEM_SPEC, _full_spec((2, 4 * CHUNK, 2 * CHUNK)),
                  _full_spec((MIX_WIDTH, D_MODEL)), _full_spec((1, D_MODEL)), _full_spec((1, D_MODEL)),
                  _full_spec((D_MODEL, 2 * MEM_WIDTH)),
                  late(D_MODEL), _full_spec((1, D_MODEL)), _full_spec((CHUNK, 2 * CHUNK)), ANY_SPEC],
        out_specs=[late(D_MODEL), late(IN_WIDTH), _full_spec((D_MODEL, 2 * MEM_WIDTH)),
                   _full_spec((MIX_WIDTH, D_MODEL)), _full_spec((SMALL_A_ROWS, D_MODEL)),
                   _full_spec((SMALL_B_ROWS, 128))],
        scratch_shapes=[pltpu.VMEM((TILE, IN_WIDTH), BF16), pltpu.VMEM((TILE, KV_WIDTH), F32),
                        pltpu.VMEM((TILE, KV_WIDTH), F32), pltpu.VMEM((MEM_LEN, D_MODEL), BF16),
                        pltpu.VMEM((4, 2 * MEM_LEN, 128), BF16), pltpu.VMEM((MEM_LEN, 2 * MEM_WIDTH), F32),
                        pltpu.VMEM((2, TILE, D_MODEL), F32), pltpu.VMEM((IN_WIDTH, D_MODEL), BF16),
                        pltpu.SemaphoreType.DMA((WIN_LOADS,)), vmem_f32(TILE, D_MODEL),
                        vmem_f32(1, D_MODEL), vmem_f32(1, D_MODEL), vmem_f32(1, D_MODEL), vmem_f32(1, A_WIDTH),
                        vmem_f32(1, A_WIDTH), vmem_f32(A_GROUPS, CHUNK, CHUNK), vmem_f32(CHUNK, A_WIDTH),
                        vmem_f32(1, 128), vmem_f32(4 * CHUNK, 2 * CHUNK), vmem_f32(1, 128)],
        compiler_params=pltpu.CompilerParams(vmem_limit_bytes=VMEM_LIMIT),
    )(au, av, sq, sk, sv, sk, sv, mq, z, mem, x2, tgt2, v_g, v_b, w_sp, b_sp, sinks, bias, w_out, g_post, g_mem,
      w_mkv, x2, g_pre, buckets, w_in_t)


def _fill_small_grads(dgpre_ref, dgpost_ref, dgmem_ref, dvg_ref, dvb_ref, dws_ref, dbs_ref, dsink_ref, drel_ref,
                      loss_ref, bk_ref, a_ref, b_ref):
    a_ref[...] = jnp.zeros_like(a_ref)
    b_ref[...] = jnp.zeros_like(b_ref)
    a_ref[0:1, :] = dgpre_ref[...]
    a_ref[1:2, :] = dgpost_ref[...]
    a_ref[2:3, :] = dgmem_ref[...]
    a_ref[3:4, :] = jnp.concatenate([dvg_ref[...], dvb_ref[...]], axis=-1)
    a_ref[ROW_LOSS:ROW_LOSS + 1, 0:128] = loss_ref[...]
    row = lax.broadcasted_iota(jnp.int32, (CHUNK, CHUNK), 0)
    col = lax.broadcasted_iota(jnp.int32, (CHUNK, CHUNK), 1)
    for g in range(A_GROUPS):
        b_ref[ROW_WS + g * CHUNK:ROW_WS + (g + 1) * CHUNK, :] = jnp.where(row >= col, dws_ref[g], 0.0)
        by_token = jnp.transpose(dbs_ref[:, g * 128:(g + 1) * 128])
        b_ref[ROW_BS + g:ROW_BS + g + 1, :] = jnp.sum(by_token, axis=0, keepdims=True)
    b_ref[ROW_SINK:ROW_SINK + 1, :] = dsink_ref[...]
    bk = bk_ref[...]
    rel_row = lax.broadcasted_iota(jnp.int32, (8, 128), 0)
    rel_col = lax.broadcasted_iota(jnp.int32, (8, 128), 1)
    rel = jnp.zeros((8, 128), F32)
    for h in range(4):
        acc = drel_ref[h * CHUNK:(h + 1) * CHUNK, :]
        for b in range(N_BUCKETS):
            rel = jnp.where((rel_row == h) & (rel_col == b), jnp.sum(jnp.where(bk == b, acc, 0.0)), rel)
    b_ref[ROW_REL:ROW_REL + 8, :] = rel


SHARD_ROWS = IN_WIDTH // N_CHIPS
SHARD_WINDOW = 768
SHARD_HALF = SHARD_ROWS // 2
DWIN_TILE = 2048
N_REL = N_CHIPS - 1


def _shard_window_start(shard):
    return (shard * SHARD_ROWS // 128) * 128


def _reduce_gradients(dproj, h, big, small, shard_arr):
    n_tok = h.shape[0]
    tile = min(DWIN_TILE, n_tok)
    n_sub = n_tok // tile
    last = N_CHIPS - 1
    n_big, n_small = len(big), len(small)
    big_half = [g.shape[2:] for g in big]
    sem_big_d2d = 2 * N_CHIPS
    sem_big_ici = sem_big_d2d + n_big
    sem_big_swap = sem_big_ici + N_REL * n_big
    sem_small_d2d = sem_big_swap + n_big
    sem_small_ici = sem_small_d2d + n_small
    n_sems = sem_small_ici + N_REL * n_small
    loc_small = n_big
    loc_out_win = loc_small + n_small
    loc_out_big = loc_out_win + 2
    loc_out_small = loc_out_big + 2 * n_big
    n_local = loc_out_small + n_small

    def relation_of_slot(s):
        return (s + 2) % N_REL + 1

    def shard_of_slot(s, my_shard):
        return my_shard ^ jnp.where(s == last, 0, relation_of_slot(s))

    def body(shard_ref, dp_ref, h_hbm, *refs):
        h_vmem, h_sem, refs = refs[-2], refs[-1], refs[:-2]
        big_hbm, refs = refs[:n_big], refs[n_big:]
        small_hbm, refs = refs[:n_small], refs[n_small:]
        out_hbm, refs = refs[0], refs[1:]
        big_out, refs = refs[:n_big], refs[n_big:]
        small_out, refs = refs[:n_small], refs[n_small:]
        part, recv_d2d, send_ici, recv_ici, mine_buf, other_buf = refs[:6]
        refs = refs[6:]
        big_own, big_recv, big_send, big_land, big_mine, big_other = (
            refs[k * n_big:(k + 1) * n_big] for k in range(6))
        refs = refs[6 * n_big:]
        small_own, small_recv, small_all = (refs[k * n_small:(k + 1) * n_small] for k in range(3))
        send_sems, recv_sems, local_sems = refs[3 * n_small:]

        s, t = pl.program_id(0), pl.program_id(1)
        x, y, c = lax.axis_index("x"), lax.axis_index("y"), lax.axis_index("c")
        my_chip = 2 * x + y
        sibling = (x, y, 1 - c)
        my_rows = pl.ds(pl.multiple_of(c * SHARD_HALF, 8), SHARD_HALF)
        other_rows = pl.ds(pl.multiple_of((1 - c) * SHARD_HALF, 8), SHARD_HALF)

        def remote(src, dst, k, to):
            return pltpu.make_async_remote_copy(src_ref=src, dst_ref=dst, send_sem=send_sems.at[k],
                                                recv_sem=recv_sems.at[k], device_id=to, device_id_type=MESH)

        def chip_at(rel):
            return (x ^ (rel >> 1), y ^ (rel & 1), c)

        def to_sibling(k):
            return remote(part.at[k % 2, other_rows, :], recv_d2d.at[k], k, sibling)

        def to_chip(k):
            return remote(send_ici.at[k], recv_ici.at[k], N_CHIPS + k, chip_at(relation_of_slot(k)))

        swap = remote(mine_buf, other_buf, 2 * N_CHIPS - 1, sibling)
        big_load = [pltpu.make_async_copy(big_hbm[w].at[:, pl.ds(c, 1)], big_own[w], local_sems.at[w])
                    for w in range(n_big)]
        big_to_sibling = [remote(big_hbm[w].at[:, pl.ds(1 - c, 1)], big_recv[w], sem_big_d2d + w, sibling)
                          for w in range(n_big)]
        big_to_chip = [[remote(big_send[w].at[k], big_land[w].at[k], sem_big_ici + N_REL * w + k, chip_at(k + 1))
                        for k in range(N_REL)] for w in range(n_big)]
        big_swap = [remote(big_mine[w], big_other[w], sem_big_swap + w, sibling) for w in range(n_big)]
        small_load = [pltpu.make_async_copy(small_hbm[i], small_own[i], local_sems.at[loc_small + i])
                      for i in range(n_small)]
        small_to_sibling = [remote(small_hbm[i], small_recv[i], sem_small_d2d + i, sibling) for i in range(n_small)]
        small_to_chip = [[remote(small_all[i].at[my_chip], small_all[i].at[my_chip],
                                 sem_small_ici + N_REL * i + k, chip_at(k + 1))
                          for k in range(N_REL)] for i in range(n_small)]

        h_loads = [pltpu.make_async_copy(h_hbm.at[rows, :], h_vmem.at[rows, :], h_sem.at[k]) for k, rows in enumerate(
            [pl.ds(0, tile)] + ([pl.ds(tile, n_tok - tile)] if n_sub > 1 else []))]

        @pl.when((s == 0) & (t == 0))
        def _():
            for cp in h_loads + big_load + big_to_sibling + small_load + small_to_sibling:
                cp.start()
            h_loads[0].wait()

        if n_sub > 1:
            @pl.when((s == 0) & (t == 1))
            def _():
                h_loads[1].wait()

        @pl.when((s == 0) & (t == n_sub - 1))
        def _():
            for cp in big_load + small_load:
                cp.wait()
            for cp in big_to_sibling + small_to_sibling:
                cp.wait_recv()
                cp.wait_send()
            for w in range(n_big):
                for k in range(N_REL):
                    shard = my_chip ^ (k + 1)
                    big_send[w][k] = (big_own[w][shard, 0] + big_recv[w][shard, 0]).astype(BF16)
                    big_to_chip[w][k].start()
            for i in range(n_small):
                small_all[i][my_chip] = small_own[i][...] + small_recv[i][...]
                for k in range(N_REL):
                    small_to_chip[i][k].start()

        @pl.when((s > 0) & (t == jnp.where(s == last, 0, min(1, n_sub - 1))))
        def _():
            k = s - 1
            cp = to_sibling(k)
            cp.wait_recv()
            cp.wait_send()
            send_ici[k] = (part[k % 2, my_rows, :] + recv_d2d[k]).astype(BF16)
            to_chip(k).start()

        def big_rows(w, half):
            rows = big_half[w][0]
            return big_out[w].at[pl.ds(pl.multiple_of(half * rows, 8), rows), :]

        big_store_mine = [pltpu.make_async_copy(big_mine[w], big_rows(w, c), local_sems.at[loc_out_big + 2 * w])
                          for w in range(n_big)]
        big_store_other = [pltpu.make_async_copy(big_other[w], big_rows(w, 1 - c),
                                                 local_sems.at[loc_out_big + 2 * w + 1]) for w in range(n_big)]
        small_store = [pltpu.make_async_copy(small_all[i], small_out[i], local_sems.at[loc_out_small + i])
                       for i in range(n_small)]

        @pl.when((s == last) & (t == 0))
        def _():
            for w in range(n_big):
                total = big_own[w][my_chip, 0] + big_recv[w][my_chip, 0]
                for k in range(N_REL):
                    big_to_chip[w][k].wait_recv()
                    total = total + big_land[w][k].astype(F32)
                big_mine[w][...] = total
                big_swap[w].start()
                big_store_mine[w].start()
            for i in range(n_small):
                for k in range(N_REL):
                    small_to_chip[i][k].wait_recv()
                small_store[i].start()

        r = _mm_tn(dp_ref[...], h_vmem[pl.ds(pl.multiple_of(t * tile, tile), tile), :])
        odd = shard_of_slot(s, shard_ref[0]) % 2
        for parity in range(2):
            rows = r[64 * parity:64 * parity + SHARD_ROWS]

            @pl.when((odd == parity) & (t == 0))
            def _():
                part[s % 2] = rows

            @pl.when((odd == parity) & (t > 0))
            def _():
                part[s % 2] += rows

        @pl.when(t == n_sub - 1)
        def _():
            to_sibling(s).start()

        @pl.when((s == last) & (t == n_sub - 1))
        def _():
            cp = to_sibling(last)
            cp.wait_recv()
            cp.wait_send()
            total = part[last % 2, my_rows, :] + recv_d2d[last]
            for k in range(last):
                to_chip(k).wait_recv()
                total = total + recv_ici[k].astype(F32)
            mine_buf[...] = total
            swap.start()
            out_mine = pltpu.make_async_copy(mine_buf, out_hbm.at[my_rows, :], local_sems.at[0])
            out_mine.start()
            swap.wait_recv()
            out_other = pltpu.make_async_copy(other_buf, out_hbm.at[other_rows, :], local_sems.at[1])
            out_other.start()
            for w in range(n_big):
                big_swap[w].wait_recv()
                big_store_other[w].start()
            stores = [out_mine, out_other] + big_store_mine + big_store_other + small_store
            for k in range(last):
                to_chip(k).wait_send()
            swap.wait_send()
            for w in range(n_big):
                for k in range(N_REL):
                    big_to_chip[w][k].wait_send()
                big_swap[w].wait_send()
            for i in range(n_small):
                for k in range(N_REL):
                    small_to_chip[i][k].wait_send()
            for cp in stores:
                cp.wait()

    half = (SHARD_HALF, D_MODEL)
    vmem = pltpu.VMEM
    scratch = [vmem((2, SHARD_ROWS, D_MODEL), F32), vmem((N_CHIPS,) + half, F32),
               vmem((N_REL,) + half, BF16), vmem((N_REL,) + half, BF16), vmem(half, F32), vmem(half, F32)]
    scratch += [vmem((N_CHIPS, 1) + hs, F32) for hs in big_half] * 2
    scratch += [vmem((N_REL,) + hs, BF16) for hs in big_half] * 2
    scratch += [vmem(hs, F32) for hs in big_half] * 2
    scratch += [vmem(a.shape, F32) for a in small] * 2 + [vmem((N_CHIPS,) + a.shape, F32) for a in small]
    scratch += [pltpu.SemaphoreType.DMA((n_sems,)), pltpu.SemaphoreType.DMA((n_sems,)),
                pltpu.SemaphoreType.DMA((n_local,)), vmem(h.shape, BF16), pltpu.SemaphoreType.DMA((2,))]
    n_hbm = n_big + n_small
    out = pl.pallas_call(
        body, name="reduce_gradients",
        out_shape=[jax.ShapeDtypeStruct((SHARD_ROWS, D_MODEL), F32)]
        + [jax.ShapeDtypeStruct((2 * hs[0], hs[1]), F32) for hs in big_half]
        + [jax.ShapeDtypeStruct((N_CHIPS,) + a.shape, F32) for a in small],
        grid_spec=pltpu.PrefetchScalarGridSpec(
            num_scalar_prefetch=1, grid=(N_CHIPS, n_sub),
            in_specs=[pl.BlockSpec((pl.Element(tile), pl.Element(SHARD_WINDOW)),
                                   lambda s, t, m: (t * tile, _shard_window_start(shard_of_slot(s, m[0])))),
                      ANY_SPEC] + [ANY_SPEC] * n_hbm,
            out_specs=[ANY_SPEC] * (1 + n_hbm),
            scratch_shapes=scratch),
        compiler_params=pltpu.CompilerParams(vmem_limit_bytes=VMEM_LIMIT),
    )(shard_arr, dproj, h, *big, *small)
    return out[:1 + n_big], out[1 + n_big:]


def _adamw(w, g, m, v):
    m2 = ADAM_B1 * m + (1.0 - ADAM_B1) * g
    v2 = ADAM_B2 * v + (1.0 - ADAM_B2) * (g * g)
    m_hat = m2 / (1.0 - ADAM_B1 ** ADAM_STEP)
    v_hat = v2 / (1.0 - ADAM_B2 ** ADAM_STEP)
    delta = -ADAM_LR * (m_hat / (jnp.sqrt(v_hat) + ADAM_EPS) + ADAM_WD * w)
    return delta, m2, v2


ADAM_STEPS = 2


def _adamw_all(shard_grads, shard_w, shard_m, shard_v, ra, rb, small_w, small_m, small_v):
    n_sh, n = len(shard_w), len(small_w)

    def body(*refs):
        sh_in, refs = refs[:4 * n_sh], refs[4 * n_sh:]
        ra_ref, rb_ref, refs = refs[0], refs[1], refs[2:]
        w_refs, m_refs, v_refs, refs = refs[:n], refs[n:2 * n], refs[2 * n:3 * n], refs[3 * n:]
        sh_out, outs = refs[:4 * n_sh], refs[4 * n_sh:]
        for k in range(n_sh):
            g = sh_in[k][...]
            delta, m2, v2 = _adamw(sh_in[n_sh + k][...], g, sh_in[2 * n_sh + k][...], sh_in[3 * n_sh + k][...])
            for ref, val in zip(sh_out[4 * k:4 * k + 4], (g, delta, m2, v2)):
                ref[...] = val

        @pl.when(pl.program_id(0) == 0)
        def _():
            g_outs, d_outs, m_outs, v_outs = outs[:n], outs[n:2 * n], outs[2 * n:3 * n], outs[3 * n:4 * n]
            ga, gb = ra_ref[0], rb_ref[0]
            for chip in range(1, N_CHIPS):
                ga = ga + ra_ref[chip]
                gb = gb + rb_ref[chip]
            outs[4 * n][...] = ga[ROW_LOSS:ROW_LOSS + 1, 0:128]
            grads = [ga[0:1, :], ga[1:2, :], ga[2:3, :], ga[3:4, :A_WIDTH], ga[3:4, A_WIDTH:],
                     gb[ROW_WS:ROW_WS + A_GROUPS * CHUNK, :].reshape(A_GROUPS, CHUNK, CHUNK),
                     gb[ROW_BS:ROW_BS + A_GROUPS, :], gb[ROW_SINK:ROW_SINK + 1, 0:4],
                     gb[ROW_REL:ROW_REL + 4, 0:N_BUCKETS]]
            for k in range(n):
                delta, m2, v2 = _adamw(w_refs[k][...], grads[k], m_refs[k][...], v_refs[k][...])
                g_outs[k][...] = grads[k]
                d_outs[k][...] = delta
                m_outs[k][...] = m2
                v_outs[k][...] = v2

    def rows_block(a):
        assert a.shape[0] % (8 * ADAM_STEPS) == 0
        return pl.BlockSpec((a.shape[0] // ADAM_STEPS, a.shape[1]), lambda i: (i, 0))

    sh_specs = [rows_block(w) for w in shard_w]
    small_in = [ra, rb, *small_w, *small_m, *small_v]
    small_out_shapes = [jax.ShapeDtypeStruct(w.shape, F32) for w in small_w] * 4 + [jax.ShapeDtypeStruct((1, 128), F32)]
    out = pl.pallas_call(
        body, name="adamw_all", grid=(ADAM_STEPS,),
        out_shape=[jax.ShapeDtypeStruct(w.shape, F32) for w in shard_w for _ in range(4)] + small_out_shapes,
        in_specs=sh_specs * 4 + [_full_spec(a.shape) for a in small_in],
        out_specs=[spec for spec in sh_specs for _ in range(4)] + [_full_spec(s.shape) for s in small_out_shapes],
        compiler_params=pltpu.CompilerParams(vmem_limit_bytes=VMEM_LIMIT),
    )(*shard_grads, *shard_w, *shard_m, *shard_v, *small_in)
    return [out[4 * k:4 * k + 4] for k in range(n_sh)], out[4 * n_sh:]


def kernel(x, mem, pre_norm_g, post_norm_g, mem_norm_g, w_in, w_mem_kv, v_norm_g, v_norm_b, w_spatial, b_spatial, attn_sinks, rel_bias, w_out, loss_target, m_pre_norm_g, m_post_norm_g, m_mem_norm_g, m_w_in, m_w_mem_kv, m_v_norm_g, m_v_norm_b, m_w_spatial, m_b_spatial, m_attn_sinks, m_rel_bias, m_w_out, v_pre_norm_g, v_post_norm_g, v_mem_norm_g, v_w_in, v_w_mem_kv, v_v_norm_g, v_v_norm_b, v_w_spatial, v_b_spatial, v_attn_sinks, v_rel_bias, v_w_out):
    n_ex, seq, _ = x.shape
    n_tok = n_ex * seq
    x2 = x.reshape(n_tok, D_MODEL)
    tgt2 = loss_target.reshape(n_tok, D_MODEL)
    buckets = jnp.asarray(_bucket_map())
    shard_arr = (2 * lax.axis_index("x") + lax.axis_index("y")).astype(jnp.int32).reshape(1)
    w_sp = w_spatial[0]
    w_in_t, m_w_in_t, v_w_in_t = (jnp.transpose(a[0]) for a in (w_in, m_w_in, v_w_in))
    rel_t, m_rel_t, v_rel_t = (jnp.transpose(a) for a in (rel_bias, m_rel_bias, v_rel_bias))

    x_arr = lax.axis_index("x").astype(jnp.int32).reshape(1)
    h_b, parts, (w_in_b, g_mkv, g_out), bias, b_sp = _gather_and_project(
        x2, pre_norm_g, w_in_t, w_mem_kv[0], w_out[0], rel_t, buckets, b_spatial[0], x_arr)
    w_mkv_b = g_mkv.reshape(D_MODEL, 2 * MEM_WIDTH)
    w_out_b = g_out.reshape(MIX_WIDTH, D_MODEL)

    dx, dproj, dwmkv, dwout, small_a, small_b = _mix(
        parts, mem, x2, tgt2, v_norm_g, v_norm_b, w_sp, b_sp, attn_sinks, bias, w_out_b, post_norm_g, mem_norm_g,
        w_mkv_b, pre_norm_g, w_in_b, buckets, n_ex, seq)

    shard_shapes = [w_mem_kv.shape[1:], w_out.shape[1:]]
    big = [g.reshape(N_CHIPS, 2, s[0] // 2, s[1]) for g, s in zip((dwmkv, dwout), shard_shapes)]
    (g_win, g_wmkv, g_wout), (ga, gb) = _reduce_gradients(dproj, h_b, big, [small_a, small_b], shard_arr)

    small_w = [pre_norm_g, post_norm_g, mem_norm_g, v_norm_g, v_norm_b, w_sp, b_spatial[0], attn_sinks, rel_t]
    small_m = [m_pre_norm_g, m_post_norm_g, m_mem_norm_g, m_v_norm_g, m_v_norm_b, m_w_spatial[0], m_b_spatial[0],
               m_attn_sinks, m_rel_t]
    small_v = [v_pre_norm_g, v_post_norm_g, v_mem_norm_g, v_v_norm_g, v_v_norm_b, v_w_spatial[0], v_b_spatial[0],
               v_attn_sinks, v_rel_t]
    big_out, small_out = _adamw_all(
        [g_win, g_wmkv, g_wout], [w_in_t, w_mem_kv[0], w_out[0]], [m_w_in_t, m_w_mem_kv[0], m_w_out[0]],
        [v_w_in_t, v_w_mem_kv[0], v_w_out[0]], ga, gb, small_w, small_m, small_v)
    n_small = len(small_w)

    outputs = [small_out[4 * n_small][0, 0], dx.reshape(x.shape)]
    for kind in range(4):
        s = small_out[kind * n_small:(kind + 1) * n_small]
        outputs += [s[0], s[1], s[2], jnp.transpose(big_out[0][kind])[None], big_out[1][kind][None], s[3], s[4],
                    s[5][None], s[6][None], s[7], jnp.transpose(s[8]), big_out[2][kind][None]]
    return tuple(outputs)
```

```python
import functools

import numpy as np
import jax
import jax.numpy as jnp
from jax import lax
from jax.experimental import pallas as pl
from jax.experimental.pallas import tpu as pltpu

F32 = jnp.float32
BF16 = jnp.bfloat16
MESH = pl.DeviceIdType.MESH

D_MODEL = 1024
CHUNK = 128
A_WIDTH = 512
A_GROUPS = 4
SWA_WIDTH = 256
KV_WIDTH = 128
MEM_WIDTH = 256
MEM_LEN = 256
MIX_WIDTH = 1024
IN_WIDTH = 2816
N_BUCKETS = 32
MAX_DISTANCE = 128
EPS = 1e-6
NEG = -1e30
QK_SCALE = 0.125
HALF_HEAD_PAIR = 64

ADAM_LR = 0.001
ADAM_B1 = 0.9
ADAM_B2 = 0.999
ADAM_EPS = 1e-08
ADAM_WD = 0.01
ADAM_STEP = 10

N_CHIPS = 4
TILE_CHUNKS = 2
TILE = TILE_CHUNKS * CHUNK
PROJ_TILE = 512
VMEM_LIMIT = 56 * 1024 * 1024

SMALL_A_ROWS = 8
ROW_LOSS = 4
ROW_WS = 0
ROW_BS = 512
ROW_SINK = 520
ROW_REL = 528
SMALL_B_ROWS = 536


def _mm(a, b):
    return lax.dot_general(a, b, (((1,), (0,)), ((), ())), preferred_element_type=F32)


def _mm_nt(a, b):
    return lax.dot_general(a, b, (((1,), (1,)), ((), ())), preferred_element_type=F32)


def _mm_tn(a, b):
    return lax.dot_general(a, b, (((0,), (0,)), ((), ())), preferred_element_type=F32)


def _bucket_map():
    qi = np.arange(CHUNK)[:, None]
    kj = np.arange(2 * CHUNK)[None, :]
    n = np.maximum(qi + CHUNK - kj, 0)
    max_exact = N_BUCKETS // 2
    large = max_exact + (np.log(np.maximum(n, 1) / max_exact) / np.log(MAX_DISTANCE / max_exact)
                         * (N_BUCKETS - max_exact)).astype(np.int32)
    large = np.minimum(large, N_BUCKETS - 1)
    return np.where(n < max_exact, n, large).astype(np.int32)


_GELU_C = 0.7978845608028654
_GELU_A = 0.044715
_GELU_K1 = 2.0 * _GELU_C
_GELU_K2 = 2.0 * _GELU_C * _GELU_A


def _gelu(x):
    x2 = x * x
    s = 1.0 / (1.0 + jnp.exp(x * (-_GELU_K1 - _GELU_K2 * x2)))
    return x * s, (s, x2)


def _gelu_grad(x, saved):
    s, x2 = saved
    return s + x * (s * (1.0 - s)) * (_GELU_K1 + 3.0 * _GELU_K2 * x2)


def _sigmoid(x):
    return 1.0 / (1.0 + jnp.exp(-x))


def _lane_lo(shape):
    return lax.broadcasted_iota(jnp.int32, shape, 1) < HALF_HEAD_PAIR


def _swa_variants(t):
    lo = _lane_lo(t.shape)
    tr = pltpu.roll(t, HALF_HEAD_PAIR, 1)
    zero = jnp.zeros_like(t)
    return (jnp.where(lo, t, zero).astype(BF16), jnp.where(lo, zero, tr).astype(BF16),
            jnp.where(lo, tr, zero).astype(BF16), jnp.where(lo, zero, t).astype(BF16))


def _swa_unvariants(d0, d1, d2, d3):
    lo = _lane_lo(d0.shape)
    zero = jnp.zeros_like(d0)
    rolled = jnp.where(lo, zero, d1) + jnp.where(lo, d2, zero)
    return jnp.where(lo, d0, zero) + jnp.where(lo, zero, d3) + pltpu.roll(rolled, HALF_HEAD_PAIR, 1)


def _mem_variants(t):
    out = []
    for pair in range(2):
        tp = t[:, pair * 128:(pair + 1) * 128]
        lo = _lane_lo(tp.shape)
        zero = jnp.zeros_like(tp)
        out.append(jnp.where(lo, tp, zero).astype(BF16))
        out.append(jnp.where(lo, zero, tp).astype(BF16))
    return out


def _mem_unvariants(d0, d1, d2, d3):
    lo = _lane_lo(d0.shape)
    return jnp.concatenate([jnp.where(lo, d0, d1), jnp.where(lo, d2, d3)], axis=-1)


def _softmax(logits, sinks):
    m = jnp.max(logits, axis=-1, keepdims=True)
    if sinks is not None:
        m = jnp.maximum(m, sinks)
    p = jnp.exp(logits - m)
    den = jnp.sum(p, axis=-1, keepdims=True)
    if sinks is None:
        return p * (1.0 / den), None
    es = jnp.exp(sinks - m)
    inv = 1.0 / (den + es)
    return p * inv, es * inv


def _band_valid(with_prev):
    qi = lax.broadcasted_iota(jnp.int32, (CHUNK, 2 * CHUNK), 0)
    kj = lax.broadcasted_iota(jnp.int32, (CHUNK, 2 * CHUNK), 1)
    in_cur = (kj >= CHUNK) & (kj - CHUNK <= qi)
    if not with_prev:
        return in_cur
    return in_cur | ((kj < CHUNK) & (kj > qi))


def _causal_weights(ws_ref):
    row = lax.broadcasted_iota(jnp.int32, (CHUNK, CHUNK), 0)
    col = lax.broadcasted_iota(jnp.int32, (CHUNK, CHUNK), 1)
    return [jnp.where(row >= col, ws_ref[g], 0.0).astype(BF16) for g in range(A_GROUPS)]


def _rows_to_lanes(a, n):
    return jnp.concatenate([a[c * CHUNK:(c + 1) * CHUNK] for c in range(n)], axis=1)


def _lanes_to_rows(a, n):
    w = a.shape[1] // n
    return jnp.concatenate([a[:, c * w:(c + 1) * w] for c in range(n)], axis=0)


def _stack_heads(pair01, pair23):
    return jnp.concatenate([pair01[:, :256], pair01[:, 256:], pair23[:, :256], pair23[:, 256:]], axis=0)


def _pair_heads(s, r):
    return (jnp.concatenate([s[0:r], s[r:2 * r]], axis=1), jnp.concatenate([s[2 * r:3 * r], s[3 * r:4 * r]], axis=1))


def _pair_operands(variants):
    return (jnp.concatenate(variants[0:2], axis=0), jnp.concatenate(variants[2:4], axis=0))


def _split_pair_grads(d_pairs):
    return d_pairs[0][:256], d_pairs[0][256:], d_pairs[1][:256], d_pairs[1][256:]


def _halves_bf16(a):
    return (a[:, :128].astype(BF16), a[:, 128:].astype(BF16))


def _group_a_forward(au, av, vg, vb, wm, bs_rows):
    gu, tu = _gelu(au)
    gv, tv = _gelu(av)
    ya, res = [], []
    for g in range(A_GROUPS):
        sl = slice(g * 128, (g + 1) * 128)
        xg = gv[:, sl]
        xc = xg - jnp.mean(xg, axis=-1, keepdims=True)
        rstd = lax.rsqrt(jnp.mean(xc * xc, axis=-1, keepdims=True) + EPS)
        xhat = xc * rstd
        vn = _rows_to_lanes((xhat * vg[:, sl] + vb[:, sl]).astype(BF16), TILE_CHUNKS)
        s = _lanes_to_rows(_mm(wm[g], vn), TILE_CHUNKS) + bs_rows[g]
        ya.append(gu[:, sl] * s)
        res.append((xhat, rstd, vn, s))
    return ya, dict(gu=gu, tu=tu, tv=tv, groups=res)


def _attention_logits(qp, k_pairs):
    return _stack_heads(_mm_nt(qp[0], k_pairs[0]), _mm_nt(qp[1], k_pairs[1]))


def _attention_out(p, v_pairs, r):
    pp = _pair_heads(p.astype(BF16), r)
    return jnp.concatenate([_mm(pp[0], v_pairs[0]), _mm(pp[1], v_pairs[1])], axis=-1), pp


def _attention_dprobs(do_pairs, v_pairs):
    return _stack_heads(_mm_nt(do_pairs[0], v_pairs[0]), _mm_nt(do_pairs[1], v_pairs[1]))


def _softmax_backward(p, dp):
    delta = jnp.sum(p * dp, axis=-1, keepdims=True)
    return p * (dp - delta), delta


def _attention_grads(dl, pp, do_pairs, qp, k_pairs, r):
    dlp = _pair_heads(dl.astype(BF16), r)
    dq = jnp.concatenate([_mm(dlp[0], k_pairs[0]), _mm(dlp[1], k_pairs[1])], axis=-1)
    dk = (_mm_tn(dlp[0], qp[0]), _mm_tn(dlp[1], qp[1]))
    dv = (_mm_tn(pp[0], do_pairs[0]), _mm_tn(pp[1], do_pairs[1]))
    return dq, dk, dv


def _tile_specs(n_tiles_ex, width):
    return pl.BlockSpec((TILE, width), lambda b, i: (b * n_tiles_ex + jnp.minimum(i, n_tiles_ex - 1), 0))


def _prev_chunk_spec(n_tiles_ex, width):
    def index(b, i):
        chunk = TILE_CHUNKS * jnp.minimum(i, n_tiles_ex - 1)
        return (b * n_tiles_ex * TILE_CHUNKS + jnp.maximum(chunk - 1, 0), 0)
    return pl.BlockSpec((CHUNK, width), index)


def _full_spec(shape):
    zeros = (0,) * len(shape)
    return pl.BlockSpec(shape, lambda *_: zeros)


SMEM_SPEC = pl.BlockSpec(memory_space=pltpu.SMEM)
ANY_SPEC = pl.BlockSpec(memory_space=pl.ANY)


def _fill_bias(rel_ref, bk_ref, out_ref):
    bk = bk_ref[...]
    for h in range(4):
        acc = jnp.zeros((CHUNK, 2 * CHUNK), F32)
        for b in range(N_BUCKETS):
            acc = jnp.where(bk == b, rel_ref[h, b], acc)
        for t, with_prev in enumerate((True, False)):
            out_ref[t, h * CHUNK:(h + 1) * CHUNK, :] = jnp.where(_band_valid(with_prev), acc, NEG)


PROJ_WIDTHS = (A_WIDTH, A_WIDTH, SWA_WIDTH, KV_WIDTH, KV_WIDTH, MEM_WIDTH, MIX_WIDTH)
PROJ_OFFSETS = tuple(int(v) for v in np.cumsum((0,) + PROJ_WIDTHS))


MXU_TILE = 256
HALF_WIDTH = IN_WIDTH // 2
PHASE_COLS = (HALF_WIDTH // MXU_TILE * MXU_TILE, IN_WIDTH - HALF_WIDTH // MXU_TILE * MXU_TILE)


def _phase_columns(phase, chip_x):
    if phase == 0:
        return 0 if chip_x == 0 else IN_WIDTH - PHASE_COLS[0]
    return PHASE_COLS[0] if chip_x == 0 else 0


def _phase_parts(phase, chip_x):
    start = _phase_columns(phase, chip_x)
    return [(k, PROJ_OFFSETS[k] - start) for k in range(len(PROJ_WIDTHS))
            if start <= PROJ_OFFSETS[k] and PROJ_OFFSETS[k + 1] <= start + PHASE_COLS[phase]]


def _gather_and_project(x2, g_pre, w_in_s, w_mkv_s, w_out_s, rel_bias_t, buckets, b_spatial, x_arr):
    n_tok = x2.shape[0]
    n_tiles = n_tok // PROJ_TILE
    last = n_tiles - 1
    shapes = [w_in_s.shape, w_mkv_s.shape, w_out_s.shape]
    n_w = len(shapes)

    def body(x_sref, x_ref, g_ref, win_hbm, wmkv_hbm, wout_hbm, rel_ref, bk_ref, bsp_ref, h_ref, *refs):
        part_refs, refs = refs[:len(PROJ_WIDTHS)], refs[len(PROJ_WIDTHS):]
        bias_ref, bs_ref, refs = refs[0], refs[1], refs[2:]
        gin_hbm, gmkv_hbm, gout_hbm, wg, stage_in, stage_mkv, stage_out, own_mkv, own_out, h_all = refs[:10]
        send_sems, recv_sems, local_sems = refs[10:]
        p, t = pl.program_id(0), pl.program_id(1)
        x, y, c = lax.axis_index("x"), lax.axis_index("y"), lax.axis_index("c")
        me, sibling = (x, y, c), (x, y, 1 - c)
        my_shard = 2 * x + y
        gathered = [wg, gmkv_hbm, gout_hbm]

        def half_rows(w, shard, half):
            rows = shapes[w][0] // 2
            if w == 0:
                return wg.at[pl.ds(pl.multiple_of(shard * shapes[0][0] + half * rows, 16), rows), :]
            return gathered[w].at[shard, pl.ds(half * rows, rows), :]

        def first(w, rel):
            src = half_rows(w, my_shard, c) if w == 0 else (own_mkv, own_out)[w - 1].at[
                pl.ds(c * (shapes[w][0] // 2), shapes[w][0] // 2), :]
            k = 3 * w + rel - 1
            return pltpu.make_async_remote_copy(
                src_ref=src, dst_ref=half_rows(w, my_shard, c), send_sem=send_sems.at[k], recv_sem=recv_sems.at[k],
                device_id=(x ^ (rel >> 1), y ^ (rel & 1), c), device_id_type=MESH)

        def landed(w, rel):
            k = 3 * w + rel - 1
            ref = half_rows(w, my_shard ^ rel, c)
            return pltpu.make_async_remote_copy(src_ref=ref, dst_ref=ref, send_sem=send_sems.at[k],
                                                recv_sem=recv_sems.at[k], device_id=me, device_id_type=MESH)

        def passed(w, rel, half, to):
            k = 9 + 3 * w + rel - 1
            ref = half_rows(w, my_shard ^ rel, half)
            return pltpu.make_async_remote_copy(src_ref=ref, dst_ref=ref, send_sem=send_sems.at[k],
                                                recv_sem=recv_sems.at[k], device_id=to, device_id_type=MESH)

        def pass_on(w, rels):
            for rel in rels:
                landed(w, rel).wait_recv()
                passed(w, rel, c, sibling).start()
            for rel in rels:
                passed(w, rel, 1 - c, me).wait_recv()

        own_stores = [pltpu.make_async_copy(own_mkv, gmkv_hbm.at[my_shard], local_sems.at[3]),
                      pltpu.make_async_copy(own_out, gout_hbm.at[my_shard], local_sems.at[4])]

        @pl.when((p == 0) & (t == 0))
        def _():
            half_rows_in = shapes[0][0] // 2
            halves = [pl.ds(pl.multiple_of(hc * half_rows_in, 8), half_rows_in) for hc in (c, 1 - c)]
            loads = [pltpu.make_async_copy(win_hbm.at[halves[0], :], stage_in.at[halves[0], :], local_sems.at[0]),
                     pltpu.make_async_copy(wmkv_hbm, stage_mkv, local_sems.at[1]),
                     pltpu.make_async_copy(wout_hbm, stage_out, local_sems.at[2]),
                     pltpu.make_async_copy(win_hbm.at[halves[1], :], stage_in.at[halves[1], :], local_sems.at[6])]
            for cp in (loads[0], loads[3], loads[1], loads[2]):
                cp.start()
            loads[0].wait()
            half_rows(0, my_shard, c)[...] = stage_in[halves[0], :].astype(BF16)
            for rel in (1, 2):
                first(0, rel).start()
            loads[3].wait()
            half_rows(0, my_shard, 1 - c)[...] = stage_in[halves[1], :].astype(BF16)
            loads[1].wait()
            loads[2].wait()
            own_mkv[...] = stage_mkv[...].astype(BF16)
            own_out[...] = stage_out[...].astype(BF16)
            for cp in own_stores:
                cp.start()
            _fill_bias(rel_ref, bk_ref, bias_ref)
            for g in range(A_GROUPS):
                bs_ref[g] = jnp.transpose(jnp.broadcast_to(bsp_ref[g:g + 1, :], (CHUNK, CHUNK)))
            pass_on(0, (1,))
            first(0, 3).start()

        @pl.when((p == 0) & (t == n_tiles // 2))
        def _():
            for w in (1, 2):
                for rel in (1, 2, 3):
                    first(w, rel).start()

        store = pltpu.make_async_copy(wg, gin_hbm, local_sems.at[5])

        @pl.when((p == 1) & (t == 0))
        def _():
            pass_on(0, (2, 3))
            store.start()

        @pl.when((p == 1) & (t == n_tiles // 2))
        def _():
            for w in (1, 2):
                pass_on(w, (1, 2, 3))

        tile_rows = pl.ds(pl.multiple_of(t * PROJ_TILE, PROJ_TILE), PROJ_TILE)

        def project(h, phase):
            start = jnp.where(x_sref[0] == 0, _phase_columns(phase, 0), _phase_columns(phase, 1))
            proj = _mm_nt(h, wg[pl.ds(pl.multiple_of(start, MXU_TILE), PHASE_COLS[phase]), :])
            for chip_x in range(2):
                @pl.when(x_sref[0] == chip_x)
                def _():
                    for k, lo in _phase_parts(phase, chip_x):
                        part_refs[k][...] = proj[:, lo:lo + PROJ_WIDTHS[k]].astype(BF16)

        @pl.when(p == 0)
        def _():
            xv = x_ref[...]
            r = lax.rsqrt(jnp.mean(xv * xv, axis=-1, keepdims=True) + EPS)
            h = (xv * r * g_ref[...]).astype(BF16)
            h_ref[...] = h
            h_all[tile_rows, :] = h
            project(h, 0)

        @pl.when(p == 1)
        def _():
            project(h_all[tile_rows, :], 1)

        @pl.when((p == 1) & (t == last))
        def _():
            for w in range(n_w):
                for rel in (1, 2, 3):
                    first(w, rel).wait_send()
                    passed(w, rel, c, sibling).wait_send()
            for cp in own_stores:
                cp.wait()
            store.wait()

    def written_in(k):
        phase_on = [next(ph for ph in range(2) if k in dict(_phase_parts(ph, chip_x))) for chip_x in range(2)]

        def index(p, t, xs):
            phase = jnp.where(xs[0] == 0, phase_on[0], phase_on[1])
            return (jnp.where(p == phase, t, jnp.where(p < phase, 0, last)), 0)
        return index

    part_specs = [pl.BlockSpec((PROJ_TILE, PROJ_WIDTHS[k]), written_in(k)) for k in range(len(PROJ_WIDTHS))]
    vmem = pltpu.VMEM
    out = pl.pallas_call(
        body, name="gather_and_project",
        out_shape=[jax.ShapeDtypeStruct((n_tok, D_MODEL), BF16)]
        + [jax.ShapeDtypeStruct((n_tok, w), BF16) for w in PROJ_WIDTHS]
        + [jax.ShapeDtypeStruct((2, 4 * CHUNK, 2 * CHUNK), F32), jax.ShapeDtypeStruct((A_GROUPS, CHUNK, CHUNK), F32)]
        + [jax.ShapeDtypeStruct((N_CHIPS * shapes[0][0], shapes[0][1]), BF16)]
        + [jax.ShapeDtypeStruct((N_CHIPS,) + s, BF16) for s in shapes[1:]],
        grid_spec=pltpu.PrefetchScalarGridSpec(
            num_scalar_prefetch=1, grid=(2, n_tiles),
            in_specs=[pl.BlockSpec((PROJ_TILE, D_MODEL), lambda p, t, xs: (jnp.where(p == 0, t, last), 0)),
                      pl.BlockSpec((1, D_MODEL), lambda p, t, xs: (0, 0)), ANY_SPEC, ANY_SPEC, ANY_SPEC, SMEM_SPEC,
                      pl.BlockSpec(buckets.shape, lambda p, t, xs: (0, 0)),
                      pl.BlockSpec(b_spatial.shape, lambda p, t, xs: (0, 0))],
            out_specs=[pl.BlockSpec((PROJ_TILE, D_MODEL), lambda p, t, xs: (jnp.where(p == 0, t, last), 0))]
            + part_specs + [pl.BlockSpec((2, 4 * CHUNK, 2 * CHUNK), lambda p, t, xs: (0, 0, 0)),
                            pl.BlockSpec((A_GROUPS, CHUNK, CHUNK), lambda p, t, xs: (0, 0, 0))] + [ANY_SPEC] * 3,
            scratch_shapes=[vmem((N_CHIPS * shapes[0][0], shapes[0][1]), BF16), vmem(shapes[0], F32),
                            vmem(shapes[1], F32), vmem(shapes[2], F32), vmem(shapes[1], BF16), vmem(shapes[2], BF16),
                            vmem((n_tok, D_MODEL), BF16),
                            pltpu.SemaphoreType.DMA((18,)), pltpu.SemaphoreType.DMA((18,)),
                            pltpu.SemaphoreType.DMA((7,))]),
        compiler_params=pltpu.CompilerParams(vmem_limit_bytes=VMEM_LIMIT),
    )(x_arr, x2, g_pre, w_in_s, w_mkv_s, w_out_s, rel_bias_t, buckets, b_spatial)
    n_parts = len(PROJ_WIDTHS)
    return out[0], list(out[1:1 + n_parts]), out[3 + n_parts:], out[1 + n_parts], out[2 + n_parts]


def _load_chunk(j, i, sk_ref, sv_ref, skp_ref, svp_ref):
    rows = slice(j * CHUNK, (j + 1) * CHUNK)
    if j == 0:
        k_prev, v_prev, table = skp_ref[...], svp_ref[...], jnp.where(i > 0, 0, 1)
    else:
        prev = slice((j - 1) * CHUNK, j * CHUNK)
        k_prev, v_prev, table = sk_ref[prev, :], sv_ref[prev, :], 0
    k_pairs = _pair_operands(_swa_variants(jnp.concatenate([k_prev, sk_ref[rows, :]], axis=0).astype(F32)))
    v_pairs = _pair_operands(_swa_variants(jnp.concatenate([v_prev, sv_ref[rows, :]], axis=0).astype(F32)))
    return rows, k_pairs, v_pairs, table


def _tile_constants(ws_ref, bs_ref, sink_ref):
    wm = _causal_weights(ws_ref)
    bs_rows = [jnp.concatenate([bs_ref[g]] * TILE_CHUNKS, axis=0) for g in range(A_GROUPS)]
    sink_col = jnp.max(jnp.concatenate([jnp.full((CHUNK, 128), sink_ref[0, h], F32) for h in range(4)] * TILE_CHUNKS,
                                       axis=0), axis=-1, keepdims=True)
    return wm, bs_rows, sink_col


def _mix(parts, mem, x2, tgt2, v_g, v_b, w_sp, b_sp, sinks, bias, w_out, g_post, g_mem, w_mkv, g_pre, w_in_t, buckets,
         n_ex, seq):
    n_tiles_ex = seq // TILE
    n_tok = n_ex * seq
    au, av, sq, sk, sv, mq, z = parts
    col = dict(zip(("au", "av", "sq", "sk", "sv", "mq", "z"),
                   (slice(PROJ_OFFSETS[k], PROJ_OFFSETS[k + 1]) for k in range(len(PROJ_WIDTHS)))))
    before_kv, after_kv = slice(0, col["sk"].start), slice(col["sv"].stop, IN_WIDTH)
    kv_cols = slice(col["sk"].start, col["sv"].stop)
    gated = slice(col["au"].start, col["av"].stop)
    cut_a, cut_z = (s.start + 3 * (s.stop - s.start) // 4 for s in (gated, col["z"]))
    back_cols = ((slice(gated.start, cut_a),), (slice(col["z"].start, cut_z),),
                 (slice(cut_a, gated.stop), slice(cut_z, col["z"].stop)), (col["sq"], col["mq"]))
    assert sum(s.stop - s.start for part in back_cols for s in part) == IN_WIDTH - 2 * KV_WIDTH

    def body(au_ref, av_ref, sq_ref, sk_ref, sv_ref, skp_ref, svp_ref, mq_ref, z_ref, mem_ref, x_ref, tgt_ref,
             vg_ref, vb_ref, ws_ref, bs_ref, sink_ref, bias_ref, wout_ref, gpost_ref, gmem_ref, wmkv_ref,
             xl_ref, gpre_ref, bk_ref, win_hbm,
             dx_ref, dproj_ref, dwmkv_ref, dwout_ref, a_ref, b_ref,
             carry_dp, carry_k, carry_v, memn_s, mem_ops, dmkv_s, carry_dout, win_s, win_sem, dh_s,
             dgpre_ref, dgpost_ref, dgmem_ref, dvg_ref, dvb_ref, dws_ref, dbs_ref, dsink_ref, drel_ref, loss_ref):
        b, i = pl.program_id(0), pl.program_id(1)
        win_load = pltpu.make_async_copy(win_hbm, win_s, win_sem)

        @pl.when((b == 0) & (i == 0))
        def _():
            win_load.start()
            for ref in (dwmkv_ref, dwout_ref, dgpre_ref, dgpost_ref, dgmem_ref, dvg_ref, dvb_ref, dws_ref, dbs_ref,
                        dsink_ref, drel_ref, loss_ref, carry_dp):
                ref[...] = jnp.zeros_like(ref)

        def normalized_mem():
            m = mem_ref[0]
            return m * lax.rsqrt(jnp.mean(m * m, axis=-1, keepdims=True) + EPS)

        @pl.when(i == 0)
        def _():
            memn_s[...] = (normalized_mem() * gmem_ref[...]).astype(BF16)
            mkv = _mm(memn_s[...], wmkv_ref[...])
            for k, pair in enumerate(_pair_operands(_mem_variants(mkv[:, :MEM_WIDTH]))
                                     + _pair_operands(_mem_variants(mkv[:, MEM_WIDTH:]))):
                mem_ops[k] = pair
            dmkv_s[...] = jnp.zeros_like(dmkv_s)
            carry_k[...] = jnp.zeros_like(carry_k)
            carry_v[...] = jnp.zeros_like(carry_v)

        @pl.when((b == 0) & (i == 0))
        def _():
            win_load.wait()

        @pl.when(i > 0)
        def _():
            dproj_ref[:, before_kv] = carry_dp[:, before_kv]
            dproj_ref[:, after_kv] = carry_dp[:, after_kv]

        def project_back(part):
            return sum(_mm(carry_dp[:, s], win_s[s, :]) for s in back_cols[part])

        @pl.when(i < n_tiles_ex)
        def _():
            dh_s[...] = project_back(0)
            wm, bs_rows, sink_col = _tile_constants(ws_ref, bs_ref, sink_ref)
            mk_pairs, mv_pairs = (mem_ops[0], mem_ops[1]), (mem_ops[2], mem_ops[3])
            vg = vg_ref[...]

            au_v, av_v = au_ref[...].astype(F32), av_ref[...].astype(F32)
            ya, res = _group_a_forward(au_v, av_v, vg, vb_ref[...], wm, bs_rows)
            swa, logits, yb = [], [], []
            for j in range(TILE_CHUNKS):
                rows, k_pairs, v_pairs, table = _load_chunk(j, i, sk_ref, sv_ref, skp_ref, svp_ref)
                qp = _halves_bf16(sq_ref[rows, :] * QK_SCALE)
                logits.append(_attention_logits(qp, k_pairs) + bias_ref[table])
                swa.append([rows, k_pairs, v_pairs, qp])
            dh_s[...] += project_back(1)
            mqp = _halves_bf16(mq_ref[...] * QK_SCALE)
            logits_mem = _attention_logits(mqp, mk_pairs)
            p_swa, sink_p = _softmax(jnp.concatenate(logits, axis=0), sink_col)
            for j in range(TILE_CHUNKS):
                out, pp = _attention_out(p_swa[j * 4 * CHUNK:(j + 1) * 4 * CHUNK], swa[j][2], CHUNK)
                yb.append(out)
                swa[j].append(pp)
            pm, _ = _softmax(logits_mem, None)
            yc, ppm = _attention_out(pm, mv_pairs, TILE)
            ycat = jnp.concatenate(ya + [jnp.concatenate(yb, axis=0), yc], axis=-1)

            zv = z_ref[...].astype(F32)
            sig = _sigmoid(zv)
            sz = zv * sig
            y_b = (ycat * sz).astype(BF16)
            halves = (slice(0, TILE // 2), slice(TILE // 2, TILE))
            o_halves = [_mm(y_b[rows], wout_ref[...]) for rows in halves]
            dh_s[...] += project_back(2)
            gp = gpost_ref[...]
            do_halves = []
            for rows, o in zip(halves, o_halves):
                r2 = lax.rsqrt(jnp.mean(o * o, axis=-1, keepdims=True) + EPS)
                nrm = o * r2
                diff = x_ref[rows, :] + nrm * gp - tgt_ref[rows, :]
                loss_ref[...] += jnp.sum(diff * diff) * (0.5 / D_MODEL)
                dout = diff * (1.0 / D_MODEL)
                carry_dout[lax.rem(i, 2), rows, :] = dout
                dgpost_ref[...] += jnp.sum(dout * nrm, axis=0, keepdims=True)
                dn = dout * gp
                do_halves.append((r2 * (dn - nrm * jnp.mean(dn * nrm, axis=-1, keepdims=True))).astype(BF16))
            do_b = jnp.concatenate(do_halves, axis=0)
            dy = _mm_nt(do_b, wout_ref[...])
            carry_dp[:, col["z"]] = (dy * ycat * (sig + sz * (1.0 - sig))).astype(BF16)
            dyc = dy * sz

            dgu, dgv = [], []
            for g in range(A_GROUPS):
                sl = slice(g * 128, (g + 1) * 128)
                xhat, rstd, vn, s = res["groups"][g]
                dya = dyc[:, sl]
                dgu.append(dya * s)
                ds = dya * res["gu"][:, sl]
                dbs_ref[:, sl] += sum(ds[c * CHUNK:(c + 1) * CHUNK] for c in range(TILE_CHUNKS))
                ds_b = _rows_to_lanes(ds.astype(BF16), TILE_CHUNKS)
                dws_ref[g] += _mm_nt(ds_b, vn)
                dvn = _lanes_to_rows(_mm_tn(wm[g], ds_b), TILE_CHUNKS)
                dvg_ref[:, sl] += jnp.sum(dvn * xhat, axis=0, keepdims=True)
                dvb_ref[:, sl] += jnp.sum(dvn, axis=0, keepdims=True)
                dxh = dvn * vg[:, sl]
                dgv.append(rstd * (dxh - jnp.mean(dxh, axis=-1, keepdims=True)
                                   - xhat * jnp.mean(dxh * xhat, axis=-1, keepdims=True)))
            carry_dp[:, col["au"]] =(jnp.concatenate(dgu, axis=-1) * _gelu_grad(au_v, res["tu"])).astype(BF16)
            carry_dp[:, col["av"]] = (jnp.concatenate(dgv, axis=-1) * _gelu_grad(av_v, res["tv"])).astype(BF16)

            do_pairs = [_halves_bf16(dyc[rows, A_WIDTH:A_WIDTH + SWA_WIDTH]) for rows, *_ in swa]
            dp_swa = jnp.concatenate(
                [_attention_dprobs(do_pairs[j], swa[j][2]) for j in range(TILE_CHUNKS)], axis=0)
            dh_s[...] += project_back(3)
            dl_swa, delta = _softmax_backward(p_swa, dp_swa)
            sink_terms = sink_p * delta
            lane4 = lax.broadcasted_iota(jnp.int32, (1, 128), 1)
            dsink_vec = jnp.zeros((1, 128), F32)
            for h in range(4):
                head_sum = sum(jnp.sum(sink_terms[(4 * j + h) * CHUNK:(4 * j + h + 1) * CHUNK])
                               for j in range(TILE_CHUNKS))
                dsink_vec = dsink_vec + jnp.where(lane4 == h, -head_sum, 0.0)
            dsink_ref[...] += dsink_vec
            drel_ref[...] += sum(dl_swa[j * 4 * CHUNK:(j + 1) * 4 * CHUNK] for j in range(TILE_CHUNKS))
            dk_parts, dv_parts = [], []
            for j, (rows, k_pairs, v_pairs, qp, pp) in enumerate(swa):
                dq, dk, dv = _attention_grads(dl_swa[j * 4 * CHUNK:(j + 1) * 4 * CHUNK], pp, do_pairs[j], qp, k_pairs,
                                              CHUNK)
                carry_dp[rows, col["sq"]] = (dq * QK_SCALE).astype(BF16)
                dk_parts.append(_swa_unvariants(*_split_pair_grads(dk)))
                dv_parts.append(_swa_unvariants(*_split_pair_grads(dv)))

            dc_pairs = _halves_bf16(dyc[:, A_WIDTH + SWA_WIDTH:])
            dp_mem = _attention_dprobs(dc_pairs, mv_pairs)
            dwout_ref[...] += _mm_tn(y_b, do_b)
            dl_mem, _ = _softmax_backward(pm, dp_mem)
            dmq, dmk, dmv = _attention_grads(dl_mem, ppm, dc_pairs, mqp, mk_pairs, TILE)
            carry_dp[:, col["mq"]] = (dmq * QK_SCALE).astype(BF16)
            dmkv_s[...] += jnp.concatenate([_mem_unvariants(*_split_pair_grads(dmk)),
                                            _mem_unvariants(*_split_pair_grads(dmv))], axis=-1)

            for parts_c, carry, cols in ((dk_parts, carry_k, col["sk"]), (dv_parts, carry_v, col["sv"])):
                @pl.when(i > 0)
                def _():
                    dproj_ref[:, cols] = (carry[...] + jnp.concatenate(
                        [jnp.zeros((TILE - CHUNK, KV_WIDTH), F32), parts_c[0][:CHUNK]], axis=0)).astype(BF16)
                new = [parts_c[0][CHUNK:]]
                for j in range(1, TILE_CHUNKS):
                    new[-1] = new[-1] + parts_c[j][:CHUNK]
                    new.append(parts_c[j][CHUNK:])
                carry[...] = jnp.concatenate(new, axis=0)

        @pl.when(i == n_tiles_ex)
        def _():
            dproj_ref[:, col["sk"]] = carry_k[...].astype(BF16)
            dproj_ref[:, col["sv"]] = carry_v[...].astype(BF16)
            d_b = dmkv_s[...].astype(BF16)
            dwmkv_ref[...] += _mm_tn(memn_s[...], d_b)
            dgmem_ref[...] += jnp.sum(_mm_nt(d_b, wmkv_ref[...]) * normalized_mem(), axis=0, keepdims=True)
            dh_s[...] = sum(project_back(part) for part in range(len(back_cols)))

        @pl.when(i > 0)
        def _():
            xv = xl_ref[...]
            r = lax.rsqrt(jnp.mean(xv * xv, axis=-1, keepdims=True) + EPS)
            xn = xv * r
            dh = dh_s[...] + _mm(dproj_ref[:, kv_cols], win_s[kv_cols, :])
            dgpre_ref[...] += jnp.sum(dh * xn, axis=0, keepdims=True)
            dhg = dh * gpre_ref[...]
            dx_ref[...] = (r * (dhg - xn * jnp.mean(dhg * xn, axis=-1, keepdims=True))
                           + carry_dout[lax.rem(i + 1, 2)])

        @pl.when((b == n_ex - 1) & (i == n_tiles_ex))
        def _():
            _fill_small_grads(dgpre_ref, dgpost_ref, dgmem_ref, dvg_ref, dvb_ref, dws_ref, dbs_ref, dsink_ref,
                              drel_ref, loss_ref, bk_ref, a_ref, b_ref)

    tile = functools.partial(_tile_specs, n_tiles_ex)
    prev = functools.partial(_prev_chunk_spec, n_tiles_ex)
    late = lambda width: pl.BlockSpec((TILE, width), lambda b, i: (b * n_tiles_ex + jnp.maximum(i - 1, 0), 0))
    vmem_f32 = lambda *shape: pltpu.VMEM(shape, F32)
    return pl.pallas_call(
        body, name="mix", grid=(n_ex, n_tiles_ex + 1),
        out_shape=[jax.ShapeDtypeStruct((n_tok, D_MODEL), F32), jax.ShapeDtypeStruct((n_tok, IN_WIDTH), BF16),
                   jax.ShapeDtypeStruct((D_MODEL, 2 * MEM_WIDTH), F32), jax.ShapeDtypeStruct((MIX_WIDTH, D_MODEL), F32),
                   jax.ShapeDtypeStruct((SMALL_A_ROWS, D_MODEL), F32), jax.ShapeDtypeStruct((SMALL_B_ROWS, 128), F32)],
        in_specs=[tile(A_WIDTH), tile(A_WIDTH), tile(SWA_WIDTH), tile(KV_WIDTH), tile(KV_WIDTH),
                  prev(KV_WIDTH), prev(KV_WIDTH), tile(MEM_WIDTH), tile(MIX_WIDTH),
                  pl.BlockSpec((1, MEM_LEN, D_MODEL), lambda b, i: (b, 0, 0)),
                  tile(D_MODEL), tile(D_MODEL),
                  _full_spec((1, A_WIDTH)), _full_spec((1, A_WIDTH)), _full_spec((A_GROUPS, CHUNK, CHUNK)),
                  _full_spec((A_GROUPS, CHUNK, CHUNK)), SMEM_SPEC, _full_spec((2, 4 * CHUNK, 2 * CHUNK)),
                  _full_spec((MIX_WIDTH, D_MODEL)), _full_spec((1, D_MODEL)), _full_spec((1, D_MODEL)),
                  _full_spec((D_MODEL, 2 * MEM_WIDTH)),
                  late(D_MODEL), _full_spec((1, D_MODEL)), _full_spec((CHUNK, 2 * CHUNK)), ANY_SPEC],
        out_specs=[late(D_MODEL), late(IN_WIDTH), _full_spec((D_MODEL, 2 * MEM_WIDTH)),
                   _full_spec((MIX_WIDTH, D_MODEL)), _full_spec((SMALL_A_ROWS, D_MODEL)),
                   _full_spec((SMALL_B_ROWS, 128))],
        scratch_shapes=[pltpu.VMEM((TILE, IN_WIDTH), BF16), pltpu.VMEM((TILE, KV_WIDTH), F32),
                        pltpu.VMEM((TILE, KV_WIDTH), F32), pltpu.VMEM((MEM_LEN, D_MODEL), BF16),
                        pltpu.VMEM((4, 2 * MEM_LEN, 128), BF16), pltpu.VMEM((MEM_LEN, 2 * MEM_WIDTH), F32),
                        pltpu.VMEM((2, TILE, D_MODEL), F32), pltpu.VMEM((IN_WIDTH, D_MODEL), BF16),
                        pltpu.SemaphoreType.DMA, vmem_f32(TILE, D_MODEL),
                        vmem_f32(1, D_MODEL), vmem_f32(1, D_MODEL), vmem_f32(1, D_MODEL), vmem_f32(1, A_WIDTH),
                        vmem_f32(1, A_WIDTH), vmem_f32(A_GROUPS, CHUNK, CHUNK), vmem_f32(CHUNK, A_WIDTH),
                        vmem_f32(1, 128), vmem_f32(4 * CHUNK, 2 * CHUNK), vmem_f32(1, 128)],
        compiler_params=pltpu.CompilerParams(vmem_limit_bytes=VMEM_LIMIT),
    )(au, av, sq, sk, sv, sk, sv, mq, z, mem, x2, tgt2, v_g, v_b, w_sp, b_sp, sinks, bias, w_out, g_post, g_mem,
      w_mkv, x2, g_pre, buckets, w_in_t)


def _fill_small_grads(dgpre_ref, dgpost_ref, dgmem_ref, dvg_ref, dvb_ref, dws_ref, dbs_ref, dsink_ref, drel_ref,
                      loss_ref, bk_ref, a_ref, b_ref):
    a_ref[...] = jnp.zeros_like(a_ref)
    b_ref[...] = jnp.zeros_like(b_ref)
    a_ref[0:1, :] = dgpre_ref[...]
    a_ref[1:2, :] = dgpost_ref[...]
    a_ref[2:3, :] = dgmem_ref[...]
    a_ref[3:4, :] = jnp.concatenate([dvg_ref[...], dvb_ref[...]], axis=-1)
    a_ref[ROW_LOSS:ROW_LOSS + 1, 0:128] = loss_ref[...]
    row = lax.broadcasted_iota(jnp.int32, (CHUNK, CHUNK), 0)
    col = lax.broadcasted_iota(jnp.int32, (CHUNK, CHUNK), 1)
    for g in range(A_GROUPS):
        b_ref[ROW_WS + g * CHUNK:ROW_WS + (g + 1) * CHUNK, :] = jnp.where(row >= col, dws_ref[g], 0.0)
        by_token = jnp.transpose(dbs_ref[:, g * 128:(g + 1) * 128])
        b_ref[ROW_BS + g:ROW_BS + g + 1, :] = jnp.sum(by_token, axis=0, keepdims=True)
    b_ref[ROW_SINK:ROW_SINK + 1, :] = dsink_ref[...]
    bk = bk_ref[...]
    rel_row = lax.broadcasted_iota(jnp.int32, (8, 128), 0)
    rel_col = lax.broadcasted_iota(jnp.int32, (8, 128), 1)
    rel = jnp.zeros((8, 128), F32)
    for h in range(4):
        acc = drel_ref[h * CHUNK:(h + 1) * CHUNK, :]
        for b in range(N_BUCKETS):
            rel = jnp.where((rel_row == h) & (rel_col == b), jnp.sum(jnp.where(bk == b, acc, 0.0)), rel)
    b_ref[ROW_REL:ROW_REL + 8, :] = rel


SHARD_ROWS = IN_WIDTH // N_CHIPS
SHARD_WINDOW = 768
SHARD_HALF = SHARD_ROWS // 2
DWIN_TILE = 2048
N_REL = N_CHIPS - 1


def _shard_window_start(shard):
    return (shard * SHARD_ROWS // 128) * 128


def _reduce_gradients(dproj, h, big, small, shard_arr):
    n_tok = h.shape[0]
    tile = min(DWIN_TILE, n_tok)
    n_sub = n_tok // tile
    last = N_CHIPS - 1
    n_big, n_small = len(big), len(small)
    big_half = [g.shape[2:] for g in big]
    sem_big_d2d = 2 * N_CHIPS
    sem_big_ici = sem_big_d2d + n_big
    sem_big_swap = sem_big_ici + N_REL * n_big
    sem_small_d2d = sem_big_swap + n_big
    sem_small_ici = sem_small_d2d + n_small
    n_sems = sem_small_ici + N_REL * n_small
    loc_small = n_big
    loc_out_win = loc_small + n_small
    loc_out_big = loc_out_win + 2
    loc_out_small = loc_out_big + 2 * n_big
    n_local = loc_out_small + n_small

    def relation_of_slot(s):
        return (s + 2) % N_REL + 1

    def shard_of_slot(s, my_shard):
        return my_shard ^ jnp.where(s == last, 0, relation_of_slot(s))

    def body(shard_ref, dp_ref, h_hbm, *refs):
        h_vmem, h_sem, refs = refs[-2], refs[-1], refs[:-2]
        big_hbm, refs = refs[:n_big], refs[n_big:]
        small_hbm, refs = refs[:n_small], refs[n_small:]
        out_hbm, refs = refs[0], refs[1:]
        big_out, refs = refs[:n_big], refs[n_big:]
        small_out, refs = refs[:n_small], refs[n_small:]
        part, recv_d2d, send_ici, recv_ici, mine_buf, other_buf = refs[:6]
        refs = refs[6:]
        big_own, big_recv, big_send, big_land, big_mine, big_other = (
            refs[k * n_big:(k + 1) * n_big] for k in range(6))
        refs = refs[6 * n_big:]
        small_own, small_recv, small_all = (refs[k * n_small:(k + 1) * n_small] for k in range(3))
        send_sems, recv_sems, local_sems = refs[3 * n_small:]

        s, t = pl.program_id(0), pl.program_id(1)
        x, y, c = lax.axis_index("x"), lax.axis_index("y"), lax.axis_index("c")
        my_chip = 2 * x + y
        sibling = (x, y, 1 - c)
        my_rows = pl.ds(pl.multiple_of(c * SHARD_HALF, 8), SHARD_HALF)
        other_rows = pl.ds(pl.multiple_of((1 - c) * SHARD_HALF, 8), SHARD_HALF)

        def remote(src, dst, k, to):
            return pltpu.make_async_remote_copy(src_ref=src, dst_ref=dst, send_sem=send_sems.at[k],
                                                recv_sem=recv_sems.at[k], device_id=to, device_id_type=MESH)

        def chip_at(rel):
            return (x ^ (rel >> 1), y ^ (rel & 1), c)

        def to_sibling(k):
            return remote(part.at[k % 2, other_rows, :], recv_d2d.at[k], k, sibling)

        def to_chip(k):
            return remote(send_ici.at[k], recv_ici.at[k], N_CHIPS + k, chip_at(relation_of_slot(k)))

        swap = remote(mine_buf, other_buf, 2 * N_CHIPS - 1, sibling)
        big_load = [pltpu.make_async_copy(big_hbm[w].at[:, pl.ds(c, 1)], big_own[w], local_sems.at[w])
                    for w in range(n_big)]
        big_to_sibling = [remote(big_hbm[w].at[:, pl.ds(1 - c, 1)], big_recv[w], sem_big_d2d + w, sibling)
                          for w in range(n_big)]
        big_to_chip = [[remote(big_send[w].at[k], big_land[w].at[k], sem_big_ici + N_REL * w + k, chip_at(k + 1))
                        for k in range(N_REL)] for w in range(n_big)]
        big_swap = [remote(big_mine[w], big_other[w], sem_big_swap + w, sibling) for w in range(n_big)]
        small_load = [pltpu.make_async_copy(small_hbm[i], small_own[i], local_sems.at[loc_small + i])
                      for i in range(n_small)]
        small_to_sibling = [remote(small_hbm[i], small_recv[i], sem_small_d2d + i, sibling) for i in range(n_small)]
        small_to_chip = [[remote(small_all[i].at[my_chip], small_all[i].at[my_chip],
                                 sem_small_ici + N_REL * i + k, chip_at(k + 1))
                          for k in range(N_REL)] for i in range(n_small)]

        h_loads = [pltpu.make_async_copy(h_hbm.at[rows, :], h_vmem.at[rows, :], h_sem.at[k]) for k, rows in enumerate(
            [pl.ds(0, tile)] + ([pl.ds(tile, n_tok - tile)] if n_sub > 1 else []))]

        @pl.when((s == 0) & (t == 0))
        def _():
            for cp in h_loads + big_load + big_to_sibling + small_load + small_to_sibling:
                cp.start()
            h_loads[0].wait()

        if n_sub > 1:
            @pl.when((s == 0) & (t == 1))
            def _():
                h_loads[1].wait()

        @pl.when((s == 0) & (t == n_sub - 1))
        def _():
            for cp in big_load + small_load:
                cp.wait()
            for cp in big_to_sibling + small_to_sibling:
                cp.wait_recv()
                cp.wait_send()
            for w in range(n_big):
                for k in range(N_REL):
                    shard = my_chip ^ (k + 1)
                    big_send[w][k] = (big_own[w][shard, 0] + big_recv[w][shard, 0]).astype(BF16)
                    big_to_chip[w][k].start()
            for i in range(n_small):
                small_all[i][my_chip] = small_own[i][...] + small_recv[i][...]
                for k in range(N_REL):
                    small_to_chip[i][k].start()

        @pl.when((s > 0) & (t == jnp.where(s == last, 0, min(1, n_sub - 1))))
        def _():
            k = s - 1
            cp = to_sibling(k)
            cp.wait_recv()
            cp.wait_send()
            send_ici[k] = (part[k % 2, my_rows, :] + recv_d2d[k]).astype(BF16)
            to_chip(k).start()

        def big_rows(w, half):
            rows = big_half[w][0]
            return big_out[w].at[pl.ds(pl.multiple_of(half * rows, 8), rows), :]

        big_store_mine = [pltpu.make_async_copy(big_mine[w], big_rows(w, c), local_sems.at[loc_out_big + 2 * w])
                          for w in range(n_big)]
        big_store_other = [pltpu.make_async_copy(big_other[w], big_rows(w, 1 - c),
                                                 local_sems.at[loc_out_big + 2 * w + 1]) for w in range(n_big)]
        small_store = [pltpu.make_async_copy(small_all[i], small_out[i], local_sems.at[loc_out_small + i])
                       for i in range(n_small)]

        @pl.when((s == last) & (t == 0))
        def _():
            for w in range(n_big):
                total = big_own[w][my_chip, 0] + big_recv[w][my_chip, 0]
                for k in range(N_REL):
                    big_to_chip[w][k].wait_recv()
                    total = total + big_land[w][k].astype(F32)
                big_mine[w][...] = total
                big_swap[w].start()
                big_store_mine[w].start()
            for i in range(n_small):
                for k in range(N_REL):
                    small_to_chip[i][k].wait_recv()
                small_store[i].start()

        r = _mm_tn(dp_ref[...], h_vmem[pl.ds(pl.multiple_of(t * tile, tile), tile), :])
        odd = shard_of_slot(s, shard_ref[0]) % 2
        for parity in range(2):
            rows = r[64 * parity:64 * parity + SHARD_ROWS]

            @pl.when((odd == parity) & (t == 0))
            def _():
                part[s % 2] = rows

            @pl.when((odd == parity) & (t > 0))
            def _():
                part[s % 2] += rows

        @pl.when(t == n_sub - 1)
        def _():
            to_sibling(s).start()

        @pl.when((s == last) & (t == n_sub - 1))
        def _():
            cp = to_sibling(last)
            cp.wait_recv()
            cp.wait_send()
            total = part[last % 2, my_rows, :] + recv_d2d[last]
            for k in range(last):
                to_chip(k).wait_recv()
                total = total + recv_ici[k].astype(F32)
            mine_buf[...] = total
            swap.start()
            out_mine = pltpu.make_async_copy(mine_buf, out_hbm.at[my_rows, :], local_sems.at[0])
            out_mine.start()
            swap.wait_recv()
            out_other = pltpu.make_async_copy(other_buf, out_hbm.at[other_rows, :], local_sems.at[1])
            out_other.start()
            for w in range(n_big):
                big_swap[w].wait_recv()
                big_store_other[w].start()
            stores = [out_mine, out_other] + big_store_mine + big_store_other + small_store
            for k in range(last):
                to_chip(k).wait_send()
            swap.wait_send()
            for w in range(n_big):
                for k in range(N_REL):
                    big_to_chip[w][k].wait_send()
                big_swap[w].wait_send()
            for i in range(n_small):
                for k in range(N_REL):
                    small_to_chip[i][k].wait_send()
            for cp in stores:
                cp.wait()

    half = (SHARD_HALF, D_MODEL)
    vmem = pltpu.VMEM
    scratch = [vmem((2, SHARD_ROWS, D_MODEL), F32), vmem((N_CHIPS,) + half, F32),
               vmem((N_REL,) + half, BF16), vmem((N_REL,) + half, BF16), vmem(half, F32), vmem(half, F32)]
    scratch += [vmem((N_CHIPS, 1) + hs, F32) for hs in big_half] * 2
    scratch += [vmem((N_REL,) + hs, BF16) for hs in big_half] * 2
    scratch += [vmem(hs, F32) for hs in big_half] * 2
    scratch += [vmem(a.shape, F32) for a in small] * 2 + [vmem((N_CHIPS,) + a.shape, F32) for a in small]
    scratch += [pltpu.SemaphoreType.DMA((n_sems,)), pltpu.SemaphoreType.DMA((n_sems,)),
                pltpu.SemaphoreType.DMA((n_local,)), vmem(h.shape, BF16), pltpu.SemaphoreType.DMA((2,))]
    n_hbm = n_big + n_small
    out = pl.pallas_call(
        body, name="reduce_gradients",
        out_shape=[jax.ShapeDtypeStruct((SHARD_ROWS, D_MODEL), F32)]
        + [jax.ShapeDtypeStruct((2 * hs[0], hs[1]), F32) for hs in big_half]
        + [jax.ShapeDtypeStruct((N_CHIPS,) + a.shape, F32) for a in small],
        grid_spec=pltpu.PrefetchScalarGridSpec(
            num_scalar_prefetch=1, grid=(N_CHIPS, n_sub),
            in_specs=[pl.BlockSpec((pl.Element(tile), pl.Element(SHARD_WINDOW)),
                                   lambda s, t, m: (t * tile, _shard_window_start(shard_of_slot(s, m[0])))),
                      ANY_SPEC] + [ANY_SPEC] * n_hbm,
            out_specs=[ANY_SPEC] * (1 + n_hbm),
            scratch_shapes=scratch),
        compiler_params=pltpu.CompilerParams(vmem_limit_bytes=VMEM_LIMIT),
    )(shard_arr, dproj, h, *big, *small)
    return out[:1 + n_big], out[1 + n_big:]


def _adamw(w, g, m, v):
    m2 = ADAM_B1 * m + (1.0 - ADAM_B1) * g
    v2 = ADAM_B2 * v + (1.0 - ADAM_B2) * (g * g)
    m_hat = m2 / (1.0 - ADAM_B1 ** ADAM_STEP)
    v_hat = v2 / (1.0 - ADAM_B2 ** ADAM_STEP)
    delta = -ADAM_LR * (m_hat / (jnp.sqrt(v_hat) + ADAM_EPS) + ADAM_WD * w)
    return delta, m2, v2


ADAM_STEPS = 1


def _adamw_all(shard_grads, shard_w, shard_m, shard_v, ra, rb, small_w, small_m, small_v):
    n_sh, n = len(shard_w), len(small_w)

    def body(*refs):
        sh_in, refs = refs[:4 * n_sh], refs[4 * n_sh:]
        ra_ref, rb_ref, refs = refs[0], refs[1], refs[2:]
        w_refs, m_refs, v_refs, refs = refs[:n], refs[n:2 * n], refs[2 * n:3 * n], refs[3 * n:]
        sh_out, outs = refs[:4 * n_sh], refs[4 * n_sh:]
        for k in range(n_sh):
            g = sh_in[k][...]
            delta, m2, v2 = _adamw(sh_in[n_sh + k][...], g, sh_in[2 * n_sh + k][...], sh_in[3 * n_sh + k][...])
            for ref, val in zip(sh_out[4 * k:4 * k + 4], (g, delta, m2, v2)):
                ref[...] = val

        @pl.when(pl.program_id(0) == 0)
        def _():
            g_outs, d_outs, m_outs, v_outs = outs[:n], outs[n:2 * n], outs[2 * n:3 * n], outs[3 * n:4 * n]
            ga, gb = ra_ref[0], rb_ref[0]
            for chip in range(1, N_CHIPS):
                ga = ga + ra_ref[chip]
                gb = gb + rb_ref[chip]
            outs[4 * n][...] = ga[ROW_LOSS:ROW_LOSS + 1, 0:128]
            grads = [ga[0:1, :], ga[1:2, :], ga[2:3, :], ga[3:4, :A_WIDTH], ga[3:4, A_WIDTH:],
                     gb[ROW_WS:ROW_WS + A_GROUPS * CHUNK, :].reshape(A_GROUPS, CHUNK, CHUNK),
                     gb[ROW_BS:ROW_BS + A_GROUPS, :], gb[ROW_SINK:ROW_SINK + 1, 0:4],
                     gb[ROW_REL:ROW_REL + 4, 0:N_BUCKETS]]
            for k in range(n):
                delta, m2, v2 = _adamw(w_refs[k][...], grads[k], m_refs[k][...], v_refs[k][...])
                g_outs[k][...] = grads[k]
                d_outs[k][...] = delta
                m_outs[k][...] = m2
                v_outs[k][...] = v2

    def rows_block(a):
        assert a.shape[0] % (8 * ADAM_STEPS) == 0
        return pl.BlockSpec((a.shape[0] // ADAM_STEPS, a.shape[1]), lambda i: (i, 0))

    sh_specs = [rows_block(w) for w in shard_w]
    small_in = [ra, rb, *small_w, *small_m, *small_v]
    small_out_shapes = [jax.ShapeDtypeStruct(w.shape, F32) for w in small_w] * 4 + [jax.ShapeDtypeStruct((1, 128), F32)]
    out = pl.pallas_call(
        body, name="adamw_all", grid=(ADAM_STEPS,),
        out_shape=[jax.ShapeDtypeStruct(w.shape, F32) for w in shard_w for _ in range(4)] + small_out_shapes,
        in_specs=sh_specs * 4 + [_full_spec(a.shape) for a in small_in],
        out_specs=[spec for spec in sh_specs for _ in range(4)] + [_full_spec(s.shape) for s in small_out_shapes],
        compiler_params=pltpu.CompilerParams(vmem_limit_bytes=VMEM_LIMIT),
    )(*shard_grads, *shard_w, *shard_m, *shard_v, *small_in)
    return [out[4 * k:4 * k + 4] for k in range(n_sh)], out[4 * n_sh:]


def kernel(x, mem, pre_norm_g, post_norm_g, mem_norm_g, w_in, w_mem_kv, v_norm_g, v_norm_b, w_spatial, b_spatial, attn_sinks, rel_bias, w_out, loss_target, m_pre_norm_g, m_post_norm_g, m_mem_norm_g, m_w_in, m_w_mem_kv, m_v_norm_g, m_v_norm_b, m_w_spatial, m_b_spatial, m_attn_sinks, m_rel_bias, m_w_out, v_pre_norm_g, v_post_norm_g, v_mem_norm_g, v_w_in, v_w_mem_kv, v_v_norm_g, v_v_norm_b, v_w_spatial, v_b_spatial, v_attn_sinks, v_rel_bias, v_w_out):
    n_ex, seq, _ = x.shape
    n_tok = n_ex * seq
    x2 = x.reshape(n_tok, D_MODEL)
    tgt2 = loss_target.reshape(n_tok, D_MODEL)
    buckets = jnp.asarray(_bucket_map())
    shard_arr = (2 * lax.axis_index("x") + lax.axis_index("y")).astype(jnp.int32).reshape(1)
    w_sp = w_spatial[0]
    w_in_t, m_w_in_t, v_w_in_t = (jnp.transpose(a[0]) for a in (w_in, m_w_in, v_w_in))
    rel_t, m_rel_t, v_rel_t = (jnp.transpose(a) for a in (rel_bias, m_rel_bias, v_rel_bias))

    x_arr = lax.axis_index("x").astype(jnp.int32).reshape(1)
    h_b, parts, (w_in_b, g_mkv, g_out), bias, b_sp = _gather_and_project(
        x2, pre_norm_g, w_in_t, w_mem_kv[0], w_out[0], rel_t, buckets, b_spatial[0], x_arr)
    w_mkv_b = g_mkv.reshape(D_MODEL, 2 * MEM_WIDTH)
    w_out_b = g_out.reshape(MIX_WIDTH, D_MODEL)

    dx, dproj, dwmkv, dwout, small_a, small_b = _mix(
        parts, mem, x2, tgt2, v_norm_g, v_norm_b, w_sp, b_sp, attn_sinks, bias, w_out_b, post_norm_g, mem_norm_g,
        w_mkv_b, pre_norm_g, w_in_b, buckets, n_ex, seq)

    shard_shapes = [w_mem_kv.shape[1:], w_out.shape[1:]]
    big = [g.reshape(N_CHIPS, 2, s[0] // 2, s[1]) for g, s in zip((dwmkv, dwout), shard_shapes)]
    (g_win, g_wmkv, g_wout), (ga, gb) = _reduce_gradients(dproj, h_b, big, [small_a, small_b], shard_arr)

    small_w = [pre_norm_g, post_norm_g, mem_norm_g, v_norm_g, v_norm_b, w_sp, b_spatial[0], attn_sinks, rel_t]
    small_m = [m_pre_norm_g, m_post_norm_g, m_mem_norm_g, m_v_norm_g, m_v_norm_b, m_w_spatial[0], m_b_spatial[0],
               m_attn_sinks, m_rel_t]
    small_v = [v_pre_norm_g, v_post_norm_g, v_mem_norm_g, v_v_norm_g, v_v_norm_b, v_w_spatial[0], v_b_spatial[0],
               v_attn_sinks, v_rel_t]
    big_out, small_out = _adamw_all(
        [g_win, g_wmkv, g_wout], [w_in_t, w_mem_kv[0], w_out[0]], [m_w_in_t, m_w_mem_kv[0], m_w_out[0]],
        [v_w_in_t, v_w_mem_kv[0], v_w_out[0]], ga, gb, small_w, small_m, small_v)
    n_small = len(small_w)

    outputs = [small_out[4 * n_small][0, 0], dx.reshape(x.shape)]
    for kind in range(4):
        s = small_out[kind * n_small:(kind + 1) * n_small]
        outputs += [s[0], s[1], s[2], jnp.transpose(big_out[0][kind])[None], big_out[1][kind][None], s[3], s[4],
                    s[5][None], s[6][None], s[7], jnp.transpose(s[8]), big_out[2][kind][None]]
    return tuple(outputs)
```

```python
import functools

import numpy as np
import jax
import jax.numpy as jnp
from jax import lax
from jax.experimental import pallas as pl
from jax.experimental.pallas import tpu as pltpu

F32 = jnp.float32
BF16 = jnp.bfloat16
MESH = pl.DeviceIdType.MESH

D_MODEL = 1024
CHUNK = 128
A_WIDTH = 512
A_GROUPS = 4
SWA_WIDTH = 256
KV_WIDTH = 128
MEM_WIDTH = 256
MEM_LEN = 256
MIX_WIDTH = 1024
IN_WIDTH = 2816
N_BUCKETS = 32
MAX_DISTANCE = 128
EPS = 1e-6
NEG = -1e30
QK_SCALE = 0.125
HALF_HEAD_PAIR = 64

ADAM_LR = 0.001
ADAM_B1 = 0.9
ADAM_B2 = 0.999
ADAM_EPS = 1e-08
ADAM_WD = 0.01
ADAM_STEP = 10

N_CHIPS = 4
TILE_CHUNKS = 2
TILE = TILE_CHUNKS * CHUNK
PROJ_TILE = 512
VMEM_LIMIT = 56 * 1024 * 1024

SMALL_A_ROWS = 8
ROW_LOSS = 4
ROW_WS = 0
ROW_BS = 512
ROW_SINK = 520
ROW_REL = 528
SMALL_B_ROWS = 536


def _mm(a, b):
    return lax.dot_general(a, b, (((1,), (0,)), ((), ())), preferred_element_type=F32)


def _mm_nt(a, b):
    return lax.dot_general(a, b, (((1,), (1,)), ((), ())), preferred_element_type=F32)


def _mm_tn(a, b):
    return lax.dot_general(a, b, (((0,), (0,)), ((), ())), preferred_element_type=F32)


def _bucket_map():
    qi = np.arange(CHUNK)[:, None]
    kj = np.arange(2 * CHUNK)[None, :]
    n = np.maximum(qi + CHUNK - kj, 0)
    max_exact = N_BUCKETS // 2
    large = max_exact + (np.log(np.maximum(n, 1) / max_exact) / np.log(MAX_DISTANCE / max_exact)
                         * (N_BUCKETS - max_exact)).astype(np.int32)
    large = np.minimum(large, N_BUCKETS - 1)
    return np.where(n < max_exact, n, large).astype(np.int32)


_GELU_C = 0.7978845608028654
_GELU_A = 0.044715
_GELU_K1 = 2.0 * _GELU_C
_GELU_K2 = 2.0 * _GELU_C * _GELU_A


def _gelu(x):
    x2 = x * x
    s = 1.0 / (1.0 + jnp.exp(x * (-_GELU_K1 - _GELU_K2 * x2)))
    return x * s, (s, x2)


def _gelu_grad(x, saved):
    s, x2 = saved
    return s + x * (s * (1.0 - s)) * (_GELU_K1 + 3.0 * _GELU_K2 * x2)


def _sigmoid(x):
    return 1.0 / (1.0 + jnp.exp(-x))


def _lane_lo(shape):
    return lax.broadcasted_iota(jnp.int32, shape, 1) < HALF_HEAD_PAIR


def _swa_variants(t):
    lo = _lane_lo(t.shape)
    tr = pltpu.roll(t, HALF_HEAD_PAIR, 1)
    zero = jnp.zeros_like(t)
    return (jnp.where(lo, t, zero).astype(BF16), jnp.where(lo, zero, tr).astype(BF16),
            jnp.where(lo, tr, zero).astype(BF16), jnp.where(lo, zero, t).astype(BF16))


def _swa_unvariants(d0, d1, d2, d3):
    lo = _lane_lo(d0.shape)
    zero = jnp.zeros_like(d0)
    rolled = jnp.where(lo, zero, d1) + jnp.where(lo, d2, zero)
    return jnp.where(lo, d0, zero) + jnp.where(lo, zero, d3) + pltpu.roll(rolled, HALF_HEAD_PAIR, 1)


def _mem_variants(t):
    out = []
    for pair in range(2):
        tp = t[:, pair * 128:(pair + 1) * 128]
        lo = _lane_lo(tp.shape)
        zero = jnp.zeros_like(tp)
        out.append(jnp.where(lo, tp, zero).astype(BF16))
        out.append(jnp.where(lo, zero, tp).astype(BF16))
    return out


def _mem_unvariants(d0, d1, d2, d3):
    lo = _lane_lo(d0.shape)
    return jnp.concatenate([jnp.where(lo, d0, d1), jnp.where(lo, d2, d3)], axis=-1)


def _softmax(logits, sinks):
    m = jnp.max(logits, axis=-1, keepdims=True)
    if sinks is not None:
        m = jnp.maximum(m, sinks)
    p = jnp.exp(logits - m)
    den = jnp.sum(p, axis=-1, keepdims=True)
    if sinks is None:
        return p * (1.0 / den), None
    es = jnp.exp(sinks - m)
    inv = 1.0 / (den + es)
    return p * inv, es * inv


def _band_valid(with_prev):
    qi = lax.broadcasted_iota(jnp.int32, (CHUNK, 2 * CHUNK), 0)
    kj = lax.broadcasted_iota(jnp.int32, (CHUNK, 2 * CHUNK), 1)
    in_cur = (kj >= CHUNK) & (kj - CHUNK <= qi)
    if not with_prev:
        return in_cur
    return in_cur | ((kj < CHUNK) & (kj > qi))


def _causal_weights(ws_ref):
    row = lax.broadcasted_iota(jnp.int32, (CHUNK, CHUNK), 0)
    col = lax.broadcasted_iota(jnp.int32, (CHUNK, CHUNK), 1)
    return [jnp.where(row >= col, ws_ref[g], 0.0).astype(BF16) for g in range(A_GROUPS)]


def _rows_to_lanes(a, n):
    return jnp.concatenate([a[c * CHUNK:(c + 1) * CHUNK] for c in range(n)], axis=1)


def _lanes_to_rows(a, n):
    w = a.shape[1] // n
    return jnp.concatenate([a[:, c * w:(c + 1) * w] for c in range(n)], axis=0)


def _stack_heads(pair01, pair23):
    return jnp.concatenate([pair01[:, :256], pair01[:, 256:], pair23[:, :256], pair23[:, 256:]], axis=0)


def _pair_heads(s, r):
    return (jnp.concatenate([s[0:r], s[r:2 * r]], axis=1), jnp.concatenate([s[2 * r:3 * r], s[3 * r:4 * r]], axis=1))


def _pair_operands(variants):
    return (jnp.concatenate(variants[0:2], axis=0), jnp.concatenate(variants[2:4], axis=0))


def _split_pair_grads(d_pairs):
    return d_pairs[0][:256], d_pairs[0][256:], d_pairs[1][:256], d_pairs[1][256:]


def _halves_bf16(a):
    return (a[:, :128].astype(BF16), a[:, 128:].astype(BF16))


def _group_a_forward(au, av, vg, vb, wm, bs_rows):
    gu, tu = _gelu(au)
    gv, tv = _gelu(av)
    ya, res = [], []
    for g in range(A_GROUPS):
        sl = slice(g * 128, (g + 1) * 128)
        xg = gv[:, sl]
        xc = xg - jnp.mean(xg, axis=-1, keepdims=True)
        rstd = lax.rsqrt(jnp.mean(xc * xc, axis=-1, keepdims=True) + EPS)
        xhat = xc * rstd
        vn = _rows_to_lanes((xhat * vg[:, sl] + vb[:, sl]).astype(BF16), TILE_CHUNKS)
        s = _lanes_to_rows(_mm(wm[g], vn), TILE_CHUNKS) + bs_rows[g]
        ya.append(gu[:, sl] * s)
        res.append((xhat, rstd, vn, s))
    return ya, dict(gu=gu, tu=tu, tv=tv, groups=res)


def _attention_logits(qp, k_pairs):
    return _stack_heads(_mm_nt(qp[0], k_pairs[0]), _mm_nt(qp[1], k_pairs[1]))


def _attention_out(p, v_pairs, r):
    pp = _pair_heads(p.astype(BF16), r)
    return jnp.concatenate([_mm(pp[0], v_pairs[0]), _mm(pp[1], v_pairs[1])], axis=-1), pp


def _attention_dprobs(do_pairs, v_pairs):
    return _stack_heads(_mm_nt(do_pairs[0], v_pairs[0]), _mm_nt(do_pairs[1], v_pairs[1]))


def _softmax_backward(p, dp):
    delta = jnp.sum(p * dp, axis=-1, keepdims=True)
    return p * (dp - delta), delta


def _attention_grads(dl, pp, do_pairs, qp, k_pairs, r):
    dlp = _pair_heads(dl.astype(BF16), r)
    dq = jnp.concatenate([_mm(dlp[0], k_pairs[0]), _mm(dlp[1], k_pairs[1])], axis=-1)
    dk = (_mm_tn(dlp[0], qp[0]), _mm_tn(dlp[1], qp[1]))
    dv = (_mm_tn(pp[0], do_pairs[0]), _mm_tn(pp[1], do_pairs[1]))
    return dq, dk, dv


def _tile_specs(n_tiles_ex, width):
    return pl.BlockSpec((TILE, width), lambda b, i: (b * n_tiles_ex + jnp.minimum(i, n_tiles_ex - 1), 0))


def _full_spec(shape):
    zeros = (0,) * len(shape)
    return pl.BlockSpec(shape, lambda *_: zeros)


SMEM_SPEC = pl.BlockSpec(memory_space=pltpu.SMEM)
ANY_SPEC = pl.BlockSpec(memory_space=pl.ANY)


def _fill_bias(rel_ref, bk_ref, out_ref):
    bk = bk_ref[...]
    for h in range(4):
        acc = jnp.zeros((CHUNK, 2 * CHUNK), F32)
        for b in range(N_BUCKETS):
            acc = jnp.where(bk == b, rel_ref[h, b], acc)
        for t, with_prev in enumerate((True, False)):
            out_ref[t, h * CHUNK:(h + 1) * CHUNK, :] = jnp.where(_band_valid(with_prev), acc, NEG)


PROJ_WIDTHS = (A_WIDTH, A_WIDTH, SWA_WIDTH, KV_WIDTH, KV_WIDTH, MEM_WIDTH, MIX_WIDTH)
PROJ_OFFSETS = tuple(int(v) for v in np.cumsum((0,) + PROJ_WIDTHS))


MXU_TILE = 256
HALF_WIDTH = IN_WIDTH // 2
PHASE_COLS = (HALF_WIDTH // MXU_TILE * MXU_TILE, IN_WIDTH - HALF_WIDTH // MXU_TILE * MXU_TILE)


def _phase_columns(phase, chip_x):
    if phase == 0:
        return 0 if chip_x == 0 else IN_WIDTH - PHASE_COLS[0]
    return PHASE_COLS[0] if chip_x == 0 else 0


def _phase_parts(phase, chip_x):
    start = _phase_columns(phase, chip_x)
    return [(k, PROJ_OFFSETS[k] - start) for k in range(len(PROJ_WIDTHS))
            if start <= PROJ_OFFSETS[k] and PROJ_OFFSETS[k + 1] <= start + PHASE_COLS[phase]]


def _gather_and_project(x2, g_pre, w_in_s, w_mkv_s, w_out_s, rel_bias_t, buckets, b_spatial, x_arr):
    n_tok = x2.shape[0]
    n_tiles = n_tok // PROJ_TILE
    last = n_tiles - 1
    shapes = [w_in_s.shape, w_mkv_s.shape, w_out_s.shape]
    n_w = len(shapes)

    def body(x_sref, x_ref, g_ref, win_hbm, wmkv_hbm, wout_hbm, rel_ref, bk_ref, bsp_ref, h_ref, *refs):
        part_refs, refs = refs[:len(PROJ_WIDTHS)], refs[len(PROJ_WIDTHS):]
        bias_ref, bs_ref, refs = refs[0], refs[1], refs[2:]
        gin_hbm, gmkv_hbm, gout_hbm, wg, stage_in, stage_mkv, stage_out, own_mkv, own_out, h_all = refs[:10]
        send_sems, recv_sems, local_sems = refs[10:]
        p, t = pl.program_id(0), pl.program_id(1)
        x, y, c = lax.axis_index("x"), lax.axis_index("y"), lax.axis_index("c")
        me, sibling = (x, y, c), (x, y, 1 - c)
        my_shard = 2 * x + y
        gathered = [wg, gmkv_hbm, gout_hbm]

        def half_rows(w, shard, half):
            rows = shapes[w][0] // 2
            if w == 0:
                return wg.at[pl.ds(pl.multiple_of(shard * shapes[0][0] + half * rows, 16), rows), :]
            return gathered[w].at[shard, pl.ds(half * rows, rows), :]

        def first(w, rel):
            src = half_rows(w, my_shard, c) if w == 0 else (own_mkv, own_out)[w - 1].at[
                pl.ds(c * (shapes[w][0] // 2), shapes[w][0] // 2), :]
            k = 3 * w + rel - 1
            return pltpu.make_async_remote_copy(
                src_ref=src, dst_ref=half_rows(w, my_shard, c), send_sem=send_sems.at[k], recv_sem=recv_sems.at[k],
                device_id=(x ^ (rel >> 1), y ^ (rel & 1), c), device_id_type=MESH)

        def landed(w, rel):
            k = 3 * w + rel - 1
            ref = half_rows(w, my_shard ^ rel, c)
            return pltpu.make_async_remote_copy(src_ref=ref, dst_ref=ref, send_sem=send_sems.at[k],
                                                recv_sem=recv_sems.at[k], device_id=me, device_id_type=MESH)

        def passed(w, rel, half, to):
            k = 9 + 3 * w + rel - 1
            ref = half_rows(w, my_shard ^ rel, half)
            return pltpu.make_async_remote_copy(src_ref=ref, dst_ref=ref, send_sem=send_sems.at[k],
                                                recv_sem=recv_sems.at[k], device_id=to, device_id_type=MESH)

        def pass_on(w, rels):
            for rel in rels:
                landed(w, rel).wait_recv()
                passed(w, rel, c, sibling).start()
            for rel in rels:
                passed(w, rel, 1 - c, me).wait_recv()

        own_stores = [pltpu.make_async_copy(own_mkv, gmkv_hbm.at[my_shard], local_sems.at[3]),
                      pltpu.make_async_copy(own_out, gout_hbm.at[my_shard], local_sems.at[4])]

        @pl.when((p == 0) & (t == 0))
        def _():
            half_rows_in = shapes[0][0] // 2
            halves = [pl.ds(pl.multiple_of(hc * half_rows_in, 8), half_rows_in) for hc in (c, 1 - c)]
            loads = [pltpu.make_async_copy(win_hbm.at[halves[0], :], stage_in.at[halves[0], :], local_sems.at[0]),
                     pltpu.make_async_copy(wmkv_hbm, stage_mkv, local_sems.at[1]),
                     pltpu.make_async_copy(wout_hbm, stage_out, local_sems.at[2]),
                     pltpu.make_async_copy(win_hbm.at[halves[1], :], stage_in.at[halves[1], :], local_sems.at[6])]
            for cp in (loads[0], loads[3], loads[1], loads[2]):
                cp.start()
            loads[0].wait()
            half_rows(0, my_shard, c)[...] = stage_in[halves[0], :].astype(BF16)
            for rel in (1, 2):
                first(0, rel).start()
            loads[3].wait()
            half_rows(0, my_shard, 1 - c)[...] = stage_in[halves[1], :].astype(BF16)
            loads[1].wait()
            loads[2].wait()
            own_mkv[...] = stage_mkv[...].astype(BF16)
            own_out[...] = stage_out[...].astype(BF16)
            for cp in own_stores:
                cp.start()
            _fill_bias(rel_ref, bk_ref, bias_ref)
            for g in range(A_GROUPS):
                bs_ref[g] = jnp.transpose(jnp.broadcast_to(bsp_ref[g:g + 1, :], (CHUNK, CHUNK)))
            pass_on(0, (1,))
            first(0, 3).start()

        @pl.when((p == 0) & (t == n_tiles // 2))
        def _():
            for w in (1, 2):
                for rel in (1, 2, 3):
                    first(w, rel).start()

        store = pltpu.make_async_copy(wg, gin_hbm, local_sems.at[5])

        @pl.when((p == 1) & (t == 0))
        def _():
            pass_on(0, (2, 3))
            store.start()

        @pl.when((p == 1) & (t == n_tiles // 2))
        def _():
            for w in (1, 2):
                pass_on(w, (1, 2, 3))

        tile_rows = pl.ds(pl.multiple_of(t * PROJ_TILE, PROJ_TILE), PROJ_TILE)

        def project(h, phase):
            start = jnp.where(x_sref[0] == 0, _phase_columns(phase, 0), _phase_columns(phase, 1))
            proj = _mm_nt(h, wg[pl.ds(pl.multiple_of(start, MXU_TILE), PHASE_COLS[phase]), :])
            for chip_x in range(2):
                @pl.when(x_sref[0] == chip_x)
                def _():
                    for k, lo in _phase_parts(phase, chip_x):
                        part_refs[k][...] = proj[:, lo:lo + PROJ_WIDTHS[k]].astype(BF16)

        @pl.when(p == 0)
        def _():
            xv = x_ref[...]
            r = lax.rsqrt(jnp.mean(xv * xv, axis=-1, keepdims=True) + EPS)
            h = (xv * r * g_ref[...]).astype(BF16)
            h_ref[...] = h
            h_all[tile_rows, :] = h
            project(h, 0)

        @pl.when(p == 1)
        def _():
            project(h_all[tile_rows, :], 1)

        @pl.when((p == 1) & (t == last))
        def _():
            for w in range(n_w):
                for rel in (1, 2, 3):
                    first(w, rel).wait_send()
                    passed(w, rel, c, sibling).wait_send()
            for cp in own_stores:
                cp.wait()
            store.wait()

    def written_in(k):
        phase_on = [next(ph for ph in range(2) if k in dict(_phase_parts(ph, chip_x))) for chip_x in range(2)]

        def index(p, t, xs):
            phase = jnp.where(xs[0] == 0, phase_on[0], phase_on[1])
            return (jnp.where(p == phase, t, jnp.where(p < phase, 0, last)), 0)
        return index

    part_specs = [pl.BlockSpec((PROJ_TILE, PROJ_WIDTHS[k]), written_in(k)) for k in range(len(PROJ_WIDTHS))]
    vmem = pltpu.VMEM
    out = pl.pallas_call(
        body, name="gather_and_project",
        out_shape=[jax.ShapeDtypeStruct((n_tok, D_MODEL), BF16)]
        + [jax.ShapeDtypeStruct((n_tok, w), BF16) for w in PROJ_WIDTHS]
        + [jax.ShapeDtypeStruct((2, 4 * CHUNK, 2 * CHUNK), F32), jax.ShapeDtypeStruct((A_GROUPS, CHUNK, CHUNK), F32)]
        + [jax.ShapeDtypeStruct((N_CHIPS * shapes[0][0], shapes[0][1]), BF16)]
        + [jax.ShapeDtypeStruct((N_CHIPS,) + s, BF16) for s in shapes[1:]],
        grid_spec=pltpu.PrefetchScalarGridSpec(
            num_scalar_prefetch=1, grid=(2, n_tiles),
            in_specs=[pl.BlockSpec((PROJ_TILE, D_MODEL), lambda p, t, xs: (jnp.where(p == 0, t, last), 0)),
                      pl.BlockSpec((1, D_MODEL), lambda p, t, xs: (0, 0)), ANY_SPEC, ANY_SPEC, ANY_SPEC, SMEM_SPEC,
                      pl.BlockSpec(buckets.shape, lambda p, t, xs: (0, 0)),
                      pl.BlockSpec(b_spatial.shape, lambda p, t, xs: (0, 0))],
            out_specs=[pl.BlockSpec((PROJ_TILE, D_MODEL), lambda p, t, xs: (jnp.where(p == 0, t, last), 0))]
            + part_specs + [pl.BlockSpec((2, 4 * CHUNK, 2 * CHUNK), lambda p, t, xs: (0, 0, 0)),
                            pl.BlockSpec((A_GROUPS, CHUNK, CHUNK), lambda p, t, xs: (0, 0, 0))] + [ANY_SPEC] * 3,
            scratch_shapes=[vmem((N_CHIPS * shapes[0][0], shapes[0][1]), BF16), vmem(shapes[0], F32),
                            vmem(shapes[1], F32), vmem(shapes[2], F32), vmem(shapes[1], BF16), vmem(shapes[2], BF16),
                            vmem((n_tok, D_MODEL), BF16),
                            pltpu.SemaphoreType.DMA((18,)), pltpu.SemaphoreType.DMA((18,)),
                            pltpu.SemaphoreType.DMA((7,))]),
        compiler_params=pltpu.CompilerParams(vmem_limit_bytes=VMEM_LIMIT),
    )(x_arr, x2, g_pre, w_in_s, w_mkv_s, w_out_s, rel_bias_t, buckets, b_spatial)
    n_parts = len(PROJ_WIDTHS)
    return out[0], list(out[1:1 + n_parts]), out[3 + n_parts:], out[1 + n_parts], out[2 + n_parts]


def _load_chunk(j, i, sk_ref, sv_ref, skp_ref, svp_ref):
    rows = slice(j * CHUNK, (j + 1) * CHUNK)
    if j == 0:
        k_prev, v_prev, table = skp_ref[...], svp_ref[...], jnp.where(i > 0, 0, 1)
    else:
        prev = slice((j - 1) * CHUNK, j * CHUNK)
        k_prev, v_prev, table = sk_ref[prev, :], sv_ref[prev, :], 0
    k_pairs = _pair_operands(_swa_variants(jnp.concatenate([k_prev, sk_ref[rows, :]], axis=0).astype(F32)))
    v_pairs = _pair_operands(_swa_variants(jnp.concatenate([v_prev, sv_ref[rows, :]], axis=0).astype(F32)))
    return rows, k_pairs, v_pairs, table


def _tile_constants(ws_ref, bs_ref, sink_ref):
    wm = _causal_weights(ws_ref)
    bs_rows = [jnp.concatenate([bs_ref[g]] * TILE_CHUNKS, axis=0) for g in range(A_GROUPS)]
    sink_col = jnp.max(jnp.concatenate([jnp.full((CHUNK, 128), sink_ref[0, h], F32) for h in range(4)] * TILE_CHUNKS,
                                       axis=0), axis=-1, keepdims=True)
    return wm, bs_rows, sink_col


def _mix(parts, mem, x2, tgt2, v_g, v_b, w_sp, b_sp, sinks, bias, w_out, g_post, g_mem, w_mkv, g_pre, w_in_t, buckets,
         n_ex, seq):
    n_tiles_ex = seq // TILE
    n_tok = n_ex * seq
    au, av, sq, sk, sv, mq, z = parts
    col = dict(zip(("au", "av", "sq", "sk", "sv", "mq", "z"),
                   (slice(PROJ_OFFSETS[k], PROJ_OFFSETS[k + 1]) for k in range(len(PROJ_WIDTHS)))))
    before_kv, after_kv = slice(0, col["sk"].start), slice(col["sv"].stop, IN_WIDTH)
    kv_cols = slice(col["sk"].start, col["sv"].stop)
    gated = slice(col["au"].start, col["av"].stop)
    cut_a, cut_z = (s.start + 3 * (s.stop - s.start) // 4 for s in (gated, col["z"]))
    back_cols = ((slice(gated.start, cut_a),), (slice(col["z"].start, cut_z),),
                 (slice(cut_a, gated.stop), slice(cut_z, col["z"].stop)), (col["sq"], col["mq"]))
    assert sum(s.stop - s.start for part in back_cols for s in part) == IN_WIDTH - 2 * KV_WIDTH

    def body(au_ref, av_ref, sq_ref, sk_ref, sv_ref, mq_ref, z_ref, mem_ref, x_ref, tgt_ref,
             vg_ref, vb_ref, ws_ref, bs_ref, sink_ref, bias_ref, wout_ref, gpost_ref, gmem_ref, wmkv_ref,
             gpre_ref, bk_ref, win_hbm,
             dx_ref, dproj_ref, dwmkv_ref, dwout_ref, a_ref, b_ref,
             carry_dp, carry_k, carry_v, memn_s, mem_ops, dmkv_s, carry_dout, win_s, win_sem, dh_s,
             skp_ref, svp_ref, carry_x,
             dgpre_ref, dgpost_ref, dgmem_ref, dvg_ref, dvb_ref, dws_ref, dbs_ref, dsink_ref, drel_ref, loss_ref):
        b, i = pl.program_id(0), pl.program_id(1)
        win_load = pltpu.make_async_copy(win_hbm, win_s, win_sem)

        @pl.when((b == 0) & (i == 0))
        def _():
            win_load.start()
            for ref in (dwmkv_ref, dwout_ref, dgpre_ref, dgpost_ref, dgmem_ref, dvg_ref, dvb_ref, dws_ref, dbs_ref,
                        dsink_ref, drel_ref, loss_ref, carry_dp, skp_ref, svp_ref):
                ref[...] = jnp.zeros_like(ref)

        def normalized_mem():
            m = mem_ref[0]
            return m * lax.rsqrt(jnp.mean(m * m, axis=-1, keepdims=True) + EPS)

        @pl.when(i == 0)
        def _():
            memn_s[...] = (normalized_mem() * gmem_ref[...]).astype(BF16)
            mkv = _mm(memn_s[...], wmkv_ref[...])
            for k, pair in enumerate(_pair_operands(_mem_variants(mkv[:, :MEM_WIDTH]))
                                     + _pair_operands(_mem_variants(mkv[:, MEM_WIDTH:]))):
                mem_ops[k] = pair
            dmkv_s[...] = jnp.zeros_like(dmkv_s)
            carry_k[...] = jnp.zeros_like(carry_k)
            carry_v[...] = jnp.zeros_like(carry_v)

        @pl.when((b == 0) & (i == 0))
        def _():
            win_load.wait()

        @pl.when(i > 0)
        def _():
            dproj_ref[:, before_kv] = carry_dp[:, before_kv]
            dproj_ref[:, after_kv] = carry_dp[:, after_kv]

        def project_back(part):
            return sum(_mm(carry_dp[:, s], win_s[s, :]) for s in back_cols[part])

        @pl.when(i < n_tiles_ex)
        def _():
            dh_s[...] = project_back(0)
            wm, bs_rows, sink_col = _tile_constants(ws_ref, bs_ref, sink_ref)
            mk_pairs, mv_pairs = (mem_ops[0], mem_ops[1]), (mem_ops[2], mem_ops[3])
            vg = vg_ref[...]

            au_v, av_v = au_ref[...].astype(F32), av_ref[...].astype(F32)
            ya, res = _group_a_forward(au_v, av_v, vg, vb_ref[...], wm, bs_rows)
            swa, logits, yb = [], [], []
            for j in range(TILE_CHUNKS):
                rows, k_pairs, v_pairs, table = _load_chunk(j, i, sk_ref, sv_ref, skp_ref, svp_ref)
                qp = _halves_bf16(sq_ref[rows, :] * QK_SCALE)
                logits.append(_attention_logits(qp, k_pairs) + bias_ref[table])
                swa.append([rows, k_pairs, v_pairs, qp])
            dh_s[...] += project_back(1)
            mqp = _halves_bf16(mq_ref[...] * QK_SCALE)
            logits_mem = _attention_logits(mqp, mk_pairs)
            p_swa, sink_p = _softmax(jnp.concatenate(logits, axis=0), sink_col)
            for j in range(TILE_CHUNKS):
                out, pp = _attention_out(p_swa[j * 4 * CHUNK:(j + 1) * 4 * CHUNK], swa[j][2], CHUNK)
                yb.append(out)
                swa[j].append(pp)
            pm, _ = _softmax(logits_mem, None)
            yc, ppm = _attention_out(pm, mv_pairs, TILE)
            ycat = jnp.concatenate(ya + [jnp.concatenate(yb, axis=0), yc], axis=-1)

            zv = z_ref[...].astype(F32)
            sig = _sigmoid(zv)
            sz = zv * sig
            y_b = (ycat * sz).astype(BF16)
            halves = (slice(0, TILE // 2), slice(TILE // 2, TILE))
            o_halves = [_mm(y_b[rows], wout_ref[...]) for rows in halves]
            dh_s[...] += project_back(2)
            gp = gpost_ref[...]
            do_halves = []
            for rows, o in zip(halves, o_halves):
                r2 = lax.rsqrt(jnp.mean(o * o, axis=-1, keepdims=True) + EPS)
                nrm = o * r2
                xv = x_ref[rows, :]
                carry_x[lax.rem(i, 2), rows, :] = xv
                diff = xv + nrm * gp - tgt_ref[rows, :]
                loss_ref[...] += jnp.sum(diff * diff) * (0.5 / D_MODEL)
                dout = diff * (1.0 / D_MODEL)
                carry_dout[lax.rem(i, 2), rows, :] = dout
                dgpost_ref[...] += jnp.sum(dout * nrm, axis=0, keepdims=True)
                dn = dout * gp
                do_halves.append((r2 * (dn - nrm * jnp.mean(dn * nrm, axis=-1, keepdims=True))).astype(BF16))
            do_b = jnp.concatenate(do_halves, axis=0)
            dy = _mm_nt(do_b, wout_ref[...])
            carry_dp[:, col["z"]] = (dy * ycat * (sig + sz * (1.0 - sig))).astype(BF16)
            dyc = dy * sz

            dgu, dgv = [], []
            for g in range(A_GROUPS):
                sl = slice(g * 128, (g + 1) * 128)
                xhat, rstd, vn, s = res["groups"][g]
                dya = dyc[:, sl]
                dgu.append(dya * s)
                ds = dya * res["gu"][:, sl]
                dbs_ref[:, sl] += sum(ds[c * CHUNK:(c + 1) * CHUNK] for c in range(TILE_CHUNKS))
                ds_b = _rows_to_lanes(ds.astype(BF16), TILE_CHUNKS)
                dws_ref[g] += _mm_nt(ds_b, vn)
                dvn = _lanes_to_rows(_mm_tn(wm[g], ds_b), TILE_CHUNKS)
                dvg_ref[:, sl] += jnp.sum(dvn * xhat, axis=0, keepdims=True)
                dvb_ref[:, sl] += jnp.sum(dvn, axis=0, keepdims=True)
                dxh = dvn * vg[:, sl]
                dgv.append(rstd * (dxh - jnp.mean(dxh, axis=-1, keepdims=True)
                                   - xhat * jnp.mean(dxh * xhat, axis=-1, keepdims=True)))
            carry_dp[:, col["au"]] =(jnp.concatenate(dgu, axis=-1) * _gelu_grad(au_v, res["tu"])).astype(BF16)
            carry_dp[:, col["av"]] = (jnp.concatenate(dgv, axis=-1) * _gelu_grad(av_v, res["tv"])).astype(BF16)

            do_pairs = [_halves_bf16(dyc[rows, A_WIDTH:A_WIDTH + SWA_WIDTH]) for rows, *_ in swa]
            dp_swa = jnp.concatenate(
                [_attention_dprobs(do_pairs[j], swa[j][2]) for j in range(TILE_CHUNKS)], axis=0)
            dh_s[...] += project_back(3)
            dl_swa, delta = _softmax_backward(p_swa, dp_swa)
            sink_terms = sink_p * delta
            lane4 = lax.broadcasted_iota(jnp.int32, (1, 128), 1)
            dsink_vec = jnp.zeros((1, 128), F32)
            for h in range(4):
                head_sum = sum(jnp.sum(sink_terms[(4 * j + h) * CHUNK:(4 * j + h + 1) * CHUNK])
                               for j in range(TILE_CHUNKS))
                dsink_vec = dsink_vec + jnp.where(lane4 == h, -head_sum, 0.0)
            dsink_ref[...] += dsink_vec
            drel_ref[...] += sum(dl_swa[j * 4 * CHUNK:(j + 1) * 4 * CHUNK] for j in range(TILE_CHUNKS))
            dk_parts, dv_parts = [], []
            for j, (rows, k_pairs, v_pairs, qp, pp) in enumerate(swa):
                dq, dk, dv = _attention_grads(dl_swa[j * 4 * CHUNK:(j + 1) * 4 * CHUNK], pp, do_pairs[j], qp, k_pairs,
                                              CHUNK)
                carry_dp[rows, col["sq"]] = (dq * QK_SCALE).astype(BF16)
                dk_parts.append(_swa_unvariants(*_split_pair_grads(dk)))
                dv_parts.append(_swa_unvariants(*_split_pair_grads(dv)))

            dc_pairs = _halves_bf16(dyc[:, A_WIDTH + SWA_WIDTH:])
            dp_mem = _attention_dprobs(dc_pairs, mv_pairs)
            dwout_ref[...] += _mm_tn(y_b, do_b)
            dl_mem, _ = _softmax_backward(pm, dp_mem)
            dmq, dmk, dmv = _attention_grads(dl_mem, ppm, dc_pairs, mqp, mk_pairs, TILE)
            carry_dp[:, col["mq"]] = (dmq * QK_SCALE).astype(BF16)
            dmkv_s[...] += jnp.concatenate([_mem_unvariants(*_split_pair_grads(dmk)),
                                            _mem_unvariants(*_split_pair_grads(dmv))], axis=-1)

            for parts_c, carry, cols in ((dk_parts, carry_k, col["sk"]), (dv_parts, carry_v, col["sv"])):
                @pl.when(i > 0)
                def _():
                    dproj_ref[:, cols] = (carry[...] + jnp.concatenate(
                        [jnp.zeros((TILE - CHUNK, KV_WIDTH), F32), parts_c[0][:CHUNK]], axis=0)).astype(BF16)
                new = [parts_c[0][CHUNK:]]
                for j in range(1, TILE_CHUNKS):
                    new[-1] = new[-1] + parts_c[j][:CHUNK]
                    new.append(parts_c[j][CHUNK:])
                carry[...] = jnp.concatenate(new, axis=0)
            skp_ref[...] = sk_ref[TILE - CHUNK:, :]
            svp_ref[...] = sv_ref[TILE - CHUNK:, :]

        @pl.when(i == n_tiles_ex)
        def _():
            dproj_ref[:, col["sk"]] = carry_k[...].astype(BF16)
            dproj_ref[:, col["sv"]] = carry_v[...].astype(BF16)
            d_b = dmkv_s[...].astype(BF16)
            dwmkv_ref[...] += _mm_tn(memn_s[...], d_b)
            dgmem_ref[...] += jnp.sum(_mm_nt(d_b, wmkv_ref[...]) * normalized_mem(), axis=0, keepdims=True)
            dh_s[...] = sum(project_back(part) for part in range(len(back_cols)))

        @pl.when(i > 0)
        def _():
            xv = carry_x[lax.rem(i + 1, 2)]
            r = lax.rsqrt(jnp.mean(xv * xv, axis=-1, keepdims=True) + EPS)
            xn = xv * r
            dh = dh_s[...] + _mm(dproj_ref[:, kv_cols], win_s[kv_cols, :])
            dgpre_ref[...] += jnp.sum(dh * xn, axis=0, keepdims=True)
            dhg = dh * gpre_ref[...]
            dx_ref[...] = (r * (dhg - xn * jnp.mean(dhg * xn, axis=-1, keepdims=True))
                           + carry_dout[lax.rem(i + 1, 2)])

        @pl.when((b == n_ex - 1) & (i == n_tiles_ex))
        def _():
            _fill_small_grads(dgpre_ref, dgpost_ref, dgmem_ref, dvg_ref, dvb_ref, dws_ref, dbs_ref, dsink_ref,
                              drel_ref, loss_ref, bk_ref, a_ref, b_ref)

    tile = functools.partial(_tile_specs, n_tiles_ex)
    late = lambda width: pl.BlockSpec((TILE, width), lambda b, i: (b * n_tiles_ex + jnp.maximum(i - 1, 0), 0))
    vmem_f32 = lambda *shape: pltpu.VMEM(shape, F32)
    return pl.pallas_call(
        body, name="mix", grid=(n_ex, n_tiles_ex + 1),
        out_shape=[jax.ShapeDtypeStruct((n_tok, D_MODEL), F32), jax.ShapeDtypeStruct((n_tok, IN_WIDTH), BF16),
                   jax.ShapeDtypeStruct((D_MODEL, 2 * MEM_WIDTH), F32), jax.ShapeDtypeStruct((MIX_WIDTH, D_MODEL), F32),
                   jax.ShapeDtypeStruct((SMALL_A_ROWS, D_MODEL), F32), jax.ShapeDtypeStruct((SMALL_B_ROWS, 128), F32)],
        in_specs=[tile(A_WIDTH), tile(A_WIDTH), tile(SWA_WIDTH), tile(KV_WIDTH), tile(KV_WIDTH),
                  tile(MEM_WIDTH), tile(MIX_WIDTH),
                  pl.BlockSpec((1, MEM_LEN, D_MODEL), lambda b, i: (b, 0, 0)),
                  tile(D_MODEL), tile(D_MODEL),
                  _full_spec((1, A_WIDTH)), _full_spec((1, A_WIDTH)), _full_spec((A_GROUPS, CHUNK, CHUNK)),
                  _full_spec((A_GROUPS, CHUNK, CHUNK)), SMEM_SPEC, _full_spec((2, 4 * CHUNK, 2 * CHUNK)),
                  _full_spec((MIX_WIDTH, D_MODEL)), _full_spec((1, D_MODEL)), _full_spec((1, D_MODEL)),
                  _full_spec((D_MODEL, 2 * MEM_WIDTH)),
                  _full_spec((1, D_MODEL)), _full_spec((CHUNK, 2 * CHUNK)), ANY_SPEC],
        out_specs=[late(D_MODEL), late(IN_WIDTH), _full_spec((D_MODEL, 2 * MEM_WIDTH)),
                   _full_spec((MIX_WIDTH, D_MODEL)), _full_spec((SMALL_A_ROWS, D_MODEL)),
                   _full_spec((SMALL_B_ROWS, 128))],
        scratch_shapes=[pltpu.VMEM((TILE, IN_WIDTH), BF16), pltpu.VMEM((TILE, KV_WIDTH), F32),
                        pltpu.VMEM((TILE, KV_WIDTH), F32), pltpu.VMEM((MEM_LEN, D_MODEL), BF16),
                        pltpu.VMEM((4, 2 * MEM_LEN, 128), BF16), pltpu.VMEM((MEM_LEN, 2 * MEM_WIDTH), F32),
                        pltpu.VMEM((2, TILE, D_MODEL), F32), pltpu.VMEM((IN_WIDTH, D_MODEL), BF16),
                        pltpu.SemaphoreType.DMA, vmem_f32(TILE, D_MODEL),
                        pltpu.VMEM((CHUNK, KV_WIDTH), BF16), pltpu.VMEM((CHUNK, KV_WIDTH), BF16),
                        vmem_f32(2, TILE, D_MODEL),
                        vmem_f32(1, D_MODEL), vmem_f32(1, D_MODEL), vmem_f32(1, D_MODEL), vmem_f32(1, A_WIDTH),
                        vmem_f32(1, A_WIDTH), vmem_f32(A_GROUPS, CHUNK, CHUNK), vmem_f32(CHUNK, A_WIDTH),
                        vmem_f32(1, 128), vmem_f32(4 * CHUNK, 2 * CHUNK), vmem_f32(1, 128)],
        compiler_params=pltpu.CompilerParams(vmem_limit_bytes=VMEM_LIMIT),
    )(au, av, sq, sk, sv, mq, z, mem, x2, tgt2, v_g, v_b, w_sp, b_sp, sinks, bias, w_out, g_post, g_mem,
      w_mkv, g_pre, buckets, w_in_t)


def _fill_small_grads(dgpre_ref, dgpost_ref, dgmem_ref, dvg_ref, dvb_ref, dws_ref, dbs_ref, dsink_ref, drel_ref,
                      loss_ref, bk_ref, a_ref, b_ref):
    a_ref[...] = jnp.zeros_like(a_ref)
    b_ref[...] = jnp.zeros_like(b_ref)
    a_ref[0:1, :] = dgpre_ref[...]
    a_ref[1:2, :] = dgpost_ref[...]
    a_ref[2:3, :] = dgmem_ref[...]
    a_ref[3:4, :] = jnp.concatenate([dvg_ref[...], dvb_ref[...]], axis=-1)
    a_ref[ROW_LOSS:ROW_LOSS + 1, 0:128] = loss_ref[...]
    row = lax.broadcasted_iota(jnp.int32, (CHUNK, CHUNK), 0)
    col = lax.broadcasted_iota(jnp.int32, (CHUNK, CHUNK), 1)
    for g in range(A_GROUPS):
        b_ref[ROW_WS + g * CHUNK:ROW_WS + (g + 1) * CHUNK, :] = jnp.where(row >= col, dws_ref[g], 0.0)
        by_token = jnp.transpose(dbs_ref[:, g * 128:(g + 1) * 128])
        b_ref[ROW_BS + g:ROW_BS + g + 1, :] = jnp.sum(by_token, axis=0, keepdims=True)
    b_ref[ROW_SINK:ROW_SINK + 1, :] = dsink_ref[...]
    bk = bk_ref[...]
    rel_row = lax.broadcasted_iota(jnp.int32, (8, 128), 0)
    rel_col = lax.broadcasted_iota(jnp.int32, (8, 128), 1)
    rel = jnp.zeros((8, 128), F32)
    for h in range(4):
        acc = drel_ref[h * CHUNK:(h + 1) * CHUNK, :]
        for b in range(N_BUCKETS):
            rel = jnp.where((rel_row == h) & (rel_col == b), jnp.sum(jnp.where(bk == b, acc, 0.0)), rel)
    b_ref[ROW_REL:ROW_REL + 8, :] = rel


SHARD_ROWS = IN_WIDTH // N_CHIPS
SHARD_WINDOW = 768
SHARD_HALF = SHARD_ROWS // 2
DWIN_TILE = 2048
N_REL = N_CHIPS - 1


def _shard_window_start(shard):
    return (shard * SHARD_ROWS // 128) * 128


def _reduce_gradients(dproj, h, big, small, shard_arr):
    n_tok = h.shape[0]
    tile = min(DWIN_TILE, n_tok)
    n_sub = n_tok // tile
    last = N_CHIPS - 1
    n_big, n_small = len(big), len(small)
    big_half = [g.shape[2:] for g in big]
    sem_big_d2d = 2 * N_CHIPS
    sem_big_ici = sem_big_d2d + n_big
    sem_big_swap = sem_big_ici + N_REL * n_big
    sem_small_d2d = sem_big_swap + n_big
    sem_small_ici = sem_small_d2d + n_small
    n_sems = sem_small_ici + N_REL * n_small
    loc_small = n_big
    loc_out_win = loc_small + n_small
    loc_out_big = loc_out_win + 2
    loc_out_small = loc_out_big + 2 * n_big
    n_local = loc_out_small + n_small

    def relation_of_slot(s):
        return (s + 2) % N_REL + 1

    def shard_of_slot(s, my_shard):
        return my_shard ^ jnp.where(s == last, 0, relation_of_slot(s))

    def body(shard_ref, dp_ref, h_hbm, *refs):
        h_vmem, h_sem, refs = refs[-2], refs[-1], refs[:-2]
        big_hbm, refs = refs[:n_big], refs[n_big:]
        small_hbm, refs = refs[:n_small], refs[n_small:]
        out_hbm, refs = refs[0], refs[1:]
        big_out, refs = refs[:n_big], refs[n_big:]
        small_out, refs = refs[:n_small], refs[n_small:]
        part, recv_d2d, send_ici, recv_ici, mine_buf, other_buf = refs[:6]
        refs = refs[6:]
        big_own, big_recv, big_send, big_land, big_mine, big_other = (
            refs[k * n_big:(k + 1) * n_big] for k in range(6))
        refs = refs[6 * n_big:]
        small_own, small_recv, small_all = (refs[k * n_small:(k + 1) * n_small] for k in range(3))
        send_sems, recv_sems, local_sems = refs[3 * n_small:]

        s, t = pl.program_id(0), pl.program_id(1)
        x, y, c = lax.axis_index("x"), lax.axis_index("y"), lax.axis_index("c")
        my_chip = 2 * x + y
        sibling = (x, y, 1 - c)
        my_rows = pl.ds(pl.multiple_of(c * SHARD_HALF, 8), SHARD_HALF)
        other_rows = pl.ds(pl.multiple_of((1 - c) * SHARD_HALF, 8), SHARD_HALF)

        def remote(src, dst, k, to):
            return pltpu.make_async_remote_copy(src_ref=src, dst_ref=dst, send_sem=send_sems.at[k],
                                                recv_sem=recv_sems.at[k], device_id=to, device_id_type=MESH)

        def chip_at(rel):
            return (x ^ (rel >> 1), y ^ (rel & 1), c)

        def to_sibling(k):
            return remote(part.at[k % 2, other_rows, :], recv_d2d.at[k], k, sibling)

        def to_chip(k):
            return remote(send_ici.at[k], recv_ici.at[k], N_CHIPS + k, chip_at(relation_of_slot(k)))

        swap = remote(mine_buf, other_buf, 2 * N_CHIPS - 1, sibling)
        big_load = [pltpu.make_async_copy(big_hbm[w].at[:, pl.ds(c, 1)], big_own[w], local_sems.at[w])
                    for w in range(n_big)]
        big_to_sibling = [remote(big_hbm[w].at[:, pl.ds(1 - c, 1)], big_recv[w], sem_big_d2d + w, sibling)
                          for w in range(n_big)]
        big_to_chip = [[remote(big_send[w].at[k], big_land[w].at[k], sem_big_ici + N_REL * w + k, chip_at(k + 1))
                        for k in range(N_REL)] for w in range(n_big)]
        big_swap = [remote(big_mine[w], big_other[w], sem_big_swap + w, sibling) for w in range(n_big)]
        small_load = [pltpu.make_async_copy(small_hbm[i], small_own[i], local_sems.at[loc_small + i])
                      for i in range(n_small)]
        small_to_sibling = [remote(small_hbm[i], small_recv[i], sem_small_d2d + i, sibling) for i in range(n_small)]
        small_to_chip = [[remote(small_all[i].at[my_chip], small_all[i].at[my_chip],
                                 sem_small_ici + N_REL * i + k, chip_at(k + 1))
                          for k in range(N_REL)] for i in range(n_small)]

        h_loads = [pltpu.make_async_copy(h_hbm.at[rows, :], h_vmem.at[rows, :], h_sem.at[k]) for k, rows in enumerate(
            [pl.ds(0, tile)] + ([pl.ds(tile, n_tok - tile)] if n_sub > 1 else []))]

        @pl.when((s == 0) & (t == 0))
        def _():
            for cp in h_loads + big_load + big_to_sibling + small_load + small_to_sibling:
                cp.start()
            h_loads[0].wait()

        if n_sub > 1:
            @pl.when((s == 0) & (t == 1))
            def _():
                h_loads[1].wait()

        @pl.when((s == 0) & (t == n_sub - 1))
        def _():
            for cp in big_load + small_load:
                cp.wait()
            for cp in big_to_sibling + small_to_sibling:
                cp.wait_recv()
                cp.wait_send()
            for w in range(n_big):
                for k in range(N_REL):
                    shard = my_chip ^ (k + 1)
                    big_send[w][k] = (big_own[w][shard, 0] + big_recv[w][shard, 0]).astype(BF16)
                    big_to_chip[w][k].start()
            for i in range(n_small):
                small_all[i][my_chip] = small_own[i][...] + small_recv[i][...]
                for k in range(N_REL):
                    small_to_chip[i][k].start()

        @pl.when((s > 0) & (t == jnp.where(s == last, 0, min(1, n_sub - 1))))
        def _():
            k = s - 1
            cp = to_sibling(k)
            cp.wait_recv()
            cp.wait_send()
            send_ici[k] = (part[k % 2, my_rows, :] + recv_d2d[k]).astype(BF16)
            to_chip(k).start()

        def big_rows(w, half):
            rows = big_half[w][0]
            return big_out[w].at[pl.ds(pl.multiple_of(half * rows, 8), rows), :]

        big_store_mine = [pltpu.make_async_copy(big_mine[w], big_rows(w, c), local_sems.at[loc_out_big + 2 * w])
                          for w in range(n_big)]
        big_store_other = [pltpu.make_async_copy(big_other[w], big_rows(w, 1 - c),
                                                 local_sems.at[loc_out_big + 2 * w + 1]) for w in range(n_big)]
        small_store = [pltpu.make_async_copy(small_all[i], small_out[i], local_sems.at[loc_out_small + i])
                       for i in range(n_small)]

        @pl.when((s == last) & (t == 0))
        def _():
            for w in range(n_big):
                total = big_own[w][my_chip, 0] + big_recv[w][my_chip, 0]
                for k in range(N_REL):
                    big_to_chip[w][k].wait_recv()
                    total = total + big_land[w][k].astype(F32)
                big_mine[w][...] = total
                big_swap[w].start()
                big_store_mine[w].start()
            for i in range(n_small):
                for k in range(N_REL):
                    small_to_chip[i][k].wait_recv()
                small_store[i].start()

        r = _mm_tn(dp_ref[...], h_vmem[pl.ds(pl.multiple_of(t * tile, tile), tile), :])
        odd = shard_of_slot(s, shard_ref[0]) % 2
        for parity in range(2):
            rows = r[64 * parity:64 * parity + SHARD_ROWS]

            @pl.when((odd == parity) & (t == 0))
            def _():
                part[s % 2] = rows

            @pl.when((odd == parity) & (t > 0))
            def _():
                part[s % 2] += rows

        @pl.when(t == n_sub - 1)
        def _():
            to_sibling(s).start()

        @pl.when((s == last) & (t == n_sub - 1))
        def _():
            cp = to_sibling(last)
            cp.wait_recv()
            cp.wait_send()
            total = part[last % 2, my_rows, :] + recv_d2d[last]
            for k in range(last):
                to_chip(k).wait_recv()
                total = total + recv_ici[k].astype(F32)
            mine_buf[...] = total
            swap.start()
            out_mine = pltpu.make_async_copy(mine_buf, out_hbm.at[my_rows, :], local_sems.at[0])
            out_mine.start()
            swap.wait_recv()
            out_other = pltpu.make_async_copy(other_buf, out_hbm.at[other_rows, :], local_sems.at[1])
            out_other.start()
            for w in range(n_big):
                big_swap[w].wait_recv()
                big_store_other[w].start()
            stores = [out_mine, out_other] + big_store_mine + big_store_other + small_store
            for k in range(last):
                to_chip(k).wait_send()
            swap.wait_send()
            for w in range(n_big):
                for k in range(N_REL):
                    big_to_chip[w][k].wait_send()
                big_swap[w].wait_send()
            for i in range(n_small):
                for k in range(N_REL):
                    small_to_chip[i][k].wait_send()
            for cp in stores:
                cp.wait()

    half = (SHARD_HALF, D_MODEL)
    vmem = pltpu.VMEM
    scratch = [vmem((2, SHARD_ROWS, D_MODEL), F32), vmem((N_CHIPS,) + half, F32),
               vmem((N_REL,) + half, BF16), vmem((N_REL,) + half, BF16), vmem(half, F32), vmem(half, F32)]
    scratch += [vmem((N_CHIPS, 1) + hs, F32) for hs in big_half] * 2
    scratch += [vmem((N_REL,) + hs, BF16) for hs in big_half] * 2
    scratch += [vmem(hs, F32) for hs in big_half] * 2
    scratch += [vmem(a.shape, F32) for a in small] * 2 + [vmem((N_CHIPS,) + a.shape, F32) for a in small]
    scratch += [pltpu.SemaphoreType.DMA((n_sems,)), pltpu.SemaphoreType.DMA((n_sems,)),
                pltpu.SemaphoreType.DMA((n_local,)), vmem(h.shape, BF16), pltpu.SemaphoreType.DMA((2,))]
    n_hbm = n_big + n_small
    out = pl.pallas_call(
        body, name="reduce_gradients",
        out_shape=[jax.ShapeDtypeStruct((SHARD_ROWS, D_MODEL), F32)]
        + [jax.ShapeDtypeStruct((2 * hs[0], hs[1]), F32) for hs in big_half]
        + [jax.ShapeDtypeStruct((N_CHIPS,) + a.shape, F32) for a in small],
        grid_spec=pltpu.PrefetchScalarGridSpec(
            num_scalar_prefetch=1, grid=(N_CHIPS, n_sub),
            in_specs=[pl.BlockSpec((pl.Element(tile), pl.Element(SHARD_WINDOW)),
                                   lambda s, t, m: (t * tile, _shard_window_start(shard_of_slot(s, m[0])))),
                      ANY_SPEC] + [ANY_SPEC] * n_hbm,
            out_specs=[ANY_SPEC] * (1 + n_hbm),
            scratch_shapes=scratch),
        compiler_params=pltpu.CompilerParams(vmem_limit_bytes=VMEM_LIMIT),
    )(shard_arr, dproj, h, *big, *small)
    return out[:1 + n_big], out[1 + n_big:]


def _adamw(w, g, m, v):
    m2 = ADAM_B1 * m + (1.0 - ADAM_B1) * g
    v2 = ADAM_B2 * v + (1.0 - ADAM_B2) * (g * g)
    m_hat = m2 / (1.0 - ADAM_B1 ** ADAM_STEP)
    v_hat = v2 / (1.0 - ADAM_B2 ** ADAM_STEP)
    delta = -ADAM_LR * (m_hat / (jnp.sqrt(v_hat) + ADAM_EPS) + ADAM_WD * w)
    return delta, m2, v2


ADAM_STEPS = 2


def _adamw_all(shard_grads, shard_w, shard_m, shard_v, ra, rb, small_w, small_m, small_v):
    n_sh, n = len(shard_w), len(small_w)

    def body(*refs):
        sh_in, refs = refs[:4 * n_sh], refs[4 * n_sh:]
        ra_ref, rb_ref, refs = refs[0], refs[1], refs[2:]
        w_refs, m_refs, v_refs, refs = refs[:n], refs[n:2 * n], refs[2 * n:3 * n], refs[3 * n:]
        sh_out, outs = refs[:4 * n_sh], refs[4 * n_sh:]
        for k in range(n_sh):
            g = sh_in[k][...]
            delta, m2, v2 = _adamw(sh_in[n_sh + k][...], g, sh_in[2 * n_sh + k][...], sh_in[3 * n_sh + k][...])
            for ref, val in zip(sh_out[4 * k:4 * k + 4], (g, delta, m2, v2)):
                ref[...] = val

        @pl.when(pl.program_id(0) == 0)
        def _():
            g_outs, d_outs, m_outs, v_outs = outs[:n], outs[n:2 * n], outs[2 * n:3 * n], outs[3 * n:4 * n]
            ga, gb = ra_ref[0], rb_ref[0]
            for chip in range(1, N_CHIPS):
                ga = ga + ra_ref[chip]
                gb = gb + rb_ref[chip]
            outs[4 * n][...] = ga[ROW_LOSS:ROW_LOSS + 1, 0:128]
            grads = [ga[0:1, :], ga[1:2, :], ga[2:3, :], ga[3:4, :A_WIDTH], ga[3:4, A_WIDTH:],
                     gb[ROW_WS:ROW_WS + A_GROUPS * CHUNK, :].reshape(A_GROUPS, CHUNK, CHUNK),
                     gb[ROW_BS:ROW_BS + A_GROUPS, :], gb[ROW_SINK:ROW_SINK + 1, 0:4],
                     gb[ROW_REL:ROW_REL + 4, 0:N_BUCKETS]]
            for k in range(n):
                delta, m2, v2 = _adamw(w_refs[k][...], grads[k], m_refs[k][...], v_refs[k][...])
                g_outs[k][...] = grads[k]
                d_outs[k][...] = delta
                m_outs[k][...] = m2
                v_outs[k][...] = v2

    def rows_block(a):
        assert a.shape[0] % (8 * ADAM_STEPS) == 0
        return pl.BlockSpec((a.shape[0] // ADAM_STEPS, a.shape[1]), lambda i: (i, 0))

    sh_specs = [rows_block(w) for w in shard_w]
    small_in = [ra, rb, *small_w, *small_m, *small_v]
    small_out_shapes = [jax.ShapeDtypeStruct(w.shape, F32) for w in small_w] * 4 + [jax.ShapeDtypeStruct((1, 128), F32)]
    out = pl.pallas_call(
        body, name="adamw_all", grid=(ADAM_STEPS,),
        out_shape=[jax.ShapeDtypeStruct(w.shape, F32) for w in shard_w for _ in range(4)] + small_out_shapes,
        in_specs=sh_specs * 4 + [_full_spec(a.shape) for a in small_in],
        out_specs=[spec for spec in sh_specs for _ in range(4)] + [_full_spec(s.shape) for s in small_out_shapes],
        compiler_params=pltpu.CompilerParams(vmem_limit_bytes=VMEM_LIMIT),
    )(*shard_grads, *shard_w, *shard_m, *shard_v, *small_in)
    return [out[4 * k:4 * k + 4] for k in range(n_sh)], out[4 * n_sh:]


def kernel(x, mem, pre_norm_g, post_norm_g, mem_norm_g, w_in, w_mem_kv, v_norm_g, v_norm_b, w_spatial, b_spatial, attn_sinks, rel_bias, w_out, loss_target, m_pre_norm_g, m_post_norm_g, m_mem_norm_g, m_w_in, m_w_mem_kv, m_v_norm_g, m_v_norm_b, m_w_spatial, m_b_spatial, m_attn_sinks, m_rel_bias, m_w_out, v_pre_norm_g, v_post_norm_g, v_mem_norm_g, v_w_in, v_w_mem_kv, v_v_norm_g, v_v_norm_b, v_w_spatial, v_b_spatial, v_attn_sinks, v_rel_bias, v_w_out):
    n_ex, seq, _ = x.shape
    n_tok = n_ex * seq
    x2 = x.reshape(n_tok, D_MODEL)
    tgt2 = loss_target.reshape(n_tok, D_MODEL)
    buckets = jnp.asarray(_bucket_map())
    shard_arr = (2 * lax.axis_index("x") + lax.axis_index("y")).astype(jnp.int32).reshape(1)
    w_sp = w_spatial[0]
    w_in_t, m_w_in_t, v_w_in_t = (jnp.transpose(a[0]) for a in (w_in, m_w_in, v_w_in))
    rel_t, m_rel_t, v_rel_t = (jnp.transpose(a) for a in (rel_bias, m_rel_bias, v_rel_bias))

    x_arr = lax.axis_index("x").astype(jnp.int32).reshape(1)
    h_b, parts, (w_in_b, g_mkv, g_out), bias, b_sp = _gather_and_project(
        x2, pre_norm_g, w_in_t, w_mem_kv[0], w_out[0], rel_t, buckets, b_spatial[0], x_arr)
    w_mkv_b = g_mkv.reshape(D_MODEL, 2 * MEM_WIDTH)
    w_out_b = g_out.reshape(MIX_WIDTH, D_MODEL)

    dx, dproj, dwmkv, dwout, small_a, small_b = _mix(
        parts, mem, x2, tgt2, v_norm_g, v_norm_b, w_sp, b_sp, attn_sinks, bias, w_out_b, post_norm_g, mem_norm_g,
        w_mkv_b, pre_norm_g, w_in_b, buckets, n_ex, seq)

    shard_shapes = [w_mem_kv.shape[1:], w_out.shape[1:]]
    big = [g.reshape(N_CHIPS, 2, s[0] // 2, s[1]) for g, s in zip((dwmkv, dwout), shard_shapes)]
    (g_win, g_wmkv, g_wout), (ga, gb) = _reduce_gradients(dproj, h_b, big, [small_a, small_b], shard_arr)

    small_w = [pre_norm_g, post_norm_g, mem_norm_g, v_norm_g, v_norm_b, w_sp, b_spatial[0], attn_sinks, rel_t]
    small_m = [m_pre_norm_g, m_post_norm_g, m_mem_norm_g, m_v_norm_g, m_v_norm_b, m_w_spatial[0], m_b_spatial[0],
               m_attn_sinks, m_rel_t]
    small_v = [v_pre_norm_g, v_post_norm_g, v_mem_norm_g, v_v_norm_g, v_v_norm_b, v_w_spatial[0], v_b_spatial[0],
               v_attn_sinks, v_rel_t]
    big_out, small_out = _adamw_all(
        [g_win, g_wmkv, g_wout], [w_in_t, w_mem_kv[0], w_out[0]], [m_w_in_t, m_w_mem_kv[0], m_w_out[0]],
        [v_w_in_t, v_w_mem_kv[0], v_w_out[0]], ga, gb, small_w, small_m, small_v)
    n_small = len(small_w)

    outputs = [small_out[4 * n_small][0, 0], dx.reshape(x.shape)]
    for kind in range(4):
        s = small_out[kind * n_small:(kind + 1) * n_small]
        outputs += [s[0], s[1], s[2], jnp.transpose(big_out[0][kind])[None], big_out[1][kind][None], s[3], s[4],
                    s[5][None], s[6][None], s[7], jnp.transpose(s[8]), big_out[2][kind][None]]
    return tuple(outputs)
```
